```python
import math
import jax, jax.numpy as jnp
from jax import lax
import numpy as np


D_MODEL = 1024
BATCH = 8
SEQ = 2048
DEPTH = 2

SSM_WIDTH = D_MODEL // 2
SSM_GROUP = 16
SSM_GROUPS = SSM_WIDTH // SSM_GROUP
SSM_STATE = 64
DT_MIN = 1e-3
DT_MAX = 1e-1
CONV_WIDTH = D_MODEL // 2
CONV_KERNEL = 31
POOL_WIDTH = D_MODEL // 2
POOL_WINDOWS = (2, 4, 8, 16)
POOL_GROUP = POOL_WIDTH // len(POOL_WINDOWS)
N_BRANCHES = 3
IN_WIDTH = SSM_WIDTH + 2 * CONV_WIDTH + POOL_WIDTH + N_BRANCHES * D_MODEL
FFN_HIDDEN = ((8 * D_MODEL + 3 * 256 - 1) // (3 * 256)) * 256
EPS = 1e-6

kernel_name = 'hybrid_s5_conformer_pool_gated_block'


def rms_norm(x, g):
    xf = x.astype(jnp.float32)
    y = xf * lax.rsqrt(jnp.mean(xf * xf, axis=-1, keepdims=True) + EPS)
    return (y * g.astype(jnp.float32)).astype(x.dtype)


def layer_norm(x, g, b):
    xf = x.astype(jnp.float32)
    mu = jnp.mean(xf, axis=-1, keepdims=True)
    var = jnp.mean(jnp.square(xf - mu), axis=-1, keepdims=True)
    y = (xf - mu) * lax.rsqrt(var + EPS)
    return (y * g.astype(jnp.float32) + b.astype(jnp.float32)).astype(x.dtype)


def _complex_linear_combine(e1, e2):
    a1r, a1i, b1r, b1i = e1
    a2r, a2i, b2r, b2i = e2
    return (a2r * a1r - a2i * a1i,
            a2r * a1i + a2i * a1r,
            a2r * b1r - a2i * b1i + b2r,
            a2r * b1i + a2i * b1r + b2i)


def s5_mixer(u, a_re, a_im, log_dt, b_re, b_im, c_re, c_im, d_skip, w_glu, b_glu):
    bsz, seq, _ = u.shape
    f32 = jnp.float32
    uf = u.astype(f32).reshape(bsz, seq, SSM_GROUPS, SSM_GROUP)
    a_re = a_re.astype(f32)
    a_im = a_im.astype(f32)
    dt = jnp.exp(log_dt.astype(f32))[:, None]
    mag = jnp.exp(dt * a_re)
    ang = dt * a_im
    abar_re = mag * jnp.cos(ang)
    abar_im = mag * jnp.sin(ang)
    den = a_re * a_re + a_im * a_im
    nr = abar_re - 1.0
    ni = abar_im
    f_re = (nr * a_re + ni * a_im) / den
    f_im = (ni * a_re - nr * a_im) / den
    b_re = b_re.astype(f32)
    b_im = b_im.astype(f32)
    bbar_re = f_re[..., None] * b_re - f_im[..., None] * b_im
    bbar_im = f_re[..., None] * b_im + f_im[..., None] * b_re
    bu_re = jnp.einsum('bsgp,gnp->bsgn', uf, bbar_re)
    bu_im = jnp.einsum('bsgp,gnp->bsgn', uf, bbar_im)
    a_seq_re = jnp.broadcast_to(abar_re, bu_re.shape)
    a_seq_im = jnp.broadcast_to(abar_im, bu_im.shape)
    _, _, h_re, h_im = lax.associative_scan(
        _complex_linear_combine, (a_seq_re, a_seq_im, bu_re, bu_im), axis=1)
    y = (jnp.einsum('bsgn,gpn->bsgp', h_re, c_re.astype(f32))
         - jnp.einsum('bsgn,gpn->bsgp', h_im, c_im.astype(f32))
         + d_skip.astype(f32) * uf)
    y = y.reshape(bsz, seq, SSM_WIDTH)
    g = jax.nn.gelu(y)
    out = g * jax.nn.sigmoid(g @ w_glu.astype(f32) + b_glu.astype(f32))
    return out.astype(u.dtype)


def conv_module(v, w_dw, b_dw, ln_g, ln_b, w_proj):
    h = v[..., :CONV_WIDTH] * jax.nn.sigmoid(v[..., CONV_WIDTH:])
    h = jnp.pad(h, ((0, 0), (CONV_KERNEL - 1, 0), (0, 0)))
    h = lax.conv_general_dilated(h, w_dw, window_strides=(1,), padding='VALID',
                                 dimension_numbers=('NWC', 'WIO', 'NWC'),
                                 feature_group_count=CONV_WIDTH) + b_dw
    h = jax.nn.silu(layer_norm(h, ln_g, ln_b))
    return h @ w_proj


def pool_mixer(u, w_group, scale, w_proj):
    bsz, seq, _ = u.shape
    uf = u.astype(jnp.float32).reshape(bsz, seq, len(POOL_WINDOWS), POOL_GROUP)
    cs = jnp.cumsum(uf, axis=1)
    pos = jnp.arange(1, seq + 1, dtype=jnp.float32)
    outs = []
    for k, w in enumerate(POOL_WINDOWS):
        c = cs[:, :, k]
        lagged = jnp.pad(c, ((0, 0), (w, 0), (0, 0)))[:, :seq]
        mean = (c - lagged) / jnp.minimum(pos, float(w))[None, :, None]
        outs.append(mean - uf[:, :, k])
    p = jnp.stack(outs, axis=2)
    p = jnp.einsum('bsgc,gcd->bsgd', p, w_group.astype(jnp.float32))
    p = p.reshape(bsz, seq, POOL_WIDTH) * scale.astype(jnp.float32)
    return p.astype(u.dtype) @ w_proj


def hybrid_layer(x, norm1, w_in, b_gate, a_re, a_im, log_dt, b_re, b_im, c_re, c_im,
                 d_skip, w_glu, b_glu, ssm_w_proj, conv_w_dw, conv_b_dw, conv_ln_g,
                 conv_ln_b, conv_w_proj, pool_w_group, pool_scale, pool_w_proj, w_out,
                 norm2, w_gate, w_up, w_down):
    bsz, seq, _ = x.shape
    h = rms_norm(x, norm1)
    z = h @ w_in
    o1 = SSM_WIDTH
    o2 = o1 + 2 * CONV_WIDTH
    o3 = o2 + POOL_WIDTH
    u_a = z[..., :o1]
    v_b = z[..., o1:o2]
    u_c = z[..., o2:o3]
    gates = jax.nn.sigmoid(z[..., o3:] + b_gate).reshape(bsz, seq, N_BRANCHES, D_MODEL)
    y_a = s5_mixer(u_a, a_re, a_im, log_dt, b_re, b_im, c_re, c_im, d_skip, w_glu, b_glu) @ ssm_w_proj
    y_b = conv_module(v_b, conv_w_dw, conv_b_dw, conv_ln_g, conv_ln_b, conv_w_proj)
    y_c = pool_mixer(u_c, pool_w_group, pool_scale, pool_w_proj)
    merged = gates[:, :, 0] * y_a + gates[:, :, 1] * y_b + gates[:, :, 2] * y_c
    x = x + merged @ w_out
    h = rms_norm(x, norm2)
    x = x + (jax.nn.silu(h @ w_gate) * (h @ w_up)) @ w_down
    return x


def _fwd_setup_inputs(seed: int = 0) -> dict:
    key = jax.random.key(seed)
    ks = iter(jax.random.split(key, 40))
    f32 = jnp.float32

    def nrm(shape, scale):
        return jax.random.normal(next(ks), shape, f32) * scale

    L, D, G, N, P = DEPTH, D_MODEL, SSM_GROUPS, SSM_STATE, SSM_GROUP
    n_idx = jnp.arange(N, dtype=f32)
    inputs = {
        'x': nrm((BATCH, SEQ, D), 1.0),
        'norm1': 1.0 + nrm((L, D), 0.02),
        'w_in': nrm((L, D, IN_WIDTH), D ** -0.5),
        'b_gate': nrm((L, N_BRANCHES * D), 0.01),
        'ssm_a_re': -0.5 + nrm((L, G, N), 0.01),
        'ssm_a_im': math.pi * n_idx + nrm((L, G, N), 0.01),
        'ssm_log_dt': jax.random.uniform(next(ks), (L, G), f32,
                                         math.log(DT_MIN), math.log(DT_MAX)),
        'ssm_b_re': nrm((L, G, N, P), (2.0 * P) ** -0.5),
        'ssm_b_im': nrm((L, G, N, P), (2.0 * P) ** -0.5),
        'ssm_c_re': nrm((L, G, P, N), (2.0 * N) ** -0.5 * 4.0),
        'ssm_c_im': nrm((L, G, P, N), (2.0 * N) ** -0.5 * 4.0),
        'ssm_d': nrm((L, G, P), 1.0),
        'ssm_w_glu': nrm((L, SSM_WIDTH, SSM_WIDTH), SSM_WIDTH ** -0.5),
        'ssm_b_glu': nrm((L, SSM_WIDTH), 0.01),
        'ssm_w_proj': nrm((L, SSM_WIDTH, D), SSM_WIDTH ** -0.5),
        'conv_w_dw': nrm((L, CONV_KERNEL, 1, CONV_WIDTH), CONV_KERNEL ** -0.5),
        'conv_b_dw': nrm((L, CONV_WIDTH), 0.01),
        'conv_ln_g': 1.0 + nrm((L, CONV_WIDTH), 0.02),
        'conv_ln_b': nrm((L, CONV_WIDTH), 0.01),
        'conv_w_proj': nrm((L, CONV_WIDTH, D), CONV_WIDTH ** -0.5),
        'pool_w_group': nrm((L, len(POOL_WINDOWS), POOL_GROUP, POOL_GROUP), POOL_GROUP ** -0.5),
        'pool_scale': 1.0 + nrm((L, POOL_WIDTH), 0.02),
        'pool_w_proj': nrm((L, POOL_WIDTH, D), POOL_WIDTH ** -0.5),
        'w_out': nrm((L, D, D), D ** -0.5),
        'norm2': 1.0 + nrm((L, D), 0.02),
        'ffn_w_gate': nrm((L, D, FFN_HIDDEN), D ** -0.5),
        'ffn_w_up': nrm((L, D, FFN_HIDDEN), D ** -0.5),
        'ffn_w_down': nrm((L, FFN_HIDDEN, D), FFN_HIDDEN ** -0.5),
        'final_norm': 1.0 + nrm((D,), 0.02),
    }
    return inputs


def _fwd_reference(x, norm1, w_in, b_gate, ssm_a_re, ssm_a_im, ssm_log_dt, ssm_b_re, ssm_b_im,
              ssm_c_re, ssm_c_im, ssm_d, ssm_w_glu, ssm_b_glu, ssm_w_proj, conv_w_dw,
              conv_b_dw, conv_ln_g, conv_ln_b, conv_w_proj, pool_w_group, pool_scale,
              pool_w_proj, w_out, norm2, ffn_w_gate, ffn_w_up, ffn_w_down, final_norm):
    for l in range(DEPTH):
        x = hybrid_layer(x, norm1[l], w_in[l], b_gate[l], ssm_a_re[l], ssm_a_im[l],
                         ssm_log_dt[l], ssm_b_re[l], ssm_b_im[l], ssm_c_re[l], ssm_c_im[l],
                         ssm_d[l], ssm_w_glu[l], ssm_b_glu[l], ssm_w_proj[l], conv_w_dw[l],
                         conv_b_dw[l], conv_ln_g[l], conv_ln_b[l], conv_w_proj[l],
                         pool_w_group[l], pool_scale[l], pool_w_proj[l], w_out[l], norm2[l],
                         ffn_w_gate[l], ffn_w_up[l], ffn_w_down[l])
    return rms_norm(x, final_norm)


import jax as _jax
import jax.numpy as _jnp

TWIN_FORMAT = 'train_step'
FWD_PARAMS = ['x', 'norm1', 'w_in', 'b_gate', 'ssm_a_re', 'ssm_a_im', 'ssm_log_dt', 'ssm_b_re', 'ssm_b_im', 'ssm_c_re', 'ssm_c_im', 'ssm_d', 'ssm_w_glu', 'ssm_b_glu', 'ssm_w_proj', 'conv_w_dw', 'conv_b_dw', 'conv_ln_g', 'conv_ln_b', 'conv_w_proj', 'pool_w_group', 'pool_scale', 'pool_w_proj', 'w_out', 'norm2', 'ffn_w_gate', 'ffn_w_up', 'ffn_w_down', 'final_norm']
TWIN_WEIGHTS = ['norm1', 'w_in', 'b_gate', 'ssm_a_re', 'ssm_a_im', 'ssm_log_dt', 'ssm_b_re', 'ssm_b_im', 'ssm_c_re', 'ssm_c_im', 'ssm_d', 'ssm_w_glu', 'ssm_b_glu', 'ssm_w_proj', 'conv_w_dw', 'conv_b_dw', 'conv_ln_g', 'conv_ln_b', 'conv_w_proj', 'pool_w_group', 'pool_scale', 'pool_w_proj', 'w_out', 'norm2', 'ffn_w_gate', 'ffn_w_up', 'ffn_w_down', 'final_norm']
TWIN_DIFF_INPUT = 'x'
TWIN_INPUTS = ['x', 'norm1', 'w_in', 'b_gate', 'ssm_a_re', 'ssm_a_im', 'ssm_log_dt', 'ssm_b_re', 'ssm_b_im', 'ssm_c_re', 'ssm_c_im', 'ssm_d', 'ssm_w_glu', 'ssm_b_glu', 'ssm_w_proj', 'conv_w_dw', 'conv_b_dw', 'conv_ln_g', 'conv_ln_b', 'conv_w_proj', 'pool_w_group', 'pool_scale', 'pool_w_proj', 'w_out', 'norm2', 'ffn_w_gate', 'ffn_w_up', 'ffn_w_down', 'final_norm', 'loss_target', 'm_norm1', 'm_w_in', 'm_b_gate', 'm_ssm_a_re', 'm_ssm_a_im', 'm_ssm_log_dt', 'm_ssm_b_re', 'm_ssm_b_im', 'm_ssm_c_re', 'm_ssm_c_im', 'm_ssm_d', 'm_ssm_w_glu', 'm_ssm_b_glu', 'm_ssm_w_proj', 'm_conv_w_dw', 'm_conv_b_dw', 'm_conv_ln_g', 'm_conv_ln_b', 'm_conv_w_proj', 'm_pool_w_group', 'm_pool_scale', 'm_pool_w_proj', 'm_w_out', 'm_norm2', 'm_ffn_w_gate', 'm_ffn_w_up', 'm_ffn_w_down', 'm_final_norm', 'v_norm1', 'v_w_in', 'v_b_gate', 'v_ssm_a_re', 'v_ssm_a_im', 'v_ssm_log_dt', 'v_ssm_b_re', 'v_ssm_b_im', 'v_ssm_c_re', 'v_ssm_c_im', 'v_ssm_d', 'v_ssm_w_glu', 'v_ssm_b_glu', 'v_ssm_w_proj', 'v_conv_w_dw', 'v_conv_b_dw', 'v_conv_ln_g', 'v_conv_ln_b', 'v_conv_w_proj', 'v_pool_w_group', 'v_pool_scale', 'v_pool_w_proj', 'v_w_out', 'v_norm2', 'v_ffn_w_gate', 'v_ffn_w_up', 'v_ffn_w_down', 'v_final_norm']
TWIN_OUTPUTS = ['loss', 'grad_x', 'grad_norm1', 'grad_w_in', 'grad_b_gate', 'grad_ssm_a_re', 'grad_ssm_a_im', 'grad_ssm_log_dt', 'grad_ssm_b_re', 'grad_ssm_b_im', 'grad_ssm_c_re', 'grad_ssm_c_im', 'grad_ssm_d', 'grad_ssm_w_glu', 'grad_ssm_b_glu', 'grad_ssm_w_proj', 'grad_conv_w_dw', 'grad_conv_b_dw', 'grad_conv_ln_g', 'grad_conv_ln_b', 'grad_conv_w_proj', 'grad_pool_w_group', 'grad_pool_scale', 'grad_pool_w_proj', 'grad_w_out', 'grad_norm2', 'grad_ffn_w_gate', 'grad_ffn_w_up', 'grad_ffn_w_down', 'grad_final_norm', 'delta_norm1', 'delta_w_in', 'delta_b_gate', 'delta_ssm_a_re', 'delta_ssm_a_im', 'delta_ssm_log_dt', 'delta_ssm_b_re', 'delta_ssm_b_im', 'delta_ssm_c_re', 'delta_ssm_c_im', 'delta_ssm_d', 'delta_ssm_w_glu', 'delta_ssm_b_glu', 'delta_ssm_w_proj', 'delta_conv_w_dw', 'delta_conv_b_dw', 'delta_conv_ln_g', 'delta_conv_ln_b', 'delta_conv_w_proj', 'delta_pool_w_group', 'delta_pool_scale', 'delta_pool_w_proj', 'delta_w_out', 'delta_norm2', 'delta_ffn_w_gate', 'delta_ffn_w_up', 'delta_ffn_w_down', 'delta_final_norm', 'new_m_norm1', 'new_m_w_in', 'new_m_b_gate', 'new_m_ssm_a_re', 'new_m_ssm_a_im', 'new_m_ssm_log_dt', 'new_m_ssm_b_re', 'new_m_ssm_b_im', 'new_m_ssm_c_re', 'new_m_ssm_c_im', 'new_m_ssm_d', 'new_m_ssm_w_glu', 'new_m_ssm_b_glu', 'new_m_ssm_w_proj', 'new_m_conv_w_dw', 'new_m_conv_b_dw', 'new_m_conv_ln_g', 'new_m_conv_ln_b', 'new_m_conv_w_proj', 'new_m_pool_w_group', 'new_m_pool_scale', 'new_m_pool_w_proj', 'new_m_w_out', 'new_m_norm2', 'new_m_ffn_w_gate', 'new_m_ffn_w_up', 'new_m_ffn_w_down', 'new_m_final_norm', 'new_v_norm1', 'new_v_w_in', 'new_v_b_gate', 'new_v_ssm_a_re', 'new_v_ssm_a_im', 'new_v_ssm_log_dt', 'new_v_ssm_b_re', 'new_v_ssm_b_im', 'new_v_ssm_c_re', 'new_v_ssm_c_im', 'new_v_ssm_d', 'new_v_ssm_w_glu', 'new_v_ssm_b_glu', 'new_v_ssm_w_proj', 'new_v_conv_w_dw', 'new_v_conv_b_dw', 'new_v_conv_ln_g', 'new_v_conv_ln_b', 'new_v_conv_w_proj', 'new_v_pool_w_group', 'new_v_pool_scale', 'new_v_pool_w_proj', 'new_v_w_out', 'new_v_norm2', 'new_v_ffn_w_gate', 'new_v_ffn_w_up', 'new_v_ffn_w_down', 'new_v_final_norm']
TWIN_LEAF_KINDS = {'loss': 'loss', 'grad_x': 'grad_x', 'grad_norm1': 'grad_w', 'grad_w_in': 'grad_w', 'grad_b_gate': 'grad_w', 'grad_ssm_a_re': 'grad_w', 'grad_ssm_a_im': 'grad_w', 'grad_ssm_log_dt': 'grad_w', 'grad_ssm_b_re': 'grad_w', 'grad_ssm_b_im': 'grad_w', 'grad_ssm_c_re': 'grad_w', 'grad_ssm_c_im': 'grad_w', 'grad_ssm_d': 'grad_w', 'grad_ssm_w_glu': 'grad_w', 'grad_ssm_b_glu': 'grad_w', 'grad_ssm_w_proj': 'grad_w', 'grad_conv_w_dw': 'grad_w', 'grad_conv_b_dw': 'grad_w', 'grad_conv_ln_g': 'grad_w', 'grad_conv_ln_b': 'grad_w', 'grad_conv_w_proj': 'grad_w', 'grad_pool_w_group': 'grad_w', 'grad_pool_scale': 'grad_w', 'grad_pool_w_proj': 'grad_w', 'grad_w_out': 'grad_w', 'grad_norm2': 'grad_w', 'grad_ffn_w_gate': 'grad_w', 'grad_ffn_w_up': 'grad_w', 'grad_ffn_w_down': 'grad_w', 'grad_final_norm': 'grad_w', 'delta_norm1': 'delta_w', 'delta_w_in': 'delta_w', 'delta_b_gate': 'delta_w', 'delta_ssm_a_re': 'delta_w', 'delta_ssm_a_im': 'delta_w', 'delta_ssm_log_dt': 'delta_w', 'delta_ssm_b_re': 'delta_w', 'delta_ssm_b_im': 'delta_w', 'delta_ssm_c_re': 'delta_w', 'delta_ssm_c_im': 'delta_w', 'delta_ssm_d': 'delta_w', 'delta_ssm_w_glu': 'delta_w', 'delta_ssm_b_glu': 'delta_w', 'delta_ssm_w_proj': 'delta_w', 'delta_conv_w_dw': 'delta_w', 'delta_conv_b_dw': 'delta_w', 'delta_conv_ln_g': 'delta_w', 'delta_conv_ln_b': 'delta_w', 'delta_conv_w_proj': 'delta_w', 'delta_pool_w_group': 'delta_w', 'delta_pool_scale': 'delta_w', 'delta_pool_w_proj': 'delta_w', 'delta_w_out': 'delta_w', 'delta_norm2': 'delta_w', 'delta_ffn_w_gate': 'delta_w', 'delta_ffn_w_up': 'delta_w', 'delta_ffn_w_down': 'delta_w', 'delta_final_norm': 'delta_w', 'new_m_norm1': 'new_m', 'new_m_w_in': 'new_m', 'new_m_b_gate': 'new_m', 'new_m_ssm_a_re': 'new_m', 'new_m_ssm_a_im': 'new_m', 'new_m_ssm_log_dt': 'new_m', 'new_m_ssm_b_re': 'new_m', 'new_m_ssm_b_im': 'new_m', 'new_m_ssm_c_re': 'new_m', 'new_m_ssm_c_im': 'new_m', 'new_m_ssm_d': 'new_m', 'new_m_ssm_w_glu': 'new_m', 'new_m_ssm_b_glu': 'new_m', 'new_m_ssm_w_proj': 'new_m', 'new_m_conv_w_dw': 'new_m', 'new_m_conv_b_dw': 'new_m', 'new_m_conv_ln_g': 'new_m', 'new_m_conv_ln_b': 'new_m', 'new_m_conv_w_proj': 'new_m', 'new_m_pool_w_group': 'new_m', 'new_m_pool_scale': 'new_m', 'new_m_pool_w_proj': 'new_m', 'new_m_w_out': 'new_m', 'new_m_norm2': 'new_m', 'new_m_ffn_w_gate': 'new_m', 'new_m_ffn_w_up': 'new_m', 'new_m_ffn_w_down': 'new_m', 'new_m_final_norm': 'new_m', 'new_v_norm1': 'new_v', 'new_v_w_in': 'new_v', 'new_v_b_gate': 'new_v', 'new_v_ssm_a_re': 'new_v', 'new_v_ssm_a_im': 'new_v', 'new_v_ssm_log_dt': 'new_v', 'new_v_ssm_b_re': 'new_v', 'new_v_ssm_b_im': 'new_v', 'new_v_ssm_c_re': 'new_v', 'new_v_ssm_c_im': 'new_v', 'new_v_ssm_d': 'new_v', 'new_v_ssm_w_glu': 'new_v', 'new_v_ssm_b_glu': 'new_v', 'new_v_ssm_w_proj': 'new_v', 'new_v_conv_w_dw': 'new_v', 'new_v_conv_b_dw': 'new_v', 'new_v_conv_ln_g': 'new_v', 'new_v_conv_ln_b': 'new_v', 'new_v_conv_w_proj': 'new_v', 'new_v_pool_w_group': 'new_v', 'new_v_pool_scale': 'new_v', 'new_v_pool_w_proj': 'new_v', 'new_v_w_out': 'new_v', 'new_v_norm2': 'new_v', 'new_v_ffn_w_gate': 'new_v', 'new_v_ffn_w_up': 'new_v', 'new_v_ffn_w_down': 'new_v', 'new_v_final_norm': 'new_v'}


def _forward(args):
    return _fwd_reference(*[args[k] for k in FWD_PARAMS])


def _output_shape():
    out = _jax.eval_shape(lambda: _forward(_fwd_setup_inputs(0)))
    return out.shape, out.dtype

N_MICROBATCH = 1
ADAM_LR = 0.001
ADAM_B1 = 0.9
ADAM_B2 = 0.999
ADAM_EPS = 1e-08
ADAM_WD = 0.01
ADAM_STEP = 10
PER_EXAMPLE_BATCH_AXIS = {'x': 0, 'loss_target': 0}
SHARED_INPUTS = []
_WEIGHT_DTYPES = {'norm1': _jnp.float32, 'w_in': _jnp.float32, 'b_gate': _jnp.float32, 'ssm_a_re': _jnp.float32, 'ssm_a_im': _jnp.float32, 'ssm_log_dt': _jnp.float32, 'ssm_b_re': _jnp.float32, 'ssm_b_im': _jnp.float32, 'ssm_c_re': _jnp.float32, 'ssm_c_im': _jnp.float32, 'ssm_d': _jnp.float32, 'ssm_w_glu': _jnp.float32, 'ssm_b_glu': _jnp.float32, 'ssm_w_proj': _jnp.float32, 'conv_w_dw': _jnp.float32, 'conv_b_dw': _jnp.float32, 'conv_ln_g': _jnp.float32, 'conv_ln_b': _jnp.float32, 'conv_w_proj': _jnp.float32, 'pool_w_group': _jnp.float32, 'pool_scale': _jnp.float32, 'pool_w_proj': _jnp.float32, 'w_out': _jnp.float32, 'norm2': _jnp.float32, 'ffn_w_gate': _jnp.float32, 'ffn_w_up': _jnp.float32, 'ffn_w_down': _jnp.float32, 'final_norm': _jnp.float32}
MOMENT_SCALE = {'norm1': 7.725712e-02, 'w_in': 3.414486e-02, 'b_gate': 1.598475e-02, 'ssm_a_re': 6.458475e-03, 'ssm_a_im': 6.633910e-03, 'ssm_log_dt': 4.110025e+00, 'ssm_b_re': 4.389571e-03, 'ssm_b_im': 4.244869e-03, 'ssm_c_re': 2.180384e-03, 'ssm_c_im': 2.146873e-03, 'ssm_d': 3.369441e-02, 'ssm_w_glu': 9.528739e-03, 'ssm_b_glu': 1.412661e-02, 'ssm_w_proj': 2.198707e-02, 'conv_w_dw': 5.396597e-02, 'conv_b_dw': 1.130382e-01, 'conv_ln_g': 6.167125e-02, 'conv_ln_b': 5.555352e-02, 'conv_w_proj': 3.669039e-02, 'pool_w_group': 7.645099e-02, 'pool_scale': 7.443476e-02, 'pool_w_proj': 5.381478e-02, 'w_out': 6.876872e-02, 'norm2': 8.388168e-02, 'ffn_w_gate': 3.463064e-02, 'ffn_w_up': 3.350759e-02, 'ffn_w_down': 5.556729e-02, 'final_norm': 1.599735e+01}


def _to_microbatches(a, axis):
    t = _jnp.moveaxis(a, axis, 0)
    t = t.reshape((N_MICROBATCH, t.shape[0] // N_MICROBATCH) + t.shape[1:])
    return _jnp.moveaxis(t, 1, axis + 1)


def setup_inputs(seed: int = 0) -> dict:
    inp = _fwd_setup_inputs(seed)
    key = _jax.random.fold_in(_jax.random.key(seed), 7919)
    shape, _ = _output_shape()
    out = dict(inp)
    out["loss_target"] = _jax.random.normal(_jax.random.fold_in(key, 0), shape, _jnp.float32)
    for i, name in enumerate(TWIN_WEIGHTS):
        w = inp[name].astype(_jnp.float32)
        if MOMENT_SCALE is None:
            s = _jnp.sqrt(_jnp.mean(_jnp.square(w)) + 1e-30)
        else:
            s = MOMENT_SCALE[name]
        km, kv = _jax.random.split(_jax.random.fold_in(key, i + 1))
        out[name] = w
        out["m_" + name] = s * _jax.random.normal(km, w.shape, _jnp.float32)
        out["v_" + name] = (s * s) * _jax.random.uniform(kv, w.shape, _jnp.float32, 0.5, 1.5)
    if N_MICROBATCH > 1:
        for name, axis in PER_EXAMPLE_BATCH_AXIS.items():
            out[name] = _to_microbatches(out[name], axis)
    return {'x': out['x'], 'norm1': out['norm1'], 'w_in': out['w_in'], 'b_gate': out['b_gate'], 'ssm_a_re': out['ssm_a_re'], 'ssm_a_im': out['ssm_a_im'], 'ssm_log_dt': out['ssm_log_dt'], 'ssm_b_re': out['ssm_b_re'], 'ssm_b_im': out['ssm_b_im'], 'ssm_c_re': out['ssm_c_re'], 'ssm_c_im': out['ssm_c_im'], 'ssm_d': out['ssm_d'], 'ssm_w_glu': out['ssm_w_glu'], 'ssm_b_glu': out['ssm_b_glu'], 'ssm_w_proj': out['ssm_w_proj'], 'conv_w_dw': out['conv_w_dw'], 'conv_b_dw': out['conv_b_dw'], 'conv_ln_g': out['conv_ln_g'], 'conv_ln_b': out['conv_ln_b'], 'conv_w_proj': out['conv_w_proj'], 'pool_w_group': out['pool_w_group'], 'pool_scale': out['pool_scale'], 'pool_w_proj': out['pool_w_proj'], 'w_out': out['w_out'], 'norm2': out['norm2'], 'ffn_w_gate': out['ffn_w_gate'], 'ffn_w_up': out['ffn_w_up'], 'ffn_w_down': out['ffn_w_down'], 'final_norm': out['final_norm'], 'loss_target': out['loss_target'], 'm_norm1': out['m_norm1'], 'm_w_in': out['m_w_in'], 'm_b_gate': out['m_b_gate'], 'm_ssm_a_re': out['m_ssm_a_re'], 'm_ssm_a_im': out['m_ssm_a_im'], 'm_ssm_log_dt': out['m_ssm_log_dt'], 'm_ssm_b_re': out['m_ssm_b_re'], 'm_ssm_b_im': out['m_ssm_b_im'], 'm_ssm_c_re': out['m_ssm_c_re'], 'm_ssm_c_im': out['m_ssm_c_im'], 'm_ssm_d': out['m_ssm_d'], 'm_ssm_w_glu': out['m_ssm_w_glu'], 'm_ssm_b_glu': out['m_ssm_b_glu'], 'm_ssm_w_proj': out['m_ssm_w_proj'], 'm_conv_w_dw': out['m_conv_w_dw'], 'm_conv_b_dw': out['m_conv_b_dw'], 'm_conv_ln_g': out['m_conv_ln_g'], 'm_conv_ln_b': out['m_conv_ln_b'], 'm_conv_w_proj': out['m_conv_w_proj'], 'm_pool_w_group': out['m_pool_w_group'], 'm_pool_scale': out['m_pool_scale'], 'm_pool_w_proj': out['m_pool_w_proj'], 'm_w_out': out['m_w_out'], 'm_norm2': out['m_norm2'], 'm_ffn_w_gate': out['m_ffn_w_gate'], 'm_ffn_w_up': out['m_ffn_w_up'], 'm_ffn_w_down': out['m_ffn_w_down'], 'm_final_norm': out['m_final_norm'], 'v_norm1': out['v_norm1'], 'v_w_in': out['v_w_in'], 'v_b_gate': out['v_b_gate'], 'v_ssm_a_re': out['v_ssm_a_re'], 'v_ssm_a_im': out['v_ssm_a_im'], 'v_ssm_log_dt': out['v_ssm_log_dt'], 'v_ssm_b_re': out['v_ssm_b_re'], 'v_ssm_b_im': out['v_ssm_b_im'], 'v_ssm_c_re': out['v_ssm_c_re'], 'v_ssm_c_im': out['v_ssm_c_im'], 'v_ssm_d': out['v_ssm_d'], 'v_ssm_w_glu': out['v_ssm_w_glu'], 'v_ssm_b_glu': out['v_ssm_b_glu'], 'v_ssm_w_proj': out['v_ssm_w_proj'], 'v_conv_w_dw': out['v_conv_w_dw'], 'v_conv_b_dw': out['v_conv_b_dw'], 'v_conv_ln_g': out['v_conv_ln_g'], 'v_conv_ln_b': out['v_conv_ln_b'], 'v_conv_w_proj': out['v_conv_w_proj'], 'v_pool_w_group': out['v_pool_w_group'], 'v_pool_scale': out['v_pool_scale'], 'v_pool_w_proj': out['v_pool_w_proj'], 'v_w_out': out['v_w_out'], 'v_norm2': out['v_norm2'], 'v_ffn_w_gate': out['v_ffn_w_gate'], 'v_ffn_w_up': out['v_ffn_w_up'], 'v_ffn_w_down': out['v_ffn_w_down'], 'v_final_norm': out['v_final_norm']}


def _loss(weights, diff, rest, loss_target):
    with _jax.named_scope("forward"):
        args = {**rest, TWIN_DIFF_INPUT: diff, **{k: w.astype(_WEIGHT_DTYPES[k]) for k, w in weights.items()}}
        y = _forward(args)
    with _jax.named_scope("loss_head"):
        err = _jnp.square(y.astype(_jnp.float32) - loss_target)
        return 0.5 * _jnp.sum(_jnp.mean(err, axis=-1)) if err.ndim else 0.5 * err


def _adamw(w, g, m, v):
    m = ADAM_B1 * m + (1.0 - ADAM_B1) * g
    v = ADAM_B2 * v + (1.0 - ADAM_B2) * _jnp.square(g)
    m_hat = m / (1.0 - ADAM_B1 ** ADAM_STEP)
    v_hat = v / (1.0 - ADAM_B2 ** ADAM_STEP)
    delta = -ADAM_LR * (m_hat / (_jnp.sqrt(v_hat) + ADAM_EPS) + ADAM_WD * w)
    return delta, m, v


def reference(x, norm1, w_in, b_gate, ssm_a_re, ssm_a_im, ssm_log_dt, ssm_b_re, ssm_b_im, ssm_c_re, ssm_c_im, ssm_d, ssm_w_glu, ssm_b_glu, ssm_w_proj, conv_w_dw, conv_b_dw, conv_ln_g, conv_ln_b, conv_w_proj, pool_w_group, pool_scale, pool_w_proj, w_out, norm2, ffn_w_gate, ffn_w_up, ffn_w_down, final_norm, loss_target, m_norm1, m_w_in, m_b_gate, m_ssm_a_re, m_ssm_a_im, m_ssm_log_dt, m_ssm_b_re, m_ssm_b_im, m_ssm_c_re, m_ssm_c_im, m_ssm_d, m_ssm_w_glu, m_ssm_b_glu, m_ssm_w_proj, m_conv_w_dw, m_conv_b_dw, m_conv_ln_g, m_conv_ln_b, m_conv_w_proj, m_pool_w_group, m_pool_scale, m_pool_w_proj, m_w_out, m_norm2, m_ffn_w_gate, m_ffn_w_up, m_ffn_w_down, m_final_norm, v_norm1, v_w_in, v_b_gate, v_ssm_a_re, v_ssm_a_im, v_ssm_log_dt, v_ssm_b_re, v_ssm_b_im, v_ssm_c_re, v_ssm_c_im, v_ssm_d, v_ssm_w_glu, v_ssm_b_glu, v_ssm_w_proj, v_conv_w_dw, v_conv_b_dw, v_conv_ln_g, v_conv_ln_b, v_conv_w_proj, v_pool_w_group, v_pool_scale, v_pool_w_proj, v_w_out, v_norm2, v_ffn_w_gate, v_ffn_w_up, v_ffn_w_down, v_final_norm):
    given = dict(x=x, norm1=norm1, w_in=w_in, b_gate=b_gate, ssm_a_re=ssm_a_re, ssm_a_im=ssm_a_im, ssm_log_dt=ssm_log_dt, ssm_b_re=ssm_b_re, ssm_b_im=ssm_b_im, ssm_c_re=ssm_c_re, ssm_c_im=ssm_c_im, ssm_d=ssm_d, ssm_w_glu=ssm_w_glu, ssm_b_glu=ssm_b_glu, ssm_w_proj=ssm_w_proj, conv_w_dw=conv_w_dw, conv_b_dw=conv_b_dw, conv_ln_g=conv_ln_g, conv_ln_b=conv_ln_b, conv_w_proj=conv_w_proj, pool_w_group=pool_w_group, pool_scale=pool_scale, pool_w_proj=pool_w_proj, w_out=w_out, norm2=norm2, ffn_w_gate=ffn_w_gate, ffn_w_up=ffn_w_up, ffn_w_down=ffn_w_down, final_norm=final_norm, loss_target=loss_target, m_norm1=m_norm1, m_w_in=m_w_in, m_b_gate=m_b_gate, m_ssm_a_re=m_ssm_a_re, m_ssm_a_im=m_ssm_a_im, m_ssm_log_dt=m_ssm_log_dt, m_ssm_b_re=m_ssm_b_re, m_ssm_b_im=m_ssm_b_im, m_ssm_c_re=m_ssm_c_re, m_ssm_c_im=m_ssm_c_im, m_ssm_d=m_ssm_d, m_ssm_w_glu=m_ssm_w_glu, m_ssm_b_glu=m_ssm_b_glu, m_ssm_w_proj=m_ssm_w_proj, m_conv_w_dw=m_conv_w_dw, m_conv_b_dw=m_conv_b_dw, m_conv_ln_g=m_conv_ln_g, m_conv_ln_b=m_conv_ln_b, m_conv_w_proj=m_conv_w_proj, m_pool_w_group=m_pool_w_group, m_pool_scale=m_pool_scale, m_pool_w_proj=m_pool_w_proj, m_w_out=m_w_out, m_norm2=m_norm2, m_ffn_w_gate=m_ffn_w_gate, m_ffn_w_up=m_ffn_w_up, m_ffn_w_down=m_ffn_w_down, m_final_norm=m_final_norm, v_norm1=v_norm1, v_w_in=v_w_in, v_b_gate=v_b_gate, v_ssm_a_re=v_ssm_a_re, v_ssm_a_im=v_ssm_a_im, v_ssm_log_dt=v_ssm_log_dt, v_ssm_b_re=v_ssm_b_re, v_ssm_b_im=v_ssm_b_im, v_ssm_c_re=v_ssm_c_re, v_ssm_c_im=v_ssm_c_im, v_ssm_d=v_ssm_d, v_ssm_w_glu=v_ssm_w_glu, v_ssm_b_glu=v_ssm_b_glu, v_ssm_w_proj=v_ssm_w_proj, v_conv_w_dw=v_conv_w_dw, v_conv_b_dw=v_conv_b_dw, v_conv_ln_g=v_conv_ln_g, v_conv_ln_b=v_conv_ln_b, v_conv_w_proj=v_conv_w_proj, v_pool_w_group=v_pool_w_group, v_pool_scale=v_pool_scale, v_pool_w_proj=v_pool_w_proj, v_w_out=v_w_out, v_norm2=v_norm2, v_ffn_w_gate=v_ffn_w_gate, v_ffn_w_up=v_ffn_w_up, v_ffn_w_down=v_ffn_w_down, v_final_norm=v_final_norm)
    weights = {n: given[n] for n in TWIN_WEIGHTS}
    shared = {n: given[n] for n in SHARED_INPUTS}
    per_example = {n: given[n] for n in ['x']}
    grad_fn = _jax.value_and_grad(_loss, argnums=(0, 1))

    def one_microbatch(ex, loss_target):
        ex = dict(ex)
        diff = ex.pop(TWIN_DIFF_INPUT)
        return grad_fn(weights, diff, {**shared, **ex}, loss_target)

    if N_MICROBATCH == 1:
        loss, (grad_w, grad_x) = one_microbatch(per_example, given["loss_target"])
    else:
        def body(carry, xs):
            loss_sum, grad_sum = carry
            l_k, (gw_k, gx_k) = one_microbatch(xs[0], xs[1])
            with _jax.named_scope("update"):
                return (loss_sum + l_k, _jax.tree.map(_jnp.add, grad_sum, gw_k)), gx_k

        init = (_jnp.zeros((), _jnp.float32), _jax.tree.map(_jnp.zeros_like, weights))
        (loss, grad_w), grad_x = _jax.lax.scan(body, init, (per_example, given["loss_target"]))
    with _jax.named_scope("update"):
        delta_w, new_m, new_v = {}, {}, {}
        for n in TWIN_WEIGHTS:
            delta_w[n], new_m[n], new_v[n] = _adamw(weights[n], grad_w[n], given["m_" + n], given["v_" + n])
    return (loss, grad_x, *[grad_w[n] for n in TWIN_WEIGHTS], *[delta_w[n] for n in TWIN_WEIGHTS],
            *[new_m[n] for n in TWIN_WEIGHTS], *[new_v[n] for n in TWIN_WEIGHTS])
```

```python
import functools
import math

import jax
import jax.numpy as jnp
from jax import lax
from jax.experimental import pallas as pl
from jax.experimental.pallas import tpu as pltpu

F32 = jnp.float32
MXU_DTYPE = jnp.bfloat16
SDS = jax.ShapeDtypeStruct
BS = pl.BlockSpec
ANY = pl.BlockSpec(memory_space=pl.ANY)
MESH = pl.DeviceIdType.MESH

EPS = 1e-6
N_CHIPS = 4
N_LAYERS = 2
SSM_GROUPS, SSM_STATE, SSM_GROUP = 32, 64, 16
CONV_KERNEL = 31
CONV_PAD = 32
POOL_WINDOWS = (2, 4, 8, 16)
GELU_C = math.sqrt(2.0 / math.pi)
ADAM_LR, ADAM_B1, ADAM_B2, ADAM_EPS, ADAM_WD, ADAM_STEP = 0.001, 0.9, 0.999, 1e-08, 0.01, 10
VMEM_LIMIT = 56 * 1024 * 1024

BIG = ("w_in", "ssm_w_glu", "ssm_w_proj", "conv_w_proj", "pool_w_proj", "w_out", "ffn_w_gate", "ffn_w_up", "ffn_w_down")
SMALL = ("norm1", "b_gate", "ssm_a_re", "ssm_a_im", "ssm_log_dt", "ssm_b_re", "ssm_b_im", "ssm_c_re", "ssm_c_im",
         "ssm_d", "ssm_b_glu", "conv_b_dw", "conv_ln_g", "conv_ln_b", "pool_w_group", "pool_scale", "norm2",
         "final_norm")
WEIGHTS = ("norm1", "w_in", "b_gate", "ssm_a_re", "ssm_a_im", "ssm_log_dt", "ssm_b_re", "ssm_b_im", "ssm_c_re",
           "ssm_c_im", "ssm_d", "ssm_w_glu", "ssm_b_glu", "ssm_w_proj", "conv_w_dw", "conv_b_dw", "conv_ln_g",
           "conv_ln_b", "conv_w_proj", "pool_w_group", "pool_scale", "pool_w_proj", "w_out", "norm2", "ffn_w_gate",
           "ffn_w_up", "ffn_w_down", "final_norm")


def _params(vmem=True):
    return pltpu.CompilerParams(vmem_limit_bytes=VMEM_LIMIT) if vmem else None


def _mm(a, b):
    return jnp.dot(a.astype(MXU_DTYPE), b.astype(MXU_DTYPE), preferred_element_type=F32)


def _mm_nt(a, b):
    return lax.dot_general(a.astype(MXU_DTYPE), b.astype(MXU_DTYPE), (((1,), (1,)), ((), ())),
                           preferred_element_type=F32)


def _mm_tn(a, b):
    return lax.dot_general(a.astype(MXU_DTYPE), b.astype(MXU_DTYPE), (((0,), (0,)), ((), ())),
                           preferred_element_type=F32)


def _sigmoid(x):
    return jax.nn.sigmoid(x)


def _gelu(x):
    t = jnp.tanh(GELU_C * (x + 0.044715 * (x * x * x)))
    return x * (0.5 * (1.0 + t)), t


def _gelu_grad(x, t):
    return 0.5 * (1.0 + t) + 0.5 * x * (1.0 - t * t) * (GELU_C * (1.0 + 3.0 * 0.044715 * x * x))


def _colsum(v):
    return jnp.sum(v, axis=0, keepdims=True)


def _row_tile(rows, cols, itemsize=4, budget=1536 * 1024):
    best = None
    for t in range(8, rows + 1, 8):
        if rows % t == 0 and t * cols * itemsize <= budget:
            best = t
    return best if best is not None else rows


def _in_proj(l, x, norm1, w_in):
    s, d = x.shape
    nc = w_in.shape[-1]
    tm = min(512, s)

    def body(x_ref, g_ref, w_ref, z_ref, h_ref):
        @pl.when(pl.program_id(1) == 0)
        def _():
            xv = x_ref[...]
            r = lax.rsqrt(jnp.mean(xv * xv, axis=-1, keepdims=True) + EPS)
            h_ref[...] = (xv * r * g_ref[...]).astype(h_ref.dtype)

        z_ref[...] = _mm(h_ref[...], w_ref[...])

    return pl.pallas_call(
        body, name=f"in_proj_l{l}", grid=(s // tm, N_CHIPS),
        in_specs=[BS((tm, d), lambda i, j: (i, 0)), BS((None, 1, d), lambda i, j: (l, 0, 0)),
                  BS((None, None, d, nc), lambda i, j: (j, l, 0, 0))],
        out_specs=[BS((tm, nc), lambda i, j: (i, j)), BS((tm, d), lambda i, j: (i, 0))],
        out_shape=[SDS((s, N_CHIPS * nc), F32), SDS((s, d), MXU_DTYPE)],
        compiler_params=_params())(x, norm1, w_in)


def _mm_cols(a, w_ref):
    return jnp.concatenate([_mm(a, w_ref[j]) for j in range(N_CHIPS)], axis=1)


def _mm_nt_cols(dv, w_ref):
    nc = w_ref.shape[-1]
    acc = _mm_nt(dv[:, 0:nc], w_ref[0])
    for j in range(1, N_CHIPS):
        acc = acc + _mm_nt(dv[:, j * nc:(j + 1) * nc], w_ref[j])
    return acc


def _merge_values(y, hc, p, zg, wglu, bglu, wpa, wpb, wpc, lng, lnb, wgrp, scale, bg):
    v = {}
    ge, th = _gelu(y)
    t = _mm(ge, wglu) + bglu
    sg = _sigmoid(t)
    sa = ge * sg
    ya = _mm_cols(sa, wpa)
    mu = jnp.mean(hc, axis=-1, keepdims=True)
    xc = hc - mu
    r = lax.rsqrt(jnp.mean(xc * xc, axis=-1, keepdims=True) + EPS)
    xh = xc * r
    ln = xh * lng + lnb
    sl = _sigmoid(ln)
    ac = ln * sl
    yb = _mm_cols(ac, wpb)
    gw = p.shape[1] // len(POOL_WINDOWS)
    q = jnp.concatenate([_mm(p[:, k * gw:(k + 1) * gw], wgrp[k]) for k in range(len(POOL_WINDOWS))], axis=1)
    pp = q * scale
    yc = _mm_cols(pp, wpc)
    d = ya.shape[1]
    gates = [_sigmoid(zg[k] + bg[:, k * d:(k + 1) * d]) for k in range(3)]
    merged = gates[0] * ya + gates[1] * yb + gates[2] * yc
    v.update(ge=ge, th=th, sg=sg, sa=sa, ya=ya, r=r, xh=xh, ln=ln, sl=sl, ac=ac, yb=yb, q=q, pp=pp, yc=yc,
             gates=gates, merged=merged)
    return v


def _merge_specs(l, tm, d, cw):
    row = lambda n: BS((None, 1, n), lambda i: (l, 0, 0))
    return [
        BS((tm, cw), lambda i: (i, 0)),
        BS((tm, cw), lambda i: (i, 0)),
        BS((tm, cw), lambda i: (i, 0)),
        BS((tm, d), lambda i: (i, 2)), BS((tm, d), lambda i: (i, 3)), BS((tm, d), lambda i: (i, 4)),
        BS((N_CHIPS, None, cw // N_CHIPS, cw), lambda i: (0, l, 0, 0)),
        row(cw),
        BS((N_CHIPS, None, cw, d // N_CHIPS), lambda i: (0, l, 0, 0)),
        BS((N_CHIPS, None, cw, d // N_CHIPS), lambda i: (0, l, 0, 0)),
        BS((N_CHIPS, None, cw, d // N_CHIPS), lambda i: (0, l, 0, 0)),
        row(cw), row(cw),
        BS((None, 4, cw // 4, cw // 4), lambda i: (l, 0, 0, 0)),
        row(cw),
        row(3 * d),
        BS((N_CHIPS, None, d // N_CHIPS, d), lambda i: (0, l, 0, 0)),
    ]


def _merge_fwd(l, x, y, hc, p, z, fw, sp):
    s, d = x.shape
    cw = y.shape[1]
    tm = min(256, s)

    def body(x_ref, y_ref, hc_ref, p_ref, z0, z1, z2, wglu, bglu, wpa, wpb, wpc, lng, lnb, wgrp, scale, bg, wout,
             x1_ref):
        v = _merge_values(y_ref[...], hc_ref[...], p_ref[...], (z0[...], z1[...], z2[...]),
                          wglu[...].reshape(cw, cw), bglu[...], wpa, wpb, wpc, lng[...], lnb[...], wgrp, scale[...],
                          bg[...])
        x1_ref[...] = x_ref[...] + _mm(v["merged"], wout[...].reshape(d, d))

    return pl.pallas_call(
        body, name=f"merge_fwd_l{l}", grid=(s // tm,),
        in_specs=[BS((tm, d), lambda i: (i, 0))] + _merge_specs(l, tm, d, cw),
        out_specs=BS((tm, d), lambda i: (i, 0)), out_shape=SDS((s, d), F32), compiler_params=_params(),
    )(x, y, hc, p, z, z, z, fw["ssm_w_glu"], sp["ssm_b_glu"], fw["ssm_w_proj"], fw["conv_w_proj"], fw["pool_w_proj"],
      sp["conv_ln_g"], sp["conv_ln_b"], sp["pool_w_group"], sp["pool_scale"], sp["b_gate"], fw["w_out"])


def _merge_bwd(l, dx1, y, hc, p, z, fw, sp):
    s, d = dx1.shape
    cw = y.shape[1]
    tm = min(256, s)
    m = MXU_DTYPE

    def body(dx1_ref, y_ref, hc_ref, p_ref, z0, z1, z2, wglu, bglu, wpa, wpb, wpc, lng, lnb, wgrp, scale, bg, wout,
             dzg_ref, dy_ref, dhc_ref, dp_ref, merged_ref, sa_ref, ac_ref, pp_ref, ge_ref, dt_ref, dya_ref, dyb_ref,
             dyc_ref, dq_ref, dbg_ref, dbglu_ref, dlng_ref, dlnb_ref, dscale_ref):
        yv = y_ref[...]
        wg = wglu[...].reshape(cw, cw)
        v = _merge_values(yv, hc_ref[...], p_ref[...], (z0[...], z1[...], z2[...]), wg, bglu[...], wpa, wpb, wpc,
                          lng[...], lnb[...], wgrp, scale[...], bg[...])
        dm = _mm_nt(dx1_ref[...], wout[...].reshape(d, d))
        ys = (v["ya"], v["yb"], v["yc"])
        dys = []
        for k in range(3):
            gk = v["gates"][k]
            dzg_ref[:, k * d:(k + 1) * d] = dm * ys[k] * (gk * (1.0 - gk))
            dys.append((dm * gk).astype(m))
        dsa = _mm_nt_cols(dys[0], wpa)
        dac = _mm_nt_cols(dys[1], wpb)
        dpp = _mm_nt_cols(dys[2], wpc)
        ge, sg = v["ge"], v["sg"]
        dt = dsa * ge * (sg * (1.0 - sg))
        dge = dsa * sg + _mm_nt(dt, wg)
        dy_ref[...] = dge * _gelu_grad(yv, v["th"])
        ln, sl, xh = v["ln"], v["sl"], v["xh"]
        dln = dac * (sl * (1.0 + ln * (1.0 - sl)))
        dxh = dln * lng[...]
        dhc_ref[...] = v["r"] * (dxh - jnp.mean(dxh, axis=-1, keepdims=True)
                                 - xh * jnp.mean(dxh * xh, axis=-1, keepdims=True))
        dq = dpp * scale[...]
        gw = cw // len(POOL_WINDOWS)
        for k in range(len(POOL_WINDOWS)):
            dp_ref[:, k * gw:(k + 1) * gw] = _mm_nt(dq[:, k * gw:(k + 1) * gw], wgrp[k])
        merged_ref[...] = v["merged"].astype(m)
        sa_ref[...] = v["sa"].astype(m)
        ac_ref[...] = v["ac"].astype(m)
        pp_ref[...] = v["pp"].astype(m)
        ge_ref[...] = ge.astype(m)
        dt_ref[...] = dt.astype(m)
        dya_ref[...] = dys[0]
        dyb_ref[...] = dys[1]
        dyc_ref[...] = dys[2]
        dq_ref[...] = dq.astype(m)

        @pl.when(pl.program_id(0) == 0)
        def _():
            for ref in (dbg_ref, dbglu_ref, dlng_ref, dlnb_ref, dscale_ref):
                ref[...] = jnp.zeros(ref.shape, F32)

        dbg_ref[...] += _colsum(dzg_ref[...])
        dbglu_ref[...] += _colsum(dt)
        dlng_ref[...] += _colsum(dln * xh)
        dlnb_ref[...] += _colsum(dln)
        dscale_ref[...] += _colsum(dpp * v["q"])

    tile = lambda n: BS((tm, n), lambda i: (i, 0))
    acc = lambda n: BS((1, n), lambda i: (0, 0))
    outs = pl.pallas_call(
        body, name=f"merge_bwd_l{l}", grid=(s // tm,),
        in_specs=[tile(d)] + _merge_specs(l, tm, d, cw),
        out_specs=[tile(3 * d), tile(cw), tile(cw), tile(cw), tile(d), tile(cw), tile(cw), tile(cw), tile(cw), tile(cw),
                   tile(d), tile(d), tile(d), tile(cw), acc(3 * d), acc(cw), acc(cw), acc(cw), acc(cw)],
        out_shape=[SDS((s, 3 * d), F32), SDS((s, cw), F32), SDS((s, cw), F32), SDS((s, cw), F32), SDS((s, d), m),
                   SDS((s, cw), m), SDS((s, cw), m), SDS((s, cw), m), SDS((s, cw), m), SDS((s, cw), m), SDS((s, d), m),
                   SDS((s, d), m), SDS((s, d), m), SDS((s, cw), m), SDS((1, 3 * d), F32), SDS((1, cw), F32),
                   SDS((1, cw), F32), SDS((1, cw), F32), SDS((1, cw), F32)],
        compiler_params=_params(),
    )(dx1, y, hc, p, z, z, z, fw["ssm_w_glu"], sp["ssm_b_glu"], fw["ssm_w_proj"], fw["conv_w_proj"], fw["pool_w_proj"],
      sp["conv_ln_g"], sp["conv_ln_b"], sp["pool_w_group"], sp["pool_scale"], sp["b_gate"], fw["w_out"])
    names = ("dzg", "dy", "dhc", "dp", "merged", "sa", "ac", "pp", "ge", "dt", "dya", "dyb", "dyc", "dq", "db_gate",
             "db_glu", "dln_g", "dln_b", "dscale")
    return dict(zip(names, outs))


def _ffn_fwd(l, x1, norm2, wg, wu, wd):
    s, d = x1.shape
    hc = wg.shape[-1]
    tm = min(512, s)

    def body(x_ref, g_ref, wg_ref, wu_ref, wd_ref, o_ref, h_scr):
        @pl.when(pl.program_id(1) == 0)
        def _():
            xv = x_ref[...]
            r = lax.rsqrt(jnp.mean(xv * xv, axis=-1, keepdims=True) + EPS)
            h_scr[...] = (xv * r * g_ref[...]).astype(h_scr.dtype)
            o_ref[...] = xv

        h = h_scr[...]
        gate = _mm(h, wg_ref[...])
        up = _mm(h, wu_ref[...])
        o_ref[...] += _mm(gate * _sigmoid(gate) * up, wd_ref[...])

    return pl.pallas_call(
        body, name=f"ffn_fwd_l{l}", grid=(s // tm, N_CHIPS),
        in_specs=[BS((tm, d), lambda i, j: (i, 0)), BS((None, 1, d), lambda i, j: (l, 0, 0)),
                  BS((None, None, d, hc), lambda i, j: (j, l, 0, 0)), BS((None, None, d, hc), lambda i, j: (j, l, 0, 0)),
                  BS((None, None, hc, d), lambda i, j: (j, l, 0, 0))],
        out_specs=BS((tm, d), lambda i, j: (i, 0)), out_shape=SDS((s, d), F32),
        scratch_shapes=[pltpu.VMEM((tm, d), MXU_DTYPE)], compiler_params=_params())(x1, norm2, wg, wu, wd)


def _ffn_bwd(l, x1, dx2, norm2, wg, wu, wd):
    s, d = x1.shape
    hc = wg.shape[-1]
    tm = min(512, s)
    m = MXU_DTYPE
    last = N_CHIPS - 1

    def body(x_ref, dx2_ref, g_ref, wg_ref, wu_ref, wd_ref, dx1_ref, h_ref, act_ref, dgate_ref, dup_ref, dn_ref,
             dh_scr, dxb_scr):
        i, j = pl.program_id(0), pl.program_id(1)

        @pl.when(j == 0)
        def _():
            xv = x_ref[...]
            r = lax.rsqrt(jnp.mean(xv * xv, axis=-1, keepdims=True) + EPS)
            h_ref[...] = (xv * r * g_ref[...]).astype(m)
            dxb_scr[...] = dx2_ref[...].astype(m)
            dh_scr[...] = jnp.zeros(dh_scr.shape, F32)

        @pl.when((i == 0) & (j == 0))
        def _():
            dn_ref[...] = jnp.zeros(dn_ref.shape, F32)

        h = h_ref[...]
        gate = _mm(h, wg_ref[...])
        up = _mm(h, wu_ref[...])
        sg = _sigmoid(gate)
        silu = gate * sg
        act_ref[...] = (silu * up).astype(m)
        dact = _mm_nt(dxb_scr[...], wd_ref[...])
        dup = (dact * silu).astype(m)
        dgate = (dact * up * (sg * (1.0 + gate * (1.0 - sg)))).astype(m)
        dup_ref[...] = dup
        dgate_ref[...] = dgate
        dh_scr[...] += _mm_nt(dgate, wg_ref[...]) + _mm_nt(dup, wu_ref[...])

        @pl.when(j == last)
        def _():
            xv = x_ref[...]
            r = lax.rsqrt(jnp.mean(xv * xv, axis=-1, keepdims=True) + EPS)
            xh = xv * r
            dh = dh_scr[...]
            dn_ref[...] += _colsum(dh * xh)
            dxh = dh * g_ref[...]
            dx1_ref[...] = dx2_ref[...] + r * (dxh - xh * jnp.mean(dxh * xh, axis=-1, keepdims=True))

    chunk = BS((None, tm, hc), lambda i, j: (j, i, 0))
    outs = pl.pallas_call(
        body, name=f"ffn_bwd_l{l}", grid=(s // tm, N_CHIPS),
        in_specs=[BS((tm, d), lambda i, j: (i, 0)), BS((tm, d), lambda i, j: (i, 0)),
                  BS((None, 1, d), lambda i, j: (l, 0, 0)),
                  BS((None, None, d, hc), lambda i, j: (j, l, 0, 0)), BS((None, None, d, hc), lambda i, j: (j, l, 0, 0)),
                  BS((None, None, hc, d), lambda i, j: (j, l, 0, 0))],
        out_specs=[BS((tm, d), lambda i, j: (i, 0)), BS((tm, d), lambda i, j: (i, 0)), chunk, chunk, chunk,
                   BS((1, d), lambda i, j: (0, 0))],
        out_shape=[SDS((s, d), F32), SDS((s, d), m), SDS((N_CHIPS, s, hc), m), SDS((N_CHIPS, s, hc), m),
                   SDS((N_CHIPS, s, hc), m), SDS((1, d), F32)],
        scratch_shapes=[pltpu.VMEM((tm, d), F32), pltpu.VMEM((tm, d), m)], compiler_params=_params(),
    )(x1, dx2, norm2, wg, wu, wd)
    return dict(zip(("dx1", "h2", "act", "dgate", "dup", "dnorm2"), outs))


def _loss_head(x, target, gf):
    s, d = x.shape
    tm = min(512, s)

    def body(x_ref, t_ref, g_ref, dx_ref, loss_ref, dg_ref):
        @pl.when(pl.program_id(0) == 0)
        def _():
            loss_ref[...] = jnp.zeros(loss_ref.shape, F32)
            dg_ref[...] = jnp.zeros(dg_ref.shape, F32)

        xv = x_ref[...]
        r = lax.rsqrt(jnp.mean(xv * xv, axis=-1, keepdims=True) + EPS)
        xh = xv * r
        err = xh * g_ref[...] - t_ref[...]
        loss_ref[...] += 0.5 * jnp.sum(jnp.mean(err * err, axis=-1, keepdims=True), axis=0, keepdims=True)
        dyv = err * (1.0 / d)
        dg_ref[...] += _colsum(dyv * xh)
        dxh = dyv * g_ref[...]
        dx_ref[...] = r * (dxh - xh * jnp.mean(dxh * xh, axis=-1, keepdims=True))

    return pl.pallas_call(
        body, name="loss_head", grid=(s // tm,),
        in_specs=[BS((tm, d), lambda i: (i, 0)), BS((tm, d), lambda i: (i, 0)), BS((1, d), lambda i: (0, 0))],
        out_specs=[BS((tm, d), lambda i: (i, 0)), BS((1, 1), lambda i: (0, 0)), BS((1, d), lambda i: (0, 0))],
        out_shape=[SDS((s, d), F32), SDS((1, 1), F32), SDS((1, d), F32)], compiler_params=_params())(x, target, gf)


def _in_proj_bwd(l, dres, x, norm1, w_in, du_a, dv1, dv2, du_c, dzg):
    s, d = x.shape
    nc = w_in.shape[-1]
    tm = min(256, s)
    m = MXU_DTYPE

    def body(dres_ref, x_ref, g_ref, w_ref, a_ref, b1_ref, b2_ref, c_ref, g3_ref, dx_ref, dz_ref, dn_ref):
        @pl.when(pl.program_id(0) == 0)
        def _():
            dn_ref[...] = jnp.zeros(dn_ref.shape, F32)

        dz = jnp.concatenate([a_ref[...], b1_ref[...], b2_ref[...], c_ref[...], g3_ref[...]], axis=1).astype(m)
        dz_ref[...] = dz
        dh = _mm_nt_cols(dz, w_ref)
        xv = x_ref[...]
        r = lax.rsqrt(jnp.mean(xv * xv, axis=-1, keepdims=True) + EPS)
        xh = xv * r
        dn_ref[...] += _colsum(dh * xh)
        dxh = dh * g_ref[...]
        dx_ref[...] = dres_ref[...] + r * (dxh - xh * jnp.mean(dxh * xh, axis=-1, keepdims=True))

    tile = lambda n: BS((tm, n), lambda i: (i, 0))
    return pl.pallas_call(
        body, name=f"in_proj_bwd_l{l}", grid=(s // tm,),
        in_specs=[tile(d), tile(d), BS((None, 1, d), lambda i: (l, 0, 0)),
                  BS((N_CHIPS, None, d, nc), lambda i: (0, l, 0, 0)),
                  tile(du_a.shape[1]), tile(dv1.shape[1]), tile(dv2.shape[1]), tile(du_c.shape[1]), tile(dzg.shape[1])],
        out_specs=[tile(d), tile(N_CHIPS * nc), BS((1, d), lambda i: (0, 0))],
        out_shape=[SDS((s, d), F32), SDS((s, N_CHIPS * nc), m), SDS((1, d), F32)], compiler_params=_params(),
    )(dres, x, norm1, w_in, du_a, dv1, dv2, du_c, dzg)


def _tn_matmul(name, a, a_spec, b, b_spec, out_shape, out_spec, grid):
    def body(a_ref, b_ref, o_ref):
        @pl.when(pl.program_id(1) == 0)
        def _():
            o_ref[...] = jnp.zeros(o_ref.shape, F32)

        o_ref[...] += _mm_tn(a_ref[...], b_ref[...])

    return pl.pallas_call(body, name=name, grid=grid, in_specs=[a_spec, b_spec], out_specs=out_spec,
                          out_shape=out_shape, compiler_params=_params())(a, b)


def _scan_consts(pw_ref, lanes, reverse):
    sgn = -1.0 if reverse else 1.0
    steps = [(k, pw_ref[2 * i], sgn * pw_ref[2 * i + 1]) for i, k in enumerate((1, 2, 4))]
    c = 4 if reverse else 3
    return steps, pw_ref[2 * c], sgn * pw_ref[2 * c + 1]


def _scan_block(br, bi, steps, row, reverse):
    for k, ar, ai in steps:
        if reverse:
            mask, sh = row < 8 - k, 8 - k
        else:
            mask, sh = row >= k, k
        sr = jnp.where(mask, pltpu.roll(br, sh, 0), 0.0)
        si = jnp.where(mask, pltpu.roll(bi, sh, 0), 0.0)
        br, bi = br + ar * sr - ai * si, bi + ar * si + ai * sr
    return br, bi


def _ssm_fwd(l, z, bblk_re, bblk_im, cblk_re, cblk_im, pw, dskip):
    s = z.shape[0]
    gc = bblk_re.shape[1]
    gl = bblk_re.shape[2]
    nblk = bblk_re.shape[0]

    def body(u_ref, bre, bim, cre, cim, pw_ref, d_ref, hre, him, y_ref):
        u = u_ref[...]
        hre[...] = _mm(u, bre[...])
        him[...] = _mm(u, bim[...])
        row = lax.broadcasted_iota(jnp.int32, (8, gl), 0)
        steps, car, cai = _scan_consts(pw_ref, gl, False)

        def step(i, carry):
            cr, ci = carry
            r0 = pl.multiple_of(i * 8, 8)
            br, bi = _scan_block(hre[pl.ds(r0, 8), :], him[pl.ds(r0, 8), :], steps, row, False)
            hr = br + car * cr - cai * ci
            hi = bi + car * ci + cai * cr
            hre[pl.ds(r0, 8), :] = hr
            him[pl.ds(r0, 8), :] = hi
            return jnp.broadcast_to(hr[7:8, :], (8, gl)), jnp.broadcast_to(hi[7:8, :], (8, gl))

        zero = jnp.zeros((8, gl), F32)
        lax.fori_loop(0, s // 8, step, (zero, zero))
        y_ref[...] = _mm(hre[...], cre[...]) - _mm(him[...], cim[...]) + d_ref[...] * u

    return pl.pallas_call(
        body, name=f"ssm_fwd_l{l}", grid=(nblk,),
        in_specs=[BS((s, gc), lambda k: (0, k)), BS((None, gc, gl), lambda k: (k, 0, 0)),
                  BS((None, gc, gl), lambda k: (k, 0, 0)), BS((None, gl, gc), lambda k: (k, 0, 0)),
                  BS((None, gl, gc), lambda k: (k, 0, 0)), BS((10, 8, gl), lambda k: (0, 0, k)),
                  BS((1, gc), lambda k: (0, k))],
        out_specs=[BS((s, gl), lambda k: (0, k)), BS((s, gl), lambda k: (0, k)), BS((s, gc), lambda k: (0, k))],
        out_shape=[SDS((s, nblk * gl), F32), SDS((s, nblk * gl), F32), SDS((s, nblk * gc), F32)],
        compiler_params=_params())(z, bblk_re, bblk_im, cblk_re, cblk_im, pw, dskip)


def _ssm_bwd(l, dy, z, hre, him, bblk_re, bblk_im, cblk_re, cblk_im, pw, dskip):
    s = z.shape[0]
    nblk, gc, gl = bblk_re.shape

    def body(dy_ref, u_ref, hre_ref, him_ref, bre, bim, cre, cim, pw_ref, d_ref,
             du_ref, dbre_ref, dbim_ref, dcre_ref, dcim_ref, dar_ref, dai_ref, dd_ref, gre, gim):
        dyv = dy_ref[...]
        u = u_ref[...]
        gre[...] = _mm_nt(dyv, cre[...])
        gim[...] = -_mm_nt(dyv, cim[...])
        dcre_ref[...] = _mm_tn(hre_ref[...], dyv)
        dcim_ref[...] = -_mm_tn(him_ref[...], dyv)
        dd_ref[...] = _colsum(dyv * u)
        row = lax.broadcasted_iota(jnp.int32, (8, gl), 0)
        steps, car, cai = _scan_consts(pw_ref, gl, True)
        n8 = s // 8

        def step(ii, carry):
            cr, ci, accr, acci = carry
            i = n8 - 1 - ii
            r0 = pl.multiple_of(i * 8, 8)
            br, bi = _scan_block(gre[pl.ds(r0, 8), :], gim[pl.ds(r0, 8), :], steps, row, True)
            dr = br + car * cr - cai * ci
            di = bi + car * ci + cai * cr
            gre[pl.ds(r0, 8), :] = dr
            gim[pl.ds(r0, 8), :] = di
            rp = pl.multiple_of(jnp.maximum(i - 1, 0) * 8, 8)
            keep = jnp.where(i > 0, 1.0, 0.0)
            pr = jnp.where(row >= 1, pltpu.roll(hre_ref[pl.ds(r0, 8), :], 1, 0),
                           keep * pltpu.roll(hre_ref[pl.ds(rp, 8), :], 1, 0))
            pi = jnp.where(row >= 1, pltpu.roll(him_ref[pl.ds(r0, 8), :], 1, 0),
                           keep * pltpu.roll(him_ref[pl.ds(rp, 8), :], 1, 0))
            accr = accr + dr * pr + di * pi
            acci = acci + di * pr - dr * pi
            return (jnp.broadcast_to(dr[0:1, :], (8, gl)), jnp.broadcast_to(di[0:1, :], (8, gl)), accr, acci)

        zero = jnp.zeros((8, gl), F32)
        _, _, accr, acci = lax.fori_loop(0, n8, step, (zero, zero, zero, zero))
        dar_ref[...] = _colsum(accr)
        dai_ref[...] = _colsum(acci)
        dbr = gre[...]
        dbi = gim[...]
        du_ref[...] = dyv * d_ref[...] + _mm_nt(dbr, bre[...]) + _mm_nt(dbi, bim[...])
        dbre_ref[...] = _mm_tn(u, dbr)
        dbim_ref[...] = _mm_tn(u, dbi)

    col = lambda n: BS((s, n), lambda k: (0, k))
    blk = lambda a, b: BS((None, a, b), lambda k: (k, 0, 0))
    outs = pl.pallas_call(
        body, name=f"ssm_bwd_l{l}", grid=(nblk,),
        in_specs=[col(gc), col(gc), col(gl), col(gl), blk(gc, gl), blk(gc, gl), blk(gl, gc), blk(gl, gc),
                  BS((10, 8, gl), lambda k: (0, 0, k)), BS((1, gc), lambda k: (0, k))],
        out_specs=[col(gc), blk(gc, gl), blk(gc, gl), blk(gl, gc), blk(gl, gc), BS((1, gl), lambda k: (0, k)),
                   BS((1, gl), lambda k: (0, k)), BS((1, gc), lambda k: (0, k))],
        out_shape=[SDS((s, nblk * gc), F32), SDS((nblk, gc, gl), F32), SDS((nblk, gc, gl), F32),
                   SDS((nblk, gl, gc), F32), SDS((nblk, gl, gc), F32), SDS((1, nblk * gl), F32),
                   SDS((1, nblk * gl), F32), SDS((1, nblk * gc), F32)],
        scratch_shapes=[pltpu.VMEM((s, gl), F32), pltpu.VMEM((s, gl), F32)], compiler_params=_params(),
    )(dy, z, hre, him, bblk_re, bblk_im, cblk_re, cblk_im, pw, dskip)
    return dict(zip(("du", "dbblk_re", "dbblk_im", "dcblk_re", "dcblk_im", "dabar_re", "dabar_im", "dd"), outs))


def _conv_fwd(l, z, wdw, bdw):
    s = z.shape[0]
    cw = wdw.shape[1]
    lb = 128
    tr = min(256, s)
    off1 = cw // lb
    off2 = 2 * cw // lb

    def body(v1_ref, v2_ref, w_ref, b_ref, hc_ref, scr):
        scr[0:CONV_PAD, :] = jnp.zeros((CONV_PAD, lb), F32)
        scr[CONV_PAD:, :] = v1_ref[...] * _sigmoid(v2_ref[...])
        for t in range(s // tr):
            acc = jnp.broadcast_to(b_ref[...], (tr, lb))
            for k in range(CONV_KERNEL):
                acc = acc + w_ref[pl.ds(k, 1), :] * scr[pl.ds(t * tr + CONV_PAD - (CONV_KERNEL - 1) + k, tr), :]
            hc_ref[pl.ds(t * tr, tr), :] = acc

    return pl.pallas_call(
        body, name=f"conv_fwd_l{l}", grid=(cw // lb,),
        in_specs=[BS((s, lb), lambda k: (0, off1 + k)), BS((s, lb), lambda k: (0, off2 + k)),
                  BS((CONV_KERNEL, lb), lambda k: (0, k)), BS((1, lb), lambda k: (0, k))],
        out_specs=BS((s, lb), lambda k: (0, k)), out_shape=SDS((s, cw), F32),
        scratch_shapes=[pltpu.VMEM((s + CONV_PAD, lb), F32)], compiler_params=_params())(z, z, wdw, bdw)


def _conv_bwd(l, dhc, z, wdw):
    s = z.shape[0]
    cw = wdw.shape[1]
    lb = 128
    tr = min(256, s)
    off1 = cw // lb
    off2 = 2 * cw // lb
    nb = cw // lb

    def body(d_ref, v1_ref, v2_ref, w_ref, dv1_ref, dv2_ref, dw_ref, db_ref, hpad, dpad):
        v1 = v1_ref[...]
        sg = _sigmoid(v2_ref[...])
        dv = d_ref[...]
        hpad[0:CONV_PAD, :] = jnp.zeros((CONV_PAD, lb), F32)
        hpad[CONV_PAD:, :] = v1 * sg
        dpad[0:s, :] = dv
        dpad[s:, :] = jnp.zeros((CONV_PAD, lb), F32)
        db_ref[...] = _colsum(dv)
        dws = [jnp.zeros((1, lb), F32) for _ in range(CONV_KERNEL)]
        for t in range(s // tr):
            dt = d_ref[pl.ds(t * tr, tr), :]
            acc = jnp.zeros((tr, lb), F32)
            for k in range(CONV_KERNEL):
                acc = acc + w_ref[pl.ds(k, 1), :] * dpad[pl.ds(t * tr + (CONV_KERNEL - 1) - k, tr), :]
                dws[k] = dws[k] + _colsum(dt * hpad[pl.ds(t * tr + CONV_PAD - (CONV_KERNEL - 1) + k, tr), :])
            sgt = _sigmoid(v2_ref[pl.ds(t * tr, tr), :])
            v1t = v1_ref[pl.ds(t * tr, tr), :]
            dv1_ref[pl.ds(t * tr, tr), :] = acc * sgt
            dv2_ref[pl.ds(t * tr, tr), :] = acc * v1t * (sgt * (1.0 - sgt))
        for k in range(CONV_KERNEL):
            dw_ref[pl.ds(k, 1), :] = dws[k]

    return pl.pallas_call(
        body, name=f"conv_bwd_l{l}", grid=(nb,),
        in_specs=[BS((s, lb), lambda k: (0, k)), BS((s, lb), lambda k: (0, off1 + k)),
                  BS((s, lb), lambda k: (0, off2 + k)), BS((CONV_KERNEL, lb), lambda k: (0, k))],
        out_specs=[BS((s, lb), lambda k: (0, k)), BS((s, lb), lambda k: (0, k)),
                   BS((CONV_KERNEL, lb), lambda k: (0, k)), BS((1, lb), lambda k: (0, k))],
        out_shape=[SDS((s, cw), F32), SDS((s, cw), F32), SDS((CONV_KERNEL, cw), F32), SDS((1, cw), F32)],
        scratch_shapes=[pltpu.VMEM((s + CONV_PAD, lb), F32), pltpu.VMEM((s + CONV_PAD, lb), F32)],
        compiler_params=_params())(dhc, z, z, wdw)


def _pool_window(k):
    return jnp.where(k == 0, float(POOL_WINDOWS[0]),
                     jnp.where(k == 1, float(POOL_WINDOWS[1]),
                               jnp.where(k == 2, float(POOL_WINDOWS[2]), float(POOL_WINDOWS[3]))))


def _pool_fwd(l, z, pw_width):
    s = z.shape[0]
    lb = pw_width // len(POOL_WINDOWS)
    off = 3 * pw_width // lb

    def body(u_ref, p_ref):
        k = pl.program_id(0)
        u = u_ref[...]
        row = lax.broadcasted_iota(jnp.int32, (s, lb), 0)
        sums = [u]
        for sh in (1, 2, 4, 8):
            prev = sums[-1]
            sums.append(prev + jnp.where(row >= sh, pltpu.roll(prev, sh, 0), 0.0))
        sel = jnp.where(k == 0, sums[1], jnp.where(k == 1, sums[2], jnp.where(k == 2, sums[3], sums[4])))
        cnt = jnp.minimum((row + 1).astype(F32), _pool_window(k))
        p_ref[...] = sel / cnt - u

    return pl.pallas_call(
        body, name=f"pool_fwd_l{l}", grid=(len(POOL_WINDOWS),),
        in_specs=[BS((s, lb), lambda k: (0, off + k))], out_specs=BS((s, lb), lambda k: (0, k)),
        out_shape=SDS((s, pw_width), F32), compiler_params=_params())(z)


def _pool_bwd(l, dp):
    s, width = dp.shape
    lb = width // len(POOL_WINDOWS)

    def body(d_ref, du_ref):
        k = pl.program_id(0)
        dv = d_ref[...]
        row = lax.broadcasted_iota(jnp.int32, (s, lb), 0)
        cnt = jnp.minimum((row + 1).astype(F32), _pool_window(k))
        sums = [dv / cnt]
        for sh in (1, 2, 4, 8):
            prev = sums[-1]
            sums.append(prev + jnp.where(row < s - sh, pltpu.roll(prev, s - sh, 0), 0.0))
        sel = jnp.where(k == 0, sums[1], jnp.where(k == 1, sums[2], jnp.where(k == 2, sums[3], sums[4])))
        du_ref[...] = sel - dv

    return pl.pallas_call(
        body, name=f"pool_bwd_l{l}", grid=(len(POOL_WINDOWS),),
        in_specs=[BS((s, lb), lambda k: (0, k))], out_specs=BS((s, lb), lambda k: (0, k)),
        out_shape=SDS((s, width), F32), compiler_params=_params())(dp)


def _zoh(a_re, a_im, log_dt):
    dt = jnp.exp(log_dt)
    mag = jnp.exp(dt * a_re)
    ang = dt * a_im
    abar_re = mag * jnp.cos(ang)
    abar_im = mag * jnp.sin(ang)
    den = a_re * a_re + a_im * a_im
    nr = abar_re - 1.0
    ni = abar_im
    f_re = (nr * a_re + ni * a_im) / den
    f_im = (ni * a_re - nr * a_im) / den
    return abar_re, abar_im, f_re, f_im


def _zoh_fwd(l, a_re, a_im, log_dt):
    def body(ar, ai, ld, o0, o1, o2, o3):
        for ref, val in zip((o0, o1, o2, o3), _zoh(ar[...], ai[...], ld[...])):
            ref[...] = val

    return pl.pallas_call(body, name=f"zoh_fwd_l{l}", out_shape=[SDS(a_re.shape, F32)] * 4)(a_re, a_im, log_dt)


def _zoh_bwd(l, a_re, a_im, log_dt, cts):
    def body(ar, ai, ld, c0, c1, c2, c3, dar, dai, dld):
        _, vjp = jax.vjp(_zoh, ar[...], ai[...], ld[...])
        g = vjp((c0[...], c1[...], c2[...], c3[...]))
        dar[...] = g[0]
        dai[...] = g[1]
        dld[...] = g[2]

    return pl.pallas_call(body, name=f"zoh_bwd_l{l}",
                          out_shape=[SDS(a_re.shape, F32), SDS(a_re.shape, F32), SDS(log_dt.shape, F32)],
                          )(a_re, a_im, log_dt, *cts)


def _bbar_fwd(l, f_re, f_im, b_re, b_im):
    def body(fr, fi, br, bi, o_re, o_im):
        o_re[...] = fr[...] * br[...] - fi[...] * bi[...]
        o_im[...] = fr[...] * bi[...] + fi[...] * br[...]

    return pl.pallas_call(body, name=f"bbar_fwd_l{l}", out_shape=[SDS(b_re.shape, F32)] * 2)(f_re, f_im, b_re, b_im)


def _bbar_bwd(l, f_re, f_im, b_re, b_im, d_re, d_im):
    def body(fr, fi, br, bi, dr, di, dfr, dfi, dbr, dbi):
        dfr[...] = jnp.sum(dr[...] * br[...] + di[...] * bi[...], axis=1, keepdims=True)
        dfi[...] = jnp.sum(di[...] * br[...] - dr[...] * bi[...], axis=1, keepdims=True)
        dbr[...] = fr[...] * dr[...] + fi[...] * di[...]
        dbi[...] = fr[...] * di[...] - fi[...] * dr[...]

    return pl.pallas_call(body, name=f"bbar_bwd_l{l}",
                          out_shape=[SDS(f_re.shape, F32), SDS(f_re.shape, F32), SDS(b_re.shape, F32),
                                     SDS(b_re.shape, F32)])(f_re, f_im, b_re, b_im, d_re, d_im)


def _powers(l, abar_re, abar_im):
    lanes = abar_re.shape[1]

    def body(ar_ref, ai_ref, o_ref):
        ar, ai = ar_ref[...], ai_ref[...]
        pows = [(ar, ai)]
        for _ in range(7):
            pr, pi = pows[-1]
            pows.append((pr * ar - pi * ai, pr * ai + pi * ar))
        row = lax.broadcasted_iota(jnp.int32, (8, lanes), 0)
        for i, k in enumerate((1, 2, 4)):
            o_ref[2 * i] = jnp.broadcast_to(pows[k - 1][0], (8, lanes))
            o_ref[2 * i + 1] = jnp.broadcast_to(pows[k - 1][1], (8, lanes))
        for slot, order in ((3, range(8)), (4, range(7, -1, -1))):
            vr = jnp.zeros((8, lanes), F32)
            vi = jnp.zeros((8, lanes), F32)
            for r, e in enumerate(order):
                vr = jnp.where(row == r, pows[e][0], vr)
                vi = jnp.where(row == r, pows[e][1], vi)
            o_ref[2 * slot] = vr
            o_ref[2 * slot + 1] = vi

    return pl.pallas_call(body, name=f"powers_l{l}", out_shape=SDS((10, 8, lanes), F32))(abar_re, abar_im)


def _block_diag(v, rows_first):
    g, a, b = v.shape
    eye = jnp.eye(8, dtype=v.dtype)
    out = jnp.einsum("kgab,gh->kgahb", v.reshape(g // 8, 8, a, b), eye)
    return out.reshape(g // 8, 8 * a, 8 * b)


def _block_diag_extract(blk, a, b):
    n = blk.shape[0]
    v = blk.reshape(n, 8, a, 8, b)
    return jnp.einsum("kgahb,gh->kgab", v, jnp.eye(8, dtype=blk.dtype)).reshape(n * 8, a, b)


def _ssm_prepare(l, prm):
    g, n, p = SSM_GROUPS, SSM_STATE, SSM_GROUP
    a_re, a_im = prm["ssm_a_re"][l], prm["ssm_a_im"][l]
    log_dt = prm["ssm_log_dt"][l].reshape(g, 1)
    abar_re, abar_im, f_re, f_im = _zoh_fwd(l, a_re, a_im, log_dt)
    b_re = prm["ssm_b_re"][l].reshape(g * n, p)
    b_im = prm["ssm_b_im"][l].reshape(g * n, p)
    fcol_re, fcol_im = f_re.reshape(g * n, 1), f_im.reshape(g * n, 1)
    bbar_re, bbar_im = _bbar_fwd(l, fcol_re, fcol_im, b_re, b_im)
    bblk_re = _block_diag(bbar_re.reshape(g, n, p).transpose(0, 2, 1), True).astype(MXU_DTYPE)
    bblk_im = _block_diag(bbar_im.reshape(g, n, p).transpose(0, 2, 1), True).astype(MXU_DTYPE)
    cblk_re = _block_diag(prm["ssm_c_re"][l].transpose(0, 2, 1), False).astype(MXU_DTYPE)
    cblk_im = _block_diag(prm["ssm_c_im"][l].transpose(0, 2, 1), False).astype(MXU_DTYPE)
    pw = _powers(l, abar_re.reshape(1, g * n), abar_im.reshape(1, g * n))
    return dict(a_re=a_re, a_im=a_im, log_dt=log_dt, b_re=b_re, b_im=b_im, fcol_re=fcol_re, fcol_im=fcol_im,
                bblk_re=bblk_re, bblk_im=bblk_im, cblk_re=cblk_re, cblk_im=cblk_im, pw=pw,
                dskip=prm["ssm_d"][l].reshape(1, g * p))


def _ssm_param_grads(l, sd, r):
    g, n, p = SSM_GROUPS, SSM_STATE, SSM_GROUP
    dbbar_re = _block_diag_extract(r["dbblk_re"], p, n).transpose(0, 2, 1).reshape(g * n, p)
    dbbar_im = _block_diag_extract(r["dbblk_im"], p, n).transpose(0, 2, 1).reshape(g * n, p)
    dfr, dfi, db_re, db_im = _bbar_bwd(l, sd["fcol_re"], sd["fcol_im"], sd["b_re"], sd["b_im"], dbbar_re, dbbar_im)
    cts = (r["dabar_re"].reshape(g, n), r["dabar_im"].reshape(g, n), dfr.reshape(g, n), dfi.reshape(g, n))
    da_re, da_im, dlog_dt = _zoh_bwd(l, sd["a_re"], sd["a_im"], sd["log_dt"], cts)
    dc_re = _block_diag_extract(r["dcblk_re"], n, p).transpose(0, 2, 1)
    dc_im = _block_diag_extract(r["dcblk_im"], n, p).transpose(0, 2, 1)
    return dict(ssm_a_re=da_re, ssm_a_im=da_im, ssm_log_dt=dlog_dt.reshape(g), ssm_b_re=db_re.reshape(g, n, p),
                ssm_b_im=db_im.reshape(g, n, p), ssm_c_re=dc_re, ssm_c_im=dc_im, ssm_d=r["dd"].reshape(g, p))


def _weight_grads(l, sv, mb, fb, dx1, dz, s):
    d = dx1.shape[1]
    cw = sv["y"].shape[1]
    ts = min(512, s)
    ns = s // ts
    hcn = fb["act"].shape[-1]
    ncw = dz.shape[1] // N_CHIPS
    pc = d // N_CHIPS
    gw = cw // len(POOL_WINDOWS)
    full = lambda n: BS((ts, n), lambda j, t: (t, 0))
    g = {}
    g["w_in"] = _tn_matmul(f"dw_in_l{l}", sv["h"], full(d), dz, BS((ts, ncw), lambda j, t: (t, j)),
                           SDS((N_CHIPS, d, ncw), F32), BS((None, d, ncw), lambda j, t: (j, 0, 0)), (N_CHIPS, ns))
    chunk = BS((None, ts, hcn), lambda j, t: (j, t, 0))
    for name, key in (("ffn_w_gate", "dgate"), ("ffn_w_up", "dup")):
        g[name] = _tn_matmul(f"d{name}_l{l}", fb["h2"], full(d), fb[key], chunk, SDS((N_CHIPS, d, hcn), F32),
                             BS((None, d, hcn), lambda j, t: (j, 0, 0)), (N_CHIPS, ns))
    g["ffn_w_down"] = _tn_matmul(f"dffn_w_down_l{l}", fb["act"], chunk, sv["dx2"], full(d),
                                 SDS((N_CHIPS, hcn, d), F32), BS((None, hcn, d), lambda j, t: (j, 0, 0)),
                                 (N_CHIPS, ns))
    g["w_out"] = _tn_matmul(f"dw_out_l{l}", mb["merged"], BS((ts, pc), lambda j, t: (t, j)), dx1, full(d),
                            SDS((N_CHIPS, pc, d), F32), BS((None, pc, d), lambda j, t: (j, 0, 0)), (N_CHIPS, ns))
    for name, a, b in (("ssm_w_proj", "sa", "dya"), ("conv_w_proj", "ac", "dyb"), ("pool_w_proj", "pp", "dyc")):
        g[name] = _tn_matmul(f"d{name}_l{l}", mb[a], full(cw), mb[b], BS((ts, pc), lambda j, t: (t, j)),
                             SDS((N_CHIPS, cw, pc), F32), BS((None, cw, pc), lambda j, t: (j, 0, 0)), (N_CHIPS, ns))
    gq = cw // N_CHIPS
    g["ssm_w_glu"] = _tn_matmul(f"dssm_w_glu_l{l}", mb["ge"], BS((ts, gq), lambda j, t: (t, j)), mb["dt"], full(cw),
                                SDS((N_CHIPS, gq, cw), F32), BS((None, gq, cw), lambda j, t: (j, 0, 0)), (N_CHIPS, ns))
    dwgrp = _tn_matmul(f"dpool_w_group_l{l}", sv["p"], BS((ts, gw), lambda j, t: (t, j)), mb["dq"],
                       BS((ts, gw), lambda j, t: (t, j)), SDS((len(POOL_WINDOWS), gw, gw), F32),
                       BS((None, gw, gw), lambda j, t: (j, 0, 0)), (len(POOL_WINDOWS), ns))
    return g, dwgrp


def _local_step(x, target, fw, prm):
    s, d = x.shape
    cw = prm["ssm_b_glu"].shape[1]
    sp = {k: prm[k].reshape(N_LAYERS, 1, -1) for k in ("norm1", "norm2", "b_gate", "ssm_b_glu", "conv_ln_g", "conv_ln_b",
                                                        "pool_scale", "conv_b_dw")}
    sp["pool_w_group"] = prm["pool_w_group"]
    saved = []
    xin = x
    for l in range(N_LAYERS):
        sd = _ssm_prepare(l, prm)
        z, h = _in_proj(l, xin, sp["norm1"], fw["w_in"])
        hre, him, y = _ssm_fwd(l, z, sd["bblk_re"], sd["bblk_im"], sd["cblk_re"], sd["cblk_im"], sd["pw"], sd["dskip"])
        wdw = fw["conv_w_dw"][l]
        hc = _conv_fwd(l, z, wdw, sp["conv_b_dw"][l])
        p = _pool_fwd(l, z, cw)
        x1 = _merge_fwd(l, xin, y, hc, p, z, fw, sp)
        x2 = _ffn_fwd(l, x1, sp["norm2"], fw["ffn_w_gate"], fw["ffn_w_up"], fw["ffn_w_down"])
        saved.append(dict(x=xin, z=z, h=h, hre=hre, him=him, y=y, hc=hc, p=p, x1=x1, sd=sd, wdw=wdw))
        xin = x2
    dx, loss, dfinal = _loss_head(xin, target, prm["final_norm"].reshape(1, d))
    big = [None] * N_LAYERS
    small = [None] * N_LAYERS
    for l in reversed(range(N_LAYERS)):
        sv = saved[l]
        sd = sv["sd"]
        sv["dx2"] = dx
        fb = _ffn_bwd(l, sv["x1"], dx, sp["norm2"], fw["ffn_w_gate"], fw["ffn_w_up"], fw["ffn_w_down"])
        mb = _merge_bwd(l, fb["dx1"], sv["y"], sv["hc"], sv["p"], sv["z"], fw, sp)
        du_c = _pool_bwd(l, mb["dp"])
        dv1, dv2, dwdw, dbdw = _conv_bwd(l, mb["dhc"], sv["z"], sv["wdw"])
        sr = _ssm_bwd(l, mb["dy"], sv["z"], sv["hre"], sv["him"], sd["bblk_re"], sd["bblk_im"], sd["cblk_re"],
                      sd["cblk_im"], sd["pw"], sd["dskip"])
        dx, dz, dnorm1 = _in_proj_bwd(l, fb["dx1"], sv["x"], sp["norm1"], fw["w_in"], sr["du"], dv1, dv2, du_c, mb["dzg"])
        big[l], dwgrp = _weight_grads(l, sv, mb, fb, fb["dx1"], dz, s)
        sg = _ssm_param_grads(l, sd, sr)
        sg.update(norm1=dnorm1.reshape(d), b_gate=mb["db_gate"].reshape(3 * d), ssm_b_glu=mb["db_glu"].reshape(cw),
                  conv_b_dw=dbdw.reshape(cw), conv_ln_g=mb["dln_g"].reshape(cw), conv_ln_b=mb["dln_b"].reshape(cw),
                  pool_w_group=dwgrp, pool_scale=mb["dscale"].reshape(cw), norm2=fb["dnorm2"].reshape(d),
                  conv_w_dw=dwdw)
        small[l] = sg
    return loss[0, 0], dx, big, small, dfinal.reshape(d)


def _place():
    return lax.axis_index("x"), lax.axis_index("y"), lax.axis_index("c")


def _other_chips(x, y):
    return [(1 - x, y), (x, 1 - y), (1 - x, 1 - y)]


def _remote(src, dst, send_sem, recv_sem, device):
    return pltpu.make_async_remote_copy(src_ref=src, dst_ref=dst, send_sem=send_sem, recv_sem=recv_sem,
                                        device_id=device, device_id_type=MESH)


def _cast_shard(name, w):
    nl, k, n = w.shape
    tr = _row_tile(k, n)

    def body(w_ref, o_ref):
        o_ref[...] = w_ref[...].astype(o_ref.dtype)

    return pl.pallas_call(body, name=f"cast_{name}", grid=(nl, k // tr),
                          in_specs=[BS((None, tr, n), lambda l, r: (l, r, 0))],
                          out_specs=BS((None, tr, n), lambda l, r: (l, r, 0)),
                          out_shape=SDS(w.shape, MXU_DTYPE))(w)


def _allgather_weights(shards):
    n = len(shards)

    def body(*refs):
        ins, outs = refs[:n], refs[n:2 * n]
        send_sems, recv_sems, fsend_sems, frecv_sems, local_sems = refs[2 * n:]
        x, y, c = _place()
        jme = 2 * x + y
        sibling = (x, y, 1 - c)
        chips = _other_chips(x, y)
        local = []
        for a in range(n):
            cp = pltpu.make_async_copy(ins[a], outs[a].at[jme], local_sems.at[a])
            cp.start()
            local.append(cp)
        sends = []
        for a in range(n):
            for k, (cx, cy) in enumerate(chips):
                cp = _remote(ins[a].at[c], outs[a].at[jme, c], send_sems.at[a, k], recv_sems.at[a, k], (cx, cy, c))
                cp.start()
                sends.append(cp)
        passed = []
        for k, (cx, cy) in enumerate(chips):
            jk = 2 * cx + cy
            for a in range(n):
                blk = outs[a].at[jk, c]
                _remote(blk, blk, send_sems.at[a, k], recv_sems.at[a, k], (cx, cy, c)).wait_recv()
                cp = _remote(blk, blk, fsend_sems.at[a, k], frecv_sems.at[a, k], sibling)
                cp.start()
                passed.append(cp)
        for k, (cx, cy) in enumerate(chips):
            jk = 2 * cx + cy
            for a in range(n):
                blk = outs[a].at[jk, 1 - c]
                _remote(blk, blk, fsend_sems.at[a, k], frecv_sems.at[a, k], sibling).wait_recv()
        for cp in sends + passed:
            cp.wait_send()
        for cp in local:
            cp.wait()

    sem = pltpu.SemaphoreType.DMA((n, 3))
    return pl.pallas_call(
        body, name="allgather_weights", in_specs=[ANY] * n, out_specs=[ANY] * n,
        out_shape=[SDS((N_CHIPS,) + w.shape, w.dtype) for w in shards],
        scratch_shapes=[sem, sem, sem, sem, pltpu.SemaphoreType.DMA((n,))])(*shards)


def _rs_sibling_halves(l, grads):
    n = len(grads)

    def body(*refs):
        ins, outs = refs[:n], refs[n:2 * n]
        send_sems, recv_sems = refs[2 * n:]
        x, y, c = _place()
        copies = []
        for a in range(n):
            rh = ins[a].shape[1] // 2
            src = ins[a].at[:, pl.ds(pl.multiple_of((1 - c) * rh, 8), rh), :]
            cp = _remote(src, outs[a], send_sems.at[a], recv_sems.at[a], (x, y, 1 - c))
            cp.start()
            copies.append(cp)
        for cp in copies:
            cp.wait()

    sem = pltpu.SemaphoreType.DMA((n,))
    return pl.pallas_call(
        body, name=f"rs_sibling_halves_l{l}", in_specs=[ANY] * n, out_specs=[ANY] * n,
        out_shape=[SDS((g.shape[0], g.shape[1] // 2, g.shape[2]), g.dtype) for g in grads],
        scratch_shapes=[sem, sem])(*grads)


def _rs_to_owner(l, parts):
    n = len(parts)

    def body(*refs):
        ins, outs = refs[:n], refs[n:2 * n]
        send_sems, recv_sems = refs[2 * n:]
        x, y, c = _place()
        copies = []
        for a in range(n):
            for k, (cx, cy) in enumerate(_other_chips(x, y)):
                cp = _remote(ins[a].at[2 * cx + cy], outs[a].at[k], send_sems.at[a, k], recv_sems.at[a, k], (cx, cy, c))
                cp.start()
                copies.append(cp)
        for cp in copies:
            cp.wait()

    sem = pltpu.SemaphoreType.DMA((n, 3))
    return pl.pallas_call(
        body, name=f"rs_to_owner_l{l}", in_specs=[ANY] * n, out_specs=[ANY] * n,
        out_shape=[SDS((3,) + p.shape[1:], p.dtype) for p in parts], scratch_shapes=[sem, sem])(*parts)


def _rs_sibling_exchange(l, halves):
    n = len(halves)

    def body(*refs):
        ins, outs = refs[:n], refs[n:2 * n]
        send_sems, recv_sems, local_sems = refs[2 * n:]
        x, y, c = _place()
        copies, local = [], []
        for a in range(n):
            lc = pltpu.make_async_copy(ins[a], outs[a].at[c], local_sems.at[a])
            lc.start()
            local.append(lc)
            cp = _remote(ins[a], outs[a].at[c], send_sems.at[a], recv_sems.at[a], (x, y, 1 - c))
            cp.start()
            copies.append(cp)
        for a, cp in enumerate(copies):
            cp.wait_send()
            _remote(ins[a], outs[a].at[1 - c], send_sems.at[a], recv_sems.at[a], (x, y, 1 - c)).wait_recv()
        for lc in local:
            lc.wait()

    sem = pltpu.SemaphoreType.DMA((n,))
    return pl.pallas_call(
        body, name=f"rs_sibling_exchange_l{l}", in_specs=[ANY] * n, out_specs=[ANY] * n,
        out_shape=[SDS((2,) + h.shape, h.dtype) for h in halves], scratch_shapes=[sem, sem, sem])(*halves)


def _add_pair(name, g, rb, c_arr):
    nj, r, cols = g.shape
    rh = r // 2
    tr = _row_tile(rh, cols)
    nt = rh // tr

    def body(c_ref, g_ref, rb_ref, o_ref):
        o_ref[...] = g_ref[...] + rb_ref[...]

    return pl.pallas_call(
        body, name=name,
        grid_spec=pltpu.PrefetchScalarGridSpec(
            num_scalar_prefetch=1, grid=(nj, nt),
            in_specs=[BS((None, tr, cols), lambda j, t, c: (j, c[0] * nt + t, 0)),
                      BS((None, tr, cols), lambda j, t, c: (j, t, 0))],
            out_specs=BS((None, tr, cols), lambda j, t, c: (j, t, 0))),
        out_shape=SDS((nj, rh, cols), F32))(c_arr, g, rb)


def _add_owner(name, part, recv, j_arr):
    _, rh, cols = part.shape
    tr = _row_tile(rh, cols, budget=1024 * 1024)
    nt = rh // tr

    def body(j_ref, p_ref, r_ref, o_ref):
        o_ref[...] = ((p_ref[...] + r_ref[0]) + r_ref[1]) + r_ref[2]

    return pl.pallas_call(
        body, name=name,
        grid_spec=pltpu.PrefetchScalarGridSpec(
            num_scalar_prefetch=1, grid=(nt,),
            in_specs=[BS((None, tr, cols), lambda t, j: (j[0], t, 0)), BS((3, tr, cols), lambda t, j: (0, t, 0))],
            out_specs=BS((tr, cols), lambda t, j: (t, 0))),
        out_shape=SDS((rh, cols), F32))(j_arr, part, recv)


def _reduce_scatter_layer(l, grads, c_arr, j_arr):
    names = list(grads)
    theirs = _rs_sibling_halves(l, [grads[n] for n in names])
    pair = [_add_pair(f"rs_add_pair_{n}_l{l}", grads[n], rb, c_arr) for n, rb in zip(names, theirs)]
    recv = _rs_to_owner(l, pair)
    mine = [_add_owner(f"rs_add_owner_{n}_l{l}", p, r, j_arr) for n, p, r in zip(names, pair, recv)]
    both = _rs_sibling_exchange(l, mine)
    return {n: b.reshape(b.shape[0] * b.shape[1], b.shape[2]) for n, b in zip(names, both)}


def _allgather_rows(buf):
    def body(x_ref, out_ref, send_sems, recv_sems, local_sem):
        x, y, c = _place()
        me, sibling = (x, y, c), (x, y, 1 - c)
        chips = _other_chips(x, y)

        def slot(px, py, pc):
            return out_ref.at[4 * px + 2 * py + pc]

        mine = pltpu.make_async_copy(x_ref, slot(*me), local_sem)
        mine.start()
        first = [_remote(x_ref, slot(*me), send_sems.at[0], recv_sems.at[0], sibling)]
        first += [_remote(x_ref, slot(*me), send_sems.at[1 + k], recv_sems.at[1 + k], (cx, cy, c))
                  for k, (cx, cy) in enumerate(chips)]
        for cp in first:
            cp.start()
        passed = []
        for k, (cx, cy) in enumerate(chips):
            blk = slot(cx, cy, c)
            _remote(blk, blk, send_sems.at[1 + k], recv_sems.at[1 + k], (cx, cy, c)).wait_recv()
            cp = _remote(blk, blk, send_sems.at[4 + k], recv_sems.at[4 + k], sibling)
            cp.start()
            passed.append(cp)
        sib = slot(x, y, 1 - c)
        _remote(sib, sib, send_sems.at[0], recv_sems.at[0], sibling).wait_recv()
        for k, (cx, cy) in enumerate(chips):
            blk = slot(cx, cy, 1 - c)
            _remote(blk, blk, send_sems.at[4 + k], recv_sems.at[4 + k], sibling).wait_recv()
        for cp in first + passed:
            cp.wait_send()
        mine.wait()

    return pl.pallas_call(
        body, name="allgather_small_grads", in_specs=[ANY], out_specs=ANY,
        out_shape=SDS((8,) + buf.shape, buf.dtype),
        scratch_shapes=[pltpu.SemaphoreType.DMA((7,)), pltpu.SemaphoreType.DMA((7,)), pltpu.SemaphoreType.DMA(())],
    )(buf)


def _sum_devices(gathered):
    _, r, cols = gathered.shape
    tr = _row_tile(r, cols, budget=256 * 1024)

    def body(g_ref, o_ref):
        acc = g_ref[0]
        for k in range(1, 8):
            acc = acc + g_ref[k]
        o_ref[...] = acc

    return pl.pallas_call(body, name="sum_small_grads", grid=(r // tr,),
                          in_specs=[BS((8, tr, cols), lambda t: (0, t, 0))], out_specs=BS((tr, cols), lambda t: (t, 0)),
                          out_shape=SDS((r, cols), F32))(gathered)


def _adamw_values(w, g, m, v):
    m = ADAM_B1 * m + (1.0 - ADAM_B1) * g
    v = ADAM_B2 * v + (1.0 - ADAM_B2) * (g * g)
    m_hat = m / (1.0 - ADAM_B1 ** ADAM_STEP)
    v_hat = v / (1.0 - ADAM_B2 ** ADAM_STEP)
    delta = -ADAM_LR * (m_hat / (jnp.sqrt(v_hat) + ADAM_EPS) + ADAM_WD * w)
    return delta, m, v


def _adamw_big(name, w, m, v, g0, g1):
    nl, r, cols = w.shape
    tr = _row_tile(r, cols, budget=1024 * 1024)
    nt = r // tr

    def body(w_ref, m_ref, v_ref, g0_ref, g1_ref, go_ref, d_ref, mo_ref, vo_ref):
        g = jnp.where(pl.program_id(0) == 0, g0_ref[...], g1_ref[...])
        delta, m_new, v_new = _adamw_values(w_ref[...], g, m_ref[...], v_ref[...])
        go_ref[...] = g
        d_ref[...] = delta
        mo_ref[...] = m_new
        vo_ref[...] = v_new

    stacked = BS((None, tr, cols), lambda l, t: (l, t, 0))
    return pl.pallas_call(
        body, name=f"adamw_{name}", grid=(nl, nt),
        in_specs=[stacked, stacked, stacked,
                  BS((tr, cols), lambda l, t: (t * (1 - l) + (nt - 1) * l, 0)),
                  BS((tr, cols), lambda l, t: (t * l, 0))],
        out_specs=[stacked] * 4, out_shape=[SDS(w.shape, F32)] * 4, compiler_params=_params())(w, m, v, g0, g1)


def _adamw_rows(w, m, v, g):
    r, cols = w.shape
    tr = _row_tile(r, cols, budget=512 * 1024)

    def body(w_ref, m_ref, v_ref, g_ref, d_ref, mo_ref, vo_ref):
        delta, m_new, v_new = _adamw_values(w_ref[...], g_ref[...], m_ref[...], v_ref[...])
        d_ref[...] = delta
        mo_ref[...] = m_new
        vo_ref[...] = v_new

    spec = BS((tr, cols), lambda t: (t, 0))
    return pl.pallas_call(body, name="adamw_small", grid=(r // tr,), in_specs=[spec] * 4, out_specs=[spec] * 3,
                          out_shape=[SDS(w.shape, F32)] * 3)(w, m, v, g)


PACK_ALIGN = 8 * 128


def _pack_rows(arrays):
    parts = []
    for a in arrays:
        flat = a.reshape(-1)
        pad = (-flat.shape[0]) % PACK_ALIGN
        if pad:
            flat = jnp.pad(flat, (0, pad))
        parts.append(flat.reshape(-1, 128))
    return jnp.concatenate(parts, axis=0)


def _unpack_rows(buf, shapes):
    out, row = [], 0
    for shape in shapes:
        size = math.prod(shape)
        rows = -(-size // PACK_ALIGN) * (PACK_ALIGN // 128)
        out.append(buf[row:row + rows].reshape(-1)[:size].reshape(shape))
        row += rows
    return out


def kernel(x, norm1, w_in, b_gate, ssm_a_re, ssm_a_im, ssm_log_dt, ssm_b_re, ssm_b_im, ssm_c_re, ssm_c_im, ssm_d, ssm_w_glu, ssm_b_glu, ssm_w_proj, conv_w_dw, conv_b_dw, conv_ln_g, conv_ln_b, conv_w_proj, pool_w_group, pool_scale, pool_w_proj, w_out, norm2, ffn_w_gate, ffn_w_up, ffn_w_down, final_norm, loss_target, m_norm1, m_w_in, m_b_gate, m_ssm_a_re, m_ssm_a_im, m_ssm_log_dt, m_ssm_b_re, m_ssm_b_im, m_ssm_c_re, m_ssm_c_im, m_ssm_d, m_ssm_w_glu, m_ssm_b_glu, m_ssm_w_proj, m_conv_w_dw, m_conv_b_dw, m_conv_ln_g, m_conv_ln_b, m_conv_w_proj, m_pool_w_group, m_pool_scale, m_pool_w_proj, m_w_out, m_norm2, m_ffn_w_gate, m_ffn_w_up, m_ffn_w_down, m_final_norm, v_norm1, v_w_in, v_b_gate, v_ssm_a_re, v_ssm_a_im, v_ssm_log_dt, v_ssm_b_re, v_ssm_b_im, v_ssm_c_re, v_ssm_c_im, v_ssm_d, v_ssm_w_glu, v_ssm_b_glu, v_ssm_w_proj, v_conv_w_dw, v_conv_b_dw, v_conv_ln_g, v_conv_ln_b, v_conv_w_proj, v_pool_w_group, v_pool_scale, v_pool_w_proj, v_w_out, v_norm2, v_ffn_w_gate, v_ffn_w_up, v_ffn_w_down, v_final_norm):
    given = dict(locals())
    prm = {n: given[n] for n in WEIGHTS}
    mom = {n: given["m_" + n] for n in WEIGHTS}
    var = {n: given["v_" + n] for n in WEIGHTS}
    cx, cy, cc = _place()
    c_arr = jnp.reshape(cc, (1,)).astype(jnp.int32)
    j_arr = jnp.reshape(2 * cx + cy, (1,)).astype(jnp.int32)

    dw_shard = prm["conv_w_dw"].reshape(N_LAYERS, CONV_KERNEL, -1)
    gathered = _allgather_weights([_cast_shard(n, prm[n]) for n in BIG] + [dw_shard])
    fw = dict(zip(BIG, gathered[:-1]))
    fw["conv_w_dw"] = gathered[-1].transpose(1, 2, 0, 3).reshape(N_LAYERS, CONV_KERNEL, -1)

    loss, dx, big, small, dfinal = _local_step(x[0], loss_target[0], fw, prm)
    loss = lax.psum(loss, ("x", "y", "c"))

    reduced = [_reduce_scatter_layer(l, big[l], c_arr, j_arr) for l in range(N_LAYERS)]
    packed_names = [n for n in SMALL if n != "final_norm"] + ["conv_w_dw"]
    local_small = [jnp.stack([small[l][n] for l in range(N_LAYERS)]) for n in packed_names] + [dfinal]
    gsum = _sum_devices(_allgather_rows(_pack_rows(local_small)))
    small_shapes = [a.shape for a in local_small]
    gsmall = dict(zip(packed_names + ["final_norm"], _unpack_rows(gsum, small_shapes)))
    lanes = dw_shard.shape[-1]
    gsmall["conv_w_dw"] = lax.dynamic_slice_in_dim(gsmall["conv_w_dw"], (2 * cx + cy) * lanes, lanes, axis=2)

    out = {}
    for n in BIG:
        g, d, m_new, v_new = _adamw_big(n, prm[n], mom[n], var[n], reduced[0][n], reduced[1][n])
        out[n] = (g, d, m_new, v_new)
    small_names = list(SMALL) + ["conv_w_dw"]
    w_rows = _pack_rows([prm[n] for n in small_names])
    m_rows = _pack_rows([mom[n] for n in small_names])
    v_rows = _pack_rows([var[n] for n in small_names])
    g_rows = _pack_rows([gsmall[n] for n in small_names])
    shapes = [prm[n].shape for n in small_names]
    d_s, m_s, v_s = (_unpack_rows(r, shapes) for r in _adamw_rows(w_rows, m_rows, v_rows, g_rows))
    for i, n in enumerate(small_names):
        out[n] = (gsmall[n].reshape(prm[n].shape), d_s[i], m_s[i], v_s[i])
    grads = [out[n][0] for n in WEIGHTS]
    deltas = [out[n][1] for n in WEIGHTS]
    new_m = [out[n][2] for n in WEIGHTS]
    new_v = [out[n][3] for n in WEIGHTS]
    return (loss, dx[None], *grads, *deltas, *new_m, *new_v)
```

```python
import functools
import math

import jax
import jax.numpy as jnp
from jax import lax
from jax.experimental import pallas as pl
from jax.experimental.pallas import tpu as pltpu

F32 = jnp.float32
MXU_DTYPE = jnp.bfloat16
SDS = jax.ShapeDtypeStruct
BS = pl.BlockSpec
ANY = pl.BlockSpec(memory_space=pl.ANY)
HBM = pl.BlockSpec(memory_space=pltpu.HBM)
SEM = pl.BlockSpec(memory_space=pltpu.SEMAPHORE)
SIDE_EFFECT = pltpu.SideEffectType.DATAFLOW_SIDE_EFFECTING
MESH = pl.DeviceIdType.MESH

EPS = 1e-6
N_CHIPS = 4
N_LAYERS = 2
SSM_GROUPS, SSM_STATE, SSM_GROUP = 32, 64, 16
CONV_KERNEL = 31
CONV_PAD = 32
POOL_WINDOWS = (2, 4, 8, 16)
GELU_C = math.sqrt(2.0 / math.pi)
ADAM_LR, ADAM_B1, ADAM_B2, ADAM_EPS, ADAM_WD, ADAM_STEP = 0.001, 0.9, 0.999, 1e-08, 0.01, 10
VMEM_LIMIT = 56 * 1024 * 1024

BIG = ("w_in", "ssm_w_glu", "ssm_w_proj", "conv_w_proj", "pool_w_proj", "w_out", "ffn_w_gate", "ffn_w_up", "ffn_w_down")
SMALL = ("norm1", "b_gate", "ssm_a_re", "ssm_a_im", "ssm_log_dt", "ssm_b_re", "ssm_b_im", "ssm_c_re", "ssm_c_im",
         "ssm_d", "ssm_b_glu", "conv_b_dw", "conv_ln_g", "conv_ln_b", "pool_w_group", "pool_scale", "norm2",
         "final_norm")
WEIGHTS = ("norm1", "w_in", "b_gate", "ssm_a_re", "ssm_a_im", "ssm_log_dt", "ssm_b_re", "ssm_b_im", "ssm_c_re",
           "ssm_c_im", "ssm_d", "ssm_w_glu", "ssm_b_glu", "ssm_w_proj", "conv_w_dw", "conv_b_dw", "conv_ln_g",
           "conv_ln_b", "conv_w_proj", "pool_w_group", "pool_scale", "pool_w_proj", "w_out", "norm2", "ffn_w_gate",
           "ffn_w_up", "ffn_w_down", "final_norm")


def _params(vmem=True):
    return pltpu.CompilerParams(vmem_limit_bytes=VMEM_LIMIT) if vmem else None


def _mm(a, b):
    return jnp.dot(a.astype(MXU_DTYPE), b.astype(MXU_DTYPE), preferred_element_type=F32)


def _mm_nt(a, b):
    return lax.dot_general(a.astype(MXU_DTYPE), b.astype(MXU_DTYPE), (((1,), (1,)), ((), ())),
                           preferred_element_type=F32)


def _mm_tn(a, b):
    return lax.dot_general(a.astype(MXU_DTYPE), b.astype(MXU_DTYPE), (((0,), (0,)), ((), ())),
                           preferred_element_type=F32)


def _sigmoid(x):
    return jax.nn.sigmoid(x)


def _gelu(x):
    t = jnp.tanh(GELU_C * (x + 0.044715 * (x * x * x)))
    return x * (0.5 * (1.0 + t)), t


def _gelu_grad(x, t):
    return 0.5 * (1.0 + t) + 0.5 * x * (1.0 - t * t) * (GELU_C * (1.0 + 3.0 * 0.044715 * x * x))


def _colsum(v):
    return jnp.sum(v, axis=0, keepdims=True)


def _row_tile(rows, cols, itemsize=4, budget=1536 * 1024):
    best = None
    for t in range(8, rows + 1, 8):
        if rows % t == 0 and t * cols * itemsize <= budget:
            best = t
    return best if best is not None else rows


def _in_proj(l, x, norm1, w_in):
    s, d = x.shape
    nc = w_in.shape[-1]
    tm = min(512, s)

    def body(x_ref, g_ref, w_ref, z_ref, h_ref):
        @pl.when(pl.program_id(1) == 0)
        def _():
            xv = x_ref[...]
            r = lax.rsqrt(jnp.mean(xv * xv, axis=-1, keepdims=True) + EPS)
            h_ref[...] = (xv * r * g_ref[...]).astype(h_ref.dtype)

        z_ref[...] = _mm(h_ref[...], w_ref[...])

    return pl.pallas_call(
        body, name=f"in_proj_l{l}", grid=(s // tm, N_CHIPS),
        in_specs=[BS((tm, d), lambda i, j: (i, 0)), BS((None, 1, d), lambda i, j: (l, 0, 0)),
                  BS((None, d, nc), lambda i, j: (j, 0, 0))],
        out_specs=[BS((tm, nc), lambda i, j: (i, j)), BS((tm, d), lambda i, j: (i, 0))],
        out_shape=[SDS((s, N_CHIPS * nc), F32), SDS((s, d), MXU_DTYPE)],
        compiler_params=_params())(x, norm1, w_in)


def _mm_cols(a, w_ref):
    return jnp.concatenate([_mm(a, w_ref[j]) for j in range(N_CHIPS)], axis=1)


def _mm_nt_cols(dv, w_ref):
    nc = w_ref.shape[-1]
    acc = _mm_nt(dv[:, 0:nc], w_ref[0])
    for j in range(1, N_CHIPS):
        acc = acc + _mm_nt(dv[:, j * nc:(j + 1) * nc], w_ref[j])
    return acc


def _merge_values(y, hc, p, zg, wglu, bglu, wpa, wpb, wpc, lng, lnb, wgrp, scale, bg):
    v = {}
    ge, th = _gelu(y)
    t = _mm(ge, wglu) + bglu
    sg = _sigmoid(t)
    sa = ge * sg
    ya = _mm_cols(sa, wpa)
    mu = jnp.mean(hc, axis=-1, keepdims=True)
    xc = hc - mu
    r = lax.rsqrt(jnp.mean(xc * xc, axis=-1, keepdims=True) + EPS)
    xh = xc * r
    ln = xh * lng + lnb
    sl = _sigmoid(ln)
    ac = ln * sl
    yb = _mm_cols(ac, wpb)
    gw = p.shape[1] // len(POOL_WINDOWS)
    q = jnp.concatenate([_mm(p[:, k * gw:(k + 1) * gw], wgrp[k]) for k in range(len(POOL_WINDOWS))], axis=1)
    pp = q * scale
    yc = _mm_cols(pp, wpc)
    d = ya.shape[1]
    gates = [_sigmoid(zg[k] + bg[:, k * d:(k + 1) * d]) for k in range(3)]
    merged = gates[0] * ya + gates[1] * yb + gates[2] * yc
    v.update(ge=ge, th=th, sg=sg, sa=sa, ya=ya, r=r, xh=xh, ln=ln, sl=sl, ac=ac, yb=yb, q=q, pp=pp, yc=yc,
             gates=gates, merged=merged)
    return v


def _merge_specs(l, tm, d, cw):
    row = lambda n: BS((None, 1, n), lambda i: (l, 0, 0))
    return [
        BS((tm, cw), lambda i: (i, 0)),
        BS((tm, cw), lambda i: (i, 0)),
        BS((tm, cw), lambda i: (i, 0)),
        BS((tm, d), lambda i: (i, 2)), BS((tm, d), lambda i: (i, 3)), BS((tm, d), lambda i: (i, 4)),
        BS((N_CHIPS, cw // N_CHIPS, cw), lambda i: (0, 0, 0)),
        row(cw),
        BS((N_CHIPS, cw, d // N_CHIPS), lambda i: (0, 0, 0)),
        BS((N_CHIPS, cw, d // N_CHIPS), lambda i: (0, 0, 0)),
        BS((N_CHIPS, cw, d // N_CHIPS), lambda i: (0, 0, 0)),
        row(cw), row(cw),
        BS((None, 4, cw // 4, cw // 4), lambda i: (l, 0, 0, 0)),
        row(cw),
        row(3 * d),
        BS((N_CHIPS, d // N_CHIPS, d), lambda i: (0, 0, 0)),
    ]


def _merge_fwd(l, x, y, hc, p, z, fw, sp):
    s, d = x.shape
    cw = y.shape[1]
    tm = min(256, s)

    def body(x_ref, y_ref, hc_ref, p_ref, z0, z1, z2, wglu, bglu, wpa, wpb, wpc, lng, lnb, wgrp, scale, bg, wout,
             x1_ref):
        v = _merge_values(y_ref[...], hc_ref[...], p_ref[...], (z0[...], z1[...], z2[...]),
                          wglu[...].reshape(cw, cw), bglu[...], wpa, wpb, wpc, lng[...], lnb[...], wgrp, scale[...],
                          bg[...])
        x1_ref[...] = x_ref[...] + _mm(v["merged"], wout[...].reshape(d, d))

    return pl.pallas_call(
        body, name=f"merge_fwd_l{l}", grid=(s // tm,),
        in_specs=[BS((tm, d), lambda i: (i, 0))] + _merge_specs(l, tm, d, cw),
        out_specs=BS((tm, d), lambda i: (i, 0)), out_shape=SDS((s, d), F32), compiler_params=_params(),
    )(x, y, hc, p, z, z, z, fw["ssm_w_glu"], sp["ssm_b_glu"], fw["ssm_w_proj"], fw["conv_w_proj"], fw["pool_w_proj"],
      sp["conv_ln_g"], sp["conv_ln_b"], sp["pool_w_group"], sp["pool_scale"], sp["b_gate"], fw["w_out"])


def _merge_bwd(l, dx1, y, hc, p, z, fw, sp):
    s, d = dx1.shape
    cw = y.shape[1]
    tm = min(256, s)
    m = MXU_DTYPE

    def body(dx1_ref, y_ref, hc_ref, p_ref, z0, z1, z2, wglu, bglu, wpa, wpb, wpc, lng, lnb, wgrp, scale, bg, wout,
             dzg_ref, dy_ref, dhc_ref, dp_ref, merged_ref, sa_ref, ac_ref, pp_ref, ge_ref, dt_ref, dya_ref, dyb_ref,
             dyc_ref, dq_ref, dbg_ref, dbglu_ref, dlng_ref, dlnb_ref, dscale_ref):
        yv = y_ref[...]
        wg = wglu[...].reshape(cw, cw)
        v = _merge_values(yv, hc_ref[...], p_ref[...], (z0[...], z1[...], z2[...]), wg, bglu[...], wpa, wpb, wpc,
                          lng[...], lnb[...], wgrp, scale[...], bg[...])
        dm = _mm_nt(dx1_ref[...], wout[...].reshape(d, d))
        ys = (v["ya"], v["yb"], v["yc"])
        dys = []
        for k in range(3):
            gk = v["gates"][k]
            dzg_ref[:, k * d:(k + 1) * d] = dm * ys[k] * (gk * (1.0 - gk))
            dys.append((dm * gk).astype(m))
        dsa = _mm_nt_cols(dys[0], wpa)
        dac = _mm_nt_cols(dys[1], wpb)
        dpp = _mm_nt_cols(dys[2], wpc)
        ge, sg = v["ge"], v["sg"]
        dt = dsa * ge * (sg * (1.0 - sg))
        dge = dsa * sg + _mm_nt(dt, wg)
        dy_ref[...] = dge * _gelu_grad(yv, v["th"])
        ln, sl, xh = v["ln"], v["sl"], v["xh"]
        dln = dac * (sl * (1.0 + ln * (1.0 - sl)))
        dxh = dln * lng[...]
        dhc_ref[...] = v["r"] * (dxh - jnp.mean(dxh, axis=-1, keepdims=True)
                                 - xh * jnp.mean(dxh * xh, axis=-1, keepdims=True))
        dq = dpp * scale[...]
        gw = cw // len(POOL_WINDOWS)
        for k in range(len(POOL_WINDOWS)):
            dp_ref[:, k * gw:(k + 1) * gw] = _mm_nt(dq[:, k * gw:(k + 1) * gw], wgrp[k])
        merged_ref[...] = v["merged"].astype(m)
        sa_ref[...] = v["sa"].astype(m)
        ac_ref[...] = v["ac"].astype(m)
        pp_ref[...] = v["pp"].astype(m)
        ge_ref[...] = ge.astype(m)
        dt_ref[...] = dt.astype(m)
        dya_ref[...] = dys[0]
        dyb_ref[...] = dys[1]
        dyc_ref[...] = dys[2]
        dq_ref[...] = dq.astype(m)

        @pl.when(pl.program_id(0) == 0)
        def _():
            for ref in (dbg_ref, dbglu_ref, dlng_ref, dlnb_ref, dscale_ref):
                ref[...] = jnp.zeros(ref.shape, F32)

        dbg_ref[...] += _colsum(dzg_ref[...])
        dbglu_ref[...] += _colsum(dt)
        dlng_ref[...] += _colsum(dln * xh)
        dlnb_ref[...] += _colsum(dln)
        dscale_ref[...] += _colsum(dpp * v["q"])

    tile = lambda n: BS((tm, n), lambda i: (i, 0))
    acc = lambda n: BS((1, n), lambda i: (0, 0))
    outs = pl.pallas_call(
        body, name=f"merge_bwd_l{l}", grid=(s // tm,),
        in_specs=[tile(d)] + _merge_specs(l, tm, d, cw),
        out_specs=[tile(3 * d), tile(cw), tile(cw), tile(cw), tile(d), tile(cw), tile(cw), tile(cw), tile(cw), tile(cw),
                   tile(d), tile(d), tile(d), tile(cw), acc(3 * d), acc(cw), acc(cw), acc(cw), acc(cw)],
        out_shape=[SDS((s, 3 * d), F32), SDS((s, cw), F32), SDS((s, cw), F32), SDS((s, cw), F32), SDS((s, d), m),
                   SDS((s, cw), m), SDS((s, cw), m), SDS((s, cw), m), SDS((s, cw), m), SDS((s, cw), m), SDS((s, d), m),
                   SDS((s, d), m), SDS((s, d), m), SDS((s, cw), m), SDS((1, 3 * d), F32), SDS((1, cw), F32),
                   SDS((1, cw), F32), SDS((1, cw), F32), SDS((1, cw), F32)],
        compiler_params=_params(),
    )(dx1, y, hc, p, z, z, z, fw["ssm_w_glu"], sp["ssm_b_glu"], fw["ssm_w_proj"], fw["conv_w_proj"], fw["pool_w_proj"],
      sp["conv_ln_g"], sp["conv_ln_b"], sp["pool_w_group"], sp["pool_scale"], sp["b_gate"], fw["w_out"])
    names = ("dzg", "dy", "dhc", "dp", "merged", "sa", "ac", "pp", "ge", "dt", "dya", "dyb", "dyc", "dq", "db_gate",
             "db_glu", "dln_g", "dln_b", "dscale")
    return dict(zip(names, outs))


def _ffn_fwd(l, x1, norm2, wg, wu, wd):
    s, d = x1.shape
    hc = wg.shape[-1]
    tm = min(512, s)

    def body(x_ref, g_ref, wg_ref, wu_ref, wd_ref, o_ref, h_scr):
        @pl.when(pl.program_id(1) == 0)
        def _():
            xv = x_ref[...]
            r = lax.rsqrt(jnp.mean(xv * xv, axis=-1, keepdims=True) + EPS)
            h_scr[...] = (xv * r * g_ref[...]).astype(h_scr.dtype)
            o_ref[...] = xv

        h = h_scr[...]
        gate = _mm(h, wg_ref[...])
        up = _mm(h, wu_ref[...])
        o_ref[...] += _mm(gate * _sigmoid(gate) * up, wd_ref[...])

    return pl.pallas_call(
        body, name=f"ffn_fwd_l{l}", grid=(s // tm, N_CHIPS),
        in_specs=[BS((tm, d), lambda i, j: (i, 0)), BS((None, 1, d), lambda i, j: (l, 0, 0)),
                  BS((None, d, hc), lambda i, j: (j, 0, 0)), BS((None, d, hc), lambda i, j: (j, 0, 0)),
                  BS((None, hc, d), lambda i, j: (j, 0, 0))],
        out_specs=BS((tm, d), lambda i, j: (i, 0)), out_shape=SDS((s, d), F32),
        scratch_shapes=[pltpu.VMEM((tm, d), MXU_DTYPE)], compiler_params=_params())(x1, norm2, wg, wu, wd)


def _ffn_bwd(l, x1, dx2, norm2, wg, wu, wd):
    s, d = x1.shape
    hc = wg.shape[-1]
    tm = min(512, s)
    m = MXU_DTYPE
    last = N_CHIPS - 1

    def body(x_ref, dx2_ref, g_ref, wg_ref, wu_ref, wd_ref, dx1_ref, h_ref, act_ref, dgate_ref, dup_ref, dn_ref,
             dh_scr, dxb_scr):
        i, j = pl.program_id(0), pl.program_id(1)

        @pl.when(j == 0)
        def _():
            xv = x_ref[...]
            r = lax.rsqrt(jnp.mean(xv * xv, axis=-1, keepdims=True) + EPS)
            h_ref[...] = (xv * r * g_ref[...]).astype(m)
            dxb_scr[...] = dx2_ref[...].astype(m)
            dh_scr[...] = jnp.zeros(dh_scr.shape, F32)

        @pl.when((i == 0) & (j == 0))
        def _():
            dn_ref[...] = jnp.zeros(dn_ref.shape, F32)

        h = h_ref[...]
        gate = _mm(h, wg_ref[...])
        up = _mm(h, wu_ref[...])
        sg = _sigmoid(gate)
        silu = gate * sg
        act_ref[...] = (silu * up).astype(m)
        dact = _mm_nt(dxb_scr[...], wd_ref[...])
        dup = (dact * silu).astype(m)
        dgate = (dact * up * (sg * (1.0 + gate * (1.0 - sg)))).astype(m)
        dup_ref[...] = dup
        dgate_ref[...] = dgate
        dh_scr[...] += _mm_nt(dgate, wg_ref[...]) + _mm_nt(dup, wu_ref[...])

        @pl.when(j == last)
        def _():
            xv = x_ref[...]
            r = lax.rsqrt(jnp.mean(xv * xv, axis=-1, keepdims=True) + EPS)
            xh = xv * r
            dh = dh_scr[...]
            dn_ref[...] += _colsum(dh * xh)
            dxh = dh * g_ref[...]
            dx1_ref[...] = dx2_ref[...] + r * (dxh - xh * jnp.mean(dxh * xh, axis=-1, keepdims=True))

    chunk = BS((None, tm, hc), lambda i, j: (j, i, 0))
    outs = pl.pallas_call(
        body, name=f"ffn_bwd_l{l}", grid=(s // tm, N_CHIPS),
        in_specs=[BS((tm, d), lambda i, j: (i, 0)), BS((tm, d), lambda i, j: (i, 0)),
                  BS((None, 1, d), lambda i, j: (l, 0, 0)),
                  BS((None, d, hc), lambda i, j: (j, 0, 0)), BS((None, d, hc), lambda i, j: (j, 0, 0)),
                  BS((None, hc, d), lambda i, j: (j, 0, 0))],
        out_specs=[BS((tm, d), lambda i, j: (i, 0)), BS((tm, d), lambda i, j: (i, 0)), chunk, chunk, chunk,
                   BS((1, d), lambda i, j: (0, 0))],
        out_shape=[SDS((s, d), F32), SDS((s, d), m), SDS((N_CHIPS, s, hc), m), SDS((N_CHIPS, s, hc), m),
                   SDS((N_CHIPS, s, hc), m), SDS((1, d), F32)],
        scratch_shapes=[pltpu.VMEM((tm, d), F32), pltpu.VMEM((tm, d), m)], compiler_params=_params(),
    )(x1, dx2, norm2, wg, wu, wd)
    return dict(zip(("dx1", "h2", "act", "dgate", "dup", "dnorm2"), outs))


def _loss_head(x, target, gf):
    s, d = x.shape
    tm = min(512, s)

    def body(x_ref, t_ref, g_ref, dx_ref, loss_ref, dg_ref):
        @pl.when(pl.program_id(0) == 0)
        def _():
            loss_ref[...] = jnp.zeros(loss_ref.shape, F32)
            dg_ref[...] = jnp.zeros(dg_ref.shape, F32)

        xv = x_ref[...]
        r = lax.rsqrt(jnp.mean(xv * xv, axis=-1, keepdims=True) + EPS)
        xh = xv * r
        err = xh * g_ref[...] - t_ref[...]
        loss_ref[...] += 0.5 * jnp.sum(jnp.mean(err * err, axis=-1, keepdims=True), axis=0, keepdims=True)
        dyv = err * (1.0 / d)
        dg_ref[...] += _colsum(dyv * xh)
        dxh = dyv * g_ref[...]
        dx_ref[...] = r * (dxh - xh * jnp.mean(dxh * xh, axis=-1, keepdims=True))

    return pl.pallas_call(
        body, name="loss_head", grid=(s // tm,),
        in_specs=[BS((tm, d), lambda i: (i, 0)), BS((tm, d), lambda i: (i, 0)), BS((1, d), lambda i: (0, 0))],
        out_specs=[BS((tm, d), lambda i: (i, 0)), BS((1, 1), lambda i: (0, 0)), BS((1, d), lambda i: (0, 0))],
        out_shape=[SDS((s, d), F32), SDS((1, 1), F32), SDS((1, d), F32)], compiler_params=_params())(x, target, gf)


def _in_proj_bwd(l, dres, x, norm1, w_in, du_a, dv1, dv2, du_c, dzg):
    s, d = x.shape
    nc = w_in.shape[-1]
    tm = min(256, s)
    m = MXU_DTYPE

    def body(dres_ref, x_ref, g_ref, w_ref, a_ref, b1_ref, b2_ref, c_ref, g3_ref, dx_ref, dz_ref, dn_ref):
        @pl.when(pl.program_id(0) == 0)
        def _():
            dn_ref[...] = jnp.zeros(dn_ref.shape, F32)

        dz = jnp.concatenate([a_ref[...], b1_ref[...], b2_ref[...], c_ref[...], g3_ref[...]], axis=1).astype(m)
        dz_ref[...] = dz
        dh = _mm_nt_cols(dz, w_ref)
        xv = x_ref[...]
        r = lax.rsqrt(jnp.mean(xv * xv, axis=-1, keepdims=True) + EPS)
        xh = xv * r
        dn_ref[...] += _colsum(dh * xh)
        dxh = dh * g_ref[...]
        dx_ref[...] = dres_ref[...] + r * (dxh - xh * jnp.mean(dxh * xh, axis=-1, keepdims=True))

    tile = lambda n: BS((tm, n), lambda i: (i, 0))
    return pl.pallas_call(
        body, name=f"in_proj_bwd_l{l}", grid=(s // tm,),
        in_specs=[tile(d), tile(d), BS((None, 1, d), lambda i: (l, 0, 0)),
                  BS((N_CHIPS, d, nc), lambda i: (0, 0, 0)),
                  tile(du_a.shape[1]), tile(dv1.shape[1]), tile(dv2.shape[1]), tile(du_c.shape[1]), tile(dzg.shape[1])],
        out_specs=[tile(d), tile(N_CHIPS * nc), BS((1, d), lambda i: (0, 0))],
        out_shape=[SDS((s, d), F32), SDS((s, N_CHIPS * nc), m), SDS((1, d), F32)], compiler_params=_params(),
    )(dres, x, norm1, w_in, du_a, dv1, dv2, du_c, dzg)


def _tn_matmul(name, a, a_spec, b, b_spec, out_shape, out_spec, grid):
    def body(a_ref, b_ref, o_ref):
        @pl.when(pl.program_id(1) == 0)
        def _():
            o_ref[...] = jnp.zeros(o_ref.shape, F32)

        o_ref[...] += _mm_tn(a_ref[...], b_ref[...])

    return pl.pallas_call(body, name=name, grid=grid, in_specs=[a_spec, b_spec], out_specs=out_spec,
                          out_shape=out_shape, compiler_params=_params())(a, b)


def _scan_consts(pw_ref, lanes, reverse):
    sgn = -1.0 if reverse else 1.0
    steps = [(k, pw_ref[2 * i], sgn * pw_ref[2 * i + 1]) for i, k in enumerate((1, 2, 4))]
    c = 4 if reverse else 3
    return steps, pw_ref[2 * c], sgn * pw_ref[2 * c + 1]


def _scan_block(br, bi, steps, row, reverse):
    for k, ar, ai in steps:
        if reverse:
            mask, sh = row < 8 - k, 8 - k
        else:
            mask, sh = row >= k, k
        sr = jnp.where(mask, pltpu.roll(br, sh, 0), 0.0)
        si = jnp.where(mask, pltpu.roll(bi, sh, 0), 0.0)
        br, bi = br + ar * sr - ai * si, bi + ar * si + ai * sr
    return br, bi


def _ssm_fwd(l, z, bblk_re, bblk_im, cblk_re, cblk_im, pw, dskip):
    s = z.shape[0]
    gc = bblk_re.shape[1]
    gl = bblk_re.shape[2]
    nblk = bblk_re.shape[0]

    def body(u_ref, bre, bim, cre, cim, pw_ref, d_ref, hre, him, y_ref):
        u = u_ref[...]
        hre[...] = _mm(u, bre[...])
        him[...] = _mm(u, bim[...])
        row = lax.broadcasted_iota(jnp.int32, (8, gl), 0)
        steps, car, cai = _scan_consts(pw_ref, gl, False)

        def step(i, carry):
            cr, ci = carry
            r0 = pl.multiple_of(i * 8, 8)
            br, bi = _scan_block(hre[pl.ds(r0, 8), :], him[pl.ds(r0, 8), :], steps, row, False)
            hr = br + car * cr - cai * ci
            hi = bi + car * ci + cai * cr
            hre[pl.ds(r0, 8), :] = hr
            him[pl.ds(r0, 8), :] = hi
            return jnp.broadcast_to(hr[7:8, :], (8, gl)), jnp.broadcast_to(hi[7:8, :], (8, gl))

        zero = jnp.zeros((8, gl), F32)
        lax.fori_loop(0, s // 8, step, (zero, zero))
        y_ref[...] = _mm(hre[...], cre[...]) - _mm(him[...], cim[...]) + d_ref[...] * u

    return pl.pallas_call(
        body, name=f"ssm_fwd_l{l}", grid=(nblk,),
        in_specs=[BS((s, gc), lambda k: (0, k)), BS((None, gc, gl), lambda k: (k, 0, 0)),
                  BS((None, gc, gl), lambda k: (k, 0, 0)), BS((None, gl, gc), lambda k: (k, 0, 0)),
                  BS((None, gl, gc), lambda k: (k, 0, 0)), BS((10, 8, gl), lambda k: (0, 0, k)),
                  BS((1, gc), lambda k: (0, k))],
        out_specs=[BS((s, gl), lambda k: (0, k)), BS((s, gl), lambda k: (0, k)), BS((s, gc), lambda k: (0, k))],
        out_shape=[SDS((s, nblk * gl), F32), SDS((s, nblk * gl), F32), SDS((s, nblk * gc), F32)],
        compiler_params=_params())(z, bblk_re, bblk_im, cblk_re, cblk_im, pw, dskip)


def _ssm_bwd(l, dy, z, hre, him, bblk_re, bblk_im, cblk_re, cblk_im, pw, dskip):
    s = z.shape[0]
    nblk, gc, gl = bblk_re.shape

    def body(dy_ref, u_ref, hre_ref, him_ref, bre, bim, cre, cim, pw_ref, d_ref,
             du_ref, dbre_ref, dbim_ref, dcre_ref, dcim_ref, dar_ref, dai_ref, dd_ref, gre, gim):
        dyv = dy_ref[...]
        u = u_ref[...]
        gre[...] = _mm_nt(dyv, cre[...])
        gim[...] = -_mm_nt(dyv, cim[...])
        dcre_ref[...] = _mm_tn(hre_ref[...], dyv)
        dcim_ref[...] = -_mm_tn(him_ref[...], dyv)
        dd_ref[...] = _colsum(dyv * u)
        row = lax.broadcasted_iota(jnp.int32, (8, gl), 0)
        steps, car, cai = _scan_consts(pw_ref, gl, True)
        n8 = s // 8

        def step(ii, carry):
            cr, ci, accr, acci = carry
            i = n8 - 1 - ii
            r0 = pl.multiple_of(i * 8, 8)
            br, bi = _scan_block(gre[pl.ds(r0, 8), :], gim[pl.ds(r0, 8), :], steps, row, True)
            dr = br + car * cr - cai * ci
            di = bi + car * ci + cai * cr
            gre[pl.ds(r0, 8), :] = dr
            gim[pl.ds(r0, 8), :] = di
            rp = pl.multiple_of(jnp.maximum(i - 1, 0) * 8, 8)
            keep = jnp.where(i > 0, 1.0, 0.0)
            pr = jnp.where(row >= 1, pltpu.roll(hre_ref[pl.ds(r0, 8), :], 1, 0),
                           keep * pltpu.roll(hre_ref[pl.ds(rp, 8), :], 1, 0))
            pi = jnp.where(row >= 1, pltpu.roll(him_ref[pl.ds(r0, 8), :], 1, 0),
                           keep * pltpu.roll(him_ref[pl.ds(rp, 8), :], 1, 0))
            accr = accr + dr * pr + di * pi
            acci = acci + di * pr - dr * pi
            return (jnp.broadcast_to(dr[0:1, :], (8, gl)), jnp.broadcast_to(di[0:1, :], (8, gl)), accr, acci)

        zero = jnp.zeros((8, gl), F32)
        _, _, accr, acci = lax.fori_loop(0, n8, step, (zero, zero, zero, zero))
        dar_ref[...] = _colsum(accr)
        dai_ref[...] = _colsum(acci)
        dbr = gre[...]
        dbi = gim[...]
        du_ref[...] = dyv * d_ref[...] + _mm_nt(dbr, bre[...]) + _mm_nt(dbi, bim[...])
        dbre_ref[...] = _mm_tn(u, dbr)
        dbim_ref[...] = _mm_tn(u, dbi)

    col = lambda n: BS((s, n), lambda k: (0, k))
    blk = lambda a, b: BS((None, a, b), lambda k: (k, 0, 0))
    outs = pl.pallas_call(
        body, name=f"ssm_bwd_l{l}", grid=(nblk,),
        in_specs=[col(gc), col(gc), col(gl), col(gl), blk(gc, gl), blk(gc, gl), blk(gl, gc), blk(gl, gc),
                  BS((10, 8, gl), lambda k: (0, 0, k)), BS((1, gc), lambda k: (0, k))],
        out_specs=[col(gc), blk(gc, gl), blk(gc, gl), blk(gl, gc), blk(gl, gc), BS((1, gl), lambda k: (0, k)),
                   BS((1, gl), lambda k: (0, k)), BS((1, gc), lambda k: (0, k))],
        out_shape=[SDS((s, nblk * gc), F32), SDS((nblk, gc, gl), F32), SDS((nblk, gc, gl), F32),
                   SDS((nblk, gl, gc), F32), SDS((nblk, gl, gc), F32), SDS((1, nblk * gl), F32),
                   SDS((1, nblk * gl), F32), SDS((1, nblk * gc), F32)],
        scratch_shapes=[pltpu.VMEM((s, gl), F32), pltpu.VMEM((s, gl), F32)], compiler_params=_params(),
    )(dy, z, hre, him, bblk_re, bblk_im, cblk_re, cblk_im, pw, dskip)
    return dict(zip(("du", "dbblk_re", "dbblk_im", "dcblk_re", "dcblk_im", "dabar_re", "dabar_im", "dd"), outs))


def _conv_fwd(l, z, wdw, bdw):
    s = z.shape[0]
    cw = wdw.shape[1]
    lb = 128
    tr = min(256, s)
    off1 = cw // lb
    off2 = 2 * cw // lb

    def body(v1_ref, v2_ref, w_ref, b_ref, hc_ref, scr):
        scr[0:CONV_PAD, :] = jnp.zeros((CONV_PAD, lb), F32)
        scr[CONV_PAD:, :] = v1_ref[...] * _sigmoid(v2_ref[...])
        for t in range(s // tr):
            acc = jnp.broadcast_to(b_ref[...], (tr, lb))
            for k in range(CONV_KERNEL):
                acc = acc + w_ref[pl.ds(k, 1), :] * scr[pl.ds(t * tr + CONV_PAD - (CONV_KERNEL - 1) + k, tr), :]
            hc_ref[pl.ds(t * tr, tr), :] = acc

    return pl.pallas_call(
        body, name=f"conv_fwd_l{l}", grid=(cw // lb,),
        in_specs=[BS((s, lb), lambda k: (0, off1 + k)), BS((s, lb), lambda k: (0, off2 + k)),
                  BS((CONV_KERNEL, lb), lambda k: (0, k)), BS((1, lb), lambda k: (0, k))],
        out_specs=BS((s, lb), lambda k: (0, k)), out_shape=SDS((s, cw), F32),
        scratch_shapes=[pltpu.VMEM((s + CONV_PAD, lb), F32)], compiler_params=_params())(z, z, wdw, bdw)


def _conv_bwd(l, dhc, z, wdw):
    s = z.shape[0]
    cw = wdw.shape[1]
    lb = 128
    tr = min(256, s)
    off1 = cw // lb
    off2 = 2 * cw // lb
    nb = cw // lb

    def body(d_ref, v1_ref, v2_ref, w_ref, dv1_ref, dv2_ref, dw_ref, db_ref, hpad, dpad):
        v1 = v1_ref[...]
        sg = _sigmoid(v2_ref[...])
        dv = d_ref[...]
        hpad[0:CONV_PAD, :] = jnp.zeros((CONV_PAD, lb), F32)
        hpad[CONV_PAD:, :] = v1 * sg
        dpad[0:s, :] = dv
        dpad[s:, :] = jnp.zeros((CONV_PAD, lb), F32)
        db_ref[...] = _colsum(dv)
        dws = [jnp.zeros((1, lb), F32) for _ in range(CONV_KERNEL)]
        for t in range(s // tr):
            dt = d_ref[pl.ds(t * tr, tr), :]
            acc = jnp.zeros((tr, lb), F32)
            for k in range(CONV_KERNEL):
                acc = acc + w_ref[pl.ds(k, 1), :] * dpad[pl.ds(t * tr + (CONV_KERNEL - 1) - k, tr), :]
                dws[k] = dws[k] + _colsum(dt * hpad[pl.ds(t * tr + CONV_PAD - (CONV_KERNEL - 1) + k, tr), :])
            sgt = _sigmoid(v2_ref[pl.ds(t * tr, tr), :])
            v1t = v1_ref[pl.ds(t * tr, tr), :]
            dv1_ref[pl.ds(t * tr, tr), :] = acc * sgt
            dv2_ref[pl.ds(t * tr, tr), :] = acc * v1t * (sgt * (1.0 - sgt))
        for k in range(CONV_KERNEL):
            dw_ref[pl.ds(k, 1), :] = dws[k]

    return pl.pallas_call(
        body, name=f"conv_bwd_l{l}", grid=(nb,),
        in_specs=[BS((s, lb), lambda k: (0, k)), BS((s, lb), lambda k: (0, off1 + k)),
                  BS((s, lb), lambda k: (0, off2 + k)), BS((CONV_KERNEL, lb), lambda k: (0, k))],
        out_specs=[BS((s, lb), lambda k: (0, k)), BS((s, lb), lambda k: (0, k)),
                   BS((CONV_KERNEL, lb), lambda k: (0, k)), BS((1, lb), lambda k: (0, k))],
        out_shape=[SDS((s, cw), F32), SDS((s, cw), F32), SDS((CONV_KERNEL, cw), F32), SDS((1, cw), F32)],
        scratch_shapes=[pltpu.VMEM((s + CONV_PAD, lb), F32), pltpu.VMEM((s + CONV_PAD, lb), F32)],
        compiler_params=_params())(dhc, z, z, wdw)


def _pool_window(k):
    return jnp.where(k == 0, float(POOL_WINDOWS[0]),
                     jnp.where(k == 1, float(POOL_WINDOWS[1]),
                               jnp.where(k == 2, float(POOL_WINDOWS[2]), float(POOL_WINDOWS[3]))))


def _pool_fwd(l, z, pw_width):
    s = z.shape[0]
    lb = pw_width // len(POOL_WINDOWS)
    off = 3 * pw_width // lb

    def body(u_ref, p_ref):
        k = pl.program_id(0)
        u = u_ref[...]
        row = lax.broadcasted_iota(jnp.int32, (s, lb), 0)
        sums = [u]
        for sh in (1, 2, 4, 8):
            prev = sums[-1]
            sums.append(prev + jnp.where(row >= sh, pltpu.roll(prev, sh, 0), 0.0))
        sel = jnp.where(k == 0, sums[1], jnp.where(k == 1, sums[2], jnp.where(k == 2, sums[3], sums[4])))
        cnt = jnp.minimum((row + 1).astype(F32), _pool_window(k))
        p_ref[...] = sel / cnt - u

    return pl.pallas_call(
        body, name=f"pool_fwd_l{l}", grid=(len(POOL_WINDOWS),),
        in_specs=[BS((s, lb), lambda k: (0, off + k))], out_specs=BS((s, lb), lambda k: (0, k)),
        out_shape=SDS((s, pw_width), F32), compiler_params=_params())(z)


def _pool_bwd(l, dp):
    s, width = dp.shape
    lb = width // len(POOL_WINDOWS)

    def body(d_ref, du_ref):
        k = pl.program_id(0)
        dv = d_ref[...]
        row = lax.broadcasted_iota(jnp.int32, (s, lb), 0)
        cnt = jnp.minimum((row + 1).astype(F32), _pool_window(k))
        sums = [dv / cnt]
        for sh in (1, 2, 4, 8):
            prev = sums[-1]
            sums.append(prev + jnp.where(row < s - sh, pltpu.roll(prev, s - sh, 0), 0.0))
        sel = jnp.where(k == 0, sums[1], jnp.where(k == 1, sums[2], jnp.where(k == 2, sums[3], sums[4])))
        du_ref[...] = sel - dv

    return pl.pallas_call(
        body, name=f"pool_bwd_l{l}", grid=(len(POOL_WINDOWS),),
        in_specs=[BS((s, lb), lambda k: (0, k))], out_specs=BS((s, lb), lambda k: (0, k)),
        out_shape=SDS((s, width), F32), compiler_params=_params())(dp)


def _zoh(a_re, a_im, log_dt):
    dt = jnp.exp(log_dt)
    mag = jnp.exp(dt * a_re)
    ang = dt * a_im
    abar_re = mag * jnp.cos(ang)
    abar_im = mag * jnp.sin(ang)
    den = a_re * a_re + a_im * a_im
    nr = abar_re - 1.0
    ni = abar_im
    f_re = (nr * a_re + ni * a_im) / den
    f_im = (ni * a_re - nr * a_im) / den
    return abar_re, abar_im, f_re, f_im


def _zoh_fwd(l, a_re, a_im, log_dt):
    def body(ar, ai, ld, o0, o1, o2, o3):
        for ref, val in zip((o0, o1, o2, o3), _zoh(ar[...], ai[...], ld[...])):
            ref[...] = val

    return pl.pallas_call(body, name=f"zoh_fwd_l{l}", out_shape=[SDS(a_re.shape, F32)] * 4)(a_re, a_im, log_dt)


def _zoh_bwd(l, a_re, a_im, log_dt, cts):
    def body(ar, ai, ld, c0, c1, c2, c3, dar, dai, dld):
        _, vjp = jax.vjp(_zoh, ar[...], ai[...], ld[...])
        g = vjp((c0[...], c1[...], c2[...], c3[...]))
        dar[...] = g[0]
        dai[...] = g[1]
        dld[...] = g[2]

    return pl.pallas_call(body, name=f"zoh_bwd_l{l}",
                          out_shape=[SDS(a_re.shape, F32), SDS(a_re.shape, F32), SDS(log_dt.shape, F32)],
                          )(a_re, a_im, log_dt, *cts)


def _bbar_fwd(l, f_re, f_im, b_re, b_im):
    def body(fr, fi, br, bi, o_re, o_im):
        o_re[...] = fr[...] * br[...] - fi[...] * bi[...]
        o_im[...] = fr[...] * bi[...] + fi[...] * br[...]

    return pl.pallas_call(body, name=f"bbar_fwd_l{l}", out_shape=[SDS(b_re.shape, F32)] * 2)(f_re, f_im, b_re, b_im)


def _bbar_bwd(l, f_re, f_im, b_re, b_im, d_re, d_im):
    def body(fr, fi, br, bi, dr, di, dfr, dfi, dbr, dbi):
        dfr[...] = jnp.sum(dr[...] * br[...] + di[...] * bi[...], axis=1, keepdims=True)
        dfi[...] = jnp.sum(di[...] * br[...] - dr[...] * bi[...], axis=1, keepdims=True)
        dbr[...] = fr[...] * dr[...] + fi[...] * di[...]
        dbi[...] = fr[...] * di[...] - fi[...] * dr[...]

    return pl.pallas_call(body, name=f"bbar_bwd_l{l}",
                          out_shape=[SDS(f_re.shape, F32), SDS(f_re.shape, F32), SDS(b_re.shape, F32),
                                     SDS(b_re.shape, F32)])(f_re, f_im, b_re, b_im, d_re, d_im)


def _powers(l, abar_re, abar_im):
    lanes = abar_re.shape[1]

    def body(ar_ref, ai_ref, o_ref):
        ar, ai = ar_ref[...], ai_ref[...]
        pows = [(ar, ai)]
        for _ in range(7):
            pr, pi = pows[-1]
            pows.append((pr * ar - pi * ai, pr * ai + pi * ar))
        row = lax.broadcasted_iota(jnp.int32, (8, lanes), 0)
        for i, k in enumerate((1, 2, 4)):
            o_ref[2 * i] = jnp.broadcast_to(pows[k - 1][0], (8, lanes))
            o_ref[2 * i + 1] = jnp.broadcast_to(pows[k - 1][1], (8, lanes))
        for slot, order in ((3, range(8)), (4, range(7, -1, -1))):
            vr = jnp.zeros((8, lanes), F32)
            vi = jnp.zeros((8, lanes), F32)
            for r, e in enumerate(order):
                vr = jnp.where(row == r, pows[e][0], vr)
                vi = jnp.where(row == r, pows[e][1], vi)
            o_ref[2 * slot] = vr
            o_ref[2 * slot + 1] = vi

    return pl.pallas_call(body, name=f"powers_l{l}", out_shape=SDS((10, 8, lanes), F32))(abar_re, abar_im)


def _block_diag(v, rows_first):
    g, a, b = v.shape
    eye = jnp.eye(8, dtype=v.dtype)
    out = jnp.einsum("kgab,gh->kgahb", v.reshape(g // 8, 8, a, b), eye)
    return out.reshape(g // 8, 8 * a, 8 * b)


def _block_diag_extract(blk, a, b):
    n = blk.shape[0]
    v = blk.reshape(n, 8, a, 8, b)
    return jnp.einsum("kgahb,gh->kgab", v, jnp.eye(8, dtype=blk.dtype)).reshape(n * 8, a, b)


def _ssm_prepare(l, prm):
    g, n, p = SSM_GROUPS, SSM_STATE, SSM_GROUP
    a_re, a_im = prm["ssm_a_re"][l], prm["ssm_a_im"][l]
    log_dt = prm["ssm_log_dt"][l].reshape(g, 1)
    abar_re, abar_im, f_re, f_im = _zoh_fwd(l, a_re, a_im, log_dt)
    b_re = prm["ssm_b_re"][l].reshape(g * n, p)
    b_im = prm["ssm_b_im"][l].reshape(g * n, p)
    fcol_re, fcol_im = f_re.reshape(g * n, 1), f_im.reshape(g * n, 1)
    bbar_re, bbar_im = _bbar_fwd(l, fcol_re, fcol_im, b_re, b_im)
    bblk_re = _block_diag(bbar_re.reshape(g, n, p).transpose(0, 2, 1), True).astype(MXU_DTYPE)
    bblk_im = _block_diag(bbar_im.reshape(g, n, p).transpose(0, 2, 1), True).astype(MXU_DTYPE)
    cblk_re = _block_diag(prm["ssm_c_re"][l].transpose(0, 2, 1), False).astype(MXU_DTYPE)
    cblk_im = _block_diag(prm["ssm_c_im"][l].transpose(0, 2, 1), False).astype(MXU_DTYPE)
    pw = _powers(l, abar_re.reshape(1, g * n), abar_im.reshape(1, g * n))
    return dict(a_re=a_re, a_im=a_im, log_dt=log_dt, b_re=b_re, b_im=b_im, fcol_re=fcol_re, fcol_im=fcol_im,
                bblk_re=bblk_re, bblk_im=bblk_im, cblk_re=cblk_re, cblk_im=cblk_im, pw=pw,
                dskip=prm["ssm_d"][l].reshape(1, g * p))


def _ssm_param_grads(l, sd, r):
    g, n, p = SSM_GROUPS, SSM_STATE, SSM_GROUP
    dbbar_re = _block_diag_extract(r["dbblk_re"], p, n).transpose(0, 2, 1).reshape(g * n, p)
    dbbar_im = _block_diag_extract(r["dbblk_im"], p, n).transpose(0, 2, 1).reshape(g * n, p)
    dfr, dfi, db_re, db_im = _bbar_bwd(l, sd["fcol_re"], sd["fcol_im"], sd["b_re"], sd["b_im"], dbbar_re, dbbar_im)
    cts = (r["dabar_re"].reshape(g, n), r["dabar_im"].reshape(g, n), dfr.reshape(g, n), dfi.reshape(g, n))
    da_re, da_im, dlog_dt = _zoh_bwd(l, sd["a_re"], sd["a_im"], sd["log_dt"], cts)
    dc_re = _block_diag_extract(r["dcblk_re"], n, p).transpose(0, 2, 1)
    dc_im = _block_diag_extract(r["dcblk_im"], n, p).transpose(0, 2, 1)
    return dict(ssm_a_re=da_re, ssm_a_im=da_im, ssm_log_dt=dlog_dt.reshape(g), ssm_b_re=db_re.reshape(g, n, p),
                ssm_b_im=db_im.reshape(g, n, p), ssm_c_re=dc_re, ssm_c_im=dc_im, ssm_d=r["dd"].reshape(g, p))


def _ffn_weight_grads(l, fb, dx2, s):
    d = dx2.shape[1]
    ts = min(512, s)
    ns = s // ts
    hcn = fb["act"].shape[-1]
    full = lambda n: BS((ts, n), lambda j, t: (t, 0))
    g = {}
    chunk = BS((None, ts, hcn), lambda j, t: (j, t, 0))
    for name, key in (("ffn_w_gate", "dgate"), ("ffn_w_up", "dup")):
        g[name] = _tn_matmul(f"d{name}_l{l}", fb["h2"], full(d), fb[key], chunk, SDS((N_CHIPS, d, hcn), F32),
                             BS((None, d, hcn), lambda j, t: (j, 0, 0)), (N_CHIPS, ns))
    g["ffn_w_down"] = _tn_matmul(f"dffn_w_down_l{l}", fb["act"], chunk, dx2, full(d),
                                 SDS((N_CHIPS, hcn, d), F32), BS((None, hcn, d), lambda j, t: (j, 0, 0)),
                                 (N_CHIPS, ns))
    return g


def _mixer_weight_grads(l, sv, mb, dx1, dz, s):
    d = dx1.shape[1]
    cw = sv["y"].shape[1]
    ts = min(512, s)
    ns = s // ts
    ncw = dz.shape[1] // N_CHIPS
    pc = d // N_CHIPS
    gw = cw // len(POOL_WINDOWS)
    full = lambda n: BS((ts, n), lambda j, t: (t, 0))
    g = {}
    g["w_in"] = _tn_matmul(f"dw_in_l{l}", sv["h"], full(d), dz, BS((ts, ncw), lambda j, t: (t, j)),
                           SDS((N_CHIPS, d, ncw), F32), BS((None, d, ncw), lambda j, t: (j, 0, 0)), (N_CHIPS, ns))
    g["w_out"] = _tn_matmul(f"dw_out_l{l}", mb["merged"], BS((ts, pc), lambda j, t: (t, j)), dx1, full(d),
                            SDS((N_CHIPS, pc, d), F32), BS((None, pc, d), lambda j, t: (j, 0, 0)), (N_CHIPS, ns))
    for name, a, b in (("ssm_w_proj", "sa", "dya"), ("conv_w_proj", "ac", "dyb"), ("pool_w_proj", "pp", "dyc")):
        g[name] = _tn_matmul(f"d{name}_l{l}", mb[a], full(cw), mb[b], BS((ts, pc), lambda j, t: (t, j)),
                             SDS((N_CHIPS, cw, pc), F32), BS((None, cw, pc), lambda j, t: (j, 0, 0)), (N_CHIPS, ns))
    gq = cw // N_CHIPS
    g["ssm_w_glu"] = _tn_matmul(f"dssm_w_glu_l{l}", mb["ge"], BS((ts, gq), lambda j, t: (t, j)), mb["dt"], full(cw),
                                SDS((N_CHIPS, gq, cw), F32), BS((None, gq, cw), lambda j, t: (j, 0, 0)), (N_CHIPS, ns))
    dwgrp = _tn_matmul(f"dpool_w_group_l{l}", sv["p"], BS((ts, gw), lambda j, t: (t, j)), mb["dq"],
                       BS((ts, gw), lambda j, t: (t, j)), SDS((len(POOL_WINDOWS), gw, gw), F32),
                       BS((None, gw, gw), lambda j, t: (j, 0, 0)), (len(POOL_WINDOWS), ns))
    return g, dwgrp


def _local_step(x, target, weights_of, prm, on_grads=None):
    s, d = x.shape
    cw = prm["ssm_b_glu"].shape[1]
    sp = {k: prm[k].reshape(N_LAYERS, 1, -1) for k in ("norm1", "norm2", "b_gate", "ssm_b_glu", "conv_ln_g", "conv_ln_b",
                                                        "pool_scale", "conv_b_dw")}
    sp["pool_w_group"] = prm["pool_w_group"]
    saved = []
    xin = x
    for l in range(N_LAYERS):
        fw = weights_of(l, xin)
        sd = _ssm_prepare(l, prm)
        z, h = _in_proj(l, xin, sp["norm1"], fw["w_in"])
        hre, him, y = _ssm_fwd(l, z, sd["bblk_re"], sd["bblk_im"], sd["cblk_re"], sd["cblk_im"], sd["pw"], sd["dskip"])
        wdw = fw["conv_w_dw"]
        hc = _conv_fwd(l, z, wdw, sp["conv_b_dw"][l])
        p = _pool_fwd(l, z, cw)
        x1 = _merge_fwd(l, xin, y, hc, p, z, fw, sp)
        x2 = _ffn_fwd(l, x1, sp["norm2"], fw["ffn_w_gate"], fw["ffn_w_up"], fw["ffn_w_down"])
        saved.append(dict(x=xin, z=z, h=h, hre=hre, him=him, y=y, hc=hc, p=p, x1=x1, sd=sd, wdw=wdw, fw=fw))
        xin = x2
    dx, loss, dfinal = _loss_head(xin, target, prm["final_norm"].reshape(1, d))
    big = [None] * N_LAYERS
    small = [None] * N_LAYERS
    norm2_rows = sp["norm2"]
    for l in reversed(range(N_LAYERS)):
        sv = saved[l]
        sd, fw = sv["sd"], sv["fw"]
        fb = _ffn_bwd(l, sv["x1"], dx, norm2_rows, fw["ffn_w_gate"], fw["ffn_w_up"], fw["ffn_w_down"])
        big[l] = _ffn_weight_grads(l, fb, dx, s)
        spl = sp
        if on_grads is not None:
            spl = dict(sp, ssm_b_glu=sp["ssm_b_glu"] + on_grads(l, "ffn", big[l]))
        mb = _merge_bwd(l, fb["dx1"], sv["y"], sv["hc"], sv["p"], sv["z"], fw, spl)
        du_c = _pool_bwd(l, mb["dp"])
        dv1, dv2, dwdw, dbdw = _conv_bwd(l, mb["dhc"], sv["z"], sv["wdw"])
        sr = _ssm_bwd(l, mb["dy"], sv["z"], sv["hre"], sv["him"], sd["bblk_re"], sd["bblk_im"], sd["cblk_re"],
                      sd["cblk_im"], sd["pw"], sd["dskip"])
        dx, dz, dnorm1 = _in_proj_bwd(l, fb["dx1"], sv["x"], sp["norm1"], fw["w_in"], sr["du"], dv1, dv2, du_c, mb["dzg"])
        mixer, dwgrp = _mixer_weight_grads(l, sv, mb, fb["dx1"], dz, s)
        big[l].update(mixer)
        if on_grads is not None:
            norm2_rows = sp["norm2"] + on_grads(l, "mixer", mixer)
        sg = _ssm_param_grads(l, sd, sr)
        sg.update(norm1=dnorm1.reshape(d), b_gate=mb["db_gate"].reshape(3 * d), ssm_b_glu=mb["db_glu"].reshape(cw),
                  conv_b_dw=dbdw.reshape(cw), conv_ln_g=mb["dln_g"].reshape(cw), conv_ln_b=mb["dln_b"].reshape(cw),
                  pool_w_group=dwgrp, pool_scale=mb["dscale"].reshape(cw), norm2=fb["dnorm2"].reshape(d),
                  conv_w_dw=dwdw)
        small[l] = sg
    return loss[0, 0], dx, big, small, dfinal.reshape(d)


def _place():
    return lax.axis_index("x"), lax.axis_index("y"), lax.axis_index("c")


def _other_chips(x, y):
    return [(1 - x, y), (x, 1 - y), (1 - x, 1 - y)]


def _remote(src, dst, send_sem, recv_sem, device):
    return pltpu.make_async_remote_copy(src_ref=src, dst_ref=dst, send_sem=send_sem, recv_sem=recv_sem,
                                        device_id=device, device_id_type=MESH)


def _hbm(v):
    return pltpu.with_memory_space_constraint(v, pltpu.HBM)


def _cast_into(name, w, place, dtype):
    nl, k, n = w.shape
    tr = _row_tile(k, n)
    nt = k // tr

    def body(place_ref, w_ref, o0_ref, o1_ref):
        @pl.when(pl.program_id(0) == 0)
        def _():
            o0_ref[...] = w_ref[...].astype(dtype)

        @pl.when(pl.program_id(0) == 1)
        def _():
            o1_ref[...] = w_ref[...].astype(dtype)

    return pl.pallas_call(
        body, name=f"cast_{name}",
        grid_spec=pltpu.PrefetchScalarGridSpec(
            num_scalar_prefetch=1, grid=(nl, nt),
            in_specs=[BS((None, tr, n), lambda l, t, pr: (l, t, 0))],
            out_specs=[BS((None, tr, n), lambda l, t, pr: (pr[0], t * (1 - l) + (nt - 1) * l, 0)),
                       BS((None, tr, n), lambda l, t, pr: (pr[0], t * l, 0))]),
        out_shape=[SDS((N_CHIPS, k, n), dtype)] * 2)(place, w)


def _gather_rows(buf, c):
    k = buf.shape[1]
    if k % 2:
        return pl.ds(0, k)
    return pl.ds(pl.multiple_of(c * (k // 2), 8), k // 2)


def _allgather_start(bufs):
    nl, n = len(bufs), len(bufs[0])
    flat = [b for layer in bufs for b in layer]

    def body(*refs):
        ins = refs[:nl * n]
        sems = refs[nl * n:nl * n + 2 * nl]
        x, y, c = _place()
        jme = 2 * x + y
        for l in range(nl):
            for a in range(n):
                buf = ins[l * n + a]
                blk = buf.at[jme, _gather_rows(buf, c)]
                for k, (cx, cy) in enumerate(_other_chips(x, y)):
                    _remote(blk, blk, sems[2 * l].at[3 * a + k], sems[2 * l + 1].at[3 * a + k], (cx, cy, c)).start()

    sem = pltpu.SemaphoreType.DMA((3 * n,))
    outs = pl.pallas_call(
        body, name="allgather_start", in_specs=[HBM] * (nl * n), out_specs=[SEM] * (2 * nl) + [HBM] * (nl * n),
        out_shape=[sem] * (2 * nl) + [pltpu.HBM(b.shape, b.dtype) for b in flat],
        input_output_aliases={i: 2 * nl + i for i in range(nl * n)},
        compiler_params=pltpu.CompilerParams(has_side_effects=SIDE_EFFECT))(*[_hbm(b) for b in flat])
    return [(outs[2 * l], outs[2 * l + 1], outs[2 * nl + l * n:2 * nl + (l + 1) * n]) for l in range(nl)]


def _allgather_wait(l, send_sems, recv_sems, bufs, after):
    n = len(bufs)

    def body(*refs):
        ins = refs[:n]
        ssem, rsem = refs[n], refs[n + 1]
        x, y, c = _place()
        jme = 2 * x + y
        for a in range(n):
            rows = _gather_rows(ins[a], c)
            for k, (cx, cy) in enumerate(_other_chips(x, y)):
                cp = _remote(ins[a].at[jme, rows], ins[a].at[2 * cx + cy, rows], ssem.at[3 * a + k],
                             rsem.at[3 * a + k], (cx, cy, c))
                cp.wait_send()
                cp.wait_recv()

    return pl.pallas_call(
        body, name=f"allgather_wait_l{l}", in_specs=[HBM] * n + [SEM, SEM, ANY], out_specs=[HBM] * n,
        out_shape=[pltpu.HBM(b.shape, b.dtype) for b in bufs], input_output_aliases={i: i for i in range(n)},
        compiler_params=pltpu.CompilerParams(has_side_effects=SIDE_EFFECT))(*bufs, send_sems, recv_sems, after)


def _allgather_forward(l, bufs):
    n = len(bufs)
    split = [a for a in range(n) if bufs[a].shape[1] % 2 == 0]

    def body(*refs):
        ins = refs[:n]
        send_sems, recv_sems = refs[2 * n:]
        x, y, c = _place()
        sibling = (x, y, 1 - c)
        copies = []
        for a in split:
            for k, (cx, cy) in enumerate(_other_chips(x, y)):
                blk = ins[a].at[2 * cx + cy, _gather_rows(ins[a], c)]
                cp = _remote(blk, blk, send_sems.at[a, k], recv_sems.at[a, k], sibling)
                cp.start()
                copies.append(cp)
        for a in split:
            for k, (cx, cy) in enumerate(_other_chips(x, y)):
                blk = ins[a].at[2 * cx + cy, _gather_rows(ins[a], 1 - c)]
                _remote(blk, blk, send_sems.at[a, k], recv_sems.at[a, k], sibling).wait_recv()
        for cp in copies:
            cp.wait_send()

    sem = pltpu.SemaphoreType.DMA((n, 3))
    return pl.pallas_call(
        body, name=f"allgather_forward_l{l}", in_specs=[ANY] * n, out_specs=[ANY] * n,
        out_shape=[SDS(b.shape, b.dtype) for b in bufs], input_output_aliases={i: i for i in range(n)},
        scratch_shapes=[sem, sem])(*bufs)


def _rs_sibling_halves(l, grads):
    n = len(grads)

    def body(*refs):
        ins, outs = refs[:n], refs[n:2 * n]
        send_sems, recv_sems = refs[2 * n:]
        x, y, c = _place()
        copies = []
        for a in range(n):
            rh = ins[a].shape[1] // 2
            src = ins[a].at[:, pl.ds(pl.multiple_of((1 - c) * rh, 8), rh), :]
            cp = _remote(src, outs[a], send_sems.at[a], recv_sems.at[a], (x, y, 1 - c))
            cp.start()
            copies.append(cp)
        for cp in copies:
            cp.wait()

    sem = pltpu.SemaphoreType.DMA((n,))
    return pl.pallas_call(
        body, name=f"rs_sibling_halves_l{l}", in_specs=[ANY] * n, out_specs=[ANY] * n,
        out_shape=[SDS((g.shape[0], g.shape[1] // 2, g.shape[2]), g.dtype) for g in grads],
        scratch_shapes=[sem, sem])(*grads)


def _rs_to_owner(l, parts):
    n = len(parts)
    lands = [lax.empty((3,) + p.shape[1:], p.dtype) for p in parts]

    def body(*refs):
        ins, zones = refs[:n], refs[n:2 * n]
        send_sems, recv_sems = refs[2 * n], refs[2 * n + 1]
        token = refs[-1]
        x, y, c = _place()
        for a in range(n):
            for k, (cx, cy) in enumerate(_other_chips(x, y)):
                _remote(ins[a].at[2 * cx + cy], zones[a].at[k], send_sems.at[3 * a + k], recv_sems.at[3 * a + k],
                        (cx, cy, c)).start()
        token[...] = jnp.zeros(token.shape, F32)

    sem = pltpu.SemaphoreType.DMA((3 * n,))
    outs = pl.pallas_call(
        body, name=f"rs_to_owner_start_{l}", in_specs=[HBM] * (2 * n),
        out_specs=[SEM, SEM] + [HBM] * (2 * n) + [pl.BlockSpec(memory_space=pltpu.VMEM)],
        out_shape=[sem, sem] + [pltpu.HBM(p.shape, p.dtype) for p in parts]
        + [pltpu.HBM(z.shape, z.dtype) for z in lands] + [SDS((8, 128), F32)],
        input_output_aliases={i: 2 + i for i in range(2 * n)},
        compiler_params=pltpu.CompilerParams(has_side_effects=SIDE_EFFECT),
    )(*[_hbm(p) for p in parts], *[_hbm(z) for z in lands])
    return outs[0], outs[1], outs[2:2 + n], outs[2 + n:2 + 2 * n], outs[-1]


def _rs_to_owner_wait(l, send_sems, recv_sems, parts, lands, after):
    n = len(parts)

    def body(*refs):
        ins, zones = refs[:n], refs[n:2 * n]
        ssem, rsem = refs[2 * n], refs[2 * n + 1]
        x, y, c = _place()
        for a in range(n):
            for k, (cx, cy) in enumerate(_other_chips(x, y)):
                cp = _remote(ins[a].at[2 * cx + cy], zones[a].at[k], ssem.at[3 * a + k], rsem.at[3 * a + k],
                             (cx, cy, c))
                cp.wait_send()
                cp.wait_recv()

    outs = pl.pallas_call(
        body, name=f"rs_to_owner_wait_{l}", in_specs=[HBM] * (2 * n) + [SEM, SEM, ANY], out_specs=[HBM] * (2 * n),
        out_shape=[pltpu.HBM(p.shape, p.dtype) for p in parts] + [pltpu.HBM(z.shape, z.dtype) for z in lands],
        input_output_aliases={i: i for i in range(2 * n)},
        compiler_params=pltpu.CompilerParams(has_side_effects=SIDE_EFFECT),
    )(*parts, *lands, send_sems, recv_sems, after)
    return outs[:n], outs[n:]


def _rs_sibling_exchange(l, both):
    n = len(both)

    def body(*refs):
        ins = refs[:n]
        send_sems, recv_sems = refs[2 * n:]
        x, y, c = _place()
        copies = []
        for a in range(n):
            cp = _remote(ins[a].at[c], ins[a].at[c], send_sems.at[a], recv_sems.at[a], (x, y, 1 - c))
            cp.start()
            copies.append(cp)
        for a, cp in enumerate(copies):
            cp.wait_send()
            _remote(ins[a].at[1 - c], ins[a].at[1 - c], send_sems.at[a], recv_sems.at[a], (x, y, 1 - c)).wait_recv()

    sem = pltpu.SemaphoreType.DMA((n,))
    return pl.pallas_call(
        body, name=f"rs_sibling_exchange_{l}", in_specs=[ANY] * n, out_specs=[ANY] * n,
        out_shape=[SDS(b.shape, b.dtype) for b in both], input_output_aliases={i: i for i in range(n)},
        scratch_shapes=[sem, sem])(*both)


def _add_pair(name, g, rb, place):
    nj, r, cols = g.shape
    rh = r // 2
    tr = _row_tile(rh, cols)
    nt = rh // tr

    def body(place_ref, g_ref, rb_ref, o_ref):
        o_ref[...] = g_ref[...] + rb_ref[...]

    return pl.pallas_call(
        body, name=name,
        grid_spec=pltpu.PrefetchScalarGridSpec(
            num_scalar_prefetch=1, grid=(nj, nt),
            in_specs=[BS((None, tr, cols), lambda j, t, pr: (j, pr[1] * nt + t, 0)),
                      BS((None, tr, cols), lambda j, t, pr: (j, t, 0))],
            out_specs=BS((None, tr, cols), lambda j, t, pr: (j, t, 0))),
        out_shape=SDS((nj, rh, cols), F32))(place, g, rb)


def _add_owner(name, part, recv, place):
    _, rh, cols = part.shape
    tr = _row_tile(rh, cols, budget=1024 * 1024)
    nt = rh // tr

    def body(place_ref, p_ref, r_ref, o_ref):
        o_ref[...] = ((p_ref[...] + r_ref[0]) + r_ref[1]) + r_ref[2]

    return pl.pallas_call(
        body, name=name,
        grid_spec=pltpu.PrefetchScalarGridSpec(
            num_scalar_prefetch=1, grid=(nt,),
            in_specs=[BS((None, tr, cols), lambda t, pr: (pr[0], t, 0)), BS((3, tr, cols), lambda t, pr: (0, t, 0))],
            out_specs=BS((None, tr, cols), lambda t, pr: (pr[1], t, 0))),
        out_shape=SDS((2, rh, cols), F32))(place, part, recv)


def _reduce_start(tag, grads, place):
    names = list(grads)
    theirs = _rs_sibling_halves(tag, [grads[n] for n in names])
    pair = [_add_pair(f"rs_add_pair_{n}_{tag}", grads[n], rb, place) for n, rb in zip(names, theirs)]
    send_sems, recv_sems, pair, lands, token = _rs_to_owner(tag, pair)
    return dict(tag=tag, names=names, send_sems=send_sems, recv_sems=recv_sems, pair=pair, lands=lands), token


def _reduce_finish(pending, place, after):
    tag, names = pending["tag"], pending["names"]
    pair, lands = _rs_to_owner_wait(tag, pending["send_sems"], pending["recv_sems"], pending["pair"],
                                    pending["lands"], after)
    mine = [_add_owner(f"rs_add_owner_{n}_{tag}", p, r, place) for n, p, r in zip(names, pair, lands)]
    both = _rs_sibling_exchange(tag, mine)
    return {n: b.reshape(b.shape[0] * b.shape[1], b.shape[2]) for n, b in zip(names, both)}


def _allgather_rows(buf):
    def body(x_ref, out_ref, send_sems, recv_sems, local_sem):
        x, y, c = _place()
        me, sibling = (x, y, c), (x, y, 1 - c)
        chips = _other_chips(x, y)

        def slot(px, py, pc):
            return out_ref.at[4 * px + 2 * py + pc]

        mine = pltpu.make_async_copy(x_ref, slot(*me), local_sem)
        mine.start()
        first = [_remote(x_ref, slot(*me), send_sems.at[0], recv_sems.at[0], sibling)]
        first += [_remote(x_ref, slot(*me), send_sems.at[1 + k], recv_sems.at[1 + k], (cx, cy, c))
                  for k, (cx, cy) in enumerate(chips)]
        for cp in first:
            cp.start()
        passed = []
        for k, (cx, cy) in enumerate(chips):
            blk = slot(cx, cy, c)
            _remote(blk, blk, send_sems.at[1 + k], recv_sems.at[1 + k], (cx, cy, c)).wait_recv()
            cp = _remote(blk, blk, send_sems.at[4 + k], recv_sems.at[4 + k], sibling)
            cp.start()
            passed.append(cp)
        sib = slot(x, y, 1 - c)
        _remote(sib, sib, send_sems.at[0], recv_sems.at[0], sibling).wait_recv()
        for k, (cx, cy) in enumerate(chips):
            blk = slot(cx, cy, 1 - c)
            _remote(blk, blk, send_sems.at[4 + k], recv_sems.at[4 + k], sibling).wait_recv()
        for cp in first + passed:
            cp.wait_send()
        mine.wait()

    return pl.pallas_call(
        body, name="allgather_small_grads", in_specs=[ANY], out_specs=ANY,
        out_shape=SDS((8,) + buf.shape, buf.dtype),
        scratch_shapes=[pltpu.SemaphoreType.DMA((7,)), pltpu.SemaphoreType.DMA((7,)), pltpu.SemaphoreType.DMA(())],
    )(buf)


def _sum_devices(gathered):
    _, r, cols = gathered.shape
    tr = _row_tile(r, cols, budget=256 * 1024)

    def body(g_ref, o_ref):
        acc = g_ref[0]
        for k in range(1, 8):
            acc = acc + g_ref[k]
        o_ref[...] = acc

    return pl.pallas_call(body, name="sum_small_grads", grid=(r // tr,),
                          in_specs=[BS((8, tr, cols), lambda t: (0, t, 0))], out_specs=BS((tr, cols), lambda t: (t, 0)),
                          out_shape=SDS((r, cols), F32))(gathered)


def _adamw_values(w, g, m, v):
    m = ADAM_B1 * m + (1.0 - ADAM_B1) * g
    v = ADAM_B2 * v + (1.0 - ADAM_B2) * (g * g)
    m_hat = m / (1.0 - ADAM_B1 ** ADAM_STEP)
    v_hat = v / (1.0 - ADAM_B2 ** ADAM_STEP)
    delta = -ADAM_LR * (m_hat / (jnp.sqrt(v_hat) + ADAM_EPS) + ADAM_WD * w)
    return delta, m, v


def _adamw_big(name, l, w, m, v, g, earlier=None):
    nl, r, cols = w.shape
    tr = _row_tile(r, cols, budget=1024 * 1024)
    nt = r // tr
    n_prev = 0 if earlier is None else 4

    def body(*refs):
        w_ref, m_ref, v_ref, g_ref = refs[:4]
        go_ref, d_ref, mo_ref, vo_ref = refs[4 + n_prev:]
        gv = g_ref[...]
        delta, m_new, v_new = _adamw_values(w_ref[...], gv, m_ref[...], v_ref[...])
        go_ref[...] = gv
        d_ref[...] = delta
        mo_ref[...] = m_new
        vo_ref[...] = v_new

    layer = BS((None, tr, cols), lambda t: (l, t, 0))
    return pl.pallas_call(
        body, name=f"adamw_{name}_l{l}", grid=(nt,),
        in_specs=[layer, layer, layer, BS((tr, cols), lambda t: (t, 0))] + [ANY] * n_prev,
        out_specs=[layer] * 4, out_shape=[SDS(w.shape, F32)] * 4,
        input_output_aliases={4 + i: i for i in range(n_prev)}, compiler_params=_params(),
    )(w, m, v, g, *(earlier or ()))


def _adamw_rows(w, m, v, g):
    r, cols = w.shape
    tr = _row_tile(r, cols, budget=512 * 1024)

    def body(w_ref, m_ref, v_ref, g_ref, d_ref, mo_ref, vo_ref):
        delta, m_new, v_new = _adamw_values(w_ref[...], g_ref[...], m_ref[...], v_ref[...])
        d_ref[...] = delta
        mo_ref[...] = m_new
        vo_ref[...] = v_new

    spec = BS((tr, cols), lambda t: (t, 0))
    return pl.pallas_call(body, name="adamw_small", grid=(r // tr,), in_specs=[spec] * 4, out_specs=[spec] * 3,
                          out_shape=[SDS(w.shape, F32)] * 3)(w, m, v, g)


PACK_ALIGN = 8 * 128


def _pack_rows(arrays):
    parts = []
    for a in arrays:
        flat = a.reshape(-1)
        pad = (-flat.shape[0]) % PACK_ALIGN
        if pad:
            flat = jnp.pad(flat, (0, pad))
        parts.append(flat.reshape(-1, 128))
    return jnp.concatenate(parts, axis=0)


def _unpack_rows(buf, shapes):
    out, row = [], 0
    for shape in shapes:
        size = math.prod(shape)
        rows = -(-size // PACK_ALIGN) * (PACK_ALIGN // 128)
        out.append(buf[row:row + rows].reshape(-1)[:size].reshape(shape))
        row += rows
    return out


def kernel(x, norm1, w_in, b_gate, ssm_a_re, ssm_a_im, ssm_log_dt, ssm_b_re, ssm_b_im, ssm_c_re, ssm_c_im, ssm_d, ssm_w_glu, ssm_b_glu, ssm_w_proj, conv_w_dw, conv_b_dw, conv_ln_g, conv_ln_b, conv_w_proj, pool_w_group, pool_scale, pool_w_proj, w_out, norm2, ffn_w_gate, ffn_w_up, ffn_w_down, final_norm, loss_target, m_norm1, m_w_in, m_b_gate, m_ssm_a_re, m_ssm_a_im, m_ssm_log_dt, m_ssm_b_re, m_ssm_b_im, m_ssm_c_re, m_ssm_c_im, m_ssm_d, m_ssm_w_glu, m_ssm_b_glu, m_ssm_w_proj, m_conv_w_dw, m_conv_b_dw, m_conv_ln_g, m_conv_ln_b, m_conv_w_proj, m_pool_w_group, m_pool_scale, m_pool_w_proj, m_w_out, m_norm2, m_ffn_w_gate, m_ffn_w_up, m_ffn_w_down, m_final_norm, v_norm1, v_w_in, v_b_gate, v_ssm_a_re, v_ssm_a_im, v_ssm_log_dt, v_ssm_b_re, v_ssm_b_im, v_ssm_c_re, v_ssm_c_im, v_ssm_d, v_ssm_w_glu, v_ssm_b_glu, v_ssm_w_proj, v_conv_w_dw, v_conv_b_dw, v_conv_ln_g, v_conv_ln_b, v_conv_w_proj, v_pool_w_group, v_pool_scale, v_pool_w_proj, v_w_out, v_norm2, v_ffn_w_gate, v_ffn_w_up, v_ffn_w_down, v_final_norm):
    given = dict(locals())
    prm = {n: given[n] for n in WEIGHTS}
    mom = {n: given["m_" + n] for n in WEIGHTS}
    var = {n: given["v_" + n] for n in WEIGHTS}
    cx, cy, cc = _place()
    place = jnp.stack([2 * cx + cy, cc]).astype(jnp.int32)

    gather_names = BIG + ("conv_w_dw",)
    dw_shard = prm["conv_w_dw"].reshape(N_LAYERS, CONV_KERNEL, -1)
    casts = [_cast_into(n, prm[n], place, MXU_DTYPE) for n in BIG] + [_cast_into("conv_w_dw", dw_shard, place, F32)]
    in_flight = _allgather_start([[pair[l] for pair in casts] for l in range(N_LAYERS)])

    def weights_of(l, x_l):
        send_sems, recv_sems, bufs = in_flight[l]
        bufs = _allgather_forward(l, _allgather_wait(l, send_sems, recv_sems, bufs, x_l))
        fw = dict(zip(gather_names, bufs))
        fw["conv_w_dw"] = fw["conv_w_dw"].transpose(1, 0, 2).reshape(CONV_KERNEL, -1)
        return fw

    pending = []
    last_token = []

    def on_grads(l, group, grads):
        started, token = _reduce_start(f"{l}_{group}", grads, place)
        pending.append((l, started))
        last_token[:] = [token]
        return token[0, 0]

    loss, dx, _, small, dfinal = _local_step(x[0], loss_target[0], weights_of, prm, on_grads)
    loss = lax.psum(loss, ("x", "y", "c"))

    reduced = [{} for _ in range(N_LAYERS)]
    for l, started in pending:
        reduced[l].update(_reduce_finish(started, place, last_token[0]))
    packed_names = [n for n in SMALL if n != "final_norm"] + ["conv_w_dw"]
    local_small = [jnp.stack([small[l][n] for l in range(N_LAYERS)]) for n in packed_names] + [dfinal]
    gsum = _sum_devices(_allgather_rows(_pack_rows(local_small)))
    small_shapes = [a.shape for a in local_small]
    gsmall = dict(zip(packed_names + ["final_norm"], _unpack_rows(gsum, small_shapes)))
    lanes = dw_shard.shape[-1]
    gsmall["conv_w_dw"] = lax.dynamic_slice_in_dim(gsmall["conv_w_dw"], (2 * cx + cy) * lanes, lanes, axis=2)

    out = {}
    for l in reversed(range(N_LAYERS)):
        for n in BIG:
            out[n] = _adamw_big(n, l, prm[n], mom[n], var[n], reduced[l][n], out.get(n))
    small_names = list(SMALL) + ["conv_w_dw"]
    w_rows = _pack_rows([prm[n] for n in small_names])
    m_rows = _pack_rows([mom[n] for n in small_names])
    v_rows = _pack_rows([var[n] for n in small_names])
    g_rows = _pack_rows([gsmall[n] for n in small_names])
    shapes = [prm[n].shape for n in small_names]
    d_s, m_s, v_s = (_unpack_rows(r, shapes) for r in _adamw_rows(w_rows, m_rows, v_rows, g_rows))
    for i, n in enumerate(small_names):
        out[n] = (gsmall[n].reshape(prm[n].shape), d_s[i], m_s[i], v_s[i])
    grads = [out[n][0] for n in WEIGHTS]
    deltas = [out[n][1] for n in WEIGHTS]
    new_m = [out[n][2] for n in WEIGHTS]
    new_v = [out[n][3] for n in WEIGHTS]
    return (loss, dx[None], *grads, *deltas, *new_m, *new_v)
```

```python
import functools
import math

import jax
import jax.numpy as jnp
from jax import lax
from jax.experimental import pallas as pl
from jax.experimental.pallas import tpu as pltpu

F32 = jnp.float32
MXU_DTYPE = jnp.bfloat16
WIRE_DTYPE = jnp.bfloat16
SDS = jax.ShapeDtypeStruct
BS = pl.BlockSpec
ANY = pl.BlockSpec(memory_space=pl.ANY)
HBM = pl.BlockSpec(memory_space=pltpu.HBM)
SEM = pl.BlockSpec(memory_space=pltpu.SEMAPHORE)
SIDE_EFFECT = pltpu.SideEffectType.DATAFLOW_SIDE_EFFECTING
MESH = pl.DeviceIdType.MESH

EPS = 1e-6
N_CHIPS = 4
N_LAYERS = 2
SSM_GROUPS, SSM_STATE, SSM_GROUP = 32, 64, 16
CONV_KERNEL = 31
CONV_PAD = 32
POOL_WINDOWS = (2, 4, 8, 16)
GELU_C = math.sqrt(2.0 / math.pi)
ADAM_LR, ADAM_B1, ADAM_B2, ADAM_EPS, ADAM_WD, ADAM_STEP = 0.001, 0.9, 0.999, 1e-08, 0.01, 10
VMEM_LIMIT = 56 * 1024 * 1024

BIG = ("w_in", "ssm_w_glu", "ssm_w_proj", "conv_w_proj", "pool_w_proj", "w_out", "ffn_w_gate", "ffn_w_up", "ffn_w_down")
TRANSPOSED = ("ffn_w_gate", "ffn_w_up")
GATHER_GROUPS = {
    "in": ("w_in",),
    "mixer": ("ssm_w_glu", "ssm_w_proj", "conv_w_proj", "pool_w_proj", "w_out", "conv_w_dw"),
    "ffn": ("ffn_w_gate", "ffn_w_up", "ffn_w_down"),
}
SMALL = ("norm1", "b_gate", "ssm_a_re", "ssm_a_im", "ssm_log_dt", "ssm_b_re", "ssm_b_im", "ssm_c_re", "ssm_c_im",
         "ssm_d", "ssm_b_glu", "conv_b_dw", "conv_ln_g", "conv_ln_b", "pool_w_group", "pool_scale", "norm2",
         "final_norm")
WEIGHTS = ("norm1", "w_in", "b_gate", "ssm_a_re", "ssm_a_im", "ssm_log_dt", "ssm_b_re", "ssm_b_im", "ssm_c_re",
           "ssm_c_im", "ssm_d", "ssm_w_glu", "ssm_b_glu", "ssm_w_proj", "conv_w_dw", "conv_b_dw", "conv_ln_g",
           "conv_ln_b", "conv_w_proj", "pool_w_group", "pool_scale", "pool_w_proj", "w_out", "norm2", "ffn_w_gate",
           "ffn_w_up", "ffn_w_down", "final_norm")


def _params(vmem=True):
    return pltpu.CompilerParams(vmem_limit_bytes=VMEM_LIMIT) if vmem else None


def _mm(a, b):
    return jnp.dot(a.astype(MXU_DTYPE), b.astype(MXU_DTYPE), preferred_element_type=F32)


def _mm_nt(a, b):
    return lax.dot_general(a.astype(MXU_DTYPE), b.astype(MXU_DTYPE), (((1,), (1,)), ((), ())),
                           preferred_element_type=F32)


def _mm_tn(a, b):
    return lax.dot_general(a.astype(MXU_DTYPE), b.astype(MXU_DTYPE), (((0,), (0,)), ((), ())),
                           preferred_element_type=F32)


def _sigmoid(x):
    return jax.nn.sigmoid(x)


def _gelu(x):
    t = jnp.tanh(GELU_C * (x + 0.044715 * (x * x * x)))
    return x * (0.5 * (1.0 + t)), t


def _gelu_grad(x, t):
    return 0.5 * (1.0 + t) + 0.5 * x * (1.0 - t * t) * (GELU_C * (1.0 + 3.0 * 0.044715 * x * x))


def _colsum(v):
    return jnp.sum(v, axis=0, keepdims=True)


def _row_tile(rows, cols, itemsize=4, budget=1536 * 1024):
    best = None
    for t in range(8, rows + 1, 8):
        if rows % t == 0 and t * cols * itemsize <= budget:
            best = t
    return best if best is not None else rows


def _in_proj(l, x, norm1, w_in):
    s, d = x.shape
    nc = w_in.shape[-1]
    tm = min(512, s)

    def body(x_ref, g_ref, w_ref, z_ref, h_ref):
        @pl.when(pl.program_id(1) == 0)
        def _():
            xv = x_ref[...]
            r = lax.rsqrt(jnp.mean(xv * xv, axis=-1, keepdims=True) + EPS)
            h_ref[...] = (xv * r * g_ref[...]).astype(h_ref.dtype)

        z_ref[...] = _mm(h_ref[...], w_ref[...])

    return pl.pallas_call(
        body, name=f"in_proj_l{l}", grid=(s // tm, N_CHIPS),
        in_specs=[BS((tm, d), lambda i, j: (i, 0)), BS((None, 1, d), lambda i, j: (l, 0, 0)),
                  BS((None, d, nc), lambda i, j: (j, 0, 0))],
        out_specs=[BS((tm, nc), lambda i, j: (i, j)), BS((tm, d), lambda i, j: (i, 0))],
        out_shape=[SDS((s, N_CHIPS * nc), F32), SDS((s, d), MXU_DTYPE)],
        compiler_params=_params())(x, norm1, w_in)


def _mm_cols(a, w_ref):
    return jnp.concatenate([_mm(a, w_ref[j]) for j in range(N_CHIPS)], axis=1)


def _mm_nt_cols(dv, w_ref):
    nc = w_ref.shape[-1]
    acc = _mm_nt(dv[:, 0:nc], w_ref[0])
    for j in range(1, N_CHIPS):
        acc = acc + _mm_nt(dv[:, j * nc:(j + 1) * nc], w_ref[j])
    return acc


def _merge_values(y, hc, p, zg, wglu, bglu, wpa, wpb, wpc, lng, lnb, wgrp, scale, bg):
    v = {}
    ge, th = _gelu(y)
    t = _mm(ge, wglu) + bglu
    sg = _sigmoid(t)
    sa = ge * sg
    ya = _mm_cols(sa, wpa)
    mu = jnp.mean(hc, axis=-1, keepdims=True)
    xc = hc - mu
    r = lax.rsqrt(jnp.mean(xc * xc, axis=-1, keepdims=True) + EPS)
    xh = xc * r
    ln = xh * lng + lnb
    sl = _sigmoid(ln)
    ac = ln * sl
    yb = _mm_cols(ac, wpb)
    gw = p.shape[1] // len(POOL_WINDOWS)
    q = jnp.concatenate([_mm(p[:, k * gw:(k + 1) * gw], wgrp[k]) for k in range(len(POOL_WINDOWS))], axis=1)
    pp = q * scale
    yc = _mm_cols(pp, wpc)
    d = ya.shape[1]
    gates = [_sigmoid(zg[k] + bg[:, k * d:(k + 1) * d]) for k in range(3)]
    merged = gates[0] * ya + gates[1] * yb + gates[2] * yc
    v.update(ge=ge, th=th, sg=sg, sa=sa, ya=ya, r=r, xh=xh, ln=ln, sl=sl, ac=ac, yb=yb, q=q, pp=pp, yc=yc,
             gates=gates, merged=merged)
    return v


def _merge_specs(l, tm, d, cw):
    row = lambda n: BS((None, 1, n), lambda i: (l, 0, 0))
    return [
        BS((tm, cw), lambda i: (i, 0)),
        BS((tm, cw), lambda i: (i, 0)),
        BS((tm, cw), lambda i: (i, 0)),
        BS((tm, d), lambda i: (i, 2)), BS((tm, d), lambda i: (i, 3)), BS((tm, d), lambda i: (i, 4)),
        BS((N_CHIPS, cw // N_CHIPS, cw), lambda i: (0, 0, 0)),
        row(cw),
        BS((N_CHIPS, cw, d // N_CHIPS), lambda i: (0, 0, 0)),
        BS((N_CHIPS, cw, d // N_CHIPS), lambda i: (0, 0, 0)),
        BS((N_CHIPS, cw, d // N_CHIPS), lambda i: (0, 0, 0)),
        row(cw), row(cw),
        BS((None, 4, cw // 4, cw // 4), lambda i: (l, 0, 0, 0)),
        row(cw),
        row(3 * d),
        BS((N_CHIPS, d // N_CHIPS, d), lambda i: (0, 0, 0)),
    ]


def _merge_fwd(l, x, y, hc, p, z, fw, sp):
    s, d = x.shape
    cw = y.shape[1]
    tm = min(256, s)

    def body(x_ref, y_ref, hc_ref, p_ref, z0, z1, z2, wglu, bglu, wpa, wpb, wpc, lng, lnb, wgrp, scale, bg, wout,
             x1_ref):
        v = _merge_values(y_ref[...], hc_ref[...], p_ref[...], (z0[...], z1[...], z2[...]),
                          wglu[...].reshape(cw, cw), bglu[...], wpa, wpb, wpc, lng[...], lnb[...], wgrp, scale[...],
                          bg[...])
        x1_ref[...] = x_ref[...] + _mm(v["merged"], wout[...].reshape(d, d))

    return pl.pallas_call(
        body, name=f"merge_fwd_l{l}", grid=(s // tm,),
        in_specs=[BS((tm, d), lambda i: (i, 0))] + _merge_specs(l, tm, d, cw),
        out_specs=BS((tm, d), lambda i: (i, 0)), out_shape=SDS((s, d), F32), compiler_params=_params(),
    )(x, y, hc, p, z, z, z, fw["ssm_w_glu"], sp["ssm_b_glu"], fw["ssm_w_proj"], fw["conv_w_proj"], fw["pool_w_proj"],
      sp["conv_ln_g"], sp["conv_ln_b"], sp["pool_w_group"], sp["pool_scale"], sp["b_gate"], fw["w_out"])


def _merge_bwd(l, dx1, y, hc, p, z, fw, sp):
    s, d = dx1.shape
    cw = y.shape[1]
    tm = min(256, s)
    m = MXU_DTYPE

    def body(dx1_ref, y_ref, hc_ref, p_ref, z0, z1, z2, wglu, bglu, wpa, wpb, wpc, lng, lnb, wgrp, scale, bg, wout,
             dzg_ref, dy_ref, dhc_ref, dp_ref, merged_ref, sa_ref, ac_ref, pp_ref, ge_ref, dt_ref, dya_ref, dyb_ref,
             dyc_ref, dq_ref, dbg_ref, dbglu_ref, dlng_ref, dlnb_ref, dscale_ref):
        yv = y_ref[...]
        wg = wglu[...].reshape(cw, cw)
        v = _merge_values(yv, hc_ref[...], p_ref[...], (z0[...], z1[...], z2[...]), wg, bglu[...], wpa, wpb, wpc,
                          lng[...], lnb[...], wgrp, scale[...], bg[...])
        dm = _mm_nt(dx1_ref[...], wout[...].reshape(d, d))
        ys = (v["ya"], v["yb"], v["yc"])
        dys = []
        for k in range(3):
            gk = v["gates"][k]
            dzg_ref[:, k * d:(k + 1) * d] = dm * ys[k] * (gk * (1.0 - gk))
            dys.append((dm * gk).astype(m))
        dsa = _mm_nt_cols(dys[0], wpa)
        dac = _mm_nt_cols(dys[1], wpb)
        dpp = _mm_nt_cols(dys[2], wpc)
        ge, sg = v["ge"], v["sg"]
        dt = dsa * ge * (sg * (1.0 - sg))
        dge = dsa * sg + _mm_nt(dt, wg)
        dy_ref[...] = dge * _gelu_grad(yv, v["th"])
        ln, sl, xh = v["ln"], v["sl"], v["xh"]
        dln = dac * (sl * (1.0 + ln * (1.0 - sl)))
        dxh = dln * lng[...]
        dhc_ref[...] = v["r"] * (dxh - jnp.mean(dxh, axis=-1, keepdims=True)
                                 - xh * jnp.mean(dxh * xh, axis=-1, keepdims=True))
        dq = dpp * scale[...]
        gw = cw // len(POOL_WINDOWS)
        for k in range(len(POOL_WINDOWS)):
            dp_ref[:, k * gw:(k + 1) * gw] = _mm_nt(dq[:, k * gw:(k + 1) * gw], wgrp[k])
        merged_ref[...] = v["merged"].astype(m)
        sa_ref[...] = v["sa"].astype(m)
        ac_ref[...] = v["ac"].astype(m)
        pp_ref[...] = v["pp"].astype(m)
        ge_ref[...] = ge.astype(m)
        dt_ref[...] = dt.astype(m)
        dya_ref[...] = dys[0]
        dyb_ref[...] = dys[1]
        dyc_ref[...] = dys[2]
        dq_ref[...] = dq.astype(m)

        @pl.when(pl.program_id(0) == 0)
        def _():
            for ref in (dbg_ref, dbglu_ref, dlng_ref, dlnb_ref, dscale_ref):
                ref[...] = jnp.zeros(ref.shape, F32)

        dbg_ref[...] += _colsum(dzg_ref[...])
        dbglu_ref[...] += _colsum(dt)
        dlng_ref[...] += _colsum(dln * xh)
        dlnb_ref[...] += _colsum(dln)
        dscale_ref[...] += _colsum(dpp * v["q"])

    tile = lambda n: BS((tm, n), lambda i: (i, 0))
    acc = lambda n: BS((1, n), lambda i: (0, 0))
    outs = pl.pallas_call(
        body, name=f"merge_bwd_l{l}", grid=(s // tm,),
        in_specs=[tile(d)] + _merge_specs(l, tm, d, cw),
        out_specs=[tile(3 * d), tile(cw), tile(cw), tile(cw), tile(d), tile(cw), tile(cw), tile(cw), tile(cw), tile(cw),
                   tile(d), tile(d), tile(d), tile(cw), acc(3 * d), acc(cw), acc(cw), acc(cw), acc(cw)],
        out_shape=[SDS((s, 3 * d), F32), SDS((s, cw), F32), SDS((s, cw), F32), SDS((s, cw), F32), SDS((s, d), m),
                   SDS((s, cw), m), SDS((s, cw), m), SDS((s, cw), m), SDS((s, cw), m), SDS((s, cw), m), SDS((s, d), m),
                   SDS((s, d), m), SDS((s, d), m), SDS((s, cw), m), SDS((1, 3 * d), F32), SDS((1, cw), F32),
                   SDS((1, cw), F32), SDS((1, cw), F32), SDS((1, cw), F32)],
        compiler_params=_params(),
    )(dx1, y, hc, p, z, z, z, fw["ssm_w_glu"], sp["ssm_b_glu"], fw["ssm_w_proj"], fw["conv_w_proj"], fw["pool_w_proj"],
      sp["conv_ln_g"], sp["conv_ln_b"], sp["pool_w_group"], sp["pool_scale"], sp["b_gate"], fw["w_out"])
    names = ("dzg", "dy", "dhc", "dp", "merged", "sa", "ac", "pp", "ge", "dt", "dya", "dyb", "dyc", "dq", "db_gate",
             "db_glu", "dln_g", "dln_b", "dscale")
    return dict(zip(names, outs))


def _ffn_fwd(l, x1, norm2, wg, wu, wd):
    s, d = x1.shape
    hc = wd.shape[1]
    tm = min(512, s)

    def body(x_ref, g_ref, wg_ref, wu_ref, wd_ref, o_ref, h_scr):
        @pl.when(pl.program_id(1) == 0)
        def _():
            xv = x_ref[...]
            r = lax.rsqrt(jnp.mean(xv * xv, axis=-1, keepdims=True) + EPS)
            h_scr[...] = (xv * r * g_ref[...]).astype(h_scr.dtype)
            o_ref[...] = xv

        h = h_scr[...]
        gate = _mm_nt(h, wg_ref[...])
        up = _mm_nt(h, wu_ref[...])
        o_ref[...] += _mm(gate * _sigmoid(gate) * up, wd_ref[...])

    return pl.pallas_call(
        body, name=f"ffn_fwd_l{l}", grid=(s // tm, N_CHIPS),
        in_specs=[BS((tm, d), lambda i, j: (i, 0)), BS((None, 1, d), lambda i, j: (l, 0, 0)),
                  BS((None, hc, d), lambda i, j: (j, 0, 0)), BS((None, hc, d), lambda i, j: (j, 0, 0)),
                  BS((None, hc, d), lambda i, j: (j, 0, 0))],
        out_specs=BS((tm, d), lambda i, j: (i, 0)), out_shape=SDS((s, d), F32),
        scratch_shapes=[pltpu.VMEM((tm, d), MXU_DTYPE)], compiler_params=_params())(x1, norm2, wg, wu, wd)


def _ffn_bwd(l, x1, dx2, norm2, wg, wu, wd):
    s, d = x1.shape
    hc = wd.shape[1]
    tm = min(512, s)
    m = MXU_DTYPE
    last = N_CHIPS - 1

    def body(x_ref, dx2_ref, g_ref, wg_ref, wu_ref, wd_ref, dx1_ref, h_ref, act_ref, dgate_ref, dup_ref, dn_ref,
             dh_scr, dxb_scr):
        i, j = pl.program_id(0), pl.program_id(1)

        @pl.when(j == 0)
        def _():
            xv = x_ref[...]
            r = lax.rsqrt(jnp.mean(xv * xv, axis=-1, keepdims=True) + EPS)
            h_ref[...] = (xv * r * g_ref[...]).astype(m)
            dxb_scr[...] = dx2_ref[...].astype(m)
            dh_scr[...] = jnp.zeros(dh_scr.shape, F32)

        @pl.when((i == 0) & (j == 0))
        def _():
            dn_ref[...] = jnp.zeros(dn_ref.shape, F32)

        h = h_ref[...]
        gate = _mm_nt(h, wg_ref[...])
        up = _mm_nt(h, wu_ref[...])
        sg = _sigmoid(gate)
        silu = gate * sg
        act_ref[...] = (silu * up).astype(m)
        dact = _mm_nt(dxb_scr[...], wd_ref[...])
        dup = (dact * silu).astype(m)
        dgate = (dact * up * (sg * (1.0 + gate * (1.0 - sg)))).astype(m)
        dup_ref[...] = dup
        dgate_ref[...] = dgate
        dh_scr[...] += _mm(dgate, wg_ref[...]) + _mm(dup, wu_ref[...])

        @pl.when(j == last)
        def _():
            xv = x_ref[...]
            r = lax.rsqrt(jnp.mean(xv * xv, axis=-1, keepdims=True) + EPS)
            xh = xv * r
            dh = dh_scr[...]
            dn_ref[...] += _colsum(dh * xh)
            dxh = dh * g_ref[...]
            dx1_ref[...] = dx2_ref[...] + r * (dxh - xh * jnp.mean(dxh * xh, axis=-1, keepdims=True))

    chunk = BS((None, tm, hc), lambda i, j: (j, i, 0))
    outs = pl.pallas_call(
        body, name=f"ffn_bwd_l{l}", grid=(s // tm, N_CHIPS),
        in_specs=[BS((tm, d), lambda i, j: (i, 0)), BS((tm, d), lambda i, j: (i, 0)),
                  BS((None, 1, d), lambda i, j: (l, 0, 0)),
                  BS((None, hc, d), lambda i, j: (j, 0, 0)), BS((None, hc, d), lambda i, j: (j, 0, 0)),
                  BS((None, hc, d), lambda i, j: (j, 0, 0))],
        out_specs=[BS((tm, d), lambda i, j: (i, 0)), BS((tm, d), lambda i, j: (i, 0)), chunk, chunk, chunk,
                   BS((1, d), lambda i, j: (0, 0))],
        out_shape=[SDS((s, d), F32), SDS((s, d), m), SDS((N_CHIPS, s, hc), m), SDS((N_CHIPS, s, hc), m),
                   SDS((N_CHIPS, s, hc), m), SDS((1, d), F32)],
        scratch_shapes=[pltpu.VMEM((tm, d), F32), pltpu.VMEM((tm, d), m)], compiler_params=_params(),
    )(x1, dx2, norm2, wg, wu, wd)
    return dict(zip(("dx1", "h2", "act", "dgate", "dup", "dnorm2"), outs))


def _loss_head(x, target, gf):
    s, d = x.shape
    tm = min(512, s)

    def body(x_ref, t_ref, g_ref, dx_ref, loss_ref, dg_ref):
        @pl.when(pl.program_id(0) == 0)
        def _():
            loss_ref[...] = jnp.zeros(loss_ref.shape, F32)
            dg_ref[...] = jnp.zeros(dg_ref.shape, F32)

        xv = x_ref[...]
        r = lax.rsqrt(jnp.mean(xv * xv, axis=-1, keepdims=True) + EPS)
        xh = xv * r
        err = xh * g_ref[...] - t_ref[...]
        loss_ref[...] += 0.5 * jnp.sum(jnp.mean(err * err, axis=-1, keepdims=True), axis=0, keepdims=True)
        dyv = err * (1.0 / d)
        dg_ref[...] += _colsum(dyv * xh)
        dxh = dyv * g_ref[...]
        dx_ref[...] = r * (dxh - xh * jnp.mean(dxh * xh, axis=-1, keepdims=True))

    return pl.pallas_call(
        body, name="loss_head", grid=(s // tm,),
        in_specs=[BS((tm, d), lambda i: (i, 0)), BS((tm, d), lambda i: (i, 0)), BS((1, d), lambda i: (0, 0))],
        out_specs=[BS((tm, d), lambda i: (i, 0)), BS((1, 1), lambda i: (0, 0)), BS((1, d), lambda i: (0, 0))],
        out_shape=[SDS((s, d), F32), SDS((1, 1), F32), SDS((1, d), F32)], compiler_params=_params())(x, target, gf)


def _in_proj_bwd(l, dres, x, norm1, w_in, du_a, dv1, dv2, du_c, dzg):
    s, d = x.shape
    nc = w_in.shape[-1]
    tm = min(256, s)
    m = MXU_DTYPE

    def body(dres_ref, x_ref, g_ref, w_ref, a_ref, b1_ref, b2_ref, c_ref, g3_ref, dx_ref, dz_ref, dn_ref):
        @pl.when(pl.program_id(0) == 0)
        def _():
            dn_ref[...] = jnp.zeros(dn_ref.shape, F32)

        dz = jnp.concatenate([a_ref[...], b1_ref[...], b2_ref[...], c_ref[...], g3_ref[...]], axis=1).astype(m)
        dz_ref[...] = dz
        dh = _mm_nt_cols(dz, w_ref)
        xv = x_ref[...]
        r = lax.rsqrt(jnp.mean(xv * xv, axis=-1, keepdims=True) + EPS)
        xh = xv * r
        dn_ref[...] += _colsum(dh * xh)
        dxh = dh * g_ref[...]
        dx_ref[...] = dres_ref[...] + r * (dxh - xh * jnp.mean(dxh * xh, axis=-1, keepdims=True))

    tile = lambda n: BS((tm, n), lambda i: (i, 0))
    return pl.pallas_call(
        body, name=f"in_proj_bwd_l{l}", grid=(s // tm,),
        in_specs=[tile(d), tile(d), BS((None, 1, d), lambda i: (l, 0, 0)),
                  BS((N_CHIPS, d, nc), lambda i: (0, 0, 0)),
                  tile(du_a.shape[1]), tile(dv1.shape[1]), tile(dv2.shape[1]), tile(du_c.shape[1]), tile(dzg.shape[1])],
        out_specs=[tile(d), tile(N_CHIPS * nc), BS((1, d), lambda i: (0, 0))],
        out_shape=[SDS((s, d), F32), SDS((s, N_CHIPS * nc), m), SDS((1, d), F32)], compiler_params=_params(),
    )(dres, x, norm1, w_in, du_a, dv1, dv2, du_c, dzg)


def _tn_matmul(name, a, a_spec, b, b_spec, out_shape, out_spec, grid, wire=True):
    last = grid[1] - 1

    def body(a_ref, b_ref, o_ref, *wire_ref):
        @pl.when(pl.program_id(1) == 0)
        def _():
            o_ref[...] = jnp.zeros(o_ref.shape, F32)

        o_ref[...] += _mm_tn(a_ref[...], b_ref[...])

        if wire:
            @pl.when(pl.program_id(1) == last)
            def _():
                wire_ref[0][...] = o_ref[...].astype(WIRE_DTYPE)

    if not wire:
        return pl.pallas_call(body, name=name, grid=grid, in_specs=[a_spec, b_spec], out_specs=out_spec,
                              out_shape=out_shape, compiler_params=_params())(a, b)
    return pl.pallas_call(body, name=name, grid=grid, in_specs=[a_spec, b_spec], out_specs=[out_spec, out_spec],
                          out_shape=[out_shape, SDS(out_shape.shape, WIRE_DTYPE)], compiler_params=_params())(a, b)


def _scan_consts(pw_ref, lanes, reverse):
    sgn = -1.0 if reverse else 1.0
    steps = [(k, pw_ref[2 * i], sgn * pw_ref[2 * i + 1]) for i, k in enumerate((1, 2, 4))]
    c = 4 if reverse else 3
    return steps, pw_ref[2 * c], sgn * pw_ref[2 * c + 1]


def _scan_block(br, bi, steps, row, reverse):
    for k, ar, ai in steps:
        if reverse:
            mask, sh = row < 8 - k, 8 - k
        else:
            mask, sh = row >= k, k
        sr = jnp.where(mask, pltpu.roll(br, sh, 0), 0.0)
        si = jnp.where(mask, pltpu.roll(bi, sh, 0), 0.0)
        br, bi = br + ar * sr - ai * si, bi + ar * si + ai * sr
    return br, bi


def _ssm_fwd(l, z, bblk_re, bblk_im, cblk_re, cblk_im, pw, dskip):
    s = z.shape[0]
    gc = bblk_re.shape[1]
    gl = bblk_re.shape[2]
    nblk = bblk_re.shape[0]

    def body(u_ref, bre, bim, cre, cim, pw_ref, d_ref, hre, him, y_ref):
        u = u_ref[...]
        hre[...] = _mm(u, bre[...])
        him[...] = _mm(u, bim[...])
        row = lax.broadcasted_iota(jnp.int32, (8, gl), 0)
        steps, car, cai = _scan_consts(pw_ref, gl, False)

        def step(i, carry):
            cr, ci = carry
            r0 = pl.multiple_of(i * 8, 8)
            br, bi = _scan_block(hre[pl.ds(r0, 8), :], him[pl.ds(r0, 8), :], steps, row, False)
            hr = br + car * cr - cai * ci
            hi = bi + car * ci + cai * cr
            hre[pl.ds(r0, 8), :] = hr
            him[pl.ds(r0, 8), :] = hi
            return jnp.broadcast_to(hr[7:8, :], (8, gl)), jnp.broadcast_to(hi[7:8, :], (8, gl))

        zero = jnp.zeros((8, gl), F32)
        lax.fori_loop(0, s // 8, step, (zero, zero))
        y_ref[...] = _mm(hre[...], cre[...]) - _mm(him[...], cim[...]) + d_ref[...] * u

    return pl.pallas_call(
        body, name=f"ssm_fwd_l{l}", grid=(nblk,),
        in_specs=[BS((s, gc), lambda k: (0, k)), BS((None, gc, gl), lambda k: (k, 0, 0)),
                  BS((None, gc, gl), lambda k: (k, 0, 0)), BS((None, gl, gc), lambda k: (k, 0, 0)),
                  BS((None, gl, gc), lambda k: (k, 0, 0)), BS((10, 8, gl), lambda k: (0, 0, k)),
                  BS((1, gc), lambda k: (0, k))],
        out_specs=[BS((s, gl), lambda k: (0, k)), BS((s, gl), lambda k: (0, k)), BS((s, gc), lambda k: (0, k))],
        out_shape=[SDS((s, nblk * gl), F32), SDS((s, nblk * gl), F32), SDS((s, nblk * gc), F32)],
        compiler_params=_params())(z, bblk_re, bblk_im, cblk_re, cblk_im, pw, dskip)


def _ssm_bwd(l, dy, z, hre, him, bblk_re, bblk_im, cblk_re, cblk_im, pw, dskip):
    s = z.shape[0]
    nblk, gc, gl = bblk_re.shape

    def body(dy_ref, u_ref, hre_ref, him_ref, bre, bim, cre, cim, pw_ref, d_ref,
             du_ref, dbre_ref, dbim_ref, dcre_ref, dcim_ref, dar_ref, dai_ref, dd_ref, gre, gim):
        dyv = dy_ref[...]
        u = u_ref[...]
        gre[...] = _mm_nt(dyv, cre[...])
        gim[...] = -_mm_nt(dyv, cim[...])
        dcre_ref[...] = _mm_tn(hre_ref[...], dyv)
        dcim_ref[...] = -_mm_tn(him_ref[...], dyv)
        dd_ref[...] = _colsum(dyv * u)
        row = lax.broadcasted_iota(jnp.int32, (8, gl), 0)
        steps, car, cai = _scan_consts(pw_ref, gl, True)
        n8 = s // 8

        def step(ii, carry):
            cr, ci, accr, acci = carry
            i = n8 - 1 - ii
            r0 = pl.multiple_of(i * 8, 8)
            br, bi = _scan_block(gre[pl.ds(r0, 8), :], gim[pl.ds(r0, 8), :], steps, row, True)
            dr = br + car * cr - cai * ci
            di = bi + car * ci + cai * cr
            gre[pl.ds(r0, 8), :] = dr
            gim[pl.ds(r0, 8), :] = di
            rp = pl.multiple_of(jnp.maximum(i - 1, 0) * 8, 8)
            keep = jnp.where(i > 0, 1.0, 0.0)
            pr = jnp.where(row >= 1, pltpu.roll(hre_ref[pl.ds(r0, 8), :], 1, 0),
                           keep * pltpu.roll(hre_ref[pl.ds(rp, 8), :], 1, 0))
            pi = jnp.where(row >= 1, pltpu.roll(him_ref[pl.ds(r0, 8), :], 1, 0),
                           keep * pltpu.roll(him_ref[pl.ds(rp, 8), :], 1, 0))
            accr = accr + dr * pr + di * pi
            acci = acci + di * pr - dr * pi
            return (jnp.broadcast_to(dr[0:1, :], (8, gl)), jnp.broadcast_to(di[0:1, :], (8, gl)), accr, acci)

        zero = jnp.zeros((8, gl), F32)
        _, _, accr, acci = lax.fori_loop(0, n8, step, (zero, zero, zero, zero))
        dar_ref[...] = _colsum(accr)
        dai_ref[...] = _colsum(acci)
        dbr = gre[...]
        dbi = gim[...]
        du_ref[...] = dyv * d_ref[...] + _mm_nt(dbr, bre[...]) + _mm_nt(dbi, bim[...])
        dbre_ref[...] = _mm_tn(u, dbr)
        dbim_ref[...] = _mm_tn(u, dbi)

    col = lambda n: BS((s, n), lambda k: (0, k))
    blk = lambda a, b: BS((None, a, b), lambda k: (k, 0, 0))
    outs = pl.pallas_call(
        body, name=f"ssm_bwd_l{l}", grid=(nblk,),
        in_specs=[col(gc), col(gc), col(gl), col(gl), blk(gc, gl), blk(gc, gl), blk(gl, gc), blk(gl, gc),
                  BS((10, 8, gl), lambda k: (0, 0, k)), BS((1, gc), lambda k: (0, k))],
        out_specs=[col(gc), blk(gc, gl), blk(gc, gl), blk(gl, gc), blk(gl, gc), BS((1, gl), lambda k: (0, k)),
                   BS((1, gl), lambda k: (0, k)), BS((1, gc), lambda k: (0, k))],
        out_shape=[SDS((s, nblk * gc), F32), SDS((nblk, gc, gl), F32), SDS((nblk, gc, gl), F32),
                   SDS((nblk, gl, gc), F32), SDS((nblk, gl, gc), F32), SDS((1, nblk * gl), F32),
                   SDS((1, nblk * gl), F32), SDS((1, nblk * gc), F32)],
        scratch_shapes=[pltpu.VMEM((s, gl), F32), pltpu.VMEM((s, gl), F32)], compiler_params=_params(),
    )(dy, z, hre, him, bblk_re, bblk_im, cblk_re, cblk_im, pw, dskip)
    return dict(zip(("du", "dbblk_re", "dbblk_im", "dcblk_re", "dcblk_im", "dabar_re", "dabar_im", "dd"), outs))


def _conv_fwd(l, z, wdw, bdw):
    s = z.shape[0]
    cw = wdw.shape[1]
    lb = 128
    tr = min(256, s)
    off1 = cw // lb
    off2 = 2 * cw // lb

    def body(v1_ref, v2_ref, w_ref, b_ref, hc_ref, scr):
        scr[0:CONV_PAD, :] = jnp.zeros((CONV_PAD, lb), F32)
        scr[CONV_PAD:, :] = v1_ref[...] * _sigmoid(v2_ref[...])
        for t in range(s // tr):
            acc = jnp.broadcast_to(b_ref[...], (tr, lb))
            for k in range(CONV_KERNEL):
                acc = acc + w_ref[pl.ds(k, 1), :] * scr[pl.ds(t * tr + CONV_PAD - (CONV_KERNEL - 1) + k, tr), :]
            hc_ref[pl.ds(t * tr, tr), :] = acc

    return pl.pallas_call(
        body, name=f"conv_fwd_l{l}", grid=(cw // lb,),
        in_specs=[BS((s, lb), lambda k: (0, off1 + k)), BS((s, lb), lambda k: (0, off2 + k)),
                  BS((CONV_KERNEL, lb), lambda k: (0, k)), BS((1, lb), lambda k: (0, k))],
        out_specs=BS((s, lb), lambda k: (0, k)), out_shape=SDS((s, cw), F32),
        scratch_shapes=[pltpu.VMEM((s + CONV_PAD, lb), F32)], compiler_params=_params())(z, z, wdw, bdw)


def _conv_bwd(l, dhc, z, wdw):
    s = z.shape[0]
    cw = wdw.shape[1]
    lb = 128
    tr = min(256, s)
    off1 = cw // lb
    off2 = 2 * cw // lb
    nb = cw // lb

    def body(d_ref, v1_ref, v2_ref, w_ref, dv1_ref, dv2_ref, dw_ref, db_ref, hpad, dpad):
        v1 = v1_ref[...]
        sg = _sigmoid(v2_ref[...])
        dv = d_ref[...]
        hpad[0:CONV_PAD, :] = jnp.zeros((CONV_PAD, lb), F32)
        hpad[CONV_PAD:, :] = v1 * sg
        dpad[0:s, :] = dv
        dpad[s:, :] = jnp.zeros((CONV_PAD, lb), F32)
        db_ref[...] = _colsum(dv)
        dws = [jnp.zeros((1, lb), F32) for _ in range(CONV_KERNEL)]
        for t in range(s // tr):
            dt = d_ref[pl.ds(t * tr, tr), :]
            acc = jnp.zeros((tr, lb), F32)
            for k in range(CONV_KERNEL):
                acc = acc + w_ref[pl.ds(k, 1), :] * dpad[pl.ds(t * tr + (CONV_KERNEL - 1) - k, tr), :]
                dws[k] = dws[k] + _colsum(dt * hpad[pl.ds(t * tr + CONV_PAD - (CONV_KERNEL - 1) + k, tr), :])
            sgt = _sigmoid(v2_ref[pl.ds(t * tr, tr), :])
            v1t = v1_ref[pl.ds(t * tr, tr), :]
            dv1_ref[pl.ds(t * tr, tr), :] = acc * sgt
            dv2_ref[pl.ds(t * tr, tr), :] = acc * v1t * (sgt * (1.0 - sgt))
        for k in range(CONV_KERNEL):
            dw_ref[pl.ds(k, 1), :] = dws[k]

    return pl.pallas_call(
        body, name=f"conv_bwd_l{l}", grid=(nb,),
        in_specs=[BS((s, lb), lambda k: (0, k)), BS((s, lb), lambda k: (0, off1 + k)),
                  BS((s, lb), lambda k: (0, off2 + k)), BS((CONV_KERNEL, lb), lambda k: (0, k))],
        out_specs=[BS((s, lb), lambda k: (0, k)), BS((s, lb), lambda k: (0, k)),
                   BS((CONV_KERNEL, lb), lambda k: (0, k)), BS((1, lb), lambda k: (0, k))],
        out_shape=[SDS((s, cw), F32), SDS((s, cw), F32), SDS((CONV_KERNEL, cw), F32), SDS((1, cw), F32)],
        scratch_shapes=[pltpu.VMEM((s + CONV_PAD, lb), F32), pltpu.VMEM((s + CONV_PAD, lb), F32)],
        compiler_params=_params())(dhc, z, z, wdw)


def _pool_window(k):
    return jnp.where(k == 0, float(POOL_WINDOWS[0]),
                     jnp.where(k == 1, float(POOL_WINDOWS[1]),
                               jnp.where(k == 2, float(POOL_WINDOWS[2]), float(POOL_WINDOWS[3]))))


def _pool_fwd(l, z, pw_width):
    s = z.shape[0]
    lb = pw_width // len(POOL_WINDOWS)
    off = 3 * pw_width // lb

    def body(u_ref, p_ref):
        k = pl.program_id(0)
        u = u_ref[...]
        row = lax.broadcasted_iota(jnp.int32, (s, lb), 0)
        sums = [u]
        for sh in (1, 2, 4, 8):
            prev = sums[-1]
            sums.append(prev + jnp.where(row >= sh, pltpu.roll(prev, sh, 0), 0.0))
        sel = jnp.where(k == 0, sums[1], jnp.where(k == 1, sums[2], jnp.where(k == 2, sums[3], sums[4])))
        cnt = jnp.minimum((row + 1).astype(F32), _pool_window(k))
        p_ref[...] = sel / cnt - u

    return pl.pallas_call(
        body, name=f"pool_fwd_l{l}", grid=(len(POOL_WINDOWS),),
        in_specs=[BS((s, lb), lambda k: (0, off + k))], out_specs=BS((s, lb), lambda k: (0, k)),
        out_shape=SDS((s, pw_width), F32), compiler_params=_params())(z)


def _pool_bwd(l, dp):
    s, width = dp.shape
    lb = width // len(POOL_WINDOWS)

    def body(d_ref, du_ref):
        k = pl.program_id(0)
        dv = d_ref[...]
        row = lax.broadcasted_iota(jnp.int32, (s, lb), 0)
        cnt = jnp.minimum((row + 1).astype(F32), _pool_window(k))
        sums = [dv / cnt]
        for sh in (1, 2, 4, 8):
            prev = sums[-1]
            sums.append(prev + jnp.where(row < s - sh, pltpu.roll(prev, s - sh, 0), 0.0))
        sel = jnp.where(k == 0, sums[1], jnp.where(k == 1, sums[2], jnp.where(k == 2, sums[3], sums[4])))
        du_ref[...] = sel - dv

    return pl.pallas_call(
        body, name=f"pool_bwd_l{l}", grid=(len(POOL_WINDOWS),),
        in_specs=[BS((s, lb), lambda k: (0, k))], out_specs=BS((s, lb), lambda k: (0, k)),
        out_shape=SDS((s, width), F32), compiler_params=_params())(dp)


def _zoh(a_re, a_im, log_dt):
    dt = jnp.exp(log_dt)
    mag = jnp.exp(dt * a_re)
    ang = dt * a_im
    abar_re = mag * jnp.cos(ang)
    abar_im = mag * jnp.sin(ang)
    den = a_re * a_re + a_im * a_im
    nr = abar_re - 1.0
    ni = abar_im
    f_re = (nr * a_re + ni * a_im) / den
    f_im = (ni * a_re - nr * a_im) / den
    return abar_re, abar_im, f_re, f_im


def _zoh_fwd(l, a_re, a_im, log_dt):
    def body(ar, ai, ld, o0, o1, o2, o3):
        for ref, val in zip((o0, o1, o2, o3), _zoh(ar[...], ai[...], ld[...])):
            ref[...] = val

    return pl.pallas_call(body, name=f"zoh_fwd_l{l}", out_shape=[SDS(a_re.shape, F32)] * 4)(a_re, a_im, log_dt)


def _zoh_bwd(l, a_re, a_im, log_dt, cts):
    def body(ar, ai, ld, c0, c1, c2, c3, dar, dai, dld):
        _, vjp = jax.vjp(_zoh, ar[...], ai[...], ld[...])
        g = vjp((c0[...], c1[...], c2[...], c3[...]))
        dar[...] = g[0]
        dai[...] = g[1]
        dld[...] = g[2]

    return pl.pallas_call(body, name=f"zoh_bwd_l{l}",
                          out_shape=[SDS(a_re.shape, F32), SDS(a_re.shape, F32), SDS(log_dt.shape, F32)],
                          )(a_re, a_im, log_dt, *cts)


def _bbar_fwd(l, f_re, f_im, b_re, b_im):
    def body(fr, fi, br, bi, o_re, o_im):
        o_re[...] = fr[...] * br[...] - fi[...] * bi[...]
        o_im[...] = fr[...] * bi[...] + fi[...] * br[...]

    return pl.pallas_call(body, name=f"bbar_fwd_l{l}", out_shape=[SDS(b_re.shape, F32)] * 2)(f_re, f_im, b_re, b_im)


def _bbar_bwd(l, f_re, f_im, b_re, b_im, d_re, d_im):
    def body(fr, fi, br, bi, dr, di, dfr, dfi, dbr, dbi):
        dfr[...] = jnp.sum(dr[...] * br[...] + di[...] * bi[...], axis=1, keepdims=True)
        dfi[...] = jnp.sum(di[...] * br[...] - dr[...] * bi[...], axis=1, keepdims=True)
        dbr[...] = fr[...] * dr[...] + fi[...] * di[...]
        dbi[...] = fr[...] * di[...] - fi[...] * dr[...]

    return pl.pallas_call(body, name=f"bbar_bwd_l{l}",
                          out_shape=[SDS(f_re.shape, F32), SDS(f_re.shape, F32), SDS(b_re.shape, F32),
                                     SDS(b_re.shape, F32)])(f_re, f_im, b_re, b_im, d_re, d_im)


def _powers(l, abar_re, abar_im):
    lanes = abar_re.shape[1]

    def body(ar_ref, ai_ref, o_ref):
        ar, ai = ar_ref[...], ai_ref[...]
        pows = [(ar, ai)]
        for _ in range(7):
            pr, pi = pows[-1]
            pows.append((pr * ar - pi * ai, pr * ai + pi * ar))
        row = lax.broadcasted_iota(jnp.int32, (8, lanes), 0)
        for i, k in enumerate((1, 2, 4)):
            o_ref[2 * i] = jnp.broadcast_to(pows[k - 1][0], (8, lanes))
            o_ref[2 * i + 1] = jnp.broadcast_to(pows[k - 1][1], (8, lanes))
        for slot, order in ((3, range(8)), (4, range(7, -1, -1))):
            vr = jnp.zeros((8, lanes), F32)
            vi = jnp.zeros((8, lanes), F32)
            for r, e in enumerate(order):
                vr = jnp.where(row == r, pows[e][0], vr)
                vi = jnp.where(row == r, pows[e][1], vi)
            o_ref[2 * slot] = vr
            o_ref[2 * slot + 1] = vi

    return pl.pallas_call(body, name=f"powers_l{l}", out_shape=SDS((10, 8, lanes), F32))(abar_re, abar_im)


def _block_diag(v, rows_first):
    g, a, b = v.shape
    eye = jnp.eye(8, dtype=v.dtype)
    out = jnp.einsum("kgab,gh->kgahb", v.reshape(g // 8, 8, a, b), eye)
    return out.reshape(g // 8, 8 * a, 8 * b)


def _block_diag_extract(blk, a, b):
    n = blk.shape[0]
    v = blk.reshape(n, 8, a, 8, b)
    return jnp.einsum("kgahb,gh->kgab", v, jnp.eye(8, dtype=blk.dtype)).reshape(n * 8, a, b)


def _ssm_prepare(l, prm):
    g, n, p = SSM_GROUPS, SSM_STATE, SSM_GROUP
    a_re, a_im = prm["ssm_a_re"][l], prm["ssm_a_im"][l]
    log_dt = prm["ssm_log_dt"][l].reshape(g, 1)
    abar_re, abar_im, f_re, f_im = _zoh_fwd(l, a_re, a_im, log_dt)
    b_re = prm["ssm_b_re"][l].reshape(g * n, p)
    b_im = prm["ssm_b_im"][l].reshape(g * n, p)
    fcol_re, fcol_im = f_re.reshape(g * n, 1), f_im.reshape(g * n, 1)
    bbar_re, bbar_im = _bbar_fwd(l, fcol_re, fcol_im, b_re, b_im)
    bblk_re = _block_diag(bbar_re.reshape(g, n, p).transpose(0, 2, 1), True).astype(MXU_DTYPE)
    bblk_im = _block_diag(bbar_im.reshape(g, n, p).transpose(0, 2, 1), True).astype(MXU_DTYPE)
    cblk_re = _block_diag(prm["ssm_c_re"][l].transpose(0, 2, 1), False).astype(MXU_DTYPE)
    cblk_im = _block_diag(prm["ssm_c_im"][l].transpose(0, 2, 1), False).astype(MXU_DTYPE)
    pw = _powers(l, abar_re.reshape(1, g * n), abar_im.reshape(1, g * n))
    return dict(a_re=a_re, a_im=a_im, log_dt=log_dt, b_re=b_re, b_im=b_im, fcol_re=fcol_re, fcol_im=fcol_im,
                bblk_re=bblk_re, bblk_im=bblk_im, cblk_re=cblk_re, cblk_im=cblk_im, pw=pw,
                dskip=prm["ssm_d"][l].reshape(1, g * p))


def _ssm_param_grads(l, sd, r):
    g, n, p = SSM_GROUPS, SSM_STATE, SSM_GROUP
    dbbar_re = _block_diag_extract(r["dbblk_re"], p, n).transpose(0, 2, 1).reshape(g * n, p)
    dbbar_im = _block_diag_extract(r["dbblk_im"], p, n).transpose(0, 2, 1).reshape(g * n, p)
    dfr, dfi, db_re, db_im = _bbar_bwd(l, sd["fcol_re"], sd["fcol_im"], sd["b_re"], sd["b_im"], dbbar_re, dbbar_im)
    cts = (r["dabar_re"].reshape(g, n), r["dabar_im"].reshape(g, n), dfr.reshape(g, n), dfi.reshape(g, n))
    da_re, da_im, dlog_dt = _zoh_bwd(l, sd["a_re"], sd["a_im"], sd["log_dt"], cts)
    dc_re = _block_diag_extract(r["dcblk_re"], n, p).transpose(0, 2, 1)
    dc_im = _block_diag_extract(r["dcblk_im"], n, p).transpose(0, 2, 1)
    return dict(ssm_a_re=da_re, ssm_a_im=da_im, ssm_log_dt=dlog_dt.reshape(g), ssm_b_re=db_re.reshape(g, n, p),
                ssm_b_im=db_im.reshape(g, n, p), ssm_c_re=dc_re, ssm_c_im=dc_im, ssm_d=r["dd"].reshape(g, p))


def _ffn_weight_grads(l, fb, dx2, s):
    d = dx2.shape[1]
    ts = min(512, s)
    ns = s // ts
    hcn = fb["act"].shape[-1]
    full = lambda n: BS((ts, n), lambda j, t: (t, 0))
    g = {}
    chunk = BS((None, ts, hcn), lambda j, t: (j, t, 0))
    for name, key, rhs in (("ffn_w_gate", "dgate", fb["h2"]), ("ffn_w_up", "dup", fb["h2"]), ("ffn_w_down", "act", dx2)):
        g[name] = _tn_matmul(f"d{name}_l{l}", fb[key], chunk, rhs, full(d), SDS((N_CHIPS, hcn, d), F32),
                             BS((None, hcn, d), lambda j, t: (j, 0, 0)), (N_CHIPS, ns))
    return g


def _mixer_weight_grads(l, sv, mb, dx1, dz, s):
    d = dx1.shape[1]
    cw = sv["y"].shape[1]
    ts = min(512, s)
    ns = s // ts
    ncw = dz.shape[1] // N_CHIPS
    pc = d // N_CHIPS
    gw = cw // len(POOL_WINDOWS)
    full = lambda n: BS((ts, n), lambda j, t: (t, 0))
    g = {}
    g["w_in"] = _tn_matmul(f"dw_in_l{l}", sv["h"], full(d), dz, BS((ts, ncw), lambda j, t: (t, j)),
                           SDS((N_CHIPS, d, ncw), F32), BS((None, d, ncw), lambda j, t: (j, 0, 0)), (N_CHIPS, ns))
    g["w_out"] = _tn_matmul(f"dw_out_l{l}", mb["merged"], BS((ts, pc), lambda j, t: (t, j)), dx1, full(d),
                            SDS((N_CHIPS, pc, d), F32), BS((None, pc, d), lambda j, t: (j, 0, 0)), (N_CHIPS, ns))
    for name, a, b in (("ssm_w_proj", "sa", "dya"), ("conv_w_proj", "ac", "dyb"), ("pool_w_proj", "pp", "dyc")):
        g[name] = _tn_matmul(f"d{name}_l{l}", mb[a], full(cw), mb[b], BS((ts, pc), lambda j, t: (t, j)),
                             SDS((N_CHIPS, cw, pc), F32), BS((None, cw, pc), lambda j, t: (j, 0, 0)), (N_CHIPS, ns))
    gq = cw // N_CHIPS
    g["ssm_w_glu"] = _tn_matmul(f"dssm_w_glu_l{l}", mb["ge"], BS((ts, gq), lambda j, t: (t, j)), mb["dt"], full(cw),
                                SDS((N_CHIPS, gq, cw), F32), BS((None, gq, cw), lambda j, t: (j, 0, 0)), (N_CHIPS, ns))
    dwgrp = _tn_matmul(f"dpool_w_group_l{l}", sv["p"], BS((ts, gw), lambda j, t: (t, j)), mb["dq"],
                       BS((ts, gw), lambda j, t: (t, j)), SDS((len(POOL_WINDOWS), gw, gw), F32),
                       BS((None, gw, gw), lambda j, t: (j, 0, 0)), (len(POOL_WINDOWS), ns), wire=False)
    return g, dwgrp


def _local_step(x, target, weights_of, prm, on_grads=None):
    s, d = x.shape
    cw = prm["ssm_b_glu"].shape[1]
    sp = {k: prm[k].reshape(N_LAYERS, 1, -1) for k in ("norm1", "norm2", "b_gate", "ssm_b_glu", "conv_ln_g", "conv_ln_b",
                                                        "pool_scale", "conv_b_dw")}
    sp["pool_w_group"] = prm["pool_w_group"]
    saved = []
    xin = x
    for l in range(N_LAYERS):
        fw = weights_of(l, "in", (xin,))
        sd = _ssm_prepare(l, prm)
        z, h = _in_proj(l, xin, sp["norm1"], fw["w_in"])
        hre, him, y = _ssm_fwd(l, z, sd["bblk_re"], sd["bblk_im"], sd["cblk_re"], sd["cblk_im"], sd["pw"], sd["dskip"])
        p = _pool_fwd(l, z, cw)
        fw.update(weights_of(l, "mixer", (y, p)))
        wdw = fw["conv_w_dw"]
        hc = _conv_fwd(l, z, wdw, sp["conv_b_dw"][l])
        x1 = _merge_fwd(l, xin, y, hc, p, z, fw, sp)
        fw.update(weights_of(l, "ffn", (x1,)))
        x2 = _ffn_fwd(l, x1, sp["norm2"], fw["ffn_w_gate"], fw["ffn_w_up"], fw["ffn_w_down"])
        saved.append(dict(x=xin, z=z, h=h, hre=hre, him=him, y=y, hc=hc, p=p, x1=x1, sd=sd, wdw=wdw, fw=fw))
        xin = x2
    dx, loss, dfinal = _loss_head(xin, target, prm["final_norm"].reshape(1, d))
    big = [None] * N_LAYERS
    small = [None] * N_LAYERS
    norm2_rows = sp["norm2"]
    for l in reversed(range(N_LAYERS)):
        sv = saved[l]
        sd, fw = sv["sd"], sv["fw"]
        fb = _ffn_bwd(l, sv["x1"], dx, norm2_rows, fw["ffn_w_gate"], fw["ffn_w_up"], fw["ffn_w_down"])
        big[l] = _ffn_weight_grads(l, fb, dx, s)
        spl = sp
        if on_grads is not None:
            spl = dict(sp, ssm_b_glu=sp["ssm_b_glu"] + on_grads(l, "ffn", big[l]))
        mb = _merge_bwd(l, fb["dx1"], sv["y"], sv["hc"], sv["p"], sv["z"], fw, spl)
        du_c = _pool_bwd(l, mb["dp"])
        dv1, dv2, dwdw, dbdw = _conv_bwd(l, mb["dhc"], sv["z"], sv["wdw"])
        sr = _ssm_bwd(l, mb["dy"], sv["z"], sv["hre"], sv["him"], sd["bblk_re"], sd["bblk_im"], sd["cblk_re"],
                      sd["cblk_im"], sd["pw"], sd["dskip"])
        dx, dz, dnorm1 = _in_proj_bwd(l, fb["dx1"], sv["x"], sp["norm1"], fw["w_in"], sr["du"], dv1, dv2, du_c, mb["dzg"])
        mixer, dwgrp = _mixer_weight_grads(l, sv, mb, fb["dx1"], dz, s)
        big[l].update(mixer)
        if on_grads is not None:
            norm2_rows = sp["norm2"] + on_grads(l, "mixer", mixer)
        sg = _ssm_param_grads(l, sd, sr)
        sg.update(norm1=dnorm1.reshape(d), b_gate=mb["db_gate"].reshape(3 * d), ssm_b_glu=mb["db_glu"].reshape(cw),
                  conv_b_dw=dbdw.reshape(cw), conv_ln_g=mb["dln_g"].reshape(cw), conv_ln_b=mb["dln_b"].reshape(cw),
                  pool_w_group=dwgrp, pool_scale=mb["dscale"].reshape(cw), norm2=fb["dnorm2"].reshape(d),
                  conv_w_dw=dwdw)
        small[l] = sg
    return loss[0, 0], dx, big, small, dfinal.reshape(d)


def _place():
    return lax.axis_index("x"), lax.axis_index("y"), lax.axis_index("c")


def _other_chips(x, y):
    return [(1 - x, y), (x, 1 - y), (1 - x, 1 - y)]


def _remote(src, dst, send_sem, recv_sem, device):
    return pltpu.make_async_remote_copy(src_ref=src, dst_ref=dst, send_sem=send_sem, recv_sem=recv_sem,
                                        device_id=device, device_id_type=MESH)


def _hbm(v):
    return pltpu.with_memory_space_constraint(v, pltpu.HBM)


def _cast_into(name, w, place, dtype):
    nl, k, n = w.shape
    tr = _row_tile(k, n)
    nt = k // tr

    def body(place_ref, w_ref, o0_ref, o1_ref):
        @pl.when(pl.program_id(0) == 0)
        def _():
            o0_ref[...] = w_ref[...].astype(dtype)

        @pl.when(pl.program_id(0) == 1)
        def _():
            o1_ref[...] = w_ref[...].astype(dtype)

    return pl.pallas_call(
        body, name=f"cast_{name}",
        grid_spec=pltpu.PrefetchScalarGridSpec(
            num_scalar_prefetch=1, grid=(nl, nt),
            in_specs=[BS((None, tr, n), lambda l, t, pr: (l, t, 0))],
            out_specs=[BS((None, tr, n), lambda l, t, pr: (pr[0], t * (1 - l) + (nt - 1) * l, 0)),
                       BS((None, tr, n), lambda l, t, pr: (pr[0], t * l, 0))]),
        out_shape=[SDS((N_CHIPS, k, n), dtype)] * 2)(place, w)


def _gather_rows(buf, c):
    k = buf.shape[1]
    if k % 2:
        return pl.ds(0, k)
    return pl.ds(pl.multiple_of(c * (k // 2), 8), k // 2)


def _allgather_start(groups):
    ng = len(groups)
    sizes = [len(g) for g in groups]
    first = [sum(sizes[:g]) for g in range(ng)]
    flat = [b for g in groups for b in g]
    nb = len(flat)

    def body(*refs):
        ins = refs[:nb]
        sems = refs[nb:nb + 2 * ng]
        x, y, c = _place()
        jme = 2 * x + y
        for g in range(ng):
            for a in range(sizes[g]):
                buf = ins[first[g] + a]
                blk = buf.at[jme, _gather_rows(buf, c)]
                for k, (cx, cy) in enumerate(_other_chips(x, y)):
                    _remote(blk, blk, sems[2 * g].at[3 * a + k], sems[2 * g + 1].at[3 * a + k], (cx, cy, c)).start()

    sem_shapes = [pltpu.SemaphoreType.DMA((3 * sizes[g // 2],)) for g in range(2 * ng)]
    outs = pl.pallas_call(
        body, name="allgather_start", in_specs=[HBM] * nb, out_specs=[SEM] * (2 * ng) + [HBM] * nb,
        out_shape=sem_shapes + [pltpu.HBM(b.shape, b.dtype) for b in flat],
        input_output_aliases={i: 2 * ng + i for i in range(nb)},
        compiler_params=pltpu.CompilerParams(has_side_effects=SIDE_EFFECT))(*[_hbm(b) for b in flat])
    return [(outs[2 * g], outs[2 * g + 1], outs[2 * ng + first[g]:2 * ng + first[g] + sizes[g]]) for g in range(ng)]


def _allgather_wait(l, send_sems, recv_sems, bufs, after):
    n = len(bufs)

    def body(*refs):
        ins = refs[:n]
        ssem, rsem = refs[n], refs[n + 1]
        x, y, c = _place()
        jme = 2 * x + y
        for a in range(n):
            rows = _gather_rows(ins[a], c)
            for k, (cx, cy) in enumerate(_other_chips(x, y)):
                cp = _remote(ins[a].at[jme, rows], ins[a].at[2 * cx + cy, rows], ssem.at[3 * a + k],
                             rsem.at[3 * a + k], (cx, cy, c))
                cp.wait_send()
                cp.wait_recv()

    return pl.pallas_call(
        body, name=f"allgather_wait_{l}", in_specs=[HBM] * n + [SEM, SEM] + [ANY] * len(after), out_specs=[HBM] * n,
        out_shape=[pltpu.HBM(b.shape, b.dtype) for b in bufs], input_output_aliases={i: i for i in range(n)},
        compiler_params=pltpu.CompilerParams(has_side_effects=SIDE_EFFECT))(*bufs, send_sems, recv_sems, *after)


def _allgather_forward(l, bufs):
    n = len(bufs)
    split = [a for a in range(n) if bufs[a].shape[1] % 2 == 0]

    def body(*refs):
        ins = refs[:n]
        send_sems, recv_sems = refs[2 * n:]
        x, y, c = _place()
        sibling = (x, y, 1 - c)
        copies = []
        for a in split:
            for k, (cx, cy) in enumerate(_other_chips(x, y)):
                blk = ins[a].at[2 * cx + cy, _gather_rows(ins[a], c)]
                cp = _remote(blk, blk, send_sems.at[a, k], recv_sems.at[a, k], sibling)
                cp.start()
                copies.append(cp)
        for a in split:
            for k, (cx, cy) in enumerate(_other_chips(x, y)):
                blk = ins[a].at[2 * cx + cy, _gather_rows(ins[a], 1 - c)]
                _remote(blk, blk, send_sems.at[a, k], recv_sems.at[a, k], sibling).wait_recv()
        for cp in copies:
            cp.wait_send()

    sem = pltpu.SemaphoreType.DMA((n, 3))
    return pl.pallas_call(
        body, name=f"allgather_forward_{l}", in_specs=[ANY] * n, out_specs=[ANY] * n,
        out_shape=[SDS(b.shape, b.dtype) for b in bufs], input_output_aliases={i: i for i in range(n)},
        scratch_shapes=[sem, sem])(*bufs)


def _rs_to_owner(l, parts):
    n = len(parts)
    lands = [lax.empty((3,) + p.shape[1:], p.dtype) for p in parts]

    def body(*refs):
        ins, zones = refs[:n], refs[n:2 * n]
        send_sems, recv_sems = refs[2 * n], refs[2 * n + 1]
        token = refs[-1]
        x, y, c = _place()
        for a in range(n):
            for k, (cx, cy) in enumerate(_other_chips(x, y)):
                _remote(ins[a].at[2 * cx + cy], zones[a].at[k], send_sems.at[3 * a + k], recv_sems.at[3 * a + k],
                        (cx, cy, c)).start()
        token[...] = jnp.zeros(token.shape, F32)

    sem = pltpu.SemaphoreType.DMA((3 * n,))
    outs = pl.pallas_call(
        body, name=f"rs_to_owner_start_{l}", in_specs=[HBM] * (2 * n),
        out_specs=[SEM, SEM] + [HBM] * (2 * n) + [pl.BlockSpec(memory_space=pltpu.VMEM)],
        out_shape=[sem, sem] + [pltpu.HBM(p.shape, p.dtype) for p in parts]
        + [pltpu.HBM(z.shape, z.dtype) for z in lands] + [SDS((8, 128), F32)],
        input_output_aliases={i: 2 + i for i in range(2 * n)},
        compiler_params=pltpu.CompilerParams(has_side_effects=SIDE_EFFECT),
    )(*[_hbm(p) for p in parts], *[_hbm(z) for z in lands])
    return outs[0], outs[1], outs[2:2 + n], outs[2 + n:2 + 2 * n], outs[-1]


def _rs_to_owner_wait(l, send_sems, recv_sems, parts, lands, after):
    n = len(parts)

    def body(*refs):
        ins, zones = refs[:n], refs[n:2 * n]
        ssem, rsem = refs[2 * n], refs[2 * n + 1]
        x, y, c = _place()
        for a in range(n):
            for k, (cx, cy) in enumerate(_other_chips(x, y)):
                cp = _remote(ins[a].at[2 * cx + cy], zones[a].at[k], ssem.at[3 * a + k], rsem.at[3 * a + k],
                             (cx, cy, c))
                cp.wait_send()
                cp.wait_recv()

    outs = pl.pallas_call(
        body, name=f"rs_to_owner_wait_{l}", in_specs=[HBM] * (2 * n) + [SEM, SEM, ANY], out_specs=[HBM] * (2 * n),
        out_shape=[pltpu.HBM(p.shape, p.dtype) for p in parts] + [pltpu.HBM(z.shape, z.dtype) for z in lands],
        input_output_aliases={i: i for i in range(2 * n)},
        compiler_params=pltpu.CompilerParams(has_side_effects=SIDE_EFFECT),
    )(*parts, *lands, send_sems, recv_sems, after)
    return outs[:n], outs[n:]


def _rs_sibling_exchange(l, both):
    n = len(both)

    def body(*refs):
        ins = refs[:n]
        send_sems, recv_sems = refs[2 * n:]
        x, y, c = _place()
        copies = []
        for a in range(n):
            cp = _remote(ins[a].at[c], ins[a].at[c], send_sems.at[a], recv_sems.at[a], (x, y, 1 - c))
            cp.start()
            copies.append(cp)
        for a, cp in enumerate(copies):
            cp.wait_send()
            _remote(ins[a].at[1 - c], ins[a].at[1 - c], send_sems.at[a], recv_sems.at[a], (x, y, 1 - c)).wait_recv()

    sem = pltpu.SemaphoreType.DMA((n,))
    return pl.pallas_call(
        body, name=f"rs_sibling_exchange_{l}", in_specs=[ANY] * n, out_specs=[ANY] * n,
        out_shape=[SDS(b.shape, b.dtype) for b in both], input_output_aliases={i: i for i in range(n)},
        scratch_shapes=[sem, sem])(*both)


def _add_owner(name, grad, recv, place):
    _, r, cols = grad.shape
    tr = _row_tile(r, cols, budget=1024 * 1024)
    nt = r // tr

    def body(place_ref, g_ref, r_ref, o_ref):
        acc = ((g_ref[...] + r_ref[0].astype(F32)) + r_ref[1].astype(F32)) + r_ref[2].astype(F32)
        o_ref[...] = acc.astype(o_ref.dtype)

    return pl.pallas_call(
        body, name=name,
        grid_spec=pltpu.PrefetchScalarGridSpec(
            num_scalar_prefetch=1, grid=(nt,),
            in_specs=[BS((None, tr, cols), lambda t, pr: (pr[0], t, 0)), BS((3, tr, cols), lambda t, pr: (0, t, 0))],
            out_specs=BS((None, tr, cols), lambda t, pr: (pr[1], t, 0))),
        out_shape=SDS((2, r, cols), WIRE_DTYPE))(place, grad, recv)


def _reduce_start(tag, grads):
    names = list(grads)
    send_sems, recv_sems, wires, lands, token = _rs_to_owner(tag, [grads[n][1] for n in names])
    return dict(tag=tag, names=names, send_sems=send_sems, recv_sems=recv_sems, wires=wires, lands=lands,
                grads=[grads[n][0] for n in names]), token


def _reduce_finish(pending, place, after):
    tag, names = pending["tag"], pending["names"]
    _, lands = _rs_to_owner_wait(tag, pending["send_sems"], pending["recv_sems"], pending["wires"],
                                 pending["lands"], after)
    mine = [_add_owner(f"rs_add_owner_{n}_{tag}", g, r, place) for n, g, r in zip(names, pending["grads"], lands)]
    return dict(zip(names, _rs_sibling_exchange(tag, mine)))


def _small_peers(x, y, c):
    return [(x, y, 1 - c)] + [(cx, cy, c) for cx, cy in _other_chips(x, y)]


def _allgather_rows_start(buf):
    land = lax.empty((8,) + buf.shape, buf.dtype)

    def body(x_ref, out_ref, send_sems, recv_sems, x_thru, out_thru):
        x, y, c = _place()
        for i, peer in enumerate(_small_peers(x, y, c)):
            _remote(x_ref, out_ref.at[4 * x + 2 * y + c], send_sems.at[i], recv_sems.at[i], peer).start()

    sem = pltpu.SemaphoreType.DMA((4,))
    return pl.pallas_call(
        body, name="allgather_small_start", in_specs=[HBM, HBM], out_specs=[SEM, SEM, HBM, HBM],
        out_shape=[sem, sem, pltpu.HBM(buf.shape, buf.dtype), pltpu.HBM(land.shape, land.dtype)],
        input_output_aliases={0: 2, 1: 3}, compiler_params=pltpu.CompilerParams(has_side_effects=SIDE_EFFECT),
    )(_hbm(buf), _hbm(land))


def _allgather_rows_wait(send_sems, recv_sems, buf, land, after):
    def body(x_ref, out_ref, ssem, rsem, after_ref, x_thru, out_thru):
        x, y, c = _place()
        for i, (px, py, pc) in enumerate(_small_peers(x, y, c)):
            cp = _remote(x_ref, out_ref.at[4 * px + 2 * py + pc], ssem.at[i], rsem.at[i], (px, py, pc))
            cp.wait_send()
            cp.wait_recv()

    return pl.pallas_call(
        body, name="allgather_small_wait", in_specs=[HBM, HBM, SEM, SEM, ANY], out_specs=[HBM, HBM],
        out_shape=[pltpu.HBM(buf.shape, buf.dtype), pltpu.HBM(land.shape, land.dtype)],
        input_output_aliases={0: 0, 1: 1}, compiler_params=pltpu.CompilerParams(has_side_effects=SIDE_EFFECT),
    )(buf, land, send_sems, recv_sems, after)


def _allgather_rows_forward(land):
    def body(in_ref, out_ref, send_sems, recv_sems):
        x, y, c = _place()
        sibling = (x, y, 1 - c)
        copies = []
        for k, (cx, cy) in enumerate(_other_chips(x, y)):
            blk = in_ref.at[4 * cx + 2 * cy + c]
            cp = _remote(blk, blk, send_sems.at[k], recv_sems.at[k], sibling)
            cp.start()
            copies.append(cp)
        for k, (cx, cy) in enumerate(_other_chips(x, y)):
            blk = in_ref.at[4 * cx + 2 * cy + 1 - c]
            _remote(blk, blk, send_sems.at[k], recv_sems.at[k], sibling).wait_recv()
        for cp in copies:
            cp.wait_send()

    sem = pltpu.SemaphoreType.DMA((3,))
    return pl.pallas_call(body, name="allgather_small_forward", in_specs=[ANY], out_specs=ANY,
                          out_shape=SDS(land.shape, land.dtype), input_output_aliases={0: 0},
                          scratch_shapes=[sem, sem])(land)


def _sum_devices(gathered, mine, place):
    _, r, cols = gathered.shape
    tr = _row_tile(r, cols, budget=256 * 1024)

    def body(place_ref, g_ref, x_ref, o_ref):
        me = 2 * place_ref[0] + place_ref[1]
        acc = jnp.where(me == 0, x_ref[...], g_ref[0])
        for k in range(1, 8):
            acc = acc + jnp.where(me == k, x_ref[...], g_ref[k])
        o_ref[...] = acc

    return pl.pallas_call(
        body, name="sum_small_grads",
        grid_spec=pltpu.PrefetchScalarGridSpec(
            num_scalar_prefetch=1, grid=(r // tr,),
            in_specs=[BS((8, tr, cols), lambda t, pr: (0, t, 0)), BS((tr, cols), lambda t, pr: (t, 0))],
            out_specs=BS((tr, cols), lambda t, pr: (t, 0))),
        out_shape=SDS((r, cols), F32))(place, gathered, mine)


def _adamw_values(w, g, m, v):
    m = ADAM_B1 * m + (1.0 - ADAM_B1) * g
    v = ADAM_B2 * v + (1.0 - ADAM_B2) * (g * g)
    m_hat = m / (1.0 - ADAM_B1 ** ADAM_STEP)
    v_hat = v / (1.0 - ADAM_B2 ** ADAM_STEP)
    delta = -ADAM_LR * (m_hat / (jnp.sqrt(v_hat) + ADAM_EPS) + ADAM_WD * w)
    return delta, m, v


def _adamw_big(name, l, w, m, v, g, earlier=None):
    nl, r, cols = w.shape
    tr = _row_tile(r, cols, budget=1024 * 1024)
    nt = r // tr
    n_prev = 0 if earlier is None else 4

    def body(*refs):
        w_ref, m_ref, v_ref, g_ref = refs[:4]
        go_ref, d_ref, mo_ref, vo_ref = refs[4 + n_prev:]
        gv = g_ref[0].astype(F32) + g_ref[1].astype(F32)
        delta, m_new, v_new = _adamw_values(w_ref[...], gv, m_ref[...], v_ref[...])
        go_ref[...] = gv
        d_ref[...] = delta
        mo_ref[...] = m_new
        vo_ref[...] = v_new

    layer = BS((None, tr, cols), lambda t: (l, t, 0))
    return pl.pallas_call(
        body, name=f"adamw_{name}_l{l}", grid=(nt,),
        in_specs=[layer, layer, layer, BS((2, tr, cols), lambda t: (0, t, 0))] + [ANY] * n_prev,
        out_specs=[layer] * 4, out_shape=[SDS(w.shape, F32)] * 4,
        input_output_aliases={4 + i: i for i in range(n_prev)}, compiler_params=_params(),
    )(w, m, v, g, *(earlier or ()))


def _adamw_rows(w, m, v, g):
    r, cols = w.shape
    tr = _row_tile(r, cols, budget=512 * 1024)

    def body(w_ref, m_ref, v_ref, g_ref, d_ref, mo_ref, vo_ref):
        delta, m_new, v_new = _adamw_values(w_ref[...], g_ref[...], m_ref[...], v_ref[...])
        d_ref[...] = delta
        mo_ref[...] = m_new
        vo_ref[...] = v_new

    spec = BS((tr, cols), lambda t: (t, 0))
    return pl.pallas_call(body, name="adamw_small", grid=(r // tr,), in_specs=[spec] * 4, out_specs=[spec] * 3,
                          out_shape=[SDS(w.shape, F32)] * 3)(w, m, v, g)


PACK_ALIGN = 8 * 128


def _pack_rows(arrays):
    parts = []
    for a in arrays:
        flat = a.reshape(-1)
        pad = (-flat.shape[0]) % PACK_ALIGN
        if pad:
            flat = jnp.pad(flat, (0, pad))
        parts.append(flat.reshape(-1, 128))
    return jnp.concatenate(parts, axis=0)


def _unpack_rows(buf, shapes):
    out, row = [], 0
    for shape in shapes:
        size = math.prod(shape)
        rows = -(-size // PACK_ALIGN) * (PACK_ALIGN // 128)
        out.append(buf[row:row + rows].reshape(-1)[:size].reshape(shape))
        row += rows
    return out


def kernel(x, norm1, w_in, b_gate, ssm_a_re, ssm_a_im, ssm_log_dt, ssm_b_re, ssm_b_im, ssm_c_re, ssm_c_im, ssm_d, ssm_w_glu, ssm_b_glu, ssm_w_proj, conv_w_dw, conv_b_dw, conv_ln_g, conv_ln_b, conv_w_proj, pool_w_group, pool_scale, pool_w_proj, w_out, norm2, ffn_w_gate, ffn_w_up, ffn_w_down, final_norm, loss_target, m_norm1, m_w_in, m_b_gate, m_ssm_a_re, m_ssm_a_im, m_ssm_log_dt, m_ssm_b_re, m_ssm_b_im, m_ssm_c_re, m_ssm_c_im, m_ssm_d, m_ssm_w_glu, m_ssm_b_glu, m_ssm_w_proj, m_conv_w_dw, m_conv_b_dw, m_conv_ln_g, m_conv_ln_b, m_conv_w_proj, m_pool_w_group, m_pool_scale, m_pool_w_proj, m_w_out, m_norm2, m_ffn_w_gate, m_ffn_w_up, m_ffn_w_down, m_final_norm, v_norm1, v_w_in, v_b_gate, v_ssm_a_re, v_ssm_a_im, v_ssm_log_dt, v_ssm_b_re, v_ssm_b_im, v_ssm_c_re, v_ssm_c_im, v_ssm_d, v_ssm_w_glu, v_ssm_b_glu, v_ssm_w_proj, v_conv_w_dw, v_conv_b_dw, v_conv_ln_g, v_conv_ln_b, v_conv_w_proj, v_pool_w_group, v_pool_scale, v_pool_w_proj, v_w_out, v_norm2, v_ffn_w_gate, v_ffn_w_up, v_ffn_w_down, v_final_norm):
    given = dict(locals())
    prm = {n: given[n] for n in WEIGHTS}
    mom = {n: given["m_" + n] for n in WEIGHTS}
    var = {n: given["v_" + n] for n in WEIGHTS}
    cx, cy, cc = _place()
    place = jnp.stack([2 * cx + cy, cc]).astype(jnp.int32)

    def kernel_view(n, a):
        return a.transpose(0, 2, 1) if n in TRANSPOSED else a

    dw_shard = prm["conv_w_dw"].reshape(N_LAYERS, CONV_KERNEL, -1)
    casts = {n: _cast_into(n, kernel_view(n, prm[n]), place, MXU_DTYPE) for n in BIG}
    casts["conv_w_dw"] = _cast_into("conv_w_dw", dw_shard, place, F32)
    order = [(l, g) for l in range(N_LAYERS) for g in GATHER_GROUPS]
    started = _allgather_start([[casts[n][l] for n in GATHER_GROUPS[g]] for l, g in order])
    in_flight = dict(zip(order, started))

    def weights_of(l, group, after):
        send_sems, recv_sems, bufs = in_flight[l, group]
        tag = f"l{l}_{group}"
        bufs = _allgather_forward(tag, _allgather_wait(tag, send_sems, recv_sems, bufs, after))
        fw = dict(zip(GATHER_GROUPS[group], bufs))
        if "conv_w_dw" in fw:
            fw["conv_w_dw"] = fw["conv_w_dw"].transpose(1, 0, 2).reshape(CONV_KERNEL, -1)
        return fw

    pending = []
    last_token = []

    def on_grads(l, group, grads):
        begun, token = _reduce_start(f"{l}_{group}", grads)
        pending.append((l, begun))
        last_token[:] = [token]
        return token[0, 0]

    loss, dx, _, small, dfinal = _local_step(x[0], loss_target[0], weights_of, prm, on_grads)
    loss = lax.psum(loss, ("x", "y", "c"))

    packed_names = [n for n in SMALL if n != "final_norm"] + ["conv_w_dw"]
    local_small = [jnp.stack([small[l][n] for l in range(N_LAYERS)]) for n in packed_names] + [dfinal]
    small_rows = _pack_rows(local_small)
    small_send, small_recv, small_rows, small_land = _allgather_rows_start(small_rows)

    reduced = [{} for _ in range(N_LAYERS)]
    for l, begun in pending:
        reduced[l].update(_reduce_finish(begun, place, last_token[0]))
    out = {}
    for l in reversed(range(N_LAYERS)):
        for n in BIG:
            out[n] = _adamw_big(n, l, kernel_view(n, prm[n]), kernel_view(n, mom[n]), kernel_view(n, var[n]),
                                reduced[l][n], out.get(n))
    last_big = out[BIG[0]][0]
    for n in BIG:
        out[n] = tuple(kernel_view(n, a) for a in out[n])

    small_rows, small_land = _allgather_rows_wait(small_send, small_recv, small_rows, small_land, last_big)
    gsum = _sum_devices(_allgather_rows_forward(small_land), small_rows, place)
    small_shapes = [a.shape for a in local_small]
    gsmall = dict(zip(packed_names + ["final_norm"], _unpack_rows(gsum, small_shapes)))
    lanes = dw_shard.shape[-1]
    gsmall["conv_w_dw"] = lax.dynamic_slice_in_dim(gsmall["conv_w_dw"], (2 * cx + cy) * lanes, lanes, axis=2)
    small_names = list(SMALL) + ["conv_w_dw"]
    w_rows = _pack_rows([prm[n] for n in small_names])
    m_rows = _pack_rows([mom[n] for n in small_names])
    v_rows = _pack_rows([var[n] for n in small_names])
    g_rows = _pack_rows([gsmall[n] for n in small_names])
    shapes = [prm[n].shape for n in small_names]
    d_s, m_s, v_s = (_unpack_rows(r, shapes) for r in _adamw_rows(w_rows, m_rows, v_rows, g_rows))
    for i, n in enumerate(small_names):
        out[n] = (gsmall[n].reshape(prm[n].shape), d_s[i], m_s[i], v_s[i])
    grads = [out[n][0] for n in WEIGHTS]
    deltas = [out[n][1] for n in WEIGHTS]
    new_m = [out[n][2] for n in WEIGHTS]
    new_v = [out[n][3] for n in WEIGHTS]
    return (loss, dx[None], *grads, *deltas, *new_m, *new_v)
```

```python
import functools
import math

import jax
import jax.numpy as jnp
from jax import lax
from jax.experimental import pallas as pl
from jax.experimental.pallas import tpu as pltpu

F32 = jnp.float32
MXU_DTYPE = jnp.bfloat16
WIRE_DTYPE = jnp.bfloat16
SDS = jax.ShapeDtypeStruct
BS = pl.BlockSpec
ANY = pl.BlockSpec(memory_space=pl.ANY)
HBM = pl.BlockSpec(memory_space=pltpu.HBM)
SEM = pl.BlockSpec(memory_space=pltpu.SEMAPHORE)
SIDE_EFFECT = pltpu.SideEffectType.DATAFLOW_SIDE_EFFECTING
MESH = pl.DeviceIdType.MESH

EPS = 1e-6
N_CHIPS = 4
N_LAYERS = 2
SSM_GROUPS, SSM_STATE, SSM_GROUP = 32, 64, 16
CONV_KERNEL = 31
CONV_PAD = 32
POOL_WINDOWS = (2, 4, 8, 16)
GELU_C = math.sqrt(2.0 / math.pi)
ADAM_LR, ADAM_B1, ADAM_B2, ADAM_EPS, ADAM_WD, ADAM_STEP = 0.001, 0.9, 0.999, 1e-08, 0.01, 10
VMEM_LIMIT = 56 * 1024 * 1024

BIG = ("w_in", "ssm_w_glu", "ssm_w_proj", "conv_w_proj", "pool_w_proj", "w_out", "ffn_w_gate", "ffn_w_up", "ffn_w_down")
TRANSPOSED = ("ffn_w_gate", "ffn_w_up")
GATHER_GROUPS = {
    "in": ("w_in",),
    "mixer": ("ssm_w_glu", "ssm_w_proj", "conv_w_proj", "pool_w_proj", "w_out", "conv_w_dw"),
    "ffn": ("ffn_w_gate", "ffn_w_up", "ffn_w_down"),
}
SMALL = ("norm1", "b_gate", "ssm_a_re", "ssm_a_im", "ssm_log_dt", "ssm_b_re", "ssm_b_im", "ssm_c_re", "ssm_c_im",
         "ssm_d", "ssm_b_glu", "conv_b_dw", "conv_ln_g", "conv_ln_b", "pool_w_group", "pool_scale", "norm2",
         "final_norm")
WEIGHTS = ("norm1", "w_in", "b_gate", "ssm_a_re", "ssm_a_im", "ssm_log_dt", "ssm_b_re", "ssm_b_im", "ssm_c_re",
           "ssm_c_im", "ssm_d", "ssm_w_glu", "ssm_b_glu", "ssm_w_proj", "conv_w_dw", "conv_b_dw", "conv_ln_g",
           "conv_ln_b", "conv_w_proj", "pool_w_group", "pool_scale", "pool_w_proj", "w_out", "norm2", "ffn_w_gate",
           "ffn_w_up", "ffn_w_down", "final_norm")


def _params(vmem=True):
    return pltpu.CompilerParams(vmem_limit_bytes=VMEM_LIMIT) if vmem else None


def _mm(a, b):
    return jnp.dot(a.astype(MXU_DTYPE), b.astype(MXU_DTYPE), preferred_element_type=F32)


def _mm_nt(a, b):
    return lax.dot_general(a.astype(MXU_DTYPE), b.astype(MXU_DTYPE), (((1,), (1,)), ((), ())),
                           preferred_element_type=F32)


def _mm_tn(a, b):
    return lax.dot_general(a.astype(MXU_DTYPE), b.astype(MXU_DTYPE), (((0,), (0,)), ((), ())),
                           preferred_element_type=F32)


def _sigmoid(x):
    return jax.nn.sigmoid(x)


def _gelu(x):
    t = jnp.tanh(GELU_C * (x + 0.044715 * (x * x * x)))
    return x * (0.5 * (1.0 + t)), t


def _gelu_grad(x, t):
    return 0.5 * (1.0 + t) + 0.5 * x * (1.0 - t * t) * (GELU_C * (1.0 + 3.0 * 0.044715 * x * x))


def _colsum(v):
    return jnp.sum(v, axis=0, keepdims=True)


def _row_tile(rows, cols, itemsize=4, budget=1536 * 1024):
    best = None
    for t in range(8, rows + 1, 8):
        if rows % t == 0 and t * cols * itemsize <= budget:
            best = t
    return best if best is not None else rows


def _in_proj(l, x, norm1, w_in):
    s, d = x.shape
    nc = w_in.shape[-1]
    tm = min(512, s)

    def body(x_ref, g_ref, w_ref, z_ref, h_ref):
        @pl.when(pl.program_id(1) == 0)
        def _():
            xv = x_ref[...]
            r = lax.rsqrt(jnp.mean(xv * xv, axis=-1, keepdims=True) + EPS)
            h_ref[...] = (xv * r * g_ref[...]).astype(h_ref.dtype)

        z_ref[...] = _mm(h_ref[...], w_ref[...])

    return pl.pallas_call(
        body, name=f"in_proj_l{l}", grid=(s // tm, N_CHIPS),
        in_specs=[BS((tm, d), lambda i, j: (i, 0)), BS((None, 1, d), lambda i, j: (l, 0, 0)),
                  BS((None, d, nc), lambda i, j: (j, 0, 0))],
        out_specs=[BS((tm, nc), lambda i, j: (i, j)), BS((tm, d), lambda i, j: (i, 0))],
        out_shape=[SDS((s, N_CHIPS * nc), F32), SDS((s, d), MXU_DTYPE)],
        compiler_params=_params())(x, norm1, w_in)


def _mm_cols(a, w_ref):
    return jnp.concatenate([_mm(a, w_ref[j]) for j in range(N_CHIPS)], axis=1)


def _mm_nt_cols(dv, w_ref):
    nc = w_ref.shape[-1]
    acc = _mm_nt(dv[:, 0:nc], w_ref[0])
    for j in range(1, N_CHIPS):
        acc = acc + _mm_nt(dv[:, j * nc:(j + 1) * nc], w_ref[j])
    return acc


def _merge_values(y, hc, p, zg, wglu, bglu, wpa, wpb, wpc, lng, lnb, wgrp, scale, bg):
    v = {}
    ge, th = _gelu(y)
    t = _mm(ge, wglu) + bglu
    sg = _sigmoid(t)
    sa = ge * sg
    ya = _mm_cols(sa, wpa)
    mu = jnp.mean(hc, axis=-1, keepdims=True)
    xc = hc - mu
    r = lax.rsqrt(jnp.mean(xc * xc, axis=-1, keepdims=True) + EPS)
    xh = xc * r
    ln = xh * lng + lnb
    sl = _sigmoid(ln)
    ac = ln * sl
    yb = _mm_cols(ac, wpb)
    gw = p.shape[1] // len(POOL_WINDOWS)
    q = jnp.concatenate([_mm(p[:, k * gw:(k + 1) * gw], wgrp[k]) for k in range(len(POOL_WINDOWS))], axis=1)
    pp = q * scale
    yc = _mm_cols(pp, wpc)
    d = ya.shape[1]
    gates = [_sigmoid(zg[k] + bg[:, k * d:(k + 1) * d]) for k in range(3)]
    merged = gates[0] * ya + gates[1] * yb + gates[2] * yc
    v.update(ge=ge, th=th, sg=sg, sa=sa, ya=ya, r=r, xh=xh, ln=ln, sl=sl, ac=ac, yb=yb, q=q, pp=pp, yc=yc,
             gates=gates, merged=merged)
    return v


def _merge_specs(l, tm, d, cw):
    row = lambda n: BS((None, 1, n), lambda i: (l, 0, 0))
    return [
        BS((tm, cw), lambda i: (i, 0)),
        BS((tm, cw), lambda i: (i, 0)),
        BS((tm, cw), lambda i: (i, 0)),
        BS((tm, d), lambda i: (i, 2)), BS((tm, d), lambda i: (i, 3)), BS((tm, d), lambda i: (i, 4)),
        BS((N_CHIPS, cw // N_CHIPS, cw), lambda i: (0, 0, 0)),
        row(cw),
        BS((N_CHIPS, cw, d // N_CHIPS), lambda i: (0, 0, 0)),
        BS((N_CHIPS, cw, d // N_CHIPS), lambda i: (0, 0, 0)),
        BS((N_CHIPS, cw, d // N_CHIPS), lambda i: (0, 0, 0)),
        row(cw), row(cw),
        BS((None, 4, cw // 4, cw // 4), lambda i: (l, 0, 0, 0)),
        row(cw),
        row(3 * d),
        BS((N_CHIPS, d // N_CHIPS, d), lambda i: (0, 0, 0)),
    ]


def _merge_fwd(l, x, y, hc, p, z, fw, sp):
    s, d = x.shape
    cw = y.shape[1]
    tm = min(256, s)

    def body(x_ref, y_ref, hc_ref, p_ref, z0, z1, z2, wglu, bglu, wpa, wpb, wpc, lng, lnb, wgrp, scale, bg, wout,
             x1_ref):
        v = _merge_values(y_ref[...], hc_ref[...], p_ref[...], (z0[...], z1[...], z2[...]),
                          wglu[...].reshape(cw, cw), bglu[...], wpa, wpb, wpc, lng[...], lnb[...], wgrp, scale[...],
                          bg[...])
        x1_ref[...] = x_ref[...] + _mm(v["merged"], wout[...].reshape(d, d))

    return pl.pallas_call(
        body, name=f"merge_fwd_l{l}", grid=(s // tm,),
        in_specs=[BS((tm, d), lambda i: (i, 0))] + _merge_specs(l, tm, d, cw),
        out_specs=BS((tm, d), lambda i: (i, 0)), out_shape=SDS((s, d), F32), compiler_params=_params(),
    )(x, y, hc, p, z, z, z, fw["ssm_w_glu"], sp["ssm_b_glu"], fw["ssm_w_proj"], fw["conv_w_proj"], fw["pool_w_proj"],
      sp["conv_ln_g"], sp["conv_ln_b"], sp["pool_w_group"], sp["pool_scale"], sp["b_gate"], fw["w_out"])


def _merge_bwd(l, dx1, y, hc, p, z, fw, sp):
    s, d = dx1.shape
    cw = y.shape[1]
    tm = min(256, s)
    m = MXU_DTYPE

    def body(dx1_ref, y_ref, hc_ref, p_ref, z0, z1, z2, wglu, bglu, wpa, wpb, wpc, lng, lnb, wgrp, scale, bg, wout,
             dzg_ref, dy_ref, dhc_ref, dp_ref, merged_ref, sa_ref, ac_ref, pp_ref, ge_ref, dt_ref, dya_ref, dyb_ref,
             dyc_ref, dq_ref, dbg_ref, dbglu_ref, dlng_ref, dlnb_ref, dscale_ref):
        yv = y_ref[...]
        wg = wglu[...].reshape(cw, cw)
        v = _merge_values(yv, hc_ref[...], p_ref[...], (z0[...], z1[...], z2[...]), wg, bglu[...], wpa, wpb, wpc,
                          lng[...], lnb[...], wgrp, scale[...], bg[...])
        dm = _mm_nt(dx1_ref[...], wout[...].reshape(d, d))
        ys = (v["ya"], v["yb"], v["yc"])
        dys = []
        for k in range(3):
            gk = v["gates"][k]
            dzg_ref[:, k * d:(k + 1) * d] = dm * ys[k] * (gk * (1.0 - gk))
            dys.append((dm * gk).astype(m))
        dsa = _mm_nt_cols(dys[0], wpa)
        dac = _mm_nt_cols(dys[1], wpb)
        dpp = _mm_nt_cols(dys[2], wpc)
        ge, sg = v["ge"], v["sg"]
        dt = dsa * ge * (sg * (1.0 - sg))
        dge = dsa * sg + _mm_nt(dt, wg)
        dy_ref[...] = dge * _gelu_grad(yv, v["th"])
        ln, sl, xh = v["ln"], v["sl"], v["xh"]
        dln = dac * (sl * (1.0 + ln * (1.0 - sl)))
        dxh = dln * lng[...]
        dhc_ref[...] = v["r"] * (dxh - jnp.mean(dxh, axis=-1, keepdims=True)
                                 - xh * jnp.mean(dxh * xh, axis=-1, keepdims=True))
        dq = dpp * scale[...]
        gw = cw // len(POOL_WINDOWS)
        for k in range(len(POOL_WINDOWS)):
            dp_ref[:, k * gw:(k + 1) * gw] = _mm_nt(dq[:, k * gw:(k + 1) * gw], wgrp[k])
        merged_ref[...] = v["merged"].astype(m)
        sa_ref[...] = v["sa"].astype(m)
        ac_ref[...] = v["ac"].astype(m)
        pp_ref[...] = v["pp"].astype(m)
        ge_ref[...] = ge.astype(m)
        dt_ref[...] = dt.astype(m)
        dya_ref[...] = dys[0]
        dyb_ref[...] = dys[1]
        dyc_ref[...] = dys[2]
        dq_ref[...] = dq.astype(m)

        @pl.when(pl.program_id(0) == 0)
        def _():
            for ref in (dbg_ref, dbglu_ref, dlng_ref, dlnb_ref, dscale_ref):
                ref[...] = jnp.zeros(ref.shape, F32)

        dbg_ref[...] += _colsum(dzg_ref[...])
        dbglu_ref[...] += _colsum(dt)
        dlng_ref[...] += _colsum(dln * xh)
        dlnb_ref[...] += _colsum(dln)
        dscale_ref[...] += _colsum(dpp * v["q"])

    tile = lambda n: BS((tm, n), lambda i: (i, 0))
    acc = lambda n: BS((1, n), lambda i: (0, 0))
    outs = pl.pallas_call(
        body, name=f"merge_bwd_l{l}", grid=(s // tm,),
        in_specs=[tile(d)] + _merge_specs(l, tm, d, cw),
        out_specs=[tile(3 * d), tile(cw), tile(cw), tile(cw), tile(d), tile(cw), tile(cw), tile(cw), tile(cw), tile(cw),
                   tile(d), tile(d), tile(d), tile(cw), acc(3 * d), acc(cw), acc(cw), acc(cw), acc(cw)],
        out_shape=[SDS((s, 3 * d), F32), SDS((s, cw), F32), SDS((s, cw), F32), SDS((s, cw), F32), SDS((s, d), m),
                   SDS((s, cw), m), SDS((s, cw), m), SDS((s, cw), m), SDS((s, cw), m), SDS((s, cw), m), SDS((s, d), m),
                   SDS((s, d), m), SDS((s, d), m), SDS((s, cw), m), SDS((1, 3 * d), F32), SDS((1, cw), F32),
                   SDS((1, cw), F32), SDS((1, cw), F32), SDS((1, cw), F32)],
        compiler_params=_params(),
    )(dx1, y, hc, p, z, z, z, fw["ssm_w_glu"], sp["ssm_b_glu"], fw["ssm_w_proj"], fw["conv_w_proj"], fw["pool_w_proj"],
      sp["conv_ln_g"], sp["conv_ln_b"], sp["pool_w_group"], sp["pool_scale"], sp["b_gate"], fw["w_out"])
    names = ("dzg", "dy", "dhc", "dp", "merged", "sa", "ac", "pp", "ge", "dt", "dya", "dyb", "dyc", "dq", "db_gate",
             "db_glu", "dln_g", "dln_b", "dscale")
    return dict(zip(names, outs))


def _ffn_fwd(l, x1, norm2, wg, wu, wd):
    s, d = x1.shape
    hc = wd.shape[1]
    tm = min(512, s)

    def body(x_ref, g_ref, wg_ref, wu_ref, wd_ref, o_ref, h_scr):
        @pl.when(pl.program_id(1) == 0)
        def _():
            xv = x_ref[...]
            r = lax.rsqrt(jnp.mean(xv * xv, axis=-1, keepdims=True) + EPS)
            h_scr[...] = (xv * r * g_ref[...]).astype(h_scr.dtype)
            o_ref[...] = xv

        h = h_scr[...]
        gate = _mm_nt(h, wg_ref[...])
        up = _mm_nt(h, wu_ref[...])
        o_ref[...] += _mm(gate * _sigmoid(gate) * up, wd_ref[...])

    return pl.pallas_call(
        body, name=f"ffn_fwd_l{l}", grid=(s // tm, N_CHIPS),
        in_specs=[BS((tm, d), lambda i, j: (i, 0)), BS((None, 1, d), lambda i, j: (l, 0, 0)),
                  BS((None, hc, d), lambda i, j: (j, 0, 0)), BS((None, hc, d), lambda i, j: (j, 0, 0)),
                  BS((None, hc, d), lambda i, j: (j, 0, 0))],
        out_specs=BS((tm, d), lambda i, j: (i, 0)), out_shape=SDS((s, d), F32),
        scratch_shapes=[pltpu.VMEM((tm, d), MXU_DTYPE)], compiler_params=_params())(x1, norm2, wg, wu, wd)


def _ffn_bwd(l, x1, dx2, norm2, wg, wu, wd):
    s, d = x1.shape
    hc = wd.shape[1]
    tm = min(512, s)
    m = MXU_DTYPE
    last = N_CHIPS - 1

    def body(x_ref, dx2_ref, g_ref, wg_ref, wu_ref, wd_ref, dx1_ref, h_ref, act_ref, dgate_ref, dup_ref, dn_ref,
             dh_scr, dxb_scr):
        i, j = pl.program_id(0), pl.program_id(1)

        @pl.when(j == 0)
        def _():
            xv = x_ref[...]
            r = lax.rsqrt(jnp.mean(xv * xv, axis=-1, keepdims=True) + EPS)
            h_ref[...] = (xv * r * g_ref[...]).astype(m)
            dxb_scr[...] = dx2_ref[...].astype(m)
            dh_scr[...] = jnp.zeros(dh_scr.shape, F32)

        @pl.when((i == 0) & (j == 0))
        def _():
            dn_ref[...] = jnp.zeros(dn_ref.shape, F32)

        h = h_ref[...]
        gate = _mm_nt(h, wg_ref[...])
        up = _mm_nt(h, wu_ref[...])
        sg = _sigmoid(gate)
        silu = gate * sg
        act_ref[...] = (silu * up).astype(m)
        dact = _mm_nt(dxb_scr[...], wd_ref[...])
        dup = (dact * silu).astype(m)
        dgate = (dact * up * (sg * (1.0 + gate * (1.0 - sg)))).astype(m)
        dup_ref[...] = dup
        dgate_ref[...] = dgate
        dh_scr[...] += _mm(dgate, wg_ref[...]) + _mm(dup, wu_ref[...])

        @pl.when(j == last)
        def _():
            xv = x_ref[...]
            r = lax.rsqrt(jnp.mean(xv * xv, axis=-1, keepdims=True) + EPS)
            xh = xv * r
            dh = dh_scr[...]
            dn_ref[...] += _colsum(dh * xh)
            dxh = dh * g_ref[...]
            dx1_ref[...] = dx2_ref[...] + r * (dxh - xh * jnp.mean(dxh * xh, axis=-1, keepdims=True))

    chunk = BS((None, tm, hc), lambda i, j: (j, i, 0))
    outs = pl.pallas_call(
        body, name=f"ffn_bwd_l{l}", grid=(s // tm, N_CHIPS),
        in_specs=[BS((tm, d), lambda i, j: (i, 0)), BS((tm, d), lambda i, j: (i, 0)),
                  BS((None, 1, d), lambda i, j: (l, 0, 0)),
                  BS((None, hc, d), lambda i, j: (j, 0, 0)), BS((None, hc, d), lambda i, j: (j, 0, 0)),
                  BS((None, hc, d), lambda i, j: (j, 0, 0))],
        out_specs=[BS((tm, d), lambda i, j: (i, 0)), BS((tm, d), lambda i, j: (i, 0)), chunk, chunk, chunk,
                   BS((1, d), lambda i, j: (0, 0))],
        out_shape=[SDS((s, d), F32), SDS((s, d), m), SDS((N_CHIPS, s, hc), m), SDS((N_CHIPS, s, hc), m),
                   SDS((N_CHIPS, s, hc), m), SDS((1, d), F32)],
        scratch_shapes=[pltpu.VMEM((tm, d), F32), pltpu.VMEM((tm, d), m)], compiler_params=_params(),
    )(x1, dx2, norm2, wg, wu, wd)
    return dict(zip(("dx1", "h2", "act", "dgate", "dup", "dnorm2"), outs))


def _loss_head(x, target, gf):
    s, d = x.shape
    tm = min(512, s)

    def body(x_ref, t_ref, g_ref, dx_ref, loss_ref, dg_ref):
        @pl.when(pl.program_id(0) == 0)
        def _():
            loss_ref[...] = jnp.zeros(loss_ref.shape, F32)
            dg_ref[...] = jnp.zeros(dg_ref.shape, F32)

        xv = x_ref[...]
        r = lax.rsqrt(jnp.mean(xv * xv, axis=-1, keepdims=True) + EPS)
        xh = xv * r
        err = xh * g_ref[...] - t_ref[...]
        loss_ref[...] += 0.5 * jnp.sum(jnp.mean(err * err, axis=-1, keepdims=True), axis=0, keepdims=True)
        dyv = err * (1.0 / d)
        dg_ref[...] += _colsum(dyv * xh)
        dxh = dyv * g_ref[...]
        dx_ref[...] = r * (dxh - xh * jnp.mean(dxh * xh, axis=-1, keepdims=True))

    return pl.pallas_call(
        body, name="loss_head", grid=(s // tm,),
        in_specs=[BS((tm, d), lambda i: (i, 0)), BS((tm, d), lambda i: (i, 0)), BS((1, d), lambda i: (0, 0))],
        out_specs=[BS((tm, d), lambda i: (i, 0)), BS((1, 1), lambda i: (0, 0)), BS((1, d), lambda i: (0, 0))],
        out_shape=[SDS((s, d), F32), SDS((1, 1), F32), SDS((1, d), F32)], compiler_params=_params())(x, target, gf)


def _in_proj_bwd(l, dres, x, norm1, w_in, du_a, dv1, dv2, du_c, dzg):
    s, d = x.shape
    nc = w_in.shape[-1]
    tm = min(256, s)
    m = MXU_DTYPE

    def body(dres_ref, x_ref, g_ref, w_ref, a_ref, b1_ref, b2_ref, c_ref, g3_ref, dx_ref, dz_ref, dn_ref):
        @pl.when(pl.program_id(0) == 0)
        def _():
            dn_ref[...] = jnp.zeros(dn_ref.shape, F32)

        dz = jnp.concatenate([a_ref[...], b1_ref[...], b2_ref[...], c_ref[...], g3_ref[...]], axis=1).astype(m)
        dz_ref[...] = dz
        dh = _mm_nt_cols(dz, w_ref)
        xv = x_ref[...]
        r = lax.rsqrt(jnp.mean(xv * xv, axis=-1, keepdims=True) + EPS)
        xh = xv * r
        dn_ref[...] += _colsum(dh * xh)
        dxh = dh * g_ref[...]
        dx_ref[...] = dres_ref[...] + r * (dxh - xh * jnp.mean(dxh * xh, axis=-1, keepdims=True))

    tile = lambda n: BS((tm, n), lambda i: (i, 0))
    return pl.pallas_call(
        body, name=f"in_proj_bwd_l{l}", grid=(s // tm,),
        in_specs=[tile(d), tile(d), BS((None, 1, d), lambda i: (l, 0, 0)),
                  BS((N_CHIPS, d, nc), lambda i: (0, 0, 0)),
                  tile(du_a.shape[1]), tile(dv1.shape[1]), tile(dv2.shape[1]), tile(du_c.shape[1]), tile(dzg.shape[1])],
        out_specs=[tile(d), tile(N_CHIPS * nc), BS((1, d), lambda i: (0, 0))],
        out_shape=[SDS((s, d), F32), SDS((s, N_CHIPS * nc), m), SDS((1, d), F32)], compiler_params=_params(),
    )(dres, x, norm1, w_in, du_a, dv1, dv2, du_c, dzg)


def _tn_matmul(name, a, a_spec, b, b_spec, out_shape, out_spec, grid, wire=True):
    last = grid[1] - 1

    def body(a_ref, b_ref, o_ref, *wire_ref):
        @pl.when(pl.program_id(1) == 0)
        def _():
            o_ref[...] = jnp.zeros(o_ref.shape, F32)

        o_ref[...] += _mm_tn(a_ref[...], b_ref[...])

        if wire:
            @pl.when(pl.program_id(1) == last)
            def _():
                wire_ref[0][...] = o_ref[...].astype(WIRE_DTYPE)

    if not wire:
        return pl.pallas_call(body, name=name, grid=grid, in_specs=[a_spec, b_spec], out_specs=out_spec,
                              out_shape=out_shape, compiler_params=_params())(a, b)
    return pl.pallas_call(body, name=name, grid=grid, in_specs=[a_spec, b_spec], out_specs=[out_spec, out_spec],
                          out_shape=[out_shape, SDS(out_shape.shape, WIRE_DTYPE)], compiler_params=_params())(a, b)


def _scan_consts(pw_ref, lanes, reverse):
    sgn = -1.0 if reverse else 1.0
    steps = [(k, pw_ref[2 * i], sgn * pw_ref[2 * i + 1]) for i, k in enumerate((1, 2, 4))]
    c = 4 if reverse else 3
    return steps, pw_ref[2 * c], sgn * pw_ref[2 * c + 1]


def _scan_block(br, bi, steps, row, reverse):
    for k, ar, ai in steps:
        if reverse:
            mask, sh = row < 8 - k, 8 - k
        else:
            mask, sh = row >= k, k
        sr = jnp.where(mask, pltpu.roll(br, sh, 0), 0.0)
        si = jnp.where(mask, pltpu.roll(bi, sh, 0), 0.0)
        br, bi = br + ar * sr - ai * si, bi + ar * si + ai * sr
    return br, bi


def _ssm_fwd(l, z, bblk_re, bblk_im, cblk_re, cblk_im, pw, dskip):
    s = z.shape[0]
    gc = bblk_re.shape[1]
    gl = bblk_re.shape[2]
    nblk = bblk_re.shape[0]

    def body(u_ref, bre, bim, cre, cim, pw_ref, d_ref, hre, him, y_ref):
        u = u_ref[...]
        hre[...] = _mm(u, bre[...])
        him[...] = _mm(u, bim[...])
        row = lax.broadcasted_iota(jnp.int32, (8, gl), 0)
        steps, car, cai = _scan_consts(pw_ref, gl, False)

        def step(i, carry):
            cr, ci = carry
            r0 = pl.multiple_of(i * 8, 8)
            br, bi = _scan_block(hre[pl.ds(r0, 8), :], him[pl.ds(r0, 8), :], steps, row, False)
            hr = br + car * cr - cai * ci
            hi = bi + car * ci + cai * cr
            hre[pl.ds(r0, 8), :] = hr
            him[pl.ds(r0, 8), :] = hi
            return jnp.broadcast_to(hr[7:8, :], (8, gl)), jnp.broadcast_to(hi[7:8, :], (8, gl))

        zero = jnp.zeros((8, gl), F32)
        lax.fori_loop(0, s // 8, step, (zero, zero))
        y_ref[...] = _mm(hre[...], cre[...]) - _mm(him[...], cim[...]) + d_ref[...] * u

    return pl.pallas_call(
        body, name=f"ssm_fwd_l{l}", grid=(nblk,),
        in_specs=[BS((s, gc), lambda k: (0, k)), BS((None, gc, gl), lambda k: (k, 0, 0)),
                  BS((None, gc, gl), lambda k: (k, 0, 0)), BS((None, gl, gc), lambda k: (k, 0, 0)),
                  BS((None, gl, gc), lambda k: (k, 0, 0)), BS((10, 8, gl), lambda k: (0, 0, k)),
                  BS((1, gc), lambda k: (0, k))],
        out_specs=[BS((s, gl), lambda k: (0, k)), BS((s, gl), lambda k: (0, k)), BS((s, gc), lambda k: (0, k))],
        out_shape=[SDS((s, nblk * gl), F32), SDS((s, nblk * gl), F32), SDS((s, nblk * gc), F32)],
        compiler_params=_params())(z, bblk_re, bblk_im, cblk_re, cblk_im, pw, dskip)


def _ssm_bwd(l, dy, z, hre, him, bblk_re, bblk_im, cblk_re, cblk_im, pw, dskip):
    s = z.shape[0]
    nblk, gc, gl = bblk_re.shape

    def body(dy_ref, u_ref, hre_ref, him_ref, bre, bim, cre, cim, pw_ref, d_ref,
             du_ref, dbre_ref, dbim_ref, dcre_ref, dcim_ref, dar_ref, dai_ref, dd_ref, gre, gim):
        dyv = dy_ref[...]
        u = u_ref[...]
        gre[...] = _mm_nt(dyv, cre[...])
        gim[...] = -_mm_nt(dyv, cim[...])
        dcre_ref[...] = _mm_tn(hre_ref[...], dyv)
        dcim_ref[...] = -_mm_tn(him_ref[...], dyv)
        dd_ref[...] = _colsum(dyv * u)
        row = lax.broadcasted_iota(jnp.int32, (8, gl), 0)
        steps, car, cai = _scan_consts(pw_ref, gl, True)
        n8 = s // 8

        def step(ii, carry):
            cr, ci, accr, acci = carry
            i = n8 - 1 - ii
            r0 = pl.multiple_of(i * 8, 8)
            br, bi = _scan_block(gre[pl.ds(r0, 8), :], gim[pl.ds(r0, 8), :], steps, row, True)
            dr = br + car * cr - cai * ci
            di = bi + car * ci + cai * cr
            gre[pl.ds(r0, 8), :] = dr
            gim[pl.ds(r0, 8), :] = di
            rp = pl.multiple_of(jnp.maximum(i - 1, 0) * 8, 8)
            keep = jnp.where(i > 0, 1.0, 0.0)
            pr = jnp.where(row >= 1, pltpu.roll(hre_ref[pl.ds(r0, 8), :], 1, 0),
                           keep * pltpu.roll(hre_ref[pl.ds(rp, 8), :], 1, 0))
            pi = jnp.where(row >= 1, pltpu.roll(him_ref[pl.ds(r0, 8), :], 1, 0),
                           keep * pltpu.roll(him_ref[pl.ds(rp, 8), :], 1, 0))
            accr = accr + dr * pr + di * pi
            acci = acci + di * pr - dr * pi
            return (jnp.broadcast_to(dr[0:1, :], (8, gl)), jnp.broadcast_to(di[0:1, :], (8, gl)), accr, acci)

        zero = jnp.zeros((8, gl), F32)
        _, _, accr, acci = lax.fori_loop(0, n8, step, (zero, zero, zero, zero))
        dar_ref[...] = _colsum(accr)
        dai_ref[...] = _colsum(acci)
        dbr = gre[...]
        dbi = gim[...]
        du_ref[...] = dyv * d_ref[...] + _mm_nt(dbr, bre[...]) + _mm_nt(dbi, bim[...])
        dbre_ref[...] = _mm_tn(u, dbr)
        dbim_ref[...] = _mm_tn(u, dbi)

    col = lambda n: BS((s, n), lambda k: (0, k))
    blk = lambda a, b: BS((None, a, b), lambda k: (k, 0, 0))
    outs = pl.pallas_call(
        body, name=f"ssm_bwd_l{l}", grid=(nblk,),
        in_specs=[col(gc), col(gc), col(gl), col(gl), blk(gc, gl), blk(gc, gl), blk(gl, gc), blk(gl, gc),
                  BS((10, 8, gl), lambda k: (0, 0, k)), BS((1, gc), lambda k: (0, k))],
        out_specs=[col(gc), blk(gc, gl), blk(gc, gl), blk(gl, gc), blk(gl, gc), BS((1, gl), lambda k: (0, k)),
                   BS((1, gl), lambda k: (0, k)), BS((1, gc), lambda k: (0, k))],
        out_shape=[SDS((s, nblk * gc), F32), SDS((nblk, gc, gl), F32), SDS((nblk, gc, gl), F32),
                   SDS((nblk, gl, gc), F32), SDS((nblk, gl, gc), F32), SDS((1, nblk * gl), F32),
                   SDS((1, nblk * gl), F32), SDS((1, nblk * gc), F32)],
        scratch_shapes=[pltpu.VMEM((s, gl), F32), pltpu.VMEM((s, gl), F32)], compiler_params=_params(),
    )(dy, z, hre, him, bblk_re, bblk_im, cblk_re, cblk_im, pw, dskip)
    return dict(zip(("du", "dbblk_re", "dbblk_im", "dcblk_re", "dcblk_im", "dabar_re", "dabar_im", "dd"), outs))


def _conv_fwd(l, z, wdw, bdw):
    s = z.shape[0]
    cw = wdw.shape[1]
    lb = 128
    tr = min(256, s)
    off1 = cw // lb
    off2 = 2 * cw // lb

    def body(v1_ref, v2_ref, w_ref, b_ref, hc_ref, scr):
        scr[0:CONV_PAD, :] = jnp.zeros((CONV_PAD, lb), F32)
        scr[CONV_PAD:, :] = v1_ref[...] * _sigmoid(v2_ref[...])
        for t in range(s // tr):
            acc = jnp.broadcast_to(b_ref[...], (tr, lb))
            for k in range(CONV_KERNEL):
                acc = acc + w_ref[pl.ds(k, 1), :] * scr[pl.ds(t * tr + CONV_PAD - (CONV_KERNEL - 1) + k, tr), :]
            hc_ref[pl.ds(t * tr, tr), :] = acc

    return pl.pallas_call(
        body, name=f"conv_fwd_l{l}", grid=(cw // lb,),
        in_specs=[BS((s, lb), lambda k: (0, off1 + k)), BS((s, lb), lambda k: (0, off2 + k)),
                  BS((CONV_KERNEL, lb), lambda k: (0, k)), BS((1, lb), lambda k: (0, k))],
        out_specs=BS((s, lb), lambda k: (0, k)), out_shape=SDS((s, cw), F32),
        scratch_shapes=[pltpu.VMEM((s + CONV_PAD, lb), F32)], compiler_params=_params())(z, z, wdw, bdw)


def _conv_bwd(l, dhc, z, wdw):
    s = z.shape[0]
    cw = wdw.shape[1]
    lb = 128
    tr = min(256, s)
    off1 = cw // lb
    off2 = 2 * cw // lb
    nb = cw // lb

    def body(d_ref, v1_ref, v2_ref, w_ref, dv1_ref, dv2_ref, dw_ref, db_ref, hpad, dpad):
        v1 = v1_ref[...]
        sg = _sigmoid(v2_ref[...])
        dv = d_ref[...]
        hpad[0:CONV_PAD, :] = jnp.zeros((CONV_PAD, lb), F32)
        hpad[CONV_PAD:, :] = v1 * sg
        dpad[0:s, :] = dv
        dpad[s:, :] = jnp.zeros((CONV_PAD, lb), F32)
        db_ref[...] = _colsum(dv)
        dws = [jnp.zeros((1, lb), F32) for _ in range(CONV_KERNEL)]
        for t in range(s // tr):
            dt = d_ref[pl.ds(t * tr, tr), :]
            acc = jnp.zeros((tr, lb), F32)
            for k in range(CONV_KERNEL):
                acc = acc + w_ref[pl.ds(k, 1), :] * dpad[pl.ds(t * tr + (CONV_KERNEL - 1) - k, tr), :]
                dws[k] = dws[k] + _colsum(dt * hpad[pl.ds(t * tr + CONV_PAD - (CONV_KERNEL - 1) + k, tr), :])
            sgt = _sigmoid(v2_ref[pl.ds(t * tr, tr), :])
            v1t = v1_ref[pl.ds(t * tr, tr), :]
            dv1_ref[pl.ds(t * tr, tr), :] = acc * sgt
            dv2_ref[pl.ds(t * tr, tr), :] = acc * v1t * (sgt * (1.0 - sgt))
        for k in range(CONV_KERNEL):
            dw_ref[pl.ds(k, 1), :] = dws[k]

    return pl.pallas_call(
        body, name=f"conv_bwd_l{l}", grid=(nb,),
        in_specs=[BS((s, lb), lambda k: (0, k)), BS((s, lb), lambda k: (0, off1 + k)),
                  BS((s, lb), lambda k: (0, off2 + k)), BS((CONV_KERNEL, lb), lambda k: (0, k))],
        out_specs=[BS((s, lb), lambda k: (0, k)), BS((s, lb), lambda k: (0, k)),
                   BS((CONV_KERNEL, lb), lambda k: (0, k)), BS((1, lb), lambda k: (0, k))],
        out_shape=[SDS((s, cw), F32), SDS((s, cw), F32), SDS((CONV_KERNEL, cw), F32), SDS((1, cw), F32)],
        scratch_shapes=[pltpu.VMEM((s + CONV_PAD, lb), F32), pltpu.VMEM((s + CONV_PAD, lb), F32)],
        compiler_params=_params())(dhc, z, z, wdw)


def _pool_window(k):
    return jnp.where(k == 0, float(POOL_WINDOWS[0]),
                     jnp.where(k == 1, float(POOL_WINDOWS[1]),
                               jnp.where(k == 2, float(POOL_WINDOWS[2]), float(POOL_WINDOWS[3]))))


def _pool_fwd(l, z, pw_width):
    s = z.shape[0]
    lb = pw_width // len(POOL_WINDOWS)
    off = 3 * pw_width // lb

    def body(u_ref, p_ref):
        k = pl.program_id(0)
        u = u_ref[...]
        row = lax.broadcasted_iota(jnp.int32, (s, lb), 0)
        sums = [u]
        for sh in (1, 2, 4, 8):
            prev = sums[-1]
            sums.append(prev + jnp.where(row >= sh, pltpu.roll(prev, sh, 0), 0.0))
        sel = jnp.where(k == 0, sums[1], jnp.where(k == 1, sums[2], jnp.where(k == 2, sums[3], sums[4])))
        cnt = jnp.minimum((row + 1).astype(F32), _pool_window(k))
        p_ref[...] = sel / cnt - u

    return pl.pallas_call(
        body, name=f"pool_fwd_l{l}", grid=(len(POOL_WINDOWS),),
        in_specs=[BS((s, lb), lambda k: (0, off + k))], out_specs=BS((s, lb), lambda k: (0, k)),
        out_shape=SDS((s, pw_width), F32), compiler_params=_params())(z)


def _pool_bwd(l, dp):
    s, width = dp.shape
    lb = width // len(POOL_WINDOWS)

    def body(d_ref, du_ref):
        k = pl.program_id(0)
        dv = d_ref[...]
        row = lax.broadcasted_iota(jnp.int32, (s, lb), 0)
        cnt = jnp.minimum((row + 1).astype(F32), _pool_window(k))
        sums = [dv / cnt]
        for sh in (1, 2, 4, 8):
            prev = sums[-1]
            sums.append(prev + jnp.where(row < s - sh, pltpu.roll(prev, s - sh, 0), 0.0))
        sel = jnp.where(k == 0, sums[1], jnp.where(k == 1, sums[2], jnp.where(k == 2, sums[3], sums[4])))
        du_ref[...] = sel - dv

    return pl.pallas_call(
        body, name=f"pool_bwd_l{l}", grid=(len(POOL_WINDOWS),),
        in_specs=[BS((s, lb), lambda k: (0, k))], out_specs=BS((s, lb), lambda k: (0, k)),
        out_shape=SDS((s, width), F32), compiler_params=_params())(dp)


def _zoh(a_re, a_im, log_dt):
    dt = jnp.exp(log_dt)
    mag = jnp.exp(dt * a_re)
    ang = dt * a_im
    abar_re = mag * jnp.cos(ang)
    abar_im = mag * jnp.sin(ang)
    den = a_re * a_re + a_im * a_im
    nr = abar_re - 1.0
    ni = abar_im
    f_re = (nr * a_re + ni * a_im) / den
    f_im = (ni * a_re - nr * a_im) / den
    return abar_re, abar_im, f_re, f_im


def _zoh_fwd(l, a_re, a_im, log_dt):
    def body(ar, ai, ld, o0, o1, o2, o3):
        for ref, val in zip((o0, o1, o2, o3), _zoh(ar[...], ai[...], ld[...])):
            ref[...] = val

    return pl.pallas_call(body, name=f"zoh_fwd_l{l}", out_shape=[SDS(a_re.shape, F32)] * 4)(a_re, a_im, log_dt)


def _zoh_bwd(l, a_re, a_im, log_dt, cts):
    def body(ar, ai, ld, c0, c1, c2, c3, dar, dai, dld):
        _, vjp = jax.vjp(_zoh, ar[...], ai[...], ld[...])
        g = vjp((c0[...], c1[...], c2[...], c3[...]))
        dar[...] = g[0]
        dai[...] = g[1]
        dld[...] = g[2]

    return pl.pallas_call(body, name=f"zoh_bwd_l{l}",
                          out_shape=[SDS(a_re.shape, F32), SDS(a_re.shape, F32), SDS(log_dt.shape, F32)],
                          )(a_re, a_im, log_dt, *cts)


def _bbar_fwd(l, f_re, f_im, b_re, b_im):
    def body(fr, fi, br, bi, o_re, o_im):
        o_re[...] = fr[...] * br[...] - fi[...] * bi[...]
        o_im[...] = fr[...] * bi[...] + fi[...] * br[...]

    return pl.pallas_call(body, name=f"bbar_fwd_l{l}", out_shape=[SDS(b_re.shape, F32)] * 2)(f_re, f_im, b_re, b_im)


def _bbar_bwd(l, f_re, f_im, b_re, b_im, d_re, d_im):
    def body(fr, fi, br, bi, dr, di, dfr, dfi, dbr, dbi):
        dfr[...] = jnp.sum(dr[...] * br[...] + di[...] * bi[...], axis=1, keepdims=True)
        dfi[...] = jnp.sum(di[...] * br[...] - dr[...] * bi[...], axis=1, keepdims=True)
        dbr[...] = fr[...] * dr[...] + fi[...] * di[...]
        dbi[...] = fr[...] * di[...] - fi[...] * dr[...]

    return pl.pallas_call(body, name=f"bbar_bwd_l{l}",
                          out_shape=[SDS(f_re.shape, F32), SDS(f_re.shape, F32), SDS(b_re.shape, F32),
                                     SDS(b_re.shape, F32)])(f_re, f_im, b_re, b_im, d_re, d_im)


def _powers(l, abar_re, abar_im):
    lanes = abar_re.shape[1]

    def body(ar_ref, ai_ref, o_ref):
        ar, ai = ar_ref[...], ai_ref[...]
        pows = [(ar, ai)]
        for _ in range(7):
            pr, pi = pows[-1]
            pows.append((pr * ar - pi * ai, pr * ai + pi * ar))
        row = lax.broadcasted_iota(jnp.int32, (8, lanes), 0)
        for i, k in enumerate((1, 2, 4)):
            o_ref[2 * i] = jnp.broadcast_to(pows[k - 1][0], (8, lanes))
            o_ref[2 * i + 1] = jnp.broadcast_to(pows[k - 1][1], (8, lanes))
        for slot, order in ((3, range(8)), (4, range(7, -1, -1))):
            vr = jnp.zeros((8, lanes), F32)
            vi = jnp.zeros((8, lanes), F32)
            for r, e in enumerate(order):
                vr = jnp.where(row == r, pows[e][0], vr)
                vi = jnp.where(row == r, pows[e][1], vi)
            o_ref[2 * slot] = vr
            o_ref[2 * slot + 1] = vi

    return pl.pallas_call(body, name=f"powers_l{l}", out_shape=SDS((10, 8, lanes), F32))(abar_re, abar_im)


def _block_diag(v, rows_first):
    g, a, b = v.shape
    eye = jnp.eye(8, dtype=v.dtype)
    out = jnp.einsum("kgab,gh->kgahb", v.reshape(g // 8, 8, a, b), eye)
    return out.reshape(g // 8, 8 * a, 8 * b)


def _block_diag_extract(blk, a, b):
    n = blk.shape[0]
    v = blk.reshape(n, 8, a, 8, b)
    return jnp.einsum("kgahb,gh->kgab", v, jnp.eye(8, dtype=blk.dtype)).reshape(n * 8, a, b)


def _ssm_prepare(l, prm):
    g, n, p = SSM_GROUPS, SSM_STATE, SSM_GROUP
    a_re, a_im = prm["ssm_a_re"][l], prm["ssm_a_im"][l]
    log_dt = prm["ssm_log_dt"][l].reshape(g, 1)
    abar_re, abar_im, f_re, f_im = _zoh_fwd(l, a_re, a_im, log_dt)
    b_re = prm["ssm_b_re"][l].reshape(g * n, p)
    b_im = prm["ssm_b_im"][l].reshape(g * n, p)
    fcol_re, fcol_im = f_re.reshape(g * n, 1), f_im.reshape(g * n, 1)
    bbar_re, bbar_im = _bbar_fwd(l, fcol_re, fcol_im, b_re, b_im)
    bblk_re = _block_diag(bbar_re.reshape(g, n, p).transpose(0, 2, 1), True).astype(MXU_DTYPE)
    bblk_im = _block_diag(bbar_im.reshape(g, n, p).transpose(0, 2, 1), True).astype(MXU_DTYPE)
    cblk_re = _block_diag(prm["ssm_c_re"][l].transpose(0, 2, 1), False).astype(MXU_DTYPE)
    cblk_im = _block_diag(prm["ssm_c_im"][l].transpose(0, 2, 1), False).astype(MXU_DTYPE)
    pw = _powers(l, abar_re.reshape(1, g * n), abar_im.reshape(1, g * n))
    return dict(a_re=a_re, a_im=a_im, log_dt=log_dt, b_re=b_re, b_im=b_im, fcol_re=fcol_re, fcol_im=fcol_im,
                bblk_re=bblk_re, bblk_im=bblk_im, cblk_re=cblk_re, cblk_im=cblk_im, pw=pw,
                dskip=prm["ssm_d"][l].reshape(1, g * p))


def _ssm_param_grads(l, sd, r):
    g, n, p = SSM_GROUPS, SSM_STATE, SSM_GROUP
    dbbar_re = _block_diag_extract(r["dbblk_re"], p, n).transpose(0, 2, 1).reshape(g * n, p)
    dbbar_im = _block_diag_extract(r["dbblk_im"], p, n).transpose(0, 2, 1).reshape(g * n, p)
    dfr, dfi, db_re, db_im = _bbar_bwd(l, sd["fcol_re"], sd["fcol_im"], sd["b_re"], sd["b_im"], dbbar_re, dbbar_im)
    cts = (r["dabar_re"].reshape(g, n), r["dabar_im"].reshape(g, n), dfr.reshape(g, n), dfi.reshape(g, n))
    da_re, da_im, dlog_dt = _zoh_bwd(l, sd["a_re"], sd["a_im"], sd["log_dt"], cts)
    dc_re = _block_diag_extract(r["dcblk_re"], n, p).transpose(0, 2, 1)
    dc_im = _block_diag_extract(r["dcblk_im"], n, p).transpose(0, 2, 1)
    return dict(ssm_a_re=da_re, ssm_a_im=da_im, ssm_log_dt=dlog_dt.reshape(g), ssm_b_re=db_re.reshape(g, n, p),
                ssm_b_im=db_im.reshape(g, n, p), ssm_c_re=dc_re, ssm_c_im=dc_im, ssm_d=r["dd"].reshape(g, p))


def _ffn_weight_grads(l, fb, dx2, s):
    d = dx2.shape[1]
    hcn = fb["act"].shape[-1]
    g = {}
    for name, key, rhs, ts in (("ffn_w_gate", "dgate", fb["h2"], s), ("ffn_w_up", "dup", fb["h2"], s),
                               ("ffn_w_down", "act", dx2, min(1024, s))):
        g[name] = _tn_matmul(f"d{name}_l{l}", fb[key], BS((None, ts, hcn), lambda j, t: (j, t, 0)), rhs,
                             BS((ts, d), lambda j, t: (t, 0)), SDS((N_CHIPS, hcn, d), F32),
                             BS((None, hcn, d), lambda j, t: (j, 0, 0)), (N_CHIPS, s // ts))
    return g


def _in_weight_grad(l, h, dz):
    s, d = h.shape
    ncw = dz.shape[1] // N_CHIPS
    return _tn_matmul(f"dw_in_l{l}", h, BS((s, d), lambda j, t: (0, 0)), dz, BS((s, ncw), lambda j, t: (0, j)),
                      SDS((N_CHIPS, d, ncw), F32), BS((None, d, ncw), lambda j, t: (j, 0, 0)), (N_CHIPS, 1))


def _fused_tn(name, pairs, kinds, s, wire):
    ts = min(512, s)
    n = len(pairs)

    def shape_of(a, b, kind):
        k, m = a.shape[1], b.shape[1]
        if kind == "rows":
            return (N_CHIPS, k // N_CHIPS, m)
        if kind == "cols":
            return (N_CHIPS, k, m // N_CHIPS)
        return (k // 128, 128, 128)

    shapes = [shape_of(a, b, kind) for (a, b), kind in zip(pairs, kinds)]
    last = s // ts - 1

    def body(*refs):
        ins, outs = refs[:2 * n], refs[2 * n:]
        first = pl.program_id(0) == 0
        for i, kind in enumerate(kinds):
            a, b = ins[2 * i][...], ins[2 * i + 1][...]
            o_ref = outs[i]

            @pl.when(first)
            def _():
                o_ref[...] = jnp.zeros(o_ref.shape, F32)

            if kind == "rows":
                o_ref[...] += _mm_tn(a, b).reshape(o_ref.shape)
            elif kind == "cols":
                full = _mm_tn(a, b)
                nc = o_ref.shape[2]
                for j in range(N_CHIPS):
                    o_ref[j] += full[:, j * nc:(j + 1) * nc]
            else:
                for k in range(o_ref.shape[0]):
                    o_ref[k] += _mm_tn(a[:, k * 128:(k + 1) * 128], b[:, k * 128:(k + 1) * 128])
        if wire:
            @pl.when(pl.program_id(0) == last)
            def _():
                for i in range(n):
                    outs[n + i][...] = outs[i][...].astype(WIRE_DTYPE)

    whole = lambda shp: BS(shp, lambda t: (0,) * len(shp))
    out_shape = [SDS(shp, F32) for shp in shapes] + ([SDS(shp, WIRE_DTYPE) for shp in shapes] if wire else [])
    outs = pl.pallas_call(
        body, name=name, grid=(s // ts,),
        in_specs=[BS((ts, v.shape[1]), lambda t: (t, 0)) for pair in pairs for v in pair],
        out_specs=[whole(o.shape) for o in out_shape], out_shape=out_shape, compiler_params=_params(),
    )(*[v for pair in pairs for v in pair])
    return [(outs[i], outs[n + i]) for i in range(n)] if wire else list(outs)


def _mixer_weight_grads(l, sv, mb, dx1, s):
    g = {}
    (g["w_out"], g["ssm_w_glu"]) = _fused_tn(f"dw_out_glu_l{l}", [(mb["merged"], dx1), (mb["ge"], mb["dt"])],
                                            ("rows", "rows"), s, True)
    (g["ssm_w_proj"], g["conv_w_proj"], g["pool_w_proj"]) = _fused_tn(
        f"dw_proj_l{l}", [(mb["sa"], mb["dya"]), (mb["ac"], mb["dyb"]), (mb["pp"], mb["dyc"])],
        ("cols", "cols", "cols"), s, True)
    (dwgrp,) = _fused_tn(f"dpool_w_group_l{l}", [(sv["p"], mb["dq"])], ("groups",), s, False)
    return g, dwgrp


def _local_step(x, target, weights_of, prm, on_grads=None):
    s, d = x.shape
    cw = prm["ssm_b_glu"].shape[1]
    sp = {k: prm[k].reshape(N_LAYERS, 1, -1) for k in ("norm1", "norm2", "b_gate", "ssm_b_glu", "conv_ln_g", "conv_ln_b",
                                                        "pool_scale", "conv_b_dw")}
    sp["pool_w_group"] = prm["pool_w_group"]
    saved = []
    xin = x
    for l in range(N_LAYERS):
        fw = weights_of(l, "in", (xin,))
        sd = _ssm_prepare(l, prm)
        z, h = _in_proj(l, xin, sp["norm1"], fw["w_in"])
        hre, him, y = _ssm_fwd(l, z, sd["bblk_re"], sd["bblk_im"], sd["cblk_re"], sd["cblk_im"], sd["pw"], sd["dskip"])
        p = _pool_fwd(l, z, cw)
        fw.update(weights_of(l, "mixer", (y, p)))
        wdw = fw["conv_w_dw"]
        hc = _conv_fwd(l, z, wdw, sp["conv_b_dw"][l])
        x1 = _merge_fwd(l, xin, y, hc, p, z, fw, sp)
        fw.update(weights_of(l, "ffn", (x1,)))
        x2 = _ffn_fwd(l, x1, sp["norm2"], fw["ffn_w_gate"], fw["ffn_w_up"], fw["ffn_w_down"])
        saved.append(dict(x=xin, z=z, h=h, hre=hre, him=him, y=y, hc=hc, p=p, x1=x1, sd=sd, wdw=wdw, fw=fw))
        xin = x2
    dx, loss, dfinal = _loss_head(xin, target, prm["final_norm"].reshape(1, d))
    big = [None] * N_LAYERS
    small = [None] * N_LAYERS
    norm2_rows = sp["norm2"]
    started = (lambda l, group, grads: on_grads(l, group, grads)) if on_grads is not None else (lambda *a: 0.0)
    for l in reversed(range(N_LAYERS)):
        sv = saved[l]
        sd, fw = sv["sd"], sv["fw"]
        fb = _ffn_bwd(l, sv["x1"], dx, norm2_rows, fw["ffn_w_gate"], fw["ffn_w_up"], fw["ffn_w_down"])
        big[l] = _ffn_weight_grads(l, fb, dx, s)
        spl = dict(sp, ssm_b_glu=sp["ssm_b_glu"] + started(l, "ffn", big[l]))
        mb = _merge_bwd(l, fb["dx1"], sv["y"], sv["hc"], sv["p"], sv["z"], fw, spl)
        mixer, dwgrp = _mixer_weight_grads(l, sv, mb, fb["dx1"], s)
        big[l].update(mixer)
        wdw = sv["wdw"] + started(l, "mixer", mixer)
        du_c = _pool_bwd(l, mb["dp"])
        dv1, dv2, dwdw, dbdw = _conv_bwd(l, mb["dhc"], sv["z"], wdw)
        sr = _ssm_bwd(l, mb["dy"], sv["z"], sv["hre"], sv["him"], sd["bblk_re"], sd["bblk_im"], sd["cblk_re"],
                      sd["cblk_im"], sd["pw"], sd["dskip"])
        dx, dz, dnorm1 = _in_proj_bwd(l, fb["dx1"], sv["x"], sp["norm1"], fw["w_in"], sr["du"], dv1, dv2, du_c, mb["dzg"])
        w_in_grad = {"w_in": _in_weight_grad(l, sv["h"], dz)}
        big[l].update(w_in_grad)
        sg = _ssm_param_grads(l, sd, sr)
        sg.update(norm1=dnorm1.reshape(d), b_gate=mb["db_gate"].reshape(3 * d), ssm_b_glu=mb["db_glu"].reshape(cw),
                  conv_b_dw=dbdw.reshape(cw), conv_ln_g=mb["dln_g"].reshape(cw), conv_ln_b=mb["dln_b"].reshape(cw),
                  pool_w_group=dwgrp, pool_scale=mb["dscale"].reshape(cw), norm2=fb["dnorm2"].reshape(d),
                  conv_w_dw=dwdw)
        small[l] = sg
        if l == N_LAYERS - 1:
            sg = dict(sg, final_norm=dfinal.reshape(d))
        norm2_rows = sp["norm2"] + (started(l, "in", w_in_grad) + started(l, "small", sg))
    return loss[0, 0], dx, big, small, dfinal.reshape(d)


def _place():
    return lax.axis_index("x"), lax.axis_index("y"), lax.axis_index("c")


def _other_chips(x, y):
    return [(1 - x, y), (x, 1 - y), (1 - x, 1 - y)]


def _remote(src, dst, send_sem, recv_sem, device):
    return pltpu.make_async_remote_copy(src_ref=src, dst_ref=dst, send_sem=send_sem, recv_sem=recv_sem,
                                        device_id=device, device_id_type=MESH)


def _hbm(v):
    return pltpu.with_memory_space_constraint(v, pltpu.HBM)


def _cast_into(name, w, place, dtype):
    nl, k, n = w.shape
    tr = _row_tile(k, n)
    nt = k // tr

    def body(place_ref, w_ref, o0_ref, o1_ref):
        @pl.when(pl.program_id(0) == 0)
        def _():
            o0_ref[...] = w_ref[...].astype(dtype)

        @pl.when(pl.program_id(0) == 1)
        def _():
            o1_ref[...] = w_ref[...].astype(dtype)

    return pl.pallas_call(
        body, name=f"cast_{name}",
        grid_spec=pltpu.PrefetchScalarGridSpec(
            num_scalar_prefetch=1, grid=(nl, nt),
            in_specs=[BS((None, tr, n), lambda l, t, pr: (l, t, 0))],
            out_specs=[BS((None, tr, n), lambda l, t, pr: (pr[0], t * (1 - l) + (nt - 1) * l, 0)),
                       BS((None, tr, n), lambda l, t, pr: (pr[0], t * l, 0))]),
        out_shape=[SDS((N_CHIPS, k, n), dtype)] * 2)(place, w)


def _gather_rows(buf, c):
    k = buf.shape[1]
    if k % 2:
        return pl.ds(0, k)
    return pl.ds(pl.multiple_of(c * (k // 2), 8), k // 2)


def _allgather_start(tag, groups):
    ng = len(groups)
    sizes = [len(g) for g in groups]
    first = [sum(sizes[:g]) for g in range(ng)]
    flat = [b for g in groups for b in g]
    nb = len(flat)

    def body(*refs):
        ins = refs[:nb]
        sems = refs[nb:nb + 2 * ng]
        x, y, c = _place()
        jme = 2 * x + y
        for g in range(ng):
            for a in range(sizes[g]):
                buf = ins[first[g] + a]
                blk = buf.at[jme, _gather_rows(buf, c)]
                for k, (cx, cy) in enumerate(_other_chips(x, y)):
                    _remote(blk, blk, sems[2 * g].at[3 * a + k], sems[2 * g + 1].at[3 * a + k], (cx, cy, c)).start()

    sem_shapes = [pltpu.SemaphoreType.DMA((3 * sizes[g // 2],)) for g in range(2 * ng)]
    outs = pl.pallas_call(
        body, name=f"allgather_start_{tag}", in_specs=[HBM] * nb, out_specs=[SEM] * (2 * ng) + [HBM] * nb,
        out_shape=sem_shapes + [pltpu.HBM(b.shape, b.dtype) for b in flat],
        input_output_aliases={i: 2 * ng + i for i in range(nb)},
        compiler_params=pltpu.CompilerParams(has_side_effects=SIDE_EFFECT))(*[_hbm(b) for b in flat])
    return [(outs[2 * g], outs[2 * g + 1], outs[2 * ng + first[g]:2 * ng + first[g] + sizes[g]]) for g in range(ng)]


def _allgather_wait(l, send_sems, recv_sems, bufs, after):
    n = len(bufs)

    def body(*refs):
        ins = refs[:n]
        ssem, rsem = refs[n], refs[n + 1]
        x, y, c = _place()
        jme = 2 * x + y
        for a in range(n):
            rows = _gather_rows(ins[a], c)
            for k, (cx, cy) in enumerate(_other_chips(x, y)):
                cp = _remote(ins[a].at[jme, rows], ins[a].at[2 * cx + cy, rows], ssem.at[3 * a + k],
                             rsem.at[3 * a + k], (cx, cy, c))
                cp.wait_send()
                cp.wait_recv()

    return pl.pallas_call(
        body, name=f"allgather_wait_{l}", in_specs=[HBM] * n + [SEM, SEM] + [ANY] * len(after), out_specs=[HBM] * n,
        out_shape=[pltpu.HBM(b.shape, b.dtype) for b in bufs], input_output_aliases={i: i for i in range(n)},
        compiler_params=pltpu.CompilerParams(has_side_effects=SIDE_EFFECT))(*bufs, send_sems, recv_sems, *after)


def _allgather_forward(l, bufs):
    n = len(bufs)
    split = [a for a in range(n) if bufs[a].shape[1] % 2 == 0]

    def body(*refs):
        ins = refs[:n]
        send_sems, recv_sems = refs[2 * n:]
        x, y, c = _place()
        sibling = (x, y, 1 - c)
        copies = []
        for a in split:
            for k, (cx, cy) in enumerate(_other_chips(x, y)):
                blk = ins[a].at[2 * cx + cy, _gather_rows(ins[a], c)]
                cp = _remote(blk, blk, send_sems.at[a, k], recv_sems.at[a, k], sibling)
                cp.start()
                copies.append(cp)
        for a in split:
            for k, (cx, cy) in enumerate(_other_chips(x, y)):
                blk = ins[a].at[2 * cx + cy, _gather_rows(ins[a], 1 - c)]
                _remote(blk, blk, send_sems.at[a, k], recv_sems.at[a, k], sibling).wait_recv()
        for cp in copies:
            cp.wait_send()

    sem = pltpu.SemaphoreType.DMA((n, 3))
    return pl.pallas_call(
        body, name=f"allgather_forward_{l}", in_specs=[ANY] * n, out_specs=[ANY] * n,
        out_shape=[SDS(b.shape, b.dtype) for b in bufs], input_output_aliases={i: i for i in range(n)},
        scratch_shapes=[sem, sem])(*bufs)


def _rs_to_owner(l, parts):
    n = len(parts)
    lands = [lax.empty((3,) + p.shape[1:], p.dtype) for p in parts]

    def body(*refs):
        ins, zones = refs[:n], refs[n:2 * n]
        send_sems, recv_sems = refs[2 * n], refs[2 * n + 1]
        token = refs[-1]
        x, y, c = _place()
        for a in range(n):
            for k, (cx, cy) in enumerate(_other_chips(x, y)):
                _remote(ins[a].at[2 * cx + cy], zones[a].at[k], send_sems.at[3 * a + k], recv_sems.at[3 * a + k],
                        (cx, cy, c)).start()
        token[...] = jnp.zeros(token.shape, F32)

    sem = pltpu.SemaphoreType.DMA((3 * n,))
    outs = pl.pallas_call(
        body, name=f"rs_to_owner_start_{l}", in_specs=[HBM] * (2 * n),
        out_specs=[SEM, SEM] + [HBM] * (2 * n) + [pl.BlockSpec(memory_space=pltpu.VMEM)],
        out_shape=[sem, sem] + [pltpu.HBM(p.shape, p.dtype) for p in parts]
        + [pltpu.HBM(z.shape, z.dtype) for z in lands] + [SDS((8, 128), F32)],
        input_output_aliases={i: 2 + i for i in range(2 * n)},
        compiler_params=pltpu.CompilerParams(has_side_effects=SIDE_EFFECT),
    )(*[_hbm(p) for p in parts], *[_hbm(z) for z in lands])
    return outs[0], outs[1], outs[2:2 + n], outs[2 + n:2 + 2 * n], outs[-1]


def _rs_to_owner_wait(l, send_sems, recv_sems, parts, lands, after):
    n = len(parts)

    def body(*refs):
        ins, zones = refs[:n], refs[n:2 * n]
        ssem, rsem = refs[2 * n], refs[2 * n + 1]
        x, y, c = _place()
        for a in range(n):
            for k, (cx, cy) in enumerate(_other_chips(x, y)):
                cp = _remote(ins[a].at[2 * cx + cy], zones[a].at[k], ssem.at[3 * a + k], rsem.at[3 * a + k],
                             (cx, cy, c))
                cp.wait_send()
                cp.wait_recv()

    outs = pl.pallas_call(
        body, name=f"rs_to_owner_wait_{l}", in_specs=[HBM] * (2 * n) + [SEM, SEM, ANY], out_specs=[HBM] * (2 * n),
        out_shape=[pltpu.HBM(p.shape, p.dtype) for p in parts] + [pltpu.HBM(z.shape, z.dtype) for z in lands],
        input_output_aliases={i: i for i in range(2 * n)},
        compiler_params=pltpu.CompilerParams(has_side_effects=SIDE_EFFECT),
    )(*parts, *lands, send_sems, recv_sems, after)
    return outs[:n], outs[n:]


def _rs_sibling_exchange(l, both):
    n = len(both)

    def body(*refs):
        ins = refs[:n]
        send_sems, recv_sems = refs[2 * n:]
        x, y, c = _place()
        copies = []
        for a in range(n):
            cp = _remote(ins[a].at[c], ins[a].at[c], send_sems.at[a], recv_sems.at[a], (x, y, 1 - c))
            cp.start()
            copies.append(cp)
        for a, cp in enumerate(copies):
            cp.wait_send()
            _remote(ins[a].at[1 - c], ins[a].at[1 - c], send_sems.at[a], recv_sems.at[a], (x, y, 1 - c)).wait_recv()

    sem = pltpu.SemaphoreType.DMA((n,))
    return pl.pallas_call(
        body, name=f"rs_sibling_exchange_{l}", in_specs=[ANY] * n, out_specs=[ANY] * n,
        out_shape=[SDS(b.shape, b.dtype) for b in both], input_output_aliases={i: i for i in range(n)},
        scratch_shapes=[sem, sem])(*both)


def _add_owner(name, grad, recv, place):
    _, r, cols = grad.shape
    tr = _row_tile(r, cols, budget=1024 * 1024)
    nt = r // tr

    def body(place_ref, g_ref, r_ref, o_ref):
        acc = ((g_ref[...] + r_ref[0].astype(F32)) + r_ref[1].astype(F32)) + r_ref[2].astype(F32)
        o_ref[...] = acc.astype(o_ref.dtype)

    return pl.pallas_call(
        body, name=name,
        grid_spec=pltpu.PrefetchScalarGridSpec(
            num_scalar_prefetch=1, grid=(nt,),
            in_specs=[BS((None, tr, cols), lambda t, pr: (pr[0], t, 0)), BS((3, tr, cols), lambda t, pr: (0, t, 0))],
            out_specs=BS((None, tr, cols), lambda t, pr: (pr[1], t, 0))),
        out_shape=SDS((2, r, cols), WIRE_DTYPE))(place, grad, recv)


def _reduce_start(tag, grads):
    names = list(grads)
    send_sems, recv_sems, wires, lands, token = _rs_to_owner(tag, [grads[n][1] for n in names])
    return dict(tag=tag, names=names, send_sems=send_sems, recv_sems=recv_sems, wires=wires, lands=lands,
                grads=[grads[n][0] for n in names]), token


def _reduce_finish(pending, place, after):
    tag, names = pending["tag"], pending["names"]
    _, lands = _rs_to_owner_wait(tag, pending["send_sems"], pending["recv_sems"], pending["wires"],
                                 pending["lands"], after)
    mine = [_add_owner(f"rs_add_owner_{n}_{tag}", g, r, place) for n, g, r in zip(names, pending["grads"], lands)]
    return dict(zip(names, _rs_sibling_exchange(tag, mine)))


def _small_peers(x, y, c):
    return [(x, y, 1 - c)] + [(cx, cy, c) for cx, cy in _other_chips(x, y)]


def _allgather_rows_start(tag, buf):
    land = lax.empty((8,) + buf.shape, buf.dtype)

    def body(x_ref, out_ref, send_sems, recv_sems, x_thru, out_thru, token):
        x, y, c = _place()
        for i, peer in enumerate(_small_peers(x, y, c)):
            _remote(x_ref, out_ref.at[4 * x + 2 * y + c], send_sems.at[i], recv_sems.at[i], peer).start()
        token[...] = jnp.zeros(token.shape, F32)

    sem = pltpu.SemaphoreType.DMA((4,))
    return pl.pallas_call(
        body, name=f"allgather_small_start_{tag}", in_specs=[HBM, HBM],
        out_specs=[SEM, SEM, HBM, HBM, pl.BlockSpec(memory_space=pltpu.VMEM)],
        out_shape=[sem, sem, pltpu.HBM(buf.shape, buf.dtype), pltpu.HBM(land.shape, land.dtype), SDS((8, 128), F32)],
        input_output_aliases={0: 2, 1: 3}, compiler_params=pltpu.CompilerParams(has_side_effects=SIDE_EFFECT),
    )(_hbm(buf), _hbm(land))


def _allgather_rows_wait(tag, send_sems, recv_sems, buf, land, after):
    def body(x_ref, out_ref, ssem, rsem, after_ref, x_thru, out_thru):
        x, y, c = _place()
        for i, (px, py, pc) in enumerate(_small_peers(x, y, c)):
            cp = _remote(x_ref, out_ref.at[4 * px + 2 * py + pc], ssem.at[i], rsem.at[i], (px, py, pc))
            cp.wait_send()
            cp.wait_recv()

    return pl.pallas_call(
        body, name=f"allgather_small_wait_{tag}", in_specs=[HBM, HBM, SEM, SEM, ANY], out_specs=[HBM, HBM],
        out_shape=[pltpu.HBM(buf.shape, buf.dtype), pltpu.HBM(land.shape, land.dtype)],
        input_output_aliases={0: 0, 1: 1}, compiler_params=pltpu.CompilerParams(has_side_effects=SIDE_EFFECT),
    )(buf, land, send_sems, recv_sems, after)


def _allgather_rows_forward(tag, land):
    def body(in_ref, out_ref, send_sems, recv_sems):
        x, y, c = _place()
        sibling = (x, y, 1 - c)
        copies = []
        for k, (cx, cy) in enumerate(_other_chips(x, y)):
            blk = in_ref.at[4 * cx + 2 * cy + c]
            cp = _remote(blk, blk, send_sems.at[k], recv_sems.at[k], sibling)
            cp.start()
            copies.append(cp)
        for k, (cx, cy) in enumerate(_other_chips(x, y)):
            blk = in_ref.at[4 * cx + 2 * cy + 1 - c]
            _remote(blk, blk, send_sems.at[k], recv_sems.at[k], sibling).wait_recv()
        for cp in copies:
            cp.wait_send()

    sem = pltpu.SemaphoreType.DMA((3,))
    return pl.pallas_call(body, name=f"allgather_small_forward_{tag}", in_specs=[ANY], out_specs=ANY,
                          out_shape=SDS(land.shape, land.dtype), input_output_aliases={0: 0},
                          scratch_shapes=[sem, sem])(land)


def _sum_devices(tag, gathered, mine, place):
    _, r, cols = gathered.shape
    tr = _row_tile(r, cols, budget=256 * 1024)

    def body(place_ref, g_ref, x_ref, o_ref):
        me = 2 * place_ref[0] + place_ref[1]
        acc = jnp.where(me == 0, x_ref[...], g_ref[0])
        for k in range(1, 8):
            acc = acc + jnp.where(me == k, x_ref[...], g_ref[k])
        o_ref[...] = acc

    return pl.pallas_call(
        body, name=f"sum_small_grads_{tag}",
        grid_spec=pltpu.PrefetchScalarGridSpec(
            num_scalar_prefetch=1, grid=(r // tr,),
            in_specs=[BS((8, tr, cols), lambda t, pr: (0, t, 0)), BS((tr, cols), lambda t, pr: (t, 0))],
            out_specs=BS((tr, cols), lambda t, pr: (t, 0))),
        out_shape=SDS((r, cols), F32))(place, gathered, mine)


def _adamw_values(w, g, m, v):
    m = ADAM_B1 * m + (1.0 - ADAM_B1) * g
    v = ADAM_B2 * v + (1.0 - ADAM_B2) * (g * g)
    m_hat = m / (1.0 - ADAM_B1 ** ADAM_STEP)
    v_hat = v / (1.0 - ADAM_B2 ** ADAM_STEP)
    delta = -ADAM_LR * (m_hat / (jnp.sqrt(v_hat) + ADAM_EPS) + ADAM_WD * w)
    return delta, m, v


def _adamw_big(name, l, w, m, v, g, earlier=None):
    nl, r, cols = w.shape
    tr = _row_tile(r, cols, budget=1024 * 1024)
    nt = r // tr
    n_prev = 0 if earlier is None else 4

    def body(*refs):
        w_ref, m_ref, v_ref, g_ref = refs[:4]
        go_ref, d_ref, mo_ref, vo_ref = refs[4 + n_prev:]
        gv = g_ref[0].astype(F32) + g_ref[1].astype(F32)
        delta, m_new, v_new = _adamw_values(w_ref[...], gv, m_ref[...], v_ref[...])
        go_ref[...] = gv
        d_ref[...] = delta
        mo_ref[...] = m_new
        vo_ref[...] = v_new

    layer = BS((None, tr, cols), lambda t: (l, t, 0))
    return pl.pallas_call(
        body, name=f"adamw_{name}_l{l}", grid=(nt,),
        in_specs=[layer, layer, layer, BS((2, tr, cols), lambda t: (0, t, 0))] + [ANY] * n_prev,
        out_specs=[layer] * 4, out_shape=[SDS(w.shape, F32)] * 4,
        input_output_aliases={4 + i: i for i in range(n_prev)}, compiler_params=_params(),
    )(w, m, v, g, *(earlier or ()))


def _adamw_rows(w, m, v, g):
    r, cols = w.shape
    tr = _row_tile(r, cols, budget=512 * 1024)

    def body(w_ref, m_ref, v_ref, g_ref, d_ref, mo_ref, vo_ref):
        delta, m_new, v_new = _adamw_values(w_ref[...], g_ref[...], m_ref[...], v_ref[...])
        d_ref[...] = delta
        mo_ref[...] = m_new
        vo_ref[...] = v_new

    spec = BS((tr, cols), lambda t: (t, 0))
    return pl.pallas_call(body, name="adamw_small", grid=(r // tr,), in_specs=[spec] * 4, out_specs=[spec] * 3,
                          out_shape=[SDS(w.shape, F32)] * 3)(w, m, v, g)


PACK_ALIGN = 8 * 128


def _pack_rows(arrays):
    parts = []
    for a in arrays:
        flat = a.reshape(-1)
        pad = (-flat.shape[0]) % PACK_ALIGN
        if pad:
            flat = jnp.pad(flat, (0, pad))
        parts.append(flat.reshape(-1, 128))
    return jnp.concatenate(parts, axis=0)


def _unpack_rows(buf, shapes):
    out, row = [], 0
    for shape in shapes:
        size = math.prod(shape)
        rows = -(-size // PACK_ALIGN) * (PACK_ALIGN // 128)
        out.append(buf[row:row + rows].reshape(-1)[:size].reshape(shape))
        row += rows
    return out


def kernel(x, norm1, w_in, b_gate, ssm_a_re, ssm_a_im, ssm_log_dt, ssm_b_re, ssm_b_im, ssm_c_re, ssm_c_im, ssm_d, ssm_w_glu, ssm_b_glu, ssm_w_proj, conv_w_dw, conv_b_dw, conv_ln_g, conv_ln_b, conv_w_proj, pool_w_group, pool_scale, pool_w_proj, w_out, norm2, ffn_w_gate, ffn_w_up, ffn_w_down, final_norm, loss_target, m_norm1, m_w_in, m_b_gate, m_ssm_a_re, m_ssm_a_im, m_ssm_log_dt, m_ssm_b_re, m_ssm_b_im, m_ssm_c_re, m_ssm_c_im, m_ssm_d, m_ssm_w_glu, m_ssm_b_glu, m_ssm_w_proj, m_conv_w_dw, m_conv_b_dw, m_conv_ln_g, m_conv_ln_b, m_conv_w_proj, m_pool_w_group, m_pool_scale, m_pool_w_proj, m_w_out, m_norm2, m_ffn_w_gate, m_ffn_w_up, m_ffn_w_down, m_final_norm, v_norm1, v_w_in, v_b_gate, v_ssm_a_re, v_ssm_a_im, v_ssm_log_dt, v_ssm_b_re, v_ssm_b_im, v_ssm_c_re, v_ssm_c_im, v_ssm_d, v_ssm_w_glu, v_ssm_b_glu, v_ssm_w_proj, v_conv_w_dw, v_conv_b_dw, v_conv_ln_g, v_conv_ln_b, v_conv_w_proj, v_pool_w_group, v_pool_scale, v_pool_w_proj, v_w_out, v_norm2, v_ffn_w_gate, v_ffn_w_up, v_ffn_w_down, v_final_norm):
    given = dict(locals())
    prm = {n: given[n] for n in WEIGHTS}
    mom = {n: given["m_" + n] for n in WEIGHTS}
    var = {n: given["v_" + n] for n in WEIGHTS}
    cx, cy, cc = _place()
    place = jnp.stack([2 * cx + cy, cc]).astype(jnp.int32)

    def kernel_view(n, a):
        return a.transpose(0, 2, 1) if n in TRANSPOSED else a

    dw_shard = prm["conv_w_dw"].reshape(N_LAYERS, CONV_KERNEL, -1)
    casts = {"w_in": _cast_into("w_in", prm["w_in"], place, MXU_DTYPE)}
    in_flight = {(0, "in"): _allgather_start("first", [[casts["w_in"][0]]])[0]}
    casts.update({n: _cast_into(n, kernel_view(n, prm[n]), place, MXU_DTYPE) for n in BIG if n != "w_in"})
    casts["conv_w_dw"] = _cast_into("conv_w_dw", dw_shard, place, F32)
    order = [(l, g) for l in range(N_LAYERS) for g in GATHER_GROUPS if (l, g) != (0, "in")]
    in_flight.update(zip(order, _allgather_start("rest", [[casts[n][l] for n in GATHER_GROUPS[g]] for l, g in order])))

    def weights_of(l, group, after):
        send_sems, recv_sems, bufs = in_flight[l, group]
        tag = f"l{l}_{group}"
        bufs = _allgather_forward(tag, _allgather_wait(tag, send_sems, recv_sems, bufs, after))
        fw = dict(zip(GATHER_GROUPS[group], bufs))
        if "conv_w_dw" in fw:
            fw["conv_w_dw"] = fw["conv_w_dw"].transpose(1, 0, 2).reshape(CONV_KERNEL, -1)
        return fw

    pending, small_pending, small_shapes = {}, {}, {}
    last_token = []

    def on_grads(l, group, grads):
        if group == "small":
            small_shapes[l] = {n: g.shape for n, g in grads.items()}
            begun = _allgather_rows_start(f"l{l}", _pack_rows(list(grads.values())))
            small_pending[l], token = begun[:4], begun[4]
        else:
            pending[l, group], token = _reduce_start(f"{l}_{group}", grads)
        last_token[:] = [token]
        return token[0, 0]

    loss, dx, _, _, _ = _local_step(x[0], loss_target[0], weights_of, prm, on_grads)
    loss = lax.psum(loss, ("x", "y", "c"))

    reduced = [{} for _ in range(N_LAYERS)]
    out = {}

    def finish(l, group, after):
        reduced[l].update(_reduce_finish(pending[l, group], place, after))

    def adamw(l, names):
        for n in names:
            out[n] = _adamw_big(n, l, kernel_view(n, prm[n]), kernel_view(n, mom[n]), kernel_view(n, var[n]),
                                reduced[l][n], out.get(n))
        return out[names[-1]][0]

    top = N_LAYERS - 1
    for group in ("ffn", "mixer", "in"):
        finish(top, group, last_token[0])
    done = adamw(top, BIG)
    for group in ("ffn", "mixer", "in"):
        finish(0, group, done)
        done = adamw(0, [n for n in GATHER_GROUPS[group] if n in BIG])
    for n in BIG:
        out[n] = tuple(kernel_view(n, a) for a in out[n])

    gsmall = {}
    for l in range(N_LAYERS):
        rows, land = _allgather_rows_wait(f"l{l}", *small_pending[l], done)
        gsum = _sum_devices(f"l{l}", _allgather_rows_forward(f"l{l}", land), rows, place)
        for n, g in zip(small_shapes[l], _unpack_rows(gsum, list(small_shapes[l].values()))):
            gsmall.setdefault(n, [None] * N_LAYERS)[l] = g
    gsmall = {n: (g[top] if n == "final_norm" else jnp.stack(g)) for n, g in gsmall.items()}
    lanes = dw_shard.shape[-1]
    gsmall["conv_w_dw"] = lax.dynamic_slice_in_dim(gsmall["conv_w_dw"], (2 * cx + cy) * lanes, lanes, axis=2)
    small_names = list(SMALL) + ["conv_w_dw"]
    w_rows = _pack_rows([prm[n] for n in small_names])
    m_rows = _pack_rows([mom[n] for n in small_names])
    v_rows = _pack_rows([var[n] for n in small_names])
    g_rows = _pack_rows([gsmall[n] for n in small_names])
    shapes = [prm[n].shape for n in small_names]
    d_s, m_s, v_s = (_unpack_rows(r, shapes) for r in _adamw_rows(w_rows, m_rows, v_rows, g_rows))
    for i, n in enumerate(small_names):
        out[n] = (gsmall[n].reshape(prm[n].shape), d_s[i], m_s[i], v_s[i])
    grads = [out[n][0] for n in WEIGHTS]
    deltas = [out[n][1] for n in WEIGHTS]
    new_m = [out[n][2] for n in WEIGHTS]
    new_v = [out[n][3] for n in WEIGHTS]
    return (loss, dx[None], *grads, *deltas, *new_m, *new_v)
```

```python
import functools
import math

import jax
import jax.numpy as jnp
from jax import lax
from jax.experimental import pallas as pl
from jax.experimental.pallas import tpu as pltpu

F32 = jnp.float32
MXU_DTYPE = jnp.bfloat16
WIRE_DTYPE = jnp.bfloat16
SDS = jax.ShapeDtypeStruct
BS = pl.BlockSpec
ANY = pl.BlockSpec(memory_space=pl.ANY)
HBM = pl.BlockSpec(memory_space=pltpu.HBM)
SEM = pl.BlockSpec(memory_space=pltpu.SEMAPHORE)
SIDE_EFFECT = pltpu.SideEffectType.DATAFLOW_SIDE_EFFECTING
MESH = pl.DeviceIdType.MESH

EPS = 1e-6
N_CHIPS = 4
N_LAYERS = 2
SSM_GROUPS, SSM_STATE, SSM_GROUP = 32, 64, 16
CONV_KERNEL = 31
CONV_PAD = 32
POOL_WINDOWS = (2, 4, 8, 16)
GELU_C = math.sqrt(2.0 / math.pi)
ADAM_LR, ADAM_B1, ADAM_B2, ADAM_EPS, ADAM_WD, ADAM_STEP = 0.001, 0.9, 0.999, 1e-08, 0.01, 10
VMEM_LIMIT = 56 * 1024 * 1024

BIG = ("w_in", "ssm_w_glu", "ssm_w_proj", "conv_w_proj", "pool_w_proj", "w_out", "ffn_w_gate", "ffn_w_up", "ffn_w_down")
TRANSPOSED = ("ffn_w_gate", "ffn_w_up")
GATHER_GROUPS = {
    "in": ("w_in",),
    "mixer": ("ssm_w_glu", "ssm_w_proj", "conv_w_proj", "pool_w_proj", "w_out", "conv_w_dw"),
    "ffn": ("ffn_w_gate", "ffn_w_up", "ffn_w_down"),
}
SMALL = ("norm1", "b_gate", "ssm_a_re", "ssm_a_im", "ssm_log_dt", "ssm_b_re", "ssm_b_im", "ssm_c_re", "ssm_c_im",
         "ssm_d", "ssm_b_glu", "conv_b_dw", "conv_ln_g", "conv_ln_b", "pool_w_group", "pool_scale", "norm2",
         "final_norm")
WEIGHTS = ("norm1", "w_in", "b_gate", "ssm_a_re", "ssm_a_im", "ssm_log_dt", "ssm_b_re", "ssm_b_im", "ssm_c_re",
           "ssm_c_im", "ssm_d", "ssm_w_glu", "ssm_b_glu", "ssm_w_proj", "conv_w_dw", "conv_b_dw", "conv_ln_g",
           "conv_ln_b", "conv_w_proj", "pool_w_group", "pool_scale", "pool_w_proj", "w_out", "norm2", "ffn_w_gate",
           "ffn_w_up", "ffn_w_down", "final_norm")


def _params(vmem=True):
    return pltpu.CompilerParams(vmem_limit_bytes=VMEM_LIMIT) if vmem else None


def _mm(a, b):
    return jnp.dot(a.astype(MXU_DTYPE), b.astype(MXU_DTYPE), preferred_element_type=F32)


def _mm_nt(a, b):
    return lax.dot_general(a.astype(MXU_DTYPE), b.astype(MXU_DTYPE), (((1,), (1,)), ((), ())),
                           preferred_element_type=F32)


def _mm_tn(a, b):
    return lax.dot_general(a.astype(MXU_DTYPE), b.astype(MXU_DTYPE), (((0,), (0,)), ((), ())),
                           preferred_element_type=F32)


def _sigmoid(x):
    return jax.nn.sigmoid(x)


def _gelu(x):
    t = jnp.tanh(GELU_C * (x + 0.044715 * (x * x * x)))
    return x * (0.5 * (1.0 + t)), t


def _gelu_grad(x, t):
    return 0.5 * (1.0 + t) + 0.5 * x * (1.0 - t * t) * (GELU_C * (1.0 + 3.0 * 0.044715 * x * x))


def _colsum(v):
    return jnp.sum(v, axis=0, keepdims=True)


def _row_tile(rows, cols, itemsize=4, budget=1536 * 1024):
    best = None
    for t in range(8, rows + 1, 8):
        if rows % t == 0 and t * cols * itemsize <= budget:
            best = t
    return best if best is not None else rows


def _in_proj(l, x, norm1, w_in):
    s, d = x.shape
    nc = w_in.shape[-1]
    tm = min(512, s)

    def body(x_ref, g_ref, w_ref, z_ref, h_ref):
        @pl.when(pl.program_id(1) == 0)
        def _():
            xv = x_ref[...]
            r = lax.rsqrt(jnp.mean(xv * xv, axis=-1, keepdims=True) + EPS)
            h_ref[...] = (xv * r * g_ref[...]).astype(h_ref.dtype)

        z_ref[...] = _mm(h_ref[...], w_ref[...])

    return pl.pallas_call(
        body, name=f"in_proj_l{l}", grid=(s // tm, N_CHIPS),
        in_specs=[BS((tm, d), lambda i, j: (i, 0)), BS((None, 1, d), lambda i, j: (l, 0, 0)),
                  BS((None, d, nc), lambda i, j: (j, 0, 0))],
        out_specs=[BS((tm, nc), lambda i, j: (i, j)), BS((tm, d), lambda i, j: (i, 0))],
        out_shape=[SDS((s, N_CHIPS * nc), F32), SDS((s, d), MXU_DTYPE)],
        compiler_params=_params())(x, norm1, w_in)


def _mm_cols(a, w_ref):
    return jnp.concatenate([_mm(a, w_ref[j]) for j in range(N_CHIPS)], axis=1)


def _mm_nt_cols(dv, w_ref):
    nc = w_ref.shape[-1]
    acc = _mm_nt(dv[:, 0:nc], w_ref[0])
    for j in range(1, N_CHIPS):
        acc = acc + _mm_nt(dv[:, j * nc:(j + 1) * nc], w_ref[j])
    return acc


def _merge_values(y, hc, p, zg, wglu, bglu, wpa, wpb, wpc, lng, lnb, wgrp, scale, bg):
    v = {}
    ge, th = _gelu(y)
    t = _mm(ge, wglu) + bglu
    sg = _sigmoid(t)
    sa = ge * sg
    ya = _mm_cols(sa, wpa)
    mu = jnp.mean(hc, axis=-1, keepdims=True)
    xc = hc - mu
    r = lax.rsqrt(jnp.mean(xc * xc, axis=-1, keepdims=True) + EPS)
    xh = xc * r
    ln = xh * lng + lnb
    sl = _sigmoid(ln)
    ac = ln * sl
    yb = _mm_cols(ac, wpb)
    gw = p.shape[1] // len(POOL_WINDOWS)
    q = jnp.concatenate([_mm(p[:, k * gw:(k + 1) * gw], wgrp[k]) for k in range(len(POOL_WINDOWS))], axis=1)
    pp = q * scale
    yc = _mm_cols(pp, wpc)
    d = ya.shape[1]
    gates = [_sigmoid(zg[k] + bg[:, k * d:(k + 1) * d]) for k in range(3)]
    merged = gates[0] * ya + gates[1] * yb + gates[2] * yc
    v.update(ge=ge, th=th, sg=sg, sa=sa, ya=ya, r=r, xh=xh, ln=ln, sl=sl, ac=ac, yb=yb, q=q, pp=pp, yc=yc,
             gates=gates, merged=merged)
    return v


def _merge_specs(l, tm, d, cw):
    row = lambda n: BS((None, 1, n), lambda i: (l, 0, 0))
    return [
        BS((tm, cw), lambda i: (i, 0)),
        BS((tm, cw), lambda i: (i, 0)),
        BS((tm, cw), lambda i: (i, 0)),
        BS((tm, d), lambda i: (i, 2)), BS((tm, d), lambda i: (i, 3)), BS((tm, d), lambda i: (i, 4)),
        BS((N_CHIPS, cw // N_CHIPS, cw), lambda i: (0, 0, 0)),
        row(cw),
        BS((N_CHIPS, cw, d // N_CHIPS), lambda i: (0, 0, 0)),
        BS((N_CHIPS, cw, d // N_CHIPS), lambda i: (0, 0, 0)),
        BS((N_CHIPS, cw, d // N_CHIPS), lambda i: (0, 0, 0)),
        row(cw), row(cw),
        BS((None, 4, cw // 4, cw // 4), lambda i: (l, 0, 0, 0)),
        row(cw),
        row(3 * d),
        BS((N_CHIPS, d // N_CHIPS, d), lambda i: (0, 0, 0)),
    ]


def _merge_fwd(l, x, y, hc, p, z, fw, sp):
    s, d = x.shape
    cw = y.shape[1]
    tm = min(256, s)

    def body(x_ref, y_ref, hc_ref, p_ref, z0, z1, z2, wglu, bglu, wpa, wpb, wpc, lng, lnb, wgrp, scale, bg, wout,
             x1_ref):
        v = _merge_values(y_ref[...], hc_ref[...], p_ref[...], (z0[...], z1[...], z2[...]),
                          wglu[...].reshape(cw, cw), bglu[...], wpa, wpb, wpc, lng[...], lnb[...], wgrp, scale[...],
                          bg[...])
        x1_ref[...] = x_ref[...] + _mm(v["merged"], wout[...].reshape(d, d))

    return pl.pallas_call(
        body, name=f"merge_fwd_l{l}", grid=(s // tm,),
        in_specs=[BS((tm, d), lambda i: (i, 0))] + _merge_specs(l, tm, d, cw),
        out_specs=BS((tm, d), lambda i: (i, 0)), out_shape=SDS((s, d), F32), compiler_params=_params(),
    )(x, y, hc, p, z, z, z, fw["ssm_w_glu"], sp["ssm_b_glu"], fw["ssm_w_proj"], fw["conv_w_proj"], fw["pool_w_proj"],
      sp["conv_ln_g"], sp["conv_ln_b"], sp["pool_w_group"], sp["pool_scale"], sp["b_gate"], fw["w_out"])


def _merge_bwd(l, dx1, y, hc, p, z, fw, sp):
    s, d = dx1.shape
    cw = y.shape[1]
    tm = min(256, s)
    m = MXU_DTYPE

    def body(dx1_ref, y_ref, hc_ref, p_ref, z0, z1, z2, wglu, bglu, wpa, wpb, wpc, lng, lnb, wgrp, scale, bg, wout,
             dzg_ref, dy_ref, dhc_ref, dp_ref, merged_ref, sa_ref, ac_ref, pp_ref, ge_ref, dt_ref, dya_ref, dyb_ref,
             dyc_ref, dq_ref, dbg_ref, dbglu_ref, dlng_ref, dlnb_ref, dscale_ref):
        yv = y_ref[...]
        wg = wglu[...].reshape(cw, cw)
        v = _merge_values(yv, hc_ref[...], p_ref[...], (z0[...], z1[...], z2[...]), wg, bglu[...], wpa, wpb, wpc,
                          lng[...], lnb[...], wgrp, scale[...], bg[...])
        dm = _mm_nt(dx1_ref[...], wout[...].reshape(d, d))
        ys = (v["ya"], v["yb"], v["yc"])
        dys = []
        for k in range(3):
            gk = v["gates"][k]
            dzg_ref[:, k * d:(k + 1) * d] = dm * ys[k] * (gk * (1.0 - gk))
            dys.append((dm * gk).astype(m))
        dsa = _mm_nt_cols(dys[0], wpa)
        dac = _mm_nt_cols(dys[1], wpb)
        dpp = _mm_nt_cols(dys[2], wpc)
        ge, sg = v["ge"], v["sg"]
        dt = dsa * ge * (sg * (1.0 - sg))
        dge = dsa * sg + _mm_nt(dt, wg)
        dy_ref[...] = dge * _gelu_grad(yv, v["th"])
        ln, sl, xh = v["ln"], v["sl"], v["xh"]
        dln = dac * (sl * (1.0 + ln * (1.0 - sl)))
        dxh = dln * lng[...]
        dhc_ref[...] = v["r"] * (dxh - jnp.mean(dxh, axis=-1, keepdims=True)
                                 - xh * jnp.mean(dxh * xh, axis=-1, keepdims=True))
        dq = dpp * scale[...]
        gw = cw // len(POOL_WINDOWS)
        for k in range(len(POOL_WINDOWS)):
            dp_ref[:, k * gw:(k + 1) * gw] = _mm_nt(dq[:, k * gw:(k + 1) * gw], wgrp[k])
        merged_ref[...] = v["merged"].astype(m)
        sa_ref[...] = v["sa"].astype(m)
        ac_ref[...] = v["ac"].astype(m)
        pp_ref[...] = v["pp"].astype(m)
        ge_ref[...] = ge.astype(m)
        dt_ref[...] = dt.astype(m)
        dya_ref[...] = dys[0]
        dyb_ref[...] = dys[1]
        dyc_ref[...] = dys[2]
        dq_ref[...] = dq.astype(m)

        @pl.when(pl.program_id(0) == 0)
        def _():
            for ref in (dbg_ref, dbglu_ref, dlng_ref, dlnb_ref, dscale_ref):
                ref[...] = jnp.zeros(ref.shape, F32)

        dbg_ref[...] += _colsum(dzg_ref[...])
        dbglu_ref[...] += _colsum(dt)
        dlng_ref[...] += _colsum(dln * xh)
        dlnb_ref[...] += _colsum(dln)
        dscale_ref[...] += _colsum(dpp * v["q"])

    tile = lambda n: BS((tm, n), lambda i: (i, 0))
    acc = lambda n: BS((1, n), lambda i: (0, 0))
    outs = pl.pallas_call(
        body, name=f"merge_bwd_l{l}", grid=(s // tm,),
        in_specs=[tile(d)] + _merge_specs(l, tm, d, cw),
        out_specs=[tile(3 * d), tile(cw), tile(cw), tile(cw), tile(d), tile(cw), tile(cw), tile(cw), tile(cw), tile(cw),
                   tile(d), tile(d), tile(d), tile(cw), acc(3 * d), acc(cw), acc(cw), acc(cw), acc(cw)],
        out_shape=[SDS((s, 3 * d), F32), SDS((s, cw), F32), SDS((s, cw), F32), SDS((s, cw), F32), SDS((s, d), m),
                   SDS((s, cw), m), SDS((s, cw), m), SDS((s, cw), m), SDS((s, cw), m), SDS((s, cw), m), SDS((s, d), m),
                   SDS((s, d), m), SDS((s, d), m), SDS((s, cw), m), SDS((1, 3 * d), F32), SDS((1, cw), F32),
                   SDS((1, cw), F32), SDS((1, cw), F32), SDS((1, cw), F32)],
        compiler_params=_params(),
    )(dx1, y, hc, p, z, z, z, fw["ssm_w_glu"], sp["ssm_b_glu"], fw["ssm_w_proj"], fw["conv_w_proj"], fw["pool_w_proj"],
      sp["conv_ln_g"], sp["conv_ln_b"], sp["pool_w_group"], sp["pool_scale"], sp["b_gate"], fw["w_out"])
    names = ("dzg", "dy", "dhc", "dp", "merged", "sa", "ac", "pp", "ge", "dt", "dya", "dyb", "dyc", "dq", "db_gate",
             "db_glu", "dln_g", "dln_b", "dscale")
    return dict(zip(names, outs))


def _ffn_fwd(l, x1, norm2, wg, wu, wd):
    s, d = x1.shape
    hc = wd.shape[1]
    tm = min(512, s)

    def body(x_ref, g_ref, wg_ref, wu_ref, wd_ref, o_ref, h_scr):
        @pl.when(pl.program_id(1) == 0)
        def _():
            xv = x_ref[...]
            r = lax.rsqrt(jnp.mean(xv * xv, axis=-1, keepdims=True) + EPS)
            h_scr[...] = (xv * r * g_ref[...]).astype(h_scr.dtype)
            o_ref[...] = xv

        h = h_scr[...]
        gate = _mm_nt(h, wg_ref[...])
        up = _mm_nt(h, wu_ref[...])
        o_ref[...] += _mm(gate * _sigmoid(gate) * up, wd_ref[...])

    return pl.pallas_call(
        body, name=f"ffn_fwd_l{l}", grid=(s // tm, N_CHIPS),
        in_specs=[BS((tm, d), lambda i, j: (i, 0)), BS((None, 1, d), lambda i, j: (l, 0, 0)),
                  BS((None, hc, d), lambda i, j: (j, 0, 0)), BS((None, hc, d), lambda i, j: (j, 0, 0)),
                  BS((None, hc, d), lambda i, j: (j, 0, 0))],
        out_specs=BS((tm, d), lambda i, j: (i, 0)), out_shape=SDS((s, d), F32),
        scratch_shapes=[pltpu.VMEM((tm, d), MXU_DTYPE)], compiler_params=_params())(x1, norm2, wg, wu, wd)


def _ffn_bwd(l, x1, dx2, norm2, wg, wu, wd):
    s, d = x1.shape
    hc = wd.shape[1]
    tm = min(512, s)
    m = MXU_DTYPE
    last = N_CHIPS - 1

    def body(x_ref, dx2_ref, g_ref, wg_ref, wu_ref, wd_ref, dx1_ref, h_ref, act_ref, dgate_ref, dup_ref, dn_ref,
             dh_scr, dxb_scr):
        i, j = pl.program_id(0), pl.program_id(1)

        @pl.when(j == 0)
        def _():
            xv = x_ref[...]
            r = lax.rsqrt(jnp.mean(xv * xv, axis=-1, keepdims=True) + EPS)
            h_ref[...] = (xv * r * g_ref[...]).astype(m)
            dxb_scr[...] = dx2_ref[...].astype(m)
            dh_scr[...] = jnp.zeros(dh_scr.shape, F32)

        @pl.when((i == 0) & (j == 0))
        def _():
            dn_ref[...] = jnp.zeros(dn_ref.shape, F32)

        h = h_ref[...]
        gate = _mm_nt(h, wg_ref[...])
        up = _mm_nt(h, wu_ref[...])
        sg = _sigmoid(gate)
        silu = gate * sg
        act_ref[...] = (silu * up).astype(m)
        dact = _mm_nt(dxb_scr[...], wd_ref[...])
        dup = (dact * silu).astype(m)
        dgate = (dact * up * (sg * (1.0 + gate * (1.0 - sg)))).astype(m)
        dup_ref[...] = dup
        dgate_ref[...] = dgate
        dh_scr[...] += _mm(dgate, wg_ref[...]) + _mm(dup, wu_ref[...])

        @pl.when(j == last)
        def _():
            xv = x_ref[...]
            r = lax.rsqrt(jnp.mean(xv * xv, axis=-1, keepdims=True) + EPS)
            xh = xv * r
            dh = dh_scr[...]
            dn_ref[...] += _colsum(dh * xh)
            dxh = dh * g_ref[...]
            dx1_ref[...] = dx2_ref[...] + r * (dxh - xh * jnp.mean(dxh * xh, axis=-1, keepdims=True))

    chunk = BS((None, tm, hc), lambda i, j: (j, i, 0))
    outs = pl.pallas_call(
        body, name=f"ffn_bwd_l{l}", grid=(s // tm, N_CHIPS),
        in_specs=[BS((tm, d), lambda i, j: (i, 0)), BS((tm, d), lambda i, j: (i, 0)),
                  BS((None, 1, d), lambda i, j: (l, 0, 0)),
                  BS((None, hc, d), lambda i, j: (j, 0, 0)), BS((None, hc, d), lambda i, j: (j, 0, 0)),
                  BS((None, hc, d), lambda i, j: (j, 0, 0))],
        out_specs=[BS((tm, d), lambda i, j: (i, 0)), BS((tm, d), lambda i, j: (i, 0)), chunk, chunk, chunk,
                   BS((1, d), lambda i, j: (0, 0))],
        out_shape=[SDS((s, d), F32), SDS((s, d), m), SDS((N_CHIPS, s, hc), m), SDS((N_CHIPS, s, hc), m),
                   SDS((N_CHIPS, s, hc), m), SDS((1, d), F32)],
        scratch_shapes=[pltpu.VMEM((tm, d), F32), pltpu.VMEM((tm, d), m)], compiler_params=_params(),
    )(x1, dx2, norm2, wg, wu, wd)
    return dict(zip(("dx1", "h2", "act", "dgate", "dup", "dnorm2"), outs))


def _loss_head(x, target, gf):
    s, d = x.shape
    tm = min(512, s)

    def body(x_ref, t_ref, g_ref, dx_ref, loss_ref, dg_ref):
        @pl.when(pl.program_id(0) == 0)
        def _():
            loss_ref[...] = jnp.zeros(loss_ref.shape, F32)
            dg_ref[...] = jnp.zeros(dg_ref.shape, F32)

        xv = x_ref[...]
        r = lax.rsqrt(jnp.mean(xv * xv, axis=-1, keepdims=True) + EPS)
        xh = xv * r
        err = xh * g_ref[...] - t_ref[...]
        loss_ref[...] += 0.5 * jnp.sum(jnp.mean(err * err, axis=-1, keepdims=True), axis=0, keepdims=True)
        dyv = err * (1.0 / d)
        dg_ref[...] += _colsum(dyv * xh)
        dxh = dyv * g_ref[...]
        dx_ref[...] = r * (dxh - xh * jnp.mean(dxh * xh, axis=-1, keepdims=True))

    return pl.pallas_call(
        body, name="loss_head", grid=(s // tm,),
        in_specs=[BS((tm, d), lambda i: (i, 0)), BS((tm, d), lambda i: (i, 0)), BS((1, d), lambda i: (0, 0))],
        out_specs=[BS((tm, d), lambda i: (i, 0)), BS((1, 1), lambda i: (0, 0)), BS((1, d), lambda i: (0, 0))],
        out_shape=[SDS((s, d), F32), SDS((1, 1), F32), SDS((1, d), F32)], compiler_params=_params())(x, target, gf)


def _in_proj_bwd(l, dres, x, norm1, w_in, du_a, dv1, dv2, du_c, dzg):
    s, d = x.shape
    nc = w_in.shape[-1]
    tm = min(256, s)
    m = MXU_DTYPE

    def body(dres_ref, x_ref, g_ref, w_ref, a_ref, b1_ref, b2_ref, c_ref, g3_ref, dx_ref, dz_ref, dn_ref):
        @pl.when(pl.program_id(0) == 0)
        def _():
            dn_ref[...] = jnp.zeros(dn_ref.shape, F32)

        dz = jnp.concatenate([a_ref[...], b1_ref[...], b2_ref[...], c_ref[...], g3_ref[...]], axis=1).astype(m)
        dz_ref[...] = dz
        dh = _mm_nt_cols(dz, w_ref)
        xv = x_ref[...]
        r = lax.rsqrt(jnp.mean(xv * xv, axis=-1, keepdims=True) + EPS)
        xh = xv * r
        dn_ref[...] += _colsum(dh * xh)
        dxh = dh * g_ref[...]
        dx_ref[...] = dres_ref[...] + r * (dxh - xh * jnp.mean(dxh * xh, axis=-1, keepdims=True))

    tile = lambda n: BS((tm, n), lambda i: (i, 0))
    return pl.pallas_call(
        body, name=f"in_proj_bwd_l{l}", grid=(s // tm,),
        in_specs=[tile(d), tile(d), BS((None, 1, d), lambda i: (l, 0, 0)),
                  BS((N_CHIPS, d, nc), lambda i: (0, 0, 0)),
                  tile(du_a.shape[1]), tile(dv1.shape[1]), tile(dv2.shape[1]), tile(du_c.shape[1]), tile(dzg.shape[1])],
        out_specs=[tile(d), tile(N_CHIPS * nc), BS((1, d), lambda i: (0, 0))],
        out_shape=[SDS((s, d), F32), SDS((s, N_CHIPS * nc), m), SDS((1, d), F32)], compiler_params=_params(),
    )(dres, x, norm1, w_in, du_a, dv1, dv2, du_c, dzg)


def _tn_matmul(name, a, a_spec, b, b_spec, out_shape, out_spec, grid, wire=True):
    last = grid[1] - 1

    def body(a_ref, b_ref, o_ref, *wire_ref):
        @pl.when(pl.program_id(1) == 0)
        def _():
            o_ref[...] = jnp.zeros(o_ref.shape, F32)

        o_ref[...] += _mm_tn(a_ref[...], b_ref[...])

        if wire:
            @pl.when(pl.program_id(1) == last)
            def _():
                wire_ref[0][...] = o_ref[...].astype(WIRE_DTYPE)

    if not wire:
        return pl.pallas_call(body, name=name, grid=grid, in_specs=[a_spec, b_spec], out_specs=out_spec,
                              out_shape=out_shape, compiler_params=_params())(a, b)
    return pl.pallas_call(body, name=name, grid=grid, in_specs=[a_spec, b_spec], out_specs=[out_spec, out_spec],
                          out_shape=[out_shape, SDS(out_shape.shape, WIRE_DTYPE)], compiler_params=_params())(a, b)


def _scan_consts(pw_ref, lanes, reverse):
    sgn = -1.0 if reverse else 1.0
    steps = [(k, pw_ref[2 * i], sgn * pw_ref[2 * i + 1]) for i, k in enumerate((1, 2, 4))]
    c = 4 if reverse else 3
    return steps, pw_ref[2 * c], sgn * pw_ref[2 * c + 1]


def _scan_block(br, bi, steps, row, reverse):
    for k, ar, ai in steps:
        if reverse:
            mask, sh = row < 8 - k, 8 - k
        else:
            mask, sh = row >= k, k
        sr = jnp.where(mask, pltpu.roll(br, sh, 0), 0.0)
        si = jnp.where(mask, pltpu.roll(bi, sh, 0), 0.0)
        br, bi = br + ar * sr - ai * si, bi + ar * si + ai * sr
    return br, bi


def _ssm_fwd(l, z, bblk_re, bblk_im, cblk_re, cblk_im, pw, dskip):
    s = z.shape[0]
    gc = bblk_re.shape[1]
    gl = bblk_re.shape[2]
    nblk = bblk_re.shape[0]

    def body(u_ref, bre, bim, cre, cim, pw_ref, d_ref, hre, him, y_ref):
        u = u_ref[...]
        hre[...] = _mm(u, bre[...])
        him[...] = _mm(u, bim[...])
        row = lax.broadcasted_iota(jnp.int32, (8, gl), 0)
        steps, car, cai = _scan_consts(pw_ref, gl, False)

        def step(i, carry):
            cr, ci = carry
            r0 = pl.multiple_of(i * 8, 8)
            br, bi = _scan_block(hre[pl.ds(r0, 8), :], him[pl.ds(r0, 8), :], steps, row, False)
            hr = br + car * cr - cai * ci
            hi = bi + car * ci + cai * cr
            hre[pl.ds(r0, 8), :] = hr
            him[pl.ds(r0, 8), :] = hi
            return jnp.broadcast_to(hr[7:8, :], (8, gl)), jnp.broadcast_to(hi[7:8, :], (8, gl))

        zero = jnp.zeros((8, gl), F32)
        lax.fori_loop(0, s // 8, step, (zero, zero))
        y_ref[...] = _mm(hre[...], cre[...]) - _mm(him[...], cim[...]) + d_ref[...] * u

    return pl.pallas_call(
        body, name=f"ssm_fwd_l{l}", grid=(nblk,),
        in_specs=[BS((s, gc), lambda k: (0, k)), BS((None, gc, gl), lambda k: (k, 0, 0)),
                  BS((None, gc, gl), lambda k: (k, 0, 0)), BS((None, gl, gc), lambda k: (k, 0, 0)),
                  BS((None, gl, gc), lambda k: (k, 0, 0)), BS((10, 8, gl), lambda k: (0, 0, k)),
                  BS((1, gc), lambda k: (0, k))],
        out_specs=[BS((s, gl), lambda k: (0, k)), BS((s, gl), lambda k: (0, k)), BS((s, gc), lambda k: (0, k))],
        out_shape=[SDS((s, nblk * gl), F32), SDS((s, nblk * gl), F32), SDS((s, nblk * gc), F32)],
        compiler_params=_params())(z, bblk_re, bblk_im, cblk_re, cblk_im, pw, dskip)


def _ssm_bwd(l, dy, z, hre, him, bblk_re, bblk_im, cblk_re, cblk_im, pw, dskip):
    s = z.shape[0]
    nblk, gc, gl = bblk_re.shape

    def body(dy_ref, u_ref, hre_ref, him_ref, bre, bim, cre, cim, pw_ref, d_ref,
             du_ref, dbre_ref, dbim_ref, dcre_ref, dcim_ref, dar_ref, dai_ref, dd_ref, gre, gim):
        dyv = dy_ref[...]
        u = u_ref[...]
        gre[...] = _mm_nt(dyv, cre[...])
        gim[...] = -_mm_nt(dyv, cim[...])
        dcre_ref[...] = _mm_tn(hre_ref[...], dyv)
        dcim_ref[...] = -_mm_tn(him_ref[...], dyv)
        dd_ref[...] = _colsum(dyv * u)
        row = lax.broadcasted_iota(jnp.int32, (8, gl), 0)
        steps, car, cai = _scan_consts(pw_ref, gl, True)
        n8 = s // 8

        def step(ii, carry):
            cr, ci, accr, acci = carry
            i = n8 - 1 - ii
            r0 = pl.multiple_of(i * 8, 8)
            br, bi = _scan_block(gre[pl.ds(r0, 8), :], gim[pl.ds(r0, 8), :], steps, row, True)
            dr = br + car * cr - cai * ci
            di = bi + car * ci + cai * cr
            gre[pl.ds(r0, 8), :] = dr
            gim[pl.ds(r0, 8), :] = di
            rp = pl.multiple_of(jnp.maximum(i - 1, 0) * 8, 8)
            keep = jnp.where(i > 0, 1.0, 0.0)
            pr = jnp.where(row >= 1, pltpu.roll(hre_ref[pl.ds(r0, 8), :], 1, 0),
                           keep * pltpu.roll(hre_ref[pl.ds(rp, 8), :], 1, 0))
            pi = jnp.where(row >= 1, pltpu.roll(him_ref[pl.ds(r0, 8), :], 1, 0),
                           keep * pltpu.roll(him_ref[pl.ds(rp, 8), :], 1, 0))
            accr = accr + dr * pr + di * pi
            acci = acci + di * pr - dr * pi
            return (jnp.broadcast_to(dr[0:1, :], (8, gl)), jnp.broadcast_to(di[0:1, :], (8, gl)), accr, acci)

        zero = jnp.zeros((8, gl), F32)
        _, _, accr, acci = lax.fori_loop(0, n8, step, (zero, zero, zero, zero))
        dar_ref[...] = _colsum(accr)
        dai_ref[...] = _colsum(acci)
        dbr = gre[...]
        dbi = gim[...]
        du_ref[...] = dyv * d_ref[...] + _mm_nt(dbr, bre[...]) + _mm_nt(dbi, bim[...])
        dbre_ref[...] = _mm_tn(u, dbr)
        dbim_ref[...] = _mm_tn(u, dbi)

    col = lambda n: BS((s, n), lambda k: (0, k))
    blk = lambda a, b: BS((None, a, b), lambda k: (k, 0, 0))
    outs = pl.pallas_call(
        body, name=f"ssm_bwd_l{l}", grid=(nblk,),
        in_specs=[col(gc), col(gc), col(gl), col(gl), blk(gc, gl), blk(gc, gl), blk(gl, gc), blk(gl, gc),
                  BS((10, 8, gl), lambda k: (0, 0, k)), BS((1, gc), lambda k: (0, k))],
        out_specs=[col(gc), blk(gc, gl), blk(gc, gl), blk(gl, gc), blk(gl, gc), BS((1, gl), lambda k: (0, k)),
                   BS((1, gl), lambda k: (0, k)), BS((1, gc), lambda k: (0, k))],
        out_shape=[SDS((s, nblk * gc), F32), SDS((nblk, gc, gl), F32), SDS((nblk, gc, gl), F32),
                   SDS((nblk, gl, gc), F32), SDS((nblk, gl, gc), F32), SDS((1, nblk * gl), F32),
                   SDS((1, nblk * gl), F32), SDS((1, nblk * gc), F32)],
        scratch_shapes=[pltpu.VMEM((s, gl), F32), pltpu.VMEM((s, gl), F32)], compiler_params=_params(),
    )(dy, z, hre, him, bblk_re, bblk_im, cblk_re, cblk_im, pw, dskip)
    return dict(zip(("du", "dbblk_re", "dbblk_im", "dcblk_re", "dcblk_im", "dabar_re", "dabar_im", "dd"), outs))


def _conv_fwd(l, z, wdw, bdw):
    s = z.shape[0]
    cw = wdw.shape[1]
    lb = 128
    tr = min(256, s)
    off1 = cw // lb
    off2 = 2 * cw // lb

    def body(v1_ref, v2_ref, w_ref, b_ref, hc_ref, scr):
        scr[0:CONV_PAD, :] = jnp.zeros((CONV_PAD, lb), F32)
        scr[CONV_PAD:, :] = v1_ref[...] * _sigmoid(v2_ref[...])
        for t in range(s // tr):
            acc = jnp.broadcast_to(b_ref[...], (tr, lb))
            for k in range(CONV_KERNEL):
                acc = acc + w_ref[pl.ds(k, 1), :] * scr[pl.ds(t * tr + CONV_PAD - (CONV_KERNEL - 1) + k, tr), :]
            hc_ref[pl.ds(t * tr, tr), :] = acc

    return pl.pallas_call(
        body, name=f"conv_fwd_l{l}", grid=(cw // lb,),
        in_specs=[BS((s, lb), lambda k: (0, off1 + k)), BS((s, lb), lambda k: (0, off2 + k)),
                  BS((CONV_KERNEL, lb), lambda k: (0, k)), BS((1, lb), lambda k: (0, k))],
        out_specs=BS((s, lb), lambda k: (0, k)), out_shape=SDS((s, cw), F32),
        scratch_shapes=[pltpu.VMEM((s + CONV_PAD, lb), F32)], compiler_params=_params())(z, z, wdw, bdw)


def _conv_bwd(l, dhc, z, wdw):
    s = z.shape[0]
    cw = wdw.shape[1]
    lb = 128
    tr = min(256, s)
    off1 = cw // lb
    off2 = 2 * cw // lb
    nb = cw // lb

    def body(d_ref, v1_ref, v2_ref, w_ref, dv1_ref, dv2_ref, dw_ref, db_ref, hpad, dpad):
        v1 = v1_ref[...]
        sg = _sigmoid(v2_ref[...])
        dv = d_ref[...]
        hpad[0:CONV_PAD, :] = jnp.zeros((CONV_PAD, lb), F32)
        hpad[CONV_PAD:, :] = v1 * sg
        dpad[0:s, :] = dv
        dpad[s:, :] = jnp.zeros((CONV_PAD, lb), F32)
        db_ref[...] = _colsum(dv)
        dws = [jnp.zeros((1, lb), F32) for _ in range(CONV_KERNEL)]
        for t in range(s // tr):
            dt = d_ref[pl.ds(t * tr, tr), :]
            acc = jnp.zeros((tr, lb), F32)
            for k in range(CONV_KERNEL):
                acc = acc + w_ref[pl.ds(k, 1), :] * dpad[pl.ds(t * tr + (CONV_KERNEL - 1) - k, tr), :]
                dws[k] = dws[k] + _colsum(dt * hpad[pl.ds(t * tr + CONV_PAD - (CONV_KERNEL - 1) + k, tr), :])
            sgt = _sigmoid(v2_ref[pl.ds(t * tr, tr), :])
            v1t = v1_ref[pl.ds(t * tr, tr), :]
            dv1_ref[pl.ds(t * tr, tr), :] = acc * sgt
            dv2_ref[pl.ds(t * tr, tr), :] = acc * v1t * (sgt * (1.0 - sgt))
        for k in range(CONV_KERNEL):
            dw_ref[pl.ds(k, 1), :] = dws[k]

    return pl.pallas_call(
        body, name=f"conv_bwd_l{l}", grid=(nb,),
        in_specs=[BS((s, lb), lambda k: (0, k)), BS((s, lb), lambda k: (0, off1 + k)),
                  BS((s, lb), lambda k: (0, off2 + k)), BS((CONV_KERNEL, lb), lambda k: (0, k))],
        out_specs=[BS((s, lb), lambda k: (0, k)), BS((s, lb), lambda k: (0, k)),
                   BS((CONV_KERNEL, lb), lambda k: (0, k)), BS((1, lb), lambda k: (0, k))],
        out_shape=[SDS((s, cw), F32), SDS((s, cw), F32), SDS((CONV_KERNEL, cw), F32), SDS((1, cw), F32)],
        scratch_shapes=[pltpu.VMEM((s + CONV_PAD, lb), F32), pltpu.VMEM((s + CONV_PAD, lb), F32)],
        compiler_params=_params())(dhc, z, z, wdw)


def _pool_window(k):
    return jnp.where(k == 0, float(POOL_WINDOWS[0]),
                     jnp.where(k == 1, float(POOL_WINDOWS[1]),
                               jnp.where(k == 2, float(POOL_WINDOWS[2]), float(POOL_WINDOWS[3]))))


def _pool_fwd(l, z, pw_width):
    s = z.shape[0]
    lb = pw_width // len(POOL_WINDOWS)
    off = 3 * pw_width // lb

    def body(u_ref, p_ref):
        k = pl.program_id(0)
        u = u_ref[...]
        row = lax.broadcasted_iota(jnp.int32, (s, lb), 0)
        sums = [u]
        for sh in (1, 2, 4, 8):
            prev = sums[-1]
            sums.append(prev + jnp.where(row >= sh, pltpu.roll(prev, sh, 0), 0.0))
        sel = jnp.where(k == 0, sums[1], jnp.where(k == 1, sums[2], jnp.where(k == 2, sums[3], sums[4])))
        cnt = jnp.minimum((row + 1).astype(F32), _pool_window(k))
        p_ref[...] = sel / cnt - u

    return pl.pallas_call(
        body, name=f"pool_fwd_l{l}", grid=(len(POOL_WINDOWS),),
        in_specs=[BS((s, lb), lambda k: (0, off + k))], out_specs=BS((s, lb), lambda k: (0, k)),
        out_shape=SDS((s, pw_width), F32), compiler_params=_params())(z)


def _pool_bwd(l, dp):
    s, width = dp.shape
    lb = width // len(POOL_WINDOWS)

    def body(d_ref, du_ref):
        k = pl.program_id(0)
        dv = d_ref[...]
        row = lax.broadcasted_iota(jnp.int32, (s, lb), 0)
        cnt = jnp.minimum((row + 1).astype(F32), _pool_window(k))
        sums = [dv / cnt]
        for sh in (1, 2, 4, 8):
            prev = sums[-1]
            sums.append(prev + jnp.where(row < s - sh, pltpu.roll(prev, s - sh, 0), 0.0))
        sel = jnp.where(k == 0, sums[1], jnp.where(k == 1, sums[2], jnp.where(k == 2, sums[3], sums[4])))
        du_ref[...] = sel - dv

    return pl.pallas_call(
        body, name=f"pool_bwd_l{l}", grid=(len(POOL_WINDOWS),),
        in_specs=[BS((s, lb), lambda k: (0, k))], out_specs=BS((s, lb), lambda k: (0, k)),
        out_shape=SDS((s, width), F32), compiler_params=_params())(dp)


def _zoh(a_re, a_im, log_dt):
    dt = jnp.exp(log_dt)
    mag = jnp.exp(dt * a_re)
    ang = dt * a_im
    abar_re = mag * jnp.cos(ang)
    abar_im = mag * jnp.sin(ang)
    den = a_re * a_re + a_im * a_im
    nr = abar_re - 1.0
    ni = abar_im
    f_re = (nr * a_re + ni * a_im) / den
    f_im = (ni * a_re - nr * a_im) / den
    return abar_re, abar_im, f_re, f_im


def _zoh_fwd(l, a_re, a_im, log_dt):
    def body(ar, ai, ld, o0, o1, o2, o3):
        for ref, val in zip((o0, o1, o2, o3), _zoh(ar[...], ai[...], ld[...])):
            ref[...] = val

    return pl.pallas_call(body, name=f"zoh_fwd_l{l}", out_shape=[SDS(a_re.shape, F32)] * 4)(a_re, a_im, log_dt)


def _zoh_bwd(l, a_re, a_im, log_dt, cts):
    def body(ar, ai, ld, c0, c1, c2, c3, dar, dai, dld):
        _, vjp = jax.vjp(_zoh, ar[...], ai[...], ld[...])
        g = vjp((c0[...], c1[...], c2[...], c3[...]))
        dar[...] = g[0]
        dai[...] = g[1]
        dld[...] = g[2]

    return pl.pallas_call(body, name=f"zoh_bwd_l{l}",
                          out_shape=[SDS(a_re.shape, F32), SDS(a_re.shape, F32), SDS(log_dt.shape, F32)],
                          )(a_re, a_im, log_dt, *cts)


def _bbar_fwd(l, f_re, f_im, b_re, b_im):
    def body(fr, fi, br, bi, o_re, o_im):
        o_re[...] = fr[...] * br[...] - fi[...] * bi[...]
        o_im[...] = fr[...] * bi[...] + fi[...] * br[...]

    return pl.pallas_call(body, name=f"bbar_fwd_l{l}", out_shape=[SDS(b_re.shape, F32)] * 2)(f_re, f_im, b_re, b_im)


def _bbar_bwd(l, f_re, f_im, b_re, b_im, d_re, d_im):
    def body(fr, fi, br, bi, dr, di, dfr, dfi, dbr, dbi):
        dfr[...] = jnp.sum(dr[...] * br[...] + di[...] * bi[...], axis=1, keepdims=True)
        dfi[...] = jnp.sum(di[...] * br[...] - dr[...] * bi[...], axis=1, keepdims=True)
        dbr[...] = fr[...] * dr[...] + fi[...] * di[...]
        dbi[...] = fr[...] * di[...] - fi[...] * dr[...]

    return pl.pallas_call(body, name=f"bbar_bwd_l{l}",
                          out_shape=[SDS(f_re.shape, F32), SDS(f_re.shape, F32), SDS(b_re.shape, F32),
                                     SDS(b_re.shape, F32)])(f_re, f_im, b_re, b_im, d_re, d_im)


def _powers(l, abar_re, abar_im):
    lanes = abar_re.shape[1]

    def body(ar_ref, ai_ref, o_ref):
        ar, ai = ar_ref[...], ai_ref[...]
        pows = [(ar, ai)]
        for _ in range(7):
            pr, pi = pows[-1]
            pows.append((pr * ar - pi * ai, pr * ai + pi * ar))
        row = lax.broadcasted_iota(jnp.int32, (8, lanes), 0)
        for i, k in enumerate((1, 2, 4)):
            o_ref[2 * i] = jnp.broadcast_to(pows[k - 1][0], (8, lanes))
            o_ref[2 * i + 1] = jnp.broadcast_to(pows[k - 1][1], (8, lanes))
        for slot, order in ((3, range(8)), (4, range(7, -1, -1))):
            vr = jnp.zeros((8, lanes), F32)
            vi = jnp.zeros((8, lanes), F32)
            for r, e in enumerate(order):
                vr = jnp.where(row == r, pows[e][0], vr)
                vi = jnp.where(row == r, pows[e][1], vi)
            o_ref[2 * slot] = vr
            o_ref[2 * slot + 1] = vi

    return pl.pallas_call(body, name=f"powers_l{l}", out_shape=SDS((10, 8, lanes), F32))(abar_re, abar_im)


def _block_diag(v, rows_first):
    g, a, b = v.shape
    eye = jnp.eye(8, dtype=v.dtype)
    out = jnp.einsum("kgab,gh->kgahb", v.reshape(g // 8, 8, a, b), eye)
    return out.reshape(g // 8, 8 * a, 8 * b)


def _block_diag_extract(blk, a, b):
    n = blk.shape[0]
    v = blk.reshape(n, 8, a, 8, b)
    return jnp.einsum("kgahb,gh->kgab", v, jnp.eye(8, dtype=blk.dtype)).reshape(n * 8, a, b)


def _ssm_prepare(l, prm):
    g, n, p = SSM_GROUPS, SSM_STATE, SSM_GROUP
    a_re, a_im = prm["ssm_a_re"][l], prm["ssm_a_im"][l]
    log_dt = prm["ssm_log_dt"][l].reshape(g, 1)
    abar_re, abar_im, f_re, f_im = _zoh_fwd(l, a_re, a_im, log_dt)
    b_re = prm["ssm_b_re"][l].reshape(g * n, p)
    b_im = prm["ssm_b_im"][l].reshape(g * n, p)
    fcol_re, fcol_im = f_re.reshape(g * n, 1), f_im.reshape(g * n, 1)
    bbar_re, bbar_im = _bbar_fwd(l, fcol_re, fcol_im, b_re, b_im)
    bblk_re = _block_diag(bbar_re.reshape(g, n, p).transpose(0, 2, 1), True).astype(MXU_DTYPE)
    bblk_im = _block_diag(bbar_im.reshape(g, n, p).transpose(0, 2, 1), True).astype(MXU_DTYPE)
    cblk_re = _block_diag(prm["ssm_c_re"][l].transpose(0, 2, 1), False).astype(MXU_DTYPE)
    cblk_im = _block_diag(prm["ssm_c_im"][l].transpose(0, 2, 1), False).astype(MXU_DTYPE)
    pw = _powers(l, abar_re.reshape(1, g * n), abar_im.reshape(1, g * n))
    return dict(a_re=a_re, a_im=a_im, log_dt=log_dt, b_re=b_re, b_im=b_im, fcol_re=fcol_re, fcol_im=fcol_im,
                bblk_re=bblk_re, bblk_im=bblk_im, cblk_re=cblk_re, cblk_im=cblk_im, pw=pw,
                dskip=prm["ssm_d"][l].reshape(1, g * p))


def _ssm_param_grads(l, sd, r):
    g, n, p = SSM_GROUPS, SSM_STATE, SSM_GROUP
    dbbar_re = _block_diag_extract(r["dbblk_re"], p, n).transpose(0, 2, 1).reshape(g * n, p)
    dbbar_im = _block_diag_extract(r["dbblk_im"], p, n).transpose(0, 2, 1).reshape(g * n, p)
    dfr, dfi, db_re, db_im = _bbar_bwd(l, sd["fcol_re"], sd["fcol_im"], sd["b_re"], sd["b_im"], dbbar_re, dbbar_im)
    cts = (r["dabar_re"].reshape(g, n), r["dabar_im"].reshape(g, n), dfr.reshape(g, n), dfi.reshape(g, n))
    da_re, da_im, dlog_dt = _zoh_bwd(l, sd["a_re"], sd["a_im"], sd["log_dt"], cts)
    dc_re = _block_diag_extract(r["dcblk_re"], n, p).transpose(0, 2, 1)
    dc_im = _block_diag_extract(r["dcblk_im"], n, p).transpose(0, 2, 1)
    return dict(ssm_a_re=da_re, ssm_a_im=da_im, ssm_log_dt=dlog_dt.reshape(g), ssm_b_re=db_re.reshape(g, n, p),
                ssm_b_im=db_im.reshape(g, n, p), ssm_c_re=dc_re, ssm_c_im=dc_im, ssm_d=r["dd"].reshape(g, p))


def _ffn_weight_grads(l, fb, dx2, s):
    d = dx2.shape[1]
    hcn = fb["act"].shape[-1]
    g = {}
    for name, key, rhs, ts in (("ffn_w_gate", "dgate", fb["h2"], s), ("ffn_w_up", "dup", fb["h2"], s),
                               ("ffn_w_down", "act", dx2, min(1024, s))):
        g[name] = _tn_matmul(f"d{name}_l{l}", fb[key], BS((None, ts, hcn), lambda j, t: (j, t, 0)), rhs,
                             BS((ts, d), lambda j, t: (t, 0)), SDS((N_CHIPS, hcn, d), F32),
                             BS((None, hcn, d), lambda j, t: (j, 0, 0)), (N_CHIPS, s // ts))
    return g


def _in_weight_grad(l, h, dz):
    s, d = h.shape
    ncw = dz.shape[1] // N_CHIPS
    return _tn_matmul(f"dw_in_l{l}", h, BS((s, d), lambda j, t: (0, 0)), dz, BS((s, ncw), lambda j, t: (0, j)),
                      SDS((N_CHIPS, d, ncw), F32), BS((None, d, ncw), lambda j, t: (j, 0, 0)), (N_CHIPS, 1))


def _fused_tn(name, pairs, kinds, s, wire):
    ts = min(512, s)
    n = len(pairs)

    def shape_of(a, b, kind):
        k, m = a.shape[1], b.shape[1]
        if kind == "rows":
            return (N_CHIPS, k // N_CHIPS, m)
        if kind == "cols":
            return (N_CHIPS, k, m // N_CHIPS)
        return (k // 128, 128, 128)

    shapes = [shape_of(a, b, kind) for (a, b), kind in zip(pairs, kinds)]
    last = s // ts - 1

    def body(*refs):
        ins, outs = refs[:2 * n], refs[2 * n:]
        first = pl.program_id(0) == 0
        for i, kind in enumerate(kinds):
            a, b = ins[2 * i][...], ins[2 * i + 1][...]
            o_ref = outs[i]

            @pl.when(first)
            def _():
                o_ref[...] = jnp.zeros(o_ref.shape, F32)

            if kind == "rows":
                o_ref[...] += _mm_tn(a, b).reshape(o_ref.shape)
            elif kind == "cols":
                full = _mm_tn(a, b)
                nc = o_ref.shape[2]
                for j in range(N_CHIPS):
                    o_ref[j] += full[:, j * nc:(j + 1) * nc]
            else:
                for k in range(o_ref.shape[0]):
                    o_ref[k] += _mm_tn(a[:, k * 128:(k + 1) * 128], b[:, k * 128:(k + 1) * 128])
        if wire:
            @pl.when(pl.program_id(0) == last)
            def _():
                for i in range(n):
                    outs[n + i][...] = outs[i][...].astype(WIRE_DTYPE)

    whole = lambda shp: BS(shp, lambda t: (0,) * len(shp))
    out_shape = [SDS(shp, F32) for shp in shapes] + ([SDS(shp, WIRE_DTYPE) for shp in shapes] if wire else [])
    outs = pl.pallas_call(
        body, name=name, grid=(s // ts,),
        in_specs=[BS((ts, v.shape[1]), lambda t: (t, 0)) for pair in pairs for v in pair],
        out_specs=[whole(o.shape) for o in out_shape], out_shape=out_shape, compiler_params=_params(),
    )(*[v for pair in pairs for v in pair])
    return [(outs[i], outs[n + i]) for i in range(n)] if wire else list(outs)


def _mixer_weight_grads(l, sv, mb, dx1, s):
    g = {}
    (g["w_out"], g["ssm_w_glu"]) = _fused_tn(f"dw_out_glu_l{l}", [(mb["merged"], dx1), (mb["ge"], mb["dt"])],
                                            ("rows", "rows"), s, True)
    (g["ssm_w_proj"], g["conv_w_proj"], g["pool_w_proj"]) = _fused_tn(
        f"dw_proj_l{l}", [(mb["sa"], mb["dya"]), (mb["ac"], mb["dyb"]), (mb["pp"], mb["dyc"])],
        ("cols", "cols", "cols"), s, True)
    (dwgrp,) = _fused_tn(f"dpool_w_group_l{l}", [(sv["p"], mb["dq"])], ("groups",), s, False)
    return g, dwgrp


def _local_step(x, target, weights_of, prm, on_grads=None):
    s, d = x.shape
    cw = prm["ssm_b_glu"].shape[1]
    sp = {k: prm[k].reshape(N_LAYERS, 1, -1) for k in ("norm1", "norm2", "b_gate", "ssm_b_glu", "conv_ln_g", "conv_ln_b",
                                                        "pool_scale", "conv_b_dw")}
    sp["pool_w_group"] = prm["pool_w_group"]
    saved = []
    xin = x
    for l in range(N_LAYERS):
        fw = weights_of(l, "in", (xin,))
        sd = _ssm_prepare(l, prm)
        z, h = _in_proj(l, xin, sp["norm1"], fw["w_in"])
        hre, him, y = _ssm_fwd(l, z, sd["bblk_re"], sd["bblk_im"], sd["cblk_re"], sd["cblk_im"], sd["pw"], sd["dskip"])
        p = _pool_fwd(l, z, cw)
        fw.update(weights_of(l, "mixer", (y, p)))
        wdw = fw["conv_w_dw"]
        hc = _conv_fwd(l, z, wdw, sp["conv_b_dw"][l])
        x1 = _merge_fwd(l, xin, y, hc, p, z, fw, sp)
        fw.update(weights_of(l, "ffn", (x1,)))
        x2 = _ffn_fwd(l, x1, sp["norm2"], fw["ffn_w_gate"], fw["ffn_w_up"], fw["ffn_w_down"])
        saved.append(dict(x=xin, z=z, h=h, hre=hre, him=him, y=y, hc=hc, p=p, x1=x1, sd=sd, wdw=wdw, fw=fw))
        xin = x2
    dx, loss, dfinal = _loss_head(xin, target, prm["final_norm"].reshape(1, d))
    big = [None] * N_LAYERS
    small = [None] * N_LAYERS
    norm2_rows = sp["norm2"]
    started = (lambda l, group, grads: on_grads(l, group, grads)) if on_grads is not None else (lambda *a: 0.0)
    for l in reversed(range(N_LAYERS)):
        sv = saved[l]
        sd, fw = sv["sd"], sv["fw"]
        fb = _ffn_bwd(l, sv["x1"], dx, norm2_rows, fw["ffn_w_gate"], fw["ffn_w_up"], fw["ffn_w_down"])
        big[l] = _ffn_weight_grads(l, fb, dx, s)
        spl = dict(sp, ssm_b_glu=sp["ssm_b_glu"] + started(l, "ffn", big[l]))
        mb = _merge_bwd(l, fb["dx1"], sv["y"], sv["hc"], sv["p"], sv["z"], fw, spl)
        mixer, dwgrp = _mixer_weight_grads(l, sv, mb, fb["dx1"], s)
        big[l].update(mixer)
        wdw = sv["wdw"] + started(l, "mixer", mixer)
        du_c = _pool_bwd(l, mb["dp"])
        dv1, dv2, dwdw, dbdw = _conv_bwd(l, mb["dhc"], sv["z"], wdw)
        sr = _ssm_bwd(l, mb["dy"], sv["z"], sv["hre"], sv["him"], sd["bblk_re"], sd["bblk_im"], sd["cblk_re"],
                      sd["cblk_im"], sd["pw"], sd["dskip"])
        dx, dz, dnorm1 = _in_proj_bwd(l, fb["dx1"], sv["x"], sp["norm1"], fw["w_in"], sr["du"], dv1, dv2, du_c, mb["dzg"])
        w_in_grad = {"w_in": _in_weight_grad(l, sv["h"], dz)}
        big[l].update(w_in_grad)
        sg = _ssm_param_grads(l, sd, sr)
        sg.update(norm1=dnorm1.reshape(d), b_gate=mb["db_gate"].reshape(3 * d), ssm_b_glu=mb["db_glu"].reshape(cw),
                  conv_b_dw=dbdw.reshape(cw), conv_ln_g=mb["dln_g"].reshape(cw), conv_ln_b=mb["dln_b"].reshape(cw),
                  pool_w_group=dwgrp, pool_scale=mb["dscale"].reshape(cw), norm2=fb["dnorm2"].reshape(d),
                  conv_w_dw=dwdw)
        small[l] = sg
        if l == N_LAYERS - 1:
            sg = dict(sg, final_norm=dfinal.reshape(d))
        norm2_rows = sp["norm2"] + (started(l, "in", w_in_grad) + started(l, "small", sg))
    return loss[0, 0], dx, big, small, dfinal.reshape(d)


def _place():
    return lax.axis_index("x"), lax.axis_index("y"), lax.axis_index("c")


def _other_chips(x, y):
    return [(1 - x, y), (x, 1 - y), (1 - x, 1 - y)]


def _remote(src, dst, send_sem, recv_sem, device):
    return pltpu.make_async_remote_copy(src_ref=src, dst_ref=dst, send_sem=send_sem, recv_sem=recv_sem,
                                        device_id=device, device_id_type=MESH)


def _hbm(v):
    return pltpu.with_memory_space_constraint(v, pltpu.HBM)


def _cast_into(name, w, place, dtype, after=()):
    nl, k, n = w.shape
    tr = _row_tile(k, n)
    nt = k // tr

    def body(place_ref, w_ref, *rest):
        o0_ref, o1_ref = rest[len(after):]

        @pl.when(pl.program_id(0) == 0)
        def _():
            o0_ref[...] = w_ref[...].astype(dtype)

        @pl.when(pl.program_id(0) == 1)
        def _():
            o1_ref[...] = w_ref[...].astype(dtype)

    return pl.pallas_call(
        body, name=f"cast_{name}",
        grid_spec=pltpu.PrefetchScalarGridSpec(
            num_scalar_prefetch=1, grid=(nl, nt),
            in_specs=[BS((None, tr, n), lambda l, t, pr: (l, t, 0))] + [ANY] * len(after),
            out_specs=[BS((None, tr, n), lambda l, t, pr: (pr[0], t * (1 - l) + (nt - 1) * l, 0)),
                       BS((None, tr, n), lambda l, t, pr: (pr[0], t * l, 0))]),
        out_shape=[SDS((N_CHIPS, k, n), dtype)] * 2)(place, w, *after)


def _gather_rows(buf, c):
    k = buf.shape[1]
    if k % 2:
        return pl.ds(0, k)
    return pl.ds(pl.multiple_of(c * (k // 2), 8), k // 2)


def _allgather_start(tag, groups):
    ng = len(groups)
    sizes = [len(g) for g in groups]
    first = [sum(sizes[:g]) for g in range(ng)]
    flat = [b for g in groups for b in g]
    nb = len(flat)

    def body(*refs):
        ins = refs[:nb]
        sems = refs[nb:nb + 2 * ng]
        token = refs[-1]
        x, y, c = _place()
        jme = 2 * x + y
        for g in range(ng):
            for a in range(sizes[g]):
                buf = ins[first[g] + a]
                blk = buf.at[jme, _gather_rows(buf, c)]
                for k, (cx, cy) in enumerate(_other_chips(x, y)):
                    _remote(blk, blk, sems[2 * g].at[3 * a + k], sems[2 * g + 1].at[3 * a + k], (cx, cy, c)).start()
        token[...] = jnp.zeros(token.shape, F32)

    sem_shapes = [pltpu.SemaphoreType.DMA((3 * sizes[g // 2],)) for g in range(2 * ng)]
    outs = pl.pallas_call(
        body, name=f"allgather_start_{tag}", in_specs=[HBM] * nb,
        out_specs=[SEM] * (2 * ng) + [HBM] * nb + [pl.BlockSpec(memory_space=pltpu.VMEM)],
        out_shape=sem_shapes + [pltpu.HBM(b.shape, b.dtype) for b in flat] + [SDS((8, 128), F32)],
        input_output_aliases={i: 2 * ng + i for i in range(nb)},
        compiler_params=pltpu.CompilerParams(has_side_effects=SIDE_EFFECT))(*[_hbm(b) for b in flat])
    per_group = [(outs[2 * g], outs[2 * g + 1], outs[2 * ng + first[g]:2 * ng + first[g] + sizes[g]])
                 for g in range(ng)]
    return per_group, outs[-1]


def _allgather_wait(l, send_sems, recv_sems, bufs, after):
    n = len(bufs)

    def body(*refs):
        ins = refs[:n]
        ssem, rsem = refs[n], refs[n + 1]
        x, y, c = _place()
        jme = 2 * x + y
        for a in range(n):
            rows = _gather_rows(ins[a], c)
            for k, (cx, cy) in enumerate(_other_chips(x, y)):
                cp = _remote(ins[a].at[jme, rows], ins[a].at[2 * cx + cy, rows], ssem.at[3 * a + k],
                             rsem.at[3 * a + k], (cx, cy, c))
                cp.wait_send()
                cp.wait_recv()

    return pl.pallas_call(
        body, name=f"allgather_wait_{l}", in_specs=[HBM] * n + [SEM, SEM] + [ANY] * len(after), out_specs=[HBM] * n,
        out_shape=[pltpu.HBM(b.shape, b.dtype) for b in bufs], input_output_aliases={i: i for i in range(n)},
        compiler_params=pltpu.CompilerParams(has_side_effects=SIDE_EFFECT))(*bufs, send_sems, recv_sems, *after)


def _allgather_forward(l, bufs):
    n = len(bufs)
    split = [a for a in range(n) if bufs[a].shape[1] % 2 == 0]

    def body(*refs):
        ins = refs[:n]
        send_sems, recv_sems = refs[2 * n:]
        x, y, c = _place()
        sibling = (x, y, 1 - c)
        copies = []
        for a in split:
            for k, (cx, cy) in enumerate(_other_chips(x, y)):
                blk = ins[a].at[2 * cx + cy, _gather_rows(ins[a], c)]
                cp = _remote(blk, blk, send_sems.at[a, k], recv_sems.at[a, k], sibling)
                cp.start()
                copies.append(cp)
        for a in split:
            for k, (cx, cy) in enumerate(_other_chips(x, y)):
                blk = ins[a].at[2 * cx + cy, _gather_rows(ins[a], 1 - c)]
                _remote(blk, blk, send_sems.at[a, k], recv_sems.at[a, k], sibling).wait_recv()
        for cp in copies:
            cp.wait_send()

    sem = pltpu.SemaphoreType.DMA((n, 3))
    return pl.pallas_call(
        body, name=f"allgather_forward_{l}", in_specs=[ANY] * n, out_specs=[ANY] * n,
        out_shape=[SDS(b.shape, b.dtype) for b in bufs], input_output_aliases={i: i for i in range(n)},
        scratch_shapes=[sem, sem])(*bufs)


def _rs_to_owner(l, parts):
    n = len(parts)
    lands = [lax.empty((3,) + p.shape[1:], p.dtype) for p in parts]

    def body(*refs):
        ins, zones = refs[:n], refs[n:2 * n]
        send_sems, recv_sems = refs[2 * n], refs[2 * n + 1]
        token = refs[-1]
        x, y, c = _place()
        for a in range(n):
            for k, (cx, cy) in enumerate(_other_chips(x, y)):
                _remote(ins[a].at[2 * cx + cy], zones[a].at[k], send_sems.at[3 * a + k], recv_sems.at[3 * a + k],
                        (cx, cy, c)).start()
        token[...] = jnp.zeros(token.shape, F32)

    sem = pltpu.SemaphoreType.DMA((3 * n,))
    outs = pl.pallas_call(
        body, name=f"rs_to_owner_start_{l}", in_specs=[HBM] * (2 * n),
        out_specs=[SEM, SEM] + [HBM] * (2 * n) + [pl.BlockSpec(memory_space=pltpu.VMEM)],
        out_shape=[sem, sem] + [pltpu.HBM(p.shape, p.dtype) for p in parts]
        + [pltpu.HBM(z.shape, z.dtype) for z in lands] + [SDS((8, 128), F32)],
        input_output_aliases={i: 2 + i for i in range(2 * n)},
        compiler_params=pltpu.CompilerParams(has_side_effects=SIDE_EFFECT),
    )(*[_hbm(p) for p in parts], *[_hbm(z) for z in lands])
    return outs[0], outs[1], outs[2:2 + n], outs[2 + n:2 + 2 * n], outs[-1]


def _rs_to_owner_wait(l, send_sems, recv_sems, parts, lands, after):
    n = len(parts)

    def body(*refs):
        ins, zones = refs[:n], refs[n:2 * n]
        ssem, rsem = refs[2 * n], refs[2 * n + 1]
        x, y, c = _place()
        for a in range(n):
            for k, (cx, cy) in enumerate(_other_chips(x, y)):
                cp = _remote(ins[a].at[2 * cx + cy], zones[a].at[k], ssem.at[3 * a + k], rsem.at[3 * a + k],
                             (cx, cy, c))
                cp.wait_send()
                cp.wait_recv()

    outs = pl.pallas_call(
        body, name=f"rs_to_owner_wait_{l}", in_specs=[HBM] * (2 * n) + [SEM, SEM] + [ANY] * len(after),
        out_specs=[HBM] * (2 * n),
        out_shape=[pltpu.HBM(p.shape, p.dtype) for p in parts] + [pltpu.HBM(z.shape, z.dtype) for z in lands],
        input_output_aliases={i: i for i in range(2 * n)},
        compiler_params=pltpu.CompilerParams(has_side_effects=SIDE_EFFECT),
    )(*parts, *lands, send_sems, recv_sems, *after)
    return outs[:n], outs[n:]


def _rs_sibling_exchange(l, both):
    n = len(both)

    def body(*refs):
        ins = refs[:n]
        send_sems, recv_sems = refs[2 * n:]
        x, y, c = _place()
        copies = []
        for a in range(n):
            cp = _remote(ins[a].at[c], ins[a].at[c], send_sems.at[a], recv_sems.at[a], (x, y, 1 - c))
            cp.start()
            copies.append(cp)
        for a, cp in enumerate(copies):
            cp.wait_send()
            _remote(ins[a].at[1 - c], ins[a].at[1 - c], send_sems.at[a], recv_sems.at[a], (x, y, 1 - c)).wait_recv()

    sem = pltpu.SemaphoreType.DMA((n,))
    return pl.pallas_call(
        body, name=f"rs_sibling_exchange_{l}", in_specs=[ANY] * n, out_specs=[ANY] * n,
        out_shape=[SDS(b.shape, b.dtype) for b in both], input_output_aliases={i: i for i in range(n)},
        scratch_shapes=[sem, sem])(*both)


def _add_owner(name, grad, recv, place):
    _, r, cols = grad.shape
    tr = _row_tile(r, cols, budget=1024 * 1024)
    nt = r // tr

    def body(place_ref, g_ref, r_ref, o_ref):
        acc = ((g_ref[...] + r_ref[0].astype(F32)) + r_ref[1].astype(F32)) + r_ref[2].astype(F32)
        o_ref[...] = acc.astype(o_ref.dtype)

    return pl.pallas_call(
        body, name=name,
        grid_spec=pltpu.PrefetchScalarGridSpec(
            num_scalar_prefetch=1, grid=(nt,),
            in_specs=[BS((None, tr, cols), lambda t, pr: (pr[0], t, 0)), BS((3, tr, cols), lambda t, pr: (0, t, 0))],
            out_specs=BS((None, tr, cols), lambda t, pr: (pr[1], t, 0))),
        out_shape=SDS((2, r, cols), WIRE_DTYPE))(place, grad, recv)


def _reduce_start(tag, grads):
    names = list(grads)
    send_sems, recv_sems, wires, lands, token = _rs_to_owner(tag, [grads[n][1] for n in names])
    return dict(tag=tag, names=names, send_sems=send_sems, recv_sems=recv_sems, wires=wires, lands=lands,
                grads=[grads[n][0] for n in names]), token


def _reduce_finish(pending, place, after):
    tag, names = pending["tag"], pending["names"]
    _, lands = _rs_to_owner_wait(tag, pending["send_sems"], pending["recv_sems"], pending["wires"],
                                 pending["lands"], after)
    mine = [_add_owner(f"rs_add_owner_{n}_{tag}", g, r, place) for n, g, r in zip(names, pending["grads"], lands)]
    return dict(zip(names, _rs_sibling_exchange(tag, mine)))


def _small_peers(x, y, c):
    return [(x, y, 1 - c)] + [(cx, cy, c) for cx, cy in _other_chips(x, y)]


def _allgather_rows_start(tag, buf):
    land = lax.empty((8,) + buf.shape, buf.dtype)

    def body(x_ref, out_ref, send_sems, recv_sems, x_thru, out_thru, token):
        x, y, c = _place()
        for i, peer in enumerate(_small_peers(x, y, c)):
            _remote(x_ref, out_ref.at[4 * x + 2 * y + c], send_sems.at[i], recv_sems.at[i], peer).start()
        token[...] = jnp.zeros(token.shape, F32)

    sem = pltpu.SemaphoreType.DMA((4,))
    return pl.pallas_call(
        body, name=f"allgather_small_start_{tag}", in_specs=[HBM, HBM],
        out_specs=[SEM, SEM, HBM, HBM, pl.BlockSpec(memory_space=pltpu.VMEM)],
        out_shape=[sem, sem, pltpu.HBM(buf.shape, buf.dtype), pltpu.HBM(land.shape, land.dtype), SDS((8, 128), F32)],
        input_output_aliases={0: 2, 1: 3}, compiler_params=pltpu.CompilerParams(has_side_effects=SIDE_EFFECT),
    )(_hbm(buf), _hbm(land))


def _allgather_rows_wait(tag, send_sems, recv_sems, buf, land, after):
    def body(x_ref, out_ref, ssem, rsem, after_ref, x_thru, out_thru):
        x, y, c = _place()
        for i, (px, py, pc) in enumerate(_small_peers(x, y, c)):
            cp = _remote(x_ref, out_ref.at[4 * px + 2 * py + pc], ssem.at[i], rsem.at[i], (px, py, pc))
            cp.wait_send()
            cp.wait_recv()

    return pl.pallas_call(
        body, name=f"allgather_small_wait_{tag}", in_specs=[HBM, HBM, SEM, SEM, ANY], out_specs=[HBM, HBM],
        out_shape=[pltpu.HBM(buf.shape, buf.dtype), pltpu.HBM(land.shape, land.dtype)],
        input_output_aliases={0: 0, 1: 1}, compiler_params=pltpu.CompilerParams(has_side_effects=SIDE_EFFECT),
    )(buf, land, send_sems, recv_sems, after)


def _allgather_rows_forward(tag, land):
    def body(in_ref, out_ref, send_sems, recv_sems):
        x, y, c = _place()
        sibling = (x, y, 1 - c)
        copies = []
        for k, (cx, cy) in enumerate(_other_chips(x, y)):
            blk = in_ref.at[4 * cx + 2 * cy + c]
            cp = _remote(blk, blk, send_sems.at[k], recv_sems.at[k], sibling)
            cp.start()
            copies.append(cp)
        for k, (cx, cy) in enumerate(_other_chips(x, y)):
            blk = in_ref.at[4 * cx + 2 * cy + 1 - c]
            _remote(blk, blk, send_sems.at[k], recv_sems.at[k], sibling).wait_recv()
        for cp in copies:
            cp.wait_send()

    sem = pltpu.SemaphoreType.DMA((3,))
    return pl.pallas_call(body, name=f"allgather_small_forward_{tag}", in_specs=[ANY], out_specs=ANY,
                          out_shape=SDS(land.shape, land.dtype), input_output_aliases={0: 0},
                          scratch_shapes=[sem, sem])(land)


def _sum_devices(tag, gathered, mine, place):
    _, r, cols = gathered.shape
    tr = _row_tile(r, cols, budget=256 * 1024)

    def body(place_ref, g_ref, x_ref, o_ref):
        me = 2 * place_ref[0] + place_ref[1]
        acc = jnp.where(me == 0, x_ref[...], g_ref[0])
        for k in range(1, 8):
            acc = acc + jnp.where(me == k, x_ref[...], g_ref[k])
        o_ref[...] = acc

    return pl.pallas_call(
        body, name=f"sum_small_grads_{tag}",
        grid_spec=pltpu.PrefetchScalarGridSpec(
            num_scalar_prefetch=1, grid=(r // tr,),
            in_specs=[BS((8, tr, cols), lambda t, pr: (0, t, 0)), BS((tr, cols), lambda t, pr: (t, 0))],
            out_specs=BS((tr, cols), lambda t, pr: (t, 0))),
        out_shape=SDS((r, cols), F32))(place, gathered, mine)


def _adamw_values(w, g, m, v):
    m = ADAM_B1 * m + (1.0 - ADAM_B1) * g
    v = ADAM_B2 * v + (1.0 - ADAM_B2) * (g * g)
    m_hat = m / (1.0 - ADAM_B1 ** ADAM_STEP)
    v_hat = v / (1.0 - ADAM_B2 ** ADAM_STEP)
    delta = -ADAM_LR * (m_hat / (jnp.sqrt(v_hat) + ADAM_EPS) + ADAM_WD * w)
    return delta, m, v


def _adamw_big(name, l, w, m, v, g, earlier=None, after=()):
    nl, r, cols = w.shape
    tr = _row_tile(r, cols, budget=1024 * 1024)
    nt = r // tr
    n_prev = 0 if earlier is None else 4

    def body(*refs):
        w_ref, m_ref, v_ref, g_ref = refs[:4]
        go_ref, d_ref, mo_ref, vo_ref = refs[4 + n_prev + len(after):]
        gv = g_ref[0].astype(F32) + g_ref[1].astype(F32)
        delta, m_new, v_new = _adamw_values(w_ref[...], gv, m_ref[...], v_ref[...])
        go_ref[...] = gv
        d_ref[...] = delta
        mo_ref[...] = m_new
        vo_ref[...] = v_new

    layer = BS((None, tr, cols), lambda t: (l, t, 0))
    return pl.pallas_call(
        body, name=f"adamw_{name}_l{l}", grid=(nt,),
        in_specs=[layer, layer, layer, BS((2, tr, cols), lambda t: (0, t, 0))] + [ANY] * (n_prev + len(after)),
        out_specs=[layer] * 4, out_shape=[SDS(w.shape, F32)] * 4,
        input_output_aliases={4 + i: i for i in range(n_prev)}, compiler_params=_params(),
    )(w, m, v, g, *(earlier or ()), *after)


def _adamw_rows(w, m, v, g):
    r, cols = w.shape
    tr = _row_tile(r, cols, budget=512 * 1024)

    def body(w_ref, m_ref, v_ref, g_ref, d_ref, mo_ref, vo_ref):
        delta, m_new, v_new = _adamw_values(w_ref[...], g_ref[...], m_ref[...], v_ref[...])
        d_ref[...] = delta
        mo_ref[...] = m_new
        vo_ref[...] = v_new

    spec = BS((tr, cols), lambda t: (t, 0))
    return pl.pallas_call(body, name="adamw_small", grid=(r // tr,), in_specs=[spec] * 4, out_specs=[spec] * 3,
                          out_shape=[SDS(w.shape, F32)] * 3)(w, m, v, g)


PACK_ALIGN = 8 * 128


def _pack_rows(arrays):
    parts = []
    for a in arrays:
        flat = a.reshape(-1)
        pad = (-flat.shape[0]) % PACK_ALIGN
        if pad:
            flat = jnp.pad(flat, (0, pad))
        parts.append(flat.reshape(-1, 128))
    return jnp.concatenate(parts, axis=0)


def _unpack_rows(buf, shapes):
    out, row = [], 0
    for shape in shapes:
        size = math.prod(shape)
        rows = -(-size // PACK_ALIGN) * (PACK_ALIGN // 128)
        out.append(buf[row:row + rows].reshape(-1)[:size].reshape(shape))
        row += rows
    return out


def kernel(x, norm1, w_in, b_gate, ssm_a_re, ssm_a_im, ssm_log_dt, ssm_b_re, ssm_b_im, ssm_c_re, ssm_c_im, ssm_d, ssm_w_glu, ssm_b_glu, ssm_w_proj, conv_w_dw, conv_b_dw, conv_ln_g, conv_ln_b, conv_w_proj, pool_w_group, pool_scale, pool_w_proj, w_out, norm2, ffn_w_gate, ffn_w_up, ffn_w_down, final_norm, loss_target, m_norm1, m_w_in, m_b_gate, m_ssm_a_re, m_ssm_a_im, m_ssm_log_dt, m_ssm_b_re, m_ssm_b_im, m_ssm_c_re, m_ssm_c_im, m_ssm_d, m_ssm_w_glu, m_ssm_b_glu, m_ssm_w_proj, m_conv_w_dw, m_conv_b_dw, m_conv_ln_g, m_conv_ln_b, m_conv_w_proj, m_pool_w_group, m_pool_scale, m_pool_w_proj, m_w_out, m_norm2, m_ffn_w_gate, m_ffn_w_up, m_ffn_w_down, m_final_norm, v_norm1, v_w_in, v_b_gate, v_ssm_a_re, v_ssm_a_im, v_ssm_log_dt, v_ssm_b_re, v_ssm_b_im, v_ssm_c_re, v_ssm_c_im, v_ssm_d, v_ssm_w_glu, v_ssm_b_glu, v_ssm_w_proj, v_conv_w_dw, v_conv_b_dw, v_conv_ln_g, v_conv_ln_b, v_conv_w_proj, v_pool_w_group, v_pool_scale, v_pool_w_proj, v_w_out, v_norm2, v_ffn_w_gate, v_ffn_w_up, v_ffn_w_down, v_final_norm):
    given = dict(locals())
    prm = {n: given[n] for n in WEIGHTS}
    mom = {n: given["m_" + n] for n in WEIGHTS}
    var = {n: given["v_" + n] for n in WEIGHTS}
    cx, cy, cc = _place()
    place = jnp.stack([2 * cx + cy, cc]).astype(jnp.int32)

    def kernel_view(n, a):
        return a.transpose(0, 2, 1) if n in TRANSPOSED else a

    dw_shard = prm["conv_w_dw"].reshape(N_LAYERS, CONV_KERNEL, -1)
    casts = {"w_in": _cast_into("w_in", prm["w_in"], place, MXU_DTYPE)}
    first, first_started = _allgather_start("first", [[casts["w_in"][0]]])
    in_flight = {(0, "in"): first[0]}
    casts.update({n: _cast_into(n, kernel_view(n, prm[n]), place, MXU_DTYPE, after=(first_started,))
                  for n in BIG if n != "w_in"})
    casts["conv_w_dw"] = _cast_into("conv_w_dw", dw_shard, place, F32, after=(first_started,))
    order = [(l, g) for l in range(N_LAYERS) for g in GATHER_GROUPS if (l, g) != (0, "in")]
    rest, rest_started = _allgather_start("rest", [[casts[n][l] for n in GATHER_GROUPS[g]] for l, g in order])
    in_flight.update(zip(order, rest))

    def weights_of(l, group, after):
        send_sems, recv_sems, bufs = in_flight[l, group]
        tag = f"l{l}_{group}"
        if (l, group) == (0, "in"):
            after = after + (rest_started,)
        bufs = _allgather_forward(tag, _allgather_wait(tag, send_sems, recv_sems, bufs, after))
        fw = dict(zip(GATHER_GROUPS[group], bufs))
        if "conv_w_dw" in fw:
            fw["conv_w_dw"] = fw["conv_w_dw"].transpose(1, 0, 2).reshape(CONV_KERNEL, -1)
        return fw

    pending, small_pending, small_shapes = {}, {}, {}
    tokens = {}

    def on_grads(l, group, grads):
        if group == "small":
            small_shapes[l] = {n: g.shape for n, g in grads.items()}
            begun = _allgather_rows_start(f"l{l}", _pack_rows(list(grads.values())))
            small_pending[l], token = begun[:4], begun[4]
        else:
            pending[l, group], token = _reduce_start(f"{l}_{group}", grads)
        tokens[l, group] = token
        return token[0, 0]

    loss, dx, _, _, _ = _local_step(x[0], loss_target[0], weights_of, prm, on_grads)
    loss = lax.psum(loss, ("x", "y", "c"))

    reduced = [{} for _ in range(N_LAYERS)]
    out = {}

    def finish(l, group, after):
        reduced[l].update(_reduce_finish(pending[l, group], place, after))

    def adamw(l, names, done):
        for n in names:
            out[n] = _adamw_big(n, l, kernel_view(n, prm[n]), kernel_view(n, mom[n]), kernel_view(n, var[n]),
                                reduced[l][n], out.get(n), after=done)
            done = (out[n][0],)
        return done

    top = N_LAYERS - 1
    done = (tokens[0, "in"], tokens[0, "small"])
    for group in ("ffn", "mixer", "in"):
        finish(top, group, done)
    done = adamw(top, BIG, done)
    for group in ("ffn", "mixer", "in"):
        finish(0, group, done)
        done = adamw(0, [n for n in GATHER_GROUPS[group] if n in BIG], done)
    for n in BIG:
        out[n] = tuple(kernel_view(n, a) for a in out[n])

    gsmall = {}
    for l in range(N_LAYERS):
        rows, land = _allgather_rows_wait(f"l{l}", *small_pending[l], done[0])
        gsum = _sum_devices(f"l{l}", _allgather_rows_forward(f"l{l}", land), rows, place)
        for n, g in zip(small_shapes[l], _unpack_rows(gsum, list(small_shapes[l].values()))):
            gsmall.setdefault(n, [None] * N_LAYERS)[l] = g
    gsmall = {n: (g[top] if n == "final_norm" else jnp.stack(g)) for n, g in gsmall.items()}
    lanes = dw_shard.shape[-1]
    gsmall["conv_w_dw"] = lax.dynamic_slice_in_dim(gsmall["conv_w_dw"], (2 * cx + cy) * lanes, lanes, axis=2)
    small_names = list(SMALL) + ["conv_w_dw"]
    w_rows = _pack_rows([prm[n] for n in small_names])
    m_rows = _pack_rows([mom[n] for n in small_names])
    v_rows = _pack_rows([var[n] for n in small_names])
    g_rows = _pack_rows([gsmall[n] for n in small_names])
    shapes = [prm[n].shape for n in small_names]
    d_s, m_s, v_s = (_unpack_rows(r, shapes) for r in _adamw_rows(w_rows, m_rows, v_rows, g_rows))
    for i, n in enumerate(small_names):
        out[n] = (gsmall[n].reshape(prm[n].shape), d_s[i], m_s[i], v_s[i])
    grads = [out[n][0] for n in WEIGHTS]
    deltas = [out[n][1] for n in WEIGHTS]
    new_m = [out[n][2] for n in WEIGHTS]
    new_v = [out[n][3] for n in WEIGHTS]
    return (loss, dx[None], *grads, *deltas, *new_m, *new_v)
```

```python
import functools
import math

import jax
import jax.numpy as jnp
from jax import lax
from jax.experimental import pallas as pl
from jax.experimental.pallas import tpu as pltpu

F32 = jnp.float32
MXU_DTYPE = jnp.bfloat16
WIRE_DTYPE = jnp.bfloat16
SDS = jax.ShapeDtypeStruct
BS = pl.BlockSpec
ANY = pl.BlockSpec(memory_space=pl.ANY)
HBM = pl.BlockSpec(memory_space=pltpu.HBM)
SEM = pl.BlockSpec(memory_space=pltpu.SEMAPHORE)
SIDE_EFFECT = pltpu.SideEffectType.DATAFLOW_SIDE_EFFECTING
MESH = pl.DeviceIdType.MESH

EPS = 1e-6
N_CHIPS = 4
N_LAYERS = 2
SSM_GROUPS, SSM_STATE, SSM_GROUP = 32, 64, 16
CONV_KERNEL = 31
CONV_PAD = 32
POOL_WINDOWS = (2, 4, 8, 16)
GELU_C = math.sqrt(2.0 / math.pi)
ADAM_LR, ADAM_B1, ADAM_B2, ADAM_EPS, ADAM_WD, ADAM_STEP = 0.001, 0.9, 0.999, 1e-08, 0.01, 10
VMEM_LIMIT = 56 * 1024 * 1024

BIG = ("w_in", "ssm_w_glu", "ssm_w_proj", "conv_w_proj", "pool_w_proj", "w_out", "ffn_w_gate", "ffn_w_up", "ffn_w_down")
TRANSPOSED = ("ffn_w_gate", "ffn_w_up")
MID = ("ssm_b_re", "ssm_b_im", "ssm_c_re", "ssm_c_im")
GATHER_GROUPS = {
    "in": ("w_in",),
    "mixer": ("ssm_w_glu", "ssm_w_proj", "conv_w_proj", "pool_w_proj", "w_out", "conv_w_dw"),
    "ffn": ("ffn_w_gate", "ffn_w_up", "ffn_w_down"),
}
SMALL = ("norm1", "b_gate", "ssm_a_re", "ssm_a_im", "ssm_log_dt", "ssm_b_re", "ssm_b_im", "ssm_c_re", "ssm_c_im",
         "ssm_d", "ssm_b_glu", "conv_b_dw", "conv_ln_g", "conv_ln_b", "pool_w_group", "pool_scale", "norm2",
         "final_norm")
WEIGHTS = ("norm1", "w_in", "b_gate", "ssm_a_re", "ssm_a_im", "ssm_log_dt", "ssm_b_re", "ssm_b_im", "ssm_c_re",
           "ssm_c_im", "ssm_d", "ssm_w_glu", "ssm_b_glu", "ssm_w_proj", "conv_w_dw", "conv_b_dw", "conv_ln_g",
           "conv_ln_b", "conv_w_proj", "pool_w_group", "pool_scale", "pool_w_proj", "w_out", "norm2", "ffn_w_gate",
           "ffn_w_up", "ffn_w_down", "final_norm")


def _params(vmem=True):
    return pltpu.CompilerParams(vmem_limit_bytes=VMEM_LIMIT) if vmem else None


def _mm(a, b):
    return jnp.dot(a.astype(MXU_DTYPE), b.astype(MXU_DTYPE), preferred_element_type=F32)


def _mm_nt(a, b):
    return lax.dot_general(a.astype(MXU_DTYPE), b.astype(MXU_DTYPE), (((1,), (1,)), ((), ())),
                           preferred_element_type=F32)


def _mm_tn(a, b):
    return lax.dot_general(a.astype(MXU_DTYPE), b.astype(MXU_DTYPE), (((0,), (0,)), ((), ())),
                           preferred_element_type=F32)


def _sigmoid(x):
    return jax.nn.sigmoid(x)


def _gelu(x):
    t = jnp.tanh(GELU_C * (x + 0.044715 * (x * x * x)))
    return x * (0.5 * (1.0 + t)), t


def _gelu_grad(x, t):
    return 0.5 * (1.0 + t) + 0.5 * x * (1.0 - t * t) * (GELU_C * (1.0 + 3.0 * 0.044715 * x * x))


def _colsum(v):
    return jnp.sum(v, axis=0, keepdims=True)


def _row_tile(rows, cols, itemsize=4, budget=1536 * 1024):
    best = None
    for t in range(8, rows + 1, 8):
        if rows % t == 0 and t * cols * itemsize <= budget:
            best = t
    return best if best is not None else rows


def _in_proj(l, x, norm1, w_in):
    s, d = x.shape
    nc = w_in.shape[-1]
    tm = min(512, s)

    def body(x_ref, g_ref, w_ref, z_ref, h_ref):
        @pl.when(pl.program_id(1) == 0)
        def _():
            xv = x_ref[...]
            r = lax.rsqrt(jnp.mean(xv * xv, axis=-1, keepdims=True) + EPS)
            h_ref[...] = (xv * r * g_ref[...]).astype(h_ref.dtype)

        z_ref[...] = _mm(h_ref[...], w_ref[...])

    return pl.pallas_call(
        body, name=f"in_proj_l{l}", grid=(s // tm, N_CHIPS),
        in_specs=[BS((tm, d), lambda i, j: (i, 0)), BS((None, 1, d), lambda i, j: (l, 0, 0)),
                  BS((None, d, nc), lambda i, j: (j, 0, 0))],
        out_specs=[BS((tm, nc), lambda i, j: (i, j)), BS((tm, d), lambda i, j: (i, 0))],
        out_shape=[SDS((s, N_CHIPS * nc), F32), SDS((s, d), MXU_DTYPE)],
        compiler_params=_params())(x, norm1, w_in)


def _mm_cols(a, w_ref):
    return jnp.concatenate([_mm(a, w_ref[j]) for j in range(N_CHIPS)], axis=1)


def _mm_nt_cols(dv, w_ref):
    nc = w_ref.shape[-1]
    acc = _mm_nt(dv[:, 0:nc], w_ref[0])
    for j in range(1, N_CHIPS):
        acc = acc + _mm_nt(dv[:, j * nc:(j + 1) * nc], w_ref[j])
    return acc


def _merge_values(y, hc, p, zg, wglu, bglu, wpa, wpb, wpc, lng, lnb, wgrp, scale, bg):
    v = {}
    ge, th = _gelu(y)
    t = _mm(ge, wglu) + bglu
    sg = _sigmoid(t)
    sa = ge * sg
    ya = _mm_cols(sa, wpa)
    mu = jnp.mean(hc, axis=-1, keepdims=True)
    xc = hc - mu
    r = lax.rsqrt(jnp.mean(xc * xc, axis=-1, keepdims=True) + EPS)
    xh = xc * r
    ln = xh * lng + lnb
    sl = _sigmoid(ln)
    ac = ln * sl
    yb = _mm_cols(ac, wpb)
    gw = p.shape[1] // len(POOL_WINDOWS)
    q = jnp.concatenate([_mm(p[:, k * gw:(k + 1) * gw], wgrp[k]) for k in range(len(POOL_WINDOWS))], axis=1)
    pp = q * scale
    yc = _mm_cols(pp, wpc)
    d = ya.shape[1]
    gates = [_sigmoid(zg[k] + bg[:, k * d:(k + 1) * d]) for k in range(3)]
    merged = gates[0] * ya + gates[1] * yb + gates[2] * yc
    v.update(ge=ge, th=th, sg=sg, sa=sa, ya=ya, r=r, xh=xh, ln=ln, sl=sl, ac=ac, yb=yb, q=q, pp=pp, yc=yc,
             gates=gates, merged=merged)
    return v


def _merge_specs(l, tm, d, cw):
    row = lambda n: BS((None, 1, n), lambda i: (l, 0, 0))
    return [
        BS((tm, cw), lambda i: (i, 0)),
        BS((tm, cw), lambda i: (i, 0)),
        BS((tm, cw), lambda i: (i, 0)),
        BS((tm, d), lambda i: (i, 2)), BS((tm, d), lambda i: (i, 3)), BS((tm, d), lambda i: (i, 4)),
        BS((N_CHIPS, cw // N_CHIPS, cw), lambda i: (0, 0, 0)),
        row(cw),
        BS((N_CHIPS, cw, d // N_CHIPS), lambda i: (0, 0, 0)),
        BS((N_CHIPS, cw, d // N_CHIPS), lambda i: (0, 0, 0)),
        BS((N_CHIPS, cw, d // N_CHIPS), lambda i: (0, 0, 0)),
        row(cw), row(cw),
        BS((None, 4, cw // 4, cw // 4), lambda i: (l, 0, 0, 0)),
        row(cw),
        row(3 * d),
        BS((N_CHIPS, d // N_CHIPS, d), lambda i: (0, 0, 0)),
    ]


def _merge_fwd(l, x, y, hc, p, z, fw, sp):
    s, d = x.shape
    cw = y.shape[1]
    tm = min(256, s)

    def body(x_ref, y_ref, hc_ref, p_ref, z0, z1, z2, wglu, bglu, wpa, wpb, wpc, lng, lnb, wgrp, scale, bg, wout,
             x1_ref):
        v = _merge_values(y_ref[...], hc_ref[...], p_ref[...], (z0[...], z1[...], z2[...]),
                          wglu[...].reshape(cw, cw), bglu[...], wpa, wpb, wpc, lng[...], lnb[...], wgrp, scale[...],
                          bg[...])
        x1_ref[...] = x_ref[...] + _mm(v["merged"], wout[...].reshape(d, d))

    return pl.pallas_call(
        body, name=f"merge_fwd_l{l}", grid=(s // tm,),
        in_specs=[BS((tm, d), lambda i: (i, 0))] + _merge_specs(l, tm, d, cw),
        out_specs=BS((tm, d), lambda i: (i, 0)), out_shape=SDS((s, d), F32), compiler_params=_params(),
    )(x, y, hc, p, z, z, z, fw["ssm_w_glu"], sp["ssm_b_glu"], fw["ssm_w_proj"], fw["conv_w_proj"], fw["pool_w_proj"],
      sp["conv_ln_g"], sp["conv_ln_b"], sp["pool_w_group"], sp["pool_scale"], sp["b_gate"], fw["w_out"])


def _merge_bwd(l, dx1, y, hc, p, z, fw, sp):
    s, d = dx1.shape
    cw = y.shape[1]
    tm = min(256, s)
    m = MXU_DTYPE

    def body(dx1_ref, y_ref, hc_ref, p_ref, z0, z1, z2, wglu, bglu, wpa, wpb, wpc, lng, lnb, wgrp, scale, bg, wout,
             dzg_ref, dy_ref, dhc_ref, dp_ref, merged_ref, sa_ref, ac_ref, pp_ref, ge_ref, dt_ref, dya_ref, dyb_ref,
             dyc_ref, dq_ref, dbg_ref, dbglu_ref, dlng_ref, dlnb_ref, dscale_ref):
        yv = y_ref[...]
        wg = wglu[...].reshape(cw, cw)
        v = _merge_values(yv, hc_ref[...], p_ref[...], (z0[...], z1[...], z2[...]), wg, bglu[...], wpa, wpb, wpc,
                          lng[...], lnb[...], wgrp, scale[...], bg[...])
        dm = _mm_nt(dx1_ref[...], wout[...].reshape(d, d))
        ys = (v["ya"], v["yb"], v["yc"])
        dys = []
        for k in range(3):
            gk = v["gates"][k]
            dzg_ref[:, k * d:(k + 1) * d] = dm * ys[k] * (gk * (1.0 - gk))
            dys.append((dm * gk).astype(m))
        dsa = _mm_nt_cols(dys[0], wpa)
        dac = _mm_nt_cols(dys[1], wpb)
        dpp = _mm_nt_cols(dys[2], wpc)
        ge, sg = v["ge"], v["sg"]
        dt = dsa * ge * (sg * (1.0 - sg))
        dge = dsa * sg + _mm_nt(dt, wg)
        dy_ref[...] = dge * _gelu_grad(yv, v["th"])
        ln, sl, xh = v["ln"], v["sl"], v["xh"]
        dln = dac * (sl * (1.0 + ln * (1.0 - sl)))
        dxh = dln * lng[...]
        dhc_ref[...] = v["r"] * (dxh - jnp.mean(dxh, axis=-1, keepdims=True)
                                 - xh * jnp.mean(dxh * xh, axis=-1, keepdims=True))
        dq = dpp * scale[...]
        gw = cw // len(POOL_WINDOWS)
        for k in range(len(POOL_WINDOWS)):
            dp_ref[:, k * gw:(k + 1) * gw] = _mm_nt(dq[:, k * gw:(k + 1) * gw], wgrp[k])
        merged_ref[...] = v["merged"].astype(m)
        sa_ref[...] = v["sa"].astype(m)
        ac_ref[...] = v["ac"].astype(m)
        pp_ref[...] = v["pp"].astype(m)
        ge_ref[...] = ge.astype(m)
        dt_ref[...] = dt.astype(m)
        dya_ref[...] = dys[0]
        dyb_ref[...] = dys[1]
        dyc_ref[...] = dys[2]
        dq_ref[...] = dq.astype(m)

        @pl.when(pl.program_id(0) == 0)
        def _():
            for ref in (dbg_ref, dbglu_ref, dlng_ref, dlnb_ref, dscale_ref):
                ref[...] = jnp.zeros(ref.shape, F32)

        dbg_ref[...] += _colsum(dzg_ref[...])
        dbglu_ref[...] += _colsum(dt)
        dlng_ref[...] += _colsum(dln * xh)
        dlnb_ref[...] += _colsum(dln)
        dscale_ref[...] += _colsum(dpp * v["q"])

    tile = lambda n: BS((tm, n), lambda i: (i, 0))
    acc = lambda n: BS((1, n), lambda i: (0, 0))
    outs = pl.pallas_call(
        body, name=f"merge_bwd_l{l}", grid=(s // tm,),
        in_specs=[tile(d)] + _merge_specs(l, tm, d, cw),
        out_specs=[tile(3 * d), tile(cw), tile(cw), tile(cw), tile(d), tile(cw), tile(cw), tile(cw), tile(cw), tile(cw),
                   tile(d), tile(d), tile(d), tile(cw), acc(3 * d), acc(cw), acc(cw), acc(cw), acc(cw)],
        out_shape=[SDS((s, 3 * d), F32), SDS((s, cw), F32), SDS((s, cw), F32), SDS((s, cw), F32), SDS((s, d), m),
                   SDS((s, cw), m), SDS((s, cw), m), SDS((s, cw), m), SDS((s, cw), m), SDS((s, cw), m), SDS((s, d), m),
                   SDS((s, d), m), SDS((s, d), m), SDS((s, cw), m), SDS((1, 3 * d), F32), SDS((1, cw), F32),
                   SDS((1, cw), F32), SDS((1, cw), F32), SDS((1, cw), F32)],
        compiler_params=_params(),
    )(dx1, y, hc, p, z, z, z, fw["ssm_w_glu"], sp["ssm_b_glu"], fw["ssm_w_proj"], fw["conv_w_proj"], fw["pool_w_proj"],
      sp["conv_ln_g"], sp["conv_ln_b"], sp["pool_w_group"], sp["pool_scale"], sp["b_gate"], fw["w_out"])
    names = ("dzg", "dy", "dhc", "dp", "merged", "sa", "ac", "pp", "ge", "dt", "dya", "dyb", "dyc", "dq", "db_gate",
             "db_glu", "dln_g", "dln_b", "dscale")
    return dict(zip(names, outs))


def _ffn_fwd(l, x1, norm2, wg, wu, wd):
    s, d = x1.shape
    hc = wd.shape[1]
    tm = min(512, s)

    def body(x_ref, g_ref, wg_ref, wu_ref, wd_ref, o_ref, h_scr):
        @pl.when(pl.program_id(1) == 0)
        def _():
            xv = x_ref[...]
            r = lax.rsqrt(jnp.mean(xv * xv, axis=-1, keepdims=True) + EPS)
            h_scr[...] = (xv * r * g_ref[...]).astype(h_scr.dtype)
            o_ref[...] = xv

        h = h_scr[...]
        gate = _mm_nt(h, wg_ref[...])
        up = _mm_nt(h, wu_ref[...])
        o_ref[...] += _mm(gate * _sigmoid(gate) * up, wd_ref[...])

    return pl.pallas_call(
        body, name=f"ffn_fwd_l{l}", grid=(s // tm, N_CHIPS),
        in_specs=[BS((tm, d), lambda i, j: (i, 0)), BS((None, 1, d), lambda i, j: (l, 0, 0)),
                  BS((None, hc, d), lambda i, j: (j, 0, 0)), BS((None, hc, d), lambda i, j: (j, 0, 0)),
                  BS((None, hc, d), lambda i, j: (j, 0, 0))],
        out_specs=BS((tm, d), lambda i, j: (i, 0)), out_shape=SDS((s, d), F32),
        scratch_shapes=[pltpu.VMEM((tm, d), MXU_DTYPE)], compiler_params=_params())(x1, norm2, wg, wu, wd)


def _ffn_bwd(l, x1, dx2, norm2, wg, wu, wd):
    s, d = x1.shape
    hc = wd.shape[1]
    tm = min(512, s)
    m = MXU_DTYPE
    last = N_CHIPS - 1

    def body(x_ref, dx2_ref, g_ref, wg_ref, wu_ref, wd_ref, dx1_ref, h_ref, act_ref, dgate_ref, dup_ref, dn_ref,
             dh_scr, dxb_scr):
        i, j = pl.program_id(0), pl.program_id(1)

        @pl.when(j == 0)
        def _():
            xv = x_ref[...]
            r = lax.rsqrt(jnp.mean(xv * xv, axis=-1, keepdims=True) + EPS)
            h_ref[...] = (xv * r * g_ref[...]).astype(m)
            dxb_scr[...] = dx2_ref[...].astype(m)
            dh_scr[...] = jnp.zeros(dh_scr.shape, F32)

        @pl.when((i == 0) & (j == 0))
        def _():
            dn_ref[...] = jnp.zeros(dn_ref.shape, F32)

        h = h_ref[...]
        gate = _mm_nt(h, wg_ref[...])
        up = _mm_nt(h, wu_ref[...])
        sg = _sigmoid(gate)
        silu = gate * sg
        act_ref[...] = (silu * up).astype(m)
        dact = _mm_nt(dxb_scr[...], wd_ref[...])
        dup = (dact * silu).astype(m)
        dgate = (dact * up * (sg * (1.0 + gate * (1.0 - sg)))).astype(m)
        dup_ref[...] = dup
        dgate_ref[...] = dgate
        dh_scr[...] += _mm(dgate, wg_ref[...]) + _mm(dup, wu_ref[...])

        @pl.when(j == last)
        def _():
            xv = x_ref[...]
            r = lax.rsqrt(jnp.mean(xv * xv, axis=-1, keepdims=True) + EPS)
            xh = xv * r
            dh = dh_scr[...]
            dn_ref[...] += _colsum(dh * xh)
            dxh = dh * g_ref[...]
            dx1_ref[...] = dx2_ref[...] + r * (dxh - xh * jnp.mean(dxh * xh, axis=-1, keepdims=True))

    chunk = BS((None, tm, hc), lambda i, j: (j, i, 0))
    outs = pl.pallas_call(
        body, name=f"ffn_bwd_l{l}", grid=(s // tm, N_CHIPS),
        in_specs=[BS((tm, d), lambda i, j: (i, 0)), BS((tm, d), lambda i, j: (i, 0)),
                  BS((None, 1, d), lambda i, j: (l, 0, 0)),
                  BS((None, hc, d), lambda i, j: (j, 0, 0)), BS((None, hc, d), lambda i, j: (j, 0, 0)),
                  BS((None, hc, d), lambda i, j: (j, 0, 0))],
        out_specs=[BS((tm, d), lambda i, j: (i, 0)), BS((tm, d), lambda i, j: (i, 0)), chunk, chunk, chunk,
                   BS((1, d), lambda i, j: (0, 0))],
        out_shape=[SDS((s, d), F32), SDS((s, d), m), SDS((N_CHIPS, s, hc), m), SDS((N_CHIPS, s, hc), m),
                   SDS((N_CHIPS, s, hc), m), SDS((1, d), F32)],
        scratch_shapes=[pltpu.VMEM((tm, d), F32), pltpu.VMEM((tm, d), m)], compiler_params=_params(),
    )(x1, dx2, norm2, wg, wu, wd)
    return dict(zip(("dx1", "h2", "act", "dgate", "dup", "dnorm2"), outs))


def _loss_head(x, target, gf):
    s, d = x.shape
    tm = min(512, s)

    def body(x_ref, t_ref, g_ref, dx_ref, loss_ref, dg_ref):
        @pl.when(pl.program_id(0) == 0)
        def _():
            loss_ref[...] = jnp.zeros(loss_ref.shape, F32)
            dg_ref[...] = jnp.zeros(dg_ref.shape, F32)

        xv = x_ref[...]
        r = lax.rsqrt(jnp.mean(xv * xv, axis=-1, keepdims=True) + EPS)
        xh = xv * r
        err = xh * g_ref[...] - t_ref[...]
        loss_ref[...] += 0.5 * jnp.sum(jnp.mean(err * err, axis=-1, keepdims=True), axis=0, keepdims=True)
        dyv = err * (1.0 / d)
        dg_ref[...] += _colsum(dyv * xh)
        dxh = dyv * g_ref[...]
        dx_ref[...] = r * (dxh - xh * jnp.mean(dxh * xh, axis=-1, keepdims=True))

    return pl.pallas_call(
        body, name="loss_head", grid=(s // tm,),
        in_specs=[BS((tm, d), lambda i: (i, 0)), BS((tm, d), lambda i: (i, 0)), BS((1, d), lambda i: (0, 0))],
        out_specs=[BS((tm, d), lambda i: (i, 0)), BS((1, 1), lambda i: (0, 0)), BS((1, d), lambda i: (0, 0))],
        out_shape=[SDS((s, d), F32), SDS((1, 1), F32), SDS((1, d), F32)], compiler_params=_params())(x, target, gf)


def _in_proj_bwd(l, dres, x, norm1, w_in, du_a, dv1, dv2, du_c, dzg):
    s, d = x.shape
    nc = w_in.shape[-1]
    tm = min(256, s)
    m = MXU_DTYPE

    def body(dres_ref, x_ref, g_ref, w_ref, a_ref, b1_ref, b2_ref, c_ref, g3_ref, dx_ref, dz_ref, dn_ref):
        @pl.when(pl.program_id(0) == 0)
        def _():
            dn_ref[...] = jnp.zeros(dn_ref.shape, F32)

        dz = jnp.concatenate([a_ref[...], b1_ref[...], b2_ref[...], c_ref[...], g3_ref[...]], axis=1).astype(m)
        dz_ref[...] = dz
        dh = _mm_nt_cols(dz, w_ref)
        xv = x_ref[...]
        r = lax.rsqrt(jnp.mean(xv * xv, axis=-1, keepdims=True) + EPS)
        xh = xv * r
        dn_ref[...] += _colsum(dh * xh)
        dxh = dh * g_ref[...]
        dx_ref[...] = dres_ref[...] + r * (dxh - xh * jnp.mean(dxh * xh, axis=-1, keepdims=True))

    tile = lambda n: BS((tm, n), lambda i: (i, 0))
    return pl.pallas_call(
        body, name=f"in_proj_bwd_l{l}", grid=(s // tm,),
        in_specs=[tile(d), tile(d), BS((None, 1, d), lambda i: (l, 0, 0)),
                  BS((N_CHIPS, d, nc), lambda i: (0, 0, 0)),
                  tile(du_a.shape[1]), tile(dv1.shape[1]), tile(dv2.shape[1]), tile(du_c.shape[1]), tile(dzg.shape[1])],
        out_specs=[tile(d), tile(N_CHIPS * nc), BS((1, d), lambda i: (0, 0))],
        out_shape=[SDS((s, d), F32), SDS((s, N_CHIPS * nc), m), SDS((1, d), F32)], compiler_params=_params(),
    )(dres, x, norm1, w_in, du_a, dv1, dv2, du_c, dzg)


def _tn_matmul(name, a, a_spec, b, b_spec, out_shape, out_spec, grid, wire=True):
    last = grid[1] - 1

    def body(a_ref, b_ref, o_ref, *wire_ref):
        @pl.when(pl.program_id(1) == 0)
        def _():
            o_ref[...] = jnp.zeros(o_ref.shape, F32)

        o_ref[...] += _mm_tn(a_ref[...], b_ref[...])

        if wire:
            @pl.when(pl.program_id(1) == last)
            def _():
                wire_ref[0][...] = o_ref[...].astype(WIRE_DTYPE)

    if not wire:
        return pl.pallas_call(body, name=name, grid=grid, in_specs=[a_spec, b_spec], out_specs=out_spec,
                              out_shape=out_shape, compiler_params=_params())(a, b)
    return pl.pallas_call(body, name=name, grid=grid, in_specs=[a_spec, b_spec], out_specs=[out_spec, out_spec],
                          out_shape=[out_shape, SDS(out_shape.shape, WIRE_DTYPE)], compiler_params=_params())(a, b)


def _scan_consts(pw_ref, lanes, reverse):
    sgn = -1.0 if reverse else 1.0
    steps = [(k, pw_ref[2 * i], sgn * pw_ref[2 * i + 1]) for i, k in enumerate((1, 2, 4))]
    c = 4 if reverse else 3
    return steps, pw_ref[2 * c], sgn * pw_ref[2 * c + 1]


def _scan_block(br, bi, steps, row, reverse):
    for k, ar, ai in steps:
        if reverse:
            mask, sh = row < 8 - k, 8 - k
        else:
            mask, sh = row >= k, k
        sr = jnp.where(mask, pltpu.roll(br, sh, 0), 0.0)
        si = jnp.where(mask, pltpu.roll(bi, sh, 0), 0.0)
        br, bi = br + ar * sr - ai * si, bi + ar * si + ai * sr
    return br, bi


def _ssm_fwd(l, z, bblk_re, bblk_im, cblk_re, cblk_im, pw, dskip):
    s = z.shape[0]
    gc = bblk_re.shape[1]
    gl = bblk_re.shape[2]
    nblk = bblk_re.shape[0]

    def body(u_ref, bre, bim, cre, cim, pw_ref, d_ref, hre, him, y_ref):
        u = u_ref[...]
        hre[...] = _mm(u, bre[...])
        him[...] = _mm(u, bim[...])
        row = lax.broadcasted_iota(jnp.int32, (8, gl), 0)
        steps, car, cai = _scan_consts(pw_ref, gl, False)

        def step(i, carry):
            cr, ci = carry
            r0 = pl.multiple_of(i * 8, 8)
            br, bi = _scan_block(hre[pl.ds(r0, 8), :], him[pl.ds(r0, 8), :], steps, row, False)
            hr = br + car * cr - cai * ci
            hi = bi + car * ci + cai * cr
            hre[pl.ds(r0, 8), :] = hr
            him[pl.ds(r0, 8), :] = hi
            return jnp.broadcast_to(hr[7:8, :], (8, gl)), jnp.broadcast_to(hi[7:8, :], (8, gl))

        zero = jnp.zeros((8, gl), F32)
        lax.fori_loop(0, s // 8, step, (zero, zero))
        y_ref[...] = _mm_nt(hre[...], cre[...]) - _mm_nt(him[...], cim[...]) + d_ref[...] * u

    return pl.pallas_call(
        body, name=f"ssm_fwd_l{l}", grid=(nblk,),
        in_specs=[BS((s, gc), lambda k: (0, k)), BS((None, gc, gl), lambda k: (k, 0, 0)),
                  BS((None, gc, gl), lambda k: (k, 0, 0)), BS((None, gc, gl), lambda k: (k, 0, 0)),
                  BS((None, gc, gl), lambda k: (k, 0, 0)), BS((10, 8, gl), lambda k: (0, 0, k)),
                  BS((1, gc), lambda k: (0, k))],
        out_specs=[BS((s, gl), lambda k: (0, k)), BS((s, gl), lambda k: (0, k)), BS((s, gc), lambda k: (0, k))],
        out_shape=[SDS((s, nblk * gl), F32), SDS((s, nblk * gl), F32), SDS((s, nblk * gc), F32)],
        compiler_params=_params())(z, bblk_re, bblk_im, cblk_re, cblk_im, pw, dskip)


def _ssm_bwd(l, dy, z, hre, him, bblk_re, bblk_im, cblk_re, cblk_im, pw, dskip):
    s = z.shape[0]
    nblk, gc, gl = bblk_re.shape

    def body(dy_ref, u_ref, hre_ref, him_ref, bre, bim, cre, cim, pw_ref, d_ref,
             du_ref, dbre_ref, dbim_ref, dcre_ref, dcim_ref, dar_ref, dai_ref, dd_ref, gre, gim):
        dyv = dy_ref[...]
        u = u_ref[...]
        gre[...] = _mm(dyv, cre[...])
        gim[...] = -_mm(dyv, cim[...])
        dcre_ref[...] = _mm_tn(dyv, hre_ref[...])
        dcim_ref[...] = -_mm_tn(dyv, him_ref[...])
        dd_ref[...] = _colsum(dyv * u)
        row = lax.broadcasted_iota(jnp.int32, (8, gl), 0)
        steps, car, cai = _scan_consts(pw_ref, gl, True)
        n8 = s // 8

        def step(ii, carry):
            cr, ci, accr, acci = carry
            i = n8 - 1 - ii
            r0 = pl.multiple_of(i * 8, 8)
            br, bi = _scan_block(gre[pl.ds(r0, 8), :], gim[pl.ds(r0, 8), :], steps, row, True)
            dr = br + car * cr - cai * ci
            di = bi + car * ci + cai * cr
            gre[pl.ds(r0, 8), :] = dr
            gim[pl.ds(r0, 8), :] = di
            rp = pl.multiple_of(jnp.maximum(i - 1, 0) * 8, 8)
            keep = jnp.where(i > 0, 1.0, 0.0)
            pr = jnp.where(row >= 1, pltpu.roll(hre_ref[pl.ds(r0, 8), :], 1, 0),
                           keep * pltpu.roll(hre_ref[pl.ds(rp, 8), :], 1, 0))
            pi = jnp.where(row >= 1, pltpu.roll(him_ref[pl.ds(r0, 8), :], 1, 0),
                           keep * pltpu.roll(him_ref[pl.ds(rp, 8), :], 1, 0))
            accr = accr + dr * pr + di * pi
            acci = acci + di * pr - dr * pi
            return (jnp.broadcast_to(dr[0:1, :], (8, gl)), jnp.broadcast_to(di[0:1, :], (8, gl)), accr, acci)

        zero = jnp.zeros((8, gl), F32)
        _, _, accr, acci = lax.fori_loop(0, n8, step, (zero, zero, zero, zero))
        dar_ref[...] = _colsum(accr)
        dai_ref[...] = _colsum(acci)
        dbr = gre[...]
        dbi = gim[...]
        du_ref[...] = dyv * d_ref[...] + _mm_nt(dbr, bre[...]) + _mm_nt(dbi, bim[...])
        dbre_ref[...] = _mm_tn(u, dbr)
        dbim_ref[...] = _mm_tn(u, dbi)

    col = lambda n: BS((s, n), lambda k: (0, k))
    blk = lambda a, b: BS((None, a, b), lambda k: (k, 0, 0))
    outs = pl.pallas_call(
        body, name=f"ssm_bwd_l{l}", grid=(nblk,),
        in_specs=[col(gc), col(gc), col(gl), col(gl), blk(gc, gl), blk(gc, gl), blk(gc, gl), blk(gc, gl),
                  BS((10, 8, gl), lambda k: (0, 0, k)), BS((1, gc), lambda k: (0, k))],
        out_specs=[col(gc), blk(gc, gl), blk(gc, gl), blk(gc, gl), blk(gc, gl), BS((1, gl), lambda k: (0, k)),
                   BS((1, gl), lambda k: (0, k)), BS((1, gc), lambda k: (0, k))],
        out_shape=[SDS((s, nblk * gc), F32), SDS((nblk, gc, gl), F32), SDS((nblk, gc, gl), F32),
                   SDS((nblk, gc, gl), F32), SDS((nblk, gc, gl), F32), SDS((1, nblk * gl), F32),
                   SDS((1, nblk * gl), F32), SDS((1, nblk * gc), F32)],
        scratch_shapes=[pltpu.VMEM((s, gl), F32), pltpu.VMEM((s, gl), F32)], compiler_params=_params(),
    )(dy, z, hre, him, bblk_re, bblk_im, cblk_re, cblk_im, pw, dskip)
    return dict(zip(("du", "dbblk_re", "dbblk_im", "dcblk_re", "dcblk_im", "dabar_re", "dabar_im", "dd"), outs))


def _conv_fwd(l, z, wdw, bdw):
    s = z.shape[0]
    cw = wdw.shape[1]
    lb = 128
    tr = min(256, s)
    off1 = cw // lb
    off2 = 2 * cw // lb

    def body(v1_ref, v2_ref, w_ref, b_ref, hc_ref, scr):
        scr[0:CONV_PAD, :] = jnp.zeros((CONV_PAD, lb), F32)
        scr[CONV_PAD:, :] = v1_ref[...] * _sigmoid(v2_ref[...])
        for t in range(s // tr):
            acc = jnp.broadcast_to(b_ref[...], (tr, lb))
            for k in range(CONV_KERNEL):
                acc = acc + w_ref[pl.ds(k, 1), :] * scr[pl.ds(t * tr + CONV_PAD - (CONV_KERNEL - 1) + k, tr), :]
            hc_ref[pl.ds(t * tr, tr), :] = acc

    return pl.pallas_call(
        body, name=f"conv_fwd_l{l}", grid=(cw // lb,),
        in_specs=[BS((s, lb), lambda k: (0, off1 + k)), BS((s, lb), lambda k: (0, off2 + k)),
                  BS((CONV_KERNEL, lb), lambda k: (0, k)), BS((1, lb), lambda k: (0, k))],
        out_specs=BS((s, lb), lambda k: (0, k)), out_shape=SDS((s, cw), F32),
        scratch_shapes=[pltpu.VMEM((s + CONV_PAD, lb), F32)], compiler_params=_params())(z, z, wdw, bdw)


def _conv_bwd(l, dhc, z, wdw):
    s = z.shape[0]
    cw = wdw.shape[1]
    lb = 128
    tr = min(256, s)
    off1 = cw // lb
    off2 = 2 * cw // lb
    nb = cw // lb

    def body(d_ref, v1_ref, v2_ref, w_ref, dv1_ref, dv2_ref, dw_ref, db_ref, hpad, dpad):
        v1 = v1_ref[...]
        sg = _sigmoid(v2_ref[...])
        dv = d_ref[...]
        hpad[0:CONV_PAD, :] = jnp.zeros((CONV_PAD, lb), F32)
        hpad[CONV_PAD:, :] = v1 * sg
        dpad[0:s, :] = dv
        dpad[s:, :] = jnp.zeros((CONV_PAD, lb), F32)
        db_ref[...] = _colsum(dv)
        dws = [jnp.zeros((1, lb), F32) for _ in range(CONV_KERNEL)]
        for t in range(s // tr):
            dt = d_ref[pl.ds(t * tr, tr), :]
            acc = jnp.zeros((tr, lb), F32)
            for k in range(CONV_KERNEL):
                acc = acc + w_ref[pl.ds(k, 1), :] * dpad[pl.ds(t * tr + (CONV_KERNEL - 1) - k, tr), :]
                dws[k] = dws[k] + _colsum(dt * hpad[pl.ds(t * tr + CONV_PAD - (CONV_KERNEL - 1) + k, tr), :])
            sgt = _sigmoid(v2_ref[pl.ds(t * tr, tr), :])
            v1t = v1_ref[pl.ds(t * tr, tr), :]
            dv1_ref[pl.ds(t * tr, tr), :] = acc * sgt
            dv2_ref[pl.ds(t * tr, tr), :] = acc * v1t * (sgt * (1.0 - sgt))
        for k in range(CONV_KERNEL):
            dw_ref[pl.ds(k, 1), :] = dws[k]

    return pl.pallas_call(
        body, name=f"conv_bwd_l{l}", grid=(nb,),
        in_specs=[BS((s, lb), lambda k: (0, k)), BS((s, lb), lambda k: (0, off1 + k)),
                  BS((s, lb), lambda k: (0, off2 + k)), BS((CONV_KERNEL, lb), lambda k: (0, k))],
        out_specs=[BS((s, lb), lambda k: (0, k)), BS((s, lb), lambda k: (0, k)),
                   BS((CONV_KERNEL, lb), lambda k: (0, k)), BS((1, lb), lambda k: (0, k))],
        out_shape=[SDS((s, cw), F32), SDS((s, cw), F32), SDS((CONV_KERNEL, cw), F32), SDS((1, cw), F32)],
        scratch_shapes=[pltpu.VMEM((s + CONV_PAD, lb), F32), pltpu.VMEM((s + CONV_PAD, lb), F32)],
        compiler_params=_params())(dhc, z, z, wdw)


def _pool_window(k):
    return jnp.where(k == 0, float(POOL_WINDOWS[0]),
                     jnp.where(k == 1, float(POOL_WINDOWS[1]),
                               jnp.where(k == 2, float(POOL_WINDOWS[2]), float(POOL_WINDOWS[3]))))


def _pool_fwd(l, z, pw_width):
    s = z.shape[0]
    lb = pw_width // len(POOL_WINDOWS)
    off = 3 * pw_width // lb

    def body(u_ref, p_ref):
        k = pl.program_id(0)
        u = u_ref[...]
        row = lax.broadcasted_iota(jnp.int32, (s, lb), 0)
        sums = [u]
        for sh in (1, 2, 4, 8):
            prev = sums[-1]
            sums.append(prev + jnp.where(row >= sh, pltpu.roll(prev, sh, 0), 0.0))
        sel = jnp.where(k == 0, sums[1], jnp.where(k == 1, sums[2], jnp.where(k == 2, sums[3], sums[4])))
        cnt = jnp.minimum((row + 1).astype(F32), _pool_window(k))
        p_ref[...] = sel / cnt - u

    return pl.pallas_call(
        body, name=f"pool_fwd_l{l}", grid=(len(POOL_WINDOWS),),
        in_specs=[BS((s, lb), lambda k: (0, off + k))], out_specs=BS((s, lb), lambda k: (0, k)),
        out_shape=SDS((s, pw_width), F32), compiler_params=_params())(z)


def _pool_bwd(l, dp):
    s, width = dp.shape
    lb = width // len(POOL_WINDOWS)

    def body(d_ref, du_ref):
        k = pl.program_id(0)
        dv = d_ref[...]
        row = lax.broadcasted_iota(jnp.int32, (s, lb), 0)
        cnt = jnp.minimum((row + 1).astype(F32), _pool_window(k))
        sums = [dv / cnt]
        for sh in (1, 2, 4, 8):
            prev = sums[-1]
            sums.append(prev + jnp.where(row < s - sh, pltpu.roll(prev, s - sh, 0), 0.0))
        sel = jnp.where(k == 0, sums[1], jnp.where(k == 1, sums[2], jnp.where(k == 2, sums[3], sums[4])))
        du_ref[...] = sel - dv

    return pl.pallas_call(
        body, name=f"pool_bwd_l{l}", grid=(len(POOL_WINDOWS),),
        in_specs=[BS((s, lb), lambda k: (0, k))], out_specs=BS((s, lb), lambda k: (0, k)),
        out_shape=SDS((s, width), F32), compiler_params=_params())(dp)


def _zoh(a_re, a_im, log_dt):
    dt = jnp.exp(log_dt)
    mag = jnp.exp(dt * a_re)
    ang = dt * a_im
    abar_re = mag * jnp.cos(ang)
    abar_im = mag * jnp.sin(ang)
    den = a_re * a_re + a_im * a_im
    nr = abar_re - 1.0
    ni = abar_im
    f_re = (nr * a_re + ni * a_im) / den
    f_im = (ni * a_re - nr * a_im) / den
    return abar_re, abar_im, f_re, f_im


def _zoh_fwd(l, a_re, a_im, log_dt):
    def body(ar, ai, ld, o0, o1, o2, o3):
        for ref, val in zip((o0, o1, o2, o3), _zoh(ar[...], ai[...], ld[...])):
            ref[...] = val

    return pl.pallas_call(body, name=f"zoh_fwd_l{l}", out_shape=[SDS(a_re.shape, F32)] * 4)(a_re, a_im, log_dt)


def _zoh_bwd(l, a_re, a_im, log_dt, cts):
    def body(ar, ai, ld, c0, c1, c2, c3, dar, dai, dld):
        _, vjp = jax.vjp(_zoh, ar[...], ai[...], ld[...])
        g = vjp((c0[...], c1[...], c2[...], c3[...]))
        dar[...] = g[0]
        dai[...] = g[1]
        dld[...] = g[2]

    return pl.pallas_call(body, name=f"zoh_bwd_l{l}",
                          out_shape=[SDS(a_re.shape, F32), SDS(a_re.shape, F32), SDS(log_dt.shape, F32)],
                          )(a_re, a_im, log_dt, *cts)


def _bbar_fwd(l, f_re, f_im, b_re, b_im):
    g, p, n = b_re.shape[1:]

    def body(fr, fi, br, bi, o_re, o_im):
        o_re[...] = (fr[...] * br[...] - fi[...] * bi[...]).astype(o_re.dtype)
        o_im[...] = (fr[...] * bi[...] + fi[...] * br[...]).astype(o_im.dtype)

    whole = lambda shp: BS(shp, lambda i: (0,) * len(shp))
    layer = BS((None, g, p, n), lambda i: (l, 0, 0, 0))
    return pl.pallas_call(body, name=f"bbar_fwd_l{l}", grid=(1,),
                          in_specs=[whole((g, 1, n)), whole((g, 1, n)), layer, layer],
                          out_specs=[whole((g, p, n))] * 2,
                          out_shape=[SDS((g, p, n), MXU_DTYPE)] * 2)(f_re, f_im, b_re, b_im)


def _bbar_bwd(l, f_re, f_im, b_re, b_im, d_re, d_im):
    g, p, n = b_re.shape[1:]

    def body(fr, fi, br, bi, dr, di, dfr, dfi, dbr, dbi):
        dfr[...] = jnp.sum(dr[...] * br[...] + di[...] * bi[...], axis=1, keepdims=True)
        dfi[...] = jnp.sum(di[...] * br[...] - dr[...] * bi[...], axis=1, keepdims=True)
        dbr[...] = fr[...] * dr[...] + fi[...] * di[...]
        dbi[...] = fr[...] * di[...] - fi[...] * dr[...]

    whole = lambda shp: BS(shp, lambda i: (0,) * len(shp))
    layer = BS((None, g, p, n), lambda i: (l, 0, 0, 0))
    return pl.pallas_call(body, name=f"bbar_bwd_l{l}", grid=(1,),
                          in_specs=[whole((g, 1, n)), whole((g, 1, n)), layer, layer, whole((g, p, n)),
                                    whole((g, p, n))],
                          out_specs=[whole((g, 1, n)), whole((g, 1, n)), whole((g, p, n)), whole((g, p, n))],
                          out_shape=[SDS((g, 1, n), F32), SDS((g, 1, n), F32), SDS((g, p, n), F32),
                                     SDS((g, p, n), F32)])(f_re, f_im, b_re, b_im, d_re, d_im)


def _powers(l, abar_re, abar_im):
    lanes = abar_re.shape[1]

    def body(ar_ref, ai_ref, o_ref):
        ar, ai = ar_ref[...], ai_ref[...]
        pows = [(ar, ai)]
        for _ in range(7):
            pr, pi = pows[-1]
            pows.append((pr * ar - pi * ai, pr * ai + pi * ar))
        row = lax.broadcasted_iota(jnp.int32, (8, lanes), 0)
        for i, k in enumerate((1, 2, 4)):
            o_ref[2 * i] = jnp.broadcast_to(pows[k - 1][0], (8, lanes))
            o_ref[2 * i + 1] = jnp.broadcast_to(pows[k - 1][1], (8, lanes))
        for slot, order in ((3, range(8)), (4, range(7, -1, -1))):
            vr = jnp.zeros((8, lanes), F32)
            vi = jnp.zeros((8, lanes), F32)
            for r, e in enumerate(order):
                vr = jnp.where(row == r, pows[e][0], vr)
                vi = jnp.where(row == r, pows[e][1], vi)
            o_ref[2 * slot] = vr
            o_ref[2 * slot + 1] = vi

    return pl.pallas_call(body, name=f"powers_l{l}", out_shape=SDS((10, 8, lanes), F32))(abar_re, abar_im)


def _block_diag(v):
    g, a, b = v.shape
    eye = jnp.eye(8, dtype=v.dtype)
    out = jnp.einsum("kgab,gh->kgahb", v.reshape(g // 8, 8, a, b), eye)
    return out.reshape(g // 8, 8 * a, 8 * b)


def _block_diag_extract(blk, a, b):
    n = blk.shape[0]
    v = blk.reshape(n, 8, a, 8, b)
    return jnp.einsum("kgahb,gh->kgab", v, jnp.eye(8, dtype=blk.dtype)).reshape(n * 8, a, b)


def _ssm_prepare(l, prm):
    g, n, p = SSM_GROUPS, SSM_STATE, SSM_GROUP
    a_re, a_im = prm["ssm_a_re"][l], prm["ssm_a_im"][l]
    log_dt = prm["ssm_log_dt"][l].reshape(g, 1)
    abar_re, abar_im, f_re, f_im = _zoh_fwd(l, a_re, a_im, log_dt)
    f_re, f_im = f_re.reshape(g, 1, n), f_im.reshape(g, 1, n)
    bbar_re, bbar_im = _bbar_fwd(l, f_re, f_im, prm["ssm_b_re"], prm["ssm_b_im"])
    pw = _powers(l, abar_re.reshape(1, g * n), abar_im.reshape(1, g * n))
    return dict(a_re=a_re, a_im=a_im, log_dt=log_dt, f_re=f_re, f_im=f_im,
                bblk_re=_block_diag(bbar_re), bblk_im=_block_diag(bbar_im),
                cblk_re=_block_diag(prm["ssm_c_re"][l].astype(MXU_DTYPE)),
                cblk_im=_block_diag(prm["ssm_c_im"][l].astype(MXU_DTYPE)), pw=pw,
                dskip=prm["ssm_d"][l].reshape(1, g * p))


def _ssm_param_grads(l, sd, r, prm):
    g, n, p = SSM_GROUPS, SSM_STATE, SSM_GROUP
    dbbar_re = _block_diag_extract(r["dbblk_re"], p, n)
    dbbar_im = _block_diag_extract(r["dbblk_im"], p, n)
    dfr, dfi, db_re, db_im = _bbar_bwd(l, sd["f_re"], sd["f_im"], prm["ssm_b_re"], prm["ssm_b_im"], dbbar_re, dbbar_im)
    cts = (r["dabar_re"].reshape(g, n), r["dabar_im"].reshape(g, n), dfr.reshape(g, n), dfi.reshape(g, n))
    da_re, da_im, dlog_dt = _zoh_bwd(l, sd["a_re"], sd["a_im"], sd["log_dt"], cts)
    return dict(ssm_a_re=da_re, ssm_a_im=da_im, ssm_log_dt=dlog_dt.reshape(g), ssm_b_re=db_re, ssm_b_im=db_im,
                ssm_c_re=_block_diag_extract(r["dcblk_re"], p, n), ssm_c_im=_block_diag_extract(r["dcblk_im"], p, n),
                ssm_d=r["dd"].reshape(g, p))


def _ffn_weight_grads(l, fb, dx2, s):
    d = dx2.shape[1]
    hcn = fb["act"].shape[-1]
    g = {}
    for name, key, rhs, ts in (("ffn_w_gate", "dgate", fb["h2"], s), ("ffn_w_up", "dup", fb["h2"], s),
                               ("ffn_w_down", "act", dx2, min(1024, s))):
        g[name] = _tn_matmul(f"d{name}_l{l}", fb[key], BS((None, ts, hcn), lambda j, t: (j, t, 0)), rhs,
                             BS((ts, d), lambda j, t: (t, 0)), SDS((N_CHIPS, hcn, d), F32),
                             BS((None, hcn, d), lambda j, t: (j, 0, 0)), (N_CHIPS, s // ts))
    return g


def _in_weight_grad(l, h, dz):
    s, d = h.shape
    ncw = dz.shape[1] // N_CHIPS
    return _tn_matmul(f"dw_in_l{l}", h, BS((s, d), lambda j, t: (0, 0)), dz, BS((s, ncw), lambda j, t: (0, j)),
                      SDS((N_CHIPS, d, ncw), F32), BS((None, d, ncw), lambda j, t: (j, 0, 0)), (N_CHIPS, 1))


def _fused_tn(name, pairs, kinds, s, wire):
    ts = min(512, s)
    n = len(pairs)

    def shape_of(a, b, kind):
        k, m = a.shape[1], b.shape[1]
        if kind == "rows":
            return (N_CHIPS, k // N_CHIPS, m)
        if kind == "cols":
            return (N_CHIPS, k, m // N_CHIPS)
        return (k // 128, 128, 128)

    shapes = [shape_of(a, b, kind) for (a, b), kind in zip(pairs, kinds)]
    last = s // ts - 1

    def body(*refs):
        ins, outs = refs[:2 * n], refs[2 * n:]
        first = pl.program_id(0) == 0
        for i, kind in enumerate(kinds):
            a, b = ins[2 * i][...], ins[2 * i + 1][...]
            o_ref = outs[i]

            @pl.when(first)
            def _():
                o_ref[...] = jnp.zeros(o_ref.shape, F32)

            if kind == "rows":
                o_ref[...] += _mm_tn(a, b).reshape(o_ref.shape)
            elif kind == "cols":
                full = _mm_tn(a, b)
                nc = o_ref.shape[2]
                for j in range(N_CHIPS):
                    o_ref[j] += full[:, j * nc:(j + 1) * nc]
            else:
                for k in range(o_ref.shape[0]):
                    o_ref[k] += _mm_tn(a[:, k * 128:(k + 1) * 128], b[:, k * 128:(k + 1) * 128])
        if wire:
            @pl.when(pl.program_id(0) == last)
            def _():
                for i in range(n):
                    outs[n + i][...] = outs[i][...].astype(WIRE_DTYPE)

    whole = lambda shp: BS(shp, lambda t: (0,) * len(shp))
    out_shape = [SDS(shp, F32) for shp in shapes] + ([SDS(shp, WIRE_DTYPE) for shp in shapes] if wire else [])
    outs = pl.pallas_call(
        body, name=name, grid=(s // ts,),
        in_specs=[BS((ts, v.shape[1]), lambda t: (t, 0)) for pair in pairs for v in pair],
        out_specs=[whole(o.shape) for o in out_shape], out_shape=out_shape, compiler_params=_params(),
    )(*[v for pair in pairs for v in pair])
    return [(outs[i], outs[n + i]) for i in range(n)] if wire else list(outs)


def _mixer_weight_grads(l, sv, mb, dx1, s):
    g = {}
    (g["w_out"], g["ssm_w_glu"]) = _fused_tn(f"dw_out_glu_l{l}", [(mb["merged"], dx1), (mb["ge"], mb["dt"])],
                                            ("rows", "rows"), s, True)
    (g["ssm_w_proj"], g["conv_w_proj"], g["pool_w_proj"]) = _fused_tn(
        f"dw_proj_l{l}", [(mb["sa"], mb["dya"]), (mb["ac"], mb["dyb"]), (mb["pp"], mb["dyc"])],
        ("cols", "cols", "cols"), s, True)
    (dwgrp,) = _fused_tn(f"dpool_w_group_l{l}", [(sv["p"], mb["dq"])], ("groups",), s, False)
    return g, dwgrp


def _local_step(x, target, weights_of, prm, on_grads=None):
    s, d = x.shape
    cw = prm["ssm_b_glu"].shape[1]
    sp = {k: prm[k].reshape(N_LAYERS, 1, -1) for k in ("norm1", "norm2", "b_gate", "ssm_b_glu", "conv_ln_g", "conv_ln_b",
                                                        "pool_scale", "conv_b_dw")}
    sp["pool_w_group"] = prm["pool_w_group"]
    saved = []
    xin = x
    for l in range(N_LAYERS):
        fw = weights_of(l, "in", (xin,))
        sd = _ssm_prepare(l, prm)
        z, h = _in_proj(l, xin, sp["norm1"], fw["w_in"])
        hre, him, y = _ssm_fwd(l, z, sd["bblk_re"], sd["bblk_im"], sd["cblk_re"], sd["cblk_im"], sd["pw"], sd["dskip"])
        p = _pool_fwd(l, z, cw)
        fw.update(weights_of(l, "mixer", (y, p)))
        wdw = fw["conv_w_dw"]
        hc = _conv_fwd(l, z, wdw, sp["conv_b_dw"][l])
        x1 = _merge_fwd(l, xin, y, hc, p, z, fw, sp)
        fw.update(weights_of(l, "ffn", (x1,)))
        x2 = _ffn_fwd(l, x1, sp["norm2"], fw["ffn_w_gate"], fw["ffn_w_up"], fw["ffn_w_down"])
        saved.append(dict(x=xin, z=z, h=h, hre=hre, him=him, y=y, hc=hc, p=p, x1=x1, sd=sd, wdw=wdw, fw=fw))
        xin = x2
    dx, loss, dfinal = _loss_head(xin, target, prm["final_norm"].reshape(1, d))
    big = [None] * N_LAYERS
    small = [None] * N_LAYERS
    norm2_rows = sp["norm2"]
    started = (lambda l, group, grads: on_grads(l, group, grads)) if on_grads is not None else (lambda *a: 0.0)
    for l in reversed(range(N_LAYERS)):
        sv = saved[l]
        sd, fw = sv["sd"], sv["fw"]
        fb = _ffn_bwd(l, sv["x1"], dx, norm2_rows, fw["ffn_w_gate"], fw["ffn_w_up"], fw["ffn_w_down"])
        big[l] = _ffn_weight_grads(l, fb, dx, s)
        spl = dict(sp, ssm_b_glu=sp["ssm_b_glu"] + started(l, "ffn", big[l]))
        mb = _merge_bwd(l, fb["dx1"], sv["y"], sv["hc"], sv["p"], sv["z"], fw, spl)
        mixer, dwgrp = _mixer_weight_grads(l, sv, mb, fb["dx1"], s)
        big[l].update(mixer)
        wdw = sv["wdw"] + started(l, "mixer", mixer)
        du_c = _pool_bwd(l, mb["dp"])
        dv1, dv2, dwdw, dbdw = _conv_bwd(l, mb["dhc"], sv["z"], wdw)
        sr = _ssm_bwd(l, mb["dy"], sv["z"], sv["hre"], sv["him"], sd["bblk_re"], sd["bblk_im"], sd["cblk_re"],
                      sd["cblk_im"], sd["pw"], sd["dskip"])
        dx, dz, dnorm1 = _in_proj_bwd(l, fb["dx1"], sv["x"], sp["norm1"], fw["w_in"], sr["du"], dv1, dv2, du_c, mb["dzg"])
        w_in_grad = {"w_in": _in_weight_grad(l, sv["h"], dz)}
        big[l].update(w_in_grad)
        sg = _ssm_param_grads(l, sd, sr, prm)
        sg.update(norm1=dnorm1.reshape(d), b_gate=mb["db_gate"].reshape(3 * d), ssm_b_glu=mb["db_glu"].reshape(cw),
                  conv_b_dw=dbdw.reshape(cw), conv_ln_g=mb["dln_g"].reshape(cw), conv_ln_b=mb["dln_b"].reshape(cw),
                  pool_w_group=dwgrp, pool_scale=mb["dscale"].reshape(cw), norm2=fb["dnorm2"].reshape(d),
                  conv_w_dw=dwdw)
        small[l] = sg
        if l == N_LAYERS - 1:
            sg = dict(sg, final_norm=dfinal.reshape(d))
        norm2_rows = sp["norm2"] + (started(l, "in", w_in_grad) + started(l, "small", sg))
    return loss[0, 0], dx, big, small, dfinal.reshape(d)


def _place():
    return lax.axis_index("x"), lax.axis_index("y"), lax.axis_index("c")


def _other_chips(x, y):
    return [(1 - x, y), (x, 1 - y), (1 - x, 1 - y)]


def _remote(src, dst, send_sem, recv_sem, device):
    return pltpu.make_async_remote_copy(src_ref=src, dst_ref=dst, send_sem=send_sem, recv_sem=recv_sem,
                                        device_id=device, device_id_type=MESH)


def _hbm(v):
    return pltpu.with_memory_space_constraint(v, pltpu.HBM)


def _cast_into(name, w, place, dtype, after=()):
    nl, k, n = w.shape
    tr = _row_tile(k, n)
    nt = k // tr

    def body(place_ref, w_ref, *rest):
        o0_ref, o1_ref = rest[len(after):]

        @pl.when(pl.program_id(0) == 0)
        def _():
            o0_ref[...] = w_ref[...].astype(dtype)

        @pl.when(pl.program_id(0) == 1)
        def _():
            o1_ref[...] = w_ref[...].astype(dtype)

    return pl.pallas_call(
        body, name=f"cast_{name}",
        grid_spec=pltpu.PrefetchScalarGridSpec(
            num_scalar_prefetch=1, grid=(nl, nt),
            in_specs=[BS((None, tr, n), lambda l, t, pr: (l, t, 0))] + [ANY] * len(after),
            out_specs=[BS((None, tr, n), lambda l, t, pr: (pr[0], t * (1 - l) + (nt - 1) * l, 0)),
                       BS((None, tr, n), lambda l, t, pr: (pr[0], t * l, 0))]),
        out_shape=[SDS((N_CHIPS, k, n), dtype)] * 2)(place, w, *after)


def _gather_rows(buf, c):
    k = buf.shape[1]
    if k % 2:
        return pl.ds(0, k)
    return pl.ds(pl.multiple_of(c * (k // 2), 8), k // 2)


def _allgather_start(tag, groups):
    ng = len(groups)
    sizes = [len(g) for g in groups]
    first = [sum(sizes[:g]) for g in range(ng)]
    flat = [b for g in groups for b in g]
    nb = len(flat)

    def body(*refs):
        ins = refs[:nb]
        sems = refs[nb:nb + 2 * ng]
        token = refs[-1]
        x, y, c = _place()
        jme = 2 * x + y
        for g in range(ng):
            for a in range(sizes[g]):
                buf = ins[first[g] + a]
                blk = buf.at[jme, _gather_rows(buf, c)]
                for k, (cx, cy) in enumerate(_other_chips(x, y)):
                    _remote(blk, blk, sems[2 * g].at[3 * a + k], sems[2 * g + 1].at[3 * a + k], (cx, cy, c)).start()
        token[...] = jnp.zeros(token.shape, F32)

    sem_shapes = [pltpu.SemaphoreType.DMA((3 * sizes[g // 2],)) for g in range(2 * ng)]
    outs = pl.pallas_call(
        body, name=f"allgather_start_{tag}", in_specs=[HBM] * nb,
        out_specs=[SEM] * (2 * ng) + [HBM] * nb + [pl.BlockSpec(memory_space=pltpu.VMEM)],
        out_shape=sem_shapes + [pltpu.HBM(b.shape, b.dtype) for b in flat] + [SDS((8, 128), F32)],
        input_output_aliases={i: 2 * ng + i for i in range(nb)},
        compiler_params=pltpu.CompilerParams(has_side_effects=SIDE_EFFECT))(*[_hbm(b) for b in flat])
    per_group = [(outs[2 * g], outs[2 * g + 1], outs[2 * ng + first[g]:2 * ng + first[g] + sizes[g]])
                 for g in range(ng)]
    return per_group, outs[-1]


def _allgather_wait(l, send_sems, recv_sems, bufs, after):
    n = len(bufs)

    def body(*refs):
        ins = refs[:n]
        ssem, rsem = refs[n], refs[n + 1]
        x, y, c = _place()
        jme = 2 * x + y
        for a in range(n):
            rows = _gather_rows(ins[a], c)
            for k, (cx, cy) in enumerate(_other_chips(x, y)):
                cp = _remote(ins[a].at[jme, rows], ins[a].at[2 * cx + cy, rows], ssem.at[3 * a + k],
                             rsem.at[3 * a + k], (cx, cy, c))
                cp.wait_send()
                cp.wait_recv()

    return pl.pallas_call(
        body, name=f"allgather_wait_{l}", in_specs=[HBM] * n + [SEM, SEM] + [ANY] * len(after), out_specs=[HBM] * n,
        out_shape=[pltpu.HBM(b.shape, b.dtype) for b in bufs], input_output_aliases={i: i for i in range(n)},
        compiler_params=pltpu.CompilerParams(has_side_effects=SIDE_EFFECT))(*bufs, send_sems, recv_sems, *after)


def _allgather_forward(l, bufs):
    n = len(bufs)
    split = [a for a in range(n) if bufs[a].shape[1] % 2 == 0]

    def body(*refs):
        ins = refs[:n]
        send_sems, recv_sems = refs[2 * n:]
        x, y, c = _place()
        sibling = (x, y, 1 - c)
        copies = []
        for a in split:
            for k, (cx, cy) in enumerate(_other_chips(x, y)):
                blk = ins[a].at[2 * cx + cy, _gather_rows(ins[a], c)]
                cp = _remote(blk, blk, send_sems.at[a, k], recv_sems.at[a, k], sibling)
                cp.start()
                copies.append(cp)
        for a in split:
            for k, (cx, cy) in enumerate(_other_chips(x, y)):
                blk = ins[a].at[2 * cx + cy, _gather_rows(ins[a], 1 - c)]
                _remote(blk, blk, send_sems.at[a, k], recv_sems.at[a, k], sibling).wait_recv()
        for cp in copies:
            cp.wait_send()

    sem = pltpu.SemaphoreType.DMA((n, 3))
    return pl.pallas_call(
        body, name=f"allgather_forward_{l}", in_specs=[ANY] * n, out_specs=[ANY] * n,
        out_shape=[SDS(b.shape, b.dtype) for b in bufs], input_output_aliases={i: i for i in range(n)},
        scratch_shapes=[sem, sem])(*bufs)


def _rs_to_owner(l, parts):
    n = len(parts)
    lands = [lax.empty((3,) + p.shape[1:], p.dtype) for p in parts]

    def body(*refs):
        ins, zones = refs[:n], refs[n:2 * n]
        send_sems, recv_sems = refs[2 * n], refs[2 * n + 1]
        token = refs[-1]
        x, y, c = _place()
        for a in range(n):
            for k, (cx, cy) in enumerate(_other_chips(x, y)):
                _remote(ins[a].at[2 * cx + cy], zones[a].at[k], send_sems.at[3 * a + k], recv_sems.at[3 * a + k],
                        (cx, cy, c)).start()
        token[...] = jnp.zeros(token.shape, F32)

    sem = pltpu.SemaphoreType.DMA((3 * n,))
    outs = pl.pallas_call(
        body, name=f"rs_to_owner_start_{l}", in_specs=[HBM] * (2 * n),
        out_specs=[SEM, SEM] + [HBM] * (2 * n) + [pl.BlockSpec(memory_space=pltpu.VMEM)],
        out_shape=[sem, sem] + [pltpu.HBM(p.shape, p.dtype) for p in parts]
        + [pltpu.HBM(z.shape, z.dtype) for z in lands] + [SDS((8, 128), F32)],
        input_output_aliases={i: 2 + i for i in range(2 * n)},
        compiler_params=pltpu.CompilerParams(has_side_effects=SIDE_EFFECT),
    )(*[_hbm(p) for p in parts], *[_hbm(z) for z in lands])
    return outs[0], outs[1], outs[2:2 + n], outs[2 + n:2 + 2 * n], outs[-1]


def _rs_to_owner_wait(l, send_sems, recv_sems, parts, lands, after):
    n = len(parts)

    def body(*refs):
        ins, zones = refs[:n], refs[n:2 * n]
        ssem, rsem = refs[2 * n], refs[2 * n + 1]
        x, y, c = _place()
        for a in range(n):
            for k, (cx, cy) in enumerate(_other_chips(x, y)):
                cp = _remote(ins[a].at[2 * cx + cy], zones[a].at[k], ssem.at[3 * a + k], rsem.at[3 * a + k],
                             (cx, cy, c))
                cp.wait_send()
                cp.wait_recv()

    outs = pl.pallas_call(
        body, name=f"rs_to_owner_wait_{l}", in_specs=[HBM] * (2 * n) + [SEM, SEM] + [ANY] * len(after),
        out_specs=[HBM] * (2 * n),
        out_shape=[pltpu.HBM(p.shape, p.dtype) for p in parts] + [pltpu.HBM(z.shape, z.dtype) for z in lands],
        input_output_aliases={i: i for i in range(2 * n)},
        compiler_params=pltpu.CompilerParams(has_side_effects=SIDE_EFFECT),
    )(*parts, *lands, send_sems, recv_sems, *after)
    return outs[:n], outs[n:]


def _rs_sibling_exchange(l, both):
    n = len(both)

    def body(*refs):
        ins = refs[:n]
        send_sems, recv_sems = refs[2 * n:]
        x, y, c = _place()
        copies = []
        for a in range(n):
            cp = _remote(ins[a].at[c], ins[a].at[c], send_sems.at[a], recv_sems.at[a], (x, y, 1 - c))
            cp.start()
            copies.append(cp)
        for a, cp in enumerate(copies):
            cp.wait_send()
            _remote(ins[a].at[1 - c], ins[a].at[1 - c], send_sems.at[a], recv_sems.at[a], (x, y, 1 - c)).wait_recv()

    sem = pltpu.SemaphoreType.DMA((n,))
    return pl.pallas_call(
        body, name=f"rs_sibling_exchange_{l}", in_specs=[ANY] * n, out_specs=[ANY] * n,
        out_shape=[SDS(b.shape, b.dtype) for b in both], input_output_aliases={i: i for i in range(n)},
        scratch_shapes=[sem, sem])(*both)


def _add_owner(name, grad, recv, place):
    _, r, cols = grad.shape
    tr = _row_tile(r, cols, budget=1024 * 1024)
    nt = r // tr

    def body(place_ref, g_ref, r_ref, o_ref):
        acc = ((g_ref[...] + r_ref[0].astype(F32)) + r_ref[1].astype(F32)) + r_ref[2].astype(F32)
        o_ref[...] = acc.astype(o_ref.dtype)

    return pl.pallas_call(
        body, name=name,
        grid_spec=pltpu.PrefetchScalarGridSpec(
            num_scalar_prefetch=1, grid=(nt,),
            in_specs=[BS((None, tr, cols), lambda t, pr: (pr[0], t, 0)), BS((3, tr, cols), lambda t, pr: (0, t, 0))],
            out_specs=BS((None, tr, cols), lambda t, pr: (pr[1], t, 0))),
        out_shape=SDS((2, r, cols), WIRE_DTYPE))(place, grad, recv)


def _reduce_start(tag, grads):
    names = list(grads)
    send_sems, recv_sems, wires, lands, token = _rs_to_owner(tag, [grads[n][1] for n in names])
    return dict(tag=tag, names=names, send_sems=send_sems, recv_sems=recv_sems, wires=wires, lands=lands,
                grads=[grads[n][0] for n in names]), token


def _reduce_finish(pending, place, after):
    tag, names = pending["tag"], pending["names"]
    _, lands = _rs_to_owner_wait(tag, pending["send_sems"], pending["recv_sems"], pending["wires"],
                                 pending["lands"], after)
    mine = [_add_owner(f"rs_add_owner_{n}_{tag}", g, r, place) for n, g, r in zip(names, pending["grads"], lands)]
    return dict(zip(names, _rs_sibling_exchange(tag, mine)))


def _small_peers(x, y, c):
    return [(x, y, 1 - c)] + [(cx, cy, c) for cx, cy in _other_chips(x, y)]


def _allgather_rows_start(tag, bufs):
    n = len(bufs)
    lands = [lax.empty((8,) + b.shape, b.dtype) for b in bufs]

    def body(*refs):
        ins, zones = refs[:n], refs[n:2 * n]
        send_sems, recv_sems = refs[2 * n], refs[2 * n + 1]
        token = refs[-1]
        x, y, c = _place()
        for a in range(n):
            for i, peer in enumerate(_small_peers(x, y, c)):
                _remote(ins[a], zones[a].at[4 * x + 2 * y + c], send_sems.at[4 * a + i], recv_sems.at[4 * a + i],
                        peer).start()
        token[...] = jnp.zeros(token.shape, F32)

    sem = pltpu.SemaphoreType.DMA((4 * n,))
    outs = pl.pallas_call(
        body, name=f"allgather_small_start_{tag}", in_specs=[HBM] * (2 * n),
        out_specs=[SEM, SEM] + [HBM] * (2 * n) + [pl.BlockSpec(memory_space=pltpu.VMEM)],
        out_shape=[sem, sem] + [pltpu.HBM(b.shape, b.dtype) for b in bufs]
        + [pltpu.HBM(z.shape, z.dtype) for z in lands] + [SDS((8, 128), F32)],
        input_output_aliases={i: 2 + i for i in range(2 * n)},
        compiler_params=pltpu.CompilerParams(has_side_effects=SIDE_EFFECT),
    )(*[_hbm(b) for b in bufs], *[_hbm(z) for z in lands])
    return outs[0], outs[1], outs[2:2 + n], outs[2 + n:2 + 2 * n], outs[-1]


def _allgather_rows_wait(tag, send_sems, recv_sems, bufs, lands, after):
    n = len(bufs)

    def body(*refs):
        ins, zones = refs[:n], refs[n:2 * n]
        ssem, rsem = refs[2 * n], refs[2 * n + 1]
        x, y, c = _place()
        for a in range(n):
            for i, (px, py, pc) in enumerate(_small_peers(x, y, c)):
                cp = _remote(ins[a], zones[a].at[4 * px + 2 * py + pc], ssem.at[4 * a + i], rsem.at[4 * a + i],
                             (px, py, pc))
                cp.wait_send()
                cp.wait_recv()

    outs = pl.pallas_call(
        body, name=f"allgather_small_wait_{tag}", in_specs=[HBM] * (2 * n) + [SEM, SEM, ANY],
        out_specs=[HBM] * (2 * n),
        out_shape=[pltpu.HBM(b.shape, b.dtype) for b in bufs] + [pltpu.HBM(z.shape, z.dtype) for z in lands],
        input_output_aliases={i: i for i in range(2 * n)},
        compiler_params=pltpu.CompilerParams(has_side_effects=SIDE_EFFECT),
    )(*bufs, *lands, send_sems, recv_sems, after)
    return outs[:n], outs[n:]


def _allgather_rows_forward(tag, lands):
    n = len(lands)

    def body(*refs):
        ins = refs[:n]
        send_sems, recv_sems = refs[2 * n:]
        x, y, c = _place()
        sibling = (x, y, 1 - c)
        copies = []
        for a in range(n):
            for k, (cx, cy) in enumerate(_other_chips(x, y)):
                blk = ins[a].at[4 * cx + 2 * cy + c]
                cp = _remote(blk, blk, send_sems.at[a, k], recv_sems.at[a, k], sibling)
                cp.start()
                copies.append(cp)
        for a in range(n):
            for k, (cx, cy) in enumerate(_other_chips(x, y)):
                blk = ins[a].at[4 * cx + 2 * cy + 1 - c]
                _remote(blk, blk, send_sems.at[a, k], recv_sems.at[a, k], sibling).wait_recv()
        for cp in copies:
            cp.wait_send()

    sem = pltpu.SemaphoreType.DMA((n, 3))
    return pl.pallas_call(body, name=f"allgather_small_forward_{tag}", in_specs=[ANY] * n, out_specs=[ANY] * n,
                          out_shape=[SDS(z.shape, z.dtype) for z in lands],
                          input_output_aliases={i: i for i in range(n)}, scratch_shapes=[sem, sem])(*lands)


def _sum_devices(tag, gathered, mine, place):
    _, r, cols = gathered.shape
    tr = _row_tile(r, cols, budget=256 * 1024)

    def body(place_ref, g_ref, x_ref, o_ref):
        me = 2 * place_ref[0] + place_ref[1]
        acc = jnp.where(me == 0, x_ref[...], g_ref[0])
        for k in range(1, 8):
            acc = acc + jnp.where(me == k, x_ref[...], g_ref[k])
        o_ref[...] = acc

    return pl.pallas_call(
        body, name=f"sum_small_grads_{tag}",
        grid_spec=pltpu.PrefetchScalarGridSpec(
            num_scalar_prefetch=1, grid=(r // tr,),
            in_specs=[BS((8, tr, cols), lambda t, pr: (0, t, 0)), BS((tr, cols), lambda t, pr: (t, 0))],
            out_specs=BS((tr, cols), lambda t, pr: (t, 0))),
        out_shape=SDS((r, cols), F32))(place, gathered, mine)


def _adamw_values(w, g, m, v):
    m = ADAM_B1 * m + (1.0 - ADAM_B1) * g
    v = ADAM_B2 * v + (1.0 - ADAM_B2) * (g * g)
    m_hat = m / (1.0 - ADAM_B1 ** ADAM_STEP)
    v_hat = v / (1.0 - ADAM_B2 ** ADAM_STEP)
    delta = -ADAM_LR * (m_hat / (jnp.sqrt(v_hat) + ADAM_EPS) + ADAM_WD * w)
    return delta, m, v


def _adamw_big(name, l, w, m, v, g, earlier=None, after=()):
    nl, r, cols = w.shape
    tr = _row_tile(r, cols, budget=1024 * 1024)
    nt = r // tr
    n_prev = 0 if earlier is None else 4

    def body(*refs):
        w_ref, m_ref, v_ref, g_ref = refs[:4]
        go_ref, d_ref, mo_ref, vo_ref = refs[4 + n_prev + len(after):]
        gv = g_ref[0].astype(F32) + g_ref[1].astype(F32)
        delta, m_new, v_new = _adamw_values(w_ref[...], gv, m_ref[...], v_ref[...])
        go_ref[...] = gv
        d_ref[...] = delta
        mo_ref[...] = m_new
        vo_ref[...] = v_new

    layer = BS((None, tr, cols), lambda t: (l, t, 0))
    return pl.pallas_call(
        body, name=f"adamw_{name}_l{l}", grid=(nt,),
        in_specs=[layer, layer, layer, BS((2, tr, cols), lambda t: (0, t, 0))] + [ANY] * (n_prev + len(after)),
        out_specs=[layer] * 4, out_shape=[SDS(w.shape, F32)] * 4,
        input_output_aliases={4 + i: i for i in range(n_prev)}, compiler_params=_params(),
    )(w, m, v, g, *(earlier or ()), *after)


def _adamw_mid(name, w, m, v, gathered, mine, place):
    shape = w.shape[1:]
    zeros = (0,) * len(shape)

    def body(place_ref, w_ref, m_ref, v_ref, *refs):
        gath, own = refs[:N_LAYERS], refs[N_LAYERS:2 * N_LAYERS]
        go_ref, d_ref, mo_ref, vo_ref = refs[2 * N_LAYERS:]
        me = 2 * place_ref[0] + place_ref[1]
        sums = []
        for l in range(N_LAYERS):
            acc = jnp.where(me == 0, own[l][...], gath[l][0])
            for k in range(1, 8):
                acc = acc + jnp.where(me == k, own[l][...], gath[l][k])
            sums.append(acc)
        gv = sums[0]
        for l in range(1, N_LAYERS):
            gv = jnp.where(pl.program_id(0) == l, sums[l], gv)
        delta, m_new, v_new = _adamw_values(w_ref[...], gv, m_ref[...], v_ref[...])
        go_ref[...] = gv
        d_ref[...] = delta
        mo_ref[...] = m_new
        vo_ref[...] = v_new

    layer = BS((None,) + shape, lambda l, pr: (l,) + zeros)
    return pl.pallas_call(
        body, name=f"adamw_{name}",
        grid_spec=pltpu.PrefetchScalarGridSpec(
            num_scalar_prefetch=1, grid=(N_LAYERS,),
            in_specs=[layer] * 3 + [BS((8,) + shape, lambda l, pr: (0,) + zeros)] * N_LAYERS
            + [BS(shape, lambda l, pr: zeros)] * N_LAYERS,
            out_specs=[layer] * 4),
        out_shape=[SDS(w.shape, F32)] * 4, compiler_params=_params())(place, w, m, v, *gathered, *mine)


def _adamw_rows(w, m, v, g):
    r, cols = w.shape
    tr = _row_tile(r, cols, budget=512 * 1024)

    def body(w_ref, m_ref, v_ref, g_ref, d_ref, mo_ref, vo_ref):
        delta, m_new, v_new = _adamw_values(w_ref[...], g_ref[...], m_ref[...], v_ref[...])
        d_ref[...] = delta
        mo_ref[...] = m_new
        vo_ref[...] = v_new

    spec = BS((tr, cols), lambda t: (t, 0))
    return pl.pallas_call(body, name="adamw_small", grid=(r // tr,), in_specs=[spec] * 4, out_specs=[spec] * 3,
                          out_shape=[SDS(w.shape, F32)] * 3)(w, m, v, g)


PACK_ALIGN = 8 * 128


def _pack_rows(arrays):
    parts = []
    for a in arrays:
        flat = a.reshape(-1)
        pad = (-flat.shape[0]) % PACK_ALIGN
        if pad:
            flat = jnp.pad(flat, (0, pad))
        parts.append(flat.reshape(-1, 128))
    return jnp.concatenate(parts, axis=0)


def _unpack_rows(buf, shapes):
    out, row = [], 0
    for shape in shapes:
        size = math.prod(shape)
        rows = -(-size // PACK_ALIGN) * (PACK_ALIGN // 128)
        out.append(buf[row:row + rows].reshape(-1)[:size].reshape(shape))
        row += rows
    return out


def kernel(x, norm1, w_in, b_gate, ssm_a_re, ssm_a_im, ssm_log_dt, ssm_b_re, ssm_b_im, ssm_c_re, ssm_c_im, ssm_d, ssm_w_glu, ssm_b_glu, ssm_w_proj, conv_w_dw, conv_b_dw, conv_ln_g, conv_ln_b, conv_w_proj, pool_w_group, pool_scale, pool_w_proj, w_out, norm2, ffn_w_gate, ffn_w_up, ffn_w_down, final_norm, loss_target, m_norm1, m_w_in, m_b_gate, m_ssm_a_re, m_ssm_a_im, m_ssm_log_dt, m_ssm_b_re, m_ssm_b_im, m_ssm_c_re, m_ssm_c_im, m_ssm_d, m_ssm_w_glu, m_ssm_b_glu, m_ssm_w_proj, m_conv_w_dw, m_conv_b_dw, m_conv_ln_g, m_conv_ln_b, m_conv_w_proj, m_pool_w_group, m_pool_scale, m_pool_w_proj, m_w_out, m_norm2, m_ffn_w_gate, m_ffn_w_up, m_ffn_w_down, m_final_norm, v_norm1, v_w_in, v_b_gate, v_ssm_a_re, v_ssm_a_im, v_ssm_log_dt, v_ssm_b_re, v_ssm_b_im, v_ssm_c_re, v_ssm_c_im, v_ssm_d, v_ssm_w_glu, v_ssm_b_glu, v_ssm_w_proj, v_conv_w_dw, v_conv_b_dw, v_conv_ln_g, v_conv_ln_b, v_conv_w_proj, v_pool_w_group, v_pool_scale, v_pool_w_proj, v_w_out, v_norm2, v_ffn_w_gate, v_ffn_w_up, v_ffn_w_down, v_final_norm):
    given = dict(locals())
    cx, cy, cc = _place()
    place = jnp.stack([2 * cx + cy, cc]).astype(jnp.int32)

    def kernel_view(n, a):
        if n in TRANSPOSED:
            return a.transpose(0, 2, 1)
        return a.transpose(0, 1, 3, 2) if n in ("ssm_b_re", "ssm_b_im") else a

    prm = {n: given[n] for n in WEIGHTS}
    mom = {n: given["m_" + n] for n in WEIGHTS}
    var = {n: given["v_" + n] for n in WEIGHTS}
    for n in MID:
        prm[n], mom[n], var[n] = kernel_view(n, prm[n]), kernel_view(n, mom[n]), kernel_view(n, var[n])

    dw_shard = prm["conv_w_dw"].reshape(N_LAYERS, CONV_KERNEL, -1)
    casts = {"w_in": _cast_into("w_in", prm["w_in"], place, MXU_DTYPE)}
    first, first_started = _allgather_start("first", [[casts["w_in"][0]]])
    in_flight = {(0, "in"): first[0]}
    casts.update({n: _cast_into(n, kernel_view(n, prm[n]), place, MXU_DTYPE, after=(first_started,))
                  for n in BIG if n != "w_in"})
    casts["conv_w_dw"] = _cast_into("conv_w_dw", dw_shard, place, F32, after=(first_started,))
    order = [(l, g) for l in range(N_LAYERS) for g in GATHER_GROUPS if (l, g) != (0, "in")]
    rest, rest_started = _allgather_start("rest", [[casts[n][l] for n in GATHER_GROUPS[g]] for l, g in order])
    in_flight.update(zip(order, rest))

    def weights_of(l, group, after):
        send_sems, recv_sems, bufs = in_flight[l, group]
        tag = f"l{l}_{group}"
        if (l, group) == (0, "in"):
            after = after + (rest_started,)
        bufs = _allgather_forward(tag, _allgather_wait(tag, send_sems, recv_sems, bufs, after))
        fw = dict(zip(GATHER_GROUPS[group], bufs))
        if "conv_w_dw" in fw:
            fw["conv_w_dw"] = fw["conv_w_dw"].transpose(1, 0, 2).reshape(CONV_KERNEL, -1)
        return fw

    pending, small_pending, small_shapes = {}, {}, {}
    tokens = {}

    def on_grads(l, group, grads):
        if group == "small":
            packed = {n: g for n, g in grads.items() if n not in MID}
            small_shapes[l] = {n: g.shape for n, g in packed.items()}
            begun = _allgather_rows_start(f"l{l}", [_pack_rows(list(packed.values()))] + [grads[n] for n in MID])
            small_pending[l], token = begun[:4], begun[4]
        else:
            pending[l, group], token = _reduce_start(f"{l}_{group}", grads)
        tokens[l, group] = token
        return token[0, 0]

    loss, dx, _, _, _ = _local_step(x[0], loss_target[0], weights_of, prm, on_grads)
    loss = lax.psum(loss, ("x", "y", "c"))

    reduced = [{} for _ in range(N_LAYERS)]
    out = {}

    def finish(l, group, after):
        reduced[l].update(_reduce_finish(pending[l, group], place, after))

    def adamw(l, names, done):
        for n in names:
            out[n] = _adamw_big(n, l, kernel_view(n, prm[n]), kernel_view(n, mom[n]), kernel_view(n, var[n]),
                                reduced[l][n], out.get(n), after=done)
            done = (out[n][0],)
        return done

    top = N_LAYERS - 1
    done = (tokens[0, "in"], tokens[0, "small"])
    for group in ("ffn", "mixer", "in"):
        finish(top, group, done)
    done = adamw(top, BIG, done)
    for group in ("ffn", "mixer", "in"):
        finish(0, group, done)
        done = adamw(0, [n for n in GATHER_GROUPS[group] if n in BIG], done)
    for n in BIG:
        out[n] = tuple(kernel_view(n, a) for a in out[n])

    gsmall = {}
    mid_mine, mid_gathered = [], []
    for l in range(N_LAYERS):
        mine, lands = _allgather_rows_wait(f"l{l}", *small_pending[l], done[0])
        lands = _allgather_rows_forward(f"l{l}", lands)
        mid_mine.append(mine[1:])
        mid_gathered.append(lands[1:])
        gsum = _sum_devices(f"l{l}", lands[0], mine[0], place)
        for n, g in zip(small_shapes[l], _unpack_rows(gsum, list(small_shapes[l].values()))):
            gsmall.setdefault(n, [None] * N_LAYERS)[l] = g
    for i, n in enumerate(MID):
        out[n] = tuple(kernel_view(n, a) for a in _adamw_mid(
            n, prm[n], mom[n], var[n], [mid_gathered[l][i] for l in range(N_LAYERS)],
            [mid_mine[l][i] for l in range(N_LAYERS)], place))
    gsmall = {n: (g[top] if n == "final_norm" else jnp.stack(g)) for n, g in gsmall.items()}
    lanes = dw_shard.shape[-1]
    gsmall["conv_w_dw"] = lax.dynamic_slice_in_dim(gsmall["conv_w_dw"], (2 * cx + cy) * lanes, lanes, axis=2)
    small_names = [n for n in SMALL if n not in MID] + ["conv_w_dw"]
    w_rows = _pack_rows([prm[n] for n in small_names])
    m_rows = _pack_rows([mom[n] for n in small_names])
    v_rows = _pack_rows([var[n] for n in small_names])
    g_rows = _pack_rows([gsmall[n] for n in small_names])
    shapes = [prm[n].shape for n in small_names]
    d_s, m_s, v_s = (_unpack_rows(r, shapes) for r in _adamw_rows(w_rows, m_rows, v_rows, g_rows))
    for i, n in enumerate(small_names):
        out[n] = (gsmall[n].reshape(prm[n].shape), d_s[i], m_s[i], v_s[i])
    grads = [out[n][0] for n in WEIGHTS]
    deltas = [out[n][1] for n in WEIGHTS]
    new_m = [out[n][2] for n in WEIGHTS]
    new_v = [out[n][3] for n in WEIGHTS]
    return (loss, dx[None], *grads, *deltas, *new_m, *new_v)
```

```python
import functools
import math

import jax
import jax.numpy as jnp
from jax import lax
from jax.experimental import pallas as pl
from jax.experimental.pallas import tpu as pltpu

F32 = jnp.float32
MXU_DTYPE = jnp.bfloat16
WIRE_DTYPE = jnp.bfloat16
SDS = jax.ShapeDtypeStruct
BS = pl.BlockSpec
ANY = pl.BlockSpec(memory_space=pl.ANY)
HBM = pl.BlockSpec(memory_space=pltpu.HBM)
SEM = pl.BlockSpec(memory_space=pltpu.SEMAPHORE)
SIDE_EFFECT = pltpu.SideEffectType.DATAFLOW_SIDE_EFFECTING
MESH = pl.DeviceIdType.MESH

EPS = 1e-6
N_CHIPS = 4
N_LAYERS = 2
SSM_GROUPS, SSM_STATE, SSM_GROUP = 32, 64, 16
CONV_KERNEL = 31
CONV_PAD = 32
POOL_WINDOWS = (2, 4, 8, 16)
GELU_C = math.sqrt(2.0 / math.pi)
ADAM_LR, ADAM_B1, ADAM_B2, ADAM_EPS, ADAM_WD, ADAM_STEP = 0.001, 0.9, 0.999, 1e-08, 0.01, 10
VMEM_LIMIT = 56 * 1024 * 1024

BIG = ("w_in", "ssm_w_glu", "ssm_w_proj", "conv_w_proj", "pool_w_proj", "w_out", "ffn_w_gate", "ffn_w_up", "ffn_w_down")
TRANSPOSED = ("ffn_w_gate", "ffn_w_up")
MID = ("ssm_b_re", "ssm_b_im", "ssm_c_re", "ssm_c_im")
GATHER_GROUPS = {
    "in": ("w_in",),
    "mixer": ("ssm_w_glu", "ssm_w_proj", "conv_w_proj", "pool_w_proj", "w_out", "conv_w_dw"),
    "ffn": ("ffn_w_gate", "ffn_w_up", "ffn_w_down"),
}
SMALL = ("norm1", "b_gate", "ssm_a_re", "ssm_a_im", "ssm_log_dt", "ssm_b_re", "ssm_b_im", "ssm_c_re", "ssm_c_im",
         "ssm_d", "ssm_b_glu", "conv_b_dw", "conv_ln_g", "conv_ln_b", "pool_w_group", "pool_scale", "norm2",
         "final_norm")
WEIGHTS = ("norm1", "w_in", "b_gate", "ssm_a_re", "ssm_a_im", "ssm_log_dt", "ssm_b_re", "ssm_b_im", "ssm_c_re",
           "ssm_c_im", "ssm_d", "ssm_w_glu", "ssm_b_glu", "ssm_w_proj", "conv_w_dw", "conv_b_dw", "conv_ln_g",
           "conv_ln_b", "conv_w_proj", "pool_w_group", "pool_scale", "pool_w_proj", "w_out", "norm2", "ffn_w_gate",
           "ffn_w_up", "ffn_w_down", "final_norm")


def _params(vmem=True):
    return pltpu.CompilerParams(vmem_limit_bytes=VMEM_LIMIT) if vmem else None


def _mm(a, b):
    return jnp.dot(a.astype(MXU_DTYPE), b.astype(MXU_DTYPE), preferred_element_type=F32)


def _mm_nt(a, b):
    return lax.dot_general(a.astype(MXU_DTYPE), b.astype(MXU_DTYPE), (((1,), (1,)), ((), ())),
                           preferred_element_type=F32)


def _mm_tn(a, b):
    return lax.dot_general(a.astype(MXU_DTYPE), b.astype(MXU_DTYPE), (((0,), (0,)), ((), ())),
                           preferred_element_type=F32)


def _sigmoid(x):
    return jax.nn.sigmoid(x)


def _gelu(x):
    t = jnp.tanh(GELU_C * (x + 0.044715 * (x * x * x)))
    return x * (0.5 * (1.0 + t)), t


def _gelu_grad(x, t):
    return 0.5 * (1.0 + t) + 0.5 * x * (1.0 - t * t) * (GELU_C * (1.0 + 3.0 * 0.044715 * x * x))


def _colsum(v):
    return jnp.sum(v, axis=0, keepdims=True)


def _row_tile(rows, cols, itemsize=4, budget=1536 * 1024):
    best = None
    for t in range(8, rows + 1, 8):
        if rows % t == 0 and t * cols * itemsize <= budget:
            best = t
    return best if best is not None else rows


def _in_proj(l, x, norm1, w_in):
    s, d = x.shape
    nc = w_in.shape[-1]
    tm = min(512, s)
    nt = s // tm

    def body(x_ref, g_ref, w_ref, z_ref, h_ref, h_all):
        i = pl.program_id(1)
        rows = pl.ds(pl.multiple_of(i * tm, tm), tm)

        @pl.when(pl.program_id(0) == 0)
        def _():
            xv = x_ref[...]
            r = lax.rsqrt(jnp.mean(xv * xv, axis=-1, keepdims=True) + EPS)
            hv = (xv * r * g_ref[...]).astype(h_ref.dtype)
            h_ref[...] = hv
            h_all[rows, :] = hv

        z_ref[...] = _mm(h_all[rows, :], w_ref[...])

    first_pass = lambda j, i: (i * (1 - jnp.minimum(j, 1)) + (nt - 1) * jnp.minimum(j, 1), 0)
    return pl.pallas_call(
        body, name=f"in_proj_l{l}", grid=(N_CHIPS, nt),
        in_specs=[BS((tm, d), first_pass), BS((None, 1, d), lambda j, i: (l, 0, 0)),
                  BS((None, d, nc), lambda j, i: (j, 0, 0))],
        out_specs=[BS((tm, nc), lambda j, i: (i, j)), BS((tm, d), first_pass)],
        out_shape=[SDS((s, N_CHIPS * nc), F32), SDS((s, d), MXU_DTYPE)],
        scratch_shapes=[pltpu.VMEM((s, d), MXU_DTYPE)], compiler_params=_params())(x, norm1, w_in)


def _mm_cols(a, w_ref):
    return jnp.concatenate([_mm(a, w_ref[j]) for j in range(N_CHIPS)], axis=1)


def _mm_nt_cols(dv, w_ref):
    nc = w_ref.shape[-1]
    acc = _mm_nt(dv[:, 0:nc], w_ref[0])
    for j in range(1, N_CHIPS):
        acc = acc + _mm_nt(dv[:, j * nc:(j + 1) * nc], w_ref[j])
    return acc


def _merge_values(y, hc, p, zg, wglu, bglu, wpa, wpb, wpc, lng, lnb, wgrp, scale, bg):
    v = {}
    ge, th = _gelu(y)
    t = _mm(ge, wglu) + bglu
    sg = _sigmoid(t)
    sa = ge * sg
    ya = _mm_cols(sa, wpa)
    mu = jnp.mean(hc, axis=-1, keepdims=True)
    xc = hc - mu
    r = lax.rsqrt(jnp.mean(xc * xc, axis=-1, keepdims=True) + EPS)
    xh = xc * r
    ln = xh * lng + lnb
    sl = _sigmoid(ln)
    ac = ln * sl
    yb = _mm_cols(ac, wpb)
    gw = p.shape[1] // len(POOL_WINDOWS)
    q = jnp.concatenate([_mm(p[:, k * gw:(k + 1) * gw], wgrp[k]) for k in range(len(POOL_WINDOWS))], axis=1)
    pp = q * scale
    yc = _mm_cols(pp, wpc)
    d = ya.shape[1]
    gates = [_sigmoid(zg[k] + bg[:, k * d:(k + 1) * d]) for k in range(3)]
    merged = gates[0] * ya + gates[1] * yb + gates[2] * yc
    v.update(ge=ge, th=th, sg=sg, sa=sa, ya=ya, r=r, xh=xh, ln=ln, sl=sl, ac=ac, yb=yb, q=q, pp=pp, yc=yc,
             gates=gates, merged=merged)
    return v


def _merge_specs(l, tm, d, cw):
    row = lambda n: BS((None, 1, n), lambda i: (l, 0, 0))
    return [
        BS((tm, cw), lambda i: (i, 0)),
        BS((tm, cw), lambda i: (i, 0)),
        BS((tm, cw), lambda i: (i, 0)),
        BS((tm, d), lambda i: (i, 2)), BS((tm, d), lambda i: (i, 3)), BS((tm, d), lambda i: (i, 4)),
        BS((N_CHIPS, cw // N_CHIPS, cw), lambda i: (0, 0, 0)),
        row(cw),
        BS((N_CHIPS, cw, d // N_CHIPS), lambda i: (0, 0, 0)),
        BS((N_CHIPS, cw, d // N_CHIPS), lambda i: (0, 0, 0)),
        BS((N_CHIPS, cw, d // N_CHIPS), lambda i: (0, 0, 0)),
        row(cw), row(cw),
        BS((None, 4, cw // 4, cw // 4), lambda i: (l, 0, 0, 0)),
        row(cw),
        row(3 * d),
        BS((N_CHIPS, d // N_CHIPS, d), lambda i: (0, 0, 0)),
    ]


def _merge_fwd(l, x, y, hc, p, z, fw, sp):
    s, d = x.shape
    cw = y.shape[1]
    tm = min(256, s)

    def body(x_ref, y_ref, hc_ref, p_ref, z0, z1, z2, wglu, bglu, wpa, wpb, wpc, lng, lnb, wgrp, scale, bg, wout,
             x1_ref):
        v = _merge_values(y_ref[...], hc_ref[...], p_ref[...], (z0[...], z1[...], z2[...]),
                          wglu[...].reshape(cw, cw), bglu[...], wpa, wpb, wpc, lng[...], lnb[...], wgrp, scale[...],
                          bg[...])
        x1_ref[...] = x_ref[...] + _mm(v["merged"], wout[...].reshape(d, d))

    return pl.pallas_call(
        body, name=f"merge_fwd_l{l}", grid=(s // tm,),
        in_specs=[BS((tm, d), lambda i: (i, 0))] + _merge_specs(l, tm, d, cw),
        out_specs=BS((tm, d), lambda i: (i, 0)), out_shape=SDS((s, d), F32), compiler_params=_params(),
    )(x, y, hc, p, z, z, z, fw["ssm_w_glu"], sp["ssm_b_glu"], fw["ssm_w_proj"], fw["conv_w_proj"], fw["pool_w_proj"],
      sp["conv_ln_g"], sp["conv_ln_b"], sp["pool_w_group"], sp["pool_scale"], sp["b_gate"], fw["w_out"])


def _merge_bwd(l, dx1, y, hc, p, z, fw, sp):
    s, d = dx1.shape
    cw = y.shape[1]
    tm = min(256, s)
    m = MXU_DTYPE

    def body(dx1_ref, y_ref, hc_ref, p_ref, z0, z1, z2, wglu, bglu, wpa, wpb, wpc, lng, lnb, wgrp, scale, bg, wout,
             dzg_ref, dy_ref, dhc_ref, dp_ref, merged_ref, sa_ref, ac_ref, pp_ref, ge_ref, dt_ref, dya_ref, dyb_ref,
             dyc_ref, dq_ref, dbg_ref, dbglu_ref, dlng_ref, dlnb_ref, dscale_ref):
        yv = y_ref[...]
        wg = wglu[...].reshape(cw, cw)
        v = _merge_values(yv, hc_ref[...], p_ref[...], (z0[...], z1[...], z2[...]), wg, bglu[...], wpa, wpb, wpc,
                          lng[...], lnb[...], wgrp, scale[...], bg[...])
        dm = _mm_nt(dx1_ref[...], wout[...].reshape(d, d))
        ys = (v["ya"], v["yb"], v["yc"])
        dys = []
        for k in range(3):
            gk = v["gates"][k]
            dzg_ref[:, k * d:(k + 1) * d] = dm * ys[k] * (gk * (1.0 - gk))
            dys.append((dm * gk).astype(m))
        dsa = _mm_nt_cols(dys[0], wpa)
        dac = _mm_nt_cols(dys[1], wpb)
        dpp = _mm_nt_cols(dys[2], wpc)
        ge, sg = v["ge"], v["sg"]
        dt = dsa * ge * (sg * (1.0 - sg))
        dge = dsa * sg + _mm_nt(dt, wg)
        dy_ref[...] = dge * _gelu_grad(yv, v["th"])
        ln, sl, xh = v["ln"], v["sl"], v["xh"]
        dln = dac * (sl * (1.0 + ln * (1.0 - sl)))
        dxh = dln * lng[...]
        dhc_ref[...] = v["r"] * (dxh - jnp.mean(dxh, axis=-1, keepdims=True)
                                 - xh * jnp.mean(dxh * xh, axis=-1, keepdims=True))
        dq = dpp * scale[...]
        gw = cw // len(POOL_WINDOWS)
        for k in range(len(POOL_WINDOWS)):
            dp_ref[:, k * gw:(k + 1) * gw] = _mm_nt(dq[:, k * gw:(k + 1) * gw], wgrp[k])
        merged_ref[...] = v["merged"].astype(m)
        sa_ref[...] = v["sa"].astype(m)
        ac_ref[...] = v["ac"].astype(m)
        pp_ref[...] = v["pp"].astype(m)
        ge_ref[...] = ge.astype(m)
        dt_ref[...] = dt.astype(m)
        dya_ref[...] = dys[0]
        dyb_ref[...] = dys[1]
        dyc_ref[...] = dys[2]
        dq_ref[...] = dq.astype(m)

        @pl.when(pl.program_id(0) == 0)
        def _():
            for ref in (dbg_ref, dbglu_ref, dlng_ref, dlnb_ref, dscale_ref):
                ref[...] = jnp.zeros(ref.shape, F32)

        dbg_ref[...] += _colsum(dzg_ref[...])
        dbglu_ref[...] += _colsum(dt)
        dlng_ref[...] += _colsum(dln * xh)
        dlnb_ref[...] += _colsum(dln)
        dscale_ref[...] += _colsum(dpp * v["q"])

    tile = lambda n: BS((tm, n), lambda i: (i, 0))
    acc = lambda n: BS((1, n), lambda i: (0, 0))
    outs = pl.pallas_call(
        body, name=f"merge_bwd_l{l}", grid=(s // tm,),
        in_specs=[tile(d)] + _merge_specs(l, tm, d, cw),
        out_specs=[tile(3 * d), tile(cw), tile(cw), tile(cw), tile(d), tile(cw), tile(cw), tile(cw), tile(cw), tile(cw),
                   tile(d), tile(d), tile(d), tile(cw), acc(3 * d), acc(cw), acc(cw), acc(cw), acc(cw)],
        out_shape=[SDS((s, 3 * d), F32), SDS((s, cw), F32), SDS((s, cw), F32), SDS((s, cw), F32), SDS((s, d), m),
                   SDS((s, cw), m), SDS((s, cw), m), SDS((s, cw), m), SDS((s, cw), m), SDS((s, cw), m), SDS((s, d), m),
                   SDS((s, d), m), SDS((s, d), m), SDS((s, cw), m), SDS((1, 3 * d), F32), SDS((1, cw), F32),
                   SDS((1, cw), F32), SDS((1, cw), F32), SDS((1, cw), F32)],
        compiler_params=_params(),
    )(dx1, y, hc, p, z, z, z, fw["ssm_w_glu"], sp["ssm_b_glu"], fw["ssm_w_proj"], fw["conv_w_proj"], fw["pool_w_proj"],
      sp["conv_ln_g"], sp["conv_ln_b"], sp["pool_w_group"], sp["pool_scale"], sp["b_gate"], fw["w_out"])
    names = ("dzg", "dy", "dhc", "dp", "merged", "sa", "ac", "pp", "ge", "dt", "dya", "dyb", "dyc", "dq", "db_gate",
             "db_glu", "dln_g", "dln_b", "dscale")
    return dict(zip(names, outs))


def _ffn_fwd(l, x1, norm2, wg, wu, wd):
    s, d = x1.shape
    hc = wd.shape[1]
    tm = min(1024, s)

    def body(x_ref, g_ref, wg_ref, wu_ref, wd_ref, o_ref, h_scr):
        @pl.when(pl.program_id(1) == 0)
        def _():
            xv = x_ref[...]
            r = lax.rsqrt(jnp.mean(xv * xv, axis=-1, keepdims=True) + EPS)
            h_scr[...] = (xv * r * g_ref[...]).astype(h_scr.dtype)
            o_ref[...] = xv

        h = h_scr[...]
        gate = _mm_nt(h, wg_ref[...])
        up = _mm_nt(h, wu_ref[...])
        o_ref[...] += _mm(gate * _sigmoid(gate) * up, wd_ref[...])

    return pl.pallas_call(
        body, name=f"ffn_fwd_l{l}", grid=(s // tm, N_CHIPS),
        in_specs=[BS((tm, d), lambda i, j: (i, 0)), BS((None, 1, d), lambda i, j: (l, 0, 0)),
                  BS((None, hc, d), lambda i, j: (j, 0, 0)), BS((None, hc, d), lambda i, j: (j, 0, 0)),
                  BS((None, hc, d), lambda i, j: (j, 0, 0))],
        out_specs=BS((tm, d), lambda i, j: (i, 0)), out_shape=SDS((s, d), F32),
        scratch_shapes=[pltpu.VMEM((tm, d), MXU_DTYPE)], compiler_params=_params())(x1, norm2, wg, wu, wd)


def _ffn_bwd(l, x1, dx2, norm2, wg, wu, wd):
    s, d = x1.shape
    hc = wd.shape[1]
    tm = min(512, s)
    m = MXU_DTYPE
    last = N_CHIPS - 1

    def body(x_ref, dx2_ref, g_ref, wg_ref, wu_ref, wd_ref, dx1_ref, h_ref, act_ref, dgate_ref, dup_ref, dn_ref,
             dh_scr, dxb_scr):
        i, j = pl.program_id(0), pl.program_id(1)

        @pl.when(j == 0)
        def _():
            xv = x_ref[...]
            r = lax.rsqrt(jnp.mean(xv * xv, axis=-1, keepdims=True) + EPS)
            h_ref[...] = (xv * r * g_ref[...]).astype(m)
            dxb_scr[...] = dx2_ref[...].astype(m)
            dh_scr[...] = jnp.zeros(dh_scr.shape, F32)

        @pl.when((i == 0) & (j == 0))
        def _():
            dn_ref[...] = jnp.zeros(dn_ref.shape, F32)

        h = h_ref[...]
        gate = _mm_nt(h, wg_ref[...])
        up = _mm_nt(h, wu_ref[...])
        sg = _sigmoid(gate)
        silu = gate * sg
        act_ref[...] = (silu * up).astype(m)
        dact = _mm_nt(dxb_scr[...], wd_ref[...])
        dup = (dact * silu).astype(m)
        dgate = (dact * up * (sg * (1.0 + gate * (1.0 - sg)))).astype(m)
        dup_ref[...] = dup
        dgate_ref[...] = dgate
        dh_scr[...] += _mm(dgate, wg_ref[...]) + _mm(dup, wu_ref[...])

        @pl.when(j == last)
        def _():
            xv = x_ref[...]
            r = lax.rsqrt(jnp.mean(xv * xv, axis=-1, keepdims=True) + EPS)
            xh = xv * r
            dh = dh_scr[...]
            dn_ref[...] += _colsum(dh * xh)
            dxh = dh * g_ref[...]
            dx1_ref[...] = dx2_ref[...] + r * (dxh - xh * jnp.mean(dxh * xh, axis=-1, keepdims=True))

    chunk = BS((None, tm, hc), lambda i, j: (j, i, 0))
    outs = pl.pallas_call(
        body, name=f"ffn_bwd_l{l}", grid=(s // tm, N_CHIPS),
        in_specs=[BS((tm, d), lambda i, j: (i, 0)), BS((tm, d), lambda i, j: (i, 0)),
                  BS((None, 1, d), lambda i, j: (l, 0, 0)),
                  BS((None, hc, d), lambda i, j: (j, 0, 0)), BS((None, hc, d), lambda i, j: (j, 0, 0)),
                  BS((None, hc, d), lambda i, j: (j, 0, 0))],
        out_specs=[BS((tm, d), lambda i, j: (i, 0)), BS((tm, d), lambda i, j: (i, 0)), chunk, chunk, chunk,
                   BS((1, d), lambda i, j: (0, 0))],
        out_shape=[SDS((s, d), F32), SDS((s, d), m), SDS((N_CHIPS, s, hc), m), SDS((N_CHIPS, s, hc), m),
                   SDS((N_CHIPS, s, hc), m), SDS((1, d), F32)],
        scratch_shapes=[pltpu.VMEM((tm, d), F32), pltpu.VMEM((tm, d), m)], compiler_params=_params(),
    )(x1, dx2, norm2, wg, wu, wd)
    return dict(zip(("dx1", "h2", "act", "dgate", "dup", "dnorm2"), outs))


def _loss_head(x, target, gf):
    s, d = x.shape
    tm = min(512, s)

    def body(x_ref, t_ref, g_ref, dx_ref, loss_ref, dg_ref):
        @pl.when(pl.program_id(0) == 0)
        def _():
            loss_ref[...] = jnp.zeros(loss_ref.shape, F32)
            dg_ref[...] = jnp.zeros(dg_ref.shape, F32)

        xv = x_ref[...]
        r = lax.rsqrt(jnp.mean(xv * xv, axis=-1, keepdims=True) + EPS)
        xh = xv * r
        err = xh * g_ref[...] - t_ref[...]
        loss_ref[...] += 0.5 * jnp.sum(jnp.mean(err * err, axis=-1, keepdims=True), axis=0, keepdims=True)
        dyv = err * (1.0 / d)
        dg_ref[...] += _colsum(dyv * xh)
        dxh = dyv * g_ref[...]
        dx_ref[...] = r * (dxh - xh * jnp.mean(dxh * xh, axis=-1, keepdims=True))

    return pl.pallas_call(
        body, name="loss_head", grid=(s // tm,),
        in_specs=[BS((tm, d), lambda i: (i, 0)), BS((tm, d), lambda i: (i, 0)), BS((1, d), lambda i: (0, 0))],
        out_specs=[BS((tm, d), lambda i: (i, 0)), BS((1, 1), lambda i: (0, 0)), BS((1, d), lambda i: (0, 0))],
        out_shape=[SDS((s, d), F32), SDS((1, 1), F32), SDS((1, d), F32)], compiler_params=_params())(x, target, gf)


def _in_proj_bwd(l, dres, x, norm1, w_in, du_a, dv1, dv2, du_c, dzg):
    s, d = x.shape
    nc = w_in.shape[-1]
    tm = min(256, s)
    m = MXU_DTYPE

    def body(dres_ref, x_ref, g_ref, w_ref, a_ref, b1_ref, b2_ref, c_ref, g3_ref, dx_ref, dz_ref, dn_ref):
        @pl.when(pl.program_id(0) == 0)
        def _():
            dn_ref[...] = jnp.zeros(dn_ref.shape, F32)

        dz = jnp.concatenate([a_ref[...], b1_ref[...], b2_ref[...], c_ref[...], g3_ref[...]], axis=1).astype(m)
        dz_ref[...] = dz
        dh = _mm_nt_cols(dz, w_ref)
        xv = x_ref[...]
        r = lax.rsqrt(jnp.mean(xv * xv, axis=-1, keepdims=True) + EPS)
        xh = xv * r
        dn_ref[...] += _colsum(dh * xh)
        dxh = dh * g_ref[...]
        dx_ref[...] = dres_ref[...] + r * (dxh - xh * jnp.mean(dxh * xh, axis=-1, keepdims=True))

    tile = lambda n: BS((tm, n), lambda i: (i, 0))
    return pl.pallas_call(
        body, name=f"in_proj_bwd_l{l}", grid=(s // tm,),
        in_specs=[tile(d), tile(d), BS((None, 1, d), lambda i: (l, 0, 0)),
                  BS((N_CHIPS, d, nc), lambda i: (0, 0, 0)),
                  tile(du_a.shape[1]), tile(dv1.shape[1]), tile(dv2.shape[1]), tile(du_c.shape[1]), tile(dzg.shape[1])],
        out_specs=[tile(d), tile(N_CHIPS * nc), BS((1, d), lambda i: (0, 0))],
        out_shape=[SDS((s, d), F32), SDS((s, N_CHIPS * nc), m), SDS((1, d), F32)], compiler_params=_params(),
    )(dres, x, norm1, w_in, du_a, dv1, dv2, du_c, dzg)


def _tn_matmul(name, a, a_spec, b, b_spec, out_shape, out_spec, grid, wire=True):
    last = grid[1] - 1

    def body(a_ref, b_ref, o_ref, *wire_ref):
        @pl.when(pl.program_id(1) == 0)
        def _():
            o_ref[...] = jnp.zeros(o_ref.shape, F32)

        o_ref[...] += _mm_tn(a_ref[...], b_ref[...])

        if wire:
            @pl.when(pl.program_id(1) == last)
            def _():
                wire_ref[0][...] = o_ref[...].astype(WIRE_DTYPE)

    if not wire:
        return pl.pallas_call(body, name=name, grid=grid, in_specs=[a_spec, b_spec], out_specs=out_spec,
                              out_shape=out_shape, compiler_params=_params())(a, b)
    return pl.pallas_call(body, name=name, grid=grid, in_specs=[a_spec, b_spec], out_specs=[out_spec, out_spec],
                          out_shape=[out_shape, SDS(out_shape.shape, WIRE_DTYPE)], compiler_params=_params())(a, b)


def _scan_consts(pw_ref, lanes, reverse):
    sgn = -1.0 if reverse else 1.0
    steps = [(k, pw_ref[2 * i], sgn * pw_ref[2 * i + 1]) for i, k in enumerate((1, 2, 4))]
    c = 4 if reverse else 3
    return steps, pw_ref[2 * c], sgn * pw_ref[2 * c + 1]


def _scan_block(br, bi, steps, row, reverse):
    for k, ar, ai in steps:
        if reverse:
            mask, sh = row < 8 - k, 8 - k
        else:
            mask, sh = row >= k, k
        sr = jnp.where(mask, pltpu.roll(br, sh, 0), 0.0)
        si = jnp.where(mask, pltpu.roll(bi, sh, 0), 0.0)
        br, bi = br + ar * sr - ai * si, bi + ar * si + ai * sr
    return br, bi


def _ssm_fwd(l, z, bblk_re, bblk_im, cblk_re, cblk_im, pw, dskip):
    s = z.shape[0]
    gc = bblk_re.shape[1]
    gl = bblk_re.shape[2]
    nblk = bblk_re.shape[0]

    def body(u_ref, bre, bim, cre, cim, pw_ref, d_ref, hre, him, y_ref):
        u = u_ref[...]
        hre[...] = _mm(u, bre[...])
        him[...] = _mm(u, bim[...])
        row = lax.broadcasted_iota(jnp.int32, (8, gl), 0)
        steps, car, cai = _scan_consts(pw_ref, gl, False)

        def step(i, carry):
            cr, ci = carry
            r0 = pl.multiple_of(i * 8, 8)
            br, bi = _scan_block(hre[pl.ds(r0, 8), :], him[pl.ds(r0, 8), :], steps, row, False)
            hr = br + car * cr - cai * ci
            hi = bi + car * ci + cai * cr
            hre[pl.ds(r0, 8), :] = hr
            him[pl.ds(r0, 8), :] = hi
            return jnp.broadcast_to(hr[7:8, :], (8, gl)), jnp.broadcast_to(hi[7:8, :], (8, gl))

        zero = jnp.zeros((8, gl), F32)
        lax.fori_loop(0, s // 8, step, (zero, zero))
        y_ref[...] = _mm_nt(hre[...], cre[...]) - _mm_nt(him[...], cim[...]) + d_ref[...] * u

    return pl.pallas_call(
        body, name=f"ssm_fwd_l{l}", grid=(nblk,),
        in_specs=[BS((s, gc), lambda k: (0, k)), BS((None, gc, gl), lambda k: (k, 0, 0)),
                  BS((None, gc, gl), lambda k: (k, 0, 0)), BS((None, gc, gl), lambda k: (k, 0, 0)),
                  BS((None, gc, gl), lambda k: (k, 0, 0)), BS((10, 8, gl), lambda k: (0, 0, k)),
                  BS((1, gc), lambda k: (0, k))],
        out_specs=[BS((s, gl), lambda k: (0, k)), BS((s, gl), lambda k: (0, k)), BS((s, gc), lambda k: (0, k))],
        out_shape=[SDS((s, nblk * gl), F32), SDS((s, nblk * gl), F32), SDS((s, nblk * gc), F32)],
        compiler_params=_params())(z, bblk_re, bblk_im, cblk_re, cblk_im, pw, dskip)


def _ssm_bwd(l, dy, z, hre, him, bblk_re, bblk_im, cblk_re, cblk_im, pw, dskip):
    s = z.shape[0]
    nblk, gc, gl = bblk_re.shape

    def body(dy_ref, u_ref, hre_ref, him_ref, bre, bim, cre, cim, pw_ref, d_ref,
             du_ref, dbre_ref, dbim_ref, dcre_ref, dcim_ref, dar_ref, dai_ref, dd_ref, gre, gim):
        dyv = dy_ref[...]
        u = u_ref[...]
        gre[...] = _mm(dyv, cre[...])
        gim[...] = -_mm(dyv, cim[...])
        dcre_ref[...] = _mm_tn(dyv, hre_ref[...])
        dcim_ref[...] = -_mm_tn(dyv, him_ref[...])
        dd_ref[...] = _colsum(dyv * u)
        row = lax.broadcasted_iota(jnp.int32, (8, gl), 0)
        steps, car, cai = _scan_consts(pw_ref, gl, True)
        n8 = s // 8

        def step(ii, carry):
            cr, ci, accr, acci = carry
            i = n8 - 1 - ii
            r0 = pl.multiple_of(i * 8, 8)
            br, bi = _scan_block(gre[pl.ds(r0, 8), :], gim[pl.ds(r0, 8), :], steps, row, True)
            dr = br + car * cr - cai * ci
            di = bi + car * ci + cai * cr
            gre[pl.ds(r0, 8), :] = dr
            gim[pl.ds(r0, 8), :] = di
            rp = pl.multiple_of(jnp.maximum(i - 1, 0) * 8, 8)
            keep = jnp.where(i > 0, 1.0, 0.0)
            pr = jnp.where(row >= 1, pltpu.roll(hre_ref[pl.ds(r0, 8), :], 1, 0),
                           keep * pltpu.roll(hre_ref[pl.ds(rp, 8), :], 1, 0))
            pi = jnp.where(row >= 1, pltpu.roll(him_ref[pl.ds(r0, 8), :], 1, 0),
                           keep * pltpu.roll(him_ref[pl.ds(rp, 8), :], 1, 0))
            accr = accr + dr * pr + di * pi
            acci = acci + di * pr - dr * pi
            return (jnp.broadcast_to(dr[0:1, :], (8, gl)), jnp.broadcast_to(di[0:1, :], (8, gl)), accr, acci)

        zero = jnp.zeros((8, gl), F32)
        _, _, accr, acci = lax.fori_loop(0, n8, step, (zero, zero, zero, zero))
        dar_ref[...] = _colsum(accr)
        dai_ref[...] = _colsum(acci)
        dbr = gre[...]
        dbi = gim[...]
        du_ref[...] = dyv * d_ref[...] + _mm_nt(dbr, bre[...]) + _mm_nt(dbi, bim[...])
        dbre_ref[...] = _mm_tn(u, dbr)
        dbim_ref[...] = _mm_tn(u, dbi)

    col = lambda n: BS((s, n), lambda k: (0, k))
    blk = lambda a, b: BS((None, a, b), lambda k: (k, 0, 0))
    outs = pl.pallas_call(
        body, name=f"ssm_bwd_l{l}", grid=(nblk,),
        in_specs=[col(gc), col(gc), col(gl), col(gl), blk(gc, gl), blk(gc, gl), blk(gc, gl), blk(gc, gl),
                  BS((10, 8, gl), lambda k: (0, 0, k)), BS((1, gc), lambda k: (0, k))],
        out_specs=[col(gc), blk(gc, gl), blk(gc, gl), blk(gc, gl), blk(gc, gl), BS((1, gl), lambda k: (0, k)),
                   BS((1, gl), lambda k: (0, k)), BS((1, gc), lambda k: (0, k))],
        out_shape=[SDS((s, nblk * gc), F32), SDS((nblk, gc, gl), F32), SDS((nblk, gc, gl), F32),
                   SDS((nblk, gc, gl), F32), SDS((nblk, gc, gl), F32), SDS((1, nblk * gl), F32),
                   SDS((1, nblk * gl), F32), SDS((1, nblk * gc), F32)],
        scratch_shapes=[pltpu.VMEM((s, gl), F32), pltpu.VMEM((s, gl), F32)], compiler_params=_params(),
    )(dy, z, hre, him, bblk_re, bblk_im, cblk_re, cblk_im, pw, dskip)
    return dict(zip(("du", "dbblk_re", "dbblk_im", "dcblk_re", "dcblk_im", "dabar_re", "dabar_im", "dd"), outs))


def _conv_fwd(l, z, wdw, bdw):
    s = z.shape[0]
    cw = wdw.shape[1]
    lb = 128
    tr = min(256, s)
    off1 = cw // lb
    off2 = 2 * cw // lb

    def body(v1_ref, v2_ref, w_ref, b_ref, hc_ref, scr):
        scr[0:CONV_PAD, :] = jnp.zeros((CONV_PAD, lb), F32)
        scr[CONV_PAD:, :] = v1_ref[...] * _sigmoid(v2_ref[...])
        for t in range(s // tr):
            acc = jnp.broadcast_to(b_ref[...], (tr, lb))
            for k in range(CONV_KERNEL):
                acc = acc + w_ref[pl.ds(k, 1), :] * scr[pl.ds(t * tr + CONV_PAD - (CONV_KERNEL - 1) + k, tr), :]
            hc_ref[pl.ds(t * tr, tr), :] = acc

    return pl.pallas_call(
        body, name=f"conv_fwd_l{l}", grid=(cw // lb,),
        in_specs=[BS((s, lb), lambda k: (0, off1 + k)), BS((s, lb), lambda k: (0, off2 + k)),
                  BS((CONV_KERNEL, lb), lambda k: (0, k)), BS((1, lb), lambda k: (0, k))],
        out_specs=BS((s, lb), lambda k: (0, k)), out_shape=SDS((s, cw), F32),
        scratch_shapes=[pltpu.VMEM((s + CONV_PAD, lb), F32)], compiler_params=_params())(z, z, wdw, bdw)


def _conv_bwd(l, dhc, z, wdw):
    s = z.shape[0]
    cw = wdw.shape[1]
    lb = 128
    tr = min(256, s)
    off1 = cw // lb
    off2 = 2 * cw // lb
    nb = cw // lb

    def body(d_ref, v1_ref, v2_ref, w_ref, dv1_ref, dv2_ref, dw_ref, db_ref, hpad, dpad):
        v1 = v1_ref[...]
        sg = _sigmoid(v2_ref[...])
        dv = d_ref[...]
        hpad[0:CONV_PAD, :] = jnp.zeros((CONV_PAD, lb), F32)
        hpad[CONV_PAD:, :] = v1 * sg
        dpad[0:s, :] = dv
        dpad[s:, :] = jnp.zeros((CONV_PAD, lb), F32)
        db_ref[...] = _colsum(dv)
        dws = [jnp.zeros((1, lb), F32) for _ in range(CONV_KERNEL)]
        for t in range(s // tr):
            dt = d_ref[pl.ds(t * tr, tr), :]
            acc = jnp.zeros((tr, lb), F32)
            for k in range(CONV_KERNEL):
                acc = acc + w_ref[pl.ds(k, 1), :] * dpad[pl.ds(t * tr + (CONV_KERNEL - 1) - k, tr), :]
                dws[k] = dws[k] + _colsum(dt * hpad[pl.ds(t * tr + CONV_PAD - (CONV_KERNEL - 1) + k, tr), :])
            sgt = _sigmoid(v2_ref[pl.ds(t * tr, tr), :])
            v1t = v1_ref[pl.ds(t * tr, tr), :]
            dv1_ref[pl.ds(t * tr, tr), :] = acc * sgt
            dv2_ref[pl.ds(t * tr, tr), :] = acc * v1t * (sgt * (1.0 - sgt))
        for k in range(CONV_KERNEL):
            dw_ref[pl.ds(k, 1), :] = dws[k]

    return pl.pallas_call(
        body, name=f"conv_bwd_l{l}", grid=(nb,),
        in_specs=[BS((s, lb), lambda k: (0, k)), BS((s, lb), lambda k: (0, off1 + k)),
                  BS((s, lb), lambda k: (0, off2 + k)), BS((CONV_KERNEL, lb), lambda k: (0, k))],
        out_specs=[BS((s, lb), lambda k: (0, k)), BS((s, lb), lambda k: (0, k)),
                   BS((CONV_KERNEL, lb), lambda k: (0, k)), BS((1, lb), lambda k: (0, k))],
        out_shape=[SDS((s, cw), F32), SDS((s, cw), F32), SDS((CONV_KERNEL, cw), F32), SDS((1, cw), F32)],
        scratch_shapes=[pltpu.VMEM((s + CONV_PAD, lb), F32), pltpu.VMEM((s + CONV_PAD, lb), F32)],
        compiler_params=_params())(dhc, z, z, wdw)


def _pool_window(k):
    return jnp.where(k == 0, float(POOL_WINDOWS[0]),
                     jnp.where(k == 1, float(POOL_WINDOWS[1]),
                               jnp.where(k == 2, float(POOL_WINDOWS[2]), float(POOL_WINDOWS[3]))))


def _pool_fwd(l, z, pw_width):
    s = z.shape[0]
    lb = pw_width // len(POOL_WINDOWS)
    off = 3 * pw_width // lb

    def body(u_ref, p_ref):
        k = pl.program_id(0)
        u = u_ref[...]
        row = lax.broadcasted_iota(jnp.int32, (s, lb), 0)
        sums = [u]
        for sh in (1, 2, 4, 8):
            prev = sums[-1]
            sums.append(prev + jnp.where(row >= sh, pltpu.roll(prev, sh, 0), 0.0))
        sel = jnp.where(k == 0, sums[1], jnp.where(k == 1, sums[2], jnp.where(k == 2, sums[3], sums[4])))
        cnt = jnp.minimum((row + 1).astype(F32), _pool_window(k))
        p_ref[...] = sel / cnt - u

    return pl.pallas_call(
        body, name=f"pool_fwd_l{l}", grid=(len(POOL_WINDOWS),),
        in_specs=[BS((s, lb), lambda k: (0, off + k))], out_specs=BS((s, lb), lambda k: (0, k)),
        out_shape=SDS((s, pw_width), F32), compiler_params=_params())(z)


def _pool_bwd(l, dp):
    s, width = dp.shape
    lb = width // len(POOL_WINDOWS)

    def body(d_ref, du_ref):
        k = pl.program_id(0)
        dv = d_ref[...]
        row = lax.broadcasted_iota(jnp.int32, (s, lb), 0)
        cnt = jnp.minimum((row + 1).astype(F32), _pool_window(k))
        sums = [dv / cnt]
        for sh in (1, 2, 4, 8):
            prev = sums[-1]
            sums.append(prev + jnp.where(row < s - sh, pltpu.roll(prev, s - sh, 0), 0.0))
        sel = jnp.where(k == 0, sums[1], jnp.where(k == 1, sums[2], jnp.where(k == 2, sums[3], sums[4])))
        du_ref[...] = sel - dv

    return pl.pallas_call(
        body, name=f"pool_bwd_l{l}", grid=(len(POOL_WINDOWS),),
        in_specs=[BS((s, lb), lambda k: (0, k))], out_specs=BS((s, lb), lambda k: (0, k)),
        out_shape=SDS((s, width), F32), compiler_params=_params())(dp)


def _zoh(a_re, a_im, log_dt):
    dt = jnp.exp(log_dt)
    mag = jnp.exp(dt * a_re)
    ang = dt * a_im
    abar_re = mag * jnp.cos(ang)
    abar_im = mag * jnp.sin(ang)
    den = a_re * a_re + a_im * a_im
    nr = abar_re - 1.0
    ni = abar_im
    f_re = (nr * a_re + ni * a_im) / den
    f_im = (ni * a_re - nr * a_im) / den
    return abar_re, abar_im, f_re, f_im


def _zoh_fwd(l, a_re, a_im, log_dt):
    def body(ar, ai, ld, o0, o1, o2, o3):
        for ref, val in zip((o0, o1, o2, o3), _zoh(ar[...], ai[...], ld[...])):
            ref[...] = val

    return pl.pallas_call(body, name=f"zoh_fwd_l{l}", out_shape=[SDS(a_re.shape, F32)] * 4)(a_re, a_im, log_dt)


def _zoh_bwd(l, a_re, a_im, log_dt, cts):
    def body(ar, ai, ld, c0, c1, c2, c3, dar, dai, dld):
        _, vjp = jax.vjp(_zoh, ar[...], ai[...], ld[...])
        g = vjp((c0[...], c1[...], c2[...], c3[...]))
        dar[...] = g[0]
        dai[...] = g[1]
        dld[...] = g[2]

    return pl.pallas_call(body, name=f"zoh_bwd_l{l}",
                          out_shape=[SDS(a_re.shape, F32), SDS(a_re.shape, F32), SDS(log_dt.shape, F32)],
                          )(a_re, a_im, log_dt, *cts)


def _bbar_fwd(l, f_re, f_im, b_re, b_im):
    g, p, n = b_re.shape[1:]

    def body(fr, fi, br, bi, o_re, o_im):
        o_re[...] = (fr[...] * br[...] - fi[...] * bi[...]).astype(o_re.dtype)
        o_im[...] = (fr[...] * bi[...] + fi[...] * br[...]).astype(o_im.dtype)

    whole = lambda shp: BS(shp, lambda i: (0,) * len(shp))
    layer = BS((None, g, p, n), lambda i: (l, 0, 0, 0))
    return pl.pallas_call(body, name=f"bbar_fwd_l{l}", grid=(1,),
                          in_specs=[whole((g, 1, n)), whole((g, 1, n)), layer, layer],
                          out_specs=[whole((g, p, n))] * 2,
                          out_shape=[SDS((g, p, n), MXU_DTYPE)] * 2)(f_re, f_im, b_re, b_im)


def _bbar_bwd(l, f_re, f_im, b_re, b_im, d_re, d_im):
    g, p, n = b_re.shape[1:]

    def body(fr, fi, br, bi, dr, di, dfr, dfi, dbr, dbi):
        dfr[...] = jnp.sum(dr[...] * br[...] + di[...] * bi[...], axis=1, keepdims=True)
        dfi[...] = jnp.sum(di[...] * br[...] - dr[...] * bi[...], axis=1, keepdims=True)
        dbr[...] = fr[...] * dr[...] + fi[...] * di[...]
        dbi[...] = fr[...] * di[...] - fi[...] * dr[...]

    whole = lambda shp: BS(shp, lambda i: (0,) * len(shp))
    layer = BS((None, g, p, n), lambda i: (l, 0, 0, 0))
    return pl.pallas_call(body, name=f"bbar_bwd_l{l}", grid=(1,),
                          in_specs=[whole((g, 1, n)), whole((g, 1, n)), layer, layer, whole((g, p, n)),
                                    whole((g, p, n))],
                          out_specs=[whole((g, 1, n)), whole((g, 1, n)), whole((g, p, n)), whole((g, p, n))],
                          out_shape=[SDS((g, 1, n), F32), SDS((g, 1, n), F32), SDS((g, p, n), F32),
                                     SDS((g, p, n), F32)])(f_re, f_im, b_re, b_im, d_re, d_im)


def _powers(l, abar_re, abar_im):
    lanes = abar_re.shape[1]

    def body(ar_ref, ai_ref, o_ref):
        ar, ai = ar_ref[...], ai_ref[...]
        pows = [(ar, ai)]
        for _ in range(7):
            pr, pi = pows[-1]
            pows.append((pr * ar - pi * ai, pr * ai + pi * ar))
        row = lax.broadcasted_iota(jnp.int32, (8, lanes), 0)
        for i, k in enumerate((1, 2, 4)):
            o_ref[2 * i] = jnp.broadcast_to(pows[k - 1][0], (8, lanes))
            o_ref[2 * i + 1] = jnp.broadcast_to(pows[k - 1][1], (8, lanes))
        for slot, order in ((3, range(8)), (4, range(7, -1, -1))):
            vr = jnp.zeros((8, lanes), F32)
            vi = jnp.zeros((8, lanes), F32)
            for r, e in enumerate(order):
                vr = jnp.where(row == r, pows[e][0], vr)
                vi = jnp.where(row == r, pows[e][1], vi)
            o_ref[2 * slot] = vr
            o_ref[2 * slot + 1] = vi

    return pl.pallas_call(body, name=f"powers_l{l}", out_shape=SDS((10, 8, lanes), F32))(abar_re, abar_im)


def _block_diag(v):
    g, a, b = v.shape
    eye = jnp.eye(8, dtype=v.dtype)
    out = jnp.einsum("kgab,gh->kgahb", v.reshape(g // 8, 8, a, b), eye)
    return out.reshape(g // 8, 8 * a, 8 * b)


def _block_diag_extract(blk, a, b):
    n = blk.shape[0]
    v = blk.reshape(n, 8, a, 8, b)
    return jnp.einsum("kgahb,gh->kgab", v, jnp.eye(8, dtype=blk.dtype)).reshape(n * 8, a, b)


def _ssm_prepare(l, prm):
    g, n, p = SSM_GROUPS, SSM_STATE, SSM_GROUP
    a_re, a_im = prm["ssm_a_re"][l], prm["ssm_a_im"][l]
    log_dt = prm["ssm_log_dt"][l].reshape(g, 1)
    abar_re, abar_im, f_re, f_im = _zoh_fwd(l, a_re, a_im, log_dt)
    f_re, f_im = f_re.reshape(g, 1, n), f_im.reshape(g, 1, n)
    bbar_re, bbar_im = _bbar_fwd(l, f_re, f_im, prm["ssm_b_re"], prm["ssm_b_im"])
    pw = _powers(l, abar_re.reshape(1, g * n), abar_im.reshape(1, g * n))
    return dict(a_re=a_re, a_im=a_im, log_dt=log_dt, f_re=f_re, f_im=f_im,
                bblk_re=_block_diag(bbar_re), bblk_im=_block_diag(bbar_im),
                cblk_re=_block_diag(prm["ssm_c_re"][l].astype(MXU_DTYPE)),
                cblk_im=_block_diag(prm["ssm_c_im"][l].astype(MXU_DTYPE)), pw=pw,
                dskip=prm["ssm_d"][l].reshape(1, g * p))


def _ssm_param_grads(l, sd, r, prm):
    g, n, p = SSM_GROUPS, SSM_STATE, SSM_GROUP
    dbbar_re = _block_diag_extract(r["dbblk_re"], p, n)
    dbbar_im = _block_diag_extract(r["dbblk_im"], p, n)
    dfr, dfi, db_re, db_im = _bbar_bwd(l, sd["f_re"], sd["f_im"], prm["ssm_b_re"], prm["ssm_b_im"], dbbar_re, dbbar_im)
    cts = (r["dabar_re"].reshape(g, n), r["dabar_im"].reshape(g, n), dfr.reshape(g, n), dfi.reshape(g, n))
    da_re, da_im, dlog_dt = _zoh_bwd(l, sd["a_re"], sd["a_im"], sd["log_dt"], cts)
    return dict(ssm_a_re=da_re, ssm_a_im=da_im, ssm_log_dt=dlog_dt.reshape(g), ssm_b_re=db_re, ssm_b_im=db_im,
                ssm_c_re=_block_diag_extract(r["dcblk_re"], p, n), ssm_c_im=_block_diag_extract(r["dcblk_im"], p, n),
                ssm_d=r["dd"].reshape(g, p))


def _ffn_weight_grads(l, fb, dx2, s):
    d = dx2.shape[1]
    hcn = fb["act"].shape[-1]
    g = {}
    for name, key, rhs, ts in (("ffn_w_gate", "dgate", fb["h2"], s), ("ffn_w_up", "dup", fb["h2"], s),
                               ("ffn_w_down", "act", dx2, min(1024, s))):
        g[name] = _tn_matmul(f"d{name}_l{l}", fb[key], BS((None, ts, hcn), lambda j, t: (j, t, 0)), rhs,
                             BS((ts, d), lambda j, t: (t, 0)), SDS((N_CHIPS, hcn, d), F32),
                             BS((None, hcn, d), lambda j, t: (j, 0, 0)), (N_CHIPS, s // ts))
    return g


def _in_weight_grad(l, h, dz):
    s, d = h.shape
    ncw = dz.shape[1] // N_CHIPS
    return _tn_matmul(f"dw_in_l{l}", h, BS((s, d), lambda j, t: (0, 0)), dz, BS((s, ncw), lambda j, t: (0, j)),
                      SDS((N_CHIPS, d, ncw), F32), BS((None, d, ncw), lambda j, t: (j, 0, 0)), (N_CHIPS, 1))


def _fused_tn(name, pairs, kinds, s, wire):
    ts = min(512, s)
    n = len(pairs)

    def shape_of(a, b, kind):
        k, m = a.shape[1], b.shape[1]
        if kind == "rows":
            return (N_CHIPS, k // N_CHIPS, m)
        if kind == "cols":
            return (N_CHIPS, k, m // N_CHIPS)
        return (k // 128, 128, 128)

    shapes = [shape_of(a, b, kind) for (a, b), kind in zip(pairs, kinds)]
    last = s // ts - 1

    def body(*refs):
        ins, outs = refs[:2 * n], refs[2 * n:]
        first = pl.program_id(0) == 0
        for i, kind in enumerate(kinds):
            a, b = ins[2 * i][...], ins[2 * i + 1][...]
            o_ref = outs[i]

            @pl.when(first)
            def _():
                o_ref[...] = jnp.zeros(o_ref.shape, F32)

            if kind == "rows":
                o_ref[...] += _mm_tn(a, b).reshape(o_ref.shape)
            elif kind == "cols":
                full = _mm_tn(a, b)
                nc = o_ref.shape[2]
                for j in range(N_CHIPS):
                    o_ref[j] += full[:, j * nc:(j + 1) * nc]
            else:
                for k in range(o_ref.shape[0]):
                    o_ref[k] += _mm_tn(a[:, k * 128:(k + 1) * 128], b[:, k * 128:(k + 1) * 128])
        if wire:
            @pl.when(pl.program_id(0) == last)
            def _():
                for i in range(n):
                    outs[n + i][...] = outs[i][...].astype(WIRE_DTYPE)

    whole = lambda shp: BS(shp, lambda t: (0,) * len(shp))
    out_shape = [SDS(shp, F32) for shp in shapes] + ([SDS(shp, WIRE_DTYPE) for shp in shapes] if wire else [])
    outs = pl.pallas_call(
        body, name=name, grid=(s // ts,),
        in_specs=[BS((ts, v.shape[1]), lambda t: (t, 0)) for pair in pairs for v in pair],
        out_specs=[whole(o.shape) for o in out_shape], out_shape=out_shape, compiler_params=_params(),
    )(*[v for pair in pairs for v in pair])
    return [(outs[i], outs[n + i]) for i in range(n)] if wire else list(outs)


def _mixer_weight_grads(l, sv, mb, dx1, s):
    g = {}
    (g["w_out"], g["ssm_w_glu"]) = _fused_tn(f"dw_out_glu_l{l}", [(mb["merged"], dx1), (mb["ge"], mb["dt"])],
                                            ("rows", "rows"), s, True)
    (g["ssm_w_proj"], g["conv_w_proj"], g["pool_w_proj"]) = _fused_tn(
        f"dw_proj_l{l}", [(mb["sa"], mb["dya"]), (mb["ac"], mb["dyb"]), (mb["pp"], mb["dyc"])],
        ("cols", "cols", "cols"), s, True)
    (dwgrp,) = _fused_tn(f"dpool_w_group_l{l}", [(sv["p"], mb["dq"])], ("groups",), s, False)
    return g, dwgrp


def _local_step(x, target, weights_of, prm, on_grads=None):
    s, d = x.shape
    cw = prm["ssm_b_glu"].shape[1]
    sp = {k: prm[k].reshape(N_LAYERS, 1, -1) for k in ("norm1", "norm2", "b_gate", "ssm_b_glu", "conv_ln_g", "conv_ln_b",
                                                        "pool_scale", "conv_b_dw")}
    sp["pool_w_group"] = prm["pool_w_group"]
    saved = []
    xin = x
    for l in range(N_LAYERS):
        fw = weights_of(l, "in", (xin,))
        sd = _ssm_prepare(l, prm)
        z, h = _in_proj(l, xin, sp["norm1"], fw["w_in"])
        hre, him, y = _ssm_fwd(l, z, sd["bblk_re"], sd["bblk_im"], sd["cblk_re"], sd["cblk_im"], sd["pw"], sd["dskip"])
        p = _pool_fwd(l, z, cw)
        fw.update(weights_of(l, "mixer", (y, p)))
        wdw = fw["conv_w_dw"]
        hc = _conv_fwd(l, z, wdw, sp["conv_b_dw"][l])
        x1 = _merge_fwd(l, xin, y, hc, p, z, fw, sp)
        fw.update(weights_of(l, "ffn", (x1,)))
        x2 = _ffn_fwd(l, x1, sp["norm2"], fw["ffn_w_gate"], fw["ffn_w_up"], fw["ffn_w_down"])
        saved.append(dict(x=xin, z=z, h=h, hre=hre, him=him, y=y, hc=hc, p=p, x1=x1, sd=sd, wdw=wdw, fw=fw))
        xin = x2
    dx, loss, dfinal = _loss_head(xin, target, prm["final_norm"].reshape(1, d))
    big = [None] * N_LAYERS
    small = [None] * N_LAYERS
    norm2_rows = sp["norm2"]
    started = (lambda l, group, grads: on_grads(l, group, grads)) if on_grads is not None else (lambda *a: 0.0)
    for l in reversed(range(N_LAYERS)):
        sv = saved[l]
        sd, fw = sv["sd"], sv["fw"]
        fb = _ffn_bwd(l, sv["x1"], dx, norm2_rows, fw["ffn_w_gate"], fw["ffn_w_up"], fw["ffn_w_down"])
        big[l] = _ffn_weight_grads(l, fb, dx, s)
        spl = dict(sp, ssm_b_glu=sp["ssm_b_glu"] + started(l, "ffn", big[l]))
        mb = _merge_bwd(l, fb["dx1"], sv["y"], sv["hc"], sv["p"], sv["z"], fw, spl)
        mixer, dwgrp = _mixer_weight_grads(l, sv, mb, fb["dx1"], s)
        big[l].update(mixer)
        wdw = sv["wdw"] + started(l, "mixer", mixer)
        du_c = _pool_bwd(l, mb["dp"])
        dv1, dv2, dwdw, dbdw = _conv_bwd(l, mb["dhc"], sv["z"], wdw)
        sr = _ssm_bwd(l, mb["dy"], sv["z"], sv["hre"], sv["him"], sd["bblk_re"], sd["bblk_im"], sd["cblk_re"],
                      sd["cblk_im"], sd["pw"], sd["dskip"])
        dx, dz, dnorm1 = _in_proj_bwd(l, fb["dx1"], sv["x"], sp["norm1"], fw["w_in"], sr["du"], dv1, dv2, du_c, mb["dzg"])
        w_in_grad = {"w_in": _in_weight_grad(l, sv["h"], dz)}
        big[l].update(w_in_grad)
        sg = _ssm_param_grads(l, sd, sr, prm)
        sg.update(norm1=dnorm1.reshape(d), b_gate=mb["db_gate"].reshape(3 * d), ssm_b_glu=mb["db_glu"].reshape(cw),
                  conv_b_dw=dbdw.reshape(cw), conv_ln_g=mb["dln_g"].reshape(cw), conv_ln_b=mb["dln_b"].reshape(cw),
                  pool_w_group=dwgrp, pool_scale=mb["dscale"].reshape(cw), norm2=fb["dnorm2"].reshape(d),
                  conv_w_dw=dwdw)
        small[l] = sg
        if l == N_LAYERS - 1:
            sg = dict(sg, final_norm=dfinal.reshape(d))
        norm2_rows = sp["norm2"] + (started(l, "in", w_in_grad) + started(l, "small", sg))
    return loss[0, 0], dx, big, small, dfinal.reshape(d)


def _place():
    return lax.axis_index("x"), lax.axis_index("y"), lax.axis_index("c")


def _other_chips(x, y):
    return [(1 - x, y), (x, 1 - y), (1 - x, 1 - y)]


def _remote(src, dst, send_sem, recv_sem, device):
    return pltpu.make_async_remote_copy(src_ref=src, dst_ref=dst, send_sem=send_sem, recv_sem=recv_sem,
                                        device_id=device, device_id_type=MESH)


def _hbm(v):
    return pltpu.with_memory_space_constraint(v, pltpu.HBM)


def _cast_into(name, w, place, dtype, after=()):
    nl, k, n = w.shape
    tr = _row_tile(k, n)
    nt = k // tr

    def body(place_ref, w_ref, *rest):
        o0_ref, o1_ref = rest[len(after):]

        @pl.when(pl.program_id(0) == 0)
        def _():
            o0_ref[...] = w_ref[...].astype(dtype)

        @pl.when(pl.program_id(0) == 1)
        def _():
            o1_ref[...] = w_ref[...].astype(dtype)

    return pl.pallas_call(
        body, name=f"cast_{name}",
        grid_spec=pltpu.PrefetchScalarGridSpec(
            num_scalar_prefetch=1, grid=(nl, nt),
            in_specs=[BS((None, tr, n), lambda l, t, pr: (l, t, 0))] + [ANY] * len(after),
            out_specs=[BS((None, tr, n), lambda l, t, pr: (pr[0], t * (1 - l) + (nt - 1) * l, 0)),
                       BS((None, tr, n), lambda l, t, pr: (pr[0], t * l, 0))]),
        out_shape=[SDS((N_CHIPS, k, n), dtype)] * 2)(place, w, *after)


def _gather_rows(buf, c):
    k = buf.shape[1]
    if k % 2:
        return pl.ds(0, k)
    return pl.ds(pl.multiple_of(c * (k // 2), 8), k // 2)


def _allgather_start(tag, groups):
    ng = len(groups)
    sizes = [len(g) for g in groups]
    first = [sum(sizes[:g]) for g in range(ng)]
    flat = [b for g in groups for b in g]
    nb = len(flat)

    def body(*refs):
        ins = refs[:nb]
        sems = refs[nb:nb + 2 * ng]
        token = refs[-1]
        x, y, c = _place()
        jme = 2 * x + y
        for g in range(ng):
            for a in range(sizes[g]):
                buf = ins[first[g] + a]
                blk = buf.at[jme, _gather_rows(buf, c)]
                for k, (cx, cy) in enumerate(_other_chips(x, y)):
                    _remote(blk, blk, sems[2 * g].at[3 * a + k], sems[2 * g + 1].at[3 * a + k], (cx, cy, c)).start()
        token[...] = jnp.zeros(token.shape, F32)

    sem_shapes = [pltpu.SemaphoreType.DMA((3 * sizes[g // 2],)) for g in range(2 * ng)]
    outs = pl.pallas_call(
        body, name=f"allgather_start_{tag}", in_specs=[HBM] * nb,
        out_specs=[SEM] * (2 * ng) + [HBM] * nb + [pl.BlockSpec(memory_space=pltpu.VMEM)],
        out_shape=sem_shapes + [pltpu.HBM(b.shape, b.dtype) for b in flat] + [SDS((8, 128), F32)],
        input_output_aliases={i: 2 * ng + i for i in range(nb)},
        compiler_params=pltpu.CompilerParams(has_side_effects=SIDE_EFFECT))(*[_hbm(b) for b in flat])
    per_group = [(outs[2 * g], outs[2 * g + 1], outs[2 * ng + first[g]:2 * ng + first[g] + sizes[g]])
                 for g in range(ng)]
    return per_group, outs[-1]


def _allgather_wait(l, send_sems, recv_sems, bufs, after):
    n = len(bufs)

    def body(*refs):
        ins = refs[:n]
        ssem, rsem = refs[n], refs[n + 1]
        x, y, c = _place()
        jme = 2 * x + y
        for a in range(n):
            rows = _gather_rows(ins[a], c)
            for k, (cx, cy) in enumerate(_other_chips(x, y)):
                cp = _remote(ins[a].at[jme, rows], ins[a].at[2 * cx + cy, rows], ssem.at[3 * a + k],
                             rsem.at[3 * a + k], (cx, cy, c))
                cp.wait_send()
                cp.wait_recv()

    return pl.pallas_call(
        body, name=f"allgather_wait_{l}", in_specs=[HBM] * n + [SEM, SEM] + [ANY] * len(after), out_specs=[HBM] * n,
        out_shape=[pltpu.HBM(b.shape, b.dtype) for b in bufs], input_output_aliases={i: i for i in range(n)},
        compiler_params=pltpu.CompilerParams(has_side_effects=SIDE_EFFECT))(*bufs, send_sems, recv_sems, *after)


def _allgather_forward(l, bufs):
    n = len(bufs)
    split = [a for a in range(n) if bufs[a].shape[1] % 2 == 0]

    def body(*refs):
        ins = refs[:n]
        send_sems, recv_sems = refs[2 * n:]
        x, y, c = _place()
        sibling = (x, y, 1 - c)
        copies = []
        for a in split:
            for k, (cx, cy) in enumerate(_other_chips(x, y)):
                blk = ins[a].at[2 * cx + cy, _gather_rows(ins[a], c)]
                cp = _remote(blk, blk, send_sems.at[a, k], recv_sems.at[a, k], sibling)
                cp.start()
                copies.append(cp)
        for a in split:
            for k, (cx, cy) in enumerate(_other_chips(x, y)):
                blk = ins[a].at[2 * cx + cy, _gather_rows(ins[a], 1 - c)]
                _remote(blk, blk, send_sems.at[a, k], recv_sems.at[a, k], sibling).wait_recv()
        for cp in copies:
            cp.wait_send()

    sem = pltpu.SemaphoreType.DMA((n, 3))
    return pl.pallas_call(
        body, name=f"allgather_forward_{l}", in_specs=[ANY] * n, out_specs=[ANY] * n,
        out_shape=[SDS(b.shape, b.dtype) for b in bufs], input_output_aliases={i: i for i in range(n)},
        scratch_shapes=[sem, sem])(*bufs)


def _rs_to_owner(l, parts):
    n = len(parts)
    lands = [lax.empty((3,) + p.shape[1:], p.dtype) for p in parts]

    def body(*refs):
        ins, zones = refs[:n], refs[n:2 * n]
        send_sems, recv_sems = refs[2 * n], refs[2 * n + 1]
        token = refs[-1]
        x, y, c = _place()
        for a in range(n):
            for k, (cx, cy) in enumerate(_other_chips(x, y)):
                _remote(ins[a].at[2 * cx + cy], zones[a].at[k], send_sems.at[3 * a + k], recv_sems.at[3 * a + k],
                        (cx, cy, c)).start()
        token[...] = jnp.zeros(token.shape, F32)

    sem = pltpu.SemaphoreType.DMA((3 * n,))
    outs = pl.pallas_call(
        body, name=f"rs_to_owner_start_{l}", in_specs=[HBM] * (2 * n),
        out_specs=[SEM, SEM] + [HBM] * (2 * n) + [pl.BlockSpec(memory_space=pltpu.VMEM)],
        out_shape=[sem, sem] + [pltpu.HBM(p.shape, p.dtype) for p in parts]
        + [pltpu.HBM(z.shape, z.dtype) for z in lands] + [SDS((8, 128), F32)],
        input_output_aliases={i: 2 + i for i in range(2 * n)},
        compiler_params=pltpu.CompilerParams(has_side_effects=SIDE_EFFECT),
    )(*[_hbm(p) for p in parts], *[_hbm(z) for z in lands])
    return outs[0], outs[1], outs[2:2 + n], outs[2 + n:2 + 2 * n], outs[-1]


def _rs_to_owner_wait(l, send_sems, recv_sems, parts, lands, after):
    n = len(parts)

    def body(*refs):
        ins, zones = refs[:n], refs[n:2 * n]
        ssem, rsem = refs[2 * n], refs[2 * n + 1]
        x, y, c = _place()
        for a in range(n):
            for k, (cx, cy) in enumerate(_other_chips(x, y)):
                cp = _remote(ins[a].at[2 * cx + cy], zones[a].at[k], ssem.at[3 * a + k], rsem.at[3 * a + k],
                             (cx, cy, c))
                cp.wait_send()
                cp.wait_recv()

    outs = pl.pallas_call(
        body, name=f"rs_to_owner_wait_{l}", in_specs=[HBM] * (2 * n) + [SEM, SEM] + [ANY] * len(after),
        out_specs=[HBM] * (2 * n),
        out_shape=[pltpu.HBM(p.shape, p.dtype) for p in parts] + [pltpu.HBM(z.shape, z.dtype) for z in lands],
        input_output_aliases={i: i for i in range(2 * n)},
        compiler_params=pltpu.CompilerParams(has_side_effects=SIDE_EFFECT),
    )(*parts, *lands, send_sems, recv_sems, *after)
    return outs[:n], outs[n:]


def _rs_sibling_exchange(l, both):
    n = len(both)

    def body(*refs):
        ins = refs[:n]
        send_sems, recv_sems = refs[2 * n:]
        x, y, c = _place()
        copies = []
        for a in range(n):
            cp = _remote(ins[a].at[c], ins[a].at[c], send_sems.at[a], recv_sems.at[a], (x, y, 1 - c))
            cp.start()
            copies.append(cp)
        for a, cp in enumerate(copies):
            cp.wait_send()
            _remote(ins[a].at[1 - c], ins[a].at[1 - c], send_sems.at[a], recv_sems.at[a], (x, y, 1 - c)).wait_recv()

    sem = pltpu.SemaphoreType.DMA((n,))
    return pl.pallas_call(
        body, name=f"rs_sibling_exchange_{l}", in_specs=[ANY] * n, out_specs=[ANY] * n,
        out_shape=[SDS(b.shape, b.dtype) for b in both], input_output_aliases={i: i for i in range(n)},
        scratch_shapes=[sem, sem])(*both)


def _add_owner(name, grad, recv, place):
    _, r, cols = grad.shape
    tr = _row_tile(r, cols, budget=1024 * 1024)
    nt = r // tr

    def body(place_ref, g_ref, r_ref, o_ref):
        acc = ((g_ref[...] + r_ref[0].astype(F32)) + r_ref[1].astype(F32)) + r_ref[2].astype(F32)
        o_ref[...] = acc.astype(o_ref.dtype)

    return pl.pallas_call(
        body, name=name,
        grid_spec=pltpu.PrefetchScalarGridSpec(
            num_scalar_prefetch=1, grid=(nt,),
            in_specs=[BS((None, tr, cols), lambda t, pr: (pr[0], t, 0)), BS((3, tr, cols), lambda t, pr: (0, t, 0))],
            out_specs=BS((None, tr, cols), lambda t, pr: (pr[1], t, 0))),
        out_shape=SDS((2, r, cols), WIRE_DTYPE))(place, grad, recv)


def _reduce_start(tag, grads):
    names = list(grads)
    send_sems, recv_sems, wires, lands, token = _rs_to_owner(tag, [grads[n][1] for n in names])
    return dict(tag=tag, names=names, send_sems=send_sems, recv_sems=recv_sems, wires=wires, lands=lands,
                grads=[grads[n][0] for n in names]), token


def _reduce_finish(pending, place, after):
    tag, names = pending["tag"], pending["names"]
    _, lands = _rs_to_owner_wait(tag, pending["send_sems"], pending["recv_sems"], pending["wires"],
                                 pending["lands"], after)
    mine = [_add_owner(f"rs_add_owner_{n}_{tag}", g, r, place) for n, g, r in zip(names, pending["grads"], lands)]
    return dict(zip(names, _rs_sibling_exchange(tag, mine)))


def _small_peers(x, y, c):
    return [(x, y, 1 - c)] + [(cx, cy, c) for cx, cy in _other_chips(x, y)]


def _allgather_rows_start(tag, bufs):
    n = len(bufs)
    lands = [lax.empty((8,) + b.shape, b.dtype) for b in bufs]

    def body(*refs):
        ins, zones = refs[:n], refs[n:2 * n]
        send_sems, recv_sems = refs[2 * n], refs[2 * n + 1]
        token = refs[-1]
        x, y, c = _place()
        for a in range(n):
            for i, peer in enumerate(_small_peers(x, y, c)):
                _remote(ins[a], zones[a].at[4 * x + 2 * y + c], send_sems.at[4 * a + i], recv_sems.at[4 * a + i],
                        peer).start()
        token[...] = jnp.zeros(token.shape, F32)

    sem = pltpu.SemaphoreType.DMA((4 * n,))
    outs = pl.pallas_call(
        body, name=f"allgather_small_start_{tag}", in_specs=[HBM] * (2 * n),
        out_specs=[SEM, SEM] + [HBM] * (2 * n) + [pl.BlockSpec(memory_space=pltpu.VMEM)],
        out_shape=[sem, sem] + [pltpu.HBM(b.shape, b.dtype) for b in bufs]
        + [pltpu.HBM(z.shape, z.dtype) for z in lands] + [SDS((8, 128), F32)],
        input_output_aliases={i: 2 + i for i in range(2 * n)},
        compiler_params=pltpu.CompilerParams(has_side_effects=SIDE_EFFECT),
    )(*[_hbm(b) for b in bufs], *[_hbm(z) for z in lands])
    return outs[0], outs[1], outs[2:2 + n], outs[2 + n:2 + 2 * n], outs[-1]


def _allgather_rows_wait(tag, send_sems, recv_sems, bufs, lands, after):
    n = len(bufs)

    def body(*refs):
        ins, zones = refs[:n], refs[n:2 * n]
        ssem, rsem = refs[2 * n], refs[2 * n + 1]
        x, y, c = _place()
        for a in range(n):
            for i, (px, py, pc) in enumerate(_small_peers(x, y, c)):
                cp = _remote(ins[a], zones[a].at[4 * px + 2 * py + pc], ssem.at[4 * a + i], rsem.at[4 * a + i],
                             (px, py, pc))
                cp.wait_send()
                cp.wait_recv()

    outs = pl.pallas_call(
        body, name=f"allgather_small_wait_{tag}", in_specs=[HBM] * (2 * n) + [SEM, SEM, ANY],
        out_specs=[HBM] * (2 * n),
        out_shape=[pltpu.HBM(b.shape, b.dtype) for b in bufs] + [pltpu.HBM(z.shape, z.dtype) for z in lands],
        input_output_aliases={i: i for i in range(2 * n)},
        compiler_params=pltpu.CompilerParams(has_side_effects=SIDE_EFFECT),
    )(*bufs, *lands, send_sems, recv_sems, after)
    return outs[:n], outs[n:]


def _allgather_rows_forward(tag, lands):
    n = len(lands)

    def body(*refs):
        ins = refs[:n]
        send_sems, recv_sems = refs[2 * n:]
        x, y, c = _place()
        sibling = (x, y, 1 - c)
        copies = []
        for a in range(n):
            for k, (cx, cy) in enumerate(_other_chips(x, y)):
                blk = ins[a].at[4 * cx + 2 * cy + c]
                cp = _remote(blk, blk, send_sems.at[a, k], recv_sems.at[a, k], sibling)
                cp.start()
                copies.append(cp)
        for a in range(n):
            for k, (cx, cy) in enumerate(_other_chips(x, y)):
                blk = ins[a].at[4 * cx + 2 * cy + 1 - c]
                _remote(blk, blk, send_sems.at[a, k], recv_sems.at[a, k], sibling).wait_recv()
        for cp in copies:
            cp.wait_send()

    sem = pltpu.SemaphoreType.DMA((n, 3))
    return pl.pallas_call(body, name=f"allgather_small_forward_{tag}", in_specs=[ANY] * n, out_specs=[ANY] * n,
                          out_shape=[SDS(z.shape, z.dtype) for z in lands],
                          input_output_aliases={i: i for i in range(n)}, scratch_shapes=[sem, sem])(*lands)


def _sum_devices(tag, gathered, mine, place):
    _, r, cols = gathered.shape
    tr = _row_tile(r, cols, budget=256 * 1024)

    def body(place_ref, g_ref, x_ref, o_ref):
        me = 2 * place_ref[0] + place_ref[1]
        acc = jnp.where(me == 0, x_ref[...], g_ref[0])
        for k in range(1, 8):
            acc = acc + jnp.where(me == k, x_ref[...], g_ref[k])
        o_ref[...] = acc

    return pl.pallas_call(
        body, name=f"sum_small_grads_{tag}",
        grid_spec=pltpu.PrefetchScalarGridSpec(
            num_scalar_prefetch=1, grid=(r // tr,),
            in_specs=[BS((8, tr, cols), lambda t, pr: (0, t, 0)), BS((tr, cols), lambda t, pr: (t, 0))],
            out_specs=BS((tr, cols), lambda t, pr: (t, 0))),
        out_shape=SDS((r, cols), F32))(place, gathered, mine)


def _adamw_values(w, g, m, v):
    m = ADAM_B1 * m + (1.0 - ADAM_B1) * g
    v = ADAM_B2 * v + (1.0 - ADAM_B2) * (g * g)
    m_hat = m / (1.0 - ADAM_B1 ** ADAM_STEP)
    v_hat = v / (1.0 - ADAM_B2 ** ADAM_STEP)
    delta = -ADAM_LR * (m_hat / (jnp.sqrt(v_hat) + ADAM_EPS) + ADAM_WD * w)
    return delta, m, v


def _adamw_big(name, l, w, m, v, g, earlier=None, after=()):
    nl, r, cols = w.shape
    tr = _row_tile(r, cols, budget=1024 * 1024)
    nt = r // tr
    n_prev = 0 if earlier is None else 4

    def body(*refs):
        w_ref, m_ref, v_ref, g_ref = refs[:4]
        go_ref, d_ref, mo_ref, vo_ref = refs[4 + n_prev + len(after):]
        gv = g_ref[0].astype(F32) + g_ref[1].astype(F32)
        delta, m_new, v_new = _adamw_values(w_ref[...], gv, m_ref[...], v_ref[...])
        go_ref[...] = gv
        d_ref[...] = delta
        mo_ref[...] = m_new
        vo_ref[...] = v_new

    layer = BS((None, tr, cols), lambda t: (l, t, 0))
    return pl.pallas_call(
        body, name=f"adamw_{name}_l{l}", grid=(nt,),
        in_specs=[layer, layer, layer, BS((2, tr, cols), lambda t: (0, t, 0))] + [ANY] * (n_prev + len(after)),
        out_specs=[layer] * 4, out_shape=[SDS(w.shape, F32)] * 4,
        input_output_aliases={4 + i: i for i in range(n_prev)}, compiler_params=_params(),
    )(w, m, v, g, *(earlier or ()), *after)


def _adamw_mid(name, w, m, v, gathered, mine, place):
    shape = w.shape[1:]
    zeros = (0,) * len(shape)

    def body(place_ref, w_ref, m_ref, v_ref, *refs):
        gath, own = refs[:N_LAYERS], refs[N_LAYERS:2 * N_LAYERS]
        go_ref, d_ref, mo_ref, vo_ref = refs[2 * N_LAYERS:]
        me = 2 * place_ref[0] + place_ref[1]
        sums = []
        for l in range(N_LAYERS):
            acc = jnp.where(me == 0, own[l][...], gath[l][0])
            for k in range(1, 8):
                acc = acc + jnp.where(me == k, own[l][...], gath[l][k])
            sums.append(acc)
        gv = sums[0]
        for l in range(1, N_LAYERS):
            gv = jnp.where(pl.program_id(0) == l, sums[l], gv)
        delta, m_new, v_new = _adamw_values(w_ref[...], gv, m_ref[...], v_ref[...])
        go_ref[...] = gv
        d_ref[...] = delta
        mo_ref[...] = m_new
        vo_ref[...] = v_new

    layer = BS((None,) + shape, lambda l, pr: (l,) + zeros)
    return pl.pallas_call(
        body, name=f"adamw_{name}",
        grid_spec=pltpu.PrefetchScalarGridSpec(
            num_scalar_prefetch=1, grid=(N_LAYERS,),
            in_specs=[layer] * 3 + [BS((8,) + shape, lambda l, pr: (0,) + zeros)] * N_LAYERS
            + [BS(shape, lambda l, pr: zeros)] * N_LAYERS,
            out_specs=[layer] * 4),
        out_shape=[SDS(w.shape, F32)] * 4, compiler_params=_params())(place, w, m, v, *gathered, *mine)


def _adamw_rows(w, m, v, g):
    r, cols = w.shape
    tr = _row_tile(r, cols, budget=512 * 1024)

    def body(w_ref, m_ref, v_ref, g_ref, d_ref, mo_ref, vo_ref):
        delta, m_new, v_new = _adamw_values(w_ref[...], g_ref[...], m_ref[...], v_ref[...])
        d_ref[...] = delta
        mo_ref[...] = m_new
        vo_ref[...] = v_new

    spec = BS((tr, cols), lambda t: (t, 0))
    return pl.pallas_call(body, name="adamw_small", grid=(r // tr,), in_specs=[spec] * 4, out_specs=[spec] * 3,
                          out_shape=[SDS(w.shape, F32)] * 3)(w, m, v, g)


PACK_ALIGN = 8 * 128
PACK_ROWS = 128


def _pack_rows(arrays):
    parts, rows = [], 0
    for a in arrays:
        flat = a.reshape(-1)
        pad = (-flat.shape[0]) % PACK_ALIGN
        if pad:
            flat = jnp.pad(flat, (0, pad))
        parts.append(flat.reshape(-1, 128))
        rows += parts[-1].shape[0]
    if rows % PACK_ROWS:
        parts.append(jnp.zeros((PACK_ROWS - rows % PACK_ROWS, 128), parts[0].dtype))
    return jnp.concatenate(parts, axis=0)


def _unpack_rows(buf, shapes):
    out, row = [], 0
    for shape in shapes:
        size = math.prod(shape)
        rows = -(-size // PACK_ALIGN) * (PACK_ALIGN // 128)
        out.append(buf[row:row + rows].reshape(-1)[:size].reshape(shape))
        row += rows
    return out


def kernel(x, norm1, w_in, b_gate, ssm_a_re, ssm_a_im, ssm_log_dt, ssm_b_re, ssm_b_im, ssm_c_re, ssm_c_im, ssm_d, ssm_w_glu, ssm_b_glu, ssm_w_proj, conv_w_dw, conv_b_dw, conv_ln_g, conv_ln_b, conv_w_proj, pool_w_group, pool_scale, pool_w_proj, w_out, norm2, ffn_w_gate, ffn_w_up, ffn_w_down, final_norm, loss_target, m_norm1, m_w_in, m_b_gate, m_ssm_a_re, m_ssm_a_im, m_ssm_log_dt, m_ssm_b_re, m_ssm_b_im, m_ssm_c_re, m_ssm_c_im, m_ssm_d, m_ssm_w_glu, m_ssm_b_glu, m_ssm_w_proj, m_conv_w_dw, m_conv_b_dw, m_conv_ln_g, m_conv_ln_b, m_conv_w_proj, m_pool_w_group, m_pool_scale, m_pool_w_proj, m_w_out, m_norm2, m_ffn_w_gate, m_ffn_w_up, m_ffn_w_down, m_final_norm, v_norm1, v_w_in, v_b_gate, v_ssm_a_re, v_ssm_a_im, v_ssm_log_dt, v_ssm_b_re, v_ssm_b_im, v_ssm_c_re, v_ssm_c_im, v_ssm_d, v_ssm_w_glu, v_ssm_b_glu, v_ssm_w_proj, v_conv_w_dw, v_conv_b_dw, v_conv_ln_g, v_conv_ln_b, v_conv_w_proj, v_pool_w_group, v_pool_scale, v_pool_w_proj, v_w_out, v_norm2, v_ffn_w_gate, v_ffn_w_up, v_ffn_w_down, v_final_norm):
    given = dict(locals())
    cx, cy, cc = _place()
    place = jnp.stack([2 * cx + cy, cc]).astype(jnp.int32)

    def kernel_view(n, a):
        if n in TRANSPOSED:
            return a.transpose(0, 2, 1)
        return a.transpose(0, 1, 3, 2) if n in ("ssm_b_re", "ssm_b_im") else a

    prm = {n: given[n] for n in WEIGHTS}
    mom = {n: given["m_" + n] for n in WEIGHTS}
    var = {n: given["v_" + n] for n in WEIGHTS}
    for n in MID:
        prm[n], mom[n], var[n] = kernel_view(n, prm[n]), kernel_view(n, mom[n]), kernel_view(n, var[n])

    dw_shard = prm["conv_w_dw"].reshape(N_LAYERS, CONV_KERNEL, -1)
    casts = {"w_in": _cast_into("w_in", prm["w_in"], place, MXU_DTYPE)}
    first, first_started = _allgather_start("first", [[casts["w_in"][0]]])
    in_flight = {(0, "in"): first[0]}
    casts.update({n: _cast_into(n, kernel_view(n, prm[n]), place, MXU_DTYPE, after=(first_started,))
                  for n in BIG if n != "w_in"})
    casts["conv_w_dw"] = _cast_into("conv_w_dw", dw_shard, place, F32, after=(first_started,))
    order = [(l, g) for l in range(N_LAYERS) for g in GATHER_GROUPS if (l, g) != (0, "in")]
    rest, rest_started = _allgather_start("rest", [[casts[n][l] for n in GATHER_GROUPS[g]] for l, g in order])
    in_flight.update(zip(order, rest))

    def weights_of(l, group, after):
        send_sems, recv_sems, bufs = in_flight[l, group]
        tag = f"l{l}_{group}"
        if (l, group) == (0, "in"):
            after = after + (rest_started,)
        bufs = _allgather_forward(tag, _allgather_wait(tag, send_sems, recv_sems, bufs, after))
        fw = dict(zip(GATHER_GROUPS[group], bufs))
        if "conv_w_dw" in fw:
            fw["conv_w_dw"] = fw["conv_w_dw"].transpose(1, 0, 2).reshape(CONV_KERNEL, -1)
        return fw

    pending, small_pending, small_shapes = {}, {}, {}
    tokens = {}

    def on_grads(l, group, grads):
        if group == "small":
            packed = {n: g for n, g in grads.items() if n not in MID}
            small_shapes[l] = {n: g.shape for n, g in packed.items()}
            begun = _allgather_rows_start(f"l{l}", [_pack_rows(list(packed.values()))] + [grads[n] for n in MID])
            small_pending[l], token = begun[:4], begun[4]
        else:
            pending[l, group], token = _reduce_start(f"{l}_{group}", grads)
        tokens[l, group] = token
        return token[0, 0]

    loss, dx, _, _, _ = _local_step(x[0], loss_target[0], weights_of, prm, on_grads)
    loss = lax.psum(loss, ("x", "y", "c"))

    reduced = [{} for _ in range(N_LAYERS)]
    out = {}

    def finish(l, group, after):
        reduced[l].update(_reduce_finish(pending[l, group], place, after))

    def adamw(l, names, done):
        for n in names:
            out[n] = _adamw_big(n, l, kernel_view(n, prm[n]), kernel_view(n, mom[n]), kernel_view(n, var[n]),
                                reduced[l][n], out.get(n), after=done)
            done = (out[n][0],)
        return done

    top = N_LAYERS - 1
    done = (tokens[0, "in"], tokens[0, "small"])
    for group in ("ffn", "mixer", "in"):
        finish(top, group, done)
    done = adamw(top, BIG, done)
    for group in ("ffn", "mixer", "in"):
        finish(0, group, done)
        done = adamw(0, [n for n in GATHER_GROUPS[group] if n in BIG], done)
    for n in BIG:
        out[n] = tuple(kernel_view(n, a) for a in out[n])

    gsmall = {}
    mid_mine, mid_gathered = [], []
    for l in range(N_LAYERS):
        mine, lands = _allgather_rows_wait(f"l{l}", *small_pending[l], done[0])
        lands = _allgather_rows_forward(f"l{l}", lands)
        mid_mine.append(mine[1:])
        mid_gathered.append(lands[1:])
        gsum = _sum_devices(f"l{l}", lands[0], mine[0], place)
        for n, g in zip(small_shapes[l], _unpack_rows(gsum, list(small_shapes[l].values()))):
            gsmall.setdefault(n, [None] * N_LAYERS)[l] = g
    for i, n in enumerate(MID):
        out[n] = tuple(kernel_view(n, a) for a in _adamw_mid(
            n, prm[n], mom[n], var[n], [mid_gathered[l][i] for l in range(N_LAYERS)],
            [mid_mine[l][i] for l in range(N_LAYERS)], place))
    gsmall = {n: (g[top] if n == "final_norm" else jnp.stack(g)) for n, g in gsmall.items()}
    lanes = dw_shard.shape[-1]
    gsmall["conv_w_dw"] = lax.dynamic_slice_in_dim(gsmall["conv_w_dw"], (2 * cx + cy) * lanes, lanes, axis=2)
    small_names = [n for n in SMALL if n not in MID] + ["conv_w_dw"]
    w_rows = _pack_rows([prm[n] for n in small_names])
    m_rows = _pack_rows([mom[n] for n in small_names])
    v_rows = _pack_rows([var[n] for n in small_names])
    g_rows = _pack_rows([gsmall[n] for n in small_names])
    shapes = [prm[n].shape for n in small_names]
    d_s, m_s, v_s = (_unpack_rows(r, shapes) for r in _adamw_rows(w_rows, m_rows, v_rows, g_rows))
    for i, n in enumerate(small_names):
        out[n] = (gsmall[n].reshape(prm[n].shape), d_s[i], m_s[i], v_s[i])
    grads = [out[n][0] for n in WEIGHTS]
    deltas = [out[n][1] for n in WEIGHTS]
    new_m = [out[n][2] for n in WEIGHTS]
    new_v = [out[n][3] for n in WEIGHTS]
    return (loss, dx[None], *grads, *deltas, *new_m, *new_v)
```

```python
import functools
import math

import jax
import jax.numpy as jnp
from jax import lax
from jax.experimental import pallas as pl
from jax.experimental.pallas import tpu as pltpu

F32 = jnp.float32
MXU_DTYPE = jnp.bfloat16
WIRE_DTYPE = jnp.bfloat16
SDS = jax.ShapeDtypeStruct
BS = pl.BlockSpec
ANY = pl.BlockSpec(memory_space=pl.ANY)
HBM = pl.BlockSpec(memory_space=pltpu.HBM)
SEM = pl.BlockSpec(memory_space=pltpu.SEMAPHORE)
SIDE_EFFECT = pltpu.SideEffectType.DATAFLOW_SIDE_EFFECTING
MESH = pl.DeviceIdType.MESH

EPS = 1e-6
N_CHIPS = 4
N_LAYERS = 2
SSM_GROUPS, SSM_STATE, SSM_GROUP = 32, 64, 16
CONV_KERNEL = 31
CONV_PAD = 32
POOL_WINDOWS = (2, 4, 8, 16)
GELU_C = math.sqrt(2.0 / math.pi)
ADAM_LR, ADAM_B1, ADAM_B2, ADAM_EPS, ADAM_WD, ADAM_STEP = 0.001, 0.9, 0.999, 1e-08, 0.01, 10
VMEM_LIMIT = 56 * 1024 * 1024

BIG = ("w_in", "ssm_w_glu", "ssm_w_proj", "conv_w_proj", "pool_w_proj", "w_out", "ffn_w_gate", "ffn_w_up", "ffn_w_down")
TRANSPOSED = ("ffn_w_gate", "ffn_w_up")
MID = ("ssm_b_re", "ssm_b_im", "ssm_c_re", "ssm_c_im")
GATHER_GROUPS = {
    "in": ("w_in",),
    "mixer": ("ssm_w_glu", "ssm_w_proj", "conv_w_proj", "pool_w_proj", "w_out", "conv_w_dw"),
    "ffn": ("ffn_w_gate", "ffn_w_up", "ffn_w_down"),
}
SMALL = ("norm1", "b_gate", "ssm_a_re", "ssm_a_im", "ssm_log_dt", "ssm_b_re", "ssm_b_im", "ssm_c_re", "ssm_c_im",
         "ssm_d", "ssm_b_glu", "conv_b_dw", "conv_ln_g", "conv_ln_b", "pool_w_group", "pool_scale", "norm2",
         "final_norm")
WEIGHTS = ("norm1", "w_in", "b_gate", "ssm_a_re", "ssm_a_im", "ssm_log_dt", "ssm_b_re", "ssm_b_im", "ssm_c_re",
           "ssm_c_im", "ssm_d", "ssm_w_glu", "ssm_b_glu", "ssm_w_proj", "conv_w_dw", "conv_b_dw", "conv_ln_g",
           "conv_ln_b", "conv_w_proj", "pool_w_group", "pool_scale", "pool_w_proj", "w_out", "norm2", "ffn_w_gate",
           "ffn_w_up", "ffn_w_down", "final_norm")


def _params(vmem=True):
    return pltpu.CompilerParams(vmem_limit_bytes=VMEM_LIMIT) if vmem else None


def _mm(a, b):
    return jnp.dot(a.astype(MXU_DTYPE), b.astype(MXU_DTYPE), preferred_element_type=F32)


def _mm_nt(a, b):
    return lax.dot_general(a.astype(MXU_DTYPE), b.astype(MXU_DTYPE), (((1,), (1,)), ((), ())),
                           preferred_element_type=F32)


def _mm_tn(a, b):
    return lax.dot_general(a.astype(MXU_DTYPE), b.astype(MXU_DTYPE), (((0,), (0,)), ((), ())),
                           preferred_element_type=F32)


def _sigmoid(x):
    return jax.nn.sigmoid(x)


def _gelu(x):
    t = jnp.tanh(GELU_C * (x + 0.044715 * (x * x * x)))
    return x * (0.5 * (1.0 + t)), t


def _gelu_grad(x, t):
    return 0.5 * (1.0 + t) + 0.5 * x * (1.0 - t * t) * (GELU_C * (1.0 + 3.0 * 0.044715 * x * x))


def _colsum(v):
    return jnp.sum(v, axis=0, keepdims=True)


def _row_tile(rows, cols, itemsize=4, budget=1536 * 1024):
    best = None
    for t in range(8, rows + 1, 8):
        if rows % t == 0 and t * cols * itemsize <= budget:
            best = t
    return best if best is not None else rows


def _in_proj(l, x, norm1, w_in):
    s, d = x.shape
    nc = w_in.shape[-1]
    tm = min(512, s)
    nt = s // tm

    def body(x_ref, g_ref, w_ref, z_ref, h_ref, h_all):
        i = pl.program_id(1)
        rows = pl.ds(pl.multiple_of(i * tm, tm), tm)

        @pl.when(pl.program_id(0) == 0)
        def _():
            xv = x_ref[...]
            r = lax.rsqrt(jnp.mean(xv * xv, axis=-1, keepdims=True) + EPS)
            hv = (xv * r * g_ref[...]).astype(h_ref.dtype)
            h_ref[...] = hv
            h_all[rows, :] = hv

        z_ref[...] = _mm(h_all[rows, :], w_ref[...])

    first_pass = lambda j, i: (i * (1 - jnp.minimum(j, 1)) + (nt - 1) * jnp.minimum(j, 1), 0)
    return pl.pallas_call(
        body, name=f"in_proj_l{l}", grid=(N_CHIPS, nt),
        in_specs=[BS((tm, d), first_pass), BS((None, 1, d), lambda j, i: (l, 0, 0)),
                  BS((None, d, nc), lambda j, i: (j, 0, 0))],
        out_specs=[BS((tm, nc), lambda j, i: (i, j)), BS((tm, d), first_pass)],
        out_shape=[SDS((s, N_CHIPS * nc), F32), SDS((s, d), MXU_DTYPE)],
        scratch_shapes=[pltpu.VMEM((s, d), MXU_DTYPE)], compiler_params=_params())(x, norm1, w_in)


def _mm_cols(a, w_ref):
    return jnp.concatenate([_mm(a, w_ref[j]) for j in range(N_CHIPS)], axis=1)


def _mm_nt_cols(dv, w_ref):
    nc = w_ref.shape[-1]
    acc = _mm_nt(dv[:, 0:nc], w_ref[0])
    for j in range(1, N_CHIPS):
        acc = acc + _mm_nt(dv[:, j * nc:(j + 1) * nc], w_ref[j])
    return acc


def _merge_values(y, hc, p, zg, wglu, bglu, wpa, wpb, wpc, lng, lnb, wgrp, scale, bg):
    v = {}
    ge, th = _gelu(y)
    t = _mm(ge, wglu) + bglu
    sg = _sigmoid(t)
    sa = ge * sg
    ya = _mm_cols(sa, wpa)
    mu = jnp.mean(hc, axis=-1, keepdims=True)
    xc = hc - mu
    r = lax.rsqrt(jnp.mean(xc * xc, axis=-1, keepdims=True) + EPS)
    xh = xc * r
    ln = xh * lng + lnb
    sl = _sigmoid(ln)
    ac = ln * sl
    yb = _mm_cols(ac, wpb)
    gw = p.shape[1] // len(POOL_WINDOWS)
    q = jnp.concatenate([_mm(p[:, k * gw:(k + 1) * gw], wgrp[k]) for k in range(len(POOL_WINDOWS))], axis=1)
    pp = q * scale
    yc = _mm_cols(pp, wpc)
    d = ya.shape[1]
    gates = [_sigmoid(zg[k] + bg[:, k * d:(k + 1) * d]) for k in range(3)]
    merged = gates[0] * ya + gates[1] * yb + gates[2] * yc
    v.update(ge=ge, th=th, sg=sg, sa=sa, ya=ya, r=r, xh=xh, ln=ln, sl=sl, ac=ac, yb=yb, q=q, pp=pp, yc=yc,
             gates=gates, merged=merged)
    return v


def _merge_specs(l, tm, d, cw):
    row = lambda n: BS((None, 1, n), lambda i: (l, 0, 0))
    return [
        BS((tm, cw), lambda i: (i, 0)),
        BS((tm, cw), lambda i: (i, 0)),
        BS((tm, cw), lambda i: (i, 0)),
        BS((tm, d), lambda i: (i, 2)), BS((tm, d), lambda i: (i, 3)), BS((tm, d), lambda i: (i, 4)),
        BS((N_CHIPS, cw // N_CHIPS, cw), lambda i: (0, 0, 0)),
        row(cw),
        BS((N_CHIPS, cw, d // N_CHIPS), lambda i: (0, 0, 0)),
        BS((N_CHIPS, cw, d // N_CHIPS), lambda i: (0, 0, 0)),
        BS((N_CHIPS, cw, d // N_CHIPS), lambda i: (0, 0, 0)),
        row(cw), row(cw),
        BS((None, 4, cw // 4, cw // 4), lambda i: (l, 0, 0, 0)),
        row(cw),
        row(3 * d),
        BS((N_CHIPS, d // N_CHIPS, d), lambda i: (0, 0, 0)),
    ]


def _merge_fwd(l, x, y, hc, p, z, fw, sp):
    s, d = x.shape
    cw = y.shape[1]
    tm = min(256, s)

    def body(x_ref, y_ref, hc_ref, p_ref, z0, z1, z2, wglu, bglu, wpa, wpb, wpc, lng, lnb, wgrp, scale, bg, wout,
             x1_ref):
        v = _merge_values(y_ref[...], hc_ref[...], p_ref[...], (z0[...], z1[...], z2[...]),
                          wglu[...].reshape(cw, cw), bglu[...], wpa, wpb, wpc, lng[...], lnb[...], wgrp, scale[...],
                          bg[...])
        x1_ref[...] = x_ref[...] + _mm(v["merged"], wout[...].reshape(d, d))

    return pl.pallas_call(
        body, name=f"merge_fwd_l{l}", grid=(s // tm,),
        in_specs=[BS((tm, d), lambda i: (i, 0))] + _merge_specs(l, tm, d, cw),
        out_specs=BS((tm, d), lambda i: (i, 0)), out_shape=SDS((s, d), F32), compiler_params=_params(),
    )(x, y, hc, p, z, z, z, fw["ssm_w_glu"], sp["ssm_b_glu"], fw["ssm_w_proj"], fw["conv_w_proj"], fw["pool_w_proj"],
      sp["conv_ln_g"], sp["conv_ln_b"], sp["pool_w_group"], sp["pool_scale"], sp["b_gate"], fw["w_out"])


def _merge_bwd(l, dx1, y, hc, p, z, fw, sp):
    s, d = dx1.shape
    cw = y.shape[1]
    tm = min(256, s)
    m = MXU_DTYPE

    def body(dx1_ref, y_ref, hc_ref, p_ref, z0, z1, z2, wglu, bglu, wpa, wpb, wpc, lng, lnb, wgrp, scale, bg, wout,
             dzg_ref, dy_ref, dhc_ref, dp_ref, merged_ref, sa_ref, ac_ref, pp_ref, ge_ref, dt_ref, dya_ref, dyb_ref,
             dyc_ref, dq_ref, dbg_ref, dbglu_ref, dlng_ref, dlnb_ref, dscale_ref):
        yv = y_ref[...]
        wg = wglu[...].reshape(cw, cw)
        v = _merge_values(yv, hc_ref[...], p_ref[...], (z0[...], z1[...], z2[...]), wg, bglu[...], wpa, wpb, wpc,
                          lng[...], lnb[...], wgrp, scale[...], bg[...])
        dm = _mm_nt(dx1_ref[...], wout[...].reshape(d, d))
        ys = (v["ya"], v["yb"], v["yc"])
        dys, dbg = [], []
        for k in range(3):
            gk = v["gates"][k]
            dzk = dm * ys[k] * (gk * (1.0 - gk))
            dbg.append(_colsum(dzk))
            dzg_ref[:, k * d:(k + 1) * d] = dzk.astype(m)
            dys.append((dm * gk).astype(m))
        dsa = _mm_nt_cols(dys[0], wpa)
        dac = _mm_nt_cols(dys[1], wpb)
        dpp = _mm_nt_cols(dys[2], wpc)
        ge, sg = v["ge"], v["sg"]
        dt = dsa * ge * (sg * (1.0 - sg))
        dge = dsa * sg + _mm_nt(dt, wg)
        dy_ref[...] = dge * _gelu_grad(yv, v["th"])
        ln, sl, xh = v["ln"], v["sl"], v["xh"]
        dln = dac * (sl * (1.0 + ln * (1.0 - sl)))
        dxh = dln * lng[...]
        dhc_ref[...] = v["r"] * (dxh - jnp.mean(dxh, axis=-1, keepdims=True)
                                 - xh * jnp.mean(dxh * xh, axis=-1, keepdims=True))
        dq = dpp * scale[...]
        gw = cw // len(POOL_WINDOWS)
        for k in range(len(POOL_WINDOWS)):
            dp_ref[:, k * gw:(k + 1) * gw] = _mm_nt(dq[:, k * gw:(k + 1) * gw], wgrp[k])
        merged_ref[...] = v["merged"].astype(m)
        sa_ref[...] = v["sa"].astype(m)
        ac_ref[...] = v["ac"].astype(m)
        pp_ref[...] = v["pp"].astype(m)
        ge_ref[...] = ge.astype(m)
        dt_ref[...] = dt.astype(m)
        dya_ref[...] = dys[0]
        dyb_ref[...] = dys[1]
        dyc_ref[...] = dys[2]
        dq_ref[...] = dq.astype(m)

        @pl.when(pl.program_id(0) == 0)
        def _():
            for ref in (dbg_ref, dbglu_ref, dlng_ref, dlnb_ref, dscale_ref):
                ref[...] = jnp.zeros(ref.shape, F32)

        dbg_ref[...] += jnp.concatenate(dbg, axis=1)
        dbglu_ref[...] += _colsum(dt)
        dlng_ref[...] += _colsum(dln * xh)
        dlnb_ref[...] += _colsum(dln)
        dscale_ref[...] += _colsum(dpp * v["q"])

    tile = lambda n: BS((tm, n), lambda i: (i, 0))
    acc = lambda n: BS((1, n), lambda i: (0, 0))
    outs = pl.pallas_call(
        body, name=f"merge_bwd_l{l}", grid=(s // tm,),
        in_specs=[tile(d)] + _merge_specs(l, tm, d, cw),
        out_specs=[tile(3 * d), tile(cw), tile(cw), tile(cw), tile(d), tile(cw), tile(cw), tile(cw), tile(cw), tile(cw),
                   tile(d), tile(d), tile(d), tile(cw), acc(3 * d), acc(cw), acc(cw), acc(cw), acc(cw)],
        out_shape=[SDS((s, 3 * d), m), SDS((s, cw), F32), SDS((s, cw), F32), SDS((s, cw), F32), SDS((s, d), m),
                   SDS((s, cw), m), SDS((s, cw), m), SDS((s, cw), m), SDS((s, cw), m), SDS((s, cw), m), SDS((s, d), m),
                   SDS((s, d), m), SDS((s, d), m), SDS((s, cw), m), SDS((1, 3 * d), F32), SDS((1, cw), F32),
                   SDS((1, cw), F32), SDS((1, cw), F32), SDS((1, cw), F32)],
        compiler_params=_params(),
    )(dx1, y, hc, p, z, z, z, fw["ssm_w_glu"], sp["ssm_b_glu"], fw["ssm_w_proj"], fw["conv_w_proj"], fw["pool_w_proj"],
      sp["conv_ln_g"], sp["conv_ln_b"], sp["pool_w_group"], sp["pool_scale"], sp["b_gate"], fw["w_out"])
    names = ("dzg", "dy", "dhc", "dp", "merged", "sa", "ac", "pp", "ge", "dt", "dya", "dyb", "dyc", "dq", "db_gate",
             "db_glu", "dln_g", "dln_b", "dscale")
    return dict(zip(names, outs))


def _ffn_fwd(l, x1, norm2, wg, wu, wd):
    s, d = x1.shape
    hc = wd.shape[1]
    tm = min(1024, s)

    def body(x_ref, g_ref, wg_ref, wu_ref, wd_ref, o_ref, h_scr):
        @pl.when(pl.program_id(1) == 0)
        def _():
            xv = x_ref[...]
            r = lax.rsqrt(jnp.mean(xv * xv, axis=-1, keepdims=True) + EPS)
            h_scr[...] = (xv * r * g_ref[...]).astype(h_scr.dtype)
            o_ref[...] = xv

        h = h_scr[...]
        gate = _mm_nt(h, wg_ref[...])
        up = _mm_nt(h, wu_ref[...])
        o_ref[...] += _mm(gate * _sigmoid(gate) * up, wd_ref[...])

    return pl.pallas_call(
        body, name=f"ffn_fwd_l{l}", grid=(s // tm, N_CHIPS),
        in_specs=[BS((tm, d), lambda i, j: (i, 0)), BS((None, 1, d), lambda i, j: (l, 0, 0)),
                  BS((None, hc, d), lambda i, j: (j, 0, 0)), BS((None, hc, d), lambda i, j: (j, 0, 0)),
                  BS((None, hc, d), lambda i, j: (j, 0, 0))],
        out_specs=BS((tm, d), lambda i, j: (i, 0)), out_shape=SDS((s, d), F32),
        scratch_shapes=[pltpu.VMEM((tm, d), MXU_DTYPE)], compiler_params=_params())(x1, norm2, wg, wu, wd)


def _ffn_bwd(l, x1, dx2, norm2, wg, wu, wd):
    s, d = x1.shape
    hc = wd.shape[1]
    tm = min(512, s)
    m = MXU_DTYPE
    last = N_CHIPS - 1

    def body(x_ref, dx2_ref, g_ref, wg_ref, wu_ref, wd_ref, dx1_ref, h_ref, act_ref, dgate_ref, dup_ref, dn_ref,
             dh_scr, dxb_scr):
        i, j = pl.program_id(0), pl.program_id(1)

        @pl.when(j == 0)
        def _():
            xv = x_ref[...]
            r = lax.rsqrt(jnp.mean(xv * xv, axis=-1, keepdims=True) + EPS)
            h_ref[...] = (xv * r * g_ref[...]).astype(m)
            dxb_scr[...] = dx2_ref[...].astype(m)
            dh_scr[...] = jnp.zeros(dh_scr.shape, F32)

        @pl.when((i == 0) & (j == 0))
        def _():
            dn_ref[...] = jnp.zeros(dn_ref.shape, F32)

        h = h_ref[...]
        gate = _mm_nt(h, wg_ref[...])
        up = _mm_nt(h, wu_ref[...])
        sg = _sigmoid(gate)
        silu = gate * sg
        act_ref[...] = (silu * up).astype(m)
        dact = _mm_nt(dxb_scr[...], wd_ref[...])
        dup = (dact * silu).astype(m)
        dgate = (dact * up * (sg * (1.0 + gate * (1.0 - sg)))).astype(m)
        dup_ref[...] = dup
        dgate_ref[...] = dgate
        dh_scr[...] += _mm(dgate, wg_ref[...]) + _mm(dup, wu_ref[...])

        @pl.when(j == last)
        def _():
            xv = x_ref[...]
            r = lax.rsqrt(jnp.mean(xv * xv, axis=-1, keepdims=True) + EPS)
            xh = xv * r
            dh = dh_scr[...]
            dn_ref[...] += _colsum(dh * xh)
            dxh = dh * g_ref[...]
            dx1_ref[...] = dx2_ref[...] + r * (dxh - xh * jnp.mean(dxh * xh, axis=-1, keepdims=True))

    chunk = BS((None, tm, hc), lambda i, j: (j, i, 0))
    outs = pl.pallas_call(
        body, name=f"ffn_bwd_l{l}", grid=(s // tm, N_CHIPS),
        in_specs=[BS((tm, d), lambda i, j: (i, 0)), BS((tm, d), lambda i, j: (i, 0)),
                  BS((None, 1, d), lambda i, j: (l, 0, 0)),
                  BS((None, hc, d), lambda i, j: (j, 0, 0)), BS((None, hc, d), lambda i, j: (j, 0, 0)),
                  BS((None, hc, d), lambda i, j: (j, 0, 0))],
        out_specs=[BS((tm, d), lambda i, j: (i, 0)), BS((tm, d), lambda i, j: (i, 0)), chunk, chunk, chunk,
                   BS((1, d), lambda i, j: (0, 0))],
        out_shape=[SDS((s, d), F32), SDS((s, d), m), SDS((N_CHIPS, s, hc), m), SDS((N_CHIPS, s, hc), m),
                   SDS((N_CHIPS, s, hc), m), SDS((1, d), F32)],
        scratch_shapes=[pltpu.VMEM((tm, d), F32), pltpu.VMEM((tm, d), m)], compiler_params=_params(),
    )(x1, dx2, norm2, wg, wu, wd)
    return dict(zip(("dx1", "h2", "act", "dgate", "dup", "dnorm2"), outs))


def _loss_head(x, target, gf):
    s, d = x.shape
    tm = min(512, s)

    def body(x_ref, t_ref, g_ref, dx_ref, loss_ref, dg_ref):
        @pl.when(pl.program_id(0) == 0)
        def _():
            loss_ref[...] = jnp.zeros(loss_ref.shape, F32)
            dg_ref[...] = jnp.zeros(dg_ref.shape, F32)

        xv = x_ref[...]
        r = lax.rsqrt(jnp.mean(xv * xv, axis=-1, keepdims=True) + EPS)
        xh = xv * r
        err = xh * g_ref[...] - t_ref[...]
        loss_ref[...] += 0.5 * jnp.sum(jnp.mean(err * err, axis=-1, keepdims=True), axis=0, keepdims=True)
        dyv = err * (1.0 / d)
        dg_ref[...] += _colsum(dyv * xh)
        dxh = dyv * g_ref[...]
        dx_ref[...] = r * (dxh - xh * jnp.mean(dxh * xh, axis=-1, keepdims=True))

    return pl.pallas_call(
        body, name="loss_head", grid=(s // tm,),
        in_specs=[BS((tm, d), lambda i: (i, 0)), BS((tm, d), lambda i: (i, 0)), BS((1, d), lambda i: (0, 0))],
        out_specs=[BS((tm, d), lambda i: (i, 0)), BS((1, 1), lambda i: (0, 0)), BS((1, d), lambda i: (0, 0))],
        out_shape=[SDS((s, d), F32), SDS((1, 1), F32), SDS((1, d), F32)], compiler_params=_params())(x, target, gf)


def _in_proj_bwd(l, dres, x, norm1, w_in, du_a, dv1, dv2, du_c, dzg):
    s, d = x.shape
    nc = w_in.shape[-1]
    tm = min(256, s)
    m = MXU_DTYPE

    def body(dres_ref, x_ref, g_ref, w_ref, a_ref, b1_ref, b2_ref, c_ref, g3_ref, dx_ref, dz_ref, dn_ref):
        @pl.when(pl.program_id(0) == 0)
        def _():
            dn_ref[...] = jnp.zeros(dn_ref.shape, F32)

        dz = jnp.concatenate([a_ref[...], b1_ref[...], b2_ref[...], c_ref[...], g3_ref[...]], axis=1).astype(m)
        dz_ref[...] = dz
        dh = _mm_nt_cols(dz, w_ref)
        xv = x_ref[...]
        r = lax.rsqrt(jnp.mean(xv * xv, axis=-1, keepdims=True) + EPS)
        xh = xv * r
        dn_ref[...] += _colsum(dh * xh)
        dxh = dh * g_ref[...]
        dx_ref[...] = dres_ref[...] + r * (dxh - xh * jnp.mean(dxh * xh, axis=-1, keepdims=True))

    tile = lambda n: BS((tm, n), lambda i: (i, 0))
    return pl.pallas_call(
        body, name=f"in_proj_bwd_l{l}", grid=(s // tm,),
        in_specs=[tile(d), tile(d), BS((None, 1, d), lambda i: (l, 0, 0)),
                  BS((N_CHIPS, d, nc), lambda i: (0, 0, 0)),
                  tile(du_a.shape[1]), tile(dv1.shape[1]), tile(dv2.shape[1]), tile(du_c.shape[1]), tile(dzg.shape[1])],
        out_specs=[tile(d), tile(N_CHIPS * nc), BS((1, d), lambda i: (0, 0))],
        out_shape=[SDS((s, d), F32), SDS((s, N_CHIPS * nc), m), SDS((1, d), F32)], compiler_params=_params(),
    )(dres, x, norm1, w_in, du_a, dv1, dv2, du_c, dzg)


def _tn_matmul(name, a, a_spec, b, b_spec, chunk_shape, grid, place):
    last = grid[1] - 1

    def body(place_ref, a_ref, b_ref, own_ref, wire_ref, *acc):
        part = _mm_tn(a_ref[...], b_ref[...])

        def emit(total):
            wire_ref[...] = total.astype(WIRE_DTYPE)

            @pl.when(pl.program_id(0) == place_ref[0])
            def _():
                own_ref[...] = total

        if last == 0:
            emit(part)
        else:
            @pl.when(pl.program_id(1) == 0)
            def _():
                acc[0][...] = part

            @pl.when(pl.program_id(1) > 0)
            def _():
                acc[0][...] += part

            @pl.when(pl.program_id(1) == last)
            def _():
                emit(acc[0][...])

    zeros = (0,) * len(chunk_shape)
    return pl.pallas_call(
        body, name=name,
        grid_spec=pltpu.PrefetchScalarGridSpec(
            num_scalar_prefetch=1, grid=grid, in_specs=[a_spec, b_spec],
            out_specs=[BS(chunk_shape, lambda j, t, pr: zeros), BS((None,) + chunk_shape, lambda j, t, pr: (j,) + zeros)],
            scratch_shapes=[pltpu.VMEM(chunk_shape, F32)] if last else []),
        out_shape=[SDS(chunk_shape, F32), SDS((N_CHIPS,) + chunk_shape, WIRE_DTYPE)],
        compiler_params=_params())(place, a, b)


def _scan_consts(pw_ref, lanes, reverse):
    sgn = -1.0 if reverse else 1.0
    steps = [(k, pw_ref[2 * i], sgn * pw_ref[2 * i + 1]) for i, k in enumerate((1, 2, 4))]
    c = 4 if reverse else 3
    return steps, pw_ref[2 * c], sgn * pw_ref[2 * c + 1]


def _scan_block(br, bi, steps, row, reverse):
    for k, ar, ai in steps:
        if reverse:
            mask, sh = row < 8 - k, 8 - k
        else:
            mask, sh = row >= k, k
        sr = jnp.where(mask, pltpu.roll(br, sh, 0), 0.0)
        si = jnp.where(mask, pltpu.roll(bi, sh, 0), 0.0)
        br, bi = br + ar * sr - ai * si, bi + ar * si + ai * sr
    return br, bi


def _ssm_fwd(l, z, bblk_re, bblk_im, cblk_re, cblk_im, pw, dskip):
    s = z.shape[0]
    gc = bblk_re.shape[1]
    gl = bblk_re.shape[2]
    nblk = bblk_re.shape[0]

    def body(u_ref, bre, bim, cre, cim, pw_ref, d_ref, hre, him, y_ref):
        u = u_ref[...]
        hre[...] = _mm(u, bre[...])
        him[...] = _mm(u, bim[...])
        row = lax.broadcasted_iota(jnp.int32, (8, gl), 0)
        steps, car, cai = _scan_consts(pw_ref, gl, False)

        def step(i, carry):
            cr, ci = carry
            r0 = pl.multiple_of(i * 8, 8)
            br, bi = _scan_block(hre[pl.ds(r0, 8), :], him[pl.ds(r0, 8), :], steps, row, False)
            hr = br + car * cr - cai * ci
            hi = bi + car * ci + cai * cr
            hre[pl.ds(r0, 8), :] = hr
            him[pl.ds(r0, 8), :] = hi
            return jnp.broadcast_to(hr[7:8, :], (8, gl)), jnp.broadcast_to(hi[7:8, :], (8, gl))

        zero = jnp.zeros((8, gl), F32)
        lax.fori_loop(0, s // 8, step, (zero, zero))
        y_ref[...] = _mm_nt(hre[...], cre[...]) - _mm_nt(him[...], cim[...]) + d_ref[...] * u

    return pl.pallas_call(
        body, name=f"ssm_fwd_l{l}", grid=(nblk,),
        in_specs=[BS((s, gc), lambda k: (0, k)), BS((None, gc, gl), lambda k: (k, 0, 0)),
                  BS((None, gc, gl), lambda k: (k, 0, 0)), BS((None, gc, gl), lambda k: (k, 0, 0)),
                  BS((None, gc, gl), lambda k: (k, 0, 0)), BS((10, 8, gl), lambda k: (0, 0, k)),
                  BS((1, gc), lambda k: (0, k))],
        out_specs=[BS((s, gl), lambda k: (0, k)), BS((s, gl), lambda k: (0, k)), BS((s, gc), lambda k: (0, k))],
        out_shape=[SDS((s, nblk * gl), F32), SDS((s, nblk * gl), F32), SDS((s, nblk * gc), F32)],
        compiler_params=_params())(z, bblk_re, bblk_im, cblk_re, cblk_im, pw, dskip)


def _ssm_bwd(l, dy, z, hre, him, bblk_re, bblk_im, cblk_re, cblk_im, pw, dskip):
    s = z.shape[0]
    nblk, gc, gl = bblk_re.shape

    def body(dy_ref, u_ref, hre_ref, him_ref, bre, bim, cre, cim, pw_ref, d_ref,
             du_ref, dbre_ref, dbim_ref, dcre_ref, dcim_ref, dar_ref, dai_ref, dd_ref, gre, gim):
        dyv = dy_ref[...]
        u = u_ref[...]
        gre[...] = _mm(dyv, cre[...])
        gim[...] = -_mm(dyv, cim[...])
        dcre_ref[...] = _mm_tn(dyv, hre_ref[...])
        dcim_ref[...] = -_mm_tn(dyv, him_ref[...])
        dd_ref[...] = _colsum(dyv * u)
        row = lax.broadcasted_iota(jnp.int32, (8, gl), 0)
        steps, car, cai = _scan_consts(pw_ref, gl, True)
        n8 = s // 8

        def step(ii, carry):
            cr, ci, accr, acci = carry
            i = n8 - 1 - ii
            r0 = pl.multiple_of(i * 8, 8)
            br, bi = _scan_block(gre[pl.ds(r0, 8), :], gim[pl.ds(r0, 8), :], steps, row, True)
            dr = br + car * cr - cai * ci
            di = bi + car * ci + cai * cr
            gre[pl.ds(r0, 8), :] = dr
            gim[pl.ds(r0, 8), :] = di
            rp = pl.multiple_of(jnp.maximum(i - 1, 0) * 8, 8)
            keep = jnp.where(i > 0, 1.0, 0.0)
            pr = jnp.where(row >= 1, pltpu.roll(hre_ref[pl.ds(r0, 8), :], 1, 0),
                           keep * pltpu.roll(hre_ref[pl.ds(rp, 8), :], 1, 0))
            pi = jnp.where(row >= 1, pltpu.roll(him_ref[pl.ds(r0, 8), :], 1, 0),
                           keep * pltpu.roll(him_ref[pl.ds(rp, 8), :], 1, 0))
            accr = accr + dr * pr + di * pi
            acci = acci + di * pr - dr * pi
            return (jnp.broadcast_to(dr[0:1, :], (8, gl)), jnp.broadcast_to(di[0:1, :], (8, gl)), accr, acci)

        zero = jnp.zeros((8, gl), F32)
        _, _, accr, acci = lax.fori_loop(0, n8, step, (zero, zero, zero, zero))
        dar_ref[...] = _colsum(accr)
        dai_ref[...] = _colsum(acci)
        dbr = gre[...]
        dbi = gim[...]
        du_ref[...] = (dyv * d_ref[...] + _mm_nt(dbr, bre[...]) + _mm_nt(dbi, bim[...])).astype(du_ref.dtype)
        dbre_ref[...] = _mm_tn(u, dbr)
        dbim_ref[...] = _mm_tn(u, dbi)

    col = lambda n: BS((s, n), lambda k: (0, k))
    blk = lambda a, b: BS((None, a, b), lambda k: (k, 0, 0))
    outs = pl.pallas_call(
        body, name=f"ssm_bwd_l{l}", grid=(nblk,),
        in_specs=[col(gc), col(gc), col(gl), col(gl), blk(gc, gl), blk(gc, gl), blk(gc, gl), blk(gc, gl),
                  BS((10, 8, gl), lambda k: (0, 0, k)), BS((1, gc), lambda k: (0, k))],
        out_specs=[col(gc), blk(gc, gl), blk(gc, gl), blk(gc, gl), blk(gc, gl), BS((1, gl), lambda k: (0, k)),
                   BS((1, gl), lambda k: (0, k)), BS((1, gc), lambda k: (0, k))],
        out_shape=[SDS((s, nblk * gc), MXU_DTYPE), SDS((nblk, gc, gl), F32), SDS((nblk, gc, gl), F32),
                   SDS((nblk, gc, gl), F32), SDS((nblk, gc, gl), F32), SDS((1, nblk * gl), F32),
                   SDS((1, nblk * gl), F32), SDS((1, nblk * gc), F32)],
        scratch_shapes=[pltpu.VMEM((s, gl), F32), pltpu.VMEM((s, gl), F32)], compiler_params=_params(),
    )(dy, z, hre, him, bblk_re, bblk_im, cblk_re, cblk_im, pw, dskip)
    return dict(zip(("du", "dbblk_re", "dbblk_im", "dcblk_re", "dcblk_im", "dabar_re", "dabar_im", "dd"), outs))


def _conv_fwd(l, z, wdw, bdw):
    s = z.shape[0]
    cw = wdw.shape[1]
    lb = 128
    tr = min(256, s)
    off1 = cw // lb
    off2 = 2 * cw // lb

    def body(v1_ref, v2_ref, w_ref, b_ref, hc_ref, scr):
        scr[0:CONV_PAD, :] = jnp.zeros((CONV_PAD, lb), F32)
        scr[CONV_PAD:, :] = v1_ref[...] * _sigmoid(v2_ref[...])
        for t in range(s // tr):
            acc = jnp.broadcast_to(b_ref[...], (tr, lb))
            for k in range(CONV_KERNEL):
                acc = acc + w_ref[pl.ds(k, 1), :] * scr[pl.ds(t * tr + CONV_PAD - (CONV_KERNEL - 1) + k, tr), :]
            hc_ref[pl.ds(t * tr, tr), :] = acc

    return pl.pallas_call(
        body, name=f"conv_fwd_l{l}", grid=(cw // lb,),
        in_specs=[BS((s, lb), lambda k: (0, off1 + k)), BS((s, lb), lambda k: (0, off2 + k)),
                  BS((CONV_KERNEL, lb), lambda k: (0, k)), BS((1, lb), lambda k: (0, k))],
        out_specs=BS((s, lb), lambda k: (0, k)), out_shape=SDS((s, cw), F32),
        scratch_shapes=[pltpu.VMEM((s + CONV_PAD, lb), F32)], compiler_params=_params())(z, z, wdw, bdw)


def _conv_bwd(l, dhc, z, wdw):
    s = z.shape[0]
    cw = wdw.shape[1]
    lb = 128
    tr = min(256, s)
    off1 = cw // lb
    off2 = 2 * cw // lb
    nb = cw // lb

    def body(d_ref, v1_ref, v2_ref, w_ref, dv1_ref, dv2_ref, dw_ref, db_ref, hpad, dpad):
        v1 = v1_ref[...]
        sg = _sigmoid(v2_ref[...])
        dv = d_ref[...]
        hpad[0:CONV_PAD, :] = jnp.zeros((CONV_PAD, lb), F32)
        hpad[CONV_PAD:, :] = v1 * sg
        dpad[0:s, :] = dv
        dpad[s:, :] = jnp.zeros((CONV_PAD, lb), F32)
        db_ref[...] = _colsum(dv)
        dws = [jnp.zeros((1, lb), F32) for _ in range(CONV_KERNEL)]
        for t in range(s // tr):
            dt = d_ref[pl.ds(t * tr, tr), :]
            acc = jnp.zeros((tr, lb), F32)
            for k in range(CONV_KERNEL):
                acc = acc + w_ref[pl.ds(k, 1), :] * dpad[pl.ds(t * tr + (CONV_KERNEL - 1) - k, tr), :]
                dws[k] = dws[k] + _colsum(dt * hpad[pl.ds(t * tr + CONV_PAD - (CONV_KERNEL - 1) + k, tr), :])
            sgt = _sigmoid(v2_ref[pl.ds(t * tr, tr), :])
            v1t = v1_ref[pl.ds(t * tr, tr), :]
            dv1_ref[pl.ds(t * tr, tr), :] = (acc * sgt).astype(dv1_ref.dtype)
            dv2_ref[pl.ds(t * tr, tr), :] = (acc * v1t * (sgt * (1.0 - sgt))).astype(dv2_ref.dtype)
        for k in range(CONV_KERNEL):
            dw_ref[pl.ds(k, 1), :] = dws[k]

    return pl.pallas_call(
        body, name=f"conv_bwd_l{l}", grid=(nb,),
        in_specs=[BS((s, lb), lambda k: (0, k)), BS((s, lb), lambda k: (0, off1 + k)),
                  BS((s, lb), lambda k: (0, off2 + k)), BS((CONV_KERNEL, lb), lambda k: (0, k))],
        out_specs=[BS((s, lb), lambda k: (0, k)), BS((s, lb), lambda k: (0, k)),
                   BS((CONV_KERNEL, lb), lambda k: (0, k)), BS((1, lb), lambda k: (0, k))],
        out_shape=[SDS((s, cw), MXU_DTYPE), SDS((s, cw), MXU_DTYPE), SDS((CONV_KERNEL, cw), F32), SDS((1, cw), F32)],
        scratch_shapes=[pltpu.VMEM((s + CONV_PAD, lb), F32), pltpu.VMEM((s + CONV_PAD, lb), F32)],
        compiler_params=_params())(dhc, z, z, wdw)


def _pool_window(k):
    return jnp.where(k == 0, float(POOL_WINDOWS[0]),
                     jnp.where(k == 1, float(POOL_WINDOWS[1]),
                               jnp.where(k == 2, float(POOL_WINDOWS[2]), float(POOL_WINDOWS[3]))))


def _pool_fwd(l, z, pw_width):
    s = z.shape[0]
    lb = pw_width // len(POOL_WINDOWS)
    off = 3 * pw_width // lb

    def body(u_ref, p_ref):
        k = pl.program_id(0)
        u = u_ref[...]
        row = lax.broadcasted_iota(jnp.int32, (s, lb), 0)
        sums = [u]
        for sh in (1, 2, 4, 8):
            prev = sums[-1]
            sums.append(prev + jnp.where(row >= sh, pltpu.roll(prev, sh, 0), 0.0))
        sel = jnp.where(k == 0, sums[1], jnp.where(k == 1, sums[2], jnp.where(k == 2, sums[3], sums[4])))
        cnt = jnp.minimum((row + 1).astype(F32), _pool_window(k))
        p_ref[...] = sel / cnt - u

    return pl.pallas_call(
        body, name=f"pool_fwd_l{l}", grid=(len(POOL_WINDOWS),),
        in_specs=[BS((s, lb), lambda k: (0, off + k))], out_specs=BS((s, lb), lambda k: (0, k)),
        out_shape=SDS((s, pw_width), F32), compiler_params=_params())(z)


def _pool_bwd(l, dp):
    s, width = dp.shape
    lb = width // len(POOL_WINDOWS)

    def body(d_ref, du_ref):
        k = pl.program_id(0)
        dv = d_ref[...]
        row = lax.broadcasted_iota(jnp.int32, (s, lb), 0)
        cnt = jnp.minimum((row + 1).astype(F32), _pool_window(k))
        sums = [dv / cnt]
        for sh in (1, 2, 4, 8):
            prev = sums[-1]
            sums.append(prev + jnp.where(row < s - sh, pltpu.roll(prev, s - sh, 0), 0.0))
        sel = jnp.where(k == 0, sums[1], jnp.where(k == 1, sums[2], jnp.where(k == 2, sums[3], sums[4])))
        du_ref[...] = (sel - dv).astype(du_ref.dtype)

    return pl.pallas_call(
        body, name=f"pool_bwd_l{l}", grid=(len(POOL_WINDOWS),),
        in_specs=[BS((s, lb), lambda k: (0, k))], out_specs=BS((s, lb), lambda k: (0, k)),
        out_shape=SDS((s, width), MXU_DTYPE), compiler_params=_params())(dp)


def _zoh(a_re, a_im, log_dt):
    dt = jnp.exp(log_dt)
    mag = jnp.exp(dt * a_re)
    ang = dt * a_im
    abar_re = mag * jnp.cos(ang)
    abar_im = mag * jnp.sin(ang)
    den = a_re * a_re + a_im * a_im
    nr = abar_re - 1.0
    ni = abar_im
    f_re = (nr * a_re + ni * a_im) / den
    f_im = (ni * a_re - nr * a_im) / den
    return abar_re, abar_im, f_re, f_im


def _zoh_fwd(l, a_re, a_im, log_dt):
    def body(ar, ai, ld, o0, o1, o2, o3):
        for ref, val in zip((o0, o1, o2, o3), _zoh(ar[...], ai[...], ld[...])):
            ref[...] = val

    return pl.pallas_call(body, name=f"zoh_fwd_l{l}", out_shape=[SDS(a_re.shape, F32)] * 4)(a_re, a_im, log_dt)


def _zoh_bwd(l, a_re, a_im, log_dt, cts):
    def body(ar, ai, ld, c0, c1, c2, c3, dar, dai, dld):
        _, vjp = jax.vjp(_zoh, ar[...], ai[...], ld[...])
        g = vjp((c0[...], c1[...], c2[...], c3[...]))
        dar[...] = g[0]
        dai[...] = g[1]
        dld[...] = g[2]

    return pl.pallas_call(body, name=f"zoh_bwd_l{l}",
                          out_shape=[SDS(a_re.shape, F32), SDS(a_re.shape, F32), SDS(log_dt.shape, F32)],
                          )(a_re, a_im, log_dt, *cts)


def _bbar_fwd(l, f_re, f_im, b_re, b_im):
    g, p, n = b_re.shape[1:]

    def body(fr, fi, br, bi, o_re, o_im):
        o_re[...] = (fr[...] * br[...] - fi[...] * bi[...]).astype(o_re.dtype)
        o_im[...] = (fr[...] * bi[...] + fi[...] * br[...]).astype(o_im.dtype)

    whole = lambda shp: BS(shp, lambda i: (0,) * len(shp))
    layer = BS((None, g, p, n), lambda i: (l, 0, 0, 0))
    return pl.pallas_call(body, name=f"bbar_fwd_l{l}", grid=(1,),
                          in_specs=[whole((g, 1, n)), whole((g, 1, n)), layer, layer],
                          out_specs=[whole((g, p, n))] * 2,
                          out_shape=[SDS((g, p, n), MXU_DTYPE)] * 2)(f_re, f_im, b_re, b_im)


def _bbar_bwd(l, f_re, f_im, b_re, b_im, d_re, d_im):
    g, p, n = b_re.shape[1:]

    def body(fr, fi, br, bi, dr, di, dfr, dfi, dbr, dbi):
        dfr[...] = jnp.sum(dr[...] * br[...] + di[...] * bi[...], axis=1, keepdims=True)
        dfi[...] = jnp.sum(di[...] * br[...] - dr[...] * bi[...], axis=1, keepdims=True)
        dbr[...] = fr[...] * dr[...] + fi[...] * di[...]
        dbi[...] = fr[...] * di[...] - fi[...] * dr[...]

    whole = lambda shp: BS(shp, lambda i: (0,) * len(shp))
    layer = BS((None, g, p, n), lambda i: (l, 0, 0, 0))
    return pl.pallas_call(body, name=f"bbar_bwd_l{l}", grid=(1,),
                          in_specs=[whole((g, 1, n)), whole((g, 1, n)), layer, layer, whole((g, p, n)),
                                    whole((g, p, n))],
                          out_specs=[whole((g, 1, n)), whole((g, 1, n)), whole((g, p, n)), whole((g, p, n))],
                          out_shape=[SDS((g, 1, n), F32), SDS((g, 1, n), F32), SDS((g, p, n), F32),
                                     SDS((g, p, n), F32)])(f_re, f_im, b_re, b_im, d_re, d_im)


def _powers(l, abar_re, abar_im):
    lanes = abar_re.shape[1]

    def body(ar_ref, ai_ref, o_ref):
        ar, ai = ar_ref[...], ai_ref[...]
        pows = [(ar, ai)]
        for _ in range(7):
            pr, pi = pows[-1]
            pows.append((pr * ar - pi * ai, pr * ai + pi * ar))
        row = lax.broadcasted_iota(jnp.int32, (8, lanes), 0)
        for i, k in enumerate((1, 2, 4)):
            o_ref[2 * i] = jnp.broadcast_to(pows[k - 1][0], (8, lanes))
            o_ref[2 * i + 1] = jnp.broadcast_to(pows[k - 1][1], (8, lanes))
        for slot, order in ((3, range(8)), (4, range(7, -1, -1))):
            vr = jnp.zeros((8, lanes), F32)
            vi = jnp.zeros((8, lanes), F32)
            for r, e in enumerate(order):
                vr = jnp.where(row == r, pows[e][0], vr)
                vi = jnp.where(row == r, pows[e][1], vi)
            o_ref[2 * slot] = vr
            o_ref[2 * slot + 1] = vi

    return pl.pallas_call(body, name=f"powers_l{l}", out_shape=SDS((10, 8, lanes), F32))(abar_re, abar_im)


def _block_diag(v):
    g, a, b = v.shape
    eye = jnp.eye(8, dtype=v.dtype)
    out = jnp.einsum("kgab,gh->kgahb", v.reshape(g // 8, 8, a, b), eye)
    return out.reshape(g // 8, 8 * a, 8 * b)


def _block_diag_extract(blk, a, b):
    n = blk.shape[0]
    v = blk.reshape(n, 8, a, 8, b)
    return jnp.einsum("kgahb,gh->kgab", v, jnp.eye(8, dtype=blk.dtype)).reshape(n * 8, a, b)


def _ssm_prepare(l, prm):
    g, n, p = SSM_GROUPS, SSM_STATE, SSM_GROUP
    a_re, a_im = prm["ssm_a_re"][l], prm["ssm_a_im"][l]
    log_dt = prm["ssm_log_dt"][l].reshape(g, 1)
    abar_re, abar_im, f_re, f_im = _zoh_fwd(l, a_re, a_im, log_dt)
    f_re, f_im = f_re.reshape(g, 1, n), f_im.reshape(g, 1, n)
    bbar_re, bbar_im = _bbar_fwd(l, f_re, f_im, prm["ssm_b_re"], prm["ssm_b_im"])
    pw = _powers(l, abar_re.reshape(1, g * n), abar_im.reshape(1, g * n))
    return dict(a_re=a_re, a_im=a_im, log_dt=log_dt, f_re=f_re, f_im=f_im,
                bblk_re=_block_diag(bbar_re), bblk_im=_block_diag(bbar_im),
                cblk_re=_block_diag(prm["ssm_c_re"][l].astype(MXU_DTYPE)),
                cblk_im=_block_diag(prm["ssm_c_im"][l].astype(MXU_DTYPE)), pw=pw,
                dskip=prm["ssm_d"][l].reshape(1, g * p))


def _ssm_param_grads(l, sd, r, prm):
    g, n, p = SSM_GROUPS, SSM_STATE, SSM_GROUP
    dbbar_re = _block_diag_extract(r["dbblk_re"], p, n)
    dbbar_im = _block_diag_extract(r["dbblk_im"], p, n)
    dfr, dfi, db_re, db_im = _bbar_bwd(l, sd["f_re"], sd["f_im"], prm["ssm_b_re"], prm["ssm_b_im"], dbbar_re, dbbar_im)
    cts = (r["dabar_re"].reshape(g, n), r["dabar_im"].reshape(g, n), dfr.reshape(g, n), dfi.reshape(g, n))
    da_re, da_im, dlog_dt = _zoh_bwd(l, sd["a_re"], sd["a_im"], sd["log_dt"], cts)
    return dict(ssm_a_re=da_re, ssm_a_im=da_im, ssm_log_dt=dlog_dt.reshape(g), ssm_b_re=db_re, ssm_b_im=db_im,
                ssm_c_re=_block_diag_extract(r["dcblk_re"], p, n), ssm_c_im=_block_diag_extract(r["dcblk_im"], p, n),
                ssm_d=r["dd"].reshape(g, p))


def _ffn_weight_grads(l, fb, dx2, s, place):
    d = dx2.shape[1]
    hcn = fb["act"].shape[-1]
    g = {}
    for name, key, rhs, ts in (("ffn_w_gate", "dgate", fb["h2"], s), ("ffn_w_up", "dup", fb["h2"], s),
                               ("ffn_w_down", "act", dx2, min(1024, s))):
        g[name] = _tn_matmul(f"d{name}_l{l}", fb[key], BS((None, ts, hcn), lambda j, t, pr: (j, t, 0)), rhs,
                             BS((ts, d), lambda j, t, pr: (t, 0)), (hcn, d), (N_CHIPS, s // ts), place)
    return g


def _in_weight_grad(l, h, dz, place):
    s, d = h.shape
    ncw = dz.shape[1] // N_CHIPS
    return _tn_matmul(f"dw_in_l{l}", h, BS((s, d), lambda j, t, pr: (0, 0)), dz, BS((s, ncw), lambda j, t, pr: (0, j)),
                      (d, ncw), (N_CHIPS, 1), place)


def _fused_tn(name, pairs, kinds, s, place):
    ts = min(512, s)
    n = len(pairs)

    def shape_of(a, b, kind):
        k, m = a.shape[1], b.shape[1]
        if kind == "rows":
            return (N_CHIPS, k // N_CHIPS, m)
        if kind == "cols":
            return (N_CHIPS, k, m // N_CHIPS)
        return (k // 128, 128, 128)

    shapes = [shape_of(a, b, kind) for (a, b), kind in zip(pairs, kinds)]
    last = s // ts - 1
    out_shape = []
    for shp, kind in zip(shapes, kinds):
        out_shape += [SDS(shp, F32)] if kind == "groups" else [SDS(shp[1:], F32), SDS(shp, WIRE_DTYPE)]

    def body(place_ref, *refs):
        ins, outs, accs = refs[:2 * n], refs[2 * n:2 * n + len(out_shape)], refs[2 * n + len(out_shape):]
        first = pl.program_id(0) == 0
        for i, kind in enumerate(kinds):
            a, b = ins[2 * i][...], ins[2 * i + 1][...]
            acc = accs[i]

            @pl.when(first)
            def _():
                acc[...] = jnp.zeros(acc.shape, F32)

            if kind == "rows":
                acc[...] += _mm_tn(a, b).reshape(acc.shape)
            elif kind == "cols":
                full = _mm_tn(a, b)
                nc = acc.shape[2]
                for j in range(N_CHIPS):
                    acc[j] += full[:, j * nc:(j + 1) * nc]
            else:
                for k in range(acc.shape[0]):
                    acc[k] += _mm_tn(a[:, k * 128:(k + 1) * 128], b[:, k * 128:(k + 1) * 128])

        @pl.when(pl.program_id(0) == last)
        def _():
            o = 0
            for i, kind in enumerate(kinds):
                if kind == "groups":
                    outs[o][...] = accs[i][...]
                    o += 1
                else:
                    outs[o][...] = accs[i][place_ref[0]]
                    outs[o + 1][...] = accs[i][...].astype(WIRE_DTYPE)
                    o += 2

    whole = lambda shp: BS(shp, lambda t, pr: (0,) * len(shp))
    outs = pl.pallas_call(
        body, name=name,
        grid_spec=pltpu.PrefetchScalarGridSpec(
            num_scalar_prefetch=1, grid=(s // ts,),
            in_specs=[BS((ts, v.shape[1]), lambda t, pr: (t, 0)) for pair in pairs for v in pair],
            out_specs=[whole(o.shape) for o in out_shape],
            scratch_shapes=[pltpu.VMEM(shp, F32) for shp in shapes]),
        out_shape=out_shape, compiler_params=_params(),
    )(place, *[v for pair in pairs for v in pair])
    res, o = [], 0
    for kind in kinds:
        if kind == "groups":
            res.append(outs[o])
            o += 1
        else:
            res.append((outs[o], outs[o + 1]))
            o += 2
    return res


def _mixer_weight_grads(l, sv, mb, dx1, s, place):
    g = {}
    (g["w_out"], g["ssm_w_glu"]) = _fused_tn(f"dw_out_glu_l{l}", [(mb["merged"], dx1), (mb["ge"], mb["dt"])],
                                            ("rows", "rows"), s, place)
    (g["ssm_w_proj"], g["conv_w_proj"], g["pool_w_proj"]) = _fused_tn(
        f"dw_proj_l{l}", [(mb["sa"], mb["dya"]), (mb["ac"], mb["dyb"]), (mb["pp"], mb["dyc"])],
        ("cols", "cols", "cols"), s, place)
    (dwgrp,) = _fused_tn(f"dpool_w_group_l{l}", [(sv["p"], mb["dq"])], ("groups",), s, place)
    return g, dwgrp


def _local_step(x, target, weights_of, prm, place, on_grads=None):
    s, d = x.shape
    cw = prm["ssm_b_glu"].shape[1]
    sp = {k: prm[k].reshape(N_LAYERS, 1, -1) for k in ("norm1", "norm2", "b_gate", "ssm_b_glu", "conv_ln_g", "conv_ln_b",
                                                        "pool_scale", "conv_b_dw")}
    sp["pool_w_group"] = prm["pool_w_group"]
    saved = []
    xin = x
    for l in range(N_LAYERS):
        fw = weights_of(l, "in", (xin,))
        sd = _ssm_prepare(l, prm)
        z, h = _in_proj(l, xin, sp["norm1"], fw["w_in"])
        hre, him, y = _ssm_fwd(l, z, sd["bblk_re"], sd["bblk_im"], sd["cblk_re"], sd["cblk_im"], sd["pw"], sd["dskip"])
        p = _pool_fwd(l, z, cw)
        fw.update(weights_of(l, "mixer", (y, p)))
        wdw = fw["conv_w_dw"]
        hc = _conv_fwd(l, z, wdw, sp["conv_b_dw"][l])
        x1 = _merge_fwd(l, xin, y, hc, p, z, fw, sp)
        fw.update(weights_of(l, "ffn", (x1,)))
        x2 = _ffn_fwd(l, x1, sp["norm2"], fw["ffn_w_gate"], fw["ffn_w_up"], fw["ffn_w_down"])
        saved.append(dict(x=xin, z=z, h=h, hre=hre, him=him, y=y, hc=hc, p=p, x1=x1, sd=sd, wdw=wdw, fw=fw))
        xin = x2
    dx, loss, dfinal = _loss_head(xin, target, prm["final_norm"].reshape(1, d))
    big = [None] * N_LAYERS
    small = [None] * N_LAYERS
    norm2_rows = sp["norm2"]
    started = (lambda l, group, grads: on_grads(l, group, grads)) if on_grads is not None else (lambda *a: 0.0)
    for l in reversed(range(N_LAYERS)):
        sv = saved[l]
        sd, fw = sv["sd"], sv["fw"]
        fb = _ffn_bwd(l, sv["x1"], dx, norm2_rows, fw["ffn_w_gate"], fw["ffn_w_up"], fw["ffn_w_down"])
        big[l] = _ffn_weight_grads(l, fb, dx, s, place)
        spl = dict(sp, ssm_b_glu=sp["ssm_b_glu"] + started(l, "ffn", big[l]))
        mb = _merge_bwd(l, fb["dx1"], sv["y"], sv["hc"], sv["p"], sv["z"], fw, spl)
        mixer, dwgrp = _mixer_weight_grads(l, sv, mb, fb["dx1"], s, place)
        big[l].update(mixer)
        wdw = sv["wdw"] + started(l, "mixer", mixer)
        du_c = _pool_bwd(l, mb["dp"])
        dv1, dv2, dwdw, dbdw = _conv_bwd(l, mb["dhc"], sv["z"], wdw)
        sr = _ssm_bwd(l, mb["dy"], sv["z"], sv["hre"], sv["him"], sd["bblk_re"], sd["bblk_im"], sd["cblk_re"],
                      sd["cblk_im"], sd["pw"], sd["dskip"])
        dx, dz, dnorm1 = _in_proj_bwd(l, fb["dx1"], sv["x"], sp["norm1"], fw["w_in"], sr["du"], dv1, dv2, du_c, mb["dzg"])
        w_in_grad = {"w_in": _in_weight_grad(l, sv["h"], dz, place)}
        big[l].update(w_in_grad)
        sg = _ssm_param_grads(l, sd, sr, prm)
        sg.update(norm1=dnorm1.reshape(d), b_gate=mb["db_gate"].reshape(3 * d), ssm_b_glu=mb["db_glu"].reshape(cw),
                  conv_b_dw=dbdw.reshape(cw), conv_ln_g=mb["dln_g"].reshape(cw), conv_ln_b=mb["dln_b"].reshape(cw),
                  pool_w_group=dwgrp, pool_scale=mb["dscale"].reshape(cw), norm2=fb["dnorm2"].reshape(d),
                  conv_w_dw=dwdw)
        small[l] = sg
        if l == N_LAYERS - 1:
            sg = dict(sg, final_norm=dfinal.reshape(d))
        norm2_rows = sp["norm2"] + (started(l, "in", w_in_grad) + started(l, "small", sg))
    return loss[0, 0], dx, big, small, dfinal.reshape(d)


def _place():
    return lax.axis_index("x"), lax.axis_index("y"), lax.axis_index("c")


def _other_chips(x, y):
    return [(1 - x, y), (x, 1 - y), (1 - x, 1 - y)]


def _remote(src, dst, send_sem, recv_sem, device):
    return pltpu.make_async_remote_copy(src_ref=src, dst_ref=dst, send_sem=send_sem, recv_sem=recv_sem,
                                        device_id=device, device_id_type=MESH)


def _hbm(v):
    return pltpu.with_memory_space_constraint(v, pltpu.HBM)


def _cast_into(name, w, place, dtype, after=()):
    nl, k, n = w.shape
    tr = _row_tile(k, n)
    nt = k // tr

    def body(place_ref, w_ref, *rest):
        o0_ref, o1_ref = rest[len(after):]

        @pl.when(pl.program_id(0) == 0)
        def _():
            o0_ref[...] = w_ref[...].astype(dtype)

        @pl.when(pl.program_id(0) == 1)
        def _():
            o1_ref[...] = w_ref[...].astype(dtype)

    return pl.pallas_call(
        body, name=f"cast_{name}",
        grid_spec=pltpu.PrefetchScalarGridSpec(
            num_scalar_prefetch=1, grid=(nl, nt),
            in_specs=[BS((None, tr, n), lambda l, t, pr: (l, t, 0))] + [ANY] * len(after),
            out_specs=[BS((None, tr, n), lambda l, t, pr: (pr[0], t * (1 - l) + (nt - 1) * l, 0)),
                       BS((None, tr, n), lambda l, t, pr: (pr[0], t * l, 0))]),
        out_shape=[SDS((N_CHIPS, k, n), dtype)] * 2)(place, w, *after)


def _gather_rows(buf, c):
    k = buf.shape[1]
    if k % 2:
        return pl.ds(0, k)
    return pl.ds(pl.multiple_of(c * (k // 2), 8), k // 2)


def _allgather_start(tag, groups):
    ng = len(groups)
    sizes = [len(g) for g in groups]
    first = [sum(sizes[:g]) for g in range(ng)]
    flat = [b for g in groups for b in g]
    nb = len(flat)

    def body(*refs):
        ins = refs[:nb]
        sems = refs[nb:nb + 2 * ng]
        token = refs[-1]
        x, y, c = _place()
        jme = 2 * x + y
        for g in range(ng):
            for a in range(sizes[g]):
                buf = ins[first[g] + a]
                blk = buf.at[jme, _gather_rows(buf, c)]
                for k, (cx, cy) in enumerate(_other_chips(x, y)):
                    _remote(blk, blk, sems[2 * g].at[3 * a + k], sems[2 * g + 1].at[3 * a + k], (cx, cy, c)).start()
        token[...] = jnp.zeros(token.shape, F32)

    sem_shapes = [pltpu.SemaphoreType.DMA((3 * sizes[g // 2],)) for g in range(2 * ng)]
    outs = pl.pallas_call(
        body, name=f"allgather_start_{tag}", in_specs=[HBM] * nb,
        out_specs=[SEM] * (2 * ng) + [HBM] * nb + [pl.BlockSpec(memory_space=pltpu.VMEM)],
        out_shape=sem_shapes + [pltpu.HBM(b.shape, b.dtype) for b in flat] + [SDS((8, 128), F32)],
        input_output_aliases={i: 2 * ng + i for i in range(nb)},
        compiler_params=pltpu.CompilerParams(has_side_effects=SIDE_EFFECT))(*[_hbm(b) for b in flat])
    per_group = [(outs[2 * g], outs[2 * g + 1], outs[2 * ng + first[g]:2 * ng + first[g] + sizes[g]])
                 for g in range(ng)]
    return per_group, outs[-1]


def _allgather_wait(l, send_sems, recv_sems, bufs, after):
    n = len(bufs)

    def body(*refs):
        ins = refs[:n]
        ssem, rsem = refs[n], refs[n + 1]
        x, y, c = _place()
        jme = 2 * x + y
        for a in range(n):
            rows = _gather_rows(ins[a], c)
            for k, (cx, cy) in enumerate(_other_chips(x, y)):
                cp = _remote(ins[a].at[jme, rows], ins[a].at[2 * cx + cy, rows], ssem.at[3 * a + k],
                             rsem.at[3 * a + k], (cx, cy, c))
                cp.wait_send()
                cp.wait_recv()

    return pl.pallas_call(
        body, name=f"allgather_wait_{l}", in_specs=[HBM] * n + [SEM, SEM] + [ANY] * len(after), out_specs=[HBM] * n,
        out_shape=[pltpu.HBM(b.shape, b.dtype) for b in bufs], input_output_aliases={i: i for i in range(n)},
        compiler_params=pltpu.CompilerParams(has_side_effects=SIDE_EFFECT))(*bufs, send_sems, recv_sems, *after)


def _allgather_forward(l, bufs):
    n = len(bufs)
    split = [a for a in range(n) if bufs[a].shape[1] % 2 == 0]

    def body(*refs):
        ins = refs[:n]
        send_sems, recv_sems = refs[2 * n:]
        x, y, c = _place()
        sibling = (x, y, 1 - c)
        copies = []
        for a in split:
            for k, (cx, cy) in enumerate(_other_chips(x, y)):
                blk = ins[a].at[2 * cx + cy, _gather_rows(ins[a], c)]
                cp = _remote(blk, blk, send_sems.at[a, k], recv_sems.at[a, k], sibling)
                cp.start()
                copies.append(cp)
        for a in split:
            for k, (cx, cy) in enumerate(_other_chips(x, y)):
                blk = ins[a].at[2 * cx + cy, _gather_rows(ins[a], 1 - c)]
                _remote(blk, blk, send_sems.at[a, k], recv_sems.at[a, k], sibling).wait_recv()
        for cp in copies:
            cp.wait_send()

    sem = pltpu.SemaphoreType.DMA((n, 3))
    return pl.pallas_call(
        body, name=f"allgather_forward_{l}", in_specs=[ANY] * n, out_specs=[ANY] * n,
        out_shape=[SDS(b.shape, b.dtype) for b in bufs], input_output_aliases={i: i for i in range(n)},
        scratch_shapes=[sem, sem])(*bufs)


def _rs_to_owner(l, parts):
    n = len(parts)
    lands = [lax.empty((3,) + p.shape[1:], p.dtype) for p in parts]

    def body(*refs):
        ins, zones = refs[:n], refs[n:2 * n]
        send_sems, recv_sems = refs[2 * n], refs[2 * n + 1]
        token = refs[-1]
        x, y, c = _place()
        for a in range(n):
            for k, (cx, cy) in enumerate(_other_chips(x, y)):
                _remote(ins[a].at[2 * cx + cy], zones[a].at[k], send_sems.at[3 * a + k], recv_sems.at[3 * a + k],
                        (cx, cy, c)).start()
        token[...] = jnp.zeros(token.shape, F32)

    sem = pltpu.SemaphoreType.DMA((3 * n,))
    outs = pl.pallas_call(
        body, name=f"rs_to_owner_start_{l}", in_specs=[HBM] * (2 * n),
        out_specs=[SEM, SEM] + [HBM] * (2 * n) + [pl.BlockSpec(memory_space=pltpu.VMEM)],
        out_shape=[sem, sem] + [pltpu.HBM(p.shape, p.dtype) for p in parts]
        + [pltpu.HBM(z.shape, z.dtype) for z in lands] + [SDS((8, 128), F32)],
        input_output_aliases={i: 2 + i for i in range(2 * n)},
        compiler_params=pltpu.CompilerParams(has_side_effects=SIDE_EFFECT),
    )(*[_hbm(p) for p in parts], *[_hbm(z) for z in lands])
    return outs[0], outs[1], outs[2:2 + n], outs[2 + n:2 + 2 * n], outs[-1]


def _rs_to_owner_wait(l, send_sems, recv_sems, parts, lands, after):
    n = len(parts)

    def body(*refs):
        ins, zones = refs[:n], refs[n:2 * n]
        ssem, rsem = refs[2 * n], refs[2 * n + 1]
        x, y, c = _place()
        for a in range(n):
            for k, (cx, cy) in enumerate(_other_chips(x, y)):
                cp = _remote(ins[a].at[2 * cx + cy], zones[a].at[k], ssem.at[3 * a + k], rsem.at[3 * a + k],
                             (cx, cy, c))
                cp.wait_send()
                cp.wait_recv()

    outs = pl.pallas_call(
        body, name=f"rs_to_owner_wait_{l}", in_specs=[HBM] * (2 * n) + [SEM, SEM] + [ANY] * len(after),
        out_specs=[HBM] * (2 * n),
        out_shape=[pltpu.HBM(p.shape, p.dtype) for p in parts] + [pltpu.HBM(z.shape, z.dtype) for z in lands],
        input_output_aliases={i: i for i in range(2 * n)},
        compiler_params=pltpu.CompilerParams(has_side_effects=SIDE_EFFECT),
    )(*parts, *lands, send_sems, recv_sems, *after)
    return outs[:n], outs[n:]


def _rs_sibling_exchange(l, both):
    n = len(both)

    def body(*refs):
        ins = refs[:n]
        send_sems, recv_sems = refs[2 * n:]
        x, y, c = _place()
        copies = []
        for a in range(n):
            cp = _remote(ins[a].at[c], ins[a].at[c], send_sems.at[a], recv_sems.at[a], (x, y, 1 - c))
            cp.start()
            copies.append(cp)
        for a, cp in enumerate(copies):
            cp.wait_send()
            _remote(ins[a].at[1 - c], ins[a].at[1 - c], send_sems.at[a], recv_sems.at[a], (x, y, 1 - c)).wait_recv()

    sem = pltpu.SemaphoreType.DMA((n,))
    return pl.pallas_call(
        body, name=f"rs_sibling_exchange_{l}", in_specs=[ANY] * n, out_specs=[ANY] * n,
        out_shape=[SDS(b.shape, b.dtype) for b in both], input_output_aliases={i: i for i in range(n)},
        scratch_shapes=[sem, sem])(*both)


def _add_owner(name, grad, recv, place):
    r, cols = grad.shape
    tr = _row_tile(r, cols, budget=1024 * 1024)
    nt = r // tr

    def body(place_ref, g_ref, r_ref, o_ref):
        acc = ((g_ref[...] + r_ref[0].astype(F32)) + r_ref[1].astype(F32)) + r_ref[2].astype(F32)
        o_ref[...] = acc.astype(o_ref.dtype)

    return pl.pallas_call(
        body, name=name,
        grid_spec=pltpu.PrefetchScalarGridSpec(
            num_scalar_prefetch=1, grid=(nt,),
            in_specs=[BS((tr, cols), lambda t, pr: (t, 0)), BS((3, tr, cols), lambda t, pr: (0, t, 0))],
            out_specs=BS((None, tr, cols), lambda t, pr: (pr[1], t, 0))),
        out_shape=SDS((2, r, cols), WIRE_DTYPE))(place, grad, recv)


def _reduce_start(tag, grads):
    names = list(grads)
    send_sems, recv_sems, wires, lands, token = _rs_to_owner(tag, [grads[n][1] for n in names])
    return dict(tag=tag, names=names, send_sems=send_sems, recv_sems=recv_sems, wires=wires, lands=lands,
                grads=[grads[n][0] for n in names]), token


def _reduce_finish(pending, place, after):
    tag, names = pending["tag"], pending["names"]
    _, lands = _rs_to_owner_wait(tag, pending["send_sems"], pending["recv_sems"], pending["wires"],
                                 pending["lands"], after)
    mine = [_add_owner(f"rs_add_owner_{n}_{tag}", g, r, place) for n, g, r in zip(names, pending["grads"], lands)]
    return dict(zip(names, _rs_sibling_exchange(tag, mine)))


def _small_peers(x, y, c):
    return [(x, y, 1 - c)] + [(cx, cy, c) for cx, cy in _other_chips(x, y)]


def _allgather_rows_start(tag, bufs):
    n = len(bufs)
    lands = [lax.empty((8,) + b.shape, b.dtype) for b in bufs]

    def body(*refs):
        ins, zones = refs[:n], refs[n:2 * n]
        send_sems, recv_sems = refs[2 * n], refs[2 * n + 1]
        token = refs[-1]
        x, y, c = _place()
        for a in range(n):
            for i, peer in enumerate(_small_peers(x, y, c)):
                _remote(ins[a], zones[a].at[4 * x + 2 * y + c], send_sems.at[4 * a + i], recv_sems.at[4 * a + i],
                        peer).start()
        token[...] = jnp.zeros(token.shape, F32)

    sem = pltpu.SemaphoreType.DMA((4 * n,))
    outs = pl.pallas_call(
        body, name=f"allgather_small_start_{tag}", in_specs=[HBM] * (2 * n),
        out_specs=[SEM, SEM] + [HBM] * (2 * n) + [pl.BlockSpec(memory_space=pltpu.VMEM)],
        out_shape=[sem, sem] + [pltpu.HBM(b.shape, b.dtype) for b in bufs]
        + [pltpu.HBM(z.shape, z.dtype) for z in lands] + [SDS((8, 128), F32)],
        input_output_aliases={i: 2 + i for i in range(2 * n)},
        compiler_params=pltpu.CompilerParams(has_side_effects=SIDE_EFFECT),
    )(*[_hbm(b) for b in bufs], *[_hbm(z) for z in lands])
    return outs[0], outs[1], outs[2:2 + n], outs[2 + n:2 + 2 * n], outs[-1]


def _allgather_rows_wait(tag, send_sems, recv_sems, bufs, lands, after):
    n = len(bufs)

    def body(*refs):
        ins, zones = refs[:n], refs[n:2 * n]
        ssem, rsem = refs[2 * n], refs[2 * n + 1]
        x, y, c = _place()
        for a in range(n):
            for i, (px, py, pc) in enumerate(_small_peers(x, y, c)):
                cp = _remote(ins[a], zones[a].at[4 * px + 2 * py + pc], ssem.at[4 * a + i], rsem.at[4 * a + i],
                             (px, py, pc))
                cp.wait_send()
                cp.wait_recv()

    outs = pl.pallas_call(
        body, name=f"allgather_small_wait_{tag}", in_specs=[HBM] * (2 * n) + [SEM, SEM, ANY],
        out_specs=[HBM] * (2 * n),
        out_shape=[pltpu.HBM(b.shape, b.dtype) for b in bufs] + [pltpu.HBM(z.shape, z.dtype) for z in lands],
        input_output_aliases={i: i for i in range(2 * n)},
        compiler_params=pltpu.CompilerParams(has_side_effects=SIDE_EFFECT),
    )(*bufs, *lands, send_sems, recv_sems, after)
    return outs[:n], outs[n:]


def _allgather_rows_forward(tag, lands):
    n = len(lands)

    def body(*refs):
        ins = refs[:n]
        send_sems, recv_sems = refs[2 * n:]
        x, y, c = _place()
        sibling = (x, y, 1 - c)
        copies = []
        for a in range(n):
            for k, (cx, cy) in enumerate(_other_chips(x, y)):
                blk = ins[a].at[4 * cx + 2 * cy + c]
                cp = _remote(blk, blk, send_sems.at[a, k], recv_sems.at[a, k], sibling)
                cp.start()
                copies.append(cp)
        for a in range(n):
            for k, (cx, cy) in enumerate(_other_chips(x, y)):
                blk = ins[a].at[4 * cx + 2 * cy + 1 - c]
                _remote(blk, blk, send_sems.at[a, k], recv_sems.at[a, k], sibling).wait_recv()
        for cp in copies:
            cp.wait_send()

    sem = pltpu.SemaphoreType.DMA((n, 3))
    return pl.pallas_call(body, name=f"allgather_small_forward_{tag}", in_specs=[ANY] * n, out_specs=[ANY] * n,
                          out_shape=[SDS(z.shape, z.dtype) for z in lands],
                          input_output_aliases={i: i for i in range(n)}, scratch_shapes=[sem, sem])(*lands)


def _sum_devices(tag, gathered, mine, place):
    _, r, cols = gathered.shape
    tr = _row_tile(r, cols, budget=256 * 1024)

    def body(place_ref, g_ref, x_ref, o_ref):
        me = 2 * place_ref[0] + place_ref[1]
        acc = jnp.where(me == 0, x_ref[...], g_ref[0])
        for k in range(1, 8):
            acc = acc + jnp.where(me == k, x_ref[...], g_ref[k])
        o_ref[...] = acc

    return pl.pallas_call(
        body, name=f"sum_small_grads_{tag}",
        grid_spec=pltpu.PrefetchScalarGridSpec(
            num_scalar_prefetch=1, grid=(r // tr,),
            in_specs=[BS((8, tr, cols), lambda t, pr: (0, t, 0)), BS((tr, cols), lambda t, pr: (t, 0))],
            out_specs=BS((tr, cols), lambda t, pr: (t, 0))),
        out_shape=SDS((r, cols), F32))(place, gathered, mine)


def _adamw_values(w, g, m, v):
    m = ADAM_B1 * m + (1.0 - ADAM_B1) * g
    v = ADAM_B2 * v + (1.0 - ADAM_B2) * (g * g)
    m_hat = m / (1.0 - ADAM_B1 ** ADAM_STEP)
    v_hat = v / (1.0 - ADAM_B2 ** ADAM_STEP)
    delta = -ADAM_LR * (m_hat / (jnp.sqrt(v_hat) + ADAM_EPS) + ADAM_WD * w)
    return delta, m, v


def _adamw_big(name, l, w, m, v, g, earlier=None, after=()):
    nl, r, cols = w.shape
    tr = _row_tile(r, cols, budget=1024 * 1024)
    nt = r // tr
    n_prev = 0 if earlier is None else 4

    def body(*refs):
        w_ref, m_ref, v_ref, g_ref = refs[:4]
        go_ref, d_ref, mo_ref, vo_ref = refs[4 + n_prev + len(after):]
        gv = g_ref[0].astype(F32) + g_ref[1].astype(F32)
        delta, m_new, v_new = _adamw_values(w_ref[...], gv, m_ref[...], v_ref[...])
        go_ref[...] = gv
        d_ref[...] = delta
        mo_ref[...] = m_new
        vo_ref[...] = v_new

    layer = BS((None, tr, cols), lambda t: (l, t, 0))
    return pl.pallas_call(
        body, name=f"adamw_{name}_l{l}", grid=(nt,),
        in_specs=[layer, layer, layer, BS((2, tr, cols), lambda t: (0, t, 0))] + [ANY] * (n_prev + len(after)),
        out_specs=[layer] * 4, out_shape=[SDS(w.shape, F32)] * 4,
        input_output_aliases={4 + i: i for i in range(n_prev)}, compiler_params=_params(),
    )(w, m, v, g, *(earlier or ()), *after)


def _adamw_mid(name, w, m, v, gathered, mine, place):
    shape = w.shape[1:]
    zeros = (0,) * len(shape)

    def body(place_ref, w_ref, m_ref, v_ref, *refs):
        gath, own = refs[:N_LAYERS], refs[N_LAYERS:2 * N_LAYERS]
        go_ref, d_ref, mo_ref, vo_ref = refs[2 * N_LAYERS:]
        me = 2 * place_ref[0] + place_ref[1]
        sums = []
        for l in range(N_LAYERS):
            acc = jnp.where(me == 0, own[l][...], gath[l][0])
            for k in range(1, 8):
                acc = acc + jnp.where(me == k, own[l][...], gath[l][k])
            sums.append(acc)
        gv = sums[0]
        for l in range(1, N_LAYERS):
            gv = jnp.where(pl.program_id(0) == l, sums[l], gv)
        delta, m_new, v_new = _adamw_values(w_ref[...], gv, m_ref[...], v_ref[...])
        go_ref[...] = gv
        d_ref[...] = delta
        mo_ref[...] = m_new
        vo_ref[...] = v_new

    layer = BS((None,) + shape, lambda l, pr: (l,) + zeros)
    return pl.pallas_call(
        body, name=f"adamw_{name}",
        grid_spec=pltpu.PrefetchScalarGridSpec(
            num_scalar_prefetch=1, grid=(N_LAYERS,),
            in_specs=[layer] * 3 + [BS((8,) + shape, lambda l, pr: (0,) + zeros)] * N_LAYERS
            + [BS(shape, lambda l, pr: zeros)] * N_LAYERS,
            out_specs=[layer] * 4),
        out_shape=[SDS(w.shape, F32)] * 4, compiler_params=_params())(place, w, m, v, *gathered, *mine)


def _adamw_rows(w, m, v, g):
    r, cols = w.shape
    tr = _row_tile(r, cols, budget=512 * 1024)

    def body(w_ref, m_ref, v_ref, g_ref, d_ref, mo_ref, vo_ref):
        delta, m_new, v_new = _adamw_values(w_ref[...], g_ref[...], m_ref[...], v_ref[...])
        d_ref[...] = delta
        mo_ref[...] = m_new
        vo_ref[...] = v_new

    spec = BS((tr, cols), lambda t: (t, 0))
    return pl.pallas_call(body, name="adamw_small", grid=(r // tr,), in_specs=[spec] * 4, out_specs=[spec] * 3,
                          out_shape=[SDS(w.shape, F32)] * 3)(w, m, v, g)


PACK_ALIGN = 8 * 128
PACK_ROWS = 128


def _pack_rows(arrays):
    parts, rows = [], 0
    for a in arrays:
        flat = a.reshape(-1)
        pad = (-flat.shape[0]) % PACK_ALIGN
        if pad:
            flat = jnp.pad(flat, (0, pad))
        parts.append(flat.reshape(-1, 128))
        rows += parts[-1].shape[0]
    if rows % PACK_ROWS:
        parts.append(jnp.zeros((PACK_ROWS - rows % PACK_ROWS, 128), parts[0].dtype))
    return jnp.concatenate(parts, axis=0)


def _unpack_rows(buf, shapes):
    out, row = [], 0
    for shape in shapes:
        size = math.prod(shape)
        rows = -(-size // PACK_ALIGN) * (PACK_ALIGN // 128)
        out.append(buf[row:row + rows].reshape(-1)[:size].reshape(shape))
        row += rows
    return out


def kernel(x, norm1, w_in, b_gate, ssm_a_re, ssm_a_im, ssm_log_dt, ssm_b_re, ssm_b_im, ssm_c_re, ssm_c_im, ssm_d, ssm_w_glu, ssm_b_glu, ssm_w_proj, conv_w_dw, conv_b_dw, conv_ln_g, conv_ln_b, conv_w_proj, pool_w_group, pool_scale, pool_w_proj, w_out, norm2, ffn_w_gate, ffn_w_up, ffn_w_down, final_norm, loss_target, m_norm1, m_w_in, m_b_gate, m_ssm_a_re, m_ssm_a_im, m_ssm_log_dt, m_ssm_b_re, m_ssm_b_im, m_ssm_c_re, m_ssm_c_im, m_ssm_d, m_ssm_w_glu, m_ssm_b_glu, m_ssm_w_proj, m_conv_w_dw, m_conv_b_dw, m_conv_ln_g, m_conv_ln_b, m_conv_w_proj, m_pool_w_group, m_pool_scale, m_pool_w_proj, m_w_out, m_norm2, m_ffn_w_gate, m_ffn_w_up, m_ffn_w_down, m_final_norm, v_norm1, v_w_in, v_b_gate, v_ssm_a_re, v_ssm_a_im, v_ssm_log_dt, v_ssm_b_re, v_ssm_b_im, v_ssm_c_re, v_ssm_c_im, v_ssm_d, v_ssm_w_glu, v_ssm_b_glu, v_ssm_w_proj, v_conv_w_dw, v_conv_b_dw, v_conv_ln_g, v_conv_ln_b, v_conv_w_proj, v_pool_w_group, v_pool_scale, v_pool_w_proj, v_w_out, v_norm2, v_ffn_w_gate, v_ffn_w_up, v_ffn_w_down, v_final_norm):
    given = dict(locals())
    cx, cy, cc = _place()
    place = jnp.stack([2 * cx + cy, cc]).astype(jnp.int32)

    def kernel_view(n, a):
        if n in TRANSPOSED:
            return a.transpose(0, 2, 1)
        return a.transpose(0, 1, 3, 2) if n in ("ssm_b_re", "ssm_b_im") else a

    prm = {n: given[n] for n in WEIGHTS}
    mom = {n: given["m_" + n] for n in WEIGHTS}
    var = {n: given["v_" + n] for n in WEIGHTS}
    for n in MID:
        prm[n], mom[n], var[n] = kernel_view(n, prm[n]), kernel_view(n, mom[n]), kernel_view(n, var[n])

    dw_shard = prm["conv_w_dw"].reshape(N_LAYERS, CONV_KERNEL, -1)
    casts = {"w_in": _cast_into("w_in", prm["w_in"], place, MXU_DTYPE)}
    first, first_started = _allgather_start("first", [[casts["w_in"][0]]])
    in_flight = {(0, "in"): first[0]}
    casts.update({n: _cast_into(n, kernel_view(n, prm[n]), place, MXU_DTYPE, after=(first_started,))
                  for n in BIG if n != "w_in"})
    casts["conv_w_dw"] = _cast_into("conv_w_dw", dw_shard, place, F32, after=(first_started,))
    order = [(l, g) for l in range(N_LAYERS) for g in GATHER_GROUPS if (l, g) != (0, "in")]
    rest, rest_started = _allgather_start("rest", [[casts[n][l] for n in GATHER_GROUPS[g]] for l, g in order])
    in_flight.update(zip(order, rest))

    def weights_of(l, group, after):
        send_sems, recv_sems, bufs = in_flight[l, group]
        tag = f"l{l}_{group}"
        if (l, group) == (0, "in"):
            after = after + (rest_started,)
        bufs = _allgather_forward(tag, _allgather_wait(tag, send_sems, recv_sems, bufs, after))
        fw = dict(zip(GATHER_GROUPS[group], bufs))
        if "conv_w_dw" in fw:
            fw["conv_w_dw"] = fw["conv_w_dw"].transpose(1, 0, 2).reshape(CONV_KERNEL, -1)
        return fw

    pending, small_pending, small_shapes = {}, {}, {}
    tokens = {}

    def on_grads(l, group, grads):
        if group == "small":
            packed = {n: g for n, g in grads.items() if n not in MID}
            small_shapes[l] = {n: g.shape for n, g in packed.items()}
            begun = _allgather_rows_start(f"l{l}", [_pack_rows(list(packed.values()))] + [grads[n] for n in MID])
            small_pending[l], token = begun[:4], begun[4]
        else:
            pending[l, group], token = _reduce_start(f"{l}_{group}", grads)
        tokens[l, group] = token
        return token[0, 0]

    loss, dx, _, _, _ = _local_step(x[0], loss_target[0], weights_of, prm, place, on_grads)
    loss = lax.psum(loss, ("x", "y", "c"))

    reduced = [{} for _ in range(N_LAYERS)]
    out = {}

    def finish(l, group, after):
        reduced[l].update(_reduce_finish(pending[l, group], place, after))

    def adamw(l, names, done):
        for n in names:
            out[n] = _adamw_big(n, l, kernel_view(n, prm[n]), kernel_view(n, mom[n]), kernel_view(n, var[n]),
                                reduced[l][n], out.get(n), after=done)
            done = (out[n][0],)
        return done

    top = N_LAYERS - 1
    done = (tokens[0, "in"], tokens[0, "small"])
    for group in ("ffn", "mixer", "in"):
        finish(top, group, done)
    done = adamw(top, BIG, done)
    for group in ("ffn", "mixer", "in"):
        finish(0, group, done)
        done = adamw(0, [n for n in GATHER_GROUPS[group] if n in BIG], done)
    for n in BIG:
        out[n] = tuple(kernel_view(n, a) for a in out[n])

    gsmall = {}
    mid_mine, mid_gathered = [], []
    for l in range(N_LAYERS):
        mine, lands = _allgather_rows_wait(f"l{l}", *small_pending[l], done[0])
        lands = _allgather_rows_forward(f"l{l}", lands)
        mid_mine.append(mine[1:])
        mid_gathered.append(lands[1:])
        gsum = _sum_devices(f"l{l}", lands[0], mine[0], place)
        for n, g in zip(small_shapes[l], _unpack_rows(gsum, list(small_shapes[l].values()))):
            gsmall.setdefault(n, [None] * N_LAYERS)[l] = g
    for i, n in enumerate(MID):
        out[n] = tuple(kernel_view(n, a) for a in _adamw_mid(
            n, prm[n], mom[n], var[n], [mid_gathered[l][i] for l in range(N_LAYERS)],
            [mid_mine[l][i] for l in range(N_LAYERS)], place))
    gsmall = {n: (g[top] if n == "final_norm" else jnp.stack(g)) for n, g in gsmall.items()}
    lanes = dw_shard.shape[-1]
    gsmall["conv_w_dw"] = lax.dynamic_slice_in_dim(gsmall["conv_w_dw"], (2 * cx + cy) * lanes, lanes, axis=2)
    small_names = [n for n in SMALL if n not in MID] + ["conv_w_dw"]
    w_rows = _pack_rows([prm[n] for n in small_names])
    m_rows = _pack_rows([mom[n] for n in small_names])
    v_rows = _pack_rows([var[n] for n in small_names])
    g_rows = _pack_rows([gsmall[n] for n in small_names])
    shapes = [prm[n].shape for n in small_names]
    d_s, m_s, v_s = (_unpack_rows(r, shapes) for r in _adamw_rows(w_rows, m_rows, v_rows, g_rows))
    for i, n in enumerate(small_names):
        out[n] = (gsmall[n].reshape(prm[n].shape), d_s[i], m_s[i], v_s[i])
    grads = [out[n][0] for n in WEIGHTS]
    deltas = [out[n][1] for n in WEIGHTS]
    new_m = [out[n][2] for n in WEIGHTS]
    new_v = [out[n][3] for n in WEIGHTS]
    return (loss, dx[None], *grads, *deltas, *new_m, *new_v)
```

```python
import functools
import math

import jax
import jax.numpy as jnp
from jax import lax
from jax.experimental import pallas as pl
from jax.experimental.pallas import tpu as pltpu

F32 = jnp.float32
MXU_DTYPE = jnp.bfloat16
WIRE_DTYPE = jnp.bfloat16
SDS = jax.ShapeDtypeStruct
BS = pl.BlockSpec
ANY = pl.BlockSpec(memory_space=pl.ANY)
HBM = pl.BlockSpec(memory_space=pltpu.HBM)
SEM = pl.BlockSpec(memory_space=pltpu.SEMAPHORE)
SIDE_EFFECT = pltpu.SideEffectType.DATAFLOW_SIDE_EFFECTING
MESH = pl.DeviceIdType.MESH

EPS = 1e-6
N_CHIPS = 4
N_LAYERS = 2
SSM_GROUPS, SSM_STATE, SSM_GROUP = 32, 64, 16
CONV_KERNEL = 31
CONV_PAD = 32
POOL_WINDOWS = (2, 4, 8, 16)
GELU_C = math.sqrt(2.0 / math.pi)
ADAM_LR, ADAM_B1, ADAM_B2, ADAM_EPS, ADAM_WD, ADAM_STEP = 0.001, 0.9, 0.999, 1e-08, 0.01, 10
VMEM_LIMIT = 56 * 1024 * 1024

BIG = ("w_in", "ssm_w_glu", "ssm_w_proj", "conv_w_proj", "pool_w_proj", "w_out", "ffn_w_gate", "ffn_w_up", "ffn_w_down")
TRANSPOSED = ("ffn_w_gate", "ffn_w_up")
MID = ("ssm_b_re", "ssm_b_im", "ssm_c_re", "ssm_c_im")
GATHER_GROUPS = {
    "in": ("w_in",),
    "mixer": ("ssm_w_glu", "ssm_w_proj", "conv_w_proj", "pool_w_proj", "w_out", "conv_w_dw"),
    "ffn": ("ffn_w_gate", "ffn_w_up", "ffn_w_down"),
}
SMALL = ("norm1", "b_gate", "ssm_a_re", "ssm_a_im", "ssm_log_dt", "ssm_b_re", "ssm_b_im", "ssm_c_re", "ssm_c_im",
         "ssm_d", "ssm_b_glu", "conv_b_dw", "conv_ln_g", "conv_ln_b", "pool_w_group", "pool_scale", "norm2",
         "final_norm")
WEIGHTS = ("norm1", "w_in", "b_gate", "ssm_a_re", "ssm_a_im", "ssm_log_dt", "ssm_b_re", "ssm_b_im", "ssm_c_re",
           "ssm_c_im", "ssm_d", "ssm_w_glu", "ssm_b_glu", "ssm_w_proj", "conv_w_dw", "conv_b_dw", "conv_ln_g",
           "conv_ln_b", "conv_w_proj", "pool_w_group", "pool_scale", "pool_w_proj", "w_out", "norm2", "ffn_w_gate",
           "ffn_w_up", "ffn_w_down", "final_norm")


def _params(vmem=True):
    return pltpu.CompilerParams(vmem_limit_bytes=VMEM_LIMIT) if vmem else None


def _mm(a, b):
    return jnp.dot(a.astype(MXU_DTYPE), b.astype(MXU_DTYPE), preferred_element_type=F32)


def _mm_nt(a, b):
    return lax.dot_general(a.astype(MXU_DTYPE), b.astype(MXU_DTYPE), (((1,), (1,)), ((), ())),
                           preferred_element_type=F32)


def _mm_tn(a, b):
    return lax.dot_general(a.astype(MXU_DTYPE), b.astype(MXU_DTYPE), (((0,), (0,)), ((), ())),
                           preferred_element_type=F32)


def _sigmoid(x):
    return jax.nn.sigmoid(x)


def _gelu(x):
    t = jnp.tanh(GELU_C * (x + 0.044715 * (x * x * x)))
    return x * (0.5 * (1.0 + t)), t


def _gelu_grad(x, t):
    return 0.5 * (1.0 + t) + 0.5 * x * (1.0 - t * t) * (GELU_C * (1.0 + 3.0 * 0.044715 * x * x))


def _colsum(v):
    return jnp.sum(v, axis=0, keepdims=True)


def _row_tile(rows, cols, itemsize=4, budget=1536 * 1024):
    best = None
    for t in range(8, rows + 1, 8):
        if rows % t == 0 and t * cols * itemsize <= budget:
            best = t
    return best if best is not None else rows


def _in_proj(l, x, norm1, w_in):
    s, d = x.shape
    nc = w_in.shape[-1]
    tm = min(512, s)
    nt = s // tm

    def body(x_ref, g_ref, w_ref, z_ref, h_ref, h_all):
        i = pl.program_id(1)
        rows = pl.ds(pl.multiple_of(i * tm, tm), tm)

        @pl.when(pl.program_id(0) == 0)
        def _():
            xv = x_ref[...]
            r = lax.rsqrt(jnp.mean(xv * xv, axis=-1, keepdims=True) + EPS)
            hv = (xv * r * g_ref[...]).astype(h_ref.dtype)
            h_ref[...] = hv.T
            h_all[rows, :] = hv

        z_ref[...] = _mm(h_all[rows, :], w_ref[...])

    tile_of = lambda j, i: i * (1 - jnp.minimum(j, 1)) + (nt - 1) * jnp.minimum(j, 1)
    return pl.pallas_call(
        body, name=f"in_proj_l{l}", grid=(N_CHIPS, nt),
        in_specs=[BS((tm, d), lambda j, i: (tile_of(j, i), 0)), BS((None, 1, d), lambda j, i: (l, 0, 0)),
                  BS((None, d, nc), lambda j, i: (j, 0, 0))],
        out_specs=[BS((tm, nc), lambda j, i: (i, j)), BS((d, tm), lambda j, i: (0, tile_of(j, i)))],
        out_shape=[SDS((s, N_CHIPS * nc), F32), SDS((d, s), MXU_DTYPE)],
        scratch_shapes=[pltpu.VMEM((s, d), MXU_DTYPE)], compiler_params=_params())(x, norm1, w_in)


def _mm_cols(a, w_ref):
    return jnp.concatenate([_mm(a, w_ref[j]) for j in range(N_CHIPS)], axis=1)


def _mm_nt_cols(dv, w_ref):
    nc = w_ref.shape[-1]
    acc = _mm_nt(dv[:, 0:nc], w_ref[0])
    for j in range(1, N_CHIPS):
        acc = acc + _mm_nt(dv[:, j * nc:(j + 1) * nc], w_ref[j])
    return acc


def _merge_values(y, hc, p, zg, wglu, bglu, wpa, wpb, wpc, lng, lnb, wgrp, scale, bg):
    v = {}
    ge, th = _gelu(y)
    t = _mm(ge, wglu) + bglu
    sg = _sigmoid(t)
    sa = ge * sg
    ya = _mm_cols(sa, wpa)
    mu = jnp.mean(hc, axis=-1, keepdims=True)
    xc = hc - mu
    r = lax.rsqrt(jnp.mean(xc * xc, axis=-1, keepdims=True) + EPS)
    xh = xc * r
    ln = xh * lng + lnb
    sl = _sigmoid(ln)
    ac = ln * sl
    yb = _mm_cols(ac, wpb)
    gw = p.shape[1] // len(POOL_WINDOWS)
    q = jnp.concatenate([_mm(p[:, k * gw:(k + 1) * gw], wgrp[k]) for k in range(len(POOL_WINDOWS))], axis=1)
    pp = q * scale
    yc = _mm_cols(pp, wpc)
    d = ya.shape[1]
    gates = [_sigmoid(zg[k] + bg[:, k * d:(k + 1) * d]) for k in range(3)]
    merged = gates[0] * ya + gates[1] * yb + gates[2] * yc
    v.update(ge=ge, th=th, sg=sg, sa=sa, ya=ya, r=r, xh=xh, ln=ln, sl=sl, ac=ac, yb=yb, q=q, pp=pp, yc=yc,
             gates=gates, merged=merged)
    return v


def _merge_specs(l, tm, d, cw):
    row = lambda n: BS((None, 1, n), lambda i: (l, 0, 0))
    return [
        BS((tm, cw), lambda i: (i, 0)),
        BS((tm, cw), lambda i: (i, 0)),
        BS((tm, cw), lambda i: (i, 0)),
        BS((tm, d), lambda i: (i, 2)), BS((tm, d), lambda i: (i, 3)), BS((tm, d), lambda i: (i, 4)),
        BS((N_CHIPS, cw // N_CHIPS, cw), lambda i: (0, 0, 0)),
        row(cw),
        BS((N_CHIPS, cw, d // N_CHIPS), lambda i: (0, 0, 0)),
        BS((N_CHIPS, cw, d // N_CHIPS), lambda i: (0, 0, 0)),
        BS((N_CHIPS, cw, d // N_CHIPS), lambda i: (0, 0, 0)),
        row(cw), row(cw),
        BS((None, 4, cw // 4, cw // 4), lambda i: (l, 0, 0, 0)),
        row(cw),
        row(3 * d),
        BS((N_CHIPS, d // N_CHIPS, d), lambda i: (0, 0, 0)),
    ]


def _merge_fwd(l, x, y, hc, p, z, fw, sp):
    s, d = x.shape
    cw = y.shape[1]
    tm = min(256, s)

    def body(x_ref, y_ref, hc_ref, p_ref, z0, z1, z2, wglu, bglu, wpa, wpb, wpc, lng, lnb, wgrp, scale, bg, wout,
             x1_ref):
        v = _merge_values(y_ref[...], hc_ref[...], p_ref[...], (z0[...], z1[...], z2[...]),
                          wglu[...].reshape(cw, cw), bglu[...], wpa, wpb, wpc, lng[...], lnb[...], wgrp, scale[...],
                          bg[...])
        x1_ref[...] = x_ref[...] + _mm(v["merged"], wout[...].reshape(d, d))

    return pl.pallas_call(
        body, name=f"merge_fwd_l{l}", grid=(s // tm,),
        in_specs=[BS((tm, d), lambda i: (i, 0))] + _merge_specs(l, tm, d, cw),
        out_specs=BS((tm, d), lambda i: (i, 0)), out_shape=SDS((s, d), F32), compiler_params=_params(),
    )(x, y, hc, p, z, z, z, fw["ssm_w_glu"], sp["ssm_b_glu"], fw["ssm_w_proj"], fw["conv_w_proj"], fw["pool_w_proj"],
      sp["conv_ln_g"], sp["conv_ln_b"], sp["pool_w_group"], sp["pool_scale"], sp["b_gate"], fw["w_out"])


def _merge_bwd(l, dx1, y, hc, p, z, fw, sp):
    s, d = dx1.shape
    cw = y.shape[1]
    tm = min(256, s)
    m = MXU_DTYPE

    def body(dx1_ref, y_ref, hc_ref, p_ref, z0, z1, z2, wglu, bglu, wpa, wpb, wpc, lng, lnb, wgrp, scale, bg, wout,
             dzg_ref, dy_ref, dhc_ref, dp_ref, merged_ref, sa_ref, ac_ref, pp_ref, ge_ref, dt_ref, dya_ref, dyb_ref,
             dyc_ref, dq_ref, dbg_ref, dbglu_ref, dlng_ref, dlnb_ref, dscale_ref):
        yv = y_ref[...]
        wg = wglu[...].reshape(cw, cw)
        v = _merge_values(yv, hc_ref[...], p_ref[...], (z0[...], z1[...], z2[...]), wg, bglu[...], wpa, wpb, wpc,
                          lng[...], lnb[...], wgrp, scale[...], bg[...])
        dm = _mm_nt(dx1_ref[...], wout[...].reshape(d, d))
        ys = (v["ya"], v["yb"], v["yc"])
        dys, dbg = [], []
        for k in range(3):
            gk = v["gates"][k]
            dzk = dm * ys[k] * (gk * (1.0 - gk))
            dbg.append(_colsum(dzk))
            dzg_ref[:, k * d:(k + 1) * d] = dzk.astype(m)
            dys.append((dm * gk).astype(m))
        dsa = _mm_nt_cols(dys[0], wpa)
        dac = _mm_nt_cols(dys[1], wpb)
        dpp = _mm_nt_cols(dys[2], wpc)
        ge, sg = v["ge"], v["sg"]
        dt = dsa * ge * (sg * (1.0 - sg))
        dge = dsa * sg + _mm_nt(dt, wg)
        dy_ref[...] = dge * _gelu_grad(yv, v["th"])
        ln, sl, xh = v["ln"], v["sl"], v["xh"]
        dln = dac * (sl * (1.0 + ln * (1.0 - sl)))
        dxh = dln * lng[...]
        dhc_ref[...] = v["r"] * (dxh - jnp.mean(dxh, axis=-1, keepdims=True)
                                 - xh * jnp.mean(dxh * xh, axis=-1, keepdims=True))
        dq = dpp * scale[...]
        gw = cw // len(POOL_WINDOWS)
        for k in range(len(POOL_WINDOWS)):
            dp_ref[:, k * gw:(k + 1) * gw] = _mm_nt(dq[:, k * gw:(k + 1) * gw], wgrp[k])
        merged_ref[...] = v["merged"].astype(m)
        sa_ref[...] = v["sa"].astype(m)
        ac_ref[...] = v["ac"].astype(m)
        pp_ref[...] = v["pp"].astype(m)
        ge_ref[...] = ge.astype(m)
        dt_ref[...] = dt.astype(m)
        dya_ref[...] = dys[0]
        dyb_ref[...] = dys[1]
        dyc_ref[...] = dys[2]
        dq_ref[...] = dq.astype(m)

        @pl.when(pl.program_id(0) == 0)
        def _():
            for ref in (dbg_ref, dbglu_ref, dlng_ref, dlnb_ref, dscale_ref):
                ref[...] = jnp.zeros(ref.shape, F32)

        dbg_ref[...] += jnp.concatenate(dbg, axis=1)
        dbglu_ref[...] += _colsum(dt)
        dlng_ref[...] += _colsum(dln * xh)
        dlnb_ref[...] += _colsum(dln)
        dscale_ref[...] += _colsum(dpp * v["q"])

    tile = lambda n: BS((tm, n), lambda i: (i, 0))
    acc = lambda n: BS((1, n), lambda i: (0, 0))
    outs = pl.pallas_call(
        body, name=f"merge_bwd_l{l}", grid=(s // tm,),
        in_specs=[tile(d)] + _merge_specs(l, tm, d, cw),
        out_specs=[tile(3 * d), tile(cw), tile(cw), tile(cw), tile(d), tile(cw), tile(cw), tile(cw), tile(cw), tile(cw),
                   tile(d), tile(d), tile(d), tile(cw), acc(3 * d), acc(cw), acc(cw), acc(cw), acc(cw)],
        out_shape=[SDS((s, 3 * d), m), SDS((s, cw), F32), SDS((s, cw), F32), SDS((s, cw), F32), SDS((s, d), m),
                   SDS((s, cw), m), SDS((s, cw), m), SDS((s, cw), m), SDS((s, cw), m), SDS((s, cw), m), SDS((s, d), m),
                   SDS((s, d), m), SDS((s, d), m), SDS((s, cw), m), SDS((1, 3 * d), F32), SDS((1, cw), F32),
                   SDS((1, cw), F32), SDS((1, cw), F32), SDS((1, cw), F32)],
        compiler_params=_params(),
    )(dx1, y, hc, p, z, z, z, fw["ssm_w_glu"], sp["ssm_b_glu"], fw["ssm_w_proj"], fw["conv_w_proj"], fw["pool_w_proj"],
      sp["conv_ln_g"], sp["conv_ln_b"], sp["pool_w_group"], sp["pool_scale"], sp["b_gate"], fw["w_out"])
    names = ("dzg", "dy", "dhc", "dp", "merged", "sa", "ac", "pp", "ge", "dt", "dya", "dyb", "dyc", "dq", "db_gate",
             "db_glu", "dln_g", "dln_b", "dscale")
    return dict(zip(names, outs))


def _ffn_fwd(l, x1, norm2, wg, wu, wd):
    s, d = x1.shape
    hc = wd.shape[1]
    tm = min(1024, s)

    def body(x_ref, g_ref, wg_ref, wu_ref, wd_ref, o_ref, h_scr):
        @pl.when(pl.program_id(1) == 0)
        def _():
            xv = x_ref[...]
            r = lax.rsqrt(jnp.mean(xv * xv, axis=-1, keepdims=True) + EPS)
            h_scr[...] = (xv * r * g_ref[...]).astype(h_scr.dtype)
            o_ref[...] = xv

        h = h_scr[...]
        gate = _mm_nt(h, wg_ref[...])
        up = _mm_nt(h, wu_ref[...])
        o_ref[...] += _mm(gate * _sigmoid(gate) * up, wd_ref[...])

    return pl.pallas_call(
        body, name=f"ffn_fwd_l{l}", grid=(s // tm, N_CHIPS),
        in_specs=[BS((tm, d), lambda i, j: (i, 0)), BS((None, 1, d), lambda i, j: (l, 0, 0)),
                  BS((None, hc, d), lambda i, j: (j, 0, 0)), BS((None, hc, d), lambda i, j: (j, 0, 0)),
                  BS((None, hc, d), lambda i, j: (j, 0, 0))],
        out_specs=BS((tm, d), lambda i, j: (i, 0)), out_shape=SDS((s, d), F32),
        scratch_shapes=[pltpu.VMEM((tm, d), MXU_DTYPE)], compiler_params=_params())(x1, norm2, wg, wu, wd)


def _ffn_bwd(l, x1, dx2, norm2, wg, wu, wd):
    s, d = x1.shape
    hc = wd.shape[1]
    tm = min(512, s)
    m = MXU_DTYPE
    last = N_CHIPS - 1

    def body(x_ref, dx2_ref, g_ref, wg_ref, wu_ref, wd_ref, dx1_ref, h_ref, act_ref, dgate_ref, dup_ref, dn_ref,
             dh_scr, dxb_scr):
        i, j = pl.program_id(0), pl.program_id(1)

        @pl.when(j == 0)
        def _():
            xv = x_ref[...]
            r = lax.rsqrt(jnp.mean(xv * xv, axis=-1, keepdims=True) + EPS)
            h_ref[...] = (xv * r * g_ref[...]).astype(m)
            dxb_scr[...] = dx2_ref[...].astype(m)
            dh_scr[...] = jnp.zeros(dh_scr.shape, F32)

        @pl.when((i == 0) & (j == 0))
        def _():
            dn_ref[...] = jnp.zeros(dn_ref.shape, F32)

        h = h_ref[...]
        gate = _mm_nt(h, wg_ref[...])
        up = _mm_nt(h, wu_ref[...])
        sg = _sigmoid(gate)
        silu = gate * sg
        act_ref[...] = (silu * up).astype(m).T
        dact = _mm_nt(dxb_scr[...], wd_ref[...])
        dup = (dact * silu).astype(m)
        dgate = (dact * up * (sg * (1.0 + gate * (1.0 - sg)))).astype(m)
        dup_ref[...] = dup.T
        dgate_ref[...] = dgate.T
        dh_scr[...] += _mm(dgate, wg_ref[...]) + _mm(dup, wu_ref[...])

        @pl.when(j == last)
        def _():
            xv = x_ref[...]
            r = lax.rsqrt(jnp.mean(xv * xv, axis=-1, keepdims=True) + EPS)
            xh = xv * r
            dh = dh_scr[...]
            dn_ref[...] += _colsum(dh * xh)
            dxh = dh * g_ref[...]
            dx1_ref[...] = dx2_ref[...] + r * (dxh - xh * jnp.mean(dxh * xh, axis=-1, keepdims=True))

    chunk = BS((None, hc, tm), lambda i, j: (j, 0, i))
    outs = pl.pallas_call(
        body, name=f"ffn_bwd_l{l}", grid=(s // tm, N_CHIPS),
        in_specs=[BS((tm, d), lambda i, j: (i, 0)), BS((tm, d), lambda i, j: (i, 0)),
                  BS((None, 1, d), lambda i, j: (l, 0, 0)),
                  BS((None, hc, d), lambda i, j: (j, 0, 0)), BS((None, hc, d), lambda i, j: (j, 0, 0)),
                  BS((None, hc, d), lambda i, j: (j, 0, 0))],
        out_specs=[BS((tm, d), lambda i, j: (i, 0)), BS((tm, d), lambda i, j: (i, 0)), chunk, chunk, chunk,
                   BS((1, d), lambda i, j: (0, 0))],
        out_shape=[SDS((s, d), F32), SDS((s, d), m), SDS((N_CHIPS, hc, s), m), SDS((N_CHIPS, hc, s), m),
                   SDS((N_CHIPS, hc, s), m), SDS((1, d), F32)],
        scratch_shapes=[pltpu.VMEM((tm, d), F32), pltpu.VMEM((tm, d), m)], compiler_params=_params(),
    )(x1, dx2, norm2, wg, wu, wd)
    return dict(zip(("dx1", "h2", "act", "dgate", "dup", "dnorm2"), outs))


def _loss_head(x, target, gf):
    s, d = x.shape
    tm = min(512, s)

    def body(x_ref, t_ref, g_ref, dx_ref, loss_ref, dg_ref):
        @pl.when(pl.program_id(0) == 0)
        def _():
            loss_ref[...] = jnp.zeros(loss_ref.shape, F32)
            dg_ref[...] = jnp.zeros(dg_ref.shape, F32)

        xv = x_ref[...]
        r = lax.rsqrt(jnp.mean(xv * xv, axis=-1, keepdims=True) + EPS)
        xh = xv * r
        err = xh * g_ref[...] - t_ref[...]
        loss_ref[...] += 0.5 * jnp.sum(jnp.mean(err * err, axis=-1, keepdims=True), axis=0, keepdims=True)
        dyv = err * (1.0 / d)
        dg_ref[...] += _colsum(dyv * xh)
        dxh = dyv * g_ref[...]
        dx_ref[...] = r * (dxh - xh * jnp.mean(dxh * xh, axis=-1, keepdims=True))

    return pl.pallas_call(
        body, name="loss_head", grid=(s // tm,),
        in_specs=[BS((tm, d), lambda i: (i, 0)), BS((tm, d), lambda i: (i, 0)), BS((1, d), lambda i: (0, 0))],
        out_specs=[BS((tm, d), lambda i: (i, 0)), BS((1, 1), lambda i: (0, 0)), BS((1, d), lambda i: (0, 0))],
        out_shape=[SDS((s, d), F32), SDS((1, 1), F32), SDS((1, d), F32)], compiler_params=_params())(x, target, gf)


def _in_proj_bwd(l, dres, x, norm1, w_in, du_a, dv1, dv2, du_c, dzg):
    s, d = x.shape
    nc = w_in.shape[-1]
    tm = min(256, s)
    m = MXU_DTYPE

    def body(dres_ref, x_ref, g_ref, w_ref, a_ref, b1_ref, b2_ref, c_ref, g3_ref, dx_ref, dz_ref, dn_ref):
        @pl.when(pl.program_id(0) == 0)
        def _():
            dn_ref[...] = jnp.zeros(dn_ref.shape, F32)

        dz = jnp.concatenate([a_ref[...], b1_ref[...], b2_ref[...], c_ref[...], g3_ref[...]], axis=1).astype(m)
        dz_ref[...] = dz
        dh = _mm_nt_cols(dz, w_ref)
        xv = x_ref[...]
        r = lax.rsqrt(jnp.mean(xv * xv, axis=-1, keepdims=True) + EPS)
        xh = xv * r
        dn_ref[...] += _colsum(dh * xh)
        dxh = dh * g_ref[...]
        dx_ref[...] = dres_ref[...] + r * (dxh - xh * jnp.mean(dxh * xh, axis=-1, keepdims=True))

    tile = lambda n: BS((tm, n), lambda i: (i, 0))
    return pl.pallas_call(
        body, name=f"in_proj_bwd_l{l}", grid=(s // tm,),
        in_specs=[tile(d), tile(d), BS((None, 1, d), lambda i: (l, 0, 0)),
                  BS((N_CHIPS, d, nc), lambda i: (0, 0, 0)),
                  tile(du_a.shape[1]), tile(dv1.shape[1]), tile(dv2.shape[1]), tile(du_c.shape[1]), tile(dzg.shape[1])],
        out_specs=[tile(d), tile(N_CHIPS * nc), BS((1, d), lambda i: (0, 0))],
        out_shape=[SDS((s, d), F32), SDS((s, N_CHIPS * nc), m), SDS((1, d), F32)], compiler_params=_params(),
    )(dres, x, norm1, w_in, du_a, dv1, dv2, du_c, dzg)


def _tn_matmul(name, a, a_spec, b, b_spec, chunk_shape, grid, place):
    last = grid[1] - 1

    def body(place_ref, a_ref, b_ref, own_ref, wire_ref, *acc):
        part = _mm(a_ref[...], b_ref[...])

        def emit(total):
            wire_ref[...] = total.astype(WIRE_DTYPE)

            @pl.when(pl.program_id(0) == place_ref[0])
            def _():
                own_ref[...] = total

        if last == 0:
            emit(part)
        else:
            @pl.when(pl.program_id(1) == 0)
            def _():
                acc[0][...] = part

            @pl.when(pl.program_id(1) > 0)
            def _():
                acc[0][...] += part

            @pl.when(pl.program_id(1) == last)
            def _():
                emit(acc[0][...])

    zeros = (0,) * len(chunk_shape)
    return pl.pallas_call(
        body, name=name,
        grid_spec=pltpu.PrefetchScalarGridSpec(
            num_scalar_prefetch=1, grid=grid, in_specs=[a_spec, b_spec],
            out_specs=[BS(chunk_shape, lambda j, t, pr: zeros), BS((None,) + chunk_shape, lambda j, t, pr: (j,) + zeros)],
            scratch_shapes=[pltpu.VMEM(chunk_shape, F32)] if last else []),
        out_shape=[SDS(chunk_shape, F32), SDS((N_CHIPS,) + chunk_shape, WIRE_DTYPE)],
        compiler_params=_params())(place, a, b)


def _scan_consts(pw_ref, lanes, reverse):
    sgn = -1.0 if reverse else 1.0
    row = lax.broadcasted_iota(jnp.int32, (8, lanes), 0)
    steps = []
    for i, k in enumerate((1, 2, 4)):
        mask = (row < 8 - k) if reverse else (row >= k)
        steps.append((k, jnp.where(mask, pw_ref[2 * i], 0.0), jnp.where(mask, sgn * pw_ref[2 * i + 1], 0.0)))
    c = 4 if reverse else 3
    return steps, pw_ref[2 * c], sgn * pw_ref[2 * c + 1]


def _scan_block(br, bi, steps, row, reverse):
    for k, ar, ai in steps:
        sh = 8 - k if reverse else k
        sr = pltpu.roll(br, sh, 0)
        si = pltpu.roll(bi, sh, 0)
        br, bi = br + ar * sr - ai * si, bi + ar * si + ai * sr
    return br, bi


def _ssm_fwd(l, z, bblk_re, bblk_im, cblk_re, cblk_im, pw, dskip):
    s = z.shape[0]
    gc = bblk_re.shape[1]
    gl = bblk_re.shape[2]
    nblk = bblk_re.shape[0]

    def body(u_ref, bre, bim, cre, cim, pw_ref, d_ref, hre, him, y_ref):
        u = u_ref[...]
        hre[...] = _mm(u, bre[...])
        him[...] = _mm(u, bim[...])
        row = lax.broadcasted_iota(jnp.int32, (8, gl), 0)
        steps, car, cai = _scan_consts(pw_ref, gl, False)

        def step(i, carry):
            cr, ci = carry
            r0 = pl.multiple_of(i * 8, 8)
            br, bi = _scan_block(hre[pl.ds(r0, 8), :], him[pl.ds(r0, 8), :], steps, row, False)
            hr = br + car * cr - cai * ci
            hi = bi + car * ci + cai * cr
            hre[pl.ds(r0, 8), :] = hr
            him[pl.ds(r0, 8), :] = hi
            return jnp.broadcast_to(hr[7:8, :], (8, gl)), jnp.broadcast_to(hi[7:8, :], (8, gl))

        zero = jnp.zeros((8, gl), F32)
        lax.fori_loop(0, s // 8, step, (zero, zero))
        y_ref[...] = _mm_nt(hre[...], cre[...]) - _mm_nt(him[...], cim[...]) + d_ref[...] * u

    return pl.pallas_call(
        body, name=f"ssm_fwd_l{l}", grid=(nblk,),
        in_specs=[BS((s, gc), lambda k: (0, k)), BS((None, gc, gl), lambda k: (k, 0, 0)),
                  BS((None, gc, gl), lambda k: (k, 0, 0)), BS((None, gc, gl), lambda k: (k, 0, 0)),
                  BS((None, gc, gl), lambda k: (k, 0, 0)), BS((10, 8, gl), lambda k: (0, 0, k)),
                  BS((1, gc), lambda k: (0, k))],
        out_specs=[BS((s, gl), lambda k: (0, k)), BS((s, gl), lambda k: (0, k)), BS((s, gc), lambda k: (0, k))],
        out_shape=[SDS((s, nblk * gl), F32), SDS((s, nblk * gl), F32), SDS((s, nblk * gc), F32)],
        compiler_params=_params())(z, bblk_re, bblk_im, cblk_re, cblk_im, pw, dskip)


def _ssm_bwd(l, dy, z, hre, him, bblk_re, bblk_im, cblk_re, cblk_im, pw, dskip):
    s = z.shape[0]
    nblk, gc, gl = bblk_re.shape

    def body(dy_ref, u_ref, hre_ref, him_ref, bre, bim, cre, cim, pw_ref, d_ref,
             du_ref, dbre_ref, dbim_ref, dcre_ref, dcim_ref, dar_ref, dai_ref, dd_ref, gre, gim):
        dyv = dy_ref[...]
        u = u_ref[...]
        gre[...] = _mm(dyv, cre[...])
        gim[...] = -_mm(dyv, cim[...])
        dcre_ref[...] = _mm_tn(dyv, hre_ref[...])
        dcim_ref[...] = -_mm_tn(dyv, him_ref[...])
        dd_ref[...] = _colsum(dyv * u)
        row = lax.broadcasted_iota(jnp.int32, (8, gl), 0)
        steps, car, cai = _scan_consts(pw_ref, gl, True)
        n8 = s // 8

        def step(ii, carry):
            cr, ci, accr, acci = carry
            i = n8 - 1 - ii
            r0 = pl.multiple_of(i * 8, 8)
            br, bi = _scan_block(gre[pl.ds(r0, 8), :], gim[pl.ds(r0, 8), :], steps, row, True)
            dr = br + car * cr - cai * ci
            di = bi + car * ci + cai * cr
            gre[pl.ds(r0, 8), :] = dr
            gim[pl.ds(r0, 8), :] = di
            rp = pl.multiple_of(jnp.maximum(i - 1, 0) * 8, 8)
            keep = jnp.where(i > 0, 1.0, 0.0)
            pr = jnp.where(row >= 1, pltpu.roll(hre_ref[pl.ds(r0, 8), :], 1, 0),
                           keep * pltpu.roll(hre_ref[pl.ds(rp, 8), :], 1, 0))
            pi = jnp.where(row >= 1, pltpu.roll(him_ref[pl.ds(r0, 8), :], 1, 0),
                           keep * pltpu.roll(him_ref[pl.ds(rp, 8), :], 1, 0))
            accr = accr + dr * pr + di * pi
            acci = acci + di * pr - dr * pi
            return (jnp.broadcast_to(dr[0:1, :], (8, gl)), jnp.broadcast_to(di[0:1, :], (8, gl)), accr, acci)

        zero = jnp.zeros((8, gl), F32)
        _, _, accr, acci = lax.fori_loop(0, n8, step, (zero, zero, zero, zero))
        dar_ref[...] = _colsum(accr)
        dai_ref[...] = _colsum(acci)
        dbr = gre[...]
        dbi = gim[...]
        du_ref[...] = (dyv * d_ref[...] + _mm_nt(dbr, bre[...]) + _mm_nt(dbi, bim[...])).astype(du_ref.dtype)
        dbre_ref[...] = _mm_tn(u, dbr)
        dbim_ref[...] = _mm_tn(u, dbi)

    col = lambda n: BS((s, n), lambda k: (0, k))
    blk = lambda a, b: BS((None, a, b), lambda k: (k, 0, 0))
    outs = pl.pallas_call(
        body, name=f"ssm_bwd_l{l}", grid=(nblk,),
        in_specs=[col(gc), col(gc), col(gl), col(gl), blk(gc, gl), blk(gc, gl), blk(gc, gl), blk(gc, gl),
                  BS((10, 8, gl), lambda k: (0, 0, k)), BS((1, gc), lambda k: (0, k))],
        out_specs=[col(gc), blk(gc, gl), blk(gc, gl), blk(gc, gl), blk(gc, gl), BS((1, gl), lambda k: (0, k)),
                   BS((1, gl), lambda k: (0, k)), BS((1, gc), lambda k: (0, k))],
        out_shape=[SDS((s, nblk * gc), MXU_DTYPE), SDS((nblk, gc, gl), F32), SDS((nblk, gc, gl), F32),
                   SDS((nblk, gc, gl), F32), SDS((nblk, gc, gl), F32), SDS((1, nblk * gl), F32),
                   SDS((1, nblk * gl), F32), SDS((1, nblk * gc), F32)],
        scratch_shapes=[pltpu.VMEM((s, gl), F32), pltpu.VMEM((s, gl), F32)], compiler_params=_params(),
    )(dy, z, hre, him, bblk_re, bblk_im, cblk_re, cblk_im, pw, dskip)
    return dict(zip(("du", "dbblk_re", "dbblk_im", "dcblk_re", "dcblk_im", "dabar_re", "dabar_im", "dd"), outs))


def _conv_fwd(l, z, wdw, bdw):
    s = z.shape[0]
    cw = wdw.shape[1]
    lb = 128
    tr = min(256, s)
    off1 = cw // lb
    off2 = 2 * cw // lb

    def body(v1_ref, v2_ref, w_ref, b_ref, hc_ref, scr):
        scr[0:CONV_PAD, :] = jnp.zeros((CONV_PAD, lb), F32)
        scr[CONV_PAD:, :] = v1_ref[...] * _sigmoid(v2_ref[...])
        for t in range(s // tr):
            acc = jnp.broadcast_to(b_ref[...], (tr, lb))
            for k in range(CONV_KERNEL):
                acc = acc + w_ref[pl.ds(k, 1), :] * scr[pl.ds(t * tr + CONV_PAD - (CONV_KERNEL - 1) + k, tr), :]
            hc_ref[pl.ds(t * tr, tr), :] = acc

    return pl.pallas_call(
        body, name=f"conv_fwd_l{l}", grid=(cw // lb,),
        in_specs=[BS((s, lb), lambda k: (0, off1 + k)), BS((s, lb), lambda k: (0, off2 + k)),
                  BS((CONV_KERNEL, lb), lambda k: (0, k)), BS((1, lb), lambda k: (0, k))],
        out_specs=BS((s, lb), lambda k: (0, k)), out_shape=SDS((s, cw), F32),
        scratch_shapes=[pltpu.VMEM((s + CONV_PAD, lb), F32)], compiler_params=_params())(z, z, wdw, bdw)


def _conv_bwd(l, dhc, z, wdw):
    s = z.shape[0]
    cw = wdw.shape[1]
    lb = 128
    tr = min(256, s)
    off1 = cw // lb
    off2 = 2 * cw // lb
    nb = cw // lb

    def body(d_ref, v1_ref, v2_ref, w_ref, dv1_ref, dv2_ref, dw_ref, db_ref, hpad, dpad):
        v1 = v1_ref[...]
        sg = _sigmoid(v2_ref[...])
        dv = d_ref[...]
        hpad[0:CONV_PAD, :] = jnp.zeros((CONV_PAD, lb), F32)
        hpad[CONV_PAD:, :] = v1 * sg
        dpad[0:s, :] = dv
        dpad[s:, :] = jnp.zeros((CONV_PAD, lb), F32)
        db_ref[...] = _colsum(dv)
        dws = [jnp.zeros((1, lb), F32) for _ in range(CONV_KERNEL)]
        for t in range(s // tr):
            dt = d_ref[pl.ds(t * tr, tr), :]
            acc = jnp.zeros((tr, lb), F32)
            for k in range(CONV_KERNEL):
                acc = acc + w_ref[pl.ds(k, 1), :] * dpad[pl.ds(t * tr + (CONV_KERNEL - 1) - k, tr), :]
                dws[k] = dws[k] + _colsum(dt * hpad[pl.ds(t * tr + CONV_PAD - (CONV_KERNEL - 1) + k, tr), :])
            sgt = _sigmoid(v2_ref[pl.ds(t * tr, tr), :])
            v1t = v1_ref[pl.ds(t * tr, tr), :]
            dv1_ref[pl.ds(t * tr, tr), :] = (acc * sgt).astype(dv1_ref.dtype)
            dv2_ref[pl.ds(t * tr, tr), :] = (acc * v1t * (sgt * (1.0 - sgt))).astype(dv2_ref.dtype)
        for k in range(CONV_KERNEL):
            dw_ref[pl.ds(k, 1), :] = dws[k]

    return pl.pallas_call(
        body, name=f"conv_bwd_l{l}", grid=(nb,),
        in_specs=[BS((s, lb), lambda k: (0, k)), BS((s, lb), lambda k: (0, off1 + k)),
                  BS((s, lb), lambda k: (0, off2 + k)), BS((CONV_KERNEL, lb), lambda k: (0, k))],
        out_specs=[BS((s, lb), lambda k: (0, k)), BS((s, lb), lambda k: (0, k)),
                   BS((CONV_KERNEL, lb), lambda k: (0, k)), BS((1, lb), lambda k: (0, k))],
        out_shape=[SDS((s, cw), MXU_DTYPE), SDS((s, cw), MXU_DTYPE), SDS((CONV_KERNEL, cw), F32), SDS((1, cw), F32)],
        scratch_shapes=[pltpu.VMEM((s + CONV_PAD, lb), F32), pltpu.VMEM((s + CONV_PAD, lb), F32)],
        compiler_params=_params())(dhc, z, z, wdw)


def _pool_window(k):
    return jnp.where(k == 0, float(POOL_WINDOWS[0]),
                     jnp.where(k == 1, float(POOL_WINDOWS[1]),
                               jnp.where(k == 2, float(POOL_WINDOWS[2]), float(POOL_WINDOWS[3]))))


def _pool_fwd(l, z, pw_width):
    s = z.shape[0]
    lb = pw_width // len(POOL_WINDOWS)
    off = 3 * pw_width // lb

    def body(u_ref, p_ref):
        k = pl.program_id(0)
        u = u_ref[...]
        row = lax.broadcasted_iota(jnp.int32, (s, lb), 0)
        sums = [u]
        for sh in (1, 2, 4, 8):
            prev = sums[-1]
            sums.append(prev + jnp.where(row >= sh, pltpu.roll(prev, sh, 0), 0.0))
        sel = jnp.where(k == 0, sums[1], jnp.where(k == 1, sums[2], jnp.where(k == 2, sums[3], sums[4])))
        cnt = jnp.minimum((row + 1).astype(F32), _pool_window(k))
        p_ref[...] = sel / cnt - u

    return pl.pallas_call(
        body, name=f"pool_fwd_l{l}", grid=(len(POOL_WINDOWS),),
        in_specs=[BS((s, lb), lambda k: (0, off + k))], out_specs=BS((s, lb), lambda k: (0, k)),
        out_shape=SDS((s, pw_width), F32), compiler_params=_params())(z)


def _pool_bwd(l, dp):
    s, width = dp.shape
    lb = width // len(POOL_WINDOWS)

    def body(d_ref, du_ref):
        k = pl.program_id(0)
        dv = d_ref[...]
        row = lax.broadcasted_iota(jnp.int32, (s, lb), 0)
        cnt = jnp.minimum((row + 1).astype(F32), _pool_window(k))
        sums = [dv / cnt]
        for sh in (1, 2, 4, 8):
            prev = sums[-1]
            sums.append(prev + jnp.where(row < s - sh, pltpu.roll(prev, s - sh, 0), 0.0))
        sel = jnp.where(k == 0, sums[1], jnp.where(k == 1, sums[2], jnp.where(k == 2, sums[3], sums[4])))
        du_ref[...] = (sel - dv).astype(du_ref.dtype)

    return pl.pallas_call(
        body, name=f"pool_bwd_l{l}", grid=(len(POOL_WINDOWS),),
        in_specs=[BS((s, lb), lambda k: (0, k))], out_specs=BS((s, lb), lambda k: (0, k)),
        out_shape=SDS((s, width), MXU_DTYPE), compiler_params=_params())(dp)


def _zoh(a_re, a_im, log_dt):
    dt = jnp.exp(log_dt)
    mag = jnp.exp(dt * a_re)
    ang = dt * a_im
    abar_re = mag * jnp.cos(ang)
    abar_im = mag * jnp.sin(ang)
    den = a_re * a_re + a_im * a_im
    nr = abar_re - 1.0
    ni = abar_im
    f_re = (nr * a_re + ni * a_im) / den
    f_im = (ni * a_re - nr * a_im) / den
    return abar_re, abar_im, f_re, f_im


def _zoh_fwd(l, a_re, a_im, log_dt):
    def body(ar, ai, ld, o0, o1, o2, o3):
        for ref, val in zip((o0, o1, o2, o3), _zoh(ar[...], ai[...], ld[...])):
            ref[...] = val

    return pl.pallas_call(body, name=f"zoh_fwd_l{l}", out_shape=[SDS(a_re.shape, F32)] * 4)(a_re, a_im, log_dt)


def _zoh_bwd(l, a_re, a_im, log_dt, cts):
    def body(ar, ai, ld, c0, c1, c2, c3, dar, dai, dld):
        _, vjp = jax.vjp(_zoh, ar[...], ai[...], ld[...])
        g = vjp((c0[...], c1[...], c2[...], c3[...]))
        dar[...] = g[0]
        dai[...] = g[1]
        dld[...] = g[2]

    return pl.pallas_call(body, name=f"zoh_bwd_l{l}",
                          out_shape=[SDS(a_re.shape, F32), SDS(a_re.shape, F32), SDS(log_dt.shape, F32)],
                          )(a_re, a_im, log_dt, *cts)


def _bbar_fwd(l, f_re, f_im, b_re, b_im):
    g, p, n = b_re.shape[1:]

    def body(fr, fi, br, bi, o_re, o_im):
        o_re[...] = (fr[...] * br[...] - fi[...] * bi[...]).astype(o_re.dtype)
        o_im[...] = (fr[...] * bi[...] + fi[...] * br[...]).astype(o_im.dtype)

    whole = lambda shp: BS(shp, lambda i: (0,) * len(shp))
    layer = BS((None, g, p, n), lambda i: (l, 0, 0, 0))
    return pl.pallas_call(body, name=f"bbar_fwd_l{l}", grid=(1,),
                          in_specs=[whole((g, 1, n)), whole((g, 1, n)), layer, layer],
                          out_specs=[whole((g, p, n))] * 2,
                          out_shape=[SDS((g, p, n), MXU_DTYPE)] * 2)(f_re, f_im, b_re, b_im)


def _bbar_bwd(l, f_re, f_im, b_re, b_im, d_re, d_im):
    g, p, n = b_re.shape[1:]

    def body(fr, fi, br, bi, dr, di, dfr, dfi, dbr, dbi):
        dfr[...] = jnp.sum(dr[...] * br[...] + di[...] * bi[...], axis=1, keepdims=True)
        dfi[...] = jnp.sum(di[...] * br[...] - dr[...] * bi[...], axis=1, keepdims=True)
        dbr[...] = fr[...] * dr[...] + fi[...] * di[...]
        dbi[...] = fr[...] * di[...] - fi[...] * dr[...]

    whole = lambda shp: BS(shp, lambda i: (0,) * len(shp))
    layer = BS((None, g, p, n), lambda i: (l, 0, 0, 0))
    return pl.pallas_call(body, name=f"bbar_bwd_l{l}", grid=(1,),
                          in_specs=[whole((g, 1, n)), whole((g, 1, n)), layer, layer, whole((g, p, n)),
                                    whole((g, p, n))],
                          out_specs=[whole((g, 1, n)), whole((g, 1, n)), whole((g, p, n)), whole((g, p, n))],
                          out_shape=[SDS((g, 1, n), F32), SDS((g, 1, n), F32), SDS((g, p, n), F32),
                                     SDS((g, p, n), F32)])(f_re, f_im, b_re, b_im, d_re, d_im)


def _powers(l, abar_re, abar_im):
    lanes = abar_re.shape[1]

    def body(ar_ref, ai_ref, o_ref):
        ar, ai = ar_ref[...], ai_ref[...]
        pows = [(ar, ai)]
        for _ in range(7):
            pr, pi = pows[-1]
            pows.append((pr * ar - pi * ai, pr * ai + pi * ar))
        row = lax.broadcasted_iota(jnp.int32, (8, lanes), 0)
        for i, k in enumerate((1, 2, 4)):
            o_ref[2 * i] = jnp.broadcast_to(pows[k - 1][0], (8, lanes))
            o_ref[2 * i + 1] = jnp.broadcast_to(pows[k - 1][1], (8, lanes))
        for slot, order in ((3, range(8)), (4, range(7, -1, -1))):
            vr = jnp.zeros((8, lanes), F32)
            vi = jnp.zeros((8, lanes), F32)
            for r, e in enumerate(order):
                vr = jnp.where(row == r, pows[e][0], vr)
                vi = jnp.where(row == r, pows[e][1], vi)
            o_ref[2 * slot] = vr
            o_ref[2 * slot + 1] = vi

    return pl.pallas_call(body, name=f"powers_l{l}", out_shape=SDS((10, 8, lanes), F32))(abar_re, abar_im)


def _block_diag(v):
    g, a, b = v.shape
    eye = jnp.eye(8, dtype=v.dtype)
    out = jnp.einsum("kgab,gh->kgahb", v.reshape(g // 8, 8, a, b), eye)
    return out.reshape(g // 8, 8 * a, 8 * b)


def _block_diag_extract(blk, a, b):
    n = blk.shape[0]
    v = blk.reshape(n, 8, a, 8, b)
    return jnp.einsum("kgahb,gh->kgab", v, jnp.eye(8, dtype=blk.dtype)).reshape(n * 8, a, b)


def _ssm_prepare(l, prm):
    g, n, p = SSM_GROUPS, SSM_STATE, SSM_GROUP
    a_re, a_im = prm["ssm_a_re"][l], prm["ssm_a_im"][l]
    log_dt = prm["ssm_log_dt"][l].reshape(g, 1)
    abar_re, abar_im, f_re, f_im = _zoh_fwd(l, a_re, a_im, log_dt)
    f_re, f_im = f_re.reshape(g, 1, n), f_im.reshape(g, 1, n)
    bbar_re, bbar_im = _bbar_fwd(l, f_re, f_im, prm["ssm_b_re"], prm["ssm_b_im"])
    pw = _powers(l, abar_re.reshape(1, g * n), abar_im.reshape(1, g * n))
    return dict(a_re=a_re, a_im=a_im, log_dt=log_dt, f_re=f_re, f_im=f_im,
                bblk_re=_block_diag(bbar_re), bblk_im=_block_diag(bbar_im),
                cblk_re=_block_diag(prm["ssm_c_re"][l].astype(MXU_DTYPE)),
                cblk_im=_block_diag(prm["ssm_c_im"][l].astype(MXU_DTYPE)), pw=pw,
                dskip=prm["ssm_d"][l].reshape(1, g * p))


def _ssm_param_grads(l, sd, r, prm):
    g, n, p = SSM_GROUPS, SSM_STATE, SSM_GROUP
    dbbar_re = _block_diag_extract(r["dbblk_re"], p, n)
    dbbar_im = _block_diag_extract(r["dbblk_im"], p, n)
    dfr, dfi, db_re, db_im = _bbar_bwd(l, sd["f_re"], sd["f_im"], prm["ssm_b_re"], prm["ssm_b_im"], dbbar_re, dbbar_im)
    cts = (r["dabar_re"].reshape(g, n), r["dabar_im"].reshape(g, n), dfr.reshape(g, n), dfi.reshape(g, n))
    da_re, da_im, dlog_dt = _zoh_bwd(l, sd["a_re"], sd["a_im"], sd["log_dt"], cts)
    return dict(ssm_a_re=da_re, ssm_a_im=da_im, ssm_log_dt=dlog_dt.reshape(g), ssm_b_re=db_re, ssm_b_im=db_im,
                ssm_c_re=_block_diag_extract(r["dcblk_re"], p, n), ssm_c_im=_block_diag_extract(r["dcblk_im"], p, n),
                ssm_d=r["dd"].reshape(g, p))


def _ffn_weight_grads(l, fb, dx2, s, place):
    d = dx2.shape[1]
    hcn = fb["act"].shape[1]
    g = {}
    for name, key, rhs, ts in (("ffn_w_gate", "dgate", fb["h2"], s), ("ffn_w_up", "dup", fb["h2"], s),
                               ("ffn_w_down", "act", dx2, min(1024, s))):
        g[name] = _tn_matmul(f"d{name}_l{l}", fb[key], BS((None, hcn, ts), lambda j, t, pr: (j, 0, t)), rhs,
                             BS((ts, d), lambda j, t, pr: (t, 0)), (hcn, d), (N_CHIPS, s // ts), place)
    return g


def _in_weight_grad(l, ht, dz, place):
    d, s = ht.shape
    ncw = dz.shape[1] // N_CHIPS
    return _tn_matmul(f"dw_in_l{l}", ht, BS((d, s), lambda j, t, pr: (0, 0)), dz, BS((s, ncw), lambda j, t, pr: (0, j)),
                      (d, ncw), (N_CHIPS, 1), place)


def _fused_tn(name, pairs, kinds, s, place):
    ts = min(512, s)
    n = len(pairs)

    def shape_of(a, b, kind):
        k, m = a.shape[1], b.shape[1]
        if kind == "rows":
            return (N_CHIPS, k // N_CHIPS, m)
        if kind == "cols":
            return (N_CHIPS, k, m // N_CHIPS)
        return (k // 128, 128, 128)

    shapes = [shape_of(a, b, kind) for (a, b), kind in zip(pairs, kinds)]
    last = s // ts - 1
    out_shape = []
    for shp, kind in zip(shapes, kinds):
        out_shape += [SDS(shp, F32)] if kind == "groups" else [SDS(shp[1:], F32), SDS(shp, WIRE_DTYPE)]

    def body(place_ref, *refs):
        ins, outs, accs = refs[:2 * n], refs[2 * n:2 * n + len(out_shape)], refs[2 * n + len(out_shape):]
        first = pl.program_id(0) == 0
        for i, kind in enumerate(kinds):
            a, b = ins[2 * i][...], ins[2 * i + 1][...]
            acc = accs[i]

            @pl.when(first)
            def _():
                acc[...] = jnp.zeros(acc.shape, F32)

            if kind == "rows":
                acc[...] += _mm_tn(a, b).reshape(acc.shape)
            elif kind == "cols":
                full = _mm_tn(a, b)
                nc = acc.shape[2]
                for j in range(N_CHIPS):
                    acc[j] += full[:, j * nc:(j + 1) * nc]
            else:
                for k in range(acc.shape[0]):
                    acc[k] += _mm_tn(a[:, k * 128:(k + 1) * 128], b[:, k * 128:(k + 1) * 128])

        @pl.when(pl.program_id(0) == last)
        def _():
            o = 0
            for i, kind in enumerate(kinds):
                if kind == "groups":
                    outs[o][...] = accs[i][...]
                    o += 1
                else:
                    outs[o][...] = accs[i][place_ref[0]]
                    outs[o + 1][...] = accs[i][...].astype(WIRE_DTYPE)
                    o += 2

    whole = lambda shp: BS(shp, lambda t, pr: (0,) * len(shp))
    outs = pl.pallas_call(
        body, name=name,
        grid_spec=pltpu.PrefetchScalarGridSpec(
            num_scalar_prefetch=1, grid=(s // ts,),
            in_specs=[BS((ts, v.shape[1]), lambda t, pr: (t, 0)) for pair in pairs for v in pair],
            out_specs=[whole(o.shape) for o in out_shape],
            scratch_shapes=[pltpu.VMEM(shp, F32) for shp in shapes]),
        out_shape=out_shape, compiler_params=_params(),
    )(place, *[v for pair in pairs for v in pair])
    res, o = [], 0
    for kind in kinds:
        if kind == "groups":
            res.append(outs[o])
            o += 1
        else:
            res.append((outs[o], outs[o + 1]))
            o += 2
    return res


def _mixer_weight_grads(l, sv, mb, dx1, s, place):
    g = {}
    (g["w_out"], g["ssm_w_glu"]) = _fused_tn(f"dw_out_glu_l{l}", [(mb["merged"], dx1), (mb["ge"], mb["dt"])],
                                            ("rows", "rows"), s, place)
    (g["ssm_w_proj"], g["conv_w_proj"], g["pool_w_proj"]) = _fused_tn(
        f"dw_proj_l{l}", [(mb["sa"], mb["dya"]), (mb["ac"], mb["dyb"]), (mb["pp"], mb["dyc"])],
        ("cols", "cols", "cols"), s, place)
    (dwgrp,) = _fused_tn(f"dpool_w_group_l{l}", [(sv["p"], mb["dq"])], ("groups",), s, place)
    return g, dwgrp


def _local_step(x, target, weights_of, prm, place, on_grads=None):
    s, d = x.shape
    cw = prm["ssm_b_glu"].shape[1]
    sp = {k: prm[k].reshape(N_LAYERS, 1, -1) for k in ("norm1", "norm2", "b_gate", "ssm_b_glu", "conv_ln_g", "conv_ln_b",
                                                        "pool_scale", "conv_b_dw")}
    sp["pool_w_group"] = prm["pool_w_group"]
    saved = []
    xin = x
    for l in range(N_LAYERS):
        fw = weights_of(l, "in", (xin,))
        sd = _ssm_prepare(l, prm)
        z, h = _in_proj(l, xin, sp["norm1"], fw["w_in"])
        hre, him, y = _ssm_fwd(l, z, sd["bblk_re"], sd["bblk_im"], sd["cblk_re"], sd["cblk_im"], sd["pw"], sd["dskip"])
        p = _pool_fwd(l, z, cw)
        fw.update(weights_of(l, "mixer", (y, p)))
        wdw = fw["conv_w_dw"]
        hc = _conv_fwd(l, z, wdw, sp["conv_b_dw"][l])
        x1 = _merge_fwd(l, xin, y, hc, p, z, fw, sp)
        fw.update(weights_of(l, "ffn", (x1,)))
        x2 = _ffn_fwd(l, x1, sp["norm2"], fw["ffn_w_gate"], fw["ffn_w_up"], fw["ffn_w_down"])
        saved.append(dict(x=xin, z=z, h=h, hre=hre, him=him, y=y, hc=hc, p=p, x1=x1, sd=sd, wdw=wdw, fw=fw))
        xin = x2
    dx, loss, dfinal = _loss_head(xin, target, prm["final_norm"].reshape(1, d))
    big = [None] * N_LAYERS
    small = [None] * N_LAYERS
    norm2_rows = sp["norm2"]
    started = (lambda l, group, grads: on_grads(l, group, grads)) if on_grads is not None else (lambda *a: 0.0)
    for l in reversed(range(N_LAYERS)):
        sv = saved[l]
        sd, fw = sv["sd"], sv["fw"]
        fb = _ffn_bwd(l, sv["x1"], dx, norm2_rows, fw["ffn_w_gate"], fw["ffn_w_up"], fw["ffn_w_down"])
        big[l] = _ffn_weight_grads(l, fb, dx, s, place)
        spl = dict(sp, ssm_b_glu=sp["ssm_b_glu"] + started(l, "ffn", big[l]))
        mb = _merge_bwd(l, fb["dx1"], sv["y"], sv["hc"], sv["p"], sv["z"], fw, spl)
        mixer, dwgrp = _mixer_weight_grads(l, sv, mb, fb["dx1"], s, place)
        big[l].update(mixer)
        wdw = sv["wdw"] + started(l, "mixer", mixer)
        du_c = _pool_bwd(l, mb["dp"])
        dv1, dv2, dwdw, dbdw = _conv_bwd(l, mb["dhc"], sv["z"], wdw)
        sr = _ssm_bwd(l, mb["dy"], sv["z"], sv["hre"], sv["him"], sd["bblk_re"], sd["bblk_im"], sd["cblk_re"],
                      sd["cblk_im"], sd["pw"], sd["dskip"])
        dx, dz, dnorm1 = _in_proj_bwd(l, fb["dx1"], sv["x"], sp["norm1"], fw["w_in"], sr["du"], dv1, dv2, du_c, mb["dzg"])
        w_in_grad = {"w_in": _in_weight_grad(l, sv["h"], dz, place)}
        big[l].update(w_in_grad)
        sg = _ssm_param_grads(l, sd, sr, prm)
        sg.update(norm1=dnorm1.reshape(d), b_gate=mb["db_gate"].reshape(3 * d), ssm_b_glu=mb["db_glu"].reshape(cw),
                  conv_b_dw=dbdw.reshape(cw), conv_ln_g=mb["dln_g"].reshape(cw), conv_ln_b=mb["dln_b"].reshape(cw),
                  pool_w_group=dwgrp, pool_scale=mb["dscale"].reshape(cw), norm2=fb["dnorm2"].reshape(d),
                  conv_w_dw=dwdw)
        small[l] = sg
        if l == N_LAYERS - 1:
            sg = dict(sg, final_norm=dfinal.reshape(d))
        norm2_rows = sp["norm2"] + (started(l, "in", w_in_grad) + started(l, "small", sg))
    return loss[0, 0], dx, big, small, dfinal.reshape(d)


def _place():
    return lax.axis_index("x"), lax.axis_index("y"), lax.axis_index("c")


def _other_chips(x, y):
    return [(1 - x, y), (x, 1 - y), (1 - x, 1 - y)]


def _remote(src, dst, send_sem, recv_sem, device):
    return pltpu.make_async_remote_copy(src_ref=src, dst_ref=dst, send_sem=send_sem, recv_sem=recv_sem,
                                        device_id=device, device_id_type=MESH)


def _hbm(v):
    return pltpu.with_memory_space_constraint(v, pltpu.HBM)


def _cast_into(name, w, place, dtype, after=()):
    nl, k, n = w.shape
    tr = _row_tile(k, n)
    nt = k // tr

    def body(place_ref, w_ref, *rest):
        o0_ref, o1_ref = rest[len(after):]

        @pl.when(pl.program_id(0) == 0)
        def _():
            o0_ref[...] = w_ref[...].astype(dtype)

        @pl.when(pl.program_id(0) == 1)
        def _():
            o1_ref[...] = w_ref[...].astype(dtype)

    return pl.pallas_call(
        body, name=f"cast_{name}",
        grid_spec=pltpu.PrefetchScalarGridSpec(
            num_scalar_prefetch=1, grid=(nl, nt),
            in_specs=[BS((None, tr, n), lambda l, t, pr: (l, t, 0))] + [ANY] * len(after),
            out_specs=[BS((None, tr, n), lambda l, t, pr: (pr[0], t * (1 - l) + (nt - 1) * l, 0)),
                       BS((None, tr, n), lambda l, t, pr: (pr[0], t * l, 0))]),
        out_shape=[SDS((N_CHIPS, k, n), dtype)] * 2)(place, w, *after)


def _gather_rows(buf, c):
    k = buf.shape[1]
    if k % 2:
        return pl.ds(0, k)
    return pl.ds(pl.multiple_of(c * (k // 2), 8), k // 2)


def _allgather_start(tag, groups):
    ng = len(groups)
    sizes = [len(g) for g in groups]
    first = [sum(sizes[:g]) for g in range(ng)]
    flat = [b for g in groups for b in g]
    nb = len(flat)

    def body(*refs):
        ins = refs[:nb]
        sems = refs[nb:nb + 2 * ng]
        token = refs[-1]
        x, y, c = _place()
        jme = 2 * x + y
        for g in range(ng):
            for a in range(sizes[g]):
                buf = ins[first[g] + a]
                blk = buf.at[jme, _gather_rows(buf, c)]
                for k, (cx, cy) in enumerate(_other_chips(x, y)):
                    _remote(blk, blk, sems[2 * g].at[3 * a + k], sems[2 * g + 1].at[3 * a + k], (cx, cy, c)).start()
        token[...] = jnp.zeros(token.shape, F32)

    sem_shapes = [pltpu.SemaphoreType.DMA((3 * sizes[g // 2],)) for g in range(2 * ng)]
    outs = pl.pallas_call(
        body, name=f"allgather_start_{tag}", in_specs=[HBM] * nb,
        out_specs=[SEM] * (2 * ng) + [HBM] * nb + [pl.BlockSpec(memory_space=pltpu.VMEM)],
        out_shape=sem_shapes + [pltpu.HBM(b.shape, b.dtype) for b in flat] + [SDS((8, 128), F32)],
        input_output_aliases={i: 2 * ng + i for i in range(nb)},
        compiler_params=pltpu.CompilerParams(has_side_effects=SIDE_EFFECT))(*[_hbm(b) for b in flat])
    per_group = [(outs[2 * g], outs[2 * g + 1], outs[2 * ng + first[g]:2 * ng + first[g] + sizes[g]])
                 for g in range(ng)]
    return per_group, outs[-1]


def _allgather_wait(l, send_sems, recv_sems, bufs, after):
    n = len(bufs)

    def body(*refs):
        ins = refs[:n]
        ssem, rsem = refs[n], refs[n + 1]
        x, y, c = _place()
        jme = 2 * x + y
        for a in range(n):
            rows = _gather_rows(ins[a], c)
            for k, (cx, cy) in enumerate(_other_chips(x, y)):
                cp = _remote(ins[a].at[jme, rows], ins[a].at[2 * cx + cy, rows], ssem.at[3 * a + k],
                             rsem.at[3 * a + k], (cx, cy, c))
                cp.wait_send()
                cp.wait_recv()

    return pl.pallas_call(
        body, name=f"allgather_wait_{l}", in_specs=[HBM] * n + [SEM, SEM] + [ANY] * len(after), out_specs=[HBM] * n,
        out_shape=[pltpu.HBM(b.shape, b.dtype) for b in bufs], input_output_aliases={i: i for i in range(n)},
        compiler_params=pltpu.CompilerParams(has_side_effects=SIDE_EFFECT))(*bufs, send_sems, recv_sems, *after)


def _allgather_forward(l, bufs):
    n = len(bufs)
    split = [a for a in range(n) if bufs[a].shape[1] % 2 == 0]

    def body(*refs):
        ins = refs[:n]
        send_sems, recv_sems = refs[2 * n:]
        x, y, c = _place()
        sibling = (x, y, 1 - c)
        copies = []
        for a in split:
            for k, (cx, cy) in enumerate(_other_chips(x, y)):
                blk = ins[a].at[2 * cx + cy, _gather_rows(ins[a], c)]
                cp = _remote(blk, blk, send_sems.at[a, k], recv_sems.at[a, k], sibling)
                cp.start()
                copies.append(cp)
        for a in split:
            for k, (cx, cy) in enumerate(_other_chips(x, y)):
                blk = ins[a].at[2 * cx + cy, _gather_rows(ins[a], 1 - c)]
                _remote(blk, blk, send_sems.at[a, k], recv_sems.at[a, k], sibling).wait_recv()
        for cp in copies:
            cp.wait_send()

    sem = pltpu.SemaphoreType.DMA((n, 3))
    return pl.pallas_call(
        body, name=f"allgather_forward_{l}", in_specs=[ANY] * n, out_specs=[ANY] * n,
        out_shape=[SDS(b.shape, b.dtype) for b in bufs], input_output_aliases={i: i for i in range(n)},
        scratch_shapes=[sem, sem])(*bufs)


def _rs_to_owner(l, parts):
    n = len(parts)
    lands = [lax.empty((3,) + p.shape[1:], p.dtype) for p in parts]

    def body(*refs):
        ins, zones = refs[:n], refs[n:2 * n]
        send_sems, recv_sems = refs[2 * n], refs[2 * n + 1]
        token = refs[-1]
        x, y, c = _place()
        for a in range(n):
            for k, (cx, cy) in enumerate(_other_chips(x, y)):
                _remote(ins[a].at[2 * cx + cy], zones[a].at[k], send_sems.at[3 * a + k], recv_sems.at[3 * a + k],
                        (cx, cy, c)).start()
        token[...] = jnp.zeros(token.shape, F32)

    sem = pltpu.SemaphoreType.DMA((3 * n,))
    outs = pl.pallas_call(
        body, name=f"rs_to_owner_start_{l}", in_specs=[HBM] * (2 * n),
        out_specs=[SEM, SEM] + [HBM] * (2 * n) + [pl.BlockSpec(memory_space=pltpu.VMEM)],
        out_shape=[sem, sem] + [pltpu.HBM(p.shape, p.dtype) for p in parts]
        + [pltpu.HBM(z.shape, z.dtype) for z in lands] + [SDS((8, 128), F32)],
        input_output_aliases={i: 2 + i for i in range(2 * n)},
        compiler_params=pltpu.CompilerParams(has_side_effects=SIDE_EFFECT),
    )(*[_hbm(p) for p in parts], *[_hbm(z) for z in lands])
    return outs[0], outs[1], outs[2:2 + n], outs[2 + n:2 + 2 * n], outs[-1]


def _rs_to_owner_wait(l, send_sems, recv_sems, parts, lands, after):
    n = len(parts)

    def body(*refs):
        ins, zones = refs[:n], refs[n:2 * n]
        ssem, rsem = refs[2 * n], refs[2 * n + 1]
        x, y, c = _place()
        for a in range(n):
            for k, (cx, cy) in enumerate(_other_chips(x, y)):
                cp = _remote(ins[a].at[2 * cx + cy], zones[a].at[k], ssem.at[3 * a + k], rsem.at[3 * a + k],
                             (cx, cy, c))
                cp.wait_send()
                cp.wait_recv()

    outs = pl.pallas_call(
        body, name=f"rs_to_owner_wait_{l}", in_specs=[HBM] * (2 * n) + [SEM, SEM] + [ANY] * len(after),
        out_specs=[HBM] * (2 * n),
        out_shape=[pltpu.HBM(p.shape, p.dtype) for p in parts] + [pltpu.HBM(z.shape, z.dtype) for z in lands],
        input_output_aliases={i: i for i in range(2 * n)},
        compiler_params=pltpu.CompilerParams(has_side_effects=SIDE_EFFECT),
    )(*parts, *lands, send_sems, recv_sems, *after)
    return outs[:n], outs[n:]


def _rs_sibling_exchange(l, both):
    n = len(both)

    def body(*refs):
        ins = refs[:n]
        send_sems, recv_sems = refs[2 * n:]
        x, y, c = _place()
        copies = []
        for a in range(n):
            cp = _remote(ins[a].at[c], ins[a].at[c], send_sems.at[a], recv_sems.at[a], (x, y, 1 - c))
            cp.start()
            copies.append(cp)
        for a, cp in enumerate(copies):
            cp.wait_send()
            _remote(ins[a].at[1 - c], ins[a].at[1 - c], send_sems.at[a], recv_sems.at[a], (x, y, 1 - c)).wait_recv()

    sem = pltpu.SemaphoreType.DMA((n,))
    return pl.pallas_call(
        body, name=f"rs_sibling_exchange_{l}", in_specs=[ANY] * n, out_specs=[ANY] * n,
        out_shape=[SDS(b.shape, b.dtype) for b in both], input_output_aliases={i: i for i in range(n)},
        scratch_shapes=[sem, sem])(*both)


def _add_owner(name, grad, recv, place):
    r, cols = grad.shape
    tr = _row_tile(r, cols, budget=1024 * 1024)
    nt = r // tr

    def body(place_ref, g_ref, r_ref, o_ref):
        acc = ((g_ref[...] + r_ref[0].astype(F32)) + r_ref[1].astype(F32)) + r_ref[2].astype(F32)
        o_ref[...] = acc.astype(o_ref.dtype)

    return pl.pallas_call(
        body, name=name,
        grid_spec=pltpu.PrefetchScalarGridSpec(
            num_scalar_prefetch=1, grid=(nt,),
            in_specs=[BS((tr, cols), lambda t, pr: (t, 0)), BS((3, tr, cols), lambda t, pr: (0, t, 0))],
            out_specs=BS((None, tr, cols), lambda t, pr: (pr[1], t, 0))),
        out_shape=SDS((2, r, cols), WIRE_DTYPE))(place, grad, recv)


def _reduce_start(tag, grads):
    names = list(grads)
    send_sems, recv_sems, wires, lands, token = _rs_to_owner(tag, [grads[n][1] for n in names])
    return dict(tag=tag, names=names, send_sems=send_sems, recv_sems=recv_sems, wires=wires, lands=lands,
                grads=[grads[n][0] for n in names]), token


def _reduce_finish(pending, place, after):
    tag, names = pending["tag"], pending["names"]
    _, lands = _rs_to_owner_wait(tag, pending["send_sems"], pending["recv_sems"], pending["wires"],
                                 pending["lands"], after)
    mine = [_add_owner(f"rs_add_owner_{n}_{tag}", g, r, place) for n, g, r in zip(names, pending["grads"], lands)]
    return dict(zip(names, _rs_sibling_exchange(tag, mine)))


def _small_peers(x, y, c):
    return [(x, y, 1 - c)] + [(cx, cy, c) for cx, cy in _other_chips(x, y)]


def _allgather_rows_start(tag, bufs):
    n = len(bufs)
    lands = [lax.empty((8,) + b.shape, b.dtype) for b in bufs]

    def body(*refs):
        ins, zones = refs[:n], refs[n:2 * n]
        send_sems, recv_sems = refs[2 * n], refs[2 * n + 1]
        token = refs[-1]
        x, y, c = _place()
        for a in range(n):
            for i, peer in enumerate(_small_peers(x, y, c)):
                _remote(ins[a], zones[a].at[4 * x + 2 * y + c], send_sems.at[4 * a + i], recv_sems.at[4 * a + i],
                        peer).start()
        token[...] = jnp.zeros(token.shape, F32)

    sem = pltpu.SemaphoreType.DMA((4 * n,))
    outs = pl.pallas_call(
        body, name=f"allgather_small_start_{tag}", in_specs=[HBM] * (2 * n),
        out_specs=[SEM, SEM] + [HBM] * (2 * n) + [pl.BlockSpec(memory_space=pltpu.VMEM)],
        out_shape=[sem, sem] + [pltpu.HBM(b.shape, b.dtype) for b in bufs]
        + [pltpu.HBM(z.shape, z.dtype) for z in lands] + [SDS((8, 128), F32)],
        input_output_aliases={i: 2 + i for i in range(2 * n)},
        compiler_params=pltpu.CompilerParams(has_side_effects=SIDE_EFFECT),
    )(*[_hbm(b) for b in bufs], *[_hbm(z) for z in lands])
    return outs[0], outs[1], outs[2:2 + n], outs[2 + n:2 + 2 * n], outs[-1]


def _allgather_rows_wait(tag, send_sems, recv_sems, bufs, lands, after):
    n = len(bufs)

    def body(*refs):
        ins, zones = refs[:n], refs[n:2 * n]
        ssem, rsem = refs[2 * n], refs[2 * n + 1]
        x, y, c = _place()
        for a in range(n):
            for i, (px, py, pc) in enumerate(_small_peers(x, y, c)):
                cp = _remote(ins[a], zones[a].at[4 * px + 2 * py + pc], ssem.at[4 * a + i], rsem.at[4 * a + i],
                             (px, py, pc))
                cp.wait_send()
                cp.wait_recv()

    outs = pl.pallas_call(
        body, name=f"allgather_small_wait_{tag}", in_specs=[HBM] * (2 * n) + [SEM, SEM, ANY],
        out_specs=[HBM] * (2 * n),
        out_shape=[pltpu.HBM(b.shape, b.dtype) for b in bufs] + [pltpu.HBM(z.shape, z.dtype) for z in lands],
        input_output_aliases={i: i for i in range(2 * n)},
        compiler_params=pltpu.CompilerParams(has_side_effects=SIDE_EFFECT),
    )(*bufs, *lands, send_sems, recv_sems, after)
    return outs[:n], outs[n:]


def _allgather_rows_forward(tag, lands):
    n = len(lands)

    def body(*refs):
        ins = refs[:n]
        send_sems, recv_sems = refs[2 * n:]
        x, y, c = _place()
        sibling = (x, y, 1 - c)
        copies = []
        for a in range(n):
            for k, (cx, cy) in enumerate(_other_chips(x, y)):
                blk = ins[a].at[4 * cx + 2 * cy + c]
                cp = _remote(blk, blk, send_sems.at[a, k], recv_sems.at[a, k], sibling)
                cp.start()
                copies.append(cp)
        for a in range(n):
            for k, (cx, cy) in enumerate(_other_chips(x, y)):
                blk = ins[a].at[4 * cx + 2 * cy + 1 - c]
                _remote(blk, blk, send_sems.at[a, k], recv_sems.at[a, k], sibling).wait_recv()
        for cp in copies:
            cp.wait_send()

    sem = pltpu.SemaphoreType.DMA((n, 3))
    return pl.pallas_call(body, name=f"allgather_small_forward_{tag}", in_specs=[ANY] * n, out_specs=[ANY] * n,
                          out_shape=[SDS(z.shape, z.dtype) for z in lands],
                          input_output_aliases={i: i for i in range(n)}, scratch_shapes=[sem, sem])(*lands)


def _sum_devices(tag, gathered, mine, place):
    _, r, cols = gathered.shape
    tr = _row_tile(r, cols, budget=256 * 1024)

    def body(place_ref, g_ref, x_ref, o_ref):
        me = 2 * place_ref[0] + place_ref[1]
        acc = jnp.where(me == 0, x_ref[...], g_ref[0])
        for k in range(1, 8):
            acc = acc + jnp.where(me == k, x_ref[...], g_ref[k])
        o_ref[...] = acc

    return pl.pallas_call(
        body, name=f"sum_small_grads_{tag}",
        grid_spec=pltpu.PrefetchScalarGridSpec(
            num_scalar_prefetch=1, grid=(r // tr,),
            in_specs=[BS((8, tr, cols), lambda t, pr: (0, t, 0)), BS((tr, cols), lambda t, pr: (t, 0))],
            out_specs=BS((tr, cols), lambda t, pr: (t, 0))),
        out_shape=SDS((r, cols), F32))(place, gathered, mine)


def _adamw_values(w, g, m, v):
    m = ADAM_B1 * m + (1.0 - ADAM_B1) * g
    v = ADAM_B2 * v + (1.0 - ADAM_B2) * (g * g)
    m_hat = m / (1.0 - ADAM_B1 ** ADAM_STEP)
    v_hat = v / (1.0 - ADAM_B2 ** ADAM_STEP)
    delta = -ADAM_LR * (m_hat / (jnp.sqrt(v_hat) + ADAM_EPS) + ADAM_WD * w)
    return delta, m, v


def _adamw_big(name, l, w, m, v, g, earlier=None, after=()):
    nl, r, cols = w.shape
    tr = _row_tile(r, cols, budget=1024 * 1024)
    nt = r // tr
    n_prev = 0 if earlier is None else 4

    def body(*refs):
        w_ref, m_ref, v_ref, g_ref = refs[:4]
        go_ref, d_ref, mo_ref, vo_ref = refs[4 + n_prev + len(after):]
        gv = g_ref[0].astype(F32) + g_ref[1].astype(F32)
        delta, m_new, v_new = _adamw_values(w_ref[...], gv, m_ref[...], v_ref[...])
        go_ref[...] = gv
        d_ref[...] = delta
        mo_ref[...] = m_new
        vo_ref[...] = v_new

    layer = BS((None, tr, cols), lambda t: (l, t, 0))
    return pl.pallas_call(
        body, name=f"adamw_{name}_l{l}", grid=(nt,),
        in_specs=[layer, layer, layer, BS((2, tr, cols), lambda t: (0, t, 0))] + [ANY] * (n_prev + len(after)),
        out_specs=[layer] * 4, out_shape=[SDS(w.shape, F32)] * 4,
        input_output_aliases={4 + i: i for i in range(n_prev)}, compiler_params=_params(),
    )(w, m, v, g, *(earlier or ()), *after)


def _adamw_mid(name, w, m, v, gathered, mine, place):
    shape = w.shape[1:]
    zeros = (0,) * len(shape)

    def body(place_ref, w_ref, m_ref, v_ref, *refs):
        gath, own = refs[:N_LAYERS], refs[N_LAYERS:2 * N_LAYERS]
        go_ref, d_ref, mo_ref, vo_ref = refs[2 * N_LAYERS:]
        me = 2 * place_ref[0] + place_ref[1]
        sums = []
        for l in range(N_LAYERS):
            acc = jnp.where(me == 0, own[l][...], gath[l][0])
            for k in range(1, 8):
                acc = acc + jnp.where(me == k, own[l][...], gath[l][k])
            sums.append(acc)
        gv = sums[0]
        for l in range(1, N_LAYERS):
            gv = jnp.where(pl.program_id(0) == l, sums[l], gv)
        delta, m_new, v_new = _adamw_values(w_ref[...], gv, m_ref[...], v_ref[...])
        go_ref[...] = gv
        d_ref[...] = delta
        mo_ref[...] = m_new
        vo_ref[...] = v_new

    layer = BS((None,) + shape, lambda l, pr: (l,) + zeros)
    return pl.pallas_call(
        body, name=f"adamw_{name}",
        grid_spec=pltpu.PrefetchScalarGridSpec(
            num_scalar_prefetch=1, grid=(N_LAYERS,),
            in_specs=[layer] * 3 + [BS((8,) + shape, lambda l, pr: (0,) + zeros)] * N_LAYERS
            + [BS(shape, lambda l, pr: zeros)] * N_LAYERS,
            out_specs=[layer] * 4),
        out_shape=[SDS(w.shape, F32)] * 4, compiler_params=_params())(place, w, m, v, *gathered, *mine)


def _adamw_rows(w, m, v, g):
    r, cols = w.shape
    tr = _row_tile(r, cols, budget=512 * 1024)

    def body(w_ref, m_ref, v_ref, g_ref, d_ref, mo_ref, vo_ref):
        delta, m_new, v_new = _adamw_values(w_ref[...], g_ref[...], m_ref[...], v_ref[...])
        d_ref[...] = delta
        mo_ref[...] = m_new
        vo_ref[...] = v_new

    spec = BS((tr, cols), lambda t: (t, 0))
    return pl.pallas_call(body, name="adamw_small", grid=(r // tr,), in_specs=[spec] * 4, out_specs=[spec] * 3,
                          out_shape=[SDS(w.shape, F32)] * 3)(w, m, v, g)


PACK_ALIGN = 8 * 128
PACK_ROWS = 128


def _pack_rows(arrays):
    parts, rows = [], 0
    for a in arrays:
        flat = a.reshape(-1)
        pad = (-flat.shape[0]) % PACK_ALIGN
        if pad:
            flat = jnp.pad(flat, (0, pad))
        parts.append(flat.reshape(-1, 128))
        rows += parts[-1].shape[0]
    if rows % PACK_ROWS:
        parts.append(jnp.zeros((PACK_ROWS - rows % PACK_ROWS, 128), parts[0].dtype))
    return jnp.concatenate(parts, axis=0)


def _unpack_rows(buf, shapes):
    out, row = [], 0
    for shape in shapes:
        size = math.prod(shape)
        rows = -(-size // PACK_ALIGN) * (PACK_ALIGN // 128)
        out.append(buf[row:row + rows].reshape(-1)[:size].reshape(shape))
        row += rows
    return out


def kernel(x, norm1, w_in, b_gate, ssm_a_re, ssm_a_im, ssm_log_dt, ssm_b_re, ssm_b_im, ssm_c_re, ssm_c_im, ssm_d, ssm_w_glu, ssm_b_glu, ssm_w_proj, conv_w_dw, conv_b_dw, conv_ln_g, conv_ln_b, conv_w_proj, pool_w_group, pool_scale, pool_w_proj, w_out, norm2, ffn_w_gate, ffn_w_up, ffn_w_down, final_norm, loss_target, m_norm1, m_w_in, m_b_gate, m_ssm_a_re, m_ssm_a_im, m_ssm_log_dt, m_ssm_b_re, m_ssm_b_im, m_ssm_c_re, m_ssm_c_im, m_ssm_d, m_ssm_w_glu, m_ssm_b_glu, m_ssm_w_proj, m_conv_w_dw, m_conv_b_dw, m_conv_ln_g, m_conv_ln_b, m_conv_w_proj, m_pool_w_group, m_pool_scale, m_pool_w_proj, m_w_out, m_norm2, m_ffn_w_gate, m_ffn_w_up, m_ffn_w_down, m_final_norm, v_norm1, v_w_in, v_b_gate, v_ssm_a_re, v_ssm_a_im, v_ssm_log_dt, v_ssm_b_re, v_ssm_b_im, v_ssm_c_re, v_ssm_c_im, v_ssm_d, v_ssm_w_glu, v_ssm_b_glu, v_ssm_w_proj, v_conv_w_dw, v_conv_b_dw, v_conv_ln_g, v_conv_ln_b, v_conv_w_proj, v_pool_w_group, v_pool_scale, v_pool_w_proj, v_w_out, v_norm2, v_ffn_w_gate, v_ffn_w_up, v_ffn_w_down, v_final_norm):
    given = dict(locals())
    cx, cy, cc = _place()
    place = jnp.stack([2 * cx + cy, cc]).astype(jnp.int32)

    def kernel_view(n, a):
        if n in TRANSPOSED:
            return a.transpose(0, 2, 1)
        return a.transpose(0, 1, 3, 2) if n in ("ssm_b_re", "ssm_b_im") else a

    prm = {n: given[n] for n in WEIGHTS}
    mom = {n: given["m_" + n] for n in WEIGHTS}
    var = {n: given["v_" + n] for n in WEIGHTS}
    for n in MID:
        prm[n], mom[n], var[n] = kernel_view(n, prm[n]), kernel_view(n, mom[n]), kernel_view(n, var[n])

    dw_shard = prm["conv_w_dw"].reshape(N_LAYERS, CONV_KERNEL, -1)
    casts = {"w_in": _cast_into("w_in", prm["w_in"], place, MXU_DTYPE)}
    first, first_started = _allgather_start("first", [[casts["w_in"][0]]])
    in_flight = {(0, "in"): first[0]}
    casts.update({n: _cast_into(n, kernel_view(n, prm[n]), place, MXU_DTYPE, after=(first_started,))
                  for n in BIG if n != "w_in"})
    casts["conv_w_dw"] = _cast_into("conv_w_dw", dw_shard, place, F32, after=(first_started,))
    order = [(l, g) for l in range(N_LAYERS) for g in GATHER_GROUPS if (l, g) != (0, "in")]
    rest, rest_started = _allgather_start("rest", [[casts[n][l] for n in GATHER_GROUPS[g]] for l, g in order])
    in_flight.update(zip(order, rest))

    def weights_of(l, group, after):
        send_sems, recv_sems, bufs = in_flight[l, group]
        tag = f"l{l}_{group}"
        if (l, group) == (0, "in"):
            after = after + (rest_started,)
        bufs = _allgather_forward(tag, _allgather_wait(tag, send_sems, recv_sems, bufs, after))
        fw = dict(zip(GATHER_GROUPS[group], bufs))
        if "conv_w_dw" in fw:
            fw["conv_w_dw"] = fw["conv_w_dw"].transpose(1, 0, 2).reshape(CONV_KERNEL, -1)
        return fw

    pending, small_pending, small_shapes = {}, {}, {}
    tokens = {}

    def on_grads(l, group, grads):
        if group == "small":
            packed = {n: g for n, g in grads.items() if n not in MID}
            small_shapes[l] = {n: g.shape for n, g in packed.items()}
            begun = _allgather_rows_start(f"l{l}", [_pack_rows(list(packed.values()))] + [grads[n] for n in MID])
            small_pending[l], token = begun[:4], begun[4]
        else:
            pending[l, group], token = _reduce_start(f"{l}_{group}", grads)
        tokens[l, group] = token
        return token[0, 0]

    loss, dx, _, _, _ = _local_step(x[0], loss_target[0], weights_of, prm, place, on_grads)
    loss = lax.psum(loss, ("x", "y", "c"))

    reduced = [{} for _ in range(N_LAYERS)]
    out = {}

    def finish(l, group, after):
        reduced[l].update(_reduce_finish(pending[l, group], place, after))

    def adamw(l, names, done):
        for n in names:
            out[n] = _adamw_big(n, l, kernel_view(n, prm[n]), kernel_view(n, mom[n]), kernel_view(n, var[n]),
                                reduced[l][n], out.get(n), after=done)
            done = (out[n][0],)
        return done

    top = N_LAYERS - 1
    done = (tokens[0, "in"], tokens[0, "small"])
    for group in ("ffn", "mixer", "in"):
        finish(top, group, done)
    done = adamw(top, BIG, done)
    for group in ("ffn", "mixer", "in"):
        finish(0, group, done)
        done = adamw(0, [n for n in GATHER_GROUPS[group] if n in BIG], done)
    for n in BIG:
        out[n] = tuple(kernel_view(n, a) for a in out[n])

    gsmall = {}
    mid_mine, mid_gathered = [], []
    for l in range(N_LAYERS):
        mine, lands = _allgather_rows_wait(f"l{l}", *small_pending[l], done[0])
        lands = _allgather_rows_forward(f"l{l}", lands)
        mid_mine.append(mine[1:])
        mid_gathered.append(lands[1:])
        gsum = _sum_devices(f"l{l}", lands[0], mine[0], place)
        for n, g in zip(small_shapes[l], _unpack_rows(gsum, list(small_shapes[l].values()))):
            gsmall.setdefault(n, [None] * N_LAYERS)[l] = g
    for i, n in enumerate(MID):
        out[n] = tuple(kernel_view(n, a) for a in _adamw_mid(
            n, prm[n], mom[n], var[n], [mid_gathered[l][i] for l in range(N_LAYERS)],
            [mid_mine[l][i] for l in range(N_LAYERS)], place))
    gsmall = {n: (g[top] if n == "final_norm" else jnp.stack(g)) for n, g in gsmall.items()}
    lanes = dw_shard.shape[-1]
    gsmall["conv_w_dw"] = lax.dynamic_slice_in_dim(gsmall["conv_w_dw"], (2 * cx + cy) * lanes, lanes, axis=2)
    small_names = [n for n in SMALL if n not in MID] + ["conv_w_dw"]
    w_rows = _pack_rows([prm[n] for n in small_names])
    m_rows = _pack_rows([mom[n] for n in small_names])
    v_rows = _pack_rows([var[n] for n in small_names])
    g_rows = _pack_rows([gsmall[n] for n in small_names])
    shapes = [prm[n].shape for n in small_names]
    d_s, m_s, v_s = (_unpack_rows(r, shapes) for r in _adamw_rows(w_rows, m_rows, v_rows, g_rows))
    for i, n in enumerate(small_names):
        out[n] = (gsmall[n].reshape(prm[n].shape), d_s[i], m_s[i], v_s[i])
    grads = [out[n][0] for n in WEIGHTS]
    deltas = [out[n][1] for n in WEIGHTS]
    new_m = [out[n][2] for n in WEIGHTS]
    new_v = [out[n][3] for n in WEIGHTS]
    return (loss, dx[None], *grads, *deltas, *new_m, *new_v)
```

```python
import functools
import math

import jax
import jax.numpy as jnp
from jax import lax
from jax.experimental import pallas as pl
from jax.experimental.pallas import tpu as pltpu

F32 = jnp.float32
MXU_DTYPE = jnp.bfloat16
WIRE_DTYPE = jnp.bfloat16
SDS = jax.ShapeDtypeStruct
BS = pl.BlockSpec
ANY = pl.BlockSpec(memory_space=pl.ANY)
HBM = pl.BlockSpec(memory_space=pltpu.HBM)
SEM = pl.BlockSpec(memory_space=pltpu.SEMAPHORE)
SIDE_EFFECT = pltpu.SideEffectType.DATAFLOW_SIDE_EFFECTING
MESH = pl.DeviceIdType.MESH

EPS = 1e-6
N_CHIPS = 4
N_LAYERS = 2
SSM_GROUPS, SSM_STATE, SSM_GROUP = 32, 64, 16
CONV_KERNEL = 31
CONV_PAD = 32
POOL_WINDOWS = (2, 4, 8, 16)
GELU_C = math.sqrt(2.0 / math.pi)
ADAM_LR, ADAM_B1, ADAM_B2, ADAM_EPS, ADAM_WD, ADAM_STEP = 0.001, 0.9, 0.999, 1e-08, 0.01, 10
VMEM_LIMIT = 56 * 1024 * 1024

BIG = ("w_in", "ssm_w_glu", "ssm_w_proj", "conv_w_proj", "pool_w_proj", "w_out", "ffn_w_gate", "ffn_w_up", "ffn_w_down")
TRANSPOSED = ("ffn_w_gate", "ffn_w_up")
MID = ("ssm_b_re", "ssm_b_im", "ssm_c_re", "ssm_c_im")
GATHER_GROUPS = {
    "in": ("w_in",),
    "mixer": ("ssm_w_glu", "ssm_w_proj", "conv_w_proj", "pool_w_proj", "w_out", "conv_w_dw"),
    "ffn": ("ffn_w_gate", "ffn_w_up", "ffn_w_down"),
}
SMALL = ("norm1", "b_gate", "ssm_a_re", "ssm_a_im", "ssm_log_dt", "ssm_b_re", "ssm_b_im", "ssm_c_re", "ssm_c_im",
         "ssm_d", "ssm_b_glu", "conv_b_dw", "conv_ln_g", "conv_ln_b", "pool_w_group", "pool_scale", "norm2",
         "final_norm")
WEIGHTS = ("norm1", "w_in", "b_gate", "ssm_a_re", "ssm_a_im", "ssm_log_dt", "ssm_b_re", "ssm_b_im", "ssm_c_re",
           "ssm_c_im", "ssm_d", "ssm_w_glu", "ssm_b_glu", "ssm_w_proj", "conv_w_dw", "conv_b_dw", "conv_ln_g",
           "conv_ln_b", "conv_w_proj", "pool_w_group", "pool_scale", "pool_w_proj", "w_out", "norm2", "ffn_w_gate",
           "ffn_w_up", "ffn_w_down", "final_norm")


def _params(vmem=True):
    return pltpu.CompilerParams(vmem_limit_bytes=VMEM_LIMIT) if vmem else None


def _mm(a, b):
    return jnp.dot(a.astype(MXU_DTYPE), b.astype(MXU_DTYPE), preferred_element_type=F32)


def _mm_nt(a, b):
    return lax.dot_general(a.astype(MXU_DTYPE), b.astype(MXU_DTYPE), (((1,), (1,)), ((), ())),
                           preferred_element_type=F32)


def _mm_tn(a, b):
    return lax.dot_general(a.astype(MXU_DTYPE), b.astype(MXU_DTYPE), (((0,), (0,)), ((), ())),
                           preferred_element_type=F32)


def _sigmoid(x):
    return jax.nn.sigmoid(x)


def _gelu(x):
    t = jnp.tanh(GELU_C * (x + 0.044715 * (x * x * x)))
    return x * (0.5 * (1.0 + t)), t


def _gelu_grad(x, t):
    return 0.5 * (1.0 + t) + 0.5 * x * (1.0 - t * t) * (GELU_C * (1.0 + 3.0 * 0.044715 * x * x))


def _colsum(v):
    return jnp.sum(v, axis=0, keepdims=True)


def _row_tile(rows, cols, itemsize=4, budget=1536 * 1024):
    best = None
    for t in range(8, rows + 1, 8):
        if rows % t == 0 and t * cols * itemsize <= budget:
            best = t
    return best if best is not None else rows


def _in_proj(l, x, norm1, w_in):
    s, d = x.shape
    nc = w_in.shape[-1]
    tm = min(1024, s)
    nt = s // tm

    def body(x_ref, g_ref, w_ref, z_ref, h_ref, h_all):
        i = pl.program_id(1)
        rows = pl.ds(pl.multiple_of(i * tm, tm), tm)

        @pl.when(pl.program_id(0) == 0)
        def _():
            xv = x_ref[...]
            r = lax.rsqrt(jnp.mean(xv * xv, axis=-1, keepdims=True) + EPS)
            hv = (xv * r * g_ref[...]).astype(h_ref.dtype)
            h_ref[...] = hv.T
            h_all[rows, :] = hv

        z_ref[...] = _mm(h_all[rows, :], w_ref[...])

    tile_of = lambda j, i: i * (1 - jnp.minimum(j, 1)) + (nt - 1) * jnp.minimum(j, 1)
    return pl.pallas_call(
        body, name=f"in_proj_l{l}", grid=(N_CHIPS, nt),
        in_specs=[BS((tm, d), lambda j, i: (tile_of(j, i), 0)), BS((None, 1, d), lambda j, i: (l, 0, 0)),
                  BS((None, d, nc), lambda j, i: (j, 0, 0))],
        out_specs=[BS((tm, nc), lambda j, i: (i, j)), BS((d, tm), lambda j, i: (0, tile_of(j, i)))],
        out_shape=[SDS((s, N_CHIPS * nc), F32), SDS((d, s), MXU_DTYPE)],
        scratch_shapes=[pltpu.VMEM((s, d), MXU_DTYPE)], compiler_params=_params())(x, norm1, w_in)


def _mm_cols(a, w_ref):
    return jnp.concatenate([_mm(a, w_ref[j]) for j in range(N_CHIPS)], axis=1)


def _mm_nt_cols(dv, w_ref):
    nc = w_ref.shape[-1]
    acc = _mm_nt(dv[:, 0:nc], w_ref[0])
    for j in range(1, N_CHIPS):
        acc = acc + _mm_nt(dv[:, j * nc:(j + 1) * nc], w_ref[j])
    return acc


def _merge_values(y, hc, p, zg, wglu, bglu, wpa, wpb, wpc, lng, lnb, wgrp, scale, bg):
    v = {}
    ge, th = _gelu(y)
    t = _mm(ge, wglu) + bglu
    sg = _sigmoid(t)
    sa = ge * sg
    ya = _mm_cols(sa, wpa)
    mu = jnp.mean(hc, axis=-1, keepdims=True)
    xc = hc - mu
    r = lax.rsqrt(jnp.mean(xc * xc, axis=-1, keepdims=True) + EPS)
    xh = xc * r
    ln = xh * lng + lnb
    sl = _sigmoid(ln)
    ac = ln * sl
    yb = _mm_cols(ac, wpb)
    gw = p.shape[1] // len(POOL_WINDOWS)
    q = jnp.concatenate([_mm(p[:, k * gw:(k + 1) * gw], wgrp[k]) for k in range(len(POOL_WINDOWS))], axis=1)
    pp = q * scale
    yc = _mm_cols(pp, wpc)
    d = ya.shape[1]
    gates = [_sigmoid(zg[k] + bg[:, k * d:(k + 1) * d]) for k in range(3)]
    merged = gates[0] * ya + gates[1] * yb + gates[2] * yc
    v.update(ge=ge, th=th, sg=sg, sa=sa, ya=ya, r=r, xh=xh, ln=ln, sl=sl, ac=ac, yb=yb, q=q, pp=pp, yc=yc,
             gates=gates, merged=merged)
    return v


def _merge_specs(l, tm, d, cw):
    row = lambda n: BS((None, 1, n), lambda i: (l, 0, 0))
    resident = lambda shp: BS(shp, lambda i: (0, 0, 0), pipeline_mode=pl.Buffered(1))
    return [
        BS((tm, cw), lambda i: (i, 0)),
        BS((tm, cw), lambda i: (i, 0)),
        BS((tm, cw), lambda i: (i, 0)),
        BS((tm, d), lambda i: (i, 2)), BS((tm, d), lambda i: (i, 3)), BS((tm, d), lambda i: (i, 4)),
        resident((N_CHIPS, cw // N_CHIPS, cw)),
        row(cw),
        resident((N_CHIPS, cw, d // N_CHIPS)),
        resident((N_CHIPS, cw, d // N_CHIPS)),
        resident((N_CHIPS, cw, d // N_CHIPS)),
        row(cw), row(cw),
        BS((None, 4, cw // 4, cw // 4), lambda i: (l, 0, 0, 0)),
        row(cw),
        row(3 * d),
        resident((N_CHIPS, d // N_CHIPS, d)),
    ]


def _merge_fwd(l, x, y, hc, p, z, fw, sp):
    s, d = x.shape
    cw = y.shape[1]
    tm = min(512, s)

    def body(x_ref, y_ref, hc_ref, p_ref, z0, z1, z2, wglu, bglu, wpa, wpb, wpc, lng, lnb, wgrp, scale, bg, wout,
             x1_ref):
        v = _merge_values(y_ref[...], hc_ref[...], p_ref[...], (z0[...], z1[...], z2[...]),
                          wglu[...].reshape(cw, cw), bglu[...], wpa, wpb, wpc, lng[...], lnb[...], wgrp, scale[...],
                          bg[...])
        x1_ref[...] = x_ref[...] + _mm(v["merged"], wout[...].reshape(d, d))

    return pl.pallas_call(
        body, name=f"merge_fwd_l{l}", grid=(s // tm,),
        in_specs=[BS((tm, d), lambda i: (i, 0))] + _merge_specs(l, tm, d, cw),
        out_specs=BS((tm, d), lambda i: (i, 0)), out_shape=SDS((s, d), F32), compiler_params=_params(),
    )(x, y, hc, p, z, z, z, fw["ssm_w_glu"], sp["ssm_b_glu"], fw["ssm_w_proj"], fw["conv_w_proj"], fw["pool_w_proj"],
      sp["conv_ln_g"], sp["conv_ln_b"], sp["pool_w_group"], sp["pool_scale"], sp["b_gate"], fw["w_out"])


def _merge_bwd(l, dx1, y, hc, p, z, fw, sp):
    s, d = dx1.shape
    cw = y.shape[1]
    tm = min(256, s)
    m = MXU_DTYPE

    def body(dx1_ref, y_ref, hc_ref, p_ref, z0, z1, z2, wglu, bglu, wpa, wpb, wpc, lng, lnb, wgrp, scale, bg, wout,
             dzg_ref, dy_ref, dhc_ref, dp_ref, merged_ref, sa_ref, ac_ref, pp_ref, ge_ref, dt_ref, dya_ref, dyb_ref,
             dyc_ref, dq_ref, dbg_ref, dbglu_ref, dlng_ref, dlnb_ref, dscale_ref):
        yv = y_ref[...]
        wg = wglu[...].reshape(cw, cw)
        v = _merge_values(yv, hc_ref[...], p_ref[...], (z0[...], z1[...], z2[...]), wg, bglu[...], wpa, wpb, wpc,
                          lng[...], lnb[...], wgrp, scale[...], bg[...])
        dm = _mm_nt(dx1_ref[...], wout[...].reshape(d, d))
        ys = (v["ya"], v["yb"], v["yc"])
        dys, dbg = [], []
        for k in range(3):
            gk = v["gates"][k]
            dzk = dm * ys[k] * (gk * (1.0 - gk))
            dbg.append(_colsum(dzk))
            dzg_ref[:, k * d:(k + 1) * d] = dzk.astype(m)
            dys.append((dm * gk).astype(m))
        dsa = _mm_nt_cols(dys[0], wpa)
        dac = _mm_nt_cols(dys[1], wpb)
        dpp = _mm_nt_cols(dys[2], wpc)
        ge, sg = v["ge"], v["sg"]
        dt = dsa * ge * (sg * (1.0 - sg))
        dge = dsa * sg + _mm_nt(dt, wg)
        dy_ref[...] = dge * _gelu_grad(yv, v["th"])
        ln, sl, xh = v["ln"], v["sl"], v["xh"]
        dln = dac * (sl * (1.0 + ln * (1.0 - sl)))
        dxh = dln * lng[...]
        dhc_ref[...] = v["r"] * (dxh - jnp.mean(dxh, axis=-1, keepdims=True)
                                 - xh * jnp.mean(dxh * xh, axis=-1, keepdims=True))
        dq = dpp * scale[...]
        gw = cw // len(POOL_WINDOWS)
        for k in range(len(POOL_WINDOWS)):
            dp_ref[:, k * gw:(k + 1) * gw] = _mm_nt(dq[:, k * gw:(k + 1) * gw], wgrp[k])
        merged_ref[...] = v["merged"].astype(m)
        sa_ref[...] = v["sa"].astype(m)
        ac_ref[...] = v["ac"].astype(m)
        pp_ref[...] = v["pp"].astype(m)
        ge_ref[...] = ge.astype(m)
        dt_ref[...] = dt.astype(m)
        dya_ref[...] = dys[0]
        dyb_ref[...] = dys[1]
        dyc_ref[...] = dys[2]
        dq_ref[...] = dq.astype(m)

        @pl.when(pl.program_id(0) == 0)
        def _():
            for ref in (dbg_ref, dbglu_ref, dlng_ref, dlnb_ref, dscale_ref):
                ref[...] = jnp.zeros(ref.shape, F32)

        dbg_ref[...] += jnp.concatenate(dbg, axis=1)
        dbglu_ref[...] += _colsum(dt)
        dlng_ref[...] += _colsum(dln * xh)
        dlnb_ref[...] += _colsum(dln)
        dscale_ref[...] += _colsum(dpp * v["q"])

    tile = lambda n: BS((tm, n), lambda i: (i, 0))
    acc = lambda n: BS((1, n), lambda i: (0, 0))
    outs = pl.pallas_call(
        body, name=f"merge_bwd_l{l}", grid=(s // tm,),
        in_specs=[tile(d)] + _merge_specs(l, tm, d, cw),
        out_specs=[tile(3 * d), tile(cw), tile(cw), tile(cw), tile(d), tile(cw), tile(cw), tile(cw), tile(cw), tile(cw),
                   tile(d), tile(d), tile(d), tile(cw), acc(3 * d), acc(cw), acc(cw), acc(cw), acc(cw)],
        out_shape=[SDS((s, 3 * d), m), SDS((s, cw), F32), SDS((s, cw), F32), SDS((s, cw), F32), SDS((s, d), m),
                   SDS((s, cw), m), SDS((s, cw), m), SDS((s, cw), m), SDS((s, cw), m), SDS((s, cw), m), SDS((s, d), m),
                   SDS((s, d), m), SDS((s, d), m), SDS((s, cw), m), SDS((1, 3 * d), F32), SDS((1, cw), F32),
                   SDS((1, cw), F32), SDS((1, cw), F32), SDS((1, cw), F32)],
        compiler_params=_params(),
    )(dx1, y, hc, p, z, z, z, fw["ssm_w_glu"], sp["ssm_b_glu"], fw["ssm_w_proj"], fw["conv_w_proj"], fw["pool_w_proj"],
      sp["conv_ln_g"], sp["conv_ln_b"], sp["pool_w_group"], sp["pool_scale"], sp["b_gate"], fw["w_out"])
    names = ("dzg", "dy", "dhc", "dp", "merged", "sa", "ac", "pp", "ge", "dt", "dya", "dyb", "dyc", "dq", "db_gate",
             "db_glu", "dln_g", "dln_b", "dscale")
    return dict(zip(names, outs))


def _ffn_fwd(l, x1, norm2, wg, wu, wd):
    s, d = x1.shape
    hc = wd.shape[1]
    tm = min(1024, s)

    def body(x_ref, g_ref, wg_ref, wu_ref, wd_ref, o_ref, h_scr):
        @pl.when(pl.program_id(1) == 0)
        def _():
            xv = x_ref[...]
            r = lax.rsqrt(jnp.mean(xv * xv, axis=-1, keepdims=True) + EPS)
            h_scr[...] = (xv * r * g_ref[...]).astype(h_scr.dtype)
            o_ref[...] = xv

        h = h_scr[...]
        gate = _mm_nt(h, wg_ref[...])
        up = _mm_nt(h, wu_ref[...])
        o_ref[...] += _mm(gate * _sigmoid(gate) * up, wd_ref[...])

    return pl.pallas_call(
        body, name=f"ffn_fwd_l{l}", grid=(s // tm, N_CHIPS),
        in_specs=[BS((tm, d), lambda i, j: (i, 0)), BS((None, 1, d), lambda i, j: (l, 0, 0)),
                  BS((None, hc, d), lambda i, j: (j, 0, 0)), BS((None, hc, d), lambda i, j: (j, 0, 0)),
                  BS((None, hc, d), lambda i, j: (j, 0, 0))],
        out_specs=BS((tm, d), lambda i, j: (i, 0)), out_shape=SDS((s, d), F32),
        scratch_shapes=[pltpu.VMEM((tm, d), MXU_DTYPE)], compiler_params=_params())(x1, norm2, wg, wu, wd)


def _ffn_bwd(l, x1, dx2, norm2, wg, wu, wd):
    s, d = x1.shape
    hc = wd.shape[1]
    tm = min(512, s)
    m = MXU_DTYPE
    last = N_CHIPS - 1

    def body(x_ref, dx2_ref, g_ref, wg_ref, wu_ref, wd_ref, dx1_ref, h_ref, act_ref, dgate_ref, dup_ref, dn_ref,
             dh_scr, dxb_scr):
        i, j = pl.program_id(0), pl.program_id(1)

        @pl.when(j == 0)
        def _():
            xv = x_ref[...]
            r = lax.rsqrt(jnp.mean(xv * xv, axis=-1, keepdims=True) + EPS)
            h_ref[...] = (xv * r * g_ref[...]).astype(m)
            dxb_scr[...] = dx2_ref[...].astype(m)
            dh_scr[...] = jnp.zeros(dh_scr.shape, F32)

        @pl.when((i == 0) & (j == 0))
        def _():
            dn_ref[...] = jnp.zeros(dn_ref.shape, F32)

        h = h_ref[...]
        gate = _mm_nt(h, wg_ref[...])
        up = _mm_nt(h, wu_ref[...])
        sg = _sigmoid(gate)
        silu = gate * sg
        act_ref[...] = (silu * up).astype(m).T
        dact = _mm_nt(dxb_scr[...], wd_ref[...])
        dup = (dact * silu).astype(m)
        dgate = (dact * up * (sg * (1.0 + gate * (1.0 - sg)))).astype(m)
        dup_ref[...] = dup.T
        dgate_ref[...] = dgate.T
        dh_scr[...] += _mm(dgate, wg_ref[...]) + _mm(dup, wu_ref[...])

        @pl.when(j == last)
        def _():
            xv = x_ref[...]
            r = lax.rsqrt(jnp.mean(xv * xv, axis=-1, keepdims=True) + EPS)
            xh = xv * r
            dh = dh_scr[...]
            dn_ref[...] += _colsum(dh * xh)
            dxh = dh * g_ref[...]
            dx1_ref[...] = dx2_ref[...] + r * (dxh - xh * jnp.mean(dxh * xh, axis=-1, keepdims=True))

    chunk = BS((None, hc, tm), lambda i, j: (j, 0, i))
    outs = pl.pallas_call(
        body, name=f"ffn_bwd_l{l}", grid=(s // tm, N_CHIPS),
        in_specs=[BS((tm, d), lambda i, j: (i, 0)), BS((tm, d), lambda i, j: (i, 0)),
                  BS((None, 1, d), lambda i, j: (l, 0, 0)),
                  BS((None, hc, d), lambda i, j: (j, 0, 0)), BS((None, hc, d), lambda i, j: (j, 0, 0)),
                  BS((None, hc, d), lambda i, j: (j, 0, 0))],
        out_specs=[BS((tm, d), lambda i, j: (i, 0)), BS((tm, d), lambda i, j: (i, 0)), chunk, chunk, chunk,
                   BS((1, d), lambda i, j: (0, 0))],
        out_shape=[SDS((s, d), F32), SDS((s, d), m), SDS((N_CHIPS, hc, s), m), SDS((N_CHIPS, hc, s), m),
                   SDS((N_CHIPS, hc, s), m), SDS((1, d), F32)],
        scratch_shapes=[pltpu.VMEM((tm, d), F32), pltpu.VMEM((tm, d), m)], compiler_params=_params(),
    )(x1, dx2, norm2, wg, wu, wd)
    return dict(zip(("dx1", "h2", "act", "dgate", "dup", "dnorm2"), outs))


def _loss_head(x, target, gf):
    s, d = x.shape
    tm = min(512, s)

    def body(x_ref, t_ref, g_ref, dx_ref, loss_ref, dg_ref):
        @pl.when(pl.program_id(0) == 0)
        def _():
            loss_ref[...] = jnp.zeros(loss_ref.shape, F32)
            dg_ref[...] = jnp.zeros(dg_ref.shape, F32)

        xv = x_ref[...]
        r = lax.rsqrt(jnp.mean(xv * xv, axis=-1, keepdims=True) + EPS)
        xh = xv * r
        err = xh * g_ref[...] - t_ref[...]
        loss_ref[...] += 0.5 * jnp.sum(jnp.mean(err * err, axis=-1, keepdims=True), axis=0, keepdims=True)
        dyv = err * (1.0 / d)
        dg_ref[...] += _colsum(dyv * xh)
        dxh = dyv * g_ref[...]
        dx_ref[...] = r * (dxh - xh * jnp.mean(dxh * xh, axis=-1, keepdims=True))

    return pl.pallas_call(
        body, name="loss_head", grid=(s // tm,),
        in_specs=[BS((tm, d), lambda i: (i, 0)), BS((tm, d), lambda i: (i, 0)), BS((1, d), lambda i: (0, 0))],
        out_specs=[BS((tm, d), lambda i: (i, 0)), BS((1, 1), lambda i: (0, 0)), BS((1, d), lambda i: (0, 0))],
        out_shape=[SDS((s, d), F32), SDS((1, 1), F32), SDS((1, d), F32)], compiler_params=_params())(x, target, gf)


def _in_proj_bwd(l, dres, x, norm1, w_in, du_a, dv1, dv2, du_c, dzg):
    s, d = x.shape
    nc = w_in.shape[-1]
    tm = min(512, s)
    m = MXU_DTYPE

    def body(dres_ref, x_ref, g_ref, w_ref, a_ref, b1_ref, b2_ref, c_ref, g3_ref, dx_ref, dz_ref, dn_ref):
        @pl.when(pl.program_id(0) == 0)
        def _():
            dn_ref[...] = jnp.zeros(dn_ref.shape, F32)

        dz = jnp.concatenate([a_ref[...], b1_ref[...], b2_ref[...], c_ref[...], g3_ref[...]], axis=1).astype(m)
        dz_ref[...] = dz
        dh = _mm_nt_cols(dz, w_ref)
        xv = x_ref[...]
        r = lax.rsqrt(jnp.mean(xv * xv, axis=-1, keepdims=True) + EPS)
        xh = xv * r
        dn_ref[...] += _colsum(dh * xh)
        dxh = dh * g_ref[...]
        dx_ref[...] = dres_ref[...] + r * (dxh - xh * jnp.mean(dxh * xh, axis=-1, keepdims=True))

    tile = lambda n: BS((tm, n), lambda i: (i, 0))
    return pl.pallas_call(
        body, name=f"in_proj_bwd_l{l}", grid=(s // tm,),
        in_specs=[tile(d), tile(d), BS((None, 1, d), lambda i: (l, 0, 0)),
                  BS((N_CHIPS, d, nc), lambda i: (0, 0, 0), pipeline_mode=pl.Buffered(1)),
                  tile(du_a.shape[1]), tile(dv1.shape[1]), tile(dv2.shape[1]), tile(du_c.shape[1]), tile(dzg.shape[1])],
        out_specs=[tile(d), tile(N_CHIPS * nc), BS((1, d), lambda i: (0, 0))],
        out_shape=[SDS((s, d), F32), SDS((s, N_CHIPS * nc), m), SDS((1, d), F32)], compiler_params=_params(),
    )(dres, x, norm1, w_in, du_a, dv1, dv2, du_c, dzg)


def _tn_matmul(name, a, a_spec, b, b_spec, chunk_shape, grid, place):
    last = grid[1] - 1

    def body(place_ref, a_ref, b_ref, own_ref, wire_ref, *acc):
        part = _mm(a_ref[...], b_ref[...])

        def emit(total):
            wire_ref[...] = total.astype(WIRE_DTYPE)

            @pl.when(pl.program_id(0) == place_ref[0])
            def _():
                own_ref[...] = total

        if last == 0:
            emit(part)
        else:
            @pl.when(pl.program_id(1) == 0)
            def _():
                acc[0][...] = part

            @pl.when(pl.program_id(1) > 0)
            def _():
                acc[0][...] += part

            @pl.when(pl.program_id(1) == last)
            def _():
                emit(acc[0][...])

    zeros = (0,) * len(chunk_shape)
    return pl.pallas_call(
        body, name=name,
        grid_spec=pltpu.PrefetchScalarGridSpec(
            num_scalar_prefetch=1, grid=grid, in_specs=[a_spec, b_spec],
            out_specs=[BS(chunk_shape, lambda j, t, pr: zeros), BS((None,) + chunk_shape, lambda j, t, pr: (j,) + zeros)],
            scratch_shapes=[pltpu.VMEM(chunk_shape, F32)] if last else []),
        out_shape=[SDS(chunk_shape, F32), SDS((N_CHIPS,) + chunk_shape, WIRE_DTYPE)],
        compiler_params=_params())(place, a, b)


def _scan_consts(pw_ref, lanes, reverse):
    sgn = -1.0 if reverse else 1.0
    row = lax.broadcasted_iota(jnp.int32, (8, lanes), 0)
    steps = []
    for i, k in enumerate((1, 2, 4)):
        mask = (row < 8 - k) if reverse else (row >= k)
        steps.append((k, jnp.where(mask, pw_ref[2 * i], 0.0), jnp.where(mask, sgn * pw_ref[2 * i + 1], 0.0)))
    c = 4 if reverse else 3
    return steps, pw_ref[2 * c], sgn * pw_ref[2 * c + 1]


def _scan_block(br, bi, steps, row, reverse):
    for k, ar, ai in steps:
        sh = 8 - k if reverse else k
        sr = pltpu.roll(br, sh, 0)
        si = pltpu.roll(bi, sh, 0)
        br, bi = br + ar * sr - ai * si, bi + ar * si + ai * sr
    return br, bi


def _ssm_fwd(l, z, bblk_re, bblk_im, cblk_re, cblk_im, pw, dskip):
    s = z.shape[0]
    gc = bblk_re.shape[1]
    gl = bblk_re.shape[2]
    nblk = bblk_re.shape[0]

    def body(u_ref, bre, bim, cre, cim, pw_ref, d_ref, hre, him, y_ref):
        u = u_ref[...]
        hre[...] = _mm(u, bre[...])
        him[...] = _mm(u, bim[...])
        row = lax.broadcasted_iota(jnp.int32, (8, gl), 0)
        steps, car, cai = _scan_consts(pw_ref, gl, False)

        def step(i, carry):
            cr, ci = carry
            r0 = pl.multiple_of(i * 8, 8)
            br, bi = _scan_block(hre[pl.ds(r0, 8), :], him[pl.ds(r0, 8), :], steps, row, False)
            hr = br + car * cr - cai * ci
            hi = bi + car * ci + cai * cr
            hre[pl.ds(r0, 8), :] = hr
            him[pl.ds(r0, 8), :] = hi
            return jnp.broadcast_to(hr[7:8, :], (8, gl)), jnp.broadcast_to(hi[7:8, :], (8, gl))

        zero = jnp.zeros((8, gl), F32)
        lax.fori_loop(0, s // 8, step, (zero, zero))
        y_ref[...] = _mm_nt(hre[...], cre[...]) - _mm_nt(him[...], cim[...]) + d_ref[...] * u

    return pl.pallas_call(
        body, name=f"ssm_fwd_l{l}", grid=(nblk,),
        in_specs=[BS((s, gc), lambda k: (0, k)), BS((None, gc, gl), lambda k: (k, 0, 0)),
                  BS((None, gc, gl), lambda k: (k, 0, 0)), BS((None, gc, gl), lambda k: (k, 0, 0)),
                  BS((None, gc, gl), lambda k: (k, 0, 0)), BS((10, 8, gl), lambda k: (0, 0, k)),
                  BS((1, gc), lambda k: (0, k))],
        out_specs=[BS((s, gl), lambda k: (0, k)), BS((s, gl), lambda k: (0, k)), BS((s, gc), lambda k: (0, k))],
        out_shape=[SDS((s, nblk * gl), F32), SDS((s, nblk * gl), F32), SDS((s, nblk * gc), F32)],
        compiler_params=_params())(z, bblk_re, bblk_im, cblk_re, cblk_im, pw, dskip)


def _ssm_bwd(l, dy, z, hre, him, bblk_re, bblk_im, cblk_re, cblk_im, pw, dskip):
    s = z.shape[0]
    nblk, gc, gl = bblk_re.shape

    def body(dy_ref, u_ref, hre_ref, him_ref, bre, bim, cre, cim, pw_ref, d_ref,
             du_ref, dbre_ref, dbim_ref, dcre_ref, dcim_ref, dar_ref, dai_ref, dd_ref, gre, gim):
        dyv = dy_ref[...]
        u = u_ref[...]
        gre[...] = _mm(dyv, cre[...])
        gim[...] = -_mm(dyv, cim[...])
        dcre_ref[...] = _mm_tn(dyv, hre_ref[...])
        dcim_ref[...] = -_mm_tn(dyv, him_ref[...])
        dd_ref[...] = _colsum(dyv * u)
        row = lax.broadcasted_iota(jnp.int32, (8, gl), 0)
        steps, car, cai = _scan_consts(pw_ref, gl, True)
        n8 = s // 8

        def step(ii, carry):
            cr, ci, accr, acci = carry
            i = n8 - 1 - ii
            r0 = pl.multiple_of(i * 8, 8)
            br, bi = _scan_block(gre[pl.ds(r0, 8), :], gim[pl.ds(r0, 8), :], steps, row, True)
            dr = br + car * cr - cai * ci
            di = bi + car * ci + cai * cr
            gre[pl.ds(r0, 8), :] = dr
            gim[pl.ds(r0, 8), :] = di
            rp = pl.multiple_of(jnp.maximum(i - 1, 0) * 8, 8)
            keep = jnp.where(i > 0, 1.0, 0.0)
            pr = jnp.where(row >= 1, pltpu.roll(hre_ref[pl.ds(r0, 8), :], 1, 0),
                           keep * pltpu.roll(hre_ref[pl.ds(rp, 8), :], 1, 0))
            pi = jnp.where(row >= 1, pltpu.roll(him_ref[pl.ds(r0, 8), :], 1, 0),
                           keep * pltpu.roll(him_ref[pl.ds(rp, 8), :], 1, 0))
            accr = accr + dr * pr + di * pi
            acci = acci + di * pr - dr * pi
            return (jnp.broadcast_to(dr[0:1, :], (8, gl)), jnp.broadcast_to(di[0:1, :], (8, gl)), accr, acci)

        zero = jnp.zeros((8, gl), F32)
        _, _, accr, acci = lax.fori_loop(0, n8, step, (zero, zero, zero, zero))
        dar_ref[...] = _colsum(accr)
        dai_ref[...] = _colsum(acci)
        dbr = gre[...]
        dbi = gim[...]
        du_ref[...] = (dyv * d_ref[...] + _mm_nt(dbr, bre[...]) + _mm_nt(dbi, bim[...])).astype(du_ref.dtype)
        dbre_ref[...] = _mm_tn(u, dbr)
        dbim_ref[...] = _mm_tn(u, dbi)

    col = lambda n: BS((s, n), lambda k: (0, k))
    blk = lambda a, b: BS((None, a, b), lambda k: (k, 0, 0))
    outs = pl.pallas_call(
        body, name=f"ssm_bwd_l{l}", grid=(nblk,),
        in_specs=[col(gc), col(gc), col(gl), col(gl), blk(gc, gl), blk(gc, gl), blk(gc, gl), blk(gc, gl),
                  BS((10, 8, gl), lambda k: (0, 0, k)), BS((1, gc), lambda k: (0, k))],
        out_specs=[col(gc), blk(gc, gl), blk(gc, gl), blk(gc, gl), blk(gc, gl), BS((1, gl), lambda k: (0, k)),
                   BS((1, gl), lambda k: (0, k)), BS((1, gc), lambda k: (0, k))],
        out_shape=[SDS((s, nblk * gc), MXU_DTYPE), SDS((nblk, gc, gl), F32), SDS((nblk, gc, gl), F32),
                   SDS((nblk, gc, gl), F32), SDS((nblk, gc, gl), F32), SDS((1, nblk * gl), F32),
                   SDS((1, nblk * gl), F32), SDS((1, nblk * gc), F32)],
        scratch_shapes=[pltpu.VMEM((s, gl), F32), pltpu.VMEM((s, gl), F32)], compiler_params=_params(),
    )(dy, z, hre, him, bblk_re, bblk_im, cblk_re, cblk_im, pw, dskip)
    return dict(zip(("du", "dbblk_re", "dbblk_im", "dcblk_re", "dcblk_im", "dabar_re", "dabar_im", "dd"), outs))


def _conv_fwd(l, z, wdw, bdw):
    s = z.shape[0]
    cw = wdw.shape[1]
    lb = 128
    tr = min(256, s)
    off1 = cw // lb
    off2 = 2 * cw // lb

    def body(v1_ref, v2_ref, w_ref, b_ref, hc_ref, scr):
        scr[0:CONV_PAD, :] = jnp.zeros((CONV_PAD, lb), F32)
        scr[CONV_PAD:, :] = v1_ref[...] * _sigmoid(v2_ref[...])
        for t in range(s // tr):
            acc = jnp.broadcast_to(b_ref[...], (tr, lb))
            for k in range(CONV_KERNEL):
                acc = acc + w_ref[pl.ds(k, 1), :] * scr[pl.ds(t * tr + CONV_PAD - (CONV_KERNEL - 1) + k, tr), :]
            hc_ref[pl.ds(t * tr, tr), :] = acc

    return pl.pallas_call(
        body, name=f"conv_fwd_l{l}", grid=(cw // lb,),
        in_specs=[BS((s, lb), lambda k: (0, off1 + k)), BS((s, lb), lambda k: (0, off2 + k)),
                  BS((CONV_KERNEL, lb), lambda k: (0, k)), BS((1, lb), lambda k: (0, k))],
        out_specs=BS((s, lb), lambda k: (0, k)), out_shape=SDS((s, cw), F32),
        scratch_shapes=[pltpu.VMEM((s + CONV_PAD, lb), F32)], compiler_params=_params())(z, z, wdw, bdw)


def _conv_bwd(l, dhc, z, wdw):
    s = z.shape[0]
    cw = wdw.shape[1]
    lb = 128
    tr = min(256, s)
    off1 = cw // lb
    off2 = 2 * cw // lb
    nb = cw // lb

    def body(d_ref, v1_ref, v2_ref, w_ref, dv1_ref, dv2_ref, dw_ref, db_ref, hpad, dpad):
        v1 = v1_ref[...]
        sg = _sigmoid(v2_ref[...])
        dv = d_ref[...]
        hpad[0:CONV_PAD, :] = jnp.zeros((CONV_PAD, lb), F32)
        hpad[CONV_PAD:, :] = v1 * sg
        dpad[0:s, :] = dv
        dpad[s:, :] = jnp.zeros((CONV_PAD, lb), F32)
        db_ref[...] = _colsum(dv)
        dws = [jnp.zeros((1, lb), F32) for _ in range(CONV_KERNEL)]
        for t in range(s // tr):
            dt = d_ref[pl.ds(t * tr, tr), :]
            acc = jnp.zeros((tr, lb), F32)
            for k in range(CONV_KERNEL):
                acc = acc + w_ref[pl.ds(k, 1), :] * dpad[pl.ds(t * tr + (CONV_KERNEL - 1) - k, tr), :]
                dws[k] = dws[k] + _colsum(dt * hpad[pl.ds(t * tr + CONV_PAD - (CONV_KERNEL - 1) + k, tr), :])
            sgt = _sigmoid(v2_ref[pl.ds(t * tr, tr), :])
            v1t = v1_ref[pl.ds(t * tr, tr), :]
            dv1_ref[pl.ds(t * tr, tr), :] = (acc * sgt).astype(dv1_ref.dtype)
            dv2_ref[pl.ds(t * tr, tr), :] = (acc * v1t * (sgt * (1.0 - sgt))).astype(dv2_ref.dtype)
        for k in range(CONV_KERNEL):
            dw_ref[pl.ds(k, 1), :] = dws[k]

    return pl.pallas_call(
        body, name=f"conv_bwd_l{l}", grid=(nb,),
        in_specs=[BS((s, lb), lambda k: (0, k)), BS((s, lb), lambda k: (0, off1 + k)),
                  BS((s, lb), lambda k: (0, off2 + k)), BS((CONV_KERNEL, lb), lambda k: (0, k))],
        out_specs=[BS((s, lb), lambda k: (0, k)), BS((s, lb), lambda k: (0, k)),
                   BS((CONV_KERNEL, lb), lambda k: (0, k)), BS((1, lb), lambda k: (0, k))],
        out_shape=[SDS((s, cw), MXU_DTYPE), SDS((s, cw), MXU_DTYPE), SDS((CONV_KERNEL, cw), F32), SDS((1, cw), F32)],
        scratch_shapes=[pltpu.VMEM((s + CONV_PAD, lb), F32), pltpu.VMEM((s + CONV_PAD, lb), F32)],
        compiler_params=_params())(dhc, z, z, wdw)


def _pool_window(k):
    return jnp.where(k == 0, float(POOL_WINDOWS[0]),
                     jnp.where(k == 1, float(POOL_WINDOWS[1]),
                               jnp.where(k == 2, float(POOL_WINDOWS[2]), float(POOL_WINDOWS[3]))))


def _pool_fwd(l, z, pw_width):
    s = z.shape[0]
    lb = pw_width // len(POOL_WINDOWS)
    off = 3 * pw_width // lb

    def body(u_ref, p_ref):
        k = pl.program_id(0)
        u = u_ref[...]
        row = lax.broadcasted_iota(jnp.int32, (s, lb), 0)
        sums = [u]
        for sh in (1, 2, 4, 8):
            prev = sums[-1]
            sums.append(prev + jnp.where(row >= sh, pltpu.roll(prev, sh, 0), 0.0))
        sel = jnp.where(k == 0, sums[1], jnp.where(k == 1, sums[2], jnp.where(k == 2, sums[3], sums[4])))
        cnt = jnp.minimum((row + 1).astype(F32), _pool_window(k))
        p_ref[...] = sel / cnt - u

    return pl.pallas_call(
        body, name=f"pool_fwd_l{l}", grid=(len(POOL_WINDOWS),),
        in_specs=[BS((s, lb), lambda k: (0, off + k))], out_specs=BS((s, lb), lambda k: (0, k)),
        out_shape=SDS((s, pw_width), F32), compiler_params=_params())(z)


def _pool_bwd(l, dp):
    s, width = dp.shape
    lb = width // len(POOL_WINDOWS)

    def body(d_ref, du_ref):
        k = pl.program_id(0)
        dv = d_ref[...]
        row = lax.broadcasted_iota(jnp.int32, (s, lb), 0)
        cnt = jnp.minimum((row + 1).astype(F32), _pool_window(k))
        sums = [dv / cnt]
        for sh in (1, 2, 4, 8):
            prev = sums[-1]
            sums.append(prev + jnp.where(row < s - sh, pltpu.roll(prev, s - sh, 0), 0.0))
        sel = jnp.where(k == 0, sums[1], jnp.where(k == 1, sums[2], jnp.where(k == 2, sums[3], sums[4])))
        du_ref[...] = (sel - dv).astype(du_ref.dtype)

    return pl.pallas_call(
        body, name=f"pool_bwd_l{l}", grid=(len(POOL_WINDOWS),),
        in_specs=[BS((s, lb), lambda k: (0, k))], out_specs=BS((s, lb), lambda k: (0, k)),
        out_shape=SDS((s, width), MXU_DTYPE), compiler_params=_params())(dp)


def _zoh(a_re, a_im, log_dt):
    dt = jnp.exp(log_dt)
    mag = jnp.exp(dt * a_re)
    ang = dt * a_im
    abar_re = mag * jnp.cos(ang)
    abar_im = mag * jnp.sin(ang)
    den = a_re * a_re + a_im * a_im
    nr = abar_re - 1.0
    ni = abar_im
    f_re = (nr * a_re + ni * a_im) / den
    f_im = (ni * a_re - nr * a_im) / den
    return abar_re, abar_im, f_re, f_im


def _zoh_fwd(l, a_re, a_im, log_dt):
    def body(ar, ai, ld, o0, o1, o2, o3):
        for ref, val in zip((o0, o1, o2, o3), _zoh(ar[...], ai[...], ld[...])):
            ref[...] = val

    return pl.pallas_call(body, name=f"zoh_fwd_l{l}", out_shape=[SDS(a_re.shape, F32)] * 4)(a_re, a_im, log_dt)


def _zoh_bwd(l, a_re, a_im, log_dt, cts):
    def body(ar, ai, ld, c0, c1, c2, c3, dar, dai, dld):
        _, vjp = jax.vjp(_zoh, ar[...], ai[...], ld[...])
        g = vjp((c0[...], c1[...], c2[...], c3[...]))
        dar[...] = g[0]
        dai[...] = g[1]
        dld[...] = g[2]

    return pl.pallas_call(body, name=f"zoh_bwd_l{l}",
                          out_shape=[SDS(a_re.shape, F32), SDS(a_re.shape, F32), SDS(log_dt.shape, F32)],
                          )(a_re, a_im, log_dt, *cts)


def _bbar_fwd(l, f_re, f_im, b_re, b_im):
    g, p, n = b_re.shape[1:]

    def body(fr, fi, br, bi, o_re, o_im):
        o_re[...] = (fr[...] * br[...] - fi[...] * bi[...]).astype(o_re.dtype)
        o_im[...] = (fr[...] * bi[...] + fi[...] * br[...]).astype(o_im.dtype)

    whole = lambda shp: BS(shp, lambda i: (0,) * len(shp))
    layer = BS((None, g, p, n), lambda i: (l, 0, 0, 0))
    return pl.pallas_call(body, name=f"bbar_fwd_l{l}", grid=(1,),
                          in_specs=[whole((g, 1, n)), whole((g, 1, n)), layer, layer],
                          out_specs=[whole((g, p, n))] * 2,
                          out_shape=[SDS((g, p, n), MXU_DTYPE)] * 2)(f_re, f_im, b_re, b_im)


def _bbar_bwd(l, f_re, f_im, b_re, b_im, d_re, d_im):
    g, p, n = b_re.shape[1:]

    def body(fr, fi, br, bi, dr, di, dfr, dfi, dbr, dbi):
        dfr[...] = jnp.sum(dr[...] * br[...] + di[...] * bi[...], axis=1, keepdims=True)
        dfi[...] = jnp.sum(di[...] * br[...] - dr[...] * bi[...], axis=1, keepdims=True)
        dbr[...] = fr[...] * dr[...] + fi[...] * di[...]
        dbi[...] = fr[...] * di[...] - fi[...] * dr[...]

    whole = lambda shp: BS(shp, lambda i: (0,) * len(shp))
    layer = BS((None, g, p, n), lambda i: (l, 0, 0, 0))
    return pl.pallas_call(body, name=f"bbar_bwd_l{l}", grid=(1,),
                          in_specs=[whole((g, 1, n)), whole((g, 1, n)), layer, layer, whole((g, p, n)),
                                    whole((g, p, n))],
                          out_specs=[whole((g, 1, n)), whole((g, 1, n)), whole((g, p, n)), whole((g, p, n))],
                          out_shape=[SDS((g, 1, n), F32), SDS((g, 1, n), F32), SDS((g, p, n), F32),
                                     SDS((g, p, n), F32)])(f_re, f_im, b_re, b_im, d_re, d_im)


def _powers(l, abar_re, abar_im):
    lanes = abar_re.shape[1]

    def body(ar_ref, ai_ref, o_ref):
        ar, ai = ar_ref[...], ai_ref[...]
        pows = [(ar, ai)]
        for _ in range(7):
            pr, pi = pows[-1]
            pows.append((pr * ar - pi * ai, pr * ai + pi * ar))
        row = lax.broadcasted_iota(jnp.int32, (8, lanes), 0)
        for i, k in enumerate((1, 2, 4)):
            o_ref[2 * i] = jnp.broadcast_to(pows[k - 1][0], (8, lanes))
            o_ref[2 * i + 1] = jnp.broadcast_to(pows[k - 1][1], (8, lanes))
        for slot, order in ((3, range(8)), (4, range(7, -1, -1))):
            vr = jnp.zeros((8, lanes), F32)
            vi = jnp.zeros((8, lanes), F32)
            for r, e in enumerate(order):
                vr = jnp.where(row == r, pows[e][0], vr)
                vi = jnp.where(row == r, pows[e][1], vi)
            o_ref[2 * slot] = vr
            o_ref[2 * slot + 1] = vi

    return pl.pallas_call(body, name=f"powers_l{l}", out_shape=SDS((10, 8, lanes), F32))(abar_re, abar_im)


def _block_diag(v):
    g, a, b = v.shape
    eye = jnp.eye(8, dtype=v.dtype)
    out = jnp.einsum("kgab,gh->kgahb", v.reshape(g // 8, 8, a, b), eye)
    return out.reshape(g // 8, 8 * a, 8 * b)


def _block_diag_extract(blk, a, b):
    n = blk.shape[0]
    v = blk.reshape(n, 8, a, 8, b)
    return jnp.einsum("kgahb,gh->kgab", v, jnp.eye(8, dtype=blk.dtype)).reshape(n * 8, a, b)


def _ssm_prepare(l, prm):
    g, n, p = SSM_GROUPS, SSM_STATE, SSM_GROUP
    a_re, a_im = prm["ssm_a_re"][l], prm["ssm_a_im"][l]
    log_dt = prm["ssm_log_dt"][l].reshape(g, 1)
    abar_re, abar_im, f_re, f_im = _zoh_fwd(l, a_re, a_im, log_dt)
    f_re, f_im = f_re.reshape(g, 1, n), f_im.reshape(g, 1, n)
    bbar_re, bbar_im = _bbar_fwd(l, f_re, f_im, prm["ssm_b_re"], prm["ssm_b_im"])
    pw = _powers(l, abar_re.reshape(1, g * n), abar_im.reshape(1, g * n))
    return dict(a_re=a_re, a_im=a_im, log_dt=log_dt, f_re=f_re, f_im=f_im,
                bblk_re=_block_diag(bbar_re), bblk_im=_block_diag(bbar_im),
                cblk_re=_block_diag(prm["ssm_c_re"][l].astype(MXU_DTYPE)),
                cblk_im=_block_diag(prm["ssm_c_im"][l].astype(MXU_DTYPE)), pw=pw,
                dskip=prm["ssm_d"][l].reshape(1, g * p))


def _ssm_param_grads(l, sd, r, prm):
    g, n, p = SSM_GROUPS, SSM_STATE, SSM_GROUP
    dbbar_re = _block_diag_extract(r["dbblk_re"], p, n)
    dbbar_im = _block_diag_extract(r["dbblk_im"], p, n)
    dfr, dfi, db_re, db_im = _bbar_bwd(l, sd["f_re"], sd["f_im"], prm["ssm_b_re"], prm["ssm_b_im"], dbbar_re, dbbar_im)
    cts = (r["dabar_re"].reshape(g, n), r["dabar_im"].reshape(g, n), dfr.reshape(g, n), dfi.reshape(g, n))
    da_re, da_im, dlog_dt = _zoh_bwd(l, sd["a_re"], sd["a_im"], sd["log_dt"], cts)
    return dict(ssm_a_re=da_re, ssm_a_im=da_im, ssm_log_dt=dlog_dt.reshape(g), ssm_b_re=db_re, ssm_b_im=db_im,
                ssm_c_re=_block_diag_extract(r["dcblk_re"], p, n), ssm_c_im=_block_diag_extract(r["dcblk_im"], p, n),
                ssm_d=r["dd"].reshape(g, p))


def _ffn_weight_grads(l, fb, dx2, s, place):
    d = dx2.shape[1]
    hcn = fb["act"].shape[1]
    g = {}
    for name, key, rhs, ts in (("ffn_w_gate", "dgate", fb["h2"], s), ("ffn_w_up", "dup", fb["h2"], s),
                               ("ffn_w_down", "act", dx2, min(1024, s))):
        g[name] = _tn_matmul(f"d{name}_l{l}", fb[key], BS((None, hcn, ts), lambda j, t, pr: (j, 0, t)), rhs,
                             BS((ts, d), lambda j, t, pr: (t, 0)), (hcn, d), (N_CHIPS, s // ts), place)
    return g


def _in_weight_grad(l, ht, dz, place):
    d, s = ht.shape
    ncw = dz.shape[1] // N_CHIPS
    return _tn_matmul(f"dw_in_l{l}", ht, BS((d, s), lambda j, t, pr: (0, 0)), dz, BS((s, ncw), lambda j, t, pr: (0, j)),
                      (d, ncw), (N_CHIPS, 1), place)


def _fused_tn(name, pairs, kinds, s, place):
    ts = min(512, s)
    n = len(pairs)

    def shape_of(a, b, kind):
        k, m = a.shape[1], b.shape[1]
        if kind == "rows":
            return (N_CHIPS, k // N_CHIPS, m)
        if kind == "cols":
            return (N_CHIPS, k, m // N_CHIPS)
        return (k // 128, 128, 128)

    shapes = [shape_of(a, b, kind) for (a, b), kind in zip(pairs, kinds)]
    last = s // ts - 1
    out_shape = []
    for shp, kind in zip(shapes, kinds):
        out_shape += [SDS(shp, F32)] if kind == "groups" else [SDS(shp[1:], F32), SDS(shp, WIRE_DTYPE)]

    def body(place_ref, *refs):
        ins, outs, accs = refs[:2 * n], refs[2 * n:2 * n + len(out_shape)], refs[2 * n + len(out_shape):]
        first = pl.program_id(0) == 0
        for i, kind in enumerate(kinds):
            a, b = ins[2 * i][...], ins[2 * i + 1][...]
            acc = accs[i]

            @pl.when(first)
            def _():
                acc[...] = jnp.zeros(acc.shape, F32)

            if kind == "rows":
                acc[...] += _mm_tn(a, b).reshape(acc.shape)
            elif kind == "cols":
                full = _mm_tn(a, b)
                nc = acc.shape[2]
                for j in range(N_CHIPS):
                    acc[j] += full[:, j * nc:(j + 1) * nc]
            else:
                for k in range(acc.shape[0]):
                    acc[k] += _mm_tn(a[:, k * 128:(k + 1) * 128], b[:, k * 128:(k + 1) * 128])

        @pl.when(pl.program_id(0) == last)
        def _():
            o = 0
            for i, kind in enumerate(kinds):
                if kind == "groups":
                    outs[o][...] = accs[i][...]
                    o += 1
                else:
                    outs[o][...] = accs[i][place_ref[0]]
                    outs[o + 1][...] = accs[i][...].astype(WIRE_DTYPE)
                    o += 2

    whole = lambda shp: BS(shp, lambda t, pr: (0,) * len(shp))
    outs = pl.pallas_call(
        body, name=name,
        grid_spec=pltpu.PrefetchScalarGridSpec(
            num_scalar_prefetch=1, grid=(s // ts,),
            in_specs=[BS((ts, v.shape[1]), lambda t, pr: (t, 0)) for pair in pairs for v in pair],
            out_specs=[whole(o.shape) for o in out_shape],
            scratch_shapes=[pltpu.VMEM(shp, F32) for shp in shapes]),
        out_shape=out_shape, compiler_params=_params(),
    )(place, *[v for pair in pairs for v in pair])
    res, o = [], 0
    for kind in kinds:
        if kind == "groups":
            res.append(outs[o])
            o += 1
        else:
            res.append((outs[o], outs[o + 1]))
            o += 2
    return res


def _mixer_weight_grads(l, sv, mb, dx1, s, place):
    g = {}
    (g["w_out"], g["ssm_w_glu"]) = _fused_tn(f"dw_out_glu_l{l}", [(mb["merged"], dx1), (mb["ge"], mb["dt"])],
                                            ("rows", "rows"), s, place)
    (g["ssm_w_proj"], g["conv_w_proj"], g["pool_w_proj"]) = _fused_tn(
        f"dw_proj_l{l}", [(mb["sa"], mb["dya"]), (mb["ac"], mb["dyb"]), (mb["pp"], mb["dyc"])],
        ("cols", "cols", "cols"), s, place)
    (dwgrp,) = _fused_tn(f"dpool_w_group_l{l}", [(sv["p"], mb["dq"])], ("groups",), s, place)
    return g, dwgrp


def _local_step(x, target, weights_of, prm, place, on_grads=None):
    s, d = x.shape
    cw = prm["ssm_b_glu"].shape[1]
    sp = {k: prm[k].reshape(N_LAYERS, 1, -1) for k in ("norm1", "norm2", "b_gate", "ssm_b_glu", "conv_ln_g", "conv_ln_b",
                                                        "pool_scale", "conv_b_dw")}
    sp["pool_w_group"] = prm["pool_w_group"]
    saved = []
    xin = x
    for l in range(N_LAYERS):
        fw = weights_of(l, "in", (xin,))
        sd = _ssm_prepare(l, prm)
        z, h = _in_proj(l, xin, sp["norm1"], fw["w_in"])
        hre, him, y = _ssm_fwd(l, z, sd["bblk_re"], sd["bblk_im"], sd["cblk_re"], sd["cblk_im"], sd["pw"], sd["dskip"])
        p = _pool_fwd(l, z, cw)
        fw.update(weights_of(l, "mixer", (y, p)))
        wdw = fw["conv_w_dw"]
        hc = _conv_fwd(l, z, wdw, sp["conv_b_dw"][l])
        x1 = _merge_fwd(l, xin, y, hc, p, z, fw, sp)
        fw.update(weights_of(l, "ffn", (x1,)))
        x2 = _ffn_fwd(l, x1, sp["norm2"], fw["ffn_w_gate"], fw["ffn_w_up"], fw["ffn_w_down"])
        saved.append(dict(x=xin, z=z, h=h, hre=hre, him=him, y=y, hc=hc, p=p, x1=x1, sd=sd, wdw=wdw, fw=fw))
        xin = x2
    dx, loss, dfinal = _loss_head(xin, target, prm["final_norm"].reshape(1, d))
    big = [None] * N_LAYERS
    small = [None] * N_LAYERS
    norm2_rows = sp["norm2"]
    started = (lambda l, group, grads: on_grads(l, group, grads)) if on_grads is not None else (lambda *a: 0.0)
    for l in reversed(range(N_LAYERS)):
        sv = saved[l]
        sd, fw = sv["sd"], sv["fw"]
        fb = _ffn_bwd(l, sv["x1"], dx, norm2_rows, fw["ffn_w_gate"], fw["ffn_w_up"], fw["ffn_w_down"])
        big[l] = _ffn_weight_grads(l, fb, dx, s, place)
        spl = dict(sp, ssm_b_glu=sp["ssm_b_glu"] + started(l, "ffn", big[l]))
        mb = _merge_bwd(l, fb["dx1"], sv["y"], sv["hc"], sv["p"], sv["z"], fw, spl)
        mixer, dwgrp = _mixer_weight_grads(l, sv, mb, fb["dx1"], s, place)
        big[l].update(mixer)
        wdw = sv["wdw"] + started(l, "mixer", mixer)
        du_c = _pool_bwd(l, mb["dp"])
        dv1, dv2, dwdw, dbdw = _conv_bwd(l, mb["dhc"], sv["z"], wdw)
        sr = _ssm_bwd(l, mb["dy"], sv["z"], sv["hre"], sv["him"], sd["bblk_re"], sd["bblk_im"], sd["cblk_re"],
                      sd["cblk_im"], sd["pw"], sd["dskip"])
        dx, dz, dnorm1 = _in_proj_bwd(l, fb["dx1"], sv["x"], sp["norm1"], fw["w_in"], sr["du"], dv1, dv2, du_c, mb["dzg"])
        w_in_grad = {"w_in": _in_weight_grad(l, sv["h"], dz, place)}
        big[l].update(w_in_grad)
        sg = _ssm_param_grads(l, sd, sr, prm)
        sg.update(norm1=dnorm1.reshape(d), b_gate=mb["db_gate"].reshape(3 * d), ssm_b_glu=mb["db_glu"].reshape(cw),
                  conv_b_dw=dbdw.reshape(cw), conv_ln_g=mb["dln_g"].reshape(cw), conv_ln_b=mb["dln_b"].reshape(cw),
                  pool_w_group=dwgrp, pool_scale=mb["dscale"].reshape(cw), norm2=fb["dnorm2"].reshape(d),
                  conv_w_dw=dwdw)
        small[l] = sg
        if l == N_LAYERS - 1:
            sg = dict(sg, final_norm=dfinal.reshape(d))
        norm2_rows = sp["norm2"] + (started(l, "in", w_in_grad) + started(l, "small", sg))
    return loss[0, 0], dx, big, small, dfinal.reshape(d)


def _place():
    return lax.axis_index("x"), lax.axis_index("y"), lax.axis_index("c")


def _other_chips(x, y):
    return [(1 - x, y), (x, 1 - y), (1 - x, 1 - y)]


def _remote(src, dst, send_sem, recv_sem, device):
    return pltpu.make_async_remote_copy(src_ref=src, dst_ref=dst, send_sem=send_sem, recv_sem=recv_sem,
                                        device_id=device, device_id_type=MESH)


def _hbm(v):
    return pltpu.with_memory_space_constraint(v, pltpu.HBM)


def _cast_into(name, w, place, dtype, after=()):
    nl, k, n = w.shape
    tr = _row_tile(k, n)
    nt = k // tr

    def body(place_ref, w_ref, *rest):
        o0_ref, o1_ref = rest[len(after):]

        @pl.when(pl.program_id(0) == 0)
        def _():
            o0_ref[...] = w_ref[...].astype(dtype)

        @pl.when(pl.program_id(0) == 1)
        def _():
            o1_ref[...] = w_ref[...].astype(dtype)

    return pl.pallas_call(
        body, name=f"cast_{name}",
        grid_spec=pltpu.PrefetchScalarGridSpec(
            num_scalar_prefetch=1, grid=(nl, nt),
            in_specs=[BS((None, tr, n), lambda l, t, pr: (l, t, 0))] + [ANY] * len(after),
            out_specs=[BS((None, tr, n), lambda l, t, pr: (pr[0], t * (1 - l) + (nt - 1) * l, 0)),
                       BS((None, tr, n), lambda l, t, pr: (pr[0], t * l, 0))]),
        out_shape=[SDS((N_CHIPS, k, n), dtype)] * 2)(place, w, *after)


def _gather_rows(buf, c):
    k = buf.shape[1]
    if k % 2:
        return pl.ds(0, k)
    return pl.ds(pl.multiple_of(c * (k // 2), 8), k // 2)


def _allgather_start(tag, groups):
    ng = len(groups)
    sizes = [len(g) for g in groups]
    first = [sum(sizes[:g]) for g in range(ng)]
    flat = [b for g in groups for b in g]
    nb = len(flat)

    def body(*refs):
        ins = refs[:nb]
        sems = refs[nb:nb + 2 * ng]
        token = refs[-1]
        x, y, c = _place()
        jme = 2 * x + y
        for g in range(ng):
            for a in range(sizes[g]):
                buf = ins[first[g] + a]
                blk = buf.at[jme, _gather_rows(buf, c)]
                for k, (cx, cy) in enumerate(_other_chips(x, y)):
                    _remote(blk, blk, sems[2 * g].at[3 * a + k], sems[2 * g + 1].at[3 * a + k], (cx, cy, c)).start()
        token[...] = jnp.zeros(token.shape, F32)

    sem_shapes = [pltpu.SemaphoreType.DMA((3 * sizes[g // 2],)) for g in range(2 * ng)]
    outs = pl.pallas_call(
        body, name=f"allgather_start_{tag}", in_specs=[HBM] * nb,
        out_specs=[SEM] * (2 * ng) + [HBM] * nb + [pl.BlockSpec(memory_space=pltpu.VMEM)],
        out_shape=sem_shapes + [pltpu.HBM(b.shape, b.dtype) for b in flat] + [SDS((8, 128), F32)],
        input_output_aliases={i: 2 * ng + i for i in range(nb)},
        compiler_params=pltpu.CompilerParams(has_side_effects=SIDE_EFFECT))(*[_hbm(b) for b in flat])
    per_group = [(outs[2 * g], outs[2 * g + 1], outs[2 * ng + first[g]:2 * ng + first[g] + sizes[g]])
                 for g in range(ng)]
    return per_group, outs[-1]


def _allgather_wait(l, send_sems, recv_sems, bufs, after):
    n = len(bufs)

    def body(*refs):
        ins = refs[:n]
        ssem, rsem = refs[n], refs[n + 1]
        x, y, c = _place()
        jme = 2 * x + y
        for a in range(n):
            rows = _gather_rows(ins[a], c)
            for k, (cx, cy) in enumerate(_other_chips(x, y)):
                cp = _remote(ins[a].at[jme, rows], ins[a].at[2 * cx + cy, rows], ssem.at[3 * a + k],
                             rsem.at[3 * a + k], (cx, cy, c))
                cp.wait_send()
                cp.wait_recv()

    return pl.pallas_call(
        body, name=f"allgather_wait_{l}", in_specs=[HBM] * n + [SEM, SEM] + [ANY] * len(after), out_specs=[HBM] * n,
        out_shape=[pltpu.HBM(b.shape, b.dtype) for b in bufs], input_output_aliases={i: i for i in range(n)},
        compiler_params=pltpu.CompilerParams(has_side_effects=SIDE_EFFECT))(*bufs, send_sems, recv_sems, *after)


def _allgather_forward(l, bufs):
    n = len(bufs)
    split = [a for a in range(n) if bufs[a].shape[1] % 2 == 0]

    def body(*refs):
        ins = refs[:n]
        send_sems, recv_sems = refs[2 * n:]
        x, y, c = _place()
        sibling = (x, y, 1 - c)
        copies = []
        for a in split:
            for k, (cx, cy) in enumerate(_other_chips(x, y)):
                blk = ins[a].at[2 * cx + cy, _gather_rows(ins[a], c)]
                cp = _remote(blk, blk, send_sems.at[a, k], recv_sems.at[a, k], sibling)
                cp.start()
                copies.append(cp)
        for a in split:
            for k, (cx, cy) in enumerate(_other_chips(x, y)):
                blk = ins[a].at[2 * cx + cy, _gather_rows(ins[a], 1 - c)]
                _remote(blk, blk, send_sems.at[a, k], recv_sems.at[a, k], sibling).wait_recv()
        for cp in copies:
            cp.wait_send()

    sem = pltpu.SemaphoreType.DMA((n, 3))
    return pl.pallas_call(
        body, name=f"allgather_forward_{l}", in_specs=[ANY] * n, out_specs=[ANY] * n,
        out_shape=[SDS(b.shape, b.dtype) for b in bufs], input_output_aliases={i: i for i in range(n)},
        scratch_shapes=[sem, sem])(*bufs)


def _rs_to_owner(l, parts):
    n = len(parts)
    lands = [lax.empty((3,) + p.shape[1:], p.dtype) for p in parts]

    def body(*refs):
        ins, zones = refs[:n], refs[n:2 * n]
        send_sems, recv_sems = refs[2 * n], refs[2 * n + 1]
        token = refs[-1]
        x, y, c = _place()
        for a in range(n):
            for k, (cx, cy) in enumerate(_other_chips(x, y)):
                _remote(ins[a].at[2 * cx + cy], zones[a].at[k], send_sems.at[3 * a + k], recv_sems.at[3 * a + k],
                        (cx, cy, c)).start()
        token[...] = jnp.zeros(token.shape, F32)

    sem = pltpu.SemaphoreType.DMA((3 * n,))
    outs = pl.pallas_call(
        body, name=f"rs_to_owner_start_{l}", in_specs=[HBM] * (2 * n),
        out_specs=[SEM, SEM] + [HBM] * (2 * n) + [pl.BlockSpec(memory_space=pltpu.VMEM)],
        out_shape=[sem, sem] + [pltpu.HBM(p.shape, p.dtype) for p in parts]
        + [pltpu.HBM(z.shape, z.dtype) for z in lands] + [SDS((8, 128), F32)],
        input_output_aliases={i: 2 + i for i in range(2 * n)},
        compiler_params=pltpu.CompilerParams(has_side_effects=SIDE_EFFECT),
    )(*[_hbm(p) for p in parts], *[_hbm(z) for z in lands])
    return outs[0], outs[1], outs[2:2 + n], outs[2 + n:2 + 2 * n], outs[-1]


def _rs_to_owner_wait(l, send_sems, recv_sems, parts, lands, after):
    n = len(parts)

    def body(*refs):
        ins, zones = refs[:n], refs[n:2 * n]
        ssem, rsem = refs[2 * n], refs[2 * n + 1]
        x, y, c = _place()
        for a in range(n):
            for k, (cx, cy) in enumerate(_other_chips(x, y)):
                cp = _remote(ins[a].at[2 * cx + cy], zones[a].at[k], ssem.at[3 * a + k], rsem.at[3 * a + k],
                             (cx, cy, c))
                cp.wait_send()
                cp.wait_recv()

    outs = pl.pallas_call(
        body, name=f"rs_to_owner_wait_{l}", in_specs=[HBM] * (2 * n) + [SEM, SEM] + [ANY] * len(after),
        out_specs=[HBM] * (2 * n),
        out_shape=[pltpu.HBM(p.shape, p.dtype) for p in parts] + [pltpu.HBM(z.shape, z.dtype) for z in lands],
        input_output_aliases={i: i for i in range(2 * n)},
        compiler_params=pltpu.CompilerParams(has_side_effects=SIDE_EFFECT),
    )(*parts, *lands, send_sems, recv_sems, *after)
    return outs[:n], outs[n:]


def _rs_sibling_exchange(l, both):
    n = len(both)

    def body(*refs):
        ins = refs[:n]
        send_sems, recv_sems = refs[2 * n:]
        x, y, c = _place()
        copies = []
        for a in range(n):
            cp = _remote(ins[a].at[c], ins[a].at[c], send_sems.at[a], recv_sems.at[a], (x, y, 1 - c))
            cp.start()
            copies.append(cp)
        for a, cp in enumerate(copies):
            cp.wait_send()
            _remote(ins[a].at[1 - c], ins[a].at[1 - c], send_sems.at[a], recv_sems.at[a], (x, y, 1 - c)).wait_recv()

    sem = pltpu.SemaphoreType.DMA((n,))
    return pl.pallas_call(
        body, name=f"rs_sibling_exchange_{l}", in_specs=[ANY] * n, out_specs=[ANY] * n,
        out_shape=[SDS(b.shape, b.dtype) for b in both], input_output_aliases={i: i for i in range(n)},
        scratch_shapes=[sem, sem])(*both)


def _add_owner(name, grad, recv, place):
    r, cols = grad.shape
    tr = _row_tile(r, cols, budget=1024 * 1024)
    nt = r // tr

    def body(place_ref, g_ref, r_ref, o_ref):
        acc = ((g_ref[...] + r_ref[0].astype(F32)) + r_ref[1].astype(F32)) + r_ref[2].astype(F32)
        o_ref[...] = acc.astype(o_ref.dtype)

    return pl.pallas_call(
        body, name=name,
        grid_spec=pltpu.PrefetchScalarGridSpec(
            num_scalar_prefetch=1, grid=(nt,),
            in_specs=[BS((tr, cols), lambda t, pr: (t, 0)), BS((3, tr, cols), lambda t, pr: (0, t, 0))],
            out_specs=BS((None, tr, cols), lambda t, pr: (pr[1], t, 0))),
        out_shape=SDS((2, r, cols), WIRE_DTYPE))(place, grad, recv)


def _reduce_start(tag, grads):
    names = list(grads)
    send_sems, recv_sems, wires, lands, token = _rs_to_owner(tag, [grads[n][1] for n in names])
    return dict(tag=tag, names=names, send_sems=send_sems, recv_sems=recv_sems, wires=wires, lands=lands,
                grads=[grads[n][0] for n in names]), token


def _reduce_finish(pending, place, after):
    tag, names = pending["tag"], pending["names"]
    _, lands = _rs_to_owner_wait(tag, pending["send_sems"], pending["recv_sems"], pending["wires"],
                                 pending["lands"], after)
    mine = [_add_owner(f"rs_add_owner_{n}_{tag}", g, r, place) for n, g, r in zip(names, pending["grads"], lands)]
    return dict(zip(names, _rs_sibling_exchange(tag, mine)))


def _small_peers(x, y, c):
    return [(x, y, 1 - c)] + [(cx, cy, c) for cx, cy in _other_chips(x, y)]


def _allgather_rows_start(tag, bufs):
    n = len(bufs)
    lands = [lax.empty((8,) + b.shape, b.dtype) for b in bufs]

    def body(*refs):
        ins, zones = refs[:n], refs[n:2 * n]
        send_sems, recv_sems = refs[2 * n], refs[2 * n + 1]
        token = refs[-1]
        x, y, c = _place()
        for a in range(n):
            for i, peer in enumerate(_small_peers(x, y, c)):
                _remote(ins[a], zones[a].at[4 * x + 2 * y + c], send_sems.at[4 * a + i], recv_sems.at[4 * a + i],
                        peer).start()
        token[...] = jnp.zeros(token.shape, F32)

    sem = pltpu.SemaphoreType.DMA((4 * n,))
    outs = pl.pallas_call(
        body, name=f"allgather_small_start_{tag}", in_specs=[HBM] * (2 * n),
        out_specs=[SEM, SEM] + [HBM] * (2 * n) + [pl.BlockSpec(memory_space=pltpu.VMEM)],
        out_shape=[sem, sem] + [pltpu.HBM(b.shape, b.dtype) for b in bufs]
        + [pltpu.HBM(z.shape, z.dtype) for z in lands] + [SDS((8, 128), F32)],
        input_output_aliases={i: 2 + i for i in range(2 * n)},
        compiler_params=pltpu.CompilerParams(has_side_effects=SIDE_EFFECT),
    )(*[_hbm(b) for b in bufs], *[_hbm(z) for z in lands])
    return outs[0], outs[1], outs[2:2 + n], outs[2 + n:2 + 2 * n], outs[-1]


def _allgather_rows_wait(tag, send_sems, recv_sems, bufs, lands, after):
    n = len(bufs)

    def body(*refs):
        ins, zones = refs[:n], refs[n:2 * n]
        ssem, rsem = refs[2 * n], refs[2 * n + 1]
        x, y, c = _place()
        for a in range(n):
            for i, (px, py, pc) in enumerate(_small_peers(x, y, c)):
                cp = _remote(ins[a], zones[a].at[4 * px + 2 * py + pc], ssem.at[4 * a + i], rsem.at[4 * a + i],
                             (px, py, pc))
                cp.wait_send()
                cp.wait_recv()

    outs = pl.pallas_call(
        body, name=f"allgather_small_wait_{tag}", in_specs=[HBM] * (2 * n) + [SEM, SEM, ANY],
        out_specs=[HBM] * (2 * n),
        out_shape=[pltpu.HBM(b.shape, b.dtype) for b in bufs] + [pltpu.HBM(z.shape, z.dtype) for z in lands],
        input_output_aliases={i: i for i in range(2 * n)},
        compiler_params=pltpu.CompilerParams(has_side_effects=SIDE_EFFECT),
    )(*bufs, *lands, send_sems, recv_sems, after)
    return outs[:n], outs[n:]


def _allgather_rows_forward(tag, lands):
    n = len(lands)

    def body(*refs):
        ins = refs[:n]
        send_sems, recv_sems = refs[2 * n:]
        x, y, c = _place()
        sibling = (x, y, 1 - c)
        copies = []
        for a in range(n):
            for k, (cx, cy) in enumerate(_other_chips(x, y)):
                blk = ins[a].at[4 * cx + 2 * cy + c]
                cp = _remote(blk, blk, send_sems.at[a, k], recv_sems.at[a, k], sibling)
                cp.start()
                copies.append(cp)
        for a in range(n):
            for k, (cx, cy) in enumerate(_other_chips(x, y)):
                blk = ins[a].at[4 * cx + 2 * cy + 1 - c]
                _remote(blk, blk, send_sems.at[a, k], recv_sems.at[a, k], sibling).wait_recv()
        for cp in copies:
            cp.wait_send()

    sem = pltpu.SemaphoreType.DMA((n, 3))
    return pl.pallas_call(body, name=f"allgather_small_forward_{tag}", in_specs=[ANY] * n, out_specs=[ANY] * n,
                          out_shape=[SDS(z.shape, z.dtype) for z in lands],
                          input_output_aliases={i: i for i in range(n)}, scratch_shapes=[sem, sem])(*lands)


def _sum_devices(tag, gathered, mine, place):
    _, r, cols = gathered.shape
    tr = _row_tile(r, cols, budget=256 * 1024)

    def body(place_ref, g_ref, x_ref, o_ref):
        me = 2 * place_ref[0] + place_ref[1]
        acc = jnp.where(me == 0, x_ref[...], g_ref[0])
        for k in range(1, 8):
            acc = acc + jnp.where(me == k, x_ref[...], g_ref[k])
        o_ref[...] = acc

    return pl.pallas_call(
        body, name=f"sum_small_grads_{tag}",
        grid_spec=pltpu.PrefetchScalarGridSpec(
            num_scalar_prefetch=1, grid=(r // tr,),
            in_specs=[BS((8, tr, cols), lambda t, pr: (0, t, 0)), BS((tr, cols), lambda t, pr: (t, 0))],
            out_specs=BS((tr, cols), lambda t, pr: (t, 0))),
        out_shape=SDS((r, cols), F32))(place, gathered, mine)


def _adamw_values(w, g, m, v):
    m = ADAM_B1 * m + (1.0 - ADAM_B1) * g
    v = ADAM_B2 * v + (1.0 - ADAM_B2) * (g * g)
    m_hat = m / (1.0 - ADAM_B1 ** ADAM_STEP)
    v_hat = v / (1.0 - ADAM_B2 ** ADAM_STEP)
    delta = -ADAM_LR * (m_hat / (jnp.sqrt(v_hat) + ADAM_EPS) + ADAM_WD * w)
    return delta, m, v


def _adamw_big(name, l, w, m, v, g, earlier=None, after=()):
    nl, r, cols = w.shape
    tr = _row_tile(r, cols, budget=1024 * 1024)
    nt = r // tr
    n_prev = 0 if earlier is None else 4

    def body(*refs):
        w_ref, m_ref, v_ref, g_ref = refs[:4]
        go_ref, d_ref, mo_ref, vo_ref = refs[4 + n_prev + len(after):]
        gv = g_ref[0].astype(F32) + g_ref[1].astype(F32)
        delta, m_new, v_new = _adamw_values(w_ref[...], gv, m_ref[...], v_ref[...])
        go_ref[...] = gv
        d_ref[...] = delta
        mo_ref[...] = m_new
        vo_ref[...] = v_new

    layer = BS((None, tr, cols), lambda t: (l, t, 0))
    return pl.pallas_call(
        body, name=f"adamw_{name}_l{l}", grid=(nt,),
        in_specs=[layer, layer, layer, BS((2, tr, cols), lambda t: (0, t, 0))] + [ANY] * (n_prev + len(after)),
        out_specs=[layer] * 4, out_shape=[SDS(w.shape, F32)] * 4,
        input_output_aliases={4 + i: i for i in range(n_prev)}, compiler_params=_params(),
    )(w, m, v, g, *(earlier or ()), *after)


def _adamw_mid(name, w, m, v, gathered, mine, place):
    shape = w.shape[1:]
    zeros = (0,) * len(shape)

    def body(place_ref, w_ref, m_ref, v_ref, *refs):
        gath, own = refs[:N_LAYERS], refs[N_LAYERS:2 * N_LAYERS]
        go_ref, d_ref, mo_ref, vo_ref = refs[2 * N_LAYERS:]
        me = 2 * place_ref[0] + place_ref[1]
        sums = []
        for l in range(N_LAYERS):
            acc = jnp.where(me == 0, own[l][...], gath[l][0])
            for k in range(1, 8):
                acc = acc + jnp.where(me == k, own[l][...], gath[l][k])
            sums.append(acc)
        gv = sums[0]
        for l in range(1, N_LAYERS):
            gv = jnp.where(pl.program_id(0) == l, sums[l], gv)
        delta, m_new, v_new = _adamw_values(w_ref[...], gv, m_ref[...], v_ref[...])
        go_ref[...] = gv
        d_ref[...] = delta
        mo_ref[...] = m_new
        vo_ref[...] = v_new

    layer = BS((None,) + shape, lambda l, pr: (l,) + zeros)
    return pl.pallas_call(
        body, name=f"adamw_{name}",
        grid_spec=pltpu.PrefetchScalarGridSpec(
            num_scalar_prefetch=1, grid=(N_LAYERS,),
            in_specs=[layer] * 3 + [BS((8,) + shape, lambda l, pr: (0,) + zeros)] * N_LAYERS
            + [BS(shape, lambda l, pr: zeros)] * N_LAYERS,
            out_specs=[layer] * 4),
        out_shape=[SDS(w.shape, F32)] * 4, compiler_params=_params())(place, w, m, v, *gathered, *mine)


def _adamw_rows(w, m, v, g):
    r, cols = w.shape
    tr = _row_tile(r, cols, budget=512 * 1024)

    def body(w_ref, m_ref, v_ref, g_ref, d_ref, mo_ref, vo_ref):
        delta, m_new, v_new = _adamw_values(w_ref[...], g_ref[...], m_ref[...], v_ref[...])
        d_ref[...] = delta
        mo_ref[...] = m_new
        vo_ref[...] = v_new

    spec = BS((tr, cols), lambda t: (t, 0))
    return pl.pallas_call(body, name="adamw_small", grid=(r // tr,), in_specs=[spec] * 4, out_specs=[spec] * 3,
                          out_shape=[SDS(w.shape, F32)] * 3)(w, m, v, g)


PACK_ALIGN = 8 * 128
PACK_ROWS = 128


def _pack_rows(arrays):
    parts, rows = [], 0
    for a in arrays:
        flat = a.reshape(-1)
        pad = (-flat.shape[0]) % PACK_ALIGN
        if pad:
            flat = jnp.pad(flat, (0, pad))
        parts.append(flat.reshape(-1, 128))
        rows += parts[-1].shape[0]
    if rows % PACK_ROWS:
        parts.append(jnp.zeros((PACK_ROWS - rows % PACK_ROWS, 128), parts[0].dtype))
    return jnp.concatenate(parts, axis=0)


def _unpack_rows(buf, shapes):
    out, row = [], 0
    for shape in shapes:
        size = math.prod(shape)
        rows = -(-size // PACK_ALIGN) * (PACK_ALIGN // 128)
        out.append(buf[row:row + rows].reshape(-1)[:size].reshape(shape))
        row += rows
    return out


def kernel(x, norm1, w_in, b_gate, ssm_a_re, ssm_a_im, ssm_log_dt, ssm_b_re, ssm_b_im, ssm_c_re, ssm_c_im, ssm_d, ssm_w_glu, ssm_b_glu, ssm_w_proj, conv_w_dw, conv_b_dw, conv_ln_g, conv_ln_b, conv_w_proj, pool_w_group, pool_scale, pool_w_proj, w_out, norm2, ffn_w_gate, ffn_w_up, ffn_w_down, final_norm, loss_target, m_norm1, m_w_in, m_b_gate, m_ssm_a_re, m_ssm_a_im, m_ssm_log_dt, m_ssm_b_re, m_ssm_b_im, m_ssm_c_re, m_ssm_c_im, m_ssm_d, m_ssm_w_glu, m_ssm_b_glu, m_ssm_w_proj, m_conv_w_dw, m_conv_b_dw, m_conv_ln_g, m_conv_ln_b, m_conv_w_proj, m_pool_w_group, m_pool_scale, m_pool_w_proj, m_w_out, m_norm2, m_ffn_w_gate, m_ffn_w_up, m_ffn_w_down, m_final_norm, v_norm1, v_w_in, v_b_gate, v_ssm_a_re, v_ssm_a_im, v_ssm_log_dt, v_ssm_b_re, v_ssm_b_im, v_ssm_c_re, v_ssm_c_im, v_ssm_d, v_ssm_w_glu, v_ssm_b_glu, v_ssm_w_proj, v_conv_w_dw, v_conv_b_dw, v_conv_ln_g, v_conv_ln_b, v_conv_w_proj, v_pool_w_group, v_pool_scale, v_pool_w_proj, v_w_out, v_norm2, v_ffn_w_gate, v_ffn_w_up, v_ffn_w_down, v_final_norm):
    given = dict(locals())
    cx, cy, cc = _place()
    place = jnp.stack([2 * cx + cy, cc]).astype(jnp.int32)

    def kernel_view(n, a):
        if n in TRANSPOSED:
            return a.transpose(0, 2, 1)
        return a.transpose(0, 1, 3, 2) if n in ("ssm_b_re", "ssm_b_im") else a

    prm = {n: given[n] for n in WEIGHTS}
    mom = {n: given["m_" + n] for n in WEIGHTS}
    var = {n: given["v_" + n] for n in WEIGHTS}
    for n in MID:
        prm[n], mom[n], var[n] = kernel_view(n, prm[n]), kernel_view(n, mom[n]), kernel_view(n, var[n])

    dw_shard = prm["conv_w_dw"].reshape(N_LAYERS, CONV_KERNEL, -1)
    casts = {"w_in": _cast_into("w_in", prm["w_in"], place, MXU_DTYPE)}
    first, first_started = _allgather_start("first", [[casts["w_in"][0]]])
    in_flight = {(0, "in"): first[0]}
    casts.update({n: _cast_into(n, kernel_view(n, prm[n]), place, MXU_DTYPE, after=(first_started,))
                  for n in BIG if n != "w_in"})
    casts["conv_w_dw"] = _cast_into("conv_w_dw", dw_shard, place, F32, after=(first_started,))
    order = [(l, g) for l in range(N_LAYERS) for g in GATHER_GROUPS if (l, g) != (0, "in")]
    rest, rest_started = _allgather_start("rest", [[casts[n][l] for n in GATHER_GROUPS[g]] for l, g in order])
    in_flight.update(zip(order, rest))

    def weights_of(l, group, after):
        send_sems, recv_sems, bufs = in_flight[l, group]
        tag = f"l{l}_{group}"
        if (l, group) == (0, "in"):
            after = after + (rest_started,)
        bufs = _allgather_forward(tag, _allgather_wait(tag, send_sems, recv_sems, bufs, after))
        fw = dict(zip(GATHER_GROUPS[group], bufs))
        if "conv_w_dw" in fw:
            fw["conv_w_dw"] = fw["conv_w_dw"].transpose(1, 0, 2).reshape(CONV_KERNEL, -1)
        return fw

    pending, small_pending, small_shapes = {}, {}, {}
    tokens = {}

    def on_grads(l, group, grads):
        if group == "small":
            packed = {n: g for n, g in grads.items() if n not in MID}
            small_shapes[l] = {n: g.shape for n, g in packed.items()}
            begun = _allgather_rows_start(f"l{l}", [_pack_rows(list(packed.values()))] + [grads[n] for n in MID])
            small_pending[l], token = begun[:4], begun[4]
        else:
            pending[l, group], token = _reduce_start(f"{l}_{group}", grads)
        tokens[l, group] = token
        return token[0, 0]

    loss, dx, _, _, _ = _local_step(x[0], loss_target[0], weights_of, prm, place, on_grads)
    loss = lax.psum(loss, ("x", "y", "c"))

    reduced = [{} for _ in range(N_LAYERS)]
    out = {}

    def finish(l, group, after):
        reduced[l].update(_reduce_finish(pending[l, group], place, after))

    def adamw(l, names, done):
        for n in names:
            out[n] = _adamw_big(n, l, kernel_view(n, prm[n]), kernel_view(n, mom[n]), kernel_view(n, var[n]),
                                reduced[l][n], out.get(n), after=done)
            done = (out[n][0],)
        return done

    top = N_LAYERS - 1
    done = (tokens[0, "in"], tokens[0, "small"])
    for group in ("ffn", "mixer", "in"):
        finish(top, group, done)
    done = adamw(top, BIG, done)
    for group in ("ffn", "mixer", "in"):
        finish(0, group, done)
        done = adamw(0, [n for n in GATHER_GROUPS[group] if n in BIG], done)
    for n in BIG:
        out[n] = tuple(kernel_view(n, a) for a in out[n])

    gsmall = {}
    mid_mine, mid_gathered = [], []
    for l in range(N_LAYERS):
        mine, lands = _allgather_rows_wait(f"l{l}", *small_pending[l], done[0])
        lands = _allgather_rows_forward(f"l{l}", lands)
        mid_mine.append(mine[1:])
        mid_gathered.append(lands[1:])
        gsum = _sum_devices(f"l{l}", lands[0], mine[0], place)
        for n, g in zip(small_shapes[l], _unpack_rows(gsum, list(small_shapes[l].values()))):
            gsmall.setdefault(n, [None] * N_LAYERS)[l] = g
    for i, n in enumerate(MID):
        out[n] = tuple(kernel_view(n, a) for a in _adamw_mid(
            n, prm[n], mom[n], var[n], [mid_gathered[l][i] for l in range(N_LAYERS)],
            [mid_mine[l][i] for l in range(N_LAYERS)], place))
    gsmall = {n: (g[top] if n == "final_norm" else jnp.stack(g)) for n, g in gsmall.items()}
    lanes = dw_shard.shape[-1]
    gsmall["conv_w_dw"] = lax.dynamic_slice_in_dim(gsmall["conv_w_dw"], (2 * cx + cy) * lanes, lanes, axis=2)
    small_names = [n for n in SMALL if n not in MID] + ["conv_w_dw"]
    w_rows = _pack_rows([prm[n] for n in small_names])
    m_rows = _pack_rows([mom[n] for n in small_names])
    v_rows = _pack_rows([var[n] for n in small_names])
    g_rows = _pack_rows([gsmall[n] for n in small_names])
    shapes = [prm[n].shape for n in small_names]
    d_s, m_s, v_s = (_unpack_rows(r, shapes) for r in _adamw_rows(w_rows, m_rows, v_rows, g_rows))
    for i, n in enumerate(small_names):
        out[n] = (gsmall[n].reshape(prm[n].shape), d_s[i], m_s[i], v_s[i])
    grads = [out[n][0] for n in WEIGHTS]
    deltas = [out[n][1] for n in WEIGHTS]
    new_m = [out[n][2] for n in WEIGHTS]
    new_v = [out[n][3] for n in WEIGHTS]
    return (loss, dx[None], *grads, *deltas, *new_m, *new_v)
```

```python
import functools
import math

import jax
import jax.numpy as jnp
from jax import lax
from jax.experimental import pallas as pl
from jax.experimental.pallas import tpu as pltpu

F32 = jnp.float32
MXU_DTYPE = jnp.bfloat16
WIRE_DTYPE = jnp.bfloat16
SDS = jax.ShapeDtypeStruct
BS = pl.BlockSpec
ANY = pl.BlockSpec(memory_space=pl.ANY)
HBM = pl.BlockSpec(memory_space=pltpu.HBM)
SEM = pl.BlockSpec(memory_space=pltpu.SEMAPHORE)
SIDE_EFFECT = pltpu.SideEffectType.DATAFLOW_SIDE_EFFECTING
MESH = pl.DeviceIdType.MESH

EPS = 1e-6
N_CHIPS = 4
N_LAYERS = 2
SSM_GROUPS, SSM_STATE, SSM_GROUP = 32, 64, 16
CONV_KERNEL = 31
CONV_PAD = 32
POOL_WINDOWS = (2, 4, 8, 16)
GELU_C = math.sqrt(2.0 / math.pi)
ADAM_LR, ADAM_B1, ADAM_B2, ADAM_EPS, ADAM_WD, ADAM_STEP = 0.001, 0.9, 0.999, 1e-08, 0.01, 10
VMEM_LIMIT = 56 * 1024 * 1024

BIG = ("w_in", "ssm_w_glu", "ssm_w_proj", "conv_w_proj", "pool_w_proj", "w_out", "ffn_w_gate", "ffn_w_up", "ffn_w_down")
TRANSPOSED = ("ffn_w_gate", "ffn_w_up")
MID = ("ssm_b_re", "ssm_b_im", "ssm_c_re", "ssm_c_im")
GATHER_GROUPS = {
    "in": ("w_in",),
    "mixer": ("ssm_w_glu", "ssm_w_proj", "conv_w_proj", "pool_w_proj", "w_out", "conv_w_dw"),
    "ffn": ("ffn_w_gate", "ffn_w_up", "ffn_w_down"),
}
SMALL = ("norm1", "b_gate", "ssm_a_re", "ssm_a_im", "ssm_log_dt", "ssm_b_re", "ssm_b_im", "ssm_c_re", "ssm_c_im",
         "ssm_d", "ssm_b_glu", "conv_b_dw", "conv_ln_g", "conv_ln_b", "pool_w_group", "pool_scale", "norm2",
         "final_norm")
WEIGHTS = ("norm1", "w_in", "b_gate", "ssm_a_re", "ssm_a_im", "ssm_log_dt", "ssm_b_re", "ssm_b_im", "ssm_c_re",
           "ssm_c_im", "ssm_d", "ssm_w_glu", "ssm_b_glu", "ssm_w_proj", "conv_w_dw", "conv_b_dw", "conv_ln_g",
           "conv_ln_b", "conv_w_proj", "pool_w_group", "pool_scale", "pool_w_proj", "w_out", "norm2", "ffn_w_gate",
           "ffn_w_up", "ffn_w_down", "final_norm")


def _params(vmem=True):
    return pltpu.CompilerParams(vmem_limit_bytes=VMEM_LIMIT) if vmem else None


def _mm(a, b):
    return jnp.dot(a.astype(MXU_DTYPE), b.astype(MXU_DTYPE), preferred_element_type=F32)


def _mm_nt(a, b):
    return lax.dot_general(a.astype(MXU_DTYPE), b.astype(MXU_DTYPE), (((1,), (1,)), ((), ())),
                           preferred_element_type=F32)


def _mm_tn(a, b):
    return lax.dot_general(a.astype(MXU_DTYPE), b.astype(MXU_DTYPE), (((0,), (0,)), ((), ())),
                           preferred_element_type=F32)


def _sigmoid(x):
    return jax.nn.sigmoid(x)


def _gelu(x):
    t = jnp.tanh(GELU_C * (x + 0.044715 * (x * x * x)))
    return x * (0.5 * (1.0 + t)), t


def _gelu_grad(x, t):
    return 0.5 * (1.0 + t) + 0.5 * x * (1.0 - t * t) * (GELU_C * (1.0 + 3.0 * 0.044715 * x * x))


def _colsum(v):
    return jnp.sum(v, axis=0, keepdims=True)


def _row_tile(rows, cols, itemsize=4, budget=1536 * 1024):
    best = None
    for t in range(8, rows + 1, 8):
        if rows % t == 0 and t * cols * itemsize <= budget:
            best = t
    return best if best is not None else rows


def _in_proj(l, x, norm1, w_in):
    s, d = x.shape
    nc = w_in.shape[-1]
    tm = min(1024, s)
    nt = s // tm

    def body(x_ref, g_ref, w_ref, z_ref, h_ref, h_all):
        i = pl.program_id(1)
        rows = pl.ds(pl.multiple_of(i * tm, tm), tm)

        @pl.when(pl.program_id(0) == 0)
        def _():
            xv = x_ref[...]
            r = lax.rsqrt(jnp.mean(xv * xv, axis=-1, keepdims=True) + EPS)
            hv = (xv * r * g_ref[...]).astype(h_ref.dtype)
            h_ref[...] = hv.T
            h_all[rows, :] = hv

        z_ref[...] = _mm(h_all[rows, :], w_ref[...])

    tile_of = lambda j, i: i * (1 - jnp.minimum(j, 1)) + (nt - 1) * jnp.minimum(j, 1)
    return pl.pallas_call(
        body, name=f"in_proj_l{l}", grid=(N_CHIPS, nt),
        in_specs=[BS((tm, d), lambda j, i: (tile_of(j, i), 0)), BS((None, 1, d), lambda j, i: (l, 0, 0)),
                  BS((None, d, nc), lambda j, i: (j, 0, 0))],
        out_specs=[BS((tm, nc), lambda j, i: (i, j)), BS((d, tm), lambda j, i: (0, tile_of(j, i)))],
        out_shape=[SDS((s, N_CHIPS * nc), F32), SDS((d, s), MXU_DTYPE)],
        scratch_shapes=[pltpu.VMEM((s, d), MXU_DTYPE)], compiler_params=_params())(x, norm1, w_in)


def _mm_cols(a, w_ref):
    return jnp.concatenate([_mm(a, w_ref[j]) for j in range(N_CHIPS)], axis=1)


def _mm_nt_cols(dv, w_ref):
    nc = w_ref.shape[-1]
    acc = _mm_nt(dv[:, 0:nc], w_ref[0])
    for j in range(1, N_CHIPS):
        acc = acc + _mm_nt(dv[:, j * nc:(j + 1) * nc], w_ref[j])
    return acc


def _merge_values(y, hc, p, zg, wglu, bglu, wpa, wpb, wpc, lng, lnb, wgrp, scale, bg):
    v = {}
    ge, th = _gelu(y)
    t = _mm(ge, wglu) + bglu
    sg = _sigmoid(t)
    sa = ge * sg
    ya = _mm_cols(sa, wpa)
    mu = jnp.mean(hc, axis=-1, keepdims=True)
    xc = hc - mu
    r = lax.rsqrt(jnp.mean(xc * xc, axis=-1, keepdims=True) + EPS)
    xh = xc * r
    ln = xh * lng + lnb
    sl = _sigmoid(ln)
    ac = ln * sl
    yb = _mm_cols(ac, wpb)
    gw = p.shape[1] // len(POOL_WINDOWS)
    q = jnp.concatenate([_mm(p[:, k * gw:(k + 1) * gw], wgrp[k]) for k in range(len(POOL_WINDOWS))], axis=1)
    pp = q * scale
    yc = _mm_cols(pp, wpc)
    d = ya.shape[1]
    gates = [_sigmoid(zg[k] + bg[:, k * d:(k + 1) * d]) for k in range(3)]
    merged = gates[0] * ya + gates[1] * yb + gates[2] * yc
    v.update(ge=ge, th=th, sg=sg, sa=sa, ya=ya, r=r, xh=xh, ln=ln, sl=sl, ac=ac, yb=yb, q=q, pp=pp, yc=yc,
             gates=gates, merged=merged)
    return v


def _merge_specs(l, tm, d, cw):
    row = lambda n: BS((None, 1, n), lambda i: (l, 0, 0))
    resident = lambda shp: BS(shp, lambda i: (0, 0, 0), pipeline_mode=pl.Buffered(1))
    return [
        BS((tm, cw), lambda i: (i, 0)),
        BS((tm, cw), lambda i: (i, 0)),
        BS((tm, cw), lambda i: (i, 0)),
        BS((tm, d), lambda i: (i, 2)), BS((tm, d), lambda i: (i, 3)), BS((tm, d), lambda i: (i, 4)),
        resident((N_CHIPS, cw // N_CHIPS, cw)),
        row(cw),
        resident((N_CHIPS, cw, d // N_CHIPS)),
        resident((N_CHIPS, cw, d // N_CHIPS)),
        resident((N_CHIPS, cw, d // N_CHIPS)),
        row(cw), row(cw),
        BS((None, 4, cw // 4, cw // 4), lambda i: (l, 0, 0, 0)),
        row(cw),
        row(3 * d),
        resident((N_CHIPS, d // N_CHIPS, d)),
    ]


def _merge_fwd(l, x, y, hc, p, z, fw, sp):
    s, d = x.shape
    cw = y.shape[1]
    tm = min(512, s)

    def body(x_ref, y_ref, hc_ref, p_ref, z0, z1, z2, wglu, bglu, wpa, wpb, wpc, lng, lnb, wgrp, scale, bg, wout,
             x1_ref):
        v = _merge_values(y_ref[...], hc_ref[...], p_ref[...], (z0[...], z1[...], z2[...]),
                          wglu[...].reshape(cw, cw), bglu[...], wpa, wpb, wpc, lng[...], lnb[...], wgrp, scale[...],
                          bg[...])
        x1_ref[...] = x_ref[...] + _mm(v["merged"], wout[...].reshape(d, d))

    return pl.pallas_call(
        body, name=f"merge_fwd_l{l}", grid=(s // tm,),
        in_specs=[BS((tm, d), lambda i: (i, 0))] + _merge_specs(l, tm, d, cw),
        out_specs=BS((tm, d), lambda i: (i, 0)), out_shape=SDS((s, d), F32), compiler_params=_params(),
    )(x, y, hc, p, z, z, z, fw["ssm_w_glu"], sp["ssm_b_glu"], fw["ssm_w_proj"], fw["conv_w_proj"], fw["pool_w_proj"],
      sp["conv_ln_g"], sp["conv_ln_b"], sp["pool_w_group"], sp["pool_scale"], sp["b_gate"], fw["w_out"])


def _merge_bwd(l, dx1, y, hc, p, z, fw, sp):
    s, d = dx1.shape
    cw = y.shape[1]
    tm = min(256, s)
    m = MXU_DTYPE

    def body(dx1_ref, y_ref, hc_ref, p_ref, z0, z1, z2, wglu, bglu, wpa, wpb, wpc, lng, lnb, wgrp, scale, bg, wout,
             dzg_ref, dy_ref, dhc_ref, dp_ref, merged_ref, sa_ref, ac_ref, pp_ref, ge_ref, dt_ref, dya_ref, dyb_ref,
             dyc_ref, dq_ref, dbg_ref, dbglu_ref, dlng_ref, dlnb_ref, dscale_ref):
        yv = y_ref[...]
        wg = wglu[...].reshape(cw, cw)
        v = _merge_values(yv, hc_ref[...], p_ref[...], (z0[...], z1[...], z2[...]), wg, bglu[...], wpa, wpb, wpc,
                          lng[...], lnb[...], wgrp, scale[...], bg[...])
        dm = _mm_nt(dx1_ref[...], wout[...].reshape(d, d))
        ys = (v["ya"], v["yb"], v["yc"])
        dys, dbg = [], []
        for k in range(3):
            gk = v["gates"][k]
            dzk = dm * ys[k] * (gk * (1.0 - gk))
            dbg.append(_colsum(dzk))
            dzg_ref[:, k * d:(k + 1) * d] = dzk.astype(m)
            dys.append((dm * gk).astype(m))
        dsa = _mm_nt_cols(dys[0], wpa)
        dac = _mm_nt_cols(dys[1], wpb)
        dpp = _mm_nt_cols(dys[2], wpc)
        ge, sg = v["ge"], v["sg"]
        dt = dsa * ge * (sg * (1.0 - sg))
        dge = dsa * sg + _mm_nt(dt, wg)
        dy_ref[...] = dge * _gelu_grad(yv, v["th"])
        ln, sl, xh = v["ln"], v["sl"], v["xh"]
        dln = dac * (sl * (1.0 + ln * (1.0 - sl)))
        dxh = dln * lng[...]
        dhc_ref[...] = v["r"] * (dxh - jnp.mean(dxh, axis=-1, keepdims=True)
                                 - xh * jnp.mean(dxh * xh, axis=-1, keepdims=True))
        dq = dpp * scale[...]
        gw = cw // len(POOL_WINDOWS)
        for k in range(len(POOL_WINDOWS)):
            dp_ref[:, k * gw:(k + 1) * gw] = _mm_nt(dq[:, k * gw:(k + 1) * gw], wgrp[k])
        merged_ref[...] = v["merged"].astype(m)
        sa_ref[...] = v["sa"].astype(m)
        ac_ref[...] = v["ac"].astype(m)
        pp_ref[...] = v["pp"].astype(m)
        ge_ref[...] = ge.astype(m)
        dt_ref[...] = dt.astype(m)
        dya_ref[...] = dys[0]
        dyb_ref[...] = dys[1]
        dyc_ref[...] = dys[2]
        dq_ref[...] = dq.astype(m)

        @pl.when(pl.program_id(0) == 0)
        def _():
            for ref in (dbg_ref, dbglu_ref, dlng_ref, dlnb_ref, dscale_ref):
                ref[...] = jnp.zeros(ref.shape, F32)

        dbg_ref[...] += jnp.concatenate(dbg, axis=1)
        dbglu_ref[...] += _colsum(dt)
        dlng_ref[...] += _colsum(dln * xh)
        dlnb_ref[...] += _colsum(dln)
        dscale_ref[...] += _colsum(dpp * v["q"])

    tile = lambda n: BS((tm, n), lambda i: (i, 0))
    acc = lambda n: BS((1, n), lambda i: (0, 0))
    outs = pl.pallas_call(
        body, name=f"merge_bwd_l{l}", grid=(s // tm,),
        in_specs=[tile(d)] + _merge_specs(l, tm, d, cw),
        out_specs=[tile(3 * d), tile(cw), tile(cw), tile(cw), tile(d), tile(cw), tile(cw), tile(cw), tile(cw), tile(cw),
                   tile(d), tile(d), tile(d), tile(cw), acc(3 * d), acc(cw), acc(cw), acc(cw), acc(cw)],
        out_shape=[SDS((s, 3 * d), m), SDS((s, cw), F32), SDS((s, cw), F32), SDS((s, cw), F32), SDS((s, d), m),
                   SDS((s, cw), m), SDS((s, cw), m), SDS((s, cw), m), SDS((s, cw), m), SDS((s, cw), m), SDS((s, d), m),
                   SDS((s, d), m), SDS((s, d), m), SDS((s, cw), m), SDS((1, 3 * d), F32), SDS((1, cw), F32),
                   SDS((1, cw), F32), SDS((1, cw), F32), SDS((1, cw), F32)],
        compiler_params=_params(),
    )(dx1, y, hc, p, z, z, z, fw["ssm_w_glu"], sp["ssm_b_glu"], fw["ssm_w_proj"], fw["conv_w_proj"], fw["pool_w_proj"],
      sp["conv_ln_g"], sp["conv_ln_b"], sp["pool_w_group"], sp["pool_scale"], sp["b_gate"], fw["w_out"])
    names = ("dzg", "dy", "dhc", "dp", "merged", "sa", "ac", "pp", "ge", "dt", "dya", "dyb", "dyc", "dq", "db_gate",
             "db_glu", "dln_g", "dln_b", "dscale")
    return dict(zip(names, outs))


def _ffn_fwd(l, x1, norm2, wg, wu, wd):
    s, d = x1.shape
    hc = wd.shape[1]
    tm = min(1024, s)

    def body(x_ref, g_ref, wg_ref, wu_ref, wd_ref, o_ref, h_scr):
        @pl.when(pl.program_id(1) == 0)
        def _():
            xv = x_ref[...]
            r = lax.rsqrt(jnp.mean(xv * xv, axis=-1, keepdims=True) + EPS)
            h_scr[...] = (xv * r * g_ref[...]).astype(h_scr.dtype)
            o_ref[...] = xv

        h = h_scr[...]
        gate = _mm_nt(h, wg_ref[...])
        up = _mm_nt(h, wu_ref[...])
        o_ref[...] += _mm(gate * _sigmoid(gate) * up, wd_ref[...])

    return pl.pallas_call(
        body, name=f"ffn_fwd_l{l}", grid=(s // tm, N_CHIPS),
        in_specs=[BS((tm, d), lambda i, j: (i, 0)), BS((None, 1, d), lambda i, j: (l, 0, 0)),
                  BS((None, hc, d), lambda i, j: (j, 0, 0)), BS((None, hc, d), lambda i, j: (j, 0, 0)),
                  BS((None, hc, d), lambda i, j: (j, 0, 0))],
        out_specs=BS((tm, d), lambda i, j: (i, 0)), out_shape=SDS((s, d), F32),
        scratch_shapes=[pltpu.VMEM((tm, d), MXU_DTYPE)], compiler_params=_params())(x1, norm2, wg, wu, wd)


def _ffn_bwd(l, x1, dx2, norm2, wg, wu, wd):
    s, d = x1.shape
    hc = wd.shape[1]
    tm = min(512, s)
    m = MXU_DTYPE
    last = N_CHIPS - 1

    def body(x_ref, dx2_ref, g_ref, wg_ref, wu_ref, wd_ref, dx1_ref, h_ref, act_ref, dgate_ref, dup_ref, dn_ref,
             dh_scr, dxb_scr):
        i, j = pl.program_id(0), pl.program_id(1)

        @pl.when(j == 0)
        def _():
            xv = x_ref[...]
            r = lax.rsqrt(jnp.mean(xv * xv, axis=-1, keepdims=True) + EPS)
            h_ref[...] = (xv * r * g_ref[...]).astype(m)
            dxb_scr[...] = dx2_ref[...].astype(m)
            dh_scr[...] = jnp.zeros(dh_scr.shape, F32)

        @pl.when((i == 0) & (j == 0))
        def _():
            dn_ref[...] = jnp.zeros(dn_ref.shape, F32)

        h = h_ref[...]
        gate = _mm_nt(h, wg_ref[...])
        up = _mm_nt(h, wu_ref[...])
        sg = _sigmoid(gate)
        silu = gate * sg
        act_ref[...] = (silu * up).astype(m).T
        dact = _mm_nt(dxb_scr[...], wd_ref[...])
        dup = (dact * silu).astype(m)
        dgate = (dact * up * (sg * (1.0 + gate * (1.0 - sg)))).astype(m)
        dup_ref[...] = dup.T
        dgate_ref[...] = dgate.T
        dh_scr[...] += _mm(dgate, wg_ref[...]) + _mm(dup, wu_ref[...])

        @pl.when(j == last)
        def _():
            xv = x_ref[...]
            r = lax.rsqrt(jnp.mean(xv * xv, axis=-1, keepdims=True) + EPS)
            xh = xv * r
            dh = dh_scr[...]
            dn_ref[...] += _colsum(dh * xh)
            dxh = dh * g_ref[...]
            dx1_ref[...] = dx2_ref[...] + r * (dxh - xh * jnp.mean(dxh * xh, axis=-1, keepdims=True))

    chunk = BS((None, hc, tm), lambda i, j: (j, 0, i))
    outs = pl.pallas_call(
        body, name=f"ffn_bwd_l{l}", grid=(s // tm, N_CHIPS),
        in_specs=[BS((tm, d), lambda i, j: (i, 0)), BS((tm, d), lambda i, j: (i, 0)),
                  BS((None, 1, d), lambda i, j: (l, 0, 0)),
                  BS((None, hc, d), lambda i, j: (j, 0, 0)), BS((None, hc, d), lambda i, j: (j, 0, 0)),
                  BS((None, hc, d), lambda i, j: (j, 0, 0))],
        out_specs=[BS((tm, d), lambda i, j: (i, 0)), BS((tm, d), lambda i, j: (i, 0)), chunk, chunk, chunk,
                   BS((1, d), lambda i, j: (0, 0))],
        out_shape=[SDS((s, d), F32), SDS((s, d), m), SDS((N_CHIPS, hc, s), m), SDS((N_CHIPS, hc, s), m),
                   SDS((N_CHIPS, hc, s), m), SDS((1, d), F32)],
        scratch_shapes=[pltpu.VMEM((tm, d), F32), pltpu.VMEM((tm, d), m)], compiler_params=_params(),
    )(x1, dx2, norm2, wg, wu, wd)
    return dict(zip(("dx1", "h2", "act", "dgate", "dup", "dnorm2"), outs))


def _loss_head(x, target, gf):
    s, d = x.shape
    tm = min(512, s)

    def body(x_ref, t_ref, g_ref, dx_ref, loss_ref, dg_ref):
        @pl.when(pl.program_id(0) == 0)
        def _():
            loss_ref[...] = jnp.zeros(loss_ref.shape, F32)
            dg_ref[...] = jnp.zeros(dg_ref.shape, F32)

        xv = x_ref[...]
        r = lax.rsqrt(jnp.mean(xv * xv, axis=-1, keepdims=True) + EPS)
        xh = xv * r
        err = xh * g_ref[...] - t_ref[...]
        loss_ref[...] += 0.5 * jnp.sum(jnp.mean(err * err, axis=-1, keepdims=True), axis=0, keepdims=True)
        dyv = err * (1.0 / d)
        dg_ref[...] += _colsum(dyv * xh)
        dxh = dyv * g_ref[...]
        dx_ref[...] = r * (dxh - xh * jnp.mean(dxh * xh, axis=-1, keepdims=True))

    return pl.pallas_call(
        body, name="loss_head", grid=(s // tm,),
        in_specs=[BS((tm, d), lambda i: (i, 0)), BS((tm, d), lambda i: (i, 0)), BS((1, d), lambda i: (0, 0))],
        out_specs=[BS((tm, d), lambda i: (i, 0)), BS((1, 1), lambda i: (0, 0)), BS((1, d), lambda i: (0, 0))],
        out_shape=[SDS((s, d), F32), SDS((1, 1), F32), SDS((1, d), F32)], compiler_params=_params())(x, target, gf)


def _in_proj_bwd(l, dres, x, norm1, w_in, du_a, dv1, dv2, du_c, dzg):
    s, d = x.shape
    nc = w_in.shape[-1]
    tm = min(512, s)
    m = MXU_DTYPE

    def body(dres_ref, x_ref, g_ref, w_ref, a_ref, b1_ref, b2_ref, c_ref, g3_ref, dx_ref, dz_ref, dn_ref):
        @pl.when(pl.program_id(0) == 0)
        def _():
            dn_ref[...] = jnp.zeros(dn_ref.shape, F32)

        dz = jnp.concatenate([a_ref[...], b1_ref[...], b2_ref[...], c_ref[...], g3_ref[...]], axis=1).astype(m)
        dz_ref[...] = dz
        dh = _mm_nt_cols(dz, w_ref)
        xv = x_ref[...]
        r = lax.rsqrt(jnp.mean(xv * xv, axis=-1, keepdims=True) + EPS)
        xh = xv * r
        dn_ref[...] += _colsum(dh * xh)
        dxh = dh * g_ref[...]
        dx_ref[...] = dres_ref[...] + r * (dxh - xh * jnp.mean(dxh * xh, axis=-1, keepdims=True))

    tile = lambda n: BS((tm, n), lambda i: (i, 0))
    return pl.pallas_call(
        body, name=f"in_proj_bwd_l{l}", grid=(s // tm,),
        in_specs=[tile(d), tile(d), BS((None, 1, d), lambda i: (l, 0, 0)),
                  BS((N_CHIPS, d, nc), lambda i: (0, 0, 0), pipeline_mode=pl.Buffered(1)),
                  tile(du_a.shape[1]), tile(dv1.shape[1]), tile(dv2.shape[1]), tile(du_c.shape[1]), tile(dzg.shape[1])],
        out_specs=[tile(d), tile(N_CHIPS * nc), BS((1, d), lambda i: (0, 0))],
        out_shape=[SDS((s, d), F32), SDS((s, N_CHIPS * nc), m), SDS((1, d), F32)], compiler_params=_params(),
    )(dres, x, norm1, w_in, du_a, dv1, dv2, du_c, dzg)


def _tn_matmul(name, a, a_spec, b, b_spec, chunk_shape, grid, place):
    last = grid[1] - 1

    def body(place_ref, a_ref, b_ref, own_ref, wire_ref, *acc):
        part = _mm(a_ref[...], b_ref[...])

        def emit(total):
            wire_ref[...] = total.astype(WIRE_DTYPE)

            @pl.when(pl.program_id(0) == place_ref[0])
            def _():
                own_ref[...] = total

        if last == 0:
            emit(part)
        else:
            @pl.when(pl.program_id(1) == 0)
            def _():
                acc[0][...] = part

            @pl.when(pl.program_id(1) > 0)
            def _():
                acc[0][...] += part

            @pl.when(pl.program_id(1) == last)
            def _():
                emit(acc[0][...])

    zeros = (0,) * len(chunk_shape)
    return pl.pallas_call(
        body, name=name,
        grid_spec=pltpu.PrefetchScalarGridSpec(
            num_scalar_prefetch=1, grid=grid, in_specs=[a_spec, b_spec],
            out_specs=[BS(chunk_shape, lambda j, t, pr: zeros), BS((None,) + chunk_shape, lambda j, t, pr: (j,) + zeros)],
            scratch_shapes=[pltpu.VMEM(chunk_shape, F32)] if last else []),
        out_shape=[SDS(chunk_shape, F32), SDS((N_CHIPS,) + chunk_shape, WIRE_DTYPE)],
        compiler_params=_params())(place, a, b)


def _scan_consts(pw_ref, lanes, reverse):
    sgn = -1.0 if reverse else 1.0
    row = lax.broadcasted_iota(jnp.int32, (8, lanes), 0)
    steps = []
    for i, k in enumerate((1, 2, 4)):
        mask = (row < 8 - k) if reverse else (row >= k)
        steps.append((k, jnp.where(mask, pw_ref[2 * i], 0.0), jnp.where(mask, sgn * pw_ref[2 * i + 1], 0.0)))
    c = 4 if reverse else 3
    return steps, pw_ref[2 * c], sgn * pw_ref[2 * c + 1]


def _scan_block(br, bi, steps, row, reverse):
    for k, ar, ai in steps:
        sh = 8 - k if reverse else k
        sr = pltpu.roll(br, sh, 0)
        si = pltpu.roll(bi, sh, 0)
        br, bi = br + ar * sr - ai * si, bi + ar * si + ai * sr
    return br, bi


def _ssm_fwd(l, z, bblk_re, bblk_im, cblk_re, cblk_im, pw, dskip):
    s = z.shape[0]
    gc = bblk_re.shape[1]
    gl = bblk_re.shape[2]
    nblk = bblk_re.shape[0]

    def body(u_ref, bre, bim, cre, cim, pw_ref, d_ref, hre, him, y_ref):
        u = u_ref[...]
        hre[...] = _mm(u, bre[...])
        him[...] = _mm(u, bim[...])
        row = lax.broadcasted_iota(jnp.int32, (8, gl), 0)
        steps, car, cai = _scan_consts(pw_ref, gl, False)

        def step(i, carry):
            cr, ci = carry
            r0 = pl.multiple_of(i * 8, 8)
            br, bi = _scan_block(hre[pl.ds(r0, 8), :], him[pl.ds(r0, 8), :], steps, row, False)
            hr = br + car * cr - cai * ci
            hi = bi + car * ci + cai * cr
            hre[pl.ds(r0, 8), :] = hr
            him[pl.ds(r0, 8), :] = hi
            return jnp.broadcast_to(hr[7:8, :], (8, gl)), jnp.broadcast_to(hi[7:8, :], (8, gl))

        zero = jnp.zeros((8, gl), F32)
        lax.fori_loop(0, s // 8, step, (zero, zero))
        y_ref[...] = _mm_nt(hre[...], cre[...]) - _mm_nt(him[...], cim[...]) + d_ref[...] * u

    return pl.pallas_call(
        body, name=f"ssm_fwd_l{l}", grid=(nblk,),
        in_specs=[BS((s, gc), lambda k: (0, k)), BS((None, gc, gl), lambda k: (k, 0, 0)),
                  BS((None, gc, gl), lambda k: (k, 0, 0)), BS((None, gc, gl), lambda k: (k, 0, 0)),
                  BS((None, gc, gl), lambda k: (k, 0, 0)), BS((10, 8, gl), lambda k: (0, 0, k)),
                  BS((1, gc), lambda k: (0, k))],
        out_specs=[BS((s, gl), lambda k: (0, k)), BS((s, gl), lambda k: (0, k)), BS((s, gc), lambda k: (0, k))],
        out_shape=[SDS((s, nblk * gl), F32), SDS((s, nblk * gl), F32), SDS((s, nblk * gc), F32)],
        compiler_params=_params())(z, bblk_re, bblk_im, cblk_re, cblk_im, pw, dskip)


def _ssm_bwd(l, dy, z, hre, him, bblk_re, bblk_im, cblk_re, cblk_im, pw, dskip):
    s = z.shape[0]
    nblk, gc, gl = bblk_re.shape

    def body(dy_ref, u_ref, hre_ref, him_ref, bre, bim, cre, cim, pw_ref, d_ref,
             du_ref, dbre_ref, dbim_ref, dcre_ref, dcim_ref, dar_ref, dai_ref, dd_ref, gre, gim):
        dyv = dy_ref[...]
        u = u_ref[...]
        gre[...] = _mm(dyv, cre[...])
        gim[...] = -_mm(dyv, cim[...])
        dcre_ref[...] = _mm_tn(dyv, hre_ref[...])
        dcim_ref[...] = -_mm_tn(dyv, him_ref[...])
        dd_ref[...] = _colsum(dyv * u)
        row = lax.broadcasted_iota(jnp.int32, (8, gl), 0)
        steps, car, cai = _scan_consts(pw_ref, gl, True)
        n8 = s // 8

        def step(ii, carry):
            cr, ci, accr, acci = carry
            i = n8 - 1 - ii
            r0 = pl.multiple_of(i * 8, 8)
            br, bi = _scan_block(gre[pl.ds(r0, 8), :], gim[pl.ds(r0, 8), :], steps, row, True)
            dr = br + car * cr - cai * ci
            di = bi + car * ci + cai * cr
            gre[pl.ds(r0, 8), :] = dr
            gim[pl.ds(r0, 8), :] = di
            rp = pl.multiple_of(jnp.maximum(i - 1, 0) * 8, 8)
            keep = jnp.where(i > 0, 1.0, 0.0)
            pr = jnp.where(row >= 1, pltpu.roll(hre_ref[pl.ds(r0, 8), :], 1, 0),
                           keep * pltpu.roll(hre_ref[pl.ds(rp, 8), :], 1, 0))
            pi = jnp.where(row >= 1, pltpu.roll(him_ref[pl.ds(r0, 8), :], 1, 0),
                           keep * pltpu.roll(him_ref[pl.ds(rp, 8), :], 1, 0))
            accr = accr + dr * pr + di * pi
            acci = acci + di * pr - dr * pi
            return (jnp.broadcast_to(dr[0:1, :], (8, gl)), jnp.broadcast_to(di[0:1, :], (8, gl)), accr, acci)

        zero = jnp.zeros((8, gl), F32)
        _, _, accr, acci = lax.fori_loop(0, n8, step, (zero, zero, zero, zero))
        dar_ref[...] = _colsum(accr)
        dai_ref[...] = _colsum(acci)
        dbr = gre[...]
        dbi = gim[...]
        du_ref[...] = (dyv * d_ref[...] + _mm_nt(dbr, bre[...]) + _mm_nt(dbi, bim[...])).astype(du_ref.dtype)
        dbre_ref[...] = _mm_tn(u, dbr)
        dbim_ref[...] = _mm_tn(u, dbi)

    col = lambda n: BS((s, n), lambda k: (0, k))
    blk = lambda a, b: BS((None, a, b), lambda k: (k, 0, 0))
    outs = pl.pallas_call(
        body, name=f"ssm_bwd_l{l}", grid=(nblk,),
        in_specs=[col(gc), col(gc), col(gl), col(gl), blk(gc, gl), blk(gc, gl), blk(gc, gl), blk(gc, gl),
                  BS((10, 8, gl), lambda k: (0, 0, k)), BS((1, gc), lambda k: (0, k))],
        out_specs=[col(gc), blk(gc, gl), blk(gc, gl), blk(gc, gl), blk(gc, gl), BS((1, gl), lambda k: (0, k)),
                   BS((1, gl), lambda k: (0, k)), BS((1, gc), lambda k: (0, k))],
        out_shape=[SDS((s, nblk * gc), MXU_DTYPE), SDS((nblk, gc, gl), F32), SDS((nblk, gc, gl), F32),
                   SDS((nblk, gc, gl), F32), SDS((nblk, gc, gl), F32), SDS((1, nblk * gl), F32),
                   SDS((1, nblk * gl), F32), SDS((1, nblk * gc), F32)],
        scratch_shapes=[pltpu.VMEM((s, gl), F32), pltpu.VMEM((s, gl), F32)], compiler_params=_params(),
    )(dy, z, hre, him, bblk_re, bblk_im, cblk_re, cblk_im, pw, dskip)
    return dict(zip(("du", "dbblk_re", "dbblk_im", "dcblk_re", "dcblk_im", "dabar_re", "dabar_im", "dd"), outs))


def _conv_fwd(l, z, wdw, bdw):
    s = z.shape[0]
    cw = wdw.shape[1]
    lb = 128
    tr = min(256, s)
    off1 = cw // lb
    off2 = 2 * cw // lb

    def body(v1_ref, v2_ref, w_ref, b_ref, hc_ref, scr):
        scr[0:CONV_PAD, :] = jnp.zeros((CONV_PAD, lb), F32)
        scr[CONV_PAD:, :] = v1_ref[...] * _sigmoid(v2_ref[...])
        for t in range(s // tr):
            acc = jnp.broadcast_to(b_ref[...], (tr, lb))
            for k in range(CONV_KERNEL):
                acc = acc + w_ref[pl.ds(k, 1), :] * scr[pl.ds(t * tr + CONV_PAD - (CONV_KERNEL - 1) + k, tr), :]
            hc_ref[pl.ds(t * tr, tr), :] = acc

    return pl.pallas_call(
        body, name=f"conv_fwd_l{l}", grid=(cw // lb,),
        in_specs=[BS((s, lb), lambda k: (0, off1 + k)), BS((s, lb), lambda k: (0, off2 + k)),
                  BS((CONV_KERNEL, lb), lambda k: (0, k)), BS((1, lb), lambda k: (0, k))],
        out_specs=BS((s, lb), lambda k: (0, k)), out_shape=SDS((s, cw), F32),
        scratch_shapes=[pltpu.VMEM((s + CONV_PAD, lb), F32)], compiler_params=_params())(z, z, wdw, bdw)


def _conv_bwd(l, dhc, z, wdw):
    s = z.shape[0]
    cw = wdw.shape[1]
    lb = 128
    tr = min(256, s)
    off1 = cw // lb
    off2 = 2 * cw // lb
    nb = cw // lb

    def body(d_ref, v1_ref, v2_ref, w_ref, dv1_ref, dv2_ref, dw_ref, db_ref, hpad, dpad):
        v1 = v1_ref[...]
        sg = _sigmoid(v2_ref[...])
        dv = d_ref[...]
        hpad[0:CONV_PAD, :] = jnp.zeros((CONV_PAD, lb), F32)
        hpad[CONV_PAD:, :] = v1 * sg
        dpad[0:s, :] = dv
        dpad[s:, :] = jnp.zeros((CONV_PAD, lb), F32)
        db_ref[...] = _colsum(dv)
        dws = [jnp.zeros((1, lb), F32) for _ in range(CONV_KERNEL)]
        for t in range(s // tr):
            dt = d_ref[pl.ds(t * tr, tr), :]
            acc = jnp.zeros((tr, lb), F32)
            for k in range(CONV_KERNEL):
                acc = acc + w_ref[pl.ds(k, 1), :] * dpad[pl.ds(t * tr + (CONV_KERNEL - 1) - k, tr), :]
                dws[k] = dws[k] + _colsum(dt * hpad[pl.ds(t * tr + CONV_PAD - (CONV_KERNEL - 1) + k, tr), :])
            sgt = _sigmoid(v2_ref[pl.ds(t * tr, tr), :])
            v1t = v1_ref[pl.ds(t * tr, tr), :]
            dv1_ref[pl.ds(t * tr, tr), :] = (acc * sgt).astype(dv1_ref.dtype)
            dv2_ref[pl.ds(t * tr, tr), :] = (acc * v1t * (sgt * (1.0 - sgt))).astype(dv2_ref.dtype)
        for k in range(CONV_KERNEL):
            dw_ref[pl.ds(k, 1), :] = dws[k]

    return pl.pallas_call(
        body, name=f"conv_bwd_l{l}", grid=(nb,),
        in_specs=[BS((s, lb), lambda k: (0, k)), BS((s, lb), lambda k: (0, off1 + k)),
                  BS((s, lb), lambda k: (0, off2 + k)), BS((CONV_KERNEL, lb), lambda k: (0, k))],
        out_specs=[BS((s, lb), lambda k: (0, k)), BS((s, lb), lambda k: (0, k)),
                   BS((CONV_KERNEL, lb), lambda k: (0, k)), BS((1, lb), lambda k: (0, k))],
        out_shape=[SDS((s, cw), MXU_DTYPE), SDS((s, cw), MXU_DTYPE), SDS((CONV_KERNEL, cw), F32), SDS((1, cw), F32)],
        scratch_shapes=[pltpu.VMEM((s + CONV_PAD, lb), F32), pltpu.VMEM((s + CONV_PAD, lb), F32)],
        compiler_params=_params())(dhc, z, z, wdw)


def _pool_window(k):
    return jnp.where(k == 0, float(POOL_WINDOWS[0]),
                     jnp.where(k == 1, float(POOL_WINDOWS[1]),
                               jnp.where(k == 2, float(POOL_WINDOWS[2]), float(POOL_WINDOWS[3]))))


def _pool_fwd(l, z, pw_width):
    s = z.shape[0]
    lb = pw_width // len(POOL_WINDOWS)
    off = 3 * pw_width // lb

    def body(u_ref, p_ref):
        k = pl.program_id(0)
        u = u_ref[...]
        row = lax.broadcasted_iota(jnp.int32, (s, lb), 0)
        sums = [u]
        for sh in (1, 2, 4, 8):
            prev = sums[-1]
            sums.append(prev + jnp.where(row >= sh, pltpu.roll(prev, sh, 0), 0.0))
        sel = jnp.where(k == 0, sums[1], jnp.where(k == 1, sums[2], jnp.where(k == 2, sums[3], sums[4])))
        cnt = jnp.minimum((row + 1).astype(F32), _pool_window(k))
        p_ref[...] = sel / cnt - u

    return pl.pallas_call(
        body, name=f"pool_fwd_l{l}", grid=(len(POOL_WINDOWS),),
        in_specs=[BS((s, lb), lambda k: (0, off + k))], out_specs=BS((s, lb), lambda k: (0, k)),
        out_shape=SDS((s, pw_width), F32), compiler_params=_params())(z)


def _pool_bwd(l, dp):
    s, width = dp.shape
    lb = width // len(POOL_WINDOWS)

    def body(d_ref, du_ref):
        k = pl.program_id(0)
        dv = d_ref[...]
        row = lax.broadcasted_iota(jnp.int32, (s, lb), 0)
        cnt = jnp.minimum((row + 1).astype(F32), _pool_window(k))
        sums = [dv / cnt]
        for sh in (1, 2, 4, 8):
            prev = sums[-1]
            sums.append(prev + jnp.where(row < s - sh, pltpu.roll(prev, s - sh, 0), 0.0))
        sel = jnp.where(k == 0, sums[1], jnp.where(k == 1, sums[2], jnp.where(k == 2, sums[3], sums[4])))
        du_ref[...] = (sel - dv).astype(du_ref.dtype)

    return pl.pallas_call(
        body, name=f"pool_bwd_l{l}", grid=(len(POOL_WINDOWS),),
        in_specs=[BS((s, lb), lambda k: (0, k))], out_specs=BS((s, lb), lambda k: (0, k)),
        out_shape=SDS((s, width), MXU_DTYPE), compiler_params=_params())(dp)


def _zoh(a_re, a_im, log_dt):
    dt = jnp.exp(log_dt)
    mag = jnp.exp(dt * a_re)
    ang = dt * a_im
    abar_re = mag * jnp.cos(ang)
    abar_im = mag * jnp.sin(ang)
    den = a_re * a_re + a_im * a_im
    nr = abar_re - 1.0
    ni = abar_im
    f_re = (nr * a_re + ni * a_im) / den
    f_im = (ni * a_re - nr * a_im) / den
    return abar_re, abar_im, f_re, f_im


def _zoh_fwd(l, a_re, a_im, log_dt):
    def body(ar, ai, ld, o0, o1, o2, o3):
        for ref, val in zip((o0, o1, o2, o3), _zoh(ar[...], ai[...], ld[...])):
            ref[...] = val

    return pl.pallas_call(body, name=f"zoh_fwd_l{l}", out_shape=[SDS(a_re.shape, F32)] * 4)(a_re, a_im, log_dt)


def _zoh_bwd(l, a_re, a_im, log_dt, cts):
    def body(ar, ai, ld, c0, c1, c2, c3, dar, dai, dld):
        _, vjp = jax.vjp(_zoh, ar[...], ai[...], ld[...])
        g = vjp((c0[...], c1[...], c2[...], c3[...]))
        dar[...] = g[0]
        dai[...] = g[1]
        dld[...] = g[2]

    return pl.pallas_call(body, name=f"zoh_bwd_l{l}",
                          out_shape=[SDS(a_re.shape, F32), SDS(a_re.shape, F32), SDS(log_dt.shape, F32)],
                          )(a_re, a_im, log_dt, *cts)


def _bbar_fwd(l, f_re, f_im, b_re, b_im):
    g, p, n = b_re.shape[1:]

    def body(fr, fi, br, bi, o_re, o_im):
        o_re[...] = (fr[...] * br[...] - fi[...] * bi[...]).astype(o_re.dtype)
        o_im[...] = (fr[...] * bi[...] + fi[...] * br[...]).astype(o_im.dtype)

    whole = lambda shp: BS(shp, lambda i: (0,) * len(shp))
    layer = BS((None, g, p, n), lambda i: (l, 0, 0, 0))
    return pl.pallas_call(body, name=f"bbar_fwd_l{l}", grid=(1,),
                          in_specs=[whole((g, 1, n)), whole((g, 1, n)), layer, layer],
                          out_specs=[whole((g, p, n))] * 2,
                          out_shape=[SDS((g, p, n), MXU_DTYPE)] * 2)(f_re, f_im, b_re, b_im)


def _bbar_bwd(l, f_re, f_im, b_re, b_im, d_re, d_im):
    g, p, n = b_re.shape[1:]

    def body(fr, fi, br, bi, dr, di, dfr, dfi, dbr, dbi):
        dfr[...] = jnp.sum(dr[...] * br[...] + di[...] * bi[...], axis=1, keepdims=True)
        dfi[...] = jnp.sum(di[...] * br[...] - dr[...] * bi[...], axis=1, keepdims=True)
        dbr[...] = fr[...] * dr[...] + fi[...] * di[...]
        dbi[...] = fr[...] * di[...] - fi[...] * dr[...]

    whole = lambda shp: BS(shp, lambda i: (0,) * len(shp))
    layer = BS((None, g, p, n), lambda i: (l, 0, 0, 0))
    return pl.pallas_call(body, name=f"bbar_bwd_l{l}", grid=(1,),
                          in_specs=[whole((g, 1, n)), whole((g, 1, n)), layer, layer, whole((g, p, n)),
                                    whole((g, p, n))],
                          out_specs=[whole((g, 1, n)), whole((g, 1, n)), whole((g, p, n)), whole((g, p, n))],
                          out_shape=[SDS((g, 1, n), F32), SDS((g, 1, n), F32), SDS((g, p, n), F32),
                                     SDS((g, p, n), F32)])(f_re, f_im, b_re, b_im, d_re, d_im)


def _powers(l, abar_re, abar_im):
    lanes = abar_re.shape[1]

    def body(ar_ref, ai_ref, o_ref):
        ar, ai = ar_ref[...], ai_ref[...]
        pows = [(ar, ai)]
        for _ in range(7):
            pr, pi = pows[-1]
            pows.append((pr * ar - pi * ai, pr * ai + pi * ar))
        row = lax.broadcasted_iota(jnp.int32, (8, lanes), 0)
        for i, k in enumerate((1, 2, 4)):
            o_ref[2 * i] = jnp.broadcast_to(pows[k - 1][0], (8, lanes))
            o_ref[2 * i + 1] = jnp.broadcast_to(pows[k - 1][1], (8, lanes))
        for slot, order in ((3, range(8)), (4, range(7, -1, -1))):
            vr = jnp.zeros((8, lanes), F32)
            vi = jnp.zeros((8, lanes), F32)
            for r, e in enumerate(order):
                vr = jnp.where(row == r, pows[e][0], vr)
                vi = jnp.where(row == r, pows[e][1], vi)
            o_ref[2 * slot] = vr
            o_ref[2 * slot + 1] = vi

    return pl.pallas_call(body, name=f"powers_l{l}", out_shape=SDS((10, 8, lanes), F32))(abar_re, abar_im)


def _block_diag(v):
    g, a, b = v.shape
    eye = jnp.eye(8, dtype=v.dtype)
    out = jnp.einsum("kgab,gh->kgahb", v.reshape(g // 8, 8, a, b), eye)
    return out.reshape(g // 8, 8 * a, 8 * b)


def _block_diag_extract(blk, a, b):
    n = blk.shape[0]
    v = blk.reshape(n, 8, a, 8, b)
    return jnp.einsum("kgahb,gh->kgab", v, jnp.eye(8, dtype=blk.dtype)).reshape(n * 8, a, b)


def _ssm_prepare(l, prm):
    g, n, p = SSM_GROUPS, SSM_STATE, SSM_GROUP
    a_re, a_im = prm["ssm_a_re"][l], prm["ssm_a_im"][l]
    log_dt = prm["ssm_log_dt"][l].reshape(g, 1)
    abar_re, abar_im, f_re, f_im = _zoh_fwd(l, a_re, a_im, log_dt)
    f_re, f_im = f_re.reshape(g, 1, n), f_im.reshape(g, 1, n)
    bbar_re, bbar_im = _bbar_fwd(l, f_re, f_im, prm["ssm_b_re"], prm["ssm_b_im"])
    pw = _powers(l, abar_re.reshape(1, g * n), abar_im.reshape(1, g * n))
    return dict(a_re=a_re, a_im=a_im, log_dt=log_dt, f_re=f_re, f_im=f_im,
                bblk_re=_block_diag(bbar_re), bblk_im=_block_diag(bbar_im),
                cblk_re=_block_diag(prm["ssm_c_re"][l].astype(MXU_DTYPE)),
                cblk_im=_block_diag(prm["ssm_c_im"][l].astype(MXU_DTYPE)), pw=pw,
                dskip=prm["ssm_d"][l].reshape(1, g * p))


def _ssm_param_grads(l, sd, r, prm):
    g, n, p = SSM_GROUPS, SSM_STATE, SSM_GROUP
    dbbar_re = _block_diag_extract(r["dbblk_re"], p, n)
    dbbar_im = _block_diag_extract(r["dbblk_im"], p, n)
    dfr, dfi, db_re, db_im = _bbar_bwd(l, sd["f_re"], sd["f_im"], prm["ssm_b_re"], prm["ssm_b_im"], dbbar_re, dbbar_im)
    cts = (r["dabar_re"].reshape(g, n), r["dabar_im"].reshape(g, n), dfr.reshape(g, n), dfi.reshape(g, n))
    da_re, da_im, dlog_dt = _zoh_bwd(l, sd["a_re"], sd["a_im"], sd["log_dt"], cts)
    return dict(ssm_a_re=da_re, ssm_a_im=da_im, ssm_log_dt=dlog_dt.reshape(g), ssm_b_re=db_re, ssm_b_im=db_im,
                ssm_c_re=_block_diag_extract(r["dcblk_re"], p, n), ssm_c_im=_block_diag_extract(r["dcblk_im"], p, n),
                ssm_d=r["dd"].reshape(g, p))


def _ffn_weight_grads(l, fb, dx2, s, place):
    d = dx2.shape[1]
    hcn = fb["act"].shape[1]
    g = {}
    ts = min(512, s)
    for name, key, rhs in (("ffn_w_gate", "dgate", fb["h2"]), ("ffn_w_up", "dup", fb["h2"]), ("ffn_w_down", "act", dx2)):
        g[name] = _tn_matmul(f"d{name}_l{l}", fb[key], BS((None, hcn, ts), lambda j, t, pr: (j, 0, t)), rhs,
                             BS((ts, d), lambda j, t, pr: (t, 0)), (hcn, d), (N_CHIPS, s // ts), place)
    return g


def _in_weight_grad(l, ht, dz, place):
    d, s = ht.shape
    ncw = dz.shape[1] // N_CHIPS
    ts = min(512, s)
    return _tn_matmul(f"dw_in_l{l}", ht, BS((d, ts), lambda j, t, pr: (0, t)), dz, BS((ts, ncw), lambda j, t, pr: (t, j)),
                      (d, ncw), (N_CHIPS, s // ts), place)


def _fused_tn(name, pairs, kinds, s, place):
    ts = min(512, s)
    n = len(pairs)

    def shape_of(a, b, kind):
        k, m = a.shape[1], b.shape[1]
        if kind == "rows":
            return (N_CHIPS, k // N_CHIPS, m)
        if kind == "cols":
            return (N_CHIPS, k, m // N_CHIPS)
        return (k // 128, 128, 128)

    shapes = [shape_of(a, b, kind) for (a, b), kind in zip(pairs, kinds)]
    last = s // ts - 1
    out_shape = []
    for shp, kind in zip(shapes, kinds):
        out_shape += [SDS(shp, F32)] if kind == "groups" else [SDS(shp[1:], F32), SDS(shp, WIRE_DTYPE)]

    def body(place_ref, *refs):
        ins, outs, accs = refs[:2 * n], refs[2 * n:2 * n + len(out_shape)], refs[2 * n + len(out_shape):]
        first = pl.program_id(0) == 0
        for i, kind in enumerate(kinds):
            a, b = ins[2 * i][...], ins[2 * i + 1][...]
            acc = accs[i]

            @pl.when(first)
            def _():
                acc[...] = jnp.zeros(acc.shape, F32)

            if kind == "rows":
                acc[...] += _mm_tn(a, b).reshape(acc.shape)
            elif kind == "cols":
                full = _mm_tn(a, b)
                nc = acc.shape[2]
                for j in range(N_CHIPS):
                    acc[j] += full[:, j * nc:(j + 1) * nc]
            else:
                for k in range(acc.shape[0]):
                    acc[k] += _mm_tn(a[:, k * 128:(k + 1) * 128], b[:, k * 128:(k + 1) * 128])

        @pl.when(pl.program_id(0) == last)
        def _():
            o = 0
            for i, kind in enumerate(kinds):
                if kind == "groups":
                    outs[o][...] = accs[i][...]
                    o += 1
                else:
                    outs[o][...] = accs[i][place_ref[0]]
                    outs[o + 1][...] = accs[i][...].astype(WIRE_DTYPE)
                    o += 2

    whole = lambda shp: BS(shp, lambda t, pr: (0,) * len(shp))
    outs = pl.pallas_call(
        body, name=name,
        grid_spec=pltpu.PrefetchScalarGridSpec(
            num_scalar_prefetch=1, grid=(s // ts,),
            in_specs=[BS((ts, v.shape[1]), lambda t, pr: (t, 0)) for pair in pairs for v in pair],
            out_specs=[whole(o.shape) for o in out_shape],
            scratch_shapes=[pltpu.VMEM(shp, F32) for shp in shapes]),
        out_shape=out_shape, compiler_params=_params(),
    )(place, *[v for pair in pairs for v in pair])
    res, o = [], 0
    for kind in kinds:
        if kind == "groups":
            res.append(outs[o])
            o += 1
        else:
            res.append((outs[o], outs[o + 1]))
            o += 2
    return res


def _mixer_weight_grads(l, sv, mb, dx1, s, place):
    g = {}
    (g["w_out"], g["ssm_w_glu"]) = _fused_tn(f"dw_out_glu_l{l}", [(mb["merged"], dx1), (mb["ge"], mb["dt"])],
                                            ("rows", "rows"), s, place)
    (g["ssm_w_proj"], g["conv_w_proj"], g["pool_w_proj"]) = _fused_tn(
        f"dw_proj_l{l}", [(mb["sa"], mb["dya"]), (mb["ac"], mb["dyb"]), (mb["pp"], mb["dyc"])],
        ("cols", "cols", "cols"), s, place)
    (dwgrp,) = _fused_tn(f"dpool_w_group_l{l}", [(sv["p"], mb["dq"])], ("groups",), s, place)
    return g, dwgrp


def _local_step(x, target, weights_of, prm, place, on_grads=None):
    s, d = x.shape
    cw = prm["ssm_b_glu"].shape[1]
    sp = {k: prm[k].reshape(N_LAYERS, 1, -1) for k in ("norm1", "norm2", "b_gate", "ssm_b_glu", "conv_ln_g", "conv_ln_b",
                                                        "pool_scale", "conv_b_dw")}
    sp["pool_w_group"] = prm["pool_w_group"]
    saved = []
    xin = x
    for l in range(N_LAYERS):
        fw = weights_of(l, "in", (xin,))
        sd = _ssm_prepare(l, prm)
        z, h = _in_proj(l, xin, sp["norm1"], fw["w_in"])
        hre, him, y = _ssm_fwd(l, z, sd["bblk_re"], sd["bblk_im"], sd["cblk_re"], sd["cblk_im"], sd["pw"], sd["dskip"])
        p = _pool_fwd(l, z, cw)
        fw.update(weights_of(l, "mixer", (y, p)))
        wdw = fw["conv_w_dw"]
        hc = _conv_fwd(l, z, wdw, sp["conv_b_dw"][l])
        x1 = _merge_fwd(l, xin, y, hc, p, z, fw, sp)
        fw.update(weights_of(l, "ffn", (x1,)))
        x2 = _ffn_fwd(l, x1, sp["norm2"], fw["ffn_w_gate"], fw["ffn_w_up"], fw["ffn_w_down"])
        saved.append(dict(x=xin, z=z, h=h, hre=hre, him=him, y=y, hc=hc, p=p, x1=x1, sd=sd, wdw=wdw, fw=fw))
        xin = x2
    dx, loss, dfinal = _loss_head(xin, target, prm["final_norm"].reshape(1, d))
    big = [None] * N_LAYERS
    small = [None] * N_LAYERS
    norm2_rows = sp["norm2"]
    started = (lambda l, group, grads: on_grads(l, group, grads)) if on_grads is not None else (lambda *a: 0.0)
    for l in reversed(range(N_LAYERS)):
        sv = saved[l]
        sd, fw = sv["sd"], sv["fw"]
        fb = _ffn_bwd(l, sv["x1"], dx, norm2_rows, fw["ffn_w_gate"], fw["ffn_w_up"], fw["ffn_w_down"])
        big[l] = _ffn_weight_grads(l, fb, dx, s, place)
        spl = dict(sp, ssm_b_glu=sp["ssm_b_glu"] + started(l, "ffn", big[l]))
        mb = _merge_bwd(l, fb["dx1"], sv["y"], sv["hc"], sv["p"], sv["z"], fw, spl)
        mixer, dwgrp = _mixer_weight_grads(l, sv, mb, fb["dx1"], s, place)
        big[l].update(mixer)
        wdw = sv["wdw"] + started(l, "mixer", mixer)
        du_c = _pool_bwd(l, mb["dp"])
        dv1, dv2, dwdw, dbdw = _conv_bwd(l, mb["dhc"], sv["z"], wdw)
        sr = _ssm_bwd(l, mb["dy"], sv["z"], sv["hre"], sv["him"], sd["bblk_re"], sd["bblk_im"], sd["cblk_re"],
                      sd["cblk_im"], sd["pw"], sd["dskip"])
        dx, dz, dnorm1 = _in_proj_bwd(l, fb["dx1"], sv["x"], sp["norm1"], fw["w_in"], sr["du"], dv1, dv2, du_c, mb["dzg"])
        w_in_grad = {"w_in": _in_weight_grad(l, sv["h"], dz, place)}
        big[l].update(w_in_grad)
        sg = _ssm_param_grads(l, sd, sr, prm)
        sg.update(norm1=dnorm1.reshape(d), b_gate=mb["db_gate"].reshape(3 * d), ssm_b_glu=mb["db_glu"].reshape(cw),
                  conv_b_dw=dbdw.reshape(cw), conv_ln_g=mb["dln_g"].reshape(cw), conv_ln_b=mb["dln_b"].reshape(cw),
                  pool_w_group=dwgrp, pool_scale=mb["dscale"].reshape(cw), norm2=fb["dnorm2"].reshape(d),
                  conv_w_dw=dwdw)
        small[l] = sg
        if l == N_LAYERS - 1:
            sg = dict(sg, final_norm=dfinal.reshape(d))
        norm2_rows = sp["norm2"] + (started(l, "in", w_in_grad) + started(l, "small", sg))
    return loss[0, 0], dx, big, small, dfinal.reshape(d)


def _place():
    return lax.axis_index("x"), lax.axis_index("y"), lax.axis_index("c")


def _other_chips(x, y):
    return [(1 - x, y), (x, 1 - y), (1 - x, 1 - y)]


def _remote(src, dst, send_sem, recv_sem, device):
    return pltpu.make_async_remote_copy(src_ref=src, dst_ref=dst, send_sem=send_sem, recv_sem=recv_sem,
                                        device_id=device, device_id_type=MESH)


def _hbm(v):
    return pltpu.with_memory_space_constraint(v, pltpu.HBM)


def _cast_into(name, w, place, dtype, after=()):
    nl, k, n = w.shape
    tr = _row_tile(k, n)
    nt = k // tr

    def body(place_ref, w_ref, *rest):
        o0_ref, o1_ref = rest[len(after):]

        @pl.when(pl.program_id(0) == 0)
        def _():
            o0_ref[...] = w_ref[...].astype(dtype)

        @pl.when(pl.program_id(0) == 1)
        def _():
            o1_ref[...] = w_ref[...].astype(dtype)

    return pl.pallas_call(
        body, name=f"cast_{name}",
        grid_spec=pltpu.PrefetchScalarGridSpec(
            num_scalar_prefetch=1, grid=(nl, nt),
            in_specs=[BS((None, tr, n), lambda l, t, pr: (l, t, 0))] + [ANY] * len(after),
            out_specs=[BS((None, tr, n), lambda l, t, pr: (pr[0], t * (1 - l) + (nt - 1) * l, 0)),
                       BS((None, tr, n), lambda l, t, pr: (pr[0], t * l, 0))]),
        out_shape=[SDS((N_CHIPS, k, n), dtype)] * 2)(place, w, *after)


def _gather_rows(buf, c):
    k = buf.shape[1]
    if k % 2:
        return pl.ds(0, k)
    return pl.ds(pl.multiple_of(c * (k // 2), 8), k // 2)


def _allgather_start(tag, groups):
    ng = len(groups)
    sizes = [len(g) for g in groups]
    first = [sum(sizes[:g]) for g in range(ng)]
    flat = [b for g in groups for b in g]
    nb = len(flat)

    def body(*refs):
        ins = refs[:nb]
        sems = refs[nb:nb + 2 * ng]
        token = refs[-1]
        x, y, c = _place()
        jme = 2 * x + y
        for g in range(ng):
            for a in range(sizes[g]):
                buf = ins[first[g] + a]
                blk = buf.at[jme, _gather_rows(buf, c)]
                for k, (cx, cy) in enumerate(_other_chips(x, y)):
                    _remote(blk, blk, sems[2 * g].at[3 * a + k], sems[2 * g + 1].at[3 * a + k], (cx, cy, c)).start()
        token[...] = jnp.zeros(token.shape, F32)

    sem_shapes = [pltpu.SemaphoreType.DMA((3 * sizes[g // 2],)) for g in range(2 * ng)]
    outs = pl.pallas_call(
        body, name=f"allgather_start_{tag}", in_specs=[HBM] * nb,
        out_specs=[SEM] * (2 * ng) + [HBM] * nb + [pl.BlockSpec(memory_space=pltpu.VMEM)],
        out_shape=sem_shapes + [pltpu.HBM(b.shape, b.dtype) for b in flat] + [SDS((8, 128), F32)],
        input_output_aliases={i: 2 * ng + i for i in range(nb)},
        compiler_params=pltpu.CompilerParams(has_side_effects=SIDE_EFFECT))(*[_hbm(b) for b in flat])
    per_group = [(outs[2 * g], outs[2 * g + 1], outs[2 * ng + first[g]:2 * ng + first[g] + sizes[g]])
                 for g in range(ng)]
    return per_group, outs[-1]


def _allgather_wait(l, send_sems, recv_sems, bufs, after):
    n = len(bufs)

    def body(*refs):
        ins = refs[:n]
        ssem, rsem = refs[n], refs[n + 1]
        x, y, c = _place()
        jme = 2 * x + y
        for a in range(n):
            rows = _gather_rows(ins[a], c)
            for k, (cx, cy) in enumerate(_other_chips(x, y)):
                cp = _remote(ins[a].at[jme, rows], ins[a].at[2 * cx + cy, rows], ssem.at[3 * a + k],
                             rsem.at[3 * a + k], (cx, cy, c))
                cp.wait_send()
                cp.wait_recv()

    return pl.pallas_call(
        body, name=f"allgather_wait_{l}", in_specs=[HBM] * n + [SEM, SEM] + [ANY] * len(after), out_specs=[HBM] * n,
        out_shape=[pltpu.HBM(b.shape, b.dtype) for b in bufs], input_output_aliases={i: i for i in range(n)},
        compiler_params=pltpu.CompilerParams(has_side_effects=SIDE_EFFECT))(*bufs, send_sems, recv_sems, *after)


def _allgather_forward(l, bufs):
    n = len(bufs)
    split = [a for a in range(n) if bufs[a].shape[1] % 2 == 0]

    def body(*refs):
        ins = refs[:n]
        send_sems, recv_sems = refs[2 * n:]
        x, y, c = _place()
        sibling = (x, y, 1 - c)
        copies = []
        for a in split:
            for k, (cx, cy) in enumerate(_other_chips(x, y)):
                blk = ins[a].at[2 * cx + cy, _gather_rows(ins[a], c)]
                cp = _remote(blk, blk, send_sems.at[a, k], recv_sems.at[a, k], sibling)
                cp.start()
                copies.append(cp)
        for a in split:
            for k, (cx, cy) in enumerate(_other_chips(x, y)):
                blk = ins[a].at[2 * cx + cy, _gather_rows(ins[a], 1 - c)]
                _remote(blk, blk, send_sems.at[a, k], recv_sems.at[a, k], sibling).wait_recv()
        for cp in copies:
            cp.wait_send()

    sem = pltpu.SemaphoreType.DMA((n, 3))
    return pl.pallas_call(
        body, name=f"allgather_forward_{l}", in_specs=[ANY] * n, out_specs=[ANY] * n,
        out_shape=[SDS(b.shape, b.dtype) for b in bufs], input_output_aliases={i: i for i in range(n)},
        scratch_shapes=[sem, sem])(*bufs)


def _rs_to_owner(l, parts):
    n = len(parts)
    lands = [lax.empty((3,) + p.shape[1:], p.dtype) for p in parts]

    def body(*refs):
        ins, zones = refs[:n], refs[n:2 * n]
        send_sems, recv_sems = refs[2 * n], refs[2 * n + 1]
        token = refs[-1]
        x, y, c = _place()
        for a in range(n):
            for k, (cx, cy) in enumerate(_other_chips(x, y)):
                _remote(ins[a].at[2 * cx + cy], zones[a].at[k], send_sems.at[3 * a + k], recv_sems.at[3 * a + k],
                        (cx, cy, c)).start()
        token[...] = jnp.zeros(token.shape, F32)

    sem = pltpu.SemaphoreType.DMA((3 * n,))
    outs = pl.pallas_call(
        body, name=f"rs_to_owner_start_{l}", in_specs=[HBM] * (2 * n),
        out_specs=[SEM, SEM] + [HBM] * (2 * n) + [pl.BlockSpec(memory_space=pltpu.VMEM)],
        out_shape=[sem, sem] + [pltpu.HBM(p.shape, p.dtype) for p in parts]
        + [pltpu.HBM(z.shape, z.dtype) for z in lands] + [SDS((8, 128), F32)],
        input_output_aliases={i: 2 + i for i in range(2 * n)},
        compiler_params=pltpu.CompilerParams(has_side_effects=SIDE_EFFECT),
    )(*[_hbm(p) for p in parts], *[_hbm(z) for z in lands])
    return outs[0], outs[1], outs[2:2 + n], outs[2 + n:2 + 2 * n], outs[-1]


def _rs_to_owner_wait(l, send_sems, recv_sems, parts, lands, after):
    n = len(parts)

    def body(*refs):
        ins, zones = refs[:n], refs[n:2 * n]
        ssem, rsem = refs[2 * n], refs[2 * n + 1]
        x, y, c = _place()
        for a in range(n):
            for k, (cx, cy) in enumerate(_other_chips(x, y)):
                cp = _remote(ins[a].at[2 * cx + cy], zones[a].at[k], ssem.at[3 * a + k], rsem.at[3 * a + k],
                             (cx, cy, c))
                cp.wait_send()
                cp.wait_recv()

    outs = pl.pallas_call(
        body, name=f"rs_to_owner_wait_{l}", in_specs=[HBM] * (2 * n) + [SEM, SEM] + [ANY] * len(after),
        out_specs=[HBM] * (2 * n),
        out_shape=[pltpu.HBM(p.shape, p.dtype) for p in parts] + [pltpu.HBM(z.shape, z.dtype) for z in lands],
        input_output_aliases={i: i for i in range(2 * n)},
        compiler_params=pltpu.CompilerParams(has_side_effects=SIDE_EFFECT),
    )(*parts, *lands, send_sems, recv_sems, *after)
    return outs[:n], outs[n:]


def _rs_sibling_exchange(l, both):
    n = len(both)

    def body(*refs):
        ins = refs[:n]
        send_sems, recv_sems = refs[2 * n:]
        x, y, c = _place()
        copies = []
        for a in range(n):
            cp = _remote(ins[a].at[c], ins[a].at[c], send_sems.at[a], recv_sems.at[a], (x, y, 1 - c))
            cp.start()
            copies.append(cp)
        for a, cp in enumerate(copies):
            cp.wait_send()
            _remote(ins[a].at[1 - c], ins[a].at[1 - c], send_sems.at[a], recv_sems.at[a], (x, y, 1 - c)).wait_recv()

    sem = pltpu.SemaphoreType.DMA((n,))
    return pl.pallas_call(
        body, name=f"rs_sibling_exchange_{l}", in_specs=[ANY] * n, out_specs=[ANY] * n,
        out_shape=[SDS(b.shape, b.dtype) for b in both], input_output_aliases={i: i for i in range(n)},
        scratch_shapes=[sem, sem])(*both)


def _add_owner(name, grad, recv, place):
    r, cols = grad.shape
    tr = _row_tile(r, cols, budget=1024 * 1024)
    nt = r // tr

    def body(place_ref, g_ref, r_ref, o_ref):
        acc = ((g_ref[...] + r_ref[0].astype(F32)) + r_ref[1].astype(F32)) + r_ref[2].astype(F32)
        o_ref[...] = acc.astype(o_ref.dtype)

    return pl.pallas_call(
        body, name=name,
        grid_spec=pltpu.PrefetchScalarGridSpec(
            num_scalar_prefetch=1, grid=(nt,),
            in_specs=[BS((tr, cols), lambda t, pr: (t, 0)), BS((3, tr, cols), lambda t, pr: (0, t, 0))],
            out_specs=BS((None, tr, cols), lambda t, pr: (pr[1], t, 0))),
        out_shape=SDS((2, r, cols), WIRE_DTYPE))(place, grad, recv)


def _reduce_start(tag, grads):
    names = list(grads)
    send_sems, recv_sems, wires, lands, token = _rs_to_owner(tag, [grads[n][1] for n in names])
    return dict(tag=tag, names=names, send_sems=send_sems, recv_sems=recv_sems, wires=wires, lands=lands,
                grads=[grads[n][0] for n in names]), token


def _reduce_finish(pending, place, after):
    tag, names = pending["tag"], pending["names"]
    _, lands = _rs_to_owner_wait(tag, pending["send_sems"], pending["recv_sems"], pending["wires"],
                                 pending["lands"], after)
    mine = [_add_owner(f"rs_add_owner_{n}_{tag}", g, r, place) for n, g, r in zip(names, pending["grads"], lands)]
    return dict(zip(names, _rs_sibling_exchange(tag, mine)))


def _small_peers(x, y, c):
    return [(x, y, 1 - c)] + [(cx, cy, c) for cx, cy in _other_chips(x, y)]


def _allgather_rows_start(tag, bufs):
    n = len(bufs)
    lands = [lax.empty((8,) + b.shape, b.dtype) for b in bufs]

    def body(*refs):
        ins, zones = refs[:n], refs[n:2 * n]
        send_sems, recv_sems = refs[2 * n], refs[2 * n + 1]
        token = refs[-1]
        x, y, c = _place()
        for a in range(n):
            for i, peer in enumerate(_small_peers(x, y, c)):
                _remote(ins[a], zones[a].at[4 * x + 2 * y + c], send_sems.at[4 * a + i], recv_sems.at[4 * a + i],
                        peer).start()
        token[...] = jnp.zeros(token.shape, F32)

    sem = pltpu.SemaphoreType.DMA((4 * n,))
    outs = pl.pallas_call(
        body, name=f"allgather_small_start_{tag}", in_specs=[HBM] * (2 * n),
        out_specs=[SEM, SEM] + [HBM] * (2 * n) + [pl.BlockSpec(memory_space=pltpu.VMEM)],
        out_shape=[sem, sem] + [pltpu.HBM(b.shape, b.dtype) for b in bufs]
        + [pltpu.HBM(z.shape, z.dtype) for z in lands] + [SDS((8, 128), F32)],
        input_output_aliases={i: 2 + i for i in range(2 * n)},
        compiler_params=pltpu.CompilerParams(has_side_effects=SIDE_EFFECT),
    )(*[_hbm(b) for b in bufs], *[_hbm(z) for z in lands])
    return outs[0], outs[1], outs[2:2 + n], outs[2 + n:2 + 2 * n], outs[-1]


def _allgather_rows_wait(tag, send_sems, recv_sems, bufs, lands, after):
    n = len(bufs)

    def body(*refs):
        ins, zones = refs[:n], refs[n:2 * n]
        ssem, rsem = refs[2 * n], refs[2 * n + 1]
        x, y, c = _place()
        for a in range(n):
            for i, (px, py, pc) in enumerate(_small_peers(x, y, c)):
                cp = _remote(ins[a], zones[a].at[4 * px + 2 * py + pc], ssem.at[4 * a + i], rsem.at[4 * a + i],
                             (px, py, pc))
                cp.wait_send()
                cp.wait_recv()

    outs = pl.pallas_call(
        body, name=f"allgather_small_wait_{tag}", in_specs=[HBM] * (2 * n) + [SEM, SEM, ANY],
        out_specs=[HBM] * (2 * n),
        out_shape=[pltpu.HBM(b.shape, b.dtype) for b in bufs] + [pltpu.HBM(z.shape, z.dtype) for z in lands],
        input_output_aliases={i: i for i in range(2 * n)},
        compiler_params=pltpu.CompilerParams(has_side_effects=SIDE_EFFECT),
    )(*bufs, *lands, send_sems, recv_sems, after)
    return outs[:n], outs[n:]


def _allgather_rows_forward(tag, lands):
    n = len(lands)

    def body(*refs):
        ins = refs[:n]
        send_sems, recv_sems = refs[2 * n:]
        x, y, c = _place()
        sibling = (x, y, 1 - c)
        copies = []
        for a in range(n):
            for k, (cx, cy) in enumerate(_other_chips(x, y)):
                blk = ins[a].at[4 * cx + 2 * cy + c]
                cp = _remote(blk, blk, send_sems.at[a, k], recv_sems.at[a, k], sibling)
                cp.start()
                copies.append(cp)
        for a in range(n):
            for k, (cx, cy) in enumerate(_other_chips(x, y)):
                blk = ins[a].at[4 * cx + 2 * cy + 1 - c]
                _remote(blk, blk, send_sems.at[a, k], recv_sems.at[a, k], sibling).wait_recv()
        for cp in copies:
            cp.wait_send()

    sem = pltpu.SemaphoreType.DMA((n, 3))
    return pl.pallas_call(body, name=f"allgather_small_forward_{tag}", in_specs=[ANY] * n, out_specs=[ANY] * n,
                          out_shape=[SDS(z.shape, z.dtype) for z in lands],
                          input_output_aliases={i: i for i in range(n)}, scratch_shapes=[sem, sem])(*lands)


def _sum_devices(tag, gathered, mine, place):
    _, r, cols = gathered.shape
    tr = _row_tile(r, cols, budget=256 * 1024)

    def body(place_ref, g_ref, x_ref, o_ref):
        me = 2 * place_ref[0] + place_ref[1]
        acc = jnp.where(me == 0, x_ref[...], g_ref[0])
        for k in range(1, 8):
            acc = acc + jnp.where(me == k, x_ref[...], g_ref[k])
        o_ref[...] = acc

    return pl.pallas_call(
        body, name=f"sum_small_grads_{tag}",
        grid_spec=pltpu.PrefetchScalarGridSpec(
            num_scalar_prefetch=1, grid=(r // tr,),
            in_specs=[BS((8, tr, cols), lambda t, pr: (0, t, 0)), BS((tr, cols), lambda t, pr: (t, 0))],
            out_specs=BS((tr, cols), lambda t, pr: (t, 0))),
        out_shape=SDS((r, cols), F32))(place, gathered, mine)


def _adamw_values(w, g, m, v):
    m = ADAM_B1 * m + (1.0 - ADAM_B1) * g
    v = ADAM_B2 * v + (1.0 - ADAM_B2) * (g * g)
    m_hat = m / (1.0 - ADAM_B1 ** ADAM_STEP)
    v_hat = v / (1.0 - ADAM_B2 ** ADAM_STEP)
    delta = -ADAM_LR * (m_hat / (jnp.sqrt(v_hat) + ADAM_EPS) + ADAM_WD * w)
    return delta, m, v


def _adamw_big(name, l, w, m, v, g, earlier=None, after=()):
    nl, r, cols = w.shape
    tr = _row_tile(r, cols, budget=1024 * 1024)
    nt = r // tr
    n_prev = 0 if earlier is None else 4

    def body(*refs):
        w_ref, m_ref, v_ref, g_ref = refs[:4]
        go_ref, d_ref, mo_ref, vo_ref = refs[4 + n_prev + len(after):]
        gv = g_ref[0].astype(F32) + g_ref[1].astype(F32)
        delta, m_new, v_new = _adamw_values(w_ref[...], gv, m_ref[...], v_ref[...])
        go_ref[...] = gv
        d_ref[...] = delta
        mo_ref[...] = m_new
        vo_ref[...] = v_new

    layer = BS((None, tr, cols), lambda t: (l, t, 0))
    return pl.pallas_call(
        body, name=f"adamw_{name}_l{l}", grid=(nt,),
        in_specs=[layer, layer, layer, BS((2, tr, cols), lambda t: (0, t, 0))] + [ANY] * (n_prev + len(after)),
        out_specs=[layer] * 4, out_shape=[SDS(w.shape, F32)] * 4,
        input_output_aliases={4 + i: i for i in range(n_prev)}, compiler_params=_params(),
    )(w, m, v, g, *(earlier or ()), *after)


def _adamw_mid(name, w, m, v, gathered, mine, place):
    shape = w.shape[1:]
    zeros = (0,) * len(shape)

    def body(place_ref, w_ref, m_ref, v_ref, *refs):
        gath, own = refs[:N_LAYERS], refs[N_LAYERS:2 * N_LAYERS]
        go_ref, d_ref, mo_ref, vo_ref = refs[2 * N_LAYERS:]
        me = 2 * place_ref[0] + place_ref[1]
        sums = []
        for l in range(N_LAYERS):
            acc = jnp.where(me == 0, own[l][...], gath[l][0])
            for k in range(1, 8):
                acc = acc + jnp.where(me == k, own[l][...], gath[l][k])
            sums.append(acc)
        gv = sums[0]
        for l in range(1, N_LAYERS):
            gv = jnp.where(pl.program_id(0) == l, sums[l], gv)
        delta, m_new, v_new = _adamw_values(w_ref[...], gv, m_ref[...], v_ref[...])
        go_ref[...] = gv
        d_ref[...] = delta
        mo_ref[...] = m_new
        vo_ref[...] = v_new

    layer = BS((None,) + shape, lambda l, pr: (l,) + zeros)
    return pl.pallas_call(
        body, name=f"adamw_{name}",
        grid_spec=pltpu.PrefetchScalarGridSpec(
            num_scalar_prefetch=1, grid=(N_LAYERS,),
            in_specs=[layer] * 3 + [BS((8,) + shape, lambda l, pr: (0,) + zeros)] * N_LAYERS
            + [BS(shape, lambda l, pr: zeros)] * N_LAYERS,
            out_specs=[layer] * 4),
        out_shape=[SDS(w.shape, F32)] * 4, compiler_params=_params())(place, w, m, v, *gathered, *mine)


def _adamw_rows(w, m, v, g):
    r, cols = w.shape
    tr = _row_tile(r, cols, budget=512 * 1024)

    def body(w_ref, m_ref, v_ref, g_ref, d_ref, mo_ref, vo_ref):
        delta, m_new, v_new = _adamw_values(w_ref[...], g_ref[...], m_ref[...], v_ref[...])
        d_ref[...] = delta
        mo_ref[...] = m_new
        vo_ref[...] = v_new

    spec = BS((tr, cols), lambda t: (t, 0))
    return pl.pallas_call(body, name="adamw_small", grid=(r // tr,), in_specs=[spec] * 4, out_specs=[spec] * 3,
                          out_shape=[SDS(w.shape, F32)] * 3)(w, m, v, g)


PACK_ALIGN = 8 * 128
PACK_ROWS = 128


def _pack_rows(arrays):
    parts, rows = [], 0
    for a in arrays:
        flat = a.reshape(-1)
        pad = (-flat.shape[0]) % PACK_ALIGN
        if pad:
            flat = jnp.pad(flat, (0, pad))
        parts.append(flat.reshape(-1, 128))
        rows += parts[-1].shape[0]
    if rows % PACK_ROWS:
        parts.append(jnp.zeros((PACK_ROWS - rows % PACK_ROWS, 128), parts[0].dtype))
    return jnp.concatenate(parts, axis=0)


def _unpack_rows(buf, shapes):
    out, row = [], 0
    for shape in shapes:
        size = math.prod(shape)
        rows = -(-size // PACK_ALIGN) * (PACK_ALIGN // 128)
        out.append(buf[row:row + rows].reshape(-1)[:size].reshape(shape))
        row += rows
    return out


def kernel(x, norm1, w_in, b_gate, ssm_a_re, ssm_a_im, ssm_log_dt, ssm_b_re, ssm_b_im, ssm_c_re, ssm_c_im, ssm_d, ssm_w_glu, ssm_b_glu, ssm_w_proj, conv_w_dw, conv_b_dw, conv_ln_g, conv_ln_b, conv_w_proj, pool_w_group, pool_scale, pool_w_proj, w_out, norm2, ffn_w_gate, ffn_w_up, ffn_w_down, final_norm, loss_target, m_norm1, m_w_in, m_b_gate, m_ssm_a_re, m_ssm_a_im, m_ssm_log_dt, m_ssm_b_re, m_ssm_b_im, m_ssm_c_re, m_ssm_c_im, m_ssm_d, m_ssm_w_glu, m_ssm_b_glu, m_ssm_w_proj, m_conv_w_dw, m_conv_b_dw, m_conv_ln_g, m_conv_ln_b, m_conv_w_proj, m_pool_w_group, m_pool_scale, m_pool_w_proj, m_w_out, m_norm2, m_ffn_w_gate, m_ffn_w_up, m_ffn_w_down, m_final_norm, v_norm1, v_w_in, v_b_gate, v_ssm_a_re, v_ssm_a_im, v_ssm_log_dt, v_ssm_b_re, v_ssm_b_im, v_ssm_c_re, v_ssm_c_im, v_ssm_d, v_ssm_w_glu, v_ssm_b_glu, v_ssm_w_proj, v_conv_w_dw, v_conv_b_dw, v_conv_ln_g, v_conv_ln_b, v_conv_w_proj, v_pool_w_group, v_pool_scale, v_pool_w_proj, v_w_out, v_norm2, v_ffn_w_gate, v_ffn_w_up, v_ffn_w_down, v_final_norm):
    given = dict(locals())
    cx, cy, cc = _place()
    place = jnp.stack([2 * cx + cy, cc]).astype(jnp.int32)

    def kernel_view(n, a):
        if n in TRANSPOSED:
            return a.transpose(0, 2, 1)
        return a.transpose(0, 1, 3, 2) if n in ("ssm_b_re", "ssm_b_im") else a

    prm = {n: given[n] for n in WEIGHTS}
    mom = {n: given["m_" + n] for n in WEIGHTS}
    var = {n: given["v_" + n] for n in WEIGHTS}
    for n in MID:
        prm[n], mom[n], var[n] = kernel_view(n, prm[n]), kernel_view(n, mom[n]), kernel_view(n, var[n])

    dw_shard = prm["conv_w_dw"].reshape(N_LAYERS, CONV_KERNEL, -1)
    casts = {"w_in": _cast_into("w_in", prm["w_in"], place, MXU_DTYPE)}
    first, first_started = _allgather_start("first", [[casts["w_in"][0]]])
    in_flight = {(0, "in"): first[0]}
    casts.update({n: _cast_into(n, kernel_view(n, prm[n]), place, MXU_DTYPE, after=(first_started,))
                  for n in BIG if n != "w_in"})
    casts["conv_w_dw"] = _cast_into("conv_w_dw", dw_shard, place, F32, after=(first_started,))
    order = [(l, g) for l in range(N_LAYERS) for g in GATHER_GROUPS if (l, g) != (0, "in")]
    rest, rest_started = _allgather_start("rest", [[casts[n][l] for n in GATHER_GROUPS[g]] for l, g in order])
    in_flight.update(zip(order, rest))

    def weights_of(l, group, after):
        send_sems, recv_sems, bufs = in_flight[l, group]
        tag = f"l{l}_{group}"
        if (l, group) == (0, "in"):
            after = after + (rest_started,)
        bufs = _allgather_forward(tag, _allgather_wait(tag, send_sems, recv_sems, bufs, after))
        fw = dict(zip(GATHER_GROUPS[group], bufs))
        if "conv_w_dw" in fw:
            fw["conv_w_dw"] = fw["conv_w_dw"].transpose(1, 0, 2).reshape(CONV_KERNEL, -1)
        return fw

    pending, small_pending, small_shapes = {}, {}, {}
    tokens = {}

    def on_grads(l, group, grads):
        if group == "small":
            packed = {n: g for n, g in grads.items() if n not in MID}
            small_shapes[l] = {n: g.shape for n, g in packed.items()}
            begun = _allgather_rows_start(f"l{l}", [_pack_rows(list(packed.values()))] + [grads[n] for n in MID])
            small_pending[l], token = begun[:4], begun[4]
        else:
            pending[l, group], token = _reduce_start(f"{l}_{group}", grads)
        tokens[l, group] = token
        return token[0, 0]

    loss, dx, _, _, _ = _local_step(x[0], loss_target[0], weights_of, prm, place, on_grads)
    loss = lax.psum(loss, ("x", "y", "c"))

    reduced = [{} for _ in range(N_LAYERS)]
    out = {}

    def finish(l, group, after):
        reduced[l].update(_reduce_finish(pending[l, group], place, after))

    def adamw(l, names, done):
        for n in names:
            out[n] = _adamw_big(n, l, kernel_view(n, prm[n]), kernel_view(n, mom[n]), kernel_view(n, var[n]),
                                reduced[l][n], out.get(n), after=done)
            done = (out[n][0],)
        return done

    top = N_LAYERS - 1
    done = (tokens[0, "in"], tokens[0, "small"])
    for group in ("ffn", "mixer", "in"):
        finish(top, group, done)
    done = adamw(top, BIG, done)
    for group in ("ffn", "mixer", "in"):
        finish(0, group, done)
        done = adamw(0, [n for n in GATHER_GROUPS[group] if n in BIG], done)
    for n in BIG:
        out[n] = tuple(kernel_view(n, a) for a in out[n])

    gsmall = {}
    mid_mine, mid_gathered = [], []
    for l in range(N_LAYERS):
        mine, lands = _allgather_rows_wait(f"l{l}", *small_pending[l], done[0])
        lands = _allgather_rows_forward(f"l{l}", lands)
        mid_mine.append(mine[1:])
        mid_gathered.append(lands[1:])
        gsum = _sum_devices(f"l{l}", lands[0], mine[0], place)
        for n, g in zip(small_shapes[l], _unpack_rows(gsum, list(small_shapes[l].values()))):
            gsmall.setdefault(n, [None] * N_LAYERS)[l] = g
    for i, n in enumerate(MID):
        out[n] = tuple(kernel_view(n, a) for a in _adamw_mid(
            n, prm[n], mom[n], var[n], [mid_gathered[l][i] for l in range(N_LAYERS)],
            [mid_mine[l][i] for l in range(N_LAYERS)], place))
    gsmall = {n: (g[top] if n == "final_norm" else jnp.stack(g)) for n, g in gsmall.items()}
    lanes = dw_shard.shape[-1]
    gsmall["conv_w_dw"] = lax.dynamic_slice_in_dim(gsmall["conv_w_dw"], (2 * cx + cy) * lanes, lanes, axis=2)
    small_names = [n for n in SMALL if n not in MID] + ["conv_w_dw"]
    w_rows = _pack_rows([prm[n] for n in small_names])
    m_rows = _pack_rows([mom[n] for n in small_names])
    v_rows = _pack_rows([var[n] for n in small_names])
    g_rows = _pack_rows([gsmall[n] for n in small_names])
    shapes = [prm[n].shape for n in small_names]
    d_s, m_s, v_s = (_unpack_rows(r, shapes) for r in _adamw_rows(w_rows, m_rows, v_rows, g_rows))
    for i, n in enumerate(small_names):
        out[n] = (gsmall[n].reshape(prm[n].shape), d_s[i], m_s[i], v_s[i])
    grads = [out[n][0] for n in WEIGHTS]
    deltas = [out[n][1] for n in WEIGHTS]
    new_m = [out[n][2] for n in WEIGHTS]
    new_v = [out[n][3] for n in WEIGHTS]
    return (loss, dx[None], *grads, *deltas, *new_m, *new_v)
```

```python
import functools
import math

import jax
import jax.numpy as jnp
from jax import lax
from jax.experimental import pallas as pl
from jax.experimental.pallas import tpu as pltpu

F32 = jnp.float32
MXU_DTYPE = jnp.bfloat16
WIRE_DTYPE = jnp.bfloat16
SDS = jax.ShapeDtypeStruct
BS = pl.BlockSpec
ANY = pl.BlockSpec(memory_space=pl.ANY)
HBM = pl.BlockSpec(memory_space=pltpu.HBM)
SEM = pl.BlockSpec(memory_space=pltpu.SEMAPHORE)
SIDE_EFFECT = pltpu.SideEffectType.DATAFLOW_SIDE_EFFECTING
MESH = pl.DeviceIdType.MESH

EPS = 1e-6
N_CHIPS = 4
N_LAYERS = 2
SSM_GROUPS, SSM_STATE, SSM_GROUP = 32, 64, 16
CONV_KERNEL = 31
CONV_PAD = 32
POOL_WINDOWS = (2, 4, 8, 16)
GELU_C = math.sqrt(2.0 / math.pi)
ADAM_LR, ADAM_B1, ADAM_B2, ADAM_EPS, ADAM_WD, ADAM_STEP = 0.001, 0.9, 0.999, 1e-08, 0.01, 10
VMEM_LIMIT = 56 * 1024 * 1024

BIG = ("w_in", "ssm_w_glu", "ssm_w_proj", "conv_w_proj", "pool_w_proj", "w_out", "ffn_w_gate", "ffn_w_up", "ffn_w_down")
TRANSPOSED = ("ffn_w_gate", "ffn_w_up")
MID = ("ssm_b_re", "ssm_b_im", "ssm_c_re", "ssm_c_im")
GATHER_GROUPS = {
    "in": ("w_in",),
    "mixer": ("ssm_w_glu", "ssm_w_proj", "conv_w_proj", "pool_w_proj", "w_out", "conv_w_dw"),
    "ffn": ("ffn_w_gate", "ffn_w_up", "ffn_w_down"),
}
SMALL = ("norm1", "b_gate", "ssm_a_re", "ssm_a_im", "ssm_log_dt", "ssm_b_re", "ssm_b_im", "ssm_c_re", "ssm_c_im",
         "ssm_d", "ssm_b_glu", "conv_b_dw", "conv_ln_g", "conv_ln_b", "pool_w_group", "pool_scale", "norm2",
         "final_norm")
WEIGHTS = ("norm1", "w_in", "b_gate", "ssm_a_re", "ssm_a_im", "ssm_log_dt", "ssm_b_re", "ssm_b_im", "ssm_c_re",
           "ssm_c_im", "ssm_d", "ssm_w_glu", "ssm_b_glu", "ssm_w_proj", "conv_w_dw", "conv_b_dw", "conv_ln_g",
           "conv_ln_b", "conv_w_proj", "pool_w_group", "pool_scale", "pool_w_proj", "w_out", "norm2", "ffn_w_gate",
           "ffn_w_up", "ffn_w_down", "final_norm")


def _params(vmem=True):
    return pltpu.CompilerParams(vmem_limit_bytes=VMEM_LIMIT) if vmem else None


def _mm(a, b):
    return jnp.dot(a.astype(MXU_DTYPE), b.astype(MXU_DTYPE), preferred_element_type=F32)


def _mm_nt(a, b):
    return lax.dot_general(a.astype(MXU_DTYPE), b.astype(MXU_DTYPE), (((1,), (1,)), ((), ())),
                           preferred_element_type=F32)


def _mm_tn(a, b):
    return lax.dot_general(a.astype(MXU_DTYPE), b.astype(MXU_DTYPE), (((0,), (0,)), ((), ())),
                           preferred_element_type=F32)


def _sigmoid(x):
    return jax.nn.sigmoid(x)


def _gelu(x):
    t = jnp.tanh(GELU_C * (x + 0.044715 * (x * x * x)))
    return x * (0.5 * (1.0 + t)), t


def _gelu_grad(x, t):
    return 0.5 * (1.0 + t) + 0.5 * x * (1.0 - t * t) * (GELU_C * (1.0 + 3.0 * 0.044715 * x * x))


def _colsum(v):
    return jnp.sum(v, axis=0, keepdims=True)


def _row_tile(rows, cols, itemsize=4, budget=1536 * 1024):
    best = None
    for t in range(8, rows + 1, 8):
        if rows % t == 0 and t * cols * itemsize <= budget:
            best = t
    return best if best is not None else rows


def _in_proj(l, x, norm1, w_in):
    s, d = x.shape
    nc = w_in.shape[-1]
    tm = min(1024, s)
    nt = s // tm

    def body(x_ref, g_ref, w_ref, z_ref, h_ref, h_all):
        i = pl.program_id(1)
        rows = pl.ds(pl.multiple_of(i * tm, tm), tm)

        @pl.when(pl.program_id(0) == 0)
        def _():
            xv = x_ref[...]
            r = lax.rsqrt(jnp.mean(xv * xv, axis=-1, keepdims=True) + EPS)
            hv = (xv * r * g_ref[...]).astype(h_ref.dtype)
            h_ref[...] = hv.T
            h_all[rows, :] = hv

        z_ref[...] = _mm(h_all[rows, :], w_ref[...])

    tile_of = lambda j, i: i * (1 - jnp.minimum(j, 1)) + (nt - 1) * jnp.minimum(j, 1)
    return pl.pallas_call(
        body, name=f"in_proj_l{l}", grid=(N_CHIPS, nt),
        in_specs=[BS((tm, d), lambda j, i: (tile_of(j, i), 0)), BS((None, 1, d), lambda j, i: (l, 0, 0)),
                  BS((None, d, nc), lambda j, i: (j, 0, 0))],
        out_specs=[BS((tm, nc), lambda j, i: (i, j)), BS((d, tm), lambda j, i: (0, tile_of(j, i)))],
        out_shape=[SDS((s, N_CHIPS * nc), F32), SDS((d, s), MXU_DTYPE)],
        scratch_shapes=[pltpu.VMEM((s, d), MXU_DTYPE)], compiler_params=_params())(x, norm1, w_in)


def _mm_cols(a, w_ref):
    return jnp.concatenate([_mm(a, w_ref[j]) for j in range(N_CHIPS)], axis=1)


def _mm_nt_cols(dv, w_ref):
    nc = w_ref.shape[-1]
    acc = _mm_nt(dv[:, 0:nc], w_ref[0])
    for j in range(1, N_CHIPS):
        acc = acc + _mm_nt(dv[:, j * nc:(j + 1) * nc], w_ref[j])
    return acc


def _merge_values(y, hc, p, zg, wglu, bglu, wpa, wpb, wpc, lng, lnb, wgrp, scale, bg):
    v = {}
    ge, th = _gelu(y)
    t = _mm(ge, wglu) + bglu
    sg = _sigmoid(t)
    sa = ge * sg
    ya = _mm_cols(sa, wpa)
    mu = jnp.mean(hc, axis=-1, keepdims=True)
    xc = hc - mu
    r = lax.rsqrt(jnp.mean(xc * xc, axis=-1, keepdims=True) + EPS)
    xh = xc * r
    ln = xh * lng + lnb
    sl = _sigmoid(ln)
    ac = ln * sl
    yb = _mm_cols(ac, wpb)
    gw = p.shape[1] // len(POOL_WINDOWS)
    q = jnp.concatenate([_mm(p[:, k * gw:(k + 1) * gw], wgrp[k]) for k in range(len(POOL_WINDOWS))], axis=1)
    pp = q * scale
    yc = _mm_cols(pp, wpc)
    d = ya.shape[1]
    gates = [_sigmoid(zg[k] + bg[:, k * d:(k + 1) * d]) for k in range(3)]
    merged = gates[0] * ya + gates[1] * yb + gates[2] * yc
    v.update(ge=ge, th=th, sg=sg, sa=sa, ya=ya, r=r, xh=xh, ln=ln, sl=sl, ac=ac, yb=yb, q=q, pp=pp, yc=yc,
             gates=gates, merged=merged)
    return v


def _merge_specs(l, tm, d, cw):
    row = lambda n: BS((None, 1, n), lambda i: (l, 0, 0))
    resident = lambda shp: BS(shp, lambda i: (0, 0, 0), pipeline_mode=pl.Buffered(1))
    return [
        BS((tm, cw), lambda i: (i, 0)),
        BS((tm, cw), lambda i: (i, 0)),
        BS((tm, cw), lambda i: (i, 0)),
        BS((tm, d), lambda i: (i, 2)), BS((tm, d), lambda i: (i, 3)), BS((tm, d), lambda i: (i, 4)),
        resident((N_CHIPS, cw // N_CHIPS, cw)),
        row(cw),
        resident((N_CHIPS, cw, d // N_CHIPS)),
        resident((N_CHIPS, cw, d // N_CHIPS)),
        resident((N_CHIPS, cw, d // N_CHIPS)),
        row(cw), row(cw),
        BS((None, 4, cw // 4, cw // 4), lambda i: (l, 0, 0, 0)),
        row(cw),
        row(3 * d),
        resident((N_CHIPS, d // N_CHIPS, d)),
    ]


def _merge_fwd(l, x, y, hc, p, z, fw, sp):
    s, d = x.shape
    cw = y.shape[1]
    tm = min(512, s)

    def body(x_ref, y_ref, hc_ref, p_ref, z0, z1, z2, wglu, bglu, wpa, wpb, wpc, lng, lnb, wgrp, scale, bg, wout,
             x1_ref):
        v = _merge_values(y_ref[...], hc_ref[...], p_ref[...], (z0[...], z1[...], z2[...]),
                          wglu[...].reshape(cw, cw), bglu[...], wpa, wpb, wpc, lng[...], lnb[...], wgrp, scale[...],
                          bg[...])
        x1_ref[...] = x_ref[...] + _mm(v["merged"], wout[...].reshape(d, d))

    return pl.pallas_call(
        body, name=f"merge_fwd_l{l}", grid=(s // tm,),
        in_specs=[BS((tm, d), lambda i: (i, 0))] + _merge_specs(l, tm, d, cw),
        out_specs=BS((tm, d), lambda i: (i, 0)), out_shape=SDS((s, d), F32), compiler_params=_params(),
    )(x, y, hc, p, z, z, z, fw["ssm_w_glu"], sp["ssm_b_glu"], fw["ssm_w_proj"], fw["conv_w_proj"], fw["pool_w_proj"],
      sp["conv_ln_g"], sp["conv_ln_b"], sp["pool_w_group"], sp["pool_scale"], sp["b_gate"], fw["w_out"])


def _merge_bwd(l, dx1, y, hc, p, z, fw, sp):
    s, d = dx1.shape
    cw = y.shape[1]
    tm = min(256, s)
    m = MXU_DTYPE

    def body(dx1_ref, y_ref, hc_ref, p_ref, z0, z1, z2, wglu, bglu, wpa, wpb, wpc, lng, lnb, wgrp, scale, bg, wout,
             dzg_ref, dy_ref, dhc_ref, dp_ref, merged_ref, sa_ref, ac_ref, pp_ref, ge_ref, dt_ref, dya_ref, dyb_ref,
             dyc_ref, dq_ref, dbg_ref, dbglu_ref, dlng_ref, dlnb_ref, dscale_ref):
        yv = y_ref[...]
        wg = wglu[...].reshape(cw, cw)
        v = _merge_values(yv, hc_ref[...], p_ref[...], (z0[...], z1[...], z2[...]), wg, bglu[...], wpa, wpb, wpc,
                          lng[...], lnb[...], wgrp, scale[...], bg[...])
        dm = _mm_nt(dx1_ref[...], wout[...].reshape(d, d))
        ys = (v["ya"], v["yb"], v["yc"])
        dys, dbg = [], []
        for k in range(3):
            gk = v["gates"][k]
            dzk = dm * ys[k] * (gk * (1.0 - gk))
            dbg.append(_colsum(dzk))
            dzg_ref[:, k * d:(k + 1) * d] = dzk.astype(m)
            dys.append((dm * gk).astype(m))
        dsa = _mm_nt_cols(dys[0], wpa)
        dac = _mm_nt_cols(dys[1], wpb)
        dpp = _mm_nt_cols(dys[2], wpc)
        ge, sg = v["ge"], v["sg"]
        dt = dsa * ge * (sg * (1.0 - sg))
        dge = dsa * sg + _mm_nt(dt, wg)
        dy_ref[...] = dge * _gelu_grad(yv, v["th"])
        ln, sl, xh = v["ln"], v["sl"], v["xh"]
        dln = dac * (sl * (1.0 + ln * (1.0 - sl)))
        dxh = dln * lng[...]
        dhc_ref[...] = v["r"] * (dxh - jnp.mean(dxh, axis=-1, keepdims=True)
                                 - xh * jnp.mean(dxh * xh, axis=-1, keepdims=True))
        dq = dpp * scale[...]
        gw = cw // len(POOL_WINDOWS)
        for k in range(len(POOL_WINDOWS)):
            dp_ref[:, k * gw:(k + 1) * gw] = _mm_nt(dq[:, k * gw:(k + 1) * gw], wgrp[k])
        merged_ref[...] = v["merged"].astype(m)
        sa_ref[...] = v["sa"].astype(m)
        ac_ref[...] = v["ac"].astype(m)
        pp_ref[...] = v["pp"].astype(m)
        ge_ref[...] = ge.astype(m)
        dt_ref[...] = dt.astype(m)
        dya_ref[...] = dys[0]
        dyb_ref[...] = dys[1]
        dyc_ref[...] = dys[2]
        dq_ref[...] = dq.astype(m)

        @pl.when(pl.program_id(0) == 0)
        def _():
            for ref in (dbg_ref, dbglu_ref, dlng_ref, dlnb_ref, dscale_ref):
                ref[...] = jnp.zeros(ref.shape, F32)

        dbg_ref[...] += jnp.concatenate(dbg, axis=1)
        dbglu_ref[...] += _colsum(dt)
        dlng_ref[...] += _colsum(dln * xh)
        dlnb_ref[...] += _colsum(dln)
        dscale_ref[...] += _colsum(dpp * v["q"])

    tile = lambda n: BS((tm, n), lambda i: (i, 0))
    acc = lambda n: BS((1, n), lambda i: (0, 0))
    outs = pl.pallas_call(
        body, name=f"merge_bwd_l{l}", grid=(s // tm,),
        in_specs=[tile(d)] + _merge_specs(l, tm, d, cw),
        out_specs=[tile(3 * d), tile(cw), tile(cw), tile(cw), tile(d), tile(cw), tile(cw), tile(cw), tile(cw), tile(cw),
                   tile(d), tile(d), tile(d), tile(cw), acc(3 * d), acc(cw), acc(cw), acc(cw), acc(cw)],
        out_shape=[SDS((s, 3 * d), m), SDS((s, cw), F32), SDS((s, cw), F32), SDS((s, cw), F32), SDS((s, d), m),
                   SDS((s, cw), m), SDS((s, cw), m), SDS((s, cw), m), SDS((s, cw), m), SDS((s, cw), m), SDS((s, d), m),
                   SDS((s, d), m), SDS((s, d), m), SDS((s, cw), m), SDS((1, 3 * d), F32), SDS((1, cw), F32),
                   SDS((1, cw), F32), SDS((1, cw), F32), SDS((1, cw), F32)],
        compiler_params=_params(),
    )(dx1, y, hc, p, z, z, z, fw["ssm_w_glu"], sp["ssm_b_glu"], fw["ssm_w_proj"], fw["conv_w_proj"], fw["pool_w_proj"],
      sp["conv_ln_g"], sp["conv_ln_b"], sp["pool_w_group"], sp["pool_scale"], sp["b_gate"], fw["w_out"])
    names = ("dzg", "dy", "dhc", "dp", "merged", "sa", "ac", "pp", "ge", "dt", "dya", "dyb", "dyc", "dq", "db_gate",
             "db_glu", "dln_g", "dln_b", "dscale")
    return dict(zip(names, outs))


def _ffn_fwd(l, x1, norm2, wg, wu, wd):
    s, d = x1.shape
    hc = wd.shape[1]
    tm = min(1024, s)

    def body(x_ref, g_ref, wg_ref, wu_ref, wd_ref, o_ref, h_scr):
        @pl.when(pl.program_id(1) == 0)
        def _():
            xv = x_ref[...]
            r = lax.rsqrt(jnp.mean(xv * xv, axis=-1, keepdims=True) + EPS)
            h_scr[...] = (xv * r * g_ref[...]).astype(h_scr.dtype)
            o_ref[...] = xv

        h = h_scr[...]
        gate = _mm_nt(h, wg_ref[...])
        up = _mm_nt(h, wu_ref[...])
        o_ref[...] += _mm(gate * _sigmoid(gate) * up, wd_ref[...])

    return pl.pallas_call(
        body, name=f"ffn_fwd_l{l}", grid=(s // tm, N_CHIPS),
        in_specs=[BS((tm, d), lambda i, j: (i, 0)), BS((None, 1, d), lambda i, j: (l, 0, 0)),
                  BS((None, hc, d), lambda i, j: (j, 0, 0)), BS((None, hc, d), lambda i, j: (j, 0, 0)),
                  BS((None, hc, d), lambda i, j: (j, 0, 0))],
        out_specs=BS((tm, d), lambda i, j: (i, 0)), out_shape=SDS((s, d), F32),
        scratch_shapes=[pltpu.VMEM((tm, d), MXU_DTYPE)], compiler_params=_params())(x1, norm2, wg, wu, wd)


def _ffn_bwd(l, x1, dx2, norm2, wg, wu, wd):
    s, d = x1.shape
    hc = wd.shape[1]
    tm = min(512, s)
    m = MXU_DTYPE
    last = N_CHIPS - 1

    def body(x_ref, dx2_ref, g_ref, wg_ref, wu_ref, wd_ref, dx1_ref, h_ref, dxb_ref, act_ref, dgate_ref, dup_ref,
             dn_ref, dh_scr):
        i, j = pl.program_id(0), pl.program_id(1)

        @pl.when(j == 0)
        def _():
            xv = x_ref[...]
            r = lax.rsqrt(jnp.mean(xv * xv, axis=-1, keepdims=True) + EPS)
            h_ref[...] = (xv * r * g_ref[...]).astype(m)
            dxb_ref[...] = dx2_ref[...].astype(m)
            dh_scr[...] = jnp.zeros(dh_scr.shape, F32)

        @pl.when((i == 0) & (j == 0))
        def _():
            dn_ref[...] = jnp.zeros(dn_ref.shape, F32)

        h = h_ref[...]
        gate = _mm_nt(h, wg_ref[...])
        up = _mm_nt(h, wu_ref[...])
        sg = _sigmoid(gate)
        silu = gate * sg
        act_ref[...] = (silu * up).astype(m).T
        dact = _mm_nt(dxb_ref[...], wd_ref[...])
        dup = (dact * silu).astype(m)
        dgate = (dact * up * (sg * (1.0 + gate * (1.0 - sg)))).astype(m)
        dup_ref[...] = dup.T
        dgate_ref[...] = dgate.T
        dh_scr[...] += _mm(dgate, wg_ref[...]) + _mm(dup, wu_ref[...])

        @pl.when(j == last)
        def _():
            xv = x_ref[...]
            r = lax.rsqrt(jnp.mean(xv * xv, axis=-1, keepdims=True) + EPS)
            xh = xv * r
            dh = dh_scr[...]
            dn_ref[...] += _colsum(dh * xh)
            dxh = dh * g_ref[...]
            dx1_ref[...] = dx2_ref[...] + r * (dxh - xh * jnp.mean(dxh * xh, axis=-1, keepdims=True))

    chunk = BS((None, hc, tm), lambda i, j: (j, 0, i))
    outs = pl.pallas_call(
        body, name=f"ffn_bwd_l{l}", grid=(s // tm, N_CHIPS),
        in_specs=[BS((tm, d), lambda i, j: (i, 0)), BS((tm, d), lambda i, j: (i, 0)),
                  BS((None, 1, d), lambda i, j: (l, 0, 0)),
                  BS((None, hc, d), lambda i, j: (j, 0, 0)), BS((None, hc, d), lambda i, j: (j, 0, 0)),
                  BS((None, hc, d), lambda i, j: (j, 0, 0))],
        out_specs=[BS((tm, d), lambda i, j: (i, 0)), BS((tm, d), lambda i, j: (i, 0)), BS((tm, d), lambda i, j: (i, 0)),
                   chunk, chunk, chunk, BS((1, d), lambda i, j: (0, 0))],
        out_shape=[SDS((s, d), F32), SDS((s, d), m), SDS((s, d), m), SDS((N_CHIPS, hc, s), m),
                   SDS((N_CHIPS, hc, s), m), SDS((N_CHIPS, hc, s), m), SDS((1, d), F32)],
        scratch_shapes=[pltpu.VMEM((tm, d), F32)], compiler_params=_params(),
    )(x1, dx2, norm2, wg, wu, wd)
    return dict(zip(("dx1", "h2", "dx2", "act", "dgate", "dup", "dnorm2"), outs))


def _loss_head(x, target, gf):
    s, d = x.shape
    tm = min(512, s)

    def body(x_ref, t_ref, g_ref, dx_ref, loss_ref, dg_ref):
        @pl.when(pl.program_id(0) == 0)
        def _():
            loss_ref[...] = jnp.zeros(loss_ref.shape, F32)
            dg_ref[...] = jnp.zeros(dg_ref.shape, F32)

        xv = x_ref[...]
        r = lax.rsqrt(jnp.mean(xv * xv, axis=-1, keepdims=True) + EPS)
        xh = xv * r
        err = xh * g_ref[...] - t_ref[...]
        loss_ref[...] += 0.5 * jnp.sum(jnp.mean(err * err, axis=-1, keepdims=True), axis=0, keepdims=True)
        dyv = err * (1.0 / d)
        dg_ref[...] += _colsum(dyv * xh)
        dxh = dyv * g_ref[...]
        dx_ref[...] = r * (dxh - xh * jnp.mean(dxh * xh, axis=-1, keepdims=True))

    return pl.pallas_call(
        body, name="loss_head", grid=(s // tm,),
        in_specs=[BS((tm, d), lambda i: (i, 0)), BS((tm, d), lambda i: (i, 0)), BS((1, d), lambda i: (0, 0))],
        out_specs=[BS((tm, d), lambda i: (i, 0)), BS((1, 1), lambda i: (0, 0)), BS((1, d), lambda i: (0, 0))],
        out_shape=[SDS((s, d), F32), SDS((1, 1), F32), SDS((1, d), F32)], compiler_params=_params())(x, target, gf)


def _in_proj_bwd(l, dres, x, norm1, w_in, du_a, dv1, dv2, du_c, dzg):
    s, d = x.shape
    nc = w_in.shape[-1]
    tm = min(512, s)
    m = MXU_DTYPE

    def body(dres_ref, x_ref, g_ref, w_ref, a_ref, b1_ref, b2_ref, c_ref, g3_ref, dx_ref, dz_ref, dn_ref):
        @pl.when(pl.program_id(0) == 0)
        def _():
            dn_ref[...] = jnp.zeros(dn_ref.shape, F32)

        dz = jnp.concatenate([a_ref[...], b1_ref[...], b2_ref[...], c_ref[...], g3_ref[...]], axis=1).astype(m)
        dz_ref[...] = dz
        dh = _mm_nt_cols(dz, w_ref)
        xv = x_ref[...]
        r = lax.rsqrt(jnp.mean(xv * xv, axis=-1, keepdims=True) + EPS)
        xh = xv * r
        dn_ref[...] += _colsum(dh * xh)
        dxh = dh * g_ref[...]
        dx_ref[...] = dres_ref[...] + r * (dxh - xh * jnp.mean(dxh * xh, axis=-1, keepdims=True))

    tile = lambda n: BS((tm, n), lambda i: (i, 0))
    return pl.pallas_call(
        body, name=f"in_proj_bwd_l{l}", grid=(s // tm,),
        in_specs=[tile(d), tile(d), BS((None, 1, d), lambda i: (l, 0, 0)),
                  BS((N_CHIPS, d, nc), lambda i: (0, 0, 0), pipeline_mode=pl.Buffered(1)),
                  tile(du_a.shape[1]), tile(dv1.shape[1]), tile(dv2.shape[1]), tile(du_c.shape[1]), tile(dzg.shape[1])],
        out_specs=[tile(d), tile(N_CHIPS * nc), BS((1, d), lambda i: (0, 0))],
        out_shape=[SDS((s, d), F32), SDS((s, N_CHIPS * nc), m), SDS((1, d), F32)], compiler_params=_params(),
    )(dres, x, norm1, w_in, du_a, dv1, dv2, du_c, dzg)


def _tn_matmul(name, a, a_spec, b, b_spec, chunk_shape, grid, place):
    last = grid[1] - 1

    def body(place_ref, a_ref, b_ref, own_ref, wire_ref, *acc):
        part = _mm(a_ref[...], b_ref[...])

        def emit(total):
            wire_ref[...] = total.astype(WIRE_DTYPE)

            @pl.when(pl.program_id(0) == place_ref[0])
            def _():
                own_ref[...] = total

        if last == 0:
            emit(part)
        else:
            @pl.when(pl.program_id(1) == 0)
            def _():
                acc[0][...] = part

            @pl.when(pl.program_id(1) > 0)
            def _():
                acc[0][...] += part

            @pl.when(pl.program_id(1) == last)
            def _():
                emit(acc[0][...])

    zeros = (0,) * len(chunk_shape)
    return pl.pallas_call(
        body, name=name,
        grid_spec=pltpu.PrefetchScalarGridSpec(
            num_scalar_prefetch=1, grid=grid, in_specs=[a_spec, b_spec],
            out_specs=[BS(chunk_shape, lambda j, t, pr: zeros), BS((None,) + chunk_shape, lambda j, t, pr: (j,) + zeros)],
            scratch_shapes=[pltpu.VMEM(chunk_shape, F32)] if last else []),
        out_shape=[SDS(chunk_shape, F32), SDS((N_CHIPS,) + chunk_shape, WIRE_DTYPE)],
        compiler_params=_params())(place, a, b)


def _scan_consts(pw_ref, lanes, reverse):
    sgn = -1.0 if reverse else 1.0
    row = lax.broadcasted_iota(jnp.int32, (8, lanes), 0)
    steps = []
    for i, k in enumerate((1, 2, 4)):
        mask = (row < 8 - k) if reverse else (row >= k)
        steps.append((k, jnp.where(mask, pw_ref[2 * i], 0.0), jnp.where(mask, sgn * pw_ref[2 * i + 1], 0.0)))
    c = 4 if reverse else 3
    return steps, pw_ref[2 * c], sgn * pw_ref[2 * c + 1]


def _scan_block(br, bi, steps, row, reverse):
    for k, ar, ai in steps:
        sh = 8 - k if reverse else k
        sr = pltpu.roll(br, sh, 0)
        si = pltpu.roll(bi, sh, 0)
        br, bi = br + ar * sr - ai * si, bi + ar * si + ai * sr
    return br, bi


def _ssm_fwd(l, z, bblk_re, bblk_im, cblk_re, cblk_im, pw, dskip):
    s = z.shape[0]
    gc = bblk_re.shape[1]
    gl = bblk_re.shape[2]
    nblk = bblk_re.shape[0]

    def body(u_ref, bre, bim, cre, cim, pw_ref, d_ref, hre, him, y_ref):
        u = u_ref[...]
        hre[...] = _mm(u, bre[...])
        him[...] = _mm(u, bim[...])
        row = lax.broadcasted_iota(jnp.int32, (8, gl), 0)
        steps, car, cai = _scan_consts(pw_ref, gl, False)

        def step(i, carry):
            cr, ci = carry
            r0 = pl.multiple_of(i * 8, 8)
            br, bi = _scan_block(hre[pl.ds(r0, 8), :], him[pl.ds(r0, 8), :], steps, row, False)
            hr = br + car * cr - cai * ci
            hi = bi + car * ci + cai * cr
            hre[pl.ds(r0, 8), :] = hr
            him[pl.ds(r0, 8), :] = hi
            return jnp.broadcast_to(hr[7:8, :], (8, gl)), jnp.broadcast_to(hi[7:8, :], (8, gl))

        zero = jnp.zeros((8, gl), F32)
        lax.fori_loop(0, s // 8, step, (zero, zero))
        y_ref[...] = _mm_nt(hre[...], cre[...]) - _mm_nt(him[...], cim[...]) + d_ref[...] * u

    return pl.pallas_call(
        body, name=f"ssm_fwd_l{l}", grid=(nblk,),
        in_specs=[BS((s, gc), lambda k: (0, k)), BS((None, gc, gl), lambda k: (k, 0, 0)),
                  BS((None, gc, gl), lambda k: (k, 0, 0)), BS((None, gc, gl), lambda k: (k, 0, 0)),
                  BS((None, gc, gl), lambda k: (k, 0, 0)), BS((10, 8, gl), lambda k: (0, 0, k)),
                  BS((1, gc), lambda k: (0, k))],
        out_specs=[BS((s, gl), lambda k: (0, k)), BS((s, gl), lambda k: (0, k)), BS((s, gc), lambda k: (0, k))],
        out_shape=[SDS((s, nblk * gl), F32), SDS((s, nblk * gl), F32), SDS((s, nblk * gc), F32)],
        compiler_params=_params())(z, bblk_re, bblk_im, cblk_re, cblk_im, pw, dskip)


def _ssm_bwd(l, dy, z, hre, him, bblk_re, bblk_im, cblk_re, cblk_im, pw, dskip):
    s = z.shape[0]
    nblk, gc, gl = bblk_re.shape

    def body(dy_ref, u_ref, hre_ref, him_ref, bre, bim, cre, cim, pw_ref, d_ref,
             du_ref, dbre_ref, dbim_ref, dcre_ref, dcim_ref, dar_ref, dai_ref, dd_ref, gre, gim):
        dyv = dy_ref[...]
        u = u_ref[...]
        gre[...] = _mm(dyv, cre[...])
        gim[...] = -_mm(dyv, cim[...])
        dcre_ref[...] = _mm_tn(dyv, hre_ref[...])
        dcim_ref[...] = -_mm_tn(dyv, him_ref[...])
        dd_ref[...] = _colsum(dyv * u)
        row = lax.broadcasted_iota(jnp.int32, (8, gl), 0)
        steps, car, cai = _scan_consts(pw_ref, gl, True)
        n8 = s // 8

        def step(ii, carry):
            cr, ci, accr, acci = carry
            i = n8 - 1 - ii
            r0 = pl.multiple_of(i * 8, 8)
            br, bi = _scan_block(gre[pl.ds(r0, 8), :], gim[pl.ds(r0, 8), :], steps, row, True)
            dr = br + car * cr - cai * ci
            di = bi + car * ci + cai * cr
            gre[pl.ds(r0, 8), :] = dr
            gim[pl.ds(r0, 8), :] = di
            rp = pl.multiple_of(jnp.maximum(i - 1, 0) * 8, 8)
            keep = jnp.where(i > 0, 1.0, 0.0)
            pr = jnp.where(row >= 1, pltpu.roll(hre_ref[pl.ds(r0, 8), :], 1, 0),
                           keep * pltpu.roll(hre_ref[pl.ds(rp, 8), :], 1, 0))
            pi = jnp.where(row >= 1, pltpu.roll(him_ref[pl.ds(r0, 8), :], 1, 0),
                           keep * pltpu.roll(him_ref[pl.ds(rp, 8), :], 1, 0))
            accr = accr + dr * pr + di * pi
            acci = acci + di * pr - dr * pi
            return (jnp.broadcast_to(dr[0:1, :], (8, gl)), jnp.broadcast_to(di[0:1, :], (8, gl)), accr, acci)

        zero = jnp.zeros((8, gl), F32)
        _, _, accr, acci = lax.fori_loop(0, n8, step, (zero, zero, zero, zero))
        dar_ref[...] = _colsum(accr)
        dai_ref[...] = _colsum(acci)
        dbr = gre[...]
        dbi = gim[...]
        du_ref[...] = (dyv * d_ref[...] + _mm_nt(dbr, bre[...]) + _mm_nt(dbi, bim[...])).astype(du_ref.dtype)
        dbre_ref[...] = _mm_tn(u, dbr)
        dbim_ref[...] = _mm_tn(u, dbi)

    col = lambda n: BS((s, n), lambda k: (0, k))
    blk = lambda a, b: BS((None, a, b), lambda k: (k, 0, 0))
    outs = pl.pallas_call(
        body, name=f"ssm_bwd_l{l}", grid=(nblk,),
        in_specs=[col(gc), col(gc), col(gl), col(gl), blk(gc, gl), blk(gc, gl), blk(gc, gl), blk(gc, gl),
                  BS((10, 8, gl), lambda k: (0, 0, k)), BS((1, gc), lambda k: (0, k))],
        out_specs=[col(gc), blk(gc, gl), blk(gc, gl), blk(gc, gl), blk(gc, gl), BS((1, gl), lambda k: (0, k)),
                   BS((1, gl), lambda k: (0, k)), BS((1, gc), lambda k: (0, k))],
        out_shape=[SDS((s, nblk * gc), MXU_DTYPE), SDS((nblk, gc, gl), F32), SDS((nblk, gc, gl), F32),
                   SDS((nblk, gc, gl), F32), SDS((nblk, gc, gl), F32), SDS((1, nblk * gl), F32),
                   SDS((1, nblk * gl), F32), SDS((1, nblk * gc), F32)],
        scratch_shapes=[pltpu.VMEM((s, gl), F32), pltpu.VMEM((s, gl), F32)], compiler_params=_params(),
    )(dy, z, hre, him, bblk_re, bblk_im, cblk_re, cblk_im, pw, dskip)
    return dict(zip(("du", "dbblk_re", "dbblk_im", "dcblk_re", "dcblk_im", "dabar_re", "dabar_im", "dd"), outs))


def _conv_fwd(l, z, wdw, bdw):
    s = z.shape[0]
    cw = wdw.shape[1]
    lb = 128
    tr = min(256, s)
    off1 = cw // lb
    off2 = 2 * cw // lb

    def body(v1_ref, v2_ref, w_ref, b_ref, hc_ref, scr):
        scr[0:CONV_PAD, :] = jnp.zeros((CONV_PAD, lb), F32)
        scr[CONV_PAD:, :] = v1_ref[...] * _sigmoid(v2_ref[...])
        for t in range(s // tr):
            acc = jnp.broadcast_to(b_ref[...], (tr, lb))
            for k in range(CONV_KERNEL):
                acc = acc + w_ref[pl.ds(k, 1), :] * scr[pl.ds(t * tr + CONV_PAD - (CONV_KERNEL - 1) + k, tr), :]
            hc_ref[pl.ds(t * tr, tr), :] = acc

    return pl.pallas_call(
        body, name=f"conv_fwd_l{l}", grid=(cw // lb,),
        in_specs=[BS((s, lb), lambda k: (0, off1 + k)), BS((s, lb), lambda k: (0, off2 + k)),
                  BS((CONV_KERNEL, lb), lambda k: (0, k)), BS((1, lb), lambda k: (0, k))],
        out_specs=BS((s, lb), lambda k: (0, k)), out_shape=SDS((s, cw), F32),
        scratch_shapes=[pltpu.VMEM((s + CONV_PAD, lb), F32)], compiler_params=_params())(z, z, wdw, bdw)


def _conv_bwd(l, dhc, z, wdw):
    s = z.shape[0]
    cw = wdw.shape[1]
    lb = 128
    tr = min(256, s)
    off1 = cw // lb
    off2 = 2 * cw // lb
    nb = cw // lb

    def body(d_ref, v1_ref, v2_ref, w_ref, dv1_ref, dv2_ref, dw_ref, db_ref, hpad, dpad):
        v1 = v1_ref[...]
        sg = _sigmoid(v2_ref[...])
        dv = d_ref[...]
        hpad[0:CONV_PAD, :] = jnp.zeros((CONV_PAD, lb), F32)
        hpad[CONV_PAD:, :] = v1 * sg
        dpad[0:s, :] = dv
        dpad[s:, :] = jnp.zeros((CONV_PAD, lb), F32)
        db_ref[...] = _colsum(dv)
        dws = [jnp.zeros((1, lb), F32) for _ in range(CONV_KERNEL)]
        for t in range(s // tr):
            dt = d_ref[pl.ds(t * tr, tr), :]
            acc = jnp.zeros((tr, lb), F32)
            for k in range(CONV_KERNEL):
                acc = acc + w_ref[pl.ds(k, 1), :] * dpad[pl.ds(t * tr + (CONV_KERNEL - 1) - k, tr), :]
                dws[k] = dws[k] + _colsum(dt * hpad[pl.ds(t * tr + CONV_PAD - (CONV_KERNEL - 1) + k, tr), :])
            sgt = _sigmoid(v2_ref[pl.ds(t * tr, tr), :])
            v1t = v1_ref[pl.ds(t * tr, tr), :]
            dv1_ref[pl.ds(t * tr, tr), :] = (acc * sgt).astype(dv1_ref.dtype)
            dv2_ref[pl.ds(t * tr, tr), :] = (acc * v1t * (sgt * (1.0 - sgt))).astype(dv2_ref.dtype)
        for k in range(CONV_KERNEL):
            dw_ref[pl.ds(k, 1), :] = dws[k]

    return pl.pallas_call(
        body, name=f"conv_bwd_l{l}", grid=(nb,),
        in_specs=[BS((s, lb), lambda k: (0, k)), BS((s, lb), lambda k: (0, off1 + k)),
                  BS((s, lb), lambda k: (0, off2 + k)), BS((CONV_KERNEL, lb), lambda k: (0, k))],
        out_specs=[BS((s, lb), lambda k: (0, k)), BS((s, lb), lambda k: (0, k)),
                   BS((CONV_KERNEL, lb), lambda k: (0, k)), BS((1, lb), lambda k: (0, k))],
        out_shape=[SDS((s, cw), MXU_DTYPE), SDS((s, cw), MXU_DTYPE), SDS((CONV_KERNEL, cw), F32), SDS((1, cw), F32)],
        scratch_shapes=[pltpu.VMEM((s + CONV_PAD, lb), F32), pltpu.VMEM((s + CONV_PAD, lb), F32)],
        compiler_params=_params())(dhc, z, z, wdw)


def _pool_window(k):
    return jnp.where(k == 0, float(POOL_WINDOWS[0]),
                     jnp.where(k == 1, float(POOL_WINDOWS[1]),
                               jnp.where(k == 2, float(POOL_WINDOWS[2]), float(POOL_WINDOWS[3]))))


def _pool_fwd(l, z, pw_width):
    s = z.shape[0]
    lb = pw_width // len(POOL_WINDOWS)
    off = 3 * pw_width // lb

    def body(u_ref, p_ref):
        k = pl.program_id(0)
        u = u_ref[...]
        row = lax.broadcasted_iota(jnp.int32, (s, lb), 0)
        sums = [u]
        for sh in (1, 2, 4, 8):
            prev = sums[-1]
            sums.append(prev + jnp.where(row >= sh, pltpu.roll(prev, sh, 0), 0.0))
        sel = jnp.where(k == 0, sums[1], jnp.where(k == 1, sums[2], jnp.where(k == 2, sums[3], sums[4])))
        cnt = jnp.minimum((row + 1).astype(F32), _pool_window(k))
        p_ref[...] = sel / cnt - u

    return pl.pallas_call(
        body, name=f"pool_fwd_l{l}", grid=(len(POOL_WINDOWS),),
        in_specs=[BS((s, lb), lambda k: (0, off + k))], out_specs=BS((s, lb), lambda k: (0, k)),
        out_shape=SDS((s, pw_width), F32), compiler_params=_params())(z)


def _pool_bwd(l, dp):
    s, width = dp.shape
    lb = width // len(POOL_WINDOWS)

    def body(d_ref, du_ref):
        k = pl.program_id(0)
        dv = d_ref[...]
        row = lax.broadcasted_iota(jnp.int32, (s, lb), 0)
        cnt = jnp.minimum((row + 1).astype(F32), _pool_window(k))
        sums = [dv / cnt]
        for sh in (1, 2, 4, 8):
            prev = sums[-1]
            sums.append(prev + jnp.where(row < s - sh, pltpu.roll(prev, s - sh, 0), 0.0))
        sel = jnp.where(k == 0, sums[1], jnp.where(k == 1, sums[2], jnp.where(k == 2, sums[3], sums[4])))
        du_ref[...] = (sel - dv).astype(du_ref.dtype)

    return pl.pallas_call(
        body, name=f"pool_bwd_l{l}", grid=(len(POOL_WINDOWS),),
        in_specs=[BS((s, lb), lambda k: (0, k))], out_specs=BS((s, lb), lambda k: (0, k)),
        out_shape=SDS((s, width), MXU_DTYPE), compiler_params=_params())(dp)


def _zoh(a_re, a_im, log_dt):
    dt = jnp.exp(log_dt)
    mag = jnp.exp(dt * a_re)
    ang = dt * a_im
    abar_re = mag * jnp.cos(ang)
    abar_im = mag * jnp.sin(ang)
    den = a_re * a_re + a_im * a_im
    nr = abar_re - 1.0
    ni = abar_im
    f_re = (nr * a_re + ni * a_im) / den
    f_im = (ni * a_re - nr * a_im) / den
    return abar_re, abar_im, f_re, f_im


def _zoh_fwd(l, a_re, a_im, log_dt):
    def body(ar, ai, ld, o0, o1, o2, o3):
        for ref, val in zip((o0, o1, o2, o3), _zoh(ar[...], ai[...], ld[...])):
            ref[...] = val

    return pl.pallas_call(body, name=f"zoh_fwd_l{l}", out_shape=[SDS(a_re.shape, F32)] * 4)(a_re, a_im, log_dt)


def _zoh_bwd(l, a_re, a_im, log_dt, cts):
    def body(ar, ai, ld, c0, c1, c2, c3, dar, dai, dld):
        _, vjp = jax.vjp(_zoh, ar[...], ai[...], ld[...])
        g = vjp((c0[...], c1[...], c2[...], c3[...]))
        dar[...] = g[0]
        dai[...] = g[1]
        dld[...] = g[2]

    return pl.pallas_call(body, name=f"zoh_bwd_l{l}",
                          out_shape=[SDS(a_re.shape, F32), SDS(a_re.shape, F32), SDS(log_dt.shape, F32)],
                          )(a_re, a_im, log_dt, *cts)


def _bbar_fwd(l, f_re, f_im, b_re, b_im):
    g, p, n = b_re.shape[1:]

    def body(fr, fi, br, bi, o_re, o_im):
        o_re[...] = (fr[...] * br[...] - fi[...] * bi[...]).astype(o_re.dtype)
        o_im[...] = (fr[...] * bi[...] + fi[...] * br[...]).astype(o_im.dtype)

    whole = lambda shp: BS(shp, lambda i: (0,) * len(shp))
    layer = BS((None, g, p, n), lambda i: (l, 0, 0, 0))
    return pl.pallas_call(body, name=f"bbar_fwd_l{l}", grid=(1,),
                          in_specs=[whole((g, 1, n)), whole((g, 1, n)), layer, layer],
                          out_specs=[whole((g, p, n))] * 2,
                          out_shape=[SDS((g, p, n), MXU_DTYPE)] * 2)(f_re, f_im, b_re, b_im)


def _bbar_bwd(l, f_re, f_im, b_re, b_im, d_re, d_im):
    g, p, n = b_re.shape[1:]

    def body(fr, fi, br, bi, dr, di, dfr, dfi, dbr, dbi):
        dfr[...] = jnp.sum(dr[...] * br[...] + di[...] * bi[...], axis=1, keepdims=True)
        dfi[...] = jnp.sum(di[...] * br[...] - dr[...] * bi[...], axis=1, keepdims=True)
        dbr[...] = fr[...] * dr[...] + fi[...] * di[...]
        dbi[...] = fr[...] * di[...] - fi[...] * dr[...]

    whole = lambda shp: BS(shp, lambda i: (0,) * len(shp))
    layer = BS((None, g, p, n), lambda i: (l, 0, 0, 0))
    return pl.pallas_call(body, name=f"bbar_bwd_l{l}", grid=(1,),
                          in_specs=[whole((g, 1, n)), whole((g, 1, n)), layer, layer, whole((g, p, n)),
                                    whole((g, p, n))],
                          out_specs=[whole((g, 1, n)), whole((g, 1, n)), whole((g, p, n)), whole((g, p, n))],
                          out_shape=[SDS((g, 1, n), F32), SDS((g, 1, n), F32), SDS((g, p, n), F32),
                                     SDS((g, p, n), F32)])(f_re, f_im, b_re, b_im, d_re, d_im)


def _powers(l, abar_re, abar_im):
    lanes = abar_re.shape[1]

    def body(ar_ref, ai_ref, o_ref):
        ar, ai = ar_ref[...], ai_ref[...]
        pows = [(ar, ai)]
        for _ in range(7):
            pr, pi = pows[-1]
            pows.append((pr * ar - pi * ai, pr * ai + pi * ar))
        row = lax.broadcasted_iota(jnp.int32, (8, lanes), 0)
        for i, k in enumerate((1, 2, 4)):
            o_ref[2 * i] = jnp.broadcast_to(pows[k - 1][0], (8, lanes))
            o_ref[2 * i + 1] = jnp.broadcast_to(pows[k - 1][1], (8, lanes))
        for slot, order in ((3, range(8)), (4, range(7, -1, -1))):
            vr = jnp.zeros((8, lanes), F32)
            vi = jnp.zeros((8, lanes), F32)
            for r, e in enumerate(order):
                vr = jnp.where(row == r, pows[e][0], vr)
                vi = jnp.where(row == r, pows[e][1], vi)
            o_ref[2 * slot] = vr
            o_ref[2 * slot + 1] = vi

    return pl.pallas_call(body, name=f"powers_l{l}", out_shape=SDS((10, 8, lanes), F32))(abar_re, abar_im)


def _block_diag(v):
    g, a, b = v.shape
    eye = jnp.eye(8, dtype=v.dtype)
    out = jnp.einsum("kgab,gh->kgahb", v.reshape(g // 8, 8, a, b), eye)
    return out.reshape(g // 8, 8 * a, 8 * b)


def _block_diag_extract(blk, a, b):
    n = blk.shape[0]
    v = blk.reshape(n, 8, a, 8, b)
    return jnp.einsum("kgahb,gh->kgab", v, jnp.eye(8, dtype=blk.dtype)).reshape(n * 8, a, b)


def _ssm_prepare(l, prm):
    g, n, p = SSM_GROUPS, SSM_STATE, SSM_GROUP
    a_re, a_im = prm["ssm_a_re"][l], prm["ssm_a_im"][l]
    log_dt = prm["ssm_log_dt"][l].reshape(g, 1)
    abar_re, abar_im, f_re, f_im = _zoh_fwd(l, a_re, a_im, log_dt)
    f_re, f_im = f_re.reshape(g, 1, n), f_im.reshape(g, 1, n)
    bbar_re, bbar_im = _bbar_fwd(l, f_re, f_im, prm["ssm_b_re"], prm["ssm_b_im"])
    pw = _powers(l, abar_re.reshape(1, g * n), abar_im.reshape(1, g * n))
    return dict(a_re=a_re, a_im=a_im, log_dt=log_dt, f_re=f_re, f_im=f_im,
                bblk_re=_block_diag(bbar_re), bblk_im=_block_diag(bbar_im),
                cblk_re=_block_diag(prm["ssm_c_re"][l].astype(MXU_DTYPE)),
                cblk_im=_block_diag(prm["ssm_c_im"][l].astype(MXU_DTYPE)), pw=pw,
                dskip=prm["ssm_d"][l].reshape(1, g * p))


def _ssm_param_grads(l, sd, r, prm):
    g, n, p = SSM_GROUPS, SSM_STATE, SSM_GROUP
    dbbar_re = _block_diag_extract(r["dbblk_re"], p, n)
    dbbar_im = _block_diag_extract(r["dbblk_im"], p, n)
    dfr, dfi, db_re, db_im = _bbar_bwd(l, sd["f_re"], sd["f_im"], prm["ssm_b_re"], prm["ssm_b_im"], dbbar_re, dbbar_im)
    cts = (r["dabar_re"].reshape(g, n), r["dabar_im"].reshape(g, n), dfr.reshape(g, n), dfi.reshape(g, n))
    da_re, da_im, dlog_dt = _zoh_bwd(l, sd["a_re"], sd["a_im"], sd["log_dt"], cts)
    return dict(ssm_a_re=da_re, ssm_a_im=da_im, ssm_log_dt=dlog_dt.reshape(g), ssm_b_re=db_re, ssm_b_im=db_im,
                ssm_c_re=_block_diag_extract(r["dcblk_re"], p, n), ssm_c_im=_block_diag_extract(r["dcblk_im"], p, n),
                ssm_d=r["dd"].reshape(g, p))


def _ffn_weight_grads(l, fb, dx2, s, place):
    d = dx2.shape[1]
    hcn = fb["act"].shape[1]
    g = {}
    for name, key, rhs in (("ffn_w_gate", "dgate", fb["h2"]), ("ffn_w_up", "dup", fb["h2"]),
                           ("ffn_w_down", "act", fb["dx2"])):
        g[name] = _tn_matmul(f"d{name}_l{l}", fb[key], BS((None, hcn, s), lambda j, t, pr: (j, 0, 0)), rhs,
                             BS((s, d), lambda j, t, pr: (0, 0)), (hcn, d), (N_CHIPS, 1), place)
    return g


def _in_weight_grad(l, ht, dz, place):
    d, s = ht.shape
    ncw = dz.shape[1] // N_CHIPS
    return _tn_matmul(f"dw_in_l{l}", ht, BS((d, s), lambda j, t, pr: (0, 0)), dz, BS((s, ncw), lambda j, t, pr: (0, j)),
                      (d, ncw), (N_CHIPS, 1), place)


def _fused_tn(name, pairs, kinds, s, place):
    n = len(pairs)

    def shape_of(a, b, kind):
        k, m = a.shape[1], b.shape[1]
        if kind == "rows":
            return (N_CHIPS, k // N_CHIPS, m)
        if kind == "cols":
            return (N_CHIPS, k, m // N_CHIPS)
        return (k // 128, 128, 128)

    shapes = [shape_of(a, b, kind) for (a, b), kind in zip(pairs, kinds)]
    out_shape = []
    for shp, kind in zip(shapes, kinds):
        out_shape += [SDS(shp, F32)] if kind == "groups" else [SDS(shp[1:], F32), SDS(shp, WIRE_DTYPE)]

    def body(place_ref, *refs):
        ins, outs, accs = refs[:2 * n], refs[2 * n:2 * n + len(out_shape)], refs[2 * n + len(out_shape):]
        o = 0
        for i, kind in enumerate(kinds):
            a, b = ins[2 * i][...], ins[2 * i + 1][...]
            if kind == "groups":
                for k in range(shapes[i][0]):
                    outs[o][k] = _mm_tn(a[:, k * 128:(k + 1) * 128], b[:, k * 128:(k + 1) * 128])
                o += 1
                continue
            acc = accs[i]
            if kind == "rows":
                acc[...] = _mm_tn(a, b).reshape(acc.shape)
            else:
                full = _mm_tn(a, b)
                nc = acc.shape[2]
                for j in range(N_CHIPS):
                    acc[j] = full[:, j * nc:(j + 1) * nc]
            outs[o][...] = acc[place_ref[0]]
            outs[o + 1][...] = acc[...].astype(WIRE_DTYPE)
            o += 2

    whole = lambda shp: BS(shp, lambda t, pr: (0,) * len(shp))
    outs = pl.pallas_call(
        body, name=name,
        grid_spec=pltpu.PrefetchScalarGridSpec(
            num_scalar_prefetch=1, grid=(1,),
            in_specs=[whole(v.shape) for pair in pairs for v in pair],
            out_specs=[whole(o.shape) for o in out_shape],
            scratch_shapes=[pltpu.VMEM(shp, F32) for shp in shapes]),
        out_shape=out_shape, compiler_params=_params(),
    )(place, *[v for pair in pairs for v in pair])
    res, o = [], 0
    for kind in kinds:
        if kind == "groups":
            res.append(outs[o])
            o += 1
        else:
            res.append((outs[o], outs[o + 1]))
            o += 2
    return res


def _mixer_weight_grads(l, sv, mb, dx1, s, place):
    g = {}
    (g["w_out"], g["ssm_w_glu"]) = _fused_tn(f"dw_out_glu_l{l}", [(mb["merged"], dx1), (mb["ge"], mb["dt"])],
                                            ("rows", "rows"), s, place)
    (g["ssm_w_proj"], g["conv_w_proj"], g["pool_w_proj"]) = _fused_tn(
        f"dw_proj_l{l}", [(mb["sa"], mb["dya"]), (mb["ac"], mb["dyb"]), (mb["pp"], mb["dyc"])],
        ("cols", "cols", "cols"), s, place)
    (dwgrp,) = _fused_tn(f"dpool_w_group_l{l}", [(sv["p"], mb["dq"])], ("groups",), s, place)
    return g, dwgrp


def _local_step(x, target, weights_of, prm, place, on_grads=None):
    s, d = x.shape
    cw = prm["ssm_b_glu"].shape[1]
    sp = {k: prm[k].reshape(N_LAYERS, 1, -1) for k in ("norm1", "norm2", "b_gate", "ssm_b_glu", "conv_ln_g", "conv_ln_b",
                                                        "pool_scale", "conv_b_dw")}
    sp["pool_w_group"] = prm["pool_w_group"]
    saved = []
    xin = x
    for l in range(N_LAYERS):
        fw = weights_of(l, "in", (xin,))
        sd = _ssm_prepare(l, prm)
        z, h = _in_proj(l, xin, sp["norm1"], fw["w_in"])
        hre, him, y = _ssm_fwd(l, z, sd["bblk_re"], sd["bblk_im"], sd["cblk_re"], sd["cblk_im"], sd["pw"], sd["dskip"])
        p = _pool_fwd(l, z, cw)
        fw.update(weights_of(l, "mixer", (y, p)))
        wdw = fw["conv_w_dw"]
        hc = _conv_fwd(l, z, wdw, sp["conv_b_dw"][l])
        x1 = _merge_fwd(l, xin, y, hc, p, z, fw, sp)
        fw.update(weights_of(l, "ffn", (x1,)))
        x2 = _ffn_fwd(l, x1, sp["norm2"], fw["ffn_w_gate"], fw["ffn_w_up"], fw["ffn_w_down"])
        saved.append(dict(x=xin, z=z, h=h, hre=hre, him=him, y=y, hc=hc, p=p, x1=x1, sd=sd, wdw=wdw, fw=fw))
        xin = x2
    dx, loss, dfinal = _loss_head(xin, target, prm["final_norm"].reshape(1, d))
    big = [None] * N_LAYERS
    small = [None] * N_LAYERS
    norm2_rows = sp["norm2"]
    started = (lambda l, group, grads: on_grads(l, group, grads)) if on_grads is not None else (lambda *a: 0.0)
    for l in reversed(range(N_LAYERS)):
        sv = saved[l]
        sd, fw = sv["sd"], sv["fw"]
        fb = _ffn_bwd(l, sv["x1"], dx, norm2_rows, fw["ffn_w_gate"], fw["ffn_w_up"], fw["ffn_w_down"])
        big[l] = _ffn_weight_grads(l, fb, dx, s, place)
        spl = dict(sp, ssm_b_glu=sp["ssm_b_glu"] + started(l, "ffn", big[l]))
        mb = _merge_bwd(l, fb["dx1"], sv["y"], sv["hc"], sv["p"], sv["z"], fw, spl)
        mixer, dwgrp = _mixer_weight_grads(l, sv, mb, fb["dx1"], s, place)
        big[l].update(mixer)
        wdw = sv["wdw"] + started(l, "mixer", mixer)
        du_c = _pool_bwd(l, mb["dp"])
        dv1, dv2, dwdw, dbdw = _conv_bwd(l, mb["dhc"], sv["z"], wdw)
        sr = _ssm_bwd(l, mb["dy"], sv["z"], sv["hre"], sv["him"], sd["bblk_re"], sd["bblk_im"], sd["cblk_re"],
                      sd["cblk_im"], sd["pw"], sd["dskip"])
        dx, dz, dnorm1 = _in_proj_bwd(l, fb["dx1"], sv["x"], sp["norm1"], fw["w_in"], sr["du"], dv1, dv2, du_c, mb["dzg"])
        w_in_grad = {"w_in": _in_weight_grad(l, sv["h"], dz, place)}
        big[l].update(w_in_grad)
        sg = _ssm_param_grads(l, sd, sr, prm)
        sg.update(norm1=dnorm1.reshape(d), b_gate=mb["db_gate"].reshape(3 * d), ssm_b_glu=mb["db_glu"].reshape(cw),
                  conv_b_dw=dbdw.reshape(cw), conv_ln_g=mb["dln_g"].reshape(cw), conv_ln_b=mb["dln_b"].reshape(cw),
                  pool_w_group=dwgrp, pool_scale=mb["dscale"].reshape(cw), norm2=fb["dnorm2"].reshape(d),
                  conv_w_dw=dwdw)
        small[l] = sg
        if l == N_LAYERS - 1:
            sg = dict(sg, final_norm=dfinal.reshape(d))
        norm2_rows = sp["norm2"] + (started(l, "in", w_in_grad) + started(l, "small", sg))
    return loss[0, 0], dx, big, small, dfinal.reshape(d)


def _place():
    return lax.axis_index("x"), lax.axis_index("y"), lax.axis_index("c")


def _other_chips(x, y):
    return [(1 - x, y), (x, 1 - y), (1 - x, 1 - y)]


def _remote(src, dst, send_sem, recv_sem, device):
    return pltpu.make_async_remote_copy(src_ref=src, dst_ref=dst, send_sem=send_sem, recv_sem=recv_sem,
                                        device_id=device, device_id_type=MESH)


def _hbm(v):
    return pltpu.with_memory_space_constraint(v, pltpu.HBM)


def _cast_into(name, w, place, dtype, after=()):
    nl, k, n = w.shape
    tr = _row_tile(k, n)
    nt = k // tr

    def body(place_ref, w_ref, *rest):
        o0_ref, o1_ref = rest[len(after):]

        @pl.when(pl.program_id(0) == 0)
        def _():
            o0_ref[...] = w_ref[...].astype(dtype)

        @pl.when(pl.program_id(0) == 1)
        def _():
            o1_ref[...] = w_ref[...].astype(dtype)

    return pl.pallas_call(
        body, name=f"cast_{name}",
        grid_spec=pltpu.PrefetchScalarGridSpec(
            num_scalar_prefetch=1, grid=(nl, nt),
            in_specs=[BS((None, tr, n), lambda l, t, pr: (l, t, 0))] + [ANY] * len(after),
            out_specs=[BS((None, tr, n), lambda l, t, pr: (pr[0], t * (1 - l) + (nt - 1) * l, 0)),
                       BS((None, tr, n), lambda l, t, pr: (pr[0], t * l, 0))]),
        out_shape=[SDS((N_CHIPS, k, n), dtype)] * 2)(place, w, *after)


def _gather_rows(buf, c):
    k = buf.shape[1]
    if k % 2:
        return pl.ds(0, k)
    return pl.ds(pl.multiple_of(c * (k // 2), 8), k // 2)


def _allgather_start(tag, groups):
    ng = len(groups)
    sizes = [len(g) for g in groups]
    first = [sum(sizes[:g]) for g in range(ng)]
    flat = [b for g in groups for b in g]
    nb = len(flat)

    def body(*refs):
        ins = refs[:nb]
        sems = refs[nb:nb + 2 * ng]
        token = refs[-1]
        x, y, c = _place()
        jme = 2 * x + y
        for g in range(ng):
            for a in range(sizes[g]):
                buf = ins[first[g] + a]
                blk = buf.at[jme, _gather_rows(buf, c)]
                for k, (cx, cy) in enumerate(_other_chips(x, y)):
                    _remote(blk, blk, sems[2 * g].at[3 * a + k], sems[2 * g + 1].at[3 * a + k], (cx, cy, c)).start()
        token[...] = jnp.zeros(token.shape, F32)

    sem_shapes = [pltpu.SemaphoreType.DMA((3 * sizes[g // 2],)) for g in range(2 * ng)]
    outs = pl.pallas_call(
        body, name=f"allgather_start_{tag}", in_specs=[HBM] * nb,
        out_specs=[SEM] * (2 * ng) + [HBM] * nb + [pl.BlockSpec(memory_space=pltpu.VMEM)],
        out_shape=sem_shapes + [pltpu.HBM(b.shape, b.dtype) for b in flat] + [SDS((8, 128), F32)],
        input_output_aliases={i: 2 * ng + i for i in range(nb)},
        compiler_params=pltpu.CompilerParams(has_side_effects=SIDE_EFFECT))(*[_hbm(b) for b in flat])
    per_group = [(outs[2 * g], outs[2 * g + 1], outs[2 * ng + first[g]:2 * ng + first[g] + sizes[g]])
                 for g in range(ng)]
    return per_group, outs[-1]


def _allgather_wait(l, send_sems, recv_sems, bufs, after):
    n = len(bufs)

    def body(*refs):
        ins = refs[:n]
        ssem, rsem = refs[n], refs[n + 1]
        x, y, c = _place()
        jme = 2 * x + y
        for a in range(n):
            rows = _gather_rows(ins[a], c)
            for k, (cx, cy) in enumerate(_other_chips(x, y)):
                cp = _remote(ins[a].at[jme, rows], ins[a].at[2 * cx + cy, rows], ssem.at[3 * a + k],
                             rsem.at[3 * a + k], (cx, cy, c))
                cp.wait_send()
                cp.wait_recv()

    return pl.pallas_call(
        body, name=f"allgather_wait_{l}", in_specs=[HBM] * n + [SEM, SEM] + [ANY] * len(after), out_specs=[HBM] * n,
        out_shape=[pltpu.HBM(b.shape, b.dtype) for b in bufs], input_output_aliases={i: i for i in range(n)},
        compiler_params=pltpu.CompilerParams(has_side_effects=SIDE_EFFECT))(*bufs, send_sems, recv_sems, *after)


def _allgather_forward(l, bufs):
    n = len(bufs)
    split = [a for a in range(n) if bufs[a].shape[1] % 2 == 0]

    def body(*refs):
        ins = refs[:n]
        send_sems, recv_sems = refs[2 * n:]
        x, y, c = _place()
        sibling = (x, y, 1 - c)
        copies = []
        for a in split:
            for k, (cx, cy) in enumerate(_other_chips(x, y)):
                blk = ins[a].at[2 * cx + cy, _gather_rows(ins[a], c)]
                cp = _remote(blk, blk, send_sems.at[a, k], recv_sems.at[a, k], sibling)
                cp.start()
                copies.append(cp)
        for a in split:
            for k, (cx, cy) in enumerate(_other_chips(x, y)):
                blk = ins[a].at[2 * cx + cy, _gather_rows(ins[a], 1 - c)]
                _remote(blk, blk, send_sems.at[a, k], recv_sems.at[a, k], sibling).wait_recv()
        for cp in copies:
            cp.wait_send()

    sem = pltpu.SemaphoreType.DMA((n, 3))
    return pl.pallas_call(
        body, name=f"allgather_forward_{l}", in_specs=[ANY] * n, out_specs=[ANY] * n,
        out_shape=[SDS(b.shape, b.dtype) for b in bufs], input_output_aliases={i: i for i in range(n)},
        scratch_shapes=[sem, sem])(*bufs)


def _rs_to_owner(l, parts):
    n = len(parts)
    lands = [lax.empty((3,) + p.shape[1:], p.dtype) for p in parts]

    def body(*refs):
        ins, zones = refs[:n], refs[n:2 * n]
        send_sems, recv_sems = refs[2 * n], refs[2 * n + 1]
        token = refs[-1]
        x, y, c = _place()
        for a in range(n):
            for k, (cx, cy) in enumerate(_other_chips(x, y)):
                _remote(ins[a].at[2 * cx + cy], zones[a].at[k], send_sems.at[3 * a + k], recv_sems.at[3 * a + k],
                        (cx, cy, c)).start()
        token[...] = jnp.zeros(token.shape, F32)

    sem = pltpu.SemaphoreType.DMA((3 * n,))
    outs = pl.pallas_call(
        body, name=f"rs_to_owner_start_{l}", in_specs=[HBM] * (2 * n),
        out_specs=[SEM, SEM] + [HBM] * (2 * n) + [pl.BlockSpec(memory_space=pltpu.VMEM)],
        out_shape=[sem, sem] + [pltpu.HBM(p.shape, p.dtype) for p in parts]
        + [pltpu.HBM(z.shape, z.dtype) for z in lands] + [SDS((8, 128), F32)],
        input_output_aliases={i: 2 + i for i in range(2 * n)},
        compiler_params=pltpu.CompilerParams(has_side_effects=SIDE_EFFECT),
    )(*[_hbm(p) for p in parts], *[_hbm(z) for z in lands])
    return outs[0], outs[1], outs[2:2 + n], outs[2 + n:2 + 2 * n], outs[-1]


def _rs_to_owner_wait(l, send_sems, recv_sems, parts, lands, after):
    n = len(parts)

    def body(*refs):
        ins, zones = refs[:n], refs[n:2 * n]
        ssem, rsem = refs[2 * n], refs[2 * n + 1]
        x, y, c = _place()
        for a in range(n):
            for k, (cx, cy) in enumerate(_other_chips(x, y)):
                cp = _remote(ins[a].at[2 * cx + cy], zones[a].at[k], ssem.at[3 * a + k], rsem.at[3 * a + k],
                             (cx, cy, c))
                cp.wait_send()
                cp.wait_recv()

    outs = pl.pallas_call(
        body, name=f"rs_to_owner_wait_{l}", in_specs=[HBM] * (2 * n) + [SEM, SEM] + [ANY] * len(after),
        out_specs=[HBM] * (2 * n),
        out_shape=[pltpu.HBM(p.shape, p.dtype) for p in parts] + [pltpu.HBM(z.shape, z.dtype) for z in lands],
        input_output_aliases={i: i for i in range(2 * n)},
        compiler_params=pltpu.CompilerParams(has_side_effects=SIDE_EFFECT),
    )(*parts, *lands, send_sems, recv_sems, *after)
    return outs[:n], outs[n:]


def _rs_sibling_exchange(l, both):
    n = len(both)

    def body(*refs):
        ins = refs[:n]
        send_sems, recv_sems = refs[2 * n:]
        x, y, c = _place()
        copies = []
        for a in range(n):
            cp = _remote(ins[a].at[c], ins[a].at[c], send_sems.at[a], recv_sems.at[a], (x, y, 1 - c))
            cp.start()
            copies.append(cp)
        for a, cp in enumerate(copies):
            cp.wait_send()
            _remote(ins[a].at[1 - c], ins[a].at[1 - c], send_sems.at[a], recv_sems.at[a], (x, y, 1 - c)).wait_recv()

    sem = pltpu.SemaphoreType.DMA((n,))
    return pl.pallas_call(
        body, name=f"rs_sibling_exchange_{l}", in_specs=[ANY] * n, out_specs=[ANY] * n,
        out_shape=[SDS(b.shape, b.dtype) for b in both], input_output_aliases={i: i for i in range(n)},
        scratch_shapes=[sem, sem])(*both)


def _add_owner(name, grad, recv, place):
    r, cols = grad.shape
    tr = _row_tile(r, cols, budget=1024 * 1024)
    nt = r // tr

    def body(place_ref, g_ref, r_ref, o_ref):
        acc = ((g_ref[...] + r_ref[0].astype(F32)) + r_ref[1].astype(F32)) + r_ref[2].astype(F32)
        o_ref[...] = acc.astype(o_ref.dtype)

    return pl.pallas_call(
        body, name=name,
        grid_spec=pltpu.PrefetchScalarGridSpec(
            num_scalar_prefetch=1, grid=(nt,),
            in_specs=[BS((tr, cols), lambda t, pr: (t, 0)), BS((3, tr, cols), lambda t, pr: (0, t, 0))],
            out_specs=BS((None, tr, cols), lambda t, pr: (pr[1], t, 0))),
        out_shape=SDS((2, r, cols), WIRE_DTYPE))(place, grad, recv)


def _reduce_start(tag, grads):
    names = list(grads)
    send_sems, recv_sems, wires, lands, token = _rs_to_owner(tag, [grads[n][1] for n in names])
    return dict(tag=tag, names=names, send_sems=send_sems, recv_sems=recv_sems, wires=wires, lands=lands,
                grads=[grads[n][0] for n in names]), token


def _reduce_finish(pending, place, after):
    tag, names = pending["tag"], pending["names"]
    _, lands = _rs_to_owner_wait(tag, pending["send_sems"], pending["recv_sems"], pending["wires"],
                                 pending["lands"], after)
    mine = [_add_owner(f"rs_add_owner_{n}_{tag}", g, r, place) for n, g, r in zip(names, pending["grads"], lands)]
    return dict(zip(names, _rs_sibling_exchange(tag, mine)))


def _small_peers(x, y, c):
    return [(x, y, 1 - c)] + [(cx, cy, c) for cx, cy in _other_chips(x, y)]


def _allgather_rows_start(tag, bufs):
    n = len(bufs)
    lands = [lax.empty((8,) + b.shape, b.dtype) for b in bufs]

    def body(*refs):
        ins, zones = refs[:n], refs[n:2 * n]
        send_sems, recv_sems = refs[2 * n], refs[2 * n + 1]
        token = refs[-1]
        x, y, c = _place()
        for a in range(n):
            for i, peer in enumerate(_small_peers(x, y, c)):
                _remote(ins[a], zones[a].at[4 * x + 2 * y + c], send_sems.at[4 * a + i], recv_sems.at[4 * a + i],
                        peer).start()
        token[...] = jnp.zeros(token.shape, F32)

    sem = pltpu.SemaphoreType.DMA((4 * n,))
    outs = pl.pallas_call(
        body, name=f"allgather_small_start_{tag}", in_specs=[HBM] * (2 * n),
        out_specs=[SEM, SEM] + [HBM] * (2 * n) + [pl.BlockSpec(memory_space=pltpu.VMEM)],
        out_shape=[sem, sem] + [pltpu.HBM(b.shape, b.dtype) for b in bufs]
        + [pltpu.HBM(z.shape, z.dtype) for z in lands] + [SDS((8, 128), F32)],
        input_output_aliases={i: 2 + i for i in range(2 * n)},
        compiler_params=pltpu.CompilerParams(has_side_effects=SIDE_EFFECT),
    )(*[_hbm(b) for b in bufs], *[_hbm(z) for z in lands])
    return outs[0], outs[1], outs[2:2 + n], outs[2 + n:2 + 2 * n], outs[-1]


def _allgather_rows_wait(tag, send_sems, recv_sems, bufs, lands, after):
    n = len(bufs)

    def body(*refs):
        ins, zones = refs[:n], refs[n:2 * n]
        ssem, rsem = refs[2 * n], refs[2 * n + 1]
        x, y, c = _place()
        for a in range(n):
            for i, (px, py, pc) in enumerate(_small_peers(x, y, c)):
                cp = _remote(ins[a], zones[a].at[4 * px + 2 * py + pc], ssem.at[4 * a + i], rsem.at[4 * a + i],
                             (px, py, pc))
                cp.wait_send()
                cp.wait_recv()

    outs = pl.pallas_call(
        body, name=f"allgather_small_wait_{tag}", in_specs=[HBM] * (2 * n) + [SEM, SEM, ANY],
        out_specs=[HBM] * (2 * n),
        out_shape=[pltpu.HBM(b.shape, b.dtype) for b in bufs] + [pltpu.HBM(z.shape, z.dtype) for z in lands],
        input_output_aliases={i: i for i in range(2 * n)},
        compiler_params=pltpu.CompilerParams(has_side_effects=SIDE_EFFECT),
    )(*bufs, *lands, send_sems, recv_sems, after)
    return outs[:n], outs[n:]


def _allgather_rows_forward(tag, lands):
    n = len(lands)

    def body(*refs):
        ins = refs[:n]
        send_sems, recv_sems = refs[2 * n:]
        x, y, c = _place()
        sibling = (x, y, 1 - c)
        copies = []
        for a in range(n):
            for k, (cx, cy) in enumerate(_other_chips(x, y)):
                blk = ins[a].at[4 * cx + 2 * cy + c]
                cp = _remote(blk, blk, send_sems.at[a, k], recv_sems.at[a, k], sibling)
                cp.start()
                copies.append(cp)
        for a in range(n):
            for k, (cx, cy) in enumerate(_other_chips(x, y)):
                blk = ins[a].at[4 * cx + 2 * cy + 1 - c]
                _remote(blk, blk, send_sems.at[a, k], recv_sems.at[a, k], sibling).wait_recv()
        for cp in copies:
            cp.wait_send()

    sem = pltpu.SemaphoreType.DMA((n, 3))
    return pl.pallas_call(body, name=f"allgather_small_forward_{tag}", in_specs=[ANY] * n, out_specs=[ANY] * n,
                          out_shape=[SDS(z.shape, z.dtype) for z in lands],
                          input_output_aliases={i: i for i in range(n)}, scratch_shapes=[sem, sem])(*lands)


def _sum_devices(tag, gathered, mine, place):
    _, r, cols = gathered.shape
    tr = _row_tile(r, cols, budget=256 * 1024)

    def body(place_ref, g_ref, x_ref, o_ref):
        me = 2 * place_ref[0] + place_ref[1]
        acc = jnp.where(me == 0, x_ref[...], g_ref[0])
        for k in range(1, 8):
            acc = acc + jnp.where(me == k, x_ref[...], g_ref[k])
        o_ref[...] = acc

    return pl.pallas_call(
        body, name=f"sum_small_grads_{tag}",
        grid_spec=pltpu.PrefetchScalarGridSpec(
            num_scalar_prefetch=1, grid=(r // tr,),
            in_specs=[BS((8, tr, cols), lambda t, pr: (0, t, 0)), BS((tr, cols), lambda t, pr: (t, 0))],
            out_specs=BS((tr, cols), lambda t, pr: (t, 0))),
        out_shape=SDS((r, cols), F32))(place, gathered, mine)


def _adamw_values(w, g, m, v):
    m = ADAM_B1 * m + (1.0 - ADAM_B1) * g
    v = ADAM_B2 * v + (1.0 - ADAM_B2) * (g * g)
    m_hat = m / (1.0 - ADAM_B1 ** ADAM_STEP)
    v_hat = v / (1.0 - ADAM_B2 ** ADAM_STEP)
    delta = -ADAM_LR * (m_hat / (jnp.sqrt(v_hat) + ADAM_EPS) + ADAM_WD * w)
    return delta, m, v


def _adamw_big(name, l, w, m, v, g, earlier=None, after=()):
    nl, r, cols = w.shape
    tr = _row_tile(r, cols, budget=1024 * 1024)
    nt = r // tr
    n_prev = 0 if earlier is None else 4

    def body(*refs):
        w_ref, m_ref, v_ref, g_ref = refs[:4]
        go_ref, d_ref, mo_ref, vo_ref = refs[4 + n_prev + len(after):]
        gv = g_ref[0].astype(F32) + g_ref[1].astype(F32)
        delta, m_new, v_new = _adamw_values(w_ref[...], gv, m_ref[...], v_ref[...])
        go_ref[...] = gv
        d_ref[...] = delta
        mo_ref[...] = m_new
        vo_ref[...] = v_new

    layer = BS((None, tr, cols), lambda t: (l, t, 0))
    return pl.pallas_call(
        body, name=f"adamw_{name}_l{l}", grid=(nt,),
        in_specs=[layer, layer, layer, BS((2, tr, cols), lambda t: (0, t, 0))] + [ANY] * (n_prev + len(after)),
        out_specs=[layer] * 4, out_shape=[SDS(w.shape, F32)] * 4,
        input_output_aliases={4 + i: i for i in range(n_prev)}, compiler_params=_params(),
    )(w, m, v, g, *(earlier or ()), *after)


def _adamw_mid(name, w, m, v, gathered, mine, place):
    shape = w.shape[1:]
    zeros = (0,) * len(shape)

    def body(place_ref, w_ref, m_ref, v_ref, *refs):
        gath, own = refs[:N_LAYERS], refs[N_LAYERS:2 * N_LAYERS]
        go_ref, d_ref, mo_ref, vo_ref = refs[2 * N_LAYERS:]
        me = 2 * place_ref[0] + place_ref[1]
        sums = []
        for l in range(N_LAYERS):
            acc = jnp.where(me == 0, own[l][...], gath[l][0])
            for k in range(1, 8):
                acc = acc + jnp.where(me == k, own[l][...], gath[l][k])
            sums.append(acc)
        gv = sums[0]
        for l in range(1, N_LAYERS):
            gv = jnp.where(pl.program_id(0) == l, sums[l], gv)
        delta, m_new, v_new = _adamw_values(w_ref[...], gv, m_ref[...], v_ref[...])
        go_ref[...] = gv
        d_ref[...] = delta
        mo_ref[...] = m_new
        vo_ref[...] = v_new

    layer = BS((None,) + shape, lambda l, pr: (l,) + zeros)
    return pl.pallas_call(
        body, name=f"adamw_{name}",
        grid_spec=pltpu.PrefetchScalarGridSpec(
            num_scalar_prefetch=1, grid=(N_LAYERS,),
            in_specs=[layer] * 3 + [BS((8,) + shape, lambda l, pr: (0,) + zeros)] * N_LAYERS
            + [BS(shape, lambda l, pr: zeros)] * N_LAYERS,
            out_specs=[layer] * 4),
        out_shape=[SDS(w.shape, F32)] * 4, compiler_params=_params())(place, w, m, v, *gathered, *mine)


def _adamw_rows(w, m, v, g):
    r, cols = w.shape
    tr = _row_tile(r, cols, budget=512 * 1024)

    def body(w_ref, m_ref, v_ref, g_ref, d_ref, mo_ref, vo_ref):
        delta, m_new, v_new = _adamw_values(w_ref[...], g_ref[...], m_ref[...], v_ref[...])
        d_ref[...] = delta
        mo_ref[...] = m_new
        vo_ref[...] = v_new

    spec = BS((tr, cols), lambda t: (t, 0))
    return pl.pallas_call(body, name="adamw_small", grid=(r // tr,), in_specs=[spec] * 4, out_specs=[spec] * 3,
                          out_shape=[SDS(w.shape, F32)] * 3)(w, m, v, g)


PACK_ALIGN = 8 * 128
PACK_ROWS = 128


def _pack_rows(arrays):
    parts, rows = [], 0
    for a in arrays:
        flat = a.reshape(-1)
        pad = (-flat.shape[0]) % PACK_ALIGN
        if pad:
            flat = jnp.pad(flat, (0, pad))
        parts.append(flat.reshape(-1, 128))
        rows += parts[-1].shape[0]
    if rows % PACK_ROWS:
        parts.append(jnp.zeros((PACK_ROWS - rows % PACK_ROWS, 128), parts[0].dtype))
    return jnp.concatenate(parts, axis=0)


def _unpack_rows(buf, shapes):
    out, row = [], 0
    for shape in shapes:
        size = math.prod(shape)
        rows = -(-size // PACK_ALIGN) * (PACK_ALIGN // 128)
        out.append(buf[row:row + rows].reshape(-1)[:size].reshape(shape))
        row += rows
    return out


def kernel(x, norm1, w_in, b_gate, ssm_a_re, ssm_a_im, ssm_log_dt, ssm_b_re, ssm_b_im, ssm_c_re, ssm_c_im, ssm_d, ssm_w_glu, ssm_b_glu, ssm_w_proj, conv_w_dw, conv_b_dw, conv_ln_g, conv_ln_b, conv_w_proj, pool_w_group, pool_scale, pool_w_proj, w_out, norm2, ffn_w_gate, ffn_w_up, ffn_w_down, final_norm, loss_target, m_norm1, m_w_in, m_b_gate, m_ssm_a_re, m_ssm_a_im, m_ssm_log_dt, m_ssm_b_re, m_ssm_b_im, m_ssm_c_re, m_ssm_c_im, m_ssm_d, m_ssm_w_glu, m_ssm_b_glu, m_ssm_w_proj, m_conv_w_dw, m_conv_b_dw, m_conv_ln_g, m_conv_ln_b, m_conv_w_proj, m_pool_w_group, m_pool_scale, m_pool_w_proj, m_w_out, m_norm2, m_ffn_w_gate, m_ffn_w_up, m_ffn_w_down, m_final_norm, v_norm1, v_w_in, v_b_gate, v_ssm_a_re, v_ssm_a_im, v_ssm_log_dt, v_ssm_b_re, v_ssm_b_im, v_ssm_c_re, v_ssm_c_im, v_ssm_d, v_ssm_w_glu, v_ssm_b_glu, v_ssm_w_proj, v_conv_w_dw, v_conv_b_dw, v_conv_ln_g, v_conv_ln_b, v_conv_w_proj, v_pool_w_group, v_pool_scale, v_pool_w_proj, v_w_out, v_norm2, v_ffn_w_gate, v_ffn_w_up, v_ffn_w_down, v_final_norm):
    given = dict(locals())
    cx, cy, cc = _place()
    place = jnp.stack([2 * cx + cy, cc]).astype(jnp.int32)

    def kernel_view(n, a):
        if n in TRANSPOSED:
            return a.transpose(0, 2, 1)
        return a.transpose(0, 1, 3, 2) if n in ("ssm_b_re", "ssm_b_im") else a

    prm = {n: given[n] for n in WEIGHTS}
    mom = {n: given["m_" + n] for n in WEIGHTS}
    var = {n: given["v_" + n] for n in WEIGHTS}
    for n in MID:
        prm[n], mom[n], var[n] = kernel_view(n, prm[n]), kernel_view(n, mom[n]), kernel_view(n, var[n])

    dw_shard = prm["conv_w_dw"].reshape(N_LAYERS, CONV_KERNEL, -1)
    casts = {"w_in": _cast_into("w_in", prm["w_in"], place, MXU_DTYPE)}
    first, first_started = _allgather_start("first", [[casts["w_in"][0]]])
    in_flight = {(0, "in"): first[0]}
    casts.update({n: _cast_into(n, kernel_view(n, prm[n]), place, MXU_DTYPE, after=(first_started,))
                  for n in BIG if n != "w_in"})
    casts["conv_w_dw"] = _cast_into("conv_w_dw", dw_shard, place, F32, after=(first_started,))
    order = [(l, g) for l in range(N_LAYERS) for g in GATHER_GROUPS if (l, g) != (0, "in")]
    rest, rest_started = _allgather_start("rest", [[casts[n][l] for n in GATHER_GROUPS[g]] for l, g in order])
    in_flight.update(zip(order, rest))

    def weights_of(l, group, after):
        send_sems, recv_sems, bufs = in_flight[l, group]
        tag = f"l{l}_{group}"
        if (l, group) == (0, "in"):
            after = after + (rest_started,)
        bufs = _allgather_forward(tag, _allgather_wait(tag, send_sems, recv_sems, bufs, after))
        fw = dict(zip(GATHER_GROUPS[group], bufs))
        if "conv_w_dw" in fw:
            fw["conv_w_dw"] = fw["conv_w_dw"].transpose(1, 0, 2).reshape(CONV_KERNEL, -1)
        return fw

    pending, small_pending, small_shapes = {}, {}, {}
    tokens = {}

    def on_grads(l, group, grads):
        if group == "small":
            packed = {n: g for n, g in grads.items() if n not in MID}
            small_shapes[l] = {n: g.shape for n, g in packed.items()}
            begun = _allgather_rows_start(f"l{l}", [_pack_rows(list(packed.values()))] + [grads[n] for n in MID])
            small_pending[l], token = begun[:4], begun[4]
        else:
            pending[l, group], token = _reduce_start(f"{l}_{group}", grads)
        tokens[l, group] = token
        return token[0, 0]

    loss, dx, _, _, _ = _local_step(x[0], loss_target[0], weights_of, prm, place, on_grads)
    loss = lax.psum(loss, ("x", "y", "c"))

    reduced = [{} for _ in range(N_LAYERS)]
    out = {}

    def finish(l, group, after):
        reduced[l].update(_reduce_finish(pending[l, group], place, after))

    def adamw(l, names, done):
        for n in names:
            out[n] = _adamw_big(n, l, kernel_view(n, prm[n]), kernel_view(n, mom[n]), kernel_view(n, var[n]),
                                reduced[l][n], out.get(n), after=done)
            done = (out[n][0],)
        return done

    top = N_LAYERS - 1
    done = (tokens[0, "in"], tokens[0, "small"])
    for group in ("ffn", "mixer", "in"):
        finish(top, group, done)
    done = adamw(top, BIG, done)
    for group in ("ffn", "mixer", "in"):
        finish(0, group, done)
        done = adamw(0, [n for n in GATHER_GROUPS[group] if n in BIG], done)
    for n in BIG:
        out[n] = tuple(kernel_view(n, a) for a in out[n])

    gsmall = {}
    mid_mine, mid_gathered = [], []
    for l in range(N_LAYERS):
        mine, lands = _allgather_rows_wait(f"l{l}", *small_pending[l], done[0])
        lands = _allgather_rows_forward(f"l{l}", lands)
        mid_mine.append(mine[1:])
        mid_gathered.append(lands[1:])
        gsum = _sum_devices(f"l{l}", lands[0], mine[0], place)
        for n, g in zip(small_shapes[l], _unpack_rows(gsum, list(small_shapes[l].values()))):
            gsmall.setdefault(n, [None] * N_LAYERS)[l] = g
    for i, n in enumerate(MID):
        out[n] = tuple(kernel_view(n, a) for a in _adamw_mid(
            n, prm[n], mom[n], var[n], [mid_gathered[l][i] for l in range(N_LAYERS)],
            [mid_mine[l][i] for l in range(N_LAYERS)], place))
    gsmall = {n: (g[top] if n == "final_norm" else jnp.stack(g)) for n, g in gsmall.items()}
    lanes = dw_shard.shape[-1]
    gsmall["conv_w_dw"] = lax.dynamic_slice_in_dim(gsmall["conv_w_dw"], (2 * cx + cy) * lanes, lanes, axis=2)
    small_names = [n for n in SMALL if n not in MID] + ["conv_w_dw"]
    w_rows = _pack_rows([prm[n] for n in small_names])
    m_rows = _pack_rows([mom[n] for n in small_names])
    v_rows = _pack_rows([var[n] for n in small_names])
    g_rows = _pack_rows([gsmall[n] for n in small_names])
    shapes = [prm[n].shape for n in small_names]
    d_s, m_s, v_s = (_unpack_rows(r, shapes) for r in _adamw_rows(w_rows, m_rows, v_rows, g_rows))
    for i, n in enumerate(small_names):
        out[n] = (gsmall[n].reshape(prm[n].shape), d_s[i], m_s[i], v_s[i])
    grads = [out[n][0] for n in WEIGHTS]
    deltas = [out[n][1] for n in WEIGHTS]
    new_m = [out[n][2] for n in WEIGHTS]
    new_v = [out[n][3] for n in WEIGHTS]
    return (loss, dx[None], *grads, *deltas, *new_m, *new_v)
```

```python
import functools
import math

import jax
import jax.numpy as jnp
from jax import lax
from jax.experimental import pallas as pl
from jax.experimental.pallas import tpu as pltpu

F32 = jnp.float32
MXU_DTYPE = jnp.bfloat16
WIRE_DTYPE = jnp.bfloat16
SDS = jax.ShapeDtypeStruct
BS = pl.BlockSpec
ANY = pl.BlockSpec(memory_space=pl.ANY)
HBM = pl.BlockSpec(memory_space=pltpu.HBM)
SEM = pl.BlockSpec(memory_space=pltpu.SEMAPHORE)
SIDE_EFFECT = pltpu.SideEffectType.DATAFLOW_SIDE_EFFECTING
MESH = pl.DeviceIdType.MESH

EPS = 1e-6
N_CHIPS = 4
N_LAYERS = 2
SSM_GROUPS, SSM_STATE, SSM_GROUP = 32, 64, 16
CONV_KERNEL = 31
CONV_PAD = 32
POOL_WINDOWS = (2, 4, 8, 16)
GELU_C = math.sqrt(2.0 / math.pi)
ADAM_LR, ADAM_B1, ADAM_B2, ADAM_EPS, ADAM_WD, ADAM_STEP = 0.001, 0.9, 0.999, 1e-08, 0.01, 10
VMEM_LIMIT = 56 * 1024 * 1024

BIG = ("w_in", "ssm_w_glu", "ssm_w_proj", "conv_w_proj", "pool_w_proj", "w_out", "ffn_w_gate", "ffn_w_up", "ffn_w_down")
TRANSPOSED = ("ffn_w_gate", "ffn_w_up")
MID = ("ssm_b_re", "ssm_b_im", "ssm_c_re", "ssm_c_im")
GATHER_GROUPS = {
    "in": ("w_in",),
    "mixer": ("ssm_w_glu", "ssm_w_proj", "conv_w_proj", "pool_w_proj", "w_out", "conv_w_dw"),
    "ffn": ("ffn_w_gate", "ffn_w_up", "ffn_w_down"),
}
SMALL = ("norm1", "b_gate", "ssm_a_re", "ssm_a_im", "ssm_log_dt", "ssm_b_re", "ssm_b_im", "ssm_c_re", "ssm_c_im",
         "ssm_d", "ssm_b_glu", "conv_b_dw", "conv_ln_g", "conv_ln_b", "pool_w_group", "pool_scale", "norm2",
         "final_norm")
WEIGHTS = ("norm1", "w_in", "b_gate", "ssm_a_re", "ssm_a_im", "ssm_log_dt", "ssm_b_re", "ssm_b_im", "ssm_c_re",
           "ssm_c_im", "ssm_d", "ssm_w_glu", "ssm_b_glu", "ssm_w_proj", "conv_w_dw", "conv_b_dw", "conv_ln_g",
           "conv_ln_b", "conv_w_proj", "pool_w_group", "pool_scale", "pool_w_proj", "w_out", "norm2", "ffn_w_gate",
           "ffn_w_up", "ffn_w_down", "final_norm")


def _params(vmem=True):
    return pltpu.CompilerParams(vmem_limit_bytes=VMEM_LIMIT) if vmem else None


def _mm(a, b):
    return jnp.dot(a.astype(MXU_DTYPE), b.astype(MXU_DTYPE), preferred_element_type=F32)


def _mm_nt(a, b):
    return lax.dot_general(a.astype(MXU_DTYPE), b.astype(MXU_DTYPE), (((1,), (1,)), ((), ())),
                           preferred_element_type=F32)


def _mm_tn(a, b):
    return lax.dot_general(a.astype(MXU_DTYPE), b.astype(MXU_DTYPE), (((0,), (0,)), ((), ())),
                           preferred_element_type=F32)


def _sigmoid(x):
    return jax.nn.sigmoid(x)


def _gelu(x):
    t = jnp.tanh(GELU_C * (x + 0.044715 * (x * x * x)))
    return x * (0.5 * (1.0 + t)), t


def _gelu_grad(x, t):
    return 0.5 * (1.0 + t) + 0.5 * x * (1.0 - t * t) * (GELU_C * (1.0 + 3.0 * 0.044715 * x * x))


def _colsum(v):
    return jnp.sum(v, axis=0, keepdims=True)


def _row_tile(rows, cols, itemsize=4, budget=1536 * 1024):
    best = None
    for t in range(8, rows + 1, 8):
        if rows % t == 0 and t * cols * itemsize <= budget:
            best = t
    return best if best is not None else rows


def _in_proj(l, x, norm1, w_in):
    s, d = x.shape
    nc = w_in.shape[-1]
    tm = min(1024, s)
    nt = s // tm

    def body(x_ref, g_ref, w_ref, z_ref, h_ref, h_all):
        i = pl.program_id(1)
        rows = pl.ds(pl.multiple_of(i * tm, tm), tm)

        @pl.when(pl.program_id(0) == 0)
        def _():
            xv = x_ref[...]
            r = lax.rsqrt(jnp.mean(xv * xv, axis=-1, keepdims=True) + EPS)
            hv = (xv * r * g_ref[...]).astype(h_ref.dtype)
            h_ref[...] = hv.T
            h_all[rows, :] = hv

        z_ref[...] = _mm(h_all[rows, :], w_ref[...])

    tile_of = lambda j, i: i * (1 - jnp.minimum(j, 1)) + (nt - 1) * jnp.minimum(j, 1)
    return pl.pallas_call(
        body, name=f"in_proj_l{l}", grid=(N_CHIPS, nt),
        in_specs=[BS((tm, d), lambda j, i: (tile_of(j, i), 0)), BS((None, 1, d), lambda j, i: (l, 0, 0)),
                  BS((None, d, nc), lambda j, i: (j, 0, 0))],
        out_specs=[BS((tm, nc), lambda j, i: (i, j)), BS((d, tm), lambda j, i: (0, tile_of(j, i)))],
        out_shape=[SDS((s, N_CHIPS * nc), F32), SDS((d, s), MXU_DTYPE)],
        scratch_shapes=[pltpu.VMEM((s, d), MXU_DTYPE)], compiler_params=_params())(x, norm1, w_in)


def _mm_cols(a, w_ref):
    return jnp.concatenate([_mm(a, w_ref[j]) for j in range(N_CHIPS)], axis=1)


def _mm_nt_cols(dv, w_ref):
    nc = w_ref.shape[-1]
    acc = _mm_nt(dv[:, 0:nc], w_ref[0])
    for j in range(1, N_CHIPS):
        acc = acc + _mm_nt(dv[:, j * nc:(j + 1) * nc], w_ref[j])
    return acc


def _merge_values(y, hc, p, zg, wglu, bglu, wpa, wpb, wpc, lng, lnb, wgrp, scale, bg):
    v = {}
    ge, th = _gelu(y)
    t = _mm(ge, wglu) + bglu
    sg = _sigmoid(t)
    sa = ge * sg
    ya = _mm_cols(sa, wpa)
    mu = jnp.mean(hc, axis=-1, keepdims=True)
    xc = hc - mu
    r = lax.rsqrt(jnp.mean(xc * xc, axis=-1, keepdims=True) + EPS)
    xh = xc * r
    ln = xh * lng + lnb
    sl = _sigmoid(ln)
    ac = ln * sl
    yb = _mm_cols(ac, wpb)
    gw = p.shape[1] // len(POOL_WINDOWS)
    q = jnp.concatenate([_mm(p[:, k * gw:(k + 1) * gw], wgrp[k]) for k in range(len(POOL_WINDOWS))], axis=1)
    pp = q * scale
    yc = _mm_cols(pp, wpc)
    d = ya.shape[1]
    gates = [_sigmoid(zg[k] + bg[:, k * d:(k + 1) * d]) for k in range(3)]
    merged = gates[0] * ya + gates[1] * yb + gates[2] * yc
    v.update(ge=ge, th=th, sg=sg, sa=sa, ya=ya, r=r, xh=xh, ln=ln, sl=sl, ac=ac, yb=yb, q=q, pp=pp, yc=yc,
             gates=gates, merged=merged)
    return v


def _merge_specs(l, tm, d, cw):
    row = lambda n: BS((None, 1, n), lambda i: (l, 0, 0))
    resident = lambda shp: BS(shp, lambda i: (0, 0, 0), pipeline_mode=pl.Buffered(1))
    return [
        BS((tm, cw), lambda i: (i, 0)),
        BS((tm, cw), lambda i: (i, 0)),
        BS((tm, cw), lambda i: (i, 0)),
        BS((tm, d), lambda i: (i, 2)), BS((tm, d), lambda i: (i, 3)), BS((tm, d), lambda i: (i, 4)),
        resident((N_CHIPS, cw // N_CHIPS, cw)),
        row(cw),
        resident((N_CHIPS, cw, d // N_CHIPS)),
        resident((N_CHIPS, cw, d // N_CHIPS)),
        resident((N_CHIPS, cw, d // N_CHIPS)),
        row(cw), row(cw),
        BS((None, 4, cw // 4, cw // 4), lambda i: (l, 0, 0, 0)),
        row(cw),
        row(3 * d),
        resident((N_CHIPS, d // N_CHIPS, d)),
    ]


def _merge_fwd(l, x, y, hc, p, z, fw, sp):
    s, d = x.shape
    cw = y.shape[1]
    tm = min(512, s)

    def body(x_ref, y_ref, hc_ref, p_ref, z0, z1, z2, wglu, bglu, wpa, wpb, wpc, lng, lnb, wgrp, scale, bg, wout,
             x1_ref):
        v = _merge_values(y_ref[...], hc_ref[...], p_ref[...], (z0[...], z1[...], z2[...]),
                          wglu[...].reshape(cw, cw), bglu[...], wpa, wpb, wpc, lng[...], lnb[...], wgrp, scale[...],
                          bg[...])
        x1_ref[...] = x_ref[...] + _mm(v["merged"], wout[...].reshape(d, d))

    return pl.pallas_call(
        body, name=f"merge_fwd_l{l}", grid=(s // tm,),
        in_specs=[BS((tm, d), lambda i: (i, 0))] + _merge_specs(l, tm, d, cw),
        out_specs=BS((tm, d), lambda i: (i, 0)), out_shape=SDS((s, d), F32), compiler_params=_params(),
    )(x, y, hc, p, z, z, z, fw["ssm_w_glu"], sp["ssm_b_glu"], fw["ssm_w_proj"], fw["conv_w_proj"], fw["pool_w_proj"],
      sp["conv_ln_g"], sp["conv_ln_b"], sp["pool_w_group"], sp["pool_scale"], sp["b_gate"], fw["w_out"])


def _merge_bwd(l, dx1, y, hc, p, z, fw, sp):
    s, d = dx1.shape
    cw = y.shape[1]
    tm = min(256, s)
    m = MXU_DTYPE

    def body(dx1_ref, y_ref, hc_ref, p_ref, z0, z1, z2, wglu, bglu, wpa, wpb, wpc, lng, lnb, wgrp, scale, bg, wout,
             dzg_ref, dy_ref, dhc_ref, dp_ref, merged_ref, sa_ref, ac_ref, pp_ref, ge_ref, dt_ref, dya_ref, dyb_ref,
             dyc_ref, dq_ref, dbg_ref, dbglu_ref, dlng_ref, dlnb_ref, dscale_ref):
        yv = y_ref[...]
        wg = wglu[...].reshape(cw, cw)
        v = _merge_values(yv, hc_ref[...], p_ref[...], (z0[...], z1[...], z2[...]), wg, bglu[...], wpa, wpb, wpc,
                          lng[...], lnb[...], wgrp, scale[...], bg[...])
        dm = _mm_nt(dx1_ref[...], wout[...].reshape(d, d))
        ys = (v["ya"], v["yb"], v["yc"])
        dys, dbg = [], []
        for k in range(3):
            gk = v["gates"][k]
            dzk = dm * ys[k] * (gk * (1.0 - gk))
            dbg.append(_colsum(dzk))
            dzg_ref[:, k * d:(k + 1) * d] = dzk.astype(m)
            dys.append((dm * gk).astype(m))
        dsa = _mm_nt_cols(dys[0], wpa)
        dac = _mm_nt_cols(dys[1], wpb)
        dpp = _mm_nt_cols(dys[2], wpc)
        ge, sg = v["ge"], v["sg"]
        dt = dsa * ge * (sg * (1.0 - sg))
        dge = dsa * sg + _mm_nt(dt, wg)
        dy_ref[...] = dge * _gelu_grad(yv, v["th"])
        ln, sl, xh = v["ln"], v["sl"], v["xh"]
        dln = dac * (sl * (1.0 + ln * (1.0 - sl)))
        dxh = dln * lng[...]
        dhc_ref[...] = v["r"] * (dxh - jnp.mean(dxh, axis=-1, keepdims=True)
                                 - xh * jnp.mean(dxh * xh, axis=-1, keepdims=True))
        dq = dpp * scale[...]
        gw = cw // len(POOL_WINDOWS)
        for k in range(len(POOL_WINDOWS)):
            dp_ref[:, k * gw:(k + 1) * gw] = _mm_nt(dq[:, k * gw:(k + 1) * gw], wgrp[k])
        merged_ref[...] = v["merged"].astype(m)
        sa_ref[...] = v["sa"].astype(m)
        ac_ref[...] = v["ac"].astype(m)
        pp_ref[...] = v["pp"].astype(m)
        ge_ref[...] = ge.astype(m)
        dt_ref[...] = dt.astype(m)
        dya_ref[...] = dys[0]
        dyb_ref[...] = dys[1]
        dyc_ref[...] = dys[2]
        dq_ref[...] = dq.astype(m)

        @pl.when(pl.program_id(0) == 0)
        def _():
            for ref in (dbg_ref, dbglu_ref, dlng_ref, dlnb_ref, dscale_ref):
                ref[...] = jnp.zeros(ref.shape, F32)

        dbg_ref[...] += jnp.concatenate(dbg, axis=1)
        dbglu_ref[...] += _colsum(dt)
        dlng_ref[...] += _colsum(dln * xh)
        dlnb_ref[...] += _colsum(dln)
        dscale_ref[...] += _colsum(dpp * v["q"])

    tile = lambda n: BS((tm, n), lambda i: (i, 0))
    acc = lambda n: BS((1, n), lambda i: (0, 0))
    outs = pl.pallas_call(
        body, name=f"merge_bwd_l{l}", grid=(s // tm,),
        in_specs=[tile(d)] + _merge_specs(l, tm, d, cw),
        out_specs=[tile(3 * d), tile(cw), tile(cw), tile(cw), tile(d), tile(cw), tile(cw), tile(cw), tile(cw), tile(cw),
                   tile(d), tile(d), tile(d), tile(cw), acc(3 * d), acc(cw), acc(cw), acc(cw), acc(cw)],
        out_shape=[SDS((s, 3 * d), m), SDS((s, cw), F32), SDS((s, cw), F32), SDS((s, cw), F32), SDS((s, d), m),
                   SDS((s, cw), m), SDS((s, cw), m), SDS((s, cw), m), SDS((s, cw), m), SDS((s, cw), m), SDS((s, d), m),
                   SDS((s, d), m), SDS((s, d), m), SDS((s, cw), m), SDS((1, 3 * d), F32), SDS((1, cw), F32),
                   SDS((1, cw), F32), SDS((1, cw), F32), SDS((1, cw), F32)],
        compiler_params=_params(),
    )(dx1, y, hc, p, z, z, z, fw["ssm_w_glu"], sp["ssm_b_glu"], fw["ssm_w_proj"], fw["conv_w_proj"], fw["pool_w_proj"],
      sp["conv_ln_g"], sp["conv_ln_b"], sp["pool_w_group"], sp["pool_scale"], sp["b_gate"], fw["w_out"])
    names = ("dzg", "dy", "dhc", "dp", "merged", "sa", "ac", "pp", "ge", "dt", "dya", "dyb", "dyc", "dq", "db_gate",
             "db_glu", "dln_g", "dln_b", "dscale")
    return dict(zip(names, outs))


def _ffn_fwd(l, x1, norm2, wg, wu, wd):
    s, d = x1.shape
    hc = wd.shape[1]
    tm = min(1024, s)

    def body(x_ref, g_ref, wg_ref, wu_ref, wd_ref, o_ref, h_scr):
        @pl.when(pl.program_id(1) == 0)
        def _():
            xv = x_ref[...]
            r = lax.rsqrt(jnp.mean(xv * xv, axis=-1, keepdims=True) + EPS)
            h_scr[...] = (xv * r * g_ref[...]).astype(h_scr.dtype)
            o_ref[...] = xv

        h = h_scr[...]
        gate = _mm_nt(h, wg_ref[...])
        up = _mm_nt(h, wu_ref[...])
        o_ref[...] += _mm(gate * _sigmoid(gate) * up, wd_ref[...])

    return pl.pallas_call(
        body, name=f"ffn_fwd_l{l}", grid=(s // tm, N_CHIPS),
        in_specs=[BS((tm, d), lambda i, j: (i, 0)), BS((None, 1, d), lambda i, j: (l, 0, 0)),
                  BS((None, hc, d), lambda i, j: (j, 0, 0)), BS((None, hc, d), lambda i, j: (j, 0, 0)),
                  BS((None, hc, d), lambda i, j: (j, 0, 0))],
        out_specs=BS((tm, d), lambda i, j: (i, 0)), out_shape=SDS((s, d), F32),
        scratch_shapes=[pltpu.VMEM((tm, d), MXU_DTYPE)], compiler_params=_params())(x1, norm2, wg, wu, wd)


def _ffn_bwd(l, x1, dx2, norm2, wg, wu, wd):
    s, d = x1.shape
    hc = wd.shape[1]
    tm = min(512, s)
    m = MXU_DTYPE
    last = N_CHIPS - 1

    def body(x_ref, dx2_ref, g_ref, wg_ref, wu_ref, wd_ref, dx1_ref, h_ref, dxb_ref, act_ref, dgate_ref, dup_ref,
             dn_ref, dh_scr):
        i, j = pl.program_id(0), pl.program_id(1)

        @pl.when(j == 0)
        def _():
            xv = x_ref[...]
            r = lax.rsqrt(jnp.mean(xv * xv, axis=-1, keepdims=True) + EPS)
            h_ref[...] = (xv * r * g_ref[...]).astype(m)
            dxb_ref[...] = dx2_ref[...].astype(m)
            dh_scr[...] = jnp.zeros(dh_scr.shape, F32)

        @pl.when((i == 0) & (j == 0))
        def _():
            dn_ref[...] = jnp.zeros(dn_ref.shape, F32)

        h = h_ref[...]
        gate = _mm_nt(h, wg_ref[...])
        up = _mm_nt(h, wu_ref[...])
        sg = _sigmoid(gate)
        silu = gate * sg
        act_ref[...] = (silu * up).astype(m).T
        dact = _mm_nt(dxb_ref[...], wd_ref[...])
        dup = (dact * silu).astype(m)
        dgate = (dact * up * (sg * (1.0 + gate * (1.0 - sg)))).astype(m)
        dup_ref[...] = dup.T
        dgate_ref[...] = dgate.T
        dh_scr[...] += _mm(dgate, wg_ref[...]) + _mm(dup, wu_ref[...])

        @pl.when(j == last)
        def _():
            xv = x_ref[...]
            r = lax.rsqrt(jnp.mean(xv * xv, axis=-1, keepdims=True) + EPS)
            xh = xv * r
            dh = dh_scr[...]
            dn_ref[...] += _colsum(dh * xh)
            dxh = dh * g_ref[...]
            dx1_ref[...] = dx2_ref[...] + r * (dxh - xh * jnp.mean(dxh * xh, axis=-1, keepdims=True))

    chunk = BS((None, hc, tm), lambda i, j: (j, 0, i))
    outs = pl.pallas_call(
        body, name=f"ffn_bwd_l{l}", grid=(s // tm, N_CHIPS),
        in_specs=[BS((tm, d), lambda i, j: (i, 0)), BS((tm, d), lambda i, j: (i, 0)),
                  BS((None, 1, d), lambda i, j: (l, 0, 0)),
                  BS((None, hc, d), lambda i, j: (j, 0, 0)), BS((None, hc, d), lambda i, j: (j, 0, 0)),
                  BS((None, hc, d), lambda i, j: (j, 0, 0))],
        out_specs=[BS((tm, d), lambda i, j: (i, 0)), BS((tm, d), lambda i, j: (i, 0)), BS((tm, d), lambda i, j: (i, 0)),
                   chunk, chunk, chunk, BS((1, d), lambda i, j: (0, 0))],
        out_shape=[SDS((s, d), F32), SDS((s, d), m), SDS((s, d), m), SDS((N_CHIPS, hc, s), m),
                   SDS((N_CHIPS, hc, s), m), SDS((N_CHIPS, hc, s), m), SDS((1, d), F32)],
        scratch_shapes=[pltpu.VMEM((tm, d), F32)], compiler_params=_params(),
    )(x1, dx2, norm2, wg, wu, wd)
    return dict(zip(("dx1", "h2", "dx2", "act", "dgate", "dup", "dnorm2"), outs))


def _loss_head(x, target, gf):
    s, d = x.shape
    tm = min(512, s)

    def body(x_ref, t_ref, g_ref, dx_ref, loss_ref, dg_ref):
        @pl.when(pl.program_id(0) == 0)
        def _():
            loss_ref[...] = jnp.zeros(loss_ref.shape, F32)
            dg_ref[...] = jnp.zeros(dg_ref.shape, F32)

        xv = x_ref[...]
        r = lax.rsqrt(jnp.mean(xv * xv, axis=-1, keepdims=True) + EPS)
        xh = xv * r
        err = xh * g_ref[...] - t_ref[...]
        loss_ref[...] += 0.5 * jnp.sum(jnp.mean(err * err, axis=-1, keepdims=True), axis=0, keepdims=True)
        dyv = err * (1.0 / d)
        dg_ref[...] += _colsum(dyv * xh)
        dxh = dyv * g_ref[...]
        dx_ref[...] = r * (dxh - xh * jnp.mean(dxh * xh, axis=-1, keepdims=True))

    return pl.pallas_call(
        body, name="loss_head", grid=(s // tm,),
        in_specs=[BS((tm, d), lambda i: (i, 0)), BS((tm, d), lambda i: (i, 0)), BS((1, d), lambda i: (0, 0))],
        out_specs=[BS((tm, d), lambda i: (i, 0)), BS((1, 1), lambda i: (0, 0)), BS((1, d), lambda i: (0, 0))],
        out_shape=[SDS((s, d), F32), SDS((1, 1), F32), SDS((1, d), F32)], compiler_params=_params())(x, target, gf)


def _in_proj_bwd(l, dres, x, norm1, w_in, du_a, dv1, dv2, du_c, dzg):
    s, d = x.shape
    nc = w_in.shape[-1]
    tm = min(512, s)
    m = MXU_DTYPE

    def body(dres_ref, x_ref, g_ref, w_ref, a_ref, b1_ref, b2_ref, c_ref, g3_ref, dx_ref, dz_ref, dn_ref):
        @pl.when(pl.program_id(0) == 0)
        def _():
            dn_ref[...] = jnp.zeros(dn_ref.shape, F32)

        dz = jnp.concatenate([a_ref[...], b1_ref[...], b2_ref[...], c_ref[...], g3_ref[...]], axis=1).astype(m)
        dz_ref[...] = dz
        dh = _mm_nt_cols(dz, w_ref)
        xv = x_ref[...]
        r = lax.rsqrt(jnp.mean(xv * xv, axis=-1, keepdims=True) + EPS)
        xh = xv * r
        dn_ref[...] += _colsum(dh * xh)
        dxh = dh * g_ref[...]
        dx_ref[...] = dres_ref[...] + r * (dxh - xh * jnp.mean(dxh * xh, axis=-1, keepdims=True))

    tile = lambda n: BS((tm, n), lambda i: (i, 0))
    return pl.pallas_call(
        body, name=f"in_proj_bwd_l{l}", grid=(s // tm,),
        in_specs=[tile(d), tile(d), BS((None, 1, d), lambda i: (l, 0, 0)),
                  BS((N_CHIPS, d, nc), lambda i: (0, 0, 0), pipeline_mode=pl.Buffered(1)),
                  tile(du_a.shape[1]), tile(dv1.shape[1]), tile(dv2.shape[1]), tile(du_c.shape[1]), tile(dzg.shape[1])],
        out_specs=[tile(d), tile(N_CHIPS * nc), BS((1, d), lambda i: (0, 0))],
        out_shape=[SDS((s, d), F32), SDS((s, N_CHIPS * nc), m), SDS((1, d), F32)], compiler_params=_params(),
    )(dres, x, norm1, w_in, du_a, dv1, dv2, du_c, dzg)


def _tn_matmul(name, a, a_spec, b, b_spec, chunk_shape, grid, place):
    last = grid[1] - 1

    def body(place_ref, a_ref, b_ref, own_ref, wire_ref, *acc):
        part = _mm(a_ref[...], b_ref[...])

        def emit(total):
            wire_ref[...] = total.astype(WIRE_DTYPE)

            @pl.when(pl.program_id(0) == place_ref[0])
            def _():
                own_ref[...] = total

        if last == 0:
            emit(part)
        else:
            @pl.when(pl.program_id(1) == 0)
            def _():
                acc[0][...] = part

            @pl.when(pl.program_id(1) > 0)
            def _():
                acc[0][...] += part

            @pl.when(pl.program_id(1) == last)
            def _():
                emit(acc[0][...])

    zeros = (0,) * len(chunk_shape)
    return pl.pallas_call(
        body, name=name,
        grid_spec=pltpu.PrefetchScalarGridSpec(
            num_scalar_prefetch=1, grid=grid, in_specs=[a_spec, b_spec],
            out_specs=[BS(chunk_shape, lambda j, t, pr: zeros), BS((None,) + chunk_shape, lambda j, t, pr: (j,) + zeros)],
            scratch_shapes=[pltpu.VMEM(chunk_shape, F32)] if last else []),
        out_shape=[SDS(chunk_shape, F32), SDS((N_CHIPS,) + chunk_shape, WIRE_DTYPE)],
        compiler_params=_params())(place, a, b)


def _scan_consts(pw_ref, lanes, reverse):
    sgn = -1.0 if reverse else 1.0
    row = lax.broadcasted_iota(jnp.int32, (8, lanes), 0)
    steps = []
    for i, k in enumerate((1, 2, 4)):
        mask = (row < 8 - k) if reverse else (row >= k)
        steps.append((k, jnp.where(mask, pw_ref[2 * i], 0.0), jnp.where(mask, sgn * pw_ref[2 * i + 1], 0.0)))
    c = 4 if reverse else 3
    return steps, pw_ref[2 * c], sgn * pw_ref[2 * c + 1]


def _scan_block(br, bi, steps, row, reverse):
    for k, ar, ai in steps:
        sh = 8 - k if reverse else k
        sr = pltpu.roll(br, sh, 0)
        si = pltpu.roll(bi, sh, 0)
        br, bi = br + ar * sr - ai * si, bi + ar * si + ai * sr
    return br, bi


def _ssm_fwd(l, z, bblk_re, bblk_im, cblk_re, cblk_im, pw, dskip):
    s = z.shape[0]
    gc = bblk_re.shape[1]
    gl = bblk_re.shape[2]
    nblk = bblk_re.shape[0]

    def body(u_ref, bre, bim, cre, cim, pw_ref, d_ref, hre, him, y_ref):
        u = u_ref[...]
        hre[...] = _mm(u, bre[...])
        him[...] = _mm(u, bim[...])
        row = lax.broadcasted_iota(jnp.int32, (8, gl), 0)
        steps, car, cai = _scan_consts(pw_ref, gl, False)

        def step(i, carry):
            cr, ci = carry
            r0 = pl.multiple_of(i * 8, 8)
            br, bi = _scan_block(hre[pl.ds(r0, 8), :], him[pl.ds(r0, 8), :], steps, row, False)
            hr = br + car * cr - cai * ci
            hi = bi + car * ci + cai * cr
            hre[pl.ds(r0, 8), :] = hr
            him[pl.ds(r0, 8), :] = hi
            return jnp.broadcast_to(hr[7:8, :], (8, gl)), jnp.broadcast_to(hi[7:8, :], (8, gl))

        zero = jnp.zeros((8, gl), F32)
        lax.fori_loop(0, s // 8, step, (zero, zero))
        y_ref[...] = _mm_nt(hre[...], cre[...]) - _mm_nt(him[...], cim[...]) + d_ref[...] * u

    return pl.pallas_call(
        body, name=f"ssm_fwd_l{l}", grid=(nblk,),
        in_specs=[BS((s, gc), lambda k: (0, k)), BS((None, gc, gl), lambda k: (k, 0, 0)),
                  BS((None, gc, gl), lambda k: (k, 0, 0)), BS((None, gc, gl), lambda k: (k, 0, 0)),
                  BS((None, gc, gl), lambda k: (k, 0, 0)), BS((10, 8, gl), lambda k: (0, 0, k)),
                  BS((1, gc), lambda k: (0, k))],
        out_specs=[BS((s, gl), lambda k: (0, k)), BS((s, gl), lambda k: (0, k)), BS((s, gc), lambda k: (0, k))],
        out_shape=[SDS((s, nblk * gl), F32), SDS((s, nblk * gl), F32), SDS((s, nblk * gc), F32)],
        compiler_params=_params())(z, bblk_re, bblk_im, cblk_re, cblk_im, pw, dskip)


def _ssm_bwd(l, dy, z, hre, him, bblk_re, bblk_im, cblk_re, cblk_im, pw, dskip):
    s = z.shape[0]
    nblk, gc, gl = bblk_re.shape

    def body(dy_ref, u_ref, hre_ref, him_ref, bre, bim, cre, cim, pw_ref, d_ref,
             du_ref, dbre_ref, dbim_ref, dcre_ref, dcim_ref, dar_ref, dai_ref, dd_ref, gre, gim):
        dyv = dy_ref[...]
        u = u_ref[...]
        gre[...] = _mm(dyv, cre[...])
        gim[...] = -_mm(dyv, cim[...])
        dcre_ref[...] = _mm_tn(dyv, hre_ref[...])
        dcim_ref[...] = -_mm_tn(dyv, him_ref[...])
        dd_ref[...] = _colsum(dyv * u)
        row = lax.broadcasted_iota(jnp.int32, (8, gl), 0)
        steps, car, cai = _scan_consts(pw_ref, gl, True)
        n8 = s // 8

        def step(ii, carry):
            cr, ci, accr, acci = carry
            i = n8 - 1 - ii
            r0 = pl.multiple_of(i * 8, 8)
            br, bi = _scan_block(gre[pl.ds(r0, 8), :], gim[pl.ds(r0, 8), :], steps, row, True)
            dr = br + car * cr - cai * ci
            di = bi + car * ci + cai * cr
            gre[pl.ds(r0, 8), :] = dr
            gim[pl.ds(r0, 8), :] = di
            rp = pl.multiple_of(jnp.maximum(i - 1, 0) * 8, 8)
            keep = jnp.where(i > 0, 1.0, 0.0)
            pr = jnp.where(row >= 1, pltpu.roll(hre_ref[pl.ds(r0, 8), :], 1, 0),
                           keep * pltpu.roll(hre_ref[pl.ds(rp, 8), :], 1, 0))
            pi = jnp.where(row >= 1, pltpu.roll(him_ref[pl.ds(r0, 8), :], 1, 0),
                           keep * pltpu.roll(him_ref[pl.ds(rp, 8), :], 1, 0))
            accr = accr + dr * pr + di * pi
            acci = acci + di * pr - dr * pi
            return (jnp.broadcast_to(dr[0:1, :], (8, gl)), jnp.broadcast_to(di[0:1, :], (8, gl)), accr, acci)

        zero = jnp.zeros((8, gl), F32)
        _, _, accr, acci = lax.fori_loop(0, n8, step, (zero, zero, zero, zero))
        dar_ref[...] = _colsum(accr)
        dai_ref[...] = _colsum(acci)
        dbr = gre[...]
        dbi = gim[...]
        du_ref[...] = (dyv * d_ref[...] + _mm_nt(dbr, bre[...]) + _mm_nt(dbi, bim[...])).astype(du_ref.dtype)
        dbre_ref[...] = _mm_tn(u, dbr)
        dbim_ref[...] = _mm_tn(u, dbi)

    col = lambda n: BS((s, n), lambda k: (0, k))
    blk = lambda a, b: BS((None, a, b), lambda k: (k, 0, 0))
    outs = pl.pallas_call(
        body, name=f"ssm_bwd_l{l}", grid=(nblk,),
        in_specs=[col(gc), col(gc), col(gl), col(gl), blk(gc, gl), blk(gc, gl), blk(gc, gl), blk(gc, gl),
                  BS((10, 8, gl), lambda k: (0, 0, k)), BS((1, gc), lambda k: (0, k))],
        out_specs=[col(gc), blk(gc, gl), blk(gc, gl), blk(gc, gl), blk(gc, gl), BS((1, gl), lambda k: (0, k)),
                   BS((1, gl), lambda k: (0, k)), BS((1, gc), lambda k: (0, k))],
        out_shape=[SDS((s, nblk * gc), MXU_DTYPE), SDS((nblk, gc, gl), F32), SDS((nblk, gc, gl), F32),
                   SDS((nblk, gc, gl), F32), SDS((nblk, gc, gl), F32), SDS((1, nblk * gl), F32),
                   SDS((1, nblk * gl), F32), SDS((1, nblk * gc), F32)],
        scratch_shapes=[pltpu.VMEM((s, gl), F32), pltpu.VMEM((s, gl), F32)], compiler_params=_params(),
    )(dy, z, hre, him, bblk_re, bblk_im, cblk_re, cblk_im, pw, dskip)
    return dict(zip(("du", "dbblk_re", "dbblk_im", "dcblk_re", "dcblk_im", "dabar_re", "dabar_im", "dd"), outs))


def _conv_fwd(l, z, wdw, bdw):
    s = z.shape[0]
    cw = wdw.shape[1]
    lb = 128
    tr = min(256, s)
    off1 = cw // lb
    off2 = 2 * cw // lb

    def body(v1_ref, v2_ref, w_ref, b_ref, hc_ref, scr):
        scr[0:CONV_PAD, :] = jnp.zeros((CONV_PAD, lb), F32)
        scr[CONV_PAD:, :] = v1_ref[...] * _sigmoid(v2_ref[...])
        for t in range(s // tr):
            acc = jnp.broadcast_to(b_ref[...], (tr, lb))
            for k in range(CONV_KERNEL):
                acc = acc + w_ref[pl.ds(k, 1), :] * scr[pl.ds(t * tr + CONV_PAD - (CONV_KERNEL - 1) + k, tr), :]
            hc_ref[pl.ds(t * tr, tr), :] = acc

    return pl.pallas_call(
        body, name=f"conv_fwd_l{l}", grid=(cw // lb,),
        in_specs=[BS((s, lb), lambda k: (0, off1 + k)), BS((s, lb), lambda k: (0, off2 + k)),
                  BS((CONV_KERNEL, lb), lambda k: (0, k)), BS((1, lb), lambda k: (0, k))],
        out_specs=BS((s, lb), lambda k: (0, k)), out_shape=SDS((s, cw), F32),
        scratch_shapes=[pltpu.VMEM((s + CONV_PAD, lb), F32)], compiler_params=_params())(z, z, wdw, bdw)


def _conv_bwd(l, dhc, z, wdw):
    s = z.shape[0]
    cw = wdw.shape[1]
    lb = 128
    tr = min(256, s)
    off1 = cw // lb
    off2 = 2 * cw // lb
    nb = cw // lb

    def body(d_ref, v1_ref, v2_ref, w_ref, dv1_ref, dv2_ref, dw_ref, db_ref, hpad, dpad):
        v1 = v1_ref[...]
        sg = _sigmoid(v2_ref[...])
        dv = d_ref[...]
        hpad[0:CONV_PAD, :] = jnp.zeros((CONV_PAD, lb), F32)
        hpad[CONV_PAD:, :] = v1 * sg
        dpad[0:s, :] = dv
        dpad[s:, :] = jnp.zeros((CONV_PAD, lb), F32)
        db_ref[...] = _colsum(dv)
        dws = [jnp.zeros((1, lb), F32) for _ in range(CONV_KERNEL)]
        for t in range(s // tr):
            dt = d_ref[pl.ds(t * tr, tr), :]
            acc = jnp.zeros((tr, lb), F32)
            for k in range(CONV_KERNEL):
                acc = acc + w_ref[pl.ds(k, 1), :] * dpad[pl.ds(t * tr + (CONV_KERNEL - 1) - k, tr), :]
                dws[k] = dws[k] + _colsum(dt * hpad[pl.ds(t * tr + CONV_PAD - (CONV_KERNEL - 1) + k, tr), :])
            sgt = _sigmoid(v2_ref[pl.ds(t * tr, tr), :])
            v1t = v1_ref[pl.ds(t * tr, tr), :]
            dv1_ref[pl.ds(t * tr, tr), :] = (acc * sgt).astype(dv1_ref.dtype)
            dv2_ref[pl.ds(t * tr, tr), :] = (acc * v1t * (sgt * (1.0 - sgt))).astype(dv2_ref.dtype)
        for k in range(CONV_KERNEL):
            dw_ref[pl.ds(k, 1), :] = dws[k]

    return pl.pallas_call(
        body, name=f"conv_bwd_l{l}", grid=(nb,),
        in_specs=[BS((s, lb), lambda k: (0, k)), BS((s, lb), lambda k: (0, off1 + k)),
                  BS((s, lb), lambda k: (0, off2 + k)), BS((CONV_KERNEL, lb), lambda k: (0, k))],
        out_specs=[BS((s, lb), lambda k: (0, k)), BS((s, lb), lambda k: (0, k)),
                   BS((CONV_KERNEL, lb), lambda k: (0, k)), BS((1, lb), lambda k: (0, k))],
        out_shape=[SDS((s, cw), MXU_DTYPE), SDS((s, cw), MXU_DTYPE), SDS((CONV_KERNEL, cw), F32), SDS((1, cw), F32)],
        scratch_shapes=[pltpu.VMEM((s + CONV_PAD, lb), F32), pltpu.VMEM((s + CONV_PAD, lb), F32)],
        compiler_params=_params())(dhc, z, z, wdw)


def _pool_window(k):
    return jnp.where(k == 0, float(POOL_WINDOWS[0]),
                     jnp.where(k == 1, float(POOL_WINDOWS[1]),
                               jnp.where(k == 2, float(POOL_WINDOWS[2]), float(POOL_WINDOWS[3]))))


def _pool_fwd(l, z, pw_width):
    s = z.shape[0]
    lb = pw_width // len(POOL_WINDOWS)
    off = 3 * pw_width // lb

    def body(u_ref, p_ref):
        k = pl.program_id(0)
        u = u_ref[...]
        row = lax.broadcasted_iota(jnp.int32, (s, lb), 0)
        sums = [u]
        for sh in (1, 2, 4, 8):
            prev = sums[-1]
            sums.append(prev + jnp.where(row >= sh, pltpu.roll(prev, sh, 0), 0.0))
        sel = jnp.where(k == 0, sums[1], jnp.where(k == 1, sums[2], jnp.where(k == 2, sums[3], sums[4])))
        cnt = jnp.minimum((row + 1).astype(F32), _pool_window(k))
        p_ref[...] = sel / cnt - u

    return pl.pallas_call(
        body, name=f"pool_fwd_l{l}", grid=(len(POOL_WINDOWS),),
        in_specs=[BS((s, lb), lambda k: (0, off + k))], out_specs=BS((s, lb), lambda k: (0, k)),
        out_shape=SDS((s, pw_width), F32), compiler_params=_params())(z)


def _pool_bwd(l, dp):
    s, width = dp.shape
    lb = width // len(POOL_WINDOWS)

    def body(d_ref, du_ref):
        k = pl.program_id(0)
        dv = d_ref[...]
        row = lax.broadcasted_iota(jnp.int32, (s, lb), 0)
        cnt = jnp.minimum((row + 1).astype(F32), _pool_window(k))
        sums = [dv / cnt]
        for sh in (1, 2, 4, 8):
            prev = sums[-1]
            sums.append(prev + jnp.where(row < s - sh, pltpu.roll(prev, s - sh, 0), 0.0))
        sel = jnp.where(k == 0, sums[1], jnp.where(k == 1, sums[2], jnp.where(k == 2, sums[3], sums[4])))
        du_ref[...] = (sel - dv).astype(du_ref.dtype)

    return pl.pallas_call(
        body, name=f"pool_bwd_l{l}", grid=(len(POOL_WINDOWS),),
        in_specs=[BS((s, lb), lambda k: (0, k))], out_specs=BS((s, lb), lambda k: (0, k)),
        out_shape=SDS((s, width), MXU_DTYPE), compiler_params=_params())(dp)


def _zoh(a_re, a_im, log_dt):
    dt = jnp.exp(log_dt)
    mag = jnp.exp(dt * a_re)
    ang = dt * a_im
    abar_re = mag * jnp.cos(ang)
    abar_im = mag * jnp.sin(ang)
    den = a_re * a_re + a_im * a_im
    nr = abar_re - 1.0
    ni = abar_im
    f_re = (nr * a_re + ni * a_im) / den
    f_im = (ni * a_re - nr * a_im) / den
    return abar_re, abar_im, f_re, f_im


def _zoh_fwd(l, a_re, a_im, log_dt):
    def body(ar, ai, ld, o0, o1, o2, o3):
        for ref, val in zip((o0, o1, o2, o3), _zoh(ar[...], ai[...], ld[...])):
            ref[...] = val

    return pl.pallas_call(body, name=f"zoh_fwd_l{l}", out_shape=[SDS(a_re.shape, F32)] * 4)(a_re, a_im, log_dt)


def _zoh_bwd(l, a_re, a_im, log_dt, cts):
    def body(ar, ai, ld, c0, c1, c2, c3, dar, dai, dld):
        _, vjp = jax.vjp(_zoh, ar[...], ai[...], ld[...])
        g = vjp((c0[...], c1[...], c2[...], c3[...]))
        dar[...] = g[0]
        dai[...] = g[1]
        dld[...] = g[2]

    return pl.pallas_call(body, name=f"zoh_bwd_l{l}",
                          out_shape=[SDS(a_re.shape, F32), SDS(a_re.shape, F32), SDS(log_dt.shape, F32)],
                          )(a_re, a_im, log_dt, *cts)


def _bbar_fwd(l, f_re, f_im, b_re, b_im):
    g, p, n = b_re.shape[1:]

    def body(fr, fi, br, bi, o_re, o_im):
        o_re[...] = (fr[...] * br[...] - fi[...] * bi[...]).astype(o_re.dtype)
        o_im[...] = (fr[...] * bi[...] + fi[...] * br[...]).astype(o_im.dtype)

    whole = lambda shp: BS(shp, lambda i: (0,) * len(shp))
    layer = BS((None, g, p, n), lambda i: (l, 0, 0, 0))
    return pl.pallas_call(body, name=f"bbar_fwd_l{l}", grid=(1,),
                          in_specs=[whole((g, 1, n)), whole((g, 1, n)), layer, layer],
                          out_specs=[whole((g, p, n))] * 2,
                          out_shape=[SDS((g, p, n), MXU_DTYPE)] * 2)(f_re, f_im, b_re, b_im)


def _bbar_bwd(l, f_re, f_im, b_re, b_im, d_re, d_im):
    g, p, n = b_re.shape[1:]

    def body(fr, fi, br, bi, dr, di, dfr, dfi, dbr, dbi):
        dfr[...] = jnp.sum(dr[...] * br[...] + di[...] * bi[...], axis=1, keepdims=True)
        dfi[...] = jnp.sum(di[...] * br[...] - dr[...] * bi[...], axis=1, keepdims=True)
        dbr[...] = fr[...] * dr[...] + fi[...] * di[...]
        dbi[...] = fr[...] * di[...] - fi[...] * dr[...]

    whole = lambda shp: BS(shp, lambda i: (0,) * len(shp))
    layer = BS((None, g, p, n), lambda i: (l, 0, 0, 0))
    return pl.pallas_call(body, name=f"bbar_bwd_l{l}", grid=(1,),
                          in_specs=[whole((g, 1, n)), whole((g, 1, n)), layer, layer, whole((g, p, n)),
                                    whole((g, p, n))],
                          out_specs=[whole((g, 1, n)), whole((g, 1, n)), whole((g, p, n)), whole((g, p, n))],
                          out_shape=[SDS((g, 1, n), F32), SDS((g, 1, n), F32), SDS((g, p, n), F32),
                                     SDS((g, p, n), F32)])(f_re, f_im, b_re, b_im, d_re, d_im)


def _powers(l, abar_re, abar_im):
    lanes = abar_re.shape[1]

    def body(ar_ref, ai_ref, o_ref):
        ar, ai = ar_ref[...], ai_ref[...]
        pows = [(ar, ai)]
        for _ in range(7):
            pr, pi = pows[-1]
            pows.append((pr * ar - pi * ai, pr * ai + pi * ar))
        row = lax.broadcasted_iota(jnp.int32, (8, lanes), 0)
        for i, k in enumerate((1, 2, 4)):
            o_ref[2 * i] = jnp.broadcast_to(pows[k - 1][0], (8, lanes))
            o_ref[2 * i + 1] = jnp.broadcast_to(pows[k - 1][1], (8, lanes))
        for slot, order in ((3, range(8)), (4, range(7, -1, -1))):
            vr = jnp.zeros((8, lanes), F32)
            vi = jnp.zeros((8, lanes), F32)
            for r, e in enumerate(order):
                vr = jnp.where(row == r, pows[e][0], vr)
                vi = jnp.where(row == r, pows[e][1], vi)
            o_ref[2 * slot] = vr
            o_ref[2 * slot + 1] = vi

    return pl.pallas_call(body, name=f"powers_l{l}", out_shape=SDS((10, 8, lanes), F32))(abar_re, abar_im)


def _block_diag(v):
    g, a, b = v.shape
    eye = jnp.eye(8, dtype=v.dtype)
    out = jnp.einsum("kgab,gh->kgahb", v.reshape(g // 8, 8, a, b), eye)
    return out.reshape(g // 8, 8 * a, 8 * b)


def _block_diag_extract(blk, a, b):
    n = blk.shape[0]
    v = blk.reshape(n, 8, a, 8, b)
    return jnp.einsum("kgahb,gh->kgab", v, jnp.eye(8, dtype=blk.dtype)).reshape(n * 8, a, b)


def _ssm_prepare(l, prm):
    g, n, p = SSM_GROUPS, SSM_STATE, SSM_GROUP
    a_re, a_im = prm["ssm_a_re"][l], prm["ssm_a_im"][l]
    log_dt = prm["ssm_log_dt"][l].reshape(g, 1)
    abar_re, abar_im, f_re, f_im = _zoh_fwd(l, a_re, a_im, log_dt)
    f_re, f_im = f_re.reshape(g, 1, n), f_im.reshape(g, 1, n)
    bbar_re, bbar_im = _bbar_fwd(l, f_re, f_im, prm["ssm_b_re"], prm["ssm_b_im"])
    pw = _powers(l, abar_re.reshape(1, g * n), abar_im.reshape(1, g * n))
    return dict(a_re=a_re, a_im=a_im, log_dt=log_dt, f_re=f_re, f_im=f_im,
                bblk_re=_block_diag(bbar_re), bblk_im=_block_diag(bbar_im),
                cblk_re=_block_diag(prm["ssm_c_re"][l].astype(MXU_DTYPE)),
                cblk_im=_block_diag(prm["ssm_c_im"][l].astype(MXU_DTYPE)), pw=pw,
                dskip=prm["ssm_d"][l].reshape(1, g * p))


def _ssm_param_grads(l, sd, r, prm):
    g, n, p = SSM_GROUPS, SSM_STATE, SSM_GROUP
    dbbar_re = _block_diag_extract(r["dbblk_re"], p, n)
    dbbar_im = _block_diag_extract(r["dbblk_im"], p, n)
    dfr, dfi, db_re, db_im = _bbar_bwd(l, sd["f_re"], sd["f_im"], prm["ssm_b_re"], prm["ssm_b_im"], dbbar_re, dbbar_im)
    cts = (r["dabar_re"].reshape(g, n), r["dabar_im"].reshape(g, n), dfr.reshape(g, n), dfi.reshape(g, n))
    da_re, da_im, dlog_dt = _zoh_bwd(l, sd["a_re"], sd["a_im"], sd["log_dt"], cts)
    return dict(ssm_a_re=da_re, ssm_a_im=da_im, ssm_log_dt=dlog_dt.reshape(g), ssm_b_re=db_re, ssm_b_im=db_im,
                ssm_c_re=_block_diag_extract(r["dcblk_re"], p, n), ssm_c_im=_block_diag_extract(r["dcblk_im"], p, n),
                ssm_d=r["dd"].reshape(g, p))


def _ffn_weight_grads(l, fb, dx2, s, place):
    d = dx2.shape[1]
    hcn = fb["act"].shape[1]
    g = {}
    for name, key, rhs in (("ffn_w_gate", "dgate", fb["h2"]), ("ffn_w_up", "dup", fb["h2"]),
                           ("ffn_w_down", "act", fb["dx2"])):
        g[name] = _tn_matmul(f"d{name}_l{l}", fb[key], BS((None, hcn, s), lambda j, t, pr: (j, 0, 0)), rhs,
                             BS((s, d), lambda j, t, pr: (0, 0)), (hcn, d), (N_CHIPS, 1), place)
    return g


def _in_weight_grad(l, ht, dz, place):
    d, s = ht.shape
    ncw = dz.shape[1] // N_CHIPS
    return _tn_matmul(f"dw_in_l{l}", ht, BS((d, s), lambda j, t, pr: (0, 0)), dz, BS((s, ncw), lambda j, t, pr: (0, j)),
                      (d, ncw), (N_CHIPS, 1), place)


def _fused_tn(name, pairs, kinds, s, place):
    n = len(pairs)

    def shape_of(a, b, kind):
        k, m = a.shape[1], b.shape[1]
        if kind == "rows":
            return (N_CHIPS, k // N_CHIPS, m)
        if kind == "cols":
            return (N_CHIPS, k, m // N_CHIPS)
        return (k // 128, 128, 128)

    shapes = [shape_of(a, b, kind) for (a, b), kind in zip(pairs, kinds)]
    out_shape = []
    for shp, kind in zip(shapes, kinds):
        out_shape += [SDS(shp, F32)] if kind == "groups" else [SDS(shp[1:], F32), SDS(shp, WIRE_DTYPE)]

    def body(place_ref, *refs):
        ins, outs, accs = refs[:2 * n], refs[2 * n:2 * n + len(out_shape)], refs[2 * n + len(out_shape):]
        o = 0
        for i, kind in enumerate(kinds):
            a, b = ins[2 * i][...], ins[2 * i + 1][...]
            if kind == "groups":
                for k in range(shapes[i][0]):
                    outs[o][k] = _mm_tn(a[:, k * 128:(k + 1) * 128], b[:, k * 128:(k + 1) * 128])
                o += 1
                continue
            acc = accs[i]
            if kind == "rows":
                acc[...] = _mm_tn(a, b).reshape(acc.shape)
            else:
                full = _mm_tn(a, b)
                nc = acc.shape[2]
                for j in range(N_CHIPS):
                    acc[j] = full[:, j * nc:(j + 1) * nc]
            outs[o][...] = acc[place_ref[0]]
            outs[o + 1][...] = acc[...].astype(WIRE_DTYPE)
            o += 2

    whole = lambda shp: BS(shp, lambda t, pr: (0,) * len(shp))
    outs = pl.pallas_call(
        body, name=name,
        grid_spec=pltpu.PrefetchScalarGridSpec(
            num_scalar_prefetch=1, grid=(1,),
            in_specs=[whole(v.shape) for pair in pairs for v in pair],
            out_specs=[whole(o.shape) for o in out_shape],
            scratch_shapes=[pltpu.VMEM(shp, F32) for shp in shapes]),
        out_shape=out_shape, compiler_params=_params(),
    )(place, *[v for pair in pairs for v in pair])
    res, o = [], 0
    for kind in kinds:
        if kind == "groups":
            res.append(outs[o])
            o += 1
        else:
            res.append((outs[o], outs[o + 1]))
            o += 2
    return res


def _mixer_weight_grads(l, sv, mb, dx1, s, place):
    g = {}
    (g["w_out"], g["ssm_w_glu"]) = _fused_tn(f"dw_out_glu_l{l}", [(mb["merged"], dx1), (mb["ge"], mb["dt"])],
                                            ("rows", "rows"), s, place)
    (g["ssm_w_proj"], g["conv_w_proj"], g["pool_w_proj"]) = _fused_tn(
        f"dw_proj_l{l}", [(mb["sa"], mb["dya"]), (mb["ac"], mb["dyb"]), (mb["pp"], mb["dyc"])],
        ("cols", "cols", "cols"), s, place)
    (dwgrp,) = _fused_tn(f"dpool_w_group_l{l}", [(sv["p"], mb["dq"])], ("groups",), s, place)
    return g, dwgrp


def _local_step(x, target, weights_of, prm, place, on_grads=None):
    s, d = x.shape
    cw = prm["ssm_b_glu"].shape[1]
    sp = {k: prm[k].reshape(N_LAYERS, 1, -1) for k in ("norm1", "norm2", "b_gate", "ssm_b_glu", "conv_ln_g", "conv_ln_b",
                                                        "pool_scale", "conv_b_dw")}
    sp["pool_w_group"] = prm["pool_w_group"]
    saved = []
    xin = x
    for l in range(N_LAYERS):
        fw = weights_of(l, "in", (xin,))
        sd = _ssm_prepare(l, prm)
        z, h = _in_proj(l, xin, sp["norm1"], fw["w_in"])
        hre, him, y = _ssm_fwd(l, z, sd["bblk_re"], sd["bblk_im"], sd["cblk_re"], sd["cblk_im"], sd["pw"], sd["dskip"])
        p = _pool_fwd(l, z, cw)
        fw.update(weights_of(l, "mixer", (y, p)))
        wdw = fw["conv_w_dw"]
        hc = _conv_fwd(l, z, wdw, sp["conv_b_dw"][l])
        x1 = _merge_fwd(l, xin, y, hc, p, z, fw, sp)
        fw.update(weights_of(l, "ffn", (x1,)))
        x2 = _ffn_fwd(l, x1, sp["norm2"], fw["ffn_w_gate"], fw["ffn_w_up"], fw["ffn_w_down"])
        saved.append(dict(x=xin, z=z, h=h, hre=hre, him=him, y=y, hc=hc, p=p, x1=x1, sd=sd, wdw=wdw, fw=fw))
        xin = x2
    dx, loss, dfinal = _loss_head(xin, target, prm["final_norm"].reshape(1, d))
    big = [None] * N_LAYERS
    small = [None] * N_LAYERS
    norm2_rows = sp["norm2"]
    started = (lambda l, group, grads: on_grads(l, group, grads)) if on_grads is not None else (lambda *a: 0.0)
    for l in reversed(range(N_LAYERS)):
        sv = saved[l]
        sd, fw = sv["sd"], sv["fw"]
        fb = _ffn_bwd(l, sv["x1"], dx, norm2_rows, fw["ffn_w_gate"], fw["ffn_w_up"], fw["ffn_w_down"])
        big[l] = _ffn_weight_grads(l, fb, dx, s, place)
        spl = dict(sp, ssm_b_glu=sp["ssm_b_glu"] + started(l, "ffn", big[l]))
        mb = _merge_bwd(l, fb["dx1"], sv["y"], sv["hc"], sv["p"], sv["z"], fw, spl)
        mixer, dwgrp = _mixer_weight_grads(l, sv, mb, fb["dx1"], s, place)
        big[l].update(mixer)
        wdw = sv["wdw"] + started(l, "mixer", mixer)
        du_c = _pool_bwd(l, mb["dp"])
        dv1, dv2, dwdw, dbdw = _conv_bwd(l, mb["dhc"], sv["z"], wdw)
        sr = _ssm_bwd(l, mb["dy"], sv["z"], sv["hre"], sv["him"], sd["bblk_re"], sd["bblk_im"], sd["cblk_re"],
                      sd["cblk_im"], sd["pw"], sd["dskip"])
        dx, dz, dnorm1 = _in_proj_bwd(l, fb["dx1"], sv["x"], sp["norm1"], fw["w_in"], sr["du"], dv1, dv2, du_c, mb["dzg"])
        w_in_grad = {"w_in": _in_weight_grad(l, sv["h"], dz, place)}
        big[l].update(w_in_grad)
        sg = _ssm_param_grads(l, sd, sr, prm)
        sg.update(norm1=dnorm1.reshape(d), b_gate=mb["db_gate"].reshape(3 * d), ssm_b_glu=mb["db_glu"].reshape(cw),
                  conv_b_dw=dbdw.reshape(cw), conv_ln_g=mb["dln_g"].reshape(cw), conv_ln_b=mb["dln_b"].reshape(cw),
                  pool_w_group=dwgrp, pool_scale=mb["dscale"].reshape(cw), norm2=fb["dnorm2"].reshape(d),
                  conv_w_dw=dwdw)
        small[l] = sg
        if l == N_LAYERS - 1:
            sg = dict(sg, final_norm=dfinal.reshape(d))
        norm2_rows = sp["norm2"] + (started(l, "in", w_in_grad) + started(l, "small", sg))
    return loss[0, 0], dx, big, small, dfinal.reshape(d)


def _place():
    return lax.axis_index("x"), lax.axis_index("y"), lax.axis_index("c")


def _other_chips(x, y):
    return [(1 - x, y), (x, 1 - y), (1 - x, 1 - y)]


def _remote(src, dst, send_sem, recv_sem, device):
    return pltpu.make_async_remote_copy(src_ref=src, dst_ref=dst, send_sem=send_sem, recv_sem=recv_sem,
                                        device_id=device, device_id_type=MESH)


def _hbm(v):
    return pltpu.with_memory_space_constraint(v, pltpu.HBM)


def _cast_into(name, w, place, dtype, after=()):
    nl, k, n = w.shape
    tr = _row_tile(k, n)
    nt = k // tr

    def body(place_ref, w_ref, *rest):
        o0_ref, o1_ref = rest[len(after):]

        @pl.when(pl.program_id(0) == 0)
        def _():
            o0_ref[...] = w_ref[...].astype(dtype)

        @pl.when(pl.program_id(0) == 1)
        def _():
            o1_ref[...] = w_ref[...].astype(dtype)

    return pl.pallas_call(
        body, name=f"cast_{name}",
        grid_spec=pltpu.PrefetchScalarGridSpec(
            num_scalar_prefetch=1, grid=(nl, nt),
            in_specs=[BS((None, tr, n), lambda l, t, pr: (l, t, 0))] + [ANY] * len(after),
            out_specs=[BS((None, tr, n), lambda l, t, pr: (pr[0], t * (1 - l) + (nt - 1) * l, 0)),
                       BS((None, tr, n), lambda l, t, pr: (pr[0], t * l, 0))]),
        out_shape=[SDS((N_CHIPS, k, n), dtype)] * 2)(place, w, *after)


def _cast_small_into(tag, ws, dtypes, place, after=()):
    n = len(ws)

    def body(place_ref, *refs):
        ins, outs = refs[:n], refs[n + len(after):]
        for i in range(n):
            @pl.when(pl.program_id(0) == 0)
            def _():
                outs[2 * i][...] = ins[i][...].astype(dtypes[i])

            @pl.when(pl.program_id(0) == 1)
            def _():
                outs[2 * i + 1][...] = ins[i][...].astype(dtypes[i])

    slot = lambda w: BS((None,) + w.shape[1:], lambda l, pr: (pr[0], 0, 0))
    outs = pl.pallas_call(
        body, name=f"cast_{tag}",
        grid_spec=pltpu.PrefetchScalarGridSpec(
            num_scalar_prefetch=1, grid=(N_LAYERS,),
            in_specs=[BS((None,) + w.shape[1:], lambda l, pr: (l, 0, 0)) for w in ws] + [ANY] * len(after),
            out_specs=[slot(w) for w in ws for _ in range(N_LAYERS)]),
        out_shape=[SDS((N_CHIPS,) + w.shape[1:], dt) for w, dt in zip(ws, dtypes) for _ in range(N_LAYERS)],
    )(place, *ws, *after)
    return [tuple(outs[N_LAYERS * i:N_LAYERS * (i + 1)]) for i in range(n)]


def _gather_rows(buf, c):
    k = buf.shape[1]
    if k % 2:
        return pl.ds(0, k)
    return pl.ds(pl.multiple_of(c * (k // 2), 8), k // 2)


def _allgather_start(tag, groups):
    ng = len(groups)
    sizes = [len(g) for g in groups]
    first = [sum(sizes[:g]) for g in range(ng)]
    flat = [b for g in groups for b in g]
    nb = len(flat)

    def body(*refs):
        ins = refs[:nb]
        sems = refs[nb:nb + 2 * ng]
        token = refs[-1]
        x, y, c = _place()
        jme = 2 * x + y
        for g in range(ng):
            for a in range(sizes[g]):
                buf = ins[first[g] + a]
                blk = buf.at[jme, _gather_rows(buf, c)]
                for k, (cx, cy) in enumerate(_other_chips(x, y)):
                    _remote(blk, blk, sems[2 * g].at[3 * a + k], sems[2 * g + 1].at[3 * a + k], (cx, cy, c)).start()
        token[...] = jnp.zeros(token.shape, F32)

    sem_shapes = [pltpu.SemaphoreType.DMA((3 * sizes[g // 2],)) for g in range(2 * ng)]
    outs = pl.pallas_call(
        body, name=f"allgather_start_{tag}", in_specs=[HBM] * nb,
        out_specs=[SEM] * (2 * ng) + [HBM] * nb + [pl.BlockSpec(memory_space=pltpu.VMEM)],
        out_shape=sem_shapes + [pltpu.HBM(b.shape, b.dtype) for b in flat] + [SDS((8, 128), F32)],
        input_output_aliases={i: 2 * ng + i for i in range(nb)},
        compiler_params=pltpu.CompilerParams(has_side_effects=SIDE_EFFECT))(*[_hbm(b) for b in flat])
    per_group = [(outs[2 * g], outs[2 * g + 1], outs[2 * ng + first[g]:2 * ng + first[g] + sizes[g]])
                 for g in range(ng)]
    return per_group, outs[-1]


def _allgather_wait(l, send_sems, recv_sems, bufs, after):
    n = len(bufs)

    def body(*refs):
        ins = refs[:n]
        ssem, rsem = refs[n], refs[n + 1]
        x, y, c = _place()
        jme = 2 * x + y
        for a in range(n):
            rows = _gather_rows(ins[a], c)
            for k, (cx, cy) in enumerate(_other_chips(x, y)):
                cp = _remote(ins[a].at[jme, rows], ins[a].at[2 * cx + cy, rows], ssem.at[3 * a + k],
                             rsem.at[3 * a + k], (cx, cy, c))
                cp.wait_send()
                cp.wait_recv()

    return pl.pallas_call(
        body, name=f"allgather_wait_{l}", in_specs=[HBM] * n + [SEM, SEM] + [ANY] * len(after), out_specs=[HBM] * n,
        out_shape=[pltpu.HBM(b.shape, b.dtype) for b in bufs], input_output_aliases={i: i for i in range(n)},
        compiler_params=pltpu.CompilerParams(has_side_effects=SIDE_EFFECT))(*bufs, send_sems, recv_sems, *after)


def _allgather_forward(l, bufs):
    n = len(bufs)
    split = [a for a in range(n) if bufs[a].shape[1] % 2 == 0]

    def body(*refs):
        ins = refs[:n]
        send_sems, recv_sems = refs[2 * n:]
        x, y, c = _place()
        sibling = (x, y, 1 - c)
        copies = []
        for a in split:
            for k, (cx, cy) in enumerate(_other_chips(x, y)):
                blk = ins[a].at[2 * cx + cy, _gather_rows(ins[a], c)]
                cp = _remote(blk, blk, send_sems.at[a, k], recv_sems.at[a, k], sibling)
                cp.start()
                copies.append(cp)
        for a in split:
            for k, (cx, cy) in enumerate(_other_chips(x, y)):
                blk = ins[a].at[2 * cx + cy, _gather_rows(ins[a], 1 - c)]
                _remote(blk, blk, send_sems.at[a, k], recv_sems.at[a, k], sibling).wait_recv()
        for cp in copies:
            cp.wait_send()

    sem = pltpu.SemaphoreType.DMA((n, 3))
    return pl.pallas_call(
        body, name=f"allgather_forward_{l}", in_specs=[ANY] * n, out_specs=[ANY] * n,
        out_shape=[SDS(b.shape, b.dtype) for b in bufs], input_output_aliases={i: i for i in range(n)},
        scratch_shapes=[sem, sem])(*bufs)


def _rs_to_owner(l, parts):
    n = len(parts)
    lands = [lax.empty((3,) + p.shape[1:], p.dtype) for p in parts]

    def body(*refs):
        ins, zones = refs[:n], refs[n:2 * n]
        send_sems, recv_sems = refs[2 * n], refs[2 * n + 1]
        token = refs[-1]
        x, y, c = _place()
        for a in range(n):
            for k, (cx, cy) in enumerate(_other_chips(x, y)):
                _remote(ins[a].at[2 * cx + cy], zones[a].at[k], send_sems.at[3 * a + k], recv_sems.at[3 * a + k],
                        (cx, cy, c)).start()
        token[...] = jnp.zeros(token.shape, F32)

    sem = pltpu.SemaphoreType.DMA((3 * n,))
    outs = pl.pallas_call(
        body, name=f"rs_to_owner_start_{l}", in_specs=[HBM] * (2 * n),
        out_specs=[SEM, SEM] + [HBM] * (2 * n) + [pl.BlockSpec(memory_space=pltpu.VMEM)],
        out_shape=[sem, sem] + [pltpu.HBM(p.shape, p.dtype) for p in parts]
        + [pltpu.HBM(z.shape, z.dtype) for z in lands] + [SDS((8, 128), F32)],
        input_output_aliases={i: 2 + i for i in range(2 * n)},
        compiler_params=pltpu.CompilerParams(has_side_effects=SIDE_EFFECT),
    )(*[_hbm(p) for p in parts], *[_hbm(z) for z in lands])
    return outs[0], outs[1], outs[2:2 + n], outs[2 + n:2 + 2 * n], outs[-1]


def _rs_to_owner_wait(l, send_sems, recv_sems, parts, lands, after):
    n = len(parts)

    def body(*refs):
        ins, zones = refs[:n], refs[n:2 * n]
        ssem, rsem = refs[2 * n], refs[2 * n + 1]
        x, y, c = _place()
        for a in range(n):
            for k, (cx, cy) in enumerate(_other_chips(x, y)):
                cp = _remote(ins[a].at[2 * cx + cy], zones[a].at[k], ssem.at[3 * a + k], rsem.at[3 * a + k],
                             (cx, cy, c))
                cp.wait_send()
                cp.wait_recv()

    outs = pl.pallas_call(
        body, name=f"rs_to_owner_wait_{l}", in_specs=[HBM] * (2 * n) + [SEM, SEM] + [ANY] * len(after),
        out_specs=[HBM] * (2 * n),
        out_shape=[pltpu.HBM(p.shape, p.dtype) for p in parts] + [pltpu.HBM(z.shape, z.dtype) for z in lands],
        input_output_aliases={i: i for i in range(2 * n)},
        compiler_params=pltpu.CompilerParams(has_side_effects=SIDE_EFFECT),
    )(*parts, *lands, send_sems, recv_sems, *after)
    return outs[:n], outs[n:]


def _rs_sibling_exchange(l, both):
    n = len(both)

    def body(*refs):
        ins = refs[:n]
        send_sems, recv_sems = refs[2 * n:]
        x, y, c = _place()
        copies = []
        for a in range(n):
            cp = _remote(ins[a].at[c], ins[a].at[c], send_sems.at[a], recv_sems.at[a], (x, y, 1 - c))
            cp.start()
            copies.append(cp)
        for a, cp in enumerate(copies):
            cp.wait_send()
            _remote(ins[a].at[1 - c], ins[a].at[1 - c], send_sems.at[a], recv_sems.at[a], (x, y, 1 - c)).wait_recv()

    sem = pltpu.SemaphoreType.DMA((n,))
    return pl.pallas_call(
        body, name=f"rs_sibling_exchange_{l}", in_specs=[ANY] * n, out_specs=[ANY] * n,
        out_shape=[SDS(b.shape, b.dtype) for b in both], input_output_aliases={i: i for i in range(n)},
        scratch_shapes=[sem, sem])(*both)


def _add_owner(name, grad, recv, place):
    r, cols = grad.shape
    tr = _row_tile(r, cols, budget=1024 * 1024)
    nt = r // tr

    def body(place_ref, g_ref, r_ref, o_ref):
        acc = ((g_ref[...] + r_ref[0].astype(F32)) + r_ref[1].astype(F32)) + r_ref[2].astype(F32)
        o_ref[...] = acc.astype(o_ref.dtype)

    return pl.pallas_call(
        body, name=name,
        grid_spec=pltpu.PrefetchScalarGridSpec(
            num_scalar_prefetch=1, grid=(nt,),
            in_specs=[BS((tr, cols), lambda t, pr: (t, 0)), BS((3, tr, cols), lambda t, pr: (0, t, 0))],
            out_specs=BS((None, tr, cols), lambda t, pr: (pr[1], t, 0))),
        out_shape=SDS((2, r, cols), WIRE_DTYPE))(place, grad, recv)


def _add_owner_small(tag, grads, recvs, place):
    n = len(grads)

    def body(place_ref, *refs):
        gs, rs, outs = refs[:n], refs[n:2 * n], refs[2 * n:]
        for g_ref, r_ref, o_ref in zip(gs, rs, outs):
            acc = ((g_ref[...] + r_ref[0].astype(F32)) + r_ref[1].astype(F32)) + r_ref[2].astype(F32)
            o_ref[...] = acc.astype(o_ref.dtype)

    return pl.pallas_call(
        body, name=f"rs_add_owner_{tag}",
        grid_spec=pltpu.PrefetchScalarGridSpec(
            num_scalar_prefetch=1, grid=(1,),
            in_specs=[BS(g.shape, lambda t, pr: (0, 0)) for g in grads]
            + [BS(r.shape, lambda t, pr: (0, 0, 0)) for r in recvs],
            out_specs=[BS((None,) + g.shape, lambda t, pr: (pr[1], 0, 0)) for g in grads]),
        out_shape=[SDS((2,) + g.shape, WIRE_DTYPE) for g in grads])(place, *grads, *recvs)


def _reduce_start(tag, grads):
    names = list(grads)
    send_sems, recv_sems, wires, lands, token = _rs_to_owner(tag, [grads[n][1] for n in names])
    return dict(tag=tag, names=names, send_sems=send_sems, recv_sems=recv_sems, wires=wires, lands=lands,
                grads=[grads[n][0] for n in names]), token


def _reduce_finish(pending, place, after):
    tag, names = pending["tag"], pending["names"]
    _, lands = _rs_to_owner_wait(tag, pending["send_sems"], pending["recv_sems"], pending["wires"],
                                 pending["lands"], after)
    if max(g.size for g in pending["grads"]) <= SMALL_GRAD_ELEMS:
        mine = _add_owner_small(tag, pending["grads"], lands, place)
    else:
        mine = [_add_owner(f"rs_add_owner_{n}_{tag}", g, r, place) for n, g, r in zip(names, pending["grads"], lands)]
    return dict(zip(names, _rs_sibling_exchange(tag, mine)))


def _small_peers(x, y, c):
    return [(x, y, 1 - c)] + [(cx, cy, c) for cx, cy in _other_chips(x, y)]


def _allgather_rows_start(tag, bufs):
    n = len(bufs)
    lands = [lax.empty((8,) + b.shape, b.dtype) for b in bufs]

    def body(*refs):
        ins, zones = refs[:n], refs[n:2 * n]
        send_sems, recv_sems = refs[2 * n], refs[2 * n + 1]
        token = refs[-1]
        x, y, c = _place()
        for a in range(n):
            for i, peer in enumerate(_small_peers(x, y, c)):
                _remote(ins[a], zones[a].at[4 * x + 2 * y + c], send_sems.at[4 * a + i], recv_sems.at[4 * a + i],
                        peer).start()
        token[...] = jnp.zeros(token.shape, F32)

    sem = pltpu.SemaphoreType.DMA((4 * n,))
    outs = pl.pallas_call(
        body, name=f"allgather_small_start_{tag}", in_specs=[HBM] * (2 * n),
        out_specs=[SEM, SEM] + [HBM] * (2 * n) + [pl.BlockSpec(memory_space=pltpu.VMEM)],
        out_shape=[sem, sem] + [pltpu.HBM(b.shape, b.dtype) for b in bufs]
        + [pltpu.HBM(z.shape, z.dtype) for z in lands] + [SDS((8, 128), F32)],
        input_output_aliases={i: 2 + i for i in range(2 * n)},
        compiler_params=pltpu.CompilerParams(has_side_effects=SIDE_EFFECT),
    )(*[_hbm(b) for b in bufs], *[_hbm(z) for z in lands])
    return outs[0], outs[1], outs[2:2 + n], outs[2 + n:2 + 2 * n], outs[-1]


def _allgather_rows_wait(tag, send_sems, recv_sems, bufs, lands, after):
    n = len(bufs)

    def body(*refs):
        ins, zones = refs[:n], refs[n:2 * n]
        ssem, rsem = refs[2 * n], refs[2 * n + 1]
        x, y, c = _place()
        for a in range(n):
            for i, (px, py, pc) in enumerate(_small_peers(x, y, c)):
                cp = _remote(ins[a], zones[a].at[4 * px + 2 * py + pc], ssem.at[4 * a + i], rsem.at[4 * a + i],
                             (px, py, pc))
                cp.wait_send()
                cp.wait_recv()

    outs = pl.pallas_call(
        body, name=f"allgather_small_wait_{tag}", in_specs=[HBM] * (2 * n) + [SEM, SEM, ANY],
        out_specs=[HBM] * (2 * n),
        out_shape=[pltpu.HBM(b.shape, b.dtype) for b in bufs] + [pltpu.HBM(z.shape, z.dtype) for z in lands],
        input_output_aliases={i: i for i in range(2 * n)},
        compiler_params=pltpu.CompilerParams(has_side_effects=SIDE_EFFECT),
    )(*bufs, *lands, send_sems, recv_sems, after)
    return outs[:n], outs[n:]


def _allgather_rows_forward(tag, lands):
    n = len(lands)

    def body(*refs):
        ins = refs[:n]
        send_sems, recv_sems = refs[2 * n:]
        x, y, c = _place()
        sibling = (x, y, 1 - c)
        copies = []
        for a in range(n):
            for k, (cx, cy) in enumerate(_other_chips(x, y)):
                blk = ins[a].at[4 * cx + 2 * cy + c]
                cp = _remote(blk, blk, send_sems.at[a, k], recv_sems.at[a, k], sibling)
                cp.start()
                copies.append(cp)
        for a in range(n):
            for k, (cx, cy) in enumerate(_other_chips(x, y)):
                blk = ins[a].at[4 * cx + 2 * cy + 1 - c]
                _remote(blk, blk, send_sems.at[a, k], recv_sems.at[a, k], sibling).wait_recv()
        for cp in copies:
            cp.wait_send()

    sem = pltpu.SemaphoreType.DMA((n, 3))
    return pl.pallas_call(body, name=f"allgather_small_forward_{tag}", in_specs=[ANY] * n, out_specs=[ANY] * n,
                          out_shape=[SDS(z.shape, z.dtype) for z in lands],
                          input_output_aliases={i: i for i in range(n)}, scratch_shapes=[sem, sem])(*lands)


def _sum_devices(tag, gathered, mine, place):
    _, r, cols = gathered.shape
    tr = _row_tile(r, cols, budget=256 * 1024)

    def body(place_ref, g_ref, x_ref, o_ref):
        me = 2 * place_ref[0] + place_ref[1]
        acc = jnp.where(me == 0, x_ref[...], g_ref[0])
        for k in range(1, 8):
            acc = acc + jnp.where(me == k, x_ref[...], g_ref[k])
        o_ref[...] = acc

    return pl.pallas_call(
        body, name=f"sum_small_grads_{tag}",
        grid_spec=pltpu.PrefetchScalarGridSpec(
            num_scalar_prefetch=1, grid=(r // tr,),
            in_specs=[BS((8, tr, cols), lambda t, pr: (0, t, 0)), BS((tr, cols), lambda t, pr: (t, 0))],
            out_specs=BS((tr, cols), lambda t, pr: (t, 0))),
        out_shape=SDS((r, cols), F32))(place, gathered, mine)


def _adamw_values(w, g, m, v):
    m = ADAM_B1 * m + (1.0 - ADAM_B1) * g
    v = ADAM_B2 * v + (1.0 - ADAM_B2) * (g * g)
    m_hat = m / (1.0 - ADAM_B1 ** ADAM_STEP)
    v_hat = v / (1.0 - ADAM_B2 ** ADAM_STEP)
    delta = -ADAM_LR * (m_hat / (jnp.sqrt(v_hat) + ADAM_EPS) + ADAM_WD * w)
    return delta, m, v


def _adamw_big(name, l, w, m, v, g, earlier=None, after=()):
    nl, r, cols = w.shape
    tr = _row_tile(r, cols, budget=1024 * 1024)
    nt = r // tr
    n_prev = 0 if earlier is None else 4

    def body(*refs):
        w_ref, m_ref, v_ref, g_ref = refs[:4]
        go_ref, d_ref, mo_ref, vo_ref = refs[4 + n_prev + len(after):]
        gv = g_ref[0].astype(F32) + g_ref[1].astype(F32)
        delta, m_new, v_new = _adamw_values(w_ref[...], gv, m_ref[...], v_ref[...])
        go_ref[...] = gv
        d_ref[...] = delta
        mo_ref[...] = m_new
        vo_ref[...] = v_new

    layer = BS((None, tr, cols), lambda t: (l, t, 0))
    return pl.pallas_call(
        body, name=f"adamw_{name}_l{l}", grid=(nt,),
        in_specs=[layer, layer, layer, BS((2, tr, cols), lambda t: (0, t, 0))] + [ANY] * (n_prev + len(after)),
        out_specs=[layer] * 4, out_shape=[SDS(w.shape, F32)] * 4,
        input_output_aliases={4 + i: i for i in range(n_prev)}, compiler_params=_params(),
    )(w, m, v, g, *(earlier or ()), *after)


def _adamw_small_group(tag, l, ws, ms, vs, gs, earlier, after=()):
    n = len(ws)
    steps = ADAMW_GROUP_STEPS
    prev = [a for e in earlier if e is not None for a in e]
    n_prev = len(prev)
    assert n_prev in (0, 4 * n)

    def body(*refs):
        w_refs, m_refs, v_refs, g_refs = refs[:n], refs[n:2 * n], refs[2 * n:3 * n], refs[3 * n:4 * n]
        outs = refs[4 * n + n_prev + len(after):]
        for i in range(n):
            gv = g_refs[i][0].astype(F32) + g_refs[i][1].astype(F32)
            delta, m_new, v_new = _adamw_values(w_refs[i][...], gv, m_refs[i][...], v_refs[i][...])
            for ref, val in zip(outs[4 * i:4 * i + 4], (gv, delta, m_new, v_new)):
                ref[...] = val

    def layer(w):
        return BS((None, w.shape[1] // steps, w.shape[2]), lambda t: (l, t, 0))

    return pl.pallas_call(
        body, name=f"adamw_{tag}_l{l}", grid=(steps,),
        in_specs=[layer(w) for w in ws] * 3
        + [BS((2, w.shape[1] // steps, w.shape[2]), lambda t: (0, t, 0)) for w in ws] + [ANY] * (n_prev + len(after)),
        out_specs=[layer(w) for w in ws for _ in range(4)],
        out_shape=[SDS(w.shape, F32) for w in ws for _ in range(4)],
        input_output_aliases={4 * n + i: i for i in range(n_prev)}, compiler_params=_params(),
    )(*ws, *ms, *vs, *gs, *prev, *after)


def _adamw_mid(name, w, m, v, gathered, mine, place):
    shape = w.shape[1:]
    zeros = (0,) * len(shape)

    def body(place_ref, w_ref, m_ref, v_ref, *refs):
        gath, own = refs[:N_LAYERS], refs[N_LAYERS:2 * N_LAYERS]
        go_ref, d_ref, mo_ref, vo_ref = refs[2 * N_LAYERS:]
        me = 2 * place_ref[0] + place_ref[1]
        sums = []
        for l in range(N_LAYERS):
            acc = jnp.where(me == 0, own[l][...], gath[l][0])
            for k in range(1, 8):
                acc = acc + jnp.where(me == k, own[l][...], gath[l][k])
            sums.append(acc)
        gv = sums[0]
        for l in range(1, N_LAYERS):
            gv = jnp.where(pl.program_id(0) == l, sums[l], gv)
        delta, m_new, v_new = _adamw_values(w_ref[...], gv, m_ref[...], v_ref[...])
        go_ref[...] = gv
        d_ref[...] = delta
        mo_ref[...] = m_new
        vo_ref[...] = v_new

    layer = BS((None,) + shape, lambda l, pr: (l,) + zeros)
    return pl.pallas_call(
        body, name=f"adamw_{name}",
        grid_spec=pltpu.PrefetchScalarGridSpec(
            num_scalar_prefetch=1, grid=(N_LAYERS,),
            in_specs=[layer] * 3 + [BS((8,) + shape, lambda l, pr: (0,) + zeros)] * N_LAYERS
            + [BS(shape, lambda l, pr: zeros)] * N_LAYERS,
            out_specs=[layer] * 4),
        out_shape=[SDS(w.shape, F32)] * 4, compiler_params=_params())(place, w, m, v, *gathered, *mine)


def _adamw_rows(w, m, v, g):
    r, cols = w.shape
    tr = _row_tile(r, cols, budget=512 * 1024)

    def body(w_ref, m_ref, v_ref, g_ref, d_ref, mo_ref, vo_ref):
        delta, m_new, v_new = _adamw_values(w_ref[...], g_ref[...], m_ref[...], v_ref[...])
        d_ref[...] = delta
        mo_ref[...] = m_new
        vo_ref[...] = v_new

    spec = BS((tr, cols), lambda t: (t, 0))
    return pl.pallas_call(body, name="adamw_small", grid=(r // tr,), in_specs=[spec] * 4, out_specs=[spec] * 3,
                          out_shape=[SDS(w.shape, F32)] * 3)(w, m, v, g)


SMALL_GRAD_ELEMS = 256 * 1024
ADAMW_GROUP_STEPS = 4
PACK_ALIGN = 8 * 128
PACK_ROWS = 128


def _pack_rows(arrays):
    parts, rows = [], 0
    for a in arrays:
        flat = a.reshape(-1)
        pad = (-flat.shape[0]) % PACK_ALIGN
        if pad:
            flat = jnp.pad(flat, (0, pad))
        parts.append(flat.reshape(-1, 128))
        rows += parts[-1].shape[0]
    if rows % PACK_ROWS:
        parts.append(jnp.zeros((PACK_ROWS - rows % PACK_ROWS, 128), parts[0].dtype))
    return jnp.concatenate(parts, axis=0)


def _unpack_rows(buf, shapes):
    out, row = [], 0
    for shape in shapes:
        size = math.prod(shape)
        rows = -(-size // PACK_ALIGN) * (PACK_ALIGN // 128)
        out.append(buf[row:row + rows].reshape(-1)[:size].reshape(shape))
        row += rows
    return out


def kernel(x, norm1, w_in, b_gate, ssm_a_re, ssm_a_im, ssm_log_dt, ssm_b_re, ssm_b_im, ssm_c_re, ssm_c_im, ssm_d, ssm_w_glu, ssm_b_glu, ssm_w_proj, conv_w_dw, conv_b_dw, conv_ln_g, conv_ln_b, conv_w_proj, pool_w_group, pool_scale, pool_w_proj, w_out, norm2, ffn_w_gate, ffn_w_up, ffn_w_down, final_norm, loss_target, m_norm1, m_w_in, m_b_gate, m_ssm_a_re, m_ssm_a_im, m_ssm_log_dt, m_ssm_b_re, m_ssm_b_im, m_ssm_c_re, m_ssm_c_im, m_ssm_d, m_ssm_w_glu, m_ssm_b_glu, m_ssm_w_proj, m_conv_w_dw, m_conv_b_dw, m_conv_ln_g, m_conv_ln_b, m_conv_w_proj, m_pool_w_group, m_pool_scale, m_pool_w_proj, m_w_out, m_norm2, m_ffn_w_gate, m_ffn_w_up, m_ffn_w_down, m_final_norm, v_norm1, v_w_in, v_b_gate, v_ssm_a_re, v_ssm_a_im, v_ssm_log_dt, v_ssm_b_re, v_ssm_b_im, v_ssm_c_re, v_ssm_c_im, v_ssm_d, v_ssm_w_glu, v_ssm_b_glu, v_ssm_w_proj, v_conv_w_dw, v_conv_b_dw, v_conv_ln_g, v_conv_ln_b, v_conv_w_proj, v_pool_w_group, v_pool_scale, v_pool_w_proj, v_w_out, v_norm2, v_ffn_w_gate, v_ffn_w_up, v_ffn_w_down, v_final_norm):
    given = dict(locals())
    cx, cy, cc = _place()
    place = jnp.stack([2 * cx + cy, cc]).astype(jnp.int32)

    def kernel_view(n, a):
        if n in TRANSPOSED:
            return a.transpose(0, 2, 1)
        return a.transpose(0, 1, 3, 2) if n in ("ssm_b_re", "ssm_b_im") else a

    prm = {n: given[n] for n in WEIGHTS}
    mom = {n: given["m_" + n] for n in WEIGHTS}
    var = {n: given["v_" + n] for n in WEIGHTS}
    for n in MID:
        prm[n], mom[n], var[n] = kernel_view(n, prm[n]), kernel_view(n, mom[n]), kernel_view(n, var[n])

    dw_shard = prm["conv_w_dw"].reshape(N_LAYERS, CONV_KERNEL, -1)
    casts = {"w_in": _cast_into("w_in", prm["w_in"], place, MXU_DTYPE)}
    first, first_started = _allgather_start("first", [[casts["w_in"][0]]])
    in_flight = {(0, "in"): first[0]}
    mixer = GATHER_GROUPS["mixer"]
    casts.update(zip(mixer, _cast_small_into(
        "mixer", [dw_shard if n == "conv_w_dw" else prm[n] for n in mixer],
        [F32 if n == "conv_w_dw" else MXU_DTYPE for n in mixer], place, after=(first_started,))))
    casts.update({n: _cast_into(n, kernel_view(n, prm[n]), place, MXU_DTYPE, after=(first_started,))
                  for n in GATHER_GROUPS["ffn"]})
    order = [(l, g) for l in range(N_LAYERS) for g in GATHER_GROUPS if (l, g) != (0, "in")]
    rest, rest_started = _allgather_start("rest", [[casts[n][l] for n in GATHER_GROUPS[g]] for l, g in order])
    in_flight.update(zip(order, rest))

    def weights_of(l, group, after):
        send_sems, recv_sems, bufs = in_flight[l, group]
        tag = f"l{l}_{group}"
        if (l, group) == (0, "in"):
            after = after + (rest_started,)
        bufs = _allgather_forward(tag, _allgather_wait(tag, send_sems, recv_sems, bufs, after))
        fw = dict(zip(GATHER_GROUPS[group], bufs))
        if "conv_w_dw" in fw:
            fw["conv_w_dw"] = fw["conv_w_dw"].transpose(1, 0, 2).reshape(CONV_KERNEL, -1)
        return fw

    pending, small_pending, small_shapes = {}, {}, {}
    tokens = {}

    def on_grads(l, group, grads):
        if group == "small":
            packed = {n: g for n, g in grads.items() if n not in MID}
            small_shapes[l] = {n: g.shape for n, g in packed.items()}
            begun = _allgather_rows_start(f"l{l}", [_pack_rows(list(packed.values()))] + [grads[n] for n in MID])
            small_pending[l], token = begun[:4], begun[4]
        else:
            pending[l, group], token = _reduce_start(f"{l}_{group}", grads)
        tokens[l, group] = token
        return token[0, 0]

    loss, dx, _, _, _ = _local_step(x[0], loss_target[0], weights_of, prm, place, on_grads)
    loss = lax.psum(loss, ("x", "y", "c"))

    reduced = [{} for _ in range(N_LAYERS)]
    out = {}

    def finish(l, group, after):
        reduced[l].update(_reduce_finish(pending[l, group], place, after))

    def adamw(l, names, done):
        small = [n for n in names if prm[n][0].size <= SMALL_GRAD_ELEMS]
        for n in names:
            if n not in small:
                out[n] = _adamw_big(n, l, kernel_view(n, prm[n]), kernel_view(n, mom[n]), kernel_view(n, var[n]),
                                    reduced[l][n], out.get(n), after=done)
                done = (out[n][0],)
        if small:
            res = _adamw_small_group("mixer", l, [prm[n] for n in small], [mom[n] for n in small],
                                     [var[n] for n in small], [reduced[l][n] for n in small],
                                     [out.get(n) for n in small], after=done)
            for i, n in enumerate(small):
                out[n] = tuple(res[4 * i:4 * i + 4])
            done = (res[0],)
        return done

    top = N_LAYERS - 1
    done = (tokens[0, "in"], tokens[0, "small"])
    for group in ("ffn", "mixer", "in"):
        finish(top, group, done)
    done = adamw(top, BIG, done)
    for group in ("ffn", "mixer", "in"):
        finish(0, group, done)
        done = adamw(0, [n for n in GATHER_GROUPS[group] if n in BIG], done)
    for n in BIG:
        out[n] = tuple(kernel_view(n, a) for a in out[n])

    gsmall = {}
    mid_mine, mid_gathered = [], []
    for l in range(N_LAYERS):
        mine, lands = _allgather_rows_wait(f"l{l}", *small_pending[l], done[0])
        lands = _allgather_rows_forward(f"l{l}", lands)
        mid_mine.append(mine[1:])
        mid_gathered.append(lands[1:])
        gsum = _sum_devices(f"l{l}", lands[0], mine[0], place)
        for n, g in zip(small_shapes[l], _unpack_rows(gsum, list(small_shapes[l].values()))):
            gsmall.setdefault(n, [None] * N_LAYERS)[l] = g
    for i, n in enumerate(MID):
        out[n] = tuple(kernel_view(n, a) for a in _adamw_mid(
            n, prm[n], mom[n], var[n], [mid_gathered[l][i] for l in range(N_LAYERS)],
            [mid_mine[l][i] for l in range(N_LAYERS)], place))
    gsmall = {n: (g[top] if n == "final_norm" else jnp.stack(g)) for n, g in gsmall.items()}
    lanes = dw_shard.shape[-1]
    gsmall["conv_w_dw"] = lax.dynamic_slice_in_dim(gsmall["conv_w_dw"], (2 * cx + cy) * lanes, lanes, axis=2)
    small_names = [n for n in SMALL if n not in MID] + ["conv_w_dw"]
    w_rows = _pack_rows([prm[n] for n in small_names])
    m_rows = _pack_rows([mom[n] for n in small_names])
    v_rows = _pack_rows([var[n] for n in small_names])
    g_rows = _pack_rows([gsmall[n] for n in small_names])
    shapes = [prm[n].shape for n in small_names]
    d_s, m_s, v_s = (_unpack_rows(r, shapes) for r in _adamw_rows(w_rows, m_rows, v_rows, g_rows))
    for i, n in enumerate(small_names):
        out[n] = (gsmall[n].reshape(prm[n].shape), d_s[i], m_s[i], v_s[i])
    grads = [out[n][0] for n in WEIGHTS]
    deltas = [out[n][1] for n in WEIGHTS]
    new_m = [out[n][2] for n in WEIGHTS]
    new_v = [out[n][3] for n in WEIGHTS]
    return (loss, dx[None], *grads, *deltas, *new_m, *new_v)
```

```python
import functools
import math

import jax
import jax.numpy as jnp
from jax import lax
from jax.experimental import pallas as pl
from jax.experimental.pallas import tpu as pltpu

F32 = jnp.float32
MXU_DTYPE = jnp.bfloat16
WIRE_DTYPE = jnp.bfloat16
SDS = jax.ShapeDtypeStruct
BS = pl.BlockSpec
ANY = pl.BlockSpec(memory_space=pl.ANY)
HBM = pl.BlockSpec(memory_space=pltpu.HBM)
SEM = pl.BlockSpec(memory_space=pltpu.SEMAPHORE)
SIDE_EFFECT = pltpu.SideEffectType.DATAFLOW_SIDE_EFFECTING
MESH = pl.DeviceIdType.MESH

EPS = 1e-6
N_CHIPS = 4
N_LAYERS = 2
SSM_GROUPS, SSM_STATE, SSM_GROUP = 32, 64, 16
CONV_KERNEL = 31
CONV_PAD = 32
POOL_WINDOWS = (2, 4, 8, 16)
GELU_C = math.sqrt(2.0 / math.pi)
ADAM_LR, ADAM_B1, ADAM_B2, ADAM_EPS, ADAM_WD, ADAM_STEP = 0.001, 0.9, 0.999, 1e-08, 0.01, 10
VMEM_LIMIT = 56 * 1024 * 1024

BIG = ("w_in", "ssm_w_glu", "ssm_w_proj", "conv_w_proj", "pool_w_proj", "w_out", "ffn_w_gate", "ffn_w_up", "ffn_w_down")
TRANSPOSED = ("ffn_w_gate", "ffn_w_up")
MID = ("ssm_b_re", "ssm_b_im", "ssm_c_re", "ssm_c_im")
GATHER_GROUPS = {
    "in": ("w_in",),
    "mixer": ("ssm_w_glu", "ssm_w_proj", "conv_w_proj", "pool_w_proj", "w_out", "conv_w_dw"),
    "ffn": ("ffn_w_gate", "ffn_w_up", "ffn_w_down"),
}
SMALL = ("norm1", "b_gate", "ssm_a_re", "ssm_a_im", "ssm_log_dt", "ssm_b_re", "ssm_b_im", "ssm_c_re", "ssm_c_im",
         "ssm_d", "ssm_b_glu", "conv_b_dw", "conv_ln_g", "conv_ln_b", "pool_w_group", "pool_scale", "norm2",
         "final_norm")
WEIGHTS = ("norm1", "w_in", "b_gate", "ssm_a_re", "ssm_a_im", "ssm_log_dt", "ssm_b_re", "ssm_b_im", "ssm_c_re",
           "ssm_c_im", "ssm_d", "ssm_w_glu", "ssm_b_glu", "ssm_w_proj", "conv_w_dw", "conv_b_dw", "conv_ln_g",
           "conv_ln_b", "conv_w_proj", "pool_w_group", "pool_scale", "pool_w_proj", "w_out", "norm2", "ffn_w_gate",
           "ffn_w_up", "ffn_w_down", "final_norm")


def _params(vmem=True):
    return pltpu.CompilerParams(vmem_limit_bytes=VMEM_LIMIT) if vmem else None


def _mm(a, b):
    return jnp.dot(a.astype(MXU_DTYPE), b.astype(MXU_DTYPE), preferred_element_type=F32)


def _mm_nt(a, b):
    return lax.dot_general(a.astype(MXU_DTYPE), b.astype(MXU_DTYPE), (((1,), (1,)), ((), ())),
                           preferred_element_type=F32)


def _mm_tn(a, b):
    return lax.dot_general(a.astype(MXU_DTYPE), b.astype(MXU_DTYPE), (((0,), (0,)), ((), ())),
                           preferred_element_type=F32)


def _sigmoid(x):
    return jax.nn.sigmoid(x)


def _gelu(x):
    t = jnp.tanh(GELU_C * (x + 0.044715 * (x * x * x)))
    return x * (0.5 * (1.0 + t)), t


def _gelu_grad(x, t):
    return 0.5 * (1.0 + t) + 0.5 * x * (1.0 - t * t) * (GELU_C * (1.0 + 3.0 * 0.044715 * x * x))


def _colsum(v):
    return jnp.sum(v, axis=0, keepdims=True)


def _row_tile(rows, cols, itemsize=4, budget=1536 * 1024):
    best = None
    for t in range(8, rows + 1, 8):
        if rows % t == 0 and t * cols * itemsize <= budget:
            best = t
    return best if best is not None else rows


def _in_proj(l, x, norm1, w_in):
    s, d = x.shape
    nc = w_in.shape[-1]
    tm = min(1024, s)
    nt = s // tm

    def body(x_ref, g_ref, w_ref, z_ref, h_ref, h_all):
        i = pl.program_id(1)
        rows = pl.ds(pl.multiple_of(i * tm, tm), tm)

        @pl.when(pl.program_id(0) == 0)
        def _():
            xv = x_ref[...]
            r = lax.rsqrt(jnp.mean(xv * xv, axis=-1, keepdims=True) + EPS)
            hv = (xv * r * g_ref[...]).astype(h_ref.dtype)
            h_ref[...] = hv.T
            h_all[rows, :] = hv

        z_ref[...] = _mm(h_all[rows, :], w_ref[...])

    tile_of = lambda j, i: i * (1 - jnp.minimum(j, 1)) + (nt - 1) * jnp.minimum(j, 1)
    return pl.pallas_call(
        body, name=f"in_proj_l{l}", grid=(N_CHIPS, nt),
        in_specs=[BS((tm, d), lambda j, i: (tile_of(j, i), 0)), BS((None, 1, d), lambda j, i: (l, 0, 0)),
                  BS((None, d, nc), lambda j, i: (j, 0, 0))],
        out_specs=[BS((tm, nc), lambda j, i: (i, j)), BS((d, tm), lambda j, i: (0, tile_of(j, i)))],
        out_shape=[SDS((s, N_CHIPS * nc), F32), SDS((d, s), MXU_DTYPE)],
        scratch_shapes=[pltpu.VMEM((s, d), MXU_DTYPE)], compiler_params=_params())(x, norm1, w_in)


def _mm_cols(a, w_ref):
    return jnp.concatenate([_mm(a, w_ref[j]) for j in range(N_CHIPS)], axis=1)


def _mm_nt_cols(dv, w_ref):
    nc = w_ref.shape[-1]
    acc = _mm_nt(dv[:, 0:nc], w_ref[0])
    for j in range(1, N_CHIPS):
        acc = acc + _mm_nt(dv[:, j * nc:(j + 1) * nc], w_ref[j])
    return acc


def _merge_values(y, hc, p, zg, wglu, bglu, wpa, wpb, wpc, lng, lnb, wgrp, scale, bg):
    v = {}
    ge, th = _gelu(y)
    t = _mm(ge, wglu) + bglu
    sg = _sigmoid(t)
    sa = ge * sg
    ya = _mm_cols(sa, wpa)
    mu = jnp.mean(hc, axis=-1, keepdims=True)
    xc = hc - mu
    r = lax.rsqrt(jnp.mean(xc * xc, axis=-1, keepdims=True) + EPS)
    xh = xc * r
    ln = xh * lng + lnb
    sl = _sigmoid(ln)
    ac = ln * sl
    yb = _mm_cols(ac, wpb)
    gw = p.shape[1] // len(POOL_WINDOWS)
    q = jnp.concatenate([_mm(p[:, k * gw:(k + 1) * gw], wgrp[k]) for k in range(len(POOL_WINDOWS))], axis=1)
    pp = q * scale
    yc = _mm_cols(pp, wpc)
    d = ya.shape[1]
    gates = [_sigmoid(zg[k] + bg[:, k * d:(k + 1) * d]) for k in range(3)]
    merged = gates[0] * ya + gates[1] * yb + gates[2] * yc
    v.update(ge=ge, th=th, sg=sg, sa=sa, ya=ya, r=r, xh=xh, ln=ln, sl=sl, ac=ac, yb=yb, q=q, pp=pp, yc=yc,
             gates=gates, merged=merged)
    return v


def _merge_specs(l, tm, d, cw):
    row = lambda n: BS((None, 1, n), lambda i: (l, 0, 0))
    resident = lambda shp: BS(shp, lambda i: (0, 0, 0), pipeline_mode=pl.Buffered(1))
    return [
        BS((tm, cw), lambda i: (i, 0)),
        BS((tm, cw), lambda i: (i, 0)),
        BS((tm, cw), lambda i: (i, 0)),
        BS((tm, d), lambda i: (i, 2)), BS((tm, d), lambda i: (i, 3)), BS((tm, d), lambda i: (i, 4)),
        resident((N_CHIPS, cw // N_CHIPS, cw)),
        row(cw),
        resident((N_CHIPS, cw, d // N_CHIPS)),
        resident((N_CHIPS, cw, d // N_CHIPS)),
        resident((N_CHIPS, cw, d // N_CHIPS)),
        row(cw), row(cw),
        BS((None, 4, cw // 4, cw // 4), lambda i: (l, 0, 0, 0)),
        row(cw),
        row(3 * d),
        resident((N_CHIPS, d // N_CHIPS, d)),
    ]


def _merge_fwd(l, x, y, hc, p, z, fw, sp):
    s, d = x.shape
    cw = y.shape[1]
    tm = min(512, s)

    def body(x_ref, y_ref, hc_ref, p_ref, z0, z1, z2, wglu, bglu, wpa, wpb, wpc, lng, lnb, wgrp, scale, bg, wout,
             x1_ref):
        v = _merge_values(y_ref[...], hc_ref[...], p_ref[...], (z0[...], z1[...], z2[...]),
                          wglu[...].reshape(cw, cw), bglu[...], wpa, wpb, wpc, lng[...], lnb[...], wgrp, scale[...],
                          bg[...])
        x1_ref[...] = x_ref[...] + _mm(v["merged"], wout[...].reshape(d, d))

    return pl.pallas_call(
        body, name=f"merge_fwd_l{l}", grid=(s // tm,),
        in_specs=[BS((tm, d), lambda i: (i, 0))] + _merge_specs(l, tm, d, cw),
        out_specs=BS((tm, d), lambda i: (i, 0)), out_shape=SDS((s, d), F32), compiler_params=_params(),
    )(x, y, hc, p, z, z, z, fw["ssm_w_glu"], sp["ssm_b_glu"], fw["ssm_w_proj"], fw["conv_w_proj"], fw["pool_w_proj"],
      sp["conv_ln_g"], sp["conv_ln_b"], sp["pool_w_group"], sp["pool_scale"], sp["b_gate"], fw["w_out"])


def _merge_bwd(l, dx1, y, hc, p, z, fw, sp):
    s, d = dx1.shape
    cw = y.shape[1]
    tm = min(256, s)
    m = MXU_DTYPE

    def body(dx1_ref, y_ref, hc_ref, p_ref, z0, z1, z2, wglu, bglu, wpa, wpb, wpc, lng, lnb, wgrp, scale, bg, wout,
             dzg_ref, dy_ref, dhc_ref, dp_ref, merged_ref, sa_ref, ac_ref, pp_ref, ge_ref, dt_ref, dya_ref, dyb_ref,
             dyc_ref, dq_ref, dbg_ref, dbglu_ref, dlng_ref, dlnb_ref, dscale_ref):
        yv = y_ref[...]
        wg = wglu[...].reshape(cw, cw)
        v = _merge_values(yv, hc_ref[...], p_ref[...], (z0[...], z1[...], z2[...]), wg, bglu[...], wpa, wpb, wpc,
                          lng[...], lnb[...], wgrp, scale[...], bg[...])
        dm = _mm_nt(dx1_ref[...], wout[...].reshape(d, d))
        ys = (v["ya"], v["yb"], v["yc"])
        dys, dbg = [], []
        for k in range(3):
            gk = v["gates"][k]
            dzk = dm * ys[k] * (gk * (1.0 - gk))
            dbg.append(_colsum(dzk))
            dzg_ref[:, k * d:(k + 1) * d] = dzk.astype(m)
            dys.append((dm * gk).astype(m))
        dsa = _mm_nt_cols(dys[0], wpa)
        dac = _mm_nt_cols(dys[1], wpb)
        dpp = _mm_nt_cols(dys[2], wpc)
        ge, sg = v["ge"], v["sg"]
        dt = dsa * ge * (sg * (1.0 - sg))
        dge = dsa * sg + _mm_nt(dt, wg)
        dy_ref[...] = dge * _gelu_grad(yv, v["th"])
        ln, sl, xh = v["ln"], v["sl"], v["xh"]
        dln = dac * (sl * (1.0 + ln * (1.0 - sl)))
        dxh = dln * lng[...]
        dhc_ref[...] = v["r"] * (dxh - jnp.mean(dxh, axis=-1, keepdims=True)
                                 - xh * jnp.mean(dxh * xh, axis=-1, keepdims=True))
        dq = dpp * scale[...]
        gw = cw // len(POOL_WINDOWS)
        for k in range(len(POOL_WINDOWS)):
            dp_ref[:, k * gw:(k + 1) * gw] = _mm_nt(dq[:, k * gw:(k + 1) * gw], wgrp[k])
        merged_ref[...] = v["merged"].astype(m)
        sa_ref[...] = v["sa"].astype(m)
        ac_ref[...] = v["ac"].astype(m)
        pp_ref[...] = v["pp"].astype(m)
        ge_ref[...] = ge.astype(m)
        dt_ref[...] = dt.astype(m)
        dya_ref[...] = dys[0]
        dyb_ref[...] = dys[1]
        dyc_ref[...] = dys[2]
        dq_ref[...] = dq.astype(m)

        @pl.when(pl.program_id(0) == 0)
        def _():
            for ref in (dbg_ref, dbglu_ref, dlng_ref, dlnb_ref, dscale_ref):
                ref[...] = jnp.zeros(ref.shape, F32)

        dbg_ref[...] += jnp.concatenate(dbg, axis=1)
        dbglu_ref[...] += _colsum(dt)
        dlng_ref[...] += _colsum(dln * xh)
        dlnb_ref[...] += _colsum(dln)
        dscale_ref[...] += _colsum(dpp * v["q"])

    tile = lambda n: BS((tm, n), lambda i: (i, 0))
    acc = lambda n: BS((1, n), lambda i: (0, 0))
    outs = pl.pallas_call(
        body, name=f"merge_bwd_l{l}", grid=(s // tm,),
        in_specs=[tile(d)] + _merge_specs(l, tm, d, cw),
        out_specs=[tile(3 * d), tile(cw), tile(cw), tile(cw), tile(d), tile(cw), tile(cw), tile(cw), tile(cw), tile(cw),
                   tile(d), tile(d), tile(d), tile(cw), acc(3 * d), acc(cw), acc(cw), acc(cw), acc(cw)],
        out_shape=[SDS((s, 3 * d), m), SDS((s, cw), F32), SDS((s, cw), F32), SDS((s, cw), F32), SDS((s, d), m),
                   SDS((s, cw), m), SDS((s, cw), m), SDS((s, cw), m), SDS((s, cw), m), SDS((s, cw), m), SDS((s, d), m),
                   SDS((s, d), m), SDS((s, d), m), SDS((s, cw), m), SDS((1, 3 * d), F32), SDS((1, cw), F32),
                   SDS((1, cw), F32), SDS((1, cw), F32), SDS((1, cw), F32)],
        compiler_params=_params(),
    )(dx1, y, hc, p, z, z, z, fw["ssm_w_glu"], sp["ssm_b_glu"], fw["ssm_w_proj"], fw["conv_w_proj"], fw["pool_w_proj"],
      sp["conv_ln_g"], sp["conv_ln_b"], sp["pool_w_group"], sp["pool_scale"], sp["b_gate"], fw["w_out"])
    names = ("dzg", "dy", "dhc", "dp", "merged", "sa", "ac", "pp", "ge", "dt", "dya", "dyb", "dyc", "dq", "db_gate",
             "db_glu", "dln_g", "dln_b", "dscale")
    return dict(zip(names, outs))


def _ffn_fwd(l, x1, norm2, wg, wu, wd):
    s, d = x1.shape
    hc = wd.shape[1]
    tm = min(1024, s)

    def body(x_ref, g_ref, wg_ref, wu_ref, wd_ref, o_ref, h_scr):
        @pl.when(pl.program_id(1) == 0)
        def _():
            xv = x_ref[...]
            r = lax.rsqrt(jnp.mean(xv * xv, axis=-1, keepdims=True) + EPS)
            h_scr[...] = (xv * r * g_ref[...]).astype(h_scr.dtype)
            o_ref[...] = xv

        h = h_scr[...]
        gate = _mm_nt(h, wg_ref[...])
        up = _mm_nt(h, wu_ref[...])
        o_ref[...] += _mm(gate * _sigmoid(gate) * up, wd_ref[...])

    return pl.pallas_call(
        body, name=f"ffn_fwd_l{l}", grid=(s // tm, N_CHIPS),
        in_specs=[BS((tm, d), lambda i, j: (i, 0)), BS((None, 1, d), lambda i, j: (l, 0, 0)),
                  BS((None, hc, d), lambda i, j: (j, 0, 0)), BS((None, hc, d), lambda i, j: (j, 0, 0)),
                  BS((None, hc, d), lambda i, j: (j, 0, 0))],
        out_specs=BS((tm, d), lambda i, j: (i, 0)), out_shape=SDS((s, d), F32),
        scratch_shapes=[pltpu.VMEM((tm, d), MXU_DTYPE)], compiler_params=_params())(x1, norm2, wg, wu, wd)


def _ffn_bwd(l, x1, dx2, norm2, wg, wu, wd):
    s, d = x1.shape
    hc = wd.shape[1]
    tm = min(512, s)
    m = MXU_DTYPE
    last = N_CHIPS - 1

    def body(x_ref, dx2_ref, g_ref, wg_ref, wu_ref, wd_ref, dx1_ref, h_ref, dxb_ref, act_ref, dgate_ref, dup_ref,
             dn_ref, dh_scr):
        i, j = pl.program_id(0), pl.program_id(1)

        @pl.when(j == 0)
        def _():
            xv = x_ref[...]
            r = lax.rsqrt(jnp.mean(xv * xv, axis=-1, keepdims=True) + EPS)
            h_ref[...] = (xv * r * g_ref[...]).astype(m)
            dxb_ref[...] = dx2_ref[...].astype(m)
            dh_scr[...] = jnp.zeros(dh_scr.shape, F32)

        @pl.when((i == 0) & (j == 0))
        def _():
            dn_ref[...] = jnp.zeros(dn_ref.shape, F32)

        h = h_ref[...]
        gate = _mm_nt(h, wg_ref[...])
        up = _mm_nt(h, wu_ref[...])
        sg = _sigmoid(gate)
        silu = gate * sg
        act_ref[...] = (silu * up).astype(m).T
        dact = _mm_nt(dxb_ref[...], wd_ref[...])
        dup = (dact * silu).astype(m)
        dgate = (dact * up * (sg * (1.0 + gate * (1.0 - sg)))).astype(m)
        dup_ref[...] = dup.T
        dgate_ref[...] = dgate.T
        dh_scr[...] += _mm(dgate, wg_ref[...]) + _mm(dup, wu_ref[...])

        @pl.when(j == last)
        def _():
            xv = x_ref[...]
            r = lax.rsqrt(jnp.mean(xv * xv, axis=-1, keepdims=True) + EPS)
            xh = xv * r
            dh = dh_scr[...]
            dn_ref[...] += _colsum(dh * xh)
            dxh = dh * g_ref[...]
            dx1_ref[...] = dx2_ref[...] + r * (dxh - xh * jnp.mean(dxh * xh, axis=-1, keepdims=True))

    chunk = BS((None, hc, tm), lambda i, j: (j, 0, i))
    outs = pl.pallas_call(
        body, name=f"ffn_bwd_l{l}", grid=(s // tm, N_CHIPS),
        in_specs=[BS((tm, d), lambda i, j: (i, 0)), BS((tm, d), lambda i, j: (i, 0)),
                  BS((None, 1, d), lambda i, j: (l, 0, 0)),
                  BS((None, hc, d), lambda i, j: (j, 0, 0)), BS((None, hc, d), lambda i, j: (j, 0, 0)),
                  BS((None, hc, d), lambda i, j: (j, 0, 0))],
        out_specs=[BS((tm, d), lambda i, j: (i, 0)), BS((tm, d), lambda i, j: (i, 0)), BS((tm, d), lambda i, j: (i, 0)),
                   chunk, chunk, chunk, BS((1, d), lambda i, j: (0, 0))],
        out_shape=[SDS((s, d), F32), SDS((s, d), m), SDS((s, d), m), SDS((N_CHIPS, hc, s), m),
                   SDS((N_CHIPS, hc, s), m), SDS((N_CHIPS, hc, s), m), SDS((1, d), F32)],
        scratch_shapes=[pltpu.VMEM((tm, d), F32)], compiler_params=_params(),
    )(x1, dx2, norm2, wg, wu, wd)
    return dict(zip(("dx1", "h2", "dx2", "act", "dgate", "dup", "dnorm2"), outs))


def _loss_head(x, target, gf):
    s, d = x.shape
    tm = min(512, s)

    def body(x_ref, t_ref, g_ref, dx_ref, loss_ref, dg_ref):
        @pl.when(pl.program_id(0) == 0)
        def _():
            loss_ref[...] = jnp.zeros(loss_ref.shape, F32)
            dg_ref[...] = jnp.zeros(dg_ref.shape, F32)

        xv = x_ref[...]
        r = lax.rsqrt(jnp.mean(xv * xv, axis=-1, keepdims=True) + EPS)
        xh = xv * r
        err = xh * g_ref[...] - t_ref[...]
        loss_ref[...] += 0.5 * jnp.sum(jnp.mean(err * err, axis=-1, keepdims=True), axis=0, keepdims=True)
        dyv = err * (1.0 / d)
        dg_ref[...] += _colsum(dyv * xh)
        dxh = dyv * g_ref[...]
        dx_ref[...] = r * (dxh - xh * jnp.mean(dxh * xh, axis=-1, keepdims=True))

    return pl.pallas_call(
        body, name="loss_head", grid=(s // tm,),
        in_specs=[BS((tm, d), lambda i: (i, 0)), BS((tm, d), lambda i: (i, 0)), BS((1, d), lambda i: (0, 0))],
        out_specs=[BS((tm, d), lambda i: (i, 0)), BS((1, 1), lambda i: (0, 0)), BS((1, d), lambda i: (0, 0))],
        out_shape=[SDS((s, d), F32), SDS((1, 1), F32), SDS((1, d), F32)], compiler_params=_params())(x, target, gf)


def _in_proj_bwd(l, dres, x, norm1, w_in, du_a, dv1, dv2, du_c, dzg):
    s, d = x.shape
    nc = w_in.shape[-1]
    tm = min(512, s)
    m = MXU_DTYPE

    def body(dres_ref, x_ref, g_ref, w_ref, a_ref, b1_ref, b2_ref, c_ref, g3_ref, dx_ref, dz_ref, dn_ref):
        @pl.when(pl.program_id(0) == 0)
        def _():
            dn_ref[...] = jnp.zeros(dn_ref.shape, F32)

        dz = jnp.concatenate([a_ref[...], b1_ref[...], b2_ref[...], c_ref[...], g3_ref[...]], axis=1).astype(m)
        dz_ref[...] = dz
        dh = _mm_nt_cols(dz, w_ref)
        xv = x_ref[...]
        r = lax.rsqrt(jnp.mean(xv * xv, axis=-1, keepdims=True) + EPS)
        xh = xv * r
        dn_ref[...] += _colsum(dh * xh)
        dxh = dh * g_ref[...]
        dx_ref[...] = dres_ref[...] + r * (dxh - xh * jnp.mean(dxh * xh, axis=-1, keepdims=True))

    tile = lambda n: BS((tm, n), lambda i: (i, 0))
    return pl.pallas_call(
        body, name=f"in_proj_bwd_l{l}", grid=(s // tm,),
        in_specs=[tile(d), tile(d), BS((None, 1, d), lambda i: (l, 0, 0)),
                  BS((N_CHIPS, d, nc), lambda i: (0, 0, 0), pipeline_mode=pl.Buffered(1)),
                  tile(du_a.shape[1]), tile(dv1.shape[1]), tile(dv2.shape[1]), tile(du_c.shape[1]), tile(dzg.shape[1])],
        out_specs=[tile(d), tile(N_CHIPS * nc), BS((1, d), lambda i: (0, 0))],
        out_shape=[SDS((s, d), F32), SDS((s, N_CHIPS * nc), m), SDS((1, d), F32)], compiler_params=_params(),
    )(dres, x, norm1, w_in, du_a, dv1, dv2, du_c, dzg)


def _tn_matmul(name, a, a_spec, b, b_spec, chunk_shape, grid, place):
    last = grid[1] - 1

    def body(place_ref, a_ref, b_ref, own_ref, wire_ref, *acc):
        part = _mm(a_ref[...], b_ref[...])

        def emit(total):
            wire_ref[...] = total.astype(WIRE_DTYPE)

            @pl.when(pl.program_id(0) == place_ref[0])
            def _():
                own_ref[...] = total

        if last == 0:
            emit(part)
        else:
            @pl.when(pl.program_id(1) == 0)
            def _():
                acc[0][...] = part

            @pl.when(pl.program_id(1) > 0)
            def _():
                acc[0][...] += part

            @pl.when(pl.program_id(1) == last)
            def _():
                emit(acc[0][...])

    zeros = (0,) * len(chunk_shape)
    return pl.pallas_call(
        body, name=name,
        grid_spec=pltpu.PrefetchScalarGridSpec(
            num_scalar_prefetch=1, grid=grid, in_specs=[a_spec, b_spec],
            out_specs=[BS(chunk_shape, lambda j, t, pr: zeros), BS((None,) + chunk_shape, lambda j, t, pr: (j,) + zeros)],
            scratch_shapes=[pltpu.VMEM(chunk_shape, F32)] if last else []),
        out_shape=[SDS(chunk_shape, F32), SDS((N_CHIPS,) + chunk_shape, WIRE_DTYPE)],
        compiler_params=_params())(place, a, b)


def _scan_consts(pw_ref, lanes, reverse):
    sgn = -1.0 if reverse else 1.0
    row = lax.broadcasted_iota(jnp.int32, (8, lanes), 0)
    steps = []
    for i, k in enumerate((1, 2, 4)):
        mask = (row < 8 - k) if reverse else (row >= k)
        steps.append((k, jnp.where(mask, pw_ref[2 * i], 0.0), jnp.where(mask, sgn * pw_ref[2 * i + 1], 0.0)))
    c = 4 if reverse else 3
    return steps, pw_ref[2 * c], sgn * pw_ref[2 * c + 1]


def _scan_block(br, bi, steps, row, reverse):
    for k, ar, ai in steps:
        sh = 8 - k if reverse else k
        sr = pltpu.roll(br, sh, 0)
        si = pltpu.roll(bi, sh, 0)
        br, bi = br + ar * sr - ai * si, bi + ar * si + ai * sr
    return br, bi


def _ssm_fwd(l, z, bblk_re, bblk_im, cblk_re, cblk_im, pw, dskip):
    s = z.shape[0]
    gc = bblk_re.shape[1]
    gl = bblk_re.shape[2]
    nblk = bblk_re.shape[0]

    def body(u_ref, bre, bim, cre, cim, pw_ref, d_ref, hre, him, y_ref):
        u = u_ref[...]
        hre[...] = _mm(u, bre[...])
        him[...] = _mm(u, bim[...])
        row = lax.broadcasted_iota(jnp.int32, (8, gl), 0)
        steps, car, cai = _scan_consts(pw_ref, gl, False)

        def step(i, carry):
            cr, ci = carry
            r0 = pl.multiple_of(i * 8, 8)
            br, bi = _scan_block(hre[pl.ds(r0, 8), :], him[pl.ds(r0, 8), :], steps, row, False)
            hr = br + car * cr - cai * ci
            hi = bi + car * ci + cai * cr
            hre[pl.ds(r0, 8), :] = hr
            him[pl.ds(r0, 8), :] = hi
            return jnp.broadcast_to(hr[7:8, :], (8, gl)), jnp.broadcast_to(hi[7:8, :], (8, gl))

        zero = jnp.zeros((8, gl), F32)
        lax.fori_loop(0, s // 8, step, (zero, zero))
        y_ref[...] = _mm_nt(hre[...], cre[...]) - _mm_nt(him[...], cim[...]) + d_ref[...] * u

    return pl.pallas_call(
        body, name=f"ssm_fwd_l{l}", grid=(nblk,),
        in_specs=[BS((s, gc), lambda k: (0, k)), BS((None, gc, gl), lambda k: (k, 0, 0)),
                  BS((None, gc, gl), lambda k: (k, 0, 0)), BS((None, gc, gl), lambda k: (k, 0, 0)),
                  BS((None, gc, gl), lambda k: (k, 0, 0)), BS((10, 8, gl), lambda k: (0, 0, k)),
                  BS((1, gc), lambda k: (0, k))],
        out_specs=[BS((s, gl), lambda k: (0, k)), BS((s, gl), lambda k: (0, k)), BS((s, gc), lambda k: (0, k))],
        out_shape=[SDS((s, nblk * gl), F32), SDS((s, nblk * gl), F32), SDS((s, nblk * gc), F32)],
        compiler_params=_params())(z, bblk_re, bblk_im, cblk_re, cblk_im, pw, dskip)


def _ssm_bwd(l, dy, z, hre, him, bblk_re, bblk_im, cblk_re, cblk_im, pw, dskip):
    s = z.shape[0]
    nblk, gc, gl = bblk_re.shape

    def body(dy_ref, u_ref, hre_ref, him_ref, bre, bim, cre, cim, pw_ref, d_ref,
             du_ref, dbre_ref, dbim_ref, dcre_ref, dcim_ref, dar_ref, dai_ref, dd_ref, gre, gim):
        dyv = dy_ref[...]
        u = u_ref[...]
        gre[...] = _mm(dyv, cre[...])
        gim[...] = -_mm(dyv, cim[...])
        dcre_ref[...] = _mm_tn(dyv, hre_ref[...])
        dcim_ref[...] = -_mm_tn(dyv, him_ref[...])
        dd_ref[...] = _colsum(dyv * u)
        row = lax.broadcasted_iota(jnp.int32, (8, gl), 0)
        steps, car, cai = _scan_consts(pw_ref, gl, True)
        n8 = s // 8

        def step(ii, carry):
            cr, ci, accr, acci = carry
            i = n8 - 1 - ii
            r0 = pl.multiple_of(i * 8, 8)
            br, bi = _scan_block(gre[pl.ds(r0, 8), :], gim[pl.ds(r0, 8), :], steps, row, True)
            dr = br + car * cr - cai * ci
            di = bi + car * ci + cai * cr
            gre[pl.ds(r0, 8), :] = dr
            gim[pl.ds(r0, 8), :] = di
            rp = pl.multiple_of(jnp.maximum(i - 1, 0) * 8, 8)
            keep = jnp.where(i > 0, 1.0, 0.0)
            pr = jnp.where(row >= 1, pltpu.roll(hre_ref[pl.ds(r0, 8), :], 1, 0),
                           keep * pltpu.roll(hre_ref[pl.ds(rp, 8), :], 1, 0))
            pi = jnp.where(row >= 1, pltpu.roll(him_ref[pl.ds(r0, 8), :], 1, 0),
                           keep * pltpu.roll(him_ref[pl.ds(rp, 8), :], 1, 0))
            accr = accr + dr * pr + di * pi
            acci = acci + di * pr - dr * pi
            return (jnp.broadcast_to(dr[0:1, :], (8, gl)), jnp.broadcast_to(di[0:1, :], (8, gl)), accr, acci)

        zero = jnp.zeros((8, gl), F32)
        _, _, accr, acci = lax.fori_loop(0, n8, step, (zero, zero, zero, zero))
        dar_ref[...] = _colsum(accr)
        dai_ref[...] = _colsum(acci)
        dbr = gre[...]
        dbi = gim[...]
        du_ref[...] = (dyv * d_ref[...] + _mm_nt(dbr, bre[...]) + _mm_nt(dbi, bim[...])).astype(du_ref.dtype)
        dbre_ref[...] = _mm_tn(u, dbr)
        dbim_ref[...] = _mm_tn(u, dbi)

    col = lambda n: BS((s, n), lambda k: (0, k))
    blk = lambda a, b: BS((None, a, b), lambda k: (k, 0, 0))
    outs = pl.pallas_call(
        body, name=f"ssm_bwd_l{l}", grid=(nblk,),
        in_specs=[col(gc), col(gc), col(gl), col(gl), blk(gc, gl), blk(gc, gl), blk(gc, gl), blk(gc, gl),
                  BS((10, 8, gl), lambda k: (0, 0, k)), BS((1, gc), lambda k: (0, k))],
        out_specs=[col(gc), blk(gc, gl), blk(gc, gl), blk(gc, gl), blk(gc, gl), BS((1, gl), lambda k: (0, k)),
                   BS((1, gl), lambda k: (0, k)), BS((1, gc), lambda k: (0, k))],
        out_shape=[SDS((s, nblk * gc), MXU_DTYPE), SDS((nblk, gc, gl), F32), SDS((nblk, gc, gl), F32),
                   SDS((nblk, gc, gl), F32), SDS((nblk, gc, gl), F32), SDS((1, nblk * gl), F32),
                   SDS((1, nblk * gl), F32), SDS((1, nblk * gc), F32)],
        scratch_shapes=[pltpu.VMEM((s, gl), F32), pltpu.VMEM((s, gl), F32)], compiler_params=_params(),
    )(dy, z, hre, him, bblk_re, bblk_im, cblk_re, cblk_im, pw, dskip)
    return dict(zip(("du", "dbblk_re", "dbblk_im", "dcblk_re", "dcblk_im", "dabar_re", "dabar_im", "dd"), outs))


def _conv_fwd(l, z, wdw, bdw):
    s = z.shape[0]
    cw = wdw.shape[1]
    lb = 128
    tr = min(256, s)
    off1 = cw // lb
    off2 = 2 * cw // lb

    def body(v1_ref, v2_ref, w_ref, b_ref, hc_ref, scr):
        scr[0:CONV_PAD, :] = jnp.zeros((CONV_PAD, lb), F32)
        scr[CONV_PAD:, :] = v1_ref[...] * _sigmoid(v2_ref[...])
        for t in range(s // tr):
            acc = jnp.broadcast_to(b_ref[...], (tr, lb))
            for k in range(CONV_KERNEL):
                acc = acc + w_ref[pl.ds(k, 1), :] * scr[pl.ds(t * tr + CONV_PAD - (CONV_KERNEL - 1) + k, tr), :]
            hc_ref[pl.ds(t * tr, tr), :] = acc

    return pl.pallas_call(
        body, name=f"conv_fwd_l{l}", grid=(cw // lb,),
        in_specs=[BS((s, lb), lambda k: (0, off1 + k)), BS((s, lb), lambda k: (0, off2 + k)),
                  BS((CONV_KERNEL, lb), lambda k: (0, k)), BS((1, lb), lambda k: (0, k))],
        out_specs=BS((s, lb), lambda k: (0, k)), out_shape=SDS((s, cw), F32),
        scratch_shapes=[pltpu.VMEM((s + CONV_PAD, lb), F32)], compiler_params=_params())(z, z, wdw, bdw)


def _conv_bwd(l, dhc, z, wdw):
    s = z.shape[0]
    cw = wdw.shape[1]
    lb = 128
    tr = min(256, s)
    off1 = cw // lb
    off2 = 2 * cw // lb
    nb = cw // lb

    def body(d_ref, v1_ref, v2_ref, w_ref, dv1_ref, dv2_ref, dw_ref, db_ref, hpad, dpad):
        v1 = v1_ref[...]
        sg = _sigmoid(v2_ref[...])
        dv = d_ref[...]
        hpad[0:CONV_PAD, :] = jnp.zeros((CONV_PAD, lb), F32)
        hpad[CONV_PAD:, :] = v1 * sg
        dpad[0:s, :] = dv
        dpad[s:, :] = jnp.zeros((CONV_PAD, lb), F32)
        db_ref[...] = _colsum(dv)
        dws = [jnp.zeros((1, lb), F32) for _ in range(CONV_KERNEL)]
        for t in range(s // tr):
            dt = d_ref[pl.ds(t * tr, tr), :]
            acc = jnp.zeros((tr, lb), F32)
            for k in range(CONV_KERNEL):
                acc = acc + w_ref[pl.ds(k, 1), :] * dpad[pl.ds(t * tr + (CONV_KERNEL - 1) - k, tr), :]
                dws[k] = dws[k] + _colsum(dt * hpad[pl.ds(t * tr + CONV_PAD - (CONV_KERNEL - 1) + k, tr), :])
            sgt = _sigmoid(v2_ref[pl.ds(t * tr, tr), :])
            v1t = v1_ref[pl.ds(t * tr, tr), :]
            dv1_ref[pl.ds(t * tr, tr), :] = (acc * sgt).astype(dv1_ref.dtype)
            dv2_ref[pl.ds(t * tr, tr), :] = (acc * v1t * (sgt * (1.0 - sgt))).astype(dv2_ref.dtype)
        for k in range(CONV_KERNEL):
            dw_ref[pl.ds(k, 1), :] = dws[k]

    return pl.pallas_call(
        body, name=f"conv_bwd_l{l}", grid=(nb,),
        in_specs=[BS((s, lb), lambda k: (0, k)), BS((s, lb), lambda k: (0, off1 + k)),
                  BS((s, lb), lambda k: (0, off2 + k)), BS((CONV_KERNEL, lb), lambda k: (0, k))],
        out_specs=[BS((s, lb), lambda k: (0, k)), BS((s, lb), lambda k: (0, k)),
                   BS((CONV_KERNEL, lb), lambda k: (0, k)), BS((1, lb), lambda k: (0, k))],
        out_shape=[SDS((s, cw), MXU_DTYPE), SDS((s, cw), MXU_DTYPE), SDS((CONV_KERNEL, cw), F32), SDS((1, cw), F32)],
        scratch_shapes=[pltpu.VMEM((s + CONV_PAD, lb), F32), pltpu.VMEM((s + CONV_PAD, lb), F32)],
        compiler_params=_params())(dhc, z, z, wdw)


def _pool_window(k):
    return jnp.where(k == 0, float(POOL_WINDOWS[0]),
                     jnp.where(k == 1, float(POOL_WINDOWS[1]),
                               jnp.where(k == 2, float(POOL_WINDOWS[2]), float(POOL_WINDOWS[3]))))


def _pool_fwd(l, z, pw_width):
    s = z.shape[0]
    lb = pw_width // len(POOL_WINDOWS)
    off = 3 * pw_width // lb

    def body(u_ref, p_ref):
        k = pl.program_id(0)
        u = u_ref[...]
        row = lax.broadcasted_iota(jnp.int32, (s, lb), 0)
        sums = [u]
        for sh in (1, 2, 4, 8):
            prev = sums[-1]
            sums.append(prev + jnp.where(row >= sh, pltpu.roll(prev, sh, 0), 0.0))
        sel = jnp.where(k == 0, sums[1], jnp.where(k == 1, sums[2], jnp.where(k == 2, sums[3], sums[4])))
        cnt = jnp.minimum((row + 1).astype(F32), _pool_window(k))
        p_ref[...] = sel / cnt - u

    return pl.pallas_call(
        body, name=f"pool_fwd_l{l}", grid=(len(POOL_WINDOWS),),
        in_specs=[BS((s, lb), lambda k: (0, off + k))], out_specs=BS((s, lb), lambda k: (0, k)),
        out_shape=SDS((s, pw_width), F32), compiler_params=_params())(z)


def _pool_bwd(l, dp):
    s, width = dp.shape
    lb = width // len(POOL_WINDOWS)

    def body(d_ref, du_ref):
        k = pl.program_id(0)
        dv = d_ref[...]
        row = lax.broadcasted_iota(jnp.int32, (s, lb), 0)
        cnt = jnp.minimum((row + 1).astype(F32), _pool_window(k))
        sums = [dv / cnt]
        for sh in (1, 2, 4, 8):
            prev = sums[-1]
            sums.append(prev + jnp.where(row < s - sh, pltpu.roll(prev, s - sh, 0), 0.0))
        sel = jnp.where(k == 0, sums[1], jnp.where(k == 1, sums[2], jnp.where(k == 2, sums[3], sums[4])))
        du_ref[...] = (sel - dv).astype(du_ref.dtype)

    return pl.pallas_call(
        body, name=f"pool_bwd_l{l}", grid=(len(POOL_WINDOWS),),
        in_specs=[BS((s, lb), lambda k: (0, k))], out_specs=BS((s, lb), lambda k: (0, k)),
        out_shape=SDS((s, width), MXU_DTYPE), compiler_params=_params())(dp)


def _zoh(a_re, a_im, log_dt):
    dt = jnp.exp(log_dt)
    mag = jnp.exp(dt * a_re)
    ang = dt * a_im
    abar_re = mag * jnp.cos(ang)
    abar_im = mag * jnp.sin(ang)
    den = a_re * a_re + a_im * a_im
    nr = abar_re - 1.0
    ni = abar_im
    f_re = (nr * a_re + ni * a_im) / den
    f_im = (ni * a_re - nr * a_im) / den
    return abar_re, abar_im, f_re, f_im


def _zoh_fwd(l, a_re, a_im, log_dt):
    def body(ar, ai, ld, o0, o1, o2, o3):
        for ref, val in zip((o0, o1, o2, o3), _zoh(ar[...], ai[...], ld[...])):
            ref[...] = val

    return pl.pallas_call(body, name=f"zoh_fwd_l{l}", out_shape=[SDS(a_re.shape, F32)] * 4)(a_re, a_im, log_dt)


def _zoh_bwd(l, a_re, a_im, log_dt, cts):
    def body(ar, ai, ld, c0, c1, c2, c3, dar, dai, dld):
        _, vjp = jax.vjp(_zoh, ar[...], ai[...], ld[...])
        g = vjp((c0[...], c1[...], c2[...], c3[...]))
        dar[...] = g[0]
        dai[...] = g[1]
        dld[...] = g[2]

    return pl.pallas_call(body, name=f"zoh_bwd_l{l}",
                          out_shape=[SDS(a_re.shape, F32), SDS(a_re.shape, F32), SDS(log_dt.shape, F32)],
                          )(a_re, a_im, log_dt, *cts)


def _bbar_fwd(l, f_re, f_im, b_re, b_im):
    g, p, n = b_re.shape[1:]

    def body(fr, fi, br, bi, o_re, o_im):
        o_re[...] = (fr[...] * br[...] - fi[...] * bi[...]).astype(o_re.dtype)
        o_im[...] = (fr[...] * bi[...] + fi[...] * br[...]).astype(o_im.dtype)

    whole = lambda shp: BS(shp, lambda i: (0,) * len(shp))
    layer = BS((None, g, p, n), lambda i: (l, 0, 0, 0))
    return pl.pallas_call(body, name=f"bbar_fwd_l{l}", grid=(1,),
                          in_specs=[whole((g, 1, n)), whole((g, 1, n)), layer, layer],
                          out_specs=[whole((g, p, n))] * 2,
                          out_shape=[SDS((g, p, n), MXU_DTYPE)] * 2)(f_re, f_im, b_re, b_im)


def _bbar_bwd(l, f_re, f_im, b_re, b_im, d_re, d_im):
    g, p, n = b_re.shape[1:]

    def body(fr, fi, br, bi, dr, di, dfr, dfi, dbr, dbi):
        dfr[...] = jnp.sum(dr[...] * br[...] + di[...] * bi[...], axis=1, keepdims=True)
        dfi[...] = jnp.sum(di[...] * br[...] - dr[...] * bi[...], axis=1, keepdims=True)
        dbr[...] = fr[...] * dr[...] + fi[...] * di[...]
        dbi[...] = fr[...] * di[...] - fi[...] * dr[...]

    whole = lambda shp: BS(shp, lambda i: (0,) * len(shp))
    layer = BS((None, g, p, n), lambda i: (l, 0, 0, 0))
    return pl.pallas_call(body, name=f"bbar_bwd_l{l}", grid=(1,),
                          in_specs=[whole((g, 1, n)), whole((g, 1, n)), layer, layer, whole((g, p, n)),
                                    whole((g, p, n))],
                          out_specs=[whole((g, 1, n)), whole((g, 1, n)), whole((g, p, n)), whole((g, p, n))],
                          out_shape=[SDS((g, 1, n), F32), SDS((g, 1, n), F32), SDS((g, p, n), F32),
                                     SDS((g, p, n), F32)])(f_re, f_im, b_re, b_im, d_re, d_im)


def _powers(l, abar_re, abar_im):
    lanes = abar_re.shape[1]

    def body(ar_ref, ai_ref, o_ref):
        ar, ai = ar_ref[...], ai_ref[...]
        pows = [(ar, ai)]
        for _ in range(7):
            pr, pi = pows[-1]
            pows.append((pr * ar - pi * ai, pr * ai + pi * ar))
        row = lax.broadcasted_iota(jnp.int32, (8, lanes), 0)
        for i, k in enumerate((1, 2, 4)):
            o_ref[2 * i] = jnp.broadcast_to(pows[k - 1][0], (8, lanes))
            o_ref[2 * i + 1] = jnp.broadcast_to(pows[k - 1][1], (8, lanes))
        for slot, order in ((3, range(8)), (4, range(7, -1, -1))):
            vr = jnp.zeros((8, lanes), F32)
            vi = jnp.zeros((8, lanes), F32)
            for r, e in enumerate(order):
                vr = jnp.where(row == r, pows[e][0], vr)
                vi = jnp.where(row == r, pows[e][1], vi)
            o_ref[2 * slot] = vr
            o_ref[2 * slot + 1] = vi

    return pl.pallas_call(body, name=f"powers_l{l}", out_shape=SDS((10, 8, lanes), F32))(abar_re, abar_im)


def _block_diag(v):
    g, a, b = v.shape
    eye = jnp.eye(8, dtype=v.dtype)
    out = jnp.einsum("kgab,gh->kgahb", v.reshape(g // 8, 8, a, b), eye)
    return out.reshape(g // 8, 8 * a, 8 * b)


def _block_diag_extract(blk, a, b):
    n = blk.shape[0]
    v = blk.reshape(n, 8, a, 8, b)
    return jnp.einsum("kgahb,gh->kgab", v, jnp.eye(8, dtype=blk.dtype)).reshape(n * 8, a, b)


def _ssm_prepare(l, prm):
    g, n, p = SSM_GROUPS, SSM_STATE, SSM_GROUP
    a_re, a_im = prm["ssm_a_re"][l], prm["ssm_a_im"][l]
    log_dt = prm["ssm_log_dt"][l].reshape(g, 1)
    abar_re, abar_im, f_re, f_im = _zoh_fwd(l, a_re, a_im, log_dt)
    f_re, f_im = f_re.reshape(g, 1, n), f_im.reshape(g, 1, n)
    bbar_re, bbar_im = _bbar_fwd(l, f_re, f_im, prm["ssm_b_re"], prm["ssm_b_im"])
    pw = _powers(l, abar_re.reshape(1, g * n), abar_im.reshape(1, g * n))
    return dict(a_re=a_re, a_im=a_im, log_dt=log_dt, f_re=f_re, f_im=f_im,
                bblk_re=_block_diag(bbar_re), bblk_im=_block_diag(bbar_im),
                cblk_re=_block_diag(prm["ssm_c_re"][l].astype(MXU_DTYPE)),
                cblk_im=_block_diag(prm["ssm_c_im"][l].astype(MXU_DTYPE)), pw=pw,
                dskip=prm["ssm_d"][l].reshape(1, g * p))


def _ssm_param_grads(l, sd, r, prm):
    g, n, p = SSM_GROUPS, SSM_STATE, SSM_GROUP
    dbbar_re = _block_diag_extract(r["dbblk_re"], p, n)
    dbbar_im = _block_diag_extract(r["dbblk_im"], p, n)
    dfr, dfi, db_re, db_im = _bbar_bwd(l, sd["f_re"], sd["f_im"], prm["ssm_b_re"], prm["ssm_b_im"], dbbar_re, dbbar_im)
    cts = (r["dabar_re"].reshape(g, n), r["dabar_im"].reshape(g, n), dfr.reshape(g, n), dfi.reshape(g, n))
    da_re, da_im, dlog_dt = _zoh_bwd(l, sd["a_re"], sd["a_im"], sd["log_dt"], cts)
    return dict(ssm_a_re=da_re, ssm_a_im=da_im, ssm_log_dt=dlog_dt.reshape(g), ssm_b_re=db_re, ssm_b_im=db_im,
                ssm_c_re=_block_diag_extract(r["dcblk_re"], p, n), ssm_c_im=_block_diag_extract(r["dcblk_im"], p, n),
                ssm_d=r["dd"].reshape(g, p))


def _ffn_weight_grads(l, fb, dx2, s, place):
    d = dx2.shape[1]
    hcn = fb["act"].shape[1]
    g = {}
    for name, key, rhs in (("ffn_w_gate", "dgate", fb["h2"]), ("ffn_w_up", "dup", fb["h2"]),
                           ("ffn_w_down", "act", fb["dx2"])):
        g[name] = _tn_matmul(f"d{name}_l{l}", fb[key], BS((None, hcn, s), lambda j, t, pr: (j, 0, 0)), rhs,
                             BS((s, d), lambda j, t, pr: (0, 0)), (hcn, d), (N_CHIPS, 1), place)
    return g


def _in_weight_grad(l, ht, dz, place):
    d, s = ht.shape
    ncw = dz.shape[1] // N_CHIPS
    return _tn_matmul(f"dw_in_l{l}", ht, BS((d, s), lambda j, t, pr: (0, 0)), dz, BS((s, ncw), lambda j, t, pr: (0, j)),
                      (d, ncw), (N_CHIPS, 1), place)


def _fused_tn(name, pairs, kinds, s, place):
    n = len(pairs)

    def shape_of(a, b, kind):
        k, m = a.shape[1], b.shape[1]
        if kind == "rows":
            return (N_CHIPS, k // N_CHIPS, m)
        if kind == "cols":
            return (N_CHIPS, k, m // N_CHIPS)
        return (k // 128, 128, 128)

    shapes = [shape_of(a, b, kind) for (a, b), kind in zip(pairs, kinds)]
    out_shape = []
    for shp, kind in zip(shapes, kinds):
        out_shape += [SDS(shp, F32)] if kind == "groups" else [SDS(shp[1:], F32), SDS(shp, WIRE_DTYPE)]

    def body(place_ref, *refs):
        ins, outs, accs = refs[:2 * n], refs[2 * n:2 * n + len(out_shape)], refs[2 * n + len(out_shape):]
        o = 0
        for i, kind in enumerate(kinds):
            a, b = ins[2 * i][...], ins[2 * i + 1][...]
            if kind == "groups":
                for k in range(shapes[i][0]):
                    outs[o][k] = _mm_tn(a[:, k * 128:(k + 1) * 128], b[:, k * 128:(k + 1) * 128])
                o += 1
                continue
            acc = accs[i]
            if kind == "rows":
                acc[...] = _mm_tn(a, b).reshape(acc.shape)
            else:
                full = _mm_tn(a, b)
                nc = acc.shape[2]
                for j in range(N_CHIPS):
                    acc[j] = full[:, j * nc:(j + 1) * nc]
            outs[o][...] = acc[place_ref[0]]
            outs[o + 1][...] = acc[...].astype(WIRE_DTYPE)
            o += 2

    whole = lambda shp: BS(shp, lambda t, pr: (0,) * len(shp))
    outs = pl.pallas_call(
        body, name=name,
        grid_spec=pltpu.PrefetchScalarGridSpec(
            num_scalar_prefetch=1, grid=(1,),
            in_specs=[whole(v.shape) for pair in pairs for v in pair],
            out_specs=[whole(o.shape) for o in out_shape],
            scratch_shapes=[pltpu.VMEM(shp, F32) for shp in shapes]),
        out_shape=out_shape, compiler_params=_params(),
    )(place, *[v for pair in pairs for v in pair])
    res, o = [], 0
    for kind in kinds:
        if kind == "groups":
            res.append(outs[o])
            o += 1
        else:
            res.append((outs[o], outs[o + 1]))
            o += 2
    return res


def _mixer_weight_grads(l, sv, mb, dx1, s, place):
    g = {}
    (g["w_out"], g["ssm_w_glu"]) = _fused_tn(f"dw_out_glu_l{l}", [(mb["merged"], dx1), (mb["ge"], mb["dt"])],
                                            ("rows", "rows"), s, place)
    (g["ssm_w_proj"], g["conv_w_proj"], g["pool_w_proj"]) = _fused_tn(
        f"dw_proj_l{l}", [(mb["sa"], mb["dya"]), (mb["ac"], mb["dyb"]), (mb["pp"], mb["dyc"])],
        ("cols", "cols", "cols"), s, place)
    (dwgrp,) = _fused_tn(f"dpool_w_group_l{l}", [(sv["p"], mb["dq"])], ("groups",), s, place)
    return g, dwgrp


def _local_step(x, target, weights_of, prm, place, on_grads=None):
    s, d = x.shape
    cw = prm["ssm_b_glu"].shape[1]
    sp = {k: prm[k].reshape(N_LAYERS, 1, -1) for k in ("norm1", "norm2", "b_gate", "ssm_b_glu", "conv_ln_g", "conv_ln_b",
                                                        "pool_scale", "conv_b_dw")}
    sp["pool_w_group"] = prm["pool_w_group"]
    saved = []
    xin = x
    for l in range(N_LAYERS):
        fw = weights_of(l, "in", (xin,))
        sd = _ssm_prepare(l, prm)
        z, h = _in_proj(l, xin, sp["norm1"], fw["w_in"])
        hre, him, y = _ssm_fwd(l, z, sd["bblk_re"], sd["bblk_im"], sd["cblk_re"], sd["cblk_im"], sd["pw"], sd["dskip"])
        p = _pool_fwd(l, z, cw)
        fw.update(weights_of(l, "mixer", (y, p)))
        wdw = fw["conv_w_dw"]
        hc = _conv_fwd(l, z, wdw, sp["conv_b_dw"][l])
        x1 = _merge_fwd(l, xin, y, hc, p, z, fw, sp)
        fw.update(weights_of(l, "ffn", (x1,)))
        x2 = _ffn_fwd(l, x1, sp["norm2"], fw["ffn_w_gate"], fw["ffn_w_up"], fw["ffn_w_down"])
        saved.append(dict(x=xin, z=z, h=h, hre=hre, him=him, y=y, hc=hc, p=p, x1=x1, sd=sd, wdw=wdw, fw=fw))
        xin = x2
    dx, loss, dfinal = _loss_head(xin, target, prm["final_norm"].reshape(1, d))
    big = [None] * N_LAYERS
    small = [None] * N_LAYERS
    norm2_rows = sp["norm2"]
    started = (lambda l, group, grads: on_grads(l, group, grads)) if on_grads is not None else (lambda *a: 0.0)
    for l in reversed(range(N_LAYERS)):
        sv = saved[l]
        sd, fw = sv["sd"], sv["fw"]
        fb = _ffn_bwd(l, sv["x1"], dx, norm2_rows, fw["ffn_w_gate"], fw["ffn_w_up"], fw["ffn_w_down"])
        big[l] = _ffn_weight_grads(l, fb, dx, s, place)
        spl = dict(sp, ssm_b_glu=sp["ssm_b_glu"] + started(l, "ffn", big[l]))
        mb = _merge_bwd(l, fb["dx1"], sv["y"], sv["hc"], sv["p"], sv["z"], fw, spl)
        mixer, dwgrp = _mixer_weight_grads(l, sv, mb, fb["dx1"], s, place)
        big[l].update(mixer)
        wdw = sv["wdw"] + started(l, "mixer", mixer)
        du_c = _pool_bwd(l, mb["dp"])
        dv1, dv2, dwdw, dbdw = _conv_bwd(l, mb["dhc"], sv["z"], wdw)
        sr = _ssm_bwd(l, mb["dy"], sv["z"], sv["hre"], sv["him"], sd["bblk_re"], sd["bblk_im"], sd["cblk_re"],
                      sd["cblk_im"], sd["pw"], sd["dskip"])
        dx, dz, dnorm1 = _in_proj_bwd(l, fb["dx1"], sv["x"], sp["norm1"], fw["w_in"], sr["du"], dv1, dv2, du_c, mb["dzg"])
        w_in_grad = {"w_in": _in_weight_grad(l, sv["h"], dz, place)}
        big[l].update(w_in_grad)
        sg = _ssm_param_grads(l, sd, sr, prm)
        sg.update(norm1=dnorm1.reshape(d), b_gate=mb["db_gate"].reshape(3 * d), ssm_b_glu=mb["db_glu"].reshape(cw),
                  conv_b_dw=dbdw.reshape(cw), conv_ln_g=mb["dln_g"].reshape(cw), conv_ln_b=mb["dln_b"].reshape(cw),
                  pool_w_group=dwgrp, pool_scale=mb["dscale"].reshape(cw), norm2=fb["dnorm2"].reshape(d),
                  conv_w_dw=dwdw)
        small[l] = sg
        if l == N_LAYERS - 1:
            sg = dict(sg, final_norm=dfinal.reshape(d))
        norm2_rows = sp["norm2"] + (started(l, "in", w_in_grad) + started(l, "small", sg))
    return loss[0, 0], dx, big, small, dfinal.reshape(d)


def _place():
    return lax.axis_index("x"), lax.axis_index("y"), lax.axis_index("c")


def _other_chips(x, y):
    return [(1 - x, y), (x, 1 - y), (1 - x, 1 - y)]


def _remote(src, dst, send_sem, recv_sem, device):
    return pltpu.make_async_remote_copy(src_ref=src, dst_ref=dst, send_sem=send_sem, recv_sem=recv_sem,
                                        device_id=device, device_id_type=MESH)


def _hbm(v):
    return pltpu.with_memory_space_constraint(v, pltpu.HBM)


def _cast_into(name, w, place, dtype, after=()):
    nl, k, n = w.shape
    tr = _row_tile(k, n)
    nt = k // tr

    def body(place_ref, w_ref, *rest):
        o0_ref, o1_ref = rest[len(after):]

        @pl.when(pl.program_id(0) == 0)
        def _():
            o0_ref[...] = w_ref[...].astype(dtype)

        @pl.when(pl.program_id(0) == 1)
        def _():
            o1_ref[...] = w_ref[...].astype(dtype)

    return pl.pallas_call(
        body, name=f"cast_{name}",
        grid_spec=pltpu.PrefetchScalarGridSpec(
            num_scalar_prefetch=1, grid=(nl, nt),
            in_specs=[BS((None, tr, n), lambda l, t, pr: (l, t, 0))] + [ANY] * len(after),
            out_specs=[BS((None, tr, n), lambda l, t, pr: (pr[0], t * (1 - l) + (nt - 1) * l, 0)),
                       BS((None, tr, n), lambda l, t, pr: (pr[0], t * l, 0))]),
        out_shape=[SDS((N_CHIPS, k, n), dtype)] * 2)(place, w, *after)


def _cast_small_into(tag, ws, dtypes, place, after=()):
    n = len(ws)

    def body(place_ref, *refs):
        ins, outs = refs[:n], refs[n + len(after):]
        for i in range(n):
            @pl.when(pl.program_id(0) == 0)
            def _():
                outs[2 * i][...] = ins[i][...].astype(dtypes[i])

            @pl.when(pl.program_id(0) == 1)
            def _():
                outs[2 * i + 1][...] = ins[i][...].astype(dtypes[i])

    slot = lambda w: BS((None,) + w.shape[1:], lambda l, pr: (pr[0], 0, 0))
    outs = pl.pallas_call(
        body, name=f"cast_{tag}",
        grid_spec=pltpu.PrefetchScalarGridSpec(
            num_scalar_prefetch=1, grid=(N_LAYERS,),
            in_specs=[BS((None,) + w.shape[1:], lambda l, pr: (l, 0, 0)) for w in ws] + [ANY] * len(after),
            out_specs=[slot(w) for w in ws for _ in range(N_LAYERS)]),
        out_shape=[SDS((N_CHIPS,) + w.shape[1:], dt) for w, dt in zip(ws, dtypes) for _ in range(N_LAYERS)],
    )(place, *ws, *after)
    return [tuple(outs[N_LAYERS * i:N_LAYERS * (i + 1)]) for i in range(n)]


def _gather_rows(buf, c):
    k = buf.shape[1]
    if k % 2:
        return pl.ds(0, k)
    return pl.ds(pl.multiple_of(c * (k // 2), 8), k // 2)


def _allgather_start(tag, groups):
    ng = len(groups)
    sizes = [len(g) for g in groups]
    first = [sum(sizes[:g]) for g in range(ng)]
    flat = [b for g in groups for b in g]
    nb = len(flat)

    def body(*refs):
        ins = refs[:nb]
        sems = refs[nb:nb + 2 * ng]
        token = refs[-1]
        x, y, c = _place()
        jme = 2 * x + y
        for g in range(ng):
            for a in range(sizes[g]):
                buf = ins[first[g] + a]
                blk = buf.at[jme, _gather_rows(buf, c)]
                for k, (cx, cy) in enumerate(_other_chips(x, y)):
                    _remote(blk, blk, sems[2 * g].at[3 * a + k], sems[2 * g + 1].at[3 * a + k], (cx, cy, c)).start()
        token[...] = jnp.zeros(token.shape, F32)

    sem_shapes = [pltpu.SemaphoreType.DMA((3 * sizes[g // 2],)) for g in range(2 * ng)]
    outs = pl.pallas_call(
        body, name=f"allgather_start_{tag}", in_specs=[HBM] * nb,
        out_specs=[SEM] * (2 * ng) + [HBM] * nb + [pl.BlockSpec(memory_space=pltpu.VMEM)],
        out_shape=sem_shapes + [pltpu.HBM(b.shape, b.dtype) for b in flat] + [SDS((8, 128), F32)],
        input_output_aliases={i: 2 * ng + i for i in range(nb)},
        compiler_params=pltpu.CompilerParams(has_side_effects=SIDE_EFFECT))(*[_hbm(b) for b in flat])
    per_group = [(outs[2 * g], outs[2 * g + 1], outs[2 * ng + first[g]:2 * ng + first[g] + sizes[g]])
                 for g in range(ng)]
    return per_group, outs[-1]


def _allgather_wait(l, send_sems, recv_sems, bufs, after):
    n = len(bufs)

    def body(*refs):
        ins = refs[:n]
        ssem, rsem = refs[n], refs[n + 1]
        x, y, c = _place()
        jme = 2 * x + y
        for a in range(n):
            rows = _gather_rows(ins[a], c)
            for k, (cx, cy) in enumerate(_other_chips(x, y)):
                cp = _remote(ins[a].at[jme, rows], ins[a].at[2 * cx + cy, rows], ssem.at[3 * a + k],
                             rsem.at[3 * a + k], (cx, cy, c))
                cp.wait_send()
                cp.wait_recv()

    return pl.pallas_call(
        body, name=f"allgather_wait_{l}", in_specs=[HBM] * n + [SEM, SEM] + [ANY] * len(after), out_specs=[HBM] * n,
        out_shape=[pltpu.HBM(b.shape, b.dtype) for b in bufs], input_output_aliases={i: i for i in range(n)},
        compiler_params=pltpu.CompilerParams(has_side_effects=SIDE_EFFECT))(*bufs, send_sems, recv_sems, *after)


def _allgather_forward(l, bufs):
    n = len(bufs)
    split = [a for a in range(n) if bufs[a].shape[1] % 2 == 0]

    def body(*refs):
        ins = refs[:n]
        send_sems, recv_sems = refs[2 * n:]
        x, y, c = _place()
        sibling = (x, y, 1 - c)
        copies = []
        for a in split:
            for k, (cx, cy) in enumerate(_other_chips(x, y)):
                blk = ins[a].at[2 * cx + cy, _gather_rows(ins[a], c)]
                cp = _remote(blk, blk, send_sems.at[a, k], recv_sems.at[a, k], sibling)
                cp.start()
                copies.append(cp)
        for a in split:
            for k, (cx, cy) in enumerate(_other_chips(x, y)):
                blk = ins[a].at[2 * cx + cy, _gather_rows(ins[a], 1 - c)]
                _remote(blk, blk, send_sems.at[a, k], recv_sems.at[a, k], sibling).wait_recv()
        for cp in copies:
            cp.wait_send()

    sem = pltpu.SemaphoreType.DMA((n, 3))
    return pl.pallas_call(
        body, name=f"allgather_forward_{l}", in_specs=[ANY] * n, out_specs=[ANY] * n,
        out_shape=[SDS(b.shape, b.dtype) for b in bufs], input_output_aliases={i: i for i in range(n)},
        scratch_shapes=[sem, sem])(*bufs)


def _rs_to_owner(l, parts):
    n = len(parts)
    lands = [lax.empty((3,) + p.shape[1:], p.dtype) for p in parts]

    def body(*refs):
        ins, zones = refs[:n], refs[n:2 * n]
        send_sems, recv_sems = refs[2 * n], refs[2 * n + 1]
        token = refs[-1]
        x, y, c = _place()
        for a in range(n):
            for k, (cx, cy) in enumerate(_other_chips(x, y)):
                _remote(ins[a].at[2 * cx + cy], zones[a].at[k], send_sems.at[3 * a + k], recv_sems.at[3 * a + k],
                        (cx, cy, c)).start()
        token[...] = jnp.zeros(token.shape, F32)

    sem = pltpu.SemaphoreType.DMA((3 * n,))
    outs = pl.pallas_call(
        body, name=f"rs_to_owner_start_{l}", in_specs=[HBM] * (2 * n),
        out_specs=[SEM, SEM] + [HBM] * (2 * n) + [pl.BlockSpec(memory_space=pltpu.VMEM)],
        out_shape=[sem, sem] + [pltpu.HBM(p.shape, p.dtype) for p in parts]
        + [pltpu.HBM(z.shape, z.dtype) for z in lands] + [SDS((8, 128), F32)],
        input_output_aliases={i: 2 + i for i in range(2 * n)},
        compiler_params=pltpu.CompilerParams(has_side_effects=SIDE_EFFECT),
    )(*[_hbm(p) for p in parts], *[_hbm(z) for z in lands])
    return outs[0], outs[1], outs[2:2 + n], outs[2 + n:2 + 2 * n], outs[-1]


def _rs_to_owner_wait(l, send_sems, recv_sems, parts, lands, after):
    n = len(parts)

    def body(*refs):
        ins, zones = refs[:n], refs[n:2 * n]
        ssem, rsem = refs[2 * n], refs[2 * n + 1]
        x, y, c = _place()
        for a in range(n):
            for k, (cx, cy) in enumerate(_other_chips(x, y)):
                cp = _remote(ins[a].at[2 * cx + cy], zones[a].at[k], ssem.at[3 * a + k], rsem.at[3 * a + k],
                             (cx, cy, c))
                cp.wait_send()
                cp.wait_recv()

    outs = pl.pallas_call(
        body, name=f"rs_to_owner_wait_{l}", in_specs=[HBM] * (2 * n) + [SEM, SEM] + [ANY] * len(after),
        out_specs=[HBM] * (2 * n),
        out_shape=[pltpu.HBM(p.shape, p.dtype) for p in parts] + [pltpu.HBM(z.shape, z.dtype) for z in lands],
        input_output_aliases={i: i for i in range(2 * n)},
        compiler_params=pltpu.CompilerParams(has_side_effects=SIDE_EFFECT),
    )(*parts, *lands, send_sems, recv_sems, *after)
    return outs[:n], outs[n:]


def _rs_sibling_exchange(l, both):
    n = len(both)

    def body(*refs):
        ins = refs[:n]
        send_sems, recv_sems = refs[2 * n:]
        x, y, c = _place()
        copies = []
        for a in range(n):
            cp = _remote(ins[a].at[c], ins[a].at[c], send_sems.at[a], recv_sems.at[a], (x, y, 1 - c))
            cp.start()
            copies.append(cp)
        for a, cp in enumerate(copies):
            cp.wait_send()
            _remote(ins[a].at[1 - c], ins[a].at[1 - c], send_sems.at[a], recv_sems.at[a], (x, y, 1 - c)).wait_recv()

    sem = pltpu.SemaphoreType.DMA((n,))
    return pl.pallas_call(
        body, name=f"rs_sibling_exchange_{l}", in_specs=[ANY] * n, out_specs=[ANY] * n,
        out_shape=[SDS(b.shape, b.dtype) for b in both], input_output_aliases={i: i for i in range(n)},
        scratch_shapes=[sem, sem])(*both)


def _add_owner(name, grad, recv, place):
    r, cols = grad.shape
    tr = _row_tile(r, cols, budget=1024 * 1024)
    nt = r // tr

    def body(place_ref, g_ref, r_ref, o_ref):
        acc = ((g_ref[...] + r_ref[0].astype(F32)) + r_ref[1].astype(F32)) + r_ref[2].astype(F32)
        o_ref[...] = acc.astype(o_ref.dtype)

    return pl.pallas_call(
        body, name=name,
        grid_spec=pltpu.PrefetchScalarGridSpec(
            num_scalar_prefetch=1, grid=(nt,),
            in_specs=[BS((tr, cols), lambda t, pr: (t, 0)), BS((3, tr, cols), lambda t, pr: (0, t, 0))],
            out_specs=BS((None, tr, cols), lambda t, pr: (pr[1], t, 0))),
        out_shape=SDS((2, r, cols), WIRE_DTYPE))(place, grad, recv)


def _add_owner_small(tag, grads, recvs, place):
    n = len(grads)

    def body(place_ref, *refs):
        gs, rs, outs = refs[:n], refs[n:2 * n], refs[2 * n:]
        for g_ref, r_ref, o_ref in zip(gs, rs, outs):
            acc = ((g_ref[...] + r_ref[0].astype(F32)) + r_ref[1].astype(F32)) + r_ref[2].astype(F32)
            o_ref[...] = acc.astype(o_ref.dtype)

    return pl.pallas_call(
        body, name=f"rs_add_owner_{tag}",
        grid_spec=pltpu.PrefetchScalarGridSpec(
            num_scalar_prefetch=1, grid=(1,),
            in_specs=[BS(g.shape, lambda t, pr: (0, 0)) for g in grads]
            + [BS(r.shape, lambda t, pr: (0, 0, 0)) for r in recvs],
            out_specs=[BS((None,) + g.shape, lambda t, pr: (pr[1], 0, 0)) for g in grads]),
        out_shape=[SDS((2,) + g.shape, WIRE_DTYPE) for g in grads])(place, *grads, *recvs)


def _reduce_start(tag, grads):
    names = list(grads)
    send_sems, recv_sems, wires, lands, token = _rs_to_owner(tag, [grads[n][1] for n in names])
    return dict(tag=tag, names=names, send_sems=send_sems, recv_sems=recv_sems, wires=wires, lands=lands,
                grads=[grads[n][0] for n in names]), token


def _reduce_finish(tag, groups, place, after):
    all_names, all_mine = [], []
    for pending in groups:
        sub, names = pending["tag"], pending["names"]
        _, lands = _rs_to_owner_wait(sub, pending["send_sems"], pending["recv_sems"], pending["wires"],
                                     pending["lands"], after)
        if max(g.size for g in pending["grads"]) <= SMALL_GRAD_ELEMS:
            mine = _add_owner_small(sub, pending["grads"], lands, place)
        else:
            mine = [_add_owner(f"rs_add_owner_{n}_{sub}", g, r, place)
                    for n, g, r in zip(names, pending["grads"], lands)]
        all_names += names
        all_mine += mine
    return dict(zip(all_names, _rs_sibling_exchange(tag, all_mine)))


def _small_peers(x, y, c):
    return [(x, y, 1 - c)] + [(cx, cy, c) for cx, cy in _other_chips(x, y)]


def _allgather_rows_start(tag, bufs):
    n = len(bufs)
    lands = [lax.empty((8,) + b.shape, b.dtype) for b in bufs]

    def body(*refs):
        ins, zones = refs[:n], refs[n:2 * n]
        send_sems, recv_sems = refs[2 * n], refs[2 * n + 1]
        token = refs[-1]
        x, y, c = _place()
        for a in range(n):
            for i, peer in enumerate(_small_peers(x, y, c)):
                _remote(ins[a], zones[a].at[4 * x + 2 * y + c], send_sems.at[4 * a + i], recv_sems.at[4 * a + i],
                        peer).start()
        token[...] = jnp.zeros(token.shape, F32)

    sem = pltpu.SemaphoreType.DMA((4 * n,))
    outs = pl.pallas_call(
        body, name=f"allgather_small_start_{tag}", in_specs=[HBM] * (2 * n),
        out_specs=[SEM, SEM] + [HBM] * (2 * n) + [pl.BlockSpec(memory_space=pltpu.VMEM)],
        out_shape=[sem, sem] + [pltpu.HBM(b.shape, b.dtype) for b in bufs]
        + [pltpu.HBM(z.shape, z.dtype) for z in lands] + [SDS((8, 128), F32)],
        input_output_aliases={i: 2 + i for i in range(2 * n)},
        compiler_params=pltpu.CompilerParams(has_side_effects=SIDE_EFFECT),
    )(*[_hbm(b) for b in bufs], *[_hbm(z) for z in lands])
    return outs[0], outs[1], outs[2:2 + n], outs[2 + n:2 + 2 * n], outs[-1]


def _allgather_rows_wait(tag, send_sems, recv_sems, bufs, lands, after):
    n = len(bufs)

    def body(*refs):
        ins, zones = refs[:n], refs[n:2 * n]
        ssem, rsem = refs[2 * n], refs[2 * n + 1]
        x, y, c = _place()
        for a in range(n):
            for i, (px, py, pc) in enumerate(_small_peers(x, y, c)):
                cp = _remote(ins[a], zones[a].at[4 * px + 2 * py + pc], ssem.at[4 * a + i], rsem.at[4 * a + i],
                             (px, py, pc))
                cp.wait_send()
                cp.wait_recv()

    outs = pl.pallas_call(
        body, name=f"allgather_small_wait_{tag}", in_specs=[HBM] * (2 * n) + [SEM, SEM, ANY],
        out_specs=[HBM] * (2 * n),
        out_shape=[pltpu.HBM(b.shape, b.dtype) for b in bufs] + [pltpu.HBM(z.shape, z.dtype) for z in lands],
        input_output_aliases={i: i for i in range(2 * n)},
        compiler_params=pltpu.CompilerParams(has_side_effects=SIDE_EFFECT),
    )(*bufs, *lands, send_sems, recv_sems, after)
    return outs[:n], outs[n:]


def _allgather_rows_forward(tag, lands):
    n = len(lands)

    def body(*refs):
        ins = refs[:n]
        send_sems, recv_sems = refs[2 * n:]
        x, y, c = _place()
        sibling = (x, y, 1 - c)
        copies = []
        for a in range(n):
            for k, (cx, cy) in enumerate(_other_chips(x, y)):
                blk = ins[a].at[4 * cx + 2 * cy + c]
                cp = _remote(blk, blk, send_sems.at[a, k], recv_sems.at[a, k], sibling)
                cp.start()
                copies.append(cp)
        for a in range(n):
            for k, (cx, cy) in enumerate(_other_chips(x, y)):
                blk = ins[a].at[4 * cx + 2 * cy + 1 - c]
                _remote(blk, blk, send_sems.at[a, k], recv_sems.at[a, k], sibling).wait_recv()
        for cp in copies:
            cp.wait_send()

    sem = pltpu.SemaphoreType.DMA((n, 3))
    return pl.pallas_call(body, name=f"allgather_small_forward_{tag}", in_specs=[ANY] * n, out_specs=[ANY] * n,
                          out_shape=[SDS(z.shape, z.dtype) for z in lands],
                          input_output_aliases={i: i for i in range(n)}, scratch_shapes=[sem, sem])(*lands)


def _sum_devices(tag, gathered, mine, place):
    _, r, cols = gathered.shape
    tr = _row_tile(r, cols, budget=256 * 1024)

    def body(place_ref, g_ref, x_ref, o_ref):
        me = 2 * place_ref[0] + place_ref[1]
        acc = jnp.where(me == 0, x_ref[...], g_ref[0])
        for k in range(1, 8):
            acc = acc + jnp.where(me == k, x_ref[...], g_ref[k])
        o_ref[...] = acc

    return pl.pallas_call(
        body, name=f"sum_small_grads_{tag}",
        grid_spec=pltpu.PrefetchScalarGridSpec(
            num_scalar_prefetch=1, grid=(r // tr,),
            in_specs=[BS((8, tr, cols), lambda t, pr: (0, t, 0)), BS((tr, cols), lambda t, pr: (t, 0))],
            out_specs=BS((tr, cols), lambda t, pr: (t, 0))),
        out_shape=SDS((r, cols), F32))(place, gathered, mine)


def _adamw_values(w, g, m, v):
    m = ADAM_B1 * m + (1.0 - ADAM_B1) * g
    v = ADAM_B2 * v + (1.0 - ADAM_B2) * (g * g)
    m_hat = m / (1.0 - ADAM_B1 ** ADAM_STEP)
    v_hat = v / (1.0 - ADAM_B2 ** ADAM_STEP)
    delta = -ADAM_LR * (m_hat / (jnp.sqrt(v_hat) + ADAM_EPS) + ADAM_WD * w)
    return delta, m, v


def _adamw_big(name, l, w, m, v, g, earlier=None, after=()):
    nl, r, cols = w.shape
    tr = _row_tile(r, cols, budget=1024 * 1024)
    nt = r // tr
    n_prev = 0 if earlier is None else 4

    def body(*refs):
        w_ref, m_ref, v_ref, g_ref = refs[:4]
        go_ref, d_ref, mo_ref, vo_ref = refs[4 + n_prev + len(after):]
        gv = g_ref[0].astype(F32) + g_ref[1].astype(F32)
        delta, m_new, v_new = _adamw_values(w_ref[...], gv, m_ref[...], v_ref[...])
        go_ref[...] = gv
        d_ref[...] = delta
        mo_ref[...] = m_new
        vo_ref[...] = v_new

    layer = BS((None, tr, cols), lambda t: (l, t, 0))
    return pl.pallas_call(
        body, name=f"adamw_{name}_l{l}", grid=(nt,),
        in_specs=[layer, layer, layer, BS((2, tr, cols), lambda t: (0, t, 0))] + [ANY] * (n_prev + len(after)),
        out_specs=[layer] * 4, out_shape=[SDS(w.shape, F32)] * 4,
        input_output_aliases={4 + i: i for i in range(n_prev)}, compiler_params=_params(),
    )(w, m, v, g, *(earlier or ()), *after)


def _adamw_small_group(tag, l, ws, ms, vs, gs, earlier, after=()):
    n = len(ws)
    steps = ADAMW_GROUP_STEPS
    prev = [a for e in earlier if e is not None for a in e]
    n_prev = len(prev)
    assert n_prev in (0, 4 * n)

    def body(*refs):
        w_refs, m_refs, v_refs, g_refs = refs[:n], refs[n:2 * n], refs[2 * n:3 * n], refs[3 * n:4 * n]
        outs = refs[4 * n + n_prev + len(after):]
        for i in range(n):
            gv = g_refs[i][0].astype(F32) + g_refs[i][1].astype(F32)
            delta, m_new, v_new = _adamw_values(w_refs[i][...], gv, m_refs[i][...], v_refs[i][...])
            for ref, val in zip(outs[4 * i:4 * i + 4], (gv, delta, m_new, v_new)):
                ref[...] = val

    def layer(w):
        return BS((None, w.shape[1] // steps, w.shape[2]), lambda t: (l, t, 0))

    return pl.pallas_call(
        body, name=f"adamw_{tag}_l{l}", grid=(steps,),
        in_specs=[layer(w) for w in ws] * 3
        + [BS((2, w.shape[1] // steps, w.shape[2]), lambda t: (0, t, 0)) for w in ws] + [ANY] * (n_prev + len(after)),
        out_specs=[layer(w) for w in ws for _ in range(4)],
        out_shape=[SDS(w.shape, F32) for w in ws for _ in range(4)],
        input_output_aliases={4 * n + i: i for i in range(n_prev)}, compiler_params=_params(),
    )(*ws, *ms, *vs, *gs, *prev, *after)


def _adamw_mid(ws, ms, vs, gathered, mine, place):
    n = len(ws)
    shape = ws[0].shape[1:]
    zeros = (0,) * len(shape)

    def body(place_ref, *refs):
        w_refs, m_refs, v_refs = refs[:n], refs[n:2 * n], refs[2 * n:3 * n]
        gath, own = refs[3 * n:(3 + N_LAYERS) * n], refs[(3 + N_LAYERS) * n:(3 + 2 * N_LAYERS) * n]
        outs = refs[(3 + 2 * N_LAYERS) * n:]
        me = 2 * place_ref[0] + place_ref[1]
        for i in range(n):
            gv = None
            for l in range(N_LAYERS):
                g_ref, x_ref = gath[l * n + i], own[l * n + i]
                acc = jnp.where(me == 0, x_ref[...], g_ref[0])
                for k in range(1, 8):
                    acc = acc + jnp.where(me == k, x_ref[...], g_ref[k])
                gv = acc if gv is None else jnp.where(pl.program_id(0) == l, acc, gv)
            delta, m_new, v_new = _adamw_values(w_refs[i][...], gv, m_refs[i][...], v_refs[i][...])
            for ref, val in zip(outs[4 * i:4 * i + 4], (gv, delta, m_new, v_new)):
                ref[...] = val

    layer = BS((None,) + shape, lambda l, pr: (l,) + zeros)
    kept = pl.Buffered(1)
    outs = pl.pallas_call(
        body, name="adamw_replicated_matrices",
        grid_spec=pltpu.PrefetchScalarGridSpec(
            num_scalar_prefetch=1, grid=(N_LAYERS,),
            in_specs=[layer] * (3 * n)
            + [BS((8,) + shape, lambda l, pr: (0,) + zeros, pipeline_mode=kept)] * (N_LAYERS * n)
            + [BS(shape, lambda l, pr: zeros, pipeline_mode=kept)] * (N_LAYERS * n),
            out_specs=[layer] * (4 * n)),
        out_shape=[SDS(ws[0].shape, F32)] * (4 * n), compiler_params=_params(),
    )(place, *ws, *ms, *vs, *[g for l in range(N_LAYERS) for g in gathered[l]],
      *[x for l in range(N_LAYERS) for x in mine[l]])
    return [tuple(outs[4 * i:4 * i + 4]) for i in range(n)]


def _adamw_rows(w, m, v, g):
    r, cols = w.shape
    tr = _row_tile(r, cols, budget=512 * 1024)

    def body(w_ref, m_ref, v_ref, g_ref, d_ref, mo_ref, vo_ref):
        delta, m_new, v_new = _adamw_values(w_ref[...], g_ref[...], m_ref[...], v_ref[...])
        d_ref[...] = delta
        mo_ref[...] = m_new
        vo_ref[...] = v_new

    spec = BS((tr, cols), lambda t: (t, 0))
    return pl.pallas_call(body, name="adamw_small", grid=(r // tr,), in_specs=[spec] * 4, out_specs=[spec] * 3,
                          out_shape=[SDS(w.shape, F32)] * 3)(w, m, v, g)


SMALL_GRAD_ELEMS = 256 * 1024
ADAMW_GROUP_STEPS = 4
PACK_ALIGN = 8 * 128
PACK_ROWS = 128


def _pack_rows(arrays):
    parts, rows = [], 0
    for a in arrays:
        flat = a.reshape(-1)
        pad = (-flat.shape[0]) % PACK_ALIGN
        if pad:
            flat = jnp.pad(flat, (0, pad))
        parts.append(flat.reshape(-1, 128))
        rows += parts[-1].shape[0]
    if rows % PACK_ROWS:
        parts.append(jnp.zeros((PACK_ROWS - rows % PACK_ROWS, 128), parts[0].dtype))
    return jnp.concatenate(parts, axis=0)


def _unpack_rows(buf, shapes):
    out, row = [], 0
    for shape in shapes:
        size = math.prod(shape)
        rows = -(-size // PACK_ALIGN) * (PACK_ALIGN // 128)
        out.append(buf[row:row + rows].reshape(-1)[:size].reshape(shape))
        row += rows
    return out


def kernel(x, norm1, w_in, b_gate, ssm_a_re, ssm_a_im, ssm_log_dt, ssm_b_re, ssm_b_im, ssm_c_re, ssm_c_im, ssm_d, ssm_w_glu, ssm_b_glu, ssm_w_proj, conv_w_dw, conv_b_dw, conv_ln_g, conv_ln_b, conv_w_proj, pool_w_group, pool_scale, pool_w_proj, w_out, norm2, ffn_w_gate, ffn_w_up, ffn_w_down, final_norm, loss_target, m_norm1, m_w_in, m_b_gate, m_ssm_a_re, m_ssm_a_im, m_ssm_log_dt, m_ssm_b_re, m_ssm_b_im, m_ssm_c_re, m_ssm_c_im, m_ssm_d, m_ssm_w_glu, m_ssm_b_glu, m_ssm_w_proj, m_conv_w_dw, m_conv_b_dw, m_conv_ln_g, m_conv_ln_b, m_conv_w_proj, m_pool_w_group, m_pool_scale, m_pool_w_proj, m_w_out, m_norm2, m_ffn_w_gate, m_ffn_w_up, m_ffn_w_down, m_final_norm, v_norm1, v_w_in, v_b_gate, v_ssm_a_re, v_ssm_a_im, v_ssm_log_dt, v_ssm_b_re, v_ssm_b_im, v_ssm_c_re, v_ssm_c_im, v_ssm_d, v_ssm_w_glu, v_ssm_b_glu, v_ssm_w_proj, v_conv_w_dw, v_conv_b_dw, v_conv_ln_g, v_conv_ln_b, v_conv_w_proj, v_pool_w_group, v_pool_scale, v_pool_w_proj, v_w_out, v_norm2, v_ffn_w_gate, v_ffn_w_up, v_ffn_w_down, v_final_norm):
    given = dict(locals())
    cx, cy, cc = _place()
    place = jnp.stack([2 * cx + cy, cc]).astype(jnp.int32)

    def kernel_view(n, a):
        if n in TRANSPOSED:
            return a.transpose(0, 2, 1)
        return a.transpose(0, 1, 3, 2) if n in ("ssm_b_re", "ssm_b_im") else a

    prm = {n: given[n] for n in WEIGHTS}
    mom = {n: given["m_" + n] for n in WEIGHTS}
    var = {n: given["v_" + n] for n in WEIGHTS}
    for n in MID:
        prm[n], mom[n], var[n] = kernel_view(n, prm[n]), kernel_view(n, mom[n]), kernel_view(n, var[n])

    dw_shard = prm["conv_w_dw"].reshape(N_LAYERS, CONV_KERNEL, -1)
    casts = {"w_in": _cast_into("w_in", prm["w_in"], place, MXU_DTYPE)}
    first, first_started = _allgather_start("first", [[casts["w_in"][0]]])
    in_flight = {(0, "in"): first[0]}
    mixer = GATHER_GROUPS["mixer"]
    casts.update(zip(mixer, _cast_small_into(
        "mixer", [dw_shard if n == "conv_w_dw" else prm[n] for n in mixer],
        [F32 if n == "conv_w_dw" else MXU_DTYPE for n in mixer], place, after=(first_started,))))
    casts.update({n: _cast_into(n, kernel_view(n, prm[n]), place, MXU_DTYPE, after=(first_started,))
                  for n in GATHER_GROUPS["ffn"]})
    order = [(l, g) for l in range(N_LAYERS) for g in GATHER_GROUPS if (l, g) != (0, "in")]
    rest, rest_started = _allgather_start("rest", [[casts[n][l] for n in GATHER_GROUPS[g]] for l, g in order])
    in_flight.update(zip(order, rest))

    arrived = {}

    def weights_of(l, group, after):
        if (l, group) in arrived:
            return arrived.pop((l, group))
        tag = f"l{l}_{group}"
        if (l, group) == (0, "in"):
            after = after + (rest_started,)
        groups = (group, "mixer") if (l > 0 and group == "in") else (group,)
        waited = [_allgather_wait(f"l{l}_{g}", *in_flight[l, g][:2], in_flight[l, g][2], after) for g in groups]
        bufs = _allgather_forward(tag, [b for w in waited for b in w])
        for g in groups:
            fw = dict(zip(GATHER_GROUPS[g], bufs[:len(GATHER_GROUPS[g])]))
            bufs = bufs[len(GATHER_GROUPS[g]):]
            if "conv_w_dw" in fw:
                fw["conv_w_dw"] = fw["conv_w_dw"].transpose(1, 0, 2).reshape(CONV_KERNEL, -1)
            arrived[l, g] = fw
        return arrived.pop((l, group))

    pending, small_pending, small_shapes = {}, {}, {}
    tokens = {}

    def on_grads(l, group, grads):
        if group == "small":
            packed = {n: g for n, g in grads.items() if n not in MID}
            small_shapes[l] = {n: g.shape for n, g in packed.items()}
            begun = _allgather_rows_start(f"l{l}", [_pack_rows(list(packed.values()))] + [grads[n] for n in MID])
            small_pending[l], token = begun[:4], begun[4]
        else:
            pending[l, group], token = _reduce_start(f"{l}_{group}", grads)
        tokens[l, group] = token
        return token[0, 0]

    loss, dx, _, _, _ = _local_step(x[0], loss_target[0], weights_of, prm, place, on_grads)
    loss = lax.psum(loss, ("x", "y", "c"))

    reduced = [{} for _ in range(N_LAYERS)]
    out = {}

    def finish(l, groups, after):
        reduced[l].update(_reduce_finish(f"l{l}_{groups[0]}", [pending[l, g] for g in groups], place, after))

    def adamw(l, names, done):
        small = [n for n in names if prm[n][0].size <= SMALL_GRAD_ELEMS]
        for n in names:
            if n not in small:
                out[n] = _adamw_big(n, l, kernel_view(n, prm[n]), kernel_view(n, mom[n]), kernel_view(n, var[n]),
                                    reduced[l][n], out.get(n), after=done)
                done = (out[n][0],)
        if small:
            res = _adamw_small_group("mixer", l, [prm[n] for n in small], [mom[n] for n in small],
                                     [var[n] for n in small], [reduced[l][n] for n in small],
                                     [out.get(n) for n in small], after=done)
            for i, n in enumerate(small):
                out[n] = tuple(res[4 * i:4 * i + 4])
            done = (res[0],)
        return done

    top = N_LAYERS - 1
    done = (tokens[0, "in"], tokens[0, "small"])
    finish(top, ("ffn", "mixer", "in"), done)
    done = adamw(top, BIG, done)
    for groups in (("ffn", "mixer"), ("in",)):
        finish(0, groups, done)
        done = adamw(0, [n for g in groups for n in GATHER_GROUPS[g] if n in BIG], done)
    for n in BIG:
        out[n] = tuple(kernel_view(n, a) for a in out[n])

    gsmall = {}
    mid_mine, mid_gathered = [], []
    for l in range(N_LAYERS):
        mine, lands = _allgather_rows_wait(f"l{l}", *small_pending[l], done[0])
        lands = _allgather_rows_forward(f"l{l}", lands)
        mid_mine.append(mine[1:])
        mid_gathered.append(lands[1:])
        gsum = _sum_devices(f"l{l}", lands[0], mine[0], place)
        for n, g in zip(small_shapes[l], _unpack_rows(gsum, list(small_shapes[l].values()))):
            gsmall.setdefault(n, [None] * N_LAYERS)[l] = g
    mid_out = _adamw_mid([prm[n] for n in MID], [mom[n] for n in MID], [var[n] for n in MID], mid_gathered, mid_mine,
                         place)
    for n, res in zip(MID, mid_out):
        out[n] = tuple(kernel_view(n, a) for a in res)
    gsmall = {n: (g[top] if n == "final_norm" else jnp.stack(g)) for n, g in gsmall.items()}
    lanes = dw_shard.shape[-1]
    gsmall["conv_w_dw"] = lax.dynamic_slice_in_dim(gsmall["conv_w_dw"], (2 * cx + cy) * lanes, lanes, axis=2)
    small_names = [n for n in SMALL if n not in MID] + ["conv_w_dw"]
    w_rows = _pack_rows([prm[n] for n in small_names])
    m_rows = _pack_rows([mom[n] for n in small_names])
    v_rows = _pack_rows([var[n] for n in small_names])
    g_rows = _pack_rows([gsmall[n] for n in small_names])
    shapes = [prm[n].shape for n in small_names]
    d_s, m_s, v_s = (_unpack_rows(r, shapes) for r in _adamw_rows(w_rows, m_rows, v_rows, g_rows))
    for i, n in enumerate(small_names):
        out[n] = (gsmall[n].reshape(prm[n].shape), d_s[i], m_s[i], v_s[i])
    grads = [out[n][0] for n in WEIGHTS]
    deltas = [out[n][1] for n in WEIGHTS]
    new_m = [out[n][2] for n in WEIGHTS]
    new_v = [out[n][3] for n in WEIGHTS]
    return (loss, dx[None], *grads, *deltas, *new_m, *new_v)
```

```python
import math

import jax
import jax.numpy as jnp
from jax import lax
from jax.experimental import pallas as pl
from jax.experimental.pallas import tpu as pltpu

F32 = jnp.float32
MXU_DTYPE = jnp.bfloat16
WIRE_DTYPE = jnp.bfloat16
SDS = jax.ShapeDtypeStruct
BS = pl.BlockSpec
ANY = pl.BlockSpec(memory_space=pl.ANY)
HBM = pl.BlockSpec(memory_space=pltpu.HBM)
SEM = pl.BlockSpec(memory_space=pltpu.SEMAPHORE)
SIDE_EFFECT = pltpu.SideEffectType.DATAFLOW_SIDE_EFFECTING
MESH = pl.DeviceIdType.MESH

EPS = 1e-6
N_CHIPS = 4
N_LAYERS = 2
SSM_GROUPS, SSM_STATE, SSM_GROUP = 32, 64, 16
CONV_KERNEL = 31
CONV_PAD = 32
POOL_WINDOWS = (2, 4, 8, 16)
GELU_C = math.sqrt(2.0 / math.pi)
ADAM_LR, ADAM_B1, ADAM_B2, ADAM_EPS, ADAM_WD, ADAM_STEP = 0.001, 0.9, 0.999, 1e-08, 0.01, 10
VMEM_LIMIT = 56 * 1024 * 1024

BIG = ("w_in", "ssm_w_glu", "ssm_w_proj", "conv_w_proj", "pool_w_proj", "w_out", "ffn_w_gate", "ffn_w_up", "ffn_w_down")
TRANSPOSED = ("ffn_w_gate", "ffn_w_up")
MID = ("ssm_b_re", "ssm_b_im", "ssm_c_re", "ssm_c_im")
GATHER_GROUPS = {
    "in": ("w_in",),
    "mixer": ("ssm_w_glu", "ssm_w_proj", "conv_w_proj", "pool_w_proj", "w_out", "conv_w_dw"),
    "ffn": ("ffn_w_gate", "ffn_w_up", "ffn_w_down"),
}
SMALL = ("norm1", "b_gate", "ssm_a_re", "ssm_a_im", "ssm_log_dt", "ssm_b_re", "ssm_b_im", "ssm_c_re", "ssm_c_im",
         "ssm_d", "ssm_b_glu", "conv_b_dw", "conv_ln_g", "conv_ln_b", "pool_w_group", "pool_scale", "norm2",
         "final_norm")
WEIGHTS = ("norm1", "w_in", "b_gate", "ssm_a_re", "ssm_a_im", "ssm_log_dt", "ssm_b_re", "ssm_b_im", "ssm_c_re",
           "ssm_c_im", "ssm_d", "ssm_w_glu", "ssm_b_glu", "ssm_w_proj", "conv_w_dw", "conv_b_dw", "conv_ln_g",
           "conv_ln_b", "conv_w_proj", "pool_w_group", "pool_scale", "pool_w_proj", "w_out", "norm2", "ffn_w_gate",
           "ffn_w_up", "ffn_w_down", "final_norm")


def _params():
    return pltpu.CompilerParams(vmem_limit_bytes=VMEM_LIMIT)


def _mm(a, b):
    return jnp.dot(a.astype(MXU_DTYPE), b.astype(MXU_DTYPE), preferred_element_type=F32)


def _mm_nt(a, b):
    return lax.dot_general(a.astype(MXU_DTYPE), b.astype(MXU_DTYPE), (((1,), (1,)), ((), ())),
                           preferred_element_type=F32)


def _mm_tn(a, b):
    return lax.dot_general(a.astype(MXU_DTYPE), b.astype(MXU_DTYPE), (((0,), (0,)), ((), ())),
                           preferred_element_type=F32)


def _sigmoid(x):
    return jax.nn.sigmoid(x)


def _gelu(x):
    t = jnp.tanh(GELU_C * (x + 0.044715 * (x * x * x)))
    return x * (0.5 * (1.0 + t)), t


def _gelu_grad(x, t):
    return 0.5 * (1.0 + t) + 0.5 * x * (1.0 - t * t) * (GELU_C * (1.0 + 3.0 * 0.044715 * x * x))


def _colsum(v):
    return jnp.sum(v, axis=0, keepdims=True)


def _row_tile(rows, cols, itemsize=4, budget=1536 * 1024):
    best = None
    for t in range(8, rows + 1, 8):
        if rows % t == 0 and t * cols * itemsize <= budget:
            best = t
    return best if best is not None else rows


def _in_proj(l, x, norm1, w_in):
    s, d = x.shape
    nc = w_in.shape[-1]
    tm = min(1024, s)
    nt = s // tm

    def body(x_ref, g_ref, w_ref, z_ref, h_ref, h_all):
        i = pl.program_id(1)
        rows = pl.ds(pl.multiple_of(i * tm, tm), tm)

        @pl.when(pl.program_id(0) == 0)
        def _():
            xv = x_ref[...]
            r = lax.rsqrt(jnp.mean(xv * xv, axis=-1, keepdims=True) + EPS)
            hv = (xv * r * g_ref[...]).astype(h_ref.dtype)
            h_ref[...] = hv.T
            h_all[rows, :] = hv

        z_ref[...] = _mm(h_all[rows, :], w_ref[...])

    tile_of = lambda j, i: i * (1 - jnp.minimum(j, 1)) + (nt - 1) * jnp.minimum(j, 1)
    return pl.pallas_call(
        body, name=f"in_proj_l{l}", grid=(N_CHIPS, nt),
        in_specs=[BS((tm, d), lambda j, i: (tile_of(j, i), 0)), BS((None, 1, d), lambda j, i: (l, 0, 0)),
                  BS((None, d, nc), lambda j, i: (j, 0, 0))],
        out_specs=[BS((tm, nc), lambda j, i: (i, j)), BS((d, tm), lambda j, i: (0, tile_of(j, i)))],
        out_shape=[SDS((s, N_CHIPS * nc), F32), SDS((d, s), MXU_DTYPE)],
        scratch_shapes=[pltpu.VMEM((s, d), MXU_DTYPE)], compiler_params=_params())(x, norm1, w_in)


def _mm_cols(a, w_ref):
    return jnp.concatenate([_mm(a, w_ref[j]) for j in range(N_CHIPS)], axis=1)


def _mm_nt_cols(dv, w_ref):
    nc = w_ref.shape[-1]
    acc = _mm_nt(dv[:, 0:nc], w_ref[0])
    for j in range(1, N_CHIPS):
        acc = acc + _mm_nt(dv[:, j * nc:(j + 1) * nc], w_ref[j])
    return acc


def _merge_values(y, hc, p, zg, wglu, bglu, wpa, wpb, wpc, lng, lnb, wgrp, scale, bg):
    v = {}
    ge, th = _gelu(y)
    t = _mm(ge, wglu) + bglu
    sg = _sigmoid(t)
    sa = ge * sg
    ya = _mm_cols(sa, wpa)
    mu = jnp.mean(hc, axis=-1, keepdims=True)
    xc = hc - mu
    r = lax.rsqrt(jnp.mean(xc * xc, axis=-1, keepdims=True) + EPS)
    xh = xc * r
    ln = xh * lng + lnb
    sl = _sigmoid(ln)
    ac = ln * sl
    yb = _mm_cols(ac, wpb)
    gw = p.shape[1] // len(POOL_WINDOWS)
    q = jnp.concatenate([_mm(p[:, k * gw:(k + 1) * gw], wgrp[k]) for k in range(len(POOL_WINDOWS))], axis=1)
    pp = q * scale
    yc = _mm_cols(pp, wpc)
    d = ya.shape[1]
    gates = [_sigmoid(zg[k] + bg[:, k * d:(k + 1) * d]) for k in range(3)]
    merged = gates[0] * ya + gates[1] * yb + gates[2] * yc
    v.update(ge=ge, th=th, sg=sg, sa=sa, ya=ya, r=r, xh=xh, ln=ln, sl=sl, ac=ac, yb=yb, q=q, pp=pp, yc=yc,
             gates=gates, merged=merged)
    return v


def _merge_specs(l, tm, d, cw):
    row = lambda n: BS((None, 1, n), lambda i: (l, 0, 0))
    resident = lambda shp: BS(shp, lambda i: (0, 0, 0), pipeline_mode=pl.Buffered(1))
    return [
        BS((tm, cw), lambda i: (i, 0)),
        BS((tm, cw), lambda i: (i, 0)),
        BS((tm, cw), lambda i: (i, 0)),
        BS((tm, d), lambda i: (i, 2)), BS((tm, d), lambda i: (i, 3)), BS((tm, d), lambda i: (i, 4)),
        resident((N_CHIPS, cw // N_CHIPS, cw)),
        row(cw),
        resident((N_CHIPS, cw, d // N_CHIPS)),
        resident((N_CHIPS, cw, d // N_CHIPS)),
        resident((N_CHIPS, cw, d // N_CHIPS)),
        row(cw), row(cw),
        BS((None, 4, cw // 4, cw // 4), lambda i: (l, 0, 0, 0)),
        row(cw),
        row(3 * d),
        resident((N_CHIPS, d // N_CHIPS, d)),
    ]


def _merge_fwd(l, x, y, hc, p, z, fw, sp):
    s, d = x.shape
    cw = y.shape[1]
    tm = min(512, s)

    def body(x_ref, y_ref, hc_ref, p_ref, z0, z1, z2, wglu, bglu, wpa, wpb, wpc, lng, lnb, wgrp, scale, bg, wout,
             x1_ref):
        v = _merge_values(y_ref[...], hc_ref[...], p_ref[...], (z0[...], z1[...], z2[...]),
                          wglu[...].reshape(cw, cw), bglu[...], wpa, wpb, wpc, lng[...], lnb[...], wgrp, scale[...],
                          bg[...])
        x1_ref[...] = x_ref[...] + _mm(v["merged"], wout[...].reshape(d, d))

    return pl.pallas_call(
        body, name=f"merge_fwd_l{l}", grid=(s // tm,),
        in_specs=[BS((tm, d), lambda i: (i, 0))] + _merge_specs(l, tm, d, cw),
        out_specs=BS((tm, d), lambda i: (i, 0)), out_shape=SDS((s, d), F32), compiler_params=_params(),
    )(x, y, hc, p, z, z, z, fw["ssm_w_glu"], sp["ssm_b_glu"], fw["ssm_w_proj"], fw["conv_w_proj"], fw["pool_w_proj"],
      sp["conv_ln_g"], sp["conv_ln_b"], sp["pool_w_group"], sp["pool_scale"], sp["b_gate"], fw["w_out"])


def _merge_bwd(l, dx1, y, hc, p, z, fw, sp):
    s, d = dx1.shape
    cw = y.shape[1]
    tm = min(256, s)
    m = MXU_DTYPE

    def body(dx1_ref, y_ref, hc_ref, p_ref, z0, z1, z2, wglu, bglu, wpa, wpb, wpc, lng, lnb, wgrp, scale, bg, wout,
             dzg_ref, dy_ref, dhc_ref, dp_ref, merged_ref, sa_ref, ac_ref, pp_ref, ge_ref, dt_ref, dya_ref, dyb_ref,
             dyc_ref, dq_ref, dbg_ref, dbglu_ref, dlng_ref, dlnb_ref, dscale_ref):
        yv = y_ref[...]
        wg = wglu[...].reshape(cw, cw)
        v = _merge_values(yv, hc_ref[...], p_ref[...], (z0[...], z1[...], z2[...]), wg, bglu[...], wpa, wpb, wpc,
                          lng[...], lnb[...], wgrp, scale[...], bg[...])
        dm = _mm_nt(dx1_ref[...], wout[...].reshape(d, d))
        ys = (v["ya"], v["yb"], v["yc"])
        dys, dbg = [], []
        for k in range(3):
            gk = v["gates"][k]
            dzk = dm * ys[k] * (gk * (1.0 - gk))
            dbg.append(_colsum(dzk))
            dzg_ref[:, k * d:(k + 1) * d] = dzk.astype(m)
            dys.append((dm * gk).astype(m))
        dsa = _mm_nt_cols(dys[0], wpa)
        dac = _mm_nt_cols(dys[1], wpb)
        dpp = _mm_nt_cols(dys[2], wpc)
        ge, sg = v["ge"], v["sg"]
        dt = dsa * ge * (sg * (1.0 - sg))
        dge = dsa * sg + _mm_nt(dt, wg)
        dy_ref[...] = dge * _gelu_grad(yv, v["th"])
        ln, sl, xh = v["ln"], v["sl"], v["xh"]
        dln = dac * (sl * (1.0 + ln * (1.0 - sl)))
        dxh = dln * lng[...]
        dhc_ref[...] = v["r"] * (dxh - jnp.mean(dxh, axis=-1, keepdims=True)
                                 - xh * jnp.mean(dxh * xh, axis=-1, keepdims=True))
        dq = dpp * scale[...]
        gw = cw // len(POOL_WINDOWS)
        for k in range(len(POOL_WINDOWS)):
            dp_ref[:, k * gw:(k + 1) * gw] = _mm_nt(dq[:, k * gw:(k + 1) * gw], wgrp[k])
        merged_ref[...] = v["merged"].astype(m)
        sa_ref[...] = v["sa"].astype(m)
        ac_ref[...] = v["ac"].astype(m)
        pp_ref[...] = v["pp"].astype(m)
        ge_ref[...] = ge.astype(m)
        dt_ref[...] = dt.astype(m)
        dya_ref[...] = dys[0]
        dyb_ref[...] = dys[1]
        dyc_ref[...] = dys[2]
        dq_ref[...] = dq.astype(m)

        @pl.when(pl.program_id(0) == 0)
        def _():
            for ref in (dbg_ref, dbglu_ref, dlng_ref, dlnb_ref, dscale_ref):
                ref[...] = jnp.zeros(ref.shape, F32)

        dbg_ref[...] += jnp.concatenate(dbg, axis=1)
        dbglu_ref[...] += _colsum(dt)
        dlng_ref[...] += _colsum(dln * xh)
        dlnb_ref[...] += _colsum(dln)
        dscale_ref[...] += _colsum(dpp * v["q"])

    tile = lambda n: BS((tm, n), lambda i: (i, 0))
    acc = lambda n: BS((1, n), lambda i: (0, 0))
    outs = pl.pallas_call(
        body, name=f"merge_bwd_l{l}", grid=(s // tm,),
        in_specs=[tile(d)] + _merge_specs(l, tm, d, cw),
        out_specs=[tile(3 * d), tile(cw), tile(cw), tile(cw), tile(d), tile(cw), tile(cw), tile(cw), tile(cw), tile(cw),
                   tile(d), tile(d), tile(d), tile(cw), acc(3 * d), acc(cw), acc(cw), acc(cw), acc(cw)],
        out_shape=[SDS((s, 3 * d), m), SDS((s, cw), F32), SDS((s, cw), F32), SDS((s, cw), F32), SDS((s, d), m),
                   SDS((s, cw), m), SDS((s, cw), m), SDS((s, cw), m), SDS((s, cw), m), SDS((s, cw), m), SDS((s, d), m),
                   SDS((s, d), m), SDS((s, d), m), SDS((s, cw), m), SDS((1, 3 * d), F32), SDS((1, cw), F32),
                   SDS((1, cw), F32), SDS((1, cw), F32), SDS((1, cw), F32)],
        compiler_params=_params(),
    )(dx1, y, hc, p, z, z, z, fw["ssm_w_glu"], sp["ssm_b_glu"], fw["ssm_w_proj"], fw["conv_w_proj"], fw["pool_w_proj"],
      sp["conv_ln_g"], sp["conv_ln_b"], sp["pool_w_group"], sp["pool_scale"], sp["b_gate"], fw["w_out"])
    names = ("dzg", "dy", "dhc", "dp", "merged", "sa", "ac", "pp", "ge", "dt", "dya", "dyb", "dyc", "dq", "db_gate",
             "db_glu", "dln_g", "dln_b", "dscale")
    return dict(zip(names, outs))


def _ffn_fwd(l, x1, norm2, wg, wu, wd):
    s, d = x1.shape
    hc = wd.shape[1]
    tm = min(512, s)

    def body(x_ref, g_ref, wg_ref, wu_ref, wd_ref, o_ref, gate_ref, up_ref, h_ref):
        @pl.when(pl.program_id(1) == 0)
        def _():
            xv = x_ref[...]
            r = lax.rsqrt(jnp.mean(xv * xv, axis=-1, keepdims=True) + EPS)
            h_ref[...] = (xv * r * g_ref[...]).astype(h_ref.dtype)
            o_ref[...] = xv

        h = h_ref[...]
        gate = _mm_nt(h, wg_ref[...])
        up = _mm_nt(h, wu_ref[...])
        gate_ref[...] = gate
        up_ref[...] = up
        o_ref[...] += _mm(gate * _sigmoid(gate) * up, wd_ref[...])

    chunk = BS((None, tm, hc), lambda i, j: (j, i, 0))
    return pl.pallas_call(
        body, name=f"ffn_fwd_l{l}", grid=(s // tm, N_CHIPS),
        in_specs=[BS((tm, d), lambda i, j: (i, 0)), BS((None, 1, d), lambda i, j: (l, 0, 0)),
                  BS((None, hc, d), lambda i, j: (j, 0, 0)), BS((None, hc, d), lambda i, j: (j, 0, 0)),
                  BS((None, hc, d), lambda i, j: (j, 0, 0))],
        out_specs=[BS((tm, d), lambda i, j: (i, 0)), chunk, chunk, BS((tm, d), lambda i, j: (i, 0))],
        out_shape=[SDS((s, d), F32), SDS((N_CHIPS, s, hc), F32), SDS((N_CHIPS, s, hc), F32), SDS((s, d), MXU_DTYPE)],
        compiler_params=_params())(x1, norm2, wg, wu, wd)


def _ffn_bwd(l, x1, dx2, gate_pre, up_pre, norm2, wg, wu, wd):
    s, d = x1.shape
    hc = wd.shape[1]
    tm = min(512, s)
    m = MXU_DTYPE
    last = N_CHIPS - 1

    def body(x_ref, dx2_ref, gate_ref, up_ref, g_ref, wg_ref, wu_ref, wd_ref, dx1_ref, dxb_ref, act_ref, dgate_ref,
             dup_ref, dn_ref, dh_scr):
        i, j = pl.program_id(0), pl.program_id(1)

        @pl.when(j == 0)
        def _():
            dxb_ref[...] = dx2_ref[...].astype(m)
            dh_scr[...] = jnp.zeros(dh_scr.shape, F32)

        @pl.when((i == 0) & (j == 0))
        def _():
            dn_ref[...] = jnp.zeros(dn_ref.shape, F32)

        gate = gate_ref[...]
        up = up_ref[...]
        sg = _sigmoid(gate)
        silu = gate * sg
        act_ref[...] = (silu * up).astype(m).T
        dact = _mm_nt(dxb_ref[...], wd_ref[...])
        dup = (dact * silu).astype(m)
        dgate = (dact * up * (sg * (1.0 + gate * (1.0 - sg)))).astype(m)
        dup_ref[...] = dup.T
        dgate_ref[...] = dgate.T
        dh_scr[...] += _mm(dgate, wg_ref[...]) + _mm(dup, wu_ref[...])

        @pl.when(j == last)
        def _():
            xv = x_ref[...]
            r = lax.rsqrt(jnp.mean(xv * xv, axis=-1, keepdims=True) + EPS)
            xh = xv * r
            dh = dh_scr[...]
            dn_ref[...] += _colsum(dh * xh)
            dxh = dh * g_ref[...]
            dx1_ref[...] = dx2_ref[...] + r * (dxh - xh * jnp.mean(dxh * xh, axis=-1, keepdims=True))

    chunk = BS((None, hc, tm), lambda i, j: (j, 0, i))
    saved = BS((None, tm, hc), lambda i, j: (j, i, 0))
    outs = pl.pallas_call(
        body, name=f"ffn_bwd_l{l}", grid=(s // tm, N_CHIPS),
        in_specs=[BS((tm, d), lambda i, j: (i, 0)), BS((tm, d), lambda i, j: (i, 0)), saved, saved,
                  BS((None, 1, d), lambda i, j: (l, 0, 0)),
                  BS((None, hc, d), lambda i, j: (j, 0, 0)), BS((None, hc, d), lambda i, j: (j, 0, 0)),
                  BS((None, hc, d), lambda i, j: (j, 0, 0))],
        out_specs=[BS((tm, d), lambda i, j: (i, 0)), BS((tm, d), lambda i, j: (i, 0)),
                   chunk, chunk, chunk, BS((1, d), lambda i, j: (0, 0))],
        out_shape=[SDS((s, d), F32), SDS((s, d), m), SDS((N_CHIPS, hc, s), m),
                   SDS((N_CHIPS, hc, s), m), SDS((N_CHIPS, hc, s), m), SDS((1, d), F32)],
        scratch_shapes=[pltpu.VMEM((tm, d), F32)], compiler_params=_params(),
    )(x1, dx2, gate_pre, up_pre, norm2, wg, wu, wd)
    return dict(zip(("dx1", "dx2", "act", "dgate", "dup", "dnorm2"), outs))


def _loss_head(x, target, gf):
    s, d = x.shape
    tm = min(512, s)

    def body(x_ref, t_ref, g_ref, dx_ref, loss_ref, dg_ref):
        @pl.when(pl.program_id(0) == 0)
        def _():
            loss_ref[...] = jnp.zeros(loss_ref.shape, F32)
            dg_ref[...] = jnp.zeros(dg_ref.shape, F32)

        xv = x_ref[...]
        r = lax.rsqrt(jnp.mean(xv * xv, axis=-1, keepdims=True) + EPS)
        xh = xv * r
        err = xh * g_ref[...] - t_ref[...]
        loss_ref[...] += 0.5 * jnp.sum(jnp.mean(err * err, axis=-1, keepdims=True), axis=0, keepdims=True)
        dyv = err * (1.0 / d)
        dg_ref[...] += _colsum(dyv * xh)
        dxh = dyv * g_ref[...]
        dx_ref[...] = r * (dxh - xh * jnp.mean(dxh * xh, axis=-1, keepdims=True))

    return pl.pallas_call(
        body, name="loss_head", grid=(s // tm,),
        in_specs=[BS((tm, d), lambda i: (i, 0)), BS((tm, d), lambda i: (i, 0)), BS((1, d), lambda i: (0, 0))],
        out_specs=[BS((tm, d), lambda i: (i, 0)), BS((1, 1), lambda i: (0, 0)), BS((1, d), lambda i: (0, 0))],
        out_shape=[SDS((s, d), F32), SDS((1, 1), F32), SDS((1, d), F32)], compiler_params=_params())(x, target, gf)


def _in_proj_bwd(l, dres, x, norm1, w_in, du_a, dv1, dv2, du_c, dzg):
    s, d = x.shape
    nc = w_in.shape[-1]
    tm = min(512, s)
    m = MXU_DTYPE

    def body(dres_ref, x_ref, g_ref, w_ref, a_ref, b1_ref, b2_ref, c_ref, g3_ref, dx_ref, dz_ref, dn_ref):
        @pl.when(pl.program_id(0) == 0)
        def _():
            dn_ref[...] = jnp.zeros(dn_ref.shape, F32)

        dz = jnp.concatenate([a_ref[...], b1_ref[...], b2_ref[...], c_ref[...], g3_ref[...]], axis=1).astype(m)
        dz_ref[...] = dz
        dh = _mm_nt_cols(dz, w_ref)
        xv = x_ref[...]
        r = lax.rsqrt(jnp.mean(xv * xv, axis=-1, keepdims=True) + EPS)
        xh = xv * r
        dn_ref[...] += _colsum(dh * xh)
        dxh = dh * g_ref[...]
        dx_ref[...] = dres_ref[...] + r * (dxh - xh * jnp.mean(dxh * xh, axis=-1, keepdims=True))

    tile = lambda n: BS((tm, n), lambda i: (i, 0))
    return pl.pallas_call(
        body, name=f"in_proj_bwd_l{l}", grid=(s // tm,),
        in_specs=[tile(d), tile(d), BS((None, 1, d), lambda i: (l, 0, 0)),
                  BS((N_CHIPS, d, nc), lambda i: (0, 0, 0), pipeline_mode=pl.Buffered(1)),
                  tile(du_a.shape[1]), tile(dv1.shape[1]), tile(dv2.shape[1]), tile(du_c.shape[1]), tile(dzg.shape[1])],
        out_specs=[tile(d), tile(N_CHIPS * nc), BS((1, d), lambda i: (0, 0))],
        out_shape=[SDS((s, d), F32), SDS((s, N_CHIPS * nc), m), SDS((1, d), F32)], compiler_params=_params(),
    )(dres, x, norm1, w_in, du_a, dv1, dv2, du_c, dzg)


def _tn_matmul(name, a, a_spec, b, b_spec, chunk_shape, grid, place):
    last = grid[1] - 1

    def body(place_ref, a_ref, b_ref, own_ref, wire_ref, *acc):
        part = _mm(a_ref[...], b_ref[...])

        def emit(total):
            wire_ref[...] = total.astype(WIRE_DTYPE)

            @pl.when(pl.program_id(0) == place_ref[0])
            def _():
                own_ref[...] = total

        if last == 0:
            emit(part)
        else:
            @pl.when(pl.program_id(1) == 0)
            def _():
                acc[0][...] = part

            @pl.when(pl.program_id(1) > 0)
            def _():
                acc[0][...] += part

            @pl.when(pl.program_id(1) == last)
            def _():
                emit(acc[0][...])

    zeros = (0,) * len(chunk_shape)
    return pl.pallas_call(
        body, name=name,
        grid_spec=pltpu.PrefetchScalarGridSpec(
            num_scalar_prefetch=1, grid=grid, in_specs=[a_spec, b_spec],
            out_specs=[BS(chunk_shape, lambda j, t, pr: zeros), BS((None,) + chunk_shape, lambda j, t, pr: (j,) + zeros)],
            scratch_shapes=[pltpu.VMEM(chunk_shape, F32)] if last else []),
        out_shape=[SDS(chunk_shape, F32), SDS((N_CHIPS,) + chunk_shape, WIRE_DTYPE)],
        compiler_params=_params())(place, a, b)


def _scan_consts(pw_ref, lanes, reverse):
    sgn = -1.0 if reverse else 1.0
    row = lax.broadcasted_iota(jnp.int32, (8, lanes), 0)
    steps = []
    for i, k in enumerate((1, 2, 4)):
        mask = (row < 8 - k) if reverse else (row >= k)
        steps.append((k, jnp.where(mask, pw_ref[2 * i], 0.0), jnp.where(mask, sgn * pw_ref[2 * i + 1], 0.0)))
    c = 4 if reverse else 3
    return steps, pw_ref[2 * c], sgn * pw_ref[2 * c + 1]


def _scan_block(br, bi, steps, reverse):
    for k, ar, ai in steps:
        sh = 8 - k if reverse else k
        sr = pltpu.roll(br, sh, 0)
        si = pltpu.roll(bi, sh, 0)
        br, bi = br + ar * sr - ai * si, bi + ar * si + ai * sr
    return br, bi


def _ssm_fwd(l, z, bblk_re, bblk_im, cblk_re, cblk_im, pw, dskip):
    s = z.shape[0]
    gc = bblk_re.shape[1]
    gl = bblk_re.shape[2]
    nblk = bblk_re.shape[0]

    def body(u_ref, bre, bim, cre, cim, pw_ref, d_ref, hre, him, y_ref):
        u = u_ref[...]
        hre[...] = _mm(u, bre[...])
        him[...] = _mm(u, bim[...])
        steps, car, cai = _scan_consts(pw_ref, gl, False)

        def step(i, carry):
            cr, ci = carry
            r0 = pl.multiple_of(i * 8, 8)
            br, bi = _scan_block(hre[pl.ds(r0, 8), :], him[pl.ds(r0, 8), :], steps, False)
            hr = br + car * cr - cai * ci
            hi = bi + car * ci + cai * cr
            hre[pl.ds(r0, 8), :] = hr
            him[pl.ds(r0, 8), :] = hi
            return jnp.broadcast_to(hr[7:8, :], (8, gl)), jnp.broadcast_to(hi[7:8, :], (8, gl))

        zero = jnp.zeros((8, gl), F32)
        lax.fori_loop(0, s // 8, step, (zero, zero))
        y_ref[...] = _mm_nt(hre[...], cre[...]) - _mm_nt(him[...], cim[...]) + d_ref[...] * u

    return pl.pallas_call(
        body, name=f"ssm_fwd_l{l}", grid=(nblk,),
        in_specs=[BS((s, gc), lambda k: (0, k)), BS((None, gc, gl), lambda k: (k, 0, 0)),
                  BS((None, gc, gl), lambda k: (k, 0, 0)), BS((None, gc, gl), lambda k: (k, 0, 0)),
                  BS((None, gc, gl), lambda k: (k, 0, 0)), BS((10, 8, gl), lambda k: (0, 0, k)),
                  BS((1, gc), lambda k: (0, k))],
        out_specs=[BS((s, gl), lambda k: (0, k)), BS((s, gl), lambda k: (0, k)), BS((s, gc), lambda k: (0, k))],
        out_shape=[SDS((s, nblk * gl), F32), SDS((s, nblk * gl), F32), SDS((s, nblk * gc), F32)],
        compiler_params=_params())(z, bblk_re, bblk_im, cblk_re, cblk_im, pw, dskip)


def _ssm_bwd(l, dy, z, hre, him, bblk_re, bblk_im, cblk_re, cblk_im, pw, dskip):
    s = z.shape[0]
    nblk, gc, gl = bblk_re.shape

    def body(dy_ref, u_ref, hre_ref, him_ref, bre, bim, cre, cim, pw_ref, d_ref,
             du_ref, dbre_ref, dbim_ref, dcre_ref, dcim_ref, dar_ref, dai_ref, dd_ref, gre, gim):
        dyv = dy_ref[...]
        u = u_ref[...]
        gre[...] = _mm(dyv, cre[...])
        gim[...] = -_mm(dyv, cim[...])
        dcre_ref[...] = _mm_tn(dyv, hre_ref[...])
        dcim_ref[...] = -_mm_tn(dyv, him_ref[...])
        dd_ref[...] = _colsum(dyv * u)
        row = lax.broadcasted_iota(jnp.int32, (8, gl), 0)
        steps, car, cai = _scan_consts(pw_ref, gl, True)
        n8 = s // 8

        def step(ii, carry):
            cr, ci, accr, acci = carry
            i = n8 - 1 - ii
            r0 = pl.multiple_of(i * 8, 8)
            br, bi = _scan_block(gre[pl.ds(r0, 8), :], gim[pl.ds(r0, 8), :], steps, True)
            dr = br + car * cr - cai * ci
            di = bi + car * ci + cai * cr
            gre[pl.ds(r0, 8), :] = dr
            gim[pl.ds(r0, 8), :] = di
            rp = pl.multiple_of(jnp.maximum(i - 1, 0) * 8, 8)
            keep = jnp.where(i > 0, 1.0, 0.0)
            pr = jnp.where(row >= 1, pltpu.roll(hre_ref[pl.ds(r0, 8), :], 1, 0),
                           keep * pltpu.roll(hre_ref[pl.ds(rp, 8), :], 1, 0))
            pi = jnp.where(row >= 1, pltpu.roll(him_ref[pl.ds(r0, 8), :], 1, 0),
                           keep * pltpu.roll(him_ref[pl.ds(rp, 8), :], 1, 0))
            accr = accr + dr * pr + di * pi
            acci = acci + di * pr - dr * pi
            return (jnp.broadcast_to(dr[0:1, :], (8, gl)), jnp.broadcast_to(di[0:1, :], (8, gl)), accr, acci)

        zero = jnp.zeros((8, gl), F32)
        _, _, accr, acci = lax.fori_loop(0, n8, step, (zero, zero, zero, zero))
        dar_ref[...] = _colsum(accr)
        dai_ref[...] = _colsum(acci)
        dbr = gre[...]
        dbi = gim[...]
        du_ref[...] = (dyv * d_ref[...] + _mm_nt(dbr, bre[...]) + _mm_nt(dbi, bim[...])).astype(du_ref.dtype)
        dbre_ref[...] = _mm_tn(u, dbr)
        dbim_ref[...] = _mm_tn(u, dbi)

    col = lambda n: BS((s, n), lambda k: (0, k))
    blk = lambda a, b: BS((None, a, b), lambda k: (k, 0, 0))
    outs = pl.pallas_call(
        body, name=f"ssm_bwd_l{l}", grid=(nblk,),
        in_specs=[col(gc), col(gc), col(gl), col(gl), blk(gc, gl), blk(gc, gl), blk(gc, gl), blk(gc, gl),
                  BS((10, 8, gl), lambda k: (0, 0, k)), BS((1, gc), lambda k: (0, k))],
        out_specs=[col(gc), blk(gc, gl), blk(gc, gl), blk(gc, gl), blk(gc, gl), BS((1, gl), lambda k: (0, k)),
                   BS((1, gl), lambda k: (0, k)), BS((1, gc), lambda k: (0, k))],
        out_shape=[SDS((s, nblk * gc), MXU_DTYPE), SDS((nblk, gc, gl), F32), SDS((nblk, gc, gl), F32),
                   SDS((nblk, gc, gl), F32), SDS((nblk, gc, gl), F32), SDS((1, nblk * gl), F32),
                   SDS((1, nblk * gl), F32), SDS((1, nblk * gc), F32)],
        scratch_shapes=[pltpu.VMEM((s, gl), F32), pltpu.VMEM((s, gl), F32)], compiler_params=_params(),
    )(dy, z, hre, him, bblk_re, bblk_im, cblk_re, cblk_im, pw, dskip)
    return dict(zip(("du", "dbblk_re", "dbblk_im", "dcblk_re", "dcblk_im", "dabar_re", "dabar_im", "dd"), outs))


def _conv_fwd(l, z, wdw, bdw):
    s = z.shape[0]
    cw = wdw.shape[1]
    lb = 128
    tr = min(256, s)
    off1 = cw // lb
    off2 = 2 * cw // lb

    def body(v1_ref, v2_ref, w_ref, b_ref, hc_ref, scr):
        scr[0:CONV_PAD, :] = jnp.zeros((CONV_PAD, lb), F32)
        scr[CONV_PAD:, :] = v1_ref[...] * _sigmoid(v2_ref[...])
        for t in range(s // tr):
            acc = jnp.broadcast_to(b_ref[...], (tr, lb))
            for k in range(CONV_KERNEL):
                acc = acc + w_ref[pl.ds(k, 1), :] * scr[pl.ds(t * tr + CONV_PAD - (CONV_KERNEL - 1) + k, tr), :]
            hc_ref[pl.ds(t * tr, tr), :] = acc

    return pl.pallas_call(
        body, name=f"conv_fwd_l{l}", grid=(cw // lb,),
        in_specs=[BS((s, lb), lambda k: (0, off1 + k)), BS((s, lb), lambda k: (0, off2 + k)),
                  BS((CONV_KERNEL, lb), lambda k: (0, k)), BS((1, lb), lambda k: (0, k))],
        out_specs=BS((s, lb), lambda k: (0, k)), out_shape=SDS((s, cw), F32),
        scratch_shapes=[pltpu.VMEM((s + CONV_PAD, lb), F32)], compiler_params=_params())(z, z, wdw, bdw)


def _conv_bwd(l, dhc, z, wdw):
    s = z.shape[0]
    cw = wdw.shape[1]
    lb = 128
    tr = min(256, s)
    off1 = cw // lb
    off2 = 2 * cw // lb
    nb = cw // lb

    def body(d_ref, v1_ref, v2_ref, w_ref, dv1_ref, dv2_ref, dw_ref, db_ref, hpad, dpad):
        v1 = v1_ref[...]
        sg = _sigmoid(v2_ref[...])
        dv = d_ref[...]
        hpad[0:CONV_PAD, :] = jnp.zeros((CONV_PAD, lb), F32)
        hpad[CONV_PAD:, :] = v1 * sg
        dpad[0:s, :] = dv
        dpad[s:, :] = jnp.zeros((CONV_PAD, lb), F32)
        db_ref[...] = _colsum(dv)
        dws = [jnp.zeros((1, lb), F32) for _ in range(CONV_KERNEL)]
        for t in range(s // tr):
            dt = d_ref[pl.ds(t * tr, tr), :]
            acc = jnp.zeros((tr, lb), F32)
            for k in range(CONV_KERNEL):
                acc = acc + w_ref[pl.ds(k, 1), :] * dpad[pl.ds(t * tr + (CONV_KERNEL - 1) - k, tr), :]
                dws[k] = dws[k] + _colsum(dt * hpad[pl.ds(t * tr + CONV_PAD - (CONV_KERNEL - 1) + k, tr), :])
            sgt = _sigmoid(v2_ref[pl.ds(t * tr, tr), :])
            v1t = v1_ref[pl.ds(t * tr, tr), :]
            dv1_ref[pl.ds(t * tr, tr), :] = (acc * sgt).astype(dv1_ref.dtype)
            dv2_ref[pl.ds(t * tr, tr), :] = (acc * v1t * (sgt * (1.0 - sgt))).astype(dv2_ref.dtype)
        for k in range(CONV_KERNEL):
            dw_ref[pl.ds(k, 1), :] = dws[k]

    return pl.pallas_call(
        body, name=f"conv_bwd_l{l}", grid=(nb,),
        in_specs=[BS((s, lb), lambda k: (0, k)), BS((s, lb), lambda k: (0, off1 + k)),
                  BS((s, lb), lambda k: (0, off2 + k)), BS((CONV_KERNEL, lb), lambda k: (0, k))],
        out_specs=[BS((s, lb), lambda k: (0, k)), BS((s, lb), lambda k: (0, k)),
                   BS((CONV_KERNEL, lb), lambda k: (0, k)), BS((1, lb), lambda k: (0, k))],
        out_shape=[SDS((s, cw), MXU_DTYPE), SDS((s, cw), MXU_DTYPE), SDS((CONV_KERNEL, cw), F32), SDS((1, cw), F32)],
        scratch_shapes=[pltpu.VMEM((s + CONV_PAD, lb), F32), pltpu.VMEM((s + CONV_PAD, lb), F32)],
        compiler_params=_params())(dhc, z, z, wdw)


def _pool_window(k):
    return jnp.where(k == 0, float(POOL_WINDOWS[0]),
                     jnp.where(k == 1, float(POOL_WINDOWS[1]),
                               jnp.where(k == 2, float(POOL_WINDOWS[2]), float(POOL_WINDOWS[3]))))


def _pool_fwd(l, z, pw_width):
    s = z.shape[0]
    lb = pw_width // len(POOL_WINDOWS)
    off = 3 * pw_width // lb

    def body(u_ref, p_ref):
        k = pl.program_id(0)
        u = u_ref[...]
        row = lax.broadcasted_iota(jnp.int32, (s, lb), 0)
        sums = [u]
        for sh in (1, 2, 4, 8):
            prev = sums[-1]
            sums.append(prev + jnp.where(row >= sh, pltpu.roll(prev, sh, 0), 0.0))
        sel = jnp.where(k == 0, sums[1], jnp.where(k == 1, sums[2], jnp.where(k == 2, sums[3], sums[4])))
        cnt = jnp.minimum((row + 1).astype(F32), _pool_window(k))
        p_ref[...] = sel / cnt - u

    return pl.pallas_call(
        body, name=f"pool_fwd_l{l}", grid=(len(POOL_WINDOWS),),
        in_specs=[BS((s, lb), lambda k: (0, off + k))], out_specs=BS((s, lb), lambda k: (0, k)),
        out_shape=SDS((s, pw_width), F32), compiler_params=_params())(z)


def _pool_bwd(l, dp):
    s, width = dp.shape
    lb = width // len(POOL_WINDOWS)

    def body(d_ref, du_ref):
        k = pl.program_id(0)
        dv = d_ref[...]
        row = lax.broadcasted_iota(jnp.int32, (s, lb), 0)
        cnt = jnp.minimum((row + 1).astype(F32), _pool_window(k))
        sums = [dv / cnt]
        for sh in (1, 2, 4, 8):
            prev = sums[-1]
            sums.append(prev + jnp.where(row < s - sh, pltpu.roll(prev, s - sh, 0), 0.0))
        sel = jnp.where(k == 0, sums[1], jnp.where(k == 1, sums[2], jnp.where(k == 2, sums[3], sums[4])))
        du_ref[...] = (sel - dv).astype(du_ref.dtype)

    return pl.pallas_call(
        body, name=f"pool_bwd_l{l}", grid=(len(POOL_WINDOWS),),
        in_specs=[BS((s, lb), lambda k: (0, k))], out_specs=BS((s, lb), lambda k: (0, k)),
        out_shape=SDS((s, width), MXU_DTYPE), compiler_params=_params())(dp)


def _zoh(a_re, a_im, log_dt):
    dt = jnp.exp(log_dt)
    mag = jnp.exp(dt * a_re)
    ang = dt * a_im
    abar_re = mag * jnp.cos(ang)
    abar_im = mag * jnp.sin(ang)
    den = a_re * a_re + a_im * a_im
    nr = abar_re - 1.0
    ni = abar_im
    f_re = (nr * a_re + ni * a_im) / den
    f_im = (ni * a_re - nr * a_im) / den
    return abar_re, abar_im, f_re, f_im


def _zoh_fwd(l, a_re, a_im, log_dt):
    def body(ar, ai, ld, o0, o1, o2, o3):
        for ref, val in zip((o0, o1, o2, o3), _zoh(ar[...], ai[...], ld[...])):
            ref[...] = val

    return pl.pallas_call(body, name=f"zoh_fwd_l{l}", out_shape=[SDS(a_re.shape, F32)] * 4)(a_re, a_im, log_dt)


def _zoh_bwd(l, a_re, a_im, log_dt, cts):
    def body(ar, ai, ld, c0, c1, c2, c3, dar, dai, dld):
        _, vjp = jax.vjp(_zoh, ar[...], ai[...], ld[...])
        g = vjp((c0[...], c1[...], c2[...], c3[...]))
        dar[...] = g[0]
        dai[...] = g[1]
        dld[...] = g[2]

    return pl.pallas_call(body, name=f"zoh_bwd_l{l}",
                          out_shape=[SDS(a_re.shape, F32), SDS(a_re.shape, F32), SDS(log_dt.shape, F32)],
                          )(a_re, a_im, log_dt, *cts)


def _bbar_fwd(l, f_re, f_im, b_re, b_im):
    g, p, n = b_re.shape[1:]

    def body(fr, fi, br, bi, o_re, o_im):
        o_re[...] = (fr[...] * br[...] - fi[...] * bi[...]).astype(o_re.dtype)
        o_im[...] = (fr[...] * bi[...] + fi[...] * br[...]).astype(o_im.dtype)

    whole = lambda shp: BS(shp, lambda i: (0,) * len(shp))
    layer = BS((None, g, p, n), lambda i: (l, 0, 0, 0))
    return pl.pallas_call(body, name=f"bbar_fwd_l{l}", grid=(1,),
                          in_specs=[whole((g, 1, n)), whole((g, 1, n)), layer, layer],
                          out_specs=[whole((g, p, n))] * 2,
                          out_shape=[SDS((g, p, n), MXU_DTYPE)] * 2)(f_re, f_im, b_re, b_im)


def _bbar_bwd(l, f_re, f_im, b_re, b_im, d_re, d_im):
    g, p, n = b_re.shape[1:]

    def body(fr, fi, br, bi, dr, di, dfr, dfi, dbr, dbi):
        dfr[...] = jnp.sum(dr[...] * br[...] + di[...] * bi[...], axis=1, keepdims=True)
        dfi[...] = jnp.sum(di[...] * br[...] - dr[...] * bi[...], axis=1, keepdims=True)
        dbr[...] = fr[...] * dr[...] + fi[...] * di[...]
        dbi[...] = fr[...] * di[...] - fi[...] * dr[...]

    whole = lambda shp: BS(shp, lambda i: (0,) * len(shp))
    layer = BS((None, g, p, n), lambda i: (l, 0, 0, 0))
    return pl.pallas_call(body, name=f"bbar_bwd_l{l}", grid=(1,),
                          in_specs=[whole((g, 1, n)), whole((g, 1, n)), layer, layer, whole((g, p, n)),
                                    whole((g, p, n))],
                          out_specs=[whole((g, 1, n)), whole((g, 1, n)), whole((g, p, n)), whole((g, p, n))],
                          out_shape=[SDS((g, 1, n), F32), SDS((g, 1, n), F32), SDS((g, p, n), F32),
                                     SDS((g, p, n), F32)])(f_re, f_im, b_re, b_im, d_re, d_im)


def _powers(l, abar_re, abar_im):
    lanes = abar_re.shape[1]

    def body(ar_ref, ai_ref, o_ref):
        ar, ai = ar_ref[...], ai_ref[...]
        pows = [(ar, ai)]
        for _ in range(7):
            pr, pi = pows[-1]
            pows.append((pr * ar - pi * ai, pr * ai + pi * ar))
        row = lax.broadcasted_iota(jnp.int32, (8, lanes), 0)
        for i, k in enumerate((1, 2, 4)):
            o_ref[2 * i] = jnp.broadcast_to(pows[k - 1][0], (8, lanes))
            o_ref[2 * i + 1] = jnp.broadcast_to(pows[k - 1][1], (8, lanes))
        for slot, order in ((3, range(8)), (4, range(7, -1, -1))):
            vr = jnp.zeros((8, lanes), F32)
            vi = jnp.zeros((8, lanes), F32)
            for r, e in enumerate(order):
                vr = jnp.where(row == r, pows[e][0], vr)
                vi = jnp.where(row == r, pows[e][1], vi)
            o_ref[2 * slot] = vr
            o_ref[2 * slot + 1] = vi

    return pl.pallas_call(body, name=f"powers_l{l}", out_shape=SDS((10, 8, lanes), F32))(abar_re, abar_im)


def _block_diag(v):
    g, a, b = v.shape
    eye = jnp.eye(8, dtype=v.dtype)
    out = jnp.einsum("kgab,gh->kgahb", v.reshape(g // 8, 8, a, b), eye)
    return out.reshape(g // 8, 8 * a, 8 * b)


def _block_diag_extract(blk, a, b):
    n = blk.shape[0]
    v = blk.reshape(n, 8, a, 8, b)
    return jnp.einsum("kgahb,gh->kgab", v, jnp.eye(8, dtype=blk.dtype)).reshape(n * 8, a, b)


def _ssm_prepare(l, prm):
    g, n, p = SSM_GROUPS, SSM_STATE, SSM_GROUP
    a_re, a_im = prm["ssm_a_re"][l], prm["ssm_a_im"][l]
    log_dt = prm["ssm_log_dt"][l].reshape(g, 1)
    abar_re, abar_im, f_re, f_im = _zoh_fwd(l, a_re, a_im, log_dt)
    f_re, f_im = f_re.reshape(g, 1, n), f_im.reshape(g, 1, n)
    bbar_re, bbar_im = _bbar_fwd(l, f_re, f_im, prm["ssm_b_re"], prm["ssm_b_im"])
    pw = _powers(l, abar_re.reshape(1, g * n), abar_im.reshape(1, g * n))
    return dict(a_re=a_re, a_im=a_im, log_dt=log_dt, f_re=f_re, f_im=f_im,
                bblk_re=_block_diag(bbar_re), bblk_im=_block_diag(bbar_im),
                cblk_re=_block_diag(prm["ssm_c_re"][l].astype(MXU_DTYPE)),
                cblk_im=_block_diag(prm["ssm_c_im"][l].astype(MXU_DTYPE)), pw=pw,
                dskip=prm["ssm_d"][l].reshape(1, g * p))


def _ssm_param_grads(l, sd, r, prm):
    g, n, p = SSM_GROUPS, SSM_STATE, SSM_GROUP
    dbbar_re = _block_diag_extract(r["dbblk_re"], p, n)
    dbbar_im = _block_diag_extract(r["dbblk_im"], p, n)
    dfr, dfi, db_re, db_im = _bbar_bwd(l, sd["f_re"], sd["f_im"], prm["ssm_b_re"], prm["ssm_b_im"], dbbar_re, dbbar_im)
    cts = (r["dabar_re"].reshape(g, n), r["dabar_im"].reshape(g, n), dfr.reshape(g, n), dfi.reshape(g, n))
    da_re, da_im, dlog_dt = _zoh_bwd(l, sd["a_re"], sd["a_im"], sd["log_dt"], cts)
    return dict(ssm_a_re=da_re, ssm_a_im=da_im, ssm_log_dt=dlog_dt.reshape(g), ssm_b_re=db_re, ssm_b_im=db_im,
                ssm_c_re=_block_diag_extract(r["dcblk_re"], p, n), ssm_c_im=_block_diag_extract(r["dcblk_im"], p, n),
                ssm_d=r["dd"].reshape(g, p))


def _ffn_weight_grads(l, fb, dx2, s, place):
    d = dx2.shape[1]
    hcn = fb["act"].shape[1]
    g = {}
    for name, key, rhs in (("ffn_w_gate", "dgate", fb["h2"]), ("ffn_w_up", "dup", fb["h2"]),
                           ("ffn_w_down", "act", fb["dx2"])):
        g[name] = _tn_matmul(f"d{name}_l{l}", fb[key], BS((None, hcn, s), lambda j, t, pr: (j, 0, 0)), rhs,
                             BS((s, d), lambda j, t, pr: (0, 0)), (hcn, d), (N_CHIPS, 1), place)
    return g


def _in_weight_grad(l, ht, dz, place):
    d, s = ht.shape
    ncw = dz.shape[1] // N_CHIPS
    return _tn_matmul(f"dw_in_l{l}", ht, BS((d, s), lambda j, t, pr: (0, 0)), dz, BS((s, ncw), lambda j, t, pr: (0, j)),
                      (d, ncw), (N_CHIPS, 1), place)


def _fused_tn(name, pairs, kinds, s, place):
    n = len(pairs)

    def shape_of(a, b, kind):
        k, m = a.shape[1], b.shape[1]
        if kind == "rows":
            return (N_CHIPS, k // N_CHIPS, m)
        if kind == "cols":
            return (N_CHIPS, k, m // N_CHIPS)
        return (k // 128, 128, 128)

    shapes = [shape_of(a, b, kind) for (a, b), kind in zip(pairs, kinds)]
    out_shape = []
    for shp, kind in zip(shapes, kinds):
        out_shape += [SDS(shp, F32)] if kind == "groups" else [SDS(shp[1:], F32), SDS(shp, WIRE_DTYPE)]

    def body(place_ref, *refs):
        ins, outs, accs = refs[:2 * n], refs[2 * n:2 * n + len(out_shape)], refs[2 * n + len(out_shape):]
        o = 0
        for i, kind in enumerate(kinds):
            a, b = ins[2 * i][...], ins[2 * i + 1][...]
            if kind == "groups":
                for k in range(shapes[i][0]):
                    outs[o][k] = _mm_tn(a[:, k * 128:(k + 1) * 128], b[:, k * 128:(k + 1) * 128])
                o += 1
                continue
            acc = accs[i]
            if kind == "rows":
                acc[...] = _mm_tn(a, b).reshape(acc.shape)
            else:
                full = _mm_tn(a, b)
                nc = acc.shape[2]
                for j in range(N_CHIPS):
                    acc[j] = full[:, j * nc:(j + 1) * nc]
            outs[o][...] = acc[place_ref[0]]
            outs[o + 1][...] = acc[...].astype(WIRE_DTYPE)
            o += 2

    whole = lambda shp: BS(shp, lambda t, pr: (0,) * len(shp))
    outs = pl.pallas_call(
        body, name=name,
        grid_spec=pltpu.PrefetchScalarGridSpec(
            num_scalar_prefetch=1, grid=(1,),
            in_specs=[whole(v.shape) for pair in pairs for v in pair],
            out_specs=[whole(o.shape) for o in out_shape],
            scratch_shapes=[pltpu.VMEM(shp, F32) for shp in shapes]),
        out_shape=out_shape, compiler_params=_params(),
    )(place, *[v for pair in pairs for v in pair])
    res, o = [], 0
    for kind in kinds:
        if kind == "groups":
            res.append(outs[o])
            o += 1
        else:
            res.append((outs[o], outs[o + 1]))
            o += 2
    return res


def _mixer_weight_grads(l, sv, mb, dx1, s, place):
    g = {}
    (g["w_out"], g["ssm_w_glu"]) = _fused_tn(f"dw_out_glu_l{l}", [(mb["merged"], dx1), (mb["ge"], mb["dt"])],
                                            ("rows", "rows"), s, place)
    (g["ssm_w_proj"], g["conv_w_proj"], g["pool_w_proj"]) = _fused_tn(
        f"dw_proj_l{l}", [(mb["sa"], mb["dya"]), (mb["ac"], mb["dyb"]), (mb["pp"], mb["dyc"])],
        ("cols", "cols", "cols"), s, place)
    (dwgrp,) = _fused_tn(f"dpool_w_group_l{l}", [(sv["p"], mb["dq"])], ("groups",), s, place)
    return g, dwgrp


def _local_step(x, target, weights_of, prm, place, on_grads=None):
    s, d = x.shape
    cw = prm["ssm_b_glu"].shape[1]
    sp = {k: prm[k].reshape(N_LAYERS, 1, -1) for k in ("norm1", "norm2", "b_gate", "ssm_b_glu", "conv_ln_g", "conv_ln_b",
                                                        "pool_scale", "conv_b_dw")}
    sp["pool_w_group"] = prm["pool_w_group"]
    saved = []
    xin = x
    for l in range(N_LAYERS):
        fw = weights_of(l, "in", (xin,))
        sd = _ssm_prepare(l, prm)
        z, h = _in_proj(l, xin, sp["norm1"], fw["w_in"])
        hre, him, y = _ssm_fwd(l, z, sd["bblk_re"], sd["bblk_im"], sd["cblk_re"], sd["cblk_im"], sd["pw"], sd["dskip"])
        p = _pool_fwd(l, z, cw)
        fw.update(weights_of(l, "mixer", (y, p)))
        wdw = fw["conv_w_dw"]
        hc = _conv_fwd(l, z, wdw, sp["conv_b_dw"][l])
        x1 = _merge_fwd(l, xin, y, hc, p, z, fw, sp)
        fw.update(weights_of(l, "ffn", (x1,)))
        x2, gate, up, h2 = _ffn_fwd(l, x1, sp["norm2"], fw["ffn_w_gate"], fw["ffn_w_up"], fw["ffn_w_down"])
        saved.append(dict(x=xin, z=z, h=h, hre=hre, him=him, y=y, hc=hc, p=p, x1=x1, sd=sd, wdw=wdw, fw=fw,
                          gate=gate, up=up, h2=h2))
        xin = x2
    dx, loss, dfinal = _loss_head(xin, target, prm["final_norm"].reshape(1, d))
    big = [None] * N_LAYERS
    small = [None] * N_LAYERS
    norm2_rows = sp["norm2"]
    started = (lambda l, group, grads: on_grads(l, group, grads)) if on_grads is not None else (lambda *a: 0.0)
    for l in reversed(range(N_LAYERS)):
        sv = saved[l]
        sd, fw = sv["sd"], sv["fw"]
        fb = _ffn_bwd(l, sv["x1"], dx, sv["gate"], sv["up"], norm2_rows, fw["ffn_w_gate"], fw["ffn_w_up"],
                      fw["ffn_w_down"])
        fb["h2"] = sv["h2"]
        big[l] = _ffn_weight_grads(l, fb, dx, s, place)
        spl = dict(sp, ssm_b_glu=sp["ssm_b_glu"] + started(l, "ffn", big[l]))
        mb = _merge_bwd(l, fb["dx1"], sv["y"], sv["hc"], sv["p"], sv["z"], fw, spl)
        mixer, dwgrp = _mixer_weight_grads(l, sv, mb, fb["dx1"], s, place)
        big[l].update(mixer)
        wdw = sv["wdw"] + started(l, "mixer", mixer)
        du_c = _pool_bwd(l, mb["dp"])
        dv1, dv2, dwdw, dbdw = _conv_bwd(l, mb["dhc"], sv["z"], wdw)
        sr = _ssm_bwd(l, mb["dy"], sv["z"], sv["hre"], sv["him"], sd["bblk_re"], sd["bblk_im"], sd["cblk_re"],
                      sd["cblk_im"], sd["pw"], sd["dskip"])
        dx, dz, dnorm1 = _in_proj_bwd(l, fb["dx1"], sv["x"], sp["norm1"], fw["w_in"], sr["du"], dv1, dv2, du_c, mb["dzg"])
        w_in_grad = {"w_in": _in_weight_grad(l, sv["h"], dz, place)}
        big[l].update(w_in_grad)
        sg = _ssm_param_grads(l, sd, sr, prm)
        sg.update(norm1=dnorm1.reshape(d), b_gate=mb["db_gate"].reshape(3 * d), ssm_b_glu=mb["db_glu"].reshape(cw),
                  conv_b_dw=dbdw.reshape(cw), conv_ln_g=mb["dln_g"].reshape(cw), conv_ln_b=mb["dln_b"].reshape(cw),
                  pool_w_group=dwgrp, pool_scale=mb["dscale"].reshape(cw), norm2=fb["dnorm2"].reshape(d),
                  conv_w_dw=dwdw)
        small[l] = sg
        if l == N_LAYERS - 1:
            sg = dict(sg, final_norm=dfinal.reshape(d))
        norm2_rows = sp["norm2"] + (started(l, "in", w_in_grad) + started(l, "small", sg))
    return loss[0, 0], dx, big, small, dfinal.reshape(d)


def _place():
    return lax.axis_index("x"), lax.axis_index("y"), lax.axis_index("c")


def _other_chips(x, y):
    return [(1 - x, y), (x, 1 - y), (1 - x, 1 - y)]


def _remote(src, dst, send_sem, recv_sem, device):
    return pltpu.make_async_remote_copy(src_ref=src, dst_ref=dst, send_sem=send_sem, recv_sem=recv_sem,
                                        device_id=device, device_id_type=MESH)


def _hbm(v):
    return pltpu.with_memory_space_constraint(v, pltpu.HBM)


def _cast_into(name, w, place, dtype, after=()):
    nl, k, n = w.shape
    tr = _row_tile(k, n)
    nt = k // tr

    def body(place_ref, w_ref, *rest):
        o0_ref, o1_ref = rest[len(after):]

        @pl.when(pl.program_id(0) == 0)
        def _():
            o0_ref[...] = w_ref[...].astype(dtype)

        @pl.when(pl.program_id(0) == 1)
        def _():
            o1_ref[...] = w_ref[...].astype(dtype)

    return pl.pallas_call(
        body, name=f"cast_{name}",
        grid_spec=pltpu.PrefetchScalarGridSpec(
            num_scalar_prefetch=1, grid=(nl, nt),
            in_specs=[BS((None, tr, n), lambda l, t, pr: (l, t, 0))] + [ANY] * len(after),
            out_specs=[BS((None, tr, n), lambda l, t, pr: (pr[0], t * (1 - l) + (nt - 1) * l, 0)),
                       BS((None, tr, n), lambda l, t, pr: (pr[0], t * l, 0))]),
        out_shape=[SDS((N_CHIPS, k, n), dtype)] * 2)(place, w, *after)


def _cast_small_into(tag, ws, dtypes, place, after=()):
    n = len(ws)

    def body(place_ref, *refs):
        ins, outs = refs[:n], refs[n + len(after):]
        for i in range(n):
            @pl.when(pl.program_id(0) == 0)
            def _():
                outs[2 * i][...] = ins[i][...].astype(dtypes[i])

            @pl.when(pl.program_id(0) == 1)
            def _():
                outs[2 * i + 1][...] = ins[i][...].astype(dtypes[i])

    slot = lambda w: BS((None,) + w.shape[1:], lambda l, pr: (pr[0], 0, 0))
    outs = pl.pallas_call(
        body, name=f"cast_{tag}",
        grid_spec=pltpu.PrefetchScalarGridSpec(
            num_scalar_prefetch=1, grid=(N_LAYERS,),
            in_specs=[BS((None,) + w.shape[1:], lambda l, pr: (l, 0, 0)) for w in ws] + [ANY] * len(after),
            out_specs=[slot(w) for w in ws for _ in range(N_LAYERS)]),
        out_shape=[SDS((N_CHIPS,) + w.shape[1:], dt) for w, dt in zip(ws, dtypes) for _ in range(N_LAYERS)],
    )(place, *ws, *after)
    return [tuple(outs[N_LAYERS * i:N_LAYERS * (i + 1)]) for i in range(n)]


def _gather_rows(buf, c):
    k = buf.shape[1]
    if k % 2:
        return pl.ds(0, k)
    return pl.ds(pl.multiple_of(c * (k // 2), 8), k // 2)


def _allgather_start(tag, groups):
    ng = len(groups)
    sizes = [len(g) for g in groups]
    first = [sum(sizes[:g]) for g in range(ng)]
    flat = [b for g in groups for b in g]
    nb = len(flat)

    def body(*refs):
        ins = refs[:nb]
        sems = refs[nb:nb + 2 * ng]
        token = refs[-1]
        x, y, c = _place()
        jme = 2 * x + y
        for g in range(ng):
            for a in range(sizes[g]):
                buf = ins[first[g] + a]
                blk = buf.at[jme, _gather_rows(buf, c)]
                for k, (cx, cy) in enumerate(_other_chips(x, y)):
                    _remote(blk, blk, sems[2 * g].at[3 * a + k], sems[2 * g + 1].at[3 * a + k], (cx, cy, c)).start()
        token[...] = jnp.zeros(token.shape, F32)

    sem_shapes = [pltpu.SemaphoreType.DMA((3 * sizes[g // 2],)) for g in range(2 * ng)]
    outs = pl.pallas_call(
        body, name=f"allgather_start_{tag}", in_specs=[HBM] * nb,
        out_specs=[SEM] * (2 * ng) + [HBM] * nb + [pl.BlockSpec(memory_space=pltpu.VMEM)],
        out_shape=sem_shapes + [pltpu.HBM(b.shape, b.dtype) for b in flat] + [SDS((8, 128), F32)],
        input_output_aliases={i: 2 * ng + i for i in range(nb)},
        compiler_params=pltpu.CompilerParams(has_side_effects=SIDE_EFFECT))(*[_hbm(b) for b in flat])
    per_group = [(outs[2 * g], outs[2 * g + 1], outs[2 * ng + first[g]:2 * ng + first[g] + sizes[g]])
                 for g in range(ng)]
    return per_group, outs[-1]


def _allgather_wait(l, send_sems, recv_sems, bufs, after):
    n = len(bufs)

    def body(*refs):
        ins = refs[:n]
        ssem, rsem = refs[n], refs[n + 1]
        x, y, c = _place()
        jme = 2 * x + y
        for a in range(n):
            rows = _gather_rows(ins[a], c)
            for k, (cx, cy) in enumerate(_other_chips(x, y)):
                cp = _remote(ins[a].at[jme, rows], ins[a].at[2 * cx + cy, rows], ssem.at[3 * a + k],
                             rsem.at[3 * a + k], (cx, cy, c))
                cp.wait_send()
                cp.wait_recv()

    return pl.pallas_call(
        body, name=f"allgather_wait_{l}", in_specs=[HBM] * n + [SEM, SEM] + [ANY] * len(after), out_specs=[HBM] * n,
        out_shape=[pltpu.HBM(b.shape, b.dtype) for b in bufs], input_output_aliases={i: i for i in range(n)},
        compiler_params=pltpu.CompilerParams(has_side_effects=SIDE_EFFECT))(*bufs, send_sems, recv_sems, *after)


def _allgather_forward(l, bufs):
    n = len(bufs)
    split = [a for a in range(n) if bufs[a].shape[1] % 2 == 0]

    def body(*refs):
        ins = refs[:n]
        send_sems, recv_sems = refs[2 * n:]
        x, y, c = _place()
        sibling = (x, y, 1 - c)
        copies = []
        for a in split:
            for k, (cx, cy) in enumerate(_other_chips(x, y)):
                blk = ins[a].at[2 * cx + cy, _gather_rows(ins[a], c)]
                cp = _remote(blk, blk, send_sems.at[a, k], recv_sems.at[a, k], sibling)
                cp.start()
                copies.append(cp)
        for a in split:
            for k, (cx, cy) in enumerate(_other_chips(x, y)):
                blk = ins[a].at[2 * cx + cy, _gather_rows(ins[a], 1 - c)]
                _remote(blk, blk, send_sems.at[a, k], recv_sems.at[a, k], sibling).wait_recv()
        for cp in copies:
            cp.wait_send()

    sem = pltpu.SemaphoreType.DMA((n, 3))
    return pl.pallas_call(
        body, name=f"allgather_forward_{l}", in_specs=[ANY] * n, out_specs=[ANY] * n,
        out_shape=[SDS(b.shape, b.dtype) for b in bufs], input_output_aliases={i: i for i in range(n)},
        scratch_shapes=[sem, sem])(*bufs)


def _rs_to_owner(l, parts):
    n = len(parts)
    lands = [lax.empty((3,) + p.shape[1:], p.dtype) for p in parts]

    def body(*refs):
        ins, zones = refs[:n], refs[n:2 * n]
        send_sems, recv_sems = refs[2 * n], refs[2 * n + 1]
        token = refs[-1]
        x, y, c = _place()
        for a in range(n):
            for k, (cx, cy) in enumerate(_other_chips(x, y)):
                _remote(ins[a].at[2 * cx + cy], zones[a].at[k], send_sems.at[3 * a + k], recv_sems.at[3 * a + k],
                        (cx, cy, c)).start()
        token[...] = jnp.zeros(token.shape, F32)

    sem = pltpu.SemaphoreType.DMA((3 * n,))
    outs = pl.pallas_call(
        body, name=f"rs_to_owner_start_{l}", in_specs=[HBM] * (2 * n),
        out_specs=[SEM, SEM] + [HBM] * (2 * n) + [pl.BlockSpec(memory_space=pltpu.VMEM)],
        out_shape=[sem, sem] + [pltpu.HBM(p.shape, p.dtype) for p in parts]
        + [pltpu.HBM(z.shape, z.dtype) for z in lands] + [SDS((8, 128), F32)],
        input_output_aliases={i: 2 + i for i in range(2 * n)},
        compiler_params=pltpu.CompilerParams(has_side_effects=SIDE_EFFECT),
    )(*[_hbm(p) for p in parts], *[_hbm(z) for z in lands])
    return outs[0], outs[1], outs[2:2 + n], outs[2 + n:2 + 2 * n], outs[-1]


def _rs_to_owner_wait(l, send_sems, recv_sems, parts, lands, after):
    n = len(parts)

    def body(*refs):
        ins, zones = refs[:n], refs[n:2 * n]
        ssem, rsem = refs[2 * n], refs[2 * n + 1]
        x, y, c = _place()
        for a in range(n):
            for k, (cx, cy) in enumerate(_other_chips(x, y)):
                cp = _remote(ins[a].at[2 * cx + cy], zones[a].at[k], ssem.at[3 * a + k], rsem.at[3 * a + k],
                             (cx, cy, c))
                cp.wait_send()
                cp.wait_recv()

    outs = pl.pallas_call(
        body, name=f"rs_to_owner_wait_{l}", in_specs=[HBM] * (2 * n) + [SEM, SEM] + [ANY] * len(after),
        out_specs=[HBM] * (2 * n),
        out_shape=[pltpu.HBM(p.shape, p.dtype) for p in parts] + [pltpu.HBM(z.shape, z.dtype) for z in lands],
        input_output_aliases={i: i for i in range(2 * n)},
        compiler_params=pltpu.CompilerParams(has_side_effects=SIDE_EFFECT),
    )(*parts, *lands, send_sems, recv_sems, *after)
    return outs[:n], outs[n:]


def _rs_sibling_exchange(l, both):
    n = len(both)

    def body(*refs):
        ins = refs[:n]
        send_sems, recv_sems = refs[2 * n:]
        x, y, c = _place()
        copies = []
        for a in range(n):
            cp = _remote(ins[a].at[c], ins[a].at[c], send_sems.at[a], recv_sems.at[a], (x, y, 1 - c))
            cp.start()
            copies.append(cp)
        for a, cp in enumerate(copies):
            cp.wait_send()
            _remote(ins[a].at[1 - c], ins[a].at[1 - c], send_sems.at[a], recv_sems.at[a], (x, y, 1 - c)).wait_recv()

    sem = pltpu.SemaphoreType.DMA((n,))
    return pl.pallas_call(
        body, name=f"rs_sibling_exchange_{l}", in_specs=[ANY] * n, out_specs=[ANY] * n,
        out_shape=[SDS(b.shape, b.dtype) for b in both], input_output_aliases={i: i for i in range(n)},
        scratch_shapes=[sem, sem])(*both)


def _add_owner(name, grad, recv, place):
    r, cols = grad.shape
    tr = _row_tile(r, cols, budget=1024 * 1024)
    nt = r // tr

    def body(place_ref, g_ref, r_ref, o_ref):
        acc = ((g_ref[...] + r_ref[0].astype(F32)) + r_ref[1].astype(F32)) + r_ref[2].astype(F32)
        o_ref[...] = acc.astype(o_ref.dtype)

    return pl.pallas_call(
        body, name=name,
        grid_spec=pltpu.PrefetchScalarGridSpec(
            num_scalar_prefetch=1, grid=(nt,),
            in_specs=[BS((tr, cols), lambda t, pr: (t, 0)), BS((3, tr, cols), lambda t, pr: (0, t, 0))],
            out_specs=BS((None, tr, cols), lambda t, pr: (pr[1], t, 0))),
        out_shape=SDS((2, r, cols), WIRE_DTYPE))(place, grad, recv)


def _add_owner_small(tag, grads, recvs, place):
    n = len(grads)

    def body(place_ref, *refs):
        gs, rs, outs = refs[:n], refs[n:2 * n], refs[2 * n:]
        for g_ref, r_ref, o_ref in zip(gs, rs, outs):
            acc = ((g_ref[...] + r_ref[0].astype(F32)) + r_ref[1].astype(F32)) + r_ref[2].astype(F32)
            o_ref[...] = acc.astype(o_ref.dtype)

    return pl.pallas_call(
        body, name=f"rs_add_owner_{tag}",
        grid_spec=pltpu.PrefetchScalarGridSpec(
            num_scalar_prefetch=1, grid=(1,),
            in_specs=[BS(g.shape, lambda t, pr: (0, 0)) for g in grads]
            + [BS(r.shape, lambda t, pr: (0, 0, 0)) for r in recvs],
            out_specs=[BS((None,) + g.shape, lambda t, pr: (pr[1], 0, 0)) for g in grads]),
        out_shape=[SDS((2,) + g.shape, WIRE_DTYPE) for g in grads])(place, *grads, *recvs)


def _reduce_start(tag, grads):
    names = list(grads)
    send_sems, recv_sems, wires, lands, token = _rs_to_owner(tag, [grads[n][1] for n in names])
    return dict(tag=tag, names=names, send_sems=send_sems, recv_sems=recv_sems, wires=wires, lands=lands,
                grads=[grads[n][0] for n in names]), token


def _reduce_finish(tag, groups, place, after):
    all_names, all_mine = [], []
    for pending in groups:
        sub, names = pending["tag"], pending["names"]
        _, lands = _rs_to_owner_wait(sub, pending["send_sems"], pending["recv_sems"], pending["wires"],
                                     pending["lands"], after)
        if max(g.size for g in pending["grads"]) <= SMALL_GRAD_ELEMS:
            mine = _add_owner_small(sub, pending["grads"], lands, place)
        else:
            mine = [_add_owner(f"rs_add_owner_{n}_{sub}", g, r, place)
                    for n, g, r in zip(names, pending["grads"], lands)]
        all_names += names
        all_mine += mine
    return dict(zip(all_names, _rs_sibling_exchange(tag, all_mine)))


def _small_peers(x, y, c):
    return [(x, y, 1 - c)] + [(cx, cy, c) for cx, cy in _other_chips(x, y)]


def _allgather_rows_start(tag, bufs):
    n = len(bufs)
    lands = [lax.empty((8,) + b.shape, b.dtype) for b in bufs]

    def body(*refs):
        ins, zones = refs[:n], refs[n:2 * n]
        send_sems, recv_sems = refs[2 * n], refs[2 * n + 1]
        token = refs[-1]
        x, y, c = _place()
        for a in range(n):
            for i, peer in enumerate(_small_peers(x, y, c)):
                _remote(ins[a], zones[a].at[4 * x + 2 * y + c], send_sems.at[4 * a + i], recv_sems.at[4 * a + i],
                        peer).start()
        token[...] = jnp.zeros(token.shape, F32)

    sem = pltpu.SemaphoreType.DMA((4 * n,))
    outs = pl.pallas_call(
        body, name=f"allgather_small_start_{tag}", in_specs=[HBM] * (2 * n),
        out_specs=[SEM, SEM] + [HBM] * (2 * n) + [pl.BlockSpec(memory_space=pltpu.VMEM)],
        out_shape=[sem, sem] + [pltpu.HBM(b.shape, b.dtype) for b in bufs]
        + [pltpu.HBM(z.shape, z.dtype) for z in lands] + [SDS((8, 128), F32)],
        input_output_aliases={i: 2 + i for i in range(2 * n)},
        compiler_params=pltpu.CompilerParams(has_side_effects=SIDE_EFFECT),
    )(*[_hbm(b) for b in bufs], *[_hbm(z) for z in lands])
    return outs[0], outs[1], outs[2:2 + n], outs[2 + n:2 + 2 * n], outs[-1]


def _allgather_rows_wait(tag, send_sems, recv_sems, bufs, lands, after):
    n = len(bufs)

    def body(*refs):
        ins, zones = refs[:n], refs[n:2 * n]
        ssem, rsem = refs[2 * n], refs[2 * n + 1]
        x, y, c = _place()
        for a in range(n):
            for i, (px, py, pc) in enumerate(_small_peers(x, y, c)):
                cp = _remote(ins[a], zones[a].at[4 * px + 2 * py + pc], ssem.at[4 * a + i], rsem.at[4 * a + i],
                             (px, py, pc))
                cp.wait_send()
                cp.wait_recv()

    outs = pl.pallas_call(
        body, name=f"allgather_small_wait_{tag}", in_specs=[HBM] * (2 * n) + [SEM, SEM, ANY],
        out_specs=[HBM] * (2 * n),
        out_shape=[pltpu.HBM(b.shape, b.dtype) for b in bufs] + [pltpu.HBM(z.shape, z.dtype) for z in lands],
        input_output_aliases={i: i for i in range(2 * n)},
        compiler_params=pltpu.CompilerParams(has_side_effects=SIDE_EFFECT),
    )(*bufs, *lands, send_sems, recv_sems, after)
    return outs[:n], outs[n:]


def _allgather_rows_forward(tag, lands):
    n = len(lands)

    def body(*refs):
        ins = refs[:n]
        send_sems, recv_sems = refs[2 * n:]
        x, y, c = _place()
        sibling = (x, y, 1 - c)
        copies = []
        for a in range(n):
            for k, (cx, cy) in enumerate(_other_chips(x, y)):
                blk = ins[a].at[4 * cx + 2 * cy + c]
                cp = _remote(blk, blk, send_sems.at[a, k], recv_sems.at[a, k], sibling)
                cp.start()
                copies.append(cp)
        for a in range(n):
            for k, (cx, cy) in enumerate(_other_chips(x, y)):
                blk = ins[a].at[4 * cx + 2 * cy + 1 - c]
                _remote(blk, blk, send_sems.at[a, k], recv_sems.at[a, k], sibling).wait_recv()
        for cp in copies:
            cp.wait_send()

    sem = pltpu.SemaphoreType.DMA((n, 3))
    return pl.pallas_call(body, name=f"allgather_small_forward_{tag}", in_specs=[ANY] * n, out_specs=[ANY] * n,
                          out_shape=[SDS(z.shape, z.dtype) for z in lands],
                          input_output_aliases={i: i for i in range(n)}, scratch_shapes=[sem, sem])(*lands)


def _sum_devices(tag, gathered, mine, place):
    _, r, cols = gathered.shape
    tr = _row_tile(r, cols, budget=256 * 1024)

    def body(place_ref, g_ref, x_ref, o_ref):
        me = 2 * place_ref[0] + place_ref[1]
        acc = jnp.where(me == 0, x_ref[...], g_ref[0])
        for k in range(1, 8):
            acc = acc + jnp.where(me == k, x_ref[...], g_ref[k])
        o_ref[...] = acc

    return pl.pallas_call(
        body, name=f"sum_small_grads_{tag}",
        grid_spec=pltpu.PrefetchScalarGridSpec(
            num_scalar_prefetch=1, grid=(r // tr,),
            in_specs=[BS((8, tr, cols), lambda t, pr: (0, t, 0)), BS((tr, cols), lambda t, pr: (t, 0))],
            out_specs=BS((tr, cols), lambda t, pr: (t, 0))),
        out_shape=SDS((r, cols), F32))(place, gathered, mine)


def _adamw_values(w, g, m, v):
    m = ADAM_B1 * m + (1.0 - ADAM_B1) * g
    v = ADAM_B2 * v + (1.0 - ADAM_B2) * (g * g)
    m_hat = m / (1.0 - ADAM_B1 ** ADAM_STEP)
    v_hat = v / (1.0 - ADAM_B2 ** ADAM_STEP)
    delta = -ADAM_LR * (m_hat / (jnp.sqrt(v_hat) + ADAM_EPS) + ADAM_WD * w)
    return delta, m, v


def _adamw_big(name, l, w, m, v, g, earlier=None, after=()):
    nl, r, cols = w.shape
    tr = _row_tile(r, cols, budget=1024 * 1024)
    nt = r // tr
    n_prev = 0 if earlier is None else 4

    def body(*refs):
        w_ref, m_ref, v_ref, g_ref = refs[:4]
        go_ref, d_ref, mo_ref, vo_ref = refs[4 + n_prev + len(after):]
        gv = g_ref[0].astype(F32) + g_ref[1].astype(F32)
        delta, m_new, v_new = _adamw_values(w_ref[...], gv, m_ref[...], v_ref[...])
        go_ref[...] = gv
        d_ref[...] = delta
        mo_ref[...] = m_new
        vo_ref[...] = v_new

    layer = BS((None, tr, cols), lambda t: (l, t, 0))
    return pl.pallas_call(
        body, name=f"adamw_{name}_l{l}", grid=(nt,),
        in_specs=[layer, layer, layer, BS((2, tr, cols), lambda t: (0, t, 0))] + [ANY] * (n_prev + len(after)),
        out_specs=[layer] * 4, out_shape=[SDS(w.shape, F32)] * 4,
        input_output_aliases={4 + i: i for i in range(n_prev)}, compiler_params=_params(),
    )(w, m, v, g, *(earlier or ()), *after)


def _adamw_small_group(tag, l, ws, ms, vs, gs, earlier, after=()):
    n = len(ws)
    steps = ADAMW_GROUP_STEPS
    prev = [a for e in earlier if e is not None for a in e]
    n_prev = len(prev)
    assert n_prev in (0, 4 * n)

    def body(*refs):
        w_refs, m_refs, v_refs, g_refs = refs[:n], refs[n:2 * n], refs[2 * n:3 * n], refs[3 * n:4 * n]
        outs = refs[4 * n + n_prev + len(after):]
        for i in range(n):
            gv = g_refs[i][0].astype(F32) + g_refs[i][1].astype(F32)
            delta, m_new, v_new = _adamw_values(w_refs[i][...], gv, m_refs[i][...], v_refs[i][...])
            for ref, val in zip(outs[4 * i:4 * i + 4], (gv, delta, m_new, v_new)):
                ref[...] = val

    def layer(w):
        return BS((None, w.shape[1] // steps, w.shape[2]), lambda t: (l, t, 0))

    return pl.pallas_call(
        body, name=f"adamw_{tag}_l{l}", grid=(steps,),
        in_specs=[layer(w) for w in ws] * 3
        + [BS((2, w.shape[1] // steps, w.shape[2]), lambda t: (0, t, 0)) for w in ws] + [ANY] * (n_prev + len(after)),
        out_specs=[layer(w) for w in ws for _ in range(4)],
        out_shape=[SDS(w.shape, F32) for w in ws for _ in range(4)],
        input_output_aliases={4 * n + i: i for i in range(n_prev)}, compiler_params=_params(),
    )(*ws, *ms, *vs, *gs, *prev, *after)


def _adamw_mid(ws, ms, vs, gathered, mine, place):
    n = len(ws)
    shape = ws[0].shape[1:]
    zeros = (0,) * len(shape)

    def body(place_ref, *refs):
        w_refs, m_refs, v_refs = refs[:n], refs[n:2 * n], refs[2 * n:3 * n]
        gath, own = refs[3 * n:(3 + N_LAYERS) * n], refs[(3 + N_LAYERS) * n:(3 + 2 * N_LAYERS) * n]
        outs = refs[(3 + 2 * N_LAYERS) * n:]
        me = 2 * place_ref[0] + place_ref[1]
        for i in range(n):
            gv = None
            for l in range(N_LAYERS):
                g_ref, x_ref = gath[l * n + i], own[l * n + i]
                acc = jnp.where(me == 0, x_ref[...], g_ref[0])
                for k in range(1, 8):
                    acc = acc + jnp.where(me == k, x_ref[...], g_ref[k])
                gv = acc if gv is None else jnp.where(pl.program_id(0) == l, acc, gv)
            delta, m_new, v_new = _adamw_values(w_refs[i][...], gv, m_refs[i][...], v_refs[i][...])
            for ref, val in zip(outs[4 * i:4 * i + 4], (gv, delta, m_new, v_new)):
                ref[...] = val

    layer = BS((None,) + shape, lambda l, pr: (l,) + zeros)
    kept = pl.Buffered(1)
    outs = pl.pallas_call(
        body, name="adamw_replicated_matrices",
        grid_spec=pltpu.PrefetchScalarGridSpec(
            num_scalar_prefetch=1, grid=(N_LAYERS,),
            in_specs=[layer] * (3 * n)
            + [BS((8,) + shape, lambda l, pr: (0,) + zeros, pipeline_mode=kept)] * (N_LAYERS * n)
            + [BS(shape, lambda l, pr: zeros, pipeline_mode=kept)] * (N_LAYERS * n),
            out_specs=[layer] * (4 * n)),
        out_shape=[SDS(ws[0].shape, F32)] * (4 * n), compiler_params=_params(),
    )(place, *ws, *ms, *vs, *[g for l in range(N_LAYERS) for g in gathered[l]],
      *[x for l in range(N_LAYERS) for x in mine[l]])
    return [tuple(outs[4 * i:4 * i + 4]) for i in range(n)]


def _adamw_rows(w, m, v, g):
    r, cols = w.shape
    tr = _row_tile(r, cols, budget=512 * 1024)

    def body(w_ref, m_ref, v_ref, g_ref, d_ref, mo_ref, vo_ref):
        delta, m_new, v_new = _adamw_values(w_ref[...], g_ref[...], m_ref[...], v_ref[...])
        d_ref[...] = delta
        mo_ref[...] = m_new
        vo_ref[...] = v_new

    spec = BS((tr, cols), lambda t: (t, 0))
    return pl.pallas_call(body, name="adamw_small", grid=(r // tr,), in_specs=[spec] * 4, out_specs=[spec] * 3,
                          out_shape=[SDS(w.shape, F32)] * 3)(w, m, v, g)


SMALL_GRAD_ELEMS = 256 * 1024
ADAMW_GROUP_STEPS = 4
PACK_ALIGN = 8 * 128
PACK_ROWS = 128


def _pack_rows(arrays):
    parts, rows = [], 0
    for a in arrays:
        flat = a.reshape(-1)
        pad = (-flat.shape[0]) % PACK_ALIGN
        if pad:
            flat = jnp.pad(flat, (0, pad))
        parts.append(flat.reshape(-1, 128))
        rows += parts[-1].shape[0]
    if rows % PACK_ROWS:
        parts.append(jnp.zeros((PACK_ROWS - rows % PACK_ROWS, 128), parts[0].dtype))
    return jnp.concatenate(parts, axis=0)


def _unpack_rows(buf, shapes):
    out, row = [], 0
    for shape in shapes:
        size = math.prod(shape)
        rows = -(-size // PACK_ALIGN) * (PACK_ALIGN // 128)
        out.append(buf[row:row + rows].reshape(-1)[:size].reshape(shape))
        row += rows
    return out


def kernel(x, norm1, w_in, b_gate, ssm_a_re, ssm_a_im, ssm_log_dt, ssm_b_re, ssm_b_im, ssm_c_re, ssm_c_im, ssm_d, ssm_w_glu, ssm_b_glu, ssm_w_proj, conv_w_dw, conv_b_dw, conv_ln_g, conv_ln_b, conv_w_proj, pool_w_group, pool_scale, pool_w_proj, w_out, norm2, ffn_w_gate, ffn_w_up, ffn_w_down, final_norm, loss_target, m_norm1, m_w_in, m_b_gate, m_ssm_a_re, m_ssm_a_im, m_ssm_log_dt, m_ssm_b_re, m_ssm_b_im, m_ssm_c_re, m_ssm_c_im, m_ssm_d, m_ssm_w_glu, m_ssm_b_glu, m_ssm_w_proj, m_conv_w_dw, m_conv_b_dw, m_conv_ln_g, m_conv_ln_b, m_conv_w_proj, m_pool_w_group, m_pool_scale, m_pool_w_proj, m_w_out, m_norm2, m_ffn_w_gate, m_ffn_w_up, m_ffn_w_down, m_final_norm, v_norm1, v_w_in, v_b_gate, v_ssm_a_re, v_ssm_a_im, v_ssm_log_dt, v_ssm_b_re, v_ssm_b_im, v_ssm_c_re, v_ssm_c_im, v_ssm_d, v_ssm_w_glu, v_ssm_b_glu, v_ssm_w_proj, v_conv_w_dw, v_conv_b_dw, v_conv_ln_g, v_conv_ln_b, v_conv_w_proj, v_pool_w_group, v_pool_scale, v_pool_w_proj, v_w_out, v_norm2, v_ffn_w_gate, v_ffn_w_up, v_ffn_w_down, v_final_norm):
    given = dict(locals())
    cx, cy, cc = _place()
    place = jnp.stack([2 * cx + cy, cc]).astype(jnp.int32)

    def kernel_view(n, a):
        if n in TRANSPOSED:
            return a.transpose(0, 2, 1)
        return a.transpose(0, 1, 3, 2) if n in ("ssm_b_re", "ssm_b_im") else a

    prm = {n: given[n] for n in WEIGHTS}
    mom = {n: given["m_" + n] for n in WEIGHTS}
    var = {n: given["v_" + n] for n in WEIGHTS}
    for n in MID:
        prm[n], mom[n], var[n] = kernel_view(n, prm[n]), kernel_view(n, mom[n]), kernel_view(n, var[n])

    dw_shard = prm["conv_w_dw"].reshape(N_LAYERS, CONV_KERNEL, -1)
    casts = {"w_in": _cast_into("w_in", prm["w_in"], place, MXU_DTYPE)}
    first, first_started = _allgather_start("first", [[casts["w_in"][0]]])
    in_flight = {(0, "in"): first[0]}
    mixer = GATHER_GROUPS["mixer"]
    casts.update(zip(mixer, _cast_small_into(
        "mixer", [dw_shard if n == "conv_w_dw" else prm[n] for n in mixer],
        [F32 if n == "conv_w_dw" else MXU_DTYPE for n in mixer], place, after=(first_started,))))
    casts.update({n: _cast_into(n, kernel_view(n, prm[n]), place, MXU_DTYPE, after=(first_started,))
                  for n in GATHER_GROUPS["ffn"]})
    order = [(l, g) for l in range(N_LAYERS) for g in GATHER_GROUPS if (l, g) != (0, "in")]
    rest, rest_started = _allgather_start("rest", [[casts[n][l] for n in GATHER_GROUPS[g]] for l, g in order])
    in_flight.update(zip(order, rest))

    arrived = {}

    def weights_of(l, group, after):
        if (l, group) in arrived:
            return arrived.pop((l, group))
        tag = f"l{l}_{group}"
        if (l, group) == (0, "in"):
            after = after + (rest_started,)
        groups = (group, "mixer") if (l > 0 and group == "in") else (group,)
        waited = [_allgather_wait(f"l{l}_{g}", *in_flight[l, g][:2], in_flight[l, g][2], after) for g in groups]
        bufs = _allgather_forward(tag, [b for w in waited for b in w])
        for g in groups:
            fw = dict(zip(GATHER_GROUPS[g], bufs[:len(GATHER_GROUPS[g])]))
            bufs = bufs[len(GATHER_GROUPS[g]):]
            if "conv_w_dw" in fw:
                fw["conv_w_dw"] = fw["conv_w_dw"].transpose(1, 0, 2).reshape(CONV_KERNEL, -1)
            arrived[l, g] = fw
        return arrived.pop((l, group))

    pending, small_pending, small_shapes = {}, {}, {}
    tokens = {}

    def on_grads(l, group, grads):
        if group == "small":
            packed = {n: g for n, g in grads.items() if n not in MID}
            small_shapes[l] = {n: g.shape for n, g in packed.items()}
            begun = _allgather_rows_start(f"l{l}", [_pack_rows(list(packed.values()))] + [grads[n] for n in MID])
            small_pending[l], token = begun[:4], begun[4]
        else:
            pending[l, group], token = _reduce_start(f"{l}_{group}", grads)
        tokens[l, group] = token
        return token[0, 0]

    loss, dx, _, _, _ = _local_step(x[0], loss_target[0], weights_of, prm, place, on_grads)
    loss = lax.psum(loss, ("x", "y", "c"))

    reduced = [{} for _ in range(N_LAYERS)]
    out = {}

    def finish(l, groups, after):
        reduced[l].update(_reduce_finish(f"l{l}_{groups[0]}", [pending[l, g] for g in groups], place, after))

    def adamw(l, names, done):
        small = [n for n in names if prm[n][0].size <= SMALL_GRAD_ELEMS]
        for n in names:
            if n not in small:
                out[n] = _adamw_big(n, l, kernel_view(n, prm[n]), kernel_view(n, mom[n]), kernel_view(n, var[n]),
                                    reduced[l][n], out.get(n), after=done)
                done = (out[n][0],)
        if small:
            res = _adamw_small_group("mixer", l, [prm[n] for n in small], [mom[n] for n in small],
                                     [var[n] for n in small], [reduced[l][n] for n in small],
                                     [out.get(n) for n in small], after=done)
            for i, n in enumerate(small):
                out[n] = tuple(res[4 * i:4 * i + 4])
            done = (res[0],)
        return done

    top = N_LAYERS - 1
    done = (tokens[0, "in"], tokens[0, "small"])
    finish(top, ("ffn", "mixer", "in"), done)
    done = adamw(top, BIG, done)
    for groups in (("ffn", "mixer"), ("in",)):
        finish(0, groups, done)
        done = adamw(0, [n for g in groups for n in GATHER_GROUPS[g] if n in BIG], done)
    for n in BIG:
        out[n] = tuple(kernel_view(n, a) for a in out[n])

    gsmall = {}
    mid_mine, mid_gathered = [], []
    for l in range(N_LAYERS):
        mine, lands = _allgather_rows_wait(f"l{l}", *small_pending[l], done[0])
        lands = _allgather_rows_forward(f"l{l}", lands)
        mid_mine.append(mine[1:])
        mid_gathered.append(lands[1:])
        gsum = _sum_devices(f"l{l}", lands[0], mine[0], place)
        for n, g in zip(small_shapes[l], _unpack_rows(gsum, list(small_shapes[l].values()))):
            gsmall.setdefault(n, [None] * N_LAYERS)[l] = g
    mid_out = _adamw_mid([prm[n] for n in MID], [mom[n] for n in MID], [var[n] for n in MID], mid_gathered, mid_mine,
                         place)
    for n, res in zip(MID, mid_out):
        out[n] = tuple(kernel_view(n, a) for a in res)
    gsmall = {n: (g[top] if n == "final_norm" else jnp.stack(g)) for n, g in gsmall.items()}
    lanes = dw_shard.shape[-1]
    gsmall["conv_w_dw"] = lax.dynamic_slice_in_dim(gsmall["conv_w_dw"], (2 * cx + cy) * lanes, lanes, axis=2)
    small_names = [n for n in SMALL if n not in MID] + ["conv_w_dw"]
    w_rows = _pack_rows([prm[n] for n in small_names])
    m_rows = _pack_rows([mom[n] for n in small_names])
    v_rows = _pack_rows([var[n] for n in small_names])
    g_rows = _pack_rows([gsmall[n] for n in small_names])
    shapes = [prm[n].shape for n in small_names]
    d_s, m_s, v_s = (_unpack_rows(r, shapes) for r in _adamw_rows(w_rows, m_rows, v_rows, g_rows))
    for i, n in enumerate(small_names):
        out[n] = (gsmall[n].reshape(prm[n].shape), d_s[i], m_s[i], v_s[i])
    grads = [out[n][0] for n in WEIGHTS]
    deltas = [out[n][1] for n in WEIGHTS]
    new_m = [out[n][2] for n in WEIGHTS]
    new_v = [out[n][3] for n in WEIGHTS]
    return (loss, dx[None], *grads, *deltas, *new_m, *new_v)
```

```python
import math

import jax
import jax.numpy as jnp
from jax import lax
from jax.experimental import pallas as pl
from jax.experimental.pallas import tpu as pltpu

F32 = jnp.float32
MXU_DTYPE = jnp.bfloat16
WIRE_DTYPE = jnp.bfloat16
SDS = jax.ShapeDtypeStruct
BS = pl.BlockSpec
ANY = pl.BlockSpec(memory_space=pl.ANY)
HBM = pl.BlockSpec(memory_space=pltpu.HBM)
SEM = pl.BlockSpec(memory_space=pltpu.SEMAPHORE)
SIDE_EFFECT = pltpu.SideEffectType.DATAFLOW_SIDE_EFFECTING
MESH = pl.DeviceIdType.MESH

EPS = 1e-6
N_CHIPS = 4
N_LAYERS = 2
SSM_GROUPS, SSM_STATE, SSM_GROUP = 32, 64, 16
CONV_KERNEL = 31
CONV_PAD = 32
POOL_WINDOWS = (2, 4, 8, 16)
GELU_C = math.sqrt(2.0 / math.pi)
ADAM_LR, ADAM_B1, ADAM_B2, ADAM_EPS, ADAM_WD, ADAM_STEP = 0.001, 0.9, 0.999, 1e-08, 0.01, 10
VMEM_LIMIT = 56 * 1024 * 1024

BIG = ("w_in", "ssm_w_glu", "ssm_w_proj", "conv_w_proj", "pool_w_proj", "w_out", "ffn_w_gate", "ffn_w_up", "ffn_w_down")
TRANSPOSED = ("ffn_w_gate", "ffn_w_up")
MID = ("ssm_b_re", "ssm_b_im", "ssm_c_re", "ssm_c_im")
GATHER_GROUPS = {
    "in": ("w_in",),
    "mixer": ("ssm_w_glu", "ssm_w_proj", "conv_w_proj", "pool_w_proj", "w_out", "conv_w_dw"),
    "ffn": ("ffn_w_gate", "ffn_w_up", "ffn_w_down"),
}
SMALL = ("norm1", "b_gate", "ssm_a_re", "ssm_a_im", "ssm_log_dt", "ssm_b_re", "ssm_b_im", "ssm_c_re", "ssm_c_im",
         "ssm_d", "ssm_b_glu", "conv_b_dw", "conv_ln_g", "conv_ln_b", "pool_w_group", "pool_scale", "norm2",
         "final_norm")
WEIGHTS = ("norm1", "w_in", "b_gate", "ssm_a_re", "ssm_a_im", "ssm_log_dt", "ssm_b_re", "ssm_b_im", "ssm_c_re",
           "ssm_c_im", "ssm_d", "ssm_w_glu", "ssm_b_glu", "ssm_w_proj", "conv_w_dw", "conv_b_dw", "conv_ln_g",
           "conv_ln_b", "conv_w_proj", "pool_w_group", "pool_scale", "pool_w_proj", "w_out", "norm2", "ffn_w_gate",
           "ffn_w_up", "ffn_w_down", "final_norm")


def _params():
    return pltpu.CompilerParams(vmem_limit_bytes=VMEM_LIMIT)


def _mm(a, b):
    return jnp.dot(a.astype(MXU_DTYPE), b.astype(MXU_DTYPE), preferred_element_type=F32)


def _mm_nt(a, b):
    return lax.dot_general(a.astype(MXU_DTYPE), b.astype(MXU_DTYPE), (((1,), (1,)), ((), ())),
                           preferred_element_type=F32)


def _mm_tn(a, b):
    return lax.dot_general(a.astype(MXU_DTYPE), b.astype(MXU_DTYPE), (((0,), (0,)), ((), ())),
                           preferred_element_type=F32)


def _sigmoid(x):
    return jax.nn.sigmoid(x)


def _gelu(x):
    t = jnp.tanh(GELU_C * (x + 0.044715 * (x * x * x)))
    return x * (0.5 * (1.0 + t)), t


def _gelu_grad(x, t):
    return 0.5 * (1.0 + t) + 0.5 * x * (1.0 - t * t) * (GELU_C * (1.0 + 3.0 * 0.044715 * x * x))


def _colsum(v):
    return jnp.sum(v, axis=0, keepdims=True)


def _row_tile(rows, cols, itemsize=4, budget=1536 * 1024):
    best = None
    for t in range(8, rows + 1, 8):
        if rows % t == 0 and t * cols * itemsize <= budget:
            best = t
    return best if best is not None else rows


def _in_proj(l, x, norm1, w_in):
    s, d = x.shape
    nc = w_in.shape[-1]
    tm = min(1024, s)
    nt = s // tm

    def body(x_ref, g_ref, w_ref, z_ref, h_ref, h_all):
        i = pl.program_id(1)
        rows = pl.ds(pl.multiple_of(i * tm, tm), tm)

        @pl.when(pl.program_id(0) == 0)
        def _():
            xv = x_ref[...]
            r = lax.rsqrt(jnp.mean(xv * xv, axis=-1, keepdims=True) + EPS)
            hv = (xv * r * g_ref[...]).astype(h_ref.dtype)
            h_ref[...] = hv.T
            h_all[rows, :] = hv

        z_ref[...] = _mm(h_all[rows, :], w_ref[...])

    tile_of = lambda j, i: i * (1 - jnp.minimum(j, 1)) + (nt - 1) * jnp.minimum(j, 1)
    return pl.pallas_call(
        body, name=f"in_proj_l{l}", grid=(N_CHIPS, nt),
        in_specs=[BS((tm, d), lambda j, i: (tile_of(j, i), 0)), BS((None, 1, d), lambda j, i: (l, 0, 0)),
                  BS((None, d, nc), lambda j, i: (j, 0, 0))],
        out_specs=[BS((tm, nc), lambda j, i: (i, j)), BS((d, tm), lambda j, i: (0, tile_of(j, i)))],
        out_shape=[SDS((s, N_CHIPS * nc), F32), SDS((d, s), MXU_DTYPE)],
        scratch_shapes=[pltpu.VMEM((s, d), MXU_DTYPE)], compiler_params=_params())(x, norm1, w_in)


def _mm_cols(a, w_ref):
    return jnp.concatenate([_mm(a, w_ref[j]) for j in range(N_CHIPS)], axis=1)


def _mm_nt_cols(dv, w_ref):
    nc = w_ref.shape[-1]
    acc = _mm_nt(dv[:, 0:nc], w_ref[0])
    for j in range(1, N_CHIPS):
        acc = acc + _mm_nt(dv[:, j * nc:(j + 1) * nc], w_ref[j])
    return acc


def _merge_values(y, hc, p, zg, wglu, bglu, wpa, wpb, wpc, lng, lnb, wgrp, scale, bg):
    v = {}
    ge, th = _gelu(y)
    t = _mm(ge, wglu) + bglu
    sg = _sigmoid(t)
    sa = ge * sg
    ya = _mm_cols(sa, wpa)
    mu = jnp.mean(hc, axis=-1, keepdims=True)
    xc = hc - mu
    r = lax.rsqrt(jnp.mean(xc * xc, axis=-1, keepdims=True) + EPS)
    xh = xc * r
    ln = xh * lng + lnb
    sl = _sigmoid(ln)
    ac = ln * sl
    yb = _mm_cols(ac, wpb)
    gw = p.shape[1] // len(POOL_WINDOWS)
    q = jnp.concatenate([_mm(p[:, k * gw:(k + 1) * gw], wgrp[k]) for k in range(len(POOL_WINDOWS))], axis=1)
    pp = q * scale
    yc = _mm_cols(pp, wpc)
    d = ya.shape[1]
    gates = [_sigmoid(zg[k] + bg[:, k * d:(k + 1) * d]) for k in range(3)]
    merged = gates[0] * ya + gates[1] * yb + gates[2] * yc
    v.update(ge=ge, th=th, sg=sg, sa=sa, ya=ya, r=r, xh=xh, ln=ln, sl=sl, ac=ac, yb=yb, q=q, pp=pp, yc=yc,
             gates=gates, merged=merged)
    return v


def _merge_specs(l, tm, d, cw):
    row = lambda n: BS((None, 1, n), lambda i: (l, 0, 0))
    resident = lambda shp: BS(shp, lambda i: (0, 0, 0), pipeline_mode=pl.Buffered(1))
    return [
        BS((tm, cw), lambda i: (i, 0)),
        BS((tm, cw), lambda i: (i, 0)),
        BS((tm, cw), lambda i: (i, 0)),
        BS((tm, d), lambda i: (i, 2)), BS((tm, d), lambda i: (i, 3)), BS((tm, d), lambda i: (i, 4)),
        resident((N_CHIPS, cw // N_CHIPS, cw)),
        row(cw),
        resident((N_CHIPS, cw, d // N_CHIPS)),
        resident((N_CHIPS, cw, d // N_CHIPS)),
        resident((N_CHIPS, cw, d // N_CHIPS)),
        row(cw), row(cw),
        BS((None, 4, cw // 4, cw // 4), lambda i: (l, 0, 0, 0)),
        row(cw),
        row(3 * d),
        resident((N_CHIPS, d // N_CHIPS, d)),
    ]


def _merge_fwd(l, x, y, hc, p, z, fw, sp):
    s, d = x.shape
    cw = y.shape[1]
    tm = min(512, s)

    def body(x_ref, y_ref, hc_ref, p_ref, z0, z1, z2, wglu, bglu, wpa, wpb, wpc, lng, lnb, wgrp, scale, bg, wout,
             x1_ref):
        v = _merge_values(y_ref[...], hc_ref[...], p_ref[...], (z0[...], z1[...], z2[...]),
                          wglu[...].reshape(cw, cw), bglu[...], wpa, wpb, wpc, lng[...], lnb[...], wgrp, scale[...],
                          bg[...])
        x1_ref[...] = x_ref[...] + _mm(v["merged"], wout[...].reshape(d, d))

    return pl.pallas_call(
        body, name=f"merge_fwd_l{l}", grid=(s // tm,),
        in_specs=[BS((tm, d), lambda i: (i, 0))] + _merge_specs(l, tm, d, cw),
        out_specs=BS((tm, d), lambda i: (i, 0)), out_shape=SDS((s, d), F32), compiler_params=_params(),
    )(x, y, hc, p, z, z, z, fw["ssm_w_glu"], sp["ssm_b_glu"], fw["ssm_w_proj"], fw["conv_w_proj"], fw["pool_w_proj"],
      sp["conv_ln_g"], sp["conv_ln_b"], sp["pool_w_group"], sp["pool_scale"], sp["b_gate"], fw["w_out"])


def _merge_bwd(l, dx1, y, hc, p, z, fw, sp):
    s, d = dx1.shape
    cw = y.shape[1]
    tm = min(256, s)
    m = MXU_DTYPE

    def body(dx1_ref, y_ref, hc_ref, p_ref, z0, z1, z2, wglu, bglu, wpa, wpb, wpc, lng, lnb, wgrp, scale, bg, wout,
             dzg_ref, dy_ref, dhc_ref, dp_ref, merged_ref, sa_ref, ac_ref, pp_ref, ge_ref, dt_ref, dya_ref, dyb_ref,
             dyc_ref, dq_ref, dbg_ref, dbglu_ref, dlng_ref, dlnb_ref, dscale_ref):
        yv = y_ref[...]
        wg = wglu[...].reshape(cw, cw)
        v = _merge_values(yv, hc_ref[...], p_ref[...], (z0[...], z1[...], z2[...]), wg, bglu[...], wpa, wpb, wpc,
                          lng[...], lnb[...], wgrp, scale[...], bg[...])
        dm = _mm_nt(dx1_ref[...], wout[...].reshape(d, d))
        ys = (v["ya"], v["yb"], v["yc"])
        dys, dbg = [], []
        for k in range(3):
            gk = v["gates"][k]
            dzk = dm * ys[k] * (gk * (1.0 - gk))
            dbg.append(_colsum(dzk))
            dzg_ref[:, k * d:(k + 1) * d] = dzk.astype(m)
            dys.append((dm * gk).astype(m))
        dsa = _mm_nt_cols(dys[0], wpa)
        dac = _mm_nt_cols(dys[1], wpb)
        dpp = _mm_nt_cols(dys[2], wpc)
        ge, sg = v["ge"], v["sg"]
        dt = dsa * ge * (sg * (1.0 - sg))
        dge = dsa * sg + _mm_nt(dt, wg)
        dy_ref[...] = dge * _gelu_grad(yv, v["th"])
        ln, sl, xh = v["ln"], v["sl"], v["xh"]
        dln = dac * (sl * (1.0 + ln * (1.0 - sl)))
        dxh = dln * lng[...]
        dhc_ref[...] = v["r"] * (dxh - jnp.mean(dxh, axis=-1, keepdims=True)
                                 - xh * jnp.mean(dxh * xh, axis=-1, keepdims=True))
        dq = dpp * scale[...]
        gw = cw // len(POOL_WINDOWS)
        for k in range(len(POOL_WINDOWS)):
            dp_ref[:, k * gw:(k + 1) * gw] = _mm_nt(dq[:, k * gw:(k + 1) * gw], wgrp[k])
        merged_ref[...] = v["merged"].astype(m)
        sa_ref[...] = v["sa"].astype(m)
        ac_ref[...] = v["ac"].astype(m)
        pp_ref[...] = v["pp"].astype(m)
        ge_ref[...] = ge.astype(m)
        dt_ref[...] = dt.astype(m)
        dya_ref[...] = dys[0]
        dyb_ref[...] = dys[1]
        dyc_ref[...] = dys[2]
        dq_ref[...] = dq.astype(m)

        @pl.when(pl.program_id(0) == 0)
        def _():
            for ref in (dbg_ref, dbglu_ref, dlng_ref, dlnb_ref, dscale_ref):
                ref[...] = jnp.zeros(ref.shape, F32)

        dbg_ref[...] += jnp.concatenate(dbg, axis=1)
        dbglu_ref[...] += _colsum(dt)
        dlng_ref[...] += _colsum(dln * xh)
        dlnb_ref[...] += _colsum(dln)
        dscale_ref[...] += _colsum(dpp * v["q"])

    tile = lambda n: BS((tm, n), lambda i: (i, 0))
    acc = lambda n: BS((1, n), lambda i: (0, 0))
    outs = pl.pallas_call(
        body, name=f"merge_bwd_l{l}", grid=(s // tm,),
        in_specs=[tile(d)] + _merge_specs(l, tm, d, cw),
        out_specs=[tile(3 * d), tile(cw), tile(cw), tile(cw), tile(d), tile(cw), tile(cw), tile(cw), tile(cw), tile(cw),
                   tile(d), tile(d), tile(d), tile(cw), acc(3 * d), acc(cw), acc(cw), acc(cw), acc(cw)],
        out_shape=[SDS((s, 3 * d), m), SDS((s, cw), F32), SDS((s, cw), F32), SDS((s, cw), F32), SDS((s, d), m),
                   SDS((s, cw), m), SDS((s, cw), m), SDS((s, cw), m), SDS((s, cw), m), SDS((s, cw), m), SDS((s, d), m),
                   SDS((s, d), m), SDS((s, d), m), SDS((s, cw), m), SDS((1, 3 * d), F32), SDS((1, cw), F32),
                   SDS((1, cw), F32), SDS((1, cw), F32), SDS((1, cw), F32)],
        compiler_params=_params(),
    )(dx1, y, hc, p, z, z, z, fw["ssm_w_glu"], sp["ssm_b_glu"], fw["ssm_w_proj"], fw["conv_w_proj"], fw["pool_w_proj"],
      sp["conv_ln_g"], sp["conv_ln_b"], sp["pool_w_group"], sp["pool_scale"], sp["b_gate"], fw["w_out"])
    names = ("dzg", "dy", "dhc", "dp", "merged", "sa", "ac", "pp", "ge", "dt", "dya", "dyb", "dyc", "dq", "db_gate",
             "db_glu", "dln_g", "dln_b", "dscale")
    return dict(zip(names, outs))


def _ffn_fwd(l, x1, norm2, wg, wu, wd):
    s, d = x1.shape
    hc = wd.shape[1]
    tm = min(1024, s)

    def body(x_ref, g_ref, wg_ref, wu_ref, wd_ref, o_ref, gate_ref, up_ref, h_ref):
        @pl.when(pl.program_id(1) == 0)
        def _():
            xv = x_ref[...]
            r = lax.rsqrt(jnp.mean(xv * xv, axis=-1, keepdims=True) + EPS)
            h_ref[...] = (xv * r * g_ref[...]).astype(h_ref.dtype)
            o_ref[...] = xv

        h = h_ref[...]
        gate = _mm_nt(h, wg_ref[...])
        up = _mm_nt(h, wu_ref[...])
        gate_ref[...] = gate
        up_ref[...] = up
        o_ref[...] += _mm(gate * _sigmoid(gate) * up, wd_ref[...])

    chunk = BS((None, tm, hc), lambda i, j: (j, i, 0))
    return pl.pallas_call(
        body, name=f"ffn_fwd_l{l}", grid=(s // tm, N_CHIPS),
        in_specs=[BS((tm, d), lambda i, j: (i, 0)), BS((None, 1, d), lambda i, j: (l, 0, 0)),
                  BS((None, hc, d), lambda i, j: (j, 0, 0)), BS((None, hc, d), lambda i, j: (j, 0, 0)),
                  BS((None, hc, d), lambda i, j: (j, 0, 0))],
        out_specs=[BS((tm, d), lambda i, j: (i, 0)), chunk, chunk, BS((tm, d), lambda i, j: (i, 0))],
        out_shape=[SDS((s, d), F32), SDS((N_CHIPS, s, hc), F32), SDS((N_CHIPS, s, hc), F32), SDS((s, d), MXU_DTYPE)],
        compiler_params=_params())(x1, norm2, wg, wu, wd)


def _ffn_bwd(l, x1, dx2, gate_pre, up_pre, norm2, wg, wu, wd):
    s, d = x1.shape
    hc = wd.shape[1]
    tm = min(512, s)
    m = MXU_DTYPE
    last = N_CHIPS - 1

    def body(x_ref, dx2_ref, gate_ref, up_ref, g_ref, wg_ref, wu_ref, wd_ref, dx1_ref, dxb_ref, act_ref, dgate_ref,
             dup_ref, dn_ref, dh_scr):
        i, j = pl.program_id(0), pl.program_id(1)

        @pl.when(j == 0)
        def _():
            dxb_ref[...] = dx2_ref[...].astype(m)
            dh_scr[...] = jnp.zeros(dh_scr.shape, F32)

        @pl.when((i == 0) & (j == 0))
        def _():
            dn_ref[...] = jnp.zeros(dn_ref.shape, F32)

        gate = gate_ref[...]
        up = up_ref[...]
        sg = _sigmoid(gate)
        silu = gate * sg
        act_ref[...] = (silu * up).astype(m).T
        dact = _mm_nt(dxb_ref[...], wd_ref[...])
        dup = (dact * silu).astype(m)
        dgate = (dact * up * (sg * (1.0 + gate * (1.0 - sg)))).astype(m)
        dup_ref[...] = dup.T
        dgate_ref[...] = dgate.T
        dh_scr[...] += _mm(dgate, wg_ref[...]) + _mm(dup, wu_ref[...])

        @pl.when(j == last)
        def _():
            xv = x_ref[...]
            r = lax.rsqrt(jnp.mean(xv * xv, axis=-1, keepdims=True) + EPS)
            xh = xv * r
            dh = dh_scr[...]
            dn_ref[...] += _colsum(dh * xh)
            dxh = dh * g_ref[...]
            dx1_ref[...] = dx2_ref[...] + r * (dxh - xh * jnp.mean(dxh * xh, axis=-1, keepdims=True))

    chunk = BS((None, hc, tm), lambda i, j: (j, 0, i))
    saved = BS((None, tm, hc), lambda i, j: (j, i, 0))
    outs = pl.pallas_call(
        body, name=f"ffn_bwd_l{l}", grid=(s // tm, N_CHIPS),
        in_specs=[BS((tm, d), lambda i, j: (i, 0)), BS((tm, d), lambda i, j: (i, 0)), saved, saved,
                  BS((None, 1, d), lambda i, j: (l, 0, 0)),
                  BS((None, hc, d), lambda i, j: (j, 0, 0)), BS((None, hc, d), lambda i, j: (j, 0, 0)),
                  BS((None, hc, d), lambda i, j: (j, 0, 0))],
        out_specs=[BS((tm, d), lambda i, j: (i, 0)), BS((tm, d), lambda i, j: (i, 0)),
                   chunk, chunk, chunk, BS((1, d), lambda i, j: (0, 0))],
        out_shape=[SDS((s, d), F32), SDS((s, d), m), SDS((N_CHIPS, hc, s), m),
                   SDS((N_CHIPS, hc, s), m), SDS((N_CHIPS, hc, s), m), SDS((1, d), F32)],
        scratch_shapes=[pltpu.VMEM((tm, d), F32)], compiler_params=_params(),
    )(x1, dx2, gate_pre, up_pre, norm2, wg, wu, wd)
    return dict(zip(("dx1", "dx2", "act", "dgate", "dup", "dnorm2"), outs))


def _loss_head(x, target, gf):
    s, d = x.shape
    tm = min(512, s)

    def body(x_ref, t_ref, g_ref, dx_ref, loss_ref, dg_ref):
        @pl.when(pl.program_id(0) == 0)
        def _():
            loss_ref[...] = jnp.zeros(loss_ref.shape, F32)
            dg_ref[...] = jnp.zeros(dg_ref.shape, F32)

        xv = x_ref[...]
        r = lax.rsqrt(jnp.mean(xv * xv, axis=-1, keepdims=True) + EPS)
        xh = xv * r
        err = xh * g_ref[...] - t_ref[...]
        loss_ref[...] += 0.5 * jnp.sum(jnp.mean(err * err, axis=-1, keepdims=True), axis=0, keepdims=True)
        dyv = err * (1.0 / d)
        dg_ref[...] += _colsum(dyv * xh)
        dxh = dyv * g_ref[...]
        dx_ref[...] = r * (dxh - xh * jnp.mean(dxh * xh, axis=-1, keepdims=True))

    return pl.pallas_call(
        body, name="loss_head", grid=(s // tm,),
        in_specs=[BS((tm, d), lambda i: (i, 0)), BS((tm, d), lambda i: (i, 0)), BS((1, d), lambda i: (0, 0))],
        out_specs=[BS((tm, d), lambda i: (i, 0)), BS((1, 1), lambda i: (0, 0)), BS((1, d), lambda i: (0, 0))],
        out_shape=[SDS((s, d), F32), SDS((1, 1), F32), SDS((1, d), F32)], compiler_params=_params())(x, target, gf)


def _in_proj_bwd(l, dres, x, norm1, w_in, du_a, dv1, dv2, du_c, dzg):
    s, d = x.shape
    nc = w_in.shape[-1]
    tm = min(256, s)
    m = MXU_DTYPE

    def body(dres_ref, x_ref, g_ref, w_ref, a_ref, b1_ref, b2_ref, c_ref, g3_ref, dx_ref, dz_ref, dn_ref):
        @pl.when(pl.program_id(0) == 0)
        def _():
            dn_ref[...] = jnp.zeros(dn_ref.shape, F32)

        dz = jnp.concatenate([a_ref[...], b1_ref[...], b2_ref[...], c_ref[...], g3_ref[...]], axis=1).astype(m)
        dz_ref[...] = dz
        dh = _mm_nt_cols(dz, w_ref)
        xv = x_ref[...]
        r = lax.rsqrt(jnp.mean(xv * xv, axis=-1, keepdims=True) + EPS)
        xh = xv * r
        dn_ref[...] += _colsum(dh * xh)
        dxh = dh * g_ref[...]
        dx_ref[...] = dres_ref[...] + r * (dxh - xh * jnp.mean(dxh * xh, axis=-1, keepdims=True))

    tile = lambda n: BS((tm, n), lambda i: (i, 0))
    return pl.pallas_call(
        body, name=f"in_proj_bwd_l{l}", grid=(s // tm,),
        in_specs=[tile(d), tile(d), BS((None, 1, d), lambda i: (l, 0, 0)),
                  BS((N_CHIPS, d, nc), lambda i: (0, 0, 0), pipeline_mode=pl.Buffered(1)),
                  tile(du_a.shape[1]), tile(dv1.shape[1]), tile(dv2.shape[1]), tile(du_c.shape[1]), tile(dzg.shape[1])],
        out_specs=[tile(d), tile(N_CHIPS * nc), BS((1, d), lambda i: (0, 0))],
        out_shape=[SDS((s, d), F32), SDS((s, N_CHIPS * nc), m), SDS((1, d), F32)], compiler_params=_params(),
    )(dres, x, norm1, w_in, du_a, dv1, dv2, du_c, dzg)


def _tn_matmul(name, a, a_spec, b, b_spec, chunk_shape, grid, place):
    last = grid[1] - 1

    def body(place_ref, a_ref, b_ref, own_ref, wire_ref, *acc):
        part = _mm(a_ref[...], b_ref[...])

        def emit(total):
            wire_ref[...] = total.astype(WIRE_DTYPE)

            @pl.when(pl.program_id(0) == place_ref[0])
            def _():
                own_ref[...] = total

        if last == 0:
            emit(part)
        else:
            @pl.when(pl.program_id(1) == 0)
            def _():
                acc[0][...] = part

            @pl.when(pl.program_id(1) > 0)
            def _():
                acc[0][...] += part

            @pl.when(pl.program_id(1) == last)
            def _():
                emit(acc[0][...])

    zeros = (0,) * len(chunk_shape)
    return pl.pallas_call(
        body, name=name,
        grid_spec=pltpu.PrefetchScalarGridSpec(
            num_scalar_prefetch=1, grid=grid, in_specs=[a_spec, b_spec],
            out_specs=[BS(chunk_shape, lambda j, t, pr: zeros), BS((None,) + chunk_shape, lambda j, t, pr: (j,) + zeros)],
            scratch_shapes=[pltpu.VMEM(chunk_shape, F32)] if last else []),
        out_shape=[SDS(chunk_shape, F32), SDS((N_CHIPS,) + chunk_shape, WIRE_DTYPE)],
        compiler_params=_params())(place, a, b)


def _scan_consts(pw_ref, lanes, reverse):
    sgn = -1.0 if reverse else 1.0
    row = lax.broadcasted_iota(jnp.int32, (8, lanes), 0)
    steps = []
    for i, k in enumerate((1, 2, 4)):
        mask = (row < 8 - k) if reverse else (row >= k)
        steps.append((k, jnp.where(mask, pw_ref[2 * i], 0.0), jnp.where(mask, sgn * pw_ref[2 * i + 1], 0.0)))
    c = 4 if reverse else 3
    return steps, pw_ref[2 * c], sgn * pw_ref[2 * c + 1]


def _scan_block(br, bi, steps, reverse):
    for k, ar, ai in steps:
        sh = 8 - k if reverse else k
        sr = pltpu.roll(br, sh, 0)
        si = pltpu.roll(bi, sh, 0)
        br, bi = br + ar * sr - ai * si, bi + ar * si + ai * sr
    return br, bi


def _ssm_fwd(l, z, bblk_re, bblk_im, cblk_re, cblk_im, pw, dskip):
    s = z.shape[0]
    gc = bblk_re.shape[1]
    gl = bblk_re.shape[2]
    nblk = bblk_re.shape[0]

    def body(u_ref, bre, bim, cre, cim, pw_ref, d_ref, hre, him, y_ref):
        u = u_ref[...]
        hre[...] = _mm(u, bre[...])
        him[...] = _mm(u, bim[...])
        steps, car, cai = _scan_consts(pw_ref, gl, False)

        def step(i, carry):
            cr, ci = carry
            r0 = pl.multiple_of(i * 8, 8)
            br, bi = _scan_block(hre[pl.ds(r0, 8), :], him[pl.ds(r0, 8), :], steps, False)
            hr = br + car * cr - cai * ci
            hi = bi + car * ci + cai * cr
            hre[pl.ds(r0, 8), :] = hr
            him[pl.ds(r0, 8), :] = hi
            return jnp.broadcast_to(hr[7:8, :], (8, gl)), jnp.broadcast_to(hi[7:8, :], (8, gl))

        zero = jnp.zeros((8, gl), F32)
        lax.fori_loop(0, s // 8, step, (zero, zero))
        y_ref[...] = _mm_nt(hre[...], cre[...]) - _mm_nt(him[...], cim[...]) + d_ref[...] * u

    return pl.pallas_call(
        body, name=f"ssm_fwd_l{l}", grid=(nblk,),
        in_specs=[BS((s, gc), lambda k: (0, k)), BS((None, gc, gl), lambda k: (k, 0, 0)),
                  BS((None, gc, gl), lambda k: (k, 0, 0)), BS((None, gc, gl), lambda k: (k, 0, 0)),
                  BS((None, gc, gl), lambda k: (k, 0, 0)), BS((10, 8, gl), lambda k: (0, 0, k)),
                  BS((1, gc), lambda k: (0, k))],
        out_specs=[BS((s, gl), lambda k: (0, k)), BS((s, gl), lambda k: (0, k)), BS((s, gc), lambda k: (0, k))],
        out_shape=[SDS((s, nblk * gl), F32), SDS((s, nblk * gl), F32), SDS((s, nblk * gc), F32)],
        compiler_params=_params())(z, bblk_re, bblk_im, cblk_re, cblk_im, pw, dskip)


def _ssm_bwd(l, dy, z, hre, him, bblk_re, bblk_im, cblk_re, cblk_im, pw, dskip):
    s = z.shape[0]
    nblk, gc, gl = bblk_re.shape

    def body(dy_ref, u_ref, hre_ref, him_ref, bre, bim, cre, cim, pw_ref, d_ref,
             du_ref, dbre_ref, dbim_ref, dcre_ref, dcim_ref, dar_ref, dai_ref, dd_ref, gre, gim):
        dyv = dy_ref[...]
        u = u_ref[...]
        gre[...] = _mm(dyv, cre[...])
        gim[...] = -_mm(dyv, cim[...])
        dcre_ref[...] = _mm_tn(dyv, hre_ref[...])
        dcim_ref[...] = -_mm_tn(dyv, him_ref[...])
        dd_ref[...] = _colsum(dyv * u)
        row = lax.broadcasted_iota(jnp.int32, (8, gl), 0)
        steps, car, cai = _scan_consts(pw_ref, gl, True)
        n8 = s // 8

        def step(ii, carry):
            cr, ci, accr, acci = carry
            i = n8 - 1 - ii
            r0 = pl.multiple_of(i * 8, 8)
            br, bi = _scan_block(gre[pl.ds(r0, 8), :], gim[pl.ds(r0, 8), :], steps, True)
            dr = br + car * cr - cai * ci
            di = bi + car * ci + cai * cr
            gre[pl.ds(r0, 8), :] = dr
            gim[pl.ds(r0, 8), :] = di
            rp = pl.multiple_of(jnp.maximum(i - 1, 0) * 8, 8)
            keep = jnp.where(i > 0, 1.0, 0.0)
            pr = jnp.where(row >= 1, pltpu.roll(hre_ref[pl.ds(r0, 8), :], 1, 0),
                           keep * pltpu.roll(hre_ref[pl.ds(rp, 8), :], 1, 0))
            pi = jnp.where(row >= 1, pltpu.roll(him_ref[pl.ds(r0, 8), :], 1, 0),
                           keep * pltpu.roll(him_ref[pl.ds(rp, 8), :], 1, 0))
            accr = accr + dr * pr + di * pi
            acci = acci + di * pr - dr * pi
            return (jnp.broadcast_to(dr[0:1, :], (8, gl)), jnp.broadcast_to(di[0:1, :], (8, gl)), accr, acci)

        zero = jnp.zeros((8, gl), F32)
        _, _, accr, acci = lax.fori_loop(0, n8, step, (zero, zero, zero, zero))
        dar_ref[...] = _colsum(accr)
        dai_ref[...] = _colsum(acci)
        dbr = gre[...]
        dbi = gim[...]
        du_ref[...] = (dyv * d_ref[...] + _mm_nt(dbr, bre[...]) + _mm_nt(dbi, bim[...])).astype(du_ref.dtype)
        dbre_ref[...] = _mm_tn(u, dbr)
        dbim_ref[...] = _mm_tn(u, dbi)

    col = lambda n: BS((s, n), lambda k: (0, k))
    blk = lambda a, b: BS((None, a, b), lambda k: (k, 0, 0))
    outs = pl.pallas_call(
        body, name=f"ssm_bwd_l{l}", grid=(nblk,),
        in_specs=[col(gc), col(gc), col(gl), col(gl), blk(gc, gl), blk(gc, gl), blk(gc, gl), blk(gc, gl),
                  BS((10, 8, gl), lambda k: (0, 0, k)), BS((1, gc), lambda k: (0, k))],
        out_specs=[col(gc), blk(gc, gl), blk(gc, gl), blk(gc, gl), blk(gc, gl), BS((1, gl), lambda k: (0, k)),
                   BS((1, gl), lambda k: (0, k)), BS((1, gc), lambda k: (0, k))],
        out_shape=[SDS((s, nblk * gc), MXU_DTYPE), SDS((nblk, gc, gl), F32), SDS((nblk, gc, gl), F32),
                   SDS((nblk, gc, gl), F32), SDS((nblk, gc, gl), F32), SDS((1, nblk * gl), F32),
                   SDS((1, nblk * gl), F32), SDS((1, nblk * gc), F32)],
        scratch_shapes=[pltpu.VMEM((s, gl), F32), pltpu.VMEM((s, gl), F32)], compiler_params=_params(),
    )(dy, z, hre, him, bblk_re, bblk_im, cblk_re, cblk_im, pw, dskip)
    return dict(zip(("du", "dbblk_re", "dbblk_im", "dcblk_re", "dcblk_im", "dabar_re", "dabar_im", "dd"), outs))


def _conv_fwd(l, z, wdw, bdw):
    s = z.shape[0]
    cw = wdw.shape[1]
    lb = 128
    tr = min(256, s)
    off1 = cw // lb
    off2 = 2 * cw // lb

    def body(v1_ref, v2_ref, w_ref, b_ref, hc_ref, scr):
        scr[0:CONV_PAD, :] = jnp.zeros((CONV_PAD, lb), F32)
        scr[CONV_PAD:, :] = v1_ref[...] * _sigmoid(v2_ref[...])
        for t in range(s // tr):
            acc = jnp.broadcast_to(b_ref[...], (tr, lb))
            for k in range(CONV_KERNEL):
                acc = acc + w_ref[pl.ds(k, 1), :] * scr[pl.ds(t * tr + CONV_PAD - (CONV_KERNEL - 1) + k, tr), :]
            hc_ref[pl.ds(t * tr, tr), :] = acc

    return pl.pallas_call(
        body, name=f"conv_fwd_l{l}", grid=(cw // lb,),
        in_specs=[BS((s, lb), lambda k: (0, off1 + k)), BS((s, lb), lambda k: (0, off2 + k)),
                  BS((CONV_KERNEL, lb), lambda k: (0, k)), BS((1, lb), lambda k: (0, k))],
        out_specs=BS((s, lb), lambda k: (0, k)), out_shape=SDS((s, cw), F32),
        scratch_shapes=[pltpu.VMEM((s + CONV_PAD, lb), F32)], compiler_params=_params())(z, z, wdw, bdw)


def _conv_bwd(l, dhc, z, wdw):
    s = z.shape[0]
    cw = wdw.shape[1]
    lb = 128
    tr = min(256, s)
    off1 = cw // lb
    off2 = 2 * cw // lb
    nb = cw // lb

    def body(d_ref, v1_ref, v2_ref, w_ref, dv1_ref, dv2_ref, dw_ref, db_ref, hpad, dpad):
        v1 = v1_ref[...]
        sg = _sigmoid(v2_ref[...])
        dv = d_ref[...]
        hpad[0:CONV_PAD, :] = jnp.zeros((CONV_PAD, lb), F32)
        hpad[CONV_PAD:, :] = v1 * sg
        dpad[0:s, :] = dv
        dpad[s:, :] = jnp.zeros((CONV_PAD, lb), F32)
        db_ref[...] = _colsum(dv)
        dws = [jnp.zeros((1, lb), F32) for _ in range(CONV_KERNEL)]
        for t in range(s // tr):
            dt = d_ref[pl.ds(t * tr, tr), :]
            acc = jnp.zeros((tr, lb), F32)
            for k in range(CONV_KERNEL):
                acc = acc + w_ref[pl.ds(k, 1), :] * dpad[pl.ds(t * tr + (CONV_KERNEL - 1) - k, tr), :]
                dws[k] = dws[k] + _colsum(dt * hpad[pl.ds(t * tr + CONV_PAD - (CONV_KERNEL - 1) + k, tr), :])
            sgt = _sigmoid(v2_ref[pl.ds(t * tr, tr), :])
            v1t = v1_ref[pl.ds(t * tr, tr), :]
            dv1_ref[pl.ds(t * tr, tr), :] = (acc * sgt).astype(dv1_ref.dtype)
            dv2_ref[pl.ds(t * tr, tr), :] = (acc * v1t * (sgt * (1.0 - sgt))).astype(dv2_ref.dtype)
        for k in range(CONV_KERNEL):
            dw_ref[pl.ds(k, 1), :] = dws[k]

    return pl.pallas_call(
        body, name=f"conv_bwd_l{l}", grid=(nb,),
        in_specs=[BS((s, lb), lambda k: (0, k)), BS((s, lb), lambda k: (0, off1 + k)),
                  BS((s, lb), lambda k: (0, off2 + k)), BS((CONV_KERNEL, lb), lambda k: (0, k))],
        out_specs=[BS((s, lb), lambda k: (0, k)), BS((s, lb), lambda k: (0, k)),
                   BS((CONV_KERNEL, lb), lambda k: (0, k)), BS((1, lb), lambda k: (0, k))],
        out_shape=[SDS((s, cw), MXU_DTYPE), SDS((s, cw), MXU_DTYPE), SDS((CONV_KERNEL, cw), F32), SDS((1, cw), F32)],
        scratch_shapes=[pltpu.VMEM((s + CONV_PAD, lb), F32), pltpu.VMEM((s + CONV_PAD, lb), F32)],
        compiler_params=_params())(dhc, z, z, wdw)


def _pool_window(k):
    return jnp.where(k == 0, float(POOL_WINDOWS[0]),
                     jnp.where(k == 1, float(POOL_WINDOWS[1]),
                               jnp.where(k == 2, float(POOL_WINDOWS[2]), float(POOL_WINDOWS[3]))))


def _pool_fwd(l, z, pw_width):
    s = z.shape[0]
    lb = pw_width // len(POOL_WINDOWS)
    off = 3 * pw_width // lb

    def body(u_ref, p_ref):
        k = pl.program_id(0)
        u = u_ref[...]
        row = lax.broadcasted_iota(jnp.int32, (s, lb), 0)
        sums = [u]
        for sh in (1, 2, 4, 8):
            prev = sums[-1]
            sums.append(prev + jnp.where(row >= sh, pltpu.roll(prev, sh, 0), 0.0))
        sel = jnp.where(k == 0, sums[1], jnp.where(k == 1, sums[2], jnp.where(k == 2, sums[3], sums[4])))
        cnt = jnp.minimum((row + 1).astype(F32), _pool_window(k))
        p_ref[...] = sel / cnt - u

    return pl.pallas_call(
        body, name=f"pool_fwd_l{l}", grid=(len(POOL_WINDOWS),),
        in_specs=[BS((s, lb), lambda k: (0, off + k))], out_specs=BS((s, lb), lambda k: (0, k)),
        out_shape=SDS((s, pw_width), F32), compiler_params=_params())(z)


def _pool_bwd(l, dp):
    s, width = dp.shape
    lb = width // len(POOL_WINDOWS)

    def body(d_ref, du_ref):
        k = pl.program_id(0)
        dv = d_ref[...]
        row = lax.broadcasted_iota(jnp.int32, (s, lb), 0)
        cnt = jnp.minimum((row + 1).astype(F32), _pool_window(k))
        sums = [dv / cnt]
        for sh in (1, 2, 4, 8):
            prev = sums[-1]
            sums.append(prev + jnp.where(row < s - sh, pltpu.roll(prev, s - sh, 0), 0.0))
        sel = jnp.where(k == 0, sums[1], jnp.where(k == 1, sums[2], jnp.where(k == 2, sums[3], sums[4])))
        du_ref[...] = (sel - dv).astype(du_ref.dtype)

    return pl.pallas_call(
        body, name=f"pool_bwd_l{l}", grid=(len(POOL_WINDOWS),),
        in_specs=[BS((s, lb), lambda k: (0, k))], out_specs=BS((s, lb), lambda k: (0, k)),
        out_shape=SDS((s, width), MXU_DTYPE), compiler_params=_params())(dp)


def _zoh(a_re, a_im, log_dt):
    dt = jnp.exp(log_dt)
    mag = jnp.exp(dt * a_re)
    ang = dt * a_im
    abar_re = mag * jnp.cos(ang)
    abar_im = mag * jnp.sin(ang)
    den = a_re * a_re + a_im * a_im
    nr = abar_re - 1.0
    ni = abar_im
    f_re = (nr * a_re + ni * a_im) / den
    f_im = (ni * a_re - nr * a_im) / den
    return abar_re, abar_im, f_re, f_im


def _zoh_fwd(l, a_re, a_im, log_dt):
    def body(ar, ai, ld, o0, o1, o2, o3):
        for ref, val in zip((o0, o1, o2, o3), _zoh(ar[...], ai[...], ld[...])):
            ref[...] = val

    return pl.pallas_call(body, name=f"zoh_fwd_l{l}", out_shape=[SDS(a_re.shape, F32)] * 4)(a_re, a_im, log_dt)


def _zoh_bwd(l, a_re, a_im, log_dt, cts):
    def body(ar, ai, ld, c0, c1, c2, c3, dar, dai, dld):
        _, vjp = jax.vjp(_zoh, ar[...], ai[...], ld[...])
        g = vjp((c0[...], c1[...], c2[...], c3[...]))
        dar[...] = g[0]
        dai[...] = g[1]
        dld[...] = g[2]

    return pl.pallas_call(body, name=f"zoh_bwd_l{l}",
                          out_shape=[SDS(a_re.shape, F32), SDS(a_re.shape, F32), SDS(log_dt.shape, F32)],
                          )(a_re, a_im, log_dt, *cts)


def _bbar_fwd(l, f_re, f_im, b_re, b_im):
    g, p, n = b_re.shape[1:]

    def body(fr, fi, br, bi, o_re, o_im):
        o_re[...] = (fr[...] * br[...] - fi[...] * bi[...]).astype(o_re.dtype)
        o_im[...] = (fr[...] * bi[...] + fi[...] * br[...]).astype(o_im.dtype)

    whole = lambda shp: BS(shp, lambda i: (0,) * len(shp))
    layer = BS((None, g, p, n), lambda i: (l, 0, 0, 0))
    return pl.pallas_call(body, name=f"bbar_fwd_l{l}", grid=(1,),
                          in_specs=[whole((g, 1, n)), whole((g, 1, n)), layer, layer],
                          out_specs=[whole((g, p, n))] * 2,
                          out_shape=[SDS((g, p, n), MXU_DTYPE)] * 2)(f_re, f_im, b_re, b_im)


def _bbar_bwd(l, f_re, f_im, b_re, b_im, d_re, d_im):
    g, p, n = b_re.shape[1:]

    def body(fr, fi, br, bi, dr, di, dfr, dfi, dbr, dbi):
        dfr[...] = jnp.sum(dr[...] * br[...] + di[...] * bi[...], axis=1, keepdims=True)
        dfi[...] = jnp.sum(di[...] * br[...] - dr[...] * bi[...], axis=1, keepdims=True)
        dbr[...] = fr[...] * dr[...] + fi[...] * di[...]
        dbi[...] = fr[...] * di[...] - fi[...] * dr[...]

    whole = lambda shp: BS(shp, lambda i: (0,) * len(shp))
    layer = BS((None, g, p, n), lambda i: (l, 0, 0, 0))
    return pl.pallas_call(body, name=f"bbar_bwd_l{l}", grid=(1,),
                          in_specs=[whole((g, 1, n)), whole((g, 1, n)), layer, layer, whole((g, p, n)),
                                    whole((g, p, n))],
                          out_specs=[whole((g, 1, n)), whole((g, 1, n)), whole((g, p, n)), whole((g, p, n))],
                          out_shape=[SDS((g, 1, n), F32), SDS((g, 1, n), F32), SDS((g, p, n), F32),
                                     SDS((g, p, n), F32)])(f_re, f_im, b_re, b_im, d_re, d_im)


def _powers(l, abar_re, abar_im):
    lanes = abar_re.shape[1]

    def body(ar_ref, ai_ref, o_ref):
        ar, ai = ar_ref[...], ai_ref[...]
        pows = [(ar, ai)]
        for _ in range(7):
            pr, pi = pows[-1]
            pows.append((pr * ar - pi * ai, pr * ai + pi * ar))
        row = lax.broadcasted_iota(jnp.int32, (8, lanes), 0)
        for i, k in enumerate((1, 2, 4)):
            o_ref[2 * i] = jnp.broadcast_to(pows[k - 1][0], (8, lanes))
            o_ref[2 * i + 1] = jnp.broadcast_to(pows[k - 1][1], (8, lanes))
        for slot, order in ((3, range(8)), (4, range(7, -1, -1))):
            vr = jnp.zeros((8, lanes), F32)
            vi = jnp.zeros((8, lanes), F32)
            for r, e in enumerate(order):
                vr = jnp.where(row == r, pows[e][0], vr)
                vi = jnp.where(row == r, pows[e][1], vi)
            o_ref[2 * slot] = vr
            o_ref[2 * slot + 1] = vi

    return pl.pallas_call(body, name=f"powers_l{l}", out_shape=SDS((10, 8, lanes), F32))(abar_re, abar_im)


def _block_diag(v):
    g, a, b = v.shape
    eye = jnp.eye(8, dtype=v.dtype)
    out = jnp.einsum("kgab,gh->kgahb", v.reshape(g // 8, 8, a, b), eye)
    return out.reshape(g // 8, 8 * a, 8 * b)


def _block_diag_extract(blk, a, b):
    n = blk.shape[0]
    v = blk.reshape(n, 8, a, 8, b)
    return jnp.einsum("kgahb,gh->kgab", v, jnp.eye(8, dtype=blk.dtype)).reshape(n * 8, a, b)


def _ssm_prepare(l, prm):
    g, n, p = SSM_GROUPS, SSM_STATE, SSM_GROUP
    a_re, a_im = prm["ssm_a_re"][l], prm["ssm_a_im"][l]
    log_dt = prm["ssm_log_dt"][l].reshape(g, 1)
    abar_re, abar_im, f_re, f_im = _zoh_fwd(l, a_re, a_im, log_dt)
    f_re, f_im = f_re.reshape(g, 1, n), f_im.reshape(g, 1, n)
    bbar_re, bbar_im = _bbar_fwd(l, f_re, f_im, prm["ssm_b_re"], prm["ssm_b_im"])
    pw = _powers(l, abar_re.reshape(1, g * n), abar_im.reshape(1, g * n))
    return dict(a_re=a_re, a_im=a_im, log_dt=log_dt, f_re=f_re, f_im=f_im,
                bblk_re=_block_diag(bbar_re), bblk_im=_block_diag(bbar_im),
                cblk_re=_block_diag(prm["ssm_c_re"][l].astype(MXU_DTYPE)),
                cblk_im=_block_diag(prm["ssm_c_im"][l].astype(MXU_DTYPE)), pw=pw,
                dskip=prm["ssm_d"][l].reshape(1, g * p))


def _ssm_param_grads(l, sd, r, prm):
    g, n, p = SSM_GROUPS, SSM_STATE, SSM_GROUP
    dbbar_re = _block_diag_extract(r["dbblk_re"], p, n)
    dbbar_im = _block_diag_extract(r["dbblk_im"], p, n)
    dfr, dfi, db_re, db_im = _bbar_bwd(l, sd["f_re"], sd["f_im"], prm["ssm_b_re"], prm["ssm_b_im"], dbbar_re, dbbar_im)
    cts = (r["dabar_re"].reshape(g, n), r["dabar_im"].reshape(g, n), dfr.reshape(g, n), dfi.reshape(g, n))
    da_re, da_im, dlog_dt = _zoh_bwd(l, sd["a_re"], sd["a_im"], sd["log_dt"], cts)
    return dict(ssm_a_re=da_re, ssm_a_im=da_im, ssm_log_dt=dlog_dt.reshape(g), ssm_b_re=db_re, ssm_b_im=db_im,
                ssm_c_re=_block_diag_extract(r["dcblk_re"], p, n), ssm_c_im=_block_diag_extract(r["dcblk_im"], p, n),
                ssm_d=r["dd"].reshape(g, p))


def _ffn_weight_grads(l, fb, dx2, s, place):
    d = dx2.shape[1]
    hcn = fb["act"].shape[1]
    g = {}
    for name, key, rhs in (("ffn_w_gate", "dgate", fb["h2"]), ("ffn_w_up", "dup", fb["h2"]),
                           ("ffn_w_down", "act", fb["dx2"])):
        g[name] = _tn_matmul(f"d{name}_l{l}", fb[key], BS((None, hcn, s), lambda j, t, pr: (j, 0, 0)), rhs,
                             BS((s, d), lambda j, t, pr: (0, 0)), (hcn, d), (N_CHIPS, 1), place)
    return g


def _in_weight_grad(l, ht, dz, place):
    d, s = ht.shape
    ncw = dz.shape[1] // N_CHIPS
    return _tn_matmul(f"dw_in_l{l}", ht, BS((d, s), lambda j, t, pr: (0, 0)), dz, BS((s, ncw), lambda j, t, pr: (0, j)),
                      (d, ncw), (N_CHIPS, 1), place)


def _fused_tn(name, pairs, kinds, s, place):
    n = len(pairs)

    def shape_of(a, b, kind):
        k, m = a.shape[1], b.shape[1]
        if kind == "rows":
            return (N_CHIPS, k // N_CHIPS, m)
        if kind == "cols":
            return (N_CHIPS, k, m // N_CHIPS)
        return (k // 128, 128, 128)

    shapes = [shape_of(a, b, kind) for (a, b), kind in zip(pairs, kinds)]
    out_shape = []
    for shp, kind in zip(shapes, kinds):
        out_shape += [SDS(shp, F32)] if kind == "groups" else [SDS(shp[1:], F32), SDS(shp, WIRE_DTYPE)]

    def body(place_ref, *refs):
        ins, outs, accs = refs[:2 * n], refs[2 * n:2 * n + len(out_shape)], refs[2 * n + len(out_shape):]
        o = 0
        for i, kind in enumerate(kinds):
            a, b = ins[2 * i][...], ins[2 * i + 1][...]
            if kind == "groups":
                for k in range(shapes[i][0]):
                    outs[o][k] = _mm_tn(a[:, k * 128:(k + 1) * 128], b[:, k * 128:(k + 1) * 128])
                o += 1
                continue
            acc = accs[i]
            if kind == "rows":
                acc[...] = _mm_tn(a, b).reshape(acc.shape)
            else:
                full = _mm_tn(a, b)
                nc = acc.shape[2]
                for j in range(N_CHIPS):
                    acc[j] = full[:, j * nc:(j + 1) * nc]
            outs[o][...] = acc[place_ref[0]]
            outs[o + 1][...] = acc[...].astype(WIRE_DTYPE)
            o += 2

    whole = lambda shp: BS(shp, lambda t, pr: (0,) * len(shp))
    outs = pl.pallas_call(
        body, name=name,
        grid_spec=pltpu.PrefetchScalarGridSpec(
            num_scalar_prefetch=1, grid=(1,),
            in_specs=[whole(v.shape) for pair in pairs for v in pair],
            out_specs=[whole(o.shape) for o in out_shape],
            scratch_shapes=[pltpu.VMEM(shp, F32) for shp in shapes]),
        out_shape=out_shape, compiler_params=_params(),
    )(place, *[v for pair in pairs for v in pair])
    res, o = [], 0
    for kind in kinds:
        if kind == "groups":
            res.append(outs[o])
            o += 1
        else:
            res.append((outs[o], outs[o + 1]))
            o += 2
    return res


def _mixer_weight_grads(l, sv, mb, dx1, s, place):
    g = {}
    (g["w_out"], g["ssm_w_glu"]) = _fused_tn(f"dw_out_glu_l{l}", [(mb["merged"], dx1), (mb["ge"], mb["dt"])],
                                            ("rows", "rows"), s, place)
    (g["ssm_w_proj"], g["conv_w_proj"], g["pool_w_proj"]) = _fused_tn(
        f"dw_proj_l{l}", [(mb["sa"], mb["dya"]), (mb["ac"], mb["dyb"]), (mb["pp"], mb["dyc"])],
        ("cols", "cols", "cols"), s, place)
    (dwgrp,) = _fused_tn(f"dpool_w_group_l{l}", [(sv["p"], mb["dq"])], ("groups",), s, place)
    return g, dwgrp


def _local_step(x, target, weights_of, prm, place, on_grads=None):
    s, d = x.shape
    cw = prm["ssm_b_glu"].shape[1]
    sp = {k: prm[k].reshape(N_LAYERS, 1, -1) for k in ("norm1", "norm2", "b_gate", "ssm_b_glu", "conv_ln_g", "conv_ln_b",
                                                        "pool_scale", "conv_b_dw")}
    sp["pool_w_group"] = prm["pool_w_group"]
    saved = []
    xin = x
    for l in range(N_LAYERS):
        fw = weights_of(l, "in", (xin,))
        sd = _ssm_prepare(l, prm)
        z, h = _in_proj(l, xin, sp["norm1"], fw["w_in"])
        hre, him, y = _ssm_fwd(l, z, sd["bblk_re"], sd["bblk_im"], sd["cblk_re"], sd["cblk_im"], sd["pw"], sd["dskip"])
        p = _pool_fwd(l, z, cw)
        fw.update(weights_of(l, "mixer", (y, p)))
        wdw = fw["conv_w_dw"]
        hc = _conv_fwd(l, z, wdw, sp["conv_b_dw"][l])
        x1 = _merge_fwd(l, xin, y, hc, p, z, fw, sp)
        fw.update(weights_of(l, "ffn", (x1,)))
        x2, gate, up, h2 = _ffn_fwd(l, x1, sp["norm2"], fw["ffn_w_gate"], fw["ffn_w_up"], fw["ffn_w_down"])
        saved.append(dict(x=xin, z=z, h=h, hre=hre, him=him, y=y, hc=hc, p=p, x1=x1, sd=sd, wdw=wdw, fw=fw,
                          gate=gate, up=up, h2=h2))
        xin = x2
    dx, loss, dfinal = _loss_head(xin, target, prm["final_norm"].reshape(1, d))
    big = [None] * N_LAYERS
    small = [None] * N_LAYERS
    norm2_rows = sp["norm2"]
    started = (lambda l, group, grads: on_grads(l, group, grads)) if on_grads is not None else (lambda *a: 0.0)
    for l in reversed(range(N_LAYERS)):
        sv = saved[l]
        sd, fw = sv["sd"], sv["fw"]
        fb = _ffn_bwd(l, sv["x1"], dx, sv["gate"], sv["up"], norm2_rows, fw["ffn_w_gate"], fw["ffn_w_up"],
                      fw["ffn_w_down"])
        fb["h2"] = sv["h2"]
        big[l] = _ffn_weight_grads(l, fb, dx, s, place)
        spl = dict(sp, ssm_b_glu=sp["ssm_b_glu"] + started(l, "ffn", big[l]))
        mb = _merge_bwd(l, fb["dx1"], sv["y"], sv["hc"], sv["p"], sv["z"], fw, spl)
        mixer, dwgrp = _mixer_weight_grads(l, sv, mb, fb["dx1"], s, place)
        big[l].update(mixer)
        wdw = sv["wdw"] + started(l, "mixer", mixer)
        du_c = _pool_bwd(l, mb["dp"])
        dv1, dv2, dwdw, dbdw = _conv_bwd(l, mb["dhc"], sv["z"], wdw)
        sr = _ssm_bwd(l, mb["dy"], sv["z"], sv["hre"], sv["him"], sd["bblk_re"], sd["bblk_im"], sd["cblk_re"],
                      sd["cblk_im"], sd["pw"], sd["dskip"])
        dx, dz, dnorm1 = _in_proj_bwd(l, fb["dx1"], sv["x"], sp["norm1"], fw["w_in"], sr["du"], dv1, dv2, du_c, mb["dzg"])
        w_in_grad = {"w_in": _in_weight_grad(l, sv["h"], dz, place)}
        big[l].update(w_in_grad)
        sg = _ssm_param_grads(l, sd, sr, prm)
        sg.update(norm1=dnorm1.reshape(d), b_gate=mb["db_gate"].reshape(3 * d), ssm_b_glu=mb["db_glu"].reshape(cw),
                  conv_b_dw=dbdw.reshape(cw), conv_ln_g=mb["dln_g"].reshape(cw), conv_ln_b=mb["dln_b"].reshape(cw),
                  pool_w_group=dwgrp, pool_scale=mb["dscale"].reshape(cw), norm2=fb["dnorm2"].reshape(d),
                  conv_w_dw=dwdw)
        small[l] = sg
        if l == N_LAYERS - 1:
            sg = dict(sg, final_norm=dfinal.reshape(d))
        norm2_rows = sp["norm2"] + (started(l, "in", w_in_grad) + started(l, "small", sg))
    return loss[0, 0], dx, big, small, dfinal.reshape(d)


def _place():
    return lax.axis_index("x"), lax.axis_index("y"), lax.axis_index("c")


def _other_chips(x, y):
    return [(1 - x, y), (x, 1 - y), (1 - x, 1 - y)]


def _remote(src, dst, send_sem, recv_sem, device):
    return pltpu.make_async_remote_copy(src_ref=src, dst_ref=dst, send_sem=send_sem, recv_sem=recv_sem,
                                        device_id=device, device_id_type=MESH)


def _hbm(v):
    return pltpu.with_memory_space_constraint(v, pltpu.HBM)


def _cast_into(name, w, place, dtype, after=()):
    nl, k, n = w.shape
    tr = _row_tile(k, n)
    nt = k // tr

    def body(place_ref, w_ref, *rest):
        o0_ref, o1_ref = rest[len(after):]

        @pl.when(pl.program_id(0) == 0)
        def _():
            o0_ref[...] = w_ref[...].astype(dtype)

        @pl.when(pl.program_id(0) == 1)
        def _():
            o1_ref[...] = w_ref[...].astype(dtype)

    return pl.pallas_call(
        body, name=f"cast_{name}",
        grid_spec=pltpu.PrefetchScalarGridSpec(
            num_scalar_prefetch=1, grid=(nl, nt),
            in_specs=[BS((None, tr, n), lambda l, t, pr: (l, t, 0))] + [ANY] * len(after),
            out_specs=[BS((None, tr, n), lambda l, t, pr: (pr[0], t * (1 - l) + (nt - 1) * l, 0)),
                       BS((None, tr, n), lambda l, t, pr: (pr[0], t * l, 0))]),
        out_shape=[SDS((N_CHIPS, k, n), dtype)] * 2)(place, w, *after)


def _cast_small_into(tag, ws, dtypes, place, after=()):
    n = len(ws)

    def body(place_ref, *refs):
        ins, outs = refs[:n], refs[n + len(after):]
        for i in range(n):
            @pl.when(pl.program_id(0) == 0)
            def _():
                outs[2 * i][...] = ins[i][...].astype(dtypes[i])

            @pl.when(pl.program_id(0) == 1)
            def _():
                outs[2 * i + 1][...] = ins[i][...].astype(dtypes[i])

    slot = lambda w: BS((None,) + w.shape[1:], lambda l, pr: (pr[0], 0, 0))
    outs = pl.pallas_call(
        body, name=f"cast_{tag}",
        grid_spec=pltpu.PrefetchScalarGridSpec(
            num_scalar_prefetch=1, grid=(N_LAYERS,),
            in_specs=[BS((None,) + w.shape[1:], lambda l, pr: (l, 0, 0)) for w in ws] + [ANY] * len(after),
            out_specs=[slot(w) for w in ws for _ in range(N_LAYERS)]),
        out_shape=[SDS((N_CHIPS,) + w.shape[1:], dt) for w, dt in zip(ws, dtypes) for _ in range(N_LAYERS)],
    )(place, *ws, *after)
    return [tuple(outs[N_LAYERS * i:N_LAYERS * (i + 1)]) for i in range(n)]


def _gather_rows(buf, c):
    k = buf.shape[1]
    if k % 2:
        return pl.ds(0, k)
    return pl.ds(pl.multiple_of(c * (k // 2), 8), k // 2)


def _allgather_start(tag, groups):
    ng = len(groups)
    sizes = [len(g) for g in groups]
    first = [sum(sizes[:g]) for g in range(ng)]
    flat = [b for g in groups for b in g]
    nb = len(flat)

    def body(*refs):
        ins = refs[:nb]
        sems = refs[nb:nb + 2 * ng]
        token = refs[-1]
        x, y, c = _place()
        jme = 2 * x + y
        for g in range(ng):
            for a in range(sizes[g]):
                buf = ins[first[g] + a]
                blk = buf.at[jme, _gather_rows(buf, c)]
                for k, (cx, cy) in enumerate(_other_chips(x, y)):
                    _remote(blk, blk, sems[2 * g].at[3 * a + k], sems[2 * g + 1].at[3 * a + k], (cx, cy, c)).start()
        token[...] = jnp.zeros(token.shape, F32)

    sem_shapes = [pltpu.SemaphoreType.DMA((3 * sizes[g // 2],)) for g in range(2 * ng)]
    outs = pl.pallas_call(
        body, name=f"allgather_start_{tag}", in_specs=[HBM] * nb,
        out_specs=[SEM] * (2 * ng) + [HBM] * nb + [pl.BlockSpec(memory_space=pltpu.VMEM)],
        out_shape=sem_shapes + [pltpu.HBM(b.shape, b.dtype) for b in flat] + [SDS((8, 128), F32)],
        input_output_aliases={i: 2 * ng + i for i in range(nb)},
        compiler_params=pltpu.CompilerParams(has_side_effects=SIDE_EFFECT))(*[_hbm(b) for b in flat])
    per_group = [(outs[2 * g], outs[2 * g + 1], outs[2 * ng + first[g]:2 * ng + first[g] + sizes[g]])
                 for g in range(ng)]
    return per_group, outs[-1]


def _allgather_wait(l, send_sems, recv_sems, bufs, after):
    n = len(bufs)

    def body(*refs):
        ins = refs[:n]
        ssem, rsem = refs[n], refs[n + 1]
        x, y, c = _place()
        jme = 2 * x + y
        for a in range(n):
            rows = _gather_rows(ins[a], c)
            for k, (cx, cy) in enumerate(_other_chips(x, y)):
                cp = _remote(ins[a].at[jme, rows], ins[a].at[2 * cx + cy, rows], ssem.at[3 * a + k],
                             rsem.at[3 * a + k], (cx, cy, c))
                cp.wait_send()
                cp.wait_recv()

    return pl.pallas_call(
        body, name=f"allgather_wait_{l}", in_specs=[HBM] * n + [SEM, SEM] + [ANY] * len(after), out_specs=[HBM] * n,
        out_shape=[pltpu.HBM(b.shape, b.dtype) for b in bufs], input_output_aliases={i: i for i in range(n)},
        compiler_params=pltpu.CompilerParams(has_side_effects=SIDE_EFFECT))(*bufs, send_sems, recv_sems, *after)


def _allgather_forward(l, bufs):
    n = len(bufs)
    split = [a for a in range(n) if bufs[a].shape[1] % 2 == 0]

    def body(*refs):
        ins = refs[:n]
        send_sems, recv_sems = refs[2 * n:]
        x, y, c = _place()
        sibling = (x, y, 1 - c)
        copies = []
        for a in split:
            for k, (cx, cy) in enumerate(_other_chips(x, y)):
                blk = ins[a].at[2 * cx + cy, _gather_rows(ins[a], c)]
                cp = _remote(blk, blk, send_sems.at[a, k], recv_sems.at[a, k], sibling)
                cp.start()
                copies.append(cp)
        for a in split:
            for k, (cx, cy) in enumerate(_other_chips(x, y)):
                blk = ins[a].at[2 * cx + cy, _gather_rows(ins[a], 1 - c)]
                _remote(blk, blk, send_sems.at[a, k], recv_sems.at[a, k], sibling).wait_recv()
        for cp in copies:
            cp.wait_send()

    sem = pltpu.SemaphoreType.DMA((n, 3))
    return pl.pallas_call(
        body, name=f"allgather_forward_{l}", in_specs=[ANY] * n, out_specs=[ANY] * n,
        out_shape=[SDS(b.shape, b.dtype) for b in bufs], input_output_aliases={i: i for i in range(n)},
        scratch_shapes=[sem, sem])(*bufs)


def _rs_to_owner(l, parts):
    n = len(parts)
    lands = [lax.empty((3,) + p.shape[1:], p.dtype) for p in parts]

    def body(*refs):
        ins, zones = refs[:n], refs[n:2 * n]
        send_sems, recv_sems = refs[2 * n], refs[2 * n + 1]
        token = refs[-1]
        x, y, c = _place()
        for a in range(n):
            for k, (cx, cy) in enumerate(_other_chips(x, y)):
                _remote(ins[a].at[2 * cx + cy], zones[a].at[k], send_sems.at[3 * a + k], recv_sems.at[3 * a + k],
                        (cx, cy, c)).start()
        token[...] = jnp.zeros(token.shape, F32)

    sem = pltpu.SemaphoreType.DMA((3 * n,))
    outs = pl.pallas_call(
        body, name=f"rs_to_owner_start_{l}", in_specs=[HBM] * (2 * n),
        out_specs=[SEM, SEM] + [HBM] * (2 * n) + [pl.BlockSpec(memory_space=pltpu.VMEM)],
        out_shape=[sem, sem] + [pltpu.HBM(p.shape, p.dtype) for p in parts]
        + [pltpu.HBM(z.shape, z.dtype) for z in lands] + [SDS((8, 128), F32)],
        input_output_aliases={i: 2 + i for i in range(2 * n)},
        compiler_params=pltpu.CompilerParams(has_side_effects=SIDE_EFFECT),
    )(*[_hbm(p) for p in parts], *[_hbm(z) for z in lands])
    return outs[0], outs[1], outs[2:2 + n], outs[2 + n:2 + 2 * n], outs[-1]


def _rs_to_owner_wait(l, send_sems, recv_sems, parts, lands, after):
    n = len(parts)

    def body(*refs):
        ins, zones = refs[:n], refs[n:2 * n]
        ssem, rsem = refs[2 * n], refs[2 * n + 1]
        x, y, c = _place()
        for a in range(n):
            for k, (cx, cy) in enumerate(_other_chips(x, y)):
                cp = _remote(ins[a].at[2 * cx + cy], zones[a].at[k], ssem.at[3 * a + k], rsem.at[3 * a + k],
                             (cx, cy, c))
                cp.wait_send()
                cp.wait_recv()

    outs = pl.pallas_call(
        body, name=f"rs_to_owner_wait_{l}", in_specs=[HBM] * (2 * n) + [SEM, SEM] + [ANY] * len(after),
        out_specs=[HBM] * (2 * n),
        out_shape=[pltpu.HBM(p.shape, p.dtype) for p in parts] + [pltpu.HBM(z.shape, z.dtype) for z in lands],
        input_output_aliases={i: i for i in range(2 * n)},
        compiler_params=pltpu.CompilerParams(has_side_effects=SIDE_EFFECT),
    )(*parts, *lands, send_sems, recv_sems, *after)
    return outs[:n], outs[n:]


def _rs_sibling_exchange(l, both):
    n = len(both)

    def body(*refs):
        ins = refs[:n]
        send_sems, recv_sems = refs[2 * n:]
        x, y, c = _place()
        copies = []
        for a in range(n):
            cp = _remote(ins[a].at[c], ins[a].at[c], send_sems.at[a], recv_sems.at[a], (x, y, 1 - c))
            cp.start()
            copies.append(cp)
        for a, cp in enumerate(copies):
            cp.wait_send()
            _remote(ins[a].at[1 - c], ins[a].at[1 - c], send_sems.at[a], recv_sems.at[a], (x, y, 1 - c)).wait_recv()

    sem = pltpu.SemaphoreType.DMA((n,))
    return pl.pallas_call(
        body, name=f"rs_sibling_exchange_{l}", in_specs=[ANY] * n, out_specs=[ANY] * n,
        out_shape=[SDS(b.shape, b.dtype) for b in both], input_output_aliases={i: i for i in range(n)},
        scratch_shapes=[sem, sem])(*both)


def _add_owner(name, grad, recv, place):
    r, cols = grad.shape
    tr = _row_tile(r, cols, budget=1024 * 1024)
    nt = r // tr

    def body(place_ref, g_ref, r_ref, o_ref):
        acc = ((g_ref[...] + r_ref[0].astype(F32)) + r_ref[1].astype(F32)) + r_ref[2].astype(F32)
        o_ref[...] = acc.astype(o_ref.dtype)

    return pl.pallas_call(
        body, name=name,
        grid_spec=pltpu.PrefetchScalarGridSpec(
            num_scalar_prefetch=1, grid=(nt,),
            in_specs=[BS((tr, cols), lambda t, pr: (t, 0)), BS((3, tr, cols), lambda t, pr: (0, t, 0))],
            out_specs=BS((None, tr, cols), lambda t, pr: (pr[1], t, 0))),
        out_shape=SDS((2, r, cols), WIRE_DTYPE))(place, grad, recv)


def _add_owner_small(tag, grads, recvs, place):
    n = len(grads)

    def body(place_ref, *refs):
        gs, rs, outs = refs[:n], refs[n:2 * n], refs[2 * n:]
        for g_ref, r_ref, o_ref in zip(gs, rs, outs):
            acc = ((g_ref[...] + r_ref[0].astype(F32)) + r_ref[1].astype(F32)) + r_ref[2].astype(F32)
            o_ref[...] = acc.astype(o_ref.dtype)

    return pl.pallas_call(
        body, name=f"rs_add_owner_{tag}",
        grid_spec=pltpu.PrefetchScalarGridSpec(
            num_scalar_prefetch=1, grid=(1,),
            in_specs=[BS(g.shape, lambda t, pr: (0, 0)) for g in grads]
            + [BS(r.shape, lambda t, pr: (0, 0, 0)) for r in recvs],
            out_specs=[BS((None,) + g.shape, lambda t, pr: (pr[1], 0, 0)) for g in grads]),
        out_shape=[SDS((2,) + g.shape, WIRE_DTYPE) for g in grads])(place, *grads, *recvs)


def _reduce_start(tag, grads):
    names = list(grads)
    send_sems, recv_sems, wires, lands, token = _rs_to_owner(tag, [grads[n][1] for n in names])
    return dict(tag=tag, names=names, send_sems=send_sems, recv_sems=recv_sems, wires=wires, lands=lands,
                grads=[grads[n][0] for n in names]), token


def _reduce_finish(tag, groups, place, after):
    all_names, all_mine = [], []
    for pending in groups:
        sub, names = pending["tag"], pending["names"]
        _, lands = _rs_to_owner_wait(sub, pending["send_sems"], pending["recv_sems"], pending["wires"],
                                     pending["lands"], after)
        if max(g.size for g in pending["grads"]) <= SMALL_GRAD_ELEMS:
            mine = _add_owner_small(sub, pending["grads"], lands, place)
        else:
            mine = [_add_owner(f"rs_add_owner_{n}_{sub}", g, r, place)
                    for n, g, r in zip(names, pending["grads"], lands)]
        all_names += names
        all_mine += mine
    return dict(zip(all_names, _rs_sibling_exchange(tag, all_mine)))


def _small_peers(x, y, c):
    return [(x, y, 1 - c)] + [(cx, cy, c) for cx, cy in _other_chips(x, y)]


def _allgather_rows_start(tag, bufs):
    n = len(bufs)
    lands = [lax.empty((8,) + b.shape, b.dtype) for b in bufs]

    def body(*refs):
        ins, zones = refs[:n], refs[n:2 * n]
        send_sems, recv_sems = refs[2 * n], refs[2 * n + 1]
        token = refs[-1]
        x, y, c = _place()
        for a in range(n):
            for i, peer in enumerate(_small_peers(x, y, c)):
                _remote(ins[a], zones[a].at[4 * x + 2 * y + c], send_sems.at[4 * a + i], recv_sems.at[4 * a + i],
                        peer).start()
        token[...] = jnp.zeros(token.shape, F32)

    sem = pltpu.SemaphoreType.DMA((4 * n,))
    outs = pl.pallas_call(
        body, name=f"allgather_small_start_{tag}", in_specs=[HBM] * (2 * n),
        out_specs=[SEM, SEM] + [HBM] * (2 * n) + [pl.BlockSpec(memory_space=pltpu.VMEM)],
        out_shape=[sem, sem] + [pltpu.HBM(b.shape, b.dtype) for b in bufs]
        + [pltpu.HBM(z.shape, z.dtype) for z in lands] + [SDS((8, 128), F32)],
        input_output_aliases={i: 2 + i for i in range(2 * n)},
        compiler_params=pltpu.CompilerParams(has_side_effects=SIDE_EFFECT),
    )(*[_hbm(b) for b in bufs], *[_hbm(z) for z in lands])
    return outs[0], outs[1], outs[2:2 + n], outs[2 + n:2 + 2 * n], outs[-1]


def _allgather_rows_wait(tag, send_sems, recv_sems, bufs, lands, after):
    n = len(bufs)

    def body(*refs):
        ins, zones = refs[:n], refs[n:2 * n]
        ssem, rsem = refs[2 * n], refs[2 * n + 1]
        x, y, c = _place()
        for a in range(n):
            for i, (px, py, pc) in enumerate(_small_peers(x, y, c)):
                cp = _remote(ins[a], zones[a].at[4 * px + 2 * py + pc], ssem.at[4 * a + i], rsem.at[4 * a + i],
                             (px, py, pc))
                cp.wait_send()
                cp.wait_recv()

    outs = pl.pallas_call(
        body, name=f"allgather_small_wait_{tag}", in_specs=[HBM] * (2 * n) + [SEM, SEM, ANY],
        out_specs=[HBM] * (2 * n),
        out_shape=[pltpu.HBM(b.shape, b.dtype) for b in bufs] + [pltpu.HBM(z.shape, z.dtype) for z in lands],
        input_output_aliases={i: i for i in range(2 * n)},
        compiler_params=pltpu.CompilerParams(has_side_effects=SIDE_EFFECT),
    )(*bufs, *lands, send_sems, recv_sems, after)
    return outs[:n], outs[n:]


def _allgather_rows_forward(tag, lands):
    n = len(lands)

    def body(*refs):
        ins = refs[:n]
        send_sems, recv_sems = refs[2 * n:]
        x, y, c = _place()
        sibling = (x, y, 1 - c)
        copies = []
        for a in range(n):
            for k, (cx, cy) in enumerate(_other_chips(x, y)):
                blk = ins[a].at[4 * cx + 2 * cy + c]
                cp = _remote(blk, blk, send_sems.at[a, k], recv_sems.at[a, k], sibling)
                cp.start()
                copies.append(cp)
        for a in range(n):
            for k, (cx, cy) in enumerate(_other_chips(x, y)):
                blk = ins[a].at[4 * cx + 2 * cy + 1 - c]
                _remote(blk, blk, send_sems.at[a, k], recv_sems.at[a, k], sibling).wait_recv()
        for cp in copies:
            cp.wait_send()

    sem = pltpu.SemaphoreType.DMA((n, 3))
    return pl.pallas_call(body, name=f"allgather_small_forward_{tag}", in_specs=[ANY] * n, out_specs=[ANY] * n,
                          out_shape=[SDS(z.shape, z.dtype) for z in lands],
                          input_output_aliases={i: i for i in range(n)}, scratch_shapes=[sem, sem])(*lands)


def _sum_devices(tag, gathered, mine, place):
    _, r, cols = gathered.shape
    tr = _row_tile(r, cols, budget=256 * 1024)

    def body(place_ref, g_ref, x_ref, o_ref):
        me = 2 * place_ref[0] + place_ref[1]
        acc = jnp.where(me == 0, x_ref[...], g_ref[0])
        for k in range(1, 8):
            acc = acc + jnp.where(me == k, x_ref[...], g_ref[k])
        o_ref[...] = acc

    return pl.pallas_call(
        body, name=f"sum_small_grads_{tag}",
        grid_spec=pltpu.PrefetchScalarGridSpec(
            num_scalar_prefetch=1, grid=(r // tr,),
            in_specs=[BS((8, tr, cols), lambda t, pr: (0, t, 0)), BS((tr, cols), lambda t, pr: (t, 0))],
            out_specs=BS((tr, cols), lambda t, pr: (t, 0))),
        out_shape=SDS((r, cols), F32))(place, gathered, mine)


def _adamw_values(w, g, m, v):
    m = ADAM_B1 * m + (1.0 - ADAM_B1) * g
    v = ADAM_B2 * v + (1.0 - ADAM_B2) * (g * g)
    m_hat = m / (1.0 - ADAM_B1 ** ADAM_STEP)
    v_hat = v / (1.0 - ADAM_B2 ** ADAM_STEP)
    delta = -ADAM_LR * (m_hat / (jnp.sqrt(v_hat) + ADAM_EPS) + ADAM_WD * w)
    return delta, m, v


def _adamw_big(name, l, w, m, v, g, earlier=None, after=()):
    nl, r, cols = w.shape
    tr = _row_tile(r, cols, budget=1024 * 1024)
    nt = r // tr
    n_prev = 0 if earlier is None else 4

    def body(*refs):
        w_ref, m_ref, v_ref, g_ref = refs[:4]
        go_ref, d_ref, mo_ref, vo_ref = refs[4 + n_prev + len(after):]
        gv = g_ref[0].astype(F32) + g_ref[1].astype(F32)
        delta, m_new, v_new = _adamw_values(w_ref[...], gv, m_ref[...], v_ref[...])
        go_ref[...] = gv
        d_ref[...] = delta
        mo_ref[...] = m_new
        vo_ref[...] = v_new

    layer = BS((None, tr, cols), lambda t: (l, t, 0))
    return pl.pallas_call(
        body, name=f"adamw_{name}_l{l}", grid=(nt,),
        in_specs=[layer, layer, layer, BS((2, tr, cols), lambda t: (0, t, 0))] + [ANY] * (n_prev + len(after)),
        out_specs=[layer] * 4, out_shape=[SDS(w.shape, F32)] * 4,
        input_output_aliases={4 + i: i for i in range(n_prev)}, compiler_params=_params(),
    )(w, m, v, g, *(earlier or ()), *after)


def _adamw_small_group(tag, l, ws, ms, vs, gs, earlier, after=()):
    n = len(ws)
    steps = ADAMW_GROUP_STEPS
    prev = [a for e in earlier if e is not None for a in e]
    n_prev = len(prev)
    assert n_prev in (0, 4 * n)

    def body(*refs):
        w_refs, m_refs, v_refs, g_refs = refs[:n], refs[n:2 * n], refs[2 * n:3 * n], refs[3 * n:4 * n]
        outs = refs[4 * n + n_prev + len(after):]
        for i in range(n):
            gv = g_refs[i][0].astype(F32) + g_refs[i][1].astype(F32)
            delta, m_new, v_new = _adamw_values(w_refs[i][...], gv, m_refs[i][...], v_refs[i][...])
            for ref, val in zip(outs[4 * i:4 * i + 4], (gv, delta, m_new, v_new)):
                ref[...] = val

    def layer(w):
        return BS((None, w.shape[1] // steps, w.shape[2]), lambda t: (l, t, 0))

    return pl.pallas_call(
        body, name=f"adamw_{tag}_l{l}", grid=(steps,),
        in_specs=[layer(w) for w in ws] * 3
        + [BS((2, w.shape[1] // steps, w.shape[2]), lambda t: (0, t, 0)) for w in ws] + [ANY] * (n_prev + len(after)),
        out_specs=[layer(w) for w in ws for _ in range(4)],
        out_shape=[SDS(w.shape, F32) for w in ws for _ in range(4)],
        input_output_aliases={4 * n + i: i for i in range(n_prev)}, compiler_params=_params(),
    )(*ws, *ms, *vs, *gs, *prev, *after)


def _adamw_mid(ws, ms, vs, gathered, mine, place):
    n = len(ws)
    shape = ws[0].shape[1:]
    zeros = (0,) * len(shape)

    def body(place_ref, *refs):
        w_refs, m_refs, v_refs = refs[:n], refs[n:2 * n], refs[2 * n:3 * n]
        gath, own = refs[3 * n:(3 + N_LAYERS) * n], refs[(3 + N_LAYERS) * n:(3 + 2 * N_LAYERS) * n]
        outs = refs[(3 + 2 * N_LAYERS) * n:]
        me = 2 * place_ref[0] + place_ref[1]
        for i in range(n):
            gv = None
            for l in range(N_LAYERS):
                g_ref, x_ref = gath[l * n + i], own[l * n + i]
                acc = jnp.where(me == 0, x_ref[...], g_ref[0])
                for k in range(1, 8):
                    acc = acc + jnp.where(me == k, x_ref[...], g_ref[k])
                gv = acc if gv is None else jnp.where(pl.program_id(0) == l, acc, gv)
            delta, m_new, v_new = _adamw_values(w_refs[i][...], gv, m_refs[i][...], v_refs[i][...])
            for ref, val in zip(outs[4 * i:4 * i + 4], (gv, delta, m_new, v_new)):
                ref[...] = val

    layer = BS((None,) + shape, lambda l, pr: (l,) + zeros)
    kept = pl.Buffered(1)
    outs = pl.pallas_call(
        body, name="adamw_replicated_matrices",
        grid_spec=pltpu.PrefetchScalarGridSpec(
            num_scalar_prefetch=1, grid=(N_LAYERS,),
            in_specs=[layer] * (3 * n)
            + [BS((8,) + shape, lambda l, pr: (0,) + zeros, pipeline_mode=kept)] * (N_LAYERS * n)
            + [BS(shape, lambda l, pr: zeros, pipeline_mode=kept)] * (N_LAYERS * n),
            out_specs=[layer] * (4 * n)),
        out_shape=[SDS(ws[0].shape, F32)] * (4 * n), compiler_params=_params(),
    )(place, *ws, *ms, *vs, *[g for l in range(N_LAYERS) for g in gathered[l]],
      *[x for l in range(N_LAYERS) for x in mine[l]])
    return [tuple(outs[4 * i:4 * i + 4]) for i in range(n)]


def _adamw_rows(w, m, v, g):
    r, cols = w.shape
    tr = _row_tile(r, cols, budget=512 * 1024)

    def body(w_ref, m_ref, v_ref, g_ref, d_ref, mo_ref, vo_ref):
        delta, m_new, v_new = _adamw_values(w_ref[...], g_ref[...], m_ref[...], v_ref[...])
        d_ref[...] = delta
        mo_ref[...] = m_new
        vo_ref[...] = v_new

    spec = BS((tr, cols), lambda t: (t, 0))
    return pl.pallas_call(body, name="adamw_small", grid=(r // tr,), in_specs=[spec] * 4, out_specs=[spec] * 3,
                          out_shape=[SDS(w.shape, F32)] * 3)(w, m, v, g)


SMALL_GRAD_ELEMS = 256 * 1024
ADAMW_GROUP_STEPS = 4
PACK_ALIGN = 8 * 128
PACK_ROWS = 128


def _pack_rows(arrays):
    parts, rows = [], 0
    for a in arrays:
        flat = a.reshape(-1)
        pad = (-flat.shape[0]) % PACK_ALIGN
        if pad:
            flat = jnp.pad(flat, (0, pad))
        parts.append(flat.reshape(-1, 128))
        rows += parts[-1].shape[0]
    if rows % PACK_ROWS:
        parts.append(jnp.zeros((PACK_ROWS - rows % PACK_ROWS, 128), parts[0].dtype))
    return jnp.concatenate(parts, axis=0)


def _unpack_rows(buf, shapes):
    out, row = [], 0
    for shape in shapes:
        size = math.prod(shape)
        rows = -(-size // PACK_ALIGN) * (PACK_ALIGN // 128)
        out.append(buf[row:row + rows].reshape(-1)[:size].reshape(shape))
        row += rows
    return out


def kernel(x, norm1, w_in, b_gate, ssm_a_re, ssm_a_im, ssm_log_dt, ssm_b_re, ssm_b_im, ssm_c_re, ssm_c_im, ssm_d, ssm_w_glu, ssm_b_glu, ssm_w_proj, conv_w_dw, conv_b_dw, conv_ln_g, conv_ln_b, conv_w_proj, pool_w_group, pool_scale, pool_w_proj, w_out, norm2, ffn_w_gate, ffn_w_up, ffn_w_down, final_norm, loss_target, m_norm1, m_w_in, m_b_gate, m_ssm_a_re, m_ssm_a_im, m_ssm_log_dt, m_ssm_b_re, m_ssm_b_im, m_ssm_c_re, m_ssm_c_im, m_ssm_d, m_ssm_w_glu, m_ssm_b_glu, m_ssm_w_proj, m_conv_w_dw, m_conv_b_dw, m_conv_ln_g, m_conv_ln_b, m_conv_w_proj, m_pool_w_group, m_pool_scale, m_pool_w_proj, m_w_out, m_norm2, m_ffn_w_gate, m_ffn_w_up, m_ffn_w_down, m_final_norm, v_norm1, v_w_in, v_b_gate, v_ssm_a_re, v_ssm_a_im, v_ssm_log_dt, v_ssm_b_re, v_ssm_b_im, v_ssm_c_re, v_ssm_c_im, v_ssm_d, v_ssm_w_glu, v_ssm_b_glu, v_ssm_w_proj, v_conv_w_dw, v_conv_b_dw, v_conv_ln_g, v_conv_ln_b, v_conv_w_proj, v_pool_w_group, v_pool_scale, v_pool_w_proj, v_w_out, v_norm2, v_ffn_w_gate, v_ffn_w_up, v_ffn_w_down, v_final_norm):
    given = dict(locals())
    cx, cy, cc = _place()
    place = jnp.stack([2 * cx + cy, cc]).astype(jnp.int32)

    def kernel_view(n, a):
        if n in TRANSPOSED:
            return a.transpose(0, 2, 1)
        return a.transpose(0, 1, 3, 2) if n in ("ssm_b_re", "ssm_b_im") else a

    prm = {n: given[n] for n in WEIGHTS}
    mom = {n: given["m_" + n] for n in WEIGHTS}
    var = {n: given["v_" + n] for n in WEIGHTS}
    for n in MID:
        prm[n], mom[n], var[n] = kernel_view(n, prm[n]), kernel_view(n, mom[n]), kernel_view(n, var[n])

    dw_shard = prm["conv_w_dw"].reshape(N_LAYERS, CONV_KERNEL, -1)
    casts = {"w_in": _cast_into("w_in", prm["w_in"], place, MXU_DTYPE)}
    first, first_started = _allgather_start("first", [[casts["w_in"][0]]])
    in_flight = {(0, "in"): first[0]}
    mixer = GATHER_GROUPS["mixer"]
    casts.update(zip(mixer, _cast_small_into(
        "mixer", [dw_shard if n == "conv_w_dw" else prm[n] for n in mixer],
        [F32 if n == "conv_w_dw" else MXU_DTYPE for n in mixer], place, after=(first_started,))))
    casts.update({n: _cast_into(n, kernel_view(n, prm[n]), place, MXU_DTYPE, after=(first_started,))
                  for n in GATHER_GROUPS["ffn"]})
    order = [(l, g) for l in range(N_LAYERS) for g in GATHER_GROUPS if (l, g) != (0, "in")]
    rest, rest_started = _allgather_start("rest", [[casts[n][l] for n in GATHER_GROUPS[g]] for l, g in order])
    in_flight.update(zip(order, rest))

    arrived = {}

    def weights_of(l, group, after):
        if (l, group) in arrived:
            return arrived.pop((l, group))
        tag = f"l{l}_{group}"
        if (l, group) == (0, "in"):
            after = after + (rest_started,)
        groups = (group, "mixer") if (l > 0 and group == "in") else (group,)
        waited = [_allgather_wait(f"l{l}_{g}", *in_flight[l, g][:2], in_flight[l, g][2], after) for g in groups]
        bufs = _allgather_forward(tag, [b for w in waited for b in w])
        for g in groups:
            fw = dict(zip(GATHER_GROUPS[g], bufs[:len(GATHER_GROUPS[g])]))
            bufs = bufs[len(GATHER_GROUPS[g]):]
            if "conv_w_dw" in fw:
                fw["conv_w_dw"] = fw["conv_w_dw"].transpose(1, 0, 2).reshape(CONV_KERNEL, -1)
            arrived[l, g] = fw
        return arrived.pop((l, group))

    pending, small_pending, small_shapes = {}, {}, {}
    tokens = {}

    def on_grads(l, group, grads):
        if group == "small":
            packed = {n: g for n, g in grads.items() if n not in MID}
            small_shapes[l] = {n: g.shape for n, g in packed.items()}
            begun = _allgather_rows_start(f"l{l}", [_pack_rows(list(packed.values()))] + [grads[n] for n in MID])
            small_pending[l], token = begun[:4], begun[4]
        else:
            pending[l, group], token = _reduce_start(f"{l}_{group}", grads)
        tokens[l, group] = token
        return token[0, 0]

    loss, dx, _, _, _ = _local_step(x[0], loss_target[0], weights_of, prm, place, on_grads)
    loss = lax.psum(loss, ("x", "y", "c"))

    reduced = [{} for _ in range(N_LAYERS)]
    out = {}

    def finish(l, groups, after):
        reduced[l].update(_reduce_finish(f"l{l}_{groups[0]}", [pending[l, g] for g in groups], place, after))

    def adamw(l, names, done):
        small = [n for n in names if prm[n][0].size <= SMALL_GRAD_ELEMS]
        for n in names:
            if n not in small:
                out[n] = _adamw_big(n, l, kernel_view(n, prm[n]), kernel_view(n, mom[n]), kernel_view(n, var[n]),
                                    reduced[l][n], out.get(n), after=done)
                done = (out[n][0],)
        if small:
            res = _adamw_small_group("mixer", l, [prm[n] for n in small], [mom[n] for n in small],
                                     [var[n] for n in small], [reduced[l][n] for n in small],
                                     [out.get(n) for n in small], after=done)
            for i, n in enumerate(small):
                out[n] = tuple(res[4 * i:4 * i + 4])
            done = (res[0],)
        return done

    top = N_LAYERS - 1
    done = (tokens[0, "in"], tokens[0, "small"])
    finish(top, ("ffn", "mixer", "in"), done)
    done = adamw(top, BIG, done)
    for groups in (("ffn", "mixer"), ("in",)):
        finish(0, groups, done)
        done = adamw(0, [n for g in groups for n in GATHER_GROUPS[g] if n in BIG], done)
    for n in BIG:
        out[n] = tuple(kernel_view(n, a) for a in out[n])

    gsmall = {}
    mid_mine, mid_gathered = [], []
    for l in range(N_LAYERS):
        mine, lands = _allgather_rows_wait(f"l{l}", *small_pending[l], done[0])
        lands = _allgather_rows_forward(f"l{l}", lands)
        mid_mine.append(mine[1:])
        mid_gathered.append(lands[1:])
        gsum = _sum_devices(f"l{l}", lands[0], mine[0], place)
        for n, g in zip(small_shapes[l], _unpack_rows(gsum, list(small_shapes[l].values()))):
            gsmall.setdefault(n, [None] * N_LAYERS)[l] = g
    mid_out = _adamw_mid([prm[n] for n in MID], [mom[n] for n in MID], [var[n] for n in MID], mid_gathered, mid_mine,
                         place)
    for n, res in zip(MID, mid_out):
        out[n] = tuple(kernel_view(n, a) for a in res)
    gsmall = {n: (g[top] if n == "final_norm" else jnp.stack(g)) for n, g in gsmall.items()}
    lanes = dw_shard.shape[-1]
    gsmall["conv_w_dw"] = lax.dynamic_slice_in_dim(gsmall["conv_w_dw"], (2 * cx + cy) * lanes, lanes, axis=2)
    small_names = [n for n in SMALL if n not in MID] + ["conv_w_dw"]
    w_rows = _pack_rows([prm[n] for n in small_names])
    m_rows = _pack_rows([mom[n] for n in small_names])
    v_rows = _pack_rows([var[n] for n in small_names])
    g_rows = _pack_rows([gsmall[n] for n in small_names])
    shapes = [prm[n].shape for n in small_names]
    d_s, m_s, v_s = (_unpack_rows(r, shapes) for r in _adamw_rows(w_rows, m_rows, v_rows, g_rows))
    for i, n in enumerate(small_names):
        out[n] = (gsmall[n].reshape(prm[n].shape), d_s[i], m_s[i], v_s[i])
    grads = [out[n][0] for n in WEIGHTS]
    deltas = [out[n][1] for n in WEIGHTS]
    new_m = [out[n][2] for n in WEIGHTS]
    new_v = [out[n][3] for n in WEIGHTS]
    return (loss, dx[None], *grads, *deltas, *new_m, *new_v)
```

```python
import math

import jax
import jax.numpy as jnp
from jax import lax
from jax.experimental import pallas as pl
from jax.experimental.pallas import tpu as pltpu

F32 = jnp.float32
MXU_DTYPE = jnp.bfloat16
WIRE_DTYPE = jnp.bfloat16
SDS = jax.ShapeDtypeStruct
BS = pl.BlockSpec
ANY = pl.BlockSpec(memory_space=pl.ANY)
HBM = pl.BlockSpec(memory_space=pltpu.HBM)
SEM = pl.BlockSpec(memory_space=pltpu.SEMAPHORE)
SIDE_EFFECT = pltpu.SideEffectType.DATAFLOW_SIDE_EFFECTING
MESH = pl.DeviceIdType.MESH

EPS = 1e-6
N_CHIPS = 4
N_LAYERS = 2
SSM_GROUPS, SSM_STATE, SSM_GROUP = 32, 64, 16
CONV_KERNEL = 31
CONV_PAD = 32
POOL_WINDOWS = (2, 4, 8, 16)
GELU_C = math.sqrt(2.0 / math.pi)
ADAM_LR, ADAM_B1, ADAM_B2, ADAM_EPS, ADAM_WD, ADAM_STEP = 0.001, 0.9, 0.999, 1e-08, 0.01, 10
VMEM_LIMIT = 56 * 1024 * 1024

BIG = ("w_in", "ssm_w_glu", "ssm_w_proj", "conv_w_proj", "pool_w_proj", "w_out", "ffn_w_gate", "ffn_w_up", "ffn_w_down")
TRANSPOSED = ("ffn_w_gate", "ffn_w_up")
MID = ("ssm_b_re", "ssm_b_im", "ssm_c_re", "ssm_c_im")
GATHER_GROUPS = {
    "in": ("w_in",),
    "mixer": ("ssm_w_glu", "ssm_w_proj", "conv_w_proj", "pool_w_proj", "w_out", "conv_w_dw"),
    "ffn": ("ffn_w_gate", "ffn_w_up", "ffn_w_down"),
}
SMALL = ("norm1", "b_gate", "ssm_a_re", "ssm_a_im", "ssm_log_dt", "ssm_b_re", "ssm_b_im", "ssm_c_re", "ssm_c_im",
         "ssm_d", "ssm_b_glu", "conv_b_dw", "conv_ln_g", "conv_ln_b", "pool_w_group", "pool_scale", "norm2",
         "final_norm")
WEIGHTS = ("norm1", "w_in", "b_gate", "ssm_a_re", "ssm_a_im", "ssm_log_dt", "ssm_b_re", "ssm_b_im", "ssm_c_re",
           "ssm_c_im", "ssm_d", "ssm_w_glu", "ssm_b_glu", "ssm_w_proj", "conv_w_dw", "conv_b_dw", "conv_ln_g",
           "conv_ln_b", "conv_w_proj", "pool_w_group", "pool_scale", "pool_w_proj", "w_out", "norm2", "ffn_w_gate",
           "ffn_w_up", "ffn_w_down", "final_norm")


def _params():
    return pltpu.CompilerParams(vmem_limit_bytes=VMEM_LIMIT)


def _mm(a, b):
    return jnp.dot(a.astype(MXU_DTYPE), b.astype(MXU_DTYPE), preferred_element_type=F32)


def _mm_nt(a, b):
    return lax.dot_general(a.astype(MXU_DTYPE), b.astype(MXU_DTYPE), (((1,), (1,)), ((), ())),
                           preferred_element_type=F32)


def _mm_tn(a, b):
    return lax.dot_general(a.astype(MXU_DTYPE), b.astype(MXU_DTYPE), (((0,), (0,)), ((), ())),
                           preferred_element_type=F32)


def _sigmoid(x):
    return jax.nn.sigmoid(x)


def _gelu(x):
    t = jnp.tanh(GELU_C * (x + 0.044715 * (x * x * x)))
    return x * (0.5 * (1.0 + t)), t


def _gelu_grad(x, t):
    return 0.5 * (1.0 + t) + 0.5 * x * (1.0 - t * t) * (GELU_C * (1.0 + 3.0 * 0.044715 * x * x))


def _colsum(v):
    return jnp.sum(v, axis=0, keepdims=True)


def _row_tile(rows, cols, itemsize=4, budget=1536 * 1024):
    best = None
    for t in range(8, rows + 1, 8):
        if rows % t == 0 and t * cols * itemsize <= budget:
            best = t
    return best if best is not None else rows


def _in_proj(l, x, norm1, w_in):
    s, d = x.shape
    nc = w_in.shape[-1]
    tm = min(1024, s)
    nt = s // tm

    def body(x_ref, g_ref, w_ref, z_ref, h_ref, h_all):
        i = pl.program_id(1)
        rows = pl.ds(pl.multiple_of(i * tm, tm), tm)

        @pl.when(pl.program_id(0) == 0)
        def _():
            xv = x_ref[...]
            r = lax.rsqrt(jnp.mean(xv * xv, axis=-1, keepdims=True) + EPS)
            hv = (xv * r * g_ref[...]).astype(h_ref.dtype)
            h_ref[...] = hv.T
            h_all[rows, :] = hv

        z_ref[...] = _mm(h_all[rows, :], w_ref[...])

    tile_of = lambda j, i: i * (1 - jnp.minimum(j, 1)) + (nt - 1) * jnp.minimum(j, 1)
    return pl.pallas_call(
        body, name=f"in_proj_l{l}", grid=(N_CHIPS, nt),
        in_specs=[BS((tm, d), lambda j, i: (tile_of(j, i), 0)), BS((None, 1, d), lambda j, i: (l, 0, 0)),
                  BS((None, d, nc), lambda j, i: (j, 0, 0))],
        out_specs=[BS((tm, nc), lambda j, i: (i, j)), BS((d, tm), lambda j, i: (0, tile_of(j, i)))],
        out_shape=[SDS((s, N_CHIPS * nc), F32), SDS((d, s), MXU_DTYPE)],
        scratch_shapes=[pltpu.VMEM((s, d), MXU_DTYPE)], compiler_params=_params())(x, norm1, w_in)


def _mm_cols(a, w_ref):
    return jnp.concatenate([_mm(a, w_ref[j]) for j in range(N_CHIPS)], axis=1)


def _mm_nt_cols(dv, w_ref):
    nc = w_ref.shape[-1]
    acc = _mm_nt(dv[:, 0:nc], w_ref[0])
    for j in range(1, N_CHIPS):
        acc = acc + _mm_nt(dv[:, j * nc:(j + 1) * nc], w_ref[j])
    return acc


def _merge_values(y, hc, p, zg, wglu, bglu, wpa, wpb, wpc, lng, lnb, wgrp, scale, bg):
    v = {}
    ge, th = _gelu(y)
    t = _mm(ge, wglu) + bglu
    sg = _sigmoid(t)
    sa = ge * sg
    ya = _mm_cols(sa, wpa)
    mu = jnp.mean(hc, axis=-1, keepdims=True)
    xc = hc - mu
    r = lax.rsqrt(jnp.mean(xc * xc, axis=-1, keepdims=True) + EPS)
    xh = xc * r
    ln = xh * lng + lnb
    sl = _sigmoid(ln)
    ac = ln * sl
    yb = _mm_cols(ac, wpb)
    gw = p.shape[1] // len(POOL_WINDOWS)
    q = jnp.concatenate([_mm(p[:, k * gw:(k + 1) * gw], wgrp[k]) for k in range(len(POOL_WINDOWS))], axis=1)
    pp = q * scale
    yc = _mm_cols(pp, wpc)
    d = ya.shape[1]
    gates = [_sigmoid(zg[k] + bg[:, k * d:(k + 1) * d]) for k in range(3)]
    merged = gates[0] * ya + gates[1] * yb + gates[2] * yc
    v.update(ge=ge, th=th, sg=sg, sa=sa, ya=ya, r=r, xh=xh, ln=ln, sl=sl, ac=ac, yb=yb, q=q, pp=pp, yc=yc,
             gates=gates, merged=merged)
    return v


def _merge_specs(l, tm, d, cw):
    row = lambda n: BS((None, 1, n), lambda i: (l, 0, 0))
    resident = lambda shp: BS(shp, lambda i: (0, 0, 0), pipeline_mode=pl.Buffered(1))
    return [
        BS((tm, cw), lambda i: (i, 0)),
        BS((tm, cw), lambda i: (i, 0)),
        BS((tm, cw), lambda i: (i, 0)),
        BS((tm, d), lambda i: (i, 2)), BS((tm, d), lambda i: (i, 3)), BS((tm, d), lambda i: (i, 4)),
        resident((N_CHIPS, cw // N_CHIPS, cw)),
        row(cw),
        resident((N_CHIPS, cw, d // N_CHIPS)),
        resident((N_CHIPS, cw, d // N_CHIPS)),
        resident((N_CHIPS, cw, d // N_CHIPS)),
        row(cw), row(cw),
        BS((None, 4, cw // 4, cw // 4), lambda i: (l, 0, 0, 0)),
        row(cw),
        row(3 * d),
        resident((N_CHIPS, d // N_CHIPS, d)),
    ]


def _merge_fwd(l, x, y, hc, p, z, fw, sp):
    s, d = x.shape
    cw = y.shape[1]
    tm = min(512, s)

    def body(x_ref, y_ref, hc_ref, p_ref, z0, z1, z2, wglu, bglu, wpa, wpb, wpc, lng, lnb, wgrp, scale, bg, wout,
             x1_ref):
        v = _merge_values(y_ref[...], hc_ref[...], p_ref[...], (z0[...], z1[...], z2[...]),
                          wglu[...].reshape(cw, cw), bglu[...], wpa, wpb, wpc, lng[...], lnb[...], wgrp, scale[...],
                          bg[...])
        x1_ref[...] = x_ref[...] + _mm(v["merged"], wout[...].reshape(d, d))

    return pl.pallas_call(
        body, name=f"merge_fwd_l{l}", grid=(s // tm,),
        in_specs=[BS((tm, d), lambda i: (i, 0))] + _merge_specs(l, tm, d, cw),
        out_specs=BS((tm, d), lambda i: (i, 0)), out_shape=SDS((s, d), F32), compiler_params=_params(),
    )(x, y, hc, p, z, z, z, fw["ssm_w_glu"], sp["ssm_b_glu"], fw["ssm_w_proj"], fw["conv_w_proj"], fw["pool_w_proj"],
      sp["conv_ln_g"], sp["conv_ln_b"], sp["pool_w_group"], sp["pool_scale"], sp["b_gate"], fw["w_out"])


def _merge_bwd(l, dx1, y, hc, p, z, fw, sp):
    s, d = dx1.shape
    cw = y.shape[1]
    tm = min(256, s)
    m = MXU_DTYPE

    def body(dx1_ref, y_ref, hc_ref, p_ref, z0, z1, z2, wglu, bglu, wpa, wpb, wpc, lng, lnb, wgrp, scale, bg, wout,
             dzg_ref, dy_ref, dhc_ref, dp_ref, merged_ref, sa_ref, ac_ref, pp_ref, ge_ref, dt_ref, dya_ref, dyb_ref,
             dyc_ref, dq_ref, dbg_ref, dbglu_ref, dlng_ref, dlnb_ref, dscale_ref):
        yv = y_ref[...]
        wg = wglu[...].reshape(cw, cw)
        v = _merge_values(yv, hc_ref[...], p_ref[...], (z0[...], z1[...], z2[...]), wg, bglu[...], wpa, wpb, wpc,
                          lng[...], lnb[...], wgrp, scale[...], bg[...])
        dm = _mm_nt(dx1_ref[...], wout[...].reshape(d, d))
        ys = (v["ya"], v["yb"], v["yc"])
        dys, dbg = [], []
        for k in range(3):
            gk = v["gates"][k]
            dzk = dm * ys[k] * (gk * (1.0 - gk))
            dbg.append(_colsum(dzk))
            dzg_ref[:, k * d:(k + 1) * d] = dzk.astype(m)
            dys.append((dm * gk).astype(m))
        dsa = _mm_nt_cols(dys[0], wpa)
        dac = _mm_nt_cols(dys[1], wpb)
        dpp = _mm_nt_cols(dys[2], wpc)
        ge, sg = v["ge"], v["sg"]
        dt = dsa * ge * (sg * (1.0 - sg))
        dge = dsa * sg + _mm_nt(dt, wg)
        dy_ref[...] = dge * _gelu_grad(yv, v["th"])
        ln, sl, xh = v["ln"], v["sl"], v["xh"]
        dln = dac * (sl * (1.0 + ln * (1.0 - sl)))
        dxh = dln * lng[...]
        dhc_ref[...] = v["r"] * (dxh - jnp.mean(dxh, axis=-1, keepdims=True)
                                 - xh * jnp.mean(dxh * xh, axis=-1, keepdims=True))
        dq = dpp * scale[...]
        gw = cw // len(POOL_WINDOWS)
        for k in range(len(POOL_WINDOWS)):
            dp_ref[:, k * gw:(k + 1) * gw] = _mm_nt(dq[:, k * gw:(k + 1) * gw], wgrp[k])
        merged_ref[...] = v["merged"].astype(m)
        sa_ref[...] = v["sa"].astype(m)
        ac_ref[...] = v["ac"].astype(m)
        pp_ref[...] = v["pp"].astype(m)
        ge_ref[...] = ge.astype(m)
        dt_ref[...] = dt.astype(m)
        dya_ref[...] = dys[0]
        dyb_ref[...] = dys[1]
        dyc_ref[...] = dys[2]
        dq_ref[...] = dq.astype(m)

        @pl.when(pl.program_id(0) == 0)
        def _():
            for ref in (dbg_ref, dbglu_ref, dlng_ref, dlnb_ref, dscale_ref):
                ref[...] = jnp.zeros(ref.shape, F32)

        dbg_ref[...] += jnp.concatenate(dbg, axis=1)
        dbglu_ref[...] += _colsum(dt)
        dlng_ref[...] += _colsum(dln * xh)
        dlnb_ref[...] += _colsum(dln)
        dscale_ref[...] += _colsum(dpp * v["q"])

    tile = lambda n: BS((tm, n), lambda i: (i, 0))
    acc = lambda n: BS((1, n), lambda i: (0, 0))
    outs = pl.pallas_call(
        body, name=f"merge_bwd_l{l}", grid=(s // tm,),
        in_specs=[tile(d)] + _merge_specs(l, tm, d, cw),
        out_specs=[tile(3 * d), tile(cw), tile(cw), tile(cw), tile(d), tile(cw), tile(cw), tile(cw), tile(cw), tile(cw),
                   tile(d), tile(d), tile(d), tile(cw), acc(3 * d), acc(cw), acc(cw), acc(cw), acc(cw)],
        out_shape=[SDS((s, 3 * d), m), SDS((s, cw), F32), SDS((s, cw), F32), SDS((s, cw), F32), SDS((s, d), m),
                   SDS((s, cw), m), SDS((s, cw), m), SDS((s, cw), m), SDS((s, cw), m), SDS((s, cw), m), SDS((s, d), m),
                   SDS((s, d), m), SDS((s, d), m), SDS((s, cw), m), SDS((1, 3 * d), F32), SDS((1, cw), F32),
                   SDS((1, cw), F32), SDS((1, cw), F32), SDS((1, cw), F32)],
        compiler_params=_params(),
    )(dx1, y, hc, p, z, z, z, fw["ssm_w_glu"], sp["ssm_b_glu"], fw["ssm_w_proj"], fw["conv_w_proj"], fw["pool_w_proj"],
      sp["conv_ln_g"], sp["conv_ln_b"], sp["pool_w_group"], sp["pool_scale"], sp["b_gate"], fw["w_out"])
    names = ("dzg", "dy", "dhc", "dp", "merged", "sa", "ac", "pp", "ge", "dt", "dya", "dyb", "dyc", "dq", "db_gate",
             "db_glu", "dln_g", "dln_b", "dscale")
    return dict(zip(names, outs))


def _ffn_fwd(l, x1, norm2, wg, wu, wd):
    s, d = x1.shape
    hc = wd.shape[1]
    tm = min(1024, s)

    def body(x_ref, g_ref, wg_ref, wu_ref, wd_ref, o_ref, gate_ref, up_ref, h_ref):
        @pl.when(pl.program_id(1) == 0)
        def _():
            xv = x_ref[...]
            r = lax.rsqrt(jnp.mean(xv * xv, axis=-1, keepdims=True) + EPS)
            h_ref[...] = (xv * r * g_ref[...]).astype(h_ref.dtype)
            o_ref[...] = xv

        h = h_ref[...]
        gate = _mm_nt(h, wg_ref[...])
        up = _mm_nt(h, wu_ref[...])
        gate_ref[...] = gate
        up_ref[...] = up
        o_ref[...] += _mm(gate * _sigmoid(gate) * up, wd_ref[...])

    chunk = BS((None, tm, hc), lambda i, j: (j, i, 0))
    return pl.pallas_call(
        body, name=f"ffn_fwd_l{l}", grid=(s // tm, N_CHIPS),
        in_specs=[BS((tm, d), lambda i, j: (i, 0)), BS((None, 1, d), lambda i, j: (l, 0, 0)),
                  BS((None, hc, d), lambda i, j: (j, 0, 0)), BS((None, hc, d), lambda i, j: (j, 0, 0)),
                  BS((None, hc, d), lambda i, j: (j, 0, 0))],
        out_specs=[BS((tm, d), lambda i, j: (i, 0)), chunk, chunk, BS((tm, d), lambda i, j: (i, 0))],
        out_shape=[SDS((s, d), F32), SDS((N_CHIPS, s, hc), F32), SDS((N_CHIPS, s, hc), F32), SDS((s, d), MXU_DTYPE)],
        compiler_params=_params())(x1, norm2, wg, wu, wd)


def _ffn_bwd(l, x1, dx2, gate_pre, up_pre, norm2, wg, wu, wd):
    s, d = x1.shape
    hc = wd.shape[1]
    tm = min(512, s)
    m = MXU_DTYPE
    last = N_CHIPS - 1

    def body(x_ref, dx2_ref, gate_ref, up_ref, g_ref, wg_ref, wu_ref, wd_ref, dx1_ref, dxb_ref, act_ref, dgate_ref,
             dup_ref, dn_ref, dh_scr):
        i, j = pl.program_id(0), pl.program_id(1)

        @pl.when(j == 0)
        def _():
            dxb_ref[...] = dx2_ref[...].astype(m)
            dh_scr[...] = jnp.zeros(dh_scr.shape, F32)

        @pl.when((i == 0) & (j == 0))
        def _():
            dn_ref[...] = jnp.zeros(dn_ref.shape, F32)

        gate = gate_ref[...]
        up = up_ref[...]
        sg = _sigmoid(gate)
        silu = gate * sg
        act_ref[...] = (silu * up).astype(m).T
        dact = _mm_nt(dxb_ref[...], wd_ref[...])
        dup = (dact * silu).astype(m)
        dgate = (dact * up * (sg * (1.0 + gate * (1.0 - sg)))).astype(m)
        dup_ref[...] = dup.T
        dgate_ref[...] = dgate.T
        dh_scr[...] += _mm(dgate, wg_ref[...]) + _mm(dup, wu_ref[...])

        @pl.when(j == last)
        def _():
            xv = x_ref[...]
            r = lax.rsqrt(jnp.mean(xv * xv, axis=-1, keepdims=True) + EPS)
            xh = xv * r
            dh = dh_scr[...]
            dn_ref[...] += _colsum(dh * xh)
            dxh = dh * g_ref[...]
            dx1_ref[...] = dx2_ref[...] + r * (dxh - xh * jnp.mean(dxh * xh, axis=-1, keepdims=True))

    chunk = BS((None, hc, tm), lambda i, j: (j, 0, i))
    saved = BS((None, tm, hc), lambda i, j: (j, i, 0))
    outs = pl.pallas_call(
        body, name=f"ffn_bwd_l{l}", grid=(s // tm, N_CHIPS),
        in_specs=[BS((tm, d), lambda i, j: (i, 0)), BS((tm, d), lambda i, j: (i, 0)), saved, saved,
                  BS((None, 1, d), lambda i, j: (l, 0, 0)),
                  BS((None, hc, d), lambda i, j: (j, 0, 0)), BS((None, hc, d), lambda i, j: (j, 0, 0)),
                  BS((None, hc, d), lambda i, j: (j, 0, 0))],
        out_specs=[BS((tm, d), lambda i, j: (i, 0)), BS((tm, d), lambda i, j: (i, 0)),
                   chunk, chunk, chunk, BS((1, d), lambda i, j: (0, 0))],
        out_shape=[SDS((s, d), F32), SDS((s, d), m), SDS((N_CHIPS, hc, s), m),
                   SDS((N_CHIPS, hc, s), m), SDS((N_CHIPS, hc, s), m), SDS((1, d), F32)],
        scratch_shapes=[pltpu.VMEM((tm, d), F32)], compiler_params=_params(),
    )(x1, dx2, gate_pre, up_pre, norm2, wg, wu, wd)
    return dict(zip(("dx1", "dx2", "act", "dgate", "dup", "dnorm2"), outs))


def _loss_head(x, target, gf):
    s, d = x.shape
    tm = min(512, s)

    def body(x_ref, t_ref, g_ref, dx_ref, loss_ref, dg_ref):
        @pl.when(pl.program_id(0) == 0)
        def _():
            loss_ref[...] = jnp.zeros(loss_ref.shape, F32)
            dg_ref[...] = jnp.zeros(dg_ref.shape, F32)

        xv = x_ref[...]
        r = lax.rsqrt(jnp.mean(xv * xv, axis=-1, keepdims=True) + EPS)
        xh = xv * r
        err = xh * g_ref[...] - t_ref[...]
        loss_ref[...] += 0.5 * jnp.sum(jnp.mean(err * err, axis=-1, keepdims=True), axis=0, keepdims=True)
        dyv = err * (1.0 / d)
        dg_ref[...] += _colsum(dyv * xh)
        dxh = dyv * g_ref[...]
        dx_ref[...] = r * (dxh - xh * jnp.mean(dxh * xh, axis=-1, keepdims=True))

    return pl.pallas_call(
        body, name="loss_head", grid=(s // tm,),
        in_specs=[BS((tm, d), lambda i: (i, 0)), BS((tm, d), lambda i: (i, 0)), BS((1, d), lambda i: (0, 0))],
        out_specs=[BS((tm, d), lambda i: (i, 0)), BS((1, 1), lambda i: (0, 0)), BS((1, d), lambda i: (0, 0))],
        out_shape=[SDS((s, d), F32), SDS((1, 1), F32), SDS((1, d), F32)], compiler_params=_params())(x, target, gf)


def _in_proj_bwd(l, dres, x, norm1, w_in, du_a, dv1, dv2, du_c, dzg):
    s, d = x.shape
    nc = w_in.shape[-1]
    tm = min(256, s)
    m = MXU_DTYPE

    def body(dres_ref, x_ref, g_ref, w_ref, a_ref, b1_ref, b2_ref, c_ref, g3_ref, dx_ref, dz_ref, dn_ref):
        @pl.when(pl.program_id(0) == 0)
        def _():
            dn_ref[...] = jnp.zeros(dn_ref.shape, F32)

        dz = jnp.concatenate([a_ref[...], b1_ref[...], b2_ref[...], c_ref[...], g3_ref[...]], axis=1).astype(m)
        dz_ref[...] = dz
        dh = _mm_nt_cols(dz, w_ref)
        xv = x_ref[...]
        r = lax.rsqrt(jnp.mean(xv * xv, axis=-1, keepdims=True) + EPS)
        xh = xv * r
        dn_ref[...] += _colsum(dh * xh)
        dxh = dh * g_ref[...]
        dx_ref[...] = dres_ref[...] + r * (dxh - xh * jnp.mean(dxh * xh, axis=-1, keepdims=True))

    tile = lambda n: BS((tm, n), lambda i: (i, 0))
    return pl.pallas_call(
        body, name=f"in_proj_bwd_l{l}", grid=(s // tm,),
        in_specs=[tile(d), tile(d), BS((None, 1, d), lambda i: (l, 0, 0)),
                  BS((N_CHIPS, d, nc), lambda i: (0, 0, 0), pipeline_mode=pl.Buffered(1)),
                  tile(du_a.shape[1]), tile(dv1.shape[1]), tile(dv2.shape[1]), tile(du_c.shape[1]), tile(dzg.shape[1])],
        out_specs=[tile(d), tile(N_CHIPS * nc), BS((1, d), lambda i: (0, 0))],
        out_shape=[SDS((s, d), F32), SDS((s, N_CHIPS * nc), m), SDS((1, d), F32)], compiler_params=_params(),
    )(dres, x, norm1, w_in, du_a, dv1, dv2, du_c, dzg)


def _tn_matmul(name, a, a_spec, b, b_spec, chunk_shape, grid, place):
    last = grid[1] - 1

    def body(place_ref, a_ref, b_ref, own_ref, wire_ref, *acc):
        part = _mm(a_ref[...], b_ref[...])

        def emit(total):
            wire_ref[...] = total.astype(WIRE_DTYPE)

            @pl.when(pl.program_id(0) == place_ref[0])
            def _():
                own_ref[...] = total

        if last == 0:
            emit(part)
        else:
            @pl.when(pl.program_id(1) == 0)
            def _():
                acc[0][...] = part

            @pl.when(pl.program_id(1) > 0)
            def _():
                acc[0][...] += part

            @pl.when(pl.program_id(1) == last)
            def _():
                emit(acc[0][...])

    zeros = (0,) * len(chunk_shape)
    return pl.pallas_call(
        body, name=name,
        grid_spec=pltpu.PrefetchScalarGridSpec(
            num_scalar_prefetch=1, grid=grid, in_specs=[a_spec, b_spec],
            out_specs=[BS(chunk_shape, lambda j, t, pr: zeros), BS((None,) + chunk_shape, lambda j, t, pr: (j,) + zeros)],
            scratch_shapes=[pltpu.VMEM(chunk_shape, F32)] if last else []),
        out_shape=[SDS(chunk_shape, F32), SDS((N_CHIPS,) + chunk_shape, WIRE_DTYPE)],
        compiler_params=_params())(place, a, b)


def _scan_consts(pw_ref, lanes, reverse):
    sgn = -1.0 if reverse else 1.0
    row = lax.broadcasted_iota(jnp.int32, (8, lanes), 0)
    steps = []
    for i, k in enumerate((1, 2, 4)):
        mask = (row < 8 - k) if reverse else (row >= k)
        steps.append((k, jnp.where(mask, pw_ref[2 * i], 0.0), jnp.where(mask, sgn * pw_ref[2 * i + 1], 0.0)))
    c = 4 if reverse else 3
    return steps, pw_ref[2 * c], sgn * pw_ref[2 * c + 1]


def _scan_block(br, bi, steps, reverse):
    for k, ar, ai in steps:
        sh = 8 - k if reverse else k
        sr = pltpu.roll(br, sh, 0)
        si = pltpu.roll(bi, sh, 0)
        br, bi = br + ar * sr - ai * si, bi + ar * si + ai * sr
    return br, bi


def _ssm_fwd(l, z, bblk_re, bblk_im, cblk_re, cblk_im, pw, dskip):
    s = z.shape[0]
    gc = bblk_re.shape[1]
    gl = bblk_re.shape[2]
    nblk = bblk_re.shape[0]

    def body(u_ref, bre, bim, cre, cim, pw_ref, d_ref, hre, him, y_ref):
        u = u_ref[...]
        hre[...] = _mm(u, bre[...])
        him[...] = _mm(u, bim[...])
        steps, car, cai = _scan_consts(pw_ref, gl, False)

        def step(i, carry):
            cr, ci = carry
            r0 = pl.multiple_of(i * 8, 8)
            br, bi = _scan_block(hre[pl.ds(r0, 8), :], him[pl.ds(r0, 8), :], steps, False)
            hr = br + car * cr - cai * ci
            hi = bi + car * ci + cai * cr
            hre[pl.ds(r0, 8), :] = hr
            him[pl.ds(r0, 8), :] = hi
            return jnp.broadcast_to(hr[7:8, :], (8, gl)), jnp.broadcast_to(hi[7:8, :], (8, gl))

        zero = jnp.zeros((8, gl), F32)
        lax.fori_loop(0, s // 8, step, (zero, zero))
        y_ref[...] = _mm_nt(hre[...], cre[...]) - _mm_nt(him[...], cim[...]) + d_ref[...] * u

    return pl.pallas_call(
        body, name=f"ssm_fwd_l{l}", grid=(nblk,),
        in_specs=[BS((s, gc), lambda k: (0, k)), BS((None, gc, gl), lambda k: (k, 0, 0)),
                  BS((None, gc, gl), lambda k: (k, 0, 0)), BS((None, gc, gl), lambda k: (k, 0, 0)),
                  BS((None, gc, gl), lambda k: (k, 0, 0)), BS((10, 8, gl), lambda k: (0, 0, k)),
                  BS((1, gc), lambda k: (0, k))],
        out_specs=[BS((s, gl), lambda k: (0, k)), BS((s, gl), lambda k: (0, k)), BS((s, gc), lambda k: (0, k))],
        out_shape=[SDS((s, nblk * gl), F32), SDS((s, nblk * gl), F32), SDS((s, nblk * gc), F32)],
        compiler_params=_params())(z, bblk_re, bblk_im, cblk_re, cblk_im, pw, dskip)


def _ssm_bwd(l, dy, z, hre, him, bblk_re, bblk_im, cblk_re, cblk_im, pw, dskip):
    s = z.shape[0]
    nblk, gc, gl = bblk_re.shape

    def body(dy_ref, u_ref, hre_ref, him_ref, bre, bim, cre, cim, pw_ref, d_ref,
             du_ref, dbre_ref, dbim_ref, dcre_ref, dcim_ref, dar_ref, dai_ref, dd_ref, gre, gim):
        dyv = dy_ref[...]
        u = u_ref[...]
        gre[...] = _mm(dyv, cre[...])
        gim[...] = -_mm(dyv, cim[...])
        dcre_ref[...] = _mm_tn(dyv, hre_ref[...])
        dcim_ref[...] = -_mm_tn(dyv, him_ref[...])
        dd_ref[...] = _colsum(dyv * u)
        row = lax.broadcasted_iota(jnp.int32, (8, gl), 0)
        steps, car, cai = _scan_consts(pw_ref, gl, True)
        n8 = s // 8

        def step(ii, carry):
            cr, ci, accr, acci = carry
            i = n8 - 1 - ii
            r0 = pl.multiple_of(i * 8, 8)
            br, bi = _scan_block(gre[pl.ds(r0, 8), :], gim[pl.ds(r0, 8), :], steps, True)
            dr = br + car * cr - cai * ci
            di = bi + car * ci + cai * cr
            gre[pl.ds(r0, 8), :] = dr
            gim[pl.ds(r0, 8), :] = di
            rp = pl.multiple_of(jnp.maximum(i - 1, 0) * 8, 8)
            keep = jnp.where(i > 0, 1.0, 0.0)
            pr = jnp.where(row >= 1, pltpu.roll(hre_ref[pl.ds(r0, 8), :], 1, 0),
                           keep * pltpu.roll(hre_ref[pl.ds(rp, 8), :], 1, 0))
            pi = jnp.where(row >= 1, pltpu.roll(him_ref[pl.ds(r0, 8), :], 1, 0),
                           keep * pltpu.roll(him_ref[pl.ds(rp, 8), :], 1, 0))
            accr = accr + dr * pr + di * pi
            acci = acci + di * pr - dr * pi
            return (jnp.broadcast_to(dr[0:1, :], (8, gl)), jnp.broadcast_to(di[0:1, :], (8, gl)), accr, acci)

        zero = jnp.zeros((8, gl), F32)
        _, _, accr, acci = lax.fori_loop(0, n8, step, (zero, zero, zero, zero))
        dar_ref[...] = _colsum(accr)
        dai_ref[...] = _colsum(acci)
        dbr = gre[...]
        dbi = gim[...]
        du_ref[...] = (dyv * d_ref[...] + _mm_nt(dbr, bre[...]) + _mm_nt(dbi, bim[...])).astype(du_ref.dtype)
        dbre_ref[...] = _mm_tn(u, dbr)
        dbim_ref[...] = _mm_tn(u, dbi)

    col = lambda n: BS((s, n), lambda k: (0, k))
    blk = lambda a, b: BS((None, a, b), lambda k: (k, 0, 0))
    outs = pl.pallas_call(
        body, name=f"ssm_bwd_l{l}", grid=(nblk,),
        in_specs=[col(gc), col(gc), col(gl), col(gl), blk(gc, gl), blk(gc, gl), blk(gc, gl), blk(gc, gl),
                  BS((10, 8, gl), lambda k: (0, 0, k)), BS((1, gc), lambda k: (0, k))],
        out_specs=[col(gc), blk(gc, gl), blk(gc, gl), blk(gc, gl), blk(gc, gl), BS((1, gl), lambda k: (0, k)),
                   BS((1, gl), lambda k: (0, k)), BS((1, gc), lambda k: (0, k))],
        out_shape=[SDS((s, nblk * gc), MXU_DTYPE), SDS((nblk, gc, gl), F32), SDS((nblk, gc, gl), F32),
                   SDS((nblk, gc, gl), F32), SDS((nblk, gc, gl), F32), SDS((1, nblk * gl), F32),
                   SDS((1, nblk * gl), F32), SDS((1, nblk * gc), F32)],
        scratch_shapes=[pltpu.VMEM((s, gl), F32), pltpu.VMEM((s, gl), F32)], compiler_params=_params(),
    )(dy, z, hre, him, bblk_re, bblk_im, cblk_re, cblk_im, pw, dskip)
    return dict(zip(("du", "dbblk_re", "dbblk_im", "dcblk_re", "dcblk_im", "dabar_re", "dabar_im", "dd"), outs))


def _conv_fwd(l, z, wdw, bdw):
    s = z.shape[0]
    cw = wdw.shape[1]
    lb = 128
    tr = min(256, s)
    off1 = cw // lb
    off2 = 2 * cw // lb

    def body(v1_ref, v2_ref, w_ref, b_ref, hc_ref, scr):
        scr[0:CONV_PAD, :] = jnp.zeros((CONV_PAD, lb), F32)
        scr[CONV_PAD:, :] = v1_ref[...] * _sigmoid(v2_ref[...])
        for t in range(s // tr):
            acc = jnp.broadcast_to(b_ref[...], (tr, lb))
            for k in range(CONV_KERNEL):
                acc = acc + w_ref[pl.ds(k, 1), :] * scr[pl.ds(t * tr + CONV_PAD - (CONV_KERNEL - 1) + k, tr), :]
            hc_ref[pl.ds(t * tr, tr), :] = acc

    return pl.pallas_call(
        body, name=f"conv_fwd_l{l}", grid=(cw // lb,),
        in_specs=[BS((s, lb), lambda k: (0, off1 + k)), BS((s, lb), lambda k: (0, off2 + k)),
                  BS((CONV_KERNEL, lb), lambda k: (0, k)), BS((1, lb), lambda k: (0, k))],
        out_specs=BS((s, lb), lambda k: (0, k)), out_shape=SDS((s, cw), F32),
        scratch_shapes=[pltpu.VMEM((s + CONV_PAD, lb), F32)], compiler_params=_params())(z, z, wdw, bdw)


def _conv_bwd(l, dhc, z, wdw):
    s = z.shape[0]
    cw = wdw.shape[1]
    lb = 128
    tr = min(256, s)
    off1 = cw // lb
    off2 = 2 * cw // lb
    nb = cw // lb

    def body(d_ref, v1_ref, v2_ref, w_ref, dv1_ref, dv2_ref, dw_ref, db_ref, hpad, dpad):
        v1 = v1_ref[...]
        sg = _sigmoid(v2_ref[...])
        dv = d_ref[...]
        hpad[0:CONV_PAD, :] = jnp.zeros((CONV_PAD, lb), F32)
        hpad[CONV_PAD:, :] = v1 * sg
        dpad[0:s, :] = dv
        dpad[s:, :] = jnp.zeros((CONV_PAD, lb), F32)
        db_ref[...] = _colsum(dv)
        dws = [jnp.zeros((1, lb), F32) for _ in range(CONV_KERNEL)]
        for t in range(s // tr):
            dt = d_ref[pl.ds(t * tr, tr), :]
            acc = jnp.zeros((tr, lb), F32)
            for k in range(CONV_KERNEL):
                acc = acc + w_ref[pl.ds(k, 1), :] * dpad[pl.ds(t * tr + (CONV_KERNEL - 1) - k, tr), :]
                dws[k] = dws[k] + _colsum(dt * hpad[pl.ds(t * tr + CONV_PAD - (CONV_KERNEL - 1) + k, tr), :])
            sgt = _sigmoid(v2_ref[pl.ds(t * tr, tr), :])
            v1t = v1_ref[pl.ds(t * tr, tr), :]
            dv1_ref[pl.ds(t * tr, tr), :] = (acc * sgt).astype(dv1_ref.dtype)
            dv2_ref[pl.ds(t * tr, tr), :] = (acc * v1t * (sgt * (1.0 - sgt))).astype(dv2_ref.dtype)
        for k in range(CONV_KERNEL):
            dw_ref[pl.ds(k, 1), :] = dws[k]

    return pl.pallas_call(
        body, name=f"conv_bwd_l{l}", grid=(nb,),
        in_specs=[BS((s, lb), lambda k: (0, k)), BS((s, lb), lambda k: (0, off1 + k)),
                  BS((s, lb), lambda k: (0, off2 + k)), BS((CONV_KERNEL, lb), lambda k: (0, k))],
        out_specs=[BS((s, lb), lambda k: (0, k)), BS((s, lb), lambda k: (0, k)),
                   BS((CONV_KERNEL, lb), lambda k: (0, k)), BS((1, lb), lambda k: (0, k))],
        out_shape=[SDS((s, cw), MXU_DTYPE), SDS((s, cw), MXU_DTYPE), SDS((CONV_KERNEL, cw), F32), SDS((1, cw), F32)],
        scratch_shapes=[pltpu.VMEM((s + CONV_PAD, lb), F32), pltpu.VMEM((s + CONV_PAD, lb), F32)],
        compiler_params=_params())(dhc, z, z, wdw)


def _pool_window(k):
    return jnp.where(k == 0, float(POOL_WINDOWS[0]),
                     jnp.where(k == 1, float(POOL_WINDOWS[1]),
                               jnp.where(k == 2, float(POOL_WINDOWS[2]), float(POOL_WINDOWS[3]))))


def _pool_fwd(l, z, pw_width):
    s = z.shape[0]
    lb = pw_width // len(POOL_WINDOWS)
    off = 3 * pw_width // lb

    def body(u_ref, p_ref):
        k = pl.program_id(0)
        u = u_ref[...]
        row = lax.broadcasted_iota(jnp.int32, (s, lb), 0)
        sums = [u]
        for sh in (1, 2, 4, 8):
            prev = sums[-1]
            sums.append(prev + jnp.where(row >= sh, pltpu.roll(prev, sh, 0), 0.0))
        sel = jnp.where(k == 0, sums[1], jnp.where(k == 1, sums[2], jnp.where(k == 2, sums[3], sums[4])))
        cnt = jnp.minimum((row + 1).astype(F32), _pool_window(k))
        p_ref[...] = sel / cnt - u

    return pl.pallas_call(
        body, name=f"pool_fwd_l{l}", grid=(len(POOL_WINDOWS),),
        in_specs=[BS((s, lb), lambda k: (0, off + k))], out_specs=BS((s, lb), lambda k: (0, k)),
        out_shape=SDS((s, pw_width), F32), compiler_params=_params())(z)


def _pool_bwd(l, dp):
    s, width = dp.shape
    lb = width // len(POOL_WINDOWS)

    def body(d_ref, du_ref):
        k = pl.program_id(0)
        dv = d_ref[...]
        row = lax.broadcasted_iota(jnp.int32, (s, lb), 0)
        cnt = jnp.minimum((row + 1).astype(F32), _pool_window(k))
        sums = [dv / cnt]
        for sh in (1, 2, 4, 8):
            prev = sums[-1]
            sums.append(prev + jnp.where(row < s - sh, pltpu.roll(prev, s - sh, 0), 0.0))
        sel = jnp.where(k == 0, sums[1], jnp.where(k == 1, sums[2], jnp.where(k == 2, sums[3], sums[4])))
        du_ref[...] = (sel - dv).astype(du_ref.dtype)

    return pl.pallas_call(
        body, name=f"pool_bwd_l{l}", grid=(len(POOL_WINDOWS),),
        in_specs=[BS((s, lb), lambda k: (0, k))], out_specs=BS((s, lb), lambda k: (0, k)),
        out_shape=SDS((s, width), MXU_DTYPE), compiler_params=_params())(dp)


def _zoh(a_re, a_im, log_dt):
    dt = jnp.exp(log_dt)
    mag = jnp.exp(dt * a_re)
    ang = dt * a_im
    abar_re = mag * jnp.cos(ang)
    abar_im = mag * jnp.sin(ang)
    den = a_re * a_re + a_im * a_im
    nr = abar_re - 1.0
    ni = abar_im
    f_re = (nr * a_re + ni * a_im) / den
    f_im = (ni * a_re - nr * a_im) / den
    return abar_re, abar_im, f_re, f_im


def _zoh_fwd(l, a_re, a_im, log_dt):
    def body(ar, ai, ld, o0, o1, o2, o3):
        for ref, val in zip((o0, o1, o2, o3), _zoh(ar[...], ai[...], ld[...])):
            ref[...] = val

    return pl.pallas_call(body, name=f"zoh_fwd_l{l}", out_shape=[SDS(a_re.shape, F32)] * 4)(a_re, a_im, log_dt)


def _zoh_bwd(l, a_re, a_im, log_dt, cts):
    def body(ar, ai, ld, c0, c1, c2, c3, dar, dai, dld):
        _, vjp = jax.vjp(_zoh, ar[...], ai[...], ld[...])
        g = vjp((c0[...], c1[...], c2[...], c3[...]))
        dar[...] = g[0]
        dai[...] = g[1]
        dld[...] = g[2]

    return pl.pallas_call(body, name=f"zoh_bwd_l{l}",
                          out_shape=[SDS(a_re.shape, F32), SDS(a_re.shape, F32), SDS(log_dt.shape, F32)],
                          )(a_re, a_im, log_dt, *cts)


def _bbar_fwd(l, f_re, f_im, b_re, b_im):
    g, p, n = b_re.shape[1:]

    def body(fr, fi, br, bi, o_re, o_im):
        o_re[...] = (fr[...] * br[...] - fi[...] * bi[...]).astype(o_re.dtype)
        o_im[...] = (fr[...] * bi[...] + fi[...] * br[...]).astype(o_im.dtype)

    whole = lambda shp: BS(shp, lambda i: (0,) * len(shp))
    layer = BS((None, g, p, n), lambda i: (l, 0, 0, 0))
    return pl.pallas_call(body, name=f"bbar_fwd_l{l}", grid=(1,),
                          in_specs=[whole((g, 1, n)), whole((g, 1, n)), layer, layer],
                          out_specs=[whole((g, p, n))] * 2,
                          out_shape=[SDS((g, p, n), MXU_DTYPE)] * 2)(f_re, f_im, b_re, b_im)


def _bbar_bwd(l, f_re, f_im, b_re, b_im, d_re, d_im):
    g, p, n = b_re.shape[1:]

    def body(fr, fi, br, bi, dr, di, dfr, dfi, dbr, dbi):
        dfr[...] = jnp.sum(dr[...] * br[...] + di[...] * bi[...], axis=1, keepdims=True)
        dfi[...] = jnp.sum(di[...] * br[...] - dr[...] * bi[...], axis=1, keepdims=True)
        dbr[...] = fr[...] * dr[...] + fi[...] * di[...]
        dbi[...] = fr[...] * di[...] - fi[...] * dr[...]

    whole = lambda shp: BS(shp, lambda i: (0,) * len(shp))
    layer = BS((None, g, p, n), lambda i: (l, 0, 0, 0))
    return pl.pallas_call(body, name=f"bbar_bwd_l{l}", grid=(1,),
                          in_specs=[whole((g, 1, n)), whole((g, 1, n)), layer, layer, whole((g, p, n)),
                                    whole((g, p, n))],
                          out_specs=[whole((g, 1, n)), whole((g, 1, n)), whole((g, p, n)), whole((g, p, n))],
                          out_shape=[SDS((g, 1, n), F32), SDS((g, 1, n), F32), SDS((g, p, n), F32),
                                     SDS((g, p, n), F32)])(f_re, f_im, b_re, b_im, d_re, d_im)


def _powers(l, abar_re, abar_im):
    lanes = abar_re.shape[1]

    def body(ar_ref, ai_ref, o_ref):
        ar, ai = ar_ref[...], ai_ref[...]
        pows = [(ar, ai)]
        for _ in range(7):
            pr, pi = pows[-1]
            pows.append((pr * ar - pi * ai, pr * ai + pi * ar))
        row = lax.broadcasted_iota(jnp.int32, (8, lanes), 0)
        for i, k in enumerate((1, 2, 4)):
            o_ref[2 * i] = jnp.broadcast_to(pows[k - 1][0], (8, lanes))
            o_ref[2 * i + 1] = jnp.broadcast_to(pows[k - 1][1], (8, lanes))
        for slot, order in ((3, range(8)), (4, range(7, -1, -1))):
            vr = jnp.zeros((8, lanes), F32)
            vi = jnp.zeros((8, lanes), F32)
            for r, e in enumerate(order):
                vr = jnp.where(row == r, pows[e][0], vr)
                vi = jnp.where(row == r, pows[e][1], vi)
            o_ref[2 * slot] = vr
            o_ref[2 * slot + 1] = vi

    return pl.pallas_call(body, name=f"powers_l{l}", out_shape=SDS((10, 8, lanes), F32))(abar_re, abar_im)


def _block_diag(v):
    g, a, b = v.shape
    eye = jnp.eye(8, dtype=v.dtype)
    out = jnp.einsum("kgab,gh->kgahb", v.reshape(g // 8, 8, a, b), eye)
    return out.reshape(g // 8, 8 * a, 8 * b)


def _block_diag_extract(blk, a, b):
    n = blk.shape[0]
    v = blk.reshape(n, 8, a, 8, b)
    return jnp.einsum("kgahb,gh->kgab", v, jnp.eye(8, dtype=blk.dtype)).reshape(n * 8, a, b)


def _ssm_prepare(l, prm):
    g, n, p = SSM_GROUPS, SSM_STATE, SSM_GROUP
    a_re, a_im = prm["ssm_a_re"][l], prm["ssm_a_im"][l]
    log_dt = prm["ssm_log_dt"][l].reshape(g, 1)
    abar_re, abar_im, f_re, f_im = _zoh_fwd(l, a_re, a_im, log_dt)
    f_re, f_im = f_re.reshape(g, 1, n), f_im.reshape(g, 1, n)
    bbar_re, bbar_im = _bbar_fwd(l, f_re, f_im, prm["ssm_b_re"], prm["ssm_b_im"])
    pw = _powers(l, abar_re.reshape(1, g * n), abar_im.reshape(1, g * n))
    return dict(a_re=a_re, a_im=a_im, log_dt=log_dt, f_re=f_re, f_im=f_im,
                bblk_re=_block_diag(bbar_re), bblk_im=_block_diag(bbar_im),
                cblk_re=_block_diag(prm["ssm_c_re"][l].astype(MXU_DTYPE)),
                cblk_im=_block_diag(prm["ssm_c_im"][l].astype(MXU_DTYPE)), pw=pw,
                dskip=prm["ssm_d"][l].reshape(1, g * p))


def _ssm_param_grads(l, sd, r, prm):
    g, n, p = SSM_GROUPS, SSM_STATE, SSM_GROUP
    dbbar_re = _block_diag_extract(r["dbblk_re"], p, n)
    dbbar_im = _block_diag_extract(r["dbblk_im"], p, n)
    dfr, dfi, db_re, db_im = _bbar_bwd(l, sd["f_re"], sd["f_im"], prm["ssm_b_re"], prm["ssm_b_im"], dbbar_re, dbbar_im)
    cts = (r["dabar_re"].reshape(g, n), r["dabar_im"].reshape(g, n), dfr.reshape(g, n), dfi.reshape(g, n))
    da_re, da_im, dlog_dt = _zoh_bwd(l, sd["a_re"], sd["a_im"], sd["log_dt"], cts)
    return dict(ssm_a_re=da_re, ssm_a_im=da_im, ssm_log_dt=dlog_dt.reshape(g), ssm_b_re=db_re, ssm_b_im=db_im,
                ssm_c_re=_block_diag_extract(r["dcblk_re"], p, n), ssm_c_im=_block_diag_extract(r["dcblk_im"], p, n),
                ssm_d=r["dd"].reshape(g, p))


def _ffn_weight_grads(l, fb, dx2, s, place):
    d = dx2.shape[1]
    hcn = fb["act"].shape[1]
    g = {}
    for name, key, rhs in (("ffn_w_gate", "dgate", fb["h2"]), ("ffn_w_up", "dup", fb["h2"]),
                           ("ffn_w_down", "act", fb["dx2"])):
        g[name] = _tn_matmul(f"d{name}_l{l}", fb[key], BS((None, hcn, s), lambda j, t, pr: (j, 0, 0)), rhs,
                             BS((s, d), lambda j, t, pr: (0, 0)), (hcn, d), (N_CHIPS, 1), place)
    return g


def _in_weight_grad(l, ht, dz, place):
    d, s = ht.shape
    ncw = dz.shape[1] // N_CHIPS
    return _tn_matmul(f"dw_in_l{l}", ht, BS((d, s), lambda j, t, pr: (0, 0)), dz, BS((s, ncw), lambda j, t, pr: (0, j)),
                      (d, ncw), (N_CHIPS, 1), place)


def _fused_tn(name, pairs, kinds, s, place):
    n = len(pairs)

    def shape_of(a, b, kind):
        k, m = a.shape[1], b.shape[1]
        if kind == "rows":
            return (N_CHIPS, k // N_CHIPS, m)
        if kind == "cols":
            return (N_CHIPS, k, m // N_CHIPS)
        return (k // 128, 128, 128)

    shapes = [shape_of(a, b, kind) for (a, b), kind in zip(pairs, kinds)]
    out_shape = []
    for shp, kind in zip(shapes, kinds):
        out_shape += [SDS(shp, F32)] if kind == "groups" else [SDS(shp[1:], F32), SDS(shp, WIRE_DTYPE)]

    def body(place_ref, *refs):
        ins, outs, accs = refs[:2 * n], refs[2 * n:2 * n + len(out_shape)], refs[2 * n + len(out_shape):]
        o = 0
        for i, kind in enumerate(kinds):
            a, b = ins[2 * i][...], ins[2 * i + 1][...]
            if kind == "groups":
                for k in range(shapes[i][0]):
                    outs[o][k] = _mm_tn(a[:, k * 128:(k + 1) * 128], b[:, k * 128:(k + 1) * 128])
                o += 1
                continue
            acc = accs[i]
            if kind == "rows":
                acc[...] = _mm_tn(a, b).reshape(acc.shape)
            else:
                full = _mm_tn(a, b)
                nc = acc.shape[2]
                for j in range(N_CHIPS):
                    acc[j] = full[:, j * nc:(j + 1) * nc]
            outs[o][...] = acc[place_ref[0]]
            outs[o + 1][...] = acc[...].astype(WIRE_DTYPE)
            o += 2

    whole = lambda shp: BS(shp, lambda t, pr: (0,) * len(shp))
    outs = pl.pallas_call(
        body, name=name,
        grid_spec=pltpu.PrefetchScalarGridSpec(
            num_scalar_prefetch=1, grid=(1,),
            in_specs=[whole(v.shape) for pair in pairs for v in pair],
            out_specs=[whole(o.shape) for o in out_shape],
            scratch_shapes=[pltpu.VMEM(shp, F32) for shp in shapes]),
        out_shape=out_shape, compiler_params=_params(),
    )(place, *[v for pair in pairs for v in pair])
    res, o = [], 0
    for kind in kinds:
        if kind == "groups":
            res.append(outs[o])
            o += 1
        else:
            res.append((outs[o], outs[o + 1]))
            o += 2
    return res


def _mixer_weight_grads(l, sv, mb, dx1, s, place):
    g = {}
    (g["w_out"], g["ssm_w_glu"]) = _fused_tn(f"dw_out_glu_l{l}", [(mb["merged"], dx1), (mb["ge"], mb["dt"])],
                                            ("rows", "rows"), s, place)
    (g["ssm_w_proj"], g["conv_w_proj"], g["pool_w_proj"]) = _fused_tn(
        f"dw_proj_l{l}", [(mb["sa"], mb["dya"]), (mb["ac"], mb["dyb"]), (mb["pp"], mb["dyc"])],
        ("cols", "cols", "cols"), s, place)
    (dwgrp,) = _fused_tn(f"dpool_w_group_l{l}", [(sv["p"], mb["dq"])], ("groups",), s, place)
    return g, dwgrp


def _local_step(x, target, weights_of, prm, place, on_grads=None):
    s, d = x.shape
    cw = prm["ssm_b_glu"].shape[1]
    sp = {k: prm[k].reshape(N_LAYERS, 1, -1) for k in ("norm1", "norm2", "b_gate", "ssm_b_glu", "conv_ln_g", "conv_ln_b",
                                                        "pool_scale", "conv_b_dw")}
    sp["pool_w_group"] = prm["pool_w_group"]
    saved = []
    xin = x
    prepared = [_ssm_prepare(l, prm) for l in range(N_LAYERS)]
    ready = tuple(sd[k] for sd in prepared for k in ("pw", "bblk_re", "bblk_im", "cblk_re", "cblk_im"))
    for l in range(N_LAYERS):
        fw = weights_of(l, "in", (xin,) + (ready if l == 0 else ()))
        sd = prepared[l]
        z, h = _in_proj(l, xin, sp["norm1"], fw["w_in"])
        hre, him, y = _ssm_fwd(l, z, sd["bblk_re"], sd["bblk_im"], sd["cblk_re"], sd["cblk_im"], sd["pw"], sd["dskip"])
        p = _pool_fwd(l, z, cw)
        fw.update(weights_of(l, "mixer", (y, p)))
        wdw = fw["conv_w_dw"]
        hc = _conv_fwd(l, z, wdw, sp["conv_b_dw"][l])
        x1 = _merge_fwd(l, xin, y, hc, p, z, fw, sp)
        fw.update(weights_of(l, "ffn", (x1,)))
        x2, gate, up, h2 = _ffn_fwd(l, x1, sp["norm2"], fw["ffn_w_gate"], fw["ffn_w_up"], fw["ffn_w_down"])
        saved.append(dict(x=xin, z=z, h=h, hre=hre, him=him, y=y, hc=hc, p=p, x1=x1, sd=sd, wdw=wdw, fw=fw,
                          gate=gate, up=up, h2=h2))
        xin = x2
    dx, loss, dfinal = _loss_head(xin, target, prm["final_norm"].reshape(1, d))
    big = [None] * N_LAYERS
    small = [None] * N_LAYERS
    norm2_rows = sp["norm2"]
    started = (lambda l, group, grads: on_grads(l, group, grads)) if on_grads is not None else (lambda *a: 0.0)
    for l in reversed(range(N_LAYERS)):
        sv = saved[l]
        sd, fw = sv["sd"], sv["fw"]
        fb = _ffn_bwd(l, sv["x1"], dx, sv["gate"], sv["up"], norm2_rows, fw["ffn_w_gate"], fw["ffn_w_up"],
                      fw["ffn_w_down"])
        fb["h2"] = sv["h2"]
        big[l] = _ffn_weight_grads(l, fb, dx, s, place)
        spl = dict(sp, ssm_b_glu=sp["ssm_b_glu"] + started(l, "ffn", big[l]))
        mb = _merge_bwd(l, fb["dx1"], sv["y"], sv["hc"], sv["p"], sv["z"], fw, spl)
        mixer, dwgrp = _mixer_weight_grads(l, sv, mb, fb["dx1"], s, place)
        big[l].update(mixer)
        wdw = sv["wdw"] + started(l, "mixer", mixer)
        du_c = _pool_bwd(l, mb["dp"])
        dv1, dv2, dwdw, dbdw = _conv_bwd(l, mb["dhc"], sv["z"], wdw)
        sr = _ssm_bwd(l, mb["dy"], sv["z"], sv["hre"], sv["him"], sd["bblk_re"], sd["bblk_im"], sd["cblk_re"],
                      sd["cblk_im"], sd["pw"], sd["dskip"])
        dx, dz, dnorm1 = _in_proj_bwd(l, fb["dx1"], sv["x"], sp["norm1"], fw["w_in"], sr["du"], dv1, dv2, du_c, mb["dzg"])
        w_in_grad = {"w_in": _in_weight_grad(l, sv["h"], dz, place)}
        big[l].update(w_in_grad)
        sg = _ssm_param_grads(l, sd, sr, prm)
        sg.update(norm1=dnorm1.reshape(d), b_gate=mb["db_gate"].reshape(3 * d), ssm_b_glu=mb["db_glu"].reshape(cw),
                  conv_b_dw=dbdw.reshape(cw), conv_ln_g=mb["dln_g"].reshape(cw), conv_ln_b=mb["dln_b"].reshape(cw),
                  pool_w_group=dwgrp, pool_scale=mb["dscale"].reshape(cw), norm2=fb["dnorm2"].reshape(d),
                  conv_w_dw=dwdw)
        small[l] = sg
        if l == N_LAYERS - 1:
            sg = dict(sg, final_norm=dfinal.reshape(d))
        norm2_rows = sp["norm2"] + (started(l, "in", w_in_grad) + started(l, "small", sg))
    return loss[0, 0], dx, big, small, dfinal.reshape(d)


def _place():
    return lax.axis_index("x"), lax.axis_index("y"), lax.axis_index("c")


def _other_chips(x, y):
    return [(1 - x, y), (x, 1 - y), (1 - x, 1 - y)]


def _remote(src, dst, send_sem, recv_sem, device):
    return pltpu.make_async_remote_copy(src_ref=src, dst_ref=dst, send_sem=send_sem, recv_sem=recv_sem,
                                        device_id=device, device_id_type=MESH)


def _hbm(v):
    return pltpu.with_memory_space_constraint(v, pltpu.HBM)


def _cast_into(name, w, place, dtype, after=()):
    nl, k, n = w.shape
    tr = _row_tile(k, n)
    nt = k // tr

    def body(place_ref, w_ref, *rest):
        o0_ref, o1_ref = rest[len(after):]

        @pl.when(pl.program_id(0) == 0)
        def _():
            o0_ref[...] = w_ref[...].astype(dtype)

        @pl.when(pl.program_id(0) == 1)
        def _():
            o1_ref[...] = w_ref[...].astype(dtype)

    return pl.pallas_call(
        body, name=f"cast_{name}",
        grid_spec=pltpu.PrefetchScalarGridSpec(
            num_scalar_prefetch=1, grid=(nl, nt),
            in_specs=[BS((None, tr, n), lambda l, t, pr: (l, t, 0))] + [ANY] * len(after),
            out_specs=[BS((None, tr, n), lambda l, t, pr: (pr[0], t * (1 - l) + (nt - 1) * l, 0)),
                       BS((None, tr, n), lambda l, t, pr: (pr[0], t * l, 0))]),
        out_shape=[SDS((N_CHIPS, k, n), dtype)] * 2)(place, w, *after)


def _cast_small_into(tag, ws, dtypes, place, after=()):
    n = len(ws)

    def body(place_ref, *refs):
        ins, outs = refs[:n], refs[n + len(after):]
        for i in range(n):
            @pl.when(pl.program_id(0) == 0)
            def _():
                outs[2 * i][...] = ins[i][...].astype(dtypes[i])

            @pl.when(pl.program_id(0) == 1)
            def _():
                outs[2 * i + 1][...] = ins[i][...].astype(dtypes[i])

    slot = lambda w: BS((None,) + w.shape[1:], lambda l, pr: (pr[0], 0, 0))
    outs = pl.pallas_call(
        body, name=f"cast_{tag}",
        grid_spec=pltpu.PrefetchScalarGridSpec(
            num_scalar_prefetch=1, grid=(N_LAYERS,),
            in_specs=[BS((None,) + w.shape[1:], lambda l, pr: (l, 0, 0)) for w in ws] + [ANY] * len(after),
            out_specs=[slot(w) for w in ws for _ in range(N_LAYERS)]),
        out_shape=[SDS((N_CHIPS,) + w.shape[1:], dt) for w, dt in zip(ws, dtypes) for _ in range(N_LAYERS)],
    )(place, *ws, *after)
    return [tuple(outs[N_LAYERS * i:N_LAYERS * (i + 1)]) for i in range(n)]


def _gather_rows(buf, c):
    k = buf.shape[1]
    if k % 2:
        return pl.ds(0, k)
    return pl.ds(pl.multiple_of(c * (k // 2), 8), k // 2)


def _allgather_start(tag, groups):
    ng = len(groups)
    sizes = [len(g) for g in groups]
    first = [sum(sizes[:g]) for g in range(ng)]
    flat = [b for g in groups for b in g]
    nb = len(flat)

    def body(*refs):
        ins = refs[:nb]
        sems = refs[nb:nb + 2 * ng]
        token = refs[-1]
        x, y, c = _place()
        jme = 2 * x + y
        for g in range(ng):
            for a in range(sizes[g]):
                buf = ins[first[g] + a]
                blk = buf.at[jme, _gather_rows(buf, c)]
                for k, (cx, cy) in enumerate(_other_chips(x, y)):
                    _remote(blk, blk, sems[2 * g].at[3 * a + k], sems[2 * g + 1].at[3 * a + k], (cx, cy, c)).start()
        token[...] = jnp.zeros(token.shape, F32)

    sem_shapes = [pltpu.SemaphoreType.DMA((3 * sizes[g // 2],)) for g in range(2 * ng)]
    outs = pl.pallas_call(
        body, name=f"allgather_start_{tag}", in_specs=[HBM] * nb,
        out_specs=[SEM] * (2 * ng) + [HBM] * nb + [pl.BlockSpec(memory_space=pltpu.VMEM)],
        out_shape=sem_shapes + [pltpu.HBM(b.shape, b.dtype) for b in flat] + [SDS((8, 128), F32)],
        input_output_aliases={i: 2 * ng + i for i in range(nb)},
        compiler_params=pltpu.CompilerParams(has_side_effects=SIDE_EFFECT))(*[_hbm(b) for b in flat])
    per_group = [(outs[2 * g], outs[2 * g + 1], outs[2 * ng + first[g]:2 * ng + first[g] + sizes[g]])
                 for g in range(ng)]
    return per_group, outs[-1]


def _allgather_wait(l, send_sems, recv_sems, bufs, after):
    n = len(bufs)

    def body(*refs):
        ins = refs[:n]
        ssem, rsem = refs[n], refs[n + 1]
        x, y, c = _place()
        jme = 2 * x + y
        for a in range(n):
            rows = _gather_rows(ins[a], c)
            for k, (cx, cy) in enumerate(_other_chips(x, y)):
                cp = _remote(ins[a].at[jme, rows], ins[a].at[2 * cx + cy, rows], ssem.at[3 * a + k],
                             rsem.at[3 * a + k], (cx, cy, c))
                cp.wait_send()
                cp.wait_recv()

    return pl.pallas_call(
        body, name=f"allgather_wait_{l}", in_specs=[HBM] * n + [SEM, SEM] + [ANY] * len(after), out_specs=[HBM] * n,
        out_shape=[pltpu.HBM(b.shape, b.dtype) for b in bufs], input_output_aliases={i: i for i in range(n)},
        compiler_params=pltpu.CompilerParams(has_side_effects=SIDE_EFFECT))(*bufs, send_sems, recv_sems, *after)


def _allgather_forward(l, bufs):
    n = len(bufs)
    split = [a for a in range(n) if bufs[a].shape[1] % 2 == 0]

    def body(*refs):
        ins = refs[:n]
        send_sems, recv_sems = refs[2 * n:]
        x, y, c = _place()
        sibling = (x, y, 1 - c)
        copies = []
        for a in split:
            for k, (cx, cy) in enumerate(_other_chips(x, y)):
                blk = ins[a].at[2 * cx + cy, _gather_rows(ins[a], c)]
                cp = _remote(blk, blk, send_sems.at[a, k], recv_sems.at[a, k], sibling)
                cp.start()
                copies.append(cp)
        for a in split:
            for k, (cx, cy) in enumerate(_other_chips(x, y)):
                blk = ins[a].at[2 * cx + cy, _gather_rows(ins[a], 1 - c)]
                _remote(blk, blk, send_sems.at[a, k], recv_sems.at[a, k], sibling).wait_recv()
        for cp in copies:
            cp.wait_send()

    sem = pltpu.SemaphoreType.DMA((n, 3))
    return pl.pallas_call(
        body, name=f"allgather_forward_{l}", in_specs=[ANY] * n, out_specs=[ANY] * n,
        out_shape=[SDS(b.shape, b.dtype) for b in bufs], input_output_aliases={i: i for i in range(n)},
        scratch_shapes=[sem, sem])(*bufs)


def _rs_to_owner(l, parts):
    n = len(parts)
    lands = [lax.empty((3,) + p.shape[1:], p.dtype) for p in parts]

    def body(*refs):
        ins, zones = refs[:n], refs[n:2 * n]
        send_sems, recv_sems = refs[2 * n], refs[2 * n + 1]
        token = refs[-1]
        x, y, c = _place()
        for a in range(n):
            for k, (cx, cy) in enumerate(_other_chips(x, y)):
                _remote(ins[a].at[2 * cx + cy], zones[a].at[k], send_sems.at[3 * a + k], recv_sems.at[3 * a + k],
                        (cx, cy, c)).start()
        token[...] = jnp.zeros(token.shape, F32)

    sem = pltpu.SemaphoreType.DMA((3 * n,))
    outs = pl.pallas_call(
        body, name=f"rs_to_owner_start_{l}", in_specs=[HBM] * (2 * n),
        out_specs=[SEM, SEM] + [HBM] * (2 * n) + [pl.BlockSpec(memory_space=pltpu.VMEM)],
        out_shape=[sem, sem] + [pltpu.HBM(p.shape, p.dtype) for p in parts]
        + [pltpu.HBM(z.shape, z.dtype) for z in lands] + [SDS((8, 128), F32)],
        input_output_aliases={i: 2 + i for i in range(2 * n)},
        compiler_params=pltpu.CompilerParams(has_side_effects=SIDE_EFFECT),
    )(*[_hbm(p) for p in parts], *[_hbm(z) for z in lands])
    return outs[0], outs[1], outs[2:2 + n], outs[2 + n:2 + 2 * n], outs[-1]


def _rs_to_owner_wait(l, send_sems, recv_sems, parts, lands, after):
    n = len(parts)

    def body(*refs):
        ins, zones = refs[:n], refs[n:2 * n]
        ssem, rsem = refs[2 * n], refs[2 * n + 1]
        x, y, c = _place()
        for a in range(n):
            for k, (cx, cy) in enumerate(_other_chips(x, y)):
                cp = _remote(ins[a].at[2 * cx + cy], zones[a].at[k], ssem.at[3 * a + k], rsem.at[3 * a + k],
                             (cx, cy, c))
                cp.wait_send()
                cp.wait_recv()

    outs = pl.pallas_call(
        body, name=f"rs_to_owner_wait_{l}", in_specs=[HBM] * (2 * n) + [SEM, SEM] + [ANY] * len(after),
        out_specs=[HBM] * (2 * n),
        out_shape=[pltpu.HBM(p.shape, p.dtype) for p in parts] + [pltpu.HBM(z.shape, z.dtype) for z in lands],
        input_output_aliases={i: i for i in range(2 * n)},
        compiler_params=pltpu.CompilerParams(has_side_effects=SIDE_EFFECT),
    )(*parts, *lands, send_sems, recv_sems, *after)
    return outs[:n], outs[n:]


def _rs_sibling_exchange(l, both):
    n = len(both)

    def body(*refs):
        ins = refs[:n]
        send_sems, recv_sems = refs[2 * n:]
        x, y, c = _place()
        copies = []
        for a in range(n):
            cp = _remote(ins[a].at[c], ins[a].at[c], send_sems.at[a], recv_sems.at[a], (x, y, 1 - c))
            cp.start()
            copies.append(cp)
        for a, cp in enumerate(copies):
            cp.wait_send()
            _remote(ins[a].at[1 - c], ins[a].at[1 - c], send_sems.at[a], recv_sems.at[a], (x, y, 1 - c)).wait_recv()

    sem = pltpu.SemaphoreType.DMA((n,))
    return pl.pallas_call(
        body, name=f"rs_sibling_exchange_{l}", in_specs=[ANY] * n, out_specs=[ANY] * n,
        out_shape=[SDS(b.shape, b.dtype) for b in both], input_output_aliases={i: i for i in range(n)},
        scratch_shapes=[sem, sem])(*both)


def _add_owner(name, grad, recv, place):
    r, cols = grad.shape
    tr = _row_tile(r, cols, budget=1024 * 1024)
    nt = r // tr

    def body(place_ref, g_ref, r_ref, o_ref):
        acc = ((g_ref[...] + r_ref[0].astype(F32)) + r_ref[1].astype(F32)) + r_ref[2].astype(F32)
        o_ref[...] = acc.astype(o_ref.dtype)

    return pl.pallas_call(
        body, name=name,
        grid_spec=pltpu.PrefetchScalarGridSpec(
            num_scalar_prefetch=1, grid=(nt,),
            in_specs=[BS((tr, cols), lambda t, pr: (t, 0)), BS((3, tr, cols), lambda t, pr: (0, t, 0))],
            out_specs=BS((None, tr, cols), lambda t, pr: (pr[1], t, 0))),
        out_shape=SDS((2, r, cols), WIRE_DTYPE))(place, grad, recv)


def _add_owner_small(tag, grads, recvs, place):
    n = len(grads)

    def body(place_ref, *refs):
        gs, rs, outs = refs[:n], refs[n:2 * n], refs[2 * n:]
        for g_ref, r_ref, o_ref in zip(gs, rs, outs):
            acc = ((g_ref[...] + r_ref[0].astype(F32)) + r_ref[1].astype(F32)) + r_ref[2].astype(F32)
            o_ref[...] = acc.astype(o_ref.dtype)

    return pl.pallas_call(
        body, name=f"rs_add_owner_{tag}",
        grid_spec=pltpu.PrefetchScalarGridSpec(
            num_scalar_prefetch=1, grid=(1,),
            in_specs=[BS(g.shape, lambda t, pr: (0, 0)) for g in grads]
            + [BS(r.shape, lambda t, pr: (0, 0, 0)) for r in recvs],
            out_specs=[BS((None,) + g.shape, lambda t, pr: (pr[1], 0, 0)) for g in grads]),
        out_shape=[SDS((2,) + g.shape, WIRE_DTYPE) for g in grads])(place, *grads, *recvs)


def _reduce_start(tag, grads):
    names = list(grads)
    send_sems, recv_sems, wires, lands, token = _rs_to_owner(tag, [grads[n][1] for n in names])
    return dict(tag=tag, names=names, send_sems=send_sems, recv_sems=recv_sems, wires=wires, lands=lands,
                grads=[grads[n][0] for n in names]), token


def _reduce_finish(tag, groups, place, after):
    all_names, all_mine = [], []
    for pending in groups:
        sub, names = pending["tag"], pending["names"]
        _, lands = _rs_to_owner_wait(sub, pending["send_sems"], pending["recv_sems"], pending["wires"],
                                     pending["lands"], after)
        if max(g.size for g in pending["grads"]) <= SMALL_GRAD_ELEMS:
            mine = _add_owner_small(sub, pending["grads"], lands, place)
        else:
            mine = [_add_owner(f"rs_add_owner_{n}_{sub}", g, r, place)
                    for n, g, r in zip(names, pending["grads"], lands)]
        all_names += names
        all_mine += mine
    return dict(zip(all_names, _rs_sibling_exchange(tag, all_mine)))


def _small_peers(x, y, c):
    return [(x, y, 1 - c)] + [(cx, cy, c) for cx, cy in _other_chips(x, y)]


def _allgather_rows_start(tag, bufs):
    n = len(bufs)
    lands = [lax.empty((8,) + b.shape, b.dtype) for b in bufs]

    def body(*refs):
        ins, zones = refs[:n], refs[n:2 * n]
        send_sems, recv_sems = refs[2 * n], refs[2 * n + 1]
        token = refs[-1]
        x, y, c = _place()
        for a in range(n):
            for i, peer in enumerate(_small_peers(x, y, c)):
                _remote(ins[a], zones[a].at[4 * x + 2 * y + c], send_sems.at[4 * a + i], recv_sems.at[4 * a + i],
                        peer).start()
        token[...] = jnp.zeros(token.shape, F32)

    sem = pltpu.SemaphoreType.DMA((4 * n,))
    outs = pl.pallas_call(
        body, name=f"allgather_small_start_{tag}", in_specs=[HBM] * (2 * n),
        out_specs=[SEM, SEM] + [HBM] * (2 * n) + [pl.BlockSpec(memory_space=pltpu.VMEM)],
        out_shape=[sem, sem] + [pltpu.HBM(b.shape, b.dtype) for b in bufs]
        + [pltpu.HBM(z.shape, z.dtype) for z in lands] + [SDS((8, 128), F32)],
        input_output_aliases={i: 2 + i for i in range(2 * n)},
        compiler_params=pltpu.CompilerParams(has_side_effects=SIDE_EFFECT),
    )(*[_hbm(b) for b in bufs], *[_hbm(z) for z in lands])
    return outs[0], outs[1], outs[2:2 + n], outs[2 + n:2 + 2 * n], outs[-1]


def _allgather_rows_wait(tag, send_sems, recv_sems, bufs, lands, after):
    n = len(bufs)

    def body(*refs):
        ins, zones = refs[:n], refs[n:2 * n]
        ssem, rsem = refs[2 * n], refs[2 * n + 1]
        x, y, c = _place()
        for a in range(n):
            for i, (px, py, pc) in enumerate(_small_peers(x, y, c)):
                cp = _remote(ins[a], zones[a].at[4 * px + 2 * py + pc], ssem.at[4 * a + i], rsem.at[4 * a + i],
                             (px, py, pc))
                cp.wait_send()
                cp.wait_recv()

    outs = pl.pallas_call(
        body, name=f"allgather_small_wait_{tag}", in_specs=[HBM] * (2 * n) + [SEM, SEM, ANY],
        out_specs=[HBM] * (2 * n),
        out_shape=[pltpu.HBM(b.shape, b.dtype) for b in bufs] + [pltpu.HBM(z.shape, z.dtype) for z in lands],
        input_output_aliases={i: i for i in range(2 * n)},
        compiler_params=pltpu.CompilerParams(has_side_effects=SIDE_EFFECT),
    )(*bufs, *lands, send_sems, recv_sems, after)
    return outs[:n], outs[n:]


def _allgather_rows_forward(tag, lands):
    n = len(lands)

    def body(*refs):
        ins = refs[:n]
        send_sems, recv_sems = refs[2 * n:]
        x, y, c = _place()
        sibling = (x, y, 1 - c)
        copies = []
        for a in range(n):
            for k, (cx, cy) in enumerate(_other_chips(x, y)):
                blk = ins[a].at[4 * cx + 2 * cy + c]
                cp = _remote(blk, blk, send_sems.at[a, k], recv_sems.at[a, k], sibling)
                cp.start()
                copies.append(cp)
        for a in range(n):
            for k, (cx, cy) in enumerate(_other_chips(x, y)):
                blk = ins[a].at[4 * cx + 2 * cy + 1 - c]
                _remote(blk, blk, send_sems.at[a, k], recv_sems.at[a, k], sibling).wait_recv()
        for cp in copies:
            cp.wait_send()

    sem = pltpu.SemaphoreType.DMA((n, 3))
    return pl.pallas_call(body, name=f"allgather_small_forward_{tag}", in_specs=[ANY] * n, out_specs=[ANY] * n,
                          out_shape=[SDS(z.shape, z.dtype) for z in lands],
                          input_output_aliases={i: i for i in range(n)}, scratch_shapes=[sem, sem])(*lands)


def _sum_devices(tag, gathered, mine, place):
    _, r, cols = gathered.shape
    tr = _row_tile(r, cols, budget=256 * 1024)

    def body(place_ref, g_ref, x_ref, o_ref):
        me = 2 * place_ref[0] + place_ref[1]
        acc = jnp.where(me == 0, x_ref[...], g_ref[0])
        for k in range(1, 8):
            acc = acc + jnp.where(me == k, x_ref[...], g_ref[k])
        o_ref[...] = acc

    return pl.pallas_call(
        body, name=f"sum_small_grads_{tag}",
        grid_spec=pltpu.PrefetchScalarGridSpec(
            num_scalar_prefetch=1, grid=(r // tr,),
            in_specs=[BS((8, tr, cols), lambda t, pr: (0, t, 0)), BS((tr, cols), lambda t, pr: (t, 0))],
            out_specs=BS((tr, cols), lambda t, pr: (t, 0))),
        out_shape=SDS((r, cols), F32))(place, gathered, mine)


def _adamw_values(w, g, m, v):
    m = ADAM_B1 * m + (1.0 - ADAM_B1) * g
    v = ADAM_B2 * v + (1.0 - ADAM_B2) * (g * g)
    m_hat = m / (1.0 - ADAM_B1 ** ADAM_STEP)
    v_hat = v / (1.0 - ADAM_B2 ** ADAM_STEP)
    delta = -ADAM_LR * (m_hat / (jnp.sqrt(v_hat) + ADAM_EPS) + ADAM_WD * w)
    return delta, m, v


def _adamw_big(name, l, w, m, v, g, earlier=None, after=()):
    nl, r, cols = w.shape
    tr = _row_tile(r, cols, budget=1024 * 1024)
    nt = r // tr
    n_prev = 0 if earlier is None else 4

    def body(*refs):
        w_ref, m_ref, v_ref, g_ref = refs[:4]
        go_ref, d_ref, mo_ref, vo_ref = refs[4 + n_prev + len(after):]
        gv = g_ref[0].astype(F32) + g_ref[1].astype(F32)
        delta, m_new, v_new = _adamw_values(w_ref[...], gv, m_ref[...], v_ref[...])
        go_ref[...] = gv
        d_ref[...] = delta
        mo_ref[...] = m_new
        vo_ref[...] = v_new

    layer = BS((None, tr, cols), lambda t: (l, t, 0))
    return pl.pallas_call(
        body, name=f"adamw_{name}_l{l}", grid=(nt,),
        in_specs=[layer, layer, layer, BS((2, tr, cols), lambda t: (0, t, 0))] + [ANY] * (n_prev + len(after)),
        out_specs=[layer] * 4, out_shape=[SDS(w.shape, F32)] * 4,
        input_output_aliases={4 + i: i for i in range(n_prev)}, compiler_params=_params(),
    )(w, m, v, g, *(earlier or ()), *after)


def _adamw_small_group(tag, l, ws, ms, vs, gs, earlier, after=()):
    n = len(ws)
    steps = ADAMW_GROUP_STEPS
    prev = [a for e in earlier if e is not None for a in e]
    n_prev = len(prev)
    assert n_prev in (0, 4 * n)

    def body(*refs):
        w_refs, m_refs, v_refs, g_refs = refs[:n], refs[n:2 * n], refs[2 * n:3 * n], refs[3 * n:4 * n]
        outs = refs[4 * n + n_prev + len(after):]
        for i in range(n):
            gv = g_refs[i][0].astype(F32) + g_refs[i][1].astype(F32)
            delta, m_new, v_new = _adamw_values(w_refs[i][...], gv, m_refs[i][...], v_refs[i][...])
            for ref, val in zip(outs[4 * i:4 * i + 4], (gv, delta, m_new, v_new)):
                ref[...] = val

    def layer(w):
        return BS((None, w.shape[1] // steps, w.shape[2]), lambda t: (l, t, 0))

    return pl.pallas_call(
        body, name=f"adamw_{tag}_l{l}", grid=(steps,),
        in_specs=[layer(w) for w in ws] * 3
        + [BS((2, w.shape[1] // steps, w.shape[2]), lambda t: (0, t, 0)) for w in ws] + [ANY] * (n_prev + len(after)),
        out_specs=[layer(w) for w in ws for _ in range(4)],
        out_shape=[SDS(w.shape, F32) for w in ws for _ in range(4)],
        input_output_aliases={4 * n + i: i for i in range(n_prev)}, compiler_params=_params(),
    )(*ws, *ms, *vs, *gs, *prev, *after)


def _adamw_mid(ws, ms, vs, gathered, mine, place):
    n = len(ws)
    shape = ws[0].shape[1:]
    zeros = (0,) * len(shape)

    def body(place_ref, *refs):
        w_refs, m_refs, v_refs = refs[:n], refs[n:2 * n], refs[2 * n:3 * n]
        gath, own = refs[3 * n:(3 + N_LAYERS) * n], refs[(3 + N_LAYERS) * n:(3 + 2 * N_LAYERS) * n]
        outs = refs[(3 + 2 * N_LAYERS) * n:]
        me = 2 * place_ref[0] + place_ref[1]
        for i in range(n):
            gv = None
            for l in range(N_LAYERS):
                g_ref, x_ref = gath[l * n + i], own[l * n + i]
                acc = jnp.where(me == 0, x_ref[...], g_ref[0])
                for k in range(1, 8):
                    acc = acc + jnp.where(me == k, x_ref[...], g_ref[k])
                gv = acc if gv is None else jnp.where(pl.program_id(0) == l, acc, gv)
            delta, m_new, v_new = _adamw_values(w_refs[i][...], gv, m_refs[i][...], v_refs[i][...])
            for ref, val in zip(outs[4 * i:4 * i + 4], (gv, delta, m_new, v_new)):
                ref[...] = val

    layer = BS((None,) + shape, lambda l, pr: (l,) + zeros)
    kept = pl.Buffered(1)
    outs = pl.pallas_call(
        body, name="adamw_replicated_matrices",
        grid_spec=pltpu.PrefetchScalarGridSpec(
            num_scalar_prefetch=1, grid=(N_LAYERS,),
            in_specs=[layer] * (3 * n)
            + [BS((8,) + shape, lambda l, pr: (0,) + zeros, pipeline_mode=kept)] * (N_LAYERS * n)
            + [BS(shape, lambda l, pr: zeros, pipeline_mode=kept)] * (N_LAYERS * n),
            out_specs=[layer] * (4 * n)),
        out_shape=[SDS(ws[0].shape, F32)] * (4 * n), compiler_params=_params(),
    )(place, *ws, *ms, *vs, *[g for l in range(N_LAYERS) for g in gathered[l]],
      *[x for l in range(N_LAYERS) for x in mine[l]])
    return [tuple(outs[4 * i:4 * i + 4]) for i in range(n)]


def _adamw_rows(w, m, v, g):
    r, cols = w.shape
    tr = _row_tile(r, cols, budget=512 * 1024)

    def body(w_ref, m_ref, v_ref, g_ref, d_ref, mo_ref, vo_ref):
        delta, m_new, v_new = _adamw_values(w_ref[...], g_ref[...], m_ref[...], v_ref[...])
        d_ref[...] = delta
        mo_ref[...] = m_new
        vo_ref[...] = v_new

    spec = BS((tr, cols), lambda t: (t, 0))
    return pl.pallas_call(body, name="adamw_small", grid=(r // tr,), in_specs=[spec] * 4, out_specs=[spec] * 3,
                          out_shape=[SDS(w.shape, F32)] * 3)(w, m, v, g)


SMALL_GRAD_ELEMS = 256 * 1024
ADAMW_GROUP_STEPS = 4
PACK_ALIGN = 8 * 128
PACK_ROWS = 128


def _pack_rows(arrays):
    parts, rows = [], 0
    for a in arrays:
        flat = a.reshape(-1)
        pad = (-flat.shape[0]) % PACK_ALIGN
        if pad:
            flat = jnp.pad(flat, (0, pad))
        parts.append(flat.reshape(-1, 128))
        rows += parts[-1].shape[0]
    if rows % PACK_ROWS:
        parts.append(jnp.zeros((PACK_ROWS - rows % PACK_ROWS, 128), parts[0].dtype))
    return jnp.concatenate(parts, axis=0)


def _unpack_rows(buf, shapes):
    out, row = [], 0
    for shape in shapes:
        size = math.prod(shape)
        rows = -(-size // PACK_ALIGN) * (PACK_ALIGN // 128)
        out.append(buf[row:row + rows].reshape(-1)[:size].reshape(shape))
        row += rows
    return out


def kernel(x, norm1, w_in, b_gate, ssm_a_re, ssm_a_im, ssm_log_dt, ssm_b_re, ssm_b_im, ssm_c_re, ssm_c_im, ssm_d, ssm_w_glu, ssm_b_glu, ssm_w_proj, conv_w_dw, conv_b_dw, conv_ln_g, conv_ln_b, conv_w_proj, pool_w_group, pool_scale, pool_w_proj, w_out, norm2, ffn_w_gate, ffn_w_up, ffn_w_down, final_norm, loss_target, m_norm1, m_w_in, m_b_gate, m_ssm_a_re, m_ssm_a_im, m_ssm_log_dt, m_ssm_b_re, m_ssm_b_im, m_ssm_c_re, m_ssm_c_im, m_ssm_d, m_ssm_w_glu, m_ssm_b_glu, m_ssm_w_proj, m_conv_w_dw, m_conv_b_dw, m_conv_ln_g, m_conv_ln_b, m_conv_w_proj, m_pool_w_group, m_pool_scale, m_pool_w_proj, m_w_out, m_norm2, m_ffn_w_gate, m_ffn_w_up, m_ffn_w_down, m_final_norm, v_norm1, v_w_in, v_b_gate, v_ssm_a_re, v_ssm_a_im, v_ssm_log_dt, v_ssm_b_re, v_ssm_b_im, v_ssm_c_re, v_ssm_c_im, v_ssm_d, v_ssm_w_glu, v_ssm_b_glu, v_ssm_w_proj, v_conv_w_dw, v_conv_b_dw, v_conv_ln_g, v_conv_ln_b, v_conv_w_proj, v_pool_w_group, v_pool_scale, v_pool_w_proj, v_w_out, v_norm2, v_ffn_w_gate, v_ffn_w_up, v_ffn_w_down, v_final_norm):
    given = dict(locals())
    cx, cy, cc = _place()
    place = jnp.stack([2 * cx + cy, cc]).astype(jnp.int32)

    def kernel_view(n, a):
        if n in TRANSPOSED:
            return a.transpose(0, 2, 1)
        return a.transpose(0, 1, 3, 2) if n in ("ssm_b_re", "ssm_b_im") else a

    prm = {n: given[n] for n in WEIGHTS}
    mom = {n: given["m_" + n] for n in WEIGHTS}
    var = {n: given["v_" + n] for n in WEIGHTS}
    for n in MID:
        prm[n], mom[n], var[n] = kernel_view(n, prm[n]), kernel_view(n, mom[n]), kernel_view(n, var[n])

    dw_shard = prm["conv_w_dw"].reshape(N_LAYERS, CONV_KERNEL, -1)
    casts = {"w_in": _cast_into("w_in", prm["w_in"], place, MXU_DTYPE)}
    first, first_started = _allgather_start("first", [[casts["w_in"][0]]])
    in_flight = {(0, "in"): first[0]}
    mixer = GATHER_GROUPS["mixer"]
    casts.update(zip(mixer, _cast_small_into(
        "mixer", [dw_shard if n == "conv_w_dw" else prm[n] for n in mixer],
        [F32 if n == "conv_w_dw" else MXU_DTYPE for n in mixer], place, after=(first_started,))))
    casts.update({n: _cast_into(n, kernel_view(n, prm[n]), place, MXU_DTYPE, after=(first_started,))
                  for n in GATHER_GROUPS["ffn"]})
    order = [(l, g) for l in range(N_LAYERS) for g in GATHER_GROUPS if (l, g) != (0, "in")]
    rest, rest_started = _allgather_start("rest", [[casts[n][l] for n in GATHER_GROUPS[g]] for l, g in order])
    in_flight.update(zip(order, rest))

    arrived = {}

    def weights_of(l, group, after):
        if (l, group) in arrived:
            return arrived.pop((l, group))
        tag = f"l{l}_{group}"
        if (l, group) == (0, "in"):
            after = after + (rest_started,)
        groups = (group, "mixer") if (l > 0 and group == "in") else (group,)
        waited = [_allgather_wait(f"l{l}_{g}", *in_flight[l, g][:2], in_flight[l, g][2], after) for g in groups]
        bufs = _allgather_forward(tag, [b for w in waited for b in w])
        for g in groups:
            fw = dict(zip(GATHER_GROUPS[g], bufs[:len(GATHER_GROUPS[g])]))
            bufs = bufs[len(GATHER_GROUPS[g]):]
            if "conv_w_dw" in fw:
                fw["conv_w_dw"] = fw["conv_w_dw"].transpose(1, 0, 2).reshape(CONV_KERNEL, -1)
            arrived[l, g] = fw
        return arrived.pop((l, group))

    pending, small_pending, small_shapes = {}, {}, {}
    tokens = {}

    def on_grads(l, group, grads):
        if group == "small":
            packed = {n: g for n, g in grads.items() if n not in MID}
            small_shapes[l] = {n: g.shape for n, g in packed.items()}
            begun = _allgather_rows_start(f"l{l}", [_pack_rows(list(packed.values()))] + [grads[n] for n in MID])
            small_pending[l], token = begun[:4], begun[4]
        else:
            pending[l, group], token = _reduce_start(f"{l}_{group}", grads)
        tokens[l, group] = token
        return token[0, 0]

    loss, dx, _, _, _ = _local_step(x[0], loss_target[0], weights_of, prm, place, on_grads)
    loss = lax.psum(loss, ("x", "y", "c"))

    reduced = [{} for _ in range(N_LAYERS)]
    out = {}

    def finish(l, groups, after):
        reduced[l].update(_reduce_finish(f"l{l}_{groups[0]}", [pending[l, g] for g in groups], place, after))

    def adamw(l, names, done):
        small = [n for n in names if prm[n][0].size <= SMALL_GRAD_ELEMS]
        for n in names:
            if n not in small:
                out[n] = _adamw_big(n, l, kernel_view(n, prm[n]), kernel_view(n, mom[n]), kernel_view(n, var[n]),
                                    reduced[l][n], out.get(n), after=done)
                done = (out[n][0],)
        if small:
            res = _adamw_small_group("mixer", l, [prm[n] for n in small], [mom[n] for n in small],
                                     [var[n] for n in small], [reduced[l][n] for n in small],
                                     [out.get(n) for n in small], after=done)
            for i, n in enumerate(small):
                out[n] = tuple(res[4 * i:4 * i + 4])
            done = (res[0],)
        return done

    top = N_LAYERS - 1
    done = (tokens[0, "in"], tokens[0, "small"])
    finish(top, ("ffn", "mixer", "in"), done)
    done = adamw(top, BIG, done)
    for groups in (("ffn", "mixer"), ("in",)):
        finish(0, groups, done)
        done = adamw(0, [n for g in groups for n in GATHER_GROUPS[g] if n in BIG], done)
    for n in BIG:
        out[n] = tuple(kernel_view(n, a) for a in out[n])

    gsmall = {}
    mid_mine, mid_gathered = [], []
    for l in range(N_LAYERS):
        mine, lands = _allgather_rows_wait(f"l{l}", *small_pending[l], done[0])
        lands = _allgather_rows_forward(f"l{l}", lands)
        mid_mine.append(mine[1:])
        mid_gathered.append(lands[1:])
        gsum = _sum_devices(f"l{l}", lands[0], mine[0], place)
        for n, g in zip(small_shapes[l], _unpack_rows(gsum, list(small_shapes[l].values()))):
            gsmall.setdefault(n, [None] * N_LAYERS)[l] = g
    mid_out = _adamw_mid([prm[n] for n in MID], [mom[n] for n in MID], [var[n] for n in MID], mid_gathered, mid_mine,
                         place)
    for n, res in zip(MID, mid_out):
        out[n] = tuple(kernel_view(n, a) for a in res)
    gsmall = {n: (g[top] if n == "final_norm" else jnp.stack(g)) for n, g in gsmall.items()}
    lanes = dw_shard.shape[-1]
    gsmall["conv_w_dw"] = lax.dynamic_slice_in_dim(gsmall["conv_w_dw"], (2 * cx + cy) * lanes, lanes, axis=2)
    small_names = [n for n in SMALL if n not in MID] + ["conv_w_dw"]
    w_rows = _pack_rows([prm[n] for n in small_names])
    m_rows = _pack_rows([mom[n] for n in small_names])
    v_rows = _pack_rows([var[n] for n in small_names])
    g_rows = _pack_rows([gsmall[n] for n in small_names])
    shapes = [prm[n].shape for n in small_names]
    d_s, m_s, v_s = (_unpack_rows(r, shapes) for r in _adamw_rows(w_rows, m_rows, v_rows, g_rows))
    for i, n in enumerate(small_names):
        out[n] = (gsmall[n].reshape(prm[n].shape), d_s[i], m_s[i], v_s[i])
    grads = [out[n][0] for n in WEIGHTS]
    deltas = [out[n][1] for n in WEIGHTS]
    new_m = [out[n][2] for n in WEIGHTS]
    new_v = [out[n][3] for n in WEIGHTS]
    return (loss, dx[None], *grads, *deltas, *new_m, *new_v)
```

```python
import math

import jax
import jax.numpy as jnp
from jax import lax
from jax.experimental import pallas as pl
from jax.experimental.pallas import tpu as pltpu

F32 = jnp.float32
MXU_DTYPE = jnp.bfloat16
WIRE_DTYPE = jnp.bfloat16
SDS = jax.ShapeDtypeStruct
BS = pl.BlockSpec
ANY = pl.BlockSpec(memory_space=pl.ANY)
HBM = pl.BlockSpec(memory_space=pltpu.HBM)
SEM = pl.BlockSpec(memory_space=pltpu.SEMAPHORE)
SIDE_EFFECT = pltpu.SideEffectType.DATAFLOW_SIDE_EFFECTING
MESH = pl.DeviceIdType.MESH

EPS = 1e-6
N_CHIPS = 4
N_LAYERS = 2
SSM_GROUPS, SSM_STATE, SSM_GROUP = 32, 64, 16
CONV_KERNEL = 31
CONV_PAD = 32
POOL_WINDOWS = (2, 4, 8, 16)
GELU_C = math.sqrt(2.0 / math.pi)
ADAM_LR, ADAM_B1, ADAM_B2, ADAM_EPS, ADAM_WD, ADAM_STEP = 0.001, 0.9, 0.999, 1e-08, 0.01, 10
VMEM_LIMIT = 56 * 1024 * 1024

BIG = ("w_in", "ssm_w_glu", "ssm_w_proj", "conv_w_proj", "pool_w_proj", "w_out", "ffn_w_gate", "ffn_w_up", "ffn_w_down")
TRANSPOSED = ("ffn_w_gate", "ffn_w_up")
MID = ("ssm_b_re", "ssm_b_im", "ssm_c_re", "ssm_c_im")
GATHER_GROUPS = {
    "in": ("w_in",),
    "mixer": ("ssm_w_glu", "ssm_w_proj", "conv_w_proj", "pool_w_proj", "w_out", "conv_w_dw"),
    "ffn": ("ffn_w_gate", "ffn_w_up", "ffn_w_down"),
}
SMALL = ("norm1", "b_gate", "ssm_a_re", "ssm_a_im", "ssm_log_dt", "ssm_b_re", "ssm_b_im", "ssm_c_re", "ssm_c_im",
         "ssm_d", "ssm_b_glu", "conv_b_dw", "conv_ln_g", "conv_ln_b", "pool_w_group", "pool_scale", "norm2",
         "final_norm")
WEIGHTS = ("norm1", "w_in", "b_gate", "ssm_a_re", "ssm_a_im", "ssm_log_dt", "ssm_b_re", "ssm_b_im", "ssm_c_re",
           "ssm_c_im", "ssm_d", "ssm_w_glu", "ssm_b_glu", "ssm_w_proj", "conv_w_dw", "conv_b_dw", "conv_ln_g",
           "conv_ln_b", "conv_w_proj", "pool_w_group", "pool_scale", "pool_w_proj", "w_out", "norm2", "ffn_w_gate",
           "ffn_w_up", "ffn_w_down", "final_norm")


def _params():
    return pltpu.CompilerParams(vmem_limit_bytes=VMEM_LIMIT)


def _mm(a, b):
    return jnp.dot(a.astype(MXU_DTYPE), b.astype(MXU_DTYPE), preferred_element_type=F32)


def _mm_nt(a, b):
    return lax.dot_general(a.astype(MXU_DTYPE), b.astype(MXU_DTYPE), (((1,), (1,)), ((), ())),
                           preferred_element_type=F32)


def _mm_tn(a, b):
    return lax.dot_general(a.astype(MXU_DTYPE), b.astype(MXU_DTYPE), (((0,), (0,)), ((), ())),
                           preferred_element_type=F32)


def _sigmoid(x):
    return jax.nn.sigmoid(x)


def _gelu(x):
    t = jnp.tanh(GELU_C * (x + 0.044715 * (x * x * x)))
    return x * (0.5 * (1.0 + t)), t


def _gelu_grad(x, t):
    return 0.5 * (1.0 + t) + 0.5 * x * (1.0 - t * t) * (GELU_C * (1.0 + 3.0 * 0.044715 * x * x))


def _colsum(v):
    return jnp.sum(v, axis=0, keepdims=True)


def _row_tile(rows, cols, itemsize=4, budget=1536 * 1024):
    best = None
    for t in range(8, rows + 1, 8):
        if rows % t == 0 and t * cols * itemsize <= budget:
            best = t
    return best if best is not None else rows


def _in_proj(l, x, norm1, w_in):
    s, d = x.shape
    nc = w_in.shape[-1]
    tm = min(1024, s)
    nt = s // tm

    def body(x_ref, g_ref, w_ref, z_ref, h_ref, h_all):
        i = pl.program_id(1)
        rows = pl.ds(pl.multiple_of(i * tm, tm), tm)

        @pl.when(pl.program_id(0) == 0)
        def _():
            xv = x_ref[...]
            r = lax.rsqrt(jnp.mean(xv * xv, axis=-1, keepdims=True) + EPS)
            hv = (xv * r * g_ref[...]).astype(h_ref.dtype)
            h_ref[...] = hv.T
            h_all[rows, :] = hv

        z_ref[...] = _mm(h_all[rows, :], w_ref[...])

    tile_of = lambda j, i: i * (1 - jnp.minimum(j, 1)) + (nt - 1) * jnp.minimum(j, 1)
    return pl.pallas_call(
        body, name=f"in_proj_l{l}", grid=(N_CHIPS, nt),
        in_specs=[BS((tm, d), lambda j, i: (tile_of(j, i), 0)), BS((None, 1, d), lambda j, i: (l, 0, 0)),
                  BS((None, d, nc), lambda j, i: (j, 0, 0))],
        out_specs=[BS((tm, nc), lambda j, i: (i, j)), BS((d, tm), lambda j, i: (0, tile_of(j, i)))],
        out_shape=[SDS((s, N_CHIPS * nc), F32), SDS((d, s), MXU_DTYPE)],
        scratch_shapes=[pltpu.VMEM((s, d), MXU_DTYPE)], compiler_params=_params())(x, norm1, w_in)


def _mm_cols(a, w_ref):
    return jnp.concatenate([_mm(a, w_ref[j]) for j in range(N_CHIPS)], axis=1)


def _mm_nt_cols(dv, w_ref):
    nc = w_ref.shape[-1]
    acc = _mm_nt(dv[:, 0:nc], w_ref[0])
    for j in range(1, N_CHIPS):
        acc = acc + _mm_nt(dv[:, j * nc:(j + 1) * nc], w_ref[j])
    return acc


def _merge_values(y, hc, p, zg, wglu, bglu, wpa, wpb, wpc, lng, lnb, wgrp, scale, bg):
    v = {}
    ge, th = _gelu(y)
    t = _mm(ge, wglu) + bglu
    sg = _sigmoid(t)
    sa = ge * sg
    ya = _mm_cols(sa, wpa)
    mu = jnp.mean(hc, axis=-1, keepdims=True)
    xc = hc - mu
    r = lax.rsqrt(jnp.mean(xc * xc, axis=-1, keepdims=True) + EPS)
    xh = xc * r
    ln = xh * lng + lnb
    sl = _sigmoid(ln)
    ac = ln * sl
    yb = _mm_cols(ac, wpb)
    gw = p.shape[1] // len(POOL_WINDOWS)
    q = jnp.concatenate([_mm(p[:, k * gw:(k + 1) * gw], wgrp[k]) for k in range(len(POOL_WINDOWS))], axis=1)
    pp = q * scale
    yc = _mm_cols(pp, wpc)
    d = ya.shape[1]
    gates = [_sigmoid(zg[k] + bg[:, k * d:(k + 1) * d]) for k in range(3)]
    merged = gates[0] * ya + gates[1] * yb + gates[2] * yc
    v.update(ge=ge, th=th, sg=sg, sa=sa, ya=ya, r=r, xh=xh, ln=ln, sl=sl, ac=ac, yb=yb, q=q, pp=pp, yc=yc,
             gates=gates, merged=merged)
    return v


def _merge_specs(l, tm, d, cw):
    row = lambda n: BS((None, 1, n), lambda i: (l, 0, 0))
    resident = lambda shp: BS(shp, lambda i: (0, 0, 0), pipeline_mode=pl.Buffered(1))
    return [
        BS((tm, cw), lambda i: (i, 0)),
        BS((tm, cw), lambda i: (i, 0)),
        BS((tm, cw), lambda i: (i, 0)),
        BS((tm, d), lambda i: (i, 2)), BS((tm, d), lambda i: (i, 3)), BS((tm, d), lambda i: (i, 4)),
        resident((N_CHIPS, cw // N_CHIPS, cw)),
        row(cw),
        resident((N_CHIPS, cw, d // N_CHIPS)),
        resident((N_CHIPS, cw, d // N_CHIPS)),
        resident((N_CHIPS, cw, d // N_CHIPS)),
        row(cw), row(cw),
        BS((None, 4, cw // 4, cw // 4), lambda i: (l, 0, 0, 0)),
        row(cw),
        row(3 * d),
        resident((N_CHIPS, d // N_CHIPS, d)),
    ]


def _merge_fwd(l, x, y, hc, p, z, fw, sp):
    s, d = x.shape
    cw = y.shape[1]
    tm = min(512, s)

    def body(x_ref, y_ref, hc_ref, p_ref, z0, z1, z2, wglu, bglu, wpa, wpb, wpc, lng, lnb, wgrp, scale, bg, wout,
             x1_ref):
        v = _merge_values(y_ref[...], hc_ref[...], p_ref[...], (z0[...], z1[...], z2[...]),
                          wglu[...].reshape(cw, cw), bglu[...], wpa, wpb, wpc, lng[...], lnb[...], wgrp, scale[...],
                          bg[...])
        x1_ref[...] = x_ref[...] + _mm(v["merged"], wout[...].reshape(d, d))

    return pl.pallas_call(
        body, name=f"merge_fwd_l{l}", grid=(s // tm,),
        in_specs=[BS((tm, d), lambda i: (i, 0))] + _merge_specs(l, tm, d, cw),
        out_specs=BS((tm, d), lambda i: (i, 0)), out_shape=SDS((s, d), F32), compiler_params=_params(),
    )(x, y, hc, p, z, z, z, fw["ssm_w_glu"], sp["ssm_b_glu"], fw["ssm_w_proj"], fw["conv_w_proj"], fw["pool_w_proj"],
      sp["conv_ln_g"], sp["conv_ln_b"], sp["pool_w_group"], sp["pool_scale"], sp["b_gate"], fw["w_out"])


def _merge_bwd(l, dx1, y, hc, p, z, fw, sp):
    s, d = dx1.shape
    cw = y.shape[1]
    tm = min(256, s)
    m = MXU_DTYPE

    def body(dx1_ref, y_ref, hc_ref, p_ref, z0, z1, z2, wglu, bglu, wpa, wpb, wpc, lng, lnb, wgrp, scale, bg, wout,
             dzg_ref, dy_ref, dhc_ref, dp_ref, merged_ref, sa_ref, ac_ref, pp_ref, ge_ref, dt_ref, dya_ref, dyb_ref,
             dyc_ref, dq_ref, dbg_ref, dbglu_ref, dlng_ref, dlnb_ref, dscale_ref):
        yv = y_ref[...]
        wg = wglu[...].reshape(cw, cw)
        v = _merge_values(yv, hc_ref[...], p_ref[...], (z0[...], z1[...], z2[...]), wg, bglu[...], wpa, wpb, wpc,
                          lng[...], lnb[...], wgrp, scale[...], bg[...])
        dm = _mm_nt(dx1_ref[...], wout[...].reshape(d, d))
        ys = (v["ya"], v["yb"], v["yc"])
        dys, dbg = [], []
        for k in range(3):
            gk = v["gates"][k]
            dzk = dm * ys[k] * (gk * (1.0 - gk))
            dbg.append(_colsum(dzk))
            dzg_ref[:, k * d:(k + 1) * d] = dzk.astype(m)
            dys.append((dm * gk).astype(m))
        dsa = _mm_nt_cols(dys[0], wpa)
        dac = _mm_nt_cols(dys[1], wpb)
        dpp = _mm_nt_cols(dys[2], wpc)
        ge, sg = v["ge"], v["sg"]
        dt = dsa * ge * (sg * (1.0 - sg))
        dge = dsa * sg + _mm_nt(dt, wg)
        dy_ref[...] = dge * _gelu_grad(yv, v["th"])
        ln, sl, xh = v["ln"], v["sl"], v["xh"]
        dln = dac * (sl * (1.0 + ln * (1.0 - sl)))
        dxh = dln * lng[...]
        dhc_ref[...] = v["r"] * (dxh - jnp.mean(dxh, axis=-1, keepdims=True)
                                 - xh * jnp.mean(dxh * xh, axis=-1, keepdims=True))
        dq = dpp * scale[...]
        gw = cw // len(POOL_WINDOWS)
        for k in range(len(POOL_WINDOWS)):
            dp_ref[:, k * gw:(k + 1) * gw] = _mm_nt(dq[:, k * gw:(k + 1) * gw], wgrp[k])
        merged_ref[...] = v["merged"].astype(m)
        sa_ref[...] = v["sa"].astype(m)
        ac_ref[...] = v["ac"].astype(m)
        pp_ref[...] = v["pp"].astype(m)
        ge_ref[...] = ge.astype(m)
        dt_ref[...] = dt.astype(m)
        dya_ref[...] = dys[0]
        dyb_ref[...] = dys[1]
        dyc_ref[...] = dys[2]
        dq_ref[...] = dq.astype(m)

        @pl.when(pl.program_id(0) == 0)
        def _():
            for ref in (dbg_ref, dbglu_ref, dlng_ref, dlnb_ref, dscale_ref):
                ref[...] = jnp.zeros(ref.shape, F32)

        dbg_ref[...] += jnp.concatenate(dbg, axis=1)
        dbglu_ref[...] += _colsum(dt)
        dlng_ref[...] += _colsum(dln * xh)
        dlnb_ref[...] += _colsum(dln)
        dscale_ref[...] += _colsum(dpp * v["q"])

    tile = lambda n: BS((tm, n), lambda i: (i, 0))
    acc = lambda n: BS((1, n), lambda i: (0, 0))
    outs = pl.pallas_call(
        body, name=f"merge_bwd_l{l}", grid=(s // tm,),
        in_specs=[tile(d)] + _merge_specs(l, tm, d, cw),
        out_specs=[tile(3 * d), tile(cw), tile(cw), tile(cw), tile(d), tile(cw), tile(cw), tile(cw), tile(cw), tile(cw),
                   tile(d), tile(d), tile(d), tile(cw), acc(3 * d), acc(cw), acc(cw), acc(cw), acc(cw)],
        out_shape=[SDS((s, 3 * d), m), SDS((s, cw), F32), SDS((s, cw), F32), SDS((s, cw), F32), SDS((s, d), m),
                   SDS((s, cw), m), SDS((s, cw), m), SDS((s, cw), m), SDS((s, cw), m), SDS((s, cw), m), SDS((s, d), m),
                   SDS((s, d), m), SDS((s, d), m), SDS((s, cw), m), SDS((1, 3 * d), F32), SDS((1, cw), F32),
                   SDS((1, cw), F32), SDS((1, cw), F32), SDS((1, cw), F32)],
        compiler_params=_params(),
    )(dx1, y, hc, p, z, z, z, fw["ssm_w_glu"], sp["ssm_b_glu"], fw["ssm_w_proj"], fw["conv_w_proj"], fw["pool_w_proj"],
      sp["conv_ln_g"], sp["conv_ln_b"], sp["pool_w_group"], sp["pool_scale"], sp["b_gate"], fw["w_out"])
    names = ("dzg", "dy", "dhc", "dp", "merged", "sa", "ac", "pp", "ge", "dt", "dya", "dyb", "dyc", "dq", "db_gate",
             "db_glu", "dln_g", "dln_b", "dscale")
    return dict(zip(names, outs))


def _ffn_fwd(l, x1, norm2, wg, wu, wd):
    s, d = x1.shape
    hc = wd.shape[1]
    tm = min(1024, s)

    def body(x_ref, g_ref, wg_ref, wu_ref, wd_ref, o_ref, gate_ref, up_ref, h_ref):
        @pl.when(pl.program_id(1) == 0)
        def _():
            xv = x_ref[...]
            r = lax.rsqrt(jnp.mean(xv * xv, axis=-1, keepdims=True) + EPS)
            h_ref[...] = (xv * r * g_ref[...]).astype(h_ref.dtype)
            o_ref[...] = xv

        h = h_ref[...]
        gate = _mm_nt(h, wg_ref[...])
        up = _mm_nt(h, wu_ref[...])
        gate_ref[...] = gate
        up_ref[...] = up
        o_ref[...] += _mm(gate * _sigmoid(gate) * up, wd_ref[...])

    chunk = BS((None, tm, hc), lambda i, j: (j, i, 0))
    return pl.pallas_call(
        body, name=f"ffn_fwd_l{l}", grid=(s // tm, N_CHIPS),
        in_specs=[BS((tm, d), lambda i, j: (i, 0)), BS((None, 1, d), lambda i, j: (l, 0, 0)),
                  BS((None, hc, d), lambda i, j: (j, 0, 0)), BS((None, hc, d), lambda i, j: (j, 0, 0)),
                  BS((None, hc, d), lambda i, j: (j, 0, 0))],
        out_specs=[BS((tm, d), lambda i, j: (i, 0)), chunk, chunk, BS((tm, d), lambda i, j: (i, 0))],
        out_shape=[SDS((s, d), F32), SDS((N_CHIPS, s, hc), F32), SDS((N_CHIPS, s, hc), F32), SDS((s, d), MXU_DTYPE)],
        compiler_params=_params())(x1, norm2, wg, wu, wd)


def _ffn_bwd(l, x1, dx2, gate_pre, up_pre, norm2, wg, wu, wd):
    s, d = x1.shape
    hc = wd.shape[1]
    tm = min(512, s)
    m = MXU_DTYPE
    last = N_CHIPS - 1

    def body(x_ref, dx2_ref, gate_ref, up_ref, g_ref, wg_ref, wu_ref, wd_ref, dx1_ref, dxb_ref, act_ref, dgate_ref,
             dup_ref, dn_ref, dh_scr):
        i, j = pl.program_id(0), pl.program_id(1)

        @pl.when(j == 0)
        def _():
            dxb_ref[...] = dx2_ref[...].astype(m)
            dh_scr[...] = jnp.zeros(dh_scr.shape, F32)

        @pl.when((i == 0) & (j == 0))
        def _():
            dn_ref[...] = jnp.zeros(dn_ref.shape, F32)

        gate = gate_ref[...]
        up = up_ref[...]
        sg = _sigmoid(gate)
        silu = gate * sg
        act_ref[...] = (silu * up).astype(m).T
        dact = _mm_nt(dxb_ref[...], wd_ref[...])
        dup = (dact * silu).astype(m)
        dgate = (dact * up * (sg * (1.0 + gate * (1.0 - sg)))).astype(m)
        dup_ref[...] = dup.T
        dgate_ref[...] = dgate.T
        dh_scr[...] += _mm(dgate, wg_ref[...]) + _mm(dup, wu_ref[...])

        @pl.when(j == last)
        def _():
            xv = x_ref[...]
            r = lax.rsqrt(jnp.mean(xv * xv, axis=-1, keepdims=True) + EPS)
            xh = xv * r
            dh = dh_scr[...]
            dn_ref[...] += _colsum(dh * xh)
            dxh = dh * g_ref[...]
            dx1_ref[...] = dx2_ref[...] + r * (dxh - xh * jnp.mean(dxh * xh, axis=-1, keepdims=True))

    chunk = BS((None, hc, tm), lambda i, j: (j, 0, i))
    saved = BS((None, tm, hc), lambda i, j: (j, i, 0))
    outs = pl.pallas_call(
        body, name=f"ffn_bwd_l{l}", grid=(s // tm, N_CHIPS),
        in_specs=[BS((tm, d), lambda i, j: (i, 0)), BS((tm, d), lambda i, j: (i, 0)), saved, saved,
                  BS((None, 1, d), lambda i, j: (l, 0, 0)),
                  BS((None, hc, d), lambda i, j: (j, 0, 0)), BS((None, hc, d), lambda i, j: (j, 0, 0)),
                  BS((None, hc, d), lambda i, j: (j, 0, 0))],
        out_specs=[BS((tm, d), lambda i, j: (i, 0)), BS((tm, d), lambda i, j: (i, 0)),
                   chunk, chunk, chunk, BS((1, d), lambda i, j: (0, 0))],
        out_shape=[SDS((s, d), F32), SDS((s, d), m), SDS((N_CHIPS, hc, s), m),
                   SDS((N_CHIPS, hc, s), m), SDS((N_CHIPS, hc, s), m), SDS((1, d), F32)],
        scratch_shapes=[pltpu.VMEM((tm, d), F32)], compiler_params=_params(),
    )(x1, dx2, gate_pre, up_pre, norm2, wg, wu, wd)
    return dict(zip(("dx1", "dx2", "act", "dgate", "dup", "dnorm2"), outs))


def _loss_head(x, target, gf):
    s, d = x.shape
    tm = min(512, s)

    def body(x_ref, t_ref, g_ref, dx_ref, loss_ref, dg_ref):
        @pl.when(pl.program_id(0) == 0)
        def _():
            loss_ref[...] = jnp.zeros(loss_ref.shape, F32)
            dg_ref[...] = jnp.zeros(dg_ref.shape, F32)

        xv = x_ref[...]
        r = lax.rsqrt(jnp.mean(xv * xv, axis=-1, keepdims=True) + EPS)
        xh = xv * r
        err = xh * g_ref[...] - t_ref[...]
        loss_ref[...] += 0.5 * jnp.sum(jnp.mean(err * err, axis=-1, keepdims=True), axis=0, keepdims=True)
        dyv = err * (1.0 / d)
        dg_ref[...] += _colsum(dyv * xh)
        dxh = dyv * g_ref[...]
        dx_ref[...] = r * (dxh - xh * jnp.mean(dxh * xh, axis=-1, keepdims=True))

    return pl.pallas_call(
        body, name="loss_head", grid=(s // tm,),
        in_specs=[BS((tm, d), lambda i: (i, 0)), BS((tm, d), lambda i: (i, 0)), BS((1, d), lambda i: (0, 0))],
        out_specs=[BS((tm, d), lambda i: (i, 0)), BS((1, 1), lambda i: (0, 0)), BS((1, d), lambda i: (0, 0))],
        out_shape=[SDS((s, d), F32), SDS((1, 1), F32), SDS((1, d), F32)], compiler_params=_params())(x, target, gf)


def _in_proj_bwd(l, dres, x, norm1, w_in, du_a, dv1, dv2, du_c, dzg):
    s, d = x.shape
    nc = w_in.shape[-1]
    tm = min(256, s)
    m = MXU_DTYPE

    def body(dres_ref, x_ref, g_ref, w_ref, a_ref, b1_ref, b2_ref, c_ref, g3_ref, dx_ref, dz_ref, dn_ref):
        @pl.when(pl.program_id(0) == 0)
        def _():
            dn_ref[...] = jnp.zeros(dn_ref.shape, F32)

        dz = jnp.concatenate([a_ref[...], b1_ref[...], b2_ref[...], c_ref[...], g3_ref[...]], axis=1).astype(m)
        dz_ref[...] = dz
        dh = _mm_nt_cols(dz, w_ref)
        xv = x_ref[...]
        r = lax.rsqrt(jnp.mean(xv * xv, axis=-1, keepdims=True) + EPS)
        xh = xv * r
        dn_ref[...] += _colsum(dh * xh)
        dxh = dh * g_ref[...]
        dx_ref[...] = dres_ref[...] + r * (dxh - xh * jnp.mean(dxh * xh, axis=-1, keepdims=True))

    tile = lambda n: BS((tm, n), lambda i: (i, 0))
    return pl.pallas_call(
        body, name=f"in_proj_bwd_l{l}", grid=(s // tm,),
        in_specs=[tile(d), tile(d), BS((None, 1, d), lambda i: (l, 0, 0)),
                  BS((N_CHIPS, d, nc), lambda i: (0, 0, 0), pipeline_mode=pl.Buffered(1)),
                  tile(du_a.shape[1]), tile(dv1.shape[1]), tile(dv2.shape[1]), tile(du_c.shape[1]), tile(dzg.shape[1])],
        out_specs=[tile(d), tile(N_CHIPS * nc), BS((1, d), lambda i: (0, 0))],
        out_shape=[SDS((s, d), F32), SDS((s, N_CHIPS * nc), m), SDS((1, d), F32)], compiler_params=_params(),
    )(dres, x, norm1, w_in, du_a, dv1, dv2, du_c, dzg)


def _tn_matmul(name, a, a_spec, b, b_spec, chunk_shape, grid, place):
    last = grid[1] - 1

    def body(place_ref, a_ref, b_ref, own_ref, wire_ref, *acc):
        part = _mm(a_ref[...], b_ref[...])

        def emit(total):
            wire_ref[...] = total.astype(WIRE_DTYPE)

            @pl.when(pl.program_id(0) == place_ref[0])
            def _():
                own_ref[...] = total

        if last == 0:
            emit(part)
        else:
            @pl.when(pl.program_id(1) == 0)
            def _():
                acc[0][...] = part

            @pl.when(pl.program_id(1) > 0)
            def _():
                acc[0][...] += part

            @pl.when(pl.program_id(1) == last)
            def _():
                emit(acc[0][...])

    zeros = (0,) * len(chunk_shape)
    return pl.pallas_call(
        body, name=name,
        grid_spec=pltpu.PrefetchScalarGridSpec(
            num_scalar_prefetch=1, grid=grid, in_specs=[a_spec, b_spec],
            out_specs=[BS(chunk_shape, lambda j, t, pr: zeros), BS((None,) + chunk_shape, lambda j, t, pr: (j,) + zeros)],
            scratch_shapes=[pltpu.VMEM(chunk_shape, F32)] if last else []),
        out_shape=[SDS(chunk_shape, F32), SDS((N_CHIPS,) + chunk_shape, WIRE_DTYPE)],
        compiler_params=_params())(place, a, b)


def _scan_consts(pw_ref, lanes, reverse):
    sgn = -1.0 if reverse else 1.0
    row = lax.broadcasted_iota(jnp.int32, (8, lanes), 0)
    steps = []
    for i, k in enumerate((1, 2, 4)):
        mask = (row < 8 - k) if reverse else (row >= k)
        steps.append((k, jnp.where(mask, pw_ref[2 * i], 0.0), jnp.where(mask, sgn * pw_ref[2 * i + 1], 0.0)))
    c = 4 if reverse else 3
    return steps, pw_ref[2 * c], sgn * pw_ref[2 * c + 1]


def _scan_block(br, bi, steps, reverse):
    for k, ar, ai in steps:
        sh = 8 - k if reverse else k
        sr = pltpu.roll(br, sh, 0)
        si = pltpu.roll(bi, sh, 0)
        br, bi = br + ar * sr - ai * si, bi + ar * si + ai * sr
    return br, bi


SSM_BLOCK_GROUPS = 8


def _ssm_fwd(l, z, b2_re, b2_im, c2_re, c2_im, pw, dskip):
    s = z.shape[0]
    gc = SSM_BLOCK_GROUPS * SSM_GROUP
    gl = SSM_BLOCK_GROUPS * SSM_STATE
    nblk = b2_re.shape[0] // gc
    tw = b2_re.shape[1]

    def body(u_ref, bre2, bim2, cre2, cim2, pw_ref, d_ref, hre, him, y_ref):
        u = u_ref[...]
        hre[...] = _mm(u, _block_matrix(bre2, gl))
        him[...] = _mm(u, _block_matrix(bim2, gl))
        steps, car, cai = _scan_consts(pw_ref, gl, False)

        def step(i, carry):
            cr, ci = carry
            r0 = pl.multiple_of(i * 8, 8)
            br, bi = _scan_block(hre[pl.ds(r0, 8), :], him[pl.ds(r0, 8), :], steps, False)
            hr = br + car * cr - cai * ci
            hi = bi + car * ci + cai * cr
            hre[pl.ds(r0, 8), :] = hr
            him[pl.ds(r0, 8), :] = hi
            return jnp.broadcast_to(hr[7:8, :], (8, gl)), jnp.broadcast_to(hi[7:8, :], (8, gl))

        zero = jnp.zeros((8, gl), F32)
        lax.fori_loop(0, s // 8, step, (zero, zero))
        y_ref[...] = (_mm_nt(hre[...], _block_matrix(cre2, gl)) - _mm_nt(him[...], _block_matrix(cim2, gl))
                      + d_ref[...] * u)

    twice = BS((gc, tw), lambda k: (k, 0))
    return pl.pallas_call(
        body, name=f"ssm_fwd_l{l}", grid=(nblk,),
        in_specs=[BS((s, gc), lambda k: (0, k)), twice, twice, twice, twice, BS((10, 8, gl), lambda k: (0, 0, k)),
                  BS((1, gc), lambda k: (0, k))],
        out_specs=[BS((s, gl), lambda k: (0, k)), BS((s, gl), lambda k: (0, k)), BS((s, gc), lambda k: (0, k))],
        out_shape=[SDS((s, nblk * gl), F32), SDS((s, nblk * gl), F32), SDS((s, nblk * gc), F32)],
        compiler_params=_params())(z, b2_re, b2_im, c2_re, c2_im, pw, dskip)


def _ssm_bwd(l, dy, z, hre, him, b2_re, b2_im, c2_re, c2_im, pw, dskip):
    s = z.shape[0]
    gc = SSM_BLOCK_GROUPS * SSM_GROUP
    gl = SSM_BLOCK_GROUPS * SSM_STATE
    nblk = b2_re.shape[0] // gc
    tw = b2_re.shape[1]

    def body(dy_ref, u_ref, hre_ref, him_ref, bre2, bim2, cre2, cim2, pw_ref, d_ref,
             du_ref, dbre_ref, dbim_ref, dcre_ref, dcim_ref, dar_ref, dai_ref, dd_ref, gre, gim):
        dyv = dy_ref[...]
        u = u_ref[...]
        gre[...] = _mm(dyv, _block_matrix(cre2, gl))
        gim[...] = -_mm(dyv, _block_matrix(cim2, gl))
        dcre_ref[...] = _block_diagonal_of(_mm_tn(dyv, hre_ref[...]))
        dcim_ref[...] = -_block_diagonal_of(_mm_tn(dyv, him_ref[...]))
        dd_ref[...] = _colsum(dyv * u)
        row = lax.broadcasted_iota(jnp.int32, (8, gl), 0)
        steps, car, cai = _scan_consts(pw_ref, gl, True)
        n8 = s // 8

        def step(ii, carry):
            cr, ci, accr, acci = carry
            i = n8 - 1 - ii
            r0 = pl.multiple_of(i * 8, 8)
            br, bi = _scan_block(gre[pl.ds(r0, 8), :], gim[pl.ds(r0, 8), :], steps, True)
            dr = br + car * cr - cai * ci
            di = bi + car * ci + cai * cr
            gre[pl.ds(r0, 8), :] = dr
            gim[pl.ds(r0, 8), :] = di
            rp = pl.multiple_of(jnp.maximum(i - 1, 0) * 8, 8)
            keep = jnp.where(i > 0, 1.0, 0.0)
            pr = jnp.where(row >= 1, pltpu.roll(hre_ref[pl.ds(r0, 8), :], 1, 0),
                           keep * pltpu.roll(hre_ref[pl.ds(rp, 8), :], 1, 0))
            pi = jnp.where(row >= 1, pltpu.roll(him_ref[pl.ds(r0, 8), :], 1, 0),
                           keep * pltpu.roll(him_ref[pl.ds(rp, 8), :], 1, 0))
            accr = accr + dr * pr + di * pi
            acci = acci + di * pr - dr * pi
            return (jnp.broadcast_to(dr[0:1, :], (8, gl)), jnp.broadcast_to(di[0:1, :], (8, gl)), accr, acci)

        zero = jnp.zeros((8, gl), F32)
        _, _, accr, acci = lax.fori_loop(0, n8, step, (zero, zero, zero, zero))
        dar_ref[...] = _colsum(accr)
        dai_ref[...] = _colsum(acci)
        dbr = gre[...]
        dbi = gim[...]
        du_ref[...] = (dyv * d_ref[...] + _mm_nt(dbr, _block_matrix(bre2, gl))
                       + _mm_nt(dbi, _block_matrix(bim2, gl))).astype(du_ref.dtype)
        dbre_ref[...] = _block_diagonal_of(_mm_tn(u, dbr))
        dbim_ref[...] = _block_diagonal_of(_mm_tn(u, dbi))

    col = lambda n: BS((s, n), lambda k: (0, k))
    twice = BS((gc, tw), lambda k: (k, 0))
    diag = BS((gc, SSM_STATE), lambda k: (k, 0))
    outs = pl.pallas_call(
        body, name=f"ssm_bwd_l{l}", grid=(nblk,),
        in_specs=[col(gc), col(gc), col(gl), col(gl), twice, twice, twice, twice,
                  BS((10, 8, gl), lambda k: (0, 0, k)), BS((1, gc), lambda k: (0, k))],
        out_specs=[col(gc), diag, diag, diag, diag, BS((1, gl), lambda k: (0, k)),
                   BS((1, gl), lambda k: (0, k)), BS((1, gc), lambda k: (0, k))],
        out_shape=[SDS((s, nblk * gc), MXU_DTYPE)] + [SDS((nblk * gc, SSM_STATE), F32)] * 4
        + [SDS((1, nblk * gl), F32), SDS((1, nblk * gl), F32), SDS((1, nblk * gc), F32)],
        scratch_shapes=[pltpu.VMEM((s, gl), F32), pltpu.VMEM((s, gl), F32)], compiler_params=_params(),
    )(dy, z, hre, him, b2_re, b2_im, c2_re, c2_im, pw, dskip)
    return dict(zip(("du", "dbbar_re", "dbbar_im", "dc_re", "dc_im", "dabar_re", "dabar_im", "dd"), outs))


def _conv_fwd(l, z, wdw, bdw):
    s = z.shape[0]
    cw = wdw.shape[1]
    lb = 128
    tr = min(256, s)
    off1 = cw // lb
    off2 = 2 * cw // lb

    def body(v1_ref, v2_ref, w_ref, b_ref, hc_ref, scr):
        scr[0:CONV_PAD, :] = jnp.zeros((CONV_PAD, lb), F32)
        scr[CONV_PAD:, :] = v1_ref[...] * _sigmoid(v2_ref[...])
        for t in range(s // tr):
            acc = jnp.broadcast_to(b_ref[...], (tr, lb))
            for k in range(CONV_KERNEL):
                acc = acc + w_ref[pl.ds(k, 1), :] * scr[pl.ds(t * tr + CONV_PAD - (CONV_KERNEL - 1) + k, tr), :]
            hc_ref[pl.ds(t * tr, tr), :] = acc

    return pl.pallas_call(
        body, name=f"conv_fwd_l{l}", grid=(cw // lb,),
        in_specs=[BS((s, lb), lambda k: (0, off1 + k)), BS((s, lb), lambda k: (0, off2 + k)),
                  BS((CONV_KERNEL, lb), lambda k: (0, k)), BS((1, lb), lambda k: (0, k))],
        out_specs=BS((s, lb), lambda k: (0, k)), out_shape=SDS((s, cw), F32),
        scratch_shapes=[pltpu.VMEM((s + CONV_PAD, lb), F32)], compiler_params=_params())(z, z, wdw, bdw)


def _conv_bwd(l, dhc, z, wdw):
    s = z.shape[0]
    cw = wdw.shape[1]
    lb = 128
    tr = min(256, s)
    off1 = cw // lb
    off2 = 2 * cw // lb
    nb = cw // lb

    def body(d_ref, v1_ref, v2_ref, w_ref, dv1_ref, dv2_ref, dw_ref, db_ref, hpad, dpad):
        v1 = v1_ref[...]
        sg = _sigmoid(v2_ref[...])
        dv = d_ref[...]
        hpad[0:CONV_PAD, :] = jnp.zeros((CONV_PAD, lb), F32)
        hpad[CONV_PAD:, :] = v1 * sg
        dpad[0:s, :] = dv
        dpad[s:, :] = jnp.zeros((CONV_PAD, lb), F32)
        db_ref[...] = _colsum(dv)
        dws = [jnp.zeros((1, lb), F32) for _ in range(CONV_KERNEL)]
        for t in range(s // tr):
            dt = d_ref[pl.ds(t * tr, tr), :]
            acc = jnp.zeros((tr, lb), F32)
            for k in range(CONV_KERNEL):
                acc = acc + w_ref[pl.ds(k, 1), :] * dpad[pl.ds(t * tr + (CONV_KERNEL - 1) - k, tr), :]
                dws[k] = dws[k] + _colsum(dt * hpad[pl.ds(t * tr + CONV_PAD - (CONV_KERNEL - 1) + k, tr), :])
            sgt = _sigmoid(v2_ref[pl.ds(t * tr, tr), :])
            v1t = v1_ref[pl.ds(t * tr, tr), :]
            dv1_ref[pl.ds(t * tr, tr), :] = (acc * sgt).astype(dv1_ref.dtype)
            dv2_ref[pl.ds(t * tr, tr), :] = (acc * v1t * (sgt * (1.0 - sgt))).astype(dv2_ref.dtype)
        for k in range(CONV_KERNEL):
            dw_ref[pl.ds(k, 1), :] = dws[k]

    return pl.pallas_call(
        body, name=f"conv_bwd_l{l}", grid=(nb,),
        in_specs=[BS((s, lb), lambda k: (0, k)), BS((s, lb), lambda k: (0, off1 + k)),
                  BS((s, lb), lambda k: (0, off2 + k)), BS((CONV_KERNEL, lb), lambda k: (0, k))],
        out_specs=[BS((s, lb), lambda k: (0, k)), BS((s, lb), lambda k: (0, k)),
                   BS((CONV_KERNEL, lb), lambda k: (0, k)), BS((1, lb), lambda k: (0, k))],
        out_shape=[SDS((s, cw), MXU_DTYPE), SDS((s, cw), MXU_DTYPE), SDS((CONV_KERNEL, cw), F32), SDS((1, cw), F32)],
        scratch_shapes=[pltpu.VMEM((s + CONV_PAD, lb), F32), pltpu.VMEM((s + CONV_PAD, lb), F32)],
        compiler_params=_params())(dhc, z, z, wdw)


def _pool_window(k):
    return jnp.where(k == 0, float(POOL_WINDOWS[0]),
                     jnp.where(k == 1, float(POOL_WINDOWS[1]),
                               jnp.where(k == 2, float(POOL_WINDOWS[2]), float(POOL_WINDOWS[3]))))


def _pool_fwd(l, z, pw_width):
    s = z.shape[0]
    lb = pw_width // len(POOL_WINDOWS)
    off = 3 * pw_width // lb

    def body(u_ref, p_ref):
        k = pl.program_id(0)
        u = u_ref[...]
        row = lax.broadcasted_iota(jnp.int32, (s, lb), 0)
        sums = [u]
        for sh in (1, 2, 4, 8):
            prev = sums[-1]
            sums.append(prev + jnp.where(row >= sh, pltpu.roll(prev, sh, 0), 0.0))
        sel = jnp.where(k == 0, sums[1], jnp.where(k == 1, sums[2], jnp.where(k == 2, sums[3], sums[4])))
        cnt = jnp.minimum((row + 1).astype(F32), _pool_window(k))
        p_ref[...] = sel / cnt - u

    return pl.pallas_call(
        body, name=f"pool_fwd_l{l}", grid=(len(POOL_WINDOWS),),
        in_specs=[BS((s, lb), lambda k: (0, off + k))], out_specs=BS((s, lb), lambda k: (0, k)),
        out_shape=SDS((s, pw_width), F32), compiler_params=_params())(z)


def _pool_bwd(l, dp):
    s, width = dp.shape
    lb = width // len(POOL_WINDOWS)

    def body(d_ref, du_ref):
        k = pl.program_id(0)
        dv = d_ref[...]
        row = lax.broadcasted_iota(jnp.int32, (s, lb), 0)
        cnt = jnp.minimum((row + 1).astype(F32), _pool_window(k))
        sums = [dv / cnt]
        for sh in (1, 2, 4, 8):
            prev = sums[-1]
            sums.append(prev + jnp.where(row < s - sh, pltpu.roll(prev, s - sh, 0), 0.0))
        sel = jnp.where(k == 0, sums[1], jnp.where(k == 1, sums[2], jnp.where(k == 2, sums[3], sums[4])))
        du_ref[...] = (sel - dv).astype(du_ref.dtype)

    return pl.pallas_call(
        body, name=f"pool_bwd_l{l}", grid=(len(POOL_WINDOWS),),
        in_specs=[BS((s, lb), lambda k: (0, k))], out_specs=BS((s, lb), lambda k: (0, k)),
        out_shape=SDS((s, width), MXU_DTYPE), compiler_params=_params())(dp)


def _zoh(a_re, a_im, log_dt):
    dt = jnp.exp(log_dt)
    mag = jnp.exp(dt * a_re)
    ang = dt * a_im
    abar_re = mag * jnp.cos(ang)
    abar_im = mag * jnp.sin(ang)
    den = a_re * a_re + a_im * a_im
    nr = abar_re - 1.0
    ni = abar_im
    f_re = (nr * a_re + ni * a_im) / den
    f_im = (ni * a_re - nr * a_im) / den
    return abar_re, abar_im, f_re, f_im


def _zoh_fwd(l, a_re, a_im, log_dt):
    def body(ar, ai, ld, o0, o1, o2, o3):
        for ref, val in zip((o0, o1, o2, o3), _zoh(ar[...], ai[...], ld[...])):
            ref[...] = val

    return pl.pallas_call(body, name=f"zoh_fwd_l{l}", out_shape=[SDS(a_re.shape, F32)] * 4)(a_re, a_im, log_dt)


def _zoh_bwd(l, a_re, a_im, log_dt, cts):
    def body(ar, ai, ld, c0, c1, c2, c3, dar, dai, dld):
        _, vjp = jax.vjp(_zoh, ar[...], ai[...], ld[...])
        g = vjp((c0[...], c1[...], c2[...], c3[...]))
        dar[...] = g[0]
        dai[...] = g[1]
        dld[...] = g[2]

    return pl.pallas_call(body, name=f"zoh_bwd_l{l}",
                          out_shape=[SDS(a_re.shape, F32), SDS(a_re.shape, F32), SDS(log_dt.shape, F32)],
                          )(a_re, a_im, log_dt, *cts)


def _bbar_fwd(l, f_re, f_im, b_re, b_im, c_re, c_im):
    g, p, n = b_re.shape[1:]
    m = MXU_DTYPE

    def body(fr, fi, br, bi, cr, ci, *outs):
        r = lax.broadcasted_iota(jnp.int32, (n, 2 * n), 0)
        c = lax.broadcasted_iota(jnp.int32, (n, 2 * n), 1)
        twice = jnp.where((c & (n - 1)) == r, 1.0, 0.0).astype(m)
        vals = (fr[...] * br[...] - fi[...] * bi[...], fr[...] * bi[...] + fi[...] * br[...], cr[...], ci[...])
        for o_ref, v in zip(outs, vals):
            o_ref[...] = _mm(v.astype(m).reshape(g * p, n), twice).astype(m)

    whole = lambda shp: BS(shp, lambda i: (0,) * len(shp))
    layer = BS((None, g, p, n), lambda i: (l, 0, 0, 0))
    return pl.pallas_call(body, name=f"bbar_fwd_l{l}", grid=(1,),
                          in_specs=[whole((g, 1, n)), whole((g, 1, n)), layer, layer, layer, layer],
                          out_specs=[whole((g * p, 2 * n))] * 4,
                          out_shape=[SDS((g * p, 2 * n), m)] * 4)(f_re, f_im, b_re, b_im, c_re, c_im)


def _block_mask(rows, lanes):
    r = lax.broadcasted_iota(jnp.int32, (rows, lanes), 0)
    c = lax.broadcasted_iota(jnp.int32, (rows, lanes), 1)
    return (r >> (SSM_GROUP.bit_length() - 1)) == (c >> (SSM_STATE.bit_length() - 1))


def _block_matrix(twice_ref, lanes):
    v = twice_ref[...]
    tiled = jnp.concatenate([v] * (lanes // v.shape[1]), axis=1)
    return jnp.where(_block_mask(v.shape[0], lanes), tiled, jnp.zeros_like(tiled))


def _block_diagonal_of(full):
    rows, lanes = full.shape
    kept = jnp.where(_block_mask(rows, lanes), full, 0.0)
    folded = kept[:, 0:128]
    for q in range(1, lanes // 128):
        folded = folded + kept[:, q * 128:(q + 1) * 128]
    return (folded + pltpu.roll(folded, SSM_STATE, 1))[:, :SSM_STATE]


def _bbar_bwd(l, f_re, f_im, b_re, b_im, d_re, d_im):
    g, p, n = b_re.shape[1:]

    def body(fr, fi, br, bi, dr, di, dfr, dfi, dbr, dbi):
        dfr[...] = jnp.sum(dr[...] * br[...] + di[...] * bi[...], axis=1, keepdims=True)
        dfi[...] = jnp.sum(di[...] * br[...] - dr[...] * bi[...], axis=1, keepdims=True)
        dbr[...] = fr[...] * dr[...] + fi[...] * di[...]
        dbi[...] = fr[...] * di[...] - fi[...] * dr[...]

    whole = lambda shp: BS(shp, lambda i: (0,) * len(shp))
    layer = BS((None, g, p, n), lambda i: (l, 0, 0, 0))
    return pl.pallas_call(body, name=f"bbar_bwd_l{l}", grid=(1,),
                          in_specs=[whole((g, 1, n)), whole((g, 1, n)), layer, layer, whole((g, p, n)),
                                    whole((g, p, n))],
                          out_specs=[whole((g, 1, n)), whole((g, 1, n)), whole((g, p, n)), whole((g, p, n))],
                          out_shape=[SDS((g, 1, n), F32), SDS((g, 1, n), F32), SDS((g, p, n), F32),
                                     SDS((g, p, n), F32)])(f_re, f_im, b_re, b_im, d_re, d_im)


def _powers(l, abar_re, abar_im):
    lanes = abar_re.shape[1]

    def body(ar_ref, ai_ref, o_ref):
        ar, ai = ar_ref[...], ai_ref[...]
        pows = [(ar, ai)]
        for _ in range(7):
            pr, pi = pows[-1]
            pows.append((pr * ar - pi * ai, pr * ai + pi * ar))
        row = lax.broadcasted_iota(jnp.int32, (8, lanes), 0)
        for i, k in enumerate((1, 2, 4)):
            o_ref[2 * i] = jnp.broadcast_to(pows[k - 1][0], (8, lanes))
            o_ref[2 * i + 1] = jnp.broadcast_to(pows[k - 1][1], (8, lanes))
        for slot, order in ((3, range(8)), (4, range(7, -1, -1))):
            vr = jnp.zeros((8, lanes), F32)
            vi = jnp.zeros((8, lanes), F32)
            for r, e in enumerate(order):
                vr = jnp.where(row == r, pows[e][0], vr)
                vi = jnp.where(row == r, pows[e][1], vi)
            o_ref[2 * slot] = vr
            o_ref[2 * slot + 1] = vi

    return pl.pallas_call(body, name=f"powers_l{l}", out_shape=SDS((10, 8, lanes), F32))(abar_re, abar_im)


def _ssm_prepare(l, prm):
    g, n, p = SSM_GROUPS, SSM_STATE, SSM_GROUP
    a_re, a_im = prm["ssm_a_re"][l], prm["ssm_a_im"][l]
    log_dt = prm["ssm_log_dt"][l].reshape(g, 1)
    abar_re, abar_im, f_re, f_im = _zoh_fwd(l, a_re, a_im, log_dt)
    f_re, f_im = f_re.reshape(g, 1, n), f_im.reshape(g, 1, n)
    b2_re, b2_im, c2_re, c2_im = _bbar_fwd(l, f_re, f_im, prm["ssm_b_re"], prm["ssm_b_im"], prm["ssm_c_re"],
                                           prm["ssm_c_im"])
    pw = _powers(l, abar_re.reshape(1, g * n), abar_im.reshape(1, g * n))
    return dict(a_re=a_re, a_im=a_im, log_dt=log_dt, f_re=f_re, f_im=f_im, b2_re=b2_re, b2_im=b2_im, c2_re=c2_re,
                c2_im=c2_im, pw=pw, dskip=prm["ssm_d"][l].reshape(1, g * p))


def _ssm_param_grads(l, sd, r, prm):
    g, n, p = SSM_GROUPS, SSM_STATE, SSM_GROUP
    dfr, dfi, db_re, db_im = _bbar_bwd(l, sd["f_re"], sd["f_im"], prm["ssm_b_re"], prm["ssm_b_im"],
                                       r["dbbar_re"].reshape(g, p, n), r["dbbar_im"].reshape(g, p, n))
    cts = (r["dabar_re"].reshape(g, n), r["dabar_im"].reshape(g, n), dfr.reshape(g, n), dfi.reshape(g, n))
    da_re, da_im, dlog_dt = _zoh_bwd(l, sd["a_re"], sd["a_im"], sd["log_dt"], cts)
    return dict(ssm_a_re=da_re, ssm_a_im=da_im, ssm_log_dt=dlog_dt.reshape(g), ssm_b_re=db_re, ssm_b_im=db_im,
                ssm_c_re=r["dc_re"].reshape(g, p, n), ssm_c_im=r["dc_im"].reshape(g, p, n),
                ssm_d=r["dd"].reshape(g, p))


def _ffn_weight_grads(l, fb, dx2, s, place):
    d = dx2.shape[1]
    hcn = fb["act"].shape[1]
    g = {}
    for name, key, rhs in (("ffn_w_gate", "dgate", fb["h2"]), ("ffn_w_up", "dup", fb["h2"]),
                           ("ffn_w_down", "act", fb["dx2"])):
        g[name] = _tn_matmul(f"d{name}_l{l}", fb[key], BS((None, hcn, s), lambda j, t, pr: (j, 0, 0)), rhs,
                             BS((s, d), lambda j, t, pr: (0, 0)), (hcn, d), (N_CHIPS, 1), place)
    return g


def _in_weight_grad(l, ht, dz, place):
    d, s = ht.shape
    ncw = dz.shape[1] // N_CHIPS
    return _tn_matmul(f"dw_in_l{l}", ht, BS((d, s), lambda j, t, pr: (0, 0)), dz, BS((s, ncw), lambda j, t, pr: (0, j)),
                      (d, ncw), (N_CHIPS, 1), place)


def _fused_tn(name, pairs, kinds, s, place):
    n = len(pairs)

    def shape_of(a, b, kind):
        k, m = a.shape[1], b.shape[1]
        if kind == "rows":
            return (N_CHIPS, k // N_CHIPS, m)
        if kind == "cols":
            return (N_CHIPS, k, m // N_CHIPS)
        return (k // 128, 128, 128)

    shapes = [shape_of(a, b, kind) for (a, b), kind in zip(pairs, kinds)]
    out_shape = []
    for shp, kind in zip(shapes, kinds):
        out_shape += [SDS(shp, F32)] if kind == "groups" else [SDS(shp[1:], F32), SDS(shp, WIRE_DTYPE)]

    def body(place_ref, *refs):
        ins, outs, accs = refs[:2 * n], refs[2 * n:2 * n + len(out_shape)], refs[2 * n + len(out_shape):]
        o = 0
        for i, kind in enumerate(kinds):
            a, b = ins[2 * i][...], ins[2 * i + 1][...]
            if kind == "groups":
                for k in range(shapes[i][0]):
                    outs[o][k] = _mm_tn(a[:, k * 128:(k + 1) * 128], b[:, k * 128:(k + 1) * 128])
                o += 1
                continue
            acc = accs[i]
            if kind == "rows":
                acc[...] = _mm_tn(a, b).reshape(acc.shape)
            else:
                full = _mm_tn(a, b)
                nc = acc.shape[2]
                for j in range(N_CHIPS):
                    acc[j] = full[:, j * nc:(j + 1) * nc]
            outs[o][...] = acc[place_ref[0]]
            outs[o + 1][...] = acc[...].astype(WIRE_DTYPE)
            o += 2

    whole = lambda shp: BS(shp, lambda t, pr: (0,) * len(shp))
    outs = pl.pallas_call(
        body, name=name,
        grid_spec=pltpu.PrefetchScalarGridSpec(
            num_scalar_prefetch=1, grid=(1,),
            in_specs=[whole(v.shape) for pair in pairs for v in pair],
            out_specs=[whole(o.shape) for o in out_shape],
            scratch_shapes=[pltpu.VMEM(shp, F32) for shp in shapes]),
        out_shape=out_shape, compiler_params=_params(),
    )(place, *[v for pair in pairs for v in pair])
    res, o = [], 0
    for kind in kinds:
        if kind == "groups":
            res.append(outs[o])
            o += 1
        else:
            res.append((outs[o], outs[o + 1]))
            o += 2
    return res


def _mixer_weight_grads(l, sv, mb, dx1, s, place):
    g = {}
    (g["w_out"], g["ssm_w_glu"]) = _fused_tn(f"dw_out_glu_l{l}", [(mb["merged"], dx1), (mb["ge"], mb["dt"])],
                                            ("rows", "rows"), s, place)
    (g["ssm_w_proj"], g["conv_w_proj"], g["pool_w_proj"]) = _fused_tn(
        f"dw_proj_l{l}", [(mb["sa"], mb["dya"]), (mb["ac"], mb["dyb"]), (mb["pp"], mb["dyc"])],
        ("cols", "cols", "cols"), s, place)
    (dwgrp,) = _fused_tn(f"dpool_w_group_l{l}", [(sv["p"], mb["dq"])], ("groups",), s, place)
    return g, dwgrp


def _local_step(x, target, weights_of, prm, place, on_grads=None):
    s, d = x.shape
    cw = prm["ssm_b_glu"].shape[1]
    sp = {k: prm[k].reshape(N_LAYERS, 1, -1) for k in ("norm1", "norm2", "b_gate", "ssm_b_glu", "conv_ln_g", "conv_ln_b",
                                                        "pool_scale", "conv_b_dw")}
    sp["pool_w_group"] = prm["pool_w_group"]
    saved = []
    xin = x
    prepared = [_ssm_prepare(l, prm) for l in range(N_LAYERS)]
    ready = tuple(sd[k] for sd in prepared for k in ("pw", "b2_re", "c2_re"))
    for l in range(N_LAYERS):
        fw = weights_of(l, "in", (xin,) + (ready if l == 0 else ()))
        sd = prepared[l]
        z, h = _in_proj(l, xin, sp["norm1"], fw["w_in"])
        hre, him, y = _ssm_fwd(l, z, sd["b2_re"], sd["b2_im"], sd["c2_re"], sd["c2_im"], sd["pw"], sd["dskip"])
        p = _pool_fwd(l, z, cw)
        fw.update(weights_of(l, "mixer", (y, p)))
        wdw = fw["conv_w_dw"]
        hc = _conv_fwd(l, z, wdw, sp["conv_b_dw"][l])
        x1 = _merge_fwd(l, xin, y, hc, p, z, fw, sp)
        fw.update(weights_of(l, "ffn", (x1,)))
        x2, gate, up, h2 = _ffn_fwd(l, x1, sp["norm2"], fw["ffn_w_gate"], fw["ffn_w_up"], fw["ffn_w_down"])
        saved.append(dict(x=xin, z=z, h=h, hre=hre, him=him, y=y, hc=hc, p=p, x1=x1, sd=sd, wdw=wdw, fw=fw,
                          gate=gate, up=up, h2=h2))
        xin = x2
    dx, loss, dfinal = _loss_head(xin, target, prm["final_norm"].reshape(1, d))
    big = [None] * N_LAYERS
    small = [None] * N_LAYERS
    norm2_rows = sp["norm2"]
    started = (lambda l, group, grads: on_grads(l, group, grads)) if on_grads is not None else (lambda *a: 0.0)
    for l in reversed(range(N_LAYERS)):
        sv = saved[l]
        sd, fw = sv["sd"], sv["fw"]
        fb = _ffn_bwd(l, sv["x1"], dx, sv["gate"], sv["up"], norm2_rows, fw["ffn_w_gate"], fw["ffn_w_up"],
                      fw["ffn_w_down"])
        fb["h2"] = sv["h2"]
        big[l] = _ffn_weight_grads(l, fb, dx, s, place)
        spl = dict(sp, ssm_b_glu=sp["ssm_b_glu"] + started(l, "ffn", big[l]))
        mb = _merge_bwd(l, fb["dx1"], sv["y"], sv["hc"], sv["p"], sv["z"], fw, spl)
        mixer, dwgrp = _mixer_weight_grads(l, sv, mb, fb["dx1"], s, place)
        big[l].update(mixer)
        wdw = sv["wdw"] + started(l, "mixer", mixer)
        du_c = _pool_bwd(l, mb["dp"])
        dv1, dv2, dwdw, dbdw = _conv_bwd(l, mb["dhc"], sv["z"], wdw)
        sr = _ssm_bwd(l, mb["dy"], sv["z"], sv["hre"], sv["him"], sd["b2_re"], sd["b2_im"], sd["c2_re"],
                      sd["c2_im"], sd["pw"], sd["dskip"])
        dx, dz, dnorm1 = _in_proj_bwd(l, fb["dx1"], sv["x"], sp["norm1"], fw["w_in"], sr["du"], dv1, dv2, du_c, mb["dzg"])
        w_in_grad = {"w_in": _in_weight_grad(l, sv["h"], dz, place)}
        big[l].update(w_in_grad)
        sg = _ssm_param_grads(l, sd, sr, prm)
        sg.update(norm1=dnorm1.reshape(d), b_gate=mb["db_gate"].reshape(3 * d), ssm_b_glu=mb["db_glu"].reshape(cw),
                  conv_b_dw=dbdw.reshape(cw), conv_ln_g=mb["dln_g"].reshape(cw), conv_ln_b=mb["dln_b"].reshape(cw),
                  pool_w_group=dwgrp, pool_scale=mb["dscale"].reshape(cw), norm2=fb["dnorm2"].reshape(d),
                  conv_w_dw=dwdw)
        small[l] = sg
        if l == N_LAYERS - 1:
            sg = dict(sg, final_norm=dfinal.reshape(d))
        norm2_rows = sp["norm2"] + (started(l, "in", w_in_grad) + started(l, "small", sg))
    return loss[0, 0], dx, big, small, dfinal.reshape(d)


def _place():
    return lax.axis_index("x"), lax.axis_index("y"), lax.axis_index("c")


def _other_chips(x, y):
    return [(1 - x, y), (x, 1 - y), (1 - x, 1 - y)]


def _remote(src, dst, send_sem, recv_sem, device):
    return pltpu.make_async_remote_copy(src_ref=src, dst_ref=dst, send_sem=send_sem, recv_sem=recv_sem,
                                        device_id=device, device_id_type=MESH)


def _hbm(v):
    return pltpu.with_memory_space_constraint(v, pltpu.HBM)


def _cast_into(name, w, place, dtype, after=()):
    nl, k, n = w.shape
    tr = _row_tile(k, n)
    nt = k // tr

    def body(place_ref, w_ref, *rest):
        o0_ref, o1_ref = rest[len(after):]

        @pl.when(pl.program_id(0) == 0)
        def _():
            o0_ref[...] = w_ref[...].astype(dtype)

        @pl.when(pl.program_id(0) == 1)
        def _():
            o1_ref[...] = w_ref[...].astype(dtype)

    return pl.pallas_call(
        body, name=f"cast_{name}",
        grid_spec=pltpu.PrefetchScalarGridSpec(
            num_scalar_prefetch=1, grid=(nl, nt),
            in_specs=[BS((None, tr, n), lambda l, t, pr: (l, t, 0))] + [ANY] * len(after),
            out_specs=[BS((None, tr, n), lambda l, t, pr: (pr[0], t * (1 - l) + (nt - 1) * l, 0)),
                       BS((None, tr, n), lambda l, t, pr: (pr[0], t * l, 0))]),
        out_shape=[SDS((N_CHIPS, k, n), dtype)] * 2)(place, w, *after)


def _cast_small_into(tag, ws, dtypes, place, after=()):
    n = len(ws)

    def body(place_ref, *refs):
        ins, outs = refs[:n], refs[n + len(after):]
        for i in range(n):
            @pl.when(pl.program_id(0) == 0)
            def _():
                outs[2 * i][...] = ins[i][...].astype(dtypes[i])

            @pl.when(pl.program_id(0) == 1)
            def _():
                outs[2 * i + 1][...] = ins[i][...].astype(dtypes[i])

    slot = lambda w: BS((None,) + w.shape[1:], lambda l, pr: (pr[0], 0, 0))
    outs = pl.pallas_call(
        body, name=f"cast_{tag}",
        grid_spec=pltpu.PrefetchScalarGridSpec(
            num_scalar_prefetch=1, grid=(N_LAYERS,),
            in_specs=[BS((None,) + w.shape[1:], lambda l, pr: (l, 0, 0)) for w in ws] + [ANY] * len(after),
            out_specs=[slot(w) for w in ws for _ in range(N_LAYERS)]),
        out_shape=[SDS((N_CHIPS,) + w.shape[1:], dt) for w, dt in zip(ws, dtypes) for _ in range(N_LAYERS)],
    )(place, *ws, *after)
    return [tuple(outs[N_LAYERS * i:N_LAYERS * (i + 1)]) for i in range(n)]


def _gather_rows(buf, c):
    k = buf.shape[1]
    if k % 2:
        return pl.ds(0, k)
    return pl.ds(pl.multiple_of(c * (k // 2), 8), k // 2)


def _allgather_start(tag, groups):
    ng = len(groups)
    sizes = [len(g) for g in groups]
    first = [sum(sizes[:g]) for g in range(ng)]
    flat = [b for g in groups for b in g]
    nb = len(flat)

    def body(*refs):
        ins = refs[:nb]
        sems = refs[nb:nb + 2 * ng]
        token = refs[-1]
        x, y, c = _place()
        jme = 2 * x + y
        for g in range(ng):
            for a in range(sizes[g]):
                buf = ins[first[g] + a]
                blk = buf.at[jme, _gather_rows(buf, c)]
                for k, (cx, cy) in enumerate(_other_chips(x, y)):
                    _remote(blk, blk, sems[2 * g].at[3 * a + k], sems[2 * g + 1].at[3 * a + k], (cx, cy, c)).start()
        token[...] = jnp.zeros(token.shape, F32)

    sem_shapes = [pltpu.SemaphoreType.DMA((3 * sizes[g // 2],)) for g in range(2 * ng)]
    outs = pl.pallas_call(
        body, name=f"allgather_start_{tag}", in_specs=[HBM] * nb,
        out_specs=[SEM] * (2 * ng) + [HBM] * nb + [pl.BlockSpec(memory_space=pltpu.VMEM)],
        out_shape=sem_shapes + [pltpu.HBM(b.shape, b.dtype) for b in flat] + [SDS((8, 128), F32)],
        input_output_aliases={i: 2 * ng + i for i in range(nb)},
        compiler_params=pltpu.CompilerParams(has_side_effects=SIDE_EFFECT))(*[_hbm(b) for b in flat])
    per_group = [(outs[2 * g], outs[2 * g + 1], outs[2 * ng + first[g]:2 * ng + first[g] + sizes[g]])
                 for g in range(ng)]
    return per_group, outs[-1]


def _allgather_wait(l, send_sems, recv_sems, bufs, after):
    n = len(bufs)

    def body(*refs):
        ins = refs[:n]
        ssem, rsem = refs[n], refs[n + 1]
        x, y, c = _place()
        jme = 2 * x + y
        for a in range(n):
            rows = _gather_rows(ins[a], c)
            for k, (cx, cy) in enumerate(_other_chips(x, y)):
                cp = _remote(ins[a].at[jme, rows], ins[a].at[2 * cx + cy, rows], ssem.at[3 * a + k],
                             rsem.at[3 * a + k], (cx, cy, c))
                cp.wait_send()
                cp.wait_recv()

    return pl.pallas_call(
        body, name=f"allgather_wait_{l}", in_specs=[HBM] * n + [SEM, SEM] + [ANY] * len(after), out_specs=[HBM] * n,
        out_shape=[pltpu.HBM(b.shape, b.dtype) for b in bufs], input_output_aliases={i: i for i in range(n)},
        compiler_params=pltpu.CompilerParams(has_side_effects=SIDE_EFFECT))(*bufs, send_sems, recv_sems, *after)


def _allgather_forward(l, bufs):
    n = len(bufs)
    split = [a for a in range(n) if bufs[a].shape[1] % 2 == 0]

    def body(*refs):
        ins = refs[:n]
        send_sems, recv_sems = refs[2 * n:]
        x, y, c = _place()
        sibling = (x, y, 1 - c)
        copies = []
        for a in split:
            for k, (cx, cy) in enumerate(_other_chips(x, y)):
                blk = ins[a].at[2 * cx + cy, _gather_rows(ins[a], c)]
                cp = _remote(blk, blk, send_sems.at[a, k], recv_sems.at[a, k], sibling)
                cp.start()
                copies.append(cp)
        for a in split:
            for k, (cx, cy) in enumerate(_other_chips(x, y)):
                blk = ins[a].at[2 * cx + cy, _gather_rows(ins[a], 1 - c)]
                _remote(blk, blk, send_sems.at[a, k], recv_sems.at[a, k], sibling).wait_recv()
        for cp in copies:
            cp.wait_send()

    sem = pltpu.SemaphoreType.DMA((n, 3))
    return pl.pallas_call(
        body, name=f"allgather_forward_{l}", in_specs=[ANY] * n, out_specs=[ANY] * n,
        out_shape=[SDS(b.shape, b.dtype) for b in bufs], input_output_aliases={i: i for i in range(n)},
        scratch_shapes=[sem, sem])(*bufs)


def _rs_to_owner(l, parts):
    n = len(parts)
    lands = [lax.empty((3,) + p.shape[1:], p.dtype) for p in parts]

    def body(*refs):
        ins, zones = refs[:n], refs[n:2 * n]
        send_sems, recv_sems = refs[2 * n], refs[2 * n + 1]
        token = refs[-1]
        x, y, c = _place()
        for a in range(n):
            for k, (cx, cy) in enumerate(_other_chips(x, y)):
                _remote(ins[a].at[2 * cx + cy], zones[a].at[k], send_sems.at[3 * a + k], recv_sems.at[3 * a + k],
                        (cx, cy, c)).start()
        token[...] = jnp.zeros(token.shape, F32)

    sem = pltpu.SemaphoreType.DMA((3 * n,))
    outs = pl.pallas_call(
        body, name=f"rs_to_owner_start_{l}", in_specs=[HBM] * (2 * n),
        out_specs=[SEM, SEM] + [HBM] * (2 * n) + [pl.BlockSpec(memory_space=pltpu.VMEM)],
        out_shape=[sem, sem] + [pltpu.HBM(p.shape, p.dtype) for p in parts]
        + [pltpu.HBM(z.shape, z.dtype) for z in lands] + [SDS((8, 128), F32)],
        input_output_aliases={i: 2 + i for i in range(2 * n)},
        compiler_params=pltpu.CompilerParams(has_side_effects=SIDE_EFFECT),
    )(*[_hbm(p) for p in parts], *[_hbm(z) for z in lands])
    return outs[0], outs[1], outs[2:2 + n], outs[2 + n:2 + 2 * n], outs[-1]


def _rs_to_owner_wait(l, send_sems, recv_sems, parts, lands, after):
    n = len(parts)

    def body(*refs):
        ins, zones = refs[:n], refs[n:2 * n]
        ssem, rsem = refs[2 * n], refs[2 * n + 1]
        x, y, c = _place()
        for a in range(n):
            for k, (cx, cy) in enumerate(_other_chips(x, y)):
                cp = _remote(ins[a].at[2 * cx + cy], zones[a].at[k], ssem.at[3 * a + k], rsem.at[3 * a + k],
                             (cx, cy, c))
                cp.wait_send()
                cp.wait_recv()

    outs = pl.pallas_call(
        body, name=f"rs_to_owner_wait_{l}", in_specs=[HBM] * (2 * n) + [SEM, SEM] + [ANY] * len(after),
        out_specs=[HBM] * (2 * n),
        out_shape=[pltpu.HBM(p.shape, p.dtype) for p in parts] + [pltpu.HBM(z.shape, z.dtype) for z in lands],
        input_output_aliases={i: i for i in range(2 * n)},
        compiler_params=pltpu.CompilerParams(has_side_effects=SIDE_EFFECT),
    )(*parts, *lands, send_sems, recv_sems, *after)
    return outs[:n], outs[n:]


def _rs_sibling_exchange(l, both):
    n = len(both)

    def body(*refs):
        ins = refs[:n]
        send_sems, recv_sems = refs[2 * n:]
        x, y, c = _place()
        copies = []
        for a in range(n):
            cp = _remote(ins[a].at[c], ins[a].at[c], send_sems.at[a], recv_sems.at[a], (x, y, 1 - c))
            cp.start()
            copies.append(cp)
        for a, cp in enumerate(copies):
            cp.wait_send()
            _remote(ins[a].at[1 - c], ins[a].at[1 - c], send_sems.at[a], recv_sems.at[a], (x, y, 1 - c)).wait_recv()

    sem = pltpu.SemaphoreType.DMA((n,))
    return pl.pallas_call(
        body, name=f"rs_sibling_exchange_{l}", in_specs=[ANY] * n, out_specs=[ANY] * n,
        out_shape=[SDS(b.shape, b.dtype) for b in both], input_output_aliases={i: i for i in range(n)},
        scratch_shapes=[sem, sem])(*both)


def _add_owner(name, grad, recv, place):
    r, cols = grad.shape
    tr = _row_tile(r, cols, budget=1024 * 1024)
    nt = r // tr

    def body(place_ref, g_ref, r_ref, o_ref):
        acc = ((g_ref[...] + r_ref[0].astype(F32)) + r_ref[1].astype(F32)) + r_ref[2].astype(F32)
        o_ref[...] = acc.astype(o_ref.dtype)

    return pl.pallas_call(
        body, name=name,
        grid_spec=pltpu.PrefetchScalarGridSpec(
            num_scalar_prefetch=1, grid=(nt,),
            in_specs=[BS((tr, cols), lambda t, pr: (t, 0)), BS((3, tr, cols), lambda t, pr: (0, t, 0))],
            out_specs=BS((None, tr, cols), lambda t, pr: (pr[1], t, 0))),
        out_shape=SDS((2, r, cols), WIRE_DTYPE))(place, grad, recv)


def _add_owner_small(tag, grads, recvs, place):
    n = len(grads)

    def body(place_ref, *refs):
        gs, rs, outs = refs[:n], refs[n:2 * n], refs[2 * n:]
        for g_ref, r_ref, o_ref in zip(gs, rs, outs):
            acc = ((g_ref[...] + r_ref[0].astype(F32)) + r_ref[1].astype(F32)) + r_ref[2].astype(F32)
            o_ref[...] = acc.astype(o_ref.dtype)

    return pl.pallas_call(
        body, name=f"rs_add_owner_{tag}",
        grid_spec=pltpu.PrefetchScalarGridSpec(
            num_scalar_prefetch=1, grid=(1,),
            in_specs=[BS(g.shape, lambda t, pr: (0, 0)) for g in grads]
            + [BS(r.shape, lambda t, pr: (0, 0, 0)) for r in recvs],
            out_specs=[BS((None,) + g.shape, lambda t, pr: (pr[1], 0, 0)) for g in grads]),
        out_shape=[SDS((2,) + g.shape, WIRE_DTYPE) for g in grads])(place, *grads, *recvs)


def _reduce_start(tag, grads):
    names = list(grads)
    send_sems, recv_sems, wires, lands, token = _rs_to_owner(tag, [grads[n][1] for n in names])
    return dict(tag=tag, names=names, send_sems=send_sems, recv_sems=recv_sems, wires=wires, lands=lands,
                grads=[grads[n][0] for n in names]), token


def _reduce_finish(tag, groups, place, after):
    all_names, all_mine = [], []
    for pending in groups:
        sub, names = pending["tag"], pending["names"]
        _, lands = _rs_to_owner_wait(sub, pending["send_sems"], pending["recv_sems"], pending["wires"],
                                     pending["lands"], after)
        if max(g.size for g in pending["grads"]) <= SMALL_GRAD_ELEMS:
            mine = _add_owner_small(sub, pending["grads"], lands, place)
        else:
            mine = [_add_owner(f"rs_add_owner_{n}_{sub}", g, r, place)
                    for n, g, r in zip(names, pending["grads"], lands)]
        all_names += names
        all_mine += mine
    return dict(zip(all_names, _rs_sibling_exchange(tag, all_mine)))


def _small_peers(x, y, c):
    return [(x, y, 1 - c)] + [(cx, cy, c) for cx, cy in _other_chips(x, y)]


def _allgather_rows_start(tag, bufs):
    n = len(bufs)
    lands = [lax.empty((8,) + b.shape, b.dtype) for b in bufs]

    def body(*refs):
        ins, zones = refs[:n], refs[n:2 * n]
        send_sems, recv_sems = refs[2 * n], refs[2 * n + 1]
        token = refs[-1]
        x, y, c = _place()
        for a in range(n):
            for i, peer in enumerate(_small_peers(x, y, c)):
                _remote(ins[a], zones[a].at[4 * x + 2 * y + c], send_sems.at[4 * a + i], recv_sems.at[4 * a + i],
                        peer).start()
        token[...] = jnp.zeros(token.shape, F32)

    sem = pltpu.SemaphoreType.DMA((4 * n,))
    outs = pl.pallas_call(
        body, name=f"allgather_small_start_{tag}", in_specs=[HBM] * (2 * n),
        out_specs=[SEM, SEM] + [HBM] * (2 * n) + [pl.BlockSpec(memory_space=pltpu.VMEM)],
        out_shape=[sem, sem] + [pltpu.HBM(b.shape, b.dtype) for b in bufs]
        + [pltpu.HBM(z.shape, z.dtype) for z in lands] + [SDS((8, 128), F32)],
        input_output_aliases={i: 2 + i for i in range(2 * n)},
        compiler_params=pltpu.CompilerParams(has_side_effects=SIDE_EFFECT),
    )(*[_hbm(b) for b in bufs], *[_hbm(z) for z in lands])
    return outs[0], outs[1], outs[2:2 + n], outs[2 + n:2 + 2 * n], outs[-1]


def _allgather_rows_wait(tag, send_sems, recv_sems, bufs, lands, after):
    n = len(bufs)

    def body(*refs):
        ins, zones = refs[:n], refs[n:2 * n]
        ssem, rsem = refs[2 * n], refs[2 * n + 1]
        x, y, c = _place()
        for a in range(n):
            for i, (px, py, pc) in enumerate(_small_peers(x, y, c)):
                cp = _remote(ins[a], zones[a].at[4 * px + 2 * py + pc], ssem.at[4 * a + i], rsem.at[4 * a + i],
                             (px, py, pc))
                cp.wait_send()
                cp.wait_recv()

    outs = pl.pallas_call(
        body, name=f"allgather_small_wait_{tag}", in_specs=[HBM] * (2 * n) + [SEM, SEM, ANY],
        out_specs=[HBM] * (2 * n),
        out_shape=[pltpu.HBM(b.shape, b.dtype) for b in bufs] + [pltpu.HBM(z.shape, z.dtype) for z in lands],
        input_output_aliases={i: i for i in range(2 * n)},
        compiler_params=pltpu.CompilerParams(has_side_effects=SIDE_EFFECT),
    )(*bufs, *lands, send_sems, recv_sems, after)
    return outs[:n], outs[n:]


def _allgather_rows_forward(tag, lands):
    n = len(lands)

    def body(*refs):
        ins = refs[:n]
        send_sems, recv_sems = refs[2 * n:]
        x, y, c = _place()
        sibling = (x, y, 1 - c)
        copies = []
        for a in range(n):
            for k, (cx, cy) in enumerate(_other_chips(x, y)):
                blk = ins[a].at[4 * cx + 2 * cy + c]
                cp = _remote(blk, blk, send_sems.at[a, k], recv_sems.at[a, k], sibling)
                cp.start()
                copies.append(cp)
        for a in range(n):
            for k, (cx, cy) in enumerate(_other_chips(x, y)):
                blk = ins[a].at[4 * cx + 2 * cy + 1 - c]
                _remote(blk, blk, send_sems.at[a, k], recv_sems.at[a, k], sibling).wait_recv()
        for cp in copies:
            cp.wait_send()

    sem = pltpu.SemaphoreType.DMA((n, 3))
    return pl.pallas_call(body, name=f"allgather_small_forward_{tag}", in_specs=[ANY] * n, out_specs=[ANY] * n,
                          out_shape=[SDS(z.shape, z.dtype) for z in lands],
                          input_output_aliases={i: i for i in range(n)}, scratch_shapes=[sem, sem])(*lands)


def _sum_devices(tag, gathered, mine, place):
    _, r, cols = gathered.shape
    tr = _row_tile(r, cols, budget=256 * 1024)

    def body(place_ref, g_ref, x_ref, o_ref):
        me = 2 * place_ref[0] + place_ref[1]
        acc = jnp.where(me == 0, x_ref[...], g_ref[0])
        for k in range(1, 8):
            acc = acc + jnp.where(me == k, x_ref[...], g_ref[k])
        o_ref[...] = acc

    return pl.pallas_call(
        body, name=f"sum_small_grads_{tag}",
        grid_spec=pltpu.PrefetchScalarGridSpec(
            num_scalar_prefetch=1, grid=(r // tr,),
            in_specs=[BS((8, tr, cols), lambda t, pr: (0, t, 0)), BS((tr, cols), lambda t, pr: (t, 0))],
            out_specs=BS((tr, cols), lambda t, pr: (t, 0))),
        out_shape=SDS((r, cols), F32))(place, gathered, mine)


def _adamw_values(w, g, m, v):
    m = ADAM_B1 * m + (1.0 - ADAM_B1) * g
    v = ADAM_B2 * v + (1.0 - ADAM_B2) * (g * g)
    m_hat = m / (1.0 - ADAM_B1 ** ADAM_STEP)
    v_hat = v / (1.0 - ADAM_B2 ** ADAM_STEP)
    delta = -ADAM_LR * (m_hat / (jnp.sqrt(v_hat) + ADAM_EPS) + ADAM_WD * w)
    return delta, m, v


def _adamw_big(name, l, w, m, v, g, earlier=None, after=()):
    nl, r, cols = w.shape
    tr = _row_tile(r, cols, budget=1024 * 1024)
    nt = r // tr
    n_prev = 0 if earlier is None else 4

    def body(*refs):
        w_ref, m_ref, v_ref, g_ref = refs[:4]
        go_ref, d_ref, mo_ref, vo_ref = refs[4 + n_prev + len(after):]
        gv = g_ref[0].astype(F32) + g_ref[1].astype(F32)
        delta, m_new, v_new = _adamw_values(w_ref[...], gv, m_ref[...], v_ref[...])
        go_ref[...] = gv
        d_ref[...] = delta
        mo_ref[...] = m_new
        vo_ref[...] = v_new

    layer = BS((None, tr, cols), lambda t: (l, t, 0))
    return pl.pallas_call(
        body, name=f"adamw_{name}_l{l}", grid=(nt,),
        in_specs=[layer, layer, layer, BS((2, tr, cols), lambda t: (0, t, 0))] + [ANY] * (n_prev + len(after)),
        out_specs=[layer] * 4, out_shape=[SDS(w.shape, F32)] * 4,
        input_output_aliases={4 + i: i for i in range(n_prev)}, compiler_params=_params(),
    )(w, m, v, g, *(earlier or ()), *after)


def _adamw_small_group(tag, l, ws, ms, vs, gs, earlier, after=()):
    n = len(ws)
    steps = ADAMW_GROUP_STEPS
    prev = [a for e in earlier if e is not None for a in e]
    n_prev = len(prev)
    assert n_prev in (0, 4 * n)

    def body(*refs):
        w_refs, m_refs, v_refs, g_refs = refs[:n], refs[n:2 * n], refs[2 * n:3 * n], refs[3 * n:4 * n]
        outs = refs[4 * n + n_prev + len(after):]
        for i in range(n):
            gv = g_refs[i][0].astype(F32) + g_refs[i][1].astype(F32)
            delta, m_new, v_new = _adamw_values(w_refs[i][...], gv, m_refs[i][...], v_refs[i][...])
            for ref, val in zip(outs[4 * i:4 * i + 4], (gv, delta, m_new, v_new)):
                ref[...] = val

    def layer(w):
        return BS((None, w.shape[1] // steps, w.shape[2]), lambda t: (l, t, 0))

    return pl.pallas_call(
        body, name=f"adamw_{tag}_l{l}", grid=(steps,),
        in_specs=[layer(w) for w in ws] * 3
        + [BS((2, w.shape[1] // steps, w.shape[2]), lambda t: (0, t, 0)) for w in ws] + [ANY] * (n_prev + len(after)),
        out_specs=[layer(w) for w in ws for _ in range(4)],
        out_shape=[SDS(w.shape, F32) for w in ws for _ in range(4)],
        input_output_aliases={4 * n + i: i for i in range(n_prev)}, compiler_params=_params(),
    )(*ws, *ms, *vs, *gs, *prev, *after)


def _adamw_mid(ws, ms, vs, gathered, mine, place):
    n = len(ws)
    shape = ws[0].shape[1:]
    zeros = (0,) * len(shape)

    def body(place_ref, *refs):
        w_refs, m_refs, v_refs = refs[:n], refs[n:2 * n], refs[2 * n:3 * n]
        gath, own = refs[3 * n:(3 + N_LAYERS) * n], refs[(3 + N_LAYERS) * n:(3 + 2 * N_LAYERS) * n]
        outs = refs[(3 + 2 * N_LAYERS) * n:]
        me = 2 * place_ref[0] + place_ref[1]
        for i in range(n):
            gv = None
            for l in range(N_LAYERS):
                g_ref, x_ref = gath[l * n + i], own[l * n + i]
                acc = jnp.where(me == 0, x_ref[...], g_ref[0])
                for k in range(1, 8):
                    acc = acc + jnp.where(me == k, x_ref[...], g_ref[k])
                gv = acc if gv is None else jnp.where(pl.program_id(0) == l, acc, gv)
            delta, m_new, v_new = _adamw_values(w_refs[i][...], gv, m_refs[i][...], v_refs[i][...])
            for ref, val in zip(outs[4 * i:4 * i + 4], (gv, delta, m_new, v_new)):
                ref[...] = val

    layer = BS((None,) + shape, lambda l, pr: (l,) + zeros)
    kept = pl.Buffered(1)
    outs = pl.pallas_call(
        body, name="adamw_replicated_matrices",
        grid_spec=pltpu.PrefetchScalarGridSpec(
            num_scalar_prefetch=1, grid=(N_LAYERS,),
            in_specs=[layer] * (3 * n)
            + [BS((8,) + shape, lambda l, pr: (0,) + zeros, pipeline_mode=kept)] * (N_LAYERS * n)
            + [BS(shape, lambda l, pr: zeros, pipeline_mode=kept)] * (N_LAYERS * n),
            out_specs=[layer] * (4 * n)),
        out_shape=[SDS(ws[0].shape, F32)] * (4 * n), compiler_params=_params(),
    )(place, *ws, *ms, *vs, *[g for l in range(N_LAYERS) for g in gathered[l]],
      *[x for l in range(N_LAYERS) for x in mine[l]])
    return [tuple(outs[4 * i:4 * i + 4]) for i in range(n)]


def _adamw_rows(w, m, v, g):
    r, cols = w.shape
    tr = _row_tile(r, cols, budget=512 * 1024)

    def body(w_ref, m_ref, v_ref, g_ref, d_ref, mo_ref, vo_ref):
        delta, m_new, v_new = _adamw_values(w_ref[...], g_ref[...], m_ref[...], v_ref[...])
        d_ref[...] = delta
        mo_ref[...] = m_new
        vo_ref[...] = v_new

    spec = BS((tr, cols), lambda t: (t, 0))
    return pl.pallas_call(body, name="adamw_small", grid=(r // tr,), in_specs=[spec] * 4, out_specs=[spec] * 3,
                          out_shape=[SDS(w.shape, F32)] * 3)(w, m, v, g)


SMALL_GRAD_ELEMS = 256 * 1024
ADAMW_GROUP_STEPS = 4
PACK_ALIGN = 8 * 128
PACK_ROWS = 128


def _pack_rows(arrays):
    parts, rows = [], 0
    for a in arrays:
        flat = a.reshape(-1)
        pad = (-flat.shape[0]) % PACK_ALIGN
        if pad:
            flat = jnp.pad(flat, (0, pad))
        parts.append(flat.reshape(-1, 128))
        rows += parts[-1].shape[0]
    if rows % PACK_ROWS:
        parts.append(jnp.zeros((PACK_ROWS - rows % PACK_ROWS, 128), parts[0].dtype))
    return jnp.concatenate(parts, axis=0)


def _unpack_rows(buf, shapes):
    out, row = [], 0
    for shape in shapes:
        size = math.prod(shape)
        rows = -(-size // PACK_ALIGN) * (PACK_ALIGN // 128)
        out.append(buf[row:row + rows].reshape(-1)[:size].reshape(shape))
        row += rows
    return out


def kernel(x, norm1, w_in, b_gate, ssm_a_re, ssm_a_im, ssm_log_dt, ssm_b_re, ssm_b_im, ssm_c_re, ssm_c_im, ssm_d, ssm_w_glu, ssm_b_glu, ssm_w_proj, conv_w_dw, conv_b_dw, conv_ln_g, conv_ln_b, conv_w_proj, pool_w_group, pool_scale, pool_w_proj, w_out, norm2, ffn_w_gate, ffn_w_up, ffn_w_down, final_norm, loss_target, m_norm1, m_w_in, m_b_gate, m_ssm_a_re, m_ssm_a_im, m_ssm_log_dt, m_ssm_b_re, m_ssm_b_im, m_ssm_c_re, m_ssm_c_im, m_ssm_d, m_ssm_w_glu, m_ssm_b_glu, m_ssm_w_proj, m_conv_w_dw, m_conv_b_dw, m_conv_ln_g, m_conv_ln_b, m_conv_w_proj, m_pool_w_group, m_pool_scale, m_pool_w_proj, m_w_out, m_norm2, m_ffn_w_gate, m_ffn_w_up, m_ffn_w_down, m_final_norm, v_norm1, v_w_in, v_b_gate, v_ssm_a_re, v_ssm_a_im, v_ssm_log_dt, v_ssm_b_re, v_ssm_b_im, v_ssm_c_re, v_ssm_c_im, v_ssm_d, v_ssm_w_glu, v_ssm_b_glu, v_ssm_w_proj, v_conv_w_dw, v_conv_b_dw, v_conv_ln_g, v_conv_ln_b, v_conv_w_proj, v_pool_w_group, v_pool_scale, v_pool_w_proj, v_w_out, v_norm2, v_ffn_w_gate, v_ffn_w_up, v_ffn_w_down, v_final_norm):
    given = dict(locals())
    cx, cy, cc = _place()
    place = jnp.stack([2 * cx + cy, cc]).astype(jnp.int32)

    def kernel_view(n, a):
        if n in TRANSPOSED:
            return a.transpose(0, 2, 1)
        return a.transpose(0, 1, 3, 2) if n in ("ssm_b_re", "ssm_b_im") else a

    prm = {n: given[n] for n in WEIGHTS}
    mom = {n: given["m_" + n] for n in WEIGHTS}
    var = {n: given["v_" + n] for n in WEIGHTS}
    for n in MID:
        prm[n], mom[n], var[n] = kernel_view(n, prm[n]), kernel_view(n, mom[n]), kernel_view(n, var[n])

    dw_shard = prm["conv_w_dw"].reshape(N_LAYERS, CONV_KERNEL, -1)
    casts = {"w_in": _cast_into("w_in", prm["w_in"], place, MXU_DTYPE)}
    first, first_started = _allgather_start("first", [[casts["w_in"][0]]])
    in_flight = {(0, "in"): first[0]}
    mixer = GATHER_GROUPS["mixer"]
    casts.update(zip(mixer, _cast_small_into(
        "mixer", [dw_shard if n == "conv_w_dw" else prm[n] for n in mixer],
        [F32 if n == "conv_w_dw" else MXU_DTYPE for n in mixer], place, after=(first_started,))))
    casts.update({n: _cast_into(n, kernel_view(n, prm[n]), place, MXU_DTYPE, after=(first_started,))
                  for n in GATHER_GROUPS["ffn"]})
    order = [(l, g) for l in range(N_LAYERS) for g in GATHER_GROUPS if (l, g) != (0, "in")]
    rest, rest_started = _allgather_start("rest", [[casts[n][l] for n in GATHER_GROUPS[g]] for l, g in order])
    in_flight.update(zip(order, rest))

    arrived = {}

    def weights_of(l, group, after):
        if (l, group) in arrived:
            return arrived.pop((l, group))
        tag = f"l{l}_{group}"
        if (l, group) == (0, "in"):
            after = after + (rest_started,)
        groups = (group, "mixer") if (l > 0 and group == "in") else (group,)
        waited = [_allgather_wait(f"l{l}_{g}", *in_flight[l, g][:2], in_flight[l, g][2], after) for g in groups]
        bufs = _allgather_forward(tag, [b for w in waited for b in w])
        for g in groups:
            fw = dict(zip(GATHER_GROUPS[g], bufs[:len(GATHER_GROUPS[g])]))
            bufs = bufs[len(GATHER_GROUPS[g]):]
            if "conv_w_dw" in fw:
                fw["conv_w_dw"] = fw["conv_w_dw"].transpose(1, 0, 2).reshape(CONV_KERNEL, -1)
            arrived[l, g] = fw
        return arrived.pop((l, group))

    pending, small_pending, small_shapes = {}, {}, {}
    tokens = {}

    def on_grads(l, group, grads):
        if group == "small":
            packed = {n: g for n, g in grads.items() if n not in MID}
            small_shapes[l] = {n: g.shape for n, g in packed.items()}
            begun = _allgather_rows_start(f"l{l}", [_pack_rows(list(packed.values()))] + [grads[n] for n in MID])
            small_pending[l], token = begun[:4], begun[4]
        else:
            pending[l, group], token = _reduce_start(f"{l}_{group}", grads)
        tokens[l, group] = token
        return token[0, 0]

    loss, dx, _, _, _ = _local_step(x[0], loss_target[0], weights_of, prm, place, on_grads)
    loss = lax.psum(loss, ("x", "y", "c"))

    reduced = [{} for _ in range(N_LAYERS)]
    out = {}

    def finish(l, groups, after):
        reduced[l].update(_reduce_finish(f"l{l}_{groups[0]}", [pending[l, g] for g in groups], place, after))

    def adamw(l, names, done):
        small = [n for n in names if prm[n][0].size <= SMALL_GRAD_ELEMS]
        for n in names:
            if n not in small:
                out[n] = _adamw_big(n, l, kernel_view(n, prm[n]), kernel_view(n, mom[n]), kernel_view(n, var[n]),
                                    reduced[l][n], out.get(n), after=done)
                done = (out[n][0],)
        if small:
            res = _adamw_small_group("mixer", l, [prm[n] for n in small], [mom[n] for n in small],
                                     [var[n] for n in small], [reduced[l][n] for n in small],
                                     [out.get(n) for n in small], after=done)
            for i, n in enumerate(small):
                out[n] = tuple(res[4 * i:4 * i + 4])
            done = (res[0],)
        return done

    top = N_LAYERS - 1
    done = (tokens[0, "in"], tokens[0, "small"])
    finish(top, ("ffn", "mixer", "in"), done)
    done = adamw(top, BIG, done)
    for groups in (("ffn", "mixer"), ("in",)):
        finish(0, groups, done)
        done = adamw(0, [n for g in groups for n in GATHER_GROUPS[g] if n in BIG], done)
    for n in BIG:
        out[n] = tuple(kernel_view(n, a) for a in out[n])

    gsmall = {}
    mid_mine, mid_gathered = [], []
    for l in range(N_LAYERS):
        mine, lands = _allgather_rows_wait(f"l{l}", *small_pending[l], done[0])
        lands = _allgather_rows_forward(f"l{l}", lands)
        mid_mine.append(mine[1:])
        mid_gathered.append(lands[1:])
        gsum = _sum_devices(f"l{l}", lands[0], mine[0], place)
        for n, g in zip(small_shapes[l], _unpack_rows(gsum, list(small_shapes[l].values()))):
            gsmall.setdefault(n, [None] * N_LAYERS)[l] = g
    mid_out = _adamw_mid([prm[n] for n in MID], [mom[n] for n in MID], [var[n] for n in MID], mid_gathered, mid_mine,
                         place)
    for n, res in zip(MID, mid_out):
        out[n] = tuple(kernel_view(n, a) for a in res)
    gsmall = {n: (g[top] if n == "final_norm" else jnp.stack(g)) for n, g in gsmall.items()}
    lanes = dw_shard.shape[-1]
    gsmall["conv_w_dw"] = lax.dynamic_slice_in_dim(gsmall["conv_w_dw"], (2 * cx + cy) * lanes, lanes, axis=2)
    small_names = [n for n in SMALL if n not in MID] + ["conv_w_dw"]
    w_rows = _pack_rows([prm[n] for n in small_names])
    m_rows = _pack_rows([mom[n] for n in small_names])
    v_rows = _pack_rows([var[n] for n in small_names])
    g_rows = _pack_rows([gsmall[n] for n in small_names])
    shapes = [prm[n].shape for n in small_names]
    d_s, m_s, v_s = (_unpack_rows(r, shapes) for r in _adamw_rows(w_rows, m_rows, v_rows, g_rows))
    for i, n in enumerate(small_names):
        out[n] = (gsmall[n].reshape(prm[n].shape), d_s[i], m_s[i], v_s[i])
    grads = [out[n][0] for n in WEIGHTS]
    deltas = [out[n][1] for n in WEIGHTS]
    new_m = [out[n][2] for n in WEIGHTS]
    new_v = [out[n][3] for n in WEIGHTS]
    return (loss, dx[None], *grads, *deltas, *new_m, *new_v)
```

```python
import math

import jax
import jax.numpy as jnp
from jax import lax
from jax.experimental import pallas as pl
from jax.experimental.pallas import tpu as pltpu

F32 = jnp.float32
MXU_DTYPE = jnp.bfloat16
WIRE_DTYPE = jnp.bfloat16
SDS = jax.ShapeDtypeStruct
BS = pl.BlockSpec
ANY = pl.BlockSpec(memory_space=pl.ANY)
HBM = pl.BlockSpec(memory_space=pltpu.HBM)
SEM = pl.BlockSpec(memory_space=pltpu.SEMAPHORE)
SIDE_EFFECT = pltpu.SideEffectType.DATAFLOW_SIDE_EFFECTING
MESH = pl.DeviceIdType.MESH

EPS = 1e-6
N_CHIPS = 4
N_LAYERS = 2
SSM_GROUPS, SSM_STATE, SSM_GROUP = 32, 64, 16
CONV_KERNEL = 31
CONV_PAD = 32
POOL_WINDOWS = (2, 4, 8, 16)
GELU_C = math.sqrt(2.0 / math.pi)
ADAM_LR, ADAM_B1, ADAM_B2, ADAM_EPS, ADAM_WD, ADAM_STEP = 0.001, 0.9, 0.999, 1e-08, 0.01, 10
VMEM_LIMIT = 56 * 1024 * 1024

BIG = ("w_in", "ssm_w_glu", "ssm_w_proj", "conv_w_proj", "pool_w_proj", "w_out", "ffn_w_gate", "ffn_w_up", "ffn_w_down")
TRANSPOSED = ("ffn_w_gate", "ffn_w_up")
MID = ("ssm_b_re", "ssm_b_im", "ssm_c_re", "ssm_c_im")
GATHER_GROUPS = {
    "in": ("w_in",),
    "mixer": ("ssm_w_glu", "ssm_w_proj", "conv_w_proj", "pool_w_proj", "w_out", "conv_w_dw"),
    "ffn": ("ffn_w_gate", "ffn_w_up", "ffn_w_down"),
}
SMALL = ("norm1", "b_gate", "ssm_a_re", "ssm_a_im", "ssm_log_dt", "ssm_b_re", "ssm_b_im", "ssm_c_re", "ssm_c_im",
         "ssm_d", "ssm_b_glu", "conv_b_dw", "conv_ln_g", "conv_ln_b", "pool_w_group", "pool_scale", "norm2",
         "final_norm")
WEIGHTS = ("norm1", "w_in", "b_gate", "ssm_a_re", "ssm_a_im", "ssm_log_dt", "ssm_b_re", "ssm_b_im", "ssm_c_re",
           "ssm_c_im", "ssm_d", "ssm_w_glu", "ssm_b_glu", "ssm_w_proj", "conv_w_dw", "conv_b_dw", "conv_ln_g",
           "conv_ln_b", "conv_w_proj", "pool_w_group", "pool_scale", "pool_w_proj", "w_out", "norm2", "ffn_w_gate",
           "ffn_w_up", "ffn_w_down", "final_norm")


def _params():
    return pltpu.CompilerParams(vmem_limit_bytes=VMEM_LIMIT)


def _mm(a, b):
    return jnp.dot(a.astype(MXU_DTYPE), b.astype(MXU_DTYPE), preferred_element_type=F32)


def _mm_nt(a, b):
    return lax.dot_general(a.astype(MXU_DTYPE), b.astype(MXU_DTYPE), (((1,), (1,)), ((), ())),
                           preferred_element_type=F32)


def _mm_tn(a, b):
    return lax.dot_general(a.astype(MXU_DTYPE), b.astype(MXU_DTYPE), (((0,), (0,)), ((), ())),
                           preferred_element_type=F32)


def _sigmoid(x):
    return jax.nn.sigmoid(x)


def _gelu(x):
    t = jnp.tanh(GELU_C * (x + 0.044715 * (x * x * x)))
    return x * (0.5 * (1.0 + t)), t


def _gelu_grad(x, t):
    return 0.5 * (1.0 + t) + 0.5 * x * (1.0 - t * t) * (GELU_C * (1.0 + 3.0 * 0.044715 * x * x))


def _colsum(v):
    return jnp.sum(v, axis=0, keepdims=True)


def _row_tile(rows, cols, itemsize=4, budget=1536 * 1024):
    best = None
    for t in range(8, rows + 1, 8):
        if rows % t == 0 and t * cols * itemsize <= budget:
            best = t
    return best if best is not None else rows


def _in_proj(l, x, norm1, w_in):
    s, d = x.shape
    nc = w_in.shape[-1]
    tm = min(1024, s)
    nt = s // tm

    def body(x_ref, g_ref, w_ref, z_ref, h_ref, h_all):
        i = pl.program_id(1)
        rows = pl.ds(pl.multiple_of(i * tm, tm), tm)

        @pl.when(pl.program_id(0) == 0)
        def _():
            xv = x_ref[...]
            r = lax.rsqrt(jnp.mean(xv * xv, axis=-1, keepdims=True) + EPS)
            hv = (xv * r * g_ref[...]).astype(h_ref.dtype)
            h_ref[...] = hv.T
            h_all[rows, :] = hv

        z_ref[...] = _mm(h_all[rows, :], w_ref[...])

    tile_of = lambda j, i: i * (1 - jnp.minimum(j, 1)) + (nt - 1) * jnp.minimum(j, 1)
    return pl.pallas_call(
        body, name=f"in_proj_l{l}", grid=(N_CHIPS, nt),
        in_specs=[BS((tm, d), lambda j, i: (tile_of(j, i), 0)), BS((None, 1, d), lambda j, i: (l, 0, 0)),
                  BS((None, d, nc), lambda j, i: (j, 0, 0))],
        out_specs=[BS((tm, nc), lambda j, i: (i, j)), BS((d, tm), lambda j, i: (0, tile_of(j, i)))],
        out_shape=[SDS((s, N_CHIPS * nc), F32), SDS((d, s), MXU_DTYPE)],
        scratch_shapes=[pltpu.VMEM((s, d), MXU_DTYPE)], compiler_params=_params())(x, norm1, w_in)


def _mm_cols(a, w_ref):
    return jnp.concatenate([_mm(a, w_ref[j]) for j in range(N_CHIPS)], axis=1)


def _mm_nt_cols(dv, w_ref):
    nc = w_ref.shape[-1]
    acc = _mm_nt(dv[:, 0:nc], w_ref[0])
    for j in range(1, N_CHIPS):
        acc = acc + _mm_nt(dv[:, j * nc:(j + 1) * nc], w_ref[j])
    return acc


def _merge_values(y, hc, p, zg, wglu, bglu, wpa, wpb, wpc, lng, lnb, wgrp, scale, bg):
    v = {}
    ge, th = _gelu(y)
    t = _mm(ge, wglu) + bglu
    sg = _sigmoid(t)
    sa = ge * sg
    ya = _mm_cols(sa, wpa)
    mu = jnp.mean(hc, axis=-1, keepdims=True)
    xc = hc - mu
    r = lax.rsqrt(jnp.mean(xc * xc, axis=-1, keepdims=True) + EPS)
    xh = xc * r
    ln = xh * lng + lnb
    sl = _sigmoid(ln)
    ac = ln * sl
    yb = _mm_cols(ac, wpb)
    gw = p.shape[1] // len(POOL_WINDOWS)
    q = jnp.concatenate([_mm(p[:, k * gw:(k + 1) * gw], wgrp[k]) for k in range(len(POOL_WINDOWS))], axis=1)
    pp = q * scale
    yc = _mm_cols(pp, wpc)
    d = ya.shape[1]
    gates = [_sigmoid(zg[k] + bg[:, k * d:(k + 1) * d]) for k in range(3)]
    merged = gates[0] * ya + gates[1] * yb + gates[2] * yc
    v.update(ge=ge, th=th, sg=sg, sa=sa, ya=ya, r=r, xh=xh, ln=ln, sl=sl, ac=ac, yb=yb, q=q, pp=pp, yc=yc,
             gates=gates, merged=merged)
    return v


def _merge_specs(l, tm, d, cw):
    row = lambda n: BS((None, 1, n), lambda i: (l, 0, 0))
    resident = lambda shp: BS(shp, lambda i: (0, 0, 0), pipeline_mode=pl.Buffered(1))
    return [
        BS((tm, cw), lambda i: (i, 0)),
        BS((tm, cw), lambda i: (i, 0)),
        BS((tm, cw), lambda i: (i, 0)),
        BS((tm, d), lambda i: (i, 2)), BS((tm, d), lambda i: (i, 3)), BS((tm, d), lambda i: (i, 4)),
        resident((N_CHIPS, cw // N_CHIPS, cw)),
        row(cw),
        resident((N_CHIPS, cw, d // N_CHIPS)),
        resident((N_CHIPS, cw, d // N_CHIPS)),
        resident((N_CHIPS, cw, d // N_CHIPS)),
        row(cw), row(cw),
        BS((None, 4, cw // 4, cw // 4), lambda i: (l, 0, 0, 0)),
        row(cw),
        row(3 * d),
        resident((N_CHIPS, d // N_CHIPS, d)),
    ]


def _merge_fwd(l, x, y, hc, p, z, fw, sp):
    s, d = x.shape
    cw = y.shape[1]
    tm = min(512, s)

    def body(x_ref, y_ref, hc_ref, p_ref, z0, z1, z2, wglu, bglu, wpa, wpb, wpc, lng, lnb, wgrp, scale, bg, wout,
             x1_ref):
        v = _merge_values(y_ref[...], hc_ref[...], p_ref[...], (z0[...], z1[...], z2[...]),
                          wglu[...].reshape(cw, cw), bglu[...], wpa, wpb, wpc, lng[...], lnb[...], wgrp, scale[...],
                          bg[...])
        x1_ref[...] = x_ref[...] + _mm(v["merged"], wout[...].reshape(d, d))

    return pl.pallas_call(
        body, name=f"merge_fwd_l{l}", grid=(s // tm,),
        in_specs=[BS((tm, d), lambda i: (i, 0))] + _merge_specs(l, tm, d, cw),
        out_specs=BS((tm, d), lambda i: (i, 0)), out_shape=SDS((s, d), F32), compiler_params=_params(),
    )(x, y, hc, p, z, z, z, fw["ssm_w_glu"], sp["ssm_b_glu"], fw["ssm_w_proj"], fw["conv_w_proj"], fw["pool_w_proj"],
      sp["conv_ln_g"], sp["conv_ln_b"], sp["pool_w_group"], sp["pool_scale"], sp["b_gate"], fw["w_out"])


def _merge_bwd(l, dx1, y, hc, p, z, fw, sp):
    s, d = dx1.shape
    cw = y.shape[1]
    tm = min(256, s)
    m = MXU_DTYPE

    def body(dx1_ref, y_ref, hc_ref, p_ref, z0, z1, z2, wglu, bglu, wpa, wpb, wpc, lng, lnb, wgrp, scale, bg, wout,
             dzg_ref, dy_ref, dhc_ref, dp_ref, merged_ref, sa_ref, ac_ref, pp_ref, ge_ref, dt_ref, dya_ref, dyb_ref,
             dyc_ref, dq_ref, dbg_ref, dbglu_ref, dlng_ref, dlnb_ref, dscale_ref):
        yv = y_ref[...]
        wg = wglu[...].reshape(cw, cw)
        v = _merge_values(yv, hc_ref[...], p_ref[...], (z0[...], z1[...], z2[...]), wg, bglu[...], wpa, wpb, wpc,
                          lng[...], lnb[...], wgrp, scale[...], bg[...])
        dm = _mm_nt(dx1_ref[...], wout[...].reshape(d, d))
        ys = (v["ya"], v["yb"], v["yc"])
        dys, dbg = [], []
        for k in range(3):
            gk = v["gates"][k]
            dzk = dm * ys[k] * (gk * (1.0 - gk))
            dbg.append(_colsum(dzk))
            dzg_ref[:, k * d:(k + 1) * d] = dzk.astype(m)
            dys.append((dm * gk).astype(m))
        dsa = _mm_nt_cols(dys[0], wpa)
        dac = _mm_nt_cols(dys[1], wpb)
        dpp = _mm_nt_cols(dys[2], wpc)
        ge, sg = v["ge"], v["sg"]
        dt = dsa * ge * (sg * (1.0 - sg))
        dge = dsa * sg + _mm_nt(dt, wg)
        dy_ref[...] = dge * _gelu_grad(yv, v["th"])
        ln, sl, xh = v["ln"], v["sl"], v["xh"]
        dln = dac * (sl * (1.0 + ln * (1.0 - sl)))
        dxh = dln * lng[...]
        dhc_ref[...] = v["r"] * (dxh - jnp.mean(dxh, axis=-1, keepdims=True)
                                 - xh * jnp.mean(dxh * xh, axis=-1, keepdims=True))
        dq = dpp * scale[...]
        gw = cw // len(POOL_WINDOWS)
        for k in range(len(POOL_WINDOWS)):
            dp_ref[:, k * gw:(k + 1) * gw] = _mm_nt(dq[:, k * gw:(k + 1) * gw], wgrp[k])
        merged_ref[...] = v["merged"].astype(m)
        sa_ref[...] = v["sa"].astype(m)
        ac_ref[...] = v["ac"].astype(m)
        pp_ref[...] = v["pp"].astype(m)
        ge_ref[...] = ge.astype(m)
        dt_ref[...] = dt.astype(m)
        dya_ref[...] = dys[0]
        dyb_ref[...] = dys[1]
        dyc_ref[...] = dys[2]
        dq_ref[...] = dq.astype(m)

        @pl.when(pl.program_id(0) == 0)
        def _():
            for ref in (dbg_ref, dbglu_ref, dlng_ref, dlnb_ref, dscale_ref):
                ref[...] = jnp.zeros(ref.shape, F32)

        dbg_ref[...] += jnp.concatenate(dbg, axis=1)
        dbglu_ref[...] += _colsum(dt)
        dlng_ref[...] += _colsum(dln * xh)
        dlnb_ref[...] += _colsum(dln)
        dscale_ref[...] += _colsum(dpp * v["q"])

    tile = lambda n: BS((tm, n), lambda i: (i, 0))
    acc = lambda n: BS((1, n), lambda i: (0, 0))
    outs = pl.pallas_call(
        body, name=f"merge_bwd_l{l}", grid=(s // tm,),
        in_specs=[tile(d)] + _merge_specs(l, tm, d, cw),
        out_specs=[tile(3 * d), tile(cw), tile(cw), tile(cw), tile(d), tile(cw), tile(cw), tile(cw), tile(cw), tile(cw),
                   tile(d), tile(d), tile(d), tile(cw), acc(3 * d), acc(cw), acc(cw), acc(cw), acc(cw)],
        out_shape=[SDS((s, 3 * d), m), SDS((s, cw), F32), SDS((s, cw), F32), SDS((s, cw), F32), SDS((s, d), m),
                   SDS((s, cw), m), SDS((s, cw), m), SDS((s, cw), m), SDS((s, cw), m), SDS((s, cw), m), SDS((s, d), m),
                   SDS((s, d), m), SDS((s, d), m), SDS((s, cw), m), SDS((1, 3 * d), F32), SDS((1, cw), F32),
                   SDS((1, cw), F32), SDS((1, cw), F32), SDS((1, cw), F32)],
        compiler_params=_params(),
    )(dx1, y, hc, p, z, z, z, fw["ssm_w_glu"], sp["ssm_b_glu"], fw["ssm_w_proj"], fw["conv_w_proj"], fw["pool_w_proj"],
      sp["conv_ln_g"], sp["conv_ln_b"], sp["pool_w_group"], sp["pool_scale"], sp["b_gate"], fw["w_out"])
    names = ("dzg", "dy", "dhc", "dp", "merged", "sa", "ac", "pp", "ge", "dt", "dya", "dyb", "dyc", "dq", "db_gate",
             "db_glu", "dln_g", "dln_b", "dscale")
    return dict(zip(names, outs))


def _ffn_fwd(l, x1, norm2, wg, wu, wd):
    s, d = x1.shape
    hc = wd.shape[1]
    tm = min(1024, s)

    def body(x_ref, g_ref, wg_ref, wu_ref, wd_ref, o_ref, gate_ref, up_ref, h_ref):
        @pl.when(pl.program_id(1) == 0)
        def _():
            xv = x_ref[...]
            r = lax.rsqrt(jnp.mean(xv * xv, axis=-1, keepdims=True) + EPS)
            h_ref[...] = (xv * r * g_ref[...]).astype(h_ref.dtype)
            o_ref[...] = xv

        h = h_ref[...]
        gate = _mm_nt(h, wg_ref[...])
        up = _mm_nt(h, wu_ref[...])
        gate_ref[...] = gate
        up_ref[...] = up
        o_ref[...] += _mm(gate * _sigmoid(gate) * up, wd_ref[...])

    chunk = BS((None, tm, hc), lambda i, j: (j, i, 0))
    return pl.pallas_call(
        body, name=f"ffn_fwd_l{l}", grid=(s // tm, N_CHIPS),
        in_specs=[BS((tm, d), lambda i, j: (i, 0)), BS((None, 1, d), lambda i, j: (l, 0, 0)),
                  BS((None, hc, d), lambda i, j: (j, 0, 0)), BS((None, hc, d), lambda i, j: (j, 0, 0)),
                  BS((None, hc, d), lambda i, j: (j, 0, 0))],
        out_specs=[BS((tm, d), lambda i, j: (i, 0)), chunk, chunk, BS((tm, d), lambda i, j: (i, 0))],
        out_shape=[SDS((s, d), F32), SDS((N_CHIPS, s, hc), F32), SDS((N_CHIPS, s, hc), F32), SDS((s, d), MXU_DTYPE)],
        compiler_params=_params())(x1, norm2, wg, wu, wd)


def _ffn_bwd(l, x1, dx2, gate_pre, up_pre, norm2, wg, wu, wd):
    s, d = x1.shape
    hc = wd.shape[1]
    tm = min(512, s)
    m = MXU_DTYPE
    last = N_CHIPS - 1

    def body(x_ref, dx2_ref, gate_ref, up_ref, g_ref, wg_ref, wu_ref, wd_ref, dx1_ref, dxb_ref, act_ref, dgate_ref,
             dup_ref, dn_ref, dh_scr):
        i, j = pl.program_id(0), pl.program_id(1)

        @pl.when(j == 0)
        def _():
            dxb_ref[...] = dx2_ref[...].astype(m)
            dh_scr[...] = jnp.zeros(dh_scr.shape, F32)

        @pl.when((i == 0) & (j == 0))
        def _():
            dn_ref[...] = jnp.zeros(dn_ref.shape, F32)

        gate = gate_ref[...]
        up = up_ref[...]
        sg = _sigmoid(gate)
        silu = gate * sg
        act_ref[...] = (silu * up).astype(m).T
        dact = _mm_nt(dxb_ref[...], wd_ref[...])
        dup = (dact * silu).astype(m)
        dgate = (dact * up * (sg * (1.0 + gate * (1.0 - sg)))).astype(m)
        dup_ref[...] = dup.T
        dgate_ref[...] = dgate.T
        dh_scr[...] += _mm(dgate, wg_ref[...]) + _mm(dup, wu_ref[...])

        @pl.when(j == last)
        def _():
            xv = x_ref[...]
            r = lax.rsqrt(jnp.mean(xv * xv, axis=-1, keepdims=True) + EPS)
            xh = xv * r
            dh = dh_scr[...]
            dn_ref[...] += _colsum(dh * xh)
            dxh = dh * g_ref[...]
            dx1_ref[...] = dx2_ref[...] + r * (dxh - xh * jnp.mean(dxh * xh, axis=-1, keepdims=True))

    chunk = BS((None, hc, tm), lambda i, j: (j, 0, i))
    saved = BS((None, tm, hc), lambda i, j: (j, i, 0))
    outs = pl.pallas_call(
        body, name=f"ffn_bwd_l{l}", grid=(s // tm, N_CHIPS),
        in_specs=[BS((tm, d), lambda i, j: (i, 0)), BS((tm, d), lambda i, j: (i, 0)), saved, saved,
                  BS((None, 1, d), lambda i, j: (l, 0, 0)),
                  BS((None, hc, d), lambda i, j: (j, 0, 0)), BS((None, hc, d), lambda i, j: (j, 0, 0)),
                  BS((None, hc, d), lambda i, j: (j, 0, 0))],
        out_specs=[BS((tm, d), lambda i, j: (i, 0)), BS((tm, d), lambda i, j: (i, 0)),
                   chunk, chunk, chunk, BS((1, d), lambda i, j: (0, 0))],
        out_shape=[SDS((s, d), F32), SDS((s, d), m), SDS((N_CHIPS, hc, s), m),
                   SDS((N_CHIPS, hc, s), m), SDS((N_CHIPS, hc, s), m), SDS((1, d), F32)],
        scratch_shapes=[pltpu.VMEM((tm, d), F32)], compiler_params=_params(),
    )(x1, dx2, gate_pre, up_pre, norm2, wg, wu, wd)
    return dict(zip(("dx1", "dx2", "act", "dgate", "dup", "dnorm2"), outs))


def _loss_head(x, target, gf):
    s, d = x.shape
    tm = min(512, s)

    def body(x_ref, t_ref, g_ref, dx_ref, loss_ref, dg_ref):
        @pl.when(pl.program_id(0) == 0)
        def _():
            loss_ref[...] = jnp.zeros(loss_ref.shape, F32)
            dg_ref[...] = jnp.zeros(dg_ref.shape, F32)

        xv = x_ref[...]
        r = lax.rsqrt(jnp.mean(xv * xv, axis=-1, keepdims=True) + EPS)
        xh = xv * r
        err = xh * g_ref[...] - t_ref[...]
        loss_ref[...] += 0.5 * jnp.sum(jnp.mean(err * err, axis=-1, keepdims=True), axis=0, keepdims=True)
        dyv = err * (1.0 / d)
        dg_ref[...] += _colsum(dyv * xh)
        dxh = dyv * g_ref[...]
        dx_ref[...] = r * (dxh - xh * jnp.mean(dxh * xh, axis=-1, keepdims=True))

    return pl.pallas_call(
        body, name="loss_head", grid=(s // tm,),
        in_specs=[BS((tm, d), lambda i: (i, 0)), BS((tm, d), lambda i: (i, 0)), BS((1, d), lambda i: (0, 0))],
        out_specs=[BS((tm, d), lambda i: (i, 0)), BS((1, 1), lambda i: (0, 0)), BS((1, d), lambda i: (0, 0))],
        out_shape=[SDS((s, d), F32), SDS((1, 1), F32), SDS((1, d), F32)], compiler_params=_params())(x, target, gf)


def _in_proj_bwd(l, dres, x, norm1, w_in, du_a, dv1, dv2, du_c, dzg):
    s, d = x.shape
    nc = w_in.shape[-1]
    tm = min(256, s)
    m = MXU_DTYPE

    def body(dres_ref, x_ref, g_ref, w_ref, a_ref, b1_ref, b2_ref, c_ref, g3_ref, dx_ref, dz_ref, dn_ref):
        @pl.when(pl.program_id(0) == 0)
        def _():
            dn_ref[...] = jnp.zeros(dn_ref.shape, F32)

        dz = jnp.concatenate([a_ref[...], b1_ref[...], b2_ref[...], c_ref[...], g3_ref[...]], axis=1).astype(m)
        dz_ref[...] = dz
        dh = _mm_nt_cols(dz, w_ref)
        xv = x_ref[...]
        r = lax.rsqrt(jnp.mean(xv * xv, axis=-1, keepdims=True) + EPS)
        xh = xv * r
        dn_ref[...] += _colsum(dh * xh)
        dxh = dh * g_ref[...]
        dx_ref[...] = dres_ref[...] + r * (dxh - xh * jnp.mean(dxh * xh, axis=-1, keepdims=True))

    tile = lambda n: BS((tm, n), lambda i: (i, 0))
    return pl.pallas_call(
        body, name=f"in_proj_bwd_l{l}", grid=(s // tm,),
        in_specs=[tile(d), tile(d), BS((None, 1, d), lambda i: (l, 0, 0)),
                  BS((N_CHIPS, d, nc), lambda i: (0, 0, 0), pipeline_mode=pl.Buffered(1)),
                  tile(du_a.shape[1]), tile(dv1.shape[1]), tile(dv2.shape[1]), tile(du_c.shape[1]), tile(dzg.shape[1])],
        out_specs=[tile(d), tile(N_CHIPS * nc), BS((1, d), lambda i: (0, 0))],
        out_shape=[SDS((s, d), F32), SDS((s, N_CHIPS * nc), m), SDS((1, d), F32)], compiler_params=_params(),
    )(dres, x, norm1, w_in, du_a, dv1, dv2, du_c, dzg)


def _tn_matmul(name, a, a_spec, b, b_spec, chunk_shape, grid, place):
    last = grid[1] - 1

    def body(place_ref, a_ref, b_ref, own_ref, wire_ref, *acc):
        part = _mm(a_ref[...], b_ref[...])

        def emit(total):
            wire_ref[...] = total.astype(WIRE_DTYPE)

            @pl.when(pl.program_id(0) == place_ref[0])
            def _():
                own_ref[...] = total

        if last == 0:
            emit(part)
        else:
            @pl.when(pl.program_id(1) == 0)
            def _():
                acc[0][...] = part

            @pl.when(pl.program_id(1) > 0)
            def _():
                acc[0][...] += part

            @pl.when(pl.program_id(1) == last)
            def _():
                emit(acc[0][...])

    zeros = (0,) * len(chunk_shape)
    return pl.pallas_call(
        body, name=name,
        grid_spec=pltpu.PrefetchScalarGridSpec(
            num_scalar_prefetch=1, grid=grid, in_specs=[a_spec, b_spec],
            out_specs=[BS(chunk_shape, lambda j, t, pr: zeros), BS((None,) + chunk_shape, lambda j, t, pr: (j,) + zeros)],
            scratch_shapes=[pltpu.VMEM(chunk_shape, F32)] if last else []),
        out_shape=[SDS(chunk_shape, F32), SDS((N_CHIPS,) + chunk_shape, WIRE_DTYPE)],
        compiler_params=_params())(place, a, b)


def _scan_consts(pw_ref, lanes, reverse):
    sgn = -1.0 if reverse else 1.0
    row = lax.broadcasted_iota(jnp.int32, (8, lanes), 0)
    steps = []
    for i, k in enumerate((1, 2, 4)):
        mask = (row < 8 - k) if reverse else (row >= k)
        steps.append((k, jnp.where(mask, pw_ref[2 * i], 0.0), jnp.where(mask, sgn * pw_ref[2 * i + 1], 0.0)))
    c = 4 if reverse else 3
    return steps, pw_ref[2 * c], sgn * pw_ref[2 * c + 1]


def _scan_block(br, bi, steps, reverse):
    for k, ar, ai in steps:
        sh = 8 - k if reverse else k
        sr = pltpu.roll(br, sh, 0)
        si = pltpu.roll(bi, sh, 0)
        br, bi = br + ar * sr - ai * si, bi + ar * si + ai * sr
    return br, bi


SSM_BLOCK_GROUPS = 8


def _ssm_fwd(l, z, b2_re, b2_im, c2_re, c2_im, pw, dskip):
    s = z.shape[0]
    gc = SSM_BLOCK_GROUPS * SSM_GROUP
    gl = SSM_BLOCK_GROUPS * SSM_STATE
    nblk = b2_re.shape[0] // gc
    tw = b2_re.shape[1]

    def body(u_ref, bre2, bim2, cre2, cim2, pw_ref, d_ref, hre, him, y_ref):
        u = u_ref[...]
        hre[...] = _mm(u, _block_matrix(bre2, gl))
        him[...] = _mm(u, _block_matrix(bim2, gl))
        steps, car, cai = _scan_consts(pw_ref, gl, False)

        def step(i, carry):
            cr, ci = carry
            r0 = pl.multiple_of(i * 8, 8)
            br, bi = _scan_block(hre[pl.ds(r0, 8), :], him[pl.ds(r0, 8), :], steps, False)
            hr = br + car * cr - cai * ci
            hi = bi + car * ci + cai * cr
            hre[pl.ds(r0, 8), :] = hr
            him[pl.ds(r0, 8), :] = hi
            return jnp.broadcast_to(hr[7:8, :], (8, gl)), jnp.broadcast_to(hi[7:8, :], (8, gl))

        zero = jnp.zeros((8, gl), F32)
        lax.fori_loop(0, s // 8, step, (zero, zero))
        y_ref[...] = (_mm_nt(hre[...], _block_matrix(cre2, gl)) - _mm_nt(him[...], _block_matrix(cim2, gl))
                      + d_ref[...] * u)

    twice = BS((gc, tw), lambda k: (k, 0))
    return pl.pallas_call(
        body, name=f"ssm_fwd_l{l}", grid=(nblk,),
        in_specs=[BS((s, gc), lambda k: (0, k)), twice, twice, twice, twice, BS((10, 8, gl), lambda k: (0, 0, k)),
                  BS((1, gc), lambda k: (0, k))],
        out_specs=[BS((s, gl), lambda k: (0, k)), BS((s, gl), lambda k: (0, k)), BS((s, gc), lambda k: (0, k))],
        out_shape=[SDS((s, nblk * gl), F32), SDS((s, nblk * gl), F32), SDS((s, nblk * gc), F32)],
        compiler_params=_params())(z, b2_re, b2_im, c2_re, c2_im, pw, dskip)


def _ssm_bwd(l, dy, z, hre, him, b2_re, b2_im, c2_re, c2_im, pw, dskip):
    s = z.shape[0]
    gc = SSM_BLOCK_GROUPS * SSM_GROUP
    gl = SSM_BLOCK_GROUPS * SSM_STATE
    nblk = b2_re.shape[0] // gc
    tw = b2_re.shape[1]

    def body(dy_ref, u_ref, hre_ref, him_ref, bre2, bim2, cre2, cim2, pw_ref, d_ref,
             du_ref, dbre_ref, dbim_ref, dcre_ref, dcim_ref, dar_ref, dai_ref, dd_ref, gre, gim):
        dyv = dy_ref[...]
        u = u_ref[...]
        gre[...] = _mm(dyv, _block_matrix(cre2, gl))
        gim[...] = -_mm(dyv, _block_matrix(cim2, gl))
        dcre_ref[...] = _block_diagonal_of(_mm_tn(dyv, hre_ref[...]))
        dcim_ref[...] = -_block_diagonal_of(_mm_tn(dyv, him_ref[...]))
        dd_ref[...] = _colsum(dyv * u)
        row = lax.broadcasted_iota(jnp.int32, (8, gl), 0)
        steps, car, cai = _scan_consts(pw_ref, gl, True)
        n8 = s // 8

        def step(ii, carry):
            cr, ci, accr, acci = carry
            i = n8 - 1 - ii
            r0 = pl.multiple_of(i * 8, 8)
            br, bi = _scan_block(gre[pl.ds(r0, 8), :], gim[pl.ds(r0, 8), :], steps, True)
            dr = br + car * cr - cai * ci
            di = bi + car * ci + cai * cr
            gre[pl.ds(r0, 8), :] = dr
            gim[pl.ds(r0, 8), :] = di
            rp = pl.multiple_of(jnp.maximum(i - 1, 0) * 8, 8)
            keep = jnp.where(i > 0, 1.0, 0.0)
            pr = jnp.where(row >= 1, pltpu.roll(hre_ref[pl.ds(r0, 8), :], 1, 0),
                           keep * pltpu.roll(hre_ref[pl.ds(rp, 8), :], 1, 0))
            pi = jnp.where(row >= 1, pltpu.roll(him_ref[pl.ds(r0, 8), :], 1, 0),
                           keep * pltpu.roll(him_ref[pl.ds(rp, 8), :], 1, 0))
            accr = accr + dr * pr + di * pi
            acci = acci + di * pr - dr * pi
            return (jnp.broadcast_to(dr[0:1, :], (8, gl)), jnp.broadcast_to(di[0:1, :], (8, gl)), accr, acci)

        zero = jnp.zeros((8, gl), F32)
        _, _, accr, acci = lax.fori_loop(0, n8, step, (zero, zero, zero, zero))
        dar_ref[...] = _colsum(accr)
        dai_ref[...] = _colsum(acci)
        dbr = gre[...]
        dbi = gim[...]
        du_ref[...] = (dyv * d_ref[...] + _mm_nt(dbr, _block_matrix(bre2, gl))
                       + _mm_nt(dbi, _block_matrix(bim2, gl))).astype(du_ref.dtype)
        dbre_ref[...] = _block_diagonal_of(_mm_tn(u, dbr))
        dbim_ref[...] = _block_diagonal_of(_mm_tn(u, dbi))

    col = lambda n: BS((s, n), lambda k: (0, k))
    twice = BS((gc, tw), lambda k: (k, 0))
    diag = BS((gc, SSM_STATE), lambda k: (k, 0))
    outs = pl.pallas_call(
        body, name=f"ssm_bwd_l{l}", grid=(nblk,),
        in_specs=[col(gc), col(gc), col(gl), col(gl), twice, twice, twice, twice,
                  BS((10, 8, gl), lambda k: (0, 0, k)), BS((1, gc), lambda k: (0, k))],
        out_specs=[col(gc), diag, diag, diag, diag, BS((1, gl), lambda k: (0, k)),
                   BS((1, gl), lambda k: (0, k)), BS((1, gc), lambda k: (0, k))],
        out_shape=[SDS((s, nblk * gc), MXU_DTYPE)] + [SDS((nblk * gc, SSM_STATE), F32)] * 4
        + [SDS((1, nblk * gl), F32), SDS((1, nblk * gl), F32), SDS((1, nblk * gc), F32)],
        scratch_shapes=[pltpu.VMEM((s, gl), F32), pltpu.VMEM((s, gl), F32)], compiler_params=_params(),
    )(dy, z, hre, him, b2_re, b2_im, c2_re, c2_im, pw, dskip)
    return dict(zip(("du", "dbbar_re", "dbbar_im", "dc_re", "dc_im", "dabar_re", "dabar_im", "dd"), outs))


def _conv_fwd(l, z, wdw, bdw):
    s = z.shape[0]
    cw = wdw.shape[1]
    lb = 128
    tr = min(256, s)
    off1 = cw // lb
    off2 = 2 * cw // lb

    def body(v1_ref, v2_ref, w_ref, b_ref, hc_ref, scr):
        scr[0:CONV_PAD, :] = jnp.zeros((CONV_PAD, lb), F32)
        scr[CONV_PAD:, :] = v1_ref[...] * _sigmoid(v2_ref[...])
        for t in range(s // tr):
            acc = jnp.broadcast_to(b_ref[...], (tr, lb))
            for k in range(CONV_KERNEL):
                acc = acc + w_ref[pl.ds(k, 1), :] * scr[pl.ds(t * tr + CONV_PAD - (CONV_KERNEL - 1) + k, tr), :]
            hc_ref[pl.ds(t * tr, tr), :] = acc

    return pl.pallas_call(
        body, name=f"conv_fwd_l{l}", grid=(cw // lb,),
        in_specs=[BS((s, lb), lambda k: (0, off1 + k)), BS((s, lb), lambda k: (0, off2 + k)),
                  BS((CONV_KERNEL, lb), lambda k: (0, k)), BS((1, lb), lambda k: (0, k))],
        out_specs=BS((s, lb), lambda k: (0, k)), out_shape=SDS((s, cw), F32),
        scratch_shapes=[pltpu.VMEM((s + CONV_PAD, lb), F32)], compiler_params=_params())(z, z, wdw, bdw)


def _conv_bwd(l, dhc, z, wdw):
    s = z.shape[0]
    cw = wdw.shape[1]
    lb = 128
    tr = min(256, s)
    off1 = cw // lb
    off2 = 2 * cw // lb
    nb = cw // lb

    def body(d_ref, v1_ref, v2_ref, w_ref, dv1_ref, dv2_ref, dw_ref, db_ref, hpad, dpad):
        v1 = v1_ref[...]
        sg = _sigmoid(v2_ref[...])
        dv = d_ref[...]
        hpad[0:CONV_PAD, :] = jnp.zeros((CONV_PAD, lb), F32)
        hpad[CONV_PAD:, :] = v1 * sg
        dpad[0:s, :] = dv
        dpad[s:, :] = jnp.zeros((CONV_PAD, lb), F32)
        db_ref[...] = _colsum(dv)
        dws = [jnp.zeros((1, lb), F32) for _ in range(CONV_KERNEL)]
        for t in range(s // tr):
            dt = d_ref[pl.ds(t * tr, tr), :]
            acc = jnp.zeros((tr, lb), F32)
            for k in range(CONV_KERNEL):
                acc = acc + w_ref[pl.ds(k, 1), :] * dpad[pl.ds(t * tr + (CONV_KERNEL - 1) - k, tr), :]
                dws[k] = dws[k] + _colsum(dt * hpad[pl.ds(t * tr + CONV_PAD - (CONV_KERNEL - 1) + k, tr), :])
            sgt = _sigmoid(v2_ref[pl.ds(t * tr, tr), :])
            v1t = v1_ref[pl.ds(t * tr, tr), :]
            dv1_ref[pl.ds(t * tr, tr), :] = (acc * sgt).astype(dv1_ref.dtype)
            dv2_ref[pl.ds(t * tr, tr), :] = (acc * v1t * (sgt * (1.0 - sgt))).astype(dv2_ref.dtype)
        for k in range(CONV_KERNEL):
            dw_ref[pl.ds(k, 1), :] = dws[k]

    return pl.pallas_call(
        body, name=f"conv_bwd_l{l}", grid=(nb,),
        in_specs=[BS((s, lb), lambda k: (0, k)), BS((s, lb), lambda k: (0, off1 + k)),
                  BS((s, lb), lambda k: (0, off2 + k)), BS((CONV_KERNEL, lb), lambda k: (0, k))],
        out_specs=[BS((s, lb), lambda k: (0, k)), BS((s, lb), lambda k: (0, k)),
                   BS((CONV_KERNEL, lb), lambda k: (0, k)), BS((1, lb), lambda k: (0, k))],
        out_shape=[SDS((s, cw), MXU_DTYPE), SDS((s, cw), MXU_DTYPE), SDS((CONV_KERNEL, cw), F32), SDS((1, cw), F32)],
        scratch_shapes=[pltpu.VMEM((s + CONV_PAD, lb), F32), pltpu.VMEM((s + CONV_PAD, lb), F32)],
        compiler_params=_params())(dhc, z, z, wdw)


def _pool_window(k):
    return jnp.where(k == 0, float(POOL_WINDOWS[0]),
                     jnp.where(k == 1, float(POOL_WINDOWS[1]),
                               jnp.where(k == 2, float(POOL_WINDOWS[2]), float(POOL_WINDOWS[3]))))


def _pool_fwd(l, z, pw_width):
    s = z.shape[0]
    lb = pw_width // len(POOL_WINDOWS)
    off = 3 * pw_width // lb

    def body(u_ref, p_ref):
        k = pl.program_id(0)
        u = u_ref[...]
        row = lax.broadcasted_iota(jnp.int32, (s, lb), 0)
        sums = [u]
        for sh in (1, 2, 4, 8):
            prev = sums[-1]
            sums.append(prev + jnp.where(row >= sh, pltpu.roll(prev, sh, 0), 0.0))
        sel = jnp.where(k == 0, sums[1], jnp.where(k == 1, sums[2], jnp.where(k == 2, sums[3], sums[4])))
        cnt = jnp.minimum((row + 1).astype(F32), _pool_window(k))
        p_ref[...] = sel / cnt - u

    return pl.pallas_call(
        body, name=f"pool_fwd_l{l}", grid=(len(POOL_WINDOWS),),
        in_specs=[BS((s, lb), lambda k: (0, off + k))], out_specs=BS((s, lb), lambda k: (0, k)),
        out_shape=SDS((s, pw_width), F32), compiler_params=_params())(z)


def _pool_bwd(l, dp):
    s, width = dp.shape
    lb = width // len(POOL_WINDOWS)

    def body(d_ref, du_ref):
        k = pl.program_id(0)
        dv = d_ref[...]
        row = lax.broadcasted_iota(jnp.int32, (s, lb), 0)
        cnt = jnp.minimum((row + 1).astype(F32), _pool_window(k))
        sums = [dv / cnt]
        for sh in (1, 2, 4, 8):
            prev = sums[-1]
            sums.append(prev + jnp.where(row < s - sh, pltpu.roll(prev, s - sh, 0), 0.0))
        sel = jnp.where(k == 0, sums[1], jnp.where(k == 1, sums[2], jnp.where(k == 2, sums[3], sums[4])))
        du_ref[...] = (sel - dv).astype(du_ref.dtype)

    return pl.pallas_call(
        body, name=f"pool_bwd_l{l}", grid=(len(POOL_WINDOWS),),
        in_specs=[BS((s, lb), lambda k: (0, k))], out_specs=BS((s, lb), lambda k: (0, k)),
        out_shape=SDS((s, width), MXU_DTYPE), compiler_params=_params())(dp)


def _zoh(a_re, a_im, log_dt):
    dt = jnp.exp(log_dt)
    mag = jnp.exp(dt * a_re)
    ang = dt * a_im
    abar_re = mag * jnp.cos(ang)
    abar_im = mag * jnp.sin(ang)
    den = a_re * a_re + a_im * a_im
    nr = abar_re - 1.0
    ni = abar_im
    f_re = (nr * a_re + ni * a_im) / den
    f_im = (ni * a_re - nr * a_im) / den
    return abar_re, abar_im, f_re, f_im


def _zoh_fwd(l, a_re, a_im, log_dt):
    def body(ar, ai, ld, o0, o1, o2, o3):
        for ref, val in zip((o0, o1, o2, o3), _zoh(ar[...], ai[...], ld[...])):
            ref[...] = val

    return pl.pallas_call(body, name=f"zoh_fwd_l{l}", out_shape=[SDS(a_re.shape, F32)] * 4)(a_re, a_im, log_dt)


def _zoh_bwd(l, a_re, a_im, log_dt, cts):
    def body(ar, ai, ld, c0, c1, c2, c3, dar, dai, dld):
        _, vjp = jax.vjp(_zoh, ar[...], ai[...], ld[...])
        g = vjp((c0[...], c1[...], c2[...], c3[...]))
        dar[...] = g[0]
        dai[...] = g[1]
        dld[...] = g[2]

    return pl.pallas_call(body, name=f"zoh_bwd_l{l}",
                          out_shape=[SDS(a_re.shape, F32), SDS(a_re.shape, F32), SDS(log_dt.shape, F32)],
                          )(a_re, a_im, log_dt, *cts)


def _bbar_fwd(l, f_re, f_im, b_re, b_im, c_re, c_im):
    g, p, n = b_re.shape[1:]
    m = MXU_DTYPE

    def body(fr, fi, br, bi, cr, ci, *outs):
        r = lax.broadcasted_iota(jnp.int32, (n, 2 * n), 0)
        c = lax.broadcasted_iota(jnp.int32, (n, 2 * n), 1)
        twice = jnp.where((c & (n - 1)) == r, 1.0, 0.0).astype(m)
        vals = (fr[...] * br[...] - fi[...] * bi[...], fr[...] * bi[...] + fi[...] * br[...], cr[...], ci[...])
        for o_ref, v in zip(outs, vals):
            o_ref[...] = _mm(v.astype(m).reshape(g * p, n), twice).astype(m)

    whole = lambda shp: BS(shp, lambda i: (0,) * len(shp))
    layer = BS((None, g, p, n), lambda i: (l, 0, 0, 0))
    return pl.pallas_call(body, name=f"bbar_fwd_l{l}", grid=(1,),
                          in_specs=[whole((g, 1, n)), whole((g, 1, n)), layer, layer, layer, layer],
                          out_specs=[whole((g * p, 2 * n))] * 4,
                          out_shape=[SDS((g * p, 2 * n), m)] * 4)(f_re, f_im, b_re, b_im, c_re, c_im)


def _block_mask(rows, lanes):
    r = lax.broadcasted_iota(jnp.int32, (rows, lanes), 0)
    c = lax.broadcasted_iota(jnp.int32, (rows, lanes), 1)
    return (r >> (SSM_GROUP.bit_length() - 1)) == (c >> (SSM_STATE.bit_length() - 1))


def _block_matrix(twice_ref, lanes):
    v = twice_ref[...]
    tiled = jnp.concatenate([v] * (lanes // v.shape[1]), axis=1)
    return jnp.where(_block_mask(v.shape[0], lanes), tiled, jnp.zeros_like(tiled))


def _block_diagonal_of(full):
    rows, lanes = full.shape
    kept = jnp.where(_block_mask(rows, lanes), full, 0.0)
    folded = kept[:, 0:128]
    for q in range(1, lanes // 128):
        folded = folded + kept[:, q * 128:(q + 1) * 128]
    return (folded + pltpu.roll(folded, SSM_STATE, 1))[:, :SSM_STATE]


def _bbar_bwd(l, f_re, f_im, b_re, b_im, d_re, d_im):
    g, p, n = b_re.shape[1:]

    def body(fr, fi, br, bi, dr, di, dfr, dfi, dbr, dbi):
        dfr[...] = jnp.sum(dr[...] * br[...] + di[...] * bi[...], axis=1, keepdims=True)
        dfi[...] = jnp.sum(di[...] * br[...] - dr[...] * bi[...], axis=1, keepdims=True)
        dbr[...] = fr[...] * dr[...] + fi[...] * di[...]
        dbi[...] = fr[...] * di[...] - fi[...] * dr[...]

    whole = lambda shp: BS(shp, lambda i: (0,) * len(shp))
    layer = BS((None, g, p, n), lambda i: (l, 0, 0, 0))
    return pl.pallas_call(body, name=f"bbar_bwd_l{l}", grid=(1,),
                          in_specs=[whole((g, 1, n)), whole((g, 1, n)), layer, layer, whole((g, p, n)),
                                    whole((g, p, n))],
                          out_specs=[whole((g, 1, n)), whole((g, 1, n)), whole((g, p, n)), whole((g, p, n))],
                          out_shape=[SDS((g, 1, n), F32), SDS((g, 1, n), F32), SDS((g, p, n), F32),
                                     SDS((g, p, n), F32)])(f_re, f_im, b_re, b_im, d_re, d_im)


def _powers(l, abar_re, abar_im):
    lanes = abar_re.shape[1]

    def body(ar_ref, ai_ref, o_ref):
        ar, ai = ar_ref[...], ai_ref[...]
        pows = [(ar, ai)]
        for _ in range(7):
            pr, pi = pows[-1]
            pows.append((pr * ar - pi * ai, pr * ai + pi * ar))
        row = lax.broadcasted_iota(jnp.int32, (8, lanes), 0)
        for i, k in enumerate((1, 2, 4)):
            o_ref[2 * i] = jnp.broadcast_to(pows[k - 1][0], (8, lanes))
            o_ref[2 * i + 1] = jnp.broadcast_to(pows[k - 1][1], (8, lanes))
        for slot, order in ((3, range(8)), (4, range(7, -1, -1))):
            vr = jnp.zeros((8, lanes), F32)
            vi = jnp.zeros((8, lanes), F32)
            for r, e in enumerate(order):
                vr = jnp.where(row == r, pows[e][0], vr)
                vi = jnp.where(row == r, pows[e][1], vi)
            o_ref[2 * slot] = vr
            o_ref[2 * slot + 1] = vi

    return pl.pallas_call(body, name=f"powers_l{l}", out_shape=SDS((10, 8, lanes), F32))(abar_re, abar_im)


def _ssm_prepare(l, prm):
    g, n, p = SSM_GROUPS, SSM_STATE, SSM_GROUP
    a_re, a_im = prm["ssm_a_re"][l], prm["ssm_a_im"][l]
    log_dt = prm["ssm_log_dt"][l].reshape(g, 1)
    abar_re, abar_im, f_re, f_im = _zoh_fwd(l, a_re, a_im, log_dt)
    f_re, f_im = f_re.reshape(g, 1, n), f_im.reshape(g, 1, n)
    b2_re, b2_im, c2_re, c2_im = _bbar_fwd(l, f_re, f_im, prm["ssm_b_re"], prm["ssm_b_im"], prm["ssm_c_re"],
                                           prm["ssm_c_im"])
    pw = _powers(l, abar_re.reshape(1, g * n), abar_im.reshape(1, g * n))
    return dict(a_re=a_re, a_im=a_im, log_dt=log_dt, f_re=f_re, f_im=f_im, b2_re=b2_re, b2_im=b2_im, c2_re=c2_re,
                c2_im=c2_im, pw=pw, dskip=prm["ssm_d"][l].reshape(1, g * p))


def _ssm_param_grads(l, sd, r, prm):
    g, n, p = SSM_GROUPS, SSM_STATE, SSM_GROUP
    dfr, dfi, db_re, db_im = _bbar_bwd(l, sd["f_re"], sd["f_im"], prm["ssm_b_re"], prm["ssm_b_im"],
                                       r["dbbar_re"].reshape(g, p, n), r["dbbar_im"].reshape(g, p, n))
    cts = (r["dabar_re"].reshape(g, n), r["dabar_im"].reshape(g, n), dfr.reshape(g, n), dfi.reshape(g, n))
    da_re, da_im, dlog_dt = _zoh_bwd(l, sd["a_re"], sd["a_im"], sd["log_dt"], cts)
    return dict(ssm_a_re=da_re, ssm_a_im=da_im, ssm_log_dt=dlog_dt.reshape(g), ssm_b_re=db_re, ssm_b_im=db_im,
                ssm_c_re=r["dc_re"].reshape(g, p, n), ssm_c_im=r["dc_im"].reshape(g, p, n),
                ssm_d=r["dd"].reshape(g, p))


def _ffn_weight_grads(l, fb, dx2, s, place):
    d = dx2.shape[1]
    hcn = fb["act"].shape[1]
    g = {}
    for name, key, rhs in (("ffn_w_gate", "dgate", fb["h2"]), ("ffn_w_up", "dup", fb["h2"]),
                           ("ffn_w_down", "act", fb["dx2"])):
        g[name] = _tn_matmul(f"d{name}_l{l}", fb[key], BS((None, hcn, s), lambda j, t, pr: (j, 0, 0)), rhs,
                             BS((s, d), lambda j, t, pr: (0, 0)), (hcn, d), (N_CHIPS, 1), place)
    return g


def _in_weight_grad(l, ht, dz, place):
    d, s = ht.shape
    ncw = dz.shape[1] // N_CHIPS
    return _tn_matmul(f"dw_in_l{l}", ht, BS((d, s), lambda j, t, pr: (0, 0)), dz, BS((s, ncw), lambda j, t, pr: (0, j)),
                      (d, ncw), (N_CHIPS, 1), place)


def _fused_tn(name, pairs, kinds, s, place):
    n = len(pairs)

    def shape_of(a, b, kind):
        k, m = a.shape[1], b.shape[1]
        if kind == "rows":
            return (N_CHIPS, k // N_CHIPS, m)
        if kind == "cols":
            return (N_CHIPS, k, m // N_CHIPS)
        return (k // 128, 128, 128)

    shapes = [shape_of(a, b, kind) for (a, b), kind in zip(pairs, kinds)]
    out_shape = []
    for shp, kind in zip(shapes, kinds):
        out_shape += [SDS(shp, F32)] if kind == "groups" else [SDS(shp[1:], F32), SDS(shp, WIRE_DTYPE)]

    def body(place_ref, *refs):
        ins, outs, accs = refs[:2 * n], refs[2 * n:2 * n + len(out_shape)], refs[2 * n + len(out_shape):]
        o = 0
        for i, kind in enumerate(kinds):
            a, b = ins[2 * i][...], ins[2 * i + 1][...]
            if kind == "groups":
                for k in range(shapes[i][0]):
                    outs[o][k] = _mm_tn(a[:, k * 128:(k + 1) * 128], b[:, k * 128:(k + 1) * 128])
                o += 1
                continue
            acc = accs[i]
            if kind == "rows":
                acc[...] = _mm_tn(a, b).reshape(acc.shape)
            else:
                full = _mm_tn(a, b)
                nc = acc.shape[2]
                for j in range(N_CHIPS):
                    acc[j] = full[:, j * nc:(j + 1) * nc]
            outs[o][...] = acc[place_ref[0]]
            outs[o + 1][...] = acc[...].astype(WIRE_DTYPE)
            o += 2

    whole = lambda shp: BS(shp, lambda t, pr: (0,) * len(shp))
    outs = pl.pallas_call(
        body, name=name,
        grid_spec=pltpu.PrefetchScalarGridSpec(
            num_scalar_prefetch=1, grid=(1,),
            in_specs=[whole(v.shape) for pair in pairs for v in pair],
            out_specs=[whole(o.shape) for o in out_shape],
            scratch_shapes=[pltpu.VMEM(shp, F32) for shp in shapes]),
        out_shape=out_shape, compiler_params=_params(),
    )(place, *[v for pair in pairs for v in pair])
    res, o = [], 0
    for kind in kinds:
        if kind == "groups":
            res.append(outs[o])
            o += 1
        else:
            res.append((outs[o], outs[o + 1]))
            o += 2
    return res


def _mixer_weight_grads(l, sv, mb, dx1, s, place):
    g = {}
    (g["w_out"], g["ssm_w_glu"]) = _fused_tn(f"dw_out_glu_l{l}", [(mb["merged"], dx1), (mb["ge"], mb["dt"])],
                                            ("rows", "rows"), s, place)
    (g["ssm_w_proj"], g["conv_w_proj"], g["pool_w_proj"]) = _fused_tn(
        f"dw_proj_l{l}", [(mb["sa"], mb["dya"]), (mb["ac"], mb["dyb"]), (mb["pp"], mb["dyc"])],
        ("cols", "cols", "cols"), s, place)
    (dwgrp,) = _fused_tn(f"dpool_w_group_l{l}", [(sv["p"], mb["dq"])], ("groups",), s, place)
    return g, dwgrp


def _local_step(x, target, weights_of, prm, place, on_grads=None):
    s, d = x.shape
    cw = prm["ssm_b_glu"].shape[1]
    sp = {k: prm[k].reshape(N_LAYERS, 1, -1) for k in ("norm1", "norm2", "b_gate", "ssm_b_glu", "conv_ln_g", "conv_ln_b",
                                                        "pool_scale", "conv_b_dw")}
    sp["pool_w_group"] = prm["pool_w_group"]
    saved = []
    xin = x
    prepared = [_ssm_prepare(l, prm) for l in range(N_LAYERS)]
    ready = tuple(sd[k] for sd in prepared for k in ("pw", "b2_re", "c2_re"))
    for l in range(N_LAYERS):
        fw = weights_of(l, "in", (xin,) + (ready if l == 0 else ()))
        sd = prepared[l]
        z, h = _in_proj(l, xin, sp["norm1"], fw["w_in"])
        hre, him, y = _ssm_fwd(l, z, sd["b2_re"], sd["b2_im"], sd["c2_re"], sd["c2_im"], sd["pw"], sd["dskip"])
        p = _pool_fwd(l, z, cw)
        fw.update(weights_of(l, "mixer", (y, p)))
        wdw = fw["conv_w_dw"]
        hc = _conv_fwd(l, z, wdw, sp["conv_b_dw"][l])
        x1 = _merge_fwd(l, xin, y, hc, p, z, fw, sp)
        fw.update(weights_of(l, "ffn", (x1,)))
        x2, gate, up, h2 = _ffn_fwd(l, x1, sp["norm2"], fw["ffn_w_gate"], fw["ffn_w_up"], fw["ffn_w_down"])
        saved.append(dict(x=xin, z=z, h=h, hre=hre, him=him, y=y, hc=hc, p=p, x1=x1, sd=sd, wdw=wdw, fw=fw,
                          gate=gate, up=up, h2=h2))
        xin = x2
    dx, loss, dfinal = _loss_head(xin, target, prm["final_norm"].reshape(1, d))
    big = [None] * N_LAYERS
    small = [None] * N_LAYERS
    norm2_rows = sp["norm2"]
    started = (lambda l, group, grads: on_grads(l, group, grads)) if on_grads is not None else (lambda *a: 0.0)
    for l in reversed(range(N_LAYERS)):
        sv = saved[l]
        sd, fw = sv["sd"], sv["fw"]
        fb = _ffn_bwd(l, sv["x1"], dx, sv["gate"], sv["up"], norm2_rows, fw["ffn_w_gate"], fw["ffn_w_up"],
                      fw["ffn_w_down"])
        fb["h2"] = sv["h2"]
        big[l] = _ffn_weight_grads(l, fb, dx, s, place)
        spl = dict(sp, ssm_b_glu=sp["ssm_b_glu"] + started(l, "ffn", big[l]))
        mb = _merge_bwd(l, fb["dx1"], sv["y"], sv["hc"], sv["p"], sv["z"], fw, spl)
        mixer, dwgrp = _mixer_weight_grads(l, sv, mb, fb["dx1"], s, place)
        big[l].update(mixer)
        wdw = sv["wdw"] + started(l, "mixer", mixer)
        du_c = _pool_bwd(l, mb["dp"])
        dv1, dv2, dwdw, dbdw = _conv_bwd(l, mb["dhc"], sv["z"], wdw)
        sr = _ssm_bwd(l, mb["dy"], sv["z"], sv["hre"], sv["him"], sd["b2_re"], sd["b2_im"], sd["c2_re"],
                      sd["c2_im"], sd["pw"], sd["dskip"])
        dx, dz, dnorm1 = _in_proj_bwd(l, fb["dx1"], sv["x"], sp["norm1"], fw["w_in"], sr["du"], dv1, dv2, du_c, mb["dzg"])
        w_in_grad = {"w_in": _in_weight_grad(l, sv["h"], dz, place)}
        big[l].update(w_in_grad)
        sg = _ssm_param_grads(l, sd, sr, prm)
        sg.update(norm1=dnorm1.reshape(d), b_gate=mb["db_gate"].reshape(3 * d), ssm_b_glu=mb["db_glu"].reshape(cw),
                  conv_b_dw=dbdw.reshape(cw), conv_ln_g=mb["dln_g"].reshape(cw), conv_ln_b=mb["dln_b"].reshape(cw),
                  pool_w_group=dwgrp, pool_scale=mb["dscale"].reshape(cw), norm2=fb["dnorm2"].reshape(d),
                  conv_w_dw=dwdw)
        small[l] = sg
        if l == N_LAYERS - 1:
            sg = dict(sg, final_norm=dfinal.reshape(d))
        norm2_rows = sp["norm2"] + (started(l, "in", w_in_grad) + started(l, "small", sg))
    return loss[0, 0], dx, big, small, dfinal.reshape(d)


def _place():
    return lax.axis_index("x"), lax.axis_index("y"), lax.axis_index("c")


def _other_chips(x, y):
    return [(1 - x, y), (x, 1 - y), (1 - x, 1 - y)]


def _remote(src, dst, send_sem, recv_sem, device):
    return pltpu.make_async_remote_copy(src_ref=src, dst_ref=dst, send_sem=send_sem, recv_sem=recv_sem,
                                        device_id=device, device_id_type=MESH)


def _hbm(v):
    return pltpu.with_memory_space_constraint(v, pltpu.HBM)


def _cast_into(name, w, place, dtype, after=()):
    nl, k, n = w.shape
    tr = _row_tile(k, n)
    nt = k // tr

    def body(place_ref, w_ref, *rest):
        o0_ref, o1_ref = rest[len(after):]

        @pl.when(pl.program_id(0) == 0)
        def _():
            o0_ref[...] = w_ref[...].astype(dtype)

        @pl.when(pl.program_id(0) == 1)
        def _():
            o1_ref[...] = w_ref[...].astype(dtype)

    return pl.pallas_call(
        body, name=f"cast_{name}",
        grid_spec=pltpu.PrefetchScalarGridSpec(
            num_scalar_prefetch=1, grid=(nl, nt),
            in_specs=[BS((None, tr, n), lambda l, t, pr: (l, t, 0))] + [ANY] * len(after),
            out_specs=[BS((None, tr, n), lambda l, t, pr: (pr[0], t * (1 - l) + (nt - 1) * l, 0)),
                       BS((None, tr, n), lambda l, t, pr: (pr[0], t * l, 0))]),
        out_shape=[SDS((N_CHIPS, k, n), dtype)] * 2)(place, w, *after)


def _cast_small_into(tag, ws, dtypes, place, after=()):
    n = len(ws)

    def body(place_ref, *refs):
        ins, outs = refs[:n], refs[n + len(after):]
        for i in range(n):
            @pl.when(pl.program_id(0) == 0)
            def _():
                outs[2 * i][...] = ins[i][...].astype(dtypes[i])

            @pl.when(pl.program_id(0) == 1)
            def _():
                outs[2 * i + 1][...] = ins[i][...].astype(dtypes[i])

    slot = lambda w: BS((None,) + w.shape[1:], lambda l, pr: (pr[0], 0, 0))
    outs = pl.pallas_call(
        body, name=f"cast_{tag}",
        grid_spec=pltpu.PrefetchScalarGridSpec(
            num_scalar_prefetch=1, grid=(N_LAYERS,),
            in_specs=[BS((None,) + w.shape[1:], lambda l, pr: (l, 0, 0)) for w in ws] + [ANY] * len(after),
            out_specs=[slot(w) for w in ws for _ in range(N_LAYERS)]),
        out_shape=[SDS((N_CHIPS,) + w.shape[1:], dt) for w, dt in zip(ws, dtypes) for _ in range(N_LAYERS)],
    )(place, *ws, *after)
    return [tuple(outs[N_LAYERS * i:N_LAYERS * (i + 1)]) for i in range(n)]


def _gather_rows(buf, c):
    k = buf.shape[1]
    if k % 2:
        return pl.ds(0, k)
    return pl.ds(pl.multiple_of(c * (k // 2), 8), k // 2)


def _allgather_start(tag, groups):
    ng = len(groups)
    sizes = [len(g) for g in groups]
    first = [sum(sizes[:g]) for g in range(ng)]
    flat = [b for g in groups for b in g]
    nb = len(flat)

    def body(*refs):
        ins = refs[:nb]
        sems = refs[nb:nb + 2 * ng]
        token = refs[-1]
        x, y, c = _place()
        jme = 2 * x + y
        for g in range(ng):
            for a in range(sizes[g]):
                buf = ins[first[g] + a]
                blk = buf.at[jme, _gather_rows(buf, c)]
                for k, (cx, cy) in enumerate(_other_chips(x, y)):
                    _remote(blk, blk, sems[2 * g].at[3 * a + k], sems[2 * g + 1].at[3 * a + k], (cx, cy, c)).start()
        token[...] = jnp.zeros(token.shape, F32)

    sem_shapes = [pltpu.SemaphoreType.DMA((3 * sizes[g // 2],)) for g in range(2 * ng)]
    outs = pl.pallas_call(
        body, name=f"allgather_start_{tag}", in_specs=[HBM] * nb,
        out_specs=[SEM] * (2 * ng) + [HBM] * nb + [pl.BlockSpec(memory_space=pltpu.VMEM)],
        out_shape=sem_shapes + [pltpu.HBM(b.shape, b.dtype) for b in flat] + [SDS((8, 128), F32)],
        input_output_aliases={i: 2 * ng + i for i in range(nb)},
        compiler_params=pltpu.CompilerParams(has_side_effects=SIDE_EFFECT))(*[_hbm(b) for b in flat])
    per_group = [(outs[2 * g], outs[2 * g + 1], outs[2 * ng + first[g]:2 * ng + first[g] + sizes[g]])
                 for g in range(ng)]
    return per_group, outs[-1]


def _allgather_wait(l, send_sems, recv_sems, bufs, after):
    n = len(bufs)

    def body(*refs):
        ins = refs[:n]
        ssem, rsem = refs[n], refs[n + 1]
        x, y, c = _place()
        jme = 2 * x + y
        for a in range(n):
            rows = _gather_rows(ins[a], c)
            for k, (cx, cy) in enumerate(_other_chips(x, y)):
                cp = _remote(ins[a].at[jme, rows], ins[a].at[2 * cx + cy, rows], ssem.at[3 * a + k],
                             rsem.at[3 * a + k], (cx, cy, c))
                cp.wait_send()
                cp.wait_recv()

    return pl.pallas_call(
        body, name=f"allgather_wait_{l}", in_specs=[HBM] * n + [SEM, SEM] + [ANY] * len(after), out_specs=[HBM] * n,
        out_shape=[pltpu.HBM(b.shape, b.dtype) for b in bufs], input_output_aliases={i: i for i in range(n)},
        compiler_params=pltpu.CompilerParams(has_side_effects=SIDE_EFFECT))(*bufs, send_sems, recv_sems, *after)


def _allgather_forward(l, bufs):
    n = len(bufs)
    split = [a for a in range(n) if bufs[a].shape[1] % 2 == 0]

    def body(*refs):
        ins = refs[:n]
        send_sems, recv_sems = refs[2 * n:]
        x, y, c = _place()
        sibling = (x, y, 1 - c)
        copies = []
        for a in split:
            for k, (cx, cy) in enumerate(_other_chips(x, y)):
                blk = ins[a].at[2 * cx + cy, _gather_rows(ins[a], c)]
                cp = _remote(blk, blk, send_sems.at[a, k], recv_sems.at[a, k], sibling)
                cp.start()
                copies.append(cp)
        for a in split:
            for k, (cx, cy) in enumerate(_other_chips(x, y)):
                blk = ins[a].at[2 * cx + cy, _gather_rows(ins[a], 1 - c)]
                _remote(blk, blk, send_sems.at[a, k], recv_sems.at[a, k], sibling).wait_recv()
        for cp in copies:
            cp.wait_send()

    sem = pltpu.SemaphoreType.DMA((n, 3))
    return pl.pallas_call(
        body, name=f"allgather_forward_{l}", in_specs=[ANY] * n, out_specs=[ANY] * n,
        out_shape=[SDS(b.shape, b.dtype) for b in bufs], input_output_aliases={i: i for i in range(n)},
        scratch_shapes=[sem, sem])(*bufs)


def _rs_to_owner(l, parts):
    n = len(parts)
    lands = [lax.empty((3,) + p.shape[1:], p.dtype) for p in parts]

    def body(*refs):
        ins, zones = refs[:n], refs[n:2 * n]
        send_sems, recv_sems = refs[2 * n], refs[2 * n + 1]
        token = refs[-1]
        x, y, c = _place()
        for a in range(n):
            for k, (cx, cy) in enumerate(_other_chips(x, y)):
                _remote(ins[a].at[2 * cx + cy], zones[a].at[k], send_sems.at[3 * a + k], recv_sems.at[3 * a + k],
                        (cx, cy, c)).start()
        token[...] = jnp.zeros(token.shape, F32)

    sem = pltpu.SemaphoreType.DMA((3 * n,))
    outs = pl.pallas_call(
        body, name=f"rs_to_owner_start_{l}", in_specs=[HBM] * (2 * n),
        out_specs=[SEM, SEM] + [HBM] * (2 * n) + [pl.BlockSpec(memory_space=pltpu.VMEM)],
        out_shape=[sem, sem] + [pltpu.HBM(p.shape, p.dtype) for p in parts]
        + [pltpu.HBM(z.shape, z.dtype) for z in lands] + [SDS((8, 128), F32)],
        input_output_aliases={i: 2 + i for i in range(2 * n)},
        compiler_params=pltpu.CompilerParams(has_side_effects=SIDE_EFFECT),
    )(*[_hbm(p) for p in parts], *[_hbm(z) for z in lands])
    return outs[0], outs[1], outs[2:2 + n], outs[2 + n:2 + 2 * n], outs[-1]


def _rs_to_owner_wait(l, send_sems, recv_sems, parts, lands, after):
    n = len(parts)

    def body(*refs):
        ins, zones = refs[:n], refs[n:2 * n]
        ssem, rsem = refs[2 * n], refs[2 * n + 1]
        x, y, c = _place()
        for a in range(n):
            for k, (cx, cy) in enumerate(_other_chips(x, y)):
                cp = _remote(ins[a].at[2 * cx + cy], zones[a].at[k], ssem.at[3 * a + k], rsem.at[3 * a + k],
                             (cx, cy, c))
                cp.wait_send()
                cp.wait_recv()

    outs = pl.pallas_call(
        body, name=f"rs_to_owner_wait_{l}", in_specs=[HBM] * (2 * n) + [SEM, SEM] + [ANY] * len(after),
        out_specs=[HBM] * (2 * n),
        out_shape=[pltpu.HBM(p.shape, p.dtype) for p in parts] + [pltpu.HBM(z.shape, z.dtype) for z in lands],
        input_output_aliases={i: i for i in range(2 * n)},
        compiler_params=pltpu.CompilerParams(has_side_effects=SIDE_EFFECT),
    )(*parts, *lands, send_sems, recv_sems, *after)
    return outs[:n], outs[n:]


def _rs_sibling_exchange(l, both):
    n = len(both)

    def body(*refs):
        ins = refs[:n]
        send_sems, recv_sems = refs[2 * n:]
        x, y, c = _place()
        copies = []
        for a in range(n):
            cp = _remote(ins[a].at[c], ins[a].at[c], send_sems.at[a], recv_sems.at[a], (x, y, 1 - c))
            cp.start()
            copies.append(cp)
        for a, cp in enumerate(copies):
            cp.wait_send()
            _remote(ins[a].at[1 - c], ins[a].at[1 - c], send_sems.at[a], recv_sems.at[a], (x, y, 1 - c)).wait_recv()

    sem = pltpu.SemaphoreType.DMA((n,))
    return pl.pallas_call(
        body, name=f"rs_sibling_exchange_{l}", in_specs=[ANY] * n, out_specs=[ANY] * n,
        out_shape=[SDS(b.shape, b.dtype) for b in both], input_output_aliases={i: i for i in range(n)},
        scratch_shapes=[sem, sem])(*both)


def _add_owner(name, grad, recv, place):
    r, cols = grad.shape
    tr = _row_tile(r, cols, budget=1024 * 1024)
    nt = r // tr

    def body(place_ref, g_ref, r_ref, o_ref):
        acc = ((g_ref[...] + r_ref[0].astype(F32)) + r_ref[1].astype(F32)) + r_ref[2].astype(F32)
        o_ref[...] = acc.astype(o_ref.dtype)

    return pl.pallas_call(
        body, name=name,
        grid_spec=pltpu.PrefetchScalarGridSpec(
            num_scalar_prefetch=1, grid=(nt,),
            in_specs=[BS((tr, cols), lambda t, pr: (t, 0)), BS((3, tr, cols), lambda t, pr: (0, t, 0))],
            out_specs=BS((None, tr, cols), lambda t, pr: (pr[1], t, 0))),
        out_shape=SDS((2, r, cols), WIRE_DTYPE))(place, grad, recv)


def _add_owner_group(tag, grads, recvs, place, steps):
    n = len(grads)

    def body(place_ref, *refs):
        gs, rs, outs = refs[:n], refs[n:2 * n], refs[2 * n:]
        for g_ref, r_ref, o_ref in zip(gs, rs, outs):
            acc = ((g_ref[...] + r_ref[0].astype(F32)) + r_ref[1].astype(F32)) + r_ref[2].astype(F32)
            o_ref[...] = acc.astype(o_ref.dtype)

    rows = [g.shape[0] // steps for g in grads]
    return pl.pallas_call(
        body, name=f"rs_add_owner_{tag}",
        grid_spec=pltpu.PrefetchScalarGridSpec(
            num_scalar_prefetch=1, grid=(steps,),
            in_specs=[BS((r, g.shape[1]), lambda t, pr: (t, 0)) for g, r in zip(grads, rows)]
            + [BS((3, r, g.shape[1]), lambda t, pr: (0, t, 0)) for g, r in zip(grads, rows)],
            out_specs=[BS((None, r, g.shape[1]), lambda t, pr: (pr[1], t, 0)) for g, r in zip(grads, rows)]),
        out_shape=[SDS((2,) + g.shape, WIRE_DTYPE) for g in grads], compiler_params=_params(),
    )(place, *grads, *recvs)


def _reduce_start(tag, grads):
    names = list(grads)
    send_sems, recv_sems, wires, lands, token = _rs_to_owner(tag, [grads[n][1] for n in names])
    return dict(tag=tag, names=names, send_sems=send_sems, recv_sems=recv_sems, wires=wires, lands=lands,
                grads=[grads[n][0] for n in names]), token


def _reduce_finish(tag, groups, place, after):
    all_names, all_mine = [], []
    for pending in groups:
        sub, names = pending["tag"], pending["names"]
        _, lands = _rs_to_owner_wait(sub, pending["send_sems"], pending["recv_sems"], pending["wires"],
                                     pending["lands"], after)
        if len(names) > 1:
            small = max(g.size for g in pending["grads"]) <= SMALL_GRAD_ELEMS
            mine = _add_owner_group(sub, pending["grads"], lands, place, 1 if small else ADAMW_GROUP_STEPS)
        else:
            mine = [_add_owner(f"rs_add_owner_{n}_{sub}", g, r, place)
                    for n, g, r in zip(names, pending["grads"], lands)]
        all_names += names
        all_mine += mine
    return dict(zip(all_names, _rs_sibling_exchange(tag, all_mine)))


def _small_peers(x, y, c):
    return [(x, y, 1 - c)] + [(cx, cy, c) for cx, cy in _other_chips(x, y)]


def _allgather_rows_start(tag, bufs):
    n = len(bufs)
    lands = [lax.empty((8,) + b.shape, b.dtype) for b in bufs]

    def body(*refs):
        ins, zones = refs[:n], refs[n:2 * n]
        send_sems, recv_sems = refs[2 * n], refs[2 * n + 1]
        token = refs[-1]
        x, y, c = _place()
        for a in range(n):
            for i, peer in enumerate(_small_peers(x, y, c)):
                _remote(ins[a], zones[a].at[4 * x + 2 * y + c], send_sems.at[4 * a + i], recv_sems.at[4 * a + i],
                        peer).start()
        token[...] = jnp.zeros(token.shape, F32)

    sem = pltpu.SemaphoreType.DMA((4 * n,))
    outs = pl.pallas_call(
        body, name=f"allgather_small_start_{tag}", in_specs=[HBM] * (2 * n),
        out_specs=[SEM, SEM] + [HBM] * (2 * n) + [pl.BlockSpec(memory_space=pltpu.VMEM)],
        out_shape=[sem, sem] + [pltpu.HBM(b.shape, b.dtype) for b in bufs]
        + [pltpu.HBM(z.shape, z.dtype) for z in lands] + [SDS((8, 128), F32)],
        input_output_aliases={i: 2 + i for i in range(2 * n)},
        compiler_params=pltpu.CompilerParams(has_side_effects=SIDE_EFFECT),
    )(*[_hbm(b) for b in bufs], *[_hbm(z) for z in lands])
    return outs[0], outs[1], outs[2:2 + n], outs[2 + n:2 + 2 * n], outs[-1]


def _allgather_rows_wait(tag, send_sems, recv_sems, bufs, lands, after):
    n = len(bufs)

    def body(*refs):
        ins, zones = refs[:n], refs[n:2 * n]
        ssem, rsem = refs[2 * n], refs[2 * n + 1]
        x, y, c = _place()
        for a in range(n):
            for i, (px, py, pc) in enumerate(_small_peers(x, y, c)):
                cp = _remote(ins[a], zones[a].at[4 * px + 2 * py + pc], ssem.at[4 * a + i], rsem.at[4 * a + i],
                             (px, py, pc))
                cp.wait_send()
                cp.wait_recv()

    outs = pl.pallas_call(
        body, name=f"allgather_small_wait_{tag}", in_specs=[HBM] * (2 * n) + [SEM, SEM, ANY],
        out_specs=[HBM] * (2 * n),
        out_shape=[pltpu.HBM(b.shape, b.dtype) for b in bufs] + [pltpu.HBM(z.shape, z.dtype) for z in lands],
        input_output_aliases={i: i for i in range(2 * n)},
        compiler_params=pltpu.CompilerParams(has_side_effects=SIDE_EFFECT),
    )(*bufs, *lands, send_sems, recv_sems, after)
    return outs[:n], outs[n:]


def _allgather_rows_forward(tag, lands):
    n = len(lands)

    def body(*refs):
        ins = refs[:n]
        send_sems, recv_sems = refs[2 * n:]
        x, y, c = _place()
        sibling = (x, y, 1 - c)
        copies = []
        for a in range(n):
            for k, (cx, cy) in enumerate(_other_chips(x, y)):
                blk = ins[a].at[4 * cx + 2 * cy + c]
                cp = _remote(blk, blk, send_sems.at[a, k], recv_sems.at[a, k], sibling)
                cp.start()
                copies.append(cp)
        for a in range(n):
            for k, (cx, cy) in enumerate(_other_chips(x, y)):
                blk = ins[a].at[4 * cx + 2 * cy + 1 - c]
                _remote(blk, blk, send_sems.at[a, k], recv_sems.at[a, k], sibling).wait_recv()
        for cp in copies:
            cp.wait_send()

    sem = pltpu.SemaphoreType.DMA((n, 3))
    return pl.pallas_call(body, name=f"allgather_small_forward_{tag}", in_specs=[ANY] * n, out_specs=[ANY] * n,
                          out_shape=[SDS(z.shape, z.dtype) for z in lands],
                          input_output_aliases={i: i for i in range(n)}, scratch_shapes=[sem, sem])(*lands)


def _sum_devices(tag, gathered, mine, place):
    _, r, cols = gathered.shape
    tr = _row_tile(r, cols, budget=256 * 1024)

    def body(place_ref, g_ref, x_ref, o_ref):
        me = 2 * place_ref[0] + place_ref[1]
        acc = jnp.where(me == 0, x_ref[...], g_ref[0])
        for k in range(1, 8):
            acc = acc + jnp.where(me == k, x_ref[...], g_ref[k])
        o_ref[...] = acc

    return pl.pallas_call(
        body, name=f"sum_small_grads_{tag}",
        grid_spec=pltpu.PrefetchScalarGridSpec(
            num_scalar_prefetch=1, grid=(r // tr,),
            in_specs=[BS((8, tr, cols), lambda t, pr: (0, t, 0)), BS((tr, cols), lambda t, pr: (t, 0))],
            out_specs=BS((tr, cols), lambda t, pr: (t, 0))),
        out_shape=SDS((r, cols), F32))(place, gathered, mine)


def _adamw_values(w, g, m, v):
    m = ADAM_B1 * m + (1.0 - ADAM_B1) * g
    v = ADAM_B2 * v + (1.0 - ADAM_B2) * (g * g)
    m_hat = m / (1.0 - ADAM_B1 ** ADAM_STEP)
    v_hat = v / (1.0 - ADAM_B2 ** ADAM_STEP)
    delta = -ADAM_LR * (m_hat / (jnp.sqrt(v_hat) + ADAM_EPS) + ADAM_WD * w)
    return delta, m, v


def _adamw_big(name, l, w, m, v, g, earlier=None, after=()):
    nl, r, cols = w.shape
    tr = _row_tile(r, cols, budget=1024 * 1024)
    nt = r // tr
    n_prev = 0 if earlier is None else 4

    def body(*refs):
        w_ref, m_ref, v_ref, g_ref = refs[:4]
        go_ref, d_ref, mo_ref, vo_ref = refs[4 + n_prev + len(after):]
        gv = g_ref[0].astype(F32) + g_ref[1].astype(F32)
        delta, m_new, v_new = _adamw_values(w_ref[...], gv, m_ref[...], v_ref[...])
        go_ref[...] = gv
        d_ref[...] = delta
        mo_ref[...] = m_new
        vo_ref[...] = v_new

    layer = BS((None, tr, cols), lambda t: (l, t, 0))
    return pl.pallas_call(
        body, name=f"adamw_{name}_l{l}", grid=(nt,),
        in_specs=[layer, layer, layer, BS((2, tr, cols), lambda t: (0, t, 0))] + [ANY] * (n_prev + len(after)),
        out_specs=[layer] * 4, out_shape=[SDS(w.shape, F32)] * 4,
        input_output_aliases={4 + i: i for i in range(n_prev)}, compiler_params=_params(),
    )(w, m, v, g, *(earlier or ()), *after)


def _adamw_small_group(tag, l, ws, ms, vs, gs, earlier, after=()):
    n = len(ws)
    steps = ADAMW_GROUP_STEPS
    prev = [a for e in earlier if e is not None for a in e]
    n_prev = len(prev)
    assert n_prev in (0, 4 * n)

    def body(*refs):
        w_refs, m_refs, v_refs, g_refs = refs[:n], refs[n:2 * n], refs[2 * n:3 * n], refs[3 * n:4 * n]
        outs = refs[4 * n + n_prev + len(after):]
        for i in range(n):
            gv = g_refs[i][0].astype(F32) + g_refs[i][1].astype(F32)
            delta, m_new, v_new = _adamw_values(w_refs[i][...], gv, m_refs[i][...], v_refs[i][...])
            for ref, val in zip(outs[4 * i:4 * i + 4], (gv, delta, m_new, v_new)):
                ref[...] = val

    def layer(w):
        return BS((None, w.shape[1] // steps, w.shape[2]), lambda t: (l, t, 0))

    return pl.pallas_call(
        body, name=f"adamw_{tag}_l{l}", grid=(steps,),
        in_specs=[layer(w) for w in ws] * 3
        + [BS((2, w.shape[1] // steps, w.shape[2]), lambda t: (0, t, 0)) for w in ws] + [ANY] * (n_prev + len(after)),
        out_specs=[layer(w) for w in ws for _ in range(4)],
        out_shape=[SDS(w.shape, F32) for w in ws for _ in range(4)],
        input_output_aliases={4 * n + i: i for i in range(n_prev)}, compiler_params=_params(),
    )(*ws, *ms, *vs, *gs, *prev, *after)


def _adamw_mid(ws, ms, vs, gathered, mine, place):
    n = len(ws)
    shape = ws[0].shape[1:]
    zeros = (0,) * len(shape)

    def body(place_ref, *refs):
        w_refs, m_refs, v_refs = refs[:n], refs[n:2 * n], refs[2 * n:3 * n]
        gath, own = refs[3 * n:(3 + N_LAYERS) * n], refs[(3 + N_LAYERS) * n:(3 + 2 * N_LAYERS) * n]
        outs = refs[(3 + 2 * N_LAYERS) * n:]
        me = 2 * place_ref[0] + place_ref[1]
        for i in range(n):
            gv = None
            for l in range(N_LAYERS):
                g_ref, x_ref = gath[l * n + i], own[l * n + i]
                acc = jnp.where(me == 0, x_ref[...], g_ref[0])
                for k in range(1, 8):
                    acc = acc + jnp.where(me == k, x_ref[...], g_ref[k])
                gv = acc if gv is None else jnp.where(pl.program_id(0) == l, acc, gv)
            delta, m_new, v_new = _adamw_values(w_refs[i][...], gv, m_refs[i][...], v_refs[i][...])
            for ref, val in zip(outs[4 * i:4 * i + 4], (gv, delta, m_new, v_new)):
                ref[...] = val

    layer = BS((None,) + shape, lambda l, pr: (l,) + zeros)
    kept = pl.Buffered(1)
    outs = pl.pallas_call(
        body, name="adamw_replicated_matrices",
        grid_spec=pltpu.PrefetchScalarGridSpec(
            num_scalar_prefetch=1, grid=(N_LAYERS,),
            in_specs=[layer] * (3 * n)
            + [BS((8,) + shape, lambda l, pr: (0,) + zeros, pipeline_mode=kept)] * (N_LAYERS * n)
            + [BS(shape, lambda l, pr: zeros, pipeline_mode=kept)] * (N_LAYERS * n),
            out_specs=[layer] * (4 * n)),
        out_shape=[SDS(ws[0].shape, F32)] * (4 * n), compiler_params=_params(),
    )(place, *ws, *ms, *vs, *[g for l in range(N_LAYERS) for g in gathered[l]],
      *[x for l in range(N_LAYERS) for x in mine[l]])
    return [tuple(outs[4 * i:4 * i + 4]) for i in range(n)]


def _adamw_rows(w, m, v, g):
    r, cols = w.shape
    tr = _row_tile(r, cols, budget=512 * 1024)

    def body(w_ref, m_ref, v_ref, g_ref, d_ref, mo_ref, vo_ref):
        delta, m_new, v_new = _adamw_values(w_ref[...], g_ref[...], m_ref[...], v_ref[...])
        d_ref[...] = delta
        mo_ref[...] = m_new
        vo_ref[...] = v_new

    spec = BS((tr, cols), lambda t: (t, 0))
    return pl.pallas_call(body, name="adamw_small", grid=(r // tr,), in_specs=[spec] * 4, out_specs=[spec] * 3,
                          out_shape=[SDS(w.shape, F32)] * 3)(w, m, v, g)


SMALL_GRAD_ELEMS = 256 * 1024
ADAMW_GROUP_STEPS = 4
PACK_ALIGN = 8 * 128
PACK_ROWS = 128


def _pack_rows(arrays):
    parts, rows = [], 0
    for a in arrays:
        flat = a.reshape(-1)
        pad = (-flat.shape[0]) % PACK_ALIGN
        if pad:
            flat = jnp.pad(flat, (0, pad))
        parts.append(flat.reshape(-1, 128))
        rows += parts[-1].shape[0]
    if rows % PACK_ROWS:
        parts.append(jnp.zeros((PACK_ROWS - rows % PACK_ROWS, 128), parts[0].dtype))
    return jnp.concatenate(parts, axis=0)


def _unpack_rows(buf, shapes):
    out, row = [], 0
    for shape in shapes:
        size = math.prod(shape)
        rows = -(-size // PACK_ALIGN) * (PACK_ALIGN // 128)
        out.append(buf[row:row + rows].reshape(-1)[:size].reshape(shape))
        row += rows
    return out


def kernel(x, norm1, w_in, b_gate, ssm_a_re, ssm_a_im, ssm_log_dt, ssm_b_re, ssm_b_im, ssm_c_re, ssm_c_im, ssm_d, ssm_w_glu, ssm_b_glu, ssm_w_proj, conv_w_dw, conv_b_dw, conv_ln_g, conv_ln_b, conv_w_proj, pool_w_group, pool_scale, pool_w_proj, w_out, norm2, ffn_w_gate, ffn_w_up, ffn_w_down, final_norm, loss_target, m_norm1, m_w_in, m_b_gate, m_ssm_a_re, m_ssm_a_im, m_ssm_log_dt, m_ssm_b_re, m_ssm_b_im, m_ssm_c_re, m_ssm_c_im, m_ssm_d, m_ssm_w_glu, m_ssm_b_glu, m_ssm_w_proj, m_conv_w_dw, m_conv_b_dw, m_conv_ln_g, m_conv_ln_b, m_conv_w_proj, m_pool_w_group, m_pool_scale, m_pool_w_proj, m_w_out, m_norm2, m_ffn_w_gate, m_ffn_w_up, m_ffn_w_down, m_final_norm, v_norm1, v_w_in, v_b_gate, v_ssm_a_re, v_ssm_a_im, v_ssm_log_dt, v_ssm_b_re, v_ssm_b_im, v_ssm_c_re, v_ssm_c_im, v_ssm_d, v_ssm_w_glu, v_ssm_b_glu, v_ssm_w_proj, v_conv_w_dw, v_conv_b_dw, v_conv_ln_g, v_conv_ln_b, v_conv_w_proj, v_pool_w_group, v_pool_scale, v_pool_w_proj, v_w_out, v_norm2, v_ffn_w_gate, v_ffn_w_up, v_ffn_w_down, v_final_norm):
    given = dict(locals())
    cx, cy, cc = _place()
    place = jnp.stack([2 * cx + cy, cc]).astype(jnp.int32)

    def kernel_view(n, a):
        if n in TRANSPOSED:
            return a.transpose(0, 2, 1)
        return a.transpose(0, 1, 3, 2) if n in ("ssm_b_re", "ssm_b_im") else a

    prm = {n: given[n] for n in WEIGHTS}
    mom = {n: given["m_" + n] for n in WEIGHTS}
    var = {n: given["v_" + n] for n in WEIGHTS}
    for n in MID:
        prm[n], mom[n], var[n] = kernel_view(n, prm[n]), kernel_view(n, mom[n]), kernel_view(n, var[n])

    dw_shard = prm["conv_w_dw"].reshape(N_LAYERS, CONV_KERNEL, -1)
    casts = {"w_in": _cast_into("w_in", prm["w_in"], place, MXU_DTYPE)}
    first, first_started = _allgather_start("first", [[casts["w_in"][0]]])
    in_flight = {(0, "in"): first[0]}
    mixer = GATHER_GROUPS["mixer"]
    casts.update(zip(mixer, _cast_small_into(
        "mixer", [dw_shard if n == "conv_w_dw" else prm[n] for n in mixer],
        [F32 if n == "conv_w_dw" else MXU_DTYPE for n in mixer], place, after=(first_started,))))
    casts.update({n: _cast_into(n, kernel_view(n, prm[n]), place, MXU_DTYPE, after=(first_started,))
                  for n in GATHER_GROUPS["ffn"]})
    order = [(l, g) for l in range(N_LAYERS) for g in GATHER_GROUPS if (l, g) != (0, "in")]
    rest, rest_started = _allgather_start("rest", [[casts[n][l] for n in GATHER_GROUPS[g]] for l, g in order])
    in_flight.update(zip(order, rest))

    arrived = {}

    def weights_of(l, group, after):
        if (l, group) in arrived:
            return arrived.pop((l, group))
        tag = f"l{l}_{group}"
        if (l, group) == (0, "in"):
            after = after + (rest_started,)
        groups = (group, "mixer") if (l > 0 and group == "in") else (group,)
        waited = [_allgather_wait(f"l{l}_{g}", *in_flight[l, g][:2], in_flight[l, g][2], after) for g in groups]
        bufs = _allgather_forward(tag, [b for w in waited for b in w])
        for g in groups:
            fw = dict(zip(GATHER_GROUPS[g], bufs[:len(GATHER_GROUPS[g])]))
            bufs = bufs[len(GATHER_GROUPS[g]):]
            if "conv_w_dw" in fw:
                fw["conv_w_dw"] = fw["conv_w_dw"].transpose(1, 0, 2).reshape(CONV_KERNEL, -1)
            arrived[l, g] = fw
        return arrived.pop((l, group))

    pending, small_pending, small_shapes = {}, {}, {}
    tokens = {}

    def on_grads(l, group, grads):
        if group == "small":
            packed = {n: g for n, g in grads.items() if n not in MID}
            small_shapes[l] = {n: g.shape for n, g in packed.items()}
            begun = _allgather_rows_start(f"l{l}", [_pack_rows(list(packed.values()))] + [grads[n] for n in MID])
            small_pending[l], token = begun[:4], begun[4]
        else:
            pending[l, group], token = _reduce_start(f"{l}_{group}", grads)
        tokens[l, group] = token
        return token[0, 0]

    loss, dx, _, _, _ = _local_step(x[0], loss_target[0], weights_of, prm, place, on_grads)
    loss = lax.psum(loss, ("x", "y", "c"))

    reduced = [{} for _ in range(N_LAYERS)]
    out = {}

    def finish(l, groups, after):
        reduced[l].update(_reduce_finish(f"l{l}_{groups[0]}", [pending[l, g] for g in groups], place, after))

    def adamw(l, names, done):
        for group in ("in", "ffn", "mixer"):
            members = [n for n in names if n in GATHER_GROUPS[group]]
            if len(members) == 1:
                n = members[0]
                out[n] = _adamw_big(n, l, kernel_view(n, prm[n]), kernel_view(n, mom[n]), kernel_view(n, var[n]),
                                    reduced[l][n], out.get(n), after=done)
                done = (out[n][0],)
            elif members:
                res = _adamw_small_group(group, l, [kernel_view(n, prm[n]) for n in members],
                                         [kernel_view(n, mom[n]) for n in members],
                                         [kernel_view(n, var[n]) for n in members],
                                         [reduced[l][n] for n in members], [out.get(n) for n in members], after=done)
                for i, n in enumerate(members):
                    out[n] = tuple(res[4 * i:4 * i + 4])
                done = (res[0],)
        return done

    top = N_LAYERS - 1
    done = (tokens[0, "in"], tokens[0, "small"])
    finish(top, ("ffn", "mixer", "in"), done)
    done = adamw(top, BIG, done)
    for groups in (("ffn", "mixer"), ("in",)):
        finish(0, groups, done)
        done = adamw(0, [n for g in groups for n in GATHER_GROUPS[g] if n in BIG], done)
    for n in BIG:
        out[n] = tuple(kernel_view(n, a) for a in out[n])

    gsmall = {}
    mid_mine, mid_gathered = [], []
    for l in range(N_LAYERS):
        mine, lands = _allgather_rows_wait(f"l{l}", *small_pending[l], done[0])
        lands = _allgather_rows_forward(f"l{l}", lands)
        mid_mine.append(mine[1:])
        mid_gathered.append(lands[1:])
        gsum = _sum_devices(f"l{l}", lands[0], mine[0], place)
        for n, g in zip(small_shapes[l], _unpack_rows(gsum, list(small_shapes[l].values()))):
            gsmall.setdefault(n, [None] * N_LAYERS)[l] = g
    mid_out = _adamw_mid([prm[n] for n in MID], [mom[n] for n in MID], [var[n] for n in MID], mid_gathered, mid_mine,
                         place)
    for n, res in zip(MID, mid_out):
        out[n] = tuple(kernel_view(n, a) for a in res)
    gsmall = {n: (g[top] if n == "final_norm" else jnp.stack(g)) for n, g in gsmall.items()}
    lanes = dw_shard.shape[-1]
    gsmall["conv_w_dw"] = lax.dynamic_slice_in_dim(gsmall["conv_w_dw"], (2 * cx + cy) * lanes, lanes, axis=2)
    small_names = [n for n in SMALL if n not in MID] + ["conv_w_dw"]
    w_rows = _pack_rows([prm[n] for n in small_names])
    m_rows = _pack_rows([mom[n] for n in small_names])
    v_rows = _pack_rows([var[n] for n in small_names])
    g_rows = _pack_rows([gsmall[n] for n in small_names])
    shapes = [prm[n].shape for n in small_names]
    d_s, m_s, v_s = (_unpack_rows(r, shapes) for r in _adamw_rows(w_rows, m_rows, v_rows, g_rows))
    for i, n in enumerate(small_names):
        out[n] = (gsmall[n].reshape(prm[n].shape), d_s[i], m_s[i], v_s[i])
    grads = [out[n][0] for n in WEIGHTS]
    deltas = [out[n][1] for n in WEIGHTS]
    new_m = [out[n][2] for n in WEIGHTS]
    new_v = [out[n][3] for n in WEIGHTS]
    return (loss, dx[None], *grads, *deltas, *new_m, *new_v)
```

```python
import math

import jax
import jax.numpy as jnp
from jax import lax
from jax.experimental import pallas as pl
from jax.experimental.pallas import tpu as pltpu

F32 = jnp.float32
MXU_DTYPE = jnp.bfloat16
WIRE_DTYPE = jnp.bfloat16
SDS = jax.ShapeDtypeStruct
BS = pl.BlockSpec
ANY = pl.BlockSpec(memory_space=pl.ANY)
HBM = pl.BlockSpec(memory_space=pltpu.HBM)
SEM = pl.BlockSpec(memory_space=pltpu.SEMAPHORE)
SIDE_EFFECT = pltpu.SideEffectType.DATAFLOW_SIDE_EFFECTING
MESH = pl.DeviceIdType.MESH

EPS = 1e-6
N_CHIPS = 4
N_LAYERS = 2
SSM_GROUPS, SSM_STATE, SSM_GROUP = 32, 64, 16
CONV_KERNEL = 31
CONV_PAD = 32
POOL_WINDOWS = (2, 4, 8, 16)
GELU_C = math.sqrt(2.0 / math.pi)
ADAM_LR, ADAM_B1, ADAM_B2, ADAM_EPS, ADAM_WD, ADAM_STEP = 0.001, 0.9, 0.999, 1e-08, 0.01, 10
VMEM_LIMIT = 56 * 1024 * 1024

BIG = ("w_in", "ssm_w_glu", "ssm_w_proj", "conv_w_proj", "pool_w_proj", "w_out", "ffn_w_gate", "ffn_w_up", "ffn_w_down")
TRANSPOSED = ("ffn_w_gate", "ffn_w_up")
MID = ("ssm_b_re", "ssm_b_im", "ssm_c_re", "ssm_c_im")
GATHER_GROUPS = {
    "in": ("w_in",),
    "mixer": ("ssm_w_glu", "ssm_w_proj", "conv_w_proj", "pool_w_proj", "w_out", "conv_w_dw"),
    "ffn": ("ffn_w_gate", "ffn_w_up", "ffn_w_down"),
}
SMALL = ("norm1", "b_gate", "ssm_a_re", "ssm_a_im", "ssm_log_dt", "ssm_b_re", "ssm_b_im", "ssm_c_re", "ssm_c_im",
         "ssm_d", "ssm_b_glu", "conv_b_dw", "conv_ln_g", "conv_ln_b", "pool_w_group", "pool_scale", "norm2",
         "final_norm")
WEIGHTS = ("norm1", "w_in", "b_gate", "ssm_a_re", "ssm_a_im", "ssm_log_dt", "ssm_b_re", "ssm_b_im", "ssm_c_re",
           "ssm_c_im", "ssm_d", "ssm_w_glu", "ssm_b_glu", "ssm_w_proj", "conv_w_dw", "conv_b_dw", "conv_ln_g",
           "conv_ln_b", "conv_w_proj", "pool_w_group", "pool_scale", "pool_w_proj", "w_out", "norm2", "ffn_w_gate",
           "ffn_w_up", "ffn_w_down", "final_norm")


def _params():
    return pltpu.CompilerParams(vmem_limit_bytes=VMEM_LIMIT)


def _mm(a, b):
    return jnp.dot(a.astype(MXU_DTYPE), b.astype(MXU_DTYPE), preferred_element_type=F32)


def _mm_nt(a, b):
    return lax.dot_general(a.astype(MXU_DTYPE), b.astype(MXU_DTYPE), (((1,), (1,)), ((), ())),
                           preferred_element_type=F32)


def _mm_tn(a, b):
    return lax.dot_general(a.astype(MXU_DTYPE), b.astype(MXU_DTYPE), (((0,), (0,)), ((), ())),
                           preferred_element_type=F32)


def _sigmoid(x):
    return jax.nn.sigmoid(x)


def _gelu(x):
    t = jnp.tanh(GELU_C * (x + 0.044715 * (x * x * x)))
    return x * (0.5 * (1.0 + t)), t


def _gelu_grad(x, t):
    return 0.5 * (1.0 + t) + 0.5 * x * (1.0 - t * t) * (GELU_C * (1.0 + 3.0 * 0.044715 * x * x))


def _colsum(v):
    return jnp.sum(v, axis=0, keepdims=True)


def _row_tile(rows, cols, itemsize=4, budget=1536 * 1024):
    best = None
    for t in range(8, rows + 1, 8):
        if rows % t == 0 and t * cols * itemsize <= budget:
            best = t
    return best if best is not None else rows


def _in_proj(l, x, norm1, w_in):
    s, d = x.shape
    nc = w_in.shape[-1]
    tm = min(1024, s)
    nt = s // tm

    def body(x_ref, g_ref, w_ref, z_ref, h_ref, h_all):
        i = pl.program_id(1)
        rows = pl.ds(pl.multiple_of(i * tm, tm), tm)

        @pl.when(pl.program_id(0) == 0)
        def _():
            xv = x_ref[...]
            r = lax.rsqrt(jnp.mean(xv * xv, axis=-1, keepdims=True) + EPS)
            hv = (xv * r * g_ref[...]).astype(h_ref.dtype)
            h_ref[...] = hv.T
            h_all[rows, :] = hv

        z_ref[...] = _mm(h_all[rows, :], w_ref[...])

    tile_of = lambda j, i: i * (1 - jnp.minimum(j, 1)) + (nt - 1) * jnp.minimum(j, 1)
    return pl.pallas_call(
        body, name=f"in_proj_l{l}", grid=(N_CHIPS, nt),
        in_specs=[BS((tm, d), lambda j, i: (tile_of(j, i), 0)), BS((None, 1, d), lambda j, i: (l, 0, 0)),
                  BS((None, d, nc), lambda j, i: (j, 0, 0))],
        out_specs=[BS((tm, nc), lambda j, i: (i, j)), BS((d, tm), lambda j, i: (0, tile_of(j, i)))],
        out_shape=[SDS((s, N_CHIPS * nc), F32), SDS((d, s), MXU_DTYPE)],
        scratch_shapes=[pltpu.VMEM((s, d), MXU_DTYPE)], compiler_params=_params())(x, norm1, w_in)


def _mm_cols(a, w_ref):
    return jnp.concatenate([_mm(a, w_ref[j]) for j in range(N_CHIPS)], axis=1)


def _mm_nt_cols(dv, w_ref):
    nc = w_ref.shape[-1]
    acc = _mm_nt(dv[:, 0:nc], w_ref[0])
    for j in range(1, N_CHIPS):
        acc = acc + _mm_nt(dv[:, j * nc:(j + 1) * nc], w_ref[j])
    return acc


def _merge_values(y, hc, p, zg, wglu, bglu, wpa, wpb, wpc, lng, lnb, wgrp, scale, bg):
    v = {}
    ge, th = _gelu(y)
    t = _mm(ge, wglu) + bglu
    sg = _sigmoid(t)
    sa = ge * sg
    ya = _mm_cols(sa, wpa)
    mu = jnp.mean(hc, axis=-1, keepdims=True)
    xc = hc - mu
    r = lax.rsqrt(jnp.mean(xc * xc, axis=-1, keepdims=True) + EPS)
    xh = xc * r
    ln = xh * lng + lnb
    sl = _sigmoid(ln)
    ac = ln * sl
    yb = _mm_cols(ac, wpb)
    gw = p.shape[1] // len(POOL_WINDOWS)
    q = jnp.concatenate([_mm(p[:, k * gw:(k + 1) * gw], wgrp[k]) for k in range(len(POOL_WINDOWS))], axis=1)
    pp = q * scale
    yc = _mm_cols(pp, wpc)
    d = ya.shape[1]
    gates = [_sigmoid(zg[k] + bg[:, k * d:(k + 1) * d]) for k in range(3)]
    merged = gates[0] * ya + gates[1] * yb + gates[2] * yc
    v.update(ge=ge, th=th, sg=sg, sa=sa, ya=ya, r=r, xh=xh, ln=ln, sl=sl, ac=ac, yb=yb, q=q, pp=pp, yc=yc,
             gates=gates, merged=merged)
    return v


def _merge_specs(l, tm, d, cw):
    row = lambda n: BS((None, 1, n), lambda i: (l, 0, 0))
    resident = lambda shp: BS(shp, lambda i: (0, 0, 0), pipeline_mode=pl.Buffered(1))
    return [
        BS((tm, cw), lambda i: (i, 0)),
        BS((tm, cw), lambda i: (i, 0)),
        BS((tm, cw), lambda i: (i, 0)),
        BS((tm, d), lambda i: (i, 2)), BS((tm, d), lambda i: (i, 3)), BS((tm, d), lambda i: (i, 4)),
        resident((N_CHIPS, cw // N_CHIPS, cw)),
        row(cw),
        resident((N_CHIPS, cw, d // N_CHIPS)),
        resident((N_CHIPS, cw, d // N_CHIPS)),
        resident((N_CHIPS, cw, d // N_CHIPS)),
        row(cw), row(cw),
        BS((None, 4, cw // 4, cw // 4), lambda i: (l, 0, 0, 0)),
        row(cw),
        row(3 * d),
        resident((N_CHIPS, d // N_CHIPS, d)),
    ]


def _merge_fwd(l, x, y, hc, p, z, fw, sp):
    s, d = x.shape
    cw = y.shape[1]
    tm = min(512, s)

    def body(x_ref, y_ref, hc_ref, p_ref, z0, z1, z2, wglu, bglu, wpa, wpb, wpc, lng, lnb, wgrp, scale, bg, wout,
             x1_ref):
        v = _merge_values(y_ref[...], hc_ref[...], p_ref[...], (z0[...], z1[...], z2[...]),
                          wglu[...].reshape(cw, cw), bglu[...], wpa, wpb, wpc, lng[...], lnb[...], wgrp, scale[...],
                          bg[...])
        x1_ref[...] = x_ref[...] + _mm(v["merged"], wout[...].reshape(d, d))

    return pl.pallas_call(
        body, name=f"merge_fwd_l{l}", grid=(s // tm,),
        in_specs=[BS((tm, d), lambda i: (i, 0))] + _merge_specs(l, tm, d, cw),
        out_specs=BS((tm, d), lambda i: (i, 0)), out_shape=SDS((s, d), F32), compiler_params=_params(),
    )(x, y, hc, p, z, z, z, fw["ssm_w_glu"], sp["ssm_b_glu"], fw["ssm_w_proj"], fw["conv_w_proj"], fw["pool_w_proj"],
      sp["conv_ln_g"], sp["conv_ln_b"], sp["pool_w_group"], sp["pool_scale"], sp["b_gate"], fw["w_out"])


def _merge_bwd(l, dx1, y, hc, p, z, fw, sp):
    s, d = dx1.shape
    cw = y.shape[1]
    tm = min(256, s)
    m = MXU_DTYPE

    def body(dx1_ref, y_ref, hc_ref, p_ref, z0, z1, z2, wglu, bglu, wpa, wpb, wpc, lng, lnb, wgrp, scale, bg, wout,
             dzg_ref, dy_ref, dhc_ref, dp_ref, merged_ref, sa_ref, ac_ref, pp_ref, ge_ref, dt_ref, dya_ref, dyb_ref,
             dyc_ref, dq_ref, dbg_ref, dbglu_ref, dlng_ref, dlnb_ref, dscale_ref):
        yv = y_ref[...]
        wg = wglu[...].reshape(cw, cw)
        v = _merge_values(yv, hc_ref[...], p_ref[...], (z0[...], z1[...], z2[...]), wg, bglu[...], wpa, wpb, wpc,
                          lng[...], lnb[...], wgrp, scale[...], bg[...])
        dm = _mm_nt(dx1_ref[...], wout[...].reshape(d, d))
        ys = (v["ya"], v["yb"], v["yc"])
        dys, dbg = [], []
        for k in range(3):
            gk = v["gates"][k]
            dzk = dm * ys[k] * (gk * (1.0 - gk))
            dbg.append(_colsum(dzk))
            dzg_ref[:, k * d:(k + 1) * d] = dzk.astype(m)
            dys.append((dm * gk).astype(m))
        dsa = _mm_nt_cols(dys[0], wpa)
        dac = _mm_nt_cols(dys[1], wpb)
        dpp = _mm_nt_cols(dys[2], wpc)
        ge, sg = v["ge"], v["sg"]
        dt = dsa * ge * (sg * (1.0 - sg))
        dge = dsa * sg + _mm_nt(dt, wg)
        dy_ref[...] = dge * _gelu_grad(yv, v["th"])
        ln, sl, xh = v["ln"], v["sl"], v["xh"]
        dln = dac * (sl * (1.0 + ln * (1.0 - sl)))
        dxh = dln * lng[...]
        dhc_ref[...] = v["r"] * (dxh - jnp.mean(dxh, axis=-1, keepdims=True)
                                 - xh * jnp.mean(dxh * xh, axis=-1, keepdims=True))
        dq = dpp * scale[...]
        gw = cw // len(POOL_WINDOWS)
        for k in range(len(POOL_WINDOWS)):
            dp_ref[:, k * gw:(k + 1) * gw] = _mm_nt(dq[:, k * gw:(k + 1) * gw], wgrp[k])
        merged_ref[...] = v["merged"].astype(m)
        sa_ref[...] = v["sa"].astype(m)
        ac_ref[...] = v["ac"].astype(m)
        pp_ref[...] = v["pp"].astype(m)
        ge_ref[...] = ge.astype(m)
        dt_ref[...] = dt.astype(m)
        dya_ref[...] = dys[0]
        dyb_ref[...] = dys[1]
        dyc_ref[...] = dys[2]
        dq_ref[...] = dq.astype(m)

        @pl.when(pl.program_id(0) == 0)
        def _():
            for ref in (dbg_ref, dbglu_ref, dlng_ref, dlnb_ref, dscale_ref):
                ref[...] = jnp.zeros(ref.shape, F32)

        dbg_ref[...] += jnp.concatenate(dbg, axis=1)
        dbglu_ref[...] += _colsum(dt)
        dlng_ref[...] += _colsum(dln * xh)
        dlnb_ref[...] += _colsum(dln)
        dscale_ref[...] += _colsum(dpp * v["q"])

    tile = lambda n: BS((tm, n), lambda i: (i, 0))
    acc = lambda n: BS((1, n), lambda i: (0, 0))
    outs = pl.pallas_call(
        body, name=f"merge_bwd_l{l}", grid=(s // tm,),
        in_specs=[tile(d)] + _merge_specs(l, tm, d, cw),
        out_specs=[tile(3 * d), tile(cw), tile(cw), tile(cw), tile(d), tile(cw), tile(cw), tile(cw), tile(cw), tile(cw),
                   tile(d), tile(d), tile(d), tile(cw), acc(3 * d), acc(cw), acc(cw), acc(cw), acc(cw)],
        out_shape=[SDS((s, 3 * d), m), SDS((s, cw), F32), SDS((s, cw), F32), SDS((s, cw), F32), SDS((s, d), m),
                   SDS((s, cw), m), SDS((s, cw), m), SDS((s, cw), m), SDS((s, cw), m), SDS((s, cw), m), SDS((s, d), m),
                   SDS((s, d), m), SDS((s, d), m), SDS((s, cw), m), SDS((1, 3 * d), F32), SDS((1, cw), F32),
                   SDS((1, cw), F32), SDS((1, cw), F32), SDS((1, cw), F32)],
        compiler_params=_params(),
    )(dx1, y, hc, p, z, z, z, fw["ssm_w_glu"], sp["ssm_b_glu"], fw["ssm_w_proj"], fw["conv_w_proj"], fw["pool_w_proj"],
      sp["conv_ln_g"], sp["conv_ln_b"], sp["pool_w_group"], sp["pool_scale"], sp["b_gate"], fw["w_out"])
    names = ("dzg", "dy", "dhc", "dp", "merged", "sa", "ac", "pp", "ge", "dt", "dya", "dyb", "dyc", "dq", "db_gate",
             "db_glu", "dln_g", "dln_b", "dscale")
    return dict(zip(names, outs))


def _ffn_fwd(l, x1, norm2, wg, wu, wd):
    s, d = x1.shape
    hc = wd.shape[1]
    tm = min(1024, s)

    def body(x_ref, g_ref, wg_ref, wu_ref, wd_ref, o_ref, gate_ref, up_ref, h_ref):
        @pl.when(pl.program_id(1) == 0)
        def _():
            xv = x_ref[...]
            r = lax.rsqrt(jnp.mean(xv * xv, axis=-1, keepdims=True) + EPS)
            h_ref[...] = (xv * r * g_ref[...]).astype(h_ref.dtype)
            o_ref[...] = xv

        h = h_ref[...]
        gate = _mm_nt(h, wg_ref[...])
        up = _mm_nt(h, wu_ref[...])
        gate_ref[...] = gate
        up_ref[...] = up
        o_ref[...] += _mm(gate * _sigmoid(gate) * up, wd_ref[...])

    chunk = BS((None, tm, hc), lambda i, j: (j, i, 0))
    return pl.pallas_call(
        body, name=f"ffn_fwd_l{l}", grid=(s // tm, N_CHIPS),
        in_specs=[BS((tm, d), lambda i, j: (i, 0)), BS((None, 1, d), lambda i, j: (l, 0, 0)),
                  BS((None, hc, d), lambda i, j: (j, 0, 0)), BS((None, hc, d), lambda i, j: (j, 0, 0)),
                  BS((None, hc, d), lambda i, j: (j, 0, 0))],
        out_specs=[BS((tm, d), lambda i, j: (i, 0)), chunk, chunk, BS((tm, d), lambda i, j: (i, 0))],
        out_shape=[SDS((s, d), F32), SDS((N_CHIPS, s, hc), F32), SDS((N_CHIPS, s, hc), F32), SDS((s, d), MXU_DTYPE)],
        compiler_params=_params())(x1, norm2, wg, wu, wd)


def _ffn_bwd(l, x1, dx2, gate_pre, up_pre, norm2, wg, wu, wd):
    s, d = x1.shape
    hc = wd.shape[1]
    tm = min(512, s)
    m = MXU_DTYPE
    last = N_CHIPS - 1

    def body(x_ref, dx2_ref, gate_ref, up_ref, g_ref, wg_ref, wu_ref, wd_ref, dx1_ref, dxb_ref, act_ref, dgate_ref,
             dup_ref, dn_ref, dh_scr):
        i, j = pl.program_id(0), pl.program_id(1)

        @pl.when(j == 0)
        def _():
            dxb_ref[...] = dx2_ref[...].astype(m)
            dh_scr[...] = jnp.zeros(dh_scr.shape, F32)

        @pl.when((i == 0) & (j == 0))
        def _():
            dn_ref[...] = jnp.zeros(dn_ref.shape, F32)

        gate = gate_ref[...]
        up = up_ref[...]
        sg = _sigmoid(gate)
        silu = gate * sg
        act_ref[...] = (silu * up).astype(m).T
        dact = _mm_nt(dxb_ref[...], wd_ref[...])
        dup = (dact * silu).astype(m)
        dgate = (dact * up * (sg * (1.0 + gate * (1.0 - sg)))).astype(m)
        dup_ref[...] = dup.T
        dgate_ref[...] = dgate.T
        dh_scr[...] += _mm(dgate, wg_ref[...]) + _mm(dup, wu_ref[...])

        @pl.when(j == last)
        def _():
            xv = x_ref[...]
            r = lax.rsqrt(jnp.mean(xv * xv, axis=-1, keepdims=True) + EPS)
            xh = xv * r
            dh = dh_scr[...]
            dn_ref[...] += _colsum(dh * xh)
            dxh = dh * g_ref[...]
            dx1_ref[...] = dx2_ref[...] + r * (dxh - xh * jnp.mean(dxh * xh, axis=-1, keepdims=True))

    chunk = BS((None, hc, tm), lambda i, j: (j, 0, i))
    saved = BS((None, tm, hc), lambda i, j: (j, i, 0))
    outs = pl.pallas_call(
        body, name=f"ffn_bwd_l{l}", grid=(s // tm, N_CHIPS),
        in_specs=[BS((tm, d), lambda i, j: (i, 0)), BS((tm, d), lambda i, j: (i, 0)), saved, saved,
                  BS((None, 1, d), lambda i, j: (l, 0, 0)),
                  BS((None, hc, d), lambda i, j: (j, 0, 0)), BS((None, hc, d), lambda i, j: (j, 0, 0)),
                  BS((None, hc, d), lambda i, j: (j, 0, 0))],
        out_specs=[BS((tm, d), lambda i, j: (i, 0)), BS((tm, d), lambda i, j: (i, 0)),
                   chunk, chunk, chunk, BS((1, d), lambda i, j: (0, 0))],
        out_shape=[SDS((s, d), F32), SDS((s, d), m), SDS((N_CHIPS, hc, s), m),
                   SDS((N_CHIPS, hc, s), m), SDS((N_CHIPS, hc, s), m), SDS((1, d), F32)],
        scratch_shapes=[pltpu.VMEM((tm, d), F32)], compiler_params=_params(),
    )(x1, dx2, gate_pre, up_pre, norm2, wg, wu, wd)
    return dict(zip(("dx1", "dx2", "act", "dgate", "dup", "dnorm2"), outs))


def _loss_head(x, target, gf):
    s, d = x.shape
    tm = min(512, s)

    def body(x_ref, t_ref, g_ref, dx_ref, loss_ref, dg_ref):
        @pl.when(pl.program_id(0) == 0)
        def _():
            loss_ref[...] = jnp.zeros(loss_ref.shape, F32)
            dg_ref[...] = jnp.zeros(dg_ref.shape, F32)

        xv = x_ref[...]
        r = lax.rsqrt(jnp.mean(xv * xv, axis=-1, keepdims=True) + EPS)
        xh = xv * r
        err = xh * g_ref[...] - t_ref[...]
        loss_ref[...] += 0.5 * jnp.sum(jnp.mean(err * err, axis=-1, keepdims=True), axis=0, keepdims=True)
        dyv = err * (1.0 / d)
        dg_ref[...] += _colsum(dyv * xh)
        dxh = dyv * g_ref[...]
        dx_ref[...] = r * (dxh - xh * jnp.mean(dxh * xh, axis=-1, keepdims=True))

    return pl.pallas_call(
        body, name="loss_head", grid=(s // tm,),
        in_specs=[BS((tm, d), lambda i: (i, 0)), BS((tm, d), lambda i: (i, 0)), BS((1, d), lambda i: (0, 0))],
        out_specs=[BS((tm, d), lambda i: (i, 0)), BS((1, 1), lambda i: (0, 0)), BS((1, d), lambda i: (0, 0))],
        out_shape=[SDS((s, d), F32), SDS((1, 1), F32), SDS((1, d), F32)], compiler_params=_params())(x, target, gf)


def _in_proj_bwd(l, dres, x, norm1, w_in, du_a, dv1, dv2, du_c, dzg):
    s, d = x.shape
    nc = w_in.shape[-1]
    tm = min(256, s)
    m = MXU_DTYPE

    def body(dres_ref, x_ref, g_ref, w_ref, a_ref, b1_ref, b2_ref, c_ref, g3_ref, dx_ref, dz_ref, dn_ref):
        @pl.when(pl.program_id(0) == 0)
        def _():
            dn_ref[...] = jnp.zeros(dn_ref.shape, F32)

        dz = jnp.concatenate([a_ref[...], b1_ref[...], b2_ref[...], c_ref[...], g3_ref[...]], axis=1).astype(m)
        dz_ref[...] = dz
        dh = _mm_nt_cols(dz, w_ref)
        xv = x_ref[...]
        r = lax.rsqrt(jnp.mean(xv * xv, axis=-1, keepdims=True) + EPS)
        xh = xv * r
        dn_ref[...] += _colsum(dh * xh)
        dxh = dh * g_ref[...]
        dx_ref[...] = dres_ref[...] + r * (dxh - xh * jnp.mean(dxh * xh, axis=-1, keepdims=True))

    tile = lambda n: BS((tm, n), lambda i: (i, 0))
    return pl.pallas_call(
        body, name=f"in_proj_bwd_l{l}", grid=(s // tm,),
        in_specs=[tile(d), tile(d), BS((None, 1, d), lambda i: (l, 0, 0)),
                  BS((N_CHIPS, d, nc), lambda i: (0, 0, 0), pipeline_mode=pl.Buffered(1)),
                  tile(du_a.shape[1]), tile(dv1.shape[1]), tile(dv2.shape[1]), tile(du_c.shape[1]), tile(dzg.shape[1])],
        out_specs=[tile(d), tile(N_CHIPS * nc), BS((1, d), lambda i: (0, 0))],
        out_shape=[SDS((s, d), F32), SDS((s, N_CHIPS * nc), m), SDS((1, d), F32)], compiler_params=_params(),
    )(dres, x, norm1, w_in, du_a, dv1, dv2, du_c, dzg)


def _tn_matmul(name, a, a_spec, b, b_spec, chunk_shape, grid, place):
    last = grid[1] - 1

    def body(place_ref, a_ref, b_ref, own_ref, wire_ref, *acc):
        part = _mm(a_ref[...], b_ref[...])

        def emit(total):
            wire_ref[...] = total.astype(WIRE_DTYPE)

            @pl.when(pl.program_id(0) == place_ref[0])
            def _():
                own_ref[...] = total

        if last == 0:
            emit(part)
        else:
            @pl.when(pl.program_id(1) == 0)
            def _():
                acc[0][...] = part

            @pl.when(pl.program_id(1) > 0)
            def _():
                acc[0][...] += part

            @pl.when(pl.program_id(1) == last)
            def _():
                emit(acc[0][...])

    zeros = (0,) * len(chunk_shape)
    return pl.pallas_call(
        body, name=name,
        grid_spec=pltpu.PrefetchScalarGridSpec(
            num_scalar_prefetch=1, grid=grid, in_specs=[a_spec, b_spec],
            out_specs=[BS(chunk_shape, lambda j, t, pr: zeros), BS((None,) + chunk_shape, lambda j, t, pr: (j,) + zeros)],
            scratch_shapes=[pltpu.VMEM(chunk_shape, F32)] if last else []),
        out_shape=[SDS(chunk_shape, F32), SDS((N_CHIPS,) + chunk_shape, WIRE_DTYPE)],
        compiler_params=_params())(place, a, b)


def _scan_consts(pw_ref, lanes, reverse):
    sgn = -1.0 if reverse else 1.0
    row = lax.broadcasted_iota(jnp.int32, (8, lanes), 0)
    steps = []
    for i, k in enumerate((1, 2, 4)):
        mask = (row < 8 - k) if reverse else (row >= k)
        steps.append((k, jnp.where(mask, pw_ref[2 * i], 0.0), jnp.where(mask, sgn * pw_ref[2 * i + 1], 0.0)))
    c = 4 if reverse else 3
    return steps, pw_ref[2 * c], sgn * pw_ref[2 * c + 1]


def _scan_block(br, bi, steps, reverse):
    for k, ar, ai in steps:
        sh = 8 - k if reverse else k
        sr = pltpu.roll(br, sh, 0)
        si = pltpu.roll(bi, sh, 0)
        br, bi = br + ar * sr - ai * si, bi + ar * si + ai * sr
    return br, bi


SSM_BLOCK_GROUPS = 8


def _ssm_fwd(l, z, b2_re, b2_im, c2_re, c2_im, pw, dskip):
    s = z.shape[0]
    gc = SSM_BLOCK_GROUPS * SSM_GROUP
    gl = SSM_BLOCK_GROUPS * SSM_STATE
    nblk = b2_re.shape[0] // gc
    tw = b2_re.shape[1]

    def body(u_ref, bre2, bim2, cre2, cim2, pw_ref, d_ref, hre, him, y_ref):
        u = u_ref[...]
        hre[...] = _mm(u, _block_matrix(bre2, gl))
        him[...] = _mm(u, _block_matrix(bim2, gl))
        steps, car, cai = _scan_consts(pw_ref, gl, False)

        def step(i, carry):
            cr, ci = carry
            r0 = pl.multiple_of(i * 8, 8)
            br, bi = _scan_block(hre[pl.ds(r0, 8), :], him[pl.ds(r0, 8), :], steps, False)
            hr = br + car * cr - cai * ci
            hi = bi + car * ci + cai * cr
            hre[pl.ds(r0, 8), :] = hr
            him[pl.ds(r0, 8), :] = hi
            return jnp.broadcast_to(hr[7:8, :], (8, gl)), jnp.broadcast_to(hi[7:8, :], (8, gl))

        zero = jnp.zeros((8, gl), F32)
        lax.fori_loop(0, s // 8, step, (zero, zero))
        y_ref[...] = (_mm_nt(hre[...], _block_matrix(cre2, gl)) - _mm_nt(him[...], _block_matrix(cim2, gl))
                      + d_ref[...] * u)

    twice = BS((gc, tw), lambda k: (k, 0))
    return pl.pallas_call(
        body, name=f"ssm_fwd_l{l}", grid=(nblk,),
        in_specs=[BS((s, gc), lambda k: (0, k)), twice, twice, twice, twice, BS((10, 8, gl), lambda k: (0, 0, k)),
                  BS((1, gc), lambda k: (0, k))],
        out_specs=[BS((s, gl), lambda k: (0, k)), BS((s, gl), lambda k: (0, k)), BS((s, gc), lambda k: (0, k))],
        out_shape=[SDS((s, nblk * gl), F32), SDS((s, nblk * gl), F32), SDS((s, nblk * gc), F32)],
        compiler_params=_params())(z, b2_re, b2_im, c2_re, c2_im, pw, dskip)


def _ssm_bwd(l, dy, z, hre, him, b2_re, b2_im, c2_re, c2_im, pw, dskip):
    s = z.shape[0]
    gc = SSM_BLOCK_GROUPS * SSM_GROUP
    gl = SSM_BLOCK_GROUPS * SSM_STATE
    nblk = b2_re.shape[0] // gc
    tw = b2_re.shape[1]

    def body(dy_ref, u_ref, hre_ref, him_ref, bre2, bim2, cre2, cim2, pw_ref, d_ref,
             du_ref, dbre_ref, dbim_ref, dcre_ref, dcim_ref, dar_ref, dai_ref, dd_ref, gre, gim):
        dyv = dy_ref[...]
        u = u_ref[...]
        gre[...] = _mm(dyv, _block_matrix(cre2, gl))
        gim[...] = -_mm(dyv, _block_matrix(cim2, gl))
        dcre_ref[...] = _block_diagonal_of(_mm_tn(dyv, hre_ref[...]))
        dcim_ref[...] = -_block_diagonal_of(_mm_tn(dyv, him_ref[...]))
        dd_ref[...] = _colsum(dyv * u)
        row = lax.broadcasted_iota(jnp.int32, (8, gl), 0)
        steps, car, cai = _scan_consts(pw_ref, gl, True)
        n8 = s // 8

        def step(ii, carry):
            cr, ci, accr, acci = carry
            i = n8 - 1 - ii
            r0 = pl.multiple_of(i * 8, 8)
            br, bi = _scan_block(gre[pl.ds(r0, 8), :], gim[pl.ds(r0, 8), :], steps, True)
            dr = br + car * cr - cai * ci
            di = bi + car * ci + cai * cr
            gre[pl.ds(r0, 8), :] = dr
            gim[pl.ds(r0, 8), :] = di
            rp = pl.multiple_of(jnp.maximum(i - 1, 0) * 8, 8)
            keep = jnp.where(i > 0, 1.0, 0.0)
            pr = jnp.where(row >= 1, pltpu.roll(hre_ref[pl.ds(r0, 8), :], 1, 0),
                           keep * pltpu.roll(hre_ref[pl.ds(rp, 8), :], 1, 0))
            pi = jnp.where(row >= 1, pltpu.roll(him_ref[pl.ds(r0, 8), :], 1, 0),
                           keep * pltpu.roll(him_ref[pl.ds(rp, 8), :], 1, 0))
            accr = accr + dr * pr + di * pi
            acci = acci + di * pr - dr * pi
            return (jnp.broadcast_to(dr[0:1, :], (8, gl)), jnp.broadcast_to(di[0:1, :], (8, gl)), accr, acci)

        zero = jnp.zeros((8, gl), F32)
        _, _, accr, acci = lax.fori_loop(0, n8, step, (zero, zero, zero, zero))
        dar_ref[...] = _colsum(accr)
        dai_ref[...] = _colsum(acci)
        dbr = gre[...]
        dbi = gim[...]
        du_ref[...] = (dyv * d_ref[...] + _mm_nt(dbr, _block_matrix(bre2, gl))
                       + _mm_nt(dbi, _block_matrix(bim2, gl))).astype(du_ref.dtype)
        dbre_ref[...] = _block_diagonal_of(_mm_tn(u, dbr))
        dbim_ref[...] = _block_diagonal_of(_mm_tn(u, dbi))

    col = lambda n: BS((s, n), lambda k: (0, k))
    twice = BS((gc, tw), lambda k: (k, 0))
    diag = BS((gc, SSM_STATE), lambda k: (k, 0))
    outs = pl.pallas_call(
        body, name=f"ssm_bwd_l{l}", grid=(nblk,),
        in_specs=[col(gc), col(gc), col(gl), col(gl), twice, twice, twice, twice,
                  BS((10, 8, gl), lambda k: (0, 0, k)), BS((1, gc), lambda k: (0, k))],
        out_specs=[col(gc), diag, diag, diag, diag, BS((1, gl), lambda k: (0, k)),
                   BS((1, gl), lambda k: (0, k)), BS((1, gc), lambda k: (0, k))],
        out_shape=[SDS((s, nblk * gc), MXU_DTYPE)] + [SDS((nblk * gc, SSM_STATE), F32)] * 4
        + [SDS((1, nblk * gl), F32), SDS((1, nblk * gl), F32), SDS((1, nblk * gc), F32)],
        scratch_shapes=[pltpu.VMEM((s, gl), F32), pltpu.VMEM((s, gl), F32)], compiler_params=_params(),
    )(dy, z, hre, him, b2_re, b2_im, c2_re, c2_im, pw, dskip)
    return dict(zip(("du", "dbbar_re", "dbbar_im", "dc_re", "dc_im", "dabar_re", "dabar_im", "dd"), outs))


def _conv_fwd(l, z, wdw, bdw):
    s = z.shape[0]
    cw = wdw.shape[1]
    lb = 128
    tr = min(256, s)
    off1 = cw // lb
    off2 = 2 * cw // lb

    def body(v1_ref, v2_ref, w_ref, b_ref, hc_ref, scr):
        scr[0:CONV_PAD, :] = jnp.zeros((CONV_PAD, lb), F32)
        scr[CONV_PAD:, :] = v1_ref[...] * _sigmoid(v2_ref[...])
        for t in range(s // tr):
            acc = jnp.broadcast_to(b_ref[...], (tr, lb))
            for k in range(CONV_KERNEL):
                acc = acc + w_ref[pl.ds(k, 1), :] * scr[pl.ds(t * tr + CONV_PAD - (CONV_KERNEL - 1) + k, tr), :]
            hc_ref[pl.ds(t * tr, tr), :] = acc

    return pl.pallas_call(
        body, name=f"conv_fwd_l{l}", grid=(cw // lb,),
        in_specs=[BS((s, lb), lambda k: (0, off1 + k)), BS((s, lb), lambda k: (0, off2 + k)),
                  BS((CONV_KERNEL, lb), lambda k: (0, k)), BS((1, lb), lambda k: (0, k))],
        out_specs=BS((s, lb), lambda k: (0, k)), out_shape=SDS((s, cw), F32),
        scratch_shapes=[pltpu.VMEM((s + CONV_PAD, lb), F32)], compiler_params=_params())(z, z, wdw, bdw)


def _conv_bwd(l, dhc, z, wdw):
    s = z.shape[0]
    cw = wdw.shape[1]
    lb = 128
    tr = min(256, s)
    off1 = cw // lb
    off2 = 2 * cw // lb
    nb = cw // lb

    def body(d_ref, v1_ref, v2_ref, w_ref, dv1_ref, dv2_ref, dw_ref, db_ref, hpad, dpad):
        v1 = v1_ref[...]
        sg = _sigmoid(v2_ref[...])
        dv = d_ref[...]
        hpad[0:CONV_PAD, :] = jnp.zeros((CONV_PAD, lb), F32)
        hpad[CONV_PAD:, :] = v1 * sg
        dpad[0:s, :] = dv
        dpad[s:, :] = jnp.zeros((CONV_PAD, lb), F32)
        db_ref[...] = _colsum(dv)
        dws = [jnp.zeros((1, lb), F32) for _ in range(CONV_KERNEL)]
        for t in range(s // tr):
            dt = d_ref[pl.ds(t * tr, tr), :]
            acc = jnp.zeros((tr, lb), F32)
            for k in range(CONV_KERNEL):
                acc = acc + w_ref[pl.ds(k, 1), :] * dpad[pl.ds(t * tr + (CONV_KERNEL - 1) - k, tr), :]
                dws[k] = dws[k] + _colsum(dt * hpad[pl.ds(t * tr + CONV_PAD - (CONV_KERNEL - 1) + k, tr), :])
            sgt = _sigmoid(v2_ref[pl.ds(t * tr, tr), :])
            v1t = v1_ref[pl.ds(t * tr, tr), :]
            dv1_ref[pl.ds(t * tr, tr), :] = (acc * sgt).astype(dv1_ref.dtype)
            dv2_ref[pl.ds(t * tr, tr), :] = (acc * v1t * (sgt * (1.0 - sgt))).astype(dv2_ref.dtype)
        for k in range(CONV_KERNEL):
            dw_ref[pl.ds(k, 1), :] = dws[k]

    return pl.pallas_call(
        body, name=f"conv_bwd_l{l}", grid=(nb,),
        in_specs=[BS((s, lb), lambda k: (0, k)), BS((s, lb), lambda k: (0, off1 + k)),
                  BS((s, lb), lambda k: (0, off2 + k)), BS((CONV_KERNEL, lb), lambda k: (0, k))],
        out_specs=[BS((s, lb), lambda k: (0, k)), BS((s, lb), lambda k: (0, k)),
                   BS((CONV_KERNEL, lb), lambda k: (0, k)), BS((1, lb), lambda k: (0, k))],
        out_shape=[SDS((s, cw), MXU_DTYPE), SDS((s, cw), MXU_DTYPE), SDS((CONV_KERNEL, cw), F32), SDS((1, cw), F32)],
        scratch_shapes=[pltpu.VMEM((s + CONV_PAD, lb), F32), pltpu.VMEM((s + CONV_PAD, lb), F32)],
        compiler_params=_params())(dhc, z, z, wdw)


def _pool_window(k):
    return jnp.where(k == 0, float(POOL_WINDOWS[0]),
                     jnp.where(k == 1, float(POOL_WINDOWS[1]),
                               jnp.where(k == 2, float(POOL_WINDOWS[2]), float(POOL_WINDOWS[3]))))


def _pool_fwd(l, z, pw_width):
    s = z.shape[0]
    lb = pw_width // len(POOL_WINDOWS)
    off = 3 * pw_width // lb

    def body(u_ref, p_ref):
        k = pl.program_id(0)
        u = u_ref[...]
        row = lax.broadcasted_iota(jnp.int32, (s, lb), 0)
        sums = [u]
        for sh in (1, 2, 4, 8):
            prev = sums[-1]
            sums.append(prev + jnp.where(row >= sh, pltpu.roll(prev, sh, 0), 0.0))
        sel = jnp.where(k == 0, sums[1], jnp.where(k == 1, sums[2], jnp.where(k == 2, sums[3], sums[4])))
        cnt = jnp.minimum((row + 1).astype(F32), _pool_window(k))
        p_ref[...] = sel / cnt - u

    return pl.pallas_call(
        body, name=f"pool_fwd_l{l}", grid=(len(POOL_WINDOWS),),
        in_specs=[BS((s, lb), lambda k: (0, off + k))], out_specs=BS((s, lb), lambda k: (0, k)),
        out_shape=SDS((s, pw_width), F32), compiler_params=_params())(z)


def _pool_bwd(l, dp):
    s, width = dp.shape
    lb = width // len(POOL_WINDOWS)

    def body(d_ref, du_ref):
        k = pl.program_id(0)
        dv = d_ref[...]
        row = lax.broadcasted_iota(jnp.int32, (s, lb), 0)
        cnt = jnp.minimum((row + 1).astype(F32), _pool_window(k))
        sums = [dv / cnt]
        for sh in (1, 2, 4, 8):
            prev = sums[-1]
            sums.append(prev + jnp.where(row < s - sh, pltpu.roll(prev, s - sh, 0), 0.0))
        sel = jnp.where(k == 0, sums[1], jnp.where(k == 1, sums[2], jnp.where(k == 2, sums[3], sums[4])))
        du_ref[...] = (sel - dv).astype(du_ref.dtype)

    return pl.pallas_call(
        body, name=f"pool_bwd_l{l}", grid=(len(POOL_WINDOWS),),
        in_specs=[BS((s, lb), lambda k: (0, k))], out_specs=BS((s, lb), lambda k: (0, k)),
        out_shape=SDS((s, width), MXU_DTYPE), compiler_params=_params())(dp)


def _zoh(a_re, a_im, log_dt):
    dt = jnp.exp(log_dt)
    mag = jnp.exp(dt * a_re)
    ang = dt * a_im
    abar_re = mag * jnp.cos(ang)
    abar_im = mag * jnp.sin(ang)
    den = a_re * a_re + a_im * a_im
    nr = abar_re - 1.0
    ni = abar_im
    f_re = (nr * a_re + ni * a_im) / den
    f_im = (ni * a_re - nr * a_im) / den
    return abar_re, abar_im, f_re, f_im


def _zoh_fwd(l, a_re, a_im, log_dt):
    def body(ar, ai, ld, o0, o1, o2, o3):
        for ref, val in zip((o0, o1, o2, o3), _zoh(ar[...], ai[...], ld[...])):
            ref[...] = val

    return pl.pallas_call(body, name=f"zoh_fwd_l{l}", out_shape=[SDS(a_re.shape, F32)] * 4)(a_re, a_im, log_dt)


def _zoh_bwd(l, a_re, a_im, log_dt, cts):
    def body(ar, ai, ld, c0, c1, c2, c3, dar, dai, dld):
        _, vjp = jax.vjp(_zoh, ar[...], ai[...], ld[...])
        g = vjp((c0[...], c1[...], c2[...], c3[...]))
        dar[...] = g[0]
        dai[...] = g[1]
        dld[...] = g[2]

    return pl.pallas_call(body, name=f"zoh_bwd_l{l}",
                          out_shape=[SDS(a_re.shape, F32), SDS(a_re.shape, F32), SDS(log_dt.shape, F32)],
                          )(a_re, a_im, log_dt, *cts)


def _bbar_fwd(l, f_re, f_im, b_re, b_im, c_re, c_im):
    g, p, n = b_re.shape[1:]
    m = MXU_DTYPE

    def body(fr, fi, br, bi, cr, ci, *outs):
        r = lax.broadcasted_iota(jnp.int32, (n, 2 * n), 0)
        c = lax.broadcasted_iota(jnp.int32, (n, 2 * n), 1)
        twice = jnp.where((c & (n - 1)) == r, 1.0, 0.0).astype(m)
        vals = (fr[...] * br[...] - fi[...] * bi[...], fr[...] * bi[...] + fi[...] * br[...], cr[...], ci[...])
        for o_ref, v in zip(outs, vals):
            o_ref[...] = _mm(v.astype(m).reshape(g * p, n), twice).astype(m)

    whole = lambda shp: BS(shp, lambda i: (0,) * len(shp))
    layer = BS((None, g, p, n), lambda i: (l, 0, 0, 0))
    return pl.pallas_call(body, name=f"bbar_fwd_l{l}", grid=(1,),
                          in_specs=[whole((g, 1, n)), whole((g, 1, n)), layer, layer, layer, layer],
                          out_specs=[whole((g * p, 2 * n))] * 4,
                          out_shape=[SDS((g * p, 2 * n), m)] * 4)(f_re, f_im, b_re, b_im, c_re, c_im)


def _block_mask(rows, lanes):
    r = lax.broadcasted_iota(jnp.int32, (rows, lanes), 0)
    c = lax.broadcasted_iota(jnp.int32, (rows, lanes), 1)
    return (r >> (SSM_GROUP.bit_length() - 1)) == (c >> (SSM_STATE.bit_length() - 1))


def _block_matrix(twice_ref, lanes):
    v = twice_ref[...]
    tiled = jnp.concatenate([v] * (lanes // v.shape[1]), axis=1)
    return jnp.where(_block_mask(v.shape[0], lanes), tiled, jnp.zeros_like(tiled))


def _block_diagonal_of(full):
    rows, lanes = full.shape
    kept = jnp.where(_block_mask(rows, lanes), full, 0.0)
    folded = kept[:, 0:128]
    for q in range(1, lanes // 128):
        folded = folded + kept[:, q * 128:(q + 1) * 128]
    return (folded + pltpu.roll(folded, SSM_STATE, 1))[:, :SSM_STATE]


def _bbar_bwd(l, f_re, f_im, b_re, b_im, d_re, d_im):
    g, p, n = b_re.shape[1:]

    def body(fr, fi, br, bi, dr, di, dfr, dfi, dbr, dbi):
        dfr[...] = jnp.sum(dr[...] * br[...] + di[...] * bi[...], axis=1, keepdims=True)
        dfi[...] = jnp.sum(di[...] * br[...] - dr[...] * bi[...], axis=1, keepdims=True)
        dbr[...] = fr[...] * dr[...] + fi[...] * di[...]
        dbi[...] = fr[...] * di[...] - fi[...] * dr[...]

    whole = lambda shp: BS(shp, lambda i: (0,) * len(shp))
    layer = BS((None, g, p, n), lambda i: (l, 0, 0, 0))
    return pl.pallas_call(body, name=f"bbar_bwd_l{l}", grid=(1,),
                          in_specs=[whole((g, 1, n)), whole((g, 1, n)), layer, layer, whole((g, p, n)),
                                    whole((g, p, n))],
                          out_specs=[whole((g, 1, n)), whole((g, 1, n)), whole((g, p, n)), whole((g, p, n))],
                          out_shape=[SDS((g, 1, n), F32), SDS((g, 1, n), F32), SDS((g, p, n), F32),
                                     SDS((g, p, n), F32)])(f_re, f_im, b_re, b_im, d_re, d_im)


def _powers(l, abar_re, abar_im):
    lanes = abar_re.shape[1]

    def body(ar_ref, ai_ref, o_ref):
        ar, ai = ar_ref[...], ai_ref[...]
        pows = [(ar, ai)]
        for _ in range(7):
            pr, pi = pows[-1]
            pows.append((pr * ar - pi * ai, pr * ai + pi * ar))
        row = lax.broadcasted_iota(jnp.int32, (8, lanes), 0)
        for i, k in enumerate((1, 2, 4)):
            o_ref[2 * i] = jnp.broadcast_to(pows[k - 1][0], (8, lanes))
            o_ref[2 * i + 1] = jnp.broadcast_to(pows[k - 1][1], (8, lanes))
        for slot, order in ((3, range(8)), (4, range(7, -1, -1))):
            vr = jnp.zeros((8, lanes), F32)
            vi = jnp.zeros((8, lanes), F32)
            for r, e in enumerate(order):
                vr = jnp.where(row == r, pows[e][0], vr)
                vi = jnp.where(row == r, pows[e][1], vi)
            o_ref[2 * slot] = vr
            o_ref[2 * slot + 1] = vi

    return pl.pallas_call(body, name=f"powers_l{l}", out_shape=SDS((10, 8, lanes), F32))(abar_re, abar_im)


def _ssm_prepare(l, prm):
    g, n, p = SSM_GROUPS, SSM_STATE, SSM_GROUP
    a_re, a_im = prm["ssm_a_re"][l], prm["ssm_a_im"][l]
    log_dt = prm["ssm_log_dt"][l].reshape(g, 1)
    abar_re, abar_im, f_re, f_im = _zoh_fwd(l, a_re, a_im, log_dt)
    f_re, f_im = f_re.reshape(g, 1, n), f_im.reshape(g, 1, n)
    b2_re, b2_im, c2_re, c2_im = _bbar_fwd(l, f_re, f_im, prm["ssm_b_re"], prm["ssm_b_im"], prm["ssm_c_re"],
                                           prm["ssm_c_im"])
    pw = _powers(l, abar_re.reshape(1, g * n), abar_im.reshape(1, g * n))
    return dict(a_re=a_re, a_im=a_im, log_dt=log_dt, f_re=f_re, f_im=f_im, b2_re=b2_re, b2_im=b2_im, c2_re=c2_re,
                c2_im=c2_im, pw=pw, dskip=prm["ssm_d"][l].reshape(1, g * p))


def _ssm_param_grads(l, sd, r, prm):
    g, n, p = SSM_GROUPS, SSM_STATE, SSM_GROUP
    dfr, dfi, db_re, db_im = _bbar_bwd(l, sd["f_re"], sd["f_im"], prm["ssm_b_re"], prm["ssm_b_im"],
                                       r["dbbar_re"].reshape(g, p, n), r["dbbar_im"].reshape(g, p, n))
    cts = (r["dabar_re"].reshape(g, n), r["dabar_im"].reshape(g, n), dfr.reshape(g, n), dfi.reshape(g, n))
    da_re, da_im, dlog_dt = _zoh_bwd(l, sd["a_re"], sd["a_im"], sd["log_dt"], cts)
    return dict(ssm_a_re=da_re, ssm_a_im=da_im, ssm_log_dt=dlog_dt.reshape(g), ssm_b_re=db_re, ssm_b_im=db_im,
                ssm_c_re=r["dc_re"].reshape(g, p, n), ssm_c_im=r["dc_im"].reshape(g, p, n),
                ssm_d=r["dd"].reshape(g, p))


def _ffn_weight_grads(l, fb, dx2, s, place):
    d = dx2.shape[1]
    hcn = fb["act"].shape[1]
    g = {}
    for name, key, rhs in (("ffn_w_gate", "dgate", fb["h2"]), ("ffn_w_up", "dup", fb["h2"]),
                           ("ffn_w_down", "act", fb["dx2"])):
        g[name] = _tn_matmul(f"d{name}_l{l}", fb[key], BS((None, hcn, s), lambda j, t, pr: (j, 0, 0)), rhs,
                             BS((s, d), lambda j, t, pr: (0, 0)), (hcn, d), (N_CHIPS, 1), place)
    return g


def _in_weight_grad(l, ht, dz, place):
    d, s = ht.shape
    ncw = dz.shape[1] // N_CHIPS
    return _tn_matmul(f"dw_in_l{l}", ht, BS((d, s), lambda j, t, pr: (0, 0)), dz, BS((s, ncw), lambda j, t, pr: (0, j)),
                      (d, ncw), (N_CHIPS, 1), place)


def _fused_tn(name, pairs, kinds, s, place):
    n = len(pairs)

    def shape_of(a, b, kind):
        k, m = a.shape[1], b.shape[1]
        if kind == "rows":
            return (N_CHIPS, k // N_CHIPS, m)
        if kind == "cols":
            return (N_CHIPS, k, m // N_CHIPS)
        return (k // 128, 128, 128)

    shapes = [shape_of(a, b, kind) for (a, b), kind in zip(pairs, kinds)]
    out_shape = []
    for shp, kind in zip(shapes, kinds):
        out_shape += [SDS(shp, F32)] if kind == "groups" else [SDS(shp[1:], F32), SDS(shp, WIRE_DTYPE)]

    def body(place_ref, *refs):
        ins, outs, accs = refs[:2 * n], refs[2 * n:2 * n + len(out_shape)], refs[2 * n + len(out_shape):]
        o = 0
        for i, kind in enumerate(kinds):
            a, b = ins[2 * i][...], ins[2 * i + 1][...]
            if kind == "groups":
                for k in range(shapes[i][0]):
                    outs[o][k] = _mm_tn(a[:, k * 128:(k + 1) * 128], b[:, k * 128:(k + 1) * 128])
                o += 1
                continue
            acc = accs[i]
            if kind == "rows":
                acc[...] = _mm_tn(a, b).reshape(acc.shape)
            else:
                full = _mm_tn(a, b)
                nc = acc.shape[2]
                for j in range(N_CHIPS):
                    acc[j] = full[:, j * nc:(j + 1) * nc]
            outs[o][...] = acc[place_ref[0]]
            outs[o + 1][...] = acc[...].astype(WIRE_DTYPE)
            o += 2

    whole = lambda shp: BS(shp, lambda t, pr: (0,) * len(shp))
    outs = pl.pallas_call(
        body, name=name,
        grid_spec=pltpu.PrefetchScalarGridSpec(
            num_scalar_prefetch=1, grid=(1,),
            in_specs=[whole(v.shape) for pair in pairs for v in pair],
            out_specs=[whole(o.shape) for o in out_shape],
            scratch_shapes=[pltpu.VMEM(shp, F32) for shp in shapes]),
        out_shape=out_shape, compiler_params=_params(),
    )(place, *[v for pair in pairs for v in pair])
    res, o = [], 0
    for kind in kinds:
        if kind == "groups":
            res.append(outs[o])
            o += 1
        else:
            res.append((outs[o], outs[o + 1]))
            o += 2
    return res


def _mixer_weight_grads(l, sv, mb, dx1, s, place):
    g = {}
    (g["w_out"], g["ssm_w_glu"]) = _fused_tn(f"dw_out_glu_l{l}", [(mb["merged"], dx1), (mb["ge"], mb["dt"])],
                                            ("rows", "rows"), s, place)
    (g["ssm_w_proj"], g["conv_w_proj"], g["pool_w_proj"]) = _fused_tn(
        f"dw_proj_l{l}", [(mb["sa"], mb["dya"]), (mb["ac"], mb["dyb"]), (mb["pp"], mb["dyc"])],
        ("cols", "cols", "cols"), s, place)
    (dwgrp,) = _fused_tn(f"dpool_w_group_l{l}", [(sv["p"], mb["dq"])], ("groups",), s, place)
    return g, dwgrp


def _local_step(x, target, weights_of, prm, place, on_grads=None):
    s, d = x.shape
    cw = prm["ssm_b_glu"].shape[1]
    sp = {k: prm[k].reshape(N_LAYERS, 1, -1) for k in ("norm1", "norm2", "b_gate", "ssm_b_glu", "conv_ln_g", "conv_ln_b",
                                                        "pool_scale", "conv_b_dw")}
    sp["pool_w_group"] = prm["pool_w_group"]
    saved = []
    xin = x
    prepared = [_ssm_prepare(l, prm) for l in range(N_LAYERS)]
    ready = tuple(sd[k] for sd in prepared for k in ("pw", "b2_re", "c2_re"))
    for l in range(N_LAYERS):
        fw = weights_of(l, "in", (xin,) + (ready if l == 0 else ()))
        sd = prepared[l]
        z, h = _in_proj(l, xin, sp["norm1"], fw["w_in"])
        hre, him, y = _ssm_fwd(l, z, sd["b2_re"], sd["b2_im"], sd["c2_re"], sd["c2_im"], sd["pw"], sd["dskip"])
        p = _pool_fwd(l, z, cw)
        fw.update(weights_of(l, "mixer", (y, p)))
        wdw = fw["conv_w_dw"]
        hc = _conv_fwd(l, z, wdw, sp["conv_b_dw"][l])
        x1 = _merge_fwd(l, xin, y, hc, p, z, fw, sp)
        fw.update(weights_of(l, "ffn", (x1,)))
        x2, gate, up, h2 = _ffn_fwd(l, x1, sp["norm2"], fw["ffn_w_gate"], fw["ffn_w_up"], fw["ffn_w_down"])
        saved.append(dict(x=xin, z=z, h=h, hre=hre, him=him, y=y, hc=hc, p=p, x1=x1, sd=sd, wdw=wdw, fw=fw,
                          gate=gate, up=up, h2=h2))
        xin = x2
    dx, loss, dfinal = _loss_head(xin, target, prm["final_norm"].reshape(1, d))
    big = [None] * N_LAYERS
    small = [None] * N_LAYERS
    norm2_rows = sp["norm2"]
    started = (lambda l, group, grads: on_grads(l, group, grads)) if on_grads is not None else (lambda *a: 0.0)
    for l in reversed(range(N_LAYERS)):
        sv = saved[l]
        sd, fw = sv["sd"], sv["fw"]
        fb = _ffn_bwd(l, sv["x1"], dx, sv["gate"], sv["up"], norm2_rows, fw["ffn_w_gate"], fw["ffn_w_up"],
                      fw["ffn_w_down"])
        fb["h2"] = sv["h2"]
        big[l] = _ffn_weight_grads(l, fb, dx, s, place)
        spl = dict(sp, ssm_b_glu=sp["ssm_b_glu"] + started(l, "ffn", big[l]))
        mb = _merge_bwd(l, fb["dx1"], sv["y"], sv["hc"], sv["p"], sv["z"], fw, spl)
        mixer, dwgrp = _mixer_weight_grads(l, sv, mb, fb["dx1"], s, place)
        du_c = _pool_bwd(l, mb["dp"])
        dv1, dv2, dwdw, dbdw = _conv_bwd(l, mb["dhc"], sv["z"], sv["wdw"])
        sr = _ssm_bwd(l, mb["dy"], sv["z"], sv["hre"], sv["him"], sd["b2_re"], sd["b2_im"], sd["c2_re"],
                      sd["c2_im"], sd["pw"], sd["dskip"])
        dx, dz, dnorm1 = _in_proj_bwd(l, fb["dx1"], sv["x"], sp["norm1"], fw["w_in"], sr["du"], dv1, dv2, du_c, mb["dzg"])
        late = dict(mixer, w_in=_in_weight_grad(l, sv["h"], dz, place))
        big[l].update(late)
        sg = _ssm_param_grads(l, sd, sr, prm)
        sg.update(norm1=dnorm1.reshape(d), b_gate=mb["db_gate"].reshape(3 * d), ssm_b_glu=mb["db_glu"].reshape(cw),
                  conv_b_dw=dbdw.reshape(cw), conv_ln_g=mb["dln_g"].reshape(cw), conv_ln_b=mb["dln_b"].reshape(cw),
                  pool_w_group=dwgrp, pool_scale=mb["dscale"].reshape(cw), norm2=fb["dnorm2"].reshape(d),
                  conv_w_dw=dwdw)
        small[l] = sg
        if l == N_LAYERS - 1:
            sg = dict(sg, final_norm=dfinal.reshape(d))
        norm2_rows = sp["norm2"] + (started(l, "in", late) + started(l, "small", sg))
    return loss[0, 0], dx, big, small, dfinal.reshape(d)


def _place():
    return lax.axis_index("x"), lax.axis_index("y"), lax.axis_index("c")


def _other_chips(x, y):
    return [(1 - x, y), (x, 1 - y), (1 - x, 1 - y)]


def _remote(src, dst, send_sem, recv_sem, device):
    return pltpu.make_async_remote_copy(src_ref=src, dst_ref=dst, send_sem=send_sem, recv_sem=recv_sem,
                                        device_id=device, device_id_type=MESH)


def _hbm(v):
    return pltpu.with_memory_space_constraint(v, pltpu.HBM)


def _cast_into(name, w, place, dtype, after=()):
    nl, k, n = w.shape
    tr = _row_tile(k, n)
    nt = k // tr

    def body(place_ref, w_ref, *rest):
        o0_ref, o1_ref = rest[len(after):]

        @pl.when(pl.program_id(0) == 0)
        def _():
            o0_ref[...] = w_ref[...].astype(dtype)

        @pl.when(pl.program_id(0) == 1)
        def _():
            o1_ref[...] = w_ref[...].astype(dtype)

    return pl.pallas_call(
        body, name=f"cast_{name}",
        grid_spec=pltpu.PrefetchScalarGridSpec(
            num_scalar_prefetch=1, grid=(nl, nt),
            in_specs=[BS((None, tr, n), lambda l, t, pr: (l, t, 0))] + [ANY] * len(after),
            out_specs=[BS((None, tr, n), lambda l, t, pr: (pr[0], t * (1 - l) + (nt - 1) * l, 0)),
                       BS((None, tr, n), lambda l, t, pr: (pr[0], t * l, 0))]),
        out_shape=[SDS((N_CHIPS, k, n), dtype)] * 2)(place, w, *after)


def _cast_small_into(tag, ws, dtypes, place, after=()):
    n = len(ws)

    def body(place_ref, *refs):
        ins, outs = refs[:n], refs[n + len(after):]
        for i in range(n):
            @pl.when(pl.program_id(0) == 0)
            def _():
                outs[2 * i][...] = ins[i][...].astype(dtypes[i])

            @pl.when(pl.program_id(0) == 1)
            def _():
                outs[2 * i + 1][...] = ins[i][...].astype(dtypes[i])

    slot = lambda w: BS((None,) + w.shape[1:], lambda l, pr: (pr[0], 0, 0))
    outs = pl.pallas_call(
        body, name=f"cast_{tag}",
        grid_spec=pltpu.PrefetchScalarGridSpec(
            num_scalar_prefetch=1, grid=(N_LAYERS,),
            in_specs=[BS((None,) + w.shape[1:], lambda l, pr: (l, 0, 0)) for w in ws] + [ANY] * len(after),
            out_specs=[slot(w) for w in ws for _ in range(N_LAYERS)]),
        out_shape=[SDS((N_CHIPS,) + w.shape[1:], dt) for w, dt in zip(ws, dtypes) for _ in range(N_LAYERS)],
    )(place, *ws, *after)
    return [tuple(outs[N_LAYERS * i:N_LAYERS * (i + 1)]) for i in range(n)]


def _gather_rows(buf, c):
    k = buf.shape[1]
    if k % 2:
        return pl.ds(0, k)
    return pl.ds(pl.multiple_of(c * (k // 2), 8), k // 2)


def _allgather_start(tag, groups):
    ng = len(groups)
    sizes = [len(g) for g in groups]
    first = [sum(sizes[:g]) for g in range(ng)]
    flat = [b for g in groups for b in g]
    nb = len(flat)

    def body(*refs):
        ins = refs[:nb]
        sems = refs[nb:nb + 2 * ng]
        token = refs[-1]
        x, y, c = _place()
        jme = 2 * x + y
        for g in range(ng):
            for a in range(sizes[g]):
                buf = ins[first[g] + a]
                blk = buf.at[jme, _gather_rows(buf, c)]
                for k, (cx, cy) in enumerate(_other_chips(x, y)):
                    _remote(blk, blk, sems[2 * g].at[3 * a + k], sems[2 * g + 1].at[3 * a + k], (cx, cy, c)).start()
        token[...] = jnp.zeros(token.shape, F32)

    sem_shapes = [pltpu.SemaphoreType.DMA((3 * sizes[g // 2],)) for g in range(2 * ng)]
    outs = pl.pallas_call(
        body, name=f"allgather_start_{tag}", in_specs=[HBM] * nb,
        out_specs=[SEM] * (2 * ng) + [HBM] * nb + [pl.BlockSpec(memory_space=pltpu.VMEM)],
        out_shape=sem_shapes + [pltpu.HBM(b.shape, b.dtype) for b in flat] + [SDS((8, 128), F32)],
        input_output_aliases={i: 2 * ng + i for i in range(nb)},
        compiler_params=pltpu.CompilerParams(has_side_effects=SIDE_EFFECT))(*[_hbm(b) for b in flat])
    per_group = [(outs[2 * g], outs[2 * g + 1], outs[2 * ng + first[g]:2 * ng + first[g] + sizes[g]])
                 for g in range(ng)]
    return per_group, outs[-1]


def _allgather_wait(l, send_sems, recv_sems, bufs, after):
    n = len(bufs)

    def body(*refs):
        ins = refs[:n]
        ssem, rsem = refs[n], refs[n + 1]
        x, y, c = _place()
        jme = 2 * x + y
        for a in range(n):
            rows = _gather_rows(ins[a], c)
            for k, (cx, cy) in enumerate(_other_chips(x, y)):
                cp = _remote(ins[a].at[jme, rows], ins[a].at[2 * cx + cy, rows], ssem.at[3 * a + k],
                             rsem.at[3 * a + k], (cx, cy, c))
                cp.wait_send()
                cp.wait_recv()

    return pl.pallas_call(
        body, name=f"allgather_wait_{l}", in_specs=[HBM] * n + [SEM, SEM] + [ANY] * len(after), out_specs=[HBM] * n,
        out_shape=[pltpu.HBM(b.shape, b.dtype) for b in bufs], input_output_aliases={i: i for i in range(n)},
        compiler_params=pltpu.CompilerParams(has_side_effects=SIDE_EFFECT))(*bufs, send_sems, recv_sems, *after)


def _allgather_forward(l, bufs):
    n = len(bufs)
    split = [a for a in range(n) if bufs[a].shape[1] % 2 == 0]

    def body(*refs):
        ins = refs[:n]
        send_sems, recv_sems = refs[2 * n:]
        x, y, c = _place()
        sibling = (x, y, 1 - c)
        copies = []
        for a in split:
            for k, (cx, cy) in enumerate(_other_chips(x, y)):
                blk = ins[a].at[2 * cx + cy, _gather_rows(ins[a], c)]
                cp = _remote(blk, blk, send_sems.at[a, k], recv_sems.at[a, k], sibling)
                cp.start()
                copies.append(cp)
        for a in split:
            for k, (cx, cy) in enumerate(_other_chips(x, y)):
                blk = ins[a].at[2 * cx + cy, _gather_rows(ins[a], 1 - c)]
                _remote(blk, blk, send_sems.at[a, k], recv_sems.at[a, k], sibling).wait_recv()
        for cp in copies:
            cp.wait_send()

    sem = pltpu.SemaphoreType.DMA((n, 3))
    return pl.pallas_call(
        body, name=f"allgather_forward_{l}", in_specs=[ANY] * n, out_specs=[ANY] * n,
        out_shape=[SDS(b.shape, b.dtype) for b in bufs], input_output_aliases={i: i for i in range(n)},
        scratch_shapes=[sem, sem])(*bufs)


def _rs_to_owner(l, parts):
    n = len(parts)
    lands = [lax.empty((3,) + p.shape[1:], p.dtype) for p in parts]

    def body(*refs):
        ins, zones = refs[:n], refs[n:2 * n]
        send_sems, recv_sems = refs[2 * n], refs[2 * n + 1]
        token = refs[-1]
        x, y, c = _place()
        for a in range(n):
            for k, (cx, cy) in enumerate(_other_chips(x, y)):
                _remote(ins[a].at[2 * cx + cy], zones[a].at[k], send_sems.at[3 * a + k], recv_sems.at[3 * a + k],
                        (cx, cy, c)).start()
        token[...] = jnp.zeros(token.shape, F32)

    sem = pltpu.SemaphoreType.DMA((3 * n,))
    outs = pl.pallas_call(
        body, name=f"rs_to_owner_start_{l}", in_specs=[HBM] * (2 * n),
        out_specs=[SEM, SEM] + [HBM] * (2 * n) + [pl.BlockSpec(memory_space=pltpu.VMEM)],
        out_shape=[sem, sem] + [pltpu.HBM(p.shape, p.dtype) for p in parts]
        + [pltpu.HBM(z.shape, z.dtype) for z in lands] + [SDS((8, 128), F32)],
        input_output_aliases={i: 2 + i for i in range(2 * n)},
        compiler_params=pltpu.CompilerParams(has_side_effects=SIDE_EFFECT),
    )(*[_hbm(p) for p in parts], *[_hbm(z) for z in lands])
    return outs[0], outs[1], outs[2:2 + n], outs[2 + n:2 + 2 * n], outs[-1]


def _rs_to_owner_wait(l, send_sems, recv_sems, parts, lands, after):
    n = len(parts)

    def body(*refs):
        ins, zones = refs[:n], refs[n:2 * n]
        ssem, rsem = refs[2 * n], refs[2 * n + 1]
        x, y, c = _place()
        for a in range(n):
            for k, (cx, cy) in enumerate(_other_chips(x, y)):
                cp = _remote(ins[a].at[2 * cx + cy], zones[a].at[k], ssem.at[3 * a + k], rsem.at[3 * a + k],
                             (cx, cy, c))
                cp.wait_send()
                cp.wait_recv()

    outs = pl.pallas_call(
        body, name=f"rs_to_owner_wait_{l}", in_specs=[HBM] * (2 * n) + [SEM, SEM] + [ANY] * len(after),
        out_specs=[HBM] * (2 * n),
        out_shape=[pltpu.HBM(p.shape, p.dtype) for p in parts] + [pltpu.HBM(z.shape, z.dtype) for z in lands],
        input_output_aliases={i: i for i in range(2 * n)},
        compiler_params=pltpu.CompilerParams(has_side_effects=SIDE_EFFECT),
    )(*parts, *lands, send_sems, recv_sems, *after)
    return outs[:n], outs[n:]


def _rs_sibling_exchange(l, both):
    n = len(both)

    def body(*refs):
        ins = refs[:n]
        send_sems, recv_sems = refs[2 * n:]
        x, y, c = _place()
        copies = []
        for a in range(n):
            cp = _remote(ins[a].at[c], ins[a].at[c], send_sems.at[a], recv_sems.at[a], (x, y, 1 - c))
            cp.start()
            copies.append(cp)
        for a, cp in enumerate(copies):
            cp.wait_send()
            _remote(ins[a].at[1 - c], ins[a].at[1 - c], send_sems.at[a], recv_sems.at[a], (x, y, 1 - c)).wait_recv()

    sem = pltpu.SemaphoreType.DMA((n,))
    return pl.pallas_call(
        body, name=f"rs_sibling_exchange_{l}", in_specs=[ANY] * n, out_specs=[ANY] * n,
        out_shape=[SDS(b.shape, b.dtype) for b in both], input_output_aliases={i: i for i in range(n)},
        scratch_shapes=[sem, sem])(*both)


def _add_owner(name, grad, recv, place):
    r, cols = grad.shape
    tr = _row_tile(r, cols, budget=1024 * 1024)
    nt = r // tr

    def body(place_ref, g_ref, r_ref, o_ref):
        acc = ((g_ref[...] + r_ref[0].astype(F32)) + r_ref[1].astype(F32)) + r_ref[2].astype(F32)
        o_ref[...] = acc.astype(o_ref.dtype)

    return pl.pallas_call(
        body, name=name,
        grid_spec=pltpu.PrefetchScalarGridSpec(
            num_scalar_prefetch=1, grid=(nt,),
            in_specs=[BS((tr, cols), lambda t, pr: (t, 0)), BS((3, tr, cols), lambda t, pr: (0, t, 0))],
            out_specs=BS((None, tr, cols), lambda t, pr: (pr[1], t, 0))),
        out_shape=SDS((2, r, cols), WIRE_DTYPE))(place, grad, recv)


def _add_owner_group(tag, grads, recvs, place, steps):
    n = len(grads)

    def body(place_ref, *refs):
        gs, rs, outs = refs[:n], refs[n:2 * n], refs[2 * n:]
        for g_ref, r_ref, o_ref in zip(gs, rs, outs):
            acc = ((g_ref[...] + r_ref[0].astype(F32)) + r_ref[1].astype(F32)) + r_ref[2].astype(F32)
            o_ref[...] = acc.astype(o_ref.dtype)

    rows = [g.shape[0] // steps for g in grads]
    return pl.pallas_call(
        body, name=f"rs_add_owner_{tag}",
        grid_spec=pltpu.PrefetchScalarGridSpec(
            num_scalar_prefetch=1, grid=(steps,),
            in_specs=[BS((r, g.shape[1]), lambda t, pr: (t, 0)) for g, r in zip(grads, rows)]
            + [BS((3, r, g.shape[1]), lambda t, pr: (0, t, 0)) for g, r in zip(grads, rows)],
            out_specs=[BS((None, r, g.shape[1]), lambda t, pr: (pr[1], t, 0)) for g, r in zip(grads, rows)]),
        out_shape=[SDS((2,) + g.shape, WIRE_DTYPE) for g in grads], compiler_params=_params(),
    )(place, *grads, *recvs)


def _reduce_start(tag, grads):
    names = list(grads)
    send_sems, recv_sems, wires, lands, token = _rs_to_owner(tag, [grads[n][1] for n in names])
    return dict(tag=tag, names=names, send_sems=send_sems, recv_sems=recv_sems, wires=wires, lands=lands,
                grads=[grads[n][0] for n in names]), token


def _reduce_finish(tag, groups, place, after):
    all_names, all_mine = [], []
    for pending in groups:
        sub, names = pending["tag"], pending["names"]
        _, lands = _rs_to_owner_wait(sub, pending["send_sems"], pending["recv_sems"], pending["wires"],
                                     pending["lands"], after)
        if len(names) > 1:
            small = max(g.size for g in pending["grads"]) <= SMALL_GRAD_ELEMS
            mine = _add_owner_group(sub, pending["grads"], lands, place, 1 if small else ADAMW_GROUP_STEPS)
        else:
            mine = [_add_owner(f"rs_add_owner_{n}_{sub}", g, r, place)
                    for n, g, r in zip(names, pending["grads"], lands)]
        all_names += names
        all_mine += mine
    return dict(zip(all_names, _rs_sibling_exchange(tag, all_mine)))


def _small_peers(x, y, c):
    return [(x, y, 1 - c)] + [(cx, cy, c) for cx, cy in _other_chips(x, y)]


def _allgather_rows_start(tag, bufs):
    n = len(bufs)
    lands = [lax.empty((8,) + b.shape, b.dtype) for b in bufs]

    def body(*refs):
        ins, zones = refs[:n], refs[n:2 * n]
        send_sems, recv_sems = refs[2 * n], refs[2 * n + 1]
        token = refs[-1]
        x, y, c = _place()
        for a in range(n):
            for i, peer in enumerate(_small_peers(x, y, c)):
                _remote(ins[a], zones[a].at[4 * x + 2 * y + c], send_sems.at[4 * a + i], recv_sems.at[4 * a + i],
                        peer).start()
        token[...] = jnp.zeros(token.shape, F32)

    sem = pltpu.SemaphoreType.DMA((4 * n,))
    outs = pl.pallas_call(
        body, name=f"allgather_small_start_{tag}", in_specs=[HBM] * (2 * n),
        out_specs=[SEM, SEM] + [HBM] * (2 * n) + [pl.BlockSpec(memory_space=pltpu.VMEM)],
        out_shape=[sem, sem] + [pltpu.HBM(b.shape, b.dtype) for b in bufs]
        + [pltpu.HBM(z.shape, z.dtype) for z in lands] + [SDS((8, 128), F32)],
        input_output_aliases={i: 2 + i for i in range(2 * n)},
        compiler_params=pltpu.CompilerParams(has_side_effects=SIDE_EFFECT),
    )(*[_hbm(b) for b in bufs], *[_hbm(z) for z in lands])
    return outs[0], outs[1], outs[2:2 + n], outs[2 + n:2 + 2 * n], outs[-1]


def _allgather_rows_wait(tag, send_sems, recv_sems, bufs, lands, after):
    n = len(bufs)

    def body(*refs):
        ins, zones = refs[:n], refs[n:2 * n]
        ssem, rsem = refs[2 * n], refs[2 * n + 1]
        x, y, c = _place()
        for a in range(n):
            for i, (px, py, pc) in enumerate(_small_peers(x, y, c)):
                cp = _remote(ins[a], zones[a].at[4 * px + 2 * py + pc], ssem.at[4 * a + i], rsem.at[4 * a + i],
                             (px, py, pc))
                cp.wait_send()
                cp.wait_recv()

    outs = pl.pallas_call(
        body, name=f"allgather_small_wait_{tag}", in_specs=[HBM] * (2 * n) + [SEM, SEM, ANY],
        out_specs=[HBM] * (2 * n),
        out_shape=[pltpu.HBM(b.shape, b.dtype) for b in bufs] + [pltpu.HBM(z.shape, z.dtype) for z in lands],
        input_output_aliases={i: i for i in range(2 * n)},
        compiler_params=pltpu.CompilerParams(has_side_effects=SIDE_EFFECT),
    )(*bufs, *lands, send_sems, recv_sems, after)
    return outs[:n], outs[n:]


def _allgather_rows_forward(tag, lands):
    n = len(lands)

    def body(*refs):
        ins = refs[:n]
        send_sems, recv_sems = refs[2 * n:]
        x, y, c = _place()
        sibling = (x, y, 1 - c)
        copies = []
        for a in range(n):
            for k, (cx, cy) in enumerate(_other_chips(x, y)):
                blk = ins[a].at[4 * cx + 2 * cy + c]
                cp = _remote(blk, blk, send_sems.at[a, k], recv_sems.at[a, k], sibling)
                cp.start()
                copies.append(cp)
        for a in range(n):
            for k, (cx, cy) in enumerate(_other_chips(x, y)):
                blk = ins[a].at[4 * cx + 2 * cy + 1 - c]
                _remote(blk, blk, send_sems.at[a, k], recv_sems.at[a, k], sibling).wait_recv()
        for cp in copies:
            cp.wait_send()

    sem = pltpu.SemaphoreType.DMA((n, 3))
    return pl.pallas_call(body, name=f"allgather_small_forward_{tag}", in_specs=[ANY] * n, out_specs=[ANY] * n,
                          out_shape=[SDS(z.shape, z.dtype) for z in lands],
                          input_output_aliases={i: i for i in range(n)}, scratch_shapes=[sem, sem])(*lands)


def _sum_devices(tag, gathered, mine, place):
    _, r, cols = gathered.shape
    tr = _row_tile(r, cols, budget=256 * 1024)

    def body(place_ref, g_ref, x_ref, o_ref):
        me = 2 * place_ref[0] + place_ref[1]
        acc = jnp.where(me == 0, x_ref[...], g_ref[0])
        for k in range(1, 8):
            acc = acc + jnp.where(me == k, x_ref[...], g_ref[k])
        o_ref[...] = acc

    return pl.pallas_call(
        body, name=f"sum_small_grads_{tag}",
        grid_spec=pltpu.PrefetchScalarGridSpec(
            num_scalar_prefetch=1, grid=(r // tr,),
            in_specs=[BS((8, tr, cols), lambda t, pr: (0, t, 0)), BS((tr, cols), lambda t, pr: (t, 0))],
            out_specs=BS((tr, cols), lambda t, pr: (t, 0))),
        out_shape=SDS((r, cols), F32))(place, gathered, mine)


def _adamw_values(w, g, m, v):
    m = ADAM_B1 * m + (1.0 - ADAM_B1) * g
    v = ADAM_B2 * v + (1.0 - ADAM_B2) * (g * g)
    m_hat = m / (1.0 - ADAM_B1 ** ADAM_STEP)
    v_hat = v / (1.0 - ADAM_B2 ** ADAM_STEP)
    delta = -ADAM_LR * (m_hat / (jnp.sqrt(v_hat) + ADAM_EPS) + ADAM_WD * w)
    return delta, m, v


def _adamw_big(name, l, w, m, v, g, earlier=None, after=()):
    nl, r, cols = w.shape
    tr = _row_tile(r, cols, budget=1024 * 1024)
    nt = r // tr
    n_prev = 0 if earlier is None else 4

    def body(*refs):
        w_ref, m_ref, v_ref, g_ref = refs[:4]
        go_ref, d_ref, mo_ref, vo_ref = refs[4 + n_prev + len(after):]
        gv = g_ref[0].astype(F32) + g_ref[1].astype(F32)
        delta, m_new, v_new = _adamw_values(w_ref[...], gv, m_ref[...], v_ref[...])
        go_ref[...] = gv
        d_ref[...] = delta
        mo_ref[...] = m_new
        vo_ref[...] = v_new

    layer = BS((None, tr, cols), lambda t: (l, t, 0))
    return pl.pallas_call(
        body, name=f"adamw_{name}_l{l}", grid=(nt,),
        in_specs=[layer, layer, layer, BS((2, tr, cols), lambda t: (0, t, 0))] + [ANY] * (n_prev + len(after)),
        out_specs=[layer] * 4, out_shape=[SDS(w.shape, F32)] * 4,
        input_output_aliases={4 + i: i for i in range(n_prev)}, compiler_params=_params(),
    )(w, m, v, g, *(earlier or ()), *after)


def _adamw_small_group(tag, l, ws, ms, vs, gs, earlier, after=()):
    n = len(ws)
    steps = ADAMW_GROUP_STEPS
    prev = [a for e in earlier if e is not None for a in e]
    n_prev = len(prev)
    assert n_prev in (0, 4 * n)

    def body(*refs):
        w_refs, m_refs, v_refs, g_refs = refs[:n], refs[n:2 * n], refs[2 * n:3 * n], refs[3 * n:4 * n]
        outs = refs[4 * n + n_prev + len(after):]
        for i in range(n):
            gv = g_refs[i][0].astype(F32) + g_refs[i][1].astype(F32)
            delta, m_new, v_new = _adamw_values(w_refs[i][...], gv, m_refs[i][...], v_refs[i][...])
            for ref, val in zip(outs[4 * i:4 * i + 4], (gv, delta, m_new, v_new)):
                ref[...] = val

    def layer(w):
        return BS((None, w.shape[1] // steps, w.shape[2]), lambda t: (l, t, 0))

    return pl.pallas_call(
        body, name=f"adamw_{tag}_l{l}", grid=(steps,),
        in_specs=[layer(w) for w in ws] * 3
        + [BS((2, w.shape[1] // steps, w.shape[2]), lambda t: (0, t, 0)) for w in ws] + [ANY] * (n_prev + len(after)),
        out_specs=[layer(w) for w in ws for _ in range(4)],
        out_shape=[SDS(w.shape, F32) for w in ws for _ in range(4)],
        input_output_aliases={4 * n + i: i for i in range(n_prev)}, compiler_params=_params(),
    )(*ws, *ms, *vs, *gs, *prev, *after)


def _adamw_mid(ws, ms, vs, gathered, mine, place):
    n = len(ws)
    shape = ws[0].shape[1:]
    zeros = (0,) * len(shape)

    def body(place_ref, *refs):
        w_refs, m_refs, v_refs = refs[:n], refs[n:2 * n], refs[2 * n:3 * n]
        gath, own = refs[3 * n:(3 + N_LAYERS) * n], refs[(3 + N_LAYERS) * n:(3 + 2 * N_LAYERS) * n]
        outs = refs[(3 + 2 * N_LAYERS) * n:]
        me = 2 * place_ref[0] + place_ref[1]
        for i in range(n):
            gv = None
            for l in range(N_LAYERS):
                g_ref, x_ref = gath[l * n + i], own[l * n + i]
                acc = jnp.where(me == 0, x_ref[...], g_ref[0])
                for k in range(1, 8):
                    acc = acc + jnp.where(me == k, x_ref[...], g_ref[k])
                gv = acc if gv is None else jnp.where(pl.program_id(0) == l, acc, gv)
            delta, m_new, v_new = _adamw_values(w_refs[i][...], gv, m_refs[i][...], v_refs[i][...])
            for ref, val in zip(outs[4 * i:4 * i + 4], (gv, delta, m_new, v_new)):
                ref[...] = val

    layer = BS((None,) + shape, lambda l, pr: (l,) + zeros)
    kept = pl.Buffered(1)
    outs = pl.pallas_call(
        body, name="adamw_replicated_matrices",
        grid_spec=pltpu.PrefetchScalarGridSpec(
            num_scalar_prefetch=1, grid=(N_LAYERS,),
            in_specs=[layer] * (3 * n)
            + [BS((8,) + shape, lambda l, pr: (0,) + zeros, pipeline_mode=kept)] * (N_LAYERS * n)
            + [BS(shape, lambda l, pr: zeros, pipeline_mode=kept)] * (N_LAYERS * n),
            out_specs=[layer] * (4 * n)),
        out_shape=[SDS(ws[0].shape, F32)] * (4 * n), compiler_params=_params(),
    )(place, *ws, *ms, *vs, *[g for l in range(N_LAYERS) for g in gathered[l]],
      *[x for l in range(N_LAYERS) for x in mine[l]])
    return [tuple(outs[4 * i:4 * i + 4]) for i in range(n)]


def _adamw_rows(w, m, v, g):
    r, cols = w.shape
    tr = _row_tile(r, cols, budget=512 * 1024)

    def body(w_ref, m_ref, v_ref, g_ref, d_ref, mo_ref, vo_ref):
        delta, m_new, v_new = _adamw_values(w_ref[...], g_ref[...], m_ref[...], v_ref[...])
        d_ref[...] = delta
        mo_ref[...] = m_new
        vo_ref[...] = v_new

    spec = BS((tr, cols), lambda t: (t, 0))
    return pl.pallas_call(body, name="adamw_small", grid=(r // tr,), in_specs=[spec] * 4, out_specs=[spec] * 3,
                          out_shape=[SDS(w.shape, F32)] * 3)(w, m, v, g)


SMALL_GRAD_ELEMS = 256 * 1024
ADAMW_GROUP_STEPS = 4
PACK_ALIGN = 8 * 128
PACK_ROWS = 128


def _pack_rows(arrays):
    parts, rows = [], 0
    for a in arrays:
        flat = a.reshape(-1)
        pad = (-flat.shape[0]) % PACK_ALIGN
        if pad:
            flat = jnp.pad(flat, (0, pad))
        parts.append(flat.reshape(-1, 128))
        rows += parts[-1].shape[0]
    if rows % PACK_ROWS:
        parts.append(jnp.zeros((PACK_ROWS - rows % PACK_ROWS, 128), parts[0].dtype))
    return jnp.concatenate(parts, axis=0)


def _unpack_rows(buf, shapes):
    out, row = [], 0
    for shape in shapes:
        size = math.prod(shape)
        rows = -(-size // PACK_ALIGN) * (PACK_ALIGN // 128)
        out.append(buf[row:row + rows].reshape(-1)[:size].reshape(shape))
        row += rows
    return out


def kernel(x, norm1, w_in, b_gate, ssm_a_re, ssm_a_im, ssm_log_dt, ssm_b_re, ssm_b_im, ssm_c_re, ssm_c_im, ssm_d, ssm_w_glu, ssm_b_glu, ssm_w_proj, conv_w_dw, conv_b_dw, conv_ln_g, conv_ln_b, conv_w_proj, pool_w_group, pool_scale, pool_w_proj, w_out, norm2, ffn_w_gate, ffn_w_up, ffn_w_down, final_norm, loss_target, m_norm1, m_w_in, m_b_gate, m_ssm_a_re, m_ssm_a_im, m_ssm_log_dt, m_ssm_b_re, m_ssm_b_im, m_ssm_c_re, m_ssm_c_im, m_ssm_d, m_ssm_w_glu, m_ssm_b_glu, m_ssm_w_proj, m_conv_w_dw, m_conv_b_dw, m_conv_ln_g, m_conv_ln_b, m_conv_w_proj, m_pool_w_group, m_pool_scale, m_pool_w_proj, m_w_out, m_norm2, m_ffn_w_gate, m_ffn_w_up, m_ffn_w_down, m_final_norm, v_norm1, v_w_in, v_b_gate, v_ssm_a_re, v_ssm_a_im, v_ssm_log_dt, v_ssm_b_re, v_ssm_b_im, v_ssm_c_re, v_ssm_c_im, v_ssm_d, v_ssm_w_glu, v_ssm_b_glu, v_ssm_w_proj, v_conv_w_dw, v_conv_b_dw, v_conv_ln_g, v_conv_ln_b, v_conv_w_proj, v_pool_w_group, v_pool_scale, v_pool_w_proj, v_w_out, v_norm2, v_ffn_w_gate, v_ffn_w_up, v_ffn_w_down, v_final_norm):
    given = dict(locals())
    cx, cy, cc = _place()
    place = jnp.stack([2 * cx + cy, cc]).astype(jnp.int32)

    def kernel_view(n, a):
        if n in TRANSPOSED:
            return a.transpose(0, 2, 1)
        return a.transpose(0, 1, 3, 2) if n in ("ssm_b_re", "ssm_b_im") else a

    prm = {n: given[n] for n in WEIGHTS}
    mom = {n: given["m_" + n] for n in WEIGHTS}
    var = {n: given["v_" + n] for n in WEIGHTS}
    for n in MID:
        prm[n], mom[n], var[n] = kernel_view(n, prm[n]), kernel_view(n, mom[n]), kernel_view(n, var[n])

    dw_shard = prm["conv_w_dw"].reshape(N_LAYERS, CONV_KERNEL, -1)
    casts = {"w_in": _cast_into("w_in", prm["w_in"], place, MXU_DTYPE)}
    first, first_started = _allgather_start("first", [[casts["w_in"][0]]])
    in_flight = {(0, "in"): first[0]}
    mixer = GATHER_GROUPS["mixer"]
    casts.update(zip(mixer, _cast_small_into(
        "mixer", [dw_shard if n == "conv_w_dw" else prm[n] for n in mixer],
        [F32 if n == "conv_w_dw" else MXU_DTYPE for n in mixer], place, after=(first_started,))))
    casts.update({n: _cast_into(n, kernel_view(n, prm[n]), place, MXU_DTYPE, after=(first_started,))
                  for n in GATHER_GROUPS["ffn"]})
    order = [(l, g) for l in range(N_LAYERS) for g in GATHER_GROUPS if (l, g) != (0, "in")]
    rest, rest_started = _allgather_start("rest", [[casts[n][l] for n in GATHER_GROUPS[g]] for l, g in order])
    in_flight.update(zip(order, rest))

    arrived = {}

    def weights_of(l, group, after):
        if (l, group) in arrived:
            return arrived.pop((l, group))
        tag = f"l{l}_{group}"
        if (l, group) == (0, "in"):
            after = after + (rest_started,)
        groups = (group, "mixer") if (l > 0 and group == "in") else (group,)
        waited = [_allgather_wait(f"l{l}_{g}", *in_flight[l, g][:2], in_flight[l, g][2], after) for g in groups]
        bufs = _allgather_forward(tag, [b for w in waited for b in w])
        for g in groups:
            fw = dict(zip(GATHER_GROUPS[g], bufs[:len(GATHER_GROUPS[g])]))
            bufs = bufs[len(GATHER_GROUPS[g]):]
            if "conv_w_dw" in fw:
                fw["conv_w_dw"] = fw["conv_w_dw"].transpose(1, 0, 2).reshape(CONV_KERNEL, -1)
            arrived[l, g] = fw
        return arrived.pop((l, group))

    pending, small_pending, small_shapes = {}, {}, {}
    tokens = {}

    def on_grads(l, group, grads):
        if group == "small":
            packed = {n: g for n, g in grads.items() if n not in MID}
            small_shapes[l] = {n: g.shape for n, g in packed.items()}
            begun = _allgather_rows_start(f"l{l}", [_pack_rows(list(packed.values()))] + [grads[n] for n in MID])
            small_pending[l], token = begun[:4], begun[4]
        else:
            pending[l, group], token = _reduce_start(f"{l}_{group}", grads)
        tokens[l, group] = token
        return token[0, 0]

    loss, dx, _, _, _ = _local_step(x[0], loss_target[0], weights_of, prm, place, on_grads)
    loss = lax.psum(loss, ("x", "y", "c"))

    reduced = [{} for _ in range(N_LAYERS)]
    out = {}

    def finish(l, groups, after):
        reduced[l].update(_reduce_finish(f"l{l}_{groups[0]}", [pending[l, g] for g in groups], place, after))

    def adamw(l, names, done):
        for group in ("in", "ffn", "mixer"):
            members = [n for n in names if n in GATHER_GROUPS[group]]
            if len(members) == 1:
                n = members[0]
                out[n] = _adamw_big(n, l, kernel_view(n, prm[n]), kernel_view(n, mom[n]), kernel_view(n, var[n]),
                                    reduced[l][n], out.get(n), after=done)
                done = (out[n][0],)
            elif members:
                res = _adamw_small_group(group, l, [kernel_view(n, prm[n]) for n in members],
                                         [kernel_view(n, mom[n]) for n in members],
                                         [kernel_view(n, var[n]) for n in members],
                                         [reduced[l][n] for n in members], [out.get(n) for n in members], after=done)
                for i, n in enumerate(members):
                    out[n] = tuple(res[4 * i:4 * i + 4])
                done = (res[0],)
        return done

    top = N_LAYERS - 1
    done = (tokens[0, "in"], tokens[0, "small"])
    finish(top, ("ffn", "in"), done)
    done = adamw(top, BIG, done)
    for group in ("ffn", "in"):
        finish(0, (group,), done)
        done = adamw(0, pending[0, group]["names"], done)
    for n in BIG:
        out[n] = tuple(kernel_view(n, a) for a in out[n])

    gsmall = {}
    mid_mine, mid_gathered = [], []
    for l in range(N_LAYERS):
        mine, lands = _allgather_rows_wait(f"l{l}", *small_pending[l], done[0])
        lands = _allgather_rows_forward(f"l{l}", lands)
        mid_mine.append(mine[1:])
        mid_gathered.append(lands[1:])
        gsum = _sum_devices(f"l{l}", lands[0], mine[0], place)
        for n, g in zip(small_shapes[l], _unpack_rows(gsum, list(small_shapes[l].values()))):
            gsmall.setdefault(n, [None] * N_LAYERS)[l] = g
    mid_out = _adamw_mid([prm[n] for n in MID], [mom[n] for n in MID], [var[n] for n in MID], mid_gathered, mid_mine,
                         place)
    for n, res in zip(MID, mid_out):
        out[n] = tuple(kernel_view(n, a) for a in res)
    gsmall = {n: (g[top] if n == "final_norm" else jnp.stack(g)) for n, g in gsmall.items()}
    lanes = dw_shard.shape[-1]
    gsmall["conv_w_dw"] = lax.dynamic_slice_in_dim(gsmall["conv_w_dw"], (2 * cx + cy) * lanes, lanes, axis=2)
    small_names = [n for n in SMALL if n not in MID] + ["conv_w_dw"]
    w_rows = _pack_rows([prm[n] for n in small_names])
    m_rows = _pack_rows([mom[n] for n in small_names])
    v_rows = _pack_rows([var[n] for n in small_names])
    g_rows = _pack_rows([gsmall[n] for n in small_names])
    shapes = [prm[n].shape for n in small_names]
    d_s, m_s, v_s = (_unpack_rows(r, shapes) for r in _adamw_rows(w_rows, m_rows, v_rows, g_rows))
    for i, n in enumerate(small_names):
        out[n] = (gsmall[n].reshape(prm[n].shape), d_s[i], m_s[i], v_s[i])
    grads = [out[n][0] for n in WEIGHTS]
    deltas = [out[n][1] for n in WEIGHTS]
    new_m = [out[n][2] for n in WEIGHTS]
    new_v = [out[n][3] for n in WEIGHTS]
    return (loss, dx[None], *grads, *deltas, *new_m, *new_v)
```

```python
import math

import jax
import jax.numpy as jnp
from jax import lax
from jax.experimental import pallas as pl
from jax.experimental.pallas import tpu as pltpu

F32 = jnp.float32
MXU_DTYPE = jnp.bfloat16
WIRE_DTYPE = jnp.bfloat16
SDS = jax.ShapeDtypeStruct
BS = pl.BlockSpec
ANY = pl.BlockSpec(memory_space=pl.ANY)
HBM = pl.BlockSpec(memory_space=pltpu.HBM)
SEM = pl.BlockSpec(memory_space=pltpu.SEMAPHORE)
SIDE_EFFECT = pltpu.SideEffectType.DATAFLOW_SIDE_EFFECTING
MESH = pl.DeviceIdType.MESH

EPS = 1e-6
N_CHIPS = 4
N_LAYERS = 2
SSM_GROUPS, SSM_STATE, SSM_GROUP = 32, 64, 16
CONV_KERNEL = 31
CONV_PAD = 32
POOL_WINDOWS = (2, 4, 8, 16)
GELU_C = math.sqrt(2.0 / math.pi)
ADAM_LR, ADAM_B1, ADAM_B2, ADAM_EPS, ADAM_WD, ADAM_STEP = 0.001, 0.9, 0.999, 1e-08, 0.01, 10
VMEM_LIMIT = 56 * 1024 * 1024

BIG = ("w_in", "ssm_w_glu", "ssm_w_proj", "conv_w_proj", "pool_w_proj", "w_out", "ffn_w_gate", "ffn_w_up", "ffn_w_down")
TRANSPOSED = ("ffn_w_gate", "ffn_w_up")
MID = ("ssm_b_re", "ssm_b_im", "ssm_c_re", "ssm_c_im")
GATHER_GROUPS = {
    "in": ("w_in",),
    "mixer": ("ssm_w_glu", "ssm_w_proj", "conv_w_proj", "pool_w_proj", "w_out", "conv_w_dw"),
    "ffn": ("ffn_w_gate", "ffn_w_up", "ffn_w_down"),
}
SMALL = ("norm1", "b_gate", "ssm_a_re", "ssm_a_im", "ssm_log_dt", "ssm_b_re", "ssm_b_im", "ssm_c_re", "ssm_c_im",
         "ssm_d", "ssm_b_glu", "conv_b_dw", "conv_ln_g", "conv_ln_b", "pool_w_group", "pool_scale", "norm2",
         "final_norm")
WEIGHTS = ("norm1", "w_in", "b_gate", "ssm_a_re", "ssm_a_im", "ssm_log_dt", "ssm_b_re", "ssm_b_im", "ssm_c_re",
           "ssm_c_im", "ssm_d", "ssm_w_glu", "ssm_b_glu", "ssm_w_proj", "conv_w_dw", "conv_b_dw", "conv_ln_g",
           "conv_ln_b", "conv_w_proj", "pool_w_group", "pool_scale", "pool_w_proj", "w_out", "norm2", "ffn_w_gate",
           "ffn_w_up", "ffn_w_down", "final_norm")


def _params():
    return pltpu.CompilerParams(vmem_limit_bytes=VMEM_LIMIT)


def _mm(a, b):
    return jnp.dot(a.astype(MXU_DTYPE), b.astype(MXU_DTYPE), preferred_element_type=F32)


def _mm_nt(a, b):
    return lax.dot_general(a.astype(MXU_DTYPE), b.astype(MXU_DTYPE), (((1,), (1,)), ((), ())),
                           preferred_element_type=F32)


def _mm_tn(a, b):
    return lax.dot_general(a.astype(MXU_DTYPE), b.astype(MXU_DTYPE), (((0,), (0,)), ((), ())),
                           preferred_element_type=F32)


def _sigmoid(x):
    return jax.nn.sigmoid(x)


def _gelu(x):
    t = jnp.tanh(GELU_C * (x + 0.044715 * (x * x * x)))
    return x * (0.5 * (1.0 + t)), t


def _gelu_grad(x, t):
    return 0.5 * (1.0 + t) + 0.5 * x * (1.0 - t * t) * (GELU_C * (1.0 + 3.0 * 0.044715 * x * x))


def _colsum(v):
    return jnp.sum(v, axis=0, keepdims=True)


def _row_tile(rows, cols, itemsize=4, budget=1536 * 1024):
    best = None
    for t in range(8, rows + 1, 8):
        if rows % t == 0 and t * cols * itemsize <= budget:
            best = t
    return best if best is not None else rows


def _in_proj(l, x, norm1, w_in):
    s, d = x.shape
    nc = w_in.shape[-1]
    tm = min(1024, s)
    nt = s // tm

    def body(x_ref, g_ref, w_ref, z_ref, h_ref, h_all):
        i = pl.program_id(1)
        rows = pl.ds(pl.multiple_of(i * tm, tm), tm)

        @pl.when(pl.program_id(0) == 0)
        def _():
            xv = x_ref[...]
            r = lax.rsqrt(jnp.mean(xv * xv, axis=-1, keepdims=True) + EPS)
            hv = (xv * r * g_ref[...]).astype(h_ref.dtype)
            h_ref[...] = hv.T
            h_all[rows, :] = hv

        z_ref[...] = _mm(h_all[rows, :], w_ref[...])

    tile_of = lambda j, i: i * (1 - jnp.minimum(j, 1)) + (nt - 1) * jnp.minimum(j, 1)
    return pl.pallas_call(
        body, name=f"in_proj_l{l}", grid=(N_CHIPS, nt),
        in_specs=[BS((tm, d), lambda j, i: (tile_of(j, i), 0)), BS((None, 1, d), lambda j, i: (l, 0, 0)),
                  BS((None, d, nc), lambda j, i: (j, 0, 0))],
        out_specs=[BS((tm, nc), lambda j, i: (i, j)), BS((d, tm), lambda j, i: (0, tile_of(j, i)))],
        out_shape=[SDS((s, N_CHIPS * nc), F32), SDS((d, s), MXU_DTYPE)],
        scratch_shapes=[pltpu.VMEM((s, d), MXU_DTYPE)], compiler_params=_params())(x, norm1, w_in)


def _mm_cols(a, w_ref):
    return jnp.concatenate([_mm(a, w_ref[j]) for j in range(N_CHIPS)], axis=1)


def _mm_nt_cols(dv, w_ref):
    nc = w_ref.shape[-1]
    acc = _mm_nt(dv[:, 0:nc], w_ref[0])
    for j in range(1, N_CHIPS):
        acc = acc + _mm_nt(dv[:, j * nc:(j + 1) * nc], w_ref[j])
    return acc


def _merge_values(y, hc, p, zg, wglu, bglu, wpa, wpb, wpc, lng, lnb, wgrp, scale, bg):
    v = {}
    ge, th = _gelu(y)
    t = _mm(ge, wglu) + bglu
    sg = _sigmoid(t)
    sa = ge * sg
    ya = _mm_cols(sa, wpa)
    mu = jnp.mean(hc, axis=-1, keepdims=True)
    xc = hc - mu
    r = lax.rsqrt(jnp.mean(xc * xc, axis=-1, keepdims=True) + EPS)
    xh = xc * r
    ln = xh * lng + lnb
    sl = _sigmoid(ln)
    ac = ln * sl
    yb = _mm_cols(ac, wpb)
    gw = p.shape[1] // len(POOL_WINDOWS)
    q = jnp.concatenate([_mm(p[:, k * gw:(k + 1) * gw], wgrp[k]) for k in range(len(POOL_WINDOWS))], axis=1)
    pp = q * scale
    yc = _mm_cols(pp, wpc)
    d = ya.shape[1]
    gates = [_sigmoid(zg[k] + bg[:, k * d:(k + 1) * d]) for k in range(3)]
    merged = gates[0] * ya + gates[1] * yb + gates[2] * yc
    v.update(ge=ge, th=th, sg=sg, sa=sa, ya=ya, r=r, xh=xh, ln=ln, sl=sl, ac=ac, yb=yb, q=q, pp=pp, yc=yc,
             gates=gates, merged=merged)
    return v


def _merge_specs(l, tm, d, cw):
    row = lambda n: BS((None, 1, n), lambda i: (l, 0, 0))
    resident = lambda shp: BS(shp, lambda i: (0, 0, 0), pipeline_mode=pl.Buffered(1))
    return [
        BS((tm, cw), lambda i: (i, 0)),
        BS((tm, cw), lambda i: (i, 0)),
        BS((tm, cw), lambda i: (i, 0)),
        BS((tm, d), lambda i: (i, 2)), BS((tm, d), lambda i: (i, 3)), BS((tm, d), lambda i: (i, 4)),
        resident((N_CHIPS, cw // N_CHIPS, cw)),
        row(cw),
        resident((N_CHIPS, cw, d // N_CHIPS)),
        resident((N_CHIPS, cw, d // N_CHIPS)),
        resident((N_CHIPS, cw, d // N_CHIPS)),
        row(cw), row(cw),
        BS((None, 4, cw // 4, cw // 4), lambda i: (l, 0, 0, 0)),
        row(cw),
        row(3 * d),
        resident((N_CHIPS, d // N_CHIPS, d)),
    ]


def _merge_fwd(l, x, y, hc, p, z, fw, sp):
    s, d = x.shape
    cw = y.shape[1]
    tm = min(512, s)

    def body(x_ref, y_ref, hc_ref, p_ref, z0, z1, z2, wglu, bglu, wpa, wpb, wpc, lng, lnb, wgrp, scale, bg, wout,
             x1_ref):
        v = _merge_values(y_ref[...], hc_ref[...], p_ref[...], (z0[...], z1[...], z2[...]),
                          wglu[...].reshape(cw, cw), bglu[...], wpa, wpb, wpc, lng[...], lnb[...], wgrp, scale[...],
                          bg[...])
        x1_ref[...] = x_ref[...] + _mm(v["merged"], wout[...].reshape(d, d))

    return pl.pallas_call(
        body, name=f"merge_fwd_l{l}", grid=(s // tm,),
        in_specs=[BS((tm, d), lambda i: (i, 0))] + _merge_specs(l, tm, d, cw),
        out_specs=BS((tm, d), lambda i: (i, 0)), out_shape=SDS((s, d), F32), compiler_params=_params(),
    )(x, y, hc, p, z, z, z, fw["ssm_w_glu"], sp["ssm_b_glu"], fw["ssm_w_proj"], fw["conv_w_proj"], fw["pool_w_proj"],
      sp["conv_ln_g"], sp["conv_ln_b"], sp["pool_w_group"], sp["pool_scale"], sp["b_gate"], fw["w_out"])


def _merge_bwd(l, dx1, y, hc, p, z, fw, sp):
    s, d = dx1.shape
    cw = y.shape[1]
    tm = min(256, s)
    m = MXU_DTYPE

    def body(dx1_ref, y_ref, hc_ref, p_ref, z0, z1, z2, wglu, bglu, wpa, wpb, wpc, lng, lnb, wgrp, scale, bg, wout,
             dzg_ref, dy_ref, dhc_ref, dp_ref, merged_ref, sa_ref, ac_ref, pp_ref, ge_ref, dt_ref, dya_ref, dyb_ref,
             dyc_ref, dq_ref, dbg_ref, dbglu_ref, dlng_ref, dlnb_ref, dscale_ref):
        yv = y_ref[...]
        wg = wglu[...].reshape(cw, cw)
        v = _merge_values(yv, hc_ref[...], p_ref[...], (z0[...], z1[...], z2[...]), wg, bglu[...], wpa, wpb, wpc,
                          lng[...], lnb[...], wgrp, scale[...], bg[...])
        dm = _mm_nt(dx1_ref[...], wout[...].reshape(d, d))
        ys = (v["ya"], v["yb"], v["yc"])
        dys, dbg = [], []
        for k in range(3):
            gk = v["gates"][k]
            dzk = dm * ys[k] * (gk * (1.0 - gk))
            dbg.append(_colsum(dzk))
            dzg_ref[:, k * d:(k + 1) * d] = dzk.astype(m)
            dys.append((dm * gk).astype(m))
        dsa = _mm_nt_cols(dys[0], wpa)
        dac = _mm_nt_cols(dys[1], wpb)
        dpp = _mm_nt_cols(dys[2], wpc)
        ge, sg = v["ge"], v["sg"]
        dt = dsa * ge * (sg * (1.0 - sg))
        dge = dsa * sg + _mm_nt(dt, wg)
        dy_ref[...] = dge * _gelu_grad(yv, v["th"])
        ln, sl, xh = v["ln"], v["sl"], v["xh"]
        dln = dac * (sl * (1.0 + ln * (1.0 - sl)))
        dxh = dln * lng[...]
        dhc_ref[...] = v["r"] * (dxh - jnp.mean(dxh, axis=-1, keepdims=True)
                                 - xh * jnp.mean(dxh * xh, axis=-1, keepdims=True))
        dq = dpp * scale[...]
        gw = cw // len(POOL_WINDOWS)
        for k in range(len(POOL_WINDOWS)):
            dp_ref[:, k * gw:(k + 1) * gw] = _mm_nt(dq[:, k * gw:(k + 1) * gw], wgrp[k])
        merged_ref[...] = v["merged"].astype(m)
        sa_ref[...] = v["sa"].astype(m)
        ac_ref[...] = v["ac"].astype(m)
        pp_ref[...] = v["pp"].astype(m)
        ge_ref[...] = ge.astype(m)
        dt_ref[...] = dt.astype(m)
        dya_ref[...] = dys[0]
        dyb_ref[...] = dys[1]
        dyc_ref[...] = dys[2]
        dq_ref[...] = dq.astype(m)

        @pl.when(pl.program_id(0) == 0)
        def _():
            for ref in (dbg_ref, dbglu_ref, dlng_ref, dlnb_ref, dscale_ref):
                ref[...] = jnp.zeros(ref.shape, F32)

        dbg_ref[...] += jnp.concatenate(dbg, axis=1)
        dbglu_ref[...] += _colsum(dt)
        dlng_ref[...] += _colsum(dln * xh)
        dlnb_ref[...] += _colsum(dln)
        dscale_ref[...] += _colsum(dpp * v["q"])

    tile = lambda n: BS((tm, n), lambda i: (i, 0))
    acc = lambda n: BS((1, n), lambda i: (0, 0))
    outs = pl.pallas_call(
        body, name=f"merge_bwd_l{l}", grid=(s // tm,),
        in_specs=[tile(d)] + _merge_specs(l, tm, d, cw),
        out_specs=[tile(3 * d), tile(cw), tile(cw), tile(cw), tile(d), tile(cw), tile(cw), tile(cw), tile(cw), tile(cw),
                   tile(d), tile(d), tile(d), tile(cw), acc(3 * d), acc(cw), acc(cw), acc(cw), acc(cw)],
        out_shape=[SDS((s, 3 * d), m), SDS((s, cw), F32), SDS((s, cw), F32), SDS((s, cw), F32), SDS((s, d), m),
                   SDS((s, cw), m), SDS((s, cw), m), SDS((s, cw), m), SDS((s, cw), m), SDS((s, cw), m), SDS((s, d), m),
                   SDS((s, d), m), SDS((s, d), m), SDS((s, cw), m), SDS((1, 3 * d), F32), SDS((1, cw), F32),
                   SDS((1, cw), F32), SDS((1, cw), F32), SDS((1, cw), F32)],
        compiler_params=_params(),
    )(dx1, y, hc, p, z, z, z, fw["ssm_w_glu"], sp["ssm_b_glu"], fw["ssm_w_proj"], fw["conv_w_proj"], fw["pool_w_proj"],
      sp["conv_ln_g"], sp["conv_ln_b"], sp["pool_w_group"], sp["pool_scale"], sp["b_gate"], fw["w_out"])
    names = ("dzg", "dy", "dhc", "dp", "merged", "sa", "ac", "pp", "ge", "dt", "dya", "dyb", "dyc", "dq", "db_gate",
             "db_glu", "dln_g", "dln_b", "dscale")
    return dict(zip(names, outs))


def _ffn_fwd(l, x1, norm2, wg, wu, wd):
    s, d = x1.shape
    hc = wd.shape[1]
    tm = min(1024, s)

    def body(x_ref, g_ref, wg_ref, wu_ref, wd_ref, o_ref, gate_ref, up_ref, h_ref):
        @pl.when(pl.program_id(1) == 0)
        def _():
            xv = x_ref[...]
            r = lax.rsqrt(jnp.mean(xv * xv, axis=-1, keepdims=True) + EPS)
            h_ref[...] = (xv * r * g_ref[...]).astype(h_ref.dtype)
            o_ref[...] = xv

        h = h_ref[...]
        gate = _mm_nt(h, wg_ref[...])
        up = _mm_nt(h, wu_ref[...])
        gate_ref[...] = gate
        up_ref[...] = up
        o_ref[...] += _mm(gate * _sigmoid(gate) * up, wd_ref[...])

    chunk = BS((None, tm, hc), lambda i, j: (j, i, 0))
    return pl.pallas_call(
        body, name=f"ffn_fwd_l{l}", grid=(s // tm, N_CHIPS),
        in_specs=[BS((tm, d), lambda i, j: (i, 0)), BS((None, 1, d), lambda i, j: (l, 0, 0)),
                  BS((None, hc, d), lambda i, j: (j, 0, 0)), BS((None, hc, d), lambda i, j: (j, 0, 0)),
                  BS((None, hc, d), lambda i, j: (j, 0, 0))],
        out_specs=[BS((tm, d), lambda i, j: (i, 0)), chunk, chunk, BS((tm, d), lambda i, j: (i, 0))],
        out_shape=[SDS((s, d), F32), SDS((N_CHIPS, s, hc), F32), SDS((N_CHIPS, s, hc), F32), SDS((s, d), MXU_DTYPE)],
        compiler_params=_params())(x1, norm2, wg, wu, wd)


def _ffn_bwd(l, x1, dx2, gate_pre, up_pre, norm2, wg, wu, wd):
    s, d = x1.shape
    hc = wd.shape[1]
    tm = min(512, s)
    m = MXU_DTYPE
    last = N_CHIPS - 1

    def body(x_ref, dx2_ref, gate_ref, up_ref, g_ref, wg_ref, wu_ref, wd_ref, dx1_ref, dxb_ref, act_ref, dgate_ref,
             dup_ref, dn_ref, dh_scr):
        i, j = pl.program_id(0), pl.program_id(1)

        @pl.when(j == 0)
        def _():
            dxb_ref[...] = dx2_ref[...].astype(m)
            dh_scr[...] = jnp.zeros(dh_scr.shape, F32)

        @pl.when((i == 0) & (j == 0))
        def _():
            dn_ref[...] = jnp.zeros(dn_ref.shape, F32)

        gate = gate_ref[...]
        up = up_ref[...]
        sg = _sigmoid(gate)
        silu = gate * sg
        act_ref[...] = (silu * up).astype(m).T
        dact = _mm_nt(dxb_ref[...], wd_ref[...])
        dup = (dact * silu).astype(m)
        dgate = (dact * up * (sg * (1.0 + gate * (1.0 - sg)))).astype(m)
        dup_ref[...] = dup.T
        dgate_ref[...] = dgate.T
        dh_scr[...] += _mm(dgate, wg_ref[...]) + _mm(dup, wu_ref[...])

        @pl.when(j == last)
        def _():
            xv = x_ref[...]
            r = lax.rsqrt(jnp.mean(xv * xv, axis=-1, keepdims=True) + EPS)
            xh = xv * r
            dh = dh_scr[...]
            dn_ref[...] += _colsum(dh * xh)
            dxh = dh * g_ref[...]
            dx1_ref[...] = dx2_ref[...] + r * (dxh - xh * jnp.mean(dxh * xh, axis=-1, keepdims=True))

    chunk = BS((None, hc, tm), lambda i, j: (j, 0, i))
    saved = BS((None, tm, hc), lambda i, j: (j, i, 0))
    outs = pl.pallas_call(
        body, name=f"ffn_bwd_l{l}", grid=(s // tm, N_CHIPS),
        in_specs=[BS((tm, d), lambda i, j: (i, 0)), BS((tm, d), lambda i, j: (i, 0)), saved, saved,
                  BS((None, 1, d), lambda i, j: (l, 0, 0)),
                  BS((None, hc, d), lambda i, j: (j, 0, 0)), BS((None, hc, d), lambda i, j: (j, 0, 0)),
                  BS((None, hc, d), lambda i, j: (j, 0, 0))],
        out_specs=[BS((tm, d), lambda i, j: (i, 0)), BS((tm, d), lambda i, j: (i, 0)),
                   chunk, chunk, chunk, BS((1, d), lambda i, j: (0, 0))],
        out_shape=[SDS((s, d), F32), SDS((s, d), m), SDS((N_CHIPS, hc, s), m),
                   SDS((N_CHIPS, hc, s), m), SDS((N_CHIPS, hc, s), m), SDS((1, d), F32)],
        scratch_shapes=[pltpu.VMEM((tm, d), F32)], compiler_params=_params(),
    )(x1, dx2, gate_pre, up_pre, norm2, wg, wu, wd)
    return dict(zip(("dx1", "dx2", "act", "dgate", "dup", "dnorm2"), outs))


def _loss_head(x, target, gf):
    s, d = x.shape
    tm = min(512, s)

    def body(x_ref, t_ref, g_ref, dx_ref, loss_ref, dg_ref):
        @pl.when(pl.program_id(0) == 0)
        def _():
            loss_ref[...] = jnp.zeros(loss_ref.shape, F32)
            dg_ref[...] = jnp.zeros(dg_ref.shape, F32)

        xv = x_ref[...]
        r = lax.rsqrt(jnp.mean(xv * xv, axis=-1, keepdims=True) + EPS)
        xh = xv * r
        err = xh * g_ref[...] - t_ref[...]
        loss_ref[...] += 0.5 * jnp.sum(jnp.mean(err * err, axis=-1, keepdims=True), axis=0, keepdims=True)
        dyv = err * (1.0 / d)
        dg_ref[...] += _colsum(dyv * xh)
        dxh = dyv * g_ref[...]
        dx_ref[...] = r * (dxh - xh * jnp.mean(dxh * xh, axis=-1, keepdims=True))

    return pl.pallas_call(
        body, name="loss_head", grid=(s // tm,),
        in_specs=[BS((tm, d), lambda i: (i, 0)), BS((tm, d), lambda i: (i, 0)), BS((1, d), lambda i: (0, 0))],
        out_specs=[BS((tm, d), lambda i: (i, 0)), BS((1, 1), lambda i: (0, 0)), BS((1, d), lambda i: (0, 0))],
        out_shape=[SDS((s, d), F32), SDS((1, 1), F32), SDS((1, d), F32)], compiler_params=_params())(x, target, gf)


def _in_proj_bwd(l, dres, x, norm1, w_in, du_a, dv1, dv2, du_c, dzg):
    s, d = x.shape
    nc = w_in.shape[-1]
    tm = min(256, s)
    m = MXU_DTYPE

    def body(dres_ref, x_ref, g_ref, w_ref, a_ref, b1_ref, b2_ref, c_ref, g3_ref, dx_ref, dz_ref, dn_ref):
        @pl.when(pl.program_id(0) == 0)
        def _():
            dn_ref[...] = jnp.zeros(dn_ref.shape, F32)

        dz = jnp.concatenate([a_ref[...], b1_ref[...], b2_ref[...], c_ref[...], g3_ref[...]], axis=1).astype(m)
        dz_ref[...] = dz
        dh = _mm_nt_cols(dz, w_ref)
        xv = x_ref[...]
        r = lax.rsqrt(jnp.mean(xv * xv, axis=-1, keepdims=True) + EPS)
        xh = xv * r
        dn_ref[...] += _colsum(dh * xh)
        dxh = dh * g_ref[...]
        dx_ref[...] = dres_ref[...] + r * (dxh - xh * jnp.mean(dxh * xh, axis=-1, keepdims=True))

    tile = lambda n: BS((tm, n), lambda i: (i, 0))
    return pl.pallas_call(
        body, name=f"in_proj_bwd_l{l}", grid=(s // tm,),
        in_specs=[tile(d), tile(d), BS((None, 1, d), lambda i: (l, 0, 0)),
                  BS((N_CHIPS, d, nc), lambda i: (0, 0, 0), pipeline_mode=pl.Buffered(1)),
                  tile(du_a.shape[1]), tile(dv1.shape[1]), tile(dv2.shape[1]), tile(du_c.shape[1]), tile(dzg.shape[1])],
        out_specs=[tile(d), tile(N_CHIPS * nc), BS((1, d), lambda i: (0, 0))],
        out_shape=[SDS((s, d), F32), SDS((s, N_CHIPS * nc), m), SDS((1, d), F32)], compiler_params=_params(),
    )(dres, x, norm1, w_in, du_a, dv1, dv2, du_c, dzg)


def _tn_matmul(name, a, a_spec, b, b_spec, chunk_shape, grid, place):
    last = grid[1] - 1

    def body(place_ref, a_ref, b_ref, own_ref, wire_ref, *acc):
        part = _mm(a_ref[...], b_ref[...])

        def emit(total):
            wire_ref[...] = total.astype(WIRE_DTYPE)

            @pl.when(pl.program_id(0) == place_ref[0])
            def _():
                own_ref[...] = total

        if last == 0:
            emit(part)
        else:
            @pl.when(pl.program_id(1) == 0)
            def _():
                acc[0][...] = part

            @pl.when(pl.program_id(1) > 0)
            def _():
                acc[0][...] += part

            @pl.when(pl.program_id(1) == last)
            def _():
                emit(acc[0][...])

    zeros = (0,) * len(chunk_shape)
    return pl.pallas_call(
        body, name=name,
        grid_spec=pltpu.PrefetchScalarGridSpec(
            num_scalar_prefetch=1, grid=grid, in_specs=[a_spec, b_spec],
            out_specs=[BS(chunk_shape, lambda j, t, pr: zeros), BS((None,) + chunk_shape, lambda j, t, pr: (j,) + zeros)],
            scratch_shapes=[pltpu.VMEM(chunk_shape, F32)] if last else []),
        out_shape=[SDS(chunk_shape, F32), SDS((N_CHIPS,) + chunk_shape, WIRE_DTYPE)],
        compiler_params=_params())(place, a, b)


def _scan_consts(pw_ref, lanes, reverse):
    sgn = -1.0 if reverse else 1.0
    row = lax.broadcasted_iota(jnp.int32, (8, lanes), 0)
    steps = []
    for i, k in enumerate((1, 2, 4)):
        mask = (row < 8 - k) if reverse else (row >= k)
        steps.append((k, jnp.where(mask, pw_ref[2 * i], 0.0), jnp.where(mask, sgn * pw_ref[2 * i + 1], 0.0)))
    c = 4 if reverse else 3
    return steps, pw_ref[2 * c], sgn * pw_ref[2 * c + 1]


def _scan_block(br, bi, steps, reverse):
    for k, ar, ai in steps:
        sh = 8 - k if reverse else k
        sr = pltpu.roll(br, sh, 0)
        si = pltpu.roll(bi, sh, 0)
        br, bi = br + ar * sr - ai * si, bi + ar * si + ai * sr
    return br, bi


SSM_BLOCK_GROUPS = 8


def _ssm_fwd(l, z, b2_re, b2_im, c2_re, c2_im, pw, dskip):
    s = z.shape[0]
    gc = SSM_BLOCK_GROUPS * SSM_GROUP
    gl = SSM_BLOCK_GROUPS * SSM_STATE
    nblk = b2_re.shape[0] // gc
    tw = b2_re.shape[1]

    def body(u_ref, bre2, bim2, cre2, cim2, pw_ref, d_ref, hre, him, y_ref):
        u = u_ref[...]
        hre[...] = _mm(u, _block_matrix(bre2, gl))
        him[...] = _mm(u, _block_matrix(bim2, gl))
        steps, car, cai = _scan_consts(pw_ref, gl, False)

        def step(i, carry):
            cr, ci = carry
            r0 = pl.multiple_of(i * 8, 8)
            br, bi = _scan_block(hre[pl.ds(r0, 8), :], him[pl.ds(r0, 8), :], steps, False)
            hr = br + car * cr - cai * ci
            hi = bi + car * ci + cai * cr
            hre[pl.ds(r0, 8), :] = hr
            him[pl.ds(r0, 8), :] = hi
            return jnp.broadcast_to(hr[7:8, :], (8, gl)), jnp.broadcast_to(hi[7:8, :], (8, gl))

        zero = jnp.zeros((8, gl), F32)
        lax.fori_loop(0, s // 8, step, (zero, zero))
        y_ref[...] = (_mm_nt(hre[...], _block_matrix(cre2, gl)) - _mm_nt(him[...], _block_matrix(cim2, gl))
                      + d_ref[...] * u)

    twice = BS((gc, tw), lambda k: (k, 0))
    return pl.pallas_call(
        body, name=f"ssm_fwd_l{l}", grid=(nblk,),
        in_specs=[BS((s, gc), lambda k: (0, k)), twice, twice, twice, twice, BS((10, 8, gl), lambda k: (0, 0, k)),
                  BS((1, gc), lambda k: (0, k))],
        out_specs=[BS((s, gl), lambda k: (0, k)), BS((s, gl), lambda k: (0, k)), BS((s, gc), lambda k: (0, k))],
        out_shape=[SDS((s, nblk * gl), F32), SDS((s, nblk * gl), F32), SDS((s, nblk * gc), F32)],
        compiler_params=_params())(z, b2_re, b2_im, c2_re, c2_im, pw, dskip)


def _ssm_bwd(l, dy, z, hre, him, b2_re, b2_im, c2_re, c2_im, pw, dskip):
    s = z.shape[0]
    gc = SSM_BLOCK_GROUPS * SSM_GROUP
    gl = SSM_BLOCK_GROUPS * SSM_STATE
    nblk = b2_re.shape[0] // gc
    tw = b2_re.shape[1]

    def body(dy_ref, u_ref, hre_ref, him_ref, bre2, bim2, cre2, cim2, pw_ref, d_ref,
             du_ref, dbre_ref, dbim_ref, dcre_ref, dcim_ref, dar_ref, dai_ref, dd_ref, gre, gim):
        dyv = dy_ref[...]
        u = u_ref[...]
        gre[...] = _mm(dyv, _block_matrix(cre2, gl))
        gim[...] = -_mm(dyv, _block_matrix(cim2, gl))
        dcre_ref[...] = _block_diagonal_of(_mm_tn(dyv, hre_ref[...]))
        dcim_ref[...] = -_block_diagonal_of(_mm_tn(dyv, him_ref[...]))
        dd_ref[...] = _colsum(dyv * u)
        row = lax.broadcasted_iota(jnp.int32, (8, gl), 0)
        steps, car, cai = _scan_consts(pw_ref, gl, True)
        n8 = s // 8

        def step(ii, carry):
            cr, ci, accr, acci = carry
            i = n8 - 1 - ii
            r0 = pl.multiple_of(i * 8, 8)
            br, bi = _scan_block(gre[pl.ds(r0, 8), :], gim[pl.ds(r0, 8), :], steps, True)
            dr = br + car * cr - cai * ci
            di = bi + car * ci + cai * cr
            gre[pl.ds(r0, 8), :] = dr
            gim[pl.ds(r0, 8), :] = di
            rp = pl.multiple_of(jnp.maximum(i - 1, 0) * 8, 8)
            keep = jnp.where(i > 0, 1.0, 0.0)
            pr = jnp.where(row >= 1, pltpu.roll(hre_ref[pl.ds(r0, 8), :], 1, 0),
                           keep * pltpu.roll(hre_ref[pl.ds(rp, 8), :], 1, 0))
            pi = jnp.where(row >= 1, pltpu.roll(him_ref[pl.ds(r0, 8), :], 1, 0),
                           keep * pltpu.roll(him_ref[pl.ds(rp, 8), :], 1, 0))
            accr = accr + dr * pr + di * pi
            acci = acci + di * pr - dr * pi
            return (jnp.broadcast_to(dr[0:1, :], (8, gl)), jnp.broadcast_to(di[0:1, :], (8, gl)), accr, acci)

        zero = jnp.zeros((8, gl), F32)
        _, _, accr, acci = lax.fori_loop(0, n8, step, (zero, zero, zero, zero))
        dar_ref[...] = _colsum(accr)
        dai_ref[...] = _colsum(acci)
        dbr = gre[...]
        dbi = gim[...]
        du_ref[...] = (dyv * d_ref[...] + _mm_nt(dbr, _block_matrix(bre2, gl))
                       + _mm_nt(dbi, _block_matrix(bim2, gl))).astype(du_ref.dtype)
        dbre_ref[...] = _block_diagonal_of(_mm_tn(u, dbr))
        dbim_ref[...] = _block_diagonal_of(_mm_tn(u, dbi))

    col = lambda n: BS((s, n), lambda k: (0, k))
    twice = BS((gc, tw), lambda k: (k, 0))
    diag = BS((gc, SSM_STATE), lambda k: (k, 0))
    outs = pl.pallas_call(
        body, name=f"ssm_bwd_l{l}", grid=(nblk,),
        in_specs=[col(gc), col(gc), col(gl), col(gl), twice, twice, twice, twice,
                  BS((10, 8, gl), lambda k: (0, 0, k)), BS((1, gc), lambda k: (0, k))],
        out_specs=[col(gc), diag, diag, diag, diag, BS((1, gl), lambda k: (0, k)),
                   BS((1, gl), lambda k: (0, k)), BS((1, gc), lambda k: (0, k))],
        out_shape=[SDS((s, nblk * gc), MXU_DTYPE)] + [SDS((nblk * gc, SSM_STATE), F32)] * 4
        + [SDS((1, nblk * gl), F32), SDS((1, nblk * gl), F32), SDS((1, nblk * gc), F32)],
        scratch_shapes=[pltpu.VMEM((s, gl), F32), pltpu.VMEM((s, gl), F32)], compiler_params=_params(),
    )(dy, z, hre, him, b2_re, b2_im, c2_re, c2_im, pw, dskip)
    return dict(zip(("du", "dbbar_re", "dbbar_im", "dc_re", "dc_im", "dabar_re", "dabar_im", "dd"), outs))


def _conv_fwd(l, z, wdw, bdw):
    s = z.shape[0]
    cw = wdw.shape[1]
    lb = 128
    tr = min(256, s)
    off1 = cw // lb
    off2 = 2 * cw // lb

    def body(v1_ref, v2_ref, w_ref, b_ref, hc_ref, scr):
        scr[0:CONV_PAD, :] = jnp.zeros((CONV_PAD, lb), F32)
        scr[CONV_PAD:, :] = v1_ref[...] * _sigmoid(v2_ref[...])
        for t in range(s // tr):
            acc = jnp.broadcast_to(b_ref[...], (tr, lb))
            for k in range(CONV_KERNEL):
                acc = acc + w_ref[pl.ds(k, 1), :] * scr[pl.ds(t * tr + CONV_PAD - (CONV_KERNEL - 1) + k, tr), :]
            hc_ref[pl.ds(t * tr, tr), :] = acc

    return pl.pallas_call(
        body, name=f"conv_fwd_l{l}", grid=(cw // lb,),
        in_specs=[BS((s, lb), lambda k: (0, off1 + k)), BS((s, lb), lambda k: (0, off2 + k)),
                  BS((CONV_KERNEL, lb), lambda k: (0, k)), BS((1, lb), lambda k: (0, k))],
        out_specs=BS((s, lb), lambda k: (0, k)), out_shape=SDS((s, cw), F32),
        scratch_shapes=[pltpu.VMEM((s + CONV_PAD, lb), F32)], compiler_params=_params())(z, z, wdw, bdw)


def _conv_bwd(l, dhc, z, wdw):
    s = z.shape[0]
    cw = wdw.shape[1]
    lb = 128
    tr = min(256, s)
    off1 = cw // lb
    off2 = 2 * cw // lb
    nb = cw // lb

    def body(d_ref, v1_ref, v2_ref, w_ref, dv1_ref, dv2_ref, dw_ref, db_ref, hpad, dpad):
        v1 = v1_ref[...]
        sg = _sigmoid(v2_ref[...])
        dv = d_ref[...]
        hpad[0:CONV_PAD, :] = jnp.zeros((CONV_PAD, lb), F32)
        hpad[CONV_PAD:, :] = v1 * sg
        dpad[0:s, :] = dv
        dpad[s:, :] = jnp.zeros((CONV_PAD, lb), F32)
        db_ref[...] = _colsum(dv)
        dws = [jnp.zeros((8, lb), F32) for _ in range(CONV_KERNEL)]
        for t in range(s // tr):
            dt = d_ref[pl.ds(t * tr, tr), :]
            acc = jnp.zeros((tr, lb), F32)
            for k in range(CONV_KERNEL):
                acc = acc + w_ref[pl.ds(k, 1), :] * dpad[pl.ds(t * tr + (CONV_KERNEL - 1) - k, tr), :]
                prod = dt * hpad[pl.ds(t * tr + CONV_PAD - (CONV_KERNEL - 1) + k, tr), :]
                dws[k] = dws[k] + jnp.sum(prod.reshape(tr // 8, 8, lb), axis=0)
            sgt = _sigmoid(v2_ref[pl.ds(t * tr, tr), :])
            v1t = v1_ref[pl.ds(t * tr, tr), :]
            dv1_ref[pl.ds(t * tr, tr), :] = (acc * sgt).astype(dv1_ref.dtype)
            dv2_ref[pl.ds(t * tr, tr), :] = (acc * v1t * (sgt * (1.0 - sgt))).astype(dv2_ref.dtype)
        for k in range(CONV_KERNEL):
            dw_ref[pl.ds(k, 1), :] = _colsum(dws[k])

    return pl.pallas_call(
        body, name=f"conv_bwd_l{l}", grid=(nb,),
        in_specs=[BS((s, lb), lambda k: (0, k)), BS((s, lb), lambda k: (0, off1 + k)),
                  BS((s, lb), lambda k: (0, off2 + k)), BS((CONV_KERNEL, lb), lambda k: (0, k))],
        out_specs=[BS((s, lb), lambda k: (0, k)), BS((s, lb), lambda k: (0, k)),
                   BS((CONV_KERNEL, lb), lambda k: (0, k)), BS((1, lb), lambda k: (0, k))],
        out_shape=[SDS((s, cw), MXU_DTYPE), SDS((s, cw), MXU_DTYPE), SDS((CONV_KERNEL, cw), F32), SDS((1, cw), F32)],
        scratch_shapes=[pltpu.VMEM((s + CONV_PAD, lb), F32), pltpu.VMEM((s + CONV_PAD, lb), F32)],
        compiler_params=_params())(dhc, z, z, wdw)


def _pool_window(k):
    return jnp.where(k == 0, float(POOL_WINDOWS[0]),
                     jnp.where(k == 1, float(POOL_WINDOWS[1]),
                               jnp.where(k == 2, float(POOL_WINDOWS[2]), float(POOL_WINDOWS[3]))))


def _pool_fwd(l, z, pw_width):
    s = z.shape[0]
    lb = pw_width // len(POOL_WINDOWS)
    off = 3 * pw_width // lb

    def body(u_ref, p_ref):
        k = pl.program_id(0)
        u = u_ref[...]
        row = lax.broadcasted_iota(jnp.int32, (s, lb), 0)
        sums = [u]
        for sh in (1, 2, 4, 8):
            prev = sums[-1]
            sums.append(prev + jnp.where(row >= sh, pltpu.roll(prev, sh, 0), 0.0))
        sel = jnp.where(k == 0, sums[1], jnp.where(k == 1, sums[2], jnp.where(k == 2, sums[3], sums[4])))
        cnt = jnp.minimum((row + 1).astype(F32), _pool_window(k))
        p_ref[...] = sel / cnt - u

    return pl.pallas_call(
        body, name=f"pool_fwd_l{l}", grid=(len(POOL_WINDOWS),),
        in_specs=[BS((s, lb), lambda k: (0, off + k))], out_specs=BS((s, lb), lambda k: (0, k)),
        out_shape=SDS((s, pw_width), F32), compiler_params=_params())(z)


def _pool_bwd(l, dp):
    s, width = dp.shape
    lb = width // len(POOL_WINDOWS)

    def body(d_ref, du_ref):
        k = pl.program_id(0)
        dv = d_ref[...]
        row = lax.broadcasted_iota(jnp.int32, (s, lb), 0)
        cnt = jnp.minimum((row + 1).astype(F32), _pool_window(k))
        sums = [dv / cnt]
        for sh in (1, 2, 4, 8):
            prev = sums[-1]
            sums.append(prev + jnp.where(row < s - sh, pltpu.roll(prev, s - sh, 0), 0.0))
        sel = jnp.where(k == 0, sums[1], jnp.where(k == 1, sums[2], jnp.where(k == 2, sums[3], sums[4])))
        du_ref[...] = (sel - dv).astype(du_ref.dtype)

    return pl.pallas_call(
        body, name=f"pool_bwd_l{l}", grid=(len(POOL_WINDOWS),),
        in_specs=[BS((s, lb), lambda k: (0, k))], out_specs=BS((s, lb), lambda k: (0, k)),
        out_shape=SDS((s, width), MXU_DTYPE), compiler_params=_params())(dp)


def _zoh(a_re, a_im, log_dt):
    dt = jnp.exp(log_dt)
    mag = jnp.exp(dt * a_re)
    ang = dt * a_im
    abar_re = mag * jnp.cos(ang)
    abar_im = mag * jnp.sin(ang)
    den = a_re * a_re + a_im * a_im
    nr = abar_re - 1.0
    ni = abar_im
    f_re = (nr * a_re + ni * a_im) / den
    f_im = (ni * a_re - nr * a_im) / den
    return abar_re, abar_im, f_re, f_im


def _zoh_fwd(l, a_re, a_im, log_dt):
    def body(ar, ai, ld, o0, o1, o2, o3):
        for ref, val in zip((o0, o1, o2, o3), _zoh(ar[...], ai[...], ld[...])):
            ref[...] = val

    return pl.pallas_call(body, name=f"zoh_fwd_l{l}", out_shape=[SDS(a_re.shape, F32)] * 4)(a_re, a_im, log_dt)


def _zoh_bwd(l, a_re, a_im, log_dt, cts):
    def body(ar, ai, ld, c0, c1, c2, c3, dar, dai, dld):
        _, vjp = jax.vjp(_zoh, ar[...], ai[...], ld[...])
        g = vjp((c0[...], c1[...], c2[...], c3[...]))
        dar[...] = g[0]
        dai[...] = g[1]
        dld[...] = g[2]

    return pl.pallas_call(body, name=f"zoh_bwd_l{l}",
                          out_shape=[SDS(a_re.shape, F32), SDS(a_re.shape, F32), SDS(log_dt.shape, F32)],
                          )(a_re, a_im, log_dt, *cts)


def _bbar_fwd(l, f_re, f_im, b_re, b_im, c_re, c_im):
    g, p, n = b_re.shape[1:]
    m = MXU_DTYPE

    def body(fr, fi, br, bi, cr, ci, *outs):
        r = lax.broadcasted_iota(jnp.int32, (n, 2 * n), 0)
        c = lax.broadcasted_iota(jnp.int32, (n, 2 * n), 1)
        twice = jnp.where((c & (n - 1)) == r, 1.0, 0.0).astype(m)
        vals = (fr[...] * br[...] - fi[...] * bi[...], fr[...] * bi[...] + fi[...] * br[...], cr[...], ci[...])
        for o_ref, v in zip(outs, vals):
            o_ref[...] = _mm(v.astype(m).reshape(g * p, n), twice).astype(m)

    whole = lambda shp: BS(shp, lambda i: (0,) * len(shp))
    layer = BS((None, g, p, n), lambda i: (l, 0, 0, 0))
    return pl.pallas_call(body, name=f"bbar_fwd_l{l}", grid=(1,),
                          in_specs=[whole((g, 1, n)), whole((g, 1, n)), layer, layer, layer, layer],
                          out_specs=[whole((g * p, 2 * n))] * 4,
                          out_shape=[SDS((g * p, 2 * n), m)] * 4)(f_re, f_im, b_re, b_im, c_re, c_im)


def _block_mask(rows, lanes):
    r = lax.broadcasted_iota(jnp.int32, (rows, lanes), 0)
    c = lax.broadcasted_iota(jnp.int32, (rows, lanes), 1)
    return (r >> (SSM_GROUP.bit_length() - 1)) == (c >> (SSM_STATE.bit_length() - 1))


def _block_matrix(twice_ref, lanes):
    v = twice_ref[...]
    tiled = jnp.concatenate([v] * (lanes // v.shape[1]), axis=1)
    return jnp.where(_block_mask(v.shape[0], lanes), tiled, jnp.zeros_like(tiled))


def _block_diagonal_of(full):
    rows, lanes = full.shape
    kept = jnp.where(_block_mask(rows, lanes), full, 0.0)
    folded = kept[:, 0:128]
    for q in range(1, lanes // 128):
        folded = folded + kept[:, q * 128:(q + 1) * 128]
    return (folded + pltpu.roll(folded, SSM_STATE, 1))[:, :SSM_STATE]


def _bbar_bwd(l, f_re, f_im, b_re, b_im, d_re, d_im):
    g, p, n = b_re.shape[1:]

    def body(fr, fi, br, bi, dr, di, dfr, dfi, dbr, dbi):
        dfr[...] = jnp.sum(dr[...] * br[...] + di[...] * bi[...], axis=1, keepdims=True)
        dfi[...] = jnp.sum(di[...] * br[...] - dr[...] * bi[...], axis=1, keepdims=True)
        dbr[...] = fr[...] * dr[...] + fi[...] * di[...]
        dbi[...] = fr[...] * di[...] - fi[...] * dr[...]

    whole = lambda shp: BS(shp, lambda i: (0,) * len(shp))
    layer = BS((None, g, p, n), lambda i: (l, 0, 0, 0))
    return pl.pallas_call(body, name=f"bbar_bwd_l{l}", grid=(1,),
                          in_specs=[whole((g, 1, n)), whole((g, 1, n)), layer, layer, whole((g, p, n)),
                                    whole((g, p, n))],
                          out_specs=[whole((g, 1, n)), whole((g, 1, n)), whole((g, p, n)), whole((g, p, n))],
                          out_shape=[SDS((g, 1, n), F32), SDS((g, 1, n), F32), SDS((g, p, n), F32),
                                     SDS((g, p, n), F32)])(f_re, f_im, b_re, b_im, d_re, d_im)


def _powers(l, abar_re, abar_im):
    lanes = abar_re.shape[1]

    def body(ar_ref, ai_ref, o_ref):
        ar, ai = ar_ref[...], ai_ref[...]
        pows = [(ar, ai)]
        for _ in range(7):
            pr, pi = pows[-1]
            pows.append((pr * ar - pi * ai, pr * ai + pi * ar))
        row = lax.broadcasted_iota(jnp.int32, (8, lanes), 0)
        for i, k in enumerate((1, 2, 4)):
            o_ref[2 * i] = jnp.broadcast_to(pows[k - 1][0], (8, lanes))
            o_ref[2 * i + 1] = jnp.broadcast_to(pows[k - 1][1], (8, lanes))
        for slot, order in ((3, range(8)), (4, range(7, -1, -1))):
            vr = jnp.zeros((8, lanes), F32)
            vi = jnp.zeros((8, lanes), F32)
            for r, e in enumerate(order):
                vr = jnp.where(row == r, pows[e][0], vr)
                vi = jnp.where(row == r, pows[e][1], vi)
            o_ref[2 * slot] = vr
            o_ref[2 * slot + 1] = vi

    return pl.pallas_call(body, name=f"powers_l{l}", out_shape=SDS((10, 8, lanes), F32))(abar_re, abar_im)


def _ssm_prepare(l, prm):
    g, n, p = SSM_GROUPS, SSM_STATE, SSM_GROUP
    a_re, a_im = prm["ssm_a_re"][l], prm["ssm_a_im"][l]
    log_dt = prm["ssm_log_dt"][l].reshape(g, 1)
    abar_re, abar_im, f_re, f_im = _zoh_fwd(l, a_re, a_im, log_dt)
    f_re, f_im = f_re.reshape(g, 1, n), f_im.reshape(g, 1, n)
    b2_re, b2_im, c2_re, c2_im = _bbar_fwd(l, f_re, f_im, prm["ssm_b_re"], prm["ssm_b_im"], prm["ssm_c_re"],
                                           prm["ssm_c_im"])
    pw = _powers(l, abar_re.reshape(1, g * n), abar_im.reshape(1, g * n))
    return dict(a_re=a_re, a_im=a_im, log_dt=log_dt, f_re=f_re, f_im=f_im, b2_re=b2_re, b2_im=b2_im, c2_re=c2_re,
                c2_im=c2_im, pw=pw, dskip=prm["ssm_d"][l].reshape(1, g * p))


def _ssm_param_grads(l, sd, r, prm):
    g, n, p = SSM_GROUPS, SSM_STATE, SSM_GROUP
    dfr, dfi, db_re, db_im = _bbar_bwd(l, sd["f_re"], sd["f_im"], prm["ssm_b_re"], prm["ssm_b_im"],
                                       r["dbbar_re"].reshape(g, p, n), r["dbbar_im"].reshape(g, p, n))
    cts = (r["dabar_re"].reshape(g, n), r["dabar_im"].reshape(g, n), dfr.reshape(g, n), dfi.reshape(g, n))
    da_re, da_im, dlog_dt = _zoh_bwd(l, sd["a_re"], sd["a_im"], sd["log_dt"], cts)
    return dict(ssm_a_re=da_re, ssm_a_im=da_im, ssm_log_dt=dlog_dt.reshape(g), ssm_b_re=db_re, ssm_b_im=db_im,
                ssm_c_re=r["dc_re"].reshape(g, p, n), ssm_c_im=r["dc_im"].reshape(g, p, n),
                ssm_d=r["dd"].reshape(g, p))


def _ffn_weight_grads(l, fb, dx2, s, place):
    d = dx2.shape[1]
    hcn = fb["act"].shape[1]
    g = {}
    for name, key, rhs in (("ffn_w_gate", "dgate", fb["h2"]), ("ffn_w_up", "dup", fb["h2"]),
                           ("ffn_w_down", "act", fb["dx2"])):
        g[name] = _tn_matmul(f"d{name}_l{l}", fb[key], BS((None, hcn, s), lambda j, t, pr: (j, 0, 0)), rhs,
                             BS((s, d), lambda j, t, pr: (0, 0)), (hcn, d), (N_CHIPS, 1), place)
    return g


def _in_weight_grad(l, ht, dz, place):
    d, s = ht.shape
    ncw = dz.shape[1] // N_CHIPS
    return _tn_matmul(f"dw_in_l{l}", ht, BS((d, s), lambda j, t, pr: (0, 0)), dz, BS((s, ncw), lambda j, t, pr: (0, j)),
                      (d, ncw), (N_CHIPS, 1), place)


def _fused_tn(name, pairs, kinds, s, place):
    n = len(pairs)

    def shape_of(a, b, kind):
        k, m = a.shape[1], b.shape[1]
        if kind == "rows":
            return (N_CHIPS, k // N_CHIPS, m)
        if kind == "cols":
            return (N_CHIPS, k, m // N_CHIPS)
        return (k // 128, 128, 128)

    shapes = [shape_of(a, b, kind) for (a, b), kind in zip(pairs, kinds)]
    out_shape = []
    for shp, kind in zip(shapes, kinds):
        out_shape += [SDS(shp, F32)] if kind == "groups" else [SDS(shp[1:], F32), SDS(shp, WIRE_DTYPE)]

    def body(place_ref, *refs):
        ins, outs, accs = refs[:2 * n], refs[2 * n:2 * n + len(out_shape)], refs[2 * n + len(out_shape):]
        o = 0
        for i, kind in enumerate(kinds):
            a, b = ins[2 * i][...], ins[2 * i + 1][...]
            if kind == "groups":
                for k in range(shapes[i][0]):
                    outs[o][k] = _mm_tn(a[:, k * 128:(k + 1) * 128], b[:, k * 128:(k + 1) * 128])
                o += 1
                continue
            acc = accs[i]
            if kind == "rows":
                acc[...] = _mm_tn(a, b).reshape(acc.shape)
            else:
                full = _mm_tn(a, b)
                nc = acc.shape[2]
                for j in range(N_CHIPS):
                    acc[j] = full[:, j * nc:(j + 1) * nc]
            outs[o][...] = acc[place_ref[0]]
            outs[o + 1][...] = acc[...].astype(WIRE_DTYPE)
            o += 2

    whole = lambda shp: BS(shp, lambda t, pr: (0,) * len(shp))
    outs = pl.pallas_call(
        body, name=name,
        grid_spec=pltpu.PrefetchScalarGridSpec(
            num_scalar_prefetch=1, grid=(1,),
            in_specs=[whole(v.shape) for pair in pairs for v in pair],
            out_specs=[whole(o.shape) for o in out_shape],
            scratch_shapes=[pltpu.VMEM(shp, F32) for shp in shapes]),
        out_shape=out_shape, compiler_params=_params(),
    )(place, *[v for pair in pairs for v in pair])
    res, o = [], 0
    for kind in kinds:
        if kind == "groups":
            res.append(outs[o])
            o += 1
        else:
            res.append((outs[o], outs[o + 1]))
            o += 2
    return res


def _mixer_weight_grads(l, sv, mb, dx1, s, place):
    g = {}
    (g["w_out"], g["ssm_w_glu"]) = _fused_tn(f"dw_out_glu_l{l}", [(mb["merged"], dx1), (mb["ge"], mb["dt"])],
                                            ("rows", "rows"), s, place)
    (g["ssm_w_proj"], g["conv_w_proj"], g["pool_w_proj"]) = _fused_tn(
        f"dw_proj_l{l}", [(mb["sa"], mb["dya"]), (mb["ac"], mb["dyb"]), (mb["pp"], mb["dyc"])],
        ("cols", "cols", "cols"), s, place)
    (dwgrp,) = _fused_tn(f"dpool_w_group_l{l}", [(sv["p"], mb["dq"])], ("groups",), s, place)
    return g, dwgrp


def _local_step(x, target, weights_of, prm, place, on_grads=None):
    s, d = x.shape
    cw = prm["ssm_b_glu"].shape[1]
    sp = {k: prm[k].reshape(N_LAYERS, 1, -1) for k in ("norm1", "norm2", "b_gate", "ssm_b_glu", "conv_ln_g", "conv_ln_b",
                                                        "pool_scale", "conv_b_dw")}
    sp["pool_w_group"] = prm["pool_w_group"]
    saved = []
    xin = x
    prepared = [_ssm_prepare(l, prm) for l in range(N_LAYERS)]
    ready = tuple(sd[k] for sd in prepared for k in ("pw", "b2_re", "c2_re"))
    for l in range(N_LAYERS):
        fw = weights_of(l, "in", (xin,) + (ready if l == 0 else ()))
        sd = prepared[l]
        z, h = _in_proj(l, xin, sp["norm1"], fw["w_in"])
        hre, him, y = _ssm_fwd(l, z, sd["b2_re"], sd["b2_im"], sd["c2_re"], sd["c2_im"], sd["pw"], sd["dskip"])
        p = _pool_fwd(l, z, cw)
        fw.update(weights_of(l, "mixer", (y, p)))
        wdw = fw["conv_w_dw"]
        hc = _conv_fwd(l, z, wdw, sp["conv_b_dw"][l])
        x1 = _merge_fwd(l, xin, y, hc, p, z, fw, sp)
        fw.update(weights_of(l, "ffn", (x1,)))
        x2, gate, up, h2 = _ffn_fwd(l, x1, sp["norm2"], fw["ffn_w_gate"], fw["ffn_w_up"], fw["ffn_w_down"])
        saved.append(dict(x=xin, z=z, h=h, hre=hre, him=him, y=y, hc=hc, p=p, x1=x1, sd=sd, wdw=wdw, fw=fw,
                          gate=gate, up=up, h2=h2))
        xin = x2
    dx, loss, dfinal = _loss_head(xin, target, prm["final_norm"].reshape(1, d))
    big = [None] * N_LAYERS
    small = [None] * N_LAYERS
    norm2_rows = sp["norm2"]
    started = (lambda l, group, grads: on_grads(l, group, grads)) if on_grads is not None else (lambda *a: 0.0)
    for l in reversed(range(N_LAYERS)):
        sv = saved[l]
        sd, fw = sv["sd"], sv["fw"]
        fb = _ffn_bwd(l, sv["x1"], dx, sv["gate"], sv["up"], norm2_rows, fw["ffn_w_gate"], fw["ffn_w_up"],
                      fw["ffn_w_down"])
        fb["h2"] = sv["h2"]
        big[l] = _ffn_weight_grads(l, fb, dx, s, place)
        spl = dict(sp, ssm_b_glu=sp["ssm_b_glu"] + started(l, "ffn", big[l]))
        mb = _merge_bwd(l, fb["dx1"], sv["y"], sv["hc"], sv["p"], sv["z"], fw, spl)
        mixer, dwgrp = _mixer_weight_grads(l, sv, mb, fb["dx1"], s, place)
        big[l].update(mixer)
        wdw = sv["wdw"] + started(l, "mixer", mixer)
        du_c = _pool_bwd(l, mb["dp"])
        dv1, dv2, dwdw, dbdw = _conv_bwd(l, mb["dhc"], sv["z"], wdw)
        sr = _ssm_bwd(l, mb["dy"], sv["z"], sv["hre"], sv["him"], sd["b2_re"], sd["b2_im"], sd["c2_re"],
                      sd["c2_im"], sd["pw"], sd["dskip"])
        dx, dz, dnorm1 = _in_proj_bwd(l, fb["dx1"], sv["x"], sp["norm1"], fw["w_in"], sr["du"], dv1, dv2, du_c, mb["dzg"])
        w_in_grad = {"w_in": _in_weight_grad(l, sv["h"], dz, place)}
        big[l].update(w_in_grad)
        sg = _ssm_param_grads(l, sd, sr, prm)
        sg.update(norm1=dnorm1.reshape(d), b_gate=mb["db_gate"].reshape(3 * d), ssm_b_glu=mb["db_glu"].reshape(cw),
                  conv_b_dw=dbdw.reshape(cw), conv_ln_g=mb["dln_g"].reshape(cw), conv_ln_b=mb["dln_b"].reshape(cw),
                  pool_w_group=dwgrp, pool_scale=mb["dscale"].reshape(cw), norm2=fb["dnorm2"].reshape(d),
                  conv_w_dw=dwdw)
        small[l] = sg
        if l == N_LAYERS - 1:
            sg = dict(sg, final_norm=dfinal.reshape(d))
        norm2_rows = sp["norm2"] + (started(l, "in", w_in_grad) + started(l, "small", sg))
    return loss[0, 0], dx, big, small, dfinal.reshape(d)


def _place():
    return lax.axis_index("x"), lax.axis_index("y"), lax.axis_index("c")


def _other_chips(x, y):
    return [(1 - x, y), (x, 1 - y), (1 - x, 1 - y)]


def _remote(src, dst, send_sem, recv_sem, device):
    return pltpu.make_async_remote_copy(src_ref=src, dst_ref=dst, send_sem=send_sem, recv_sem=recv_sem,
                                        device_id=device, device_id_type=MESH)


def _hbm(v):
    return pltpu.with_memory_space_constraint(v, pltpu.HBM)


def _cast_into(name, w, place, dtype, after=()):
    nl, k, n = w.shape
    tr = _row_tile(k, n)
    nt = k // tr

    def body(place_ref, w_ref, *rest):
        o0_ref, o1_ref = rest[len(after):]

        @pl.when(pl.program_id(0) == 0)
        def _():
            o0_ref[...] = w_ref[...].astype(dtype)

        @pl.when(pl.program_id(0) == 1)
        def _():
            o1_ref[...] = w_ref[...].astype(dtype)

    return pl.pallas_call(
        body, name=f"cast_{name}",
        grid_spec=pltpu.PrefetchScalarGridSpec(
            num_scalar_prefetch=1, grid=(nl, nt),
            in_specs=[BS((None, tr, n), lambda l, t, pr: (l, t, 0))] + [ANY] * len(after),
            out_specs=[BS((None, tr, n), lambda l, t, pr: (pr[0], t * (1 - l) + (nt - 1) * l, 0)),
                       BS((None, tr, n), lambda l, t, pr: (pr[0], t * l, 0))]),
        out_shape=[SDS((N_CHIPS, k, n), dtype)] * 2)(place, w, *after)


def _cast_small_into(tag, ws, dtypes, place, after=()):
    n = len(ws)

    def body(place_ref, *refs):
        ins, outs = refs[:n], refs[n + len(after):]
        for i in range(n):
            @pl.when(pl.program_id(0) == 0)
            def _():
                outs[2 * i][...] = ins[i][...].astype(dtypes[i])

            @pl.when(pl.program_id(0) == 1)
            def _():
                outs[2 * i + 1][...] = ins[i][...].astype(dtypes[i])

    slot = lambda w: BS((None,) + w.shape[1:], lambda l, pr: (pr[0], 0, 0))
    outs = pl.pallas_call(
        body, name=f"cast_{tag}",
        grid_spec=pltpu.PrefetchScalarGridSpec(
            num_scalar_prefetch=1, grid=(N_LAYERS,),
            in_specs=[BS((None,) + w.shape[1:], lambda l, pr: (l, 0, 0)) for w in ws] + [ANY] * len(after),
            out_specs=[slot(w) for w in ws for _ in range(N_LAYERS)]),
        out_shape=[SDS((N_CHIPS,) + w.shape[1:], dt) for w, dt in zip(ws, dtypes) for _ in range(N_LAYERS)],
    )(place, *ws, *after)
    return [tuple(outs[N_LAYERS * i:N_LAYERS * (i + 1)]) for i in range(n)]


def _gather_rows(buf, c):
    k = buf.shape[1]
    if k % 2:
        return pl.ds(0, k)
    return pl.ds(pl.multiple_of(c * (k // 2), 8), k // 2)


def _allgather_start(tag, groups):
    ng = len(groups)
    sizes = [len(g) for g in groups]
    first = [sum(sizes[:g]) for g in range(ng)]
    flat = [b for g in groups for b in g]
    nb = len(flat)

    def body(*refs):
        ins = refs[:nb]
        sems = refs[nb:nb + 2 * ng]
        token = refs[-1]
        x, y, c = _place()
        jme = 2 * x + y
        for g in range(ng):
            for a in range(sizes[g]):
                buf = ins[first[g] + a]
                blk = buf.at[jme, _gather_rows(buf, c)]
                for k, (cx, cy) in enumerate(_other_chips(x, y)):
                    _remote(blk, blk, sems[2 * g].at[3 * a + k], sems[2 * g + 1].at[3 * a + k], (cx, cy, c)).start()
        token[...] = jnp.zeros(token.shape, F32)

    sem_shapes = [pltpu.SemaphoreType.DMA((3 * sizes[g // 2],)) for g in range(2 * ng)]
    outs = pl.pallas_call(
        body, name=f"allgather_start_{tag}", in_specs=[HBM] * nb,
        out_specs=[SEM] * (2 * ng) + [HBM] * nb + [pl.BlockSpec(memory_space=pltpu.VMEM)],
        out_shape=sem_shapes + [pltpu.HBM(b.shape, b.dtype) for b in flat] + [SDS((8, 128), F32)],
        input_output_aliases={i: 2 * ng + i for i in range(nb)},
        compiler_params=pltpu.CompilerParams(has_side_effects=SIDE_EFFECT))(*[_hbm(b) for b in flat])
    per_group = [(outs[2 * g], outs[2 * g + 1], outs[2 * ng + first[g]:2 * ng + first[g] + sizes[g]])
                 for g in range(ng)]
    return per_group, outs[-1]


def _allgather_wait(l, send_sems, recv_sems, bufs, after):
    n = len(bufs)

    def body(*refs):
        ins = refs[:n]
        ssem, rsem = refs[n], refs[n + 1]
        x, y, c = _place()
        jme = 2 * x + y
        for a in range(n):
            rows = _gather_rows(ins[a], c)
            for k, (cx, cy) in enumerate(_other_chips(x, y)):
                cp = _remote(ins[a].at[jme, rows], ins[a].at[2 * cx + cy, rows], ssem.at[3 * a + k],
                             rsem.at[3 * a + k], (cx, cy, c))
                cp.wait_send()
                cp.wait_recv()

    return pl.pallas_call(
        body, name=f"allgather_wait_{l}", in_specs=[HBM] * n + [SEM, SEM] + [ANY] * len(after), out_specs=[HBM] * n,
        out_shape=[pltpu.HBM(b.shape, b.dtype) for b in bufs], input_output_aliases={i: i for i in range(n)},
        compiler_params=pltpu.CompilerParams(has_side_effects=SIDE_EFFECT))(*bufs, send_sems, recv_sems, *after)


def _allgather_forward(l, bufs):
    n = len(bufs)
    split = [a for a in range(n) if bufs[a].shape[1] % 2 == 0]

    def body(*refs):
        ins = refs[:n]
        send_sems, recv_sems = refs[2 * n:]
        x, y, c = _place()
        sibling = (x, y, 1 - c)
        copies = []
        for a in split:
            for k, (cx, cy) in enumerate(_other_chips(x, y)):
                blk = ins[a].at[2 * cx + cy, _gather_rows(ins[a], c)]
                cp = _remote(blk, blk, send_sems.at[a, k], recv_sems.at[a, k], sibling)
                cp.start()
                copies.append(cp)
        for a in split:
            for k, (cx, cy) in enumerate(_other_chips(x, y)):
                blk = ins[a].at[2 * cx + cy, _gather_rows(ins[a], 1 - c)]
                _remote(blk, blk, send_sems.at[a, k], recv_sems.at[a, k], sibling).wait_recv()
        for cp in copies:
            cp.wait_send()

    sem = pltpu.SemaphoreType.DMA((n, 3))
    return pl.pallas_call(
        body, name=f"allgather_forward_{l}", in_specs=[ANY] * n, out_specs=[ANY] * n,
        out_shape=[SDS(b.shape, b.dtype) for b in bufs], input_output_aliases={i: i for i in range(n)},
        scratch_shapes=[sem, sem])(*bufs)


def _rs_to_owner(l, parts):
    n = len(parts)
    lands = [lax.empty((3,) + p.shape[1:], p.dtype) for p in parts]

    def body(*refs):
        ins, zones = refs[:n], refs[n:2 * n]
        send_sems, recv_sems = refs[2 * n], refs[2 * n + 1]
        token = refs[-1]
        x, y, c = _place()
        for a in range(n):
            for k, (cx, cy) in enumerate(_other_chips(x, y)):
                _remote(ins[a].at[2 * cx + cy], zones[a].at[k], send_sems.at[3 * a + k], recv_sems.at[3 * a + k],
                        (cx, cy, c)).start()
        token[...] = jnp.zeros(token.shape, F32)

    sem = pltpu.SemaphoreType.DMA((3 * n,))
    outs = pl.pallas_call(
        body, name=f"rs_to_owner_start_{l}", in_specs=[HBM] * (2 * n),
        out_specs=[SEM, SEM] + [HBM] * (2 * n) + [pl.BlockSpec(memory_space=pltpu.VMEM)],
        out_shape=[sem, sem] + [pltpu.HBM(p.shape, p.dtype) for p in parts]
        + [pltpu.HBM(z.shape, z.dtype) for z in lands] + [SDS((8, 128), F32)],
        input_output_aliases={i: 2 + i for i in range(2 * n)},
        compiler_params=pltpu.CompilerParams(has_side_effects=SIDE_EFFECT),
    )(*[_hbm(p) for p in parts], *[_hbm(z) for z in lands])
    return outs[0], outs[1], outs[2:2 + n], outs[2 + n:2 + 2 * n], outs[-1]


def _rs_to_owner_wait(l, send_sems, recv_sems, parts, lands, after):
    n = len(parts)

    def body(*refs):
        ins, zones = refs[:n], refs[n:2 * n]
        ssem, rsem = refs[2 * n], refs[2 * n + 1]
        x, y, c = _place()
        for a in range(n):
            for k, (cx, cy) in enumerate(_other_chips(x, y)):
                cp = _remote(ins[a].at[2 * cx + cy], zones[a].at[k], ssem.at[3 * a + k], rsem.at[3 * a + k],
                             (cx, cy, c))
                cp.wait_send()
                cp.wait_recv()

    outs = pl.pallas_call(
        body, name=f"rs_to_owner_wait_{l}", in_specs=[HBM] * (2 * n) + [SEM, SEM] + [ANY] * len(after),
        out_specs=[HBM] * (2 * n),
        out_shape=[pltpu.HBM(p.shape, p.dtype) for p in parts] + [pltpu.HBM(z.shape, z.dtype) for z in lands],
        input_output_aliases={i: i for i in range(2 * n)},
        compiler_params=pltpu.CompilerParams(has_side_effects=SIDE_EFFECT),
    )(*parts, *lands, send_sems, recv_sems, *after)
    return outs[:n], outs[n:]


def _rs_sibling_exchange(l, both):
    n = len(both)

    def body(*refs):
        ins = refs[:n]
        send_sems, recv_sems = refs[2 * n:]
        x, y, c = _place()
        copies = []
        for a in range(n):
            cp = _remote(ins[a].at[c], ins[a].at[c], send_sems.at[a], recv_sems.at[a], (x, y, 1 - c))
            cp.start()
            copies.append(cp)
        for a, cp in enumerate(copies):
            cp.wait_send()
            _remote(ins[a].at[1 - c], ins[a].at[1 - c], send_sems.at[a], recv_sems.at[a], (x, y, 1 - c)).wait_recv()

    sem = pltpu.SemaphoreType.DMA((n,))
    return pl.pallas_call(
        body, name=f"rs_sibling_exchange_{l}", in_specs=[ANY] * n, out_specs=[ANY] * n,
        out_shape=[SDS(b.shape, b.dtype) for b in both], input_output_aliases={i: i for i in range(n)},
        scratch_shapes=[sem, sem])(*both)


def _add_owner(name, grad, recv, place):
    r, cols = grad.shape
    tr = _row_tile(r, cols, budget=1024 * 1024)
    nt = r // tr

    def body(place_ref, g_ref, r_ref, o_ref):
        acc = ((g_ref[...] + r_ref[0].astype(F32)) + r_ref[1].astype(F32)) + r_ref[2].astype(F32)
        o_ref[...] = acc.astype(o_ref.dtype)

    return pl.pallas_call(
        body, name=name,
        grid_spec=pltpu.PrefetchScalarGridSpec(
            num_scalar_prefetch=1, grid=(nt,),
            in_specs=[BS((tr, cols), lambda t, pr: (t, 0)), BS((3, tr, cols), lambda t, pr: (0, t, 0))],
            out_specs=BS((None, tr, cols), lambda t, pr: (pr[1], t, 0))),
        out_shape=SDS((2, r, cols), WIRE_DTYPE))(place, grad, recv)


def _add_owner_group(tag, grads, recvs, place, steps):
    n = len(grads)

    def body(place_ref, *refs):
        gs, rs, outs = refs[:n], refs[n:2 * n], refs[2 * n:]
        for g_ref, r_ref, o_ref in zip(gs, rs, outs):
            acc = ((g_ref[...] + r_ref[0].astype(F32)) + r_ref[1].astype(F32)) + r_ref[2].astype(F32)
            o_ref[...] = acc.astype(o_ref.dtype)

    rows = [g.shape[0] // steps for g in grads]
    return pl.pallas_call(
        body, name=f"rs_add_owner_{tag}",
        grid_spec=pltpu.PrefetchScalarGridSpec(
            num_scalar_prefetch=1, grid=(steps,),
            in_specs=[BS((r, g.shape[1]), lambda t, pr: (t, 0)) for g, r in zip(grads, rows)]
            + [BS((3, r, g.shape[1]), lambda t, pr: (0, t, 0)) for g, r in zip(grads, rows)],
            out_specs=[BS((None, r, g.shape[1]), lambda t, pr: (pr[1], t, 0)) for g, r in zip(grads, rows)]),
        out_shape=[SDS((2,) + g.shape, WIRE_DTYPE) for g in grads], compiler_params=_params(),
    )(place, *grads, *recvs)


def _reduce_start(tag, grads):
    names = list(grads)
    send_sems, recv_sems, wires, lands, token = _rs_to_owner(tag, [grads[n][1] for n in names])
    return dict(tag=tag, names=names, send_sems=send_sems, recv_sems=recv_sems, wires=wires, lands=lands,
                grads=[grads[n][0] for n in names]), token


def _reduce_finish(tag, groups, place, after):
    all_names, all_mine = [], []
    for pending in groups:
        sub, names = pending["tag"], pending["names"]
        _, lands = _rs_to_owner_wait(sub, pending["send_sems"], pending["recv_sems"], pending["wires"],
                                     pending["lands"], after)
        if len(names) > 1:
            small = max(g.size for g in pending["grads"]) <= SMALL_GRAD_ELEMS
            mine = _add_owner_group(sub, pending["grads"], lands, place, 1 if small else ADAMW_GROUP_STEPS)
        else:
            mine = [_add_owner(f"rs_add_owner_{n}_{sub}", g, r, place)
                    for n, g, r in zip(names, pending["grads"], lands)]
        all_names += names
        all_mine += mine
    return dict(zip(all_names, _rs_sibling_exchange(tag, all_mine)))


def _small_peers(x, y, c):
    return [(x, y, 1 - c)] + [(cx, cy, c) for cx, cy in _other_chips(x, y)]


def _allgather_rows_start(tag, bufs):
    n = len(bufs)
    lands = [lax.empty((8,) + b.shape, b.dtype) for b in bufs]

    def body(*refs):
        ins, zones = refs[:n], refs[n:2 * n]
        send_sems, recv_sems = refs[2 * n], refs[2 * n + 1]
        token = refs[-1]
        x, y, c = _place()
        for a in range(n):
            for i, peer in enumerate(_small_peers(x, y, c)):
                _remote(ins[a], zones[a].at[4 * x + 2 * y + c], send_sems.at[4 * a + i], recv_sems.at[4 * a + i],
                        peer).start()
        token[...] = jnp.zeros(token.shape, F32)

    sem = pltpu.SemaphoreType.DMA((4 * n,))
    outs = pl.pallas_call(
        body, name=f"allgather_small_start_{tag}", in_specs=[HBM] * (2 * n),
        out_specs=[SEM, SEM] + [HBM] * (2 * n) + [pl.BlockSpec(memory_space=pltpu.VMEM)],
        out_shape=[sem, sem] + [pltpu.HBM(b.shape, b.dtype) for b in bufs]
        + [pltpu.HBM(z.shape, z.dtype) for z in lands] + [SDS((8, 128), F32)],
        input_output_aliases={i: 2 + i for i in range(2 * n)},
        compiler_params=pltpu.CompilerParams(has_side_effects=SIDE_EFFECT),
    )(*[_hbm(b) for b in bufs], *[_hbm(z) for z in lands])
    return outs[0], outs[1], outs[2:2 + n], outs[2 + n:2 + 2 * n], outs[-1]


def _allgather_rows_wait(tag, send_sems, recv_sems, bufs, lands, after):
    n = len(bufs)

    def body(*refs):
        ins, zones = refs[:n], refs[n:2 * n]
        ssem, rsem = refs[2 * n], refs[2 * n + 1]
        x, y, c = _place()
        for a in range(n):
            for i, (px, py, pc) in enumerate(_small_peers(x, y, c)):
                cp = _remote(ins[a], zones[a].at[4 * px + 2 * py + pc], ssem.at[4 * a + i], rsem.at[4 * a + i],
                             (px, py, pc))
                cp.wait_send()
                cp.wait_recv()

    outs = pl.pallas_call(
        body, name=f"allgather_small_wait_{tag}", in_specs=[HBM] * (2 * n) + [SEM, SEM, ANY],
        out_specs=[HBM] * (2 * n),
        out_shape=[pltpu.HBM(b.shape, b.dtype) for b in bufs] + [pltpu.HBM(z.shape, z.dtype) for z in lands],
        input_output_aliases={i: i for i in range(2 * n)},
        compiler_params=pltpu.CompilerParams(has_side_effects=SIDE_EFFECT),
    )(*bufs, *lands, send_sems, recv_sems, after)
    return outs[:n], outs[n:]


def _allgather_rows_forward(tag, lands):
    n = len(lands)

    def body(*refs):
        ins = refs[:n]
        send_sems, recv_sems = refs[2 * n:]
        x, y, c = _place()
        sibling = (x, y, 1 - c)
        copies = []
        for a in range(n):
            for k, (cx, cy) in enumerate(_other_chips(x, y)):
                blk = ins[a].at[4 * cx + 2 * cy + c]
                cp = _remote(blk, blk, send_sems.at[a, k], recv_sems.at[a, k], sibling)
                cp.start()
                copies.append(cp)
        for a in range(n):
            for k, (cx, cy) in enumerate(_other_chips(x, y)):
                blk = ins[a].at[4 * cx + 2 * cy + 1 - c]
                _remote(blk, blk, send_sems.at[a, k], recv_sems.at[a, k], sibling).wait_recv()
        for cp in copies:
            cp.wait_send()

    sem = pltpu.SemaphoreType.DMA((n, 3))
    return pl.pallas_call(body, name=f"allgather_small_forward_{tag}", in_specs=[ANY] * n, out_specs=[ANY] * n,
                          out_shape=[SDS(z.shape, z.dtype) for z in lands],
                          input_output_aliases={i: i for i in range(n)}, scratch_shapes=[sem, sem])(*lands)


def _sum_devices(tag, gathered, mine, place):
    _, r, cols = gathered.shape
    tr = _row_tile(r, cols, budget=256 * 1024)

    def body(place_ref, g_ref, x_ref, o_ref):
        me = 2 * place_ref[0] + place_ref[1]
        acc = jnp.where(me == 0, x_ref[...], g_ref[0])
        for k in range(1, 8):
            acc = acc + jnp.where(me == k, x_ref[...], g_ref[k])
        o_ref[...] = acc

    return pl.pallas_call(
        body, name=f"sum_small_grads_{tag}",
        grid_spec=pltpu.PrefetchScalarGridSpec(
            num_scalar_prefetch=1, grid=(r // tr,),
            in_specs=[BS((8, tr, cols), lambda t, pr: (0, t, 0)), BS((tr, cols), lambda t, pr: (t, 0))],
            out_specs=BS((tr, cols), lambda t, pr: (t, 0))),
        out_shape=SDS((r, cols), F32))(place, gathered, mine)


def _adamw_values(w, g, m, v):
    m = ADAM_B1 * m + (1.0 - ADAM_B1) * g
    v = ADAM_B2 * v + (1.0 - ADAM_B2) * (g * g)
    m_hat = m / (1.0 - ADAM_B1 ** ADAM_STEP)
    v_hat = v / (1.0 - ADAM_B2 ** ADAM_STEP)
    delta = -ADAM_LR * (m_hat / (jnp.sqrt(v_hat) + ADAM_EPS) + ADAM_WD * w)
    return delta, m, v


def _adamw_big(name, l, w, m, v, g, earlier=None, after=()):
    nl, r, cols = w.shape
    tr = _row_tile(r, cols, budget=1024 * 1024)
    nt = r // tr
    n_prev = 0 if earlier is None else 4

    def body(*refs):
        w_ref, m_ref, v_ref, g_ref = refs[:4]
        go_ref, d_ref, mo_ref, vo_ref = refs[4 + n_prev + len(after):]
        gv = g_ref[0].astype(F32) + g_ref[1].astype(F32)
        delta, m_new, v_new = _adamw_values(w_ref[...], gv, m_ref[...], v_ref[...])
        go_ref[...] = gv
        d_ref[...] = delta
        mo_ref[...] = m_new
        vo_ref[...] = v_new

    layer = BS((None, tr, cols), lambda t: (l, t, 0))
    return pl.pallas_call(
        body, name=f"adamw_{name}_l{l}", grid=(nt,),
        in_specs=[layer, layer, layer, BS((2, tr, cols), lambda t: (0, t, 0))] + [ANY] * (n_prev + len(after)),
        out_specs=[layer] * 4, out_shape=[SDS(w.shape, F32)] * 4,
        input_output_aliases={4 + i: i for i in range(n_prev)}, compiler_params=_params(),
    )(w, m, v, g, *(earlier or ()), *after)


def _adamw_small_group(tag, l, ws, ms, vs, gs, earlier, after=()):
    n = len(ws)
    steps = ADAMW_GROUP_STEPS
    prev = [a for e in earlier if e is not None for a in e]
    n_prev = len(prev)
    assert n_prev in (0, 4 * n)

    def body(*refs):
        w_refs, m_refs, v_refs, g_refs = refs[:n], refs[n:2 * n], refs[2 * n:3 * n], refs[3 * n:4 * n]
        outs = refs[4 * n + n_prev + len(after):]
        for i in range(n):
            gv = g_refs[i][0].astype(F32) + g_refs[i][1].astype(F32)
            delta, m_new, v_new = _adamw_values(w_refs[i][...], gv, m_refs[i][...], v_refs[i][...])
            for ref, val in zip(outs[4 * i:4 * i + 4], (gv, delta, m_new, v_new)):
                ref[...] = val

    def layer(w):
        return BS((None, w.shape[1] // steps, w.shape[2]), lambda t: (l, t, 0))

    return pl.pallas_call(
        body, name=f"adamw_{tag}_l{l}", grid=(steps,),
        in_specs=[layer(w) for w in ws] * 3
        + [BS((2, w.shape[1] // steps, w.shape[2]), lambda t: (0, t, 0)) for w in ws] + [ANY] * (n_prev + len(after)),
        out_specs=[layer(w) for w in ws for _ in range(4)],
        out_shape=[SDS(w.shape, F32) for w in ws for _ in range(4)],
        input_output_aliases={4 * n + i: i for i in range(n_prev)}, compiler_params=_params(),
    )(*ws, *ms, *vs, *gs, *prev, *after)


def _adamw_mid(ws, ms, vs, gathered, mine, place):
    n = len(ws)
    shape = ws[0].shape[1:]
    zeros = (0,) * len(shape)

    def body(place_ref, *refs):
        w_refs, m_refs, v_refs = refs[:n], refs[n:2 * n], refs[2 * n:3 * n]
        gath, own = refs[3 * n:(3 + N_LAYERS) * n], refs[(3 + N_LAYERS) * n:(3 + 2 * N_LAYERS) * n]
        outs = refs[(3 + 2 * N_LAYERS) * n:]
        me = 2 * place_ref[0] + place_ref[1]
        for i in range(n):
            gv = None
            for l in range(N_LAYERS):
                g_ref, x_ref = gath[l * n + i], own[l * n + i]
                acc = jnp.where(me == 0, x_ref[...], g_ref[0])
                for k in range(1, 8):
                    acc = acc + jnp.where(me == k, x_ref[...], g_ref[k])
                gv = acc if gv is None else jnp.where(pl.program_id(0) == l, acc, gv)
            delta, m_new, v_new = _adamw_values(w_refs[i][...], gv, m_refs[i][...], v_refs[i][...])
            for ref, val in zip(outs[4 * i:4 * i + 4], (gv, delta, m_new, v_new)):
                ref[...] = val

    layer = BS((None,) + shape, lambda l, pr: (l,) + zeros)
    kept = pl.Buffered(1)
    outs = pl.pallas_call(
        body, name="adamw_replicated_matrices",
        grid_spec=pltpu.PrefetchScalarGridSpec(
            num_scalar_prefetch=1, grid=(N_LAYERS,),
            in_specs=[layer] * (3 * n)
            + [BS((8,) + shape, lambda l, pr: (0,) + zeros, pipeline_mode=kept)] * (N_LAYERS * n)
            + [BS(shape, lambda l, pr: zeros, pipeline_mode=kept)] * (N_LAYERS * n),
            out_specs=[layer] * (4 * n)),
        out_shape=[SDS(ws[0].shape, F32)] * (4 * n), compiler_params=_params(),
    )(place, *ws, *ms, *vs, *[g for l in range(N_LAYERS) for g in gathered[l]],
      *[x for l in range(N_LAYERS) for x in mine[l]])
    return [tuple(outs[4 * i:4 * i + 4]) for i in range(n)]


def _adamw_rows(w, m, v, g):
    r, cols = w.shape
    tr = _row_tile(r, cols, budget=512 * 1024)

    def body(w_ref, m_ref, v_ref, g_ref, d_ref, mo_ref, vo_ref):
        delta, m_new, v_new = _adamw_values(w_ref[...], g_ref[...], m_ref[...], v_ref[...])
        d_ref[...] = delta
        mo_ref[...] = m_new
        vo_ref[...] = v_new

    spec = BS((tr, cols), lambda t: (t, 0))
    return pl.pallas_call(body, name="adamw_small", grid=(r // tr,), in_specs=[spec] * 4, out_specs=[spec] * 3,
                          out_shape=[SDS(w.shape, F32)] * 3)(w, m, v, g)


SMALL_GRAD_ELEMS = 256 * 1024
ADAMW_GROUP_STEPS = 4
PACK_ALIGN = 8 * 128
PACK_ROWS = 128


def _pack_rows(arrays):
    parts, rows = [], 0
    for a in arrays:
        flat = a.reshape(-1)
        pad = (-flat.shape[0]) % PACK_ALIGN
        if pad:
            flat = jnp.pad(flat, (0, pad))
        parts.append(flat.reshape(-1, 128))
        rows += parts[-1].shape[0]
    if rows % PACK_ROWS:
        parts.append(jnp.zeros((PACK_ROWS - rows % PACK_ROWS, 128), parts[0].dtype))
    return jnp.concatenate(parts, axis=0)


def _unpack_rows(buf, shapes):
    out, row = [], 0
    for shape in shapes:
        size = math.prod(shape)
        rows = -(-size // PACK_ALIGN) * (PACK_ALIGN // 128)
        out.append(buf[row:row + rows].reshape(-1)[:size].reshape(shape))
        row += rows
    return out


def kernel(x, norm1, w_in, b_gate, ssm_a_re, ssm_a_im, ssm_log_dt, ssm_b_re, ssm_b_im, ssm_c_re, ssm_c_im, ssm_d, ssm_w_glu, ssm_b_glu, ssm_w_proj, conv_w_dw, conv_b_dw, conv_ln_g, conv_ln_b, conv_w_proj, pool_w_group, pool_scale, pool_w_proj, w_out, norm2, ffn_w_gate, ffn_w_up, ffn_w_down, final_norm, loss_target, m_norm1, m_w_in, m_b_gate, m_ssm_a_re, m_ssm_a_im, m_ssm_log_dt, m_ssm_b_re, m_ssm_b_im, m_ssm_c_re, m_ssm_c_im, m_ssm_d, m_ssm_w_glu, m_ssm_b_glu, m_ssm_w_proj, m_conv_w_dw, m_conv_b_dw, m_conv_ln_g, m_conv_ln_b, m_conv_w_proj, m_pool_w_group, m_pool_scale, m_pool_w_proj, m_w_out, m_norm2, m_ffn_w_gate, m_ffn_w_up, m_ffn_w_down, m_final_norm, v_norm1, v_w_in, v_b_gate, v_ssm_a_re, v_ssm_a_im, v_ssm_log_dt, v_ssm_b_re, v_ssm_b_im, v_ssm_c_re, v_ssm_c_im, v_ssm_d, v_ssm_w_glu, v_ssm_b_glu, v_ssm_w_proj, v_conv_w_dw, v_conv_b_dw, v_conv_ln_g, v_conv_ln_b, v_conv_w_proj, v_pool_w_group, v_pool_scale, v_pool_w_proj, v_w_out, v_norm2, v_ffn_w_gate, v_ffn_w_up, v_ffn_w_down, v_final_norm):
    given = dict(locals())
    cx, cy, cc = _place()
    place = jnp.stack([2 * cx + cy, cc]).astype(jnp.int32)

    def kernel_view(n, a):
        if n in TRANSPOSED:
            return a.transpose(0, 2, 1)
        return a.transpose(0, 1, 3, 2) if n in ("ssm_b_re", "ssm_b_im") else a

    prm = {n: given[n] for n in WEIGHTS}
    mom = {n: given["m_" + n] for n in WEIGHTS}
    var = {n: given["v_" + n] for n in WEIGHTS}
    for n in MID:
        prm[n], mom[n], var[n] = kernel_view(n, prm[n]), kernel_view(n, mom[n]), kernel_view(n, var[n])

    dw_shard = prm["conv_w_dw"].reshape(N_LAYERS, CONV_KERNEL, -1)
    casts = {"w_in": _cast_into("w_in", prm["w_in"], place, MXU_DTYPE)}
    first, first_started = _allgather_start("first", [[casts["w_in"][0]]])
    in_flight = {(0, "in"): first[0]}
    mixer = GATHER_GROUPS["mixer"]
    casts.update(zip(mixer, _cast_small_into(
        "mixer", [dw_shard if n == "conv_w_dw" else prm[n] for n in mixer],
        [F32 if n == "conv_w_dw" else MXU_DTYPE for n in mixer], place, after=(first_started,))))
    casts.update({n: _cast_into(n, kernel_view(n, prm[n]), place, MXU_DTYPE, after=(first_started,))
                  for n in GATHER_GROUPS["ffn"]})
    order = [(l, g) for l in range(N_LAYERS) for g in GATHER_GROUPS if (l, g) != (0, "in")]
    rest, rest_started = _allgather_start("rest", [[casts[n][l] for n in GATHER_GROUPS[g]] for l, g in order])
    in_flight.update(zip(order, rest))

    arrived = {}

    def weights_of(l, group, after):
        if (l, group) in arrived:
            return arrived.pop((l, group))
        tag = f"l{l}_{group}"
        if (l, group) == (0, "in"):
            after = after + (rest_started,)
        groups = (group, "mixer") if (l > 0 and group == "in") else (group,)
        waited = [_allgather_wait(f"l{l}_{g}", *in_flight[l, g][:2], in_flight[l, g][2], after) for g in groups]
        bufs = _allgather_forward(tag, [b for w in waited for b in w])
        for g in groups:
            fw = dict(zip(GATHER_GROUPS[g], bufs[:len(GATHER_GROUPS[g])]))
            bufs = bufs[len(GATHER_GROUPS[g]):]
            if "conv_w_dw" in fw:
                fw["conv_w_dw"] = fw["conv_w_dw"].transpose(1, 0, 2).reshape(CONV_KERNEL, -1)
            arrived[l, g] = fw
        return arrived.pop((l, group))

    pending, small_pending, small_shapes = {}, {}, {}
    tokens = {}

    def on_grads(l, group, grads):
        if group == "small":
            packed = {n: g for n, g in grads.items() if n not in MID}
            small_shapes[l] = {n: g.shape for n, g in packed.items()}
            begun = _allgather_rows_start(f"l{l}", [_pack_rows(list(packed.values()))] + [grads[n] for n in MID])
            small_pending[l], token = begun[:4], begun[4]
        else:
            pending[l, group], token = _reduce_start(f"{l}_{group}", grads)
        tokens[l, group] = token
        return token[0, 0]

    loss, dx, _, _, _ = _local_step(x[0], loss_target[0], weights_of, prm, place, on_grads)
    loss = lax.psum(loss, ("x", "y", "c"))

    reduced = [{} for _ in range(N_LAYERS)]
    out = {}

    def finish(l, groups, after):
        reduced[l].update(_reduce_finish(f"l{l}_{groups[0]}", [pending[l, g] for g in groups], place, after))

    def adamw(l, names, done):
        for group in ("in", "ffn", "mixer"):
            members = [n for n in names if n in GATHER_GROUPS[group]]
            if len(members) == 1:
                n = members[0]
                out[n] = _adamw_big(n, l, kernel_view(n, prm[n]), kernel_view(n, mom[n]), kernel_view(n, var[n]),
                                    reduced[l][n], out.get(n), after=done)
                done = (out[n][0],)
            elif members:
                res = _adamw_small_group(group, l, [kernel_view(n, prm[n]) for n in members],
                                         [kernel_view(n, mom[n]) for n in members],
                                         [kernel_view(n, var[n]) for n in members],
                                         [reduced[l][n] for n in members], [out.get(n) for n in members], after=done)
                for i, n in enumerate(members):
                    out[n] = tuple(res[4 * i:4 * i + 4])
                done = (res[0],)
        return done

    top = N_LAYERS - 1
    done = (tokens[0, "in"], tokens[0, "small"])
    finish(top, ("ffn", "mixer", "in"), done)
    done = adamw(top, BIG, done)
    for groups in (("ffn", "mixer"), ("in",)):
        finish(0, groups, done)
        done = adamw(0, [n for g in groups for n in GATHER_GROUPS[g] if n in BIG], done)
    for n in BIG:
        out[n] = tuple(kernel_view(n, a) for a in out[n])

    gsmall = {}
    mid_mine, mid_gathered = [], []
    for l in range(N_LAYERS):
        mine, lands = _allgather_rows_wait(f"l{l}", *small_pending[l], done[0])
        lands = _allgather_rows_forward(f"l{l}", lands)
        mid_mine.append(mine[1:])
        mid_gathered.append(lands[1:])
        gsum = _sum_devices(f"l{l}", lands[0], mine[0], place)
        for n, g in zip(small_shapes[l], _unpack_rows(gsum, list(small_shapes[l].values()))):
            gsmall.setdefault(n, [None] * N_LAYERS)[l] = g
    mid_out = _adamw_mid([prm[n] for n in MID], [mom[n] for n in MID], [var[n] for n in MID], mid_gathered, mid_mine,
                         place)
    for n, res in zip(MID, mid_out):
        out[n] = tuple(kernel_view(n, a) for a in res)
    gsmall = {n: (g[top] if n == "final_norm" else jnp.stack(g)) for n, g in gsmall.items()}
    lanes = dw_shard.shape[-1]
    gsmall["conv_w_dw"] = lax.dynamic_slice_in_dim(gsmall["conv_w_dw"], (2 * cx + cy) * lanes, lanes, axis=2)
    small_names = [n for n in SMALL if n not in MID] + ["conv_w_dw"]
    w_rows = _pack_rows([prm[n] for n in small_names])
    m_rows = _pack_rows([mom[n] for n in small_names])
    v_rows = _pack_rows([var[n] for n in small_names])
    g_rows = _pack_rows([gsmall[n] for n in small_names])
    shapes = [prm[n].shape for n in small_names]
    d_s, m_s, v_s = (_unpack_rows(r, shapes) for r in _adamw_rows(w_rows, m_rows, v_rows, g_rows))
    for i, n in enumerate(small_names):
        out[n] = (gsmall[n].reshape(prm[n].shape), d_s[i], m_s[i], v_s[i])
    grads = [out[n][0] for n in WEIGHTS]
    deltas = [out[n][1] for n in WEIGHTS]
    new_m = [out[n][2] for n in WEIGHTS]
    new_v = [out[n][3] for n in WEIGHTS]
    return (loss, dx[None], *grads, *deltas, *new_m, *new_v)
```

```python
import math

import jax
import jax.numpy as jnp
from jax import lax
from jax.experimental import pallas as pl
from jax.experimental.pallas import tpu as pltpu

F32 = jnp.float32
MXU_DTYPE = jnp.bfloat16
WIRE_DTYPE = jnp.bfloat16
SDS = jax.ShapeDtypeStruct
BS = pl.BlockSpec
ANY = pl.BlockSpec(memory_space=pl.ANY)
HBM = pl.BlockSpec(memory_space=pltpu.HBM)
SEM = pl.BlockSpec(memory_space=pltpu.SEMAPHORE)
SIDE_EFFECT = pltpu.SideEffectType.DATAFLOW_SIDE_EFFECTING
MESH = pl.DeviceIdType.MESH

EPS = 1e-6
N_CHIPS = 4
N_LAYERS = 2
SSM_GROUPS, SSM_STATE, SSM_GROUP = 32, 64, 16
CONV_KERNEL = 31
CONV_PAD = 32
POOL_WINDOWS = (2, 4, 8, 16)
GELU_C = math.sqrt(2.0 / math.pi)
ADAM_LR, ADAM_B1, ADAM_B2, ADAM_EPS, ADAM_WD, ADAM_STEP = 0.001, 0.9, 0.999, 1e-08, 0.01, 10
VMEM_LIMIT = 56 * 1024 * 1024

BIG = ("w_in", "ssm_w_glu", "ssm_w_proj", "conv_w_proj", "pool_w_proj", "w_out", "ffn_w_gate", "ffn_w_up", "ffn_w_down")
TRANSPOSED = ("ffn_w_gate", "ffn_w_up")
MID = ("ssm_b_re", "ssm_b_im", "ssm_c_re", "ssm_c_im")
GATHER_GROUPS = {
    "in": ("w_in",),
    "mixer": ("ssm_w_glu", "ssm_w_proj", "conv_w_proj", "pool_w_proj", "w_out", "conv_w_dw"),
    "ffn": ("ffn_w_gate", "ffn_w_up", "ffn_w_down"),
}
SMALL = ("norm1", "b_gate", "ssm_a_re", "ssm_a_im", "ssm_log_dt", "ssm_b_re", "ssm_b_im", "ssm_c_re", "ssm_c_im",
         "ssm_d", "ssm_b_glu", "conv_b_dw", "conv_ln_g", "conv_ln_b", "pool_w_group", "pool_scale", "norm2",
         "final_norm")
WEIGHTS = ("norm1", "w_in", "b_gate", "ssm_a_re", "ssm_a_im", "ssm_log_dt", "ssm_b_re", "ssm_b_im", "ssm_c_re",
           "ssm_c_im", "ssm_d", "ssm_w_glu", "ssm_b_glu", "ssm_w_proj", "conv_w_dw", "conv_b_dw", "conv_ln_g",
           "conv_ln_b", "conv_w_proj", "pool_w_group", "pool_scale", "pool_w_proj", "w_out", "norm2", "ffn_w_gate",
           "ffn_w_up", "ffn_w_down", "final_norm")


def _params():
    return pltpu.CompilerParams(vmem_limit_bytes=VMEM_LIMIT)


def _mm(a, b):
    return jnp.dot(a.astype(MXU_DTYPE), b.astype(MXU_DTYPE), preferred_element_type=F32)


def _mm_nt(a, b):
    return lax.dot_general(a.astype(MXU_DTYPE), b.astype(MXU_DTYPE), (((1,), (1,)), ((), ())),
                           preferred_element_type=F32)


def _mm_tn(a, b):
    return lax.dot_general(a.astype(MXU_DTYPE), b.astype(MXU_DTYPE), (((0,), (0,)), ((), ())),
                           preferred_element_type=F32)


def _sigmoid(x):
    return jax.nn.sigmoid(x)


def _gelu(x):
    t = jnp.tanh(GELU_C * (x + 0.044715 * (x * x * x)))
    return x * (0.5 * (1.0 + t)), t


def _gelu_grad(x, t):
    return 0.5 * (1.0 + t) + 0.5 * x * (1.0 - t * t) * (GELU_C * (1.0 + 3.0 * 0.044715 * x * x))


def _colsum(v):
    return jnp.sum(v, axis=0, keepdims=True)


def _row_tile(rows, cols, itemsize=4, budget=1536 * 1024):
    best = None
    for t in range(8, rows + 1, 8):
        if rows % t == 0 and t * cols * itemsize <= budget:
            best = t
    return best if best is not None else rows


def _in_proj(l, x, norm1, w_in):
    s, d = x.shape
    nc = w_in.shape[-1]
    tm = min(1024, s)
    nt = s // tm

    def body(x_ref, g_ref, w_ref, z_ref, h_ref, h_all):
        i = pl.program_id(1)
        rows = pl.ds(pl.multiple_of(i * tm, tm), tm)

        @pl.when(pl.program_id(0) == 0)
        def _():
            xv = x_ref[...]
            r = lax.rsqrt(jnp.mean(xv * xv, axis=-1, keepdims=True) + EPS)
            hv = (xv * r * g_ref[...]).astype(h_ref.dtype)
            h_ref[...] = hv.T
            h_all[rows, :] = hv

        z_ref[...] = _mm(h_all[rows, :], w_ref[...])

    tile_of = lambda j, i: i * (1 - jnp.minimum(j, 1)) + (nt - 1) * jnp.minimum(j, 1)
    return pl.pallas_call(
        body, name=f"in_proj_l{l}", grid=(N_CHIPS, nt),
        in_specs=[BS((tm, d), lambda j, i: (tile_of(j, i), 0)), BS((None, 1, d), lambda j, i: (l, 0, 0)),
                  BS((None, d, nc), lambda j, i: (j, 0, 0))],
        out_specs=[BS((tm, nc), lambda j, i: (i, j)), BS((d, tm), lambda j, i: (0, tile_of(j, i)))],
        out_shape=[SDS((s, N_CHIPS * nc), F32), SDS((d, s), MXU_DTYPE)],
        scratch_shapes=[pltpu.VMEM((s, d), MXU_DTYPE)], compiler_params=_params())(x, norm1, w_in)


def _mm_cols(a, w_ref):
    return jnp.concatenate([_mm(a, w_ref[j]) for j in range(N_CHIPS)], axis=1)


def _mm_nt_cols(dv, w_ref):
    nc = w_ref.shape[-1]
    acc = _mm_nt(dv[:, 0:nc], w_ref[0])
    for j in range(1, N_CHIPS):
        acc = acc + _mm_nt(dv[:, j * nc:(j + 1) * nc], w_ref[j])
    return acc


def _merge_values(y, hc, p, zg, wglu, bglu, wpa, wpb, wpc, lng, lnb, wgrp, scale, bg):
    v = {}
    ge, th = _gelu(y)
    t = _mm(ge, wglu) + bglu
    sg = _sigmoid(t)
    sa = ge * sg
    ya = _mm_cols(sa, wpa)
    mu = jnp.mean(hc, axis=-1, keepdims=True)
    xc = hc - mu
    r = lax.rsqrt(jnp.mean(xc * xc, axis=-1, keepdims=True) + EPS)
    xh = xc * r
    ln = xh * lng + lnb
    sl = _sigmoid(ln)
    ac = ln * sl
    yb = _mm_cols(ac, wpb)
    gw = p.shape[1] // len(POOL_WINDOWS)
    q = jnp.concatenate([_mm(p[:, k * gw:(k + 1) * gw], wgrp[k]) for k in range(len(POOL_WINDOWS))], axis=1)
    pp = q * scale
    yc = _mm_cols(pp, wpc)
    d = ya.shape[1]
    gates = [_sigmoid(zg[k] + bg[:, k * d:(k + 1) * d]) for k in range(3)]
    merged = gates[0] * ya + gates[1] * yb + gates[2] * yc
    v.update(ge=ge, th=th, sg=sg, sa=sa, ya=ya, r=r, xh=xh, ln=ln, sl=sl, ac=ac, yb=yb, q=q, pp=pp, yc=yc,
             gates=gates, merged=merged)
    return v


def _merge_specs(l, tm, d, cw):
    row = lambda n: BS((None, 1, n), lambda i: (l, 0, 0))
    resident = lambda shp: BS(shp, lambda i: (0, 0, 0), pipeline_mode=pl.Buffered(1))
    return [
        BS((tm, cw), lambda i: (i, 0)),
        BS((tm, cw), lambda i: (i, 0)),
        BS((tm, cw), lambda i: (i, 0)),
        BS((tm, d), lambda i: (i, 2)), BS((tm, d), lambda i: (i, 3)), BS((tm, d), lambda i: (i, 4)),
        resident((N_CHIPS, cw // N_CHIPS, cw)),
        row(cw),
        resident((N_CHIPS, cw, d // N_CHIPS)),
        resident((N_CHIPS, cw, d // N_CHIPS)),
        resident((N_CHIPS, cw, d // N_CHIPS)),
        row(cw), row(cw),
        BS((None, 4, cw // 4, cw // 4), lambda i: (l, 0, 0, 0)),
        row(cw),
        row(3 * d),
        resident((N_CHIPS, d // N_CHIPS, d)),
    ]


def _merge_fwd(l, x, y, hc, p, z, fw, sp):
    s, d = x.shape
    cw = y.shape[1]
    tm = min(512, s)

    def body(x_ref, y_ref, hc_ref, p_ref, z0, z1, z2, wglu, bglu, wpa, wpb, wpc, lng, lnb, wgrp, scale, bg, wout,
             x1_ref):
        v = _merge_values(y_ref[...], hc_ref[...], p_ref[...], (z0[...], z1[...], z2[...]),
                          wglu[...].reshape(cw, cw), bglu[...], wpa, wpb, wpc, lng[...], lnb[...], wgrp, scale[...],
                          bg[...])
        x1_ref[...] = x_ref[...] + _mm(v["merged"], wout[...].reshape(d, d))

    return pl.pallas_call(
        body, name=f"merge_fwd_l{l}", grid=(s // tm,),
        in_specs=[BS((tm, d), lambda i: (i, 0))] + _merge_specs(l, tm, d, cw),
        out_specs=BS((tm, d), lambda i: (i, 0)), out_shape=SDS((s, d), F32), compiler_params=_params(),
    )(x, y, hc, p, z, z, z, fw["ssm_w_glu"], sp["ssm_b_glu"], fw["ssm_w_proj"], fw["conv_w_proj"], fw["pool_w_proj"],
      sp["conv_ln_g"], sp["conv_ln_b"], sp["pool_w_group"], sp["pool_scale"], sp["b_gate"], fw["w_out"])


def _merge_bwd(l, dx1, y, hc, p, z, fw, sp):
    s, d = dx1.shape
    cw = y.shape[1]
    tm = min(256, s)
    m = MXU_DTYPE

    def body(dx1_ref, y_ref, hc_ref, p_ref, z0, z1, z2, wglu, bglu, wpa, wpb, wpc, lng, lnb, wgrp, scale, bg, wout,
             dzg_ref, dy_ref, dhc_ref, dp_ref, merged_ref, sa_ref, ac_ref, pp_ref, ge_ref, dt_ref, dya_ref, dyb_ref,
             dyc_ref, dq_ref, dbg_ref, dbglu_ref, dlng_ref, dlnb_ref, dscale_ref):
        yv = y_ref[...]
        wg = wglu[...].reshape(cw, cw)
        v = _merge_values(yv, hc_ref[...], p_ref[...], (z0[...], z1[...], z2[...]), wg, bglu[...], wpa, wpb, wpc,
                          lng[...], lnb[...], wgrp, scale[...], bg[...])
        dm = _mm_nt(dx1_ref[...], wout[...].reshape(d, d))
        ys = (v["ya"], v["yb"], v["yc"])
        dys, dbg = [], []
        for k in range(3):
            gk = v["gates"][k]
            dzk = dm * ys[k] * (gk * (1.0 - gk))
            dbg.append(_colsum(dzk))
            dzg_ref[:, k * d:(k + 1) * d] = dzk.astype(m)
            dys.append((dm * gk).astype(m))
        dsa = _mm_nt_cols(dys[0], wpa)
        dac = _mm_nt_cols(dys[1], wpb)
        dpp = _mm_nt_cols(dys[2], wpc)
        ge, sg = v["ge"], v["sg"]
        dt = dsa * ge * (sg * (1.0 - sg))
        dge = dsa * sg + _mm_nt(dt, wg)
        dy_ref[...] = dge * _gelu_grad(yv, v["th"])
        ln, sl, xh = v["ln"], v["sl"], v["xh"]
        dln = dac * (sl * (1.0 + ln * (1.0 - sl)))
        dxh = dln * lng[...]
        dhc_ref[...] = v["r"] * (dxh - jnp.mean(dxh, axis=-1, keepdims=True)
                                 - xh * jnp.mean(dxh * xh, axis=-1, keepdims=True))
        dq = dpp * scale[...]
        gw = cw // len(POOL_WINDOWS)
        for k in range(len(POOL_WINDOWS)):
            dp_ref[:, k * gw:(k + 1) * gw] = _mm_nt(dq[:, k * gw:(k + 1) * gw], wgrp[k])
        merged_ref[...] = v["merged"].astype(m)
        sa_ref[...] = v["sa"].astype(m)
        ac_ref[...] = v["ac"].astype(m)
        pp_ref[...] = v["pp"].astype(m)
        ge_ref[...] = ge.astype(m)
        dt_ref[...] = dt.astype(m)
        dya_ref[...] = dys[0]
        dyb_ref[...] = dys[1]
        dyc_ref[...] = dys[2]
        dq_ref[...] = dq.astype(m)

        @pl.when(pl.program_id(0) == 0)
        def _():
            for ref in (dbg_ref, dbglu_ref, dlng_ref, dlnb_ref, dscale_ref):
                ref[...] = jnp.zeros(ref.shape, F32)

        dbg_ref[...] += jnp.concatenate(dbg, axis=1)
        dbglu_ref[...] += _colsum(dt)
        dlng_ref[...] += _colsum(dln * xh)
        dlnb_ref[...] += _colsum(dln)
        dscale_ref[...] += _colsum(dpp * v["q"])

    tile = lambda n: BS((tm, n), lambda i: (i, 0))
    acc = lambda n: BS((1, n), lambda i: (0, 0))
    outs = pl.pallas_call(
        body, name=f"merge_bwd_l{l}", grid=(s // tm,),
        in_specs=[tile(d)] + _merge_specs(l, tm, d, cw),
        out_specs=[tile(3 * d), tile(cw), tile(cw), tile(cw), tile(d), tile(cw), tile(cw), tile(cw), tile(cw), tile(cw),
                   tile(d), tile(d), tile(d), tile(cw), acc(3 * d), acc(cw), acc(cw), acc(cw), acc(cw)],
        out_shape=[SDS((s, 3 * d), m), SDS((s, cw), F32), SDS((s, cw), F32), SDS((s, cw), F32), SDS((s, d), m),
                   SDS((s, cw), m), SDS((s, cw), m), SDS((s, cw), m), SDS((s, cw), m), SDS((s, cw), m), SDS((s, d), m),
                   SDS((s, d), m), SDS((s, d), m), SDS((s, cw), m), SDS((1, 3 * d), F32), SDS((1, cw), F32),
                   SDS((1, cw), F32), SDS((1, cw), F32), SDS((1, cw), F32)],
        compiler_params=_params(),
    )(dx1, y, hc, p, z, z, z, fw["ssm_w_glu"], sp["ssm_b_glu"], fw["ssm_w_proj"], fw["conv_w_proj"], fw["pool_w_proj"],
      sp["conv_ln_g"], sp["conv_ln_b"], sp["pool_w_group"], sp["pool_scale"], sp["b_gate"], fw["w_out"])
    names = ("dzg", "dy", "dhc", "dp", "merged", "sa", "ac", "pp", "ge", "dt", "dya", "dyb", "dyc", "dq", "db_gate",
             "db_glu", "dln_g", "dln_b", "dscale")
    return dict(zip(names, outs))


def _ffn_fwd(l, x1, norm2, wg, wu, wd):
    s, d = x1.shape
    hc = wd.shape[1]
    tm = min(1024, s)

    def body(x_ref, g_ref, wg_ref, wu_ref, wd_ref, o_ref, gate_ref, up_ref, h_ref):
        @pl.when(pl.program_id(1) == 0)
        def _():
            xv = x_ref[...]
            r = lax.rsqrt(jnp.mean(xv * xv, axis=-1, keepdims=True) + EPS)
            h_ref[...] = (xv * r * g_ref[...]).astype(h_ref.dtype)
            o_ref[...] = xv

        h = h_ref[...]
        gate = _mm_nt(h, wg_ref[...])
        up = _mm_nt(h, wu_ref[...])
        gate_ref[...] = gate
        up_ref[...] = up
        o_ref[...] += _mm(gate * _sigmoid(gate) * up, wd_ref[...])

    chunk = BS((None, tm, hc), lambda i, j: (j, i, 0))
    return pl.pallas_call(
        body, name=f"ffn_fwd_l{l}", grid=(s // tm, N_CHIPS),
        in_specs=[BS((tm, d), lambda i, j: (i, 0)), BS((None, 1, d), lambda i, j: (l, 0, 0)),
                  BS((None, hc, d), lambda i, j: (j, 0, 0)), BS((None, hc, d), lambda i, j: (j, 0, 0)),
                  BS((None, hc, d), lambda i, j: (j, 0, 0))],
        out_specs=[BS((tm, d), lambda i, j: (i, 0)), chunk, chunk, BS((tm, d), lambda i, j: (i, 0))],
        out_shape=[SDS((s, d), F32), SDS((N_CHIPS, s, hc), F32), SDS((N_CHIPS, s, hc), F32), SDS((s, d), MXU_DTYPE)],
        compiler_params=_params())(x1, norm2, wg, wu, wd)


def _ffn_bwd(l, x1, dx2, gate_pre, up_pre, norm2, wg, wu, wd):
    s, d = x1.shape
    hc = wd.shape[1]
    tm = min(512, s)
    m = MXU_DTYPE
    last = N_CHIPS - 1

    def body(x_ref, dx2_ref, gate_ref, up_ref, g_ref, wg_ref, wu_ref, wd_ref, dx1_ref, dxb_ref, act_ref, dgate_ref,
             dup_ref, dn_ref, dh_scr):
        i, j = pl.program_id(0), pl.program_id(1)

        @pl.when(j == 0)
        def _():
            dxb_ref[...] = dx2_ref[...].astype(m)
            dh_scr[...] = jnp.zeros(dh_scr.shape, F32)

        @pl.when((i == 0) & (j == 0))
        def _():
            dn_ref[...] = jnp.zeros(dn_ref.shape, F32)

        gate = gate_ref[...]
        up = up_ref[...]
        sg = _sigmoid(gate)
        silu = gate * sg
        act_ref[...] = (silu * up).astype(m).T
        dact = _mm_nt(dxb_ref[...], wd_ref[...])
        dup = (dact * silu).astype(m)
        dgate = (dact * up * (sg * (1.0 + gate * (1.0 - sg)))).astype(m)
        dup_ref[...] = dup.T
        dgate_ref[...] = dgate.T
        dh_scr[...] += _mm(dgate, wg_ref[...]) + _mm(dup, wu_ref[...])

        @pl.when(j == last)
        def _():
            xv = x_ref[...]
            r = lax.rsqrt(jnp.mean(xv * xv, axis=-1, keepdims=True) + EPS)
            xh = xv * r
            dh = dh_scr[...]
            dn_ref[...] += _colsum(dh * xh)
            dxh = dh * g_ref[...]
            dx1_ref[...] = dx2_ref[...] + r * (dxh - xh * jnp.mean(dxh * xh, axis=-1, keepdims=True))

    chunk = BS((None, hc, tm), lambda i, j: (j, 0, i))
    saved = BS((None, tm, hc), lambda i, j: (j, i, 0))
    outs = pl.pallas_call(
        body, name=f"ffn_bwd_l{l}", grid=(s // tm, N_CHIPS),
        in_specs=[BS((tm, d), lambda i, j: (i, 0)), BS((tm, d), lambda i, j: (i, 0)), saved, saved,
                  BS((None, 1, d), lambda i, j: (l, 0, 0)),
                  BS((None, hc, d), lambda i, j: (j, 0, 0)), BS((None, hc, d), lambda i, j: (j, 0, 0)),
                  BS((None, hc, d), lambda i, j: (j, 0, 0))],
        out_specs=[BS((tm, d), lambda i, j: (i, 0)), BS((tm, d), lambda i, j: (i, 0)),
                   chunk, chunk, chunk, BS((1, d), lambda i, j: (0, 0))],
        out_shape=[SDS((s, d), F32), SDS((s, d), m), SDS((N_CHIPS, hc, s), m),
                   SDS((N_CHIPS, hc, s), m), SDS((N_CHIPS, hc, s), m), SDS((1, d), F32)],
        scratch_shapes=[pltpu.VMEM((tm, d), F32)], compiler_params=_params(),
    )(x1, dx2, gate_pre, up_pre, norm2, wg, wu, wd)
    return dict(zip(("dx1", "dx2", "act", "dgate", "dup", "dnorm2"), outs))


def _loss_head(x, target, gf):
    s, d = x.shape
    tm = min(512, s)

    def body(x_ref, t_ref, g_ref, dx_ref, loss_ref, dg_ref):
        @pl.when(pl.program_id(0) == 0)
        def _():
            loss_ref[...] = jnp.zeros(loss_ref.shape, F32)
            dg_ref[...] = jnp.zeros(dg_ref.shape, F32)

        xv = x_ref[...]
        r = lax.rsqrt(jnp.mean(xv * xv, axis=-1, keepdims=True) + EPS)
        xh = xv * r
        err = xh * g_ref[...] - t_ref[...]
        loss_ref[...] += 0.5 * jnp.sum(jnp.mean(err * err, axis=-1, keepdims=True), axis=0, keepdims=True)
        dyv = err * (1.0 / d)
        dg_ref[...] += _colsum(dyv * xh)
        dxh = dyv * g_ref[...]
        dx_ref[...] = r * (dxh - xh * jnp.mean(dxh * xh, axis=-1, keepdims=True))

    return pl.pallas_call(
        body, name="loss_head", grid=(s // tm,),
        in_specs=[BS((tm, d), lambda i: (i, 0)), BS((tm, d), lambda i: (i, 0)), BS((1, d), lambda i: (0, 0))],
        out_specs=[BS((tm, d), lambda i: (i, 0)), BS((1, 1), lambda i: (0, 0)), BS((1, d), lambda i: (0, 0))],
        out_shape=[SDS((s, d), F32), SDS((1, 1), F32), SDS((1, d), F32)], compiler_params=_params())(x, target, gf)


def _in_proj_bwd(l, dres, x, norm1, w_in, du_a, dv1, dv2, du_c, dzg):
    s, d = x.shape
    nc = w_in.shape[-1]
    tm = min(256, s)
    m = MXU_DTYPE

    def body(dres_ref, x_ref, g_ref, w_ref, a_ref, b1_ref, b2_ref, c_ref, g3_ref, dx_ref, dz_ref, dn_ref):
        @pl.when(pl.program_id(0) == 0)
        def _():
            dn_ref[...] = jnp.zeros(dn_ref.shape, F32)

        dz = jnp.concatenate([a_ref[...], b1_ref[...], b2_ref[...], c_ref[...], g3_ref[...]], axis=1).astype(m)
        dz_ref[...] = dz
        dh = _mm_nt_cols(dz, w_ref)
        xv = x_ref[...]
        r = lax.rsqrt(jnp.mean(xv * xv, axis=-1, keepdims=True) + EPS)
        xh = xv * r
        dn_ref[...] += _colsum(dh * xh)
        dxh = dh * g_ref[...]
        dx_ref[...] = dres_ref[...] + r * (dxh - xh * jnp.mean(dxh * xh, axis=-1, keepdims=True))

    tile = lambda n: BS((tm, n), lambda i: (i, 0))
    return pl.pallas_call(
        body, name=f"in_proj_bwd_l{l}", grid=(s // tm,),
        in_specs=[tile(d), tile(d), BS((None, 1, d), lambda i: (l, 0, 0)),
                  BS((N_CHIPS, d, nc), lambda i: (0, 0, 0), pipeline_mode=pl.Buffered(1)),
                  tile(du_a.shape[1]), tile(dv1.shape[1]), tile(dv2.shape[1]), tile(du_c.shape[1]), tile(dzg.shape[1])],
        out_specs=[tile(d), tile(N_CHIPS * nc), BS((1, d), lambda i: (0, 0))],
        out_shape=[SDS((s, d), F32), SDS((s, N_CHIPS * nc), m), SDS((1, d), F32)], compiler_params=_params(),
    )(dres, x, norm1, w_in, du_a, dv1, dv2, du_c, dzg)


def _tn_matmul(name, a, a_spec, b, b_spec, chunk_shape, grid, place):
    last = grid[1] - 1

    def body(place_ref, a_ref, b_ref, own_ref, wire_ref, *acc):
        part = _mm(a_ref[...], b_ref[...])

        def emit(total):
            wire_ref[...] = total.astype(WIRE_DTYPE)

            @pl.when(pl.program_id(0) == place_ref[0])
            def _():
                own_ref[...] = total

        if last == 0:
            emit(part)
        else:
            @pl.when(pl.program_id(1) == 0)
            def _():
                acc[0][...] = part

            @pl.when(pl.program_id(1) > 0)
            def _():
                acc[0][...] += part

            @pl.when(pl.program_id(1) == last)
            def _():
                emit(acc[0][...])

    zeros = (0,) * len(chunk_shape)
    return pl.pallas_call(
        body, name=name,
        grid_spec=pltpu.PrefetchScalarGridSpec(
            num_scalar_prefetch=1, grid=grid, in_specs=[a_spec, b_spec],
            out_specs=[BS(chunk_shape, lambda j, t, pr: zeros), BS((None,) + chunk_shape, lambda j, t, pr: (j,) + zeros)],
            scratch_shapes=[pltpu.VMEM(chunk_shape, F32)] if last else []),
        out_shape=[SDS(chunk_shape, F32), SDS((N_CHIPS,) + chunk_shape, WIRE_DTYPE)],
        compiler_params=_params())(place, a, b)


SCAN_SPLIT = 2


def _scan_consts(pw_ref, cols, reverse):
    sgn = -1.0 if reverse else 1.0
    row = lax.broadcasted_iota(jnp.int32, (8, cols.size), 0)
    steps = []
    for i, k in enumerate((1, 2, 4)):
        mask = (row < 8 - k) if reverse else (row >= k)
        steps.append((k, jnp.where(mask, pw_ref[2 * i, :, cols], 0.0),
                      jnp.where(mask, sgn * pw_ref[2 * i + 1, :, cols], 0.0)))
    c = 4 if reverse else 3
    return steps, pw_ref[2 * c, :, cols], sgn * pw_ref[2 * c + 1, :, cols]


def _scan_block(br, bi, steps, reverse):
    for k, ar, ai in steps:
        sh = 8 - k if reverse else k
        sr = pltpu.roll(br, sh, 0)
        si = pltpu.roll(bi, sh, 0)
        br, bi = br + ar * sr - ai * si, bi + ar * si + ai * sr
    return br, bi


SSM_BLOCK_GROUPS = 8


def _ssm_fwd(l, z, b2_re, b2_im, c2_re, c2_im, pw, dskip):
    s = z.shape[0]
    gc = SSM_BLOCK_GROUPS * SSM_GROUP
    gl = SSM_BLOCK_GROUPS * SSM_STATE
    nblk = b2_re.shape[0] // gc
    tw = b2_re.shape[1]

    def body(u_ref, bre2, bim2, cre2, cim2, pw_ref, d_ref, hre, him, y_ref):
        u = u_ref[...]
        hre[...] = _mm(u, _block_matrix(bre2, gl))
        him[...] = _mm(u, _block_matrix(bim2, gl))
        hw = gl // SCAN_SPLIT
        for q in range(SCAN_SPLIT):
            cols = pl.ds(q * hw, hw)
            steps, car, cai = _scan_consts(pw_ref, cols, False)

            def step(i, carry, cols=cols, steps=steps, car=car, cai=cai):
                cr, ci = carry
                r0 = pl.multiple_of(i * 8, 8)
                br, bi = _scan_block(hre[pl.ds(r0, 8), cols], him[pl.ds(r0, 8), cols], steps, False)
                hr = br + car * cr - cai * ci
                hi = bi + car * ci + cai * cr
                hre[pl.ds(r0, 8), cols] = hr
                him[pl.ds(r0, 8), cols] = hi
                return jnp.broadcast_to(hr[7:8, :], (8, hw)), jnp.broadcast_to(hi[7:8, :], (8, hw))

            zero = jnp.zeros((8, hw), F32)
            lax.fori_loop(0, s // 8, step, (zero, zero))
        y_ref[...] = (_mm_nt(hre[...], _block_matrix(cre2, gl)) - _mm_nt(him[...], _block_matrix(cim2, gl))
                      + d_ref[...] * u)

    twice = BS((gc, tw), lambda k: (k, 0))
    return pl.pallas_call(
        body, name=f"ssm_fwd_l{l}", grid=(nblk,),
        in_specs=[BS((s, gc), lambda k: (0, k)), twice, twice, twice, twice, BS((10, 8, gl), lambda k: (0, 0, k)),
                  BS((1, gc), lambda k: (0, k))],
        out_specs=[BS((s, gl), lambda k: (0, k)), BS((s, gl), lambda k: (0, k)), BS((s, gc), lambda k: (0, k))],
        out_shape=[SDS((s, nblk * gl), F32), SDS((s, nblk * gl), F32), SDS((s, nblk * gc), F32)],
        compiler_params=_params())(z, b2_re, b2_im, c2_re, c2_im, pw, dskip)


def _ssm_bwd(l, dy, z, hre, him, b2_re, b2_im, c2_re, c2_im, pw, dskip):
    s = z.shape[0]
    gc = SSM_BLOCK_GROUPS * SSM_GROUP
    gl = SSM_BLOCK_GROUPS * SSM_STATE
    nblk = b2_re.shape[0] // gc
    tw = b2_re.shape[1]

    def body(dy_ref, u_ref, hre_ref, him_ref, bre2, bim2, cre2, cim2, pw_ref, d_ref,
             du_ref, dbre_ref, dbim_ref, dcre_ref, dcim_ref, dar_ref, dai_ref, dd_ref, gre, gim):
        dyv = dy_ref[...]
        u = u_ref[...]
        gre[...] = _mm(dyv, _block_matrix(cre2, gl))
        gim[...] = -_mm(dyv, _block_matrix(cim2, gl))
        dcre_ref[...] = _block_diagonal_of(_mm_tn(dyv, hre_ref[...]))
        dcim_ref[...] = -_block_diagonal_of(_mm_tn(dyv, him_ref[...]))
        dd_ref[...] = _colsum(dyv * u)
        n8 = s // 8
        hw = gl // SCAN_SPLIT
        row = lax.broadcasted_iota(jnp.int32, (8, hw), 0)
        for q in range(SCAN_SPLIT):
            cols = pl.ds(q * hw, hw)
            steps, car, cai = _scan_consts(pw_ref, cols, True)

            def step(ii, carry, cols=cols, steps=steps, car=car, cai=cai):
                cr, ci, accr, acci = carry
                i = n8 - 1 - ii
                r0 = pl.multiple_of(i * 8, 8)
                br, bi = _scan_block(gre[pl.ds(r0, 8), cols], gim[pl.ds(r0, 8), cols], steps, True)
                dr = br + car * cr - cai * ci
                di = bi + car * ci + cai * cr
                gre[pl.ds(r0, 8), cols] = dr
                gim[pl.ds(r0, 8), cols] = di
                rp = pl.multiple_of(jnp.maximum(i - 1, 0) * 8, 8)
                keep = jnp.where(i > 0, 1.0, 0.0)
                pr = jnp.where(row >= 1, pltpu.roll(hre_ref[pl.ds(r0, 8), cols], 1, 0),
                               keep * pltpu.roll(hre_ref[pl.ds(rp, 8), cols], 1, 0))
                pi = jnp.where(row >= 1, pltpu.roll(him_ref[pl.ds(r0, 8), cols], 1, 0),
                               keep * pltpu.roll(him_ref[pl.ds(rp, 8), cols], 1, 0))
                accr = accr + dr * pr + di * pi
                acci = acci + di * pr - dr * pi
                return (jnp.broadcast_to(dr[0:1, :], (8, hw)), jnp.broadcast_to(di[0:1, :], (8, hw)), accr, acci)

            zero = jnp.zeros((8, hw), F32)
            _, _, accr, acci = lax.fori_loop(0, n8, step, (zero, zero, zero, zero))
            dar_ref[:, cols] = _colsum(accr)
            dai_ref[:, cols] = _colsum(acci)
        dbr = gre[...]
        dbi = gim[...]
        du_ref[...] = (dyv * d_ref[...] + _mm_nt(dbr, _block_matrix(bre2, gl))
                       + _mm_nt(dbi, _block_matrix(bim2, gl))).astype(du_ref.dtype)
        dbre_ref[...] = _block_diagonal_of(_mm_tn(u, dbr))
        dbim_ref[...] = _block_diagonal_of(_mm_tn(u, dbi))

    col = lambda n: BS((s, n), lambda k: (0, k))
    twice = BS((gc, tw), lambda k: (k, 0))
    diag = BS((gc, SSM_STATE), lambda k: (k, 0))
    outs = pl.pallas_call(
        body, name=f"ssm_bwd_l{l}", grid=(nblk,),
        in_specs=[col(gc), col(gc), col(gl), col(gl), twice, twice, twice, twice,
                  BS((10, 8, gl), lambda k: (0, 0, k)), BS((1, gc), lambda k: (0, k))],
        out_specs=[col(gc), diag, diag, diag, diag, BS((1, gl), lambda k: (0, k)),
                   BS((1, gl), lambda k: (0, k)), BS((1, gc), lambda k: (0, k))],
        out_shape=[SDS((s, nblk * gc), MXU_DTYPE)] + [SDS((nblk * gc, SSM_STATE), F32)] * 4
        + [SDS((1, nblk * gl), F32), SDS((1, nblk * gl), F32), SDS((1, nblk * gc), F32)],
        scratch_shapes=[pltpu.VMEM((s, gl), F32), pltpu.VMEM((s, gl), F32)], compiler_params=_params(),
    )(dy, z, hre, him, b2_re, b2_im, c2_re, c2_im, pw, dskip)
    return dict(zip(("du", "dbbar_re", "dbbar_im", "dc_re", "dc_im", "dabar_re", "dabar_im", "dd"), outs))


def _conv_fwd(l, z, wdw, bdw):
    s = z.shape[0]
    cw = wdw.shape[1]
    lb = 128
    tr = min(256, s)
    off1 = cw // lb
    off2 = 2 * cw // lb

    def body(v1_ref, v2_ref, w_ref, b_ref, hc_ref, scr):
        scr[0:CONV_PAD, :] = jnp.zeros((CONV_PAD, lb), F32)
        scr[CONV_PAD:, :] = v1_ref[...] * _sigmoid(v2_ref[...])
        for t in range(s // tr):
            acc = jnp.broadcast_to(b_ref[...], (tr, lb))
            for k in range(CONV_KERNEL):
                acc = acc + w_ref[pl.ds(k, 1), :] * scr[pl.ds(t * tr + CONV_PAD - (CONV_KERNEL - 1) + k, tr), :]
            hc_ref[pl.ds(t * tr, tr), :] = acc

    return pl.pallas_call(
        body, name=f"conv_fwd_l{l}", grid=(cw // lb,),
        in_specs=[BS((s, lb), lambda k: (0, off1 + k)), BS((s, lb), lambda k: (0, off2 + k)),
                  BS((CONV_KERNEL, lb), lambda k: (0, k)), BS((1, lb), lambda k: (0, k))],
        out_specs=BS((s, lb), lambda k: (0, k)), out_shape=SDS((s, cw), F32),
        scratch_shapes=[pltpu.VMEM((s + CONV_PAD, lb), F32)], compiler_params=_params())(z, z, wdw, bdw)


def _conv_bwd(l, dhc, z, wdw):
    s = z.shape[0]
    cw = wdw.shape[1]
    lb = 128
    tr = min(256, s)
    off1 = cw // lb
    off2 = 2 * cw // lb
    nb = cw // lb

    def body(d_ref, v1_ref, v2_ref, w_ref, dv1_ref, dv2_ref, dw_ref, db_ref, hpad, dpad):
        v1 = v1_ref[...]
        sg = _sigmoid(v2_ref[...])
        dv = d_ref[...]
        hpad[0:CONV_PAD, :] = jnp.zeros((CONV_PAD, lb), F32)
        hpad[CONV_PAD:, :] = v1 * sg
        dpad[0:s, :] = dv
        dpad[s:, :] = jnp.zeros((CONV_PAD, lb), F32)
        db_ref[...] = _colsum(dv)
        dws = [jnp.zeros((1, lb), F32) for _ in range(CONV_KERNEL)]
        for t in range(s // tr):
            dt = d_ref[pl.ds(t * tr, tr), :]
            acc = jnp.zeros((tr, lb), F32)
            for k in range(CONV_KERNEL):
                acc = acc + w_ref[pl.ds(k, 1), :] * dpad[pl.ds(t * tr + (CONV_KERNEL - 1) - k, tr), :]
                dws[k] = dws[k] + _colsum(dt * hpad[pl.ds(t * tr + CONV_PAD - (CONV_KERNEL - 1) + k, tr), :])
            sgt = _sigmoid(v2_ref[pl.ds(t * tr, tr), :])
            v1t = v1_ref[pl.ds(t * tr, tr), :]
            dv1_ref[pl.ds(t * tr, tr), :] = (acc * sgt).astype(dv1_ref.dtype)
            dv2_ref[pl.ds(t * tr, tr), :] = (acc * v1t * (sgt * (1.0 - sgt))).astype(dv2_ref.dtype)
        for k in range(CONV_KERNEL):
            dw_ref[pl.ds(k, 1), :] = dws[k]

    return pl.pallas_call(
        body, name=f"conv_bwd_l{l}", grid=(nb,),
        in_specs=[BS((s, lb), lambda k: (0, k)), BS((s, lb), lambda k: (0, off1 + k)),
                  BS((s, lb), lambda k: (0, off2 + k)), BS((CONV_KERNEL, lb), lambda k: (0, k))],
        out_specs=[BS((s, lb), lambda k: (0, k)), BS((s, lb), lambda k: (0, k)),
                   BS((CONV_KERNEL, lb), lambda k: (0, k)), BS((1, lb), lambda k: (0, k))],
        out_shape=[SDS((s, cw), MXU_DTYPE), SDS((s, cw), MXU_DTYPE), SDS((CONV_KERNEL, cw), F32), SDS((1, cw), F32)],
        scratch_shapes=[pltpu.VMEM((s + CONV_PAD, lb), F32), pltpu.VMEM((s + CONV_PAD, lb), F32)],
        compiler_params=_params())(dhc, z, z, wdw)


def _pool_window(k):
    return jnp.where(k == 0, float(POOL_WINDOWS[0]),
                     jnp.where(k == 1, float(POOL_WINDOWS[1]),
                               jnp.where(k == 2, float(POOL_WINDOWS[2]), float(POOL_WINDOWS[3]))))


def _pool_fwd(l, z, pw_width):
    s = z.shape[0]
    lb = pw_width // len(POOL_WINDOWS)
    off = 3 * pw_width // lb

    def body(u_ref, p_ref):
        k = pl.program_id(0)
        u = u_ref[...]
        row = lax.broadcasted_iota(jnp.int32, (s, lb), 0)
        sums = [u]
        for sh in (1, 2, 4, 8):
            prev = sums[-1]
            sums.append(prev + jnp.where(row >= sh, pltpu.roll(prev, sh, 0), 0.0))
        sel = jnp.where(k == 0, sums[1], jnp.where(k == 1, sums[2], jnp.where(k == 2, sums[3], sums[4])))
        cnt = jnp.minimum((row + 1).astype(F32), _pool_window(k))
        p_ref[...] = sel / cnt - u

    return pl.pallas_call(
        body, name=f"pool_fwd_l{l}", grid=(len(POOL_WINDOWS),),
        in_specs=[BS((s, lb), lambda k: (0, off + k))], out_specs=BS((s, lb), lambda k: (0, k)),
        out_shape=SDS((s, pw_width), F32), compiler_params=_params())(z)


def _pool_bwd(l, dp):
    s, width = dp.shape
    lb = width // len(POOL_WINDOWS)

    def body(d_ref, du_ref):
        k = pl.program_id(0)
        dv = d_ref[...]
        row = lax.broadcasted_iota(jnp.int32, (s, lb), 0)
        cnt = jnp.minimum((row + 1).astype(F32), _pool_window(k))
        sums = [dv / cnt]
        for sh in (1, 2, 4, 8):
            prev = sums[-1]
            sums.append(prev + jnp.where(row < s - sh, pltpu.roll(prev, s - sh, 0), 0.0))
        sel = jnp.where(k == 0, sums[1], jnp.where(k == 1, sums[2], jnp.where(k == 2, sums[3], sums[4])))
        du_ref[...] = (sel - dv).astype(du_ref.dtype)

    return pl.pallas_call(
        body, name=f"pool_bwd_l{l}", grid=(len(POOL_WINDOWS),),
        in_specs=[BS((s, lb), lambda k: (0, k))], out_specs=BS((s, lb), lambda k: (0, k)),
        out_shape=SDS((s, width), MXU_DTYPE), compiler_params=_params())(dp)


def _zoh(a_re, a_im, log_dt):
    dt = jnp.exp(log_dt)
    mag = jnp.exp(dt * a_re)
    ang = dt * a_im
    abar_re = mag * jnp.cos(ang)
    abar_im = mag * jnp.sin(ang)
    den = a_re * a_re + a_im * a_im
    nr = abar_re - 1.0
    ni = abar_im
    f_re = (nr * a_re + ni * a_im) / den
    f_im = (ni * a_re - nr * a_im) / den
    return abar_re, abar_im, f_re, f_im


def _zoh_fwd(l, a_re, a_im, log_dt):
    def body(ar, ai, ld, o0, o1, o2, o3):
        for ref, val in zip((o0, o1, o2, o3), _zoh(ar[...], ai[...], ld[...])):
            ref[...] = val

    return pl.pallas_call(body, name=f"zoh_fwd_l{l}", out_shape=[SDS(a_re.shape, F32)] * 4)(a_re, a_im, log_dt)


def _zoh_bwd(l, a_re, a_im, log_dt, cts):
    def body(ar, ai, ld, c0, c1, c2, c3, dar, dai, dld):
        _, vjp = jax.vjp(_zoh, ar[...], ai[...], ld[...])
        g = vjp((c0[...], c1[...], c2[...], c3[...]))
        dar[...] = g[0]
        dai[...] = g[1]
        dld[...] = g[2]

    return pl.pallas_call(body, name=f"zoh_bwd_l{l}",
                          out_shape=[SDS(a_re.shape, F32), SDS(a_re.shape, F32), SDS(log_dt.shape, F32)],
                          )(a_re, a_im, log_dt, *cts)


def _bbar_fwd(l, f_re, f_im, b_re, b_im, c_re, c_im):
    g, p, n = b_re.shape[1:]
    m = MXU_DTYPE

    def body(fr, fi, br, bi, cr, ci, *outs):
        r = lax.broadcasted_iota(jnp.int32, (n, 2 * n), 0)
        c = lax.broadcasted_iota(jnp.int32, (n, 2 * n), 1)
        twice = jnp.where((c & (n - 1)) == r, 1.0, 0.0).astype(m)
        vals = (fr[...] * br[...] - fi[...] * bi[...], fr[...] * bi[...] + fi[...] * br[...], cr[...], ci[...])
        for o_ref, v in zip(outs, vals):
            o_ref[...] = _mm(v.astype(m).reshape(g * p, n), twice).astype(m)

    whole = lambda shp: BS(shp, lambda i: (0,) * len(shp))
    layer = BS((None, g, p, n), lambda i: (l, 0, 0, 0))
    return pl.pallas_call(body, name=f"bbar_fwd_l{l}", grid=(1,),
                          in_specs=[whole((g, 1, n)), whole((g, 1, n)), layer, layer, layer, layer],
                          out_specs=[whole((g * p, 2 * n))] * 4,
                          out_shape=[SDS((g * p, 2 * n), m)] * 4)(f_re, f_im, b_re, b_im, c_re, c_im)


def _block_mask(rows, lanes):
    r = lax.broadcasted_iota(jnp.int32, (rows, lanes), 0)
    c = lax.broadcasted_iota(jnp.int32, (rows, lanes), 1)
    return (r >> (SSM_GROUP.bit_length() - 1)) == (c >> (SSM_STATE.bit_length() - 1))


def _block_matrix(twice_ref, lanes):
    v = twice_ref[...]
    tiled = jnp.concatenate([v] * (lanes // v.shape[1]), axis=1)
    return jnp.where(_block_mask(v.shape[0], lanes), tiled, jnp.zeros_like(tiled))


def _block_diagonal_of(full):
    rows, lanes = full.shape
    kept = jnp.where(_block_mask(rows, lanes), full, 0.0)
    folded = kept[:, 0:128]
    for q in range(1, lanes // 128):
        folded = folded + kept[:, q * 128:(q + 1) * 128]
    return (folded + pltpu.roll(folded, SSM_STATE, 1))[:, :SSM_STATE]


def _bbar_bwd(l, f_re, f_im, b_re, b_im, d_re, d_im):
    g, p, n = b_re.shape[1:]

    def body(fr, fi, br, bi, dr, di, dfr, dfi, dbr, dbi):
        dfr[...] = jnp.sum(dr[...] * br[...] + di[...] * bi[...], axis=1, keepdims=True)
        dfi[...] = jnp.sum(di[...] * br[...] - dr[...] * bi[...], axis=1, keepdims=True)
        dbr[...] = fr[...] * dr[...] + fi[...] * di[...]
        dbi[...] = fr[...] * di[...] - fi[...] * dr[...]

    whole = lambda shp: BS(shp, lambda i: (0,) * len(shp))
    layer = BS((None, g, p, n), lambda i: (l, 0, 0, 0))
    return pl.pallas_call(body, name=f"bbar_bwd_l{l}", grid=(1,),
                          in_specs=[whole((g, 1, n)), whole((g, 1, n)), layer, layer, whole((g, p, n)),
                                    whole((g, p, n))],
                          out_specs=[whole((g, 1, n)), whole((g, 1, n)), whole((g, p, n)), whole((g, p, n))],
                          out_shape=[SDS((g, 1, n), F32), SDS((g, 1, n), F32), SDS((g, p, n), F32),
                                     SDS((g, p, n), F32)])(f_re, f_im, b_re, b_im, d_re, d_im)


def _powers(l, abar_re, abar_im):
    lanes = abar_re.shape[1]

    def body(ar_ref, ai_ref, o_ref):
        ar, ai = ar_ref[...], ai_ref[...]
        pows = [(ar, ai)]
        for _ in range(7):
            pr, pi = pows[-1]
            pows.append((pr * ar - pi * ai, pr * ai + pi * ar))
        row = lax.broadcasted_iota(jnp.int32, (8, lanes), 0)
        for i, k in enumerate((1, 2, 4)):
            o_ref[2 * i] = jnp.broadcast_to(pows[k - 1][0], (8, lanes))
            o_ref[2 * i + 1] = jnp.broadcast_to(pows[k - 1][1], (8, lanes))
        for slot, order in ((3, range(8)), (4, range(7, -1, -1))):
            vr = jnp.zeros((8, lanes), F32)
            vi = jnp.zeros((8, lanes), F32)
            for r, e in enumerate(order):
                vr = jnp.where(row == r, pows[e][0], vr)
                vi = jnp.where(row == r, pows[e][1], vi)
            o_ref[2 * slot] = vr
            o_ref[2 * slot + 1] = vi

    return pl.pallas_call(body, name=f"powers_l{l}", out_shape=SDS((10, 8, lanes), F32))(abar_re, abar_im)


def _ssm_prepare(l, prm):
    g, n, p = SSM_GROUPS, SSM_STATE, SSM_GROUP
    a_re, a_im = prm["ssm_a_re"][l], prm["ssm_a_im"][l]
    log_dt = prm["ssm_log_dt"][l].reshape(g, 1)
    abar_re, abar_im, f_re, f_im = _zoh_fwd(l, a_re, a_im, log_dt)
    f_re, f_im = f_re.reshape(g, 1, n), f_im.reshape(g, 1, n)
    b2_re, b2_im, c2_re, c2_im = _bbar_fwd(l, f_re, f_im, prm["ssm_b_re"], prm["ssm_b_im"], prm["ssm_c_re"],
                                           prm["ssm_c_im"])
    pw = _powers(l, abar_re.reshape(1, g * n), abar_im.reshape(1, g * n))
    return dict(a_re=a_re, a_im=a_im, log_dt=log_dt, f_re=f_re, f_im=f_im, b2_re=b2_re, b2_im=b2_im, c2_re=c2_re,
                c2_im=c2_im, pw=pw, dskip=prm["ssm_d"][l].reshape(1, g * p))


def _ssm_param_grads(l, sd, r, prm):
    g, n, p = SSM_GROUPS, SSM_STATE, SSM_GROUP
    dfr, dfi, db_re, db_im = _bbar_bwd(l, sd["f_re"], sd["f_im"], prm["ssm_b_re"], prm["ssm_b_im"],
                                       r["dbbar_re"].reshape(g, p, n), r["dbbar_im"].reshape(g, p, n))
    cts = (r["dabar_re"].reshape(g, n), r["dabar_im"].reshape(g, n), dfr.reshape(g, n), dfi.reshape(g, n))
    da_re, da_im, dlog_dt = _zoh_bwd(l, sd["a_re"], sd["a_im"], sd["log_dt"], cts)
    return dict(ssm_a_re=da_re, ssm_a_im=da_im, ssm_log_dt=dlog_dt.reshape(g), ssm_b_re=db_re, ssm_b_im=db_im,
                ssm_c_re=r["dc_re"].reshape(g, p, n), ssm_c_im=r["dc_im"].reshape(g, p, n),
                ssm_d=r["dd"].reshape(g, p))


def _ffn_weight_grads(l, fb, dx2, s, place):
    d = dx2.shape[1]
    hcn = fb["act"].shape[1]
    g = {}
    for name, key, rhs in (("ffn_w_gate", "dgate", fb["h2"]), ("ffn_w_up", "dup", fb["h2"]),
                           ("ffn_w_down", "act", fb["dx2"])):
        g[name] = _tn_matmul(f"d{name}_l{l}", fb[key], BS((None, hcn, s), lambda j, t, pr: (j, 0, 0)), rhs,
                             BS((s, d), lambda j, t, pr: (0, 0)), (hcn, d), (N_CHIPS, 1), place)
    return g


def _in_weight_grad(l, ht, dz, place):
    d, s = ht.shape
    ncw = dz.shape[1] // N_CHIPS
    return _tn_matmul(f"dw_in_l{l}", ht, BS((d, s), lambda j, t, pr: (0, 0)), dz, BS((s, ncw), lambda j, t, pr: (0, j)),
                      (d, ncw), (N_CHIPS, 1), place)


def _fused_tn(name, pairs, kinds, s, place):
    n = len(pairs)

    def shape_of(a, b, kind):
        k, m = a.shape[1], b.shape[1]
        if kind == "rows":
            return (N_CHIPS, k // N_CHIPS, m)
        if kind == "cols":
            return (N_CHIPS, k, m // N_CHIPS)
        return (k // 128, 128, 128)

    shapes = [shape_of(a, b, kind) for (a, b), kind in zip(pairs, kinds)]
    out_shape = []
    for shp, kind in zip(shapes, kinds):
        out_shape += [SDS(shp, F32)] if kind == "groups" else [SDS(shp[1:], F32), SDS(shp, WIRE_DTYPE)]

    def body(place_ref, *refs):
        ins, outs, accs = refs[:2 * n], refs[2 * n:2 * n + len(out_shape)], refs[2 * n + len(out_shape):]
        o = 0
        for i, kind in enumerate(kinds):
            a, b = ins[2 * i][...], ins[2 * i + 1][...]
            if kind == "groups":
                for k in range(shapes[i][0]):
                    outs[o][k] = _mm_tn(a[:, k * 128:(k + 1) * 128], b[:, k * 128:(k + 1) * 128])
                o += 1
                continue
            acc = accs[i]
            if kind == "rows":
                acc[...] = _mm_tn(a, b).reshape(acc.shape)
            else:
                full = _mm_tn(a, b)
                nc = acc.shape[2]
                for j in range(N_CHIPS):
                    acc[j] = full[:, j * nc:(j + 1) * nc]
            outs[o][...] = acc[place_ref[0]]
            outs[o + 1][...] = acc[...].astype(WIRE_DTYPE)
            o += 2

    whole = lambda shp: BS(shp, lambda t, pr: (0,) * len(shp))
    outs = pl.pallas_call(
        body, name=name,
        grid_spec=pltpu.PrefetchScalarGridSpec(
            num_scalar_prefetch=1, grid=(1,),
            in_specs=[whole(v.shape) for pair in pairs for v in pair],
            out_specs=[whole(o.shape) for o in out_shape],
            scratch_shapes=[pltpu.VMEM(shp, F32) for shp in shapes]),
        out_shape=out_shape, compiler_params=_params(),
    )(place, *[v for pair in pairs for v in pair])
    res, o = [], 0
    for kind in kinds:
        if kind == "groups":
            res.append(outs[o])
            o += 1
        else:
            res.append((outs[o], outs[o + 1]))
            o += 2
    return res


def _mixer_weight_grads(l, sv, mb, dx1, s, place):
    g = {}
    (g["w_out"], g["ssm_w_glu"]) = _fused_tn(f"dw_out_glu_l{l}", [(mb["merged"], dx1), (mb["ge"], mb["dt"])],
                                            ("rows", "rows"), s, place)
    (g["ssm_w_proj"], g["conv_w_proj"], g["pool_w_proj"]) = _fused_tn(
        f"dw_proj_l{l}", [(mb["sa"], mb["dya"]), (mb["ac"], mb["dyb"]), (mb["pp"], mb["dyc"])],
        ("cols", "cols", "cols"), s, place)
    (dwgrp,) = _fused_tn(f"dpool_w_group_l{l}", [(sv["p"], mb["dq"])], ("groups",), s, place)
    return g, dwgrp


def _local_step(x, target, weights_of, prm, place, on_grads=None):
    s, d = x.shape
    cw = prm["ssm_b_glu"].shape[1]
    sp = {k: prm[k].reshape(N_LAYERS, 1, -1) for k in ("norm1", "norm2", "b_gate", "ssm_b_glu", "conv_ln_g", "conv_ln_b",
                                                        "pool_scale", "conv_b_dw")}
    sp["pool_w_group"] = prm["pool_w_group"]
    saved = []
    xin = x
    prepared = [_ssm_prepare(l, prm) for l in range(N_LAYERS)]
    ready = tuple(sd[k] for sd in prepared for k in ("pw", "b2_re", "c2_re"))
    for l in range(N_LAYERS):
        fw = weights_of(l, "in", (xin,) + (ready if l == 0 else ()))
        sd = prepared[l]
        z, h = _in_proj(l, xin, sp["norm1"], fw["w_in"])
        hre, him, y = _ssm_fwd(l, z, sd["b2_re"], sd["b2_im"], sd["c2_re"], sd["c2_im"], sd["pw"], sd["dskip"])
        p = _pool_fwd(l, z, cw)
        fw.update(weights_of(l, "mixer", (y, p)))
        wdw = fw["conv_w_dw"]
        hc = _conv_fwd(l, z, wdw, sp["conv_b_dw"][l])
        x1 = _merge_fwd(l, xin, y, hc, p, z, fw, sp)
        fw.update(weights_of(l, "ffn", (x1,)))
        x2, gate, up, h2 = _ffn_fwd(l, x1, sp["norm2"], fw["ffn_w_gate"], fw["ffn_w_up"], fw["ffn_w_down"])
        saved.append(dict(x=xin, z=z, h=h, hre=hre, him=him, y=y, hc=hc, p=p, x1=x1, sd=sd, wdw=wdw, fw=fw,
                          gate=gate, up=up, h2=h2))
        xin = x2
    dx, loss, dfinal = _loss_head(xin, target, prm["final_norm"].reshape(1, d))
    big = [None] * N_LAYERS
    small = [None] * N_LAYERS
    norm2_rows = sp["norm2"]
    started = (lambda l, group, grads: on_grads(l, group, grads)) if on_grads is not None else (lambda *a: 0.0)
    for l in reversed(range(N_LAYERS)):
        sv = saved[l]
        sd, fw = sv["sd"], sv["fw"]
        fb = _ffn_bwd(l, sv["x1"], dx, sv["gate"], sv["up"], norm2_rows, fw["ffn_w_gate"], fw["ffn_w_up"],
                      fw["ffn_w_down"])
        fb["h2"] = sv["h2"]
        big[l] = _ffn_weight_grads(l, fb, dx, s, place)
        spl = dict(sp, ssm_b_glu=sp["ssm_b_glu"] + started(l, "ffn", big[l]))
        mb = _merge_bwd(l, fb["dx1"], sv["y"], sv["hc"], sv["p"], sv["z"], fw, spl)
        mixer, dwgrp = _mixer_weight_grads(l, sv, mb, fb["dx1"], s, place)
        big[l].update(mixer)
        wdw = sv["wdw"] + started(l, "mixer", mixer)
        du_c = _pool_bwd(l, mb["dp"])
        dv1, dv2, dwdw, dbdw = _conv_bwd(l, mb["dhc"], sv["z"], wdw)
        sr = _ssm_bwd(l, mb["dy"], sv["z"], sv["hre"], sv["him"], sd["b2_re"], sd["b2_im"], sd["c2_re"],
                      sd["c2_im"], sd["pw"], sd["dskip"])
        dx, dz, dnorm1 = _in_proj_bwd(l, fb["dx1"], sv["x"], sp["norm1"], fw["w_in"], sr["du"], dv1, dv2, du_c, mb["dzg"])
        w_in_grad = {"w_in": _in_weight_grad(l, sv["h"], dz, place)}
        big[l].update(w_in_grad)
        sg = _ssm_param_grads(l, sd, sr, prm)
        sg.update(norm1=dnorm1.reshape(d), b_gate=mb["db_gate"].reshape(3 * d), ssm_b_glu=mb["db_glu"].reshape(cw),
                  conv_b_dw=dbdw.reshape(cw), conv_ln_g=mb["dln_g"].reshape(cw), conv_ln_b=mb["dln_b"].reshape(cw),
                  pool_w_group=dwgrp, pool_scale=mb["dscale"].reshape(cw), norm2=fb["dnorm2"].reshape(d),
                  conv_w_dw=dwdw)
        small[l] = sg
        if l == N_LAYERS - 1:
            sg = dict(sg, final_norm=dfinal.reshape(d))
        norm2_rows = sp["norm2"] + (started(l, "in", w_in_grad) + started(l, "small", sg))
    return loss[0, 0], dx, big, small, dfinal.reshape(d)


def _place():
    return lax.axis_index("x"), lax.axis_index("y"), lax.axis_index("c")


def _other_chips(x, y):
    return [(1 - x, y), (x, 1 - y), (1 - x, 1 - y)]


def _remote(src, dst, send_sem, recv_sem, device):
    return pltpu.make_async_remote_copy(src_ref=src, dst_ref=dst, send_sem=send_sem, recv_sem=recv_sem,
                                        device_id=device, device_id_type=MESH)


def _hbm(v):
    return pltpu.with_memory_space_constraint(v, pltpu.HBM)


def _cast_into(name, w, place, dtype, after=()):
    nl, k, n = w.shape
    tr = _row_tile(k, n)
    nt = k // tr

    def body(place_ref, w_ref, *rest):
        o0_ref, o1_ref = rest[len(after):]

        @pl.when(pl.program_id(0) == 0)
        def _():
            o0_ref[...] = w_ref[...].astype(dtype)

        @pl.when(pl.program_id(0) == 1)
        def _():
            o1_ref[...] = w_ref[...].astype(dtype)

    return pl.pallas_call(
        body, name=f"cast_{name}",
        grid_spec=pltpu.PrefetchScalarGridSpec(
            num_scalar_prefetch=1, grid=(nl, nt),
            in_specs=[BS((None, tr, n), lambda l, t, pr: (l, t, 0))] + [ANY] * len(after),
            out_specs=[BS((None, tr, n), lambda l, t, pr: (pr[0], t * (1 - l) + (nt - 1) * l, 0)),
                       BS((None, tr, n), lambda l, t, pr: (pr[0], t * l, 0))]),
        out_shape=[SDS((N_CHIPS, k, n), dtype)] * 2)(place, w, *after)


def _cast_small_into(tag, ws, dtypes, place, after=()):
    n = len(ws)

    def body(place_ref, *refs):
        ins, outs = refs[:n], refs[n + len(after):]
        for i in range(n):
            @pl.when(pl.program_id(0) == 0)
            def _():
                outs[2 * i][...] = ins[i][...].astype(dtypes[i])

            @pl.when(pl.program_id(0) == 1)
            def _():
                outs[2 * i + 1][...] = ins[i][...].astype(dtypes[i])

    slot = lambda w: BS((None,) + w.shape[1:], lambda l, pr: (pr[0], 0, 0))
    outs = pl.pallas_call(
        body, name=f"cast_{tag}",
        grid_spec=pltpu.PrefetchScalarGridSpec(
            num_scalar_prefetch=1, grid=(N_LAYERS,),
            in_specs=[BS((None,) + w.shape[1:], lambda l, pr: (l, 0, 0)) for w in ws] + [ANY] * len(after),
            out_specs=[slot(w) for w in ws for _ in range(N_LAYERS)]),
        out_shape=[SDS((N_CHIPS,) + w.shape[1:], dt) for w, dt in zip(ws, dtypes) for _ in range(N_LAYERS)],
    )(place, *ws, *after)
    return [tuple(outs[N_LAYERS * i:N_LAYERS * (i + 1)]) for i in range(n)]


def _gather_rows(buf, c):
    k = buf.shape[1]
    if k % 2:
        return pl.ds(0, k)
    return pl.ds(pl.multiple_of(c * (k // 2), 8), k // 2)


def _allgather_start(tag, groups):
    ng = len(groups)
    sizes = [len(g) for g in groups]
    first = [sum(sizes[:g]) for g in range(ng)]
    flat = [b for g in groups for b in g]
    nb = len(flat)

    def body(*refs):
        ins = refs[:nb]
        sems = refs[nb:nb + 2 * ng]
        token = refs[-1]
        x, y, c = _place()
        jme = 2 * x + y
        for g in range(ng):
            for a in range(sizes[g]):
                buf = ins[first[g] + a]
                blk = buf.at[jme, _gather_rows(buf, c)]
                for k, (cx, cy) in enumerate(_other_chips(x, y)):
                    _remote(blk, blk, sems[2 * g].at[3 * a + k], sems[2 * g + 1].at[3 * a + k], (cx, cy, c)).start()
        token[...] = jnp.zeros(token.shape, F32)

    sem_shapes = [pltpu.SemaphoreType.DMA((3 * sizes[g // 2],)) for g in range(2 * ng)]
    outs = pl.pallas_call(
        body, name=f"allgather_start_{tag}", in_specs=[HBM] * nb,
        out_specs=[SEM] * (2 * ng) + [HBM] * nb + [pl.BlockSpec(memory_space=pltpu.VMEM)],
        out_shape=sem_shapes + [pltpu.HBM(b.shape, b.dtype) for b in flat] + [SDS((8, 128), F32)],
        input_output_aliases={i: 2 * ng + i for i in range(nb)},
        compiler_params=pltpu.CompilerParams(has_side_effects=SIDE_EFFECT))(*[_hbm(b) for b in flat])
    per_group = [(outs[2 * g], outs[2 * g + 1], outs[2 * ng + first[g]:2 * ng + first[g] + sizes[g]])
                 for g in range(ng)]
    return per_group, outs[-1]


def _allgather_wait(l, send_sems, recv_sems, bufs, after):
    n = len(bufs)

    def body(*refs):
        ins = refs[:n]
        ssem, rsem = refs[n], refs[n + 1]
        x, y, c = _place()
        jme = 2 * x + y
        for a in range(n):
            rows = _gather_rows(ins[a], c)
            for k, (cx, cy) in enumerate(_other_chips(x, y)):
                cp = _remote(ins[a].at[jme, rows], ins[a].at[2 * cx + cy, rows], ssem.at[3 * a + k],
                             rsem.at[3 * a + k], (cx, cy, c))
                cp.wait_send()
                cp.wait_recv()

    return pl.pallas_call(
        body, name=f"allgather_wait_{l}", in_specs=[HBM] * n + [SEM, SEM] + [ANY] * len(after), out_specs=[HBM] * n,
        out_shape=[pltpu.HBM(b.shape, b.dtype) for b in bufs], input_output_aliases={i: i for i in range(n)},
        compiler_params=pltpu.CompilerParams(has_side_effects=SIDE_EFFECT))(*bufs, send_sems, recv_sems, *after)


def _allgather_forward(l, bufs):
    n = len(bufs)
    split = [a for a in range(n) if bufs[a].shape[1] % 2 == 0]

    def body(*refs):
        ins = refs[:n]
        send_sems, recv_sems = refs[2 * n:]
        x, y, c = _place()
        sibling = (x, y, 1 - c)
        copies = []
        for a in split:
            for k, (cx, cy) in enumerate(_other_chips(x, y)):
                blk = ins[a].at[2 * cx + cy, _gather_rows(ins[a], c)]
                cp = _remote(blk, blk, send_sems.at[a, k], recv_sems.at[a, k], sibling)
                cp.start()
                copies.append(cp)
        for a in split:
            for k, (cx, cy) in enumerate(_other_chips(x, y)):
                blk = ins[a].at[2 * cx + cy, _gather_rows(ins[a], 1 - c)]
                _remote(blk, blk, send_sems.at[a, k], recv_sems.at[a, k], sibling).wait_recv()
        for cp in copies:
            cp.wait_send()

    sem = pltpu.SemaphoreType.DMA((n, 3))
    return pl.pallas_call(
        body, name=f"allgather_forward_{l}", in_specs=[ANY] * n, out_specs=[ANY] * n,
        out_shape=[SDS(b.shape, b.dtype) for b in bufs], input_output_aliases={i: i for i in range(n)},
        scratch_shapes=[sem, sem])(*bufs)


def _rs_to_owner(l, parts):
    n = len(parts)
    lands = [lax.empty((3,) + p.shape[1:], p.dtype) for p in parts]

    def body(*refs):
        ins, zones = refs[:n], refs[n:2 * n]
        send_sems, recv_sems = refs[2 * n], refs[2 * n + 1]
        token = refs[-1]
        x, y, c = _place()
        for a in range(n):
            for k, (cx, cy) in enumerate(_other_chips(x, y)):
                _remote(ins[a].at[2 * cx + cy], zones[a].at[k], send_sems.at[3 * a + k], recv_sems.at[3 * a + k],
                        (cx, cy, c)).start()
        token[...] = jnp.zeros(token.shape, F32)

    sem = pltpu.SemaphoreType.DMA((3 * n,))
    outs = pl.pallas_call(
        body, name=f"rs_to_owner_start_{l}", in_specs=[HBM] * (2 * n),
        out_specs=[SEM, SEM] + [HBM] * (2 * n) + [pl.BlockSpec(memory_space=pltpu.VMEM)],
        out_shape=[sem, sem] + [pltpu.HBM(p.shape, p.dtype) for p in parts]
        + [pltpu.HBM(z.shape, z.dtype) for z in lands] + [SDS((8, 128), F32)],
        input_output_aliases={i: 2 + i for i in range(2 * n)},
        compiler_params=pltpu.CompilerParams(has_side_effects=SIDE_EFFECT),
    )(*[_hbm(p) for p in parts], *[_hbm(z) for z in lands])
    return outs[0], outs[1], outs[2:2 + n], outs[2 + n:2 + 2 * n], outs[-1]


def _rs_to_owner_wait(l, send_sems, recv_sems, parts, lands, after):
    n = len(parts)

    def body(*refs):
        ins, zones = refs[:n], refs[n:2 * n]
        ssem, rsem = refs[2 * n], refs[2 * n + 1]
        x, y, c = _place()
        for a in range(n):
            for k, (cx, cy) in enumerate(_other_chips(x, y)):
                cp = _remote(ins[a].at[2 * cx + cy], zones[a].at[k], ssem.at[3 * a + k], rsem.at[3 * a + k],
                             (cx, cy, c))
                cp.wait_send()
                cp.wait_recv()

    outs = pl.pallas_call(
        body, name=f"rs_to_owner_wait_{l}", in_specs=[HBM] * (2 * n) + [SEM, SEM] + [ANY] * len(after),
        out_specs=[HBM] * (2 * n),
        out_shape=[pltpu.HBM(p.shape, p.dtype) for p in parts] + [pltpu.HBM(z.shape, z.dtype) for z in lands],
        input_output_aliases={i: i for i in range(2 * n)},
        compiler_params=pltpu.CompilerParams(has_side_effects=SIDE_EFFECT),
    )(*parts, *lands, send_sems, recv_sems, *after)
    return outs[:n], outs[n:]


def _rs_sibling_exchange(l, both):
    n = len(both)

    def body(*refs):
        ins = refs[:n]
        send_sems, recv_sems = refs[2 * n:]
        x, y, c = _place()
        copies = []
        for a in range(n):
            cp = _remote(ins[a].at[c], ins[a].at[c], send_sems.at[a], recv_sems.at[a], (x, y, 1 - c))
            cp.start()
            copies.append(cp)
        for a, cp in enumerate(copies):
            cp.wait_send()
            _remote(ins[a].at[1 - c], ins[a].at[1 - c], send_sems.at[a], recv_sems.at[a], (x, y, 1 - c)).wait_recv()

    sem = pltpu.SemaphoreType.DMA((n,))
    return pl.pallas_call(
        body, name=f"rs_sibling_exchange_{l}", in_specs=[ANY] * n, out_specs=[ANY] * n,
        out_shape=[SDS(b.shape, b.dtype) for b in both], input_output_aliases={i: i for i in range(n)},
        scratch_shapes=[sem, sem])(*both)


def _add_owner(name, grad, recv, place):
    r, cols = grad.shape
    tr = _row_tile(r, cols, budget=1024 * 1024)
    nt = r // tr

    def body(place_ref, g_ref, r_ref, o_ref):
        acc = ((g_ref[...] + r_ref[0].astype(F32)) + r_ref[1].astype(F32)) + r_ref[2].astype(F32)
        o_ref[...] = acc.astype(o_ref.dtype)

    return pl.pallas_call(
        body, name=name,
        grid_spec=pltpu.PrefetchScalarGridSpec(
            num_scalar_prefetch=1, grid=(nt,),
            in_specs=[BS((tr, cols), lambda t, pr: (t, 0)), BS((3, tr, cols), lambda t, pr: (0, t, 0))],
            out_specs=BS((None, tr, cols), lambda t, pr: (pr[1], t, 0))),
        out_shape=SDS((2, r, cols), WIRE_DTYPE))(place, grad, recv)


def _add_owner_group(tag, grads, recvs, place, steps):
    n = len(grads)

    def body(place_ref, *refs):
        gs, rs, outs = refs[:n], refs[n:2 * n], refs[2 * n:]
        for g_ref, r_ref, o_ref in zip(gs, rs, outs):
            acc = ((g_ref[...] + r_ref[0].astype(F32)) + r_ref[1].astype(F32)) + r_ref[2].astype(F32)
            o_ref[...] = acc.astype(o_ref.dtype)

    rows = [g.shape[0] // steps for g in grads]
    return pl.pallas_call(
        body, name=f"rs_add_owner_{tag}",
        grid_spec=pltpu.PrefetchScalarGridSpec(
            num_scalar_prefetch=1, grid=(steps,),
            in_specs=[BS((r, g.shape[1]), lambda t, pr: (t, 0)) for g, r in zip(grads, rows)]
            + [BS((3, r, g.shape[1]), lambda t, pr: (0, t, 0)) for g, r in zip(grads, rows)],
            out_specs=[BS((None, r, g.shape[1]), lambda t, pr: (pr[1], t, 0)) for g, r in zip(grads, rows)]),
        out_shape=[SDS((2,) + g.shape, WIRE_DTYPE) for g in grads], compiler_params=_params(),
    )(place, *grads, *recvs)


def _reduce_start(tag, grads):
    names = list(grads)
    send_sems, recv_sems, wires, lands, token = _rs_to_owner(tag, [grads[n][1] for n in names])
    return dict(tag=tag, names=names, send_sems=send_sems, recv_sems=recv_sems, wires=wires, lands=lands,
                grads=[grads[n][0] for n in names]), token


def _reduce_finish(tag, groups, place, after):
    all_names, all_mine = [], []
    for pending in groups:
        sub, names = pending["tag"], pending["names"]
        _, lands = _rs_to_owner_wait(sub, pending["send_sems"], pending["recv_sems"], pending["wires"],
                                     pending["lands"], after)
        if len(names) > 1:
            small = max(g.size for g in pending["grads"]) <= SMALL_GRAD_ELEMS
            mine = _add_owner_group(sub, pending["grads"], lands, place, 1 if small else ADAMW_GROUP_STEPS)
        else:
            mine = [_add_owner(f"rs_add_owner_{n}_{sub}", g, r, place)
                    for n, g, r in zip(names, pending["grads"], lands)]
        all_names += names
        all_mine += mine
    return dict(zip(all_names, _rs_sibling_exchange(tag, all_mine)))


def _small_peers(x, y, c):
    return [(x, y, 1 - c)] + [(cx, cy, c) for cx, cy in _other_chips(x, y)]


def _allgather_rows_start(tag, bufs):
    n = len(bufs)
    lands = [lax.empty((8,) + b.shape, b.dtype) for b in bufs]

    def body(*refs):
        ins, zones = refs[:n], refs[n:2 * n]
        send_sems, recv_sems = refs[2 * n], refs[2 * n + 1]
        token = refs[-1]
        x, y, c = _place()
        for a in range(n):
            for i, peer in enumerate(_small_peers(x, y, c)):
                _remote(ins[a], zones[a].at[4 * x + 2 * y + c], send_sems.at[4 * a + i], recv_sems.at[4 * a + i],
                        peer).start()
        token[...] = jnp.zeros(token.shape, F32)

    sem = pltpu.SemaphoreType.DMA((4 * n,))
    outs = pl.pallas_call(
        body, name=f"allgather_small_start_{tag}", in_specs=[HBM] * (2 * n),
        out_specs=[SEM, SEM] + [HBM] * (2 * n) + [pl.BlockSpec(memory_space=pltpu.VMEM)],
        out_shape=[sem, sem] + [pltpu.HBM(b.shape, b.dtype) for b in bufs]
        + [pltpu.HBM(z.shape, z.dtype) for z in lands] + [SDS((8, 128), F32)],
        input_output_aliases={i: 2 + i for i in range(2 * n)},
        compiler_params=pltpu.CompilerParams(has_side_effects=SIDE_EFFECT),
    )(*[_hbm(b) for b in bufs], *[_hbm(z) for z in lands])
    return outs[0], outs[1], outs[2:2 + n], outs[2 + n:2 + 2 * n], outs[-1]


def _allgather_rows_wait(tag, send_sems, recv_sems, bufs, lands, after):
    n = len(bufs)

    def body(*refs):
        ins, zones = refs[:n], refs[n:2 * n]
        ssem, rsem = refs[2 * n], refs[2 * n + 1]
        x, y, c = _place()
        for a in range(n):
            for i, (px, py, pc) in enumerate(_small_peers(x, y, c)):
                cp = _remote(ins[a], zones[a].at[4 * px + 2 * py + pc], ssem.at[4 * a + i], rsem.at[4 * a + i],
                             (px, py, pc))
                cp.wait_send()
                cp.wait_recv()

    outs = pl.pallas_call(
        body, name=f"allgather_small_wait_{tag}", in_specs=[HBM] * (2 * n) + [SEM, SEM, ANY],
        out_specs=[HBM] * (2 * n),
        out_shape=[pltpu.HBM(b.shape, b.dtype) for b in bufs] + [pltpu.HBM(z.shape, z.dtype) for z in lands],
        input_output_aliases={i: i for i in range(2 * n)},
        compiler_params=pltpu.CompilerParams(has_side_effects=SIDE_EFFECT),
    )(*bufs, *lands, send_sems, recv_sems, after)
    return outs[:n], outs[n:]


def _allgather_rows_forward(tag, lands):
    n = len(lands)

    def body(*refs):
        ins = refs[:n]
        send_sems, recv_sems = refs[2 * n:]
        x, y, c = _place()
        sibling = (x, y, 1 - c)
        copies = []
        for a in range(n):
            for k, (cx, cy) in enumerate(_other_chips(x, y)):
                blk = ins[a].at[4 * cx + 2 * cy + c]
                cp = _remote(blk, blk, send_sems.at[a, k], recv_sems.at[a, k], sibling)
                cp.start()
                copies.append(cp)
        for a in range(n):
            for k, (cx, cy) in enumerate(_other_chips(x, y)):
                blk = ins[a].at[4 * cx + 2 * cy + 1 - c]
                _remote(blk, blk, send_sems.at[a, k], recv_sems.at[a, k], sibling).wait_recv()
        for cp in copies:
            cp.wait_send()

    sem = pltpu.SemaphoreType.DMA((n, 3))
    return pl.pallas_call(body, name=f"allgather_small_forward_{tag}", in_specs=[ANY] * n, out_specs=[ANY] * n,
                          out_shape=[SDS(z.shape, z.dtype) for z in lands],
                          input_output_aliases={i: i for i in range(n)}, scratch_shapes=[sem, sem])(*lands)


def _sum_devices(tag, gathered, mine, place):
    _, r, cols = gathered.shape
    tr = _row_tile(r, cols, budget=256 * 1024)

    def body(place_ref, g_ref, x_ref, o_ref):
        me = 2 * place_ref[0] + place_ref[1]
        acc = jnp.where(me == 0, x_ref[...], g_ref[0])
        for k in range(1, 8):
            acc = acc + jnp.where(me == k, x_ref[...], g_ref[k])
        o_ref[...] = acc

    return pl.pallas_call(
        body, name=f"sum_small_grads_{tag}",
        grid_spec=pltpu.PrefetchScalarGridSpec(
            num_scalar_prefetch=1, grid=(r // tr,),
            in_specs=[BS((8, tr, cols), lambda t, pr: (0, t, 0)), BS((tr, cols), lambda t, pr: (t, 0))],
            out_specs=BS((tr, cols), lambda t, pr: (t, 0))),
        out_shape=SDS((r, cols), F32))(place, gathered, mine)


def _adamw_values(w, g, m, v):
    m = ADAM_B1 * m + (1.0 - ADAM_B1) * g
    v = ADAM_B2 * v + (1.0 - ADAM_B2) * (g * g)
    m_hat = m / (1.0 - ADAM_B1 ** ADAM_STEP)
    v_hat = v / (1.0 - ADAM_B2 ** ADAM_STEP)
    delta = -ADAM_LR * (m_hat / (jnp.sqrt(v_hat) + ADAM_EPS) + ADAM_WD * w)
    return delta, m, v


def _adamw_big(name, l, w, m, v, g, earlier=None, after=()):
    nl, r, cols = w.shape
    tr = _row_tile(r, cols, budget=1024 * 1024)
    nt = r // tr
    n_prev = 0 if earlier is None else 4

    def body(*refs):
        w_ref, m_ref, v_ref, g_ref = refs[:4]
        go_ref, d_ref, mo_ref, vo_ref = refs[4 + n_prev + len(after):]
        gv = g_ref[0].astype(F32) + g_ref[1].astype(F32)
        delta, m_new, v_new = _adamw_values(w_ref[...], gv, m_ref[...], v_ref[...])
        go_ref[...] = gv
        d_ref[...] = delta
        mo_ref[...] = m_new
        vo_ref[...] = v_new

    layer = BS((None, tr, cols), lambda t: (l, t, 0))
    return pl.pallas_call(
        body, name=f"adamw_{name}_l{l}", grid=(nt,),
        in_specs=[layer, layer, layer, BS((2, tr, cols), lambda t: (0, t, 0))] + [ANY] * (n_prev + len(after)),
        out_specs=[layer] * 4, out_shape=[SDS(w.shape, F32)] * 4,
        input_output_aliases={4 + i: i for i in range(n_prev)}, compiler_params=_params(),
    )(w, m, v, g, *(earlier or ()), *after)


def _adamw_small_group(tag, l, ws, ms, vs, gs, earlier, after=()):
    n = len(ws)
    steps = ADAMW_GROUP_STEPS
    prev = [a for e in earlier if e is not None for a in e]
    n_prev = len(prev)
    assert n_prev in (0, 4 * n)

    def body(*refs):
        w_refs, m_refs, v_refs, g_refs = refs[:n], refs[n:2 * n], refs[2 * n:3 * n], refs[3 * n:4 * n]
        outs = refs[4 * n + n_prev + len(after):]
        for i in range(n):
            gv = g_refs[i][0].astype(F32) + g_refs[i][1].astype(F32)
            delta, m_new, v_new = _adamw_values(w_refs[i][...], gv, m_refs[i][...], v_refs[i][...])
            for ref, val in zip(outs[4 * i:4 * i + 4], (gv, delta, m_new, v_new)):
                ref[...] = val

    def layer(w):
        return BS((None, w.shape[1] // steps, w.shape[2]), lambda t: (l, t, 0))

    return pl.pallas_call(
        body, name=f"adamw_{tag}_l{l}", grid=(steps,),
        in_specs=[layer(w) for w in ws] * 3
        + [BS((2, w.shape[1] // steps, w.shape[2]), lambda t: (0, t, 0)) for w in ws] + [ANY] * (n_prev + len(after)),
        out_specs=[layer(w) for w in ws for _ in range(4)],
        out_shape=[SDS(w.shape, F32) for w in ws for _ in range(4)],
        input_output_aliases={4 * n + i: i for i in range(n_prev)}, compiler_params=_params(),
    )(*ws, *ms, *vs, *gs, *prev, *after)


def _adamw_mid(ws, ms, vs, gathered, mine, place):
    n = len(ws)
    shape = ws[0].shape[1:]
    zeros = (0,) * len(shape)

    def body(place_ref, *refs):
        w_refs, m_refs, v_refs = refs[:n], refs[n:2 * n], refs[2 * n:3 * n]
        gath, own = refs[3 * n:(3 + N_LAYERS) * n], refs[(3 + N_LAYERS) * n:(3 + 2 * N_LAYERS) * n]
        outs = refs[(3 + 2 * N_LAYERS) * n:]
        me = 2 * place_ref[0] + place_ref[1]
        for i in range(n):
            gv = None
            for l in range(N_LAYERS):
                g_ref, x_ref = gath[l * n + i], own[l * n + i]
                acc = jnp.where(me == 0, x_ref[...], g_ref[0])
                for k in range(1, 8):
                    acc = acc + jnp.where(me == k, x_ref[...], g_ref[k])
                gv = acc if gv is None else jnp.where(pl.program_id(0) == l, acc, gv)
            delta, m_new, v_new = _adamw_values(w_refs[i][...], gv, m_refs[i][...], v_refs[i][...])
            for ref, val in zip(outs[4 * i:4 * i + 4], (gv, delta, m_new, v_new)):
                ref[...] = val

    layer = BS((None,) + shape, lambda l, pr: (l,) + zeros)
    kept = pl.Buffered(1)
    outs = pl.pallas_call(
        body, name="adamw_replicated_matrices",
        grid_spec=pltpu.PrefetchScalarGridSpec(
            num_scalar_prefetch=1, grid=(N_LAYERS,),
            in_specs=[layer] * (3 * n)
            + [BS((8,) + shape, lambda l, pr: (0,) + zeros, pipeline_mode=kept)] * (N_LAYERS * n)
            + [BS(shape, lambda l, pr: zeros, pipeline_mode=kept)] * (N_LAYERS * n),
            out_specs=[layer] * (4 * n)),
        out_shape=[SDS(ws[0].shape, F32)] * (4 * n), compiler_params=_params(),
    )(place, *ws, *ms, *vs, *[g for l in range(N_LAYERS) for g in gathered[l]],
      *[x for l in range(N_LAYERS) for x in mine[l]])
    return [tuple(outs[4 * i:4 * i + 4]) for i in range(n)]


def _adamw_rows(w, m, v, g):
    r, cols = w.shape
    tr = _row_tile(r, cols, budget=512 * 1024)

    def body(w_ref, m_ref, v_ref, g_ref, d_ref, mo_ref, vo_ref):
        delta, m_new, v_new = _adamw_values(w_ref[...], g_ref[...], m_ref[...], v_ref[...])
        d_ref[...] = delta
        mo_ref[...] = m_new
        vo_ref[...] = v_new

    spec = BS((tr, cols), lambda t: (t, 0))
    return pl.pallas_call(body, name="adamw_small", grid=(r // tr,), in_specs=[spec] * 4, out_specs=[spec] * 3,
                          out_shape=[SDS(w.shape, F32)] * 3)(w, m, v, g)


SMALL_GRAD_ELEMS = 256 * 1024
ADAMW_GROUP_STEPS = 4
PACK_ALIGN = 8 * 128
PACK_ROWS = 128


def _pack_rows(arrays):
    parts, rows = [], 0
    for a in arrays:
        flat = a.reshape(-1)
        pad = (-flat.shape[0]) % PACK_ALIGN
        if pad:
            flat = jnp.pad(flat, (0, pad))
        parts.append(flat.reshape(-1, 128))
        rows += parts[-1].shape[0]
    if rows % PACK_ROWS:
        parts.append(jnp.zeros((PACK_ROWS - rows % PACK_ROWS, 128), parts[0].dtype))
    return jnp.concatenate(parts, axis=0)


def _unpack_rows(buf, shapes):
    out, row = [], 0
    for shape in shapes:
        size = math.prod(shape)
        rows = -(-size // PACK_ALIGN) * (PACK_ALIGN // 128)
        out.append(buf[row:row + rows].reshape(-1)[:size].reshape(shape))
        row += rows
    return out


def kernel(x, norm1, w_in, b_gate, ssm_a_re, ssm_a_im, ssm_log_dt, ssm_b_re, ssm_b_im, ssm_c_re, ssm_c_im, ssm_d, ssm_w_glu, ssm_b_glu, ssm_w_proj, conv_w_dw, conv_b_dw, conv_ln_g, conv_ln_b, conv_w_proj, pool_w_group, pool_scale, pool_w_proj, w_out, norm2, ffn_w_gate, ffn_w_up, ffn_w_down, final_norm, loss_target, m_norm1, m_w_in, m_b_gate, m_ssm_a_re, m_ssm_a_im, m_ssm_log_dt, m_ssm_b_re, m_ssm_b_im, m_ssm_c_re, m_ssm_c_im, m_ssm_d, m_ssm_w_glu, m_ssm_b_glu, m_ssm_w_proj, m_conv_w_dw, m_conv_b_dw, m_conv_ln_g, m_conv_ln_b, m_conv_w_proj, m_pool_w_group, m_pool_scale, m_pool_w_proj, m_w_out, m_norm2, m_ffn_w_gate, m_ffn_w_up, m_ffn_w_down, m_final_norm, v_norm1, v_w_in, v_b_gate, v_ssm_a_re, v_ssm_a_im, v_ssm_log_dt, v_ssm_b_re, v_ssm_b_im, v_ssm_c_re, v_ssm_c_im, v_ssm_d, v_ssm_w_glu, v_ssm_b_glu, v_ssm_w_proj, v_conv_w_dw, v_conv_b_dw, v_conv_ln_g, v_conv_ln_b, v_conv_w_proj, v_pool_w_group, v_pool_scale, v_pool_w_proj, v_w_out, v_norm2, v_ffn_w_gate, v_ffn_w_up, v_ffn_w_down, v_final_norm):
    given = dict(locals())
    cx, cy, cc = _place()
    place = jnp.stack([2 * cx + cy, cc]).astype(jnp.int32)

    def kernel_view(n, a):
        if n in TRANSPOSED:
            return a.transpose(0, 2, 1)
        return a.transpose(0, 1, 3, 2) if n in ("ssm_b_re", "ssm_b_im") else a

    prm = {n: given[n] for n in WEIGHTS}
    mom = {n: given["m_" + n] for n in WEIGHTS}
    var = {n: given["v_" + n] for n in WEIGHTS}
    for n in MID:
        prm[n], mom[n], var[n] = kernel_view(n, prm[n]), kernel_view(n, mom[n]), kernel_view(n, var[n])

    dw_shard = prm["conv_w_dw"].reshape(N_LAYERS, CONV_KERNEL, -1)
    casts = {"w_in": _cast_into("w_in", prm["w_in"], place, MXU_DTYPE)}
    first, first_started = _allgather_start("first", [[casts["w_in"][0]]])
    in_flight = {(0, "in"): first[0]}
    mixer = GATHER_GROUPS["mixer"]
    casts.update(zip(mixer, _cast_small_into(
        "mixer", [dw_shard if n == "conv_w_dw" else prm[n] for n in mixer],
        [F32 if n == "conv_w_dw" else MXU_DTYPE for n in mixer], place, after=(first_started,))))
    casts.update({n: _cast_into(n, kernel_view(n, prm[n]), place, MXU_DTYPE, after=(first_started,))
                  for n in GATHER_GROUPS["ffn"]})
    order = [(l, g) for l in range(N_LAYERS) for g in GATHER_GROUPS if (l, g) != (0, "in")]
    rest, rest_started = _allgather_start("rest", [[casts[n][l] for n in GATHER_GROUPS[g]] for l, g in order])
    in_flight.update(zip(order, rest))

    arrived = {}

    def weights_of(l, group, after):
        if (l, group) in arrived:
            return arrived.pop((l, group))
        tag = f"l{l}_{group}"
        if (l, group) == (0, "in"):
            after = after + (rest_started,)
        groups = (group, "mixer") if (l > 0 and group == "in") else (group,)
        waited = [_allgather_wait(f"l{l}_{g}", *in_flight[l, g][:2], in_flight[l, g][2], after) for g in groups]
        bufs = _allgather_forward(tag, [b for w in waited for b in w])
        for g in groups:
            fw = dict(zip(GATHER_GROUPS[g], bufs[:len(GATHER_GROUPS[g])]))
            bufs = bufs[len(GATHER_GROUPS[g]):]
            if "conv_w_dw" in fw:
                fw["conv_w_dw"] = fw["conv_w_dw"].transpose(1, 0, 2).reshape(CONV_KERNEL, -1)
            arrived[l, g] = fw
        return arrived.pop((l, group))

    pending, small_pending, small_shapes = {}, {}, {}
    tokens = {}

    def on_grads(l, group, grads):
        if group == "small":
            packed = {n: g for n, g in grads.items() if n not in MID}
            small_shapes[l] = {n: g.shape for n, g in packed.items()}
            begun = _allgather_rows_start(f"l{l}", [_pack_rows(list(packed.values()))] + [grads[n] for n in MID])
            small_pending[l], token = begun[:4], begun[4]
        else:
            pending[l, group], token = _reduce_start(f"{l}_{group}", grads)
        tokens[l, group] = token
        return token[0, 0]

    loss, dx, _, _, _ = _local_step(x[0], loss_target[0], weights_of, prm, place, on_grads)
    loss = lax.psum(loss, ("x", "y", "c"))

    reduced = [{} for _ in range(N_LAYERS)]
    out = {}

    def finish(l, groups, after):
        reduced[l].update(_reduce_finish(f"l{l}_{groups[0]}", [pending[l, g] for g in groups], place, after))

    def adamw(l, names, done):
        for group in ("in", "ffn", "mixer"):
            members = [n for n in names if n in GATHER_GROUPS[group]]
            if len(members) == 1:
                n = members[0]
                out[n] = _adamw_big(n, l, kernel_view(n, prm[n]), kernel_view(n, mom[n]), kernel_view(n, var[n]),
                                    reduced[l][n], out.get(n), after=done)
                done = (out[n][0],)
            elif members:
                res = _adamw_small_group(group, l, [kernel_view(n, prm[n]) for n in members],
                                         [kernel_view(n, mom[n]) for n in members],
                                         [kernel_view(n, var[n]) for n in members],
                                         [reduced[l][n] for n in members], [out.get(n) for n in members], after=done)
                for i, n in enumerate(members):
                    out[n] = tuple(res[4 * i:4 * i + 4])
                done = (res[0],)
        return done

    top = N_LAYERS - 1
    done = (tokens[0, "in"], tokens[0, "small"])
    finish(top, ("ffn", "mixer", "in"), done)
    done = adamw(top, BIG, done)
    for groups in (("ffn", "mixer"), ("in",)):
        finish(0, groups, done)
        done = adamw(0, [n for g in groups for n in GATHER_GROUPS[g] if n in BIG], done)
    for n in BIG:
        out[n] = tuple(kernel_view(n, a) for a in out[n])

    gsmall = {}
    mid_mine, mid_gathered = [], []
    for l in range(N_LAYERS):
        mine, lands = _allgather_rows_wait(f"l{l}", *small_pending[l], done[0])
        lands = _allgather_rows_forward(f"l{l}", lands)
        mid_mine.append(mine[1:])
        mid_gathered.append(lands[1:])
        gsum = _sum_devices(f"l{l}", lands[0], mine[0], place)
        for n, g in zip(small_shapes[l], _unpack_rows(gsum, list(small_shapes[l].values()))):
            gsmall.setdefault(n, [None] * N_LAYERS)[l] = g
    mid_out = _adamw_mid([prm[n] for n in MID], [mom[n] for n in MID], [var[n] for n in MID], mid_gathered, mid_mine,
                         place)
    for n, res in zip(MID, mid_out):
        out[n] = tuple(kernel_view(n, a) for a in res)
    gsmall = {n: (g[top] if n == "final_norm" else jnp.stack(g)) for n, g in gsmall.items()}
    lanes = dw_shard.shape[-1]
    gsmall["conv_w_dw"] = lax.dynamic_slice_in_dim(gsmall["conv_w_dw"], (2 * cx + cy) * lanes, lanes, axis=2)
    small_names = [n for n in SMALL if n not in MID] + ["conv_w_dw"]
    w_rows = _pack_rows([prm[n] for n in small_names])
    m_rows = _pack_rows([mom[n] for n in small_names])
    v_rows = _pack_rows([var[n] for n in small_names])
    g_rows = _pack_rows([gsmall[n] for n in small_names])
    shapes = [prm[n].shape for n in small_names]
    d_s, m_s, v_s = (_unpack_rows(r, shapes) for r in _adamw_rows(w_rows, m_rows, v_rows, g_rows))
    for i, n in enumerate(small_names):
        out[n] = (gsmall[n].reshape(prm[n].shape), d_s[i], m_s[i], v_s[i])
    grads = [out[n][0] for n in WEIGHTS]
    deltas = [out[n][1] for n in WEIGHTS]
    new_m = [out[n][2] for n in WEIGHTS]
    new_v = [out[n][3] for n in WEIGHTS]
    return (loss, dx[None], *grads, *deltas, *new_m, *new_v)
```
